```python
import jax, jax.numpy as jnp
from jax import lax
import numpy as np

D_MODEL = 1024
BATCH = 8
SEQ = 2048
DEPTH = 2

MIX_WIDTH = D_MODEL
DN_HEADS = 4
DN_HEAD_K = 128
DN_HEAD_V = 128
DN_K_WIDTH = DN_HEADS * DN_HEAD_K
DN_V_WIDTH = DN_HEADS * DN_HEAD_V
QKV_DIM = 2 * DN_K_WIDTH + DN_V_WIDTH
CONV_WIDTH = 4
DN_CHUNK = 64
GM_GROUPS = 4
GM_GROUP_DIM = 128
GM_WIDTH = GM_GROUPS * GM_GROUP_DIM
GM_CHUNK = 128
D_FF = -(-8 * D_MODEL // (3 * 256)) * 256
EPS = 1e-6

Q_OFF = 0
K_OFF = Q_OFF + DN_K_WIDTH
V_OFF = K_OFF + DN_K_WIDTH
Z_OFF = V_OFF + DN_V_WIDTH
BETA_OFF = Z_OFF + DN_V_WIDTH
A_OFF = BETA_OFF + DN_HEADS
GM_OFF = A_OFF + DN_HEADS
IN_DIM = GM_OFF + 2 * GM_WIDTH

kernel_name = "hybrid_gdn_gmlp_parallel_heads"


def _rmsnorm(x, g):
    xf = x.astype(jnp.float32)
    y = xf * lax.rsqrt(jnp.mean(xf * xf, axis=-1, keepdims=True) + EPS)
    return (y * g.astype(jnp.float32)).astype(x.dtype)


def _l2norm(x):
    return x * lax.rsqrt(jnp.sum(x * x, axis=-1, keepdims=True) + EPS)


def _causal_depthwise_conv(x, w):
    c = x.shape[-1]
    return lax.conv_general_dilated(
        x, w[:, None, :].astype(x.dtype), window_strides=(1,),
        padding=((CONV_WIDTH - 1, 0),), dimension_numbers=("NWC", "WIO", "NWC"),
        feature_group_count=c)


def _gated_delta_rule(q, k, v, g, beta):
    b, t, h, dk = q.shape
    dv = v.shape[-1]
    n = t // DN_CHUNK

    def chunks(a):
        a = jnp.moveaxis(a, 2, 1)
        return a.reshape((b, h, n, DN_CHUNK) + a.shape[3:])

    q, k, v, g, beta = (chunks(a) for a in (q, k, v, g, beta))
    g = jnp.cumsum(g, axis=-1)
    idx = jnp.arange(DN_CHUNK)
    incl = idx[:, None] >= idx[None, :]
    strict = idx[:, None] > idx[None, :]
    diff = g[..., :, None] - g[..., None, :]
    decay = jnp.where(incl, jnp.exp(jnp.where(incl, diff, 0.0)), 0.0)
    k_beta = k * beta[..., None]
    v_beta = v * beta[..., None]
    a_mat = jnp.where(strict, jnp.einsum("bhncd,bhnsd->bhncs", k_beta, k) * decay, 0.0)
    eye = jnp.eye(DN_CHUNK, dtype=a_mat.dtype)
    tri = a_mat + eye
    u = lax.linalg.triangular_solve(tri, v_beta, left_side=True, lower=True)
    w = lax.linalg.triangular_solve(tri, k_beta * jnp.exp(g)[..., None], left_side=True, lower=True)
    qk = jnp.einsum("bhncd,bhnsd->bhncs", q, k) * decay

    def step(s, xs):
        q_c, k_c, u_c, w_c, g_c, qk_c = xs
        v_new = u_c - jnp.einsum("bhcd,bhde->bhce", w_c, s)
        o = (jnp.einsum("bhcd,bhde->bhce", q_c * jnp.exp(g_c)[..., None], s)
             + jnp.einsum("bhcs,bhse->bhce", qk_c, v_new))
        g_last = g_c[..., -1]
        s = (s * jnp.exp(g_last)[..., None, None]
             + jnp.einsum("bhcd,bhce->bhde", k_c * jnp.exp(g_last[..., None] - g_c)[..., None], v_new))
        return s, o

    xs = tuple(jnp.moveaxis(a, 2, 0) for a in (q, k, u, w, g, qk))
    s0 = jnp.zeros((b, h, dk, dv), jnp.float32)
    _, o = lax.scan(step, s0, xs)
    o = jnp.moveaxis(o, 0, 2).reshape(b, h, t, dv)
    return jnp.moveaxis(o, 1, 2)


def _hybrid_mixer(hn, w_in, conv_w, a_log, dt_bias, o_norm_g, ln_v_g, ln_v_b, w_s, b_s, w_out):
    b, t, _ = hn.shape
    proj = hn @ w_in

    qkv = jax.nn.silu(_causal_depthwise_conv(proj[..., Q_OFF:Z_OFF], conv_w)).astype(jnp.float32)
    q = _l2norm(qkv[..., Q_OFF:K_OFF].reshape(b, t, DN_HEADS, DN_HEAD_K)) * (DN_HEAD_K ** -0.5)
    k = _l2norm(qkv[..., K_OFF:V_OFF].reshape(b, t, DN_HEADS, DN_HEAD_K))
    v = qkv[..., V_OFF:Z_OFF].reshape(b, t, DN_HEADS, DN_HEAD_V)
    z = proj[..., Z_OFF:BETA_OFF].astype(jnp.float32).reshape(b, t, DN_HEADS, DN_HEAD_V)
    beta = jax.nn.sigmoid(proj[..., BETA_OFF:A_OFF].astype(jnp.float32))
    g = -jnp.exp(a_log.astype(jnp.float32)) * jax.nn.softplus(
        proj[..., A_OFF:GM_OFF].astype(jnp.float32) + dt_bias.astype(jnp.float32))
    o = _gated_delta_rule(q, k, v, g, beta)
    o = o * lax.rsqrt(jnp.mean(o * o, axis=-1, keepdims=True) + EPS)
    o = o * o_norm_g.astype(jnp.float32) * jax.nn.silu(z)
    o_dn = o.reshape(b, t, DN_V_WIDTH).astype(hn.dtype)

    gm = jax.nn.gelu(proj[..., GM_OFF:IN_DIM])
    u_g = gm[..., :GM_WIDTH]
    v_g = gm[..., GM_WIDTH:].astype(jnp.float32).reshape(b, t, GM_GROUPS, GM_GROUP_DIM)
    mu = jnp.mean(v_g, axis=-1, keepdims=True)
    var = jnp.mean(jnp.square(v_g - mu), axis=-1, keepdims=True)
    v_g = ((v_g - mu) * lax.rsqrt(var + EPS) * ln_v_g.reshape(GM_GROUPS, GM_GROUP_DIM)
           + ln_v_b.reshape(GM_GROUPS, GM_GROUP_DIM)).astype(hn.dtype)
    v_g = v_g.reshape(b, t // GM_CHUNK, GM_CHUNK, GM_GROUPS, GM_GROUP_DIM)
    pos = jnp.arange(GM_CHUNK)
    ws = jnp.where(pos[:, None] >= pos[None, :], w_s, 0.0).astype(hn.dtype)
    sp = jnp.einsum("gts,bnsgc->bntgc", ws, v_g) + b_s.T[:, :, None].astype(hn.dtype)
    o_gm = u_g * sp.reshape(b, t, GM_WIDTH)

    return jnp.concatenate([o_dn, o_gm], axis=-1) @ w_out


def _swiglu(h, w_gate, w_up, w_down):
    return (jax.nn.silu(h @ w_gate) * (h @ w_up)) @ w_down


def _fwd_setup_inputs(seed: int = 0) -> dict:
    key = jax.random.key(seed)
    ks = jax.random.split(key, 20)
    f32 = jnp.float32
    nrm = lambda k, shape, scale: jax.random.normal(k, shape, f32) * scale
    x = jax.random.normal(ks[0], (BATCH, SEQ, D_MODEL), f32)
    norm_mix = 1.0 + nrm(ks[1], (DEPTH, D_MODEL), 0.02)
    w_in = nrm(ks[2], (DEPTH, D_MODEL, IN_DIM), D_MODEL ** -0.5)
    conv_w = nrm(ks[3], (DEPTH, CONV_WIDTH, QKV_DIM), CONV_WIDTH ** -0.5)
    a_log = jnp.log(jax.random.uniform(ks[4], (DEPTH, DN_HEADS), f32, 1.0, 16.0))
    dt = jnp.exp(jax.random.uniform(ks[5], (DEPTH, DN_HEADS), f32, np.log(1e-3), np.log(1e-1)))
    dt_bias = dt + jnp.log(-jnp.expm1(-dt))
    o_norm_g = 1.0 + nrm(ks[6], (DEPTH, DN_HEAD_V), 0.02)
    ln_v_g = 1.0 + nrm(ks[7], (DEPTH, GM_WIDTH), 0.02)
    ln_v_b = nrm(ks[8], (DEPTH, GM_WIDTH), 0.02)
    w_s = nrm(ks[9], (DEPTH, GM_GROUPS, GM_CHUNK, GM_CHUNK), GM_CHUNK ** -0.5)
    b_s = 1.0 + nrm(ks[10], (DEPTH, GM_GROUPS, GM_CHUNK), 0.02)
    w_out = nrm(ks[11], (DEPTH, MIX_WIDTH, D_MODEL), MIX_WIDTH ** -0.5)
    norm_ffn = 1.0 + nrm(ks[12], (DEPTH, D_MODEL), 0.02)
    w_gate = nrm(ks[13], (DEPTH, D_MODEL, D_FF), D_MODEL ** -0.5)
    w_up = nrm(ks[14], (DEPTH, D_MODEL, D_FF), D_MODEL ** -0.5)
    w_down = nrm(ks[15], (DEPTH, D_FF, D_MODEL), D_FF ** -0.5)
    norm_final = 1.0 + nrm(ks[16], (D_MODEL,), 0.02)
    return {"x": x, "norm_mix": norm_mix, "w_in": w_in, "conv_w": conv_w,
            "a_log": a_log, "dt_bias": dt_bias, "o_norm_g": o_norm_g,
            "ln_v_g": ln_v_g, "ln_v_b": ln_v_b, "w_s": w_s, "b_s": b_s,
            "w_out": w_out, "norm_ffn": norm_ffn, "w_gate": w_gate, "w_up": w_up,
            "w_down": w_down, "norm_final": norm_final}


def _fwd_reference(x, norm_mix, w_in, conv_w, a_log, dt_bias, o_norm_g, ln_v_g, ln_v_b,
              w_s, b_s, w_out, norm_ffn, w_gate, w_up, w_down, norm_final):
    h = x
    for l in range(DEPTH):
        hn = _rmsnorm(h, norm_mix[l])
        h = h + _hybrid_mixer(hn, w_in[l], conv_w[l], a_log[l], dt_bias[l], o_norm_g[l],
                              ln_v_g[l], ln_v_b[l], w_s[l], b_s[l], w_out[l])
        h = h + _swiglu(_rmsnorm(h, norm_ffn[l]), w_gate[l], w_up[l], w_down[l])
    return _rmsnorm(h, norm_final)


import jax as _jax
import jax.numpy as _jnp

TWIN_FORMAT = 'train_step'
FWD_PARAMS = ['x', 'norm_mix', 'w_in', 'conv_w', 'a_log', 'dt_bias', 'o_norm_g', 'ln_v_g', 'ln_v_b', 'w_s', 'b_s', 'w_out', 'norm_ffn', 'w_gate', 'w_up', 'w_down', 'norm_final']
TWIN_WEIGHTS = ['norm_mix', 'w_in', 'conv_w', 'a_log', 'dt_bias', 'o_norm_g', 'ln_v_g', 'ln_v_b', 'w_s', 'b_s', 'w_out', 'norm_ffn', 'w_gate', 'w_up', 'w_down', 'norm_final']
TWIN_DIFF_INPUT = 'x'
TWIN_INPUTS = ['x', 'norm_mix', 'w_in', 'conv_w', 'a_log', 'dt_bias', 'o_norm_g', 'ln_v_g', 'ln_v_b', 'w_s', 'b_s', 'w_out', 'norm_ffn', 'w_gate', 'w_up', 'w_down', 'norm_final', 'loss_target', 'm_norm_mix', 'm_w_in', 'm_conv_w', 'm_a_log', 'm_dt_bias', 'm_o_norm_g', 'm_ln_v_g', 'm_ln_v_b', 'm_w_s', 'm_b_s', 'm_w_out', 'm_norm_ffn', 'm_w_gate', 'm_w_up', 'm_w_down', 'm_norm_final', 'v_norm_mix', 'v_w_in', 'v_conv_w', 'v_a_log', 'v_dt_bias', 'v_o_norm_g', 'v_ln_v_g', 'v_ln_v_b', 'v_w_s', 'v_b_s', 'v_w_out', 'v_norm_ffn', 'v_w_gate', 'v_w_up', 'v_w_down', 'v_norm_final']
TWIN_OUTPUTS = ['loss', 'grad_x', 'grad_norm_mix', 'grad_w_in', 'grad_conv_w', 'grad_a_log', 'grad_dt_bias', 'grad_o_norm_g', 'grad_ln_v_g', 'grad_ln_v_b', 'grad_w_s', 'grad_b_s', 'grad_w_out', 'grad_norm_ffn', 'grad_w_gate', 'grad_w_up', 'grad_w_down', 'grad_norm_final', 'delta_norm_mix', 'delta_w_in', 'delta_conv_w', 'delta_a_log', 'delta_dt_bias', 'delta_o_norm_g', 'delta_ln_v_g', 'delta_ln_v_b', 'delta_w_s', 'delta_b_s', 'delta_w_out', 'delta_norm_ffn', 'delta_w_gate', 'delta_w_up', 'delta_w_down', 'delta_norm_final', 'new_m_norm_mix', 'new_m_w_in', 'new_m_conv_w', 'new_m_a_log', 'new_m_dt_bias', 'new_m_o_norm_g', 'new_m_ln_v_g', 'new_m_ln_v_b', 'new_m_w_s', 'new_m_b_s', 'new_m_w_out', 'new_m_norm_ffn', 'new_m_w_gate', 'new_m_w_up', 'new_m_w_down', 'new_m_norm_final', 'new_v_norm_mix', 'new_v_w_in', 'new_v_conv_w', 'new_v_a_log', 'new_v_dt_bias', 'new_v_o_norm_g', 'new_v_ln_v_g', 'new_v_ln_v_b', 'new_v_w_s', 'new_v_b_s', 'new_v_w_out', 'new_v_norm_ffn', 'new_v_w_gate', 'new_v_w_up', 'new_v_w_down', 'new_v_norm_final']
TWIN_LEAF_KINDS = {'loss': 'loss', 'grad_x': 'grad_x', 'grad_norm_mix': 'grad_w', 'grad_w_in': 'grad_w', 'grad_conv_w': 'grad_w', 'grad_a_log': 'grad_w', 'grad_dt_bias': 'grad_w', 'grad_o_norm_g': 'grad_w', 'grad_ln_v_g': 'grad_w', 'grad_ln_v_b': 'grad_w', 'grad_w_s': 'grad_w', 'grad_b_s': 'grad_w', 'grad_w_out': 'grad_w', 'grad_norm_ffn': 'grad_w', 'grad_w_gate': 'grad_w', 'grad_w_up': 'grad_w', 'grad_w_down': 'grad_w', 'grad_norm_final': 'grad_w', 'delta_norm_mix': 'delta_w', 'delta_w_in': 'delta_w', 'delta_conv_w': 'delta_w', 'delta_a_log': 'delta_w', 'delta_dt_bias': 'delta_w', 'delta_o_norm_g': 'delta_w', 'delta_ln_v_g': 'delta_w', 'delta_ln_v_b': 'delta_w', 'delta_w_s': 'delta_w', 'delta_b_s': 'delta_w', 'delta_w_out': 'delta_w', 'delta_norm_ffn': 'delta_w', 'delta_w_gate': 'delta_w', 'delta_w_up': 'delta_w', 'delta_w_down': 'delta_w', 'delta_norm_final': 'delta_w', 'new_m_norm_mix': 'new_m', 'new_m_w_in': 'new_m', 'new_m_conv_w': 'new_m', 'new_m_a_log': 'new_m', 'new_m_dt_bias': 'new_m', 'new_m_o_norm_g': 'new_m', 'new_m_ln_v_g': 'new_m', 'new_m_ln_v_b': 'new_m', 'new_m_w_s': 'new_m', 'new_m_b_s': 'new_m', 'new_m_w_out': 'new_m', 'new_m_norm_ffn': 'new_m', 'new_m_w_gate': 'new_m', 'new_m_w_up': 'new_m', 'new_m_w_down': 'new_m', 'new_m_norm_final': 'new_m', 'new_v_norm_mix': 'new_v', 'new_v_w_in': 'new_v', 'new_v_conv_w': 'new_v', 'new_v_a_log': 'new_v', 'new_v_dt_bias': 'new_v', 'new_v_o_norm_g': 'new_v', 'new_v_ln_v_g': 'new_v', 'new_v_ln_v_b': 'new_v', 'new_v_w_s': 'new_v', 'new_v_b_s': 'new_v', 'new_v_w_out': 'new_v', 'new_v_norm_ffn': 'new_v', 'new_v_w_gate': 'new_v', 'new_v_w_up': 'new_v', 'new_v_w_down': 'new_v', 'new_v_norm_final': 'new_v'}


def _forward(args):
    return _fwd_reference(*[args[k] for k in FWD_PARAMS])


def _output_shape():
    out = _jax.eval_shape(lambda: _forward(_fwd_setup_inputs(0)))
    return out.shape, out.dtype

N_MICROBATCH = 1
ADAM_LR = 0.001
ADAM_B1 = 0.9
ADAM_B2 = 0.999
ADAM_EPS = 1e-08
ADAM_WD = 0.01
ADAM_STEP = 10
PER_EXAMPLE_BATCH_AXIS = {'x': 0, 'loss_target': 0}
SHARED_INPUTS = []
_WEIGHT_DTYPES = {'norm_mix': _jnp.float32, 'w_in': _jnp.float32, 'conv_w': _jnp.float32, 'a_log': _jnp.float32, 'dt_bias': _jnp.float32, 'o_norm_g': _jnp.float32, 'ln_v_g': _jnp.float32, 'ln_v_b': _jnp.float32, 'w_s': _jnp.float32, 'b_s': _jnp.float32, 'w_out': _jnp.float32, 'norm_ffn': _jnp.float32, 'w_gate': _jnp.float32, 'w_up': _jnp.float32, 'w_down': _jnp.float32, 'norm_final': _jnp.float32}
MOMENT_SCALE = {'norm_mix': 1.090968e-01, 'w_in': 6.279330e-02, 'conv_w': 4.987467e-02, 'a_log': 3.145759e-01, 'dt_bias': 3.085337e-01, 'o_norm_g': 1.318556e-01, 'ln_v_g': 5.255322e-02, 'ln_v_b': 5.197593e-02, 'w_s': 5.096952e-02, 'b_s': 7.257216e-02, 'w_out': 7.887743e-02, 'norm_ffn': 8.268781e-02, 'w_gate': 3.486296e-02, 'w_up': 3.384778e-02, 'w_down': 5.598987e-02, 'norm_final': 1.604113e+01}


def _to_microbatches(a, axis):
    t = _jnp.moveaxis(a, axis, 0)
    t = t.reshape((N_MICROBATCH, t.shape[0] // N_MICROBATCH) + t.shape[1:])
    return _jnp.moveaxis(t, 1, axis + 1)


def setup_inputs(seed: int = 0) -> dict:
    inp = _fwd_setup_inputs(seed)
    key = _jax.random.fold_in(_jax.random.key(seed), 7919)
    shape, _ = _output_shape()
    out = dict(inp)
    out["loss_target"] = _jax.random.normal(_jax.random.fold_in(key, 0), shape, _jnp.float32)
    for i, name in enumerate(TWIN_WEIGHTS):
        w = inp[name].astype(_jnp.float32)
        if MOMENT_SCALE is None:
            s = _jnp.sqrt(_jnp.mean(_jnp.square(w)) + 1e-30)
        else:
            s = MOMENT_SCALE[name]
        km, kv = _jax.random.split(_jax.random.fold_in(key, i + 1))
        out[name] = w
        out["m_" + name] = s * _jax.random.normal(km, w.shape, _jnp.float32)
        out["v_" + name] = (s * s) * _jax.random.uniform(kv, w.shape, _jnp.float32, 0.5, 1.5)
    if N_MICROBATCH > 1:
        for name, axis in PER_EXAMPLE_BATCH_AXIS.items():
            out[name] = _to_microbatches(out[name], axis)
    return {'x': out['x'], 'norm_mix': out['norm_mix'], 'w_in': out['w_in'], 'conv_w': out['conv_w'], 'a_log': out['a_log'], 'dt_bias': out['dt_bias'], 'o_norm_g': out['o_norm_g'], 'ln_v_g': out['ln_v_g'], 'ln_v_b': out['ln_v_b'], 'w_s': out['w_s'], 'b_s': out['b_s'], 'w_out': out['w_out'], 'norm_ffn': out['norm_ffn'], 'w_gate': out['w_gate'], 'w_up': out['w_up'], 'w_down': out['w_down'], 'norm_final': out['norm_final'], 'loss_target': out['loss_target'], 'm_norm_mix': out['m_norm_mix'], 'm_w_in': out['m_w_in'], 'm_conv_w': out['m_conv_w'], 'm_a_log': out['m_a_log'], 'm_dt_bias': out['m_dt_bias'], 'm_o_norm_g': out['m_o_norm_g'], 'm_ln_v_g': out['m_ln_v_g'], 'm_ln_v_b': out['m_ln_v_b'], 'm_w_s': out['m_w_s'], 'm_b_s': out['m_b_s'], 'm_w_out': out['m_w_out'], 'm_norm_ffn': out['m_norm_ffn'], 'm_w_gate': out['m_w_gate'], 'm_w_up': out['m_w_up'], 'm_w_down': out['m_w_down'], 'm_norm_final': out['m_norm_final'], 'v_norm_mix': out['v_norm_mix'], 'v_w_in': out['v_w_in'], 'v_conv_w': out['v_conv_w'], 'v_a_log': out['v_a_log'], 'v_dt_bias': out['v_dt_bias'], 'v_o_norm_g': out['v_o_norm_g'], 'v_ln_v_g': out['v_ln_v_g'], 'v_ln_v_b': out['v_ln_v_b'], 'v_w_s': out['v_w_s'], 'v_b_s': out['v_b_s'], 'v_w_out': out['v_w_out'], 'v_norm_ffn': out['v_norm_ffn'], 'v_w_gate': out['v_w_gate'], 'v_w_up': out['v_w_up'], 'v_w_down': out['v_w_down'], 'v_norm_final': out['v_norm_final']}


def _loss(weights, diff, rest, loss_target):
    with _jax.named_scope("forward"):
        args = {**rest, TWIN_DIFF_INPUT: diff, **{k: w.astype(_WEIGHT_DTYPES[k]) for k, w in weights.items()}}
        y = _forward(args)
    with _jax.named_scope("loss_head"):
        err = _jnp.square(y.astype(_jnp.float32) - loss_target)
        return 0.5 * _jnp.sum(_jnp.mean(err, axis=-1)) if err.ndim else 0.5 * err


def _adamw(w, g, m, v):
    m = ADAM_B1 * m + (1.0 - ADAM_B1) * g
    v = ADAM_B2 * v + (1.0 - ADAM_B2) * _jnp.square(g)
    m_hat = m / (1.0 - ADAM_B1 ** ADAM_STEP)
    v_hat = v / (1.0 - ADAM_B2 ** ADAM_STEP)
    delta = -ADAM_LR * (m_hat / (_jnp.sqrt(v_hat) + ADAM_EPS) + ADAM_WD * w)
    return delta, m, v


def reference(x, norm_mix, w_in, conv_w, a_log, dt_bias, o_norm_g, ln_v_g, ln_v_b, w_s, b_s, w_out, norm_ffn, w_gate, w_up, w_down, norm_final, loss_target, m_norm_mix, m_w_in, m_conv_w, m_a_log, m_dt_bias, m_o_norm_g, m_ln_v_g, m_ln_v_b, m_w_s, m_b_s, m_w_out, m_norm_ffn, m_w_gate, m_w_up, m_w_down, m_norm_final, v_norm_mix, v_w_in, v_conv_w, v_a_log, v_dt_bias, v_o_norm_g, v_ln_v_g, v_ln_v_b, v_w_s, v_b_s, v_w_out, v_norm_ffn, v_w_gate, v_w_up, v_w_down, v_norm_final):
    given = dict(x=x, norm_mix=norm_mix, w_in=w_in, conv_w=conv_w, a_log=a_log, dt_bias=dt_bias, o_norm_g=o_norm_g, ln_v_g=ln_v_g, ln_v_b=ln_v_b, w_s=w_s, b_s=b_s, w_out=w_out, norm_ffn=norm_ffn, w_gate=w_gate, w_up=w_up, w_down=w_down, norm_final=norm_final, loss_target=loss_target, m_norm_mix=m_norm_mix, m_w_in=m_w_in, m_conv_w=m_conv_w, m_a_log=m_a_log, m_dt_bias=m_dt_bias, m_o_norm_g=m_o_norm_g, m_ln_v_g=m_ln_v_g, m_ln_v_b=m_ln_v_b, m_w_s=m_w_s, m_b_s=m_b_s, m_w_out=m_w_out, m_norm_ffn=m_norm_ffn, m_w_gate=m_w_gate, m_w_up=m_w_up, m_w_down=m_w_down, m_norm_final=m_norm_final, v_norm_mix=v_norm_mix, v_w_in=v_w_in, v_conv_w=v_conv_w, v_a_log=v_a_log, v_dt_bias=v_dt_bias, v_o_norm_g=v_o_norm_g, v_ln_v_g=v_ln_v_g, v_ln_v_b=v_ln_v_b, v_w_s=v_w_s, v_b_s=v_b_s, v_w_out=v_w_out, v_norm_ffn=v_norm_ffn, v_w_gate=v_w_gate, v_w_up=v_w_up, v_w_down=v_w_down, v_norm_final=v_norm_final)
    weights = {n: given[n] for n in TWIN_WEIGHTS}
    shared = {n: given[n] for n in SHARED_INPUTS}
    per_example = {n: given[n] for n in ['x']}
    grad_fn = _jax.value_and_grad(_loss, argnums=(0, 1))

    def one_microbatch(ex, loss_target):
        ex = dict(ex)
        diff = ex.pop(TWIN_DIFF_INPUT)
        return grad_fn(weights, diff, {**shared, **ex}, loss_target)

    if N_MICROBATCH == 1:
        loss, (grad_w, grad_x) = one_microbatch(per_example, given["loss_target"])
    else:
        def body(carry, xs):
            loss_sum, grad_sum = carry
            l_k, (gw_k, gx_k) = one_microbatch(xs[0], xs[1])
            with _jax.named_scope("update"):
                return (loss_sum + l_k, _jax.tree.map(_jnp.add, grad_sum, gw_k)), gx_k

        init = (_jnp.zeros((), _jnp.float32), _jax.tree.map(_jnp.zeros_like, weights))
        (loss, grad_w), grad_x = _jax.lax.scan(body, init, (per_example, given["loss_target"]))
    with _jax.named_scope("update"):
        delta_w, new_m, new_v = {}, {}, {}
        for n in TWIN_WEIGHTS:
            delta_w[n], new_m[n], new_v[n] = _adamw(weights[n], grad_w[n], given["m_" + n], given["v_" + n])
    return (loss, grad_x, *[grad_w[n] for n in TWIN_WEIGHTS], *[delta_w[n] for n in TWIN_WEIGHTS],
            *[new_m[n] for n in TWIN_WEIGHTS], *[new_v[n] for n in TWIN_WEIGHTS])
```

```python
import functools

import jax
import jax.numpy as jnp
from jax import lax
from jax.experimental import pallas as pl
from jax.experimental.pallas import tpu as pltpu

F32 = jnp.float32
BF16 = jnp.bfloat16
MESH = pl.DeviceIdType.MESH
ANY = pl.BlockSpec(memory_space=pl.ANY)
HIGHEST = lax.Precision.HIGHEST

T = 2048
D = 1024
DEPTH = 2
NCHIP = 4
HEADS = 4
HD = 128
HW = HEADS * HD
CH = 64
GCH = 128
IN_DIM = 3080
NP = 3200
BA_OFF = 3072
FF_SH = 704
EPS = 1e-6
LANE = 128
VMEM_LIMIT = 56 * 1024 * 1024

ADAM_LR = 0.001
ADAM_B1 = 0.9
ADAM_B2 = 0.999
ADAM_EPS = 1e-08
ADAM_WD = 0.01
ADAM_STEP = 10


def _cparams(sem=None):
    return pltpu.CompilerParams(dimension_semantics=sem, vmem_limit_bytes=VMEM_LIMIT)


_DIMS = {"nn": (((1,), (0,)), ((), ())), "nt": (((1,), (1,)), ((), ())), "tn": (((0,), (0,)), ((), ()))}


def _mm(name, mode, a, bs, *, tm, tn, tk, out_dtypes=(F32,), reduce_g=False, resid=None, extras=(), epilogue=None):
    nb = len(bs)
    ga = a.shape[0]
    gbs = [b.shape[0] for b in bs]
    g_n = max([ga] + gbs)
    if mode == "tn":
        k_n, m_n = a.shape[1:]
    else:
        m_n, k_n = a.shape[1:]
    n_n = bs[0].shape[1] if mode == "nt" else bs[0].shape[2]
    assert m_n % tm == 0 and n_n % tn == 0 and k_n % tk == 0, (name, m_n, n_n, k_n)
    mi, nj, kk = m_n // tm, n_n // tn, k_n // tk
    if reduce_g:
        grid = (mi, nj, g_n, kk)
        ids = lambda i, j, g, k: (g, i, j, k)
        n_red = g_n * kk
        red_idx = lambda: pl.program_id(2) * kk + pl.program_id(3)
        sem = ("parallel", "parallel", "arbitrary", "arbitrary")
    else:
        grid = (g_n, mi, nj, kk)
        ids = lambda g, i, j, k: (g, i, j, k)
        n_red = kk
        red_idx = lambda: pl.program_id(3)
        sem = ("parallel", "parallel", "parallel", "arbitrary")

    def pick(gsz, g):
        return g if gsz > 1 else 0

    def a_map(*p):
        g, i, j, k = ids(*p)
        return (pick(ga, g), k, i) if mode == "tn" else (pick(ga, g), i, k)

    def b_map(gsz):
        def f(*p):
            g, i, j, k = ids(*p)
            return (pick(gsz, g), j, k) if mode == "nt" else (pick(gsz, g), k, j)
        return f

    def o_map(gsz):
        def f(*p):
            g, i, j, k = ids(*p)
            return (0 if reduce_g else pick(gsz, g), i, j)
        return f

    a_spec = pl.BlockSpec((None, tk, tm) if mode == "tn" else (None, tm, tk), a_map)
    b_specs = [pl.BlockSpec((None, tn, tk) if mode == "nt" else (None, tk, tn), b_map(gs)) for gs in gbs]
    x_specs = [pl.BlockSpec((None, tm, tn), o_map(e.shape[0])) for e in extras]
    r_specs = [pl.BlockSpec((None, tm, tn), o_map(resid.shape[0]))] if resid is not None else []
    g_out = 1 if reduce_g else g_n
    out_shape = [jax.ShapeDtypeStruct((g_out, m_n, n_n), dt) for dt in out_dtypes]
    out_specs = [pl.BlockSpec((None, tm, tn), o_map(g_out)) for _ in out_dtypes]
    nx, nr, no = len(extras), len(r_specs), len(out_dtypes)
    dims = _DIMS[mode]

    def body(*refs):
        a_ref = refs[0]
        b_refs = refs[1:1 + nb]
        x_refs = refs[1 + nb:1 + nb + nx]
        r_refs = refs[1 + nb + nx:1 + nb + nx + nr]
        o_refs = refs[1 + nb + nx + nr:1 + nb + nx + nr + no]
        acc_refs = refs[1 + nb + nx + nr + no:]
        r = red_idx()
        av = a_ref[...]
        for b_ref, acc in zip(b_refs, acc_refs):
            p = lax.dot_general(av, b_ref[...], dims, preferred_element_type=F32)

            @pl.when(r == 0)
            def _():
                acc[...] = p

            @pl.when(r > 0)
            def _():
                acc[...] += p

        @pl.when(r == n_red - 1)
        def _():
            accs = [acc[...] for acc in acc_refs]
            if r_refs:
                accs[0] = accs[0] + r_refs[0][...]
            outs = epilogue(accs, [x[...] for x in x_refs]) if epilogue is not None else accs
            for o_ref, o in zip(o_refs, outs):
                o_ref[...] = o.astype(o_ref.dtype)

    return pl.pallas_call(
        body, name=name, grid=grid,
        in_specs=[a_spec] + b_specs + x_specs + r_specs,
        out_specs=out_specs, out_shape=out_shape,
        scratch_shapes=[pltpu.VMEM((tm, tn), F32) for _ in range(nb)],
        compiler_params=_cparams(sem),
    )(a, *bs, *extras, *([resid] if resid is not None else []))


def _sigmoid(x):
    return 1.0 / (1.0 + jnp.exp(-x))


def _silu(x):
    return x * _sigmoid(x)


def _gelu(x):
    return 0.5 * x * (1.0 + jnp.tanh(0.7978845608028654 * (x + 0.044715 * (x * x * x))))


def _rms_fn(h, gain):
    return h * lax.rsqrt(jnp.mean(h * h, axis=-1, keepdims=True) + EPS) * gain


def _shift_impl(x, s):
    n = x.shape[0]
    rolled = pltpu.roll(x, s % n, 0)
    row = lax.broadcasted_iota(jnp.int32, x.shape, 0)
    return jnp.where((row >= s) & (row < n + s), rolled, 0.0)


@functools.partial(jax.custom_vjp, nondiff_argnums=(1,))
def _shift(x, s):
    return _shift_impl(x, s)


def _shift_fwd(x, s):
    return _shift_impl(x, s), None


def _shift_bwd(s, _, g):
    return (_shift_impl(g, -s),)


_shift.defvjp(_shift_fwd, _shift_bwd)


def _prep_fn(x, w, qk_scale, is_v):
    y = x * w[3:4, :]
    for i in range(3):
        y = y + _shift(x, 3 - i) * w[i:i + 1, :]
    y = _silu(y)
    nrm = lax.rsqrt(jnp.sum(y * y, axis=-1, keepdims=True) + EPS) * qk_scale
    return y * jnp.where(is_v, 1.0, nrm)


def _softplus(x):
    return jnp.maximum(x, 0.0) + jnp.log(1.0 + jnp.exp(-jnp.abs(x)))


def _gates_fn(ba, a_log, dt_bias):
    lane = lax.broadcasted_iota(jnp.int32, ba.shape, 1)
    beta = _sigmoid(ba)
    g = -jnp.exp(a_log) * _softplus(ba + dt_bias)
    return jnp.where(lane < HEADS, beta, g)


def _dot16(a, b, dims=_DIMS["nn"]):
    return lax.dot_general(a.astype(BF16), b.astype(BF16), dims, preferred_element_type=F32)


def _dot32(a, b):
    return jnp.dot(a, b, preferred_element_type=F32, precision=HIGHEST)


def _chunk_fn(q, k, v, g, beta, s):
    row = lax.broadcasted_iota(jnp.int32, (CH, CH), 0)
    col = lax.broadcasted_iota(jnp.int32, (CH, CH), 1)
    incl = row >= col
    strict = row > col
    lmat = incl.astype(F32)
    gc = _dot32(lmat, jnp.broadcast_to(g, (CH, HD)))[:, :1]
    diff = _dot32(lmat, jnp.where(strict, jnp.broadcast_to(g, (CH, CH)), 0.0))
    decay = jnp.where(incl, jnp.exp(jnp.where(incl, diff, 0.0)), 0.0)
    k_beta = k * beta
    v_beta = v * beta
    a = jnp.where(strict, _dot16(k_beta, k, _DIMS["nt"]) * decay, 0.0)
    t = (row == col).astype(F32) - a
    p = a
    for _ in range(5):
        p = _dot32(p, p)
        t = t + _dot32(t, p)
    eg = jnp.exp(gc)
    u = _dot32(t, v_beta)
    w = _dot32(t, k_beta * eg)
    qk = _dot16(q, k, _DIMS["nt"]) * decay
    v_new = u - _dot16(w, s)
    o = _dot16(q * eg, s) + _dot16(qk, v_new)
    g_last = gc[CH - 1:CH, :]
    s_new = s * jnp.exp(g_last) + _dot16(k * jnp.exp(g_last - gc), v_new, _DIMS["tn"])
    return o, s_new


def _mix_fn(o, z, ur, vr, ong, lng, lnb, ws, bst):
    row = lax.broadcasted_iota(jnp.int32, (GCH, GCH), 0)
    col = lax.broadcasted_iota(jnp.int32, (GCH, GCH), 1)
    causal = row >= col
    ug = _gelu(ur)
    vg = _gelu(vr)
    outs_dn, outs_gm = [], []
    for h in range(HEADS):
        sl = slice(h * HD, (h + 1) * HD)
        oh = o[:, sl]
        oh = oh * lax.rsqrt(jnp.mean(oh * oh, axis=-1, keepdims=True) + EPS)
        outs_dn.append(oh * ong * _silu(z[:, sl]))
        vh = vg[:, sl]
        mu = jnp.mean(vh, axis=-1, keepdims=True)
        var = jnp.mean(jnp.square(vh - mu), axis=-1, keepdims=True)
        vn = (vh - mu) * lax.rsqrt(var + EPS) * lng[:, sl] + lnb[:, sl]
        sp = _dot16(jnp.where(causal, ws[h], 0.0), vn) + bst[:, h:h + 1]
        outs_gm.append(ug[:, sl] * sp)
    return jnp.concatenate(outs_dn + outs_gm, axis=-1)


def _loss_fn(h, gain, tgt):
    y = _rms_fn(h, gain)
    return 0.5 * jnp.sum(jnp.mean(jnp.square(y - tgt), axis=-1))


RT = 256


def _rows(n=D):
    return pl.BlockSpec((RT, n), lambda i: (i, 0))


def _whole(shape):
    nd = len(shape)
    return pl.BlockSpec(shape, lambda i: (0,) * nd)


def _rmsnorm(name, h, gain):
    def body(h_ref, g_ref, o_ref):
        o_ref[...] = _rms_fn(h_ref[...], g_ref[...]).astype(BF16)

    return pl.pallas_call(
        body, name=name, grid=(T // RT,), in_specs=[_rows(), _whole((1, D))], out_specs=_rows(),
        out_shape=jax.ShapeDtypeStruct((T, D), BF16), compiler_params=_cparams(("parallel",)),
    )(h, gain)


def _rmsnorm_bwd(name, dhn, h, gain, resid):
    def body(dhn_ref, h_ref, g_ref, r_ref, dh_ref, dg_ref):
        _, vjp = jax.vjp(_rms_fn, h_ref[...], g_ref[...])
        dh, dg = vjp(dhn_ref[...])
        dh_ref[...] = r_ref[...] + dh

        @pl.when(pl.program_id(0) == 0)
        def _():
            dg_ref[...] = dg

        @pl.when(pl.program_id(0) > 0)
        def _():
            dg_ref[...] += dg

    return pl.pallas_call(
        body, name=name, grid=(T // RT,), in_specs=[_rows(), _rows(), _whole((1, D)), _rows()],
        out_specs=[_rows(), _whole((1, D))],
        out_shape=[jax.ShapeDtypeStruct((T, D), F32), jax.ShapeDtypeStruct((1, D), F32)],
        compiler_params=_cparams(("arbitrary",)),
    )(dhn, h, gain, resid)


def _loss_head(h, gain, tgt):
    def body(h_ref, g_ref, t_ref, l_ref, dh_ref, dg_ref):
        loss, vjp = jax.vjp(lambda hh, gg: _loss_fn(hh, gg, t_ref[...]), h_ref[...], g_ref[...])
        dh, dg = vjp(jnp.ones((), F32))
        dh_ref[...] = dh
        lv = jnp.full((1, LANE), loss, F32)

        @pl.when(pl.program_id(0) == 0)
        def _():
            dg_ref[...] = dg
            l_ref[...] = lv

        @pl.when(pl.program_id(0) > 0)
        def _():
            dg_ref[...] += dg
            l_ref[...] += lv

    return pl.pallas_call(
        body, name="loss_head", grid=(T // RT,), in_specs=[_rows(), _whole((1, D)), _rows()],
        out_specs=[_whole((1, LANE)), _rows(), _whole((1, D))],
        out_shape=[jax.ShapeDtypeStruct((1, LANE), F32), jax.ShapeDtypeStruct((T, D), F32),
                   jax.ShapeDtypeStruct((1, D), F32)],
        compiler_params=_cparams(("arbitrary",)),
    )(h, gain, tgt)


def _prep_flags():
    j = pl.program_id(0)
    qk_scale = jnp.where(j < HEADS, HD ** -0.5, 1.0).astype(F32)
    return qk_scale, j >= 2 * HEADS


def _prep(proj, conv_w):
    def body(x_ref, w_ref, o_ref):
        qk_scale, is_v = _prep_flags()
        o_ref[...] = _prep_fn(x_ref[...], w_ref[...], qk_scale, is_v)

    col = lambda j: (0, j)
    return pl.pallas_call(
        body, name="gdn_prep", grid=(3 * HEADS,),
        in_specs=[pl.BlockSpec((T, HD), col), pl.BlockSpec((4, HD), col)], out_specs=pl.BlockSpec((T, HD), col),
        out_shape=jax.ShapeDtypeStruct((T, 3 * HW), F32), compiler_params=_cparams(("parallel",)),
    )(proj, conv_w)


def _prep_bwd(proj, conv_w, dqkv):
    def body(x_ref, w_ref, d_ref, dx_ref, dw_ref):
        qk_scale, is_v = _prep_flags()
        _, vjp = jax.vjp(lambda x, w: _prep_fn(x, w, qk_scale, is_v), x_ref[...], w_ref[...])
        dx, dw = vjp(d_ref[...])
        dx_ref[...] = dx
        dw_ref[...] = dw

    col = lambda j: (0, j)
    return pl.pallas_call(
        body, name="gdn_prep_bwd", grid=(3 * HEADS,),
        in_specs=[pl.BlockSpec((T, HD), col), pl.BlockSpec((4, HD), col), pl.BlockSpec((T, HD), col)],
        out_specs=[pl.BlockSpec((T, HD), col), pl.BlockSpec((4, HD), col)],
        out_shape=[jax.ShapeDtypeStruct((T, 3 * HW), F32), jax.ShapeDtypeStruct((4, 3 * HW), F32)],
        compiler_params=_cparams(("parallel",)),
    )(proj, conv_w, dqkv)


BA_BLK = BA_OFF // LANE


def _gates(proj, a_log, dt_bias):
    def body(x_ref, a_ref, d_ref, b_out, g_out):
        bg = _gates_fn(x_ref[...], a_ref[...], d_ref[...])
        for h in range(HEADS):
            b_out[:, h * HD:(h + 1) * HD] = jnp.broadcast_to(bg[:, h:h + 1], (T, HD))
            g_out[:, h * HD:(h + 1) * HD] = jnp.broadcast_to(bg[:, HEADS + h:HEADS + h + 1], (T, HD))

    return pl.pallas_call(
        body, name="gdn_gates", grid=(1,),
        in_specs=[pl.BlockSpec((T, LANE), lambda i: (0, BA_BLK)), _whole((1, LANE)), _whole((1, LANE))],
        out_specs=[_whole((T, HW)), _whole((T, HW))],
        out_shape=[jax.ShapeDtypeStruct((T, HW), F32)] * 2, compiler_params=_cparams(("arbitrary",)),
    )(proj, a_log, dt_bias)


def _gates_bwd(proj, a_log, dt_bias, dbeta, dg):
    def body(x_ref, a_ref, d_ref, db_ref, dg_ref, dx_ref, da_ref, dd_ref):
        lane = lax.broadcasted_iota(jnp.int32, (T, LANE), 1)
        cot = jnp.zeros((T, LANE), F32)
        for h in range(HEADS):
            cot = jnp.where(lane == h, db_ref[:, h * HD:(h + 1) * HD], cot)
            cot = jnp.where(lane == HEADS + h, dg_ref[:, h * HD:(h + 1) * HD], cot)
        _, vjp = jax.vjp(_gates_fn, x_ref[...], a_ref[...], d_ref[...])
        dx_ref[...], da_ref[...], dd_ref[...] = vjp(cot)

    return pl.pallas_call(
        body, name="gdn_gates_bwd", grid=(1,),
        in_specs=[pl.BlockSpec((T, LANE), lambda i: (0, BA_BLK)), _whole((1, LANE)), _whole((1, LANE)),
                  _whole((T, HW)), _whole((T, HW))],
        out_specs=[_whole((T, LANE)), _whole((1, LANE)), _whole((1, LANE))],
        out_shape=[jax.ShapeDtypeStruct((T, LANE), F32), jax.ShapeDtypeStruct((1, LANE), F32),
                   jax.ShapeDtypeStruct((1, LANE), F32)],
        compiler_params=_cparams(("arbitrary",)),
    )(proj, a_log, dt_bias, dbeta, dg)


NCK = T // CH


def _gdn(qkv, beta, g):
    def body(x_ref, b_ref, g_ref, o_ref, sh_ref, s_ref):
        @pl.when(pl.program_id(0) == 0)
        def _():
            s_ref[...] = jnp.zeros_like(s_ref)

        for h in range(HEADS):
            sl = slice(h * HD, (h + 1) * HD)
            q, k, v = (x_ref[:, c * HW + h * HD:c * HW + (h + 1) * HD] for c in range(3))
            s = s_ref[h]
            sh_ref[h, 0] = s
            o, s_new = _chunk_fn(q, k, v, g_ref[:, sl][:, :1], b_ref[:, sl][:, :1], s)
            o_ref[:, sl] = o
            s_ref[h] = s_new

    blk = lambda w: pl.BlockSpec((CH, w), lambda n: (n, 0))
    return pl.pallas_call(
        body, name="gdn_scan", grid=(NCK,),
        in_specs=[blk(3 * HW), blk(HW), blk(HW)],
        out_specs=[blk(HW), pl.BlockSpec((HEADS, 1, HD, HD), lambda n: (0, n, 0, 0))],
        out_shape=[jax.ShapeDtypeStruct((T, HW), F32), jax.ShapeDtypeStruct((HEADS, NCK, HD, HD), F32)],
        scratch_shapes=[pltpu.VMEM((HEADS, HD, HD), F32)], compiler_params=_cparams(("arbitrary",)),
    )(qkv, beta, g)


def _gdn_bwd(qkv, beta, g, s_hist, do):
    def body(x_ref, b_ref, g_ref, sh_ref, do_ref, dx_ref, db_ref, dg_ref, ds_ref):
        @pl.when(pl.program_id(0) == 0)
        def _():
            ds_ref[...] = jnp.zeros_like(ds_ref)

        for h in range(HEADS):
            sl = slice(h * HD, (h + 1) * HD)
            cols = [slice(c * HW + h * HD, c * HW + (h + 1) * HD) for c in range(3)]
            q, k, v = (x_ref[:, c] for c in cols)
            _, vjp = jax.vjp(_chunk_fn, q, k, v, g_ref[:, sl][:, :1], b_ref[:, sl][:, :1], sh_ref[h, 0])
            dq, dk, dv, dg, db, ds = vjp((do_ref[:, sl], ds_ref[h]))
            dx_ref[:, cols[0]] = dq
            dx_ref[:, cols[1]] = dk
            dx_ref[:, cols[2]] = dv
            dg_ref[:, sl] = jnp.broadcast_to(dg, (CH, HD))
            db_ref[:, sl] = jnp.broadcast_to(db, (CH, HD))
            ds_ref[h] = ds

    blk = lambda w: pl.BlockSpec((CH, w), lambda n: (NCK - 1 - n, 0))
    return pl.pallas_call(
        body, name="gdn_scan_bwd", grid=(NCK,),
        in_specs=[blk(3 * HW), blk(HW), blk(HW),
                  pl.BlockSpec((HEADS, 1, HD, HD), lambda n: (0, NCK - 1 - n, 0, 0)), blk(HW)],
        out_specs=[blk(3 * HW), blk(HW), blk(HW)],
        out_shape=[jax.ShapeDtypeStruct((T, 3 * HW), F32)] + [jax.ShapeDtypeStruct((T, HW), F32)] * 2,
        scratch_shapes=[pltpu.VMEM((HEADS, HD, HD), F32)], compiler_params=_cparams(("arbitrary",)),
    )(qkv, beta, g, s_hist, do)


def _mix_specs():
    pc = lambda c: pl.BlockSpec((GCH, HW), lambda i: (i, c))
    return [pl.BlockSpec((GCH, HW), lambda i: (i, 0)), pc(3), pc(4), pc(5), _whole((1, HD)), _whole((1, HW)),
            _whole((1, HW)), _whole((HEADS, GCH, GCH)), _whole((GCH, LANE))]


def _mix(o, proj, ong, lng, lnb, ws, bst):
    def body(o_ref, z_ref, u_ref, v_ref, ong_ref, lng_ref, lnb_ref, ws_ref, bs_ref, m_ref):
        m_ref[...] = _mix_fn(o_ref[...], z_ref[...], u_ref[...], v_ref[...], ong_ref[...], lng_ref[...],
                             lnb_ref[...], ws_ref[...], bs_ref[...]).astype(BF16)

    return pl.pallas_call(
        body, name="mix", grid=(T // GCH,), in_specs=_mix_specs(),
        out_specs=pl.BlockSpec((GCH, D), lambda i: (i, 0)), out_shape=jax.ShapeDtypeStruct((T, D), BF16),
        compiler_params=_cparams(("parallel",)),
    )(o, proj, proj, proj, ong, lng, lnb, ws, bst)


def _mix_bwd(o, proj, ong, lng, lnb, ws, bst, dmix):
    def body(o_ref, z_ref, u_ref, v_ref, ong_ref, lng_ref, lnb_ref, ws_ref, bs_ref, dm_ref,
             do_ref, dzuv_ref, dong_ref, dlng_ref, dlnb_ref, dws_ref, dbs_ref):
        _, vjp = jax.vjp(_mix_fn, o_ref[...], z_ref[...], u_ref[...], v_ref[...], ong_ref[...], lng_ref[...],
                         lnb_ref[...], ws_ref[...], bs_ref[...])
        do, dz, du, dv, dong, dlng, dlnb, dws, dbs = vjp(dm_ref[...])
        do_ref[...] = do
        dzuv_ref[:, 0:HW] = dz
        dzuv_ref[:, HW:2 * HW] = du
        dzuv_ref[:, 2 * HW:3 * HW] = dv
        acc = [(dong_ref, dong), (dlng_ref, dlng), (dlnb_ref, dlnb), (dws_ref, dws), (dbs_ref, dbs)]

        @pl.when(pl.program_id(0) == 0)
        def _():
            for r, val in acc:
                r[...] = val

        @pl.when(pl.program_id(0) > 0)
        def _():
            for r, val in acc:
                r[...] += val

    shp = lambda *s: jax.ShapeDtypeStruct(s, F32)
    return pl.pallas_call(
        body, name="mix_bwd", grid=(T // GCH,),
        in_specs=_mix_specs() + [pl.BlockSpec((GCH, D), lambda i: (i, 0))],
        out_specs=[pl.BlockSpec((GCH, HW), lambda i: (i, 0)), pl.BlockSpec((GCH, 3 * HW), lambda i: (i, 0)),
                   _whole((1, HD)), _whole((1, HW)), _whole((1, HW)), _whole((HEADS, GCH, GCH)), _whole((GCH, LANE))],
        out_shape=[shp(T, HW), shp(T, 3 * HW), shp(1, HD), shp(1, HW), shp(1, HW), shp(HEADS, GCH, GCH), shp(GCH, LANE)],
        compiler_params=_cparams(("arbitrary",)),
    )(o, proj, proj, proj, ong, lng, lnb, ws, bst, dmix)


def _swiglu_epilogue(accs, _):
    gate, up = accs
    return [gate, up, _silu(gate) * up]


def _swiglu_bwd_epilogue(accs, extras):
    dact = accs[0]
    gate, up = (e.astype(F32) for e in extras)
    sg = _sigmoid(gate)
    return [dact * up * (sg * (1.0 + gate * (1.0 - sg))), dact * (gate * sg)]


def _layer_fwd(h, p):
    hn = _rmsnorm("rms_mix", h, p["norm_mix"])
    proj = _mm("in_proj", "nn", hn[None], [p["w_in"][None]], tm=1024, tn=640, tk=D)[0][0]
    qkv = _prep(proj, p["conv_w"])
    beta, g = _gates(proj, p["a_log"], p["dt_bias"])
    o, s_hist = _gdn(qkv, beta, g)
    mix = _mix(o, proj, p["o_norm_g"], p["ln_v_g"], p["ln_v_b"], p["w_s"], p["bst"])
    h1 = _mm("out_proj", "nn", mix[None], [p["w_out"][None]], tm=1024, tn=512, tk=D, resid=h[None])[0][0]
    h2n = _rmsnorm("rms_ffn", h1, p["norm_ffn"])
    gate, up, act = _mm("ffn_in", "nn", h2n[None], [p["w_gate"], p["w_up"]], tm=1024, tn=FF_SH, tk=D,
                        out_dtypes=(BF16, BF16, BF16), epilogue=_swiglu_epilogue)
    h2 = _mm("ffn_out", "nn", act, [p["w_down"]], tm=1024, tn=512, tk=FF_SH, reduce_g=True, resid=h1[None])[0][0]
    saved = dict(h=h, hn=hn, proj=proj, qkv=qkv, beta=beta, g=g, o=o, s_hist=s_hist, mix=mix, h1=h1, h2n=h2n,
                 gate=gate, up=up, act=act)
    return h2, saved


def _layer_bwd(dh2, p, s):
    dh2b = dh2.astype(BF16)[None]
    dgate, dup = _mm("ffn_out_bwd", "nt", dh2b, [p["w_down"]], tm=1024, tn=FF_SH, tk=D, out_dtypes=(BF16, BF16),
                     extras=(s["gate"], s["up"]), epilogue=_swiglu_bwd_epilogue)
    dh2n = _mm("ffn_gate_bwd", "nt", dgate, [p["w_gate"]], tm=1024, tn=512, tk=FF_SH, reduce_g=True)[0]
    dh2n = _mm("ffn_up_bwd", "nt", dup, [p["w_up"]], tm=1024, tn=512, tk=FF_SH, reduce_g=True, resid=dh2n)[0][0]
    dh1, d_norm_ffn = _rmsnorm_bwd("rms_ffn_bwd", dh2n, s["h1"], p["norm_ffn"], dh2)
    d_w_down = _mm("ffn_wdown_grad", "tn", s["act"], [dh2b], tm=FF_SH, tn=512, tk=1024)[0]
    d_w_gate = _mm("ffn_wgate_grad", "tn", s["h2n"][None], [dgate], tm=512, tn=FF_SH, tk=1024)[0]
    d_w_up = _mm("ffn_wup_grad", "tn", s["h2n"][None], [dup], tm=512, tn=FF_SH, tk=1024)[0]
    dh1b = dh1.astype(BF16)[None]
    dmix = _mm("out_proj_bwd", "nt", dh1b, [p["w_out"][None]], tm=1024, tn=512, tk=D)[0][0]
    d_w_out = _mm("out_proj_wgrad", "tn", s["mix"][None], [dh1b], tm=512, tn=512, tk=1024)[0][0]
    do, dzuv, d_ong, d_lng, d_lnb, d_ws, d_bst = _mix_bwd(
        s["o"], s["proj"], p["o_norm_g"], p["ln_v_g"], p["ln_v_b"], p["w_s"], p["bst"], dmix)
    dqkv, dbeta, dg = _gdn_bwd(s["qkv"], s["beta"], s["g"], s["s_hist"], do)
    dqkv_x, d_conv = _prep_bwd(s["proj"], p["conv_w"], dqkv)
    dba, d_a_log, d_dt_bias = _gates_bwd(s["proj"], p["a_log"], p["dt_bias"], dbeta, dg)
    dproj = jnp.concatenate([dqkv_x, dzuv, dba], axis=1).astype(BF16)[None]
    dhn = _mm("in_proj_bwd", "nt", dproj, [p["w_in"][None]], tm=1024, tn=512, tk=640)[0][0]
    dh, d_norm_mix = _rmsnorm_bwd("rms_mix_bwd", dhn, s["h"], p["norm_mix"], dh1)
    d_w_in = _mm("in_proj_wgrad", "tn", s["hn"][None], [dproj], tm=512, tn=640, tk=1024)[0][0]
    grads = dict(norm_mix=d_norm_mix, w_in=d_w_in, conv_w=d_conv, a_log=d_a_log, dt_bias=d_dt_bias, o_norm_g=d_ong,
                 ln_v_g=d_lng, ln_v_b=d_lnb, w_s=d_ws, bst=d_bst, w_out=d_w_out, norm_ffn=d_norm_ffn,
                 w_gate=d_w_gate, w_up=d_w_up, w_down=d_w_down)
    return dh, grads


def _lanes(v, off=0):
    return jnp.zeros((1, LANE), F32).at[0, off:off + v.shape[0]].set(v)


def _layer_params(l, big, small):
    w_in = jnp.concatenate([big["w_in"][j, l] for j in range(NCHIP)], axis=1)
    w_in = jnp.concatenate([w_in[:, :2048], w_in[:, 2056:IN_DIM], w_in[:, 2048:2056],
                            jnp.zeros((D, NP - IN_DIM), w_in.dtype)], axis=1)
    return dict(
        w_in=w_in,
        w_out=big["w_out"][:, l].reshape(D, D),
        w_gate=big["w_gate"][:, l], w_up=big["w_up"][:, l], w_down=big["w_down"][:, l],
        conv_w=jnp.concatenate([big["conv_w"][j, l] for j in range(NCHIP)], axis=1),
        norm_mix=small["norm_mix"][l][None], norm_ffn=small["norm_ffn"][l][None],
        a_log=_lanes(small["a_log"][l], HEADS), dt_bias=_lanes(small["dt_bias"][l], HEADS),
        o_norm_g=small["o_norm_g"][l][None], ln_v_g=small["ln_v_g"][l][None], ln_v_b=small["ln_v_b"][l][None],
        w_s=small["w_s"][l],
        bst=jnp.pad(small["b_s"][l].T, ((0, 0), (0, LANE - HEADS))),
    )


def _reference_layout(g):
    w_in = g["w_in"]
    w_in = jnp.concatenate([w_in[:, :2048], w_in[:, BA_OFF:BA_OFF + 8], w_in[:, 2048:BA_OFF]], axis=1)
    return dict(
        w_in=w_in.reshape(D, NCHIP, IN_DIM // NCHIP).transpose(1, 0, 2),
        w_out=g["w_out"].reshape(NCHIP, D // NCHIP, D),
        w_gate=g["w_gate"], w_up=g["w_up"], w_down=g["w_down"],
        conv_w=g["conv_w"], norm_mix=g["norm_mix"][0], norm_ffn=g["norm_ffn"][0],
        a_log=g["a_log"][0, HEADS:2 * HEADS], dt_bias=g["dt_bias"][0, HEADS:2 * HEADS],
        o_norm_g=g["o_norm_g"][0], ln_v_g=g["ln_v_g"][0], ln_v_b=g["ln_v_b"][0], w_s=g["w_s"],
        b_s=g["bst"][:, :HEADS].T,
    )


def _local_step(x, tgt, layers, norm_final):
    h = x
    saved = []
    for p in layers:
        h, s = _layer_fwd(h, p)
        saved.append(s)
    loss, dh, d_norm_final = _loss_head(h, norm_final, tgt)
    grads = [None] * DEPTH
    for l in reversed(range(DEPTH)):
        dh, grads[l] = _layer_bwd(dh, layers[l], saved[l])
    return loss, dh, grads, d_norm_final


def _place():
    x, y, c = lax.axis_index("x"), lax.axis_index("y"), lax.axis_index("c")
    return x, y, c, [(1 - x, y), (x, 1 - y), (1 - x, 1 - y)]


def _remote(src, dst, send_sem, recv_sem, to):
    return pltpu.make_async_remote_copy(src_ref=src, dst_ref=dst, send_sem=send_sem, recv_sem=recv_sem,
                                        device_id=to, device_id_type=MESH)


def _comm_call(name, body, ins, out_shape, n_sems):
    return pl.pallas_call(
        body, name=name, in_specs=[ANY] * len(ins), out_specs=[ANY] * len(out_shape), out_shape=out_shape,
        scratch_shapes=[pltpu.SemaphoreType.DMA((n,)) for n in n_sems],
        compiler_params=pltpu.CompilerParams(has_side_effects=True),
    )(*ins)


def _gather_weights(shards, conv):
    n = len(shards)

    def body(*refs):
        ins, conv_in = refs[:n], refs[n]
        outs, conv_out = refs[n + 1:2 * n + 1], refs[2 * n + 1]
        ici_s, ici_r, d2d_s, d2d_r, loc = refs[2 * n + 2:]
        x, y, c, others = _place()
        chip = 2 * x + y
        sibling = (x, y, 1 - c)
        local = [pltpu.make_async_copy(ins[a], outs[a].at[chip], loc.at[a]) for a in range(n)]
        local.append(pltpu.make_async_copy(conv_in, conv_out.at[chip], loc.at[n]))
        for cp in local:
            cp.start()

        def half(a, of_c):
            hr = ins[a].shape[1] // 2
            return pl.ds(pl.multiple_of(of_c * hr, 16), hr)

        sends = []
        for a in range(n):
            for k, (ox, oy) in enumerate(others):
                sends.append(_remote(ins[a].at[:, half(a, c)], outs[a].at[chip, :, half(a, c)],
                                     ici_s.at[3 * a + k], ici_r.at[3 * a + k], (ox, oy, c)))
        for k, (ox, oy) in enumerate(others):
            sends.append(_remote(conv_in, conv_out.at[chip], ici_s.at[3 * n + k], ici_r.at[3 * n + k], (ox, oy, c)))
        for cp in sends:
            cp.start()
        passed = []
        for a in range(n):
            for k, (ox, oy) in enumerate(others):
                landed = outs[a].at[2 * ox + oy, :, half(a, c)]
                _remote(landed, landed, ici_s.at[3 * a + k], ici_r.at[3 * a + k], (ox, oy, c)).wait_recv()
                cp = _remote(landed, landed, d2d_s.at[3 * a + k], d2d_r.at[3 * a + k], sibling)
                cp.start()
                passed.append(cp)
        for k, (ox, oy) in enumerate(others):
            landed = conv_out.at[2 * ox + oy]
            _remote(landed, landed, ici_s.at[3 * n + k], ici_r.at[3 * n + k], (ox, oy, c)).wait_recv()
        for a in range(n):
            for k, (ox, oy) in enumerate(others):
                landed = outs[a].at[2 * ox + oy, :, half(a, 1 - c)]
                _remote(landed, landed, d2d_s.at[3 * a + k], d2d_r.at[3 * a + k], sibling).wait_recv()
        for cp in sends + passed:
            cp.wait_send()
        for cp in local:
            cp.wait()

    out_shape = [jax.ShapeDtypeStruct((NCHIP,) + s.shape, s.dtype) for s in list(shards) + [conv]]
    return _comm_call("gather_weights", body, list(shards) + [conv], out_shape,
                      [3 * n + 3, 3 * n + 3, 3 * n, 3 * n, n + 1])


def _exchange_halves(gs):
    n = len(gs)

    def body(*refs):
        ins, outs = refs[:n], refs[n:2 * n]
        send_s, recv_s = refs[2 * n:]
        x, y, c, _ = _place()
        copies = []
        for a in range(n):
            hr = ins[a].shape[1] // 2
            theirs = ins[a].at[:, pl.ds(pl.multiple_of((1 - c) * hr, 8), hr)]
            copies.append(_remote(theirs, outs[a], send_s.at[a], recv_s.at[a], (x, y, 1 - c)))
        for cp in copies:
            cp.start()
        for cp in copies:
            cp.wait()

    out_shape = [jax.ShapeDtypeStruct((NCHIP, g.shape[1] // 2, g.shape[2]), F32) for g in gs]
    return _comm_call("exchange_halves", body, gs, out_shape, [n, n])


def _scatter_partials(ps):
    n = len(ps)

    def body(*refs):
        ins, outs = refs[:n], refs[n:2 * n]
        send_s, recv_s = refs[2 * n:]
        x, y, c, others = _place()
        copies = []
        for a in range(n):
            for k, (ox, oy) in enumerate(others):
                copies.append(_remote(ins[a].at[2 * ox + oy], outs[a].at[k], send_s.at[3 * a + k],
                                      recv_s.at[3 * a + k], (ox, oy, c)))
        for cp in copies:
            cp.start()
        for cp in copies:
            cp.wait()

    out_shape = [jax.ShapeDtypeStruct((3,) + p.shape[1:], p.dtype) for p in ps]
    return _comm_call("scatter_partials", body, ps, out_shape, [3 * n, 3 * n])


def _join_halves(rs):
    n = len(rs)

    def body(*refs):
        ins = [refs[DEPTH * a:DEPTH * (a + 1)] for a in range(n)]
        outs = refs[DEPTH * n:(DEPTH + 1) * n]
        send_s, recv_s, loc = refs[(DEPTH + 1) * n:]
        x, y, c, _ = _place()
        copies, local = [], []
        for a in range(n):
            for l in range(DEPTH):
                i = DEPTH * a + l
                local.append(pltpu.make_async_copy(ins[a][l], outs[a].at[l, c], loc.at[i]))
                copies.append(_remote(ins[a][l], outs[a].at[l, c], send_s.at[i], recv_s.at[i], (x, y, 1 - c)))
        for cp in local + copies:
            cp.start()
        for a in range(n):
            for l in range(DEPTH):
                i = DEPTH * a + l
                landed = outs[a].at[l, 1 - c]
                _remote(landed, landed, send_s.at[i], recv_s.at[i], (x, y, 1 - c)).wait_recv()
        for cp in copies:
            cp.wait_send()
        for cp in local:
            cp.wait()

    flat = [r for pair in rs for r in pair]
    out_shape = [jax.ShapeDtypeStruct((DEPTH, 2) + pair[0].shape, F32) for pair in rs]
    return _comm_call("join_halves", body, flat, out_shape, [DEPTH * n, DEPTH * n, DEPTH * n])


NDEV = 8


def _allreduce_small(buf):
    r = buf.shape[0]

    def body(in_ref, out_ref, gath, send_s, recv_s):
        x, y, c, _ = _place()
        me = 4 * x + 2 * y + c
        copies = []
        for rel in range(1, NDEV):
            px = 1 - x if rel & 4 else x
            py = 1 - y if rel & 2 else y
            pc = 1 - c if rel & 1 else c
            copies.append((_remote(in_ref, gath.at[me], send_s.at[rel - 1], recv_s.at[rel - 1], (px, py, pc)),
                           4 * px + 2 * py + pc))
        for cp, _ in copies:
            cp.start()
        gath[me] = in_ref[...]
        for rel, (cp, peer) in enumerate(copies):
            landed = gath.at[peer]
            _remote(landed, landed, send_s.at[rel], recv_s.at[rel], (x, y, c)).wait_recv()
        for cp, _ in copies:
            cp.wait_send()
        total = gath[0]
        for d in range(1, NDEV):
            total = total + gath[d]
        out_ref[...] = total

    vm = pl.BlockSpec(memory_space=pltpu.VMEM)
    return pl.pallas_call(
        body, name="allreduce_small", in_specs=[vm], out_specs=vm, out_shape=jax.ShapeDtypeStruct((r, LANE), F32),
        scratch_shapes=[pltpu.VMEM((NDEV, r, LANE), F32), pltpu.SemaphoreType.DMA((NDEV - 1,)),
                        pltpu.SemaphoreType.DMA((NDEV - 1,))],
        compiler_params=pltpu.CompilerParams(has_side_effects=True, vmem_limit_bytes=VMEM_LIMIT),
    )(buf)


MAX_ROW_TILE = 512
BF16_ROWS = 16


def _row_tile(rows):
    for t in range(min(rows, MAX_ROW_TILE) // BF16_ROWS * BF16_ROWS, 0, -BF16_ROWS):
        if rows % t == 0:
            return t
    raise ValueError(rows)


def _sum_halves(g, theirs, c_arr):
    nch, rows, cols = g.shape
    hr = rows // 2
    tr = _row_tile(hr)

    def body(c_ref, g_ref, t_ref, o_ref, ob_ref):
        s = g_ref[...] + t_ref[...]
        o_ref[...] = s
        ob_ref[...] = s.astype(BF16)

    blk = pl.BlockSpec((None, tr, cols), lambda j, i, c_ref: (j, i, 0))
    return pl.pallas_call(
        body, name="sum_halves",
        grid_spec=pltpu.PrefetchScalarGridSpec(
            num_scalar_prefetch=1, grid=(nch, hr // tr),
            in_specs=[pl.BlockSpec((None, None, tr, cols), lambda j, i, c_ref: (j, c_ref[0], i, 0)), blk],
            out_specs=[blk, blk]),
        out_shape=[jax.ShapeDtypeStruct((nch, hr, cols), F32), jax.ShapeDtypeStruct((nch, hr, cols), BF16)],
        compiler_params=_cparams(("parallel", "parallel")),
    )(c_arr, g.reshape(nch, 2, hr, cols), theirs)


def _sum_chips(p, q, chip_arr):
    _, rows, cols = p.shape
    tr = _row_tile(rows)

    def body(chip_ref, p_ref, q0, q1, q2, o_ref):
        o_ref[...] = ((p_ref[...] + q0[...].astype(F32)) + q1[...].astype(F32)) + q2[...].astype(F32)

    qs = lambda k: pl.BlockSpec((None, tr, cols), lambda i, chip_ref: (k, i, 0))
    return pl.pallas_call(
        body, name="sum_chips",
        grid_spec=pltpu.PrefetchScalarGridSpec(
            num_scalar_prefetch=1, grid=(rows // tr,),
            in_specs=[pl.BlockSpec((None, tr, cols), lambda i, chip_ref: (chip_ref[0], i, 0)), qs(0), qs(1), qs(2)],
            out_specs=pl.BlockSpec((tr, cols), lambda i, chip_ref: (i, 0))),
        out_shape=jax.ShapeDtypeStruct((rows, cols), F32),
        compiler_params=_cparams(("parallel",)),
    )(chip_arr, p, q, q, q)


def _adamw(w, g, m, v):
    rows, cols = w.shape
    tr = _row_tile(rows)

    def body(w_ref, g_ref, m_ref, v_ref, d_ref, nm_ref, nv_ref):
        gv = g_ref[...]
        nm = ADAM_B1 * m_ref[...] + (1.0 - ADAM_B1) * gv
        nv = ADAM_B2 * v_ref[...] + (1.0 - ADAM_B2) * jnp.square(gv)
        m_hat = nm / (1.0 - ADAM_B1 ** ADAM_STEP)
        v_hat = nv / (1.0 - ADAM_B2 ** ADAM_STEP)
        d_ref[...] = -ADAM_LR * (m_hat / (jnp.sqrt(v_hat) + ADAM_EPS) + ADAM_WD * w_ref[...])
        nm_ref[...] = nm
        nv_ref[...] = nv

    blk = pl.BlockSpec((tr, cols), lambda i: (i, 0))
    return pl.pallas_call(
        body, name="adamw", grid=(rows // tr,), in_specs=[blk] * 4, out_specs=[blk] * 3,
        out_shape=[jax.ShapeDtypeStruct((rows, cols), F32)] * 3, compiler_params=_cparams(("parallel",)),
    )(w, g, m, v)


BIG = ("w_in", "w_out", "w_gate", "w_up", "w_down")
SMALL = ("norm_mix", "a_log", "dt_bias", "o_norm_g", "ln_v_g", "ln_v_b", "w_s", "b_s", "norm_ffn", "norm_final")
ORDER = ("norm_mix", "w_in", "conv_w", "a_log", "dt_bias", "o_norm_g", "ln_v_g", "ln_v_b", "w_s", "b_s", "w_out",
         "norm_ffn", "w_gate", "w_up", "w_down", "norm_final")


def _pack(arrs):
    flat = jnp.concatenate([a.reshape(-1) for a in arrs])
    rows = -(-flat.shape[0] // (BF16_ROWS * LANE)) * BF16_ROWS
    return jnp.pad(flat, (0, rows * LANE - flat.shape[0])).reshape(rows, LANE)


def _unpack(buf, like):
    flat = buf.reshape(-1)
    out, off = [], 0
    for a in like:
        out.append(flat[off:off + a.size].reshape(a.shape))
        off += a.size
    return out


def kernel(x, norm_mix, w_in, conv_w, a_log, dt_bias, o_norm_g, ln_v_g, ln_v_b, w_s, b_s, w_out, norm_ffn, w_gate, w_up, w_down, norm_final, loss_target, m_norm_mix, m_w_in, m_conv_w, m_a_log, m_dt_bias, m_o_norm_g, m_ln_v_g, m_ln_v_b, m_w_s, m_b_s, m_w_out, m_norm_ffn, m_w_gate, m_w_up, m_w_down, m_norm_final, v_norm_mix, v_w_in, v_conv_w, v_a_log, v_dt_bias, v_o_norm_g, v_ln_v_g, v_ln_v_b, v_w_s, v_b_s, v_w_out, v_norm_ffn, v_w_gate, v_w_up, v_w_down, v_norm_final):
    w = dict(norm_mix=norm_mix, w_in=w_in, conv_w=conv_w, a_log=a_log, dt_bias=dt_bias, o_norm_g=o_norm_g,
             ln_v_g=ln_v_g, ln_v_b=ln_v_b, w_s=w_s, b_s=b_s, w_out=w_out, norm_ffn=norm_ffn, w_gate=w_gate, w_up=w_up,
             w_down=w_down, norm_final=norm_final)
    m = dict(norm_mix=m_norm_mix, w_in=m_w_in, conv_w=m_conv_w, a_log=m_a_log, dt_bias=m_dt_bias, o_norm_g=m_o_norm_g,
             ln_v_g=m_ln_v_g, ln_v_b=m_ln_v_b, w_s=m_w_s, b_s=m_b_s, w_out=m_w_out, norm_ffn=m_norm_ffn,
             w_gate=m_w_gate, w_up=m_w_up, w_down=m_w_down, norm_final=m_norm_final)
    v = dict(norm_mix=v_norm_mix, w_in=v_w_in, conv_w=v_conv_w, a_log=v_a_log, dt_bias=v_dt_bias, o_norm_g=v_o_norm_g,
             ln_v_g=v_ln_v_g, ln_v_b=v_ln_v_b, w_s=v_w_s, b_s=v_b_s, w_out=v_w_out, norm_ffn=v_norm_ffn,
             w_gate=v_w_gate, w_up=v_w_up, w_down=v_w_down, norm_final=v_norm_final)
    chip = 2 * lax.axis_index("x") + lax.axis_index("y")
    chip_arr = jnp.reshape(chip, (1,)).astype(jnp.int32)
    c_arr = jnp.reshape(lax.axis_index("c"), (1,)).astype(jnp.int32)

    gathered = _gather_weights([w[n].astype(BF16) for n in BIG], conv_w)
    big = dict(zip(BIG + ("conv_w",), gathered))
    layers = [_layer_params(l, big, w) for l in range(DEPTH)]

    loss_lanes, dx, grads, d_norm_final = _local_step(x[0], loss_target[0], layers, norm_final[None])
    loss = lax.psum(loss_lanes[0, 0], ("x", "y", "c"))
    gl = [_reference_layout(g) for g in grads]

    mine = [gl[l][n] for n in BIG for l in range(DEPTH)]
    theirs = _exchange_halves(mine)
    sums = [_sum_halves(g, t, c_arr) for g, t in zip(mine, theirs)]
    arrived = _scatter_partials([s16 for _, s16 in sums])
    reduced = [_sum_chips(s32, q, chip_arr) for (s32, _), q in zip(sums, arrived)]
    joined = _join_halves([reduced[DEPTH * a:DEPTH * (a + 1)] for a in range(len(BIG))])
    g_out = {n: j.reshape(w[n].shape) for n, j in zip(BIG, joined)}

    small_g = [jnp.stack([gl[l][n] for l in range(DEPTH)]) for n in SMALL[:-1]] + [d_norm_final[0]]
    conv_g = jnp.stack([gl[l]["conv_w"] for l in range(DEPTH)])
    total = _allreduce_small(_pack(small_g + [conv_g]))
    *small_r, conv_r = _unpack(total, small_g + [conv_g])
    g_out.update(zip(SMALL, small_r))
    g_out["conv_w"] = lax.dynamic_slice_in_dim(conv_r, chip * conv_w.shape[2], conv_w.shape[2], axis=2)

    delta, new_m, new_v = {}, {}, {}
    for n in BIG:
        two_d = lambda a: a.reshape(-1, a.shape[-1])
        d, nm, nv = _adamw(two_d(w[n]), two_d(g_out[n]), two_d(m[n]), two_d(v[n]))
        delta[n], new_m[n], new_v[n] = (a.reshape(w[n].shape) for a in (d, nm, nv))
    rest = SMALL + ("conv_w",)
    like = [w[n] for n in rest]
    d, nm, nv = _adamw(*[_pack([src[n] for n in rest]) for src in (w, g_out, m, v)])
    for dst, buf in ((delta, d), (new_m, nm), (new_v, nv)):
        dst.update(zip(rest, _unpack(buf, like)))

    return (loss, dx[None], *[g_out[n] for n in ORDER], *[delta[n] for n in ORDER], *[new_m[n] for n in ORDER],
            *[new_v[n] for n in ORDER])
```

```python
import functools

import jax
import jax.numpy as jnp
from jax import lax
from jax.experimental import pallas as pl
from jax.experimental.pallas import tpu as pltpu

F32 = jnp.float32
BF16 = jnp.bfloat16
MESH = pl.DeviceIdType.MESH
ANY = pl.BlockSpec(memory_space=pl.ANY)
HIGHEST = lax.Precision.HIGHEST

T = 2048
D = 1024
DEPTH = 2
NCHIP = 4
HEADS = 4
HD = 128
HW = HEADS * HD
CH = 64
GCH = 128
IN_DIM = 3080
NP = 3200
BA_OFF = 3072
FF_SH = 704
EPS = 1e-6
LANE = 128
VMEM_LIMIT = 56 * 1024 * 1024

ADAM_LR = 0.001
ADAM_B1 = 0.9
ADAM_B2 = 0.999
ADAM_EPS = 1e-08
ADAM_WD = 0.01
ADAM_STEP = 10


def _cparams(sem=None):
    return pltpu.CompilerParams(dimension_semantics=sem, vmem_limit_bytes=VMEM_LIMIT)


_DIMS = {"nn": (((1,), (0,)), ((), ())), "nt": (((1,), (1,)), ((), ())), "tn": (((0,), (0,)), ((), ()))}


def _mm(name, mode, a, bs, *, tm, tn, tk, out_dtypes=(F32,), reduce_g=False, resid=None, extras=(), epilogue=None):
    nb = len(bs)
    ga = a.shape[0]
    gbs = [b.shape[0] for b in bs]
    g_n = max([ga] + gbs)
    if mode == "tn":
        k_n, m_n = a.shape[1:]
    else:
        m_n, k_n = a.shape[1:]
    n_n = bs[0].shape[1] if mode == "nt" else bs[0].shape[2]
    assert m_n % tm == 0 and n_n % tn == 0 and k_n % tk == 0, (name, m_n, n_n, k_n)
    mi, nj, kk = m_n // tm, n_n // tn, k_n // tk
    if reduce_g:
        grid = (mi, nj, g_n, kk)
        ids = lambda i, j, g, k: (g, i, j, k)
        n_red = g_n * kk
        red_idx = lambda: pl.program_id(2) * kk + pl.program_id(3)
        sem = ("parallel", "parallel", "arbitrary", "arbitrary")
    else:
        grid = (g_n, mi, nj, kk)
        ids = lambda g, i, j, k: (g, i, j, k)
        n_red = kk
        red_idx = lambda: pl.program_id(3)
        sem = ("parallel", "parallel", "parallel", "arbitrary")

    def pick(gsz, g):
        return g if gsz > 1 else 0

    def a_map(*p):
        g, i, j, k = ids(*p)
        return (pick(ga, g), k, i) if mode == "tn" else (pick(ga, g), i, k)

    def b_map(gsz):
        def f(*p):
            g, i, j, k = ids(*p)
            return (pick(gsz, g), j, k) if mode == "nt" else (pick(gsz, g), k, j)
        return f

    def o_map(gsz):
        def f(*p):
            g, i, j, k = ids(*p)
            return (0 if reduce_g else pick(gsz, g), i, j)
        return f

    a_spec = pl.BlockSpec((None, tk, tm) if mode == "tn" else (None, tm, tk), a_map)
    b_specs = [pl.BlockSpec((None, tn, tk) if mode == "nt" else (None, tk, tn), b_map(gs)) for gs in gbs]
    x_specs = [pl.BlockSpec((None, tm, tn), o_map(e.shape[0])) for e in extras]
    r_specs = [pl.BlockSpec((None, tm, tn), o_map(resid.shape[0]))] if resid is not None else []
    g_out = 1 if reduce_g else g_n
    out_shape = [jax.ShapeDtypeStruct((g_out, m_n, n_n), dt) for dt in out_dtypes]
    out_specs = [pl.BlockSpec((None, tm, tn), o_map(g_out)) for _ in out_dtypes]
    nx, nr, no = len(extras), len(r_specs), len(out_dtypes)
    dims = _DIMS[mode]

    def body(*refs):
        a_ref = refs[0]
        b_refs = refs[1:1 + nb]
        x_refs = refs[1 + nb:1 + nb + nx]
        r_refs = refs[1 + nb + nx:1 + nb + nx + nr]
        o_refs = refs[1 + nb + nx + nr:1 + nb + nx + nr + no]
        acc_refs = refs[1 + nb + nx + nr + no:]
        r = red_idx()
        av = a_ref[...]
        for b_ref, acc in zip(b_refs, acc_refs):
            p = lax.dot_general(av, b_ref[...], dims, preferred_element_type=F32)

            @pl.when(r == 0)
            def _():
                acc[...] = p

            @pl.when(r > 0)
            def _():
                acc[...] += p

        @pl.when(r == n_red - 1)
        def _():
            accs = [acc[...] for acc in acc_refs]
            if r_refs:
                accs[0] = accs[0] + r_refs[0][...]
            outs = epilogue(accs, [x[...] for x in x_refs]) if epilogue is not None else accs
            for o_ref, o in zip(o_refs, outs):
                o_ref[...] = o.astype(o_ref.dtype)

    return pl.pallas_call(
        body, name=name, grid=grid,
        in_specs=[a_spec] + b_specs + x_specs + r_specs,
        out_specs=out_specs, out_shape=out_shape,
        scratch_shapes=[pltpu.VMEM((tm, tn), F32) for _ in range(nb)],
        compiler_params=_cparams(sem),
    )(a, *bs, *extras, *([resid] if resid is not None else []))


def _sigmoid(x):
    return 1.0 / (1.0 + jnp.exp(-x))


def _silu(x):
    return x * _sigmoid(x)


def _gelu(x):
    return 0.5 * x * (1.0 + jnp.tanh(0.7978845608028654 * (x + 0.044715 * (x * x * x))))


def _rms_fn(h, gain):
    return h * lax.rsqrt(jnp.mean(h * h, axis=-1, keepdims=True) + EPS) * gain


def _shift_impl(x, s):
    n = x.shape[0]
    rolled = pltpu.roll(x, s % n, 0)
    row = lax.broadcasted_iota(jnp.int32, x.shape, 0)
    return jnp.where((row >= s) & (row < n + s), rolled, 0.0)


@functools.partial(jax.custom_vjp, nondiff_argnums=(1,))
def _shift(x, s):
    return _shift_impl(x, s)


def _shift_fwd(x, s):
    return _shift_impl(x, s), None


def _shift_bwd(s, _, g):
    return (_shift_impl(g, -s),)


_shift.defvjp(_shift_fwd, _shift_bwd)


def _prep_fn(x, w, qk_scale, is_v):
    y = x * w[3:4, :]
    for i in range(3):
        y = y + _shift(x, 3 - i) * w[i:i + 1, :]
    y = _silu(y)
    nrm = lax.rsqrt(jnp.sum(y * y, axis=-1, keepdims=True) + EPS) * qk_scale
    return y * jnp.where(is_v, 1.0, nrm)


def _softplus(x):
    return jnp.maximum(x, 0.0) + jnp.log(1.0 + jnp.exp(-jnp.abs(x)))


def _gates_fn(ba, a_log, dt_bias):
    lane = lax.broadcasted_iota(jnp.int32, ba.shape, 1)
    beta = _sigmoid(ba)
    g = -jnp.exp(a_log) * _softplus(ba + dt_bias)
    return jnp.where(lane < HEADS, beta, g)


def _dot16(a, b, dims=_DIMS["nn"]):
    return lax.dot_general(a.astype(BF16), b.astype(BF16), dims, preferred_element_type=F32)


def _dot32(a, b):
    return jnp.dot(a, b, preferred_element_type=F32, precision=HIGHEST)


def _dot3(a, b, dims=_DIMS["nn"]):
    return lax.dot_general(a, b, dims, preferred_element_type=F32, precision=lax.Precision.HIGH)


def _tri_inverses(mats):
    row = lax.broadcasted_iota(jnp.int32, (CH, CH), 0)
    col = lax.broadcasted_iota(jnp.int32, (CH, CH), 1)
    eye = (row == col).astype(F32)
    ts = [eye - a for a in mats]
    ps = list(mats)
    for _ in range(5):
        ps = [_dot3(p, p) for p in ps]
        ts = [t + _dot3(t, p) for t, p in zip(ts, ps)]
    return ts


@jax.custom_vjp
def _tri_solves(mats, rhs):
    return [_dot3(t, b) for t, b in zip(_tri_inverses(mats), rhs)]


def _tri_solves_fwd(mats, rhs):
    ts = _tri_inverses(mats)
    xs = [_dot3(t, b) for t, b in zip(ts, rhs)]
    return xs, (ts, xs)


def _tri_solves_bwd(res, dxs):
    ts, xs = res
    dbs = [_dot3(t, dx, _DIMS["tn"]) for t, dx in zip(ts, dxs)]
    return [-_dot3(db, x, _DIMS["nt"]) for db, x in zip(dbs, xs)], dbs


_tri_solves.defvjp(_tri_solves_fwd, _tri_solves_bwd)


def _chunk_prep_fn(xs, bgs):
    row = lax.broadcasted_iota(jnp.int32, (CH, CH), 0)
    col = lax.broadcasted_iota(jnp.int32, (CH, CH), 1)
    incl = row >= col
    strict = row > col
    lmat = incl.astype(F32)
    n = len(xs)
    items = [(i, h) for i in range(n) for h in range(HEADS)]
    part = lambda i, h, c: xs[i][:, c * HW + h * HD:c * HW + (h + 1) * HD]
    q = [part(i, h, 0) for i, h in items]
    k = [part(i, h, 1) for i, h in items]
    v = [part(i, h, 2) for i, h in items]
    beta = [bgs[i][:, h:h + 1] for i, h in items]
    gc_all = [_dot32(lmat, bg) for bg in bgs]
    gc = [gc_all[i][:, HEADS + h:HEADS + h + 1] for i, h in items]
    gmat = [jnp.where(strict, jnp.broadcast_to(bgs[i][:, HEADS + h:HEADS + h + 1], (CH, CH)), 0.0) for i, h in items]
    diff = [_dot3(lmat, m) for m in gmat]
    decay = [jnp.where(incl, jnp.exp(jnp.where(incl, d, 0.0)), 0.0) for d in diff]
    k_beta = [kk * b for kk, b in zip(k, beta)]
    kk_t = [_dot16(kb, kk, _DIMS["nt"]) for kb, kk in zip(k_beta, k)]
    qk_t = [_dot16(qq, kk, _DIMS["nt"]) for qq, kk in zip(q, k)]
    a = [jnp.where(strict, m * d, 0.0) for m, d in zip(kk_t, decay)]
    eg = [jnp.exp(g) for g in gc]
    rhs = [jnp.concatenate([vv * b, kb * e], axis=-1) for vv, b, kb, e in zip(v, beta, k_beta, eg)]
    uw = _tri_solves(a, rhs)
    qk = [m * d for m, d in zip(qk_t, decay)]
    g_last = [g[CH - 1:CH, :] for g in gc]
    qe = [qq * e for qq, e in zip(q, eg)]
    kd = [kk * jnp.exp(gl - g) for kk, gl, g in zip(k, g_last, gc)]
    egl = [jnp.broadcast_to(jnp.exp(gl), (1, HD)) for gl in g_last]
    out = []
    for i in range(n):
        mine = slice(i * HEADS, (i + 1) * HEADS)
        cat = lambda vals: jnp.concatenate(vals[mine], axis=-1)
        out.append((cat([x[:, :HD] for x in uw]), cat([x[:, HD:] for x in uw]), cat(qe), cat(kd),
                    jnp.concatenate([m[None] for m in qk[mine]], axis=0), cat(egl)))
    return out


def _chunk_state_fn(u, w, qe, kd, qk, egl, s):
    ws = [_dot16(a, b) for a, b in zip(w, s)]
    qs = [_dot16(a, b) for a, b in zip(qe, s)]
    v_new = [a - b for a, b in zip(u, ws)]
    o = [a + _dot16(b, c) for a, b, c in zip(qs, qk, v_new)]
    s_new = [a * e + _dot16(b, c, _DIMS["tn"]) for a, e, b, c in zip(s, egl, kd, v_new)]
    return o, s_new


def _mix_fn(o, z, ur, vr, ong, lng, lnb, ws, bst):
    row = lax.broadcasted_iota(jnp.int32, (GCH, GCH), 0)
    col = lax.broadcasted_iota(jnp.int32, (GCH, GCH), 1)
    causal = row >= col
    ug = _gelu(ur)
    vg = _gelu(vr)
    outs_dn, outs_gm = [], []
    for h in range(HEADS):
        sl = slice(h * HD, (h + 1) * HD)
        oh = o[:, sl]
        oh = oh * lax.rsqrt(jnp.mean(oh * oh, axis=-1, keepdims=True) + EPS)
        outs_dn.append(oh * ong * _silu(z[:, sl]))
        vh = vg[:, sl]
        mu = jnp.mean(vh, axis=-1, keepdims=True)
        var = jnp.mean(jnp.square(vh - mu), axis=-1, keepdims=True)
        vn = (vh - mu) * lax.rsqrt(var + EPS) * lng[:, sl] + lnb[:, sl]
        sp = _dot16(jnp.where(causal, ws[h], 0.0), vn) + bst[:, h:h + 1]
        outs_gm.append(ug[:, sl] * sp)
    return jnp.concatenate(outs_dn + outs_gm, axis=-1)


def _loss_fn(h, gain, tgt):
    y = _rms_fn(h, gain)
    return 0.5 * jnp.sum(jnp.mean(jnp.square(y - tgt), axis=-1))


RT = 256


def _rows(n=D):
    return pl.BlockSpec((RT, n), lambda i: (i, 0))


def _whole(shape):
    nd = len(shape)
    return pl.BlockSpec(shape, lambda i: (0,) * nd)


def _rmsnorm(name, h, gain):
    def body(h_ref, g_ref, o_ref):
        o_ref[...] = _rms_fn(h_ref[...], g_ref[...]).astype(BF16)

    return pl.pallas_call(
        body, name=name, grid=(T // RT,), in_specs=[_rows(), _whole((1, D))], out_specs=_rows(),
        out_shape=jax.ShapeDtypeStruct((T, D), BF16), compiler_params=_cparams(("parallel",)),
    )(h, gain)


def _rmsnorm_bwd(name, dhn, h, gain, resid):
    def body(dhn_ref, h_ref, g_ref, r_ref, dh_ref, dg_ref):
        _, vjp = jax.vjp(_rms_fn, h_ref[...], g_ref[...])
        dh, dg = vjp(dhn_ref[...])
        dh_ref[...] = r_ref[...] + dh

        @pl.when(pl.program_id(0) == 0)
        def _():
            dg_ref[...] = dg

        @pl.when(pl.program_id(0) > 0)
        def _():
            dg_ref[...] += dg

    return pl.pallas_call(
        body, name=name, grid=(T // RT,), in_specs=[_rows(), _rows(), _whole((1, D)), _rows()],
        out_specs=[_rows(), _whole((1, D))],
        out_shape=[jax.ShapeDtypeStruct((T, D), F32), jax.ShapeDtypeStruct((1, D), F32)],
        compiler_params=_cparams(("arbitrary",)),
    )(dhn, h, gain, resid)


def _loss_head(h, gain, tgt):
    def body(h_ref, g_ref, t_ref, l_ref, dh_ref, dg_ref):
        loss, vjp = jax.vjp(lambda hh, gg: _loss_fn(hh, gg, t_ref[...]), h_ref[...], g_ref[...])
        dh, dg = vjp(jnp.ones((), F32))
        dh_ref[...] = dh
        lv = jnp.full((1, LANE), loss, F32)

        @pl.when(pl.program_id(0) == 0)
        def _():
            dg_ref[...] = dg
            l_ref[...] = lv

        @pl.when(pl.program_id(0) > 0)
        def _():
            dg_ref[...] += dg
            l_ref[...] += lv

    return pl.pallas_call(
        body, name="loss_head", grid=(T // RT,), in_specs=[_rows(), _whole((1, D)), _rows()],
        out_specs=[_whole((1, LANE)), _rows(), _whole((1, D))],
        out_shape=[jax.ShapeDtypeStruct((1, LANE), F32), jax.ShapeDtypeStruct((T, D), F32),
                   jax.ShapeDtypeStruct((1, D), F32)],
        compiler_params=_cparams(("arbitrary",)),
    )(h, gain, tgt)


def _prep_flags():
    j = pl.program_id(0)
    qk_scale = jnp.where(j < HEADS, HD ** -0.5, 1.0).astype(F32)
    return qk_scale, j >= 2 * HEADS


def _prep(proj, conv_w):
    def body(x_ref, w_ref, o_ref):
        qk_scale, is_v = _prep_flags()
        o_ref[...] = _prep_fn(x_ref[...], w_ref[...], qk_scale, is_v)

    col = lambda j: (0, j)
    return pl.pallas_call(
        body, name="gdn_prep", grid=(3 * HEADS,),
        in_specs=[pl.BlockSpec((T, HD), col), pl.BlockSpec((4, HD), col)], out_specs=pl.BlockSpec((T, HD), col),
        out_shape=jax.ShapeDtypeStruct((T, 3 * HW), F32), compiler_params=_cparams(("parallel",)),
    )(proj, conv_w)


def _prep_bwd(proj, conv_w, dqkv):
    def body(x_ref, w_ref, d_ref, dx_ref, dw_ref):
        qk_scale, is_v = _prep_flags()
        _, vjp = jax.vjp(lambda x, w: _prep_fn(x, w, qk_scale, is_v), x_ref[...], w_ref[...])
        dx, dw = vjp(d_ref[...])
        dx_ref[...] = dx
        dw_ref[...] = dw

    col = lambda j: (0, j)
    return pl.pallas_call(
        body, name="gdn_prep_bwd", grid=(3 * HEADS,),
        in_specs=[pl.BlockSpec((T, HD), col), pl.BlockSpec((4, HD), col), pl.BlockSpec((T, HD), col)],
        out_specs=[pl.BlockSpec((T, HD), col), pl.BlockSpec((4, HD), col)],
        out_shape=[jax.ShapeDtypeStruct((T, 3 * HW), F32), jax.ShapeDtypeStruct((4, 3 * HW), F32)],
        compiler_params=_cparams(("parallel",)),
    )(proj, conv_w, dqkv)


BA_BLK = BA_OFF // LANE


def _gates(proj, a_log, dt_bias):
    def body(x_ref, a_ref, d_ref, o_ref):
        o_ref[...] = _gates_fn(x_ref[...], a_ref[...], d_ref[...])

    return pl.pallas_call(
        body, name="gdn_gates", grid=(1,),
        in_specs=[pl.BlockSpec((T, LANE), lambda i: (0, BA_BLK)), _whole((1, LANE)), _whole((1, LANE))],
        out_specs=_whole((T, LANE)),
        out_shape=jax.ShapeDtypeStruct((T, LANE), F32), compiler_params=_cparams(("arbitrary",)),
    )(proj, a_log, dt_bias)


def _gates_bwd(proj, a_log, dt_bias, dbg):
    def body(x_ref, a_ref, d_ref, dbg_ref, dx_ref, da_ref, dd_ref):
        _, vjp = jax.vjp(_gates_fn, x_ref[...], a_ref[...], d_ref[...])
        dx_ref[...], da_ref[...], dd_ref[...] = vjp(dbg_ref[...])

    return pl.pallas_call(
        body, name="gdn_gates_bwd", grid=(1,),
        in_specs=[pl.BlockSpec((T, LANE), lambda i: (0, BA_BLK)), _whole((1, LANE)), _whole((1, LANE)),
                  _whole((T, LANE))],
        out_specs=[_whole((T, LANE)), _whole((1, LANE)), _whole((1, LANE))],
        out_shape=[jax.ShapeDtypeStruct((T, LANE), F32), jax.ShapeDtypeStruct((1, LANE), F32),
                   jax.ShapeDtypeStruct((1, LANE), F32)],
        compiler_params=_cparams(("arbitrary",)),
    )(proj, a_log, dt_bias, dbg)


NCK = T // CH
CPS = 2


def _chunk_prep_specs(rev=False):
    at = (lambda n: NCK - 1 - n) if rev else (lambda n: n)
    wide = pl.BlockSpec((CH, HW), lambda n: (at(n), 0))
    return [wide, wide, wide, wide, pl.BlockSpec((HEADS, CH, CH), lambda n: (0, at(n), 0)),
            pl.BlockSpec((None, 1, HW), lambda n: (at(n), 0, 0))]


def _chunk_prep_shapes(dtypes):
    shp = [(T, HW), (T, HW), (T, HW), (T, HW), (HEADS, T, CH), (NCK, 1, HW)]
    return [jax.ShapeDtypeStruct(s, dt) for s, dt in zip(shp, dtypes)]


def _chunk_prep(qkv, bg):
    def body(x_ref, bg_ref, *o_refs):
        rows = [slice(ci * CH, (ci + 1) * CH) for ci in range(CPS)]
        res = _chunk_prep_fn([x_ref[r, :] for r in rows], [bg_ref[r, :] for r in rows])
        for ci, (u, w, qe, kd, qk, egl) in enumerate(res):
            for o_ref, val in zip(o_refs[:4], (u, w, qe, kd)):
                o_ref[rows[ci], :] = val.astype(o_ref.dtype)
            o_refs[4][:, rows[ci], :] = qk.astype(BF16)
            o_refs[5][ci] = egl

    wide = pl.BlockSpec((CPS * CH, HW), lambda n: (n, 0))
    return pl.pallas_call(
        body, name="gdn_chunk_prep", grid=(NCK // CPS,),
        in_specs=[pl.BlockSpec((CPS * CH, 3 * HW), lambda n: (n, 0)), pl.BlockSpec((CPS * CH, LANE), lambda n: (n, 0))],
        out_specs=[wide, wide, wide, wide, pl.BlockSpec((HEADS, CPS * CH, CH), lambda n: (0, n, 0)),
                   pl.BlockSpec((CPS, 1, HW), lambda n: (n, 0, 0))],
        out_shape=_chunk_prep_shapes((F32, BF16, BF16, BF16, BF16, F32)),
        compiler_params=_cparams(("parallel",)),
    )(qkv, bg)


def _chunk_prep_bwd(qkv, bg, cots):
    def body(x_ref, bg_ref, du, dw, dqe, dkd, dqk, degl, dx_ref, dbg_ref):
        rows = [slice(ci * CH, (ci + 1) * CH) for ci in range(CPS)]
        _, vjp = jax.vjp(_chunk_prep_fn, [x_ref[r, :] for r in rows], [bg_ref[r, :] for r in rows])
        dxs, dbgs = vjp([(du[r, :], dw[r, :], dqe[r, :], dkd[r, :], dqk[:, r, :], degl[ci])
                         for ci, r in enumerate(rows)])
        for r, dx, dbg in zip(rows, dxs, dbgs):
            dx_ref[r, :] = dx
            dbg_ref[r, :] = dbg

    wide = pl.BlockSpec((CPS * CH, HW), lambda n: (n, 0))
    return pl.pallas_call(
        body, name="gdn_chunk_prep_bwd", grid=(NCK // CPS,),
        in_specs=[pl.BlockSpec((CPS * CH, 3 * HW), lambda n: (n, 0)), pl.BlockSpec((CPS * CH, LANE), lambda n: (n, 0)),
                  wide, wide, wide, wide, pl.BlockSpec((HEADS, CPS * CH, CH), lambda n: (0, n, 0)),
                  pl.BlockSpec((CPS, 1, HW), lambda n: (n, 0, 0))],
        out_specs=[pl.BlockSpec((CPS * CH, 3 * HW), lambda n: (n, 0)), pl.BlockSpec((CPS * CH, LANE), lambda n: (n, 0))],
        out_shape=[jax.ShapeDtypeStruct((T, 3 * HW), F32), jax.ShapeDtypeStruct((T, LANE), F32)],
        compiler_params=_cparams(("parallel",)),
    )(qkv, bg, *cots)


def _head_args(refs):
    u, w, qe, kd, qk, egl = refs
    sls = [slice(h * HD, (h + 1) * HD) for h in range(HEADS)]
    return ([u[:, sl] for sl in sls], [w[:, sl].astype(F32) for sl in sls], [qe[:, sl].astype(F32) for sl in sls],
            [kd[:, sl].astype(F32) for sl in sls], [qk[h].astype(F32) for h in range(HEADS)],
            [egl[:, sl] for sl in sls])


def _chunk_scan(prep):
    def body(*refs):
        o_ref, sh_ref, s_ref = refs[6:]

        @pl.when(pl.program_id(0) == 0)
        def _():
            s_ref[...] = jnp.zeros_like(s_ref)

        s = [s_ref[h] for h in range(HEADS)]
        for h in range(HEADS):
            sh_ref[h, 0] = s[h]
        o, s_new = _chunk_state_fn(*_head_args(refs[:6]), s)
        for h in range(HEADS):
            o_ref[:, h * HD:(h + 1) * HD] = o[h]
            s_ref[h] = s_new[h]

    return pl.pallas_call(
        body, name="gdn_scan", grid=(NCK,), in_specs=_chunk_prep_specs(),
        out_specs=[pl.BlockSpec((CH, HW), lambda n: (n, 0)), pl.BlockSpec((HEADS, 1, HD, HD), lambda n: (0, n, 0, 0))],
        out_shape=[jax.ShapeDtypeStruct((T, HW), F32), jax.ShapeDtypeStruct((HEADS, NCK, HD, HD), F32)],
        scratch_shapes=[pltpu.VMEM((HEADS, HD, HD), F32)], compiler_params=_cparams(("arbitrary",)),
    )(*prep)


def _chunk_scan_bwd(prep, s_hist, do):
    def body(*refs):
        sh_ref, do_ref = refs[6:8]
        d_refs = refs[8:14]
        ds_ref = refs[14]

        @pl.when(pl.program_id(0) == 0)
        def _():
            ds_ref[...] = jnp.zeros_like(ds_ref)

        sls = [slice(h * HD, (h + 1) * HD) for h in range(HEADS)]
        _, vjp = jax.vjp(_chunk_state_fn, *_head_args(refs[:6]), [sh_ref[h, 0] for h in range(HEADS)])
        du, dw, dqe, dkd, dqk, degl, ds = vjp(([do_ref[:, sl] for sl in sls], [ds_ref[h] for h in range(HEADS)]))
        for h, sl in enumerate(sls):
            for d_ref, val in zip(d_refs[:4], (du, dw, dqe, dkd)):
                d_ref[:, sl] = val[h]
            d_refs[4][h] = dqk[h]
            d_refs[5][:, sl] = degl[h]
            ds_ref[h] = ds[h]

    rev = lambda n: NCK - 1 - n
    return pl.pallas_call(
        body, name="gdn_scan_bwd", grid=(NCK,),
        in_specs=_chunk_prep_specs(rev=True) + [pl.BlockSpec((HEADS, 1, HD, HD), lambda n: (0, rev(n), 0, 0)),
                                                pl.BlockSpec((CH, HW), lambda n: (rev(n), 0))],
        out_specs=_chunk_prep_specs(rev=True), out_shape=_chunk_prep_shapes((F32,) * 6),
        scratch_shapes=[pltpu.VMEM((HEADS, HD, HD), F32)], compiler_params=_cparams(("arbitrary",)),
    )(*prep, s_hist, do)


def _mix_specs():
    pc = lambda c: pl.BlockSpec((GCH, HW), lambda i: (i, c))
    return [pl.BlockSpec((GCH, HW), lambda i: (i, 0)), pc(3), pc(4), pc(5), _whole((1, HD)), _whole((1, HW)),
            _whole((1, HW)), _whole((HEADS, GCH, GCH)), _whole((GCH, LANE))]


def _mix(o, proj, ong, lng, lnb, ws, bst):
    def body(o_ref, z_ref, u_ref, v_ref, ong_ref, lng_ref, lnb_ref, ws_ref, bs_ref, m_ref):
        m_ref[...] = _mix_fn(o_ref[...], z_ref[...], u_ref[...], v_ref[...], ong_ref[...], lng_ref[...],
                             lnb_ref[...], ws_ref[...], bs_ref[...]).astype(BF16)

    return pl.pallas_call(
        body, name="mix", grid=(T // GCH,), in_specs=_mix_specs(),
        out_specs=pl.BlockSpec((GCH, D), lambda i: (i, 0)), out_shape=jax.ShapeDtypeStruct((T, D), BF16),
        compiler_params=_cparams(("parallel",)),
    )(o, proj, proj, proj, ong, lng, lnb, ws, bst)


def _mix_bwd(o, proj, ong, lng, lnb, ws, bst, dmix):
    def body(o_ref, z_ref, u_ref, v_ref, ong_ref, lng_ref, lnb_ref, ws_ref, bs_ref, dm_ref,
             do_ref, dzuv_ref, dong_ref, dlng_ref, dlnb_ref, dws_ref, dbs_ref):
        _, vjp = jax.vjp(_mix_fn, o_ref[...], z_ref[...], u_ref[...], v_ref[...], ong_ref[...], lng_ref[...],
                         lnb_ref[...], ws_ref[...], bs_ref[...])
        do, dz, du, dv, dong, dlng, dlnb, dws, dbs = vjp(dm_ref[...])
        do_ref[...] = do
        dzuv_ref[:, 0:HW] = dz
        dzuv_ref[:, HW:2 * HW] = du
        dzuv_ref[:, 2 * HW:3 * HW] = dv
        acc = [(dong_ref, dong), (dlng_ref, dlng), (dlnb_ref, dlnb), (dws_ref, dws), (dbs_ref, dbs)]

        @pl.when(pl.program_id(0) == 0)
        def _():
            for r, val in acc:
                r[...] = val

        @pl.when(pl.program_id(0) > 0)
        def _():
            for r, val in acc:
                r[...] += val

    shp = lambda *s: jax.ShapeDtypeStruct(s, F32)
    return pl.pallas_call(
        body, name="mix_bwd", grid=(T // GCH,),
        in_specs=_mix_specs() + [pl.BlockSpec((GCH, D), lambda i: (i, 0))],
        out_specs=[pl.BlockSpec((GCH, HW), lambda i: (i, 0)), pl.BlockSpec((GCH, 3 * HW), lambda i: (i, 0)),
                   _whole((1, HD)), _whole((1, HW)), _whole((1, HW)), _whole((HEADS, GCH, GCH)), _whole((GCH, LANE))],
        out_shape=[shp(T, HW), shp(T, 3 * HW), shp(1, HD), shp(1, HW), shp(1, HW), shp(HEADS, GCH, GCH), shp(GCH, LANE)],
        compiler_params=_cparams(("arbitrary",)),
    )(o, proj, proj, proj, ong, lng, lnb, ws, bst, dmix)


def _swiglu_epilogue(accs, _):
    gate, up = accs
    return [gate, up, _silu(gate) * up]


def _swiglu_bwd_epilogue(accs, extras):
    dact = accs[0]
    gate, up = (e.astype(F32) for e in extras)
    sg = _sigmoid(gate)
    return [dact * up * (sg * (1.0 + gate * (1.0 - sg))), dact * (gate * sg)]


def _layer_fwd(h, p):
    hn = _rmsnorm("rms_mix", h, p["norm_mix"])
    proj = _mm("in_proj", "nn", hn[None], [p["w_in"][None]], tm=1024, tn=640, tk=D)[0][0]
    qkv = _prep(proj, p["conv_w"])
    bg = _gates(proj, p["a_log"], p["dt_bias"])
    prep = _chunk_prep(qkv, bg)
    o, s_hist = _chunk_scan(prep)
    mix = _mix(o, proj, p["o_norm_g"], p["ln_v_g"], p["ln_v_b"], p["w_s"], p["bst"])
    h1 = _mm("out_proj", "nn", mix[None], [p["w_out"][None]], tm=1024, tn=512, tk=D, resid=h[None])[0][0]
    h2n = _rmsnorm("rms_ffn", h1, p["norm_ffn"])
    gate, up, act = _mm("ffn_in", "nn", h2n[None], [p["w_gate"], p["w_up"]], tm=1024, tn=FF_SH, tk=D,
                        out_dtypes=(BF16, BF16, BF16), epilogue=_swiglu_epilogue)
    h2 = _mm("ffn_out", "nn", act, [p["w_down"]], tm=1024, tn=512, tk=FF_SH, reduce_g=True, resid=h1[None])[0][0]
    saved = dict(h=h, hn=hn, proj=proj, qkv=qkv, bg=bg, prep=prep, o=o, s_hist=s_hist, mix=mix, h1=h1, h2n=h2n,
                 gate=gate, up=up, act=act)
    return h2, saved


def _layer_bwd(dh2, p, s):
    dh2b = dh2.astype(BF16)[None]
    dgate, dup = _mm("ffn_out_bwd", "nt", dh2b, [p["w_down"]], tm=1024, tn=FF_SH, tk=D, out_dtypes=(BF16, BF16),
                     extras=(s["gate"], s["up"]), epilogue=_swiglu_bwd_epilogue)
    dh2n = _mm("ffn_gate_bwd", "nt", dgate, [p["w_gate"]], tm=1024, tn=512, tk=FF_SH, reduce_g=True)[0]
    dh2n = _mm("ffn_up_bwd", "nt", dup, [p["w_up"]], tm=1024, tn=512, tk=FF_SH, reduce_g=True, resid=dh2n)[0][0]
    dh1, d_norm_ffn = _rmsnorm_bwd("rms_ffn_bwd", dh2n, s["h1"], p["norm_ffn"], dh2)
    d_w_down = _mm("ffn_wdown_grad", "tn", s["act"], [dh2b], tm=FF_SH, tn=512, tk=1024)[0]
    d_w_gate = _mm("ffn_wgate_grad", "tn", s["h2n"][None], [dgate], tm=512, tn=FF_SH, tk=1024)[0]
    d_w_up = _mm("ffn_wup_grad", "tn", s["h2n"][None], [dup], tm=512, tn=FF_SH, tk=1024)[0]
    dh1b = dh1.astype(BF16)[None]
    dmix = _mm("out_proj_bwd", "nt", dh1b, [p["w_out"][None]], tm=1024, tn=512, tk=D)[0][0]
    d_w_out = _mm("out_proj_wgrad", "tn", s["mix"][None], [dh1b], tm=512, tn=512, tk=1024)[0][0]
    do, dzuv, d_ong, d_lng, d_lnb, d_ws, d_bst = _mix_bwd(
        s["o"], s["proj"], p["o_norm_g"], p["ln_v_g"], p["ln_v_b"], p["w_s"], p["bst"], dmix)
    dqkv, dbg = _chunk_prep_bwd(s["qkv"], s["bg"], _chunk_scan_bwd(s["prep"], s["s_hist"], do))
    dqkv_x, d_conv = _prep_bwd(s["proj"], p["conv_w"], dqkv)
    dba, d_a_log, d_dt_bias = _gates_bwd(s["proj"], p["a_log"], p["dt_bias"], dbg)
    dproj = jnp.concatenate([dqkv_x, dzuv, dba], axis=1).astype(BF16)[None]
    dhn = _mm("in_proj_bwd", "nt", dproj, [p["w_in"][None]], tm=1024, tn=512, tk=640)[0][0]
    dh, d_norm_mix = _rmsnorm_bwd("rms_mix_bwd", dhn, s["h"], p["norm_mix"], dh1)
    d_w_in = _mm("in_proj_wgrad", "tn", s["hn"][None], [dproj], tm=512, tn=640, tk=1024)[0][0]
    grads = dict(norm_mix=d_norm_mix, w_in=d_w_in, conv_w=d_conv, a_log=d_a_log, dt_bias=d_dt_bias, o_norm_g=d_ong,
                 ln_v_g=d_lng, ln_v_b=d_lnb, w_s=d_ws, bst=d_bst, w_out=d_w_out, norm_ffn=d_norm_ffn,
                 w_gate=d_w_gate, w_up=d_w_up, w_down=d_w_down)
    return dh, grads


def _lanes(v, off=0):
    return jnp.zeros((1, LANE), F32).at[0, off:off + v.shape[0]].set(v)


def _layer_params(l, big, small):
    w_in = jnp.concatenate([big["w_in"][j, l] for j in range(NCHIP)], axis=1)
    w_in = jnp.concatenate([w_in[:, :2048], w_in[:, 2056:IN_DIM], w_in[:, 2048:2056],
                            jnp.zeros((D, NP - IN_DIM), w_in.dtype)], axis=1)
    return dict(
        w_in=w_in,
        w_out=big["w_out"][:, l].reshape(D, D),
        w_gate=big["w_gate"][:, l], w_up=big["w_up"][:, l], w_down=big["w_down"][:, l],
        conv_w=jnp.concatenate([big["conv_w"][j, l] for j in range(NCHIP)], axis=1),
        norm_mix=small["norm_mix"][l][None], norm_ffn=small["norm_ffn"][l][None],
        a_log=_lanes(small["a_log"][l], HEADS), dt_bias=_lanes(small["dt_bias"][l], HEADS),
        o_norm_g=small["o_norm_g"][l][None], ln_v_g=small["ln_v_g"][l][None], ln_v_b=small["ln_v_b"][l][None],
        w_s=small["w_s"][l],
        bst=jnp.pad(small["b_s"][l].T, ((0, 0), (0, LANE - HEADS))),
    )


def _reference_layout(g):
    w_in = g["w_in"]
    w_in = jnp.concatenate([w_in[:, :2048], w_in[:, BA_OFF:BA_OFF + 8], w_in[:, 2048:BA_OFF]], axis=1)
    return dict(
        w_in=w_in.reshape(D, NCHIP, IN_DIM // NCHIP).transpose(1, 0, 2),
        w_out=g["w_out"].reshape(NCHIP, D // NCHIP, D),
        w_gate=g["w_gate"], w_up=g["w_up"], w_down=g["w_down"],
        conv_w=g["conv_w"], norm_mix=g["norm_mix"][0], norm_ffn=g["norm_ffn"][0],
        a_log=g["a_log"][0, HEADS:2 * HEADS], dt_bias=g["dt_bias"][0, HEADS:2 * HEADS],
        o_norm_g=g["o_norm_g"][0], ln_v_g=g["ln_v_g"][0], ln_v_b=g["ln_v_b"][0], w_s=g["w_s"],
        b_s=g["bst"][:, :HEADS].T,
    )


def _local_step(x, tgt, layers, norm_final):
    h = x
    saved = []
    for p in layers:
        h, s = _layer_fwd(h, p)
        saved.append(s)
    loss, dh, d_norm_final = _loss_head(h, norm_final, tgt)
    grads = [None] * DEPTH
    for l in reversed(range(DEPTH)):
        dh, grads[l] = _layer_bwd(dh, layers[l], saved[l])
    return loss, dh, grads, d_norm_final


def _place():
    x, y, c = lax.axis_index("x"), lax.axis_index("y"), lax.axis_index("c")
    return x, y, c, [(1 - x, y), (x, 1 - y), (1 - x, 1 - y)]


def _remote(src, dst, send_sem, recv_sem, to):
    return pltpu.make_async_remote_copy(src_ref=src, dst_ref=dst, send_sem=send_sem, recv_sem=recv_sem,
                                        device_id=to, device_id_type=MESH)


def _comm_call(name, body, ins, out_shape, n_sems, aliases=None):
    return pl.pallas_call(
        body, name=name, in_specs=[ANY] * len(ins), out_specs=[ANY] * len(out_shape), out_shape=out_shape,
        scratch_shapes=[pltpu.SemaphoreType.DMA((n,)) for n in n_sems], input_output_aliases=aliases or {},
        compiler_params=pltpu.CompilerParams(has_side_effects=True),
    )(*ins)


def _gather_weights(shards, conv):
    n = len(shards)

    def body(*refs):
        ins, conv_in = refs[:n], refs[n]
        outs, conv_out = refs[n + 1:2 * n + 1], refs[2 * n + 1]
        ici_s, ici_r, d2d_s, d2d_r = refs[2 * n + 2:]
        x, y, c, others = _place()
        chip = 2 * x + y
        sibling = (x, y, 1 - c)

        def half(a, of_c):
            hr = ins[a].shape[1] // 2
            return pl.ds(pl.multiple_of(of_c * hr, 16), hr)

        sends = []
        for a in range(n):
            for k, (ox, oy) in enumerate(others):
                sends.append(_remote(ins[a].at[:, half(a, c)], outs[a].at[chip, :, half(a, c)],
                                     ici_s.at[3 * a + k], ici_r.at[3 * a + k], (ox, oy, c)))
        for k, (ox, oy) in enumerate(others):
            sends.append(_remote(conv_in, conv_out.at[chip], ici_s.at[3 * n + k], ici_r.at[3 * n + k], (ox, oy, c)))
        for cp in sends:
            cp.start()
        passed = []
        for a in range(n):
            for k, (ox, oy) in enumerate(others):
                landed = outs[a].at[2 * ox + oy, :, half(a, c)]
                _remote(landed, landed, ici_s.at[3 * a + k], ici_r.at[3 * a + k], (ox, oy, c)).wait_recv()
                cp = _remote(landed, landed, d2d_s.at[3 * a + k], d2d_r.at[3 * a + k], sibling)
                cp.start()
                passed.append(cp)
        for k, (ox, oy) in enumerate(others):
            landed = conv_out.at[2 * ox + oy]
            _remote(landed, landed, ici_s.at[3 * n + k], ici_r.at[3 * n + k], (ox, oy, c)).wait_recv()
        for a in range(n):
            for k, (ox, oy) in enumerate(others):
                landed = outs[a].at[2 * ox + oy, :, half(a, 1 - c)]
                _remote(landed, landed, d2d_s.at[3 * a + k], d2d_r.at[3 * a + k], sibling).wait_recv()
        for cp in sends + passed:
            cp.wait_send()

    out_shape = [jax.ShapeDtypeStruct((NCHIP,) + s.shape, s.dtype) for s in list(shards) + [conv]]
    return _comm_call("gather_weights", body, list(shards) + [conv], out_shape, [3 * n + 3, 3 * n + 3, 3 * n, 3 * n])


def _exchange_halves(gs):
    n = len(gs)

    def body(*refs):
        ins, outs = refs[:n], refs[n:2 * n]
        send_s, recv_s = refs[2 * n:]
        x, y, c, _ = _place()
        copies = []
        for a in range(n):
            hr = ins[a].shape[1] // 2
            theirs = ins[a].at[:, pl.ds(pl.multiple_of((1 - c) * hr, 8), hr)]
            copies.append(_remote(theirs, outs[a], send_s.at[a], recv_s.at[a], (x, y, 1 - c)))
        for cp in copies:
            cp.start()
        for cp in copies:
            cp.wait()

    out_shape = [jax.ShapeDtypeStruct((NCHIP, g.shape[1] // 2, g.shape[2]), F32) for g in gs]
    return _comm_call("exchange_halves", body, gs, out_shape, [n, n])


def _scatter_partials(ps):
    n = len(ps)

    def body(*refs):
        ins, outs = refs[:n], refs[n:2 * n]
        send_s, recv_s = refs[2 * n:]
        x, y, c, others = _place()
        copies = []
        for a in range(n):
            for k, (ox, oy) in enumerate(others):
                copies.append(_remote(ins[a].at[2 * ox + oy], outs[a].at[k], send_s.at[3 * a + k],
                                      recv_s.at[3 * a + k], (ox, oy, c)))
        for cp in copies:
            cp.start()
        for cp in copies:
            cp.wait()

    out_shape = [jax.ShapeDtypeStruct((3,) + p.shape[1:], p.dtype) for p in ps]
    return _comm_call("scatter_partials", body, ps, out_shape, [3 * n, 3 * n])


def _join_halves(rs):
    n = len(rs)

    def body(*refs):
        outs = refs[n:2 * n]
        send_s, recv_s = refs[2 * n:]
        x, y, c, _ = _place()
        sibling = (x, y, 1 - c)
        copies = [_remote(outs[a].at[c], outs[a].at[c], send_s.at[a], recv_s.at[a], sibling) for a in range(n)]
        for cp in copies:
            cp.start()
        for a in range(n):
            landed = outs[a].at[1 - c]
            _remote(landed, landed, send_s.at[a], recv_s.at[a], sibling).wait_recv()
        for cp in copies:
            cp.wait_send()

    out_shape = [jax.ShapeDtypeStruct(r.shape, r.dtype) for r in rs]
    return _comm_call("join_halves", body, rs, out_shape, [n, n], aliases={a: a for a in range(n)})


NDEV = 8


def _allreduce_small(buf):
    r = buf.shape[0]

    def body(in_ref, out_ref, gath, send_s, recv_s):
        x, y, c, _ = _place()
        me = 4 * x + 2 * y + c
        copies = []
        for rel in range(1, NDEV):
            px = 1 - x if rel & 4 else x
            py = 1 - y if rel & 2 else y
            pc = 1 - c if rel & 1 else c
            copies.append((_remote(in_ref, gath.at[me], send_s.at[rel - 1], recv_s.at[rel - 1], (px, py, pc)),
                           4 * px + 2 * py + pc))
        for cp, _ in copies:
            cp.start()
        gath[me] = in_ref[...]
        for rel, (cp, peer) in enumerate(copies):
            landed = gath.at[peer]
            _remote(landed, landed, send_s.at[rel], recv_s.at[rel], (x, y, c)).wait_recv()
        for cp, _ in copies:
            cp.wait_send()
        total = gath[0]
        for d in range(1, NDEV):
            total = total + gath[d]
        out_ref[...] = total

    vm = pl.BlockSpec(memory_space=pltpu.VMEM)
    return pl.pallas_call(
        body, name="allreduce_small", in_specs=[vm], out_specs=vm, out_shape=jax.ShapeDtypeStruct((r, LANE), F32),
        scratch_shapes=[pltpu.VMEM((NDEV, r, LANE), F32), pltpu.SemaphoreType.DMA((NDEV - 1,)),
                        pltpu.SemaphoreType.DMA((NDEV - 1,))],
        compiler_params=pltpu.CompilerParams(has_side_effects=True, vmem_limit_bytes=VMEM_LIMIT),
    )(buf)


MAX_ROW_TILE = 512
BF16_ROWS = 16


def _row_tile(rows):
    for t in range(min(rows, MAX_ROW_TILE) // BF16_ROWS * BF16_ROWS, 0, -BF16_ROWS):
        if rows % t == 0:
            return t
    raise ValueError(rows)


def _sum_halves(g, theirs, c_arr):
    nch, rows, cols = g.shape
    hr = rows // 2
    tr = _row_tile(hr)

    def body(c_ref, g_ref, t_ref, o_ref, ob_ref):
        s = g_ref[...] + t_ref[...]
        o_ref[...] = s
        ob_ref[...] = s.astype(BF16)

    blk = pl.BlockSpec((None, tr, cols), lambda j, i, c_ref: (j, i, 0))
    return pl.pallas_call(
        body, name="sum_halves",
        grid_spec=pltpu.PrefetchScalarGridSpec(
            num_scalar_prefetch=1, grid=(nch, hr // tr),
            in_specs=[pl.BlockSpec((None, None, tr, cols), lambda j, i, c_ref: (j, c_ref[0], i, 0)), blk],
            out_specs=[blk, blk]),
        out_shape=[jax.ShapeDtypeStruct((nch, hr, cols), F32), jax.ShapeDtypeStruct((nch, hr, cols), BF16)],
        compiler_params=_cparams(("parallel", "parallel")),
    )(c_arr, g.reshape(nch, 2, hr, cols), theirs)


def _sum_chips(p, q, place):
    _, rows, cols = p.shape
    tr = _row_tile(rows)

    def body(place_ref, p_ref, q0, q1, q2, o_ref):
        o_ref[...] = ((p_ref[...] + q0[...].astype(F32)) + q1[...].astype(F32)) + q2[...].astype(F32)

    qs = lambda k: pl.BlockSpec((None, tr, cols), lambda i, place_ref: (k, i, 0))
    return pl.pallas_call(
        body, name="sum_chips",
        grid_spec=pltpu.PrefetchScalarGridSpec(
            num_scalar_prefetch=1, grid=(rows // tr,),
            in_specs=[pl.BlockSpec((None, tr, cols), lambda i, place_ref: (place_ref[0], i, 0)), qs(0), qs(1), qs(2)],
            out_specs=pl.BlockSpec((None, tr, cols), lambda i, place_ref: (place_ref[1], i, 0))),
        out_shape=jax.ShapeDtypeStruct((2, rows, cols), F32),
        compiler_params=_cparams(("parallel",)),
    )(place, p, q, q, q)


def _adamw(w, g, m, v):
    rows, cols = w.shape
    tr = _row_tile(rows)

    def body(w_ref, g_ref, m_ref, v_ref, d_ref, nm_ref, nv_ref):
        gv = g_ref[...]
        nm = ADAM_B1 * m_ref[...] + (1.0 - ADAM_B1) * gv
        nv = ADAM_B2 * v_ref[...] + (1.0 - ADAM_B2) * jnp.square(gv)
        m_hat = nm / (1.0 - ADAM_B1 ** ADAM_STEP)
        v_hat = nv / (1.0 - ADAM_B2 ** ADAM_STEP)
        d_ref[...] = -ADAM_LR * (m_hat / (jnp.sqrt(v_hat) + ADAM_EPS) + ADAM_WD * w_ref[...])
        nm_ref[...] = nm
        nv_ref[...] = nv

    blk = pl.BlockSpec((tr, cols), lambda i: (i, 0))
    return pl.pallas_call(
        body, name="adamw", grid=(rows // tr,), in_specs=[blk] * 4, out_specs=[blk] * 3,
        out_shape=[jax.ShapeDtypeStruct((rows, cols), F32)] * 3, compiler_params=_cparams(("parallel",)),
    )(w, g, m, v)


BIG = ("w_in", "w_out", "w_gate", "w_up", "w_down")
SMALL = ("norm_mix", "a_log", "dt_bias", "o_norm_g", "ln_v_g", "ln_v_b", "w_s", "b_s", "norm_ffn", "norm_final")
ORDER = ("norm_mix", "w_in", "conv_w", "a_log", "dt_bias", "o_norm_g", "ln_v_g", "ln_v_b", "w_s", "b_s", "w_out",
         "norm_ffn", "w_gate", "w_up", "w_down", "norm_final")


def _pack(arrs):
    flat = jnp.concatenate([a.reshape(-1) for a in arrs])
    rows = -(-flat.shape[0] // (BF16_ROWS * LANE)) * BF16_ROWS
    return jnp.pad(flat, (0, rows * LANE - flat.shape[0])).reshape(rows, LANE)


def _unpack(buf, like):
    flat = buf.reshape(-1)
    out, off = [], 0
    for a in like:
        out.append(flat[off:off + a.size].reshape(a.shape))
        off += a.size
    return out


def kernel(x, norm_mix, w_in, conv_w, a_log, dt_bias, o_norm_g, ln_v_g, ln_v_b, w_s, b_s, w_out, norm_ffn, w_gate, w_up, w_down, norm_final, loss_target, m_norm_mix, m_w_in, m_conv_w, m_a_log, m_dt_bias, m_o_norm_g, m_ln_v_g, m_ln_v_b, m_w_s, m_b_s, m_w_out, m_norm_ffn, m_w_gate, m_w_up, m_w_down, m_norm_final, v_norm_mix, v_w_in, v_conv_w, v_a_log, v_dt_bias, v_o_norm_g, v_ln_v_g, v_ln_v_b, v_w_s, v_b_s, v_w_out, v_norm_ffn, v_w_gate, v_w_up, v_w_down, v_norm_final):
    w = dict(norm_mix=norm_mix, w_in=w_in, conv_w=conv_w, a_log=a_log, dt_bias=dt_bias, o_norm_g=o_norm_g,
             ln_v_g=ln_v_g, ln_v_b=ln_v_b, w_s=w_s, b_s=b_s, w_out=w_out, norm_ffn=norm_ffn, w_gate=w_gate, w_up=w_up,
             w_down=w_down, norm_final=norm_final)
    m = dict(norm_mix=m_norm_mix, w_in=m_w_in, conv_w=m_conv_w, a_log=m_a_log, dt_bias=m_dt_bias, o_norm_g=m_o_norm_g,
             ln_v_g=m_ln_v_g, ln_v_b=m_ln_v_b, w_s=m_w_s, b_s=m_b_s, w_out=m_w_out, norm_ffn=m_norm_ffn,
             w_gate=m_w_gate, w_up=m_w_up, w_down=m_w_down, norm_final=m_norm_final)
    v = dict(norm_mix=v_norm_mix, w_in=v_w_in, conv_w=v_conv_w, a_log=v_a_log, dt_bias=v_dt_bias, o_norm_g=v_o_norm_g,
             ln_v_g=v_ln_v_g, ln_v_b=v_ln_v_b, w_s=v_w_s, b_s=v_b_s, w_out=v_w_out, norm_ffn=v_norm_ffn,
             w_gate=v_w_gate, w_up=v_w_up, w_down=v_w_down, norm_final=v_norm_final)
    chip = 2 * lax.axis_index("x") + lax.axis_index("y")
    place = jnp.stack([chip, lax.axis_index("c")]).astype(jnp.int32)
    c_arr = place[1:]

    own = [w[n].astype(BF16) for n in BIG] + [conv_w]
    gathered = _gather_weights(own[:-1], conv_w)
    big = {n: lax.dynamic_update_index_in_dim(g, o, chip, 0) for n, g, o in zip(BIG + ("conv_w",), gathered, own)}
    layers = [_layer_params(l, big, w) for l in range(DEPTH)]

    loss_lanes, dx, grads, d_norm_final = _local_step(x[0], loss_target[0], layers, norm_final[None])
    loss = lax.psum(loss_lanes[0, 0], ("x", "y", "c"))
    gl = [_reference_layout(g) for g in grads]

    mine = [gl[l][n] for n in BIG for l in range(DEPTH)]
    theirs = _exchange_halves(mine)
    sums = [_sum_halves(g, t, c_arr) for g, t in zip(mine, theirs)]
    arrived = _scatter_partials([s16 for _, s16 in sums])
    joined = _join_halves([_sum_chips(s32, q, place) for (s32, _), q in zip(sums, arrived)])
    g_out = {n: jnp.stack(joined[DEPTH * a:DEPTH * (a + 1)]).reshape(w[n].shape) for a, n in enumerate(BIG)}

    small_g = [jnp.stack([gl[l][n] for l in range(DEPTH)]) for n in SMALL[:-1]] + [d_norm_final[0]]
    conv_g = jnp.stack([gl[l]["conv_w"] for l in range(DEPTH)])
    total = _allreduce_small(_pack(small_g + [conv_g]))
    *small_r, conv_r = _unpack(total, small_g + [conv_g])
    g_out.update(zip(SMALL, small_r))
    g_out["conv_w"] = lax.dynamic_slice_in_dim(conv_r, chip * conv_w.shape[2], conv_w.shape[2], axis=2)

    delta, new_m, new_v = {}, {}, {}
    for n in BIG:
        two_d = lambda a: a.reshape(-1, a.shape[-1])
        d, nm, nv = _adamw(two_d(w[n]), two_d(g_out[n]), two_d(m[n]), two_d(v[n]))
        delta[n], new_m[n], new_v[n] = (a.reshape(w[n].shape) for a in (d, nm, nv))
    rest = SMALL + ("conv_w",)
    like = [w[n] for n in rest]
    d, nm, nv = _adamw(*[_pack([src[n] for n in rest]) for src in (w, g_out, m, v)])
    for dst, buf in ((delta, d), (new_m, nm), (new_v, nv)):
        dst.update(zip(rest, _unpack(buf, like)))

    return (loss, dx[None], *[g_out[n] for n in ORDER], *[delta[n] for n in ORDER], *[new_m[n] for n in ORDER],
            *[new_v[n] for n in ORDER])
```

```python
import functools

import jax
import jax.numpy as jnp
from jax import lax
from jax.experimental import pallas as pl
from jax.experimental.pallas import tpu as pltpu

F32 = jnp.float32
BF16 = jnp.bfloat16
MESH = pl.DeviceIdType.MESH
ANY = pl.BlockSpec(memory_space=pl.ANY)
HIGHEST = lax.Precision.HIGHEST

T = 2048
D = 1024
DEPTH = 2
NCHIP = 4
HEADS = 4
HD = 128
HW = HEADS * HD
CH = 64
GCH = 128
IN_DIM = 3080
NP = 3200
BA_OFF = 3072
FF_SH = 704
EPS = 1e-6
LANE = 128
VMEM_LIMIT = 56 * 1024 * 1024

ADAM_LR = 0.001
ADAM_B1 = 0.9
ADAM_B2 = 0.999
ADAM_EPS = 1e-08
ADAM_WD = 0.01
ADAM_STEP = 10


def _cparams(sem=None):
    return pltpu.CompilerParams(dimension_semantics=sem, vmem_limit_bytes=VMEM_LIMIT)


_DIMS = {"nn": (((1,), (0,)), ((), ())), "nt": (((1,), (1,)), ((), ())), "tn": (((0,), (0,)), ((), ()))}


def _mm(name, mode, a, bs, *, tm, tn, tk, out_dtypes=(F32,), reduce_g=False, resid=None, extras=(), epilogue=None,
        b_g=None, b_spec=None, g_n=None, n_n=None):
    nb = len(bs)
    ga = a.shape[0]
    gbs = [1 if b_spec is not None else (b.shape[0] if b_g is None else g_n) for b in bs]
    g_n = max([ga] + gbs)
    if mode == "tn":
        k_n, m_n = a.shape[1:]
    else:
        m_n, k_n = a.shape[1:]
    if n_n is None:
        n_n = bs[0].shape[1] if mode == "nt" else bs[0].shape[2]
    assert m_n % tm == 0 and n_n % tn == 0 and k_n % tk == 0, (name, m_n, n_n, k_n)
    mi, nj, kk = m_n // tm, n_n // tn, k_n // tk
    if reduce_g:
        grid = (mi, nj, g_n, kk)
        ids = lambda i, j, g, k: (g, i, j, k)
        n_red = g_n * kk
        red_idx = lambda: pl.program_id(2) * kk + pl.program_id(3)
        sem = ("parallel", "parallel", "arbitrary", "arbitrary")
    else:
        grid = (g_n, mi, nj, kk)
        ids = lambda g, i, j, k: (g, i, j, k)
        n_red = kk
        red_idx = lambda: pl.program_id(3)
        sem = ("parallel", "parallel", "parallel", "arbitrary")

    def pick(gsz, g):
        return g if gsz > 1 else 0

    def a_map(*p):
        g, i, j, k = ids(*p)
        return (pick(ga, g), k, i) if mode == "tn" else (pick(ga, g), i, k)

    def b_map(gsz):
        def f(*p):
            g, i, j, k = ids(*p)
            if b_spec is not None:
                return b_spec[1](g, i, j, k)
            lead = pick(gsz, g) if b_g is None else b_g(g)
            return (lead, j, k) if mode == "nt" else (lead, k, j)
        return f

    def o_map(gsz):
        def f(*p):
            g, i, j, k = ids(*p)
            return (0 if reduce_g else pick(gsz, g), i, j)
        return f

    a_spec = pl.BlockSpec((None, tk, tm) if mode == "tn" else (None, tm, tk), a_map)
    b_block = b_spec[0] if b_spec is not None else ((None, tn, tk) if mode == "nt" else (None, tk, tn))
    b_specs = [pl.BlockSpec(b_block, b_map(gs)) for gs in gbs]
    x_specs = [pl.BlockSpec((None, tm, tn), o_map(e.shape[0])) for e in extras]
    r_specs = [pl.BlockSpec((None, tm, tn), o_map(resid.shape[0]))] if resid is not None else []
    g_out = 1 if reduce_g else g_n
    out_shape = [jax.ShapeDtypeStruct((g_out, m_n, n_n), dt) for dt in out_dtypes]
    out_specs = [pl.BlockSpec((None, tm, tn), o_map(g_out)) for _ in out_dtypes]
    nx, nr, no = len(extras), len(r_specs), len(out_dtypes)
    dims = _DIMS[mode]

    def body(*refs):
        a_ref = refs[0]
        b_refs = refs[1:1 + nb]
        x_refs = refs[1 + nb:1 + nb + nx]
        r_refs = refs[1 + nb + nx:1 + nb + nx + nr]
        o_refs = refs[1 + nb + nx + nr:1 + nb + nx + nr + no]
        acc_refs = refs[1 + nb + nx + nr + no:]
        r = red_idx()
        av = a_ref[...]
        for b_ref, acc in zip(b_refs, acc_refs):
            p = lax.dot_general(av, b_ref[...], dims, preferred_element_type=F32)

            @pl.when(r == 0)
            def _():
                acc[...] = p

            @pl.when(r > 0)
            def _():
                acc[...] += p

        @pl.when(r == n_red - 1)
        def _():
            accs = [acc[...] for acc in acc_refs]
            if r_refs:
                accs[0] = accs[0] + r_refs[0][...]
            outs = epilogue(accs, [x[...] for x in x_refs]) if epilogue is not None else accs
            for o_ref, o in zip(o_refs, outs):
                o_ref[...] = o.astype(o_ref.dtype)

    return pl.pallas_call(
        body, name=name, grid=grid,
        in_specs=[a_spec] + b_specs + x_specs + r_specs,
        out_specs=out_specs, out_shape=out_shape,
        scratch_shapes=[pltpu.VMEM((tm, tn), F32) for _ in range(nb)],
        compiler_params=_cparams(sem),
    )(a, *bs, *extras, *([resid] if resid is not None else []))


def _sigmoid(x):
    return 1.0 / (1.0 + jnp.exp(-x))


def _silu(x):
    return x * _sigmoid(x)


def _gelu(x):
    return 0.5 * x * (1.0 + jnp.tanh(0.7978845608028654 * (x + 0.044715 * (x * x * x))))


def _rms_fn(h, gain):
    return h * lax.rsqrt(jnp.mean(h * h, axis=-1, keepdims=True) + EPS) * gain


def _shift_impl(x, s):
    n = x.shape[0]
    rolled = pltpu.roll(x, s % n, 0)
    row = lax.broadcasted_iota(jnp.int32, x.shape, 0)
    return jnp.where((row >= s) & (row < n + s), rolled, 0.0)


@functools.partial(jax.custom_vjp, nondiff_argnums=(1,))
def _shift(x, s):
    return _shift_impl(x, s)


def _shift_fwd(x, s):
    return _shift_impl(x, s), None


def _shift_bwd(s, _, g):
    return (_shift_impl(g, -s),)


_shift.defvjp(_shift_fwd, _shift_bwd)


def _prep_fn(x, w, qk_scale, is_v):
    y = x * w[3:4, :]
    for i in range(3):
        y = y + _shift(x, 3 - i) * w[i:i + 1, :]
    y = _silu(y)
    nrm = lax.rsqrt(jnp.sum(y * y, axis=-1, keepdims=True) + EPS) * qk_scale
    return y * jnp.where(is_v, 1.0, nrm)


def _softplus(x):
    return jnp.maximum(x, 0.0) + jnp.log(1.0 + jnp.exp(-jnp.abs(x)))


def _gates_fn(ba, a_log, dt_bias):
    lane = lax.broadcasted_iota(jnp.int32, ba.shape, 1)
    beta = _sigmoid(ba)
    g = -jnp.exp(a_log) * _softplus(ba + dt_bias)
    return jnp.where(lane < HEADS, beta, g)


def _dot16(a, b, dims=_DIMS["nn"]):
    return lax.dot_general(a.astype(BF16), b.astype(BF16), dims, preferred_element_type=F32)


def _dot32(a, b):
    return jnp.dot(a, b, preferred_element_type=F32, precision=HIGHEST)


def _dot3(a, b, dims=_DIMS["nn"]):
    return lax.dot_general(a, b, dims, preferred_element_type=F32, precision=lax.Precision.HIGH)


def _tri_inverses(mats):
    row = lax.broadcasted_iota(jnp.int32, (CH, CH), 0)
    col = lax.broadcasted_iota(jnp.int32, (CH, CH), 1)
    eye = (row == col).astype(F32)
    ts = [eye - a for a in mats]
    ps = list(mats)
    for _ in range(5):
        ps = [_dot3(p, p) for p in ps]
        ts = [t + _dot3(t, p) for t, p in zip(ts, ps)]
    return ts


@jax.custom_vjp
def _tri_solves(mats, rhs):
    return [_dot3(t, b) for t, b in zip(_tri_inverses(mats), rhs)]


def _tri_solves_fwd(mats, rhs):
    ts = _tri_inverses(mats)
    xs = [_dot3(t, b) for t, b in zip(ts, rhs)]
    return xs, (ts, xs)


def _tri_solves_bwd(res, dxs):
    ts, xs = res
    dbs = [_dot3(t, dx, _DIMS["tn"]) for t, dx in zip(ts, dxs)]
    return [-_dot3(db, x, _DIMS["nt"]) for db, x in zip(dbs, xs)], dbs


_tri_solves.defvjp(_tri_solves_fwd, _tri_solves_bwd)


def _chunk_prep_fn(xs, bgs):
    row = lax.broadcasted_iota(jnp.int32, (CH, CH), 0)
    col = lax.broadcasted_iota(jnp.int32, (CH, CH), 1)
    incl = row >= col
    strict = row > col
    lmat = incl.astype(F32)
    n = len(xs)
    items = [(i, h) for i in range(n) for h in range(HEADS)]
    part = lambda i, h, c: xs[i][:, c * HW + h * HD:c * HW + (h + 1) * HD]
    q = [part(i, h, 0) for i, h in items]
    k = [part(i, h, 1) for i, h in items]
    v = [part(i, h, 2) for i, h in items]
    beta = [bgs[i][:, h:h + 1] for i, h in items]
    gc_all = [_dot32(lmat, bg) for bg in bgs]
    gc = [gc_all[i][:, HEADS + h:HEADS + h + 1] for i, h in items]
    gmat = [jnp.where(strict, jnp.broadcast_to(bgs[i][:, HEADS + h:HEADS + h + 1], (CH, CH)), 0.0) for i, h in items]
    diff = [_dot3(lmat, m) for m in gmat]
    decay = [jnp.where(incl, jnp.exp(jnp.where(incl, d, 0.0)), 0.0) for d in diff]
    k_beta = [kk * b for kk, b in zip(k, beta)]
    kk_t = [_dot16(kb, kk, _DIMS["nt"]) for kb, kk in zip(k_beta, k)]
    qk_t = [_dot16(qq, kk, _DIMS["nt"]) for qq, kk in zip(q, k)]
    a = [jnp.where(strict, m * d, 0.0) for m, d in zip(kk_t, decay)]
    eg = [jnp.exp(g) for g in gc]
    rhs = [jnp.concatenate([vv * b, kb * e], axis=-1) for vv, b, kb, e in zip(v, beta, k_beta, eg)]
    uw = _tri_solves(a, rhs)
    qk = [m * d for m, d in zip(qk_t, decay)]
    g_last = [g[CH - 1:CH, :] for g in gc]
    qe = [qq * e for qq, e in zip(q, eg)]
    kd = [kk * jnp.exp(gl - g) for kk, gl, g in zip(k, g_last, gc)]
    egl = [jnp.broadcast_to(jnp.exp(gl), (1, HD)) for gl in g_last]
    out = []
    for i in range(n):
        mine = slice(i * HEADS, (i + 1) * HEADS)
        cat = lambda vals: jnp.concatenate(vals[mine], axis=-1)
        out.append((cat([x[:, :HD] for x in uw]), cat([x[:, HD:] for x in uw]), cat(qe), cat(kd),
                    jnp.concatenate([m[None] for m in qk[mine]], axis=0), cat(egl)))
    return out


def _chunk_state_fn(u, w, qe, kd, qk, egl, s):
    ws = [_dot16(a, b) for a, b in zip(w, s)]
    qs = [_dot16(a, b) for a, b in zip(qe, s)]
    v_new = [a - b for a, b in zip(u, ws)]
    o = [a + _dot16(b, c) for a, b, c in zip(qs, qk, v_new)]
    s_new = [a * e + _dot16(b, c, _DIMS["tn"]) for a, e, b, c in zip(s, egl, kd, v_new)]
    return o, s_new


def _mix_fn(o, z, ur, vr, ong, lng, lnb, ws, bst):
    row = lax.broadcasted_iota(jnp.int32, (GCH, GCH), 0)
    col = lax.broadcasted_iota(jnp.int32, (GCH, GCH), 1)
    causal = row >= col
    ug = _gelu(ur)
    vg = _gelu(vr)
    outs_dn, outs_gm = [], []
    for h in range(HEADS):
        sl = slice(h * HD, (h + 1) * HD)
        oh = o[:, sl]
        oh = oh * lax.rsqrt(jnp.mean(oh * oh, axis=-1, keepdims=True) + EPS)
        outs_dn.append(oh * ong * _silu(z[:, sl]))
        vh = vg[:, sl]
        mu = jnp.mean(vh, axis=-1, keepdims=True)
        var = jnp.mean(jnp.square(vh - mu), axis=-1, keepdims=True)
        vn = (vh - mu) * lax.rsqrt(var + EPS) * lng[:, sl] + lnb[:, sl]
        sp = _dot16(jnp.where(causal, ws[h], 0.0), vn) + bst[:, h:h + 1]
        outs_gm.append(ug[:, sl] * sp)
    return jnp.concatenate(outs_dn + outs_gm, axis=-1)


def _loss_fn(h, gain, tgt):
    y = _rms_fn(h, gain)
    return 0.5 * jnp.sum(jnp.mean(jnp.square(y - tgt), axis=-1))


RT = 256


def _rows(n=D):
    return pl.BlockSpec((RT, n), lambda i: (i, 0))


def _whole(shape):
    nd = len(shape)
    return pl.BlockSpec(shape, lambda i: (0,) * nd)


def _rmsnorm(name, h, gain):
    def body(h_ref, g_ref, o_ref):
        o_ref[...] = _rms_fn(h_ref[...], g_ref[...]).astype(BF16)

    return pl.pallas_call(
        body, name=name, grid=(T // RT,), in_specs=[_rows(), _whole((1, D))], out_specs=_rows(),
        out_shape=jax.ShapeDtypeStruct((T, D), BF16), compiler_params=_cparams(("parallel",)),
    )(h, gain)


def _rmsnorm_bwd(name, dhn, h, gain, resid):
    def body(dhn_ref, h_ref, g_ref, r_ref, dh_ref, dh16_ref, dg_ref):
        _, vjp = jax.vjp(_rms_fn, h_ref[...], g_ref[...])
        dh, dg = vjp(dhn_ref[...])
        dh = r_ref[...] + dh
        dh_ref[...] = dh
        dh16_ref[...] = dh.astype(BF16)

        @pl.when(pl.program_id(0) == 0)
        def _():
            dg_ref[...] = dg

        @pl.when(pl.program_id(0) > 0)
        def _():
            dg_ref[...] += dg

    return pl.pallas_call(
        body, name=name, grid=(T // RT,), in_specs=[_rows(), _rows(), _whole((1, D)), _rows()],
        out_specs=[_rows(), _rows(), _whole((1, D))],
        out_shape=[jax.ShapeDtypeStruct((T, D), F32), jax.ShapeDtypeStruct((T, D), BF16),
                   jax.ShapeDtypeStruct((1, D), F32)],
        compiler_params=_cparams(("arbitrary",)),
    )(dhn, h, gain, resid)


def _loss_head(h, gain, tgt):
    def body(h_ref, g_ref, t_ref, l_ref, dh_ref, dh16_ref, dg_ref):
        loss, vjp = jax.vjp(lambda hh, gg: _loss_fn(hh, gg, t_ref[...]), h_ref[...], g_ref[...])
        dh, dg = vjp(jnp.ones((), F32))
        dh_ref[...] = dh
        dh16_ref[...] = dh.astype(BF16)
        lv = jnp.full((1, LANE), loss, F32)

        @pl.when(pl.program_id(0) == 0)
        def _():
            dg_ref[...] = dg
            l_ref[...] = lv

        @pl.when(pl.program_id(0) > 0)
        def _():
            dg_ref[...] += dg
            l_ref[...] += lv

    return pl.pallas_call(
        body, name="loss_head", grid=(T // RT,), in_specs=[_rows(), _whole((1, D)), _rows()],
        out_specs=[_whole((1, LANE)), _rows(), _rows(), _whole((1, D))],
        out_shape=[jax.ShapeDtypeStruct((1, LANE), F32), jax.ShapeDtypeStruct((T, D), F32),
                   jax.ShapeDtypeStruct((T, D), BF16), jax.ShapeDtypeStruct((1, D), F32)],
        compiler_params=_cparams(("arbitrary",)),
    )(h, gain, tgt)


def _prep_flags():
    j = pl.program_id(0)
    qk_scale = jnp.where(j < HEADS, HD ** -0.5, 1.0).astype(F32)
    return qk_scale, j >= 2 * HEADS


def _prep(proj, conv_w):
    def body(x_ref, w_ref, o_ref):
        qk_scale, is_v = _prep_flags()
        o_ref[...] = _prep_fn(x_ref[...], w_ref[...], qk_scale, is_v)

    col = lambda j: (0, j)
    return pl.pallas_call(
        body, name="gdn_prep", grid=(3 * HEADS,),
        in_specs=[pl.BlockSpec((T, HD), col), pl.BlockSpec((4, HD), col)], out_specs=pl.BlockSpec((T, HD), col),
        out_shape=jax.ShapeDtypeStruct((T, 3 * HW), F32), compiler_params=_cparams(("parallel",)),
    )(proj, conv_w)


def _prep_bwd(proj, conv_w, dqkv, dproj):
    def body(x_ref, w_ref, d_ref, _, dx_ref, dw_ref):
        qk_scale, is_v = _prep_flags()
        _, vjp = jax.vjp(lambda x, w: _prep_fn(x, w, qk_scale, is_v), x_ref[...], w_ref[...])
        dx, dw = vjp(d_ref[...])
        dx_ref[...] = dx.astype(BF16)
        dw_ref[...] = dw

    col = lambda j: (0, j)
    return pl.pallas_call(
        body, name="gdn_prep_bwd", grid=(3 * HEADS,),
        in_specs=[pl.BlockSpec((T, HD), col), pl.BlockSpec((4, HD), col), pl.BlockSpec((T, HD), col), ANY],
        out_specs=[pl.BlockSpec((T, HD), col), pl.BlockSpec((4, HD), col)],
        out_shape=[jax.ShapeDtypeStruct((T, NP), BF16), jax.ShapeDtypeStruct((4, 3 * HW), F32)],
        input_output_aliases={3: 0}, compiler_params=_cparams(("parallel",)),
    )(proj, conv_w, dqkv, dproj)


BA_BLK = BA_OFF // LANE


def _gates(proj, a_log, dt_bias):
    def body(x_ref, a_ref, d_ref, o_ref):
        o_ref[...] = _gates_fn(x_ref[...], a_ref[...], d_ref[...])

    return pl.pallas_call(
        body, name="gdn_gates", grid=(1,),
        in_specs=[pl.BlockSpec((T, LANE), lambda i: (0, BA_BLK)), _whole((1, LANE)), _whole((1, LANE))],
        out_specs=_whole((T, LANE)),
        out_shape=jax.ShapeDtypeStruct((T, LANE), F32), compiler_params=_cparams(("arbitrary",)),
    )(proj, a_log, dt_bias)


def _gates_bwd(proj, a_log, dt_bias, dbg, dproj):
    def body(x_ref, a_ref, d_ref, dbg_ref, _, dx_ref, da_ref, dd_ref):
        _, vjp = jax.vjp(_gates_fn, x_ref[...], a_ref[...], d_ref[...])
        dx, da_ref[...], dd_ref[...] = vjp(dbg_ref[...])
        dx_ref[...] = dx.astype(BF16)

    ba = pl.BlockSpec((T, LANE), lambda i: (0, BA_BLK))
    return pl.pallas_call(
        body, name="gdn_gates_bwd", grid=(1,),
        in_specs=[ba, _whole((1, LANE)), _whole((1, LANE)), _whole((T, LANE)), ANY],
        out_specs=[ba, _whole((1, LANE)), _whole((1, LANE))],
        out_shape=[jax.ShapeDtypeStruct((T, NP), BF16), jax.ShapeDtypeStruct((1, LANE), F32),
                   jax.ShapeDtypeStruct((1, LANE), F32)],
        input_output_aliases={4: 0}, compiler_params=_cparams(("arbitrary",)),
    )(proj, a_log, dt_bias, dbg, dproj)


NCK = T // CH
CPS = 2


def _chunk_prep_specs(rev=False):
    at = (lambda n: NCK - 1 - n) if rev else (lambda n: n)
    wide = pl.BlockSpec((CH, HW), lambda n: (at(n), 0))
    return [wide, wide, wide, wide, pl.BlockSpec((HEADS, CH, CH), lambda n: (0, at(n), 0)),
            pl.BlockSpec((None, 1, HW), lambda n: (at(n), 0, 0))]


def _chunk_prep_shapes(dtypes):
    shp = [(T, HW), (T, HW), (T, HW), (T, HW), (HEADS, T, CH), (NCK, 1, HW)]
    return [jax.ShapeDtypeStruct(s, dt) for s, dt in zip(shp, dtypes)]


def _chunk_prep(qkv, bg):
    def body(x_ref, bg_ref, *o_refs):
        rows = [slice(ci * CH, (ci + 1) * CH) for ci in range(CPS)]
        res = _chunk_prep_fn([x_ref[r, :] for r in rows], [bg_ref[r, :] for r in rows])
        for ci, (u, w, qe, kd, qk, egl) in enumerate(res):
            for o_ref, val in zip(o_refs[:4], (u, w, qe, kd)):
                o_ref[rows[ci], :] = val.astype(o_ref.dtype)
            o_refs[4][:, rows[ci], :] = qk.astype(BF16)
            o_refs[5][ci] = egl

    wide = pl.BlockSpec((CPS * CH, HW), lambda n: (n, 0))
    return pl.pallas_call(
        body, name="gdn_chunk_prep", grid=(NCK // CPS,),
        in_specs=[pl.BlockSpec((CPS * CH, 3 * HW), lambda n: (n, 0)), pl.BlockSpec((CPS * CH, LANE), lambda n: (n, 0))],
        out_specs=[wide, wide, wide, wide, pl.BlockSpec((HEADS, CPS * CH, CH), lambda n: (0, n, 0)),
                   pl.BlockSpec((CPS, 1, HW), lambda n: (n, 0, 0))],
        out_shape=_chunk_prep_shapes((F32, BF16, BF16, BF16, BF16, F32)),
        compiler_params=_cparams(("parallel",)),
    )(qkv, bg)


def _chunk_prep_bwd(qkv, bg, cots):
    def body(x_ref, bg_ref, du, dw, dqe, dkd, dqk, degl, dx_ref, dbg_ref):
        rows = [slice(ci * CH, (ci + 1) * CH) for ci in range(CPS)]
        _, vjp = jax.vjp(_chunk_prep_fn, [x_ref[r, :] for r in rows], [bg_ref[r, :] for r in rows])
        dxs, dbgs = vjp([(du[r, :], dw[r, :], dqe[r, :], dkd[r, :], dqk[:, r, :], degl[ci])
                         for ci, r in enumerate(rows)])
        for r, dx, dbg in zip(rows, dxs, dbgs):
            dx_ref[r, :] = dx
            dbg_ref[r, :] = dbg

    wide = pl.BlockSpec((CPS * CH, HW), lambda n: (n, 0))
    return pl.pallas_call(
        body, name="gdn_chunk_prep_bwd", grid=(NCK // CPS,),
        in_specs=[pl.BlockSpec((CPS * CH, 3 * HW), lambda n: (n, 0)), pl.BlockSpec((CPS * CH, LANE), lambda n: (n, 0)),
                  wide, wide, wide, wide, pl.BlockSpec((HEADS, CPS * CH, CH), lambda n: (0, n, 0)),
                  pl.BlockSpec((CPS, 1, HW), lambda n: (n, 0, 0))],
        out_specs=[pl.BlockSpec((CPS * CH, 3 * HW), lambda n: (n, 0)), pl.BlockSpec((CPS * CH, LANE), lambda n: (n, 0))],
        out_shape=[jax.ShapeDtypeStruct((T, 3 * HW), F32), jax.ShapeDtypeStruct((T, LANE), F32)],
        compiler_params=_cparams(("parallel",)),
    )(qkv, bg, *cots)


def _head_args(refs):
    u, w, qe, kd, qk, egl = refs
    sls = [slice(h * HD, (h + 1) * HD) for h in range(HEADS)]
    return ([u[:, sl] for sl in sls], [w[:, sl].astype(F32) for sl in sls], [qe[:, sl].astype(F32) for sl in sls],
            [kd[:, sl].astype(F32) for sl in sls], [qk[h].astype(F32) for h in range(HEADS)],
            [egl[:, sl] for sl in sls])


def _chunk_scan(prep):
    def body(*refs):
        o_ref, sh_ref, s_ref = refs[6:]

        @pl.when(pl.program_id(0) == 0)
        def _():
            s_ref[...] = jnp.zeros_like(s_ref)

        s = [s_ref[h] for h in range(HEADS)]
        for h in range(HEADS):
            sh_ref[h, 0] = s[h]
        o, s_new = _chunk_state_fn(*_head_args(refs[:6]), s)
        for h in range(HEADS):
            o_ref[:, h * HD:(h + 1) * HD] = o[h]
            s_ref[h] = s_new[h]

    return pl.pallas_call(
        body, name="gdn_scan", grid=(NCK,), in_specs=_chunk_prep_specs(),
        out_specs=[pl.BlockSpec((CH, HW), lambda n: (n, 0)), pl.BlockSpec((HEADS, 1, HD, HD), lambda n: (0, n, 0, 0))],
        out_shape=[jax.ShapeDtypeStruct((T, HW), F32), jax.ShapeDtypeStruct((HEADS, NCK, HD, HD), F32)],
        scratch_shapes=[pltpu.VMEM((HEADS, HD, HD), F32)], compiler_params=_cparams(("arbitrary",)),
    )(*prep)


def _chunk_scan_bwd(prep, s_hist, do):
    def body(*refs):
        sh_ref, do_ref = refs[6:8]
        d_refs = refs[8:14]
        ds_ref = refs[14]

        @pl.when(pl.program_id(0) == 0)
        def _():
            ds_ref[...] = jnp.zeros_like(ds_ref)

        sls = [slice(h * HD, (h + 1) * HD) for h in range(HEADS)]
        _, vjp = jax.vjp(_chunk_state_fn, *_head_args(refs[:6]), [sh_ref[h, 0] for h in range(HEADS)])
        du, dw, dqe, dkd, dqk, degl, ds = vjp(([do_ref[:, sl] for sl in sls], [ds_ref[h] for h in range(HEADS)]))
        for h, sl in enumerate(sls):
            for d_ref, val in zip(d_refs[:4], (du, dw, dqe, dkd)):
                d_ref[:, sl] = val[h]
            d_refs[4][h] = dqk[h]
            d_refs[5][:, sl] = degl[h]
            ds_ref[h] = ds[h]

    rev = lambda n: NCK - 1 - n
    return pl.pallas_call(
        body, name="gdn_scan_bwd", grid=(NCK,),
        in_specs=_chunk_prep_specs(rev=True) + [pl.BlockSpec((HEADS, 1, HD, HD), lambda n: (0, rev(n), 0, 0)),
                                                pl.BlockSpec((CH, HW), lambda n: (rev(n), 0))],
        out_specs=_chunk_prep_specs(rev=True), out_shape=_chunk_prep_shapes((F32,) * 6),
        scratch_shapes=[pltpu.VMEM((HEADS, HD, HD), F32)], compiler_params=_cparams(("arbitrary",)),
    )(*prep, s_hist, do)


def _mix_specs():
    pc = lambda c: pl.BlockSpec((GCH, HW), lambda i: (i, c))
    return [pl.BlockSpec((GCH, HW), lambda i: (i, 0)), pc(3), pc(4), pc(5), _whole((1, HD)), _whole((1, HW)),
            _whole((1, HW)), _whole((HEADS, GCH, GCH)), _whole((GCH, LANE))]


def _mix(o, proj, ong, lng, lnb, ws, bst):
    def body(o_ref, z_ref, u_ref, v_ref, ong_ref, lng_ref, lnb_ref, ws_ref, bs_ref, m_ref):
        m_ref[...] = _mix_fn(o_ref[...], z_ref[...], u_ref[...], v_ref[...], ong_ref[...], lng_ref[...],
                             lnb_ref[...], ws_ref[...], bs_ref[...]).astype(BF16)

    return pl.pallas_call(
        body, name="mix", grid=(T // GCH,), in_specs=_mix_specs(),
        out_specs=pl.BlockSpec((GCH, D), lambda i: (i, 0)), out_shape=jax.ShapeDtypeStruct((T, D), BF16),
        compiler_params=_cparams(("parallel",)),
    )(o, proj, proj, proj, ong, lng, lnb, ws, bst)


def _mix_bwd(o, proj, ong, lng, lnb, ws, bst, dmix):
    def body(o_ref, z_ref, u_ref, v_ref, ong_ref, lng_ref, lnb_ref, ws_ref, bs_ref, dm_ref,
             do_ref, dzuv_ref, dong_ref, dlng_ref, dlnb_ref, dws_ref, dbs_ref):
        _, vjp = jax.vjp(_mix_fn, o_ref[...], z_ref[...], u_ref[...], v_ref[...], ong_ref[...], lng_ref[...],
                         lnb_ref[...], ws_ref[...], bs_ref[...])
        do, dz, du, dv, dong, dlng, dlnb, dws, dbs = vjp(dm_ref[...])
        do_ref[...] = do
        dzuv_ref[:, 0:HW] = dz.astype(BF16)
        dzuv_ref[:, HW:2 * HW] = du.astype(BF16)
        dzuv_ref[:, 2 * HW:3 * HW] = dv.astype(BF16)
        acc = [(dong_ref, dong), (dlng_ref, dlng), (dlnb_ref, dlnb), (dws_ref, dws), (dbs_ref, dbs)]

        @pl.when(pl.program_id(0) == 0)
        def _():
            for r, val in acc:
                r[...] = val

        @pl.when(pl.program_id(0) > 0)
        def _():
            for r, val in acc:
                r[...] += val

    shp = lambda *s: jax.ShapeDtypeStruct(s, F32)
    return pl.pallas_call(
        body, name="mix_bwd", grid=(T // GCH,),
        in_specs=_mix_specs() + [pl.BlockSpec((GCH, D), lambda i: (i, 0))],
        out_specs=[pl.BlockSpec((GCH, HW), lambda i: (i, 0)), pl.BlockSpec((GCH, 3 * HW), lambda i: (i, 1)),
                   _whole((1, HD)), _whole((1, HW)), _whole((1, HW)), _whole((HEADS, GCH, GCH)), _whole((GCH, LANE))],
        out_shape=[shp(T, HW), jax.ShapeDtypeStruct((T, NP), BF16), shp(1, HD), shp(1, HW), shp(1, HW),
                   shp(HEADS, GCH, GCH), shp(GCH, LANE)],
        compiler_params=_cparams(("arbitrary",)),
    )(o, proj, proj, proj, ong, lng, lnb, ws, bst, dmix)


def _swiglu_epilogue(accs, _):
    gate, up = accs
    return [gate, up, _silu(gate) * up]


def _swiglu_bwd_epilogue(accs, extras):
    dact = accs[0]
    gate, up = (e.astype(F32) for e in extras)
    sg = _sigmoid(gate)
    return [dact * up * (sg * (1.0 + gate * (1.0 - sg))), dact * (gate * sg)]


def _layer_fwd(h, p):
    hn = _rmsnorm("rms_mix", h, p["norm_mix"])
    proj = _mm("in_proj", "nn", hn[None], [p["w_in"][None]], tm=1024, tn=640, tk=D)[0][0]
    qkv = _prep(proj, p["conv_w"])
    bg = _gates(proj, p["a_log"], p["dt_bias"])
    prep = _chunk_prep(qkv, bg)
    o, s_hist = _chunk_scan(prep)
    mix = _mix(o, proj, p["o_norm_g"], p["ln_v_g"], p["ln_v_b"], p["w_s"], p["bst"])
    l = p["layer"]
    of_layer = dict(b_g=lambda g: DEPTH * g + l, g_n=NCHIP)
    h1 = _mm("out_proj", "nn", mix[None], [p["w_out"]], tm=1024, tn=512, tk=D // NCHIP, resid=h[None], n_n=D,
             b_spec=((None, D // NCHIP, 512), lambda g, i, j, k: (DEPTH * k + l, 0, j)))[0][0]
    h2n = _rmsnorm("rms_ffn", h1, p["norm_ffn"])
    gate, up, act = _mm("ffn_in", "nn", h2n[None], [p["w_gate"], p["w_up"]], tm=1024, tn=FF_SH, tk=D,
                        out_dtypes=(BF16, BF16, BF16), epilogue=_swiglu_epilogue, **of_layer)
    h2 = _mm("ffn_out", "nn", act, [p["w_down"]], tm=1024, tn=512, tk=FF_SH, reduce_g=True, resid=h1[None],
             **of_layer)[0][0]
    saved = dict(h=h, hn=hn, proj=proj, qkv=qkv, bg=bg, prep=prep, o=o, s_hist=s_hist, mix=mix, h1=h1, h2n=h2n,
                 gate=gate, up=up, act=act)
    return h2, saved


def _layer_bwd(dh2, dh2b, p, s):
    l = p["layer"]
    of_layer = dict(b_g=lambda g: DEPTH * g + l, g_n=NCHIP)
    dh2b = dh2b[None]
    dgate, dup = _mm("ffn_out_bwd", "nt", dh2b, [p["w_down"]], tm=1024, tn=FF_SH, tk=D, out_dtypes=(BF16, BF16),
                     extras=(s["gate"], s["up"]), epilogue=_swiglu_bwd_epilogue, **of_layer)
    dh2n = _mm("ffn_gate_bwd", "nt", dgate, [p["w_gate"]], tm=1024, tn=512, tk=FF_SH, reduce_g=True, **of_layer)[0]
    dh2n = _mm("ffn_up_bwd", "nt", dup, [p["w_up"]], tm=1024, tn=512, tk=FF_SH, reduce_g=True, resid=dh2n,
               **of_layer)[0][0]
    dh1, dh1b, d_norm_ffn = _rmsnorm_bwd("rms_ffn_bwd", dh2n, s["h1"], p["norm_ffn"], dh2)
    d_w_down = _mm("ffn_wdown_grad", "tn", s["act"], [dh2b], tm=FF_SH, tn=512, tk=1024)[0]
    d_w_gate = _mm("ffn_wgate_grad", "tn", s["h2n"][None], [dgate], tm=512, tn=FF_SH, tk=1024)[0]
    d_w_up = _mm("ffn_wup_grad", "tn", s["h2n"][None], [dup], tm=512, tn=FF_SH, tk=1024)[0]
    dh1b = dh1b[None]
    dmix = _mm("out_proj_bwd", "nt", dh1b, [p["w_out"]], tm=1024, tn=D // NCHIP, tk=D, n_n=D,
               b_spec=((None, D // NCHIP, D), lambda g, i, j, k: (DEPTH * j + l, 0, k)))[0][0]
    d_w_out = _mm("out_proj_wgrad", "tn", s["mix"][None], [dh1b], tm=512, tn=512, tk=1024)[0][0]
    do, dproj, d_ong, d_lng, d_lnb, d_ws, d_bst = _mix_bwd(
        s["o"], s["proj"], p["o_norm_g"], p["ln_v_g"], p["ln_v_b"], p["w_s"], p["bst"], dmix)
    dqkv, dbg = _chunk_prep_bwd(s["qkv"], s["bg"], _chunk_scan_bwd(s["prep"], s["s_hist"], do))
    dproj, d_conv = _prep_bwd(s["proj"], p["conv_w"], dqkv, dproj)
    dproj, d_a_log, d_dt_bias = _gates_bwd(s["proj"], p["a_log"], p["dt_bias"], dbg, dproj)
    dproj = dproj[None]
    dhn = _mm("in_proj_bwd", "nt", dproj, [p["w_in"][None]], tm=1024, tn=512, tk=640)[0][0]
    dh, dhb, d_norm_mix = _rmsnorm_bwd("rms_mix_bwd", dhn, s["h"], p["norm_mix"], dh1)
    d_w_in = _mm("in_proj_wgrad", "tn", s["hn"][None], [dproj], tm=512, tn=640, tk=1024)[0]
    grads = dict(norm_mix=d_norm_mix, w_in=d_w_in, conv_w=d_conv, a_log=d_a_log, dt_bias=d_dt_bias, o_norm_g=d_ong,
                 ln_v_g=d_lng, ln_v_b=d_lnb, w_s=d_ws, bst=d_bst, w_out=d_w_out, norm_ffn=d_norm_ffn,
                 w_gate=d_w_gate, w_up=d_w_up, w_down=d_w_down)
    return dh, dhb, grads


def _lanes(v, off=0):
    return jnp.zeros((1, LANE), F32).at[0, off:off + v.shape[0]].set(v)


def _w_in_pieces():
    regions = [(0, 2048, 0), (2048, 2056, BA_OFF), (2056, IN_DIM, 2048)]
    sh = IN_DIM // NCHIP
    out = []
    for j in range(NCHIP):
        for lo, hi, at in regions:
            a, b = max(lo, j * sh), min(hi, (j + 1) * sh)
            if a < b:
                out.append((j, a - j * sh, at + a - lo, b - a))
    return out


W_IN_PIECES = _w_in_pieces()
WT = 256


def _assemble_w_in(gathered, own, l, place):
    def body(place_ref, g_ref, own_ref, o_ref):
        o_ref[:, IN_DIM:] = jnp.zeros((WT, NP - IN_DIM), BF16)
        mine = own_ref[...]
        for j, src, dst, width in W_IN_PIECES:
            val = jnp.where(place_ref[0] == j, mine[:, src:src + width], g_ref[j, :, src:src + width])
            o_ref[:, dst:dst + width] = val

    sh = IN_DIM // NCHIP
    return pl.pallas_call(
        body, name="assemble_w_in",
        grid_spec=pltpu.PrefetchScalarGridSpec(
            num_scalar_prefetch=1, grid=(D // WT,),
            in_specs=[pl.BlockSpec((NCHIP, None, WT, sh), lambda i, place_ref: (0, l, i, 0)),
                      pl.BlockSpec((None, WT, sh), lambda i, place_ref: (l, i, 0))],
            out_specs=pl.BlockSpec((WT, NP), lambda i, place_ref: (i, 0))),
        out_shape=jax.ShapeDtypeStruct((D, NP), BF16), compiler_params=_cparams(("parallel",)),
    )(place, gathered, own)


def _layer_params(l, big, small):
    return dict(
        layer=l, w_in=big["w_in"][l], w_out=big["w_out"], w_gate=big["w_gate"], w_up=big["w_up"], w_down=big["w_down"],
        conv_w=jnp.concatenate([big["conv_w"][j, l] for j in range(NCHIP)], axis=1),
        norm_mix=small["norm_mix"][l][None], norm_ffn=small["norm_ffn"][l][None],
        a_log=_lanes(small["a_log"][l], HEADS), dt_bias=_lanes(small["dt_bias"][l], HEADS),
        o_norm_g=small["o_norm_g"][l][None], ln_v_g=small["ln_v_g"][l][None], ln_v_b=small["ln_v_b"][l][None],
        w_s=small["w_s"][l],
        bst=jnp.pad(small["b_s"][l].T, ((0, 0), (0, LANE - HEADS))),
    )


def _reference_layout(g):
    return dict(
        w_in=g["w_in"],
        w_out=g["w_out"].reshape(NCHIP, D // NCHIP, D),
        w_gate=g["w_gate"], w_up=g["w_up"], w_down=g["w_down"],
        conv_w=g["conv_w"], norm_mix=g["norm_mix"][0], norm_ffn=g["norm_ffn"][0],
        a_log=g["a_log"][0, HEADS:2 * HEADS], dt_bias=g["dt_bias"][0, HEADS:2 * HEADS],
        o_norm_g=g["o_norm_g"][0], ln_v_g=g["ln_v_g"][0], ln_v_b=g["ln_v_b"][0], w_s=g["w_s"],
        b_s=g["bst"][:, :HEADS].T,
    )


def _local_step(x, tgt, layers, norm_final):
    h = x
    saved = []
    for p in layers:
        h, s = _layer_fwd(h, p)
        saved.append(s)
    loss, dh, dhb, d_norm_final = _loss_head(h, norm_final, tgt)
    grads = [None] * DEPTH
    for l in reversed(range(DEPTH)):
        dh, dhb, grads[l] = _layer_bwd(dh, dhb, layers[l], saved[l])
    return loss, dh, grads, d_norm_final


def _place():
    x, y, c = lax.axis_index("x"), lax.axis_index("y"), lax.axis_index("c")
    return x, y, c, [(1 - x, y), (x, 1 - y), (1 - x, 1 - y)]


def _remote(src, dst, send_sem, recv_sem, to):
    return pltpu.make_async_remote_copy(src_ref=src, dst_ref=dst, send_sem=send_sem, recv_sem=recv_sem,
                                        device_id=to, device_id_type=MESH)


def _comm_call(name, body, ins, out_shape, n_sems, aliases=None):
    return pl.pallas_call(
        body, name=name, in_specs=[ANY] * len(ins), out_specs=[ANY] * len(out_shape), out_shape=out_shape,
        scratch_shapes=[pltpu.SemaphoreType.DMA((n,)) for n in n_sems], input_output_aliases=aliases or {},
        compiler_params=pltpu.CompilerParams(has_side_effects=True),
    )(*ins)


def _gather_weights(shards, conv):
    n = len(shards)

    def body(*refs):
        ins, conv_in = refs[:n], refs[n]
        outs, conv_out = refs[n + 1:2 * n + 1], refs[2 * n + 1]
        ici_s, ici_r, d2d_s, d2d_r = refs[2 * n + 2:]
        x, y, c, others = _place()
        chip = 2 * x + y
        sibling = (x, y, 1 - c)

        def half(a, of_c):
            hr = ins[a].shape[1] // 2
            return pl.ds(pl.multiple_of(of_c * hr, 16), hr)

        sends = []
        for a in range(n):
            for k, (ox, oy) in enumerate(others):
                sends.append(_remote(ins[a].at[:, half(a, c)], outs[a].at[chip, :, half(a, c)],
                                     ici_s.at[3 * a + k], ici_r.at[3 * a + k], (ox, oy, c)))
        for k, (ox, oy) in enumerate(others):
            sends.append(_remote(conv_in, conv_out.at[chip], ici_s.at[3 * n + k], ici_r.at[3 * n + k], (ox, oy, c)))
        for cp in sends:
            cp.start()
        passed = []
        for a in range(n):
            for k, (ox, oy) in enumerate(others):
                landed = outs[a].at[2 * ox + oy, :, half(a, c)]
                _remote(landed, landed, ici_s.at[3 * a + k], ici_r.at[3 * a + k], (ox, oy, c)).wait_recv()
                cp = _remote(landed, landed, d2d_s.at[3 * a + k], d2d_r.at[3 * a + k], sibling)
                cp.start()
                passed.append(cp)
        for k, (ox, oy) in enumerate(others):
            landed = conv_out.at[2 * ox + oy]
            _remote(landed, landed, ici_s.at[3 * n + k], ici_r.at[3 * n + k], (ox, oy, c)).wait_recv()
        for a in range(n):
            for k, (ox, oy) in enumerate(others):
                landed = outs[a].at[2 * ox + oy, :, half(a, 1 - c)]
                _remote(landed, landed, d2d_s.at[3 * a + k], d2d_r.at[3 * a + k], sibling).wait_recv()
        for cp in sends + passed:
            cp.wait_send()

    out_shape = [jax.ShapeDtypeStruct((NCHIP,) + s.shape, s.dtype) for s in list(shards) + [conv]]
    return _comm_call("gather_weights", body, list(shards) + [conv], out_shape, [3 * n + 3, 3 * n + 3, 3 * n, 3 * n])


def _exchange_halves(gs):
    n = len(gs)

    def body(*refs):
        ins, outs = refs[:n], refs[n:2 * n]
        send_s, recv_s = refs[2 * n:]
        x, y, c, _ = _place()
        copies = []
        for a in range(n):
            hr = ins[a].shape[1] // 2
            theirs = ins[a].at[:, pl.ds(pl.multiple_of((1 - c) * hr, 8), hr)]
            copies.append(_remote(theirs, outs[a], send_s.at[a], recv_s.at[a], (x, y, 1 - c)))
        for cp in copies:
            cp.start()
        for cp in copies:
            cp.wait()

    out_shape = [jax.ShapeDtypeStruct((g.shape[0], g.shape[1] // 2, g.shape[2]), F32) for g in gs]
    return _comm_call("exchange_halves", body, gs, out_shape, [n, n])


def _scatter_partials(ps):
    n = len(ps)

    def body(*refs):
        ins, outs = refs[:n], refs[n:2 * n]
        send_s, recv_s = refs[2 * n:]
        x, y, c, others = _place()
        copies = []
        for a in range(n):
            for k, (ox, oy) in enumerate(others):
                copies.append(_remote(ins[a].at[2 * ox + oy], outs[a].at[k], send_s.at[3 * a + k],
                                      recv_s.at[3 * a + k], (ox, oy, c)))
        for cp in copies:
            cp.start()
        for cp in copies:
            cp.wait()

    out_shape = [jax.ShapeDtypeStruct((3,) + p.shape[1:], p.dtype) for p in ps]
    return _comm_call("scatter_partials", body, ps, out_shape, [3 * n, 3 * n])


def _join_halves(rs):
    n = len(rs)

    def body(*refs):
        outs = refs[n:2 * n]
        send_s, recv_s = refs[2 * n:]
        x, y, c, _ = _place()
        sibling = (x, y, 1 - c)
        copies = [_remote(outs[a].at[c], outs[a].at[c], send_s.at[a], recv_s.at[a], sibling) for a in range(n)]
        for cp in copies:
            cp.start()
        for a in range(n):
            landed = outs[a].at[1 - c]
            _remote(landed, landed, send_s.at[a], recv_s.at[a], sibling).wait_recv()
        for cp in copies:
            cp.wait_send()

    out_shape = [jax.ShapeDtypeStruct(r.shape, r.dtype) for r in rs]
    return _comm_call("join_halves", body, rs, out_shape, [n, n], aliases={a: a for a in range(n)})


NDEV = 8


def _allreduce_small(buf):
    r = buf.shape[0]

    def body(in_ref, out_ref, gath, send_s, recv_s):
        x, y, c, _ = _place()
        me = 4 * x + 2 * y + c
        copies = []
        for rel in range(1, NDEV):
            px = 1 - x if rel & 4 else x
            py = 1 - y if rel & 2 else y
            pc = 1 - c if rel & 1 else c
            copies.append((_remote(in_ref, gath.at[me], send_s.at[rel - 1], recv_s.at[rel - 1], (px, py, pc)),
                           4 * px + 2 * py + pc))
        for cp, _ in copies:
            cp.start()
        gath[me] = in_ref[...]
        for rel, (cp, peer) in enumerate(copies):
            landed = gath.at[peer]
            _remote(landed, landed, send_s.at[rel], recv_s.at[rel], (x, y, c)).wait_recv()
        for cp, _ in copies:
            cp.wait_send()
        total = gath[0]
        for d in range(1, NDEV):
            total = total + gath[d]
        out_ref[...] = total

    vm = pl.BlockSpec(memory_space=pltpu.VMEM)
    return pl.pallas_call(
        body, name="allreduce_small", in_specs=[vm], out_specs=vm, out_shape=jax.ShapeDtypeStruct((r, LANE), F32),
        scratch_shapes=[pltpu.VMEM((NDEV, r, LANE), F32), pltpu.SemaphoreType.DMA((NDEV - 1,)),
                        pltpu.SemaphoreType.DMA((NDEV - 1,))],
        compiler_params=pltpu.CompilerParams(has_side_effects=True, vmem_limit_bytes=VMEM_LIMIT),
    )(buf)


MAX_ROW_TILE = 512
BF16_ROWS = 16


def _row_tile(rows):
    for t in range(min(rows, MAX_ROW_TILE) // BF16_ROWS * BF16_ROWS, 0, -BF16_ROWS):
        if rows % t == 0:
            return t
    raise ValueError(rows)


def _sum_halves(g, theirs, c_arr):
    nch, rows, cols = g.shape
    hr = rows // 2
    tr = _row_tile(hr)

    def body(c_ref, g_ref, t_ref, o_ref, ob_ref):
        s = g_ref[...] + t_ref[...]
        o_ref[...] = s
        ob_ref[...] = s.astype(BF16)

    blk = pl.BlockSpec((None, tr, cols), lambda j, i, c_ref: (j, i, 0))
    return pl.pallas_call(
        body, name="sum_halves",
        grid_spec=pltpu.PrefetchScalarGridSpec(
            num_scalar_prefetch=1, grid=(nch, hr // tr),
            in_specs=[pl.BlockSpec((None, None, tr, cols), lambda j, i, c_ref: (j, c_ref[0], i, 0)), blk],
            out_specs=[blk, blk]),
        out_shape=[jax.ShapeDtypeStruct((nch, hr, cols), F32), jax.ShapeDtypeStruct((nch, hr, cols), BF16)],
        compiler_params=_cparams(("parallel", "parallel")),
    )(c_arr, g.reshape(nch, 2, hr, cols), theirs)


def _sum_halves_w_in(g, theirs, c_arr):
    hr = D // 2
    sh = IN_DIM // NCHIP

    def body(c_ref, g_ref, t_ref, o_ref, ob_ref):
        s = g_ref[...] + t_ref[...]
        for j, dst, src, width in W_IN_PIECES:
            o_ref[j, :, dst:dst + width] = s[:, src:src + width]
            ob_ref[j, :, dst:dst + width] = s[:, src:src + width].astype(BF16)

    out = pl.BlockSpec((NCHIP, WT, sh), lambda i, c_ref: (0, i, 0))
    return pl.pallas_call(
        body, name="sum_halves_w_in",
        grid_spec=pltpu.PrefetchScalarGridSpec(
            num_scalar_prefetch=1, grid=(hr // WT,),
            in_specs=[pl.BlockSpec((None, WT, NP), lambda i, c_ref: (c_ref[0], i, 0)),
                      pl.BlockSpec((None, WT, NP), lambda i, c_ref: (0, i, 0))],
            out_specs=[out, out]),
        out_shape=[jax.ShapeDtypeStruct((NCHIP, hr, sh), F32), jax.ShapeDtypeStruct((NCHIP, hr, sh), BF16)],
        compiler_params=_cparams(("parallel",)),
    )(c_arr, g.reshape(2, hr, NP), theirs)


def _sum_chips(p, q, place):
    _, rows, cols = p.shape
    tr = _row_tile(rows)

    def body(place_ref, p_ref, q0, q1, q2, o_ref):
        o_ref[...] = ((p_ref[...] + q0[...].astype(F32)) + q1[...].astype(F32)) + q2[...].astype(F32)

    qs = lambda k: pl.BlockSpec((None, tr, cols), lambda i, place_ref: (k, i, 0))
    return pl.pallas_call(
        body, name="sum_chips",
        grid_spec=pltpu.PrefetchScalarGridSpec(
            num_scalar_prefetch=1, grid=(rows // tr,),
            in_specs=[pl.BlockSpec((None, tr, cols), lambda i, place_ref: (place_ref[0], i, 0)), qs(0), qs(1), qs(2)],
            out_specs=pl.BlockSpec((None, tr, cols), lambda i, place_ref: (place_ref[1], i, 0))),
        out_shape=jax.ShapeDtypeStruct((2, rows, cols), F32),
        compiler_params=_cparams(("parallel",)),
    )(place, p, q, q, q)


def _adamw(w, g, m, v):
    rows, cols = w.shape
    tr = _row_tile(rows)

    def body(w_ref, g_ref, m_ref, v_ref, d_ref, nm_ref, nv_ref):
        gv = g_ref[...]
        nm = ADAM_B1 * m_ref[...] + (1.0 - ADAM_B1) * gv
        nv = ADAM_B2 * v_ref[...] + (1.0 - ADAM_B2) * jnp.square(gv)
        m_hat = nm / (1.0 - ADAM_B1 ** ADAM_STEP)
        v_hat = nv / (1.0 - ADAM_B2 ** ADAM_STEP)
        d_ref[...] = -ADAM_LR * (m_hat / (jnp.sqrt(v_hat) + ADAM_EPS) + ADAM_WD * w_ref[...])
        nm_ref[...] = nm
        nv_ref[...] = nv

    blk = pl.BlockSpec((tr, cols), lambda i: (i, 0))
    return pl.pallas_call(
        body, name="adamw", grid=(rows // tr,), in_specs=[blk] * 4, out_specs=[blk] * 3,
        out_shape=[jax.ShapeDtypeStruct((rows, cols), F32)] * 3, compiler_params=_cparams(("parallel",)),
    )(w, g, m, v)


BIG = ("w_in", "w_out", "w_gate", "w_up", "w_down")
SMALL = ("norm_mix", "a_log", "dt_bias", "o_norm_g", "ln_v_g", "ln_v_b", "w_s", "b_s", "norm_ffn", "norm_final")
ORDER = ("norm_mix", "w_in", "conv_w", "a_log", "dt_bias", "o_norm_g", "ln_v_g", "ln_v_b", "w_s", "b_s", "w_out",
         "norm_ffn", "w_gate", "w_up", "w_down", "norm_final")


def _pack(arrs):
    flat = jnp.concatenate([a.reshape(-1) for a in arrs])
    rows = -(-flat.shape[0] // (BF16_ROWS * LANE)) * BF16_ROWS
    return jnp.pad(flat, (0, rows * LANE - flat.shape[0])).reshape(rows, LANE)


def _unpack(buf, like):
    flat = buf.reshape(-1)
    out, off = [], 0
    for a in like:
        out.append(flat[off:off + a.size].reshape(a.shape))
        off += a.size
    return out


def kernel(x, norm_mix, w_in, conv_w, a_log, dt_bias, o_norm_g, ln_v_g, ln_v_b, w_s, b_s, w_out, norm_ffn, w_gate, w_up, w_down, norm_final, loss_target, m_norm_mix, m_w_in, m_conv_w, m_a_log, m_dt_bias, m_o_norm_g, m_ln_v_g, m_ln_v_b, m_w_s, m_b_s, m_w_out, m_norm_ffn, m_w_gate, m_w_up, m_w_down, m_norm_final, v_norm_mix, v_w_in, v_conv_w, v_a_log, v_dt_bias, v_o_norm_g, v_ln_v_g, v_ln_v_b, v_w_s, v_b_s, v_w_out, v_norm_ffn, v_w_gate, v_w_up, v_w_down, v_norm_final):
    w = dict(norm_mix=norm_mix, w_in=w_in, conv_w=conv_w, a_log=a_log, dt_bias=dt_bias, o_norm_g=o_norm_g,
             ln_v_g=ln_v_g, ln_v_b=ln_v_b, w_s=w_s, b_s=b_s, w_out=w_out, norm_ffn=norm_ffn, w_gate=w_gate, w_up=w_up,
             w_down=w_down, norm_final=norm_final)
    m = dict(norm_mix=m_norm_mix, w_in=m_w_in, conv_w=m_conv_w, a_log=m_a_log, dt_bias=m_dt_bias, o_norm_g=m_o_norm_g,
             ln_v_g=m_ln_v_g, ln_v_b=m_ln_v_b, w_s=m_w_s, b_s=m_b_s, w_out=m_w_out, norm_ffn=m_norm_ffn,
             w_gate=m_w_gate, w_up=m_w_up, w_down=m_w_down, norm_final=m_norm_final)
    v = dict(norm_mix=v_norm_mix, w_in=v_w_in, conv_w=v_conv_w, a_log=v_a_log, dt_bias=v_dt_bias, o_norm_g=v_o_norm_g,
             ln_v_g=v_ln_v_g, ln_v_b=v_ln_v_b, w_s=v_w_s, b_s=v_b_s, w_out=v_w_out, norm_ffn=v_norm_ffn,
             w_gate=v_w_gate, w_up=v_w_up, w_down=v_w_down, norm_final=v_norm_final)
    chip = 2 * lax.axis_index("x") + lax.axis_index("y")
    place = jnp.stack([chip, lax.axis_index("c")]).astype(jnp.int32)
    c_arr = place[1:]

    own = [w[n].astype(BF16) for n in BIG] + [conv_w]
    gathered = _gather_weights(own[:-1], conv_w)
    big = {n: lax.dynamic_update_index_in_dim(g, o, chip, 0) for n, g, o in zip(BIG + ("conv_w",), gathered, own)}
    for n in BIG[1:]:
        big[n] = big[n].reshape((NCHIP * DEPTH,) + big[n].shape[2:])
    big["w_in"] = [_assemble_w_in(gathered[0], own[0], l, place) for l in range(DEPTH)]
    layers = [_layer_params(l, big, w) for l in range(DEPTH)]

    loss_lanes, dx, grads, d_norm_final = _local_step(x[0], loss_target[0], layers, norm_final[None])
    loss = lax.psum(loss_lanes[0, 0], ("x", "y", "c"))
    gl = [_reference_layout(g) for g in grads]

    mine = [gl[l][n] for n in BIG for l in range(DEPTH)]
    theirs = _exchange_halves(mine)
    sums = [(_sum_halves_w_in if i < DEPTH else _sum_halves)(g, t, c_arr) for i, (g, t) in enumerate(zip(mine, theirs))]
    arrived = _scatter_partials([s16 for _, s16 in sums])
    joined = _join_halves([_sum_chips(s32, q, place) for (s32, _), q in zip(sums, arrived)])
    g_out = {n: jnp.stack(joined[DEPTH * a:DEPTH * (a + 1)]).reshape(w[n].shape) for a, n in enumerate(BIG)}

    small_g = [jnp.stack([gl[l][n] for l in range(DEPTH)]) for n in SMALL[:-1]] + [d_norm_final[0]]
    conv_g = jnp.stack([gl[l]["conv_w"] for l in range(DEPTH)])
    total = _allreduce_small(_pack(small_g + [conv_g]))
    *small_r, conv_r = _unpack(total, small_g + [conv_g])
    g_out.update(zip(SMALL, small_r))
    g_out["conv_w"] = lax.dynamic_slice_in_dim(conv_r, chip * conv_w.shape[2], conv_w.shape[2], axis=2)

    delta, new_m, new_v = {}, {}, {}
    for n in BIG:
        two_d = lambda a: a.reshape(-1, a.shape[-1])
        d, nm, nv = _adamw(two_d(w[n]), two_d(g_out[n]), two_d(m[n]), two_d(v[n]))
        delta[n], new_m[n], new_v[n] = (a.reshape(w[n].shape) for a in (d, nm, nv))
    rest = SMALL + ("conv_w",)
    like = [w[n] for n in rest]
    d, nm, nv = _adamw(*[_pack([src[n] for n in rest]) for src in (w, g_out, m, v)])
    for dst, buf in ((delta, d), (new_m, nm), (new_v, nv)):
        dst.update(zip(rest, _unpack(buf, like)))

    return (loss, dx[None], *[g_out[n] for n in ORDER], *[delta[n] for n in ORDER], *[new_m[n] for n in ORDER],
            *[new_v[n] for n in ORDER])
```

```python
import functools

import jax
import jax.numpy as jnp
from jax import lax
from jax.experimental import pallas as pl
from jax.experimental.pallas import tpu as pltpu

F32 = jnp.float32
BF16 = jnp.bfloat16
MESH = pl.DeviceIdType.MESH
ANY = pl.BlockSpec(memory_space=pl.ANY)
HIGHEST = lax.Precision.HIGHEST

T = 2048
D = 1024
DEPTH = 2
NCHIP = 4
HEADS = 4
HD = 128
HW = HEADS * HD
CH = 64
GCH = 128
IN_DIM = 3080
NP = 3200
BA_OFF = 3072
FF_SH = 704
EPS = 1e-6
LANE = 128
VMEM_LIMIT = 56 * 1024 * 1024

ADAM_LR = 0.001
ADAM_B1 = 0.9
ADAM_B2 = 0.999
ADAM_EPS = 1e-08
ADAM_WD = 0.01
ADAM_STEP = 10


def _cparams(sem=None):
    return pltpu.CompilerParams(dimension_semantics=sem, vmem_limit_bytes=VMEM_LIMIT)


_DIMS = {"nn": (((1,), (0,)), ((), ())), "nt": (((1,), (1,)), ((), ())), "tn": (((0,), (0,)), ((), ()))}


def _mm(name, mode, a, bs, *, tm, tn, tk, out_dtypes=(F32,), reduce_g=False, resid=None, extras=(), epilogue=None,
        b_g=None, b_spec=None, g_n=None, n_n=None):
    nb = len(bs)
    ga = a.shape[0]
    gbs = [1 if b_spec is not None else (b.shape[0] if b_g is None else g_n) for b in bs]
    g_n = max([ga] + gbs)
    if mode == "tn":
        k_n, m_n = a.shape[1:]
    else:
        m_n, k_n = a.shape[1:]
    if n_n is None:
        n_n = bs[0].shape[1] if mode == "nt" else bs[0].shape[2]
    assert m_n % tm == 0 and n_n % tn == 0 and k_n % tk == 0, (name, m_n, n_n, k_n)
    mi, nj, kk = m_n // tm, n_n // tn, k_n // tk
    if reduce_g:
        grid = (mi, nj, g_n, kk)
        ids = lambda i, j, g, k: (g, i, j, k)
        n_red = g_n * kk
        red_idx = lambda: pl.program_id(2) * kk + pl.program_id(3)
        sem = ("parallel", "parallel", "arbitrary", "arbitrary")
    else:
        grid = (g_n, mi, nj, kk)
        ids = lambda g, i, j, k: (g, i, j, k)
        n_red = kk
        red_idx = lambda: pl.program_id(3)
        sem = ("parallel", "parallel", "parallel", "arbitrary")

    def pick(gsz, g):
        return g if gsz > 1 else 0

    def a_map(*p):
        g, i, j, k = ids(*p)
        return (pick(ga, g), k, i) if mode == "tn" else (pick(ga, g), i, k)

    def b_map(gsz):
        def f(*p):
            g, i, j, k = ids(*p)
            if b_spec is not None:
                return b_spec[1](g, i, j, k)
            lead = pick(gsz, g) if b_g is None else b_g(g)
            return (lead, j, k) if mode == "nt" else (lead, k, j)
        return f

    def o_map(gsz):
        def f(*p):
            g, i, j, k = ids(*p)
            return (0 if reduce_g else pick(gsz, g), i, j)
        return f

    a_spec = pl.BlockSpec((None, tk, tm) if mode == "tn" else (None, tm, tk), a_map)
    b_block = b_spec[0] if b_spec is not None else ((None, tn, tk) if mode == "nt" else (None, tk, tn))
    b_specs = [pl.BlockSpec(b_block, b_map(gs)) for gs in gbs]
    x_specs = [pl.BlockSpec((None, tm, tn), o_map(e.shape[0])) for e in extras]
    r_specs = [pl.BlockSpec((None, tm, tn), o_map(resid.shape[0]))] if resid is not None else []
    g_out = 1 if reduce_g else g_n
    out_shape = [jax.ShapeDtypeStruct((g_out, m_n, n_n), dt) for dt in out_dtypes]
    out_specs = [pl.BlockSpec((None, tm, tn), o_map(g_out)) for _ in out_dtypes]
    nx, nr, no = len(extras), len(r_specs), len(out_dtypes)
    dims = _DIMS[mode]

    def body(*refs):
        a_ref = refs[0]
        b_refs = refs[1:1 + nb]
        x_refs = refs[1 + nb:1 + nb + nx]
        r_refs = refs[1 + nb + nx:1 + nb + nx + nr]
        o_refs = refs[1 + nb + nx + nr:1 + nb + nx + nr + no]
        acc_refs = refs[1 + nb + nx + nr + no:]
        r = red_idx()
        av = a_ref[...]
        for b_ref, acc in zip(b_refs, acc_refs):
            p = lax.dot_general(av, b_ref[...], dims, preferred_element_type=F32)

            @pl.when(r == 0)
            def _():
                acc[...] = p

            @pl.when(r > 0)
            def _():
                acc[...] += p

        @pl.when(r == n_red - 1)
        def _():
            accs = [acc[...] for acc in acc_refs]
            if r_refs:
                accs[0] = accs[0] + r_refs[0][...]
            outs = epilogue(accs, [x[...] for x in x_refs]) if epilogue is not None else accs
            for o_ref, o in zip(o_refs, outs):
                o_ref[...] = o.astype(o_ref.dtype)

    return pl.pallas_call(
        body, name=name, grid=grid,
        in_specs=[a_spec] + b_specs + x_specs + r_specs,
        out_specs=out_specs, out_shape=out_shape,
        scratch_shapes=[pltpu.VMEM((tm, tn), F32) for _ in range(nb)],
        compiler_params=_cparams(sem),
    )(a, *bs, *extras, *([resid] if resid is not None else []))


def _sigmoid(x):
    return 1.0 / (1.0 + jnp.exp(-x))


def _silu(x):
    return x * _sigmoid(x)


def _gelu(x):
    return 0.5 * x * (1.0 + jnp.tanh(0.7978845608028654 * (x + 0.044715 * (x * x * x))))


def _rms_fn(h, gain):
    return h * lax.rsqrt(jnp.mean(h * h, axis=-1, keepdims=True) + EPS) * gain


def _shift_impl(x, s):
    n = x.shape[0]
    rolled = pltpu.roll(x, s % n, 0)
    row = lax.broadcasted_iota(jnp.int32, x.shape, 0)
    return jnp.where((row >= s) & (row < n + s), rolled, 0.0)


@functools.partial(jax.custom_vjp, nondiff_argnums=(1,))
def _shift(x, s):
    return _shift_impl(x, s)


def _shift_fwd(x, s):
    return _shift_impl(x, s), None


def _shift_bwd(s, _, g):
    return (_shift_impl(g, -s),)


_shift.defvjp(_shift_fwd, _shift_bwd)


def _prep_fn(x, w, qk_scale, is_v):
    y = x * w[3:4, :]
    for i in range(3):
        y = y + _shift(x, 3 - i) * w[i:i + 1, :]
    y = _silu(y)
    nrm = lax.rsqrt(jnp.sum(y * y, axis=-1, keepdims=True) + EPS) * qk_scale
    return y * jnp.where(is_v, 1.0, nrm)


def _softplus(x):
    return jnp.maximum(x, 0.0) + jnp.log(1.0 + jnp.exp(-jnp.abs(x)))


def _gates_fn(ba, a_log, dt_bias):
    lane = lax.broadcasted_iota(jnp.int32, ba.shape, 1)
    beta = _sigmoid(ba)
    g = -jnp.exp(a_log) * _softplus(ba + dt_bias)
    return jnp.where(lane < HEADS, beta, g)


def _dot16(a, b, dims=_DIMS["nn"]):
    return lax.dot_general(a.astype(BF16), b.astype(BF16), dims, preferred_element_type=F32)


def _dot32(a, b):
    return jnp.dot(a, b, preferred_element_type=F32, precision=HIGHEST)


def _dot3(a, b, dims=_DIMS["nn"]):
    return lax.dot_general(a, b, dims, preferred_element_type=F32, precision=lax.Precision.HIGH)


def _tri_inverses(mats):
    row = lax.broadcasted_iota(jnp.int32, (CH, CH), 0)
    col = lax.broadcasted_iota(jnp.int32, (CH, CH), 1)
    eye = (row == col).astype(F32)
    ts = [eye - a for a in mats]
    ps = list(mats)
    for _ in range(5):
        ps = [_dot3(p, p) for p in ps]
        ts = [t + _dot3(t, p) for t, p in zip(ts, ps)]
    return ts


@jax.custom_vjp
def _tri_solves(mats, rhs):
    return [_dot3(t, b) for t, b in zip(_tri_inverses(mats), rhs)]


def _tri_solves_fwd(mats, rhs):
    ts = _tri_inverses(mats)
    xs = [_dot3(t, b) for t, b in zip(ts, rhs)]
    return xs, (ts, xs)


def _tri_solves_bwd(res, dxs):
    ts, xs = res
    dbs = [_dot3(t, dx, _DIMS["tn"]) for t, dx in zip(ts, dxs)]
    return [-_dot3(db, x, _DIMS["nt"]) for db, x in zip(dbs, xs)], dbs


_tri_solves.defvjp(_tri_solves_fwd, _tri_solves_bwd)


def _chunk_prep_fn(xs, bgs):
    row = lax.broadcasted_iota(jnp.int32, (CH, CH), 0)
    col = lax.broadcasted_iota(jnp.int32, (CH, CH), 1)
    incl = row >= col
    strict = row > col
    lmat = incl.astype(F32)
    n = len(xs)
    items = [(i, h) for i in range(n) for h in range(HEADS)]
    part = lambda i, h, c: xs[i][:, c * HW + h * HD:c * HW + (h + 1) * HD]
    q = [part(i, h, 0) for i, h in items]
    k = [part(i, h, 1) for i, h in items]
    v = [part(i, h, 2) for i, h in items]
    beta = [bgs[i][:, h:h + 1] for i, h in items]
    gc_all = [_dot32(lmat, bg) for bg in bgs]
    gc = [gc_all[i][:, HEADS + h:HEADS + h + 1] for i, h in items]
    gmat = [jnp.where(strict, jnp.broadcast_to(bgs[i][:, HEADS + h:HEADS + h + 1], (CH, CH)), 0.0) for i, h in items]
    diff = [_dot3(lmat, m) for m in gmat]
    decay = [jnp.where(incl, jnp.exp(jnp.where(incl, d, 0.0)), 0.0) for d in diff]
    k_beta = [kk * b for kk, b in zip(k, beta)]
    kk_t = [_dot16(kb, kk, _DIMS["nt"]) for kb, kk in zip(k_beta, k)]
    qk_t = [_dot16(qq, kk, _DIMS["nt"]) for qq, kk in zip(q, k)]
    a = [jnp.where(strict, m * d, 0.0) for m, d in zip(kk_t, decay)]
    eg = [jnp.exp(g) for g in gc]
    rhs = [jnp.concatenate([vv * b, kb * e], axis=-1) for vv, b, kb, e in zip(v, beta, k_beta, eg)]
    uw = _tri_solves(a, rhs)
    qk = [m * d for m, d in zip(qk_t, decay)]
    g_last = [g[CH - 1:CH, :] for g in gc]
    qe = [qq * e for qq, e in zip(q, eg)]
    kd = [kk * jnp.exp(gl - g) for kk, gl, g in zip(k, g_last, gc)]
    egl = [jnp.broadcast_to(jnp.exp(gl), (1, HD)) for gl in g_last]
    out = []
    for i in range(n):
        mine = slice(i * HEADS, (i + 1) * HEADS)
        cat = lambda vals: jnp.concatenate(vals[mine], axis=-1)
        out.append((cat([x[:, :HD] for x in uw]), cat([x[:, HD:] for x in uw]), cat(qe), cat(kd),
                    jnp.concatenate([m[None] for m in qk[mine]], axis=0), cat(egl)))
    return out


def _chunk_state_fn(u, w, qe, kd, qk, egl, s):
    ws = [_dot16(a, b) for a, b in zip(w, s)]
    qs = [_dot16(a, b) for a, b in zip(qe, s)]
    v_new = [a - b for a, b in zip(u, ws)]
    o = [a + _dot16(b, c) for a, b, c in zip(qs, qk, v_new)]
    s_new = [a * e + _dot16(b, c, _DIMS["tn"]) for a, e, b, c in zip(s, egl, kd, v_new)]
    return o, s_new


def _mix_fn(o, z, ur, vr, ong, lng, lnb, ws, bst):
    row = lax.broadcasted_iota(jnp.int32, (GCH, GCH), 0)
    col = lax.broadcasted_iota(jnp.int32, (GCH, GCH), 1)
    causal = row >= col
    ug = _gelu(ur)
    vg = _gelu(vr)
    outs_dn, outs_gm = [], []
    for h in range(HEADS):
        sl = slice(h * HD, (h + 1) * HD)
        oh = o[:, sl]
        oh = oh * lax.rsqrt(jnp.mean(oh * oh, axis=-1, keepdims=True) + EPS)
        outs_dn.append(oh * ong * _silu(z[:, sl]))
        vh = vg[:, sl]
        mu = jnp.mean(vh, axis=-1, keepdims=True)
        var = jnp.mean(jnp.square(vh - mu), axis=-1, keepdims=True)
        vn = (vh - mu) * lax.rsqrt(var + EPS) * lng[:, sl] + lnb[:, sl]
        sp = _dot16(jnp.where(causal, ws[h], 0.0), vn) + bst[:, h:h + 1]
        outs_gm.append(ug[:, sl] * sp)
    return jnp.concatenate(outs_dn + outs_gm, axis=-1)


def _loss_fn(h, gain, tgt):
    y = _rms_fn(h, gain)
    return 0.5 * jnp.sum(jnp.mean(jnp.square(y - tgt), axis=-1))


RT = 256


def _rows(n=D):
    return pl.BlockSpec((RT, n), lambda i: (i, 0))


def _whole(shape):
    nd = len(shape)
    return pl.BlockSpec(shape, lambda i: (0,) * nd)


def _rmsnorm(name, h, gain):
    def body(h_ref, g_ref, o_ref):
        o_ref[...] = _rms_fn(h_ref[...], g_ref[...]).astype(BF16)

    return pl.pallas_call(
        body, name=name, grid=(T // RT,), in_specs=[_rows(), _whole((1, D))], out_specs=_rows(),
        out_shape=jax.ShapeDtypeStruct((T, D), BF16), compiler_params=_cparams(("parallel",)),
    )(h, gain)


def _rmsnorm_bwd(name, dhn, h, gain, resid):
    def body(dhn_ref, h_ref, g_ref, r_ref, dh_ref, dh16_ref, dg_ref):
        _, vjp = jax.vjp(_rms_fn, h_ref[...], g_ref[...])
        dh, dg = vjp(dhn_ref[...])
        dh = r_ref[...] + dh
        dh_ref[...] = dh
        dh16_ref[...] = dh.astype(BF16)

        @pl.when(pl.program_id(0) == 0)
        def _():
            dg_ref[...] = dg

        @pl.when(pl.program_id(0) > 0)
        def _():
            dg_ref[...] += dg

    return pl.pallas_call(
        body, name=name, grid=(T // RT,), in_specs=[_rows(), _rows(), _whole((1, D)), _rows()],
        out_specs=[_rows(), _rows(), _whole((1, D))],
        out_shape=[jax.ShapeDtypeStruct((T, D), F32), jax.ShapeDtypeStruct((T, D), BF16),
                   jax.ShapeDtypeStruct((1, D), F32)],
        compiler_params=_cparams(("arbitrary",)),
    )(dhn, h, gain, resid)


def _loss_head(h, gain, tgt):
    def body(h_ref, g_ref, t_ref, l_ref, dh_ref, dh16_ref, dg_ref):
        loss, vjp = jax.vjp(lambda hh, gg: _loss_fn(hh, gg, t_ref[...]), h_ref[...], g_ref[...])
        dh, dg = vjp(jnp.ones((), F32))
        dh_ref[...] = dh
        dh16_ref[...] = dh.astype(BF16)
        lv = jnp.full((1, LANE), loss, F32)

        @pl.when(pl.program_id(0) == 0)
        def _():
            dg_ref[...] = dg
            l_ref[...] = lv

        @pl.when(pl.program_id(0) > 0)
        def _():
            dg_ref[...] += dg
            l_ref[...] += lv

    return pl.pallas_call(
        body, name="loss_head", grid=(T // RT,), in_specs=[_rows(), _whole((1, D)), _rows()],
        out_specs=[_whole((1, LANE)), _rows(), _rows(), _whole((1, D))],
        out_shape=[jax.ShapeDtypeStruct((1, LANE), F32), jax.ShapeDtypeStruct((T, D), F32),
                   jax.ShapeDtypeStruct((T, D), BF16), jax.ShapeDtypeStruct((1, D), F32)],
        compiler_params=_cparams(("arbitrary",)),
    )(h, gain, tgt)


def _prep_flags():
    j = pl.program_id(0)
    qk_scale = jnp.where(j < HEADS, HD ** -0.5, 1.0).astype(F32)
    return qk_scale, j >= 2 * HEADS


def _prep(proj, conv_w):
    def body(x_ref, w_ref, o_ref):
        qk_scale, is_v = _prep_flags()
        o_ref[...] = _prep_fn(x_ref[...], w_ref[...], qk_scale, is_v)

    col = lambda j: (0, j)
    return pl.pallas_call(
        body, name="gdn_prep", grid=(3 * HEADS,),
        in_specs=[pl.BlockSpec((T, HD), col), pl.BlockSpec((4, HD), col)], out_specs=pl.BlockSpec((T, HD), col),
        out_shape=jax.ShapeDtypeStruct((T, 3 * HW), F32), compiler_params=_cparams(("parallel",)),
    )(proj, conv_w)


def _prep_bwd(proj, conv_w, dqkv, dproj):
    def body(x_ref, w_ref, d_ref, _, dx_ref, dw_ref):
        qk_scale, is_v = _prep_flags()
        _, vjp = jax.vjp(lambda x, w: _prep_fn(x, w, qk_scale, is_v), x_ref[...], w_ref[...])
        dx, dw = vjp(d_ref[...])
        dx_ref[...] = dx.astype(BF16)
        dw_ref[...] = dw

    col = lambda j: (0, j)
    return pl.pallas_call(
        body, name="gdn_prep_bwd", grid=(3 * HEADS,),
        in_specs=[pl.BlockSpec((T, HD), col), pl.BlockSpec((4, HD), col), pl.BlockSpec((T, HD), col), ANY],
        out_specs=[pl.BlockSpec((T, HD), col), pl.BlockSpec((4, HD), col)],
        out_shape=[jax.ShapeDtypeStruct((T, NP), BF16), jax.ShapeDtypeStruct((4, 3 * HW), F32)],
        input_output_aliases={3: 0}, compiler_params=_cparams(("parallel",)),
    )(proj, conv_w, dqkv, dproj)


BA_BLK = BA_OFF // LANE


def _gates(proj, a_log, dt_bias):
    def body(x_ref, a_ref, d_ref, o_ref):
        o_ref[...] = _gates_fn(x_ref[...], a_ref[...], d_ref[...])

    return pl.pallas_call(
        body, name="gdn_gates", grid=(1,),
        in_specs=[pl.BlockSpec((T, LANE), lambda i: (0, BA_BLK)), _whole((1, LANE)), _whole((1, LANE))],
        out_specs=_whole((T, LANE)),
        out_shape=jax.ShapeDtypeStruct((T, LANE), F32), compiler_params=_cparams(("arbitrary",)),
    )(proj, a_log, dt_bias)


def _gates_bwd(proj, a_log, dt_bias, dbg, dproj):
    def body(x_ref, a_ref, d_ref, dbg_ref, _, dx_ref, da_ref, dd_ref):
        _, vjp = jax.vjp(_gates_fn, x_ref[...], a_ref[...], d_ref[...])
        dx, da_ref[...], dd_ref[...] = vjp(dbg_ref[...])
        dx_ref[...] = dx.astype(BF16)

    ba = pl.BlockSpec((T, LANE), lambda i: (0, BA_BLK))
    return pl.pallas_call(
        body, name="gdn_gates_bwd", grid=(1,),
        in_specs=[ba, _whole((1, LANE)), _whole((1, LANE)), _whole((T, LANE)), ANY],
        out_specs=[ba, _whole((1, LANE)), _whole((1, LANE))],
        out_shape=[jax.ShapeDtypeStruct((T, NP), BF16), jax.ShapeDtypeStruct((1, LANE), F32),
                   jax.ShapeDtypeStruct((1, LANE), F32)],
        input_output_aliases={4: 0}, compiler_params=_cparams(("arbitrary",)),
    )(proj, a_log, dt_bias, dbg, dproj)


NCK = T // CH
CPS = 2


def _chunk_prep_specs(rev=False):
    at = (lambda n: NCK - 1 - n) if rev else (lambda n: n)
    wide = pl.BlockSpec((CH, HW), lambda n: (at(n), 0))
    return [wide, wide, wide, wide, pl.BlockSpec((HEADS, CH, CH), lambda n: (0, at(n), 0)),
            pl.BlockSpec((None, 1, HW), lambda n: (at(n), 0, 0))]


def _chunk_prep_shapes(dtypes):
    shp = [(T, HW), (T, HW), (T, HW), (T, HW), (HEADS, T, CH), (NCK, 1, HW)]
    return [jax.ShapeDtypeStruct(s, dt) for s, dt in zip(shp, dtypes)]


def _chunk_prep(qkv, bg):
    def body(x_ref, bg_ref, *o_refs):
        rows = [slice(ci * CH, (ci + 1) * CH) for ci in range(CPS)]
        res = _chunk_prep_fn([x_ref[r, :] for r in rows], [bg_ref[r, :] for r in rows])
        for ci, (u, w, qe, kd, qk, egl) in enumerate(res):
            for o_ref, val in zip(o_refs[:4], (u, w, qe, kd)):
                o_ref[rows[ci], :] = val.astype(o_ref.dtype)
            o_refs[4][:, rows[ci], :] = qk.astype(BF16)
            o_refs[5][ci] = egl

    wide = pl.BlockSpec((CPS * CH, HW), lambda n: (n, 0))
    return pl.pallas_call(
        body, name="gdn_chunk_prep", grid=(NCK // CPS,),
        in_specs=[pl.BlockSpec((CPS * CH, 3 * HW), lambda n: (n, 0)), pl.BlockSpec((CPS * CH, LANE), lambda n: (n, 0))],
        out_specs=[wide, wide, wide, wide, pl.BlockSpec((HEADS, CPS * CH, CH), lambda n: (0, n, 0)),
                   pl.BlockSpec((CPS, 1, HW), lambda n: (n, 0, 0))],
        out_shape=_chunk_prep_shapes((F32, BF16, BF16, BF16, BF16, F32)),
        compiler_params=_cparams(("parallel",)),
    )(qkv, bg)


def _chunk_prep_bwd(qkv, bg, cots):
    def body(x_ref, bg_ref, du, dw, dqe, dkd, dqk, degl, dx_ref, dbg_ref):
        rows = [slice(ci * CH, (ci + 1) * CH) for ci in range(CPS)]
        _, vjp = jax.vjp(_chunk_prep_fn, [x_ref[r, :] for r in rows], [bg_ref[r, :] for r in rows])
        dxs, dbgs = vjp([(du[r, :], dw[r, :], dqe[r, :], dkd[r, :], dqk[:, r, :], degl[ci])
                         for ci, r in enumerate(rows)])
        for r, dx, dbg in zip(rows, dxs, dbgs):
            dx_ref[r, :] = dx
            dbg_ref[r, :] = dbg

    wide = pl.BlockSpec((CPS * CH, HW), lambda n: (n, 0))
    return pl.pallas_call(
        body, name="gdn_chunk_prep_bwd", grid=(NCK // CPS,),
        in_specs=[pl.BlockSpec((CPS * CH, 3 * HW), lambda n: (n, 0)), pl.BlockSpec((CPS * CH, LANE), lambda n: (n, 0)),
                  wide, wide, wide, wide, pl.BlockSpec((HEADS, CPS * CH, CH), lambda n: (0, n, 0)),
                  pl.BlockSpec((CPS, 1, HW), lambda n: (n, 0, 0))],
        out_specs=[pl.BlockSpec((CPS * CH, 3 * HW), lambda n: (n, 0)), pl.BlockSpec((CPS * CH, LANE), lambda n: (n, 0))],
        out_shape=[jax.ShapeDtypeStruct((T, 3 * HW), F32), jax.ShapeDtypeStruct((T, LANE), F32)],
        compiler_params=_cparams(("parallel",)),
    )(qkv, bg, *cots)


def _head_args(refs):
    u, w, qe, kd, qk, egl = refs
    sls = [slice(h * HD, (h + 1) * HD) for h in range(HEADS)]
    return ([u[:, sl] for sl in sls], [w[:, sl].astype(F32) for sl in sls], [qe[:, sl].astype(F32) for sl in sls],
            [kd[:, sl].astype(F32) for sl in sls], [qk[h].astype(F32) for h in range(HEADS)],
            [egl[:, sl] for sl in sls])


def _chunk_scan(prep):
    def body(*refs):
        o_ref, sh_ref, s_ref = refs[6:]

        @pl.when(pl.program_id(0) == 0)
        def _():
            s_ref[...] = jnp.zeros_like(s_ref)

        s = [s_ref[h] for h in range(HEADS)]
        for h in range(HEADS):
            sh_ref[h, 0] = s[h]
        o, s_new = _chunk_state_fn(*_head_args(refs[:6]), s)
        for h in range(HEADS):
            o_ref[:, h * HD:(h + 1) * HD] = o[h]
            s_ref[h] = s_new[h]

    return pl.pallas_call(
        body, name="gdn_scan", grid=(NCK,), in_specs=_chunk_prep_specs(),
        out_specs=[pl.BlockSpec((CH, HW), lambda n: (n, 0)), pl.BlockSpec((HEADS, 1, HD, HD), lambda n: (0, n, 0, 0))],
        out_shape=[jax.ShapeDtypeStruct((T, HW), F32), jax.ShapeDtypeStruct((HEADS, NCK, HD, HD), F32)],
        scratch_shapes=[pltpu.VMEM((HEADS, HD, HD), F32)], compiler_params=_cparams(("arbitrary",)),
    )(*prep)


def _chunk_scan_bwd(prep, s_hist, do):
    def body(*refs):
        sh_ref, do_ref = refs[6:8]
        d_refs = refs[8:14]
        ds_ref = refs[14]

        @pl.when(pl.program_id(0) == 0)
        def _():
            ds_ref[...] = jnp.zeros_like(ds_ref)

        sls = [slice(h * HD, (h + 1) * HD) for h in range(HEADS)]
        _, vjp = jax.vjp(_chunk_state_fn, *_head_args(refs[:6]), [sh_ref[h, 0] for h in range(HEADS)])
        du, dw, dqe, dkd, dqk, degl, ds = vjp(([do_ref[:, sl] for sl in sls], [ds_ref[h] for h in range(HEADS)]))
        for h, sl in enumerate(sls):
            for d_ref, val in zip(d_refs[:4], (du, dw, dqe, dkd)):
                d_ref[:, sl] = val[h]
            d_refs[4][h] = dqk[h]
            d_refs[5][:, sl] = degl[h]
            ds_ref[h] = ds[h]

    rev = lambda n: NCK - 1 - n
    return pl.pallas_call(
        body, name="gdn_scan_bwd", grid=(NCK,),
        in_specs=_chunk_prep_specs(rev=True) + [pl.BlockSpec((HEADS, 1, HD, HD), lambda n: (0, rev(n), 0, 0)),
                                                pl.BlockSpec((CH, HW), lambda n: (rev(n), 0))],
        out_specs=_chunk_prep_specs(rev=True), out_shape=_chunk_prep_shapes((F32,) * 6),
        scratch_shapes=[pltpu.VMEM((HEADS, HD, HD), F32)], compiler_params=_cparams(("arbitrary",)),
    )(*prep, s_hist, do)


def _mix_specs():
    pc = lambda c: pl.BlockSpec((GCH, HW), lambda i: (i, c))
    return [pl.BlockSpec((GCH, HW), lambda i: (i, 0)), pc(3), pc(4), pc(5), _whole((1, HD)), _whole((1, HW)),
            _whole((1, HW)), _whole((HEADS, GCH, GCH)), _whole((GCH, LANE))]


def _mix(o, proj, ong, lng, lnb, ws, bst):
    def body(o_ref, z_ref, u_ref, v_ref, ong_ref, lng_ref, lnb_ref, ws_ref, bs_ref, m_ref):
        m_ref[...] = _mix_fn(o_ref[...], z_ref[...], u_ref[...], v_ref[...], ong_ref[...], lng_ref[...],
                             lnb_ref[...], ws_ref[...], bs_ref[...]).astype(BF16)

    return pl.pallas_call(
        body, name="mix", grid=(T // GCH,), in_specs=_mix_specs(),
        out_specs=pl.BlockSpec((GCH, D), lambda i: (i, 0)), out_shape=jax.ShapeDtypeStruct((T, D), BF16),
        compiler_params=_cparams(("parallel",)),
    )(o, proj, proj, proj, ong, lng, lnb, ws, bst)


def _mix_bwd(o, proj, ong, lng, lnb, ws, bst, dmix):
    def body(o_ref, z_ref, u_ref, v_ref, ong_ref, lng_ref, lnb_ref, ws_ref, bs_ref, dm_ref,
             do_ref, dzuv_ref, dong_ref, dlng_ref, dlnb_ref, dws_ref, dbs_ref):
        _, vjp = jax.vjp(_mix_fn, o_ref[...], z_ref[...], u_ref[...], v_ref[...], ong_ref[...], lng_ref[...],
                         lnb_ref[...], ws_ref[...], bs_ref[...])
        do, dz, du, dv, dong, dlng, dlnb, dws, dbs = vjp(dm_ref[...])
        do_ref[...] = do
        dzuv_ref[:, 0:HW] = dz.astype(BF16)
        dzuv_ref[:, HW:2 * HW] = du.astype(BF16)
        dzuv_ref[:, 2 * HW:3 * HW] = dv.astype(BF16)
        acc = [(dong_ref, dong), (dlng_ref, dlng), (dlnb_ref, dlnb), (dws_ref, dws), (dbs_ref, dbs)]

        @pl.when(pl.program_id(0) == 0)
        def _():
            for r, val in acc:
                r[...] = val

        @pl.when(pl.program_id(0) > 0)
        def _():
            for r, val in acc:
                r[...] += val

    shp = lambda *s: jax.ShapeDtypeStruct(s, F32)
    return pl.pallas_call(
        body, name="mix_bwd", grid=(T // GCH,),
        in_specs=_mix_specs() + [pl.BlockSpec((GCH, D), lambda i: (i, 0))],
        out_specs=[pl.BlockSpec((GCH, HW), lambda i: (i, 0)), pl.BlockSpec((GCH, 3 * HW), lambda i: (i, 1)),
                   _whole((1, HD)), _whole((1, HW)), _whole((1, HW)), _whole((HEADS, GCH, GCH)), _whole((GCH, LANE))],
        out_shape=[shp(T, HW), jax.ShapeDtypeStruct((T, NP), BF16), shp(1, HD), shp(1, HW), shp(1, HW),
                   shp(HEADS, GCH, GCH), shp(GCH, LANE)],
        compiler_params=_cparams(("arbitrary",)),
    )(o, proj, proj, proj, ong, lng, lnb, ws, bst, dmix)


def _swiglu_epilogue(accs, _):
    gate, up = accs
    return [gate, up, _silu(gate) * up]


def _swiglu_bwd_epilogue(accs, extras):
    dact = accs[0]
    gate, up = (e.astype(F32) for e in extras)
    sg = _sigmoid(gate)
    return [dact * up * (sg * (1.0 + gate * (1.0 - sg))), dact * (gate * sg)]


def _layer_fwd(h, p):
    hn = _rmsnorm("rms_mix", h, p["norm_mix"])
    proj = _mm("in_proj", "nn", hn[None], [p["w_in"][None]], tm=1024, tn=640, tk=D)[0][0]
    qkv = _prep(proj, p["conv_w"])
    bg = _gates(proj, p["a_log"], p["dt_bias"])
    prep = _chunk_prep(qkv, bg)
    o, s_hist = _chunk_scan(prep)
    mix = _mix(o, proj, p["o_norm_g"], p["ln_v_g"], p["ln_v_b"], p["w_s"], p["bst"])
    l = p["layer"]
    of_layer = dict(b_g=lambda g: DEPTH * g + l, g_n=NCHIP)
    h1 = _mm("out_proj", "nn", mix[None], [p["w_out"]], tm=1024, tn=512, tk=D // NCHIP, resid=h[None], n_n=D,
             b_spec=((None, D // NCHIP, 512), lambda g, i, j, k: (DEPTH * k + l, 0, j)))[0][0]
    h2n = _rmsnorm("rms_ffn", h1, p["norm_ffn"])
    gate, up, act = _mm("ffn_in", "nn", h2n[None], [p["w_gate"], p["w_up"]], tm=1024, tn=FF_SH, tk=D,
                        out_dtypes=(BF16, BF16, BF16), epilogue=_swiglu_epilogue, **of_layer)
    h2 = _mm("ffn_out", "nn", act, [p["w_down"]], tm=1024, tn=512, tk=FF_SH, reduce_g=True, resid=h1[None],
             **of_layer)[0][0]
    saved = dict(h=h, hn=hn, proj=proj, qkv=qkv, bg=bg, prep=prep, o=o, s_hist=s_hist, mix=mix, h1=h1, h2n=h2n,
                 gate=gate, up=up, act=act)
    return h2, saved


def _layer_bwd(dh2, dh2b, p, s):
    l = p["layer"]
    of_layer = dict(b_g=lambda g: DEPTH * g + l, g_n=NCHIP)
    dh2b = dh2b[None]
    dgate, dup = _mm("ffn_out_bwd", "nt", dh2b, [p["w_down"]], tm=1024, tn=FF_SH, tk=D, out_dtypes=(BF16, BF16),
                     extras=(s["gate"], s["up"]), epilogue=_swiglu_bwd_epilogue, **of_layer)
    dh2n = _mm("ffn_gate_bwd", "nt", dgate, [p["w_gate"]], tm=1024, tn=512, tk=FF_SH, reduce_g=True, **of_layer)[0]
    dh2n = _mm("ffn_up_bwd", "nt", dup, [p["w_up"]], tm=1024, tn=512, tk=FF_SH, reduce_g=True, resid=dh2n,
               **of_layer)[0][0]
    dh1, dh1b, d_norm_ffn = _rmsnorm_bwd("rms_ffn_bwd", dh2n, s["h1"], p["norm_ffn"], dh2)
    d_w_down = _mm("ffn_wdown_grad", "tn", s["act"], [dh2b], tm=FF_SH, tn=512, tk=1024)[0]
    d_w_gate = _mm("ffn_wgate_grad", "tn", s["h2n"][None], [dgate], tm=512, tn=FF_SH, tk=1024)[0]
    d_w_up = _mm("ffn_wup_grad", "tn", s["h2n"][None], [dup], tm=512, tn=FF_SH, tk=1024)[0]
    dh1b = dh1b[None]
    dmix = _mm("out_proj_bwd", "nt", dh1b, [p["w_out"]], tm=1024, tn=D // NCHIP, tk=D, n_n=D,
               b_spec=((None, D // NCHIP, D), lambda g, i, j, k: (DEPTH * j + l, 0, k)))[0][0]
    d_w_out = _mm("out_proj_wgrad", "tn", s["mix"][None], [dh1b], tm=512, tn=512, tk=1024)[0][0]
    do, dproj, d_ong, d_lng, d_lnb, d_ws, d_bst = _mix_bwd(
        s["o"], s["proj"], p["o_norm_g"], p["ln_v_g"], p["ln_v_b"], p["w_s"], p["bst"], dmix)
    dqkv, dbg = _chunk_prep_bwd(s["qkv"], s["bg"], _chunk_scan_bwd(s["prep"], s["s_hist"], do))
    dproj, d_conv = _prep_bwd(s["proj"], p["conv_w"], dqkv, dproj)
    dproj, d_a_log, d_dt_bias = _gates_bwd(s["proj"], p["a_log"], p["dt_bias"], dbg, dproj)
    dproj = dproj[None]
    dhn = _mm("in_proj_bwd", "nt", dproj, [p["w_in"][None]], tm=1024, tn=512, tk=640)[0][0]
    dh, dhb, d_norm_mix = _rmsnorm_bwd("rms_mix_bwd", dhn, s["h"], p["norm_mix"], dh1)
    d_w_in = _mm("in_proj_wgrad", "tn", s["hn"][None], [dproj], tm=512, tn=640, tk=1024)[0]
    grads = dict(norm_mix=d_norm_mix, w_in=d_w_in, conv_w=d_conv, a_log=d_a_log, dt_bias=d_dt_bias, o_norm_g=d_ong,
                 ln_v_g=d_lng, ln_v_b=d_lnb, w_s=d_ws, bst=d_bst, w_out=d_w_out, norm_ffn=d_norm_ffn,
                 w_gate=d_w_gate, w_up=d_w_up, w_down=d_w_down)
    return dh, dhb, grads


def _lanes(v, off=0):
    return jnp.zeros((1, LANE), F32).at[0, off:off + v.shape[0]].set(v)


def _w_in_pieces():
    regions = [(0, 2048, 0), (2048, 2056, BA_OFF), (2056, IN_DIM, 2048)]
    sh = IN_DIM // NCHIP
    out = []
    for j in range(NCHIP):
        for lo, hi, at in regions:
            a, b = max(lo, j * sh), min(hi, (j + 1) * sh)
            if a < b:
                out.append((j, a - j * sh, at + a - lo, b - a))
    return out


W_IN_PIECES = _w_in_pieces()
WT = 256


def _assemble_w_in(gathered, own, l, place):
    def body(place_ref, g_ref, own_ref, o_ref):
        o_ref[:, IN_DIM:] = jnp.zeros((WT, NP - IN_DIM), BF16)
        mine = own_ref[...]
        for j, src, dst, width in W_IN_PIECES:
            val = jnp.where(place_ref[0] == j, mine[:, src:src + width], g_ref[j, :, src:src + width])
            o_ref[:, dst:dst + width] = val

    sh = IN_DIM // NCHIP
    return pl.pallas_call(
        body, name="assemble_w_in",
        grid_spec=pltpu.PrefetchScalarGridSpec(
            num_scalar_prefetch=1, grid=(D // WT,),
            in_specs=[pl.BlockSpec((NCHIP, None, WT, sh), lambda i, place_ref: (0, l, i, 0)),
                      pl.BlockSpec((None, WT, sh), lambda i, place_ref: (l, i, 0))],
            out_specs=pl.BlockSpec((WT, NP), lambda i, place_ref: (i, 0))),
        out_shape=jax.ShapeDtypeStruct((D, NP), BF16), compiler_params=_cparams(("parallel",)),
    )(place, gathered, own)


def _layer_params(l, big, small):
    return dict(
        layer=l, w_in=big["w_in"][l], w_out=big["w_out"], w_gate=big["w_gate"], w_up=big["w_up"], w_down=big["w_down"],
        conv_w=jnp.concatenate([big["conv_w"][j, l] for j in range(NCHIP)], axis=1),
        norm_mix=small["norm_mix"][l][None], norm_ffn=small["norm_ffn"][l][None],
        a_log=_lanes(small["a_log"][l], HEADS), dt_bias=_lanes(small["dt_bias"][l], HEADS),
        o_norm_g=small["o_norm_g"][l][None], ln_v_g=small["ln_v_g"][l][None], ln_v_b=small["ln_v_b"][l][None],
        w_s=small["w_s"][l],
        bst=jnp.pad(small["b_s"][l].T, ((0, 0), (0, LANE - HEADS))),
    )


def _reference_layout(g):
    return dict(
        w_in=g["w_in"],
        w_out=g["w_out"].reshape(NCHIP, D // NCHIP, D),
        w_gate=g["w_gate"], w_up=g["w_up"], w_down=g["w_down"],
        conv_w=g["conv_w"], norm_mix=g["norm_mix"][0], norm_ffn=g["norm_ffn"][0],
        a_log=g["a_log"][0, HEADS:2 * HEADS], dt_bias=g["dt_bias"][0, HEADS:2 * HEADS],
        o_norm_g=g["o_norm_g"][0], ln_v_g=g["ln_v_g"][0], ln_v_b=g["ln_v_b"][0], w_s=g["w_s"],
        b_s=g["bst"][:, :HEADS].T,
    )


def _local_step(x, tgt, layers, norm_final):
    h = x
    saved = []
    for p in layers:
        h, s = _layer_fwd(h, p)
        saved.append(s)
    loss, dh, dhb, d_norm_final = _loss_head(h, norm_final, tgt)
    grads = [None] * DEPTH
    for l in reversed(range(DEPTH)):
        dh, dhb, grads[l] = _layer_bwd(dh, dhb, layers[l], saved[l])
    return loss, dh, grads, d_norm_final


def _place():
    x, y, c = lax.axis_index("x"), lax.axis_index("y"), lax.axis_index("c")
    return x, y, c, [(1 - x, y), (x, 1 - y), (1 - x, 1 - y)]


def _remote(src, dst, send_sem, recv_sem, to):
    return pltpu.make_async_remote_copy(src_ref=src, dst_ref=dst, send_sem=send_sem, recv_sem=recv_sem,
                                        device_id=to, device_id_type=MESH)


def _comm_call(name, body, ins, out_shape, n_sems, aliases=None):
    return pl.pallas_call(
        body, name=name, in_specs=[ANY] * len(ins), out_specs=[ANY] * len(out_shape), out_shape=out_shape,
        scratch_shapes=[pltpu.SemaphoreType.DMA((n,)) for n in n_sems], input_output_aliases=aliases or {},
        compiler_params=pltpu.CompilerParams(has_side_effects=True),
    )(*ins)


def _gather_weights(shards, conv):
    n = len(shards)

    def body(*refs):
        ins, conv_in = refs[:n], refs[n]
        outs, conv_out = refs[n + 1:2 * n + 1], refs[2 * n + 1]
        ici_s, ici_r, d2d_s, d2d_r = refs[2 * n + 2:]
        x, y, c, others = _place()
        chip = 2 * x + y
        sibling = (x, y, 1 - c)

        def half(a, of_c):
            hr = ins[a].shape[1] // 2
            return pl.ds(pl.multiple_of(of_c * hr, 16), hr)

        sends = []
        for a in range(n):
            for k, (ox, oy) in enumerate(others):
                sends.append(_remote(ins[a].at[:, half(a, c)], outs[a].at[chip, :, half(a, c)],
                                     ici_s.at[3 * a + k], ici_r.at[3 * a + k], (ox, oy, c)))
        for k, (ox, oy) in enumerate(others):
            sends.append(_remote(conv_in, conv_out.at[chip], ici_s.at[3 * n + k], ici_r.at[3 * n + k], (ox, oy, c)))
        for cp in sends:
            cp.start()
        passed = []
        for a in range(n):
            for k, (ox, oy) in enumerate(others):
                landed = outs[a].at[2 * ox + oy, :, half(a, c)]
                _remote(landed, landed, ici_s.at[3 * a + k], ici_r.at[3 * a + k], (ox, oy, c)).wait_recv()
                cp = _remote(landed, landed, d2d_s.at[3 * a + k], d2d_r.at[3 * a + k], sibling)
                cp.start()
                passed.append(cp)
        for k, (ox, oy) in enumerate(others):
            landed = conv_out.at[2 * ox + oy]
            _remote(landed, landed, ici_s.at[3 * n + k], ici_r.at[3 * n + k], (ox, oy, c)).wait_recv()
        for a in range(n):
            for k, (ox, oy) in enumerate(others):
                landed = outs[a].at[2 * ox + oy, :, half(a, 1 - c)]
                _remote(landed, landed, d2d_s.at[3 * a + k], d2d_r.at[3 * a + k], sibling).wait_recv()
        for cp in sends + passed:
            cp.wait_send()

    out_shape = [jax.ShapeDtypeStruct((NCHIP,) + s.shape, s.dtype) for s in list(shards) + [conv]]
    return _comm_call("gather_weights", body, list(shards) + [conv], out_shape, [3 * n + 3, 3 * n + 3, 3 * n, 3 * n])


def _exchange_halves(gs):
    n = len(gs)

    def body(*refs):
        ins, outs = refs[:n], refs[n:2 * n]
        send_s, recv_s = refs[2 * n:]
        x, y, c, _ = _place()
        copies = []
        for a in range(n):
            hr = ins[a].shape[1] // 2
            theirs = ins[a].at[:, pl.ds(pl.multiple_of((1 - c) * hr, 8), hr)]
            copies.append(_remote(theirs, outs[a], send_s.at[a], recv_s.at[a], (x, y, 1 - c)))
        for cp in copies:
            cp.start()
        for cp in copies:
            cp.wait()

    out_shape = [jax.ShapeDtypeStruct((g.shape[0], g.shape[1] // 2, g.shape[2]), F32) for g in gs]
    return _comm_call("exchange_halves", body, gs, out_shape, [n, n])


def _scatter_partials(ps):
    n = len(ps)

    def body(*refs):
        ins, outs = refs[:n], refs[n:2 * n]
        send_s, recv_s = refs[2 * n:]
        x, y, c, others = _place()
        copies = []
        for a in range(n):
            for k, (ox, oy) in enumerate(others):
                copies.append(_remote(ins[a].at[2 * ox + oy], outs[a].at[k], send_s.at[3 * a + k],
                                      recv_s.at[3 * a + k], (ox, oy, c)))
        for cp in copies:
            cp.start()
        for cp in copies:
            cp.wait()

    out_shape = [jax.ShapeDtypeStruct((3,) + p.shape[1:], p.dtype) for p in ps]
    return _comm_call("scatter_partials", body, ps, out_shape, [3 * n, 3 * n])


def _join_halves(rs):
    n = len(rs)

    def body(*refs):
        outs = refs[n:2 * n]
        send_s, recv_s = refs[2 * n:]
        x, y, c, _ = _place()
        sibling = (x, y, 1 - c)

        def half(a, of_c):
            hr = outs[a].shape[1] // 2
            return outs[a].at[:, pl.ds(pl.multiple_of(of_c * hr, 8), hr)]

        copies = [_remote(half(a, c), half(a, c), send_s.at[a], recv_s.at[a], sibling) for a in range(n)]
        for cp in copies:
            cp.start()
        for a in range(n):
            landed = half(a, 1 - c)
            _remote(landed, landed, send_s.at[a], recv_s.at[a], sibling).wait_recv()
        for cp in copies:
            cp.wait_send()

    out_shape = [jax.ShapeDtypeStruct(r.shape, r.dtype) for r in rs]
    return _comm_call("join_halves", body, rs, out_shape, [n, n], aliases={a: a for a in range(n)})


NDEV = 8


def _allreduce_small(buf):
    r = buf.shape[0]

    def body(in_ref, out_ref, gath, send_s, recv_s):
        x, y, c, _ = _place()
        me = 4 * x + 2 * y + c
        copies = []
        for rel in range(1, NDEV):
            px = 1 - x if rel & 4 else x
            py = 1 - y if rel & 2 else y
            pc = 1 - c if rel & 1 else c
            copies.append((_remote(in_ref, gath.at[me], send_s.at[rel - 1], recv_s.at[rel - 1], (px, py, pc)),
                           4 * px + 2 * py + pc))
        for cp, _ in copies:
            cp.start()
        gath[me] = in_ref[...]
        for rel, (cp, peer) in enumerate(copies):
            landed = gath.at[peer]
            _remote(landed, landed, send_s.at[rel], recv_s.at[rel], (x, y, c)).wait_recv()
        for cp, _ in copies:
            cp.wait_send()
        total = gath[0]
        for d in range(1, NDEV):
            total = total + gath[d]
        out_ref[...] = total

    vm = pl.BlockSpec(memory_space=pltpu.VMEM)
    return pl.pallas_call(
        body, name="allreduce_small", in_specs=[vm], out_specs=vm, out_shape=jax.ShapeDtypeStruct((r, LANE), F32),
        scratch_shapes=[pltpu.VMEM((NDEV, r, LANE), F32), pltpu.SemaphoreType.DMA((NDEV - 1,)),
                        pltpu.SemaphoreType.DMA((NDEV - 1,))],
        compiler_params=pltpu.CompilerParams(has_side_effects=True, vmem_limit_bytes=VMEM_LIMIT),
    )(buf)


MAX_ROW_TILE = 512
BF16_ROWS = 16


def _row_tile(rows):
    for t in range(min(rows, MAX_ROW_TILE) // BF16_ROWS * BF16_ROWS, 0, -BF16_ROWS):
        if rows % t == 0:
            return t
    raise ValueError(rows)


def _sum_halves(g, theirs, c_arr):
    nch, rows, cols = g.shape
    hr = rows // 2
    tr = _row_tile(hr)

    def body(c_ref, g_ref, t_ref, o_ref, ob_ref):
        s = g_ref[...] + t_ref[...]
        o_ref[...] = s
        ob_ref[...] = s.astype(BF16)

    blk = pl.BlockSpec((None, tr, cols), lambda j, i, c_ref: (j, i, 0))
    return pl.pallas_call(
        body, name="sum_halves",
        grid_spec=pltpu.PrefetchScalarGridSpec(
            num_scalar_prefetch=1, grid=(nch, hr // tr),
            in_specs=[pl.BlockSpec((None, None, tr, cols), lambda j, i, c_ref: (j, c_ref[0], i, 0)), blk],
            out_specs=[blk, blk]),
        out_shape=[jax.ShapeDtypeStruct((nch, hr, cols), F32), jax.ShapeDtypeStruct((nch, hr, cols), BF16)],
        compiler_params=_cparams(("parallel", "parallel")),
    )(c_arr, g.reshape(nch, 2, hr, cols), theirs)


def _sum_halves_w_in(g, theirs, c_arr):
    hr = D // 2
    sh = IN_DIM // NCHIP

    def body(c_ref, g_ref, t_ref, o_ref, ob_ref):
        s = g_ref[...] + t_ref[...]
        for j, dst, src, width in W_IN_PIECES:
            o_ref[j, :, dst:dst + width] = s[:, src:src + width]
            ob_ref[j, :, dst:dst + width] = s[:, src:src + width].astype(BF16)

    out = pl.BlockSpec((NCHIP, WT, sh), lambda i, c_ref: (0, i, 0))
    return pl.pallas_call(
        body, name="sum_halves_w_in",
        grid_spec=pltpu.PrefetchScalarGridSpec(
            num_scalar_prefetch=1, grid=(hr // WT,),
            in_specs=[pl.BlockSpec((None, WT, NP), lambda i, c_ref: (c_ref[0], i, 0)),
                      pl.BlockSpec((None, WT, NP), lambda i, c_ref: (0, i, 0))],
            out_specs=[out, out]),
        out_shape=[jax.ShapeDtypeStruct((NCHIP, hr, sh), F32), jax.ShapeDtypeStruct((NCHIP, hr, sh), BF16)],
        compiler_params=_cparams(("parallel",)),
    )(c_arr, g.reshape(2, hr, NP), theirs)


def _sum_chips(p, q, place, l, into=None):
    _, rows, cols = p.shape
    tr = _row_tile(rows)
    steps = rows // tr

    def body(place_ref, p_ref, q0, q1, q2, *rest):
        rest[-1][...] = ((p_ref[...] + q0[...].astype(F32)) + q1[...].astype(F32)) + q2[...].astype(F32)

    qs = lambda k: pl.BlockSpec((None, tr, cols), lambda i, place_ref: (k, i, 0))
    return pl.pallas_call(
        body, name="sum_chips",
        grid_spec=pltpu.PrefetchScalarGridSpec(
            num_scalar_prefetch=1, grid=(steps,),
            in_specs=[pl.BlockSpec((None, tr, cols), lambda i, place_ref: (place_ref[0], i, 0)), qs(0), qs(1), qs(2)]
            + ([ANY] if into is not None else []),
            out_specs=pl.BlockSpec((None, tr, cols), lambda i, place_ref: (l, place_ref[1] * steps + i, 0))),
        out_shape=jax.ShapeDtypeStruct((DEPTH, 2 * rows, cols), F32),
        input_output_aliases={5: 0} if into is not None else {},
        compiler_params=_cparams(("parallel",)),
    )(place, p, q, q, q, *([into] if into is not None else []))


def _adamw(w, g, m, v):
    layers, rows, cols = w.shape
    tr = _row_tile(rows)

    def body(w_ref, g_ref, m_ref, v_ref, d_ref, nm_ref, nv_ref):
        gv = g_ref[...]
        nm = ADAM_B1 * m_ref[...] + (1.0 - ADAM_B1) * gv
        nv = ADAM_B2 * v_ref[...] + (1.0 - ADAM_B2) * jnp.square(gv)
        m_hat = nm / (1.0 - ADAM_B1 ** ADAM_STEP)
        v_hat = nv / (1.0 - ADAM_B2 ** ADAM_STEP)
        d_ref[...] = -ADAM_LR * (m_hat / (jnp.sqrt(v_hat) + ADAM_EPS) + ADAM_WD * w_ref[...])
        nm_ref[...] = nm
        nv_ref[...] = nv

    blk = pl.BlockSpec((None, tr, cols), lambda l, i: (l, i, 0))
    return pl.pallas_call(
        body, name="adamw", grid=(layers, rows // tr), in_specs=[blk] * 4, out_specs=[blk] * 3,
        out_shape=[jax.ShapeDtypeStruct(w.shape, F32)] * 3, compiler_params=_cparams(("parallel", "parallel")),
    )(w, g, m, v)


BIG = ("w_in", "w_out", "w_gate", "w_up", "w_down")
SMALL = ("norm_mix", "a_log", "dt_bias", "o_norm_g", "ln_v_g", "ln_v_b", "w_s", "b_s", "norm_ffn", "norm_final")
ORDER = ("norm_mix", "w_in", "conv_w", "a_log", "dt_bias", "o_norm_g", "ln_v_g", "ln_v_b", "w_s", "b_s", "w_out",
         "norm_ffn", "w_gate", "w_up", "w_down", "norm_final")


F32_ROWS = 8
PACK_ROWS = 128


def _lane_rows(size):
    return -(-size // (F32_ROWS * LANE)) * F32_ROWS


def _pack(arrs):
    parts = [jnp.pad(a.reshape(-1), (0, _lane_rows(a.size) * LANE - a.size)).reshape(-1, LANE) for a in arrs]
    rows = sum(p.shape[0] for p in parts)
    if rows % PACK_ROWS:
        parts.append(jnp.zeros((-rows % PACK_ROWS, LANE), F32))
    return jnp.concatenate(parts, axis=0)


def _unpack(buf, like):
    out, row = [], 0
    for a in like:
        n = _lane_rows(a.size)
        out.append(buf[row:row + n].reshape(-1)[:a.size].reshape(a.shape))
        row += n
    return out


def kernel(x, norm_mix, w_in, conv_w, a_log, dt_bias, o_norm_g, ln_v_g, ln_v_b, w_s, b_s, w_out, norm_ffn, w_gate, w_up, w_down, norm_final, loss_target, m_norm_mix, m_w_in, m_conv_w, m_a_log, m_dt_bias, m_o_norm_g, m_ln_v_g, m_ln_v_b, m_w_s, m_b_s, m_w_out, m_norm_ffn, m_w_gate, m_w_up, m_w_down, m_norm_final, v_norm_mix, v_w_in, v_conv_w, v_a_log, v_dt_bias, v_o_norm_g, v_ln_v_g, v_ln_v_b, v_w_s, v_b_s, v_w_out, v_norm_ffn, v_w_gate, v_w_up, v_w_down, v_norm_final):
    w = dict(norm_mix=norm_mix, w_in=w_in, conv_w=conv_w, a_log=a_log, dt_bias=dt_bias, o_norm_g=o_norm_g,
             ln_v_g=ln_v_g, ln_v_b=ln_v_b, w_s=w_s, b_s=b_s, w_out=w_out, norm_ffn=norm_ffn, w_gate=w_gate, w_up=w_up,
             w_down=w_down, norm_final=norm_final)
    m = dict(norm_mix=m_norm_mix, w_in=m_w_in, conv_w=m_conv_w, a_log=m_a_log, dt_bias=m_dt_bias, o_norm_g=m_o_norm_g,
             ln_v_g=m_ln_v_g, ln_v_b=m_ln_v_b, w_s=m_w_s, b_s=m_b_s, w_out=m_w_out, norm_ffn=m_norm_ffn,
             w_gate=m_w_gate, w_up=m_w_up, w_down=m_w_down, norm_final=m_norm_final)
    v = dict(norm_mix=v_norm_mix, w_in=v_w_in, conv_w=v_conv_w, a_log=v_a_log, dt_bias=v_dt_bias, o_norm_g=v_o_norm_g,
             ln_v_g=v_ln_v_g, ln_v_b=v_ln_v_b, w_s=v_w_s, b_s=v_b_s, w_out=v_w_out, norm_ffn=v_norm_ffn,
             w_gate=v_w_gate, w_up=v_w_up, w_down=v_w_down, norm_final=v_norm_final)
    chip = 2 * lax.axis_index("x") + lax.axis_index("y")
    place = jnp.stack([chip, lax.axis_index("c")]).astype(jnp.int32)
    c_arr = place[1:]

    own = [w[n].astype(BF16) for n in BIG] + [conv_w]
    gathered = _gather_weights(own[:-1], conv_w)
    big = {n: lax.dynamic_update_index_in_dim(g, o, chip, 0) for n, g, o in zip(BIG + ("conv_w",), gathered, own)}
    for n in BIG[1:]:
        big[n] = big[n].reshape((NCHIP * DEPTH,) + big[n].shape[2:])
    big["w_in"] = [_assemble_w_in(gathered[0], own[0], l, place) for l in range(DEPTH)]
    layers = [_layer_params(l, big, w) for l in range(DEPTH)]

    loss_lanes, dx, grads, d_norm_final = _local_step(x[0], loss_target[0], layers, norm_final[None])
    loss = lax.psum(loss_lanes[0, 0], ("x", "y", "c"))
    gl = [_reference_layout(g) for g in grads]

    mine = [gl[l][n] for n in BIG for l in range(DEPTH)]
    theirs = _exchange_halves(mine)
    sums = [(_sum_halves_w_in if i < DEPTH else _sum_halves)(g, t, c_arr) for i, (g, t) in enumerate(zip(mine, theirs))]
    arrived = _scatter_partials([s16 for _, s16 in sums])
    reduced = []
    for a in range(len(BIG)):
        buf = None
        for l in range(DEPTH):
            buf = _sum_chips(sums[DEPTH * a + l][0], arrived[DEPTH * a + l], place, l, into=buf)
        reduced.append(buf)
    g_out = dict(zip(BIG, _join_halves(reduced)))

    small_g = [jnp.stack([gl[l][n] for l in range(DEPTH)]) for n in SMALL[:-1]] + [d_norm_final[0]]
    conv_g = jnp.stack([gl[l]["conv_w"] for l in range(DEPTH)])
    total = _allreduce_small(_pack(small_g + [conv_g]))
    *small_r, conv_r = _unpack(total, small_g + [conv_g])
    g_out.update(zip(SMALL, small_r))
    g_out["conv_w"] = lax.dynamic_slice_in_dim(conv_r, chip * conv_w.shape[2], conv_w.shape[2], axis=2)

    delta, new_m, new_v = {}, {}, {}
    for n in BIG:
        delta[n], new_m[n], new_v[n] = _adamw(w[n], g_out[n], m[n], v[n])
    rest = SMALL + ("conv_w",)
    like = [w[n] for n in rest]
    d, nm, nv = _adamw(*[_pack([src[n] for n in rest])[None] for src in (w, g_out, m, v)])
    for dst, buf in ((delta, d), (new_m, nm), (new_v, nv)):
        dst.update(zip(rest, _unpack(buf[0], like)))

    return (loss, dx[None], *[g_out[n] for n in ORDER], *[delta[n] for n in ORDER], *[new_m[n] for n in ORDER],
            *[new_v[n] for n in ORDER])
```

```python
import functools

import jax
import jax.numpy as jnp
from jax import lax
from jax.experimental import pallas as pl
from jax.experimental.pallas import tpu as pltpu

F32 = jnp.float32
BF16 = jnp.bfloat16
MESH = pl.DeviceIdType.MESH
ANY = pl.BlockSpec(memory_space=pl.ANY)
HIGHEST = lax.Precision.HIGHEST

T = 2048
D = 1024
DEPTH = 2
NCHIP = 4
HEADS = 4
HD = 128
HW = HEADS * HD
CH = 64
GCH = 128
IN_DIM = 3080
NP = 3200
BA_OFF = 3072
FF_SH = 704
EPS = 1e-6
LANE = 128
VMEM_LIMIT = 56 * 1024 * 1024

ADAM_LR = 0.001
ADAM_B1 = 0.9
ADAM_B2 = 0.999
ADAM_EPS = 1e-08
ADAM_WD = 0.01
ADAM_STEP = 10


def _cparams(sem=None):
    return pltpu.CompilerParams(dimension_semantics=sem, vmem_limit_bytes=VMEM_LIMIT)


_DIMS = {"nn": (((1,), (0,)), ((), ())), "nt": (((1,), (1,)), ((), ())), "tn": (((0,), (0,)), ((), ()))}


def _mm(name, mode, a, bs, *, tm, tn, tk, out_dtypes=(F32,), reduce_g=False, resid=None, extras=(), epilogue=None,
        b_g=None, b_spec=None, g_n=None, n_n=None, after=()):
    nb = len(bs)
    ga = a.shape[0]
    gbs = [1 if b_spec is not None else (b.shape[0] if b_g is None else g_n) for b in bs]
    g_n = max([ga] + gbs)
    if mode == "tn":
        k_n, m_n = a.shape[1:]
    else:
        m_n, k_n = a.shape[1:]
    if n_n is None:
        n_n = bs[0].shape[1] if mode == "nt" else bs[0].shape[2]
    assert m_n % tm == 0 and n_n % tn == 0 and k_n % tk == 0, (name, m_n, n_n, k_n)
    mi, nj, kk = m_n // tm, n_n // tn, k_n // tk
    if reduce_g:
        grid = (mi, nj, g_n, kk)
        ids = lambda i, j, g, k: (g, i, j, k)
        n_red = g_n * kk
        red_idx = lambda: pl.program_id(2) * kk + pl.program_id(3)
        sem = ("parallel", "parallel", "arbitrary", "arbitrary")
    else:
        grid = (g_n, mi, nj, kk)
        ids = lambda g, i, j, k: (g, i, j, k)
        n_red = kk
        red_idx = lambda: pl.program_id(3)
        sem = ("parallel", "parallel", "parallel", "arbitrary")

    def pick(gsz, g):
        return g if gsz > 1 else 0

    def a_map(*p):
        g, i, j, k = ids(*p)
        return (pick(ga, g), k, i) if mode == "tn" else (pick(ga, g), i, k)

    def b_map(gsz):
        def f(*p):
            g, i, j, k = ids(*p)
            if b_spec is not None:
                return b_spec[1](g, i, j, k)
            lead = pick(gsz, g) if b_g is None else b_g(g)
            return (lead, j, k) if mode == "nt" else (lead, k, j)
        return f

    def o_map(gsz):
        def f(*p):
            g, i, j, k = ids(*p)
            return (0 if reduce_g else pick(gsz, g), i, j)
        return f

    a_spec = pl.BlockSpec((None, tk, tm) if mode == "tn" else (None, tm, tk), a_map)
    b_block = b_spec[0] if b_spec is not None else ((None, tn, tk) if mode == "nt" else (None, tk, tn))
    b_specs = [pl.BlockSpec(b_block, b_map(gs)) for gs in gbs]
    x_specs = [pl.BlockSpec((None, tm, tn), o_map(e.shape[0])) for e in extras]
    r_specs = [pl.BlockSpec((None, tm, tn), o_map(resid.shape[0]))] if resid is not None else []
    g_out = 1 if reduce_g else g_n
    out_shape = [jax.ShapeDtypeStruct((g_out, m_n, n_n), dt) for dt in out_dtypes]
    out_specs = [pl.BlockSpec((None, tm, tn), o_map(g_out)) for _ in out_dtypes]
    nx, nr, no = len(extras), len(r_specs), len(out_dtypes)
    n_in = 1 + nb + nx + nr + len(after)
    dims = _DIMS[mode]

    def body(*refs):
        a_ref = refs[0]
        b_refs = refs[1:1 + nb]
        x_refs = refs[1 + nb:1 + nb + nx]
        r_refs = refs[1 + nb + nx:1 + nb + nx + nr]
        o_refs = refs[n_in:n_in + no]
        acc_refs = refs[n_in + no:]
        r = red_idx()
        av = a_ref[...]
        for b_ref, acc in zip(b_refs, acc_refs):
            p = lax.dot_general(av, b_ref[...], dims, preferred_element_type=F32)

            @pl.when(r == 0)
            def _():
                acc[...] = p

            @pl.when(r > 0)
            def _():
                acc[...] += p

        @pl.when(r == n_red - 1)
        def _():
            accs = [acc[...] for acc in acc_refs]
            if r_refs:
                accs[0] = accs[0] + r_refs[0][...]
            outs = epilogue(accs, [x[...] for x in x_refs]) if epilogue is not None else accs
            for o_ref, o in zip(o_refs, outs):
                o_ref[...] = o.astype(o_ref.dtype)

    return pl.pallas_call(
        body, name=name, grid=grid,
        in_specs=[a_spec] + b_specs + x_specs + r_specs + [ANY] * len(after),
        out_specs=out_specs, out_shape=out_shape,
        scratch_shapes=[pltpu.VMEM((tm, tn), F32) for _ in range(nb)],
        compiler_params=_cparams(sem),
    )(a, *bs, *extras, *([resid] if resid is not None else []), *after)


def _sigmoid(x):
    return 1.0 / (1.0 + jnp.exp(-x))


def _silu(x):
    return x * _sigmoid(x)


def _gelu(x):
    return 0.5 * x * (1.0 + jnp.tanh(0.7978845608028654 * (x + 0.044715 * (x * x * x))))


def _rms_fn(h, gain):
    return h * lax.rsqrt(jnp.mean(h * h, axis=-1, keepdims=True) + EPS) * gain


def _shift_impl(x, s):
    n = x.shape[0]
    rolled = pltpu.roll(x, s % n, 0)
    row = lax.broadcasted_iota(jnp.int32, x.shape, 0)
    return jnp.where((row >= s) & (row < n + s), rolled, 0.0)


@functools.partial(jax.custom_vjp, nondiff_argnums=(1,))
def _shift(x, s):
    return _shift_impl(x, s)


def _shift_fwd(x, s):
    return _shift_impl(x, s), None


def _shift_bwd(s, _, g):
    return (_shift_impl(g, -s),)


_shift.defvjp(_shift_fwd, _shift_bwd)


def _prep_fn(x, w, qk_scale, is_v):
    y = x * w[3:4, :]
    for i in range(3):
        y = y + _shift(x, 3 - i) * w[i:i + 1, :]
    y = _silu(y)
    nrm = lax.rsqrt(jnp.sum(y * y, axis=-1, keepdims=True) + EPS) * qk_scale
    return y * jnp.where(is_v, 1.0, nrm)


def _softplus(x):
    return jnp.maximum(x, 0.0) + jnp.log(1.0 + jnp.exp(-jnp.abs(x)))


def _gates_fn(ba, a_log, dt_bias):
    lane = lax.broadcasted_iota(jnp.int32, ba.shape, 1)
    beta = _sigmoid(ba)
    g = -jnp.exp(a_log) * _softplus(ba + dt_bias)
    return jnp.where(lane < HEADS, beta, g)


def _dot16(a, b, dims=_DIMS["nn"]):
    return lax.dot_general(a.astype(BF16), b.astype(BF16), dims, preferred_element_type=F32)


def _dot32(a, b):
    return jnp.dot(a, b, preferred_element_type=F32, precision=HIGHEST)


def _dot3(a, b, dims=_DIMS["nn"]):
    return lax.dot_general(a, b, dims, preferred_element_type=F32, precision=lax.Precision.HIGH)


def _tri_inverses(mats):
    row = lax.broadcasted_iota(jnp.int32, (CH, CH), 0)
    col = lax.broadcasted_iota(jnp.int32, (CH, CH), 1)
    eye = (row == col).astype(F32)
    ts = [eye - a for a in mats]
    ps = list(mats)
    for _ in range(5):
        ps = [_dot3(p, p) for p in ps]
        ts = [t + _dot3(t, p) for t, p in zip(ts, ps)]
    return ts


@jax.custom_vjp
def _tri_solves(mats, rhs):
    return [_dot3(t, b) for t, b in zip(_tri_inverses(mats), rhs)]


def _tri_solves_fwd(mats, rhs):
    ts = _tri_inverses(mats)
    xs = [_dot3(t, b) for t, b in zip(ts, rhs)]
    return xs, (ts, xs)


def _tri_solves_bwd(res, dxs):
    ts, xs = res
    dbs = [_dot3(t, dx, _DIMS["tn"]) for t, dx in zip(ts, dxs)]
    return [-_dot3(db, x, _DIMS["nt"]) for db, x in zip(dbs, xs)], dbs


_tri_solves.defvjp(_tri_solves_fwd, _tri_solves_bwd)


def _chunk_prep_fn(xs, bgs):
    row = lax.broadcasted_iota(jnp.int32, (CH, CH), 0)
    col = lax.broadcasted_iota(jnp.int32, (CH, CH), 1)
    incl = row >= col
    strict = row > col
    lmat = incl.astype(F32)
    n = len(xs)
    items = [(i, h) for i in range(n) for h in range(HEADS)]
    part = lambda i, h, c: xs[i][:, c * HW + h * HD:c * HW + (h + 1) * HD]
    q = [part(i, h, 0) for i, h in items]
    k = [part(i, h, 1) for i, h in items]
    v = [part(i, h, 2) for i, h in items]
    beta = [bgs[i][:, h:h + 1] for i, h in items]
    gc_all = [_dot32(lmat, bg) for bg in bgs]
    gc = [gc_all[i][:, HEADS + h:HEADS + h + 1] for i, h in items]
    gmat = [jnp.where(strict, jnp.broadcast_to(bgs[i][:, HEADS + h:HEADS + h + 1], (CH, CH)), 0.0) for i, h in items]
    diff = [_dot3(lmat, m) for m in gmat]
    decay = [jnp.where(incl, jnp.exp(jnp.where(incl, d, 0.0)), 0.0) for d in diff]
    k_beta = [kk * b for kk, b in zip(k, beta)]
    kk_t = [_dot16(kb, kk, _DIMS["nt"]) for kb, kk in zip(k_beta, k)]
    qk_t = [_dot16(qq, kk, _DIMS["nt"]) for qq, kk in zip(q, k)]
    a = [jnp.where(strict, m * d, 0.0) for m, d in zip(kk_t, decay)]
    eg = [jnp.exp(g) for g in gc]
    rhs = [jnp.concatenate([vv * b, kb * e], axis=-1) for vv, b, kb, e in zip(v, beta, k_beta, eg)]
    uw = _tri_solves(a, rhs)
    qk = [m * d for m, d in zip(qk_t, decay)]
    g_last = [g[CH - 1:CH, :] for g in gc]
    qe = [qq * e for qq, e in zip(q, eg)]
    kd = [kk * jnp.exp(gl - g) for kk, gl, g in zip(k, g_last, gc)]
    egl = [jnp.broadcast_to(jnp.exp(gl), (1, HD)) for gl in g_last]
    out = []
    for i in range(n):
        mine = slice(i * HEADS, (i + 1) * HEADS)
        cat = lambda vals: jnp.concatenate(vals[mine], axis=-1)
        out.append((cat([x[:, :HD] for x in uw]), cat([x[:, HD:] for x in uw]), cat(qe), cat(kd),
                    jnp.concatenate([m[None] for m in qk[mine]], axis=0), cat(egl)))
    return out


def _chunk_state_fn(u, w, qe, kd, qk, egl, s):
    ws = [_dot16(a, b) for a, b in zip(w, s)]
    qs = [_dot16(a, b) for a, b in zip(qe, s)]
    v_new = [a - b for a, b in zip(u, ws)]
    o = [a + _dot16(b, c) for a, b, c in zip(qs, qk, v_new)]
    s_new = [a * e + _dot16(b, c, _DIMS["tn"]) for a, e, b, c in zip(s, egl, kd, v_new)]
    return o, s_new


def _mix_fn(o, z, ur, vr, ong, lng, lnb, ws, bst):
    row = lax.broadcasted_iota(jnp.int32, (GCH, GCH), 0)
    col = lax.broadcasted_iota(jnp.int32, (GCH, GCH), 1)
    causal = row >= col
    ug = _gelu(ur)
    vg = _gelu(vr)
    outs_dn, outs_gm = [], []
    for h in range(HEADS):
        sl = slice(h * HD, (h + 1) * HD)
        oh = o[:, sl]
        oh = oh * lax.rsqrt(jnp.mean(oh * oh, axis=-1, keepdims=True) + EPS)
        outs_dn.append(oh * ong * _silu(z[:, sl]))
        vh = vg[:, sl]
        mu = jnp.mean(vh, axis=-1, keepdims=True)
        var = jnp.mean(jnp.square(vh - mu), axis=-1, keepdims=True)
        vn = (vh - mu) * lax.rsqrt(var + EPS) * lng[:, sl] + lnb[:, sl]
        sp = _dot16(jnp.where(causal, ws[h], 0.0), vn) + bst[:, h:h + 1]
        outs_gm.append(ug[:, sl] * sp)
    return jnp.concatenate(outs_dn + outs_gm, axis=-1)


def _loss_fn(h, gain, tgt):
    y = _rms_fn(h, gain)
    return 0.5 * jnp.sum(jnp.mean(jnp.square(y - tgt), axis=-1))


RT = 256


def _rows(n=D):
    return pl.BlockSpec((RT, n), lambda i: (i, 0))


def _whole(shape):
    nd = len(shape)
    return pl.BlockSpec(shape, lambda i: (0,) * nd)


def _rmsnorm(name, h, gain):
    def body(h_ref, g_ref, o_ref):
        o_ref[...] = _rms_fn(h_ref[...], g_ref[...]).astype(BF16)

    return pl.pallas_call(
        body, name=name, grid=(T // RT,), in_specs=[_rows(), _whole((1, D))], out_specs=_rows(),
        out_shape=jax.ShapeDtypeStruct((T, D), BF16), compiler_params=_cparams(("parallel",)),
    )(h, gain)


def _rmsnorm_bwd(name, dhn, h, gain, resid):
    def body(dhn_ref, h_ref, g_ref, r_ref, dh_ref, dh16_ref, dg_ref):
        _, vjp = jax.vjp(_rms_fn, h_ref[...], g_ref[...])
        dh, dg = vjp(dhn_ref[...])
        dh = r_ref[...] + dh
        dh_ref[...] = dh
        dh16_ref[...] = dh.astype(BF16)

        @pl.when(pl.program_id(0) == 0)
        def _():
            dg_ref[...] = dg

        @pl.when(pl.program_id(0) > 0)
        def _():
            dg_ref[...] += dg

    return pl.pallas_call(
        body, name=name, grid=(T // RT,), in_specs=[_rows(), _rows(), _whole((1, D)), _rows()],
        out_specs=[_rows(), _rows(), _whole((1, D))],
        out_shape=[jax.ShapeDtypeStruct((T, D), F32), jax.ShapeDtypeStruct((T, D), BF16),
                   jax.ShapeDtypeStruct((1, D), F32)],
        compiler_params=_cparams(("arbitrary",)),
    )(dhn, h, gain, resid)


def _loss_head(h, gain, tgt):
    def body(h_ref, g_ref, t_ref, l_ref, dh_ref, dh16_ref, dg_ref):
        loss, vjp = jax.vjp(lambda hh, gg: _loss_fn(hh, gg, t_ref[...]), h_ref[...], g_ref[...])
        dh, dg = vjp(jnp.ones((), F32))
        dh_ref[...] = dh
        dh16_ref[...] = dh.astype(BF16)
        lv = jnp.full((1, LANE), loss, F32)

        @pl.when(pl.program_id(0) == 0)
        def _():
            dg_ref[...] = dg
            l_ref[...] = lv

        @pl.when(pl.program_id(0) > 0)
        def _():
            dg_ref[...] += dg
            l_ref[...] += lv

    return pl.pallas_call(
        body, name="loss_head", grid=(T // RT,), in_specs=[_rows(), _whole((1, D)), _rows()],
        out_specs=[_whole((1, LANE)), _rows(), _rows(), _whole((1, D))],
        out_shape=[jax.ShapeDtypeStruct((1, LANE), F32), jax.ShapeDtypeStruct((T, D), F32),
                   jax.ShapeDtypeStruct((T, D), BF16), jax.ShapeDtypeStruct((1, D), F32)],
        compiler_params=_cparams(("arbitrary",)),
    )(h, gain, tgt)


def _prep_flags():
    j = pl.program_id(0)
    qk_scale = jnp.where(j < HEADS, HD ** -0.5, 1.0).astype(F32)
    return qk_scale, j >= 2 * HEADS


def _prep(proj, conv_w):
    def body(x_ref, w_ref, o_ref):
        qk_scale, is_v = _prep_flags()
        o_ref[...] = _prep_fn(x_ref[...], w_ref[...], qk_scale, is_v)

    col = lambda j: (0, j)
    return pl.pallas_call(
        body, name="gdn_prep", grid=(3 * HEADS,),
        in_specs=[pl.BlockSpec((T, HD), col), pl.BlockSpec((4, HD), col)], out_specs=pl.BlockSpec((T, HD), col),
        out_shape=jax.ShapeDtypeStruct((T, 3 * HW), F32), compiler_params=_cparams(("parallel",)),
    )(proj, conv_w)


def _prep_bwd(proj, conv_w, dqkv, dproj):
    def body(x_ref, w_ref, d_ref, _, dx_ref, dw_ref):
        qk_scale, is_v = _prep_flags()
        _, vjp = jax.vjp(lambda x, w: _prep_fn(x, w, qk_scale, is_v), x_ref[...], w_ref[...])
        dx, dw = vjp(d_ref[...])
        dx_ref[...] = dx.astype(BF16)
        dw_ref[...] = dw

    col = lambda j: (0, j)
    return pl.pallas_call(
        body, name="gdn_prep_bwd", grid=(3 * HEADS,),
        in_specs=[pl.BlockSpec((T, HD), col), pl.BlockSpec((4, HD), col), pl.BlockSpec((T, HD), col), ANY],
        out_specs=[pl.BlockSpec((T, HD), col), pl.BlockSpec((4, HD), col)],
        out_shape=[jax.ShapeDtypeStruct((T, NP), BF16), jax.ShapeDtypeStruct((4, 3 * HW), F32)],
        input_output_aliases={3: 0}, compiler_params=_cparams(("parallel",)),
    )(proj, conv_w, dqkv, dproj)


BA_BLK = BA_OFF // LANE


def _gates(proj, a_log, dt_bias):
    def body(x_ref, a_ref, d_ref, o_ref):
        o_ref[...] = _gates_fn(x_ref[...], a_ref[...], d_ref[...])

    return pl.pallas_call(
        body, name="gdn_gates", grid=(1,),
        in_specs=[pl.BlockSpec((T, LANE), lambda i: (0, BA_BLK)), _whole((1, LANE)), _whole((1, LANE))],
        out_specs=_whole((T, LANE)),
        out_shape=jax.ShapeDtypeStruct((T, LANE), F32), compiler_params=_cparams(("arbitrary",)),
    )(proj, a_log, dt_bias)


def _gates_bwd(proj, a_log, dt_bias, dbg, dproj):
    def body(x_ref, a_ref, d_ref, dbg_ref, _, dx_ref, da_ref, dd_ref):
        _, vjp = jax.vjp(_gates_fn, x_ref[...], a_ref[...], d_ref[...])
        dx, da_ref[...], dd_ref[...] = vjp(dbg_ref[...])
        dx_ref[...] = dx.astype(BF16)

    ba = pl.BlockSpec((T, LANE), lambda i: (0, BA_BLK))
    return pl.pallas_call(
        body, name="gdn_gates_bwd", grid=(1,),
        in_specs=[ba, _whole((1, LANE)), _whole((1, LANE)), _whole((T, LANE)), ANY],
        out_specs=[ba, _whole((1, LANE)), _whole((1, LANE))],
        out_shape=[jax.ShapeDtypeStruct((T, NP), BF16), jax.ShapeDtypeStruct((1, LANE), F32),
                   jax.ShapeDtypeStruct((1, LANE), F32)],
        input_output_aliases={4: 0}, compiler_params=_cparams(("arbitrary",)),
    )(proj, a_log, dt_bias, dbg, dproj)


NCK = T // CH
CPS = 2


def _chunk_prep_specs(rev=False):
    at = (lambda n: NCK - 1 - n) if rev else (lambda n: n)
    wide = pl.BlockSpec((CH, HW), lambda n: (at(n), 0))
    return [wide, wide, wide, wide, pl.BlockSpec((HEADS, CH, CH), lambda n: (0, at(n), 0)),
            pl.BlockSpec((None, 1, HW), lambda n: (at(n), 0, 0))]


def _chunk_prep_shapes(dtypes):
    shp = [(T, HW), (T, HW), (T, HW), (T, HW), (HEADS, T, CH), (NCK, 1, HW)]
    return [jax.ShapeDtypeStruct(s, dt) for s, dt in zip(shp, dtypes)]


def _chunk_prep(qkv, bg):
    def body(x_ref, bg_ref, *o_refs):
        rows = [slice(ci * CH, (ci + 1) * CH) for ci in range(CPS)]
        res = _chunk_prep_fn([x_ref[r, :] for r in rows], [bg_ref[r, :] for r in rows])
        for ci, (u, w, qe, kd, qk, egl) in enumerate(res):
            for o_ref, val in zip(o_refs[:4], (u, w, qe, kd)):
                o_ref[rows[ci], :] = val.astype(o_ref.dtype)
            o_refs[4][:, rows[ci], :] = qk.astype(BF16)
            o_refs[5][ci] = egl

    wide = pl.BlockSpec((CPS * CH, HW), lambda n: (n, 0))
    return pl.pallas_call(
        body, name="gdn_chunk_prep", grid=(NCK // CPS,),
        in_specs=[pl.BlockSpec((CPS * CH, 3 * HW), lambda n: (n, 0)), pl.BlockSpec((CPS * CH, LANE), lambda n: (n, 0))],
        out_specs=[wide, wide, wide, wide, pl.BlockSpec((HEADS, CPS * CH, CH), lambda n: (0, n, 0)),
                   pl.BlockSpec((CPS, 1, HW), lambda n: (n, 0, 0))],
        out_shape=_chunk_prep_shapes((F32, BF16, BF16, BF16, BF16, F32)),
        compiler_params=_cparams(("parallel",)),
    )(qkv, bg)


def _chunk_prep_bwd(qkv, bg, cots):
    def body(x_ref, bg_ref, du, dw, dqe, dkd, dqk, degl, dx_ref, dbg_ref):
        rows = [slice(ci * CH, (ci + 1) * CH) for ci in range(CPS)]
        _, vjp = jax.vjp(_chunk_prep_fn, [x_ref[r, :] for r in rows], [bg_ref[r, :] for r in rows])
        dxs, dbgs = vjp([(du[r, :], dw[r, :], dqe[r, :], dkd[r, :], dqk[:, r, :], degl[ci])
                         for ci, r in enumerate(rows)])
        for r, dx, dbg in zip(rows, dxs, dbgs):
            dx_ref[r, :] = dx
            dbg_ref[r, :] = dbg

    wide = pl.BlockSpec((CPS * CH, HW), lambda n: (n, 0))
    return pl.pallas_call(
        body, name="gdn_chunk_prep_bwd", grid=(NCK // CPS,),
        in_specs=[pl.BlockSpec((CPS * CH, 3 * HW), lambda n: (n, 0)), pl.BlockSpec((CPS * CH, LANE), lambda n: (n, 0)),
                  wide, wide, wide, wide, pl.BlockSpec((HEADS, CPS * CH, CH), lambda n: (0, n, 0)),
                  pl.BlockSpec((CPS, 1, HW), lambda n: (n, 0, 0))],
        out_specs=[pl.BlockSpec((CPS * CH, 3 * HW), lambda n: (n, 0)), pl.BlockSpec((CPS * CH, LANE), lambda n: (n, 0))],
        out_shape=[jax.ShapeDtypeStruct((T, 3 * HW), F32), jax.ShapeDtypeStruct((T, LANE), F32)],
        compiler_params=_cparams(("parallel",)),
    )(qkv, bg, *cots)


def _head_args(refs):
    u, w, qe, kd, qk, egl = refs
    sls = [slice(h * HD, (h + 1) * HD) for h in range(HEADS)]
    return ([u[:, sl] for sl in sls], [w[:, sl].astype(F32) for sl in sls], [qe[:, sl].astype(F32) for sl in sls],
            [kd[:, sl].astype(F32) for sl in sls], [qk[h].astype(F32) for h in range(HEADS)],
            [egl[:, sl] for sl in sls])


def _chunk_scan(prep):
    def body(*refs):
        o_ref, sh_ref, s_ref = refs[6:]

        @pl.when(pl.program_id(0) == 0)
        def _():
            s_ref[...] = jnp.zeros_like(s_ref)

        s = [s_ref[h] for h in range(HEADS)]
        for h in range(HEADS):
            sh_ref[h, 0] = s[h]
        o, s_new = _chunk_state_fn(*_head_args(refs[:6]), s)
        for h in range(HEADS):
            o_ref[:, h * HD:(h + 1) * HD] = o[h]
            s_ref[h] = s_new[h]

    return pl.pallas_call(
        body, name="gdn_scan", grid=(NCK,), in_specs=_chunk_prep_specs(),
        out_specs=[pl.BlockSpec((CH, HW), lambda n: (n, 0)), pl.BlockSpec((HEADS, 1, HD, HD), lambda n: (0, n, 0, 0))],
        out_shape=[jax.ShapeDtypeStruct((T, HW), F32), jax.ShapeDtypeStruct((HEADS, NCK, HD, HD), F32)],
        scratch_shapes=[pltpu.VMEM((HEADS, HD, HD), F32)], compiler_params=_cparams(("arbitrary",)),
    )(*prep)


def _chunk_scan_bwd(prep, s_hist, do):
    def body(*refs):
        sh_ref, do_ref = refs[6:8]
        d_refs = refs[8:14]
        ds_ref = refs[14]

        @pl.when(pl.program_id(0) == 0)
        def _():
            ds_ref[...] = jnp.zeros_like(ds_ref)

        sls = [slice(h * HD, (h + 1) * HD) for h in range(HEADS)]
        _, vjp = jax.vjp(_chunk_state_fn, *_head_args(refs[:6]), [sh_ref[h, 0] for h in range(HEADS)])
        du, dw, dqe, dkd, dqk, degl, ds = vjp(([do_ref[:, sl] for sl in sls], [ds_ref[h] for h in range(HEADS)]))
        for h, sl in enumerate(sls):
            for d_ref, val in zip(d_refs[:4], (du, dw, dqe, dkd)):
                d_ref[:, sl] = val[h]
            d_refs[4][h] = dqk[h]
            d_refs[5][:, sl] = degl[h]
            ds_ref[h] = ds[h]

    rev = lambda n: NCK - 1 - n
    return pl.pallas_call(
        body, name="gdn_scan_bwd", grid=(NCK,),
        in_specs=_chunk_prep_specs(rev=True) + [pl.BlockSpec((HEADS, 1, HD, HD), lambda n: (0, rev(n), 0, 0)),
                                                pl.BlockSpec((CH, HW), lambda n: (rev(n), 0))],
        out_specs=_chunk_prep_specs(rev=True), out_shape=_chunk_prep_shapes((F32,) * 6),
        scratch_shapes=[pltpu.VMEM((HEADS, HD, HD), F32)], compiler_params=_cparams(("arbitrary",)),
    )(*prep, s_hist, do)


def _mix_specs():
    pc = lambda c: pl.BlockSpec((GCH, HW), lambda i: (i, c))
    return [pl.BlockSpec((GCH, HW), lambda i: (i, 0)), pc(3), pc(4), pc(5), _whole((1, HD)), _whole((1, HW)),
            _whole((1, HW)), _whole((HEADS, GCH, GCH)), _whole((GCH, LANE))]


def _mix(o, proj, ong, lng, lnb, ws, bst):
    def body(o_ref, z_ref, u_ref, v_ref, ong_ref, lng_ref, lnb_ref, ws_ref, bs_ref, m_ref):
        m_ref[...] = _mix_fn(o_ref[...], z_ref[...], u_ref[...], v_ref[...], ong_ref[...], lng_ref[...],
                             lnb_ref[...], ws_ref[...], bs_ref[...]).astype(BF16)

    return pl.pallas_call(
        body, name="mix", grid=(T // GCH,), in_specs=_mix_specs(),
        out_specs=pl.BlockSpec((GCH, D), lambda i: (i, 0)), out_shape=jax.ShapeDtypeStruct((T, D), BF16),
        compiler_params=_cparams(("parallel",)),
    )(o, proj, proj, proj, ong, lng, lnb, ws, bst)


def _mix_bwd(o, proj, ong, lng, lnb, ws, bst, dmix):
    def body(o_ref, z_ref, u_ref, v_ref, ong_ref, lng_ref, lnb_ref, ws_ref, bs_ref, dm_ref,
             do_ref, dzuv_ref, dong_ref, dlng_ref, dlnb_ref, dws_ref, dbs_ref):
        _, vjp = jax.vjp(_mix_fn, o_ref[...], z_ref[...], u_ref[...], v_ref[...], ong_ref[...], lng_ref[...],
                         lnb_ref[...], ws_ref[...], bs_ref[...])
        do, dz, du, dv, dong, dlng, dlnb, dws, dbs = vjp(dm_ref[...])
        do_ref[...] = do
        dzuv_ref[:, 0:HW] = dz.astype(BF16)
        dzuv_ref[:, HW:2 * HW] = du.astype(BF16)
        dzuv_ref[:, 2 * HW:3 * HW] = dv.astype(BF16)
        acc = [(dong_ref, dong), (dlng_ref, dlng), (dlnb_ref, dlnb), (dws_ref, dws), (dbs_ref, dbs)]

        @pl.when(pl.program_id(0) == 0)
        def _():
            for r, val in acc:
                r[...] = val

        @pl.when(pl.program_id(0) > 0)
        def _():
            for r, val in acc:
                r[...] += val

    shp = lambda *s: jax.ShapeDtypeStruct(s, F32)
    return pl.pallas_call(
        body, name="mix_bwd", grid=(T // GCH,),
        in_specs=_mix_specs() + [pl.BlockSpec((GCH, D), lambda i: (i, 0))],
        out_specs=[pl.BlockSpec((GCH, HW), lambda i: (i, 0)), pl.BlockSpec((GCH, 3 * HW), lambda i: (i, 1)),
                   _whole((1, HD)), _whole((1, HW)), _whole((1, HW)), _whole((HEADS, GCH, GCH)), _whole((GCH, LANE))],
        out_shape=[shp(T, HW), jax.ShapeDtypeStruct((T, NP), BF16), shp(1, HD), shp(1, HW), shp(1, HW),
                   shp(HEADS, GCH, GCH), shp(GCH, LANE)],
        compiler_params=_cparams(("arbitrary",)),
    )(o, proj, proj, proj, ong, lng, lnb, ws, bst, dmix)


def _swiglu_epilogue(accs, _):
    gate, up = accs
    return [gate, up, _silu(gate) * up]


def _swiglu_bwd_epilogue(accs, extras):
    dact = accs[0]
    gate, up = (e.astype(F32) for e in extras)
    sg = _sigmoid(gate)
    return [dact * up * (sg * (1.0 + gate * (1.0 - sg))), dact * (gate * sg)]


def _layer_fwd(h, p):
    hn = _rmsnorm("rms_mix", h, p["norm_mix"])
    proj = _mm("in_proj", "nn", hn[None], [p["w_in"][None]], tm=1024, tn=640, tk=D)[0][0]
    qkv = _prep(proj, p["conv_w"])
    bg = _gates(proj, p["a_log"], p["dt_bias"])
    prep = _chunk_prep(qkv, bg)
    o, s_hist = _chunk_scan(prep)
    mix = _mix(o, proj, p["o_norm_g"], p["ln_v_g"], p["ln_v_b"], p["w_s"], p["bst"])
    l = p["layer"]
    of_layer = dict(b_g=lambda g: DEPTH * g + l, g_n=NCHIP)
    h1 = _mm("out_proj", "nn", mix[None], [p["w_out"]], tm=1024, tn=512, tk=D // NCHIP, resid=h[None], n_n=D,
             b_spec=((None, D // NCHIP, 512), lambda g, i, j, k: (DEPTH * k + l, 0, j)))[0][0]
    h2n = _rmsnorm("rms_ffn", h1, p["norm_ffn"])
    gate, up, act = _mm("ffn_in", "nn", h2n[None], [p["w_gate"], p["w_up"]], tm=1024, tn=FF_SH, tk=D,
                        out_dtypes=(BF16, BF16, BF16), epilogue=_swiglu_epilogue, **of_layer)
    h2 = _mm("ffn_out", "nn", act, [p["w_down"]], tm=1024, tn=512, tk=FF_SH, reduce_g=True, resid=h1[None],
             **of_layer)[0][0]
    saved = dict(h=h, hn=hn, proj=proj, qkv=qkv, bg=bg, prep=prep, o=o, s_hist=s_hist, mix=mix, h1=h1, h2n=h2n,
                 gate=gate, up=up, act=act)
    return h2, saved


def _layer_bwd_ffn(dh2, dh2b, p, s, after=()):
    l = p["layer"]
    of_layer = dict(b_g=lambda g: DEPTH * g + l, g_n=NCHIP)
    dh2b = dh2b[None]
    dgate, dup = _mm("ffn_out_bwd", "nt", dh2b, [p["w_down"]], tm=1024, tn=FF_SH, tk=D, out_dtypes=(BF16, BF16),
                     extras=(s["gate"], s["up"]), epilogue=_swiglu_bwd_epilogue, after=after, **of_layer)
    dh2n = _mm("ffn_gate_bwd", "nt", dgate, [p["w_gate"]], tm=1024, tn=512, tk=FF_SH, reduce_g=True, **of_layer)[0]
    dh2n = _mm("ffn_up_bwd", "nt", dup, [p["w_up"]], tm=1024, tn=512, tk=FF_SH, reduce_g=True, resid=dh2n,
               **of_layer)[0][0]
    dh1, dh1b, d_norm_ffn = _rmsnorm_bwd("rms_ffn_bwd", dh2n, s["h1"], p["norm_ffn"], dh2)
    d_w_down = _mm("ffn_wdown_grad", "tn", s["act"], [dh2b], tm=FF_SH, tn=512, tk=1024)[0]
    d_w_gate = _mm("ffn_wgate_grad", "tn", s["h2n"][None], [dgate], tm=512, tn=FF_SH, tk=1024)[0]
    d_w_up = _mm("ffn_wup_grad", "tn", s["h2n"][None], [dup], tm=512, tn=FF_SH, tk=1024)[0]
    return dh1, dh1b, dict(norm_ffn=d_norm_ffn, w_gate=d_w_gate, w_up=d_w_up, w_down=d_w_down)


def _layer_bwd_mixer(dh1, dh1b, p, s, after=()):
    l = p["layer"]
    dh1b = dh1b[None]
    dmix = _mm("out_proj_bwd", "nt", dh1b, [p["w_out"]], tm=1024, tn=D // NCHIP, tk=D, n_n=D, after=after,
               b_spec=((None, D // NCHIP, D), lambda g, i, j, k: (DEPTH * j + l, 0, k)))[0][0]
    d_w_out = _mm("out_proj_wgrad", "tn", s["mix"][None], [dh1b], tm=512, tn=512, tk=1024)[0][0]
    do, dproj, d_ong, d_lng, d_lnb, d_ws, d_bst = _mix_bwd(
        s["o"], s["proj"], p["o_norm_g"], p["ln_v_g"], p["ln_v_b"], p["w_s"], p["bst"], dmix)
    dqkv, dbg = _chunk_prep_bwd(s["qkv"], s["bg"], _chunk_scan_bwd(s["prep"], s["s_hist"], do))
    dproj, d_conv = _prep_bwd(s["proj"], p["conv_w"], dqkv, dproj)
    dproj, d_a_log, d_dt_bias = _gates_bwd(s["proj"], p["a_log"], p["dt_bias"], dbg, dproj)
    dproj = dproj[None]
    dhn = _mm("in_proj_bwd", "nt", dproj, [p["w_in"][None]], tm=1024, tn=512, tk=640)[0][0]
    dh, dhb, d_norm_mix = _rmsnorm_bwd("rms_mix_bwd", dhn, s["h"], p["norm_mix"], dh1)
    d_w_in = _mm("in_proj_wgrad", "tn", s["hn"][None], [dproj], tm=512, tn=640, tk=1024)[0]
    grads = dict(norm_mix=d_norm_mix, w_in=d_w_in, conv_w=d_conv, a_log=d_a_log, dt_bias=d_dt_bias, o_norm_g=d_ong,
                 ln_v_g=d_lng, ln_v_b=d_lnb, w_s=d_ws, bst=d_bst, w_out=d_w_out)
    return dh, dhb, grads


def _lanes(v, off=0):
    return jnp.zeros((1, LANE), F32).at[0, off:off + v.shape[0]].set(v)


def _w_in_pieces():
    regions = [(0, 2048, 0), (2048, 2056, BA_OFF), (2056, IN_DIM, 2048)]
    sh = IN_DIM // NCHIP
    out = []
    for j in range(NCHIP):
        for lo, hi, at in regions:
            a, b = max(lo, j * sh), min(hi, (j + 1) * sh)
            if a < b:
                out.append((j, a - j * sh, at + a - lo, b - a))
    return out


W_IN_PIECES = _w_in_pieces()
WT = 256


def _assemble_w_in(gathered, own, l, place):
    def body(place_ref, g_ref, own_ref, o_ref):
        o_ref[:, IN_DIM:] = jnp.zeros((WT, NP - IN_DIM), BF16)
        mine = own_ref[...]
        for j, src, dst, width in W_IN_PIECES:
            val = jnp.where(place_ref[0] == j, mine[:, src:src + width], g_ref[j, :, src:src + width])
            o_ref[:, dst:dst + width] = val

    sh = IN_DIM // NCHIP
    return pl.pallas_call(
        body, name="assemble_w_in",
        grid_spec=pltpu.PrefetchScalarGridSpec(
            num_scalar_prefetch=1, grid=(D // WT,),
            in_specs=[pl.BlockSpec((NCHIP, None, WT, sh), lambda i, place_ref: (0, l, i, 0)),
                      pl.BlockSpec((None, WT, sh), lambda i, place_ref: (l, i, 0))],
            out_specs=pl.BlockSpec((WT, NP), lambda i, place_ref: (i, 0))),
        out_shape=jax.ShapeDtypeStruct((D, NP), BF16), compiler_params=_cparams(("parallel",)),
    )(place, gathered, own)


def _layer_params(l, big, small):
    return dict(
        layer=l, w_in=big["w_in"][l], w_out=big["w_out"], w_gate=big["w_gate"], w_up=big["w_up"], w_down=big["w_down"],
        conv_w=jnp.concatenate([big["conv_w"][j, l] for j in range(NCHIP)], axis=1),
        norm_mix=small["norm_mix"][l][None], norm_ffn=small["norm_ffn"][l][None],
        a_log=_lanes(small["a_log"][l], HEADS), dt_bias=_lanes(small["dt_bias"][l], HEADS),
        o_norm_g=small["o_norm_g"][l][None], ln_v_g=small["ln_v_g"][l][None], ln_v_b=small["ln_v_b"][l][None],
        w_s=small["w_s"][l],
        bst=jnp.pad(small["b_s"][l].T, ((0, 0), (0, LANE - HEADS))),
    )


def _reference_layout(g):
    return dict(
        w_in=g["w_in"],
        w_out=g["w_out"].reshape(NCHIP, D // NCHIP, D),
        w_gate=g["w_gate"], w_up=g["w_up"], w_down=g["w_down"],
        conv_w=g["conv_w"], norm_mix=g["norm_mix"][0], norm_ffn=g["norm_ffn"][0],
        a_log=g["a_log"][0, HEADS:2 * HEADS], dt_bias=g["dt_bias"][0, HEADS:2 * HEADS],
        o_norm_g=g["o_norm_g"][0], ln_v_g=g["ln_v_g"][0], ln_v_b=g["ln_v_b"][0], w_s=g["w_s"],
        b_s=g["bst"][:, :HEADS].T,
    )


def _forward(x, tgt, layers, norm_final):
    h = x
    saved = []
    for p in layers:
        h, s = _layer_fwd(h, p)
        saved.append(s)
    return (saved,) + tuple(_loss_head(h, norm_final, tgt))


def _local_step(x, tgt, layers, norm_final):
    saved, loss, dh, dhb, d_norm_final = _forward(x, tgt, layers, norm_final)
    grads = [None] * DEPTH
    for l in reversed(range(DEPTH)):
        dh1, dh1b, g_ffn = _layer_bwd_ffn(dh, dhb, layers[l], saved[l])
        dh, dhb, g_mix = _layer_bwd_mixer(dh1, dh1b, layers[l], saved[l])
        grads[l] = {**g_ffn, **g_mix}
    return loss, dh, grads, d_norm_final


def _place():
    x, y, c = lax.axis_index("x"), lax.axis_index("y"), lax.axis_index("c")
    return x, y, c, [(1 - x, y), (x, 1 - y), (1 - x, 1 - y)]


def _remote(src, dst, send_sem, recv_sem, to):
    return pltpu.make_async_remote_copy(src_ref=src, dst_ref=dst, send_sem=send_sem, recv_sem=recv_sem,
                                        device_id=to, device_id_type=MESH)


def _comm_call(name, body, ins, out_shape, n_sems, aliases=None):
    return pl.pallas_call(
        body, name=name, in_specs=[ANY] * len(ins), out_specs=[ANY] * len(out_shape), out_shape=out_shape,
        scratch_shapes=[pltpu.SemaphoreType.DMA((n,)) for n in n_sems], input_output_aliases=aliases or {},
        compiler_params=pltpu.CompilerParams(has_side_effects=True),
    )(*ins)


def _gather_weights(shards, conv):
    n = len(shards)

    def body(*refs):
        ins, conv_in = refs[:n], refs[n]
        outs, conv_out = refs[n + 1:2 * n + 1], refs[2 * n + 1]
        ici_s, ici_r, d2d_s, d2d_r = refs[2 * n + 2:]
        x, y, c, others = _place()
        chip = 2 * x + y
        sibling = (x, y, 1 - c)

        def half(a, of_c):
            hr = ins[a].shape[1] // 2
            return pl.ds(pl.multiple_of(of_c * hr, 16), hr)

        sends = []
        for a in range(n):
            for k, (ox, oy) in enumerate(others):
                sends.append(_remote(ins[a].at[:, half(a, c)], outs[a].at[chip, :, half(a, c)],
                                     ici_s.at[3 * a + k], ici_r.at[3 * a + k], (ox, oy, c)))
        for k, (ox, oy) in enumerate(others):
            sends.append(_remote(conv_in, conv_out.at[chip], ici_s.at[3 * n + k], ici_r.at[3 * n + k], (ox, oy, c)))
        for cp in sends:
            cp.start()
        passed = []
        for a in range(n):
            for k, (ox, oy) in enumerate(others):
                landed = outs[a].at[2 * ox + oy, :, half(a, c)]
                _remote(landed, landed, ici_s.at[3 * a + k], ici_r.at[3 * a + k], (ox, oy, c)).wait_recv()
                cp = _remote(landed, landed, d2d_s.at[3 * a + k], d2d_r.at[3 * a + k], sibling)
                cp.start()
                passed.append(cp)
        for k, (ox, oy) in enumerate(others):
            landed = conv_out.at[2 * ox + oy]
            _remote(landed, landed, ici_s.at[3 * n + k], ici_r.at[3 * n + k], (ox, oy, c)).wait_recv()
        for a in range(n):
            for k, (ox, oy) in enumerate(others):
                landed = outs[a].at[2 * ox + oy, :, half(a, 1 - c)]
                _remote(landed, landed, d2d_s.at[3 * a + k], d2d_r.at[3 * a + k], sibling).wait_recv()
        for cp in sends + passed:
            cp.wait_send()

    out_shape = [jax.ShapeDtypeStruct((NCHIP,) + s.shape, s.dtype) for s in list(shards) + [conv]]
    return _comm_call("gather_weights", body, list(shards) + [conv], out_shape, [3 * n + 3, 3 * n + 3, 3 * n, 3 * n])


def _exchange_halves(gs):
    n = len(gs)

    def body(*refs):
        ins, outs = refs[:n], refs[n:2 * n]
        send_s, recv_s = refs[2 * n:]
        x, y, c, _ = _place()
        copies = []
        for a in range(n):
            hr = ins[a].shape[1] // 2
            theirs = ins[a].at[:, pl.ds(pl.multiple_of((1 - c) * hr, 8), hr)]
            copies.append(_remote(theirs, outs[a], send_s.at[a], recv_s.at[a], (x, y, 1 - c)))
        for cp in copies:
            cp.start()
        for cp in copies:
            cp.wait()

    out_shape = [jax.ShapeDtypeStruct((g.shape[0], g.shape[1] // 2, g.shape[2]), F32) for g in gs]
    return _comm_call("exchange_halves", body, gs, out_shape, [n, n])


def _scatter_partials(ps):
    n = len(ps)

    def body(*refs):
        ins, outs = refs[:n], refs[n:2 * n]
        send_s, recv_s = refs[2 * n:]
        x, y, c, others = _place()
        copies = []
        for a in range(n):
            for k, (ox, oy) in enumerate(others):
                copies.append(_remote(ins[a].at[2 * ox + oy], outs[a].at[k], send_s.at[3 * a + k],
                                      recv_s.at[3 * a + k], (ox, oy, c)))
        for cp in copies:
            cp.start()
        for cp in copies:
            cp.wait()

    out_shape = [jax.ShapeDtypeStruct((3,) + p.shape[1:], p.dtype) for p in ps]
    return _comm_call("scatter_partials", body, ps, out_shape, [3 * n, 3 * n])


HBM_SPEC = pl.BlockSpec(memory_space=pltpu.HBM)
SEM_SPEC = pl.BlockSpec(memory_space=pltpu.SEMAPHORE)
DATAFLOW = pltpu.SideEffectType.DATAFLOW_SIDE_EFFECTING


def _exchange_plan(srcs, lands):
    x, y, c, _ = _place()
    plan = []
    for src, land in zip(srcs, lands):
        hr = src.shape[1] // 2
        plan.append((src.at[:, pl.ds(pl.multiple_of((1 - c) * hr, 8), hr)], land, (x, y, 1 - c)))
    return plan


def _scatter_plan(srcs, lands):
    x, y, c, others = _place()
    return [(src.at[2 * ox + oy], land.at[k], (ox, oy, c))
            for src, land in zip(srcs, lands) for k, (ox, oy) in enumerate(others)]


def _split_start(name, plan, srcs, land_shapes, n_copies):
    n = len(srcs)
    lands = [pltpu.with_memory_space_constraint(lax.empty(s.shape, s.dtype), pltpu.HBM) for s in land_shapes]
    srcs = [pltpu.with_memory_space_constraint(s, pltpu.HBM) for s in srcs]

    def body(*refs):
        send_s, recv_s = refs[2 * n], refs[2 * n + 1]
        for i, (src, dst, to) in enumerate(plan(refs[:n], refs[n:2 * n])):
            _remote(src, dst, send_s.at[i], recv_s.at[i], to).start()
        refs[-1][...] = jnp.zeros_like(refs[-1])

    thru = [pltpu.HBM(s.shape, s.dtype) for s in srcs + lands]
    out = pl.pallas_call(
        body, name=name, in_specs=[HBM_SPEC] * (2 * n),
        out_specs=[SEM_SPEC, SEM_SPEC] + [HBM_SPEC] * (2 * n) + [pl.BlockSpec(memory_space=pltpu.VMEM)],
        out_shape=[pltpu.SemaphoreType.DMA((n_copies,)), pltpu.SemaphoreType.DMA((n_copies,))] + thru
        + [jax.ShapeDtypeStruct((F32_ROWS, LANE), F32)],
        input_output_aliases={i: 2 + i for i in range(2 * n)},
        compiler_params=pltpu.CompilerParams(has_side_effects=DATAFLOW),
    )(*srcs, *lands)
    return dict(sems=out[:2], srcs=out[2:2 + n], lands=out[2 + n:2 + 2 * n], token=out[-1])


def _split_wait(name, plan, started, after):
    n = len(started["srcs"])

    def body(*refs):
        send_s, recv_s = refs[2 * n], refs[2 * n + 1]
        for i, (src, dst, to) in enumerate(plan(refs[:n], refs[n:2 * n])):
            cp = _remote(src, dst, send_s.at[i], recv_s.at[i], to)
            cp.wait_send()
            cp.wait_recv()

    arrs = list(started["srcs"]) + list(started["lands"])
    out = pl.pallas_call(
        body, name=name, in_specs=[HBM_SPEC] * (2 * n) + [SEM_SPEC, SEM_SPEC, ANY],
        out_specs=[HBM_SPEC] * (2 * n), out_shape=[pltpu.HBM(s.shape, s.dtype) for s in arrs],
        input_output_aliases={i: i for i in range(2 * n)},
        compiler_params=pltpu.CompilerParams(has_side_effects=DATAFLOW),
    )(*arrs, *started["sems"], after)
    return out[:n], out[n:]


def _join_halves(rs):
    n = len(rs)

    def body(*refs):
        outs = refs[n:2 * n]
        send_s, recv_s = refs[2 * n:]
        x, y, c, _ = _place()
        sibling = (x, y, 1 - c)

        def half(a, of_c):
            hr = outs[a].shape[1] // 2
            return outs[a].at[:, pl.ds(pl.multiple_of(of_c * hr, 8), hr)]

        copies = [_remote(half(a, c), half(a, c), send_s.at[a], recv_s.at[a], sibling) for a in range(n)]
        for cp in copies:
            cp.start()
        for a in range(n):
            landed = half(a, 1 - c)
            _remote(landed, landed, send_s.at[a], recv_s.at[a], sibling).wait_recv()
        for cp in copies:
            cp.wait_send()

    out_shape = [jax.ShapeDtypeStruct(r.shape, r.dtype) for r in rs]
    return _comm_call("join_halves", body, rs, out_shape, [n, n], aliases={a: a for a in range(n)})


NDEV = 8


def _allreduce_small(buf):
    r = buf.shape[0]

    def body(in_ref, out_ref, gath, send_s, recv_s):
        x, y, c, _ = _place()
        me = 4 * x + 2 * y + c
        copies = []
        for rel in range(1, NDEV):
            px = 1 - x if rel & 4 else x
            py = 1 - y if rel & 2 else y
            pc = 1 - c if rel & 1 else c
            copies.append((_remote(in_ref, gath.at[me], send_s.at[rel - 1], recv_s.at[rel - 1], (px, py, pc)),
                           4 * px + 2 * py + pc))
        for cp, _ in copies:
            cp.start()
        gath[me] = in_ref[...]
        for rel, (cp, peer) in enumerate(copies):
            landed = gath.at[peer]
            _remote(landed, landed, send_s.at[rel], recv_s.at[rel], (x, y, c)).wait_recv()
        for cp, _ in copies:
            cp.wait_send()
        total = gath[0]
        for d in range(1, NDEV):
            total = total + gath[d]
        out_ref[...] = total

    vm = pl.BlockSpec(memory_space=pltpu.VMEM)
    return pl.pallas_call(
        body, name="allreduce_small", in_specs=[vm], out_specs=vm, out_shape=jax.ShapeDtypeStruct((r, LANE), F32),
        scratch_shapes=[pltpu.VMEM((NDEV, r, LANE), F32), pltpu.SemaphoreType.DMA((NDEV - 1,)),
                        pltpu.SemaphoreType.DMA((NDEV - 1,))],
        compiler_params=pltpu.CompilerParams(has_side_effects=True, vmem_limit_bytes=VMEM_LIMIT),
    )(buf)


MAX_ROW_TILE = 512
BF16_ROWS = 16


def _row_tile(rows):
    for t in range(min(rows, MAX_ROW_TILE) // BF16_ROWS * BF16_ROWS, 0, -BF16_ROWS):
        if rows % t == 0:
            return t
    raise ValueError(rows)


def _sum_halves(g, theirs, c_arr):
    nch, rows, cols = g.shape
    hr = rows // 2
    tr = _row_tile(hr)

    def body(c_ref, g_ref, t_ref, o_ref, ob_ref):
        s = g_ref[...] + t_ref[...]
        o_ref[...] = s
        ob_ref[...] = s.astype(BF16)

    blk = pl.BlockSpec((None, tr, cols), lambda j, i, c_ref: (j, i, 0))
    return pl.pallas_call(
        body, name="sum_halves",
        grid_spec=pltpu.PrefetchScalarGridSpec(
            num_scalar_prefetch=1, grid=(nch, hr // tr),
            in_specs=[pl.BlockSpec((None, None, tr, cols), lambda j, i, c_ref: (j, c_ref[0], i, 0)), blk],
            out_specs=[blk, blk]),
        out_shape=[jax.ShapeDtypeStruct((nch, hr, cols), F32), jax.ShapeDtypeStruct((nch, hr, cols), BF16)],
        compiler_params=_cparams(("parallel", "parallel")),
    )(c_arr, g.reshape(nch, 2, hr, cols), theirs)


def _sum_halves_w_in(g, theirs, c_arr):
    hr = D // 2
    sh = IN_DIM // NCHIP

    def body(c_ref, g_ref, t_ref, o_ref, ob_ref):
        s = g_ref[...] + t_ref[...]
        for j, dst, src, width in W_IN_PIECES:
            o_ref[j, :, dst:dst + width] = s[:, src:src + width]
            ob_ref[j, :, dst:dst + width] = s[:, src:src + width].astype(BF16)

    out = pl.BlockSpec((NCHIP, WT, sh), lambda i, c_ref: (0, i, 0))
    return pl.pallas_call(
        body, name="sum_halves_w_in",
        grid_spec=pltpu.PrefetchScalarGridSpec(
            num_scalar_prefetch=1, grid=(hr // WT,),
            in_specs=[pl.BlockSpec((None, WT, NP), lambda i, c_ref: (c_ref[0], i, 0)),
                      pl.BlockSpec((None, WT, NP), lambda i, c_ref: (0, i, 0))],
            out_specs=[out, out]),
        out_shape=[jax.ShapeDtypeStruct((NCHIP, hr, sh), F32), jax.ShapeDtypeStruct((NCHIP, hr, sh), BF16)],
        compiler_params=_cparams(("parallel",)),
    )(c_arr, g.reshape(2, hr, NP), theirs)


def _sum_chips(p, q, place, l, into=None):
    _, rows, cols = p.shape
    tr = _row_tile(rows)
    steps = rows // tr

    def body(place_ref, p_ref, q0, q1, q2, *rest):
        rest[-1][...] = ((p_ref[...] + q0[...].astype(F32)) + q1[...].astype(F32)) + q2[...].astype(F32)

    qs = lambda k: pl.BlockSpec((None, tr, cols), lambda i, place_ref: (k, i, 0))
    return pl.pallas_call(
        body, name="sum_chips",
        grid_spec=pltpu.PrefetchScalarGridSpec(
            num_scalar_prefetch=1, grid=(steps,),
            in_specs=[pl.BlockSpec((None, tr, cols), lambda i, place_ref: (place_ref[0], i, 0)), qs(0), qs(1), qs(2)]
            + ([ANY] if into is not None else []),
            out_specs=pl.BlockSpec((None, tr, cols), lambda i, place_ref: (l, place_ref[1] * steps + i, 0))),
        out_shape=jax.ShapeDtypeStruct((DEPTH, 2 * rows, cols), F32),
        input_output_aliases={5: 0} if into is not None else {},
        compiler_params=_cparams(("parallel",)),
    )(place, p, q, q, q, *([into] if into is not None else []))


def _adamw(w, g, m, v):
    layers, rows, cols = w.shape
    tr = _row_tile(rows)

    def body(w_ref, g_ref, m_ref, v_ref, d_ref, nm_ref, nv_ref):
        gv = g_ref[...]
        nm = ADAM_B1 * m_ref[...] + (1.0 - ADAM_B1) * gv
        nv = ADAM_B2 * v_ref[...] + (1.0 - ADAM_B2) * jnp.square(gv)
        m_hat = nm / (1.0 - ADAM_B1 ** ADAM_STEP)
        v_hat = nv / (1.0 - ADAM_B2 ** ADAM_STEP)
        d_ref[...] = -ADAM_LR * (m_hat / (jnp.sqrt(v_hat) + ADAM_EPS) + ADAM_WD * w_ref[...])
        nm_ref[...] = nm
        nv_ref[...] = nv

    blk = pl.BlockSpec((None, tr, cols), lambda l, i: (l, i, 0))
    return pl.pallas_call(
        body, name="adamw", grid=(layers, rows // tr), in_specs=[blk] * 4, out_specs=[blk] * 3,
        out_shape=[jax.ShapeDtypeStruct(w.shape, F32)] * 3, compiler_params=_cparams(("parallel", "parallel")),
    )(w, g, m, v)


BIG = ("w_in", "w_out", "w_gate", "w_up", "w_down")
SMALL = ("norm_mix", "a_log", "dt_bias", "o_norm_g", "ln_v_g", "ln_v_b", "w_s", "b_s", "norm_ffn", "norm_final")
ORDER = ("norm_mix", "w_in", "conv_w", "a_log", "dt_bias", "o_norm_g", "ln_v_g", "ln_v_b", "w_s", "b_s", "w_out",
         "norm_ffn", "w_gate", "w_up", "w_down", "norm_final")


F32_ROWS = 8
PACK_ROWS = 128


def _lane_rows(size):
    return -(-size // (F32_ROWS * LANE)) * F32_ROWS


def _pack(arrs):
    parts = [jnp.pad(a.reshape(-1), (0, _lane_rows(a.size) * LANE - a.size)).reshape(-1, LANE) for a in arrs]
    rows = sum(p.shape[0] for p in parts)
    if rows % PACK_ROWS:
        parts.append(jnp.zeros((-rows % PACK_ROWS, LANE), F32))
    return jnp.concatenate(parts, axis=0)


def _unpack(buf, like):
    out, row = [], 0
    for a in like:
        n = _lane_rows(a.size)
        out.append(buf[row:row + n].reshape(-1)[:a.size].reshape(a.shape))
        row += n
    return out


def kernel(x, norm_mix, w_in, conv_w, a_log, dt_bias, o_norm_g, ln_v_g, ln_v_b, w_s, b_s, w_out, norm_ffn, w_gate, w_up, w_down, norm_final, loss_target, m_norm_mix, m_w_in, m_conv_w, m_a_log, m_dt_bias, m_o_norm_g, m_ln_v_g, m_ln_v_b, m_w_s, m_b_s, m_w_out, m_norm_ffn, m_w_gate, m_w_up, m_w_down, m_norm_final, v_norm_mix, v_w_in, v_conv_w, v_a_log, v_dt_bias, v_o_norm_g, v_ln_v_g, v_ln_v_b, v_w_s, v_b_s, v_w_out, v_norm_ffn, v_w_gate, v_w_up, v_w_down, v_norm_final):
    w = dict(norm_mix=norm_mix, w_in=w_in, conv_w=conv_w, a_log=a_log, dt_bias=dt_bias, o_norm_g=o_norm_g,
             ln_v_g=ln_v_g, ln_v_b=ln_v_b, w_s=w_s, b_s=b_s, w_out=w_out, norm_ffn=norm_ffn, w_gate=w_gate, w_up=w_up,
             w_down=w_down, norm_final=norm_final)
    m = dict(norm_mix=m_norm_mix, w_in=m_w_in, conv_w=m_conv_w, a_log=m_a_log, dt_bias=m_dt_bias, o_norm_g=m_o_norm_g,
             ln_v_g=m_ln_v_g, ln_v_b=m_ln_v_b, w_s=m_w_s, b_s=m_b_s, w_out=m_w_out, norm_ffn=m_norm_ffn,
             w_gate=m_w_gate, w_up=m_w_up, w_down=m_w_down, norm_final=m_norm_final)
    v = dict(norm_mix=v_norm_mix, w_in=v_w_in, conv_w=v_conv_w, a_log=v_a_log, dt_bias=v_dt_bias, o_norm_g=v_o_norm_g,
             ln_v_g=v_ln_v_g, ln_v_b=v_ln_v_b, w_s=v_w_s, b_s=v_b_s, w_out=v_w_out, norm_ffn=v_norm_ffn,
             w_gate=v_w_gate, w_up=v_w_up, w_down=v_w_down, norm_final=v_norm_final)
    chip = 2 * lax.axis_index("x") + lax.axis_index("y")
    place = jnp.stack([chip, lax.axis_index("c")]).astype(jnp.int32)
    c_arr = place[1:]

    own = [w[n].astype(BF16) for n in BIG] + [conv_w]
    gathered = _gather_weights(own[:-1], conv_w)
    big = {n: lax.dynamic_update_index_in_dim(g, o, chip, 0) for n, g, o in zip(BIG + ("conv_w",), gathered, own)}
    for n in BIG[1:]:
        big[n] = big[n].reshape((NCHIP * DEPTH,) + big[n].shape[2:])
    big["w_in"] = [_assemble_w_in(gathered[0], own[0], l, place) for l in range(DEPTH)]
    layers = [_layer_params(l, big, w) for l in range(DEPTH)]

    saved, loss_lanes, dh, dhb, d_norm_final = _forward(x[0], loss_target[0], layers, norm_final[None])
    loss = lax.psum(loss_lanes[0, 0], ("x", "y", "c"))

    def sum_halves(mine, theirs):
        return [(_sum_halves_w_in if n == "w_in" else _sum_halves)(g, t, c_arr) for n, g, t in zip(BIG, mine, theirs)]

    def half_shapes(mine):
        return [jax.ShapeDtypeStruct((g.shape[0], g.shape[1] // 2, g.shape[2]), F32) for g in mine]

    gl, sums, arrived = [None] * DEPTH, [None] * DEPTH, [None] * DEPTH
    last = DEPTH - 1
    dh1, dh1b, g_ffn = _layer_bwd_ffn(dh, dhb, layers[last], saved[last])
    dh, dhb, g_mix = _layer_bwd_mixer(dh1, dh1b, layers[last], saved[last])
    gl[last] = _reference_layout({**g_ffn, **g_mix})
    mine = [gl[last][n] for n in BIG]
    exchange = _split_start("exchange_start", _exchange_plan, mine, half_shapes(mine), len(BIG))
    dh1, dh1b, g_ffn = _layer_bwd_ffn(dh, dhb, layers[0], saved[0], after=[exchange["token"]])
    sums[last] = sum_halves(*_split_wait("exchange_wait", _exchange_plan, exchange, dh1))
    partial = [s16 for _, s16 in sums[last]]
    scatter = _split_start("scatter_start", _scatter_plan, partial,
                           [jax.ShapeDtypeStruct((3,) + p.shape[1:], p.dtype) for p in partial], 3 * len(BIG))
    dx, _, g_mix = _layer_bwd_mixer(dh1, dh1b, layers[0], saved[0], after=[scatter["token"]])
    arrived[last] = _split_wait("scatter_wait", _scatter_plan, scatter, dx)[1]
    gl[0] = _reference_layout({**g_ffn, **g_mix})
    mine = [gl[0][n] for n in BIG]
    sums[0] = sum_halves(mine, _exchange_halves(mine))
    arrived[0] = _scatter_partials([s16 for _, s16 in sums[0]])
    reduced = []
    for a in range(len(BIG)):
        buf = None
        for l in range(DEPTH):
            buf = _sum_chips(sums[l][a][0], arrived[l][a], place, l, into=buf)
        reduced.append(buf)
    g_out = dict(zip(BIG, _join_halves(reduced)))

    small_g = [jnp.stack([gl[l][n] for l in range(DEPTH)]) for n in SMALL[:-1]] + [d_norm_final[0]]
    conv_g = jnp.stack([gl[l]["conv_w"] for l in range(DEPTH)])
    total = _allreduce_small(_pack(small_g + [conv_g]))
    *small_r, conv_r = _unpack(total, small_g + [conv_g])
    g_out.update(zip(SMALL, small_r))
    g_out["conv_w"] = lax.dynamic_slice_in_dim(conv_r, chip * conv_w.shape[2], conv_w.shape[2], axis=2)

    delta, new_m, new_v = {}, {}, {}
    for n in BIG:
        delta[n], new_m[n], new_v[n] = _adamw(w[n], g_out[n], m[n], v[n])
    rest = SMALL + ("conv_w",)
    like = [w[n] for n in rest]
    d, nm, nv = _adamw(*[_pack([src[n] for n in rest])[None] for src in (w, g_out, m, v)])
    for dst, buf in ((delta, d), (new_m, nm), (new_v, nv)):
        dst.update(zip(rest, _unpack(buf[0], like)))

    return (loss, dx[None], *[g_out[n] for n in ORDER], *[delta[n] for n in ORDER], *[new_m[n] for n in ORDER],
            *[new_v[n] for n in ORDER])
```

```python
import functools

import jax
import jax.numpy as jnp
from jax import lax
from jax.experimental import pallas as pl
from jax.experimental.pallas import tpu as pltpu

F32 = jnp.float32
BF16 = jnp.bfloat16
MESH = pl.DeviceIdType.MESH
ANY = pl.BlockSpec(memory_space=pl.ANY)
HIGHEST = lax.Precision.HIGHEST

T = 2048
D = 1024
DEPTH = 2
NCHIP = 4
HEADS = 4
HD = 128
HW = HEADS * HD
CH = 64
GCH = 128
IN_DIM = 3080
NP = 3200
BA_OFF = 3072
FF_SH = 704
EPS = 1e-6
LANE = 128
VMEM_LIMIT = 56 * 1024 * 1024

ADAM_LR = 0.001
ADAM_B1 = 0.9
ADAM_B2 = 0.999
ADAM_EPS = 1e-08
ADAM_WD = 0.01
ADAM_STEP = 10


def _cparams(sem=None):
    return pltpu.CompilerParams(dimension_semantics=sem, vmem_limit_bytes=VMEM_LIMIT)


_DIMS = {"nn": (((1,), (0,)), ((), ())), "nt": (((1,), (1,)), ((), ())), "tn": (((0,), (0,)), ((), ()))}


def _mm(name, mode, a, bs, *, tm, tn, tk, out_dtypes=(F32,), reduce_g=False, resid=None, extras=(), epilogue=None,
        b_spec=None, n_n=None, after=()):
    nb = len(bs)
    ga = a.shape[0]
    gbs = [1 if b_spec is not None else b.shape[0] for b in bs]
    g_n = max([ga] + gbs)
    if mode == "tn":
        k_n, m_n = a.shape[1:]
    else:
        m_n, k_n = a.shape[1:]
    if n_n is None:
        n_n = bs[0].shape[1] if mode == "nt" else bs[0].shape[2]
    assert m_n % tm == 0 and n_n % tn == 0 and k_n % tk == 0, (name, m_n, n_n, k_n)
    mi, nj, kk = m_n // tm, n_n // tn, k_n // tk
    if reduce_g:
        grid = (mi, nj, g_n, kk)
        ids = lambda i, j, g, k: (g, i, j, k)
        n_red = g_n * kk
        red_idx = lambda: pl.program_id(2) * kk + pl.program_id(3)
        sem = ("parallel", "parallel", "arbitrary", "arbitrary")
    else:
        grid = (g_n, mi, nj, kk)
        ids = lambda g, i, j, k: (g, i, j, k)
        n_red = kk
        red_idx = lambda: pl.program_id(3)
        sem = ("parallel", "parallel", "parallel", "arbitrary")

    def pick(gsz, g):
        return g if gsz > 1 else 0

    def a_map(*p):
        g, i, j, k = ids(*p)
        return (pick(ga, g), k, i) if mode == "tn" else (pick(ga, g), i, k)

    def b_map(gsz):
        def f(*p):
            g, i, j, k = ids(*p)
            if b_spec is not None:
                return b_spec[1](g, i, j, k)
            return (pick(gsz, g), j, k) if mode == "nt" else (pick(gsz, g), k, j)
        return f

    def o_map(gsz):
        def f(*p):
            g, i, j, k = ids(*p)
            return (0 if reduce_g else pick(gsz, g), i, j)
        return f

    a_spec = pl.BlockSpec((None, tk, tm) if mode == "tn" else (None, tm, tk), a_map)
    b_block = b_spec[0] if b_spec is not None else ((None, tn, tk) if mode == "nt" else (None, tk, tn))
    b_specs = [pl.BlockSpec(b_block, b_map(gs)) for gs in gbs]
    x_specs = [pl.BlockSpec((None, tm, tn), o_map(e.shape[0])) for e in extras]
    r_specs = [pl.BlockSpec((None, tm, tn), o_map(resid.shape[0]))] if resid is not None else []
    g_out = 1 if reduce_g else g_n
    out_shape = [jax.ShapeDtypeStruct((g_out, m_n, n_n), dt) for dt in out_dtypes]
    out_specs = [pl.BlockSpec((None, tm, tn), o_map(g_out)) for _ in out_dtypes]
    nx, nr, no = len(extras), len(r_specs), len(out_dtypes)
    n_in = 1 + nb + nx + nr + len(after)
    dims = _DIMS[mode]

    def body(*refs):
        a_ref = refs[0]
        b_refs = refs[1:1 + nb]
        x_refs = refs[1 + nb:1 + nb + nx]
        r_refs = refs[1 + nb + nx:1 + nb + nx + nr]
        o_refs = refs[n_in:n_in + no]
        acc_refs = refs[n_in + no:]
        r = red_idx()
        av = a_ref[...]
        for b_ref, acc in zip(b_refs, acc_refs):
            p = lax.dot_general(av, b_ref[...], dims, preferred_element_type=F32)

            @pl.when(r == 0)
            def _():
                acc[...] = p

            @pl.when(r > 0)
            def _():
                acc[...] += p

        @pl.when(r == n_red - 1)
        def _():
            accs = [acc[...] for acc in acc_refs]
            if r_refs:
                accs[0] = accs[0] + r_refs[0][...]
            outs = epilogue(accs, [x[...] for x in x_refs]) if epilogue is not None else accs
            for o_ref, o in zip(o_refs, outs):
                o_ref[...] = o.astype(o_ref.dtype)

    return pl.pallas_call(
        body, name=name, grid=grid,
        in_specs=[a_spec] + b_specs + x_specs + r_specs + [ANY] * len(after),
        out_specs=out_specs, out_shape=out_shape,
        scratch_shapes=[pltpu.VMEM((tm, tn), F32) for _ in range(nb)],
        compiler_params=_cparams(sem),
    )(a, *bs, *extras, *([resid] if resid is not None else []), *after)


def _sigmoid(x):
    return 1.0 / (1.0 + jnp.exp(-x))


def _silu(x):
    return x * _sigmoid(x)


def _gelu(x):
    return 0.5 * x * (1.0 + jnp.tanh(0.7978845608028654 * (x + 0.044715 * (x * x * x))))


def _rms_fn(h, gain):
    return h * lax.rsqrt(jnp.mean(h * h, axis=-1, keepdims=True) + EPS) * gain


def _shift_impl(x, s):
    n = x.shape[0]
    rolled = pltpu.roll(x, s % n, 0)
    row = lax.broadcasted_iota(jnp.int32, x.shape, 0)
    return jnp.where((row >= s) & (row < n + s), rolled, 0.0)


@functools.partial(jax.custom_vjp, nondiff_argnums=(1,))
def _shift(x, s):
    return _shift_impl(x, s)


def _shift_fwd(x, s):
    return _shift_impl(x, s), None


def _shift_bwd(s, _, g):
    return (_shift_impl(g, -s),)


_shift.defvjp(_shift_fwd, _shift_bwd)


def _prep_fn(x, w, qk_scale, is_v):
    y = x * w[3:4, :]
    for i in range(3):
        y = y + _shift(x, 3 - i) * w[i:i + 1, :]
    y = _silu(y)
    nrm = lax.rsqrt(jnp.sum(y * y, axis=-1, keepdims=True) + EPS) * qk_scale
    return y * jnp.where(is_v, 1.0, nrm)


def _softplus(x):
    return jnp.maximum(x, 0.0) + jnp.log(1.0 + jnp.exp(-jnp.abs(x)))


def _gates_fn(ba, a_log, dt_bias):
    lane = lax.broadcasted_iota(jnp.int32, ba.shape, 1)
    beta = _sigmoid(ba)
    g = -jnp.exp(a_log) * _softplus(ba + dt_bias)
    return jnp.where(lane < HEADS, beta, g)


def _dot16(a, b, dims=_DIMS["nn"]):
    return lax.dot_general(a.astype(BF16), b.astype(BF16), dims, preferred_element_type=F32)


def _dot32(a, b):
    return jnp.dot(a, b, preferred_element_type=F32, precision=HIGHEST)


def _dot3(a, b, dims=_DIMS["nn"]):
    return lax.dot_general(a, b, dims, preferred_element_type=F32, precision=lax.Precision.HIGH)


def _tri_inverses(mats):
    row = lax.broadcasted_iota(jnp.int32, (CH, CH), 0)
    col = lax.broadcasted_iota(jnp.int32, (CH, CH), 1)
    eye = (row == col).astype(F32)
    ts = [eye - a for a in mats]
    ps = list(mats)
    for _ in range(5):
        ps = [_dot3(p, p) for p in ps]
        ts = [t + _dot3(t, p) for t, p in zip(ts, ps)]
    return ts


@jax.custom_vjp
def _tri_solves(mats, rhs):
    return [_dot3(t, b) for t, b in zip(_tri_inverses(mats), rhs)]


def _tri_solves_fwd(mats, rhs):
    ts = _tri_inverses(mats)
    xs = [_dot3(t, b) for t, b in zip(ts, rhs)]
    return xs, (ts, xs)


def _tri_solves_bwd(res, dxs):
    ts, xs = res
    dbs = [_dot3(t, dx, _DIMS["tn"]) for t, dx in zip(ts, dxs)]
    return [-_dot3(db, x, _DIMS["nt"]) for db, x in zip(dbs, xs)], dbs


_tri_solves.defvjp(_tri_solves_fwd, _tri_solves_bwd)


def _chunk_prep_fn(xs, bgs):
    row = lax.broadcasted_iota(jnp.int32, (CH, CH), 0)
    col = lax.broadcasted_iota(jnp.int32, (CH, CH), 1)
    incl = row >= col
    strict = row > col
    lmat = incl.astype(F32)
    n = len(xs)
    items = [(i, h) for i in range(n) for h in range(HEADS)]
    part = lambda i, h, c: xs[i][:, c * HW + h * HD:c * HW + (h + 1) * HD]
    q = [part(i, h, 0) for i, h in items]
    k = [part(i, h, 1) for i, h in items]
    v = [part(i, h, 2) for i, h in items]
    beta = [bgs[i][:, h:h + 1] for i, h in items]
    gc_all = [_dot32(lmat, bg) for bg in bgs]
    gc = [gc_all[i][:, HEADS + h:HEADS + h + 1] for i, h in items]
    gmat = [jnp.where(strict, jnp.broadcast_to(bgs[i][:, HEADS + h:HEADS + h + 1], (CH, CH)), 0.0) for i, h in items]
    diff = [_dot3(lmat, m) for m in gmat]
    decay = [jnp.where(incl, jnp.exp(jnp.where(incl, d, 0.0)), 0.0) for d in diff]
    k_beta = [kk * b for kk, b in zip(k, beta)]
    kk_t = [_dot16(kb, kk, _DIMS["nt"]) for kb, kk in zip(k_beta, k)]
    qk_t = [_dot16(qq, kk, _DIMS["nt"]) for qq, kk in zip(q, k)]
    a = [jnp.where(strict, m * d, 0.0) for m, d in zip(kk_t, decay)]
    eg = [jnp.exp(g) for g in gc]
    rhs = [jnp.concatenate([vv * b, kb * e], axis=-1) for vv, b, kb, e in zip(v, beta, k_beta, eg)]
    uw = _tri_solves(a, rhs)
    qk = [m * d for m, d in zip(qk_t, decay)]
    g_last = [g[CH - 1:CH, :] for g in gc]
    qe = [qq * e for qq, e in zip(q, eg)]
    kd = [kk * jnp.exp(gl - g) for kk, gl, g in zip(k, g_last, gc)]
    egl = [jnp.broadcast_to(jnp.exp(gl), (1, HD)) for gl in g_last]
    out = []
    for i in range(n):
        mine = slice(i * HEADS, (i + 1) * HEADS)
        cat = lambda vals: jnp.concatenate(vals[mine], axis=-1)
        out.append((cat([x[:, :HD] for x in uw]), cat([x[:, HD:] for x in uw]), cat(qe), cat(kd),
                    jnp.concatenate([m[None] for m in qk[mine]], axis=0), cat(egl)))
    return out


def _chunk_state_fn(u, w, qe, kd, qk, egl, s):
    ws = [_dot16(a, b) for a, b in zip(w, s)]
    qs = [_dot16(a, b) for a, b in zip(qe, s)]
    v_new = [a - b for a, b in zip(u, ws)]
    o = [a + _dot16(b, c) for a, b, c in zip(qs, qk, v_new)]
    s_new = [a * e + _dot16(b, c, _DIMS["tn"]) for a, e, b, c in zip(s, egl, kd, v_new)]
    return o, s_new


def _mix_fn(o, z, ur, vr, ong, lng, lnb, ws, bst):
    row = lax.broadcasted_iota(jnp.int32, (GCH, GCH), 0)
    col = lax.broadcasted_iota(jnp.int32, (GCH, GCH), 1)
    causal = row >= col
    ug = _gelu(ur)
    vg = _gelu(vr)
    outs_dn, outs_gm = [], []
    for h in range(HEADS):
        sl = slice(h * HD, (h + 1) * HD)
        oh = o[:, sl]
        oh = oh * lax.rsqrt(jnp.mean(oh * oh, axis=-1, keepdims=True) + EPS)
        outs_dn.append(oh * ong * _silu(z[:, sl]))
        vh = vg[:, sl]
        mu = jnp.mean(vh, axis=-1, keepdims=True)
        var = jnp.mean(jnp.square(vh - mu), axis=-1, keepdims=True)
        vn = (vh - mu) * lax.rsqrt(var + EPS) * lng[:, sl] + lnb[:, sl]
        sp = _dot16(jnp.where(causal, ws[h], 0.0), vn) + bst[:, h:h + 1]
        outs_gm.append(ug[:, sl] * sp)
    return jnp.concatenate(outs_dn + outs_gm, axis=-1)


def _loss_fn(h, gain, tgt):
    y = _rms_fn(h, gain)
    return 0.5 * jnp.sum(jnp.mean(jnp.square(y - tgt), axis=-1))


RT = 256


def _rows(n=D):
    return pl.BlockSpec((RT, n), lambda i: (i, 0))


def _whole(shape):
    nd = len(shape)
    return pl.BlockSpec(shape, lambda i: (0,) * nd)


def _rmsnorm(name, h, gain):
    def body(h_ref, g_ref, o_ref):
        o_ref[...] = _rms_fn(h_ref[...], g_ref[...]).astype(BF16)

    return pl.pallas_call(
        body, name=name, grid=(T // RT,), in_specs=[_rows(), _whole((1, D))], out_specs=_rows(),
        out_shape=jax.ShapeDtypeStruct((T, D), BF16), compiler_params=_cparams(("parallel",)),
    )(h, gain)


def _rmsnorm_bwd(name, dhn, h, gain, resid):
    def body(dhn_ref, h_ref, g_ref, r_ref, dh_ref, dh16_ref, dg_ref):
        _, vjp = jax.vjp(_rms_fn, h_ref[...], g_ref[...])
        dh, dg = vjp(dhn_ref[...])
        dh = r_ref[...] + dh
        dh_ref[...] = dh
        dh16_ref[...] = dh.astype(BF16)

        @pl.when(pl.program_id(0) == 0)
        def _():
            dg_ref[...] = dg

        @pl.when(pl.program_id(0) > 0)
        def _():
            dg_ref[...] += dg

    return pl.pallas_call(
        body, name=name, grid=(T // RT,), in_specs=[_rows(), _rows(), _whole((1, D)), _rows()],
        out_specs=[_rows(), _rows(), _whole((1, D))],
        out_shape=[jax.ShapeDtypeStruct((T, D), F32), jax.ShapeDtypeStruct((T, D), BF16),
                   jax.ShapeDtypeStruct((1, D), F32)],
        compiler_params=_cparams(("arbitrary",)),
    )(dhn, h, gain, resid)


def _loss_head(h, gain, tgt):
    def body(h_ref, g_ref, t_ref, l_ref, dh_ref, dh16_ref, dg_ref):
        loss, vjp = jax.vjp(lambda hh, gg: _loss_fn(hh, gg, t_ref[...]), h_ref[...], g_ref[...])
        dh, dg = vjp(jnp.ones((), F32))
        dh_ref[...] = dh
        dh16_ref[...] = dh.astype(BF16)
        lv = jnp.full((1, LANE), loss, F32)

        @pl.when(pl.program_id(0) == 0)
        def _():
            dg_ref[...] = dg
            l_ref[...] = lv

        @pl.when(pl.program_id(0) > 0)
        def _():
            dg_ref[...] += dg
            l_ref[...] += lv

    return pl.pallas_call(
        body, name="loss_head", grid=(T // RT,), in_specs=[_rows(), _whole((1, D)), _rows()],
        out_specs=[_whole((1, LANE)), _rows(), _rows(), _whole((1, D))],
        out_shape=[jax.ShapeDtypeStruct((1, LANE), F32), jax.ShapeDtypeStruct((T, D), F32),
                   jax.ShapeDtypeStruct((T, D), BF16), jax.ShapeDtypeStruct((1, D), F32)],
        compiler_params=_cparams(("arbitrary",)),
    )(h, gain, tgt)


def _prep_flags():
    j = pl.program_id(0)
    qk_scale = jnp.where(j < HEADS, HD ** -0.5, 1.0).astype(F32)
    return qk_scale, j >= 2 * HEADS


def _prep(proj, conv_w):
    def body(x_ref, w_ref, o_ref):
        qk_scale, is_v = _prep_flags()
        o_ref[...] = _prep_fn(x_ref[...], w_ref[...], qk_scale, is_v)

    col = lambda j: (0, j)
    return pl.pallas_call(
        body, name="gdn_prep", grid=(3 * HEADS,),
        in_specs=[pl.BlockSpec((T, HD), col), pl.BlockSpec((4, HD), col)], out_specs=pl.BlockSpec((T, HD), col),
        out_shape=jax.ShapeDtypeStruct((T, 3 * HW), F32), compiler_params=_cparams(("parallel",)),
    )(proj, conv_w)


def _prep_bwd(proj, conv_w, dqkv, dproj):
    def body(x_ref, w_ref, d_ref, _, dx_ref, dw_ref):
        qk_scale, is_v = _prep_flags()
        _, vjp = jax.vjp(lambda x, w: _prep_fn(x, w, qk_scale, is_v), x_ref[...], w_ref[...])
        dx, dw = vjp(d_ref[...])
        dx_ref[...] = dx.astype(BF16)
        dw_ref[...] = dw

    col = lambda j: (0, j)
    return pl.pallas_call(
        body, name="gdn_prep_bwd", grid=(3 * HEADS,),
        in_specs=[pl.BlockSpec((T, HD), col), pl.BlockSpec((4, HD), col), pl.BlockSpec((T, HD), col), ANY],
        out_specs=[pl.BlockSpec((T, HD), col), pl.BlockSpec((4, HD), col)],
        out_shape=[jax.ShapeDtypeStruct((T, NP), BF16), jax.ShapeDtypeStruct((4, 3 * HW), F32)],
        input_output_aliases={3: 0}, compiler_params=_cparams(("parallel",)),
    )(proj, conv_w, dqkv, dproj)


BA_BLK = BA_OFF // LANE


def _gates(proj, a_log, dt_bias):
    def body(x_ref, a_ref, d_ref, o_ref):
        o_ref[...] = _gates_fn(x_ref[...], a_ref[...], d_ref[...])

    return pl.pallas_call(
        body, name="gdn_gates", grid=(1,),
        in_specs=[pl.BlockSpec((T, LANE), lambda i: (0, BA_BLK)), _whole((1, LANE)), _whole((1, LANE))],
        out_specs=_whole((T, LANE)),
        out_shape=jax.ShapeDtypeStruct((T, LANE), F32), compiler_params=_cparams(("arbitrary",)),
    )(proj, a_log, dt_bias)


def _gates_bwd(proj, a_log, dt_bias, dbg, dproj):
    def body(x_ref, a_ref, d_ref, dbg_ref, _, dx_ref, da_ref, dd_ref):
        _, vjp = jax.vjp(_gates_fn, x_ref[...], a_ref[...], d_ref[...])
        dx, da_ref[...], dd_ref[...] = vjp(dbg_ref[...])
        dx_ref[...] = dx.astype(BF16)

    ba = pl.BlockSpec((T, LANE), lambda i: (0, BA_BLK))
    return pl.pallas_call(
        body, name="gdn_gates_bwd", grid=(1,),
        in_specs=[ba, _whole((1, LANE)), _whole((1, LANE)), _whole((T, LANE)), ANY],
        out_specs=[ba, _whole((1, LANE)), _whole((1, LANE))],
        out_shape=[jax.ShapeDtypeStruct((T, NP), BF16), jax.ShapeDtypeStruct((1, LANE), F32),
                   jax.ShapeDtypeStruct((1, LANE), F32)],
        input_output_aliases={4: 0}, compiler_params=_cparams(("arbitrary",)),
    )(proj, a_log, dt_bias, dbg, dproj)


NCK = T // CH
CPS = 2


def _chunk_prep_specs(rev=False):
    at = (lambda n: NCK - 1 - n) if rev else (lambda n: n)
    wide = pl.BlockSpec((CH, HW), lambda n: (at(n), 0))
    return [wide, wide, wide, wide, pl.BlockSpec((HEADS, CH, CH), lambda n: (0, at(n), 0)),
            pl.BlockSpec((None, 1, HW), lambda n: (at(n), 0, 0))]


def _chunk_prep_shapes(dtypes):
    shp = [(T, HW), (T, HW), (T, HW), (T, HW), (HEADS, T, CH), (NCK, 1, HW)]
    return [jax.ShapeDtypeStruct(s, dt) for s, dt in zip(shp, dtypes)]


def _chunk_prep(qkv, bg):
    def body(x_ref, bg_ref, *o_refs):
        rows = [slice(ci * CH, (ci + 1) * CH) for ci in range(CPS)]
        res = _chunk_prep_fn([x_ref[r, :] for r in rows], [bg_ref[r, :] for r in rows])
        for ci, (u, w, qe, kd, qk, egl) in enumerate(res):
            for o_ref, val in zip(o_refs[:4], (u, w, qe, kd)):
                o_ref[rows[ci], :] = val.astype(o_ref.dtype)
            o_refs[4][:, rows[ci], :] = qk.astype(BF16)
            o_refs[5][ci] = egl

    wide = pl.BlockSpec((CPS * CH, HW), lambda n: (n, 0))
    return pl.pallas_call(
        body, name="gdn_chunk_prep", grid=(NCK // CPS,),
        in_specs=[pl.BlockSpec((CPS * CH, 3 * HW), lambda n: (n, 0)), pl.BlockSpec((CPS * CH, LANE), lambda n: (n, 0))],
        out_specs=[wide, wide, wide, wide, pl.BlockSpec((HEADS, CPS * CH, CH), lambda n: (0, n, 0)),
                   pl.BlockSpec((CPS, 1, HW), lambda n: (n, 0, 0))],
        out_shape=_chunk_prep_shapes((F32, BF16, BF16, BF16, BF16, F32)),
        compiler_params=_cparams(("parallel",)),
    )(qkv, bg)


def _chunk_prep_bwd(qkv, bg, cots):
    def body(x_ref, bg_ref, du, dw, dqe, dkd, dqk, degl, dx_ref, dbg_ref):
        rows = [slice(ci * CH, (ci + 1) * CH) for ci in range(CPS)]
        _, vjp = jax.vjp(_chunk_prep_fn, [x_ref[r, :] for r in rows], [bg_ref[r, :] for r in rows])
        dxs, dbgs = vjp([(du[r, :], dw[r, :], dqe[r, :], dkd[r, :], dqk[:, r, :], degl[ci])
                         for ci, r in enumerate(rows)])
        for r, dx, dbg in zip(rows, dxs, dbgs):
            dx_ref[r, :] = dx
            dbg_ref[r, :] = dbg

    wide = pl.BlockSpec((CPS * CH, HW), lambda n: (n, 0))
    return pl.pallas_call(
        body, name="gdn_chunk_prep_bwd", grid=(NCK // CPS,),
        in_specs=[pl.BlockSpec((CPS * CH, 3 * HW), lambda n: (n, 0)), pl.BlockSpec((CPS * CH, LANE), lambda n: (n, 0)),
                  wide, wide, wide, wide, pl.BlockSpec((HEADS, CPS * CH, CH), lambda n: (0, n, 0)),
                  pl.BlockSpec((CPS, 1, HW), lambda n: (n, 0, 0))],
        out_specs=[pl.BlockSpec((CPS * CH, 3 * HW), lambda n: (n, 0)), pl.BlockSpec((CPS * CH, LANE), lambda n: (n, 0))],
        out_shape=[jax.ShapeDtypeStruct((T, 3 * HW), F32), jax.ShapeDtypeStruct((T, LANE), F32)],
        compiler_params=_cparams(("parallel",)),
    )(qkv, bg, *cots)


def _head_args(refs):
    u, w, qe, kd, qk, egl = refs
    sls = [slice(h * HD, (h + 1) * HD) for h in range(HEADS)]
    return ([u[:, sl] for sl in sls], [w[:, sl].astype(F32) for sl in sls], [qe[:, sl].astype(F32) for sl in sls],
            [kd[:, sl].astype(F32) for sl in sls], [qk[h].astype(F32) for h in range(HEADS)],
            [egl[:, sl] for sl in sls])


def _chunk_scan(prep):
    def body(*refs):
        o_ref, sh_ref, s_ref = refs[6:]

        @pl.when(pl.program_id(0) == 0)
        def _():
            s_ref[...] = jnp.zeros_like(s_ref)

        s = [s_ref[h] for h in range(HEADS)]
        for h in range(HEADS):
            sh_ref[h, 0] = s[h]
        o, s_new = _chunk_state_fn(*_head_args(refs[:6]), s)
        for h in range(HEADS):
            o_ref[:, h * HD:(h + 1) * HD] = o[h]
            s_ref[h] = s_new[h]

    return pl.pallas_call(
        body, name="gdn_scan", grid=(NCK,), in_specs=_chunk_prep_specs(),
        out_specs=[pl.BlockSpec((CH, HW), lambda n: (n, 0)), pl.BlockSpec((HEADS, 1, HD, HD), lambda n: (0, n, 0, 0))],
        out_shape=[jax.ShapeDtypeStruct((T, HW), F32), jax.ShapeDtypeStruct((HEADS, NCK, HD, HD), F32)],
        scratch_shapes=[pltpu.VMEM((HEADS, HD, HD), F32)], compiler_params=_cparams(("arbitrary",)),
    )(*prep)


def _chunk_scan_bwd(prep, s_hist, do):
    def body(*refs):
        sh_ref, do_ref = refs[6:8]
        d_refs = refs[8:14]
        ds_ref = refs[14]

        @pl.when(pl.program_id(0) == 0)
        def _():
            ds_ref[...] = jnp.zeros_like(ds_ref)

        sls = [slice(h * HD, (h + 1) * HD) for h in range(HEADS)]
        _, vjp = jax.vjp(_chunk_state_fn, *_head_args(refs[:6]), [sh_ref[h, 0] for h in range(HEADS)])
        du, dw, dqe, dkd, dqk, degl, ds = vjp(([do_ref[:, sl] for sl in sls], [ds_ref[h] for h in range(HEADS)]))
        for h, sl in enumerate(sls):
            for d_ref, val in zip(d_refs[:4], (du, dw, dqe, dkd)):
                d_ref[:, sl] = val[h]
            d_refs[4][h] = dqk[h]
            d_refs[5][:, sl] = degl[h]
            ds_ref[h] = ds[h]

    rev = lambda n: NCK - 1 - n
    return pl.pallas_call(
        body, name="gdn_scan_bwd", grid=(NCK,),
        in_specs=_chunk_prep_specs(rev=True) + [pl.BlockSpec((HEADS, 1, HD, HD), lambda n: (0, rev(n), 0, 0)),
                                                pl.BlockSpec((CH, HW), lambda n: (rev(n), 0))],
        out_specs=_chunk_prep_specs(rev=True), out_shape=_chunk_prep_shapes((F32,) * 6),
        scratch_shapes=[pltpu.VMEM((HEADS, HD, HD), F32)], compiler_params=_cparams(("arbitrary",)),
    )(*prep, s_hist, do)


def _mix_specs():
    pc = lambda c: pl.BlockSpec((GCH, HW), lambda i: (i, c))
    return [pl.BlockSpec((GCH, HW), lambda i: (i, 0)), pc(3), pc(4), pc(5), _whole((1, HD)), _whole((1, HW)),
            _whole((1, HW)), _whole((HEADS, GCH, GCH)), _whole((GCH, LANE))]


def _mix(o, proj, ong, lng, lnb, ws, bst):
    def body(o_ref, z_ref, u_ref, v_ref, ong_ref, lng_ref, lnb_ref, ws_ref, bs_ref, m_ref):
        m_ref[...] = _mix_fn(o_ref[...], z_ref[...], u_ref[...], v_ref[...], ong_ref[...], lng_ref[...],
                             lnb_ref[...], ws_ref[...], bs_ref[...]).astype(BF16)

    return pl.pallas_call(
        body, name="mix", grid=(T // GCH,), in_specs=_mix_specs(),
        out_specs=pl.BlockSpec((GCH, D), lambda i: (i, 0)), out_shape=jax.ShapeDtypeStruct((T, D), BF16),
        compiler_params=_cparams(("parallel",)),
    )(o, proj, proj, proj, ong, lng, lnb, ws, bst)


def _mix_bwd(o, proj, ong, lng, lnb, ws, bst, dmix):
    def body(o_ref, z_ref, u_ref, v_ref, ong_ref, lng_ref, lnb_ref, ws_ref, bs_ref, dm_ref,
             do_ref, dzuv_ref, dong_ref, dlng_ref, dlnb_ref, dws_ref, dbs_ref):
        _, vjp = jax.vjp(_mix_fn, o_ref[...], z_ref[...], u_ref[...], v_ref[...], ong_ref[...], lng_ref[...],
                         lnb_ref[...], ws_ref[...], bs_ref[...])
        do, dz, du, dv, dong, dlng, dlnb, dws, dbs = vjp(dm_ref[...])
        do_ref[...] = do
        dzuv_ref[:, 0:HW] = dz.astype(BF16)
        dzuv_ref[:, HW:2 * HW] = du.astype(BF16)
        dzuv_ref[:, 2 * HW:3 * HW] = dv.astype(BF16)
        acc = [(dong_ref, dong), (dlng_ref, dlng), (dlnb_ref, dlnb), (dws_ref, dws), (dbs_ref, dbs)]

        @pl.when(pl.program_id(0) == 0)
        def _():
            for r, val in acc:
                r[...] = val

        @pl.when(pl.program_id(0) > 0)
        def _():
            for r, val in acc:
                r[...] += val

    shp = lambda *s: jax.ShapeDtypeStruct(s, F32)
    return pl.pallas_call(
        body, name="mix_bwd", grid=(T // GCH,),
        in_specs=_mix_specs() + [pl.BlockSpec((GCH, D), lambda i: (i, 0))],
        out_specs=[pl.BlockSpec((GCH, HW), lambda i: (i, 0)), pl.BlockSpec((GCH, 3 * HW), lambda i: (i, 1)),
                   _whole((1, HD)), _whole((1, HW)), _whole((1, HW)), _whole((HEADS, GCH, GCH)), _whole((GCH, LANE))],
        out_shape=[shp(T, HW), jax.ShapeDtypeStruct((T, NP), BF16), shp(1, HD), shp(1, HW), shp(1, HW),
                   shp(HEADS, GCH, GCH), shp(GCH, LANE)],
        compiler_params=_cparams(("arbitrary",)),
    )(o, proj, proj, proj, ong, lng, lnb, ws, bst, dmix)


def _swiglu_epilogue(accs, _):
    gate, up = accs
    return [gate, up, _silu(gate) * up]


def _swiglu_bwd_epilogue(accs, extras):
    dact = accs[0]
    gate, up = (e.astype(F32) for e in extras)
    sg = _sigmoid(gate)
    return [dact * up * (sg * (1.0 + gate * (1.0 - sg))), dact * (gate * sg)]


def _layer_fwd(h, p):
    hn = _rmsnorm("rms_mix", h, p["norm_mix"])
    proj = _mm("in_proj", "nn", hn[None], [p["w_in"][None]], tm=1024, tn=640, tk=D)[0][0]
    qkv = _prep(proj, p["conv_w"])
    bg = _gates(proj, p["a_log"], p["dt_bias"])
    prep = _chunk_prep(qkv, bg)
    o, s_hist = _chunk_scan(prep)
    if "late" in p:
        p.update(p.pop("late")(o))
    mix = _mix(o, proj, p["o_norm_g"], p["ln_v_g"], p["ln_v_b"], p["w_s"], p["bst"])
    h1 = _mm("out_proj", "nn", mix[None], [p["w_out"]], tm=1024, tn=512, tk=D // NCHIP, resid=h[None], n_n=D,
             b_spec=((None, D // NCHIP, 512), lambda g, i, j, k: (k, 0, j)))[0][0]
    h2n = _rmsnorm("rms_ffn", h1, p["norm_ffn"])
    gate, up, act = _mm("ffn_in", "nn", h2n[None], [p["w_gate"], p["w_up"]], tm=1024, tn=FF_SH, tk=D,
                        out_dtypes=(BF16, BF16, BF16), epilogue=_swiglu_epilogue)
    h2 = _mm("ffn_out", "nn", act, [p["w_down"]], tm=1024, tn=512, tk=FF_SH, reduce_g=True, resid=h1[None])[0][0]
    saved = dict(h=h, hn=hn, proj=proj, qkv=qkv, bg=bg, prep=prep, o=o, s_hist=s_hist, mix=mix, h1=h1, h2n=h2n,
                 gate=gate, up=up, act=act)
    return h2, saved


def _layer_bwd_ffn(dh2, dh2b, p, s, after=()):
    dh2b = dh2b[None]
    dgate, dup = _mm("ffn_out_bwd", "nt", dh2b, [p["w_down"]], tm=1024, tn=FF_SH, tk=D, out_dtypes=(BF16, BF16),
                     extras=(s["gate"], s["up"]), epilogue=_swiglu_bwd_epilogue, after=after)
    dh2n = _mm("ffn_gate_bwd", "nt", dgate, [p["w_gate"]], tm=1024, tn=512, tk=FF_SH, reduce_g=True)[0]
    dh2n = _mm("ffn_up_bwd", "nt", dup, [p["w_up"]], tm=1024, tn=512, tk=FF_SH, reduce_g=True, resid=dh2n)[0][0]
    dh1, dh1b, d_norm_ffn = _rmsnorm_bwd("rms_ffn_bwd", dh2n, s["h1"], p["norm_ffn"], dh2)
    d_w_down = _mm("ffn_wdown_grad", "tn", s["act"], [dh2b], tm=FF_SH, tn=512, tk=1024)[0]
    d_w_gate = _mm("ffn_wgate_grad", "tn", s["h2n"][None], [dgate], tm=512, tn=FF_SH, tk=1024)[0]
    d_w_up = _mm("ffn_wup_grad", "tn", s["h2n"][None], [dup], tm=512, tn=FF_SH, tk=1024)[0]
    return dh1, dh1b, dict(norm_ffn=d_norm_ffn, w_gate=d_w_gate, w_up=d_w_up, w_down=d_w_down)


def _layer_bwd_mixer(dh1, dh1b, p, s, after=()):
    dh1b = dh1b[None]
    dmix = _mm("out_proj_bwd", "nt", dh1b, [p["w_out"]], tm=1024, tn=D // NCHIP, tk=D, n_n=D, after=after,
               b_spec=((None, D // NCHIP, D), lambda g, i, j, k: (j, 0, k)))[0][0]
    d_w_out = _mm("out_proj_wgrad", "tn", s["mix"][None], [dh1b], tm=512, tn=512, tk=1024)[0][0]
    do, dproj, d_ong, d_lng, d_lnb, d_ws, d_bst = _mix_bwd(
        s["o"], s["proj"], p["o_norm_g"], p["ln_v_g"], p["ln_v_b"], p["w_s"], p["bst"], dmix)
    dqkv, dbg = _chunk_prep_bwd(s["qkv"], s["bg"], _chunk_scan_bwd(s["prep"], s["s_hist"], do))
    dproj, d_conv = _prep_bwd(s["proj"], p["conv_w"], dqkv, dproj)
    dproj, d_a_log, d_dt_bias = _gates_bwd(s["proj"], p["a_log"], p["dt_bias"], dbg, dproj)
    dproj = dproj[None]
    dhn = _mm("in_proj_bwd", "nt", dproj, [p["w_in"][None]], tm=1024, tn=512, tk=640)[0][0]
    dh, dhb, d_norm_mix = _rmsnorm_bwd("rms_mix_bwd", dhn, s["h"], p["norm_mix"], dh1)
    d_w_in = _mm("in_proj_wgrad", "tn", s["hn"][None], [dproj], tm=512, tn=640, tk=1024)[0]
    grads = dict(norm_mix=d_norm_mix, w_in=d_w_in, conv_w=d_conv, a_log=d_a_log, dt_bias=d_dt_bias, o_norm_g=d_ong,
                 ln_v_g=d_lng, ln_v_b=d_lnb, w_s=d_ws, bst=d_bst, w_out=d_w_out)
    return dh, dhb, grads


def _lanes(v, off=0):
    return jnp.zeros((1, LANE), F32).at[0, off:off + v.shape[0]].set(v)


def _w_in_pieces():
    regions = [(0, 2048, 0), (2048, 2056, BA_OFF), (2056, IN_DIM, 2048)]
    sh = IN_DIM // NCHIP
    out = []
    for j in range(NCHIP):
        for lo, hi, at in regions:
            a, b = max(lo, j * sh), min(hi, (j + 1) * sh)
            if a < b:
                out.append((j, a - j * sh, at + a - lo, b - a))
    return out


W_IN_PIECES = _w_in_pieces()
WT = 256


def _assemble_w_in(gathered, own, place):
    def body(place_ref, g_ref, own_ref, o_ref):
        o_ref[:, IN_DIM:] = jnp.zeros((WT, NP - IN_DIM), BF16)
        mine = own_ref[...]
        for j, src, dst, width in W_IN_PIECES:
            val = jnp.where(place_ref[0] == j, mine[:, src:src + width], g_ref[j, :, src:src + width])
            o_ref[:, dst:dst + width] = val

    sh = IN_DIM // NCHIP
    return pl.pallas_call(
        body, name="assemble_w_in",
        grid_spec=pltpu.PrefetchScalarGridSpec(
            num_scalar_prefetch=1, grid=(D // WT,),
            in_specs=[pl.BlockSpec((NCHIP, WT, sh), lambda i, place_ref: (0, i, 0)),
                      pl.BlockSpec((WT, sh), lambda i, place_ref: (i, 0))],
            out_specs=pl.BlockSpec((WT, NP), lambda i, place_ref: (i, 0))),
        out_shape=jax.ShapeDtypeStruct((D, NP), BF16), compiler_params=_cparams(("parallel",)),
    )(place, gathered, own)


def _layer_params(l, big, small):
    return dict(
        {k: v for k, v in big.items() if k != "conv_w"},
        conv_w=jnp.concatenate([big["conv_w"][j, l] for j in range(NCHIP)], axis=1),
        norm_mix=small["norm_mix"][l][None], norm_ffn=small["norm_ffn"][l][None],
        a_log=_lanes(small["a_log"][l], HEADS), dt_bias=_lanes(small["dt_bias"][l], HEADS),
        o_norm_g=small["o_norm_g"][l][None], ln_v_g=small["ln_v_g"][l][None], ln_v_b=small["ln_v_b"][l][None],
        w_s=small["w_s"][l],
        bst=jnp.pad(small["b_s"][l].T, ((0, 0), (0, LANE - HEADS))),
    )


def _reference_layout(g):
    return dict(
        w_in=g["w_in"],
        w_out=g["w_out"].reshape(NCHIP, D // NCHIP, D),
        w_gate=g["w_gate"], w_up=g["w_up"], w_down=g["w_down"],
        conv_w=g["conv_w"], norm_mix=g["norm_mix"][0], norm_ffn=g["norm_ffn"][0],
        a_log=g["a_log"][0, HEADS:2 * HEADS], dt_bias=g["dt_bias"][0, HEADS:2 * HEADS],
        o_norm_g=g["o_norm_g"][0], ln_v_g=g["ln_v_g"][0], ln_v_b=g["ln_v_b"][0], w_s=g["w_s"],
        b_s=g["bst"][:, :HEADS].T,
    )


def _forward(x, tgt, layers, norm_final):
    h = x
    saved, params = [], []
    for p in layers:
        p = p(h) if callable(p) else p
        h, s = _layer_fwd(h, p)
        saved.append(s)
        params.append(p)
    return (saved, params) + tuple(_loss_head(h, norm_final, tgt))


def _local_step(x, tgt, layers, norm_final):
    saved, layers, loss, dh, dhb, d_norm_final = _forward(x, tgt, layers, norm_final)
    grads = [None] * DEPTH
    for l in reversed(range(DEPTH)):
        dh1, dh1b, g_ffn = _layer_bwd_ffn(dh, dhb, layers[l], saved[l])
        dh, dhb, g_mix = _layer_bwd_mixer(dh1, dh1b, layers[l], saved[l])
        grads[l] = {**g_ffn, **g_mix}
    return loss, dh, grads, d_norm_final


def _place():
    x, y, c = lax.axis_index("x"), lax.axis_index("y"), lax.axis_index("c")
    return x, y, c, [(1 - x, y), (x, 1 - y), (1 - x, 1 - y)]


def _remote(src, dst, send_sem, recv_sem, to):
    return pltpu.make_async_remote_copy(src_ref=src, dst_ref=dst, send_sem=send_sem, recv_sem=recv_sem,
                                        device_id=to, device_id_type=MESH)


def _comm_call(name, body, ins, out_shape, n_sems, aliases=None):
    return pl.pallas_call(
        body, name=name, in_specs=[ANY] * len(ins), out_specs=[ANY] * len(out_shape), out_shape=out_shape,
        scratch_shapes=[pltpu.SemaphoreType.DMA((n,)) for n in n_sems], input_output_aliases=aliases or {},
        compiler_params=pltpu.CompilerParams(has_side_effects=True),
    )(*ins)


def _half_rows(ref, of_c, dim):
    hr = ref.shape[dim] // 2
    return pl.ds(pl.multiple_of(of_c * hr, BF16_ROWS), hr)


def _gather_plan(whole):
    def plan(srcs, lands):
        x, y, c, others = _place()
        chip = 2 * x + y
        out = []
        for src, land, all_of_it in zip(srcs, lands, whole):
            for ox, oy in others:
                if all_of_it:
                    out.append((src, land.at[chip], (ox, oy, c)))
                else:
                    out.append((src.at[_half_rows(src, c, 0)], land.at[chip, _half_rows(src, c, 0)], (ox, oy, c)))
        return out
    return plan


def _forward_halves(lands):
    n = len(lands)

    def body(*refs):
        outs = refs[n:2 * n]
        send_s, recv_s = refs[2 * n:]
        x, y, c, others = _place()
        sibling = (x, y, 1 - c)
        copies = []
        for a in range(n):
            for k, (ox, oy) in enumerate(others):
                mine = outs[a].at[2 * ox + oy, _half_rows(outs[a], c, 1)]
                copies.append(_remote(mine, mine, send_s.at[3 * a + k], recv_s.at[3 * a + k], sibling))
        for cp in copies:
            cp.start()
        for a in range(n):
            for k, (ox, oy) in enumerate(others):
                landed = outs[a].at[2 * ox + oy, _half_rows(outs[a], 1 - c, 1)]
                _remote(landed, landed, send_s.at[3 * a + k], recv_s.at[3 * a + k], sibling).wait_recv()
        for cp in copies:
            cp.wait_send()

    out_shape = [jax.ShapeDtypeStruct(g.shape, g.dtype) for g in lands]
    return _comm_call("forward_halves", body, lands, out_shape, [3 * n, 3 * n], aliases={a: a for a in range(n)})


def _exchange_halves(gs):
    n = len(gs)

    def body(*refs):
        ins, outs = refs[:n], refs[n:2 * n]
        send_s, recv_s = refs[2 * n:]
        x, y, c, _ = _place()
        copies = []
        for a in range(n):
            hr = ins[a].shape[1] // 2
            theirs = ins[a].at[:, pl.ds(pl.multiple_of((1 - c) * hr, 8), hr)]
            copies.append(_remote(theirs, outs[a], send_s.at[a], recv_s.at[a], (x, y, 1 - c)))
        for cp in copies:
            cp.start()
        for cp in copies:
            cp.wait()

    out_shape = [jax.ShapeDtypeStruct((g.shape[0], g.shape[1] // 2, g.shape[2]), F32) for g in gs]
    return _comm_call("exchange_halves", body, gs, out_shape, [n, n])


def _scatter_partials(ps):
    n = len(ps)

    def body(*refs):
        ins, outs = refs[:n], refs[n:2 * n]
        send_s, recv_s = refs[2 * n:]
        x, y, c, others = _place()
        copies = []
        for a in range(n):
            for k, (ox, oy) in enumerate(others):
                copies.append(_remote(ins[a].at[2 * ox + oy], outs[a].at[k], send_s.at[3 * a + k],
                                      recv_s.at[3 * a + k], (ox, oy, c)))
        for cp in copies:
            cp.start()
        for cp in copies:
            cp.wait()

    out_shape = [jax.ShapeDtypeStruct((3,) + p.shape[1:], p.dtype) for p in ps]
    return _comm_call("scatter_partials", body, ps, out_shape, [3 * n, 3 * n])


HBM_SPEC = pl.BlockSpec(memory_space=pltpu.HBM)
SEM_SPEC = pl.BlockSpec(memory_space=pltpu.SEMAPHORE)
DATAFLOW = pltpu.SideEffectType.DATAFLOW_SIDE_EFFECTING


def _exchange_plan(srcs, lands):
    x, y, c, _ = _place()
    plan = []
    for src, land in zip(srcs, lands):
        hr = src.shape[1] // 2
        plan.append((src.at[:, pl.ds(pl.multiple_of((1 - c) * hr, 8), hr)], land, (x, y, 1 - c)))
    return plan


def _scatter_plan(srcs, lands):
    x, y, c, others = _place()
    return [(src.at[2 * ox + oy], land.at[k], (ox, oy, c))
            for src, land in zip(srcs, lands) for k, (ox, oy) in enumerate(others)]


def _split_start(name, plan, srcs, land_shapes, n_copies):
    n = len(srcs)
    lands = [pltpu.with_memory_space_constraint(lax.empty(s.shape, s.dtype), pltpu.HBM) for s in land_shapes]
    srcs = [pltpu.with_memory_space_constraint(s, pltpu.HBM) for s in srcs]

    def body(*refs):
        send_s, recv_s = refs[2 * n], refs[2 * n + 1]
        for i, (src, dst, to) in enumerate(plan(refs[:n], refs[n:2 * n])):
            _remote(src, dst, send_s.at[i], recv_s.at[i], to).start()
        refs[-1][...] = jnp.zeros_like(refs[-1])

    thru = [pltpu.HBM(s.shape, s.dtype) for s in srcs + lands]
    out = pl.pallas_call(
        body, name=name, in_specs=[HBM_SPEC] * (2 * n),
        out_specs=[SEM_SPEC, SEM_SPEC] + [HBM_SPEC] * (2 * n) + [pl.BlockSpec(memory_space=pltpu.VMEM)],
        out_shape=[pltpu.SemaphoreType.DMA((n_copies,)), pltpu.SemaphoreType.DMA((n_copies,))] + thru
        + [jax.ShapeDtypeStruct((F32_ROWS, LANE), F32)],
        input_output_aliases={i: 2 + i for i in range(2 * n)},
        compiler_params=pltpu.CompilerParams(has_side_effects=DATAFLOW),
    )(*srcs, *lands)
    return dict(sems=out[:2], srcs=out[2:2 + n], lands=out[2 + n:2 + 2 * n], token=out[-1])


def _split_wait(name, plan, started, after):
    n = len(started["srcs"])

    def body(*refs):
        send_s, recv_s = refs[2 * n], refs[2 * n + 1]
        for i, (src, dst, to) in enumerate(plan(refs[:n], refs[n:2 * n])):
            cp = _remote(src, dst, send_s.at[i], recv_s.at[i], to)
            cp.wait_send()
            cp.wait_recv()

    arrs = list(started["srcs"]) + list(started["lands"])
    out = pl.pallas_call(
        body, name=name, in_specs=[HBM_SPEC] * (2 * n) + [SEM_SPEC, SEM_SPEC, ANY],
        out_specs=[HBM_SPEC] * (2 * n), out_shape=[pltpu.HBM(s.shape, s.dtype) for s in arrs],
        input_output_aliases={i: i for i in range(2 * n)},
        compiler_params=pltpu.CompilerParams(has_side_effects=DATAFLOW),
    )(*arrs, *started["sems"], after)
    return out[:n], out[n:]


def _join_halves(rs):
    n = len(rs)

    def body(*refs):
        outs = refs[n:2 * n]
        send_s, recv_s = refs[2 * n:]
        x, y, c, _ = _place()
        sibling = (x, y, 1 - c)

        def half(a, of_c):
            hr = outs[a].shape[1] // 2
            return outs[a].at[:, pl.ds(pl.multiple_of(of_c * hr, 8), hr)]

        copies = [_remote(half(a, c), half(a, c), send_s.at[a], recv_s.at[a], sibling) for a in range(n)]
        for cp in copies:
            cp.start()
        for a in range(n):
            landed = half(a, 1 - c)
            _remote(landed, landed, send_s.at[a], recv_s.at[a], sibling).wait_recv()
        for cp in copies:
            cp.wait_send()

    out_shape = [jax.ShapeDtypeStruct(r.shape, r.dtype) for r in rs]
    return _comm_call("join_halves", body, rs, out_shape, [n, n], aliases={a: a for a in range(n)})


NDEV = 8


def _allreduce_small(buf):
    r = buf.shape[0]

    def body(in_ref, out_ref, gath, send_s, recv_s):
        x, y, c, _ = _place()
        me = 4 * x + 2 * y + c
        copies = []
        for rel in range(1, NDEV):
            px = 1 - x if rel & 4 else x
            py = 1 - y if rel & 2 else y
            pc = 1 - c if rel & 1 else c
            copies.append((_remote(in_ref, gath.at[me], send_s.at[rel - 1], recv_s.at[rel - 1], (px, py, pc)),
                           4 * px + 2 * py + pc))
        for cp, _ in copies:
            cp.start()
        gath[me] = in_ref[...]
        for rel, (cp, peer) in enumerate(copies):
            landed = gath.at[peer]
            _remote(landed, landed, send_s.at[rel], recv_s.at[rel], (x, y, c)).wait_recv()
        for cp, _ in copies:
            cp.wait_send()
        total = gath[0]
        for d in range(1, NDEV):
            total = total + gath[d]
        out_ref[...] = total

    vm = pl.BlockSpec(memory_space=pltpu.VMEM)
    return pl.pallas_call(
        body, name="allreduce_small", in_specs=[vm], out_specs=vm, out_shape=jax.ShapeDtypeStruct((r, LANE), F32),
        scratch_shapes=[pltpu.VMEM((NDEV, r, LANE), F32), pltpu.SemaphoreType.DMA((NDEV - 1,)),
                        pltpu.SemaphoreType.DMA((NDEV - 1,))],
        compiler_params=pltpu.CompilerParams(has_side_effects=True, vmem_limit_bytes=VMEM_LIMIT),
    )(buf)


MAX_ROW_TILE = 512
BF16_ROWS = 16


def _row_tile(rows):
    for t in range(min(rows, MAX_ROW_TILE) // BF16_ROWS * BF16_ROWS, 0, -BF16_ROWS):
        if rows % t == 0:
            return t
    raise ValueError(rows)


def _sum_halves(g, theirs, c_arr):
    nch, rows, cols = g.shape
    hr = rows // 2
    tr = _row_tile(hr)

    def body(c_ref, g_ref, t_ref, o_ref, ob_ref):
        s = g_ref[...] + t_ref[...]
        o_ref[...] = s
        ob_ref[...] = s.astype(BF16)

    blk = pl.BlockSpec((None, tr, cols), lambda j, i, c_ref: (j, i, 0))
    return pl.pallas_call(
        body, name="sum_halves",
        grid_spec=pltpu.PrefetchScalarGridSpec(
            num_scalar_prefetch=1, grid=(nch, hr // tr),
            in_specs=[pl.BlockSpec((None, None, tr, cols), lambda j, i, c_ref: (j, c_ref[0], i, 0)), blk],
            out_specs=[blk, blk]),
        out_shape=[jax.ShapeDtypeStruct((nch, hr, cols), F32), jax.ShapeDtypeStruct((nch, hr, cols), BF16)],
        compiler_params=_cparams(("parallel", "parallel")),
    )(c_arr, g.reshape(nch, 2, hr, cols), theirs)


def _sum_halves_w_in(g, theirs, c_arr):
    hr = D // 2
    sh = IN_DIM // NCHIP

    def body(c_ref, g_ref, t_ref, o_ref, ob_ref):
        s = g_ref[...] + t_ref[...]
        for j, dst, src, width in W_IN_PIECES:
            o_ref[j, :, dst:dst + width] = s[:, src:src + width]
            ob_ref[j, :, dst:dst + width] = s[:, src:src + width].astype(BF16)

    out = pl.BlockSpec((NCHIP, WT, sh), lambda i, c_ref: (0, i, 0))
    return pl.pallas_call(
        body, name="sum_halves_w_in",
        grid_spec=pltpu.PrefetchScalarGridSpec(
            num_scalar_prefetch=1, grid=(hr // WT,),
            in_specs=[pl.BlockSpec((None, WT, NP), lambda i, c_ref: (c_ref[0], i, 0)),
                      pl.BlockSpec((None, WT, NP), lambda i, c_ref: (0, i, 0))],
            out_specs=[out, out]),
        out_shape=[jax.ShapeDtypeStruct((NCHIP, hr, sh), F32), jax.ShapeDtypeStruct((NCHIP, hr, sh), BF16)],
        compiler_params=_cparams(("parallel",)),
    )(c_arr, g.reshape(2, hr, NP), theirs)


def _sum_chips(p, q, place, l, into=None):
    _, rows, cols = p.shape
    tr = _row_tile(rows)
    steps = rows // tr

    def body(place_ref, p_ref, q0, q1, q2, *rest):
        rest[-1][...] = ((p_ref[...] + q0[...].astype(F32)) + q1[...].astype(F32)) + q2[...].astype(F32)

    qs = lambda k: pl.BlockSpec((None, tr, cols), lambda i, place_ref: (k, i, 0))
    return pl.pallas_call(
        body, name="sum_chips",
        grid_spec=pltpu.PrefetchScalarGridSpec(
            num_scalar_prefetch=1, grid=(steps,),
            in_specs=[pl.BlockSpec((None, tr, cols), lambda i, place_ref: (place_ref[0], i, 0)), qs(0), qs(1), qs(2)]
            + ([ANY] if into is not None else []),
            out_specs=pl.BlockSpec((None, tr, cols), lambda i, place_ref: (l, place_ref[1] * steps + i, 0))),
        out_shape=jax.ShapeDtypeStruct((DEPTH, 2 * rows, cols), F32),
        input_output_aliases={5: 0} if into is not None else {},
        compiler_params=_cparams(("parallel",)),
    )(place, p, q, q, q, *([into] if into is not None else []))


def _adamw(w, g, m, v):
    layers, rows, cols = w.shape
    tr = _row_tile(rows)

    def body(w_ref, g_ref, m_ref, v_ref, d_ref, nm_ref, nv_ref):
        gv = g_ref[...]
        nm = ADAM_B1 * m_ref[...] + (1.0 - ADAM_B1) * gv
        nv = ADAM_B2 * v_ref[...] + (1.0 - ADAM_B2) * jnp.square(gv)
        m_hat = nm / (1.0 - ADAM_B1 ** ADAM_STEP)
        v_hat = nv / (1.0 - ADAM_B2 ** ADAM_STEP)
        d_ref[...] = -ADAM_LR * (m_hat / (jnp.sqrt(v_hat) + ADAM_EPS) + ADAM_WD * w_ref[...])
        nm_ref[...] = nm
        nv_ref[...] = nv

    blk = pl.BlockSpec((None, tr, cols), lambda l, i: (l, i, 0))
    return pl.pallas_call(
        body, name="adamw", grid=(layers, rows // tr), in_specs=[blk] * 4, out_specs=[blk] * 3,
        out_shape=[jax.ShapeDtypeStruct(w.shape, F32)] * 3, compiler_params=_cparams(("parallel", "parallel")),
    )(w, g, m, v)


BIG = ("w_in", "w_out", "w_gate", "w_up", "w_down")
SMALL = ("norm_mix", "a_log", "dt_bias", "o_norm_g", "ln_v_g", "ln_v_b", "w_s", "b_s", "norm_ffn", "norm_final")
ORDER = ("norm_mix", "w_in", "conv_w", "a_log", "dt_bias", "o_norm_g", "ln_v_g", "ln_v_b", "w_s", "b_s", "w_out",
         "norm_ffn", "w_gate", "w_up", "w_down", "norm_final")


F32_ROWS = 8
PACK_ROWS = 128


def _lane_rows(size):
    return -(-size // (F32_ROWS * LANE)) * F32_ROWS


def _pack(arrs):
    parts = [jnp.pad(a.reshape(-1), (0, _lane_rows(a.size) * LANE - a.size)).reshape(-1, LANE) for a in arrs]
    rows = sum(p.shape[0] for p in parts)
    if rows % PACK_ROWS:
        parts.append(jnp.zeros((-rows % PACK_ROWS, LANE), F32))
    return jnp.concatenate(parts, axis=0)


def _unpack(buf, like):
    out, row = [], 0
    for a in like:
        n = _lane_rows(a.size)
        out.append(buf[row:row + n].reshape(-1)[:a.size].reshape(a.shape))
        row += n
    return out


def kernel(x, norm_mix, w_in, conv_w, a_log, dt_bias, o_norm_g, ln_v_g, ln_v_b, w_s, b_s, w_out, norm_ffn, w_gate, w_up, w_down, norm_final, loss_target, m_norm_mix, m_w_in, m_conv_w, m_a_log, m_dt_bias, m_o_norm_g, m_ln_v_g, m_ln_v_b, m_w_s, m_b_s, m_w_out, m_norm_ffn, m_w_gate, m_w_up, m_w_down, m_norm_final, v_norm_mix, v_w_in, v_conv_w, v_a_log, v_dt_bias, v_o_norm_g, v_ln_v_g, v_ln_v_b, v_w_s, v_b_s, v_w_out, v_norm_ffn, v_w_gate, v_w_up, v_w_down, v_norm_final):
    w = dict(norm_mix=norm_mix, w_in=w_in, conv_w=conv_w, a_log=a_log, dt_bias=dt_bias, o_norm_g=o_norm_g,
             ln_v_g=ln_v_g, ln_v_b=ln_v_b, w_s=w_s, b_s=b_s, w_out=w_out, norm_ffn=norm_ffn, w_gate=w_gate, w_up=w_up,
             w_down=w_down, norm_final=norm_final)
    m = dict(norm_mix=m_norm_mix, w_in=m_w_in, conv_w=m_conv_w, a_log=m_a_log, dt_bias=m_dt_bias, o_norm_g=m_o_norm_g,
             ln_v_g=m_ln_v_g, ln_v_b=m_ln_v_b, w_s=m_w_s, b_s=m_b_s, w_out=m_w_out, norm_ffn=m_norm_ffn,
             w_gate=m_w_gate, w_up=m_w_up, w_down=m_w_down, norm_final=m_norm_final)
    v = dict(norm_mix=v_norm_mix, w_in=v_w_in, conv_w=v_conv_w, a_log=v_a_log, dt_bias=v_dt_bias, o_norm_g=v_o_norm_g,
             ln_v_g=v_ln_v_g, ln_v_b=v_ln_v_b, w_s=v_w_s, b_s=v_b_s, w_out=v_w_out, norm_ffn=v_norm_ffn,
             w_gate=v_w_gate, w_up=v_w_up, w_down=v_w_down, norm_final=v_norm_final)
    chip = 2 * lax.axis_index("x") + lax.axis_index("y")
    place = jnp.stack([chip, lax.axis_index("c")]).astype(jnp.int32)
    c_arr = place[1:]

    own = {n: [w[n][l].astype(BF16) for l in range(DEPTH)] for n in BIG}
    by_chip = lambda a: jax.ShapeDtypeStruct((NCHIP,) + a.shape, a.dtype)

    def start(name, srcs, whole):
        return _split_start(name, _gather_plan(whole), srcs, [by_chip(a) for a in srcs], 3 * len(srcs))

    def finish(name, started, whole, after):
        srcs, lands = _split_wait(name, _gather_plan(whole), started, after)
        passed = iter(_forward_halves([g for g, all_of_it in zip(lands, whole) if not all_of_it]))
        lands = [g if all_of_it else next(passed) for g, all_of_it in zip(lands, whole)]
        return srcs, [lax.dynamic_update_index_in_dim(g, o, chip, 0) for g, o in zip(lands, srcs)]

    ffn = BIG[1:]
    first = start("gather_first_start", [own["w_in"][0], conv_w], [False, True])
    early = start("gather_early_start", [own[n][0] for n in ffn], [False] * len(ffn))
    later = start("gather_later_start", [own[n][1] for n in BIG], [False] * len(BIG))
    (own_w_in, _), (w_in_by_chip, conv_by_chip) = finish("gather_first_wait", first, [False, True], x)

    def late(after):
        return dict(zip(ffn, finish("gather_early_wait", early, [False] * len(ffn), after)[1]))

    layer0 = _layer_params(0, dict(w_in=_assemble_w_in(w_in_by_chip, own_w_in, place), conv_w=conv_by_chip, late=late), w)

    def layer1(after):
        srcs, by = finish("gather_later_wait", later, [False] * len(BIG), after)
        big = dict(zip(ffn, by[1:]), w_in=_assemble_w_in(by[0], srcs[0], place), conv_w=conv_by_chip)
        return _layer_params(1, big, w)

    saved, layers, loss_lanes, dh, dhb, d_norm_final = _forward(x[0], loss_target[0], [layer0, layer1],
                                                                 norm_final[None])
    loss = lax.psum(loss_lanes[0, 0], ("x", "y", "c"))

    def sum_halves(mine, theirs):
        return [(_sum_halves_w_in if n == "w_in" else _sum_halves)(g, t, c_arr) for n, g, t in zip(BIG, mine, theirs)]

    def half_shapes(mine):
        return [jax.ShapeDtypeStruct((g.shape[0], g.shape[1] // 2, g.shape[2]), F32) for g in mine]

    gl, sums, arrived = [None] * DEPTH, [None] * DEPTH, [None] * DEPTH
    last = DEPTH - 1
    dh1, dh1b, g_ffn = _layer_bwd_ffn(dh, dhb, layers[last], saved[last])
    dh, dhb, g_mix = _layer_bwd_mixer(dh1, dh1b, layers[last], saved[last])
    gl[last] = _reference_layout({**g_ffn, **g_mix})
    mine = [gl[last][n] for n in BIG]
    exchange = _split_start("exchange_start", _exchange_plan, mine, half_shapes(mine), len(BIG))
    dh1, dh1b, g_ffn = _layer_bwd_ffn(dh, dhb, layers[0], saved[0], after=[exchange["token"]])
    sums[last] = sum_halves(*_split_wait("exchange_wait", _exchange_plan, exchange, dh1))
    partial = [s16 for _, s16 in sums[last]]
    scatter = _split_start("scatter_start", _scatter_plan, partial,
                           [jax.ShapeDtypeStruct((3,) + p.shape[1:], p.dtype) for p in partial], 3 * len(BIG))
    dx, _, g_mix = _layer_bwd_mixer(dh1, dh1b, layers[0], saved[0], after=[scatter["token"]])
    arrived[last] = _split_wait("scatter_wait", _scatter_plan, scatter, dx)[1]
    gl[0] = _reference_layout({**g_ffn, **g_mix})
    mine = [gl[0][n] for n in BIG]
    sums[0] = sum_halves(mine, _exchange_halves(mine))
    arrived[0] = _scatter_partials([s16 for _, s16 in sums[0]])
    reduced = []
    for a in range(len(BIG)):
        buf = None
        for l in range(DEPTH):
            buf = _sum_chips(sums[l][a][0], arrived[l][a], place, l, into=buf)
        reduced.append(buf)
    g_out = dict(zip(BIG, _join_halves(reduced)))

    small_g = [jnp.stack([gl[l][n] for l in range(DEPTH)]) for n in SMALL[:-1]] + [d_norm_final[0]]
    conv_g = jnp.stack([gl[l]["conv_w"] for l in range(DEPTH)])
    total = _allreduce_small(_pack(small_g + [conv_g]))
    *small_r, conv_r = _unpack(total, small_g + [conv_g])
    g_out.update(zip(SMALL, small_r))
    g_out["conv_w"] = lax.dynamic_slice_in_dim(conv_r, chip * conv_w.shape[2], conv_w.shape[2], axis=2)

    delta, new_m, new_v = {}, {}, {}
    for n in BIG:
        delta[n], new_m[n], new_v[n] = _adamw(w[n], g_out[n], m[n], v[n])
    rest = SMALL + ("conv_w",)
    like = [w[n] for n in rest]
    d, nm, nv = _adamw(*[_pack([src[n] for n in rest])[None] for src in (w, g_out, m, v)])
    for dst, buf in ((delta, d), (new_m, nm), (new_v, nv)):
        dst.update(zip(rest, _unpack(buf[0], like)))

    return (loss, dx[None], *[g_out[n] for n in ORDER], *[delta[n] for n in ORDER], *[new_m[n] for n in ORDER],
            *[new_v[n] for n in ORDER])
```

```python
import functools

import jax
import jax.numpy as jnp
from jax import lax
from jax.experimental import pallas as pl
from jax.experimental.pallas import tpu as pltpu

F32 = jnp.float32
BF16 = jnp.bfloat16
MESH = pl.DeviceIdType.MESH
ANY = pl.BlockSpec(memory_space=pl.ANY)
HIGHEST = lax.Precision.HIGHEST

T = 2048
D = 1024
DEPTH = 2
NCHIP = 4
HEADS = 4
HD = 128
HW = HEADS * HD
CH = 64
GCH = 128
IN_DIM = 3080
NP = 3200
BA_OFF = 3072
FF_SH = 704
EPS = 1e-6
LANE = 128
VMEM_LIMIT = 56 * 1024 * 1024

ADAM_LR = 0.001
ADAM_B1 = 0.9
ADAM_B2 = 0.999
ADAM_EPS = 1e-08
ADAM_WD = 0.01
ADAM_STEP = 10


def _cparams(sem=None):
    return pltpu.CompilerParams(dimension_semantics=sem, vmem_limit_bytes=VMEM_LIMIT)


_DIMS = {"nn": (((1,), (0,)), ((), ())), "nt": (((1,), (1,)), ((), ())), "tn": (((0,), (0,)), ((), ()))}


def _mm(name, mode, a, bs, *, tm, tn, tk, out_dtypes=(F32,), reduce_g=False, resid=None, extras=(), epilogue=None,
        b_spec=None, n_n=None, after=()):
    nb = len(bs)
    ga = a.shape[0]
    gbs = [1 if b_spec is not None else b.shape[0] for b in bs]
    g_n = max([ga] + gbs)
    if mode == "tn":
        k_n, m_n = a.shape[1:]
    else:
        m_n, k_n = a.shape[1:]
    if n_n is None:
        n_n = bs[0].shape[1] if mode == "nt" else bs[0].shape[2]
    assert m_n % tm == 0 and n_n % tn == 0 and k_n % tk == 0, (name, m_n, n_n, k_n)
    mi, nj, kk = m_n // tm, n_n // tn, k_n // tk
    if reduce_g:
        grid = (mi, nj, g_n, kk)
        ids = lambda i, j, g, k: (g, i, j, k)
        n_red = g_n * kk
        red_idx = lambda: pl.program_id(2) * kk + pl.program_id(3)
        sem = ("parallel", "parallel", "arbitrary", "arbitrary")
    else:
        grid = (g_n, mi, nj, kk)
        ids = lambda g, i, j, k: (g, i, j, k)
        n_red = kk
        red_idx = lambda: pl.program_id(3)
        sem = ("parallel", "parallel", "parallel", "arbitrary")

    def pick(gsz, g):
        return g if gsz > 1 else 0

    def a_map(*p):
        g, i, j, k = ids(*p)
        return (pick(ga, g), k, i) if mode == "tn" else (pick(ga, g), i, k)

    def b_map(gsz):
        def f(*p):
            g, i, j, k = ids(*p)
            if b_spec is not None:
                return b_spec[1](g, i, j, k)
            return (pick(gsz, g), j, k) if mode == "nt" else (pick(gsz, g), k, j)
        return f

    def o_map(gsz):
        def f(*p):
            g, i, j, k = ids(*p)
            return (0 if reduce_g else pick(gsz, g), i, j)
        return f

    a_spec = pl.BlockSpec((None, tk, tm) if mode == "tn" else (None, tm, tk), a_map)
    b_block = b_spec[0] if b_spec is not None else ((None, tn, tk) if mode == "nt" else (None, tk, tn))
    b_specs = [pl.BlockSpec(b_block, b_map(gs)) for gs in gbs]
    x_specs = [pl.BlockSpec((None, tm, tn), o_map(e.shape[0])) for e in extras]
    r_specs = [pl.BlockSpec((None, tm, tn), o_map(resid.shape[0]))] if resid is not None else []
    g_out = 1 if reduce_g else g_n
    out_shape = [jax.ShapeDtypeStruct((g_out, m_n, n_n), dt) for dt in out_dtypes]
    out_specs = [pl.BlockSpec((None, tm, tn), o_map(g_out)) for _ in out_dtypes]
    nx, nr, no = len(extras), len(r_specs), len(out_dtypes)
    n_in = 1 + nb + nx + nr + len(after)
    dims = _DIMS[mode]

    def body(*refs):
        a_ref = refs[0]
        b_refs = refs[1:1 + nb]
        x_refs = refs[1 + nb:1 + nb + nx]
        r_refs = refs[1 + nb + nx:1 + nb + nx + nr]
        o_refs = refs[n_in:n_in + no]
        acc_refs = refs[n_in + no:]
        r = red_idx()
        av = a_ref[...]
        for b_ref, acc in zip(b_refs, acc_refs):
            p = lax.dot_general(av, b_ref[...], dims, preferred_element_type=F32)

            @pl.when(r == 0)
            def _():
                acc[...] = p

            @pl.when(r > 0)
            def _():
                acc[...] += p

        @pl.when(r == n_red - 1)
        def _():
            accs = [acc[...] for acc in acc_refs]
            if r_refs:
                accs[0] = accs[0] + r_refs[0][...]
            outs = epilogue(accs, [x[...] for x in x_refs]) if epilogue is not None else accs
            for o_ref, o in zip(o_refs, outs):
                o_ref[...] = o.astype(o_ref.dtype)

    return pl.pallas_call(
        body, name=name, grid=grid,
        in_specs=[a_spec] + b_specs + x_specs + r_specs + [ANY] * len(after),
        out_specs=out_specs, out_shape=out_shape,
        scratch_shapes=[pltpu.VMEM((tm, tn), F32) for _ in range(nb)],
        compiler_params=_cparams(sem),
    )(a, *bs, *extras, *([resid] if resid is not None else []), *after)


def _sigmoid(x):
    return 1.0 / (1.0 + jnp.exp(-x))


def _silu(x):
    return x * _sigmoid(x)


def _gelu(x):
    return 0.5 * x * (1.0 + jnp.tanh(0.7978845608028654 * (x + 0.044715 * (x * x * x))))


def _rms_fn(h, gain):
    return h * lax.rsqrt(jnp.mean(h * h, axis=-1, keepdims=True) + EPS) * gain


def _shift_impl(x, s):
    n = x.shape[0]
    rolled = pltpu.roll(x, s % n, 0)
    row = lax.broadcasted_iota(jnp.int32, x.shape, 0)
    return jnp.where((row >= s) & (row < n + s), rolled, 0.0)


@functools.partial(jax.custom_vjp, nondiff_argnums=(1,))
def _shift(x, s):
    return _shift_impl(x, s)


def _shift_fwd(x, s):
    return _shift_impl(x, s), None


def _shift_bwd(s, _, g):
    return (_shift_impl(g, -s),)


_shift.defvjp(_shift_fwd, _shift_bwd)


def _prep_fn(x, w, qk_scale, is_v):
    y = x * w[3:4, :]
    for i in range(3):
        y = y + _shift(x, 3 - i) * w[i:i + 1, :]
    y = _silu(y)
    nrm = lax.rsqrt(jnp.sum(y * y, axis=-1, keepdims=True) + EPS) * qk_scale
    return y * jnp.where(is_v, 1.0, nrm)


def _softplus(x):
    return jnp.maximum(x, 0.0) + jnp.log(1.0 + jnp.exp(-jnp.abs(x)))


def _gates_fn(ba, a_log, dt_bias):
    lane = lax.broadcasted_iota(jnp.int32, ba.shape, 1)
    beta = _sigmoid(ba)
    g = -jnp.exp(a_log) * _softplus(ba + dt_bias)
    return jnp.where(lane < HEADS, beta, g)


def _dot16(a, b, dims=_DIMS["nn"]):
    return lax.dot_general(a.astype(BF16), b.astype(BF16), dims, preferred_element_type=F32)


def _dot32(a, b):
    return jnp.dot(a, b, preferred_element_type=F32, precision=HIGHEST)


def _dot3(a, b, dims=_DIMS["nn"]):
    return lax.dot_general(a, b, dims, preferred_element_type=F32, precision=lax.Precision.HIGH)


def _tri_inverses(mats):
    row = lax.broadcasted_iota(jnp.int32, (CH, CH), 0)
    col = lax.broadcasted_iota(jnp.int32, (CH, CH), 1)
    eye = (row == col).astype(F32)
    ts = [eye - a for a in mats]
    ps = list(mats)
    for _ in range(5):
        ps = [_dot3(p, p) for p in ps]
        ts = [t + _dot3(t, p) for t, p in zip(ts, ps)]
    return ts


@jax.custom_vjp
def _tri_solves(mats, rhs):
    return [_dot3(t, b) for t, b in zip(_tri_inverses(mats), rhs)]


def _tri_solves_fwd(mats, rhs):
    ts = _tri_inverses(mats)
    xs = [_dot3(t, b) for t, b in zip(ts, rhs)]
    return xs, (ts, xs)


def _tri_solves_bwd(res, dxs):
    ts, xs = res
    dbs = [_dot3(t, dx, _DIMS["tn"]) for t, dx in zip(ts, dxs)]
    return [-_dot3(db, x, _DIMS["nt"]) for db, x in zip(dbs, xs)], dbs


_tri_solves.defvjp(_tri_solves_fwd, _tri_solves_bwd)


def _chunk_prep_fn(xs, bgs):
    row = lax.broadcasted_iota(jnp.int32, (CH, CH), 0)
    col = lax.broadcasted_iota(jnp.int32, (CH, CH), 1)
    incl = row >= col
    strict = row > col
    lmat = incl.astype(F32)
    n = len(xs)
    items = [(i, h) for i in range(n) for h in range(HEADS)]
    part = lambda i, h, c: xs[i][:, c * HW + h * HD:c * HW + (h + 1) * HD]
    q = [part(i, h, 0) for i, h in items]
    k = [part(i, h, 1) for i, h in items]
    v = [part(i, h, 2) for i, h in items]
    beta = [bgs[i][:, h:h + 1] for i, h in items]
    gc_all = [_dot32(lmat, bg) for bg in bgs]
    gc = [gc_all[i][:, HEADS + h:HEADS + h + 1] for i, h in items]
    gmat = [jnp.where(strict, jnp.broadcast_to(bgs[i][:, HEADS + h:HEADS + h + 1], (CH, CH)), 0.0) for i, h in items]
    diff = [_dot3(lmat, m) for m in gmat]
    decay = [jnp.where(incl, jnp.exp(jnp.where(incl, d, 0.0)), 0.0) for d in diff]
    k_beta = [kk * b for kk, b in zip(k, beta)]
    kk_t = [_dot16(kb, kk, _DIMS["nt"]) for kb, kk in zip(k_beta, k)]
    qk_t = [_dot16(qq, kk, _DIMS["nt"]) for qq, kk in zip(q, k)]
    a = [jnp.where(strict, m * d, 0.0) for m, d in zip(kk_t, decay)]
    eg = [jnp.exp(g) for g in gc]
    rhs = [jnp.concatenate([vv * b, kb * e], axis=-1) for vv, b, kb, e in zip(v, beta, k_beta, eg)]
    uw = _tri_solves(a, rhs)
    qk = [m * d for m, d in zip(qk_t, decay)]
    g_last = [g[CH - 1:CH, :] for g in gc]
    qe = [qq * e for qq, e in zip(q, eg)]
    kd = [kk * jnp.exp(gl - g) for kk, gl, g in zip(k, g_last, gc)]
    egl = [jnp.broadcast_to(jnp.exp(gl), (1, HD)) for gl in g_last]
    out = []
    for i in range(n):
        mine = slice(i * HEADS, (i + 1) * HEADS)
        cat = lambda vals: jnp.concatenate(vals[mine], axis=-1)
        out.append((cat([x[:, :HD] for x in uw]), cat([x[:, HD:] for x in uw]), cat(qe), cat(kd),
                    jnp.concatenate([m[None] for m in qk[mine]], axis=0), cat(egl)))
    return out


def _chunk_state_fn(u, w, qe, kd, qk, egl, s):
    ws = [_dot16(a, b) for a, b in zip(w, s)]
    qs = [_dot16(a, b) for a, b in zip(qe, s)]
    v_new = [a - b for a, b in zip(u, ws)]
    o = [a + _dot16(b, c) for a, b, c in zip(qs, qk, v_new)]
    s_new = [a * e + _dot16(b, c, _DIMS["tn"]) for a, e, b, c in zip(s, egl, kd, v_new)]
    return o, s_new


def _mix_fn(o, z, ur, vr, ong, lng, lnb, ws, bst):
    row = lax.broadcasted_iota(jnp.int32, (GCH, GCH), 0)
    col = lax.broadcasted_iota(jnp.int32, (GCH, GCH), 1)
    causal = row >= col
    ug = _gelu(ur)
    vg = _gelu(vr)
    outs_dn, outs_gm = [], []
    for h in range(HEADS):
        sl = slice(h * HD, (h + 1) * HD)
        oh = o[:, sl]
        oh = oh * lax.rsqrt(jnp.mean(oh * oh, axis=-1, keepdims=True) + EPS)
        outs_dn.append(oh * ong * _silu(z[:, sl]))
        vh = vg[:, sl]
        mu = jnp.mean(vh, axis=-1, keepdims=True)
        var = jnp.mean(jnp.square(vh - mu), axis=-1, keepdims=True)
        vn = (vh - mu) * lax.rsqrt(var + EPS) * lng[:, sl] + lnb[:, sl]
        sp = _dot16(jnp.where(causal, ws[h], 0.0), vn) + bst[:, h:h + 1]
        outs_gm.append(ug[:, sl] * sp)
    return jnp.concatenate(outs_dn + outs_gm, axis=-1)


def _loss_fn(h, gain, tgt):
    y = _rms_fn(h, gain)
    return 0.5 * jnp.sum(jnp.mean(jnp.square(y - tgt), axis=-1))


RT = 256


def _rows(n=D):
    return pl.BlockSpec((RT, n), lambda i: (i, 0))


def _whole(shape):
    nd = len(shape)
    return pl.BlockSpec(shape, lambda i: (0,) * nd)


def _rmsnorm(name, h, gain):
    def body(h_ref, g_ref, o_ref):
        o_ref[...] = _rms_fn(h_ref[...], g_ref[...]).astype(BF16)

    return pl.pallas_call(
        body, name=name, grid=(T // RT,), in_specs=[_rows(), _whole((1, D))], out_specs=_rows(),
        out_shape=jax.ShapeDtypeStruct((T, D), BF16), compiler_params=_cparams(("parallel",)),
    )(h, gain)


def _rmsnorm_bwd(name, dhn, h, gain, resid):
    def body(dhn_ref, h_ref, g_ref, r_ref, dh_ref, dh16_ref, dg_ref):
        _, vjp = jax.vjp(_rms_fn, h_ref[...], g_ref[...])
        dh, dg = vjp(dhn_ref[...])
        dh = r_ref[...] + dh
        dh_ref[...] = dh
        dh16_ref[...] = dh.astype(BF16)

        @pl.when(pl.program_id(0) == 0)
        def _():
            dg_ref[...] = dg

        @pl.when(pl.program_id(0) > 0)
        def _():
            dg_ref[...] += dg

    return pl.pallas_call(
        body, name=name, grid=(T // RT,), in_specs=[_rows(), _rows(), _whole((1, D)), _rows()],
        out_specs=[_rows(), _rows(), _whole((1, D))],
        out_shape=[jax.ShapeDtypeStruct((T, D), F32), jax.ShapeDtypeStruct((T, D), BF16),
                   jax.ShapeDtypeStruct((1, D), F32)],
        compiler_params=_cparams(("arbitrary",)),
    )(dhn, h, gain, resid)


def _loss_head(h, gain, tgt):
    def body(h_ref, g_ref, t_ref, l_ref, dh_ref, dh16_ref, dg_ref):
        loss, vjp = jax.vjp(lambda hh, gg: _loss_fn(hh, gg, t_ref[...]), h_ref[...], g_ref[...])
        dh, dg = vjp(jnp.ones((), F32))
        dh_ref[...] = dh
        dh16_ref[...] = dh.astype(BF16)
        lv = jnp.full((1, LANE), loss, F32)

        @pl.when(pl.program_id(0) == 0)
        def _():
            dg_ref[...] = dg
            l_ref[...] = lv

        @pl.when(pl.program_id(0) > 0)
        def _():
            dg_ref[...] += dg
            l_ref[...] += lv

    return pl.pallas_call(
        body, name="loss_head", grid=(T // RT,), in_specs=[_rows(), _whole((1, D)), _rows()],
        out_specs=[_whole((1, LANE)), _rows(), _rows(), _whole((1, D))],
        out_shape=[jax.ShapeDtypeStruct((1, LANE), F32), jax.ShapeDtypeStruct((T, D), F32),
                   jax.ShapeDtypeStruct((T, D), BF16), jax.ShapeDtypeStruct((1, D), F32)],
        compiler_params=_cparams(("arbitrary",)),
    )(h, gain, tgt)


def _prep_flags():
    j = pl.program_id(0)
    qk_scale = jnp.where(j < HEADS, HD ** -0.5, 1.0).astype(F32)
    return qk_scale, j >= 2 * HEADS


def _prep(proj, conv_w):
    def body(x_ref, w_ref, o_ref):
        qk_scale, is_v = _prep_flags()
        o_ref[...] = _prep_fn(x_ref[...], w_ref[...], qk_scale, is_v)

    col = lambda j: (0, j)
    return pl.pallas_call(
        body, name="gdn_prep", grid=(3 * HEADS,),
        in_specs=[pl.BlockSpec((T, HD), col), pl.BlockSpec((4, HD), col)], out_specs=pl.BlockSpec((T, HD), col),
        out_shape=jax.ShapeDtypeStruct((T, 3 * HW), F32), compiler_params=_cparams(("parallel",)),
    )(proj, conv_w)


def _prep_bwd(proj, conv_w, dqkv, dproj):
    def body(x_ref, w_ref, d_ref, _, dx_ref, dw_ref):
        qk_scale, is_v = _prep_flags()
        _, vjp = jax.vjp(lambda x, w: _prep_fn(x, w, qk_scale, is_v), x_ref[...], w_ref[...])
        dx, dw = vjp(d_ref[...])
        dx_ref[...] = dx.astype(BF16)
        dw_ref[...] = dw

    col = lambda j: (0, j)
    return pl.pallas_call(
        body, name="gdn_prep_bwd", grid=(3 * HEADS,),
        in_specs=[pl.BlockSpec((T, HD), col), pl.BlockSpec((4, HD), col), pl.BlockSpec((T, HD), col), ANY],
        out_specs=[pl.BlockSpec((T, HD), col), pl.BlockSpec((4, HD), col)],
        out_shape=[jax.ShapeDtypeStruct((T, NP), BF16), jax.ShapeDtypeStruct((4, 3 * HW), F32)],
        input_output_aliases={3: 0}, compiler_params=_cparams(("parallel",)),
    )(proj, conv_w, dqkv, dproj)


BA_BLK = BA_OFF // LANE


def _gates(proj, a_log, dt_bias):
    def body(x_ref, a_ref, d_ref, o_ref):
        o_ref[...] = _gates_fn(x_ref[...], a_ref[...], d_ref[...])

    return pl.pallas_call(
        body, name="gdn_gates", grid=(1,),
        in_specs=[pl.BlockSpec((T, LANE), lambda i: (0, BA_BLK)), _whole((1, LANE)), _whole((1, LANE))],
        out_specs=_whole((T, LANE)),
        out_shape=jax.ShapeDtypeStruct((T, LANE), F32), compiler_params=_cparams(("arbitrary",)),
    )(proj, a_log, dt_bias)


def _gates_bwd(proj, a_log, dt_bias, dbg, dproj):
    def body(x_ref, a_ref, d_ref, dbg_ref, _, dx_ref, da_ref, dd_ref):
        _, vjp = jax.vjp(_gates_fn, x_ref[...], a_ref[...], d_ref[...])
        dx, da_ref[...], dd_ref[...] = vjp(dbg_ref[...])
        dx_ref[...] = dx.astype(BF16)

    ba = pl.BlockSpec((T, LANE), lambda i: (0, BA_BLK))
    return pl.pallas_call(
        body, name="gdn_gates_bwd", grid=(1,),
        in_specs=[ba, _whole((1, LANE)), _whole((1, LANE)), _whole((T, LANE)), ANY],
        out_specs=[ba, _whole((1, LANE)), _whole((1, LANE))],
        out_shape=[jax.ShapeDtypeStruct((T, NP), BF16), jax.ShapeDtypeStruct((1, LANE), F32),
                   jax.ShapeDtypeStruct((1, LANE), F32)],
        input_output_aliases={4: 0}, compiler_params=_cparams(("arbitrary",)),
    )(proj, a_log, dt_bias, dbg, dproj)


NCK = T // CH
CPS = 2


def _chunk_prep_specs(rev=False):
    at = (lambda n: NCK - 1 - n) if rev else (lambda n: n)
    wide = pl.BlockSpec((CH, HW), lambda n: (at(n), 0))
    return [wide, wide, wide, wide, pl.BlockSpec((HEADS, CH, CH), lambda n: (0, at(n), 0)),
            pl.BlockSpec((None, 1, HW), lambda n: (at(n), 0, 0))]


def _chunk_prep_shapes(dtypes):
    shp = [(T, HW), (T, HW), (T, HW), (T, HW), (HEADS, T, CH), (NCK, 1, HW)]
    return [jax.ShapeDtypeStruct(s, dt) for s, dt in zip(shp, dtypes)]


def _chunk_prep(qkv, bg):
    def body(x_ref, bg_ref, *o_refs):
        rows = [slice(ci * CH, (ci + 1) * CH) for ci in range(CPS)]
        res = _chunk_prep_fn([x_ref[r, :] for r in rows], [bg_ref[r, :] for r in rows])
        for ci, (u, w, qe, kd, qk, egl) in enumerate(res):
            for o_ref, val in zip(o_refs[:4], (u, w, qe, kd)):
                o_ref[rows[ci], :] = val.astype(o_ref.dtype)
            o_refs[4][:, rows[ci], :] = qk.astype(BF16)
            o_refs[5][ci] = egl

    wide = pl.BlockSpec((CPS * CH, HW), lambda n: (n, 0))
    return pl.pallas_call(
        body, name="gdn_chunk_prep", grid=(NCK // CPS,),
        in_specs=[pl.BlockSpec((CPS * CH, 3 * HW), lambda n: (n, 0)), pl.BlockSpec((CPS * CH, LANE), lambda n: (n, 0))],
        out_specs=[wide, wide, wide, wide, pl.BlockSpec((HEADS, CPS * CH, CH), lambda n: (0, n, 0)),
                   pl.BlockSpec((CPS, 1, HW), lambda n: (n, 0, 0))],
        out_shape=_chunk_prep_shapes((F32, BF16, BF16, BF16, BF16, F32)),
        compiler_params=_cparams(("parallel",)),
    )(qkv, bg)


def _chunk_prep_bwd(qkv, bg, cots):
    def body(x_ref, bg_ref, du, dw, dqe, dkd, dqk, degl, dx_ref, dbg_ref):
        rows = [slice(ci * CH, (ci + 1) * CH) for ci in range(CPS)]
        _, vjp = jax.vjp(_chunk_prep_fn, [x_ref[r, :] for r in rows], [bg_ref[r, :] for r in rows])
        dxs, dbgs = vjp([(du[r, :], dw[r, :], dqe[r, :], dkd[r, :], dqk[:, r, :], degl[ci])
                         for ci, r in enumerate(rows)])
        for r, dx, dbg in zip(rows, dxs, dbgs):
            dx_ref[r, :] = dx
            dbg_ref[r, :] = dbg

    wide = pl.BlockSpec((CPS * CH, HW), lambda n: (n, 0))
    return pl.pallas_call(
        body, name="gdn_chunk_prep_bwd", grid=(NCK // CPS,),
        in_specs=[pl.BlockSpec((CPS * CH, 3 * HW), lambda n: (n, 0)), pl.BlockSpec((CPS * CH, LANE), lambda n: (n, 0)),
                  wide, wide, wide, wide, pl.BlockSpec((HEADS, CPS * CH, CH), lambda n: (0, n, 0)),
                  pl.BlockSpec((CPS, 1, HW), lambda n: (n, 0, 0))],
        out_specs=[pl.BlockSpec((CPS * CH, 3 * HW), lambda n: (n, 0)), pl.BlockSpec((CPS * CH, LANE), lambda n: (n, 0))],
        out_shape=[jax.ShapeDtypeStruct((T, 3 * HW), F32), jax.ShapeDtypeStruct((T, LANE), F32)],
        compiler_params=_cparams(("parallel",)),
    )(qkv, bg, *cots)


def _head_args(refs):
    u, w, qe, kd, qk, egl = refs
    sls = [slice(h * HD, (h + 1) * HD) for h in range(HEADS)]
    return ([u[:, sl] for sl in sls], [w[:, sl].astype(F32) for sl in sls], [qe[:, sl].astype(F32) for sl in sls],
            [kd[:, sl].astype(F32) for sl in sls], [qk[h].astype(F32) for h in range(HEADS)],
            [egl[:, sl] for sl in sls])


def _chunk_scan(prep):
    def body(*refs):
        o_ref, sh_ref, s_ref = refs[6:]

        @pl.when(pl.program_id(0) == 0)
        def _():
            s_ref[...] = jnp.zeros_like(s_ref)

        s = [s_ref[h] for h in range(HEADS)]
        for h in range(HEADS):
            sh_ref[h, 0] = s[h]
        o, s_new = _chunk_state_fn(*_head_args(refs[:6]), s)
        for h in range(HEADS):
            o_ref[:, h * HD:(h + 1) * HD] = o[h]
            s_ref[h] = s_new[h]

    return pl.pallas_call(
        body, name="gdn_scan", grid=(NCK,), in_specs=_chunk_prep_specs(),
        out_specs=[pl.BlockSpec((CH, HW), lambda n: (n, 0)), pl.BlockSpec((HEADS, 1, HD, HD), lambda n: (0, n, 0, 0))],
        out_shape=[jax.ShapeDtypeStruct((T, HW), F32), jax.ShapeDtypeStruct((HEADS, NCK, HD, HD), F32)],
        scratch_shapes=[pltpu.VMEM((HEADS, HD, HD), F32)], compiler_params=_cparams(("arbitrary",)),
    )(*prep)


def _chunk_scan_bwd(prep, s_hist, do):
    def body(*refs):
        sh_ref, do_ref = refs[6:8]
        d_refs = refs[8:14]
        ds_ref = refs[14]

        @pl.when(pl.program_id(0) == 0)
        def _():
            ds_ref[...] = jnp.zeros_like(ds_ref)

        sls = [slice(h * HD, (h + 1) * HD) for h in range(HEADS)]
        _, vjp = jax.vjp(_chunk_state_fn, *_head_args(refs[:6]), [sh_ref[h, 0] for h in range(HEADS)])
        du, dw, dqe, dkd, dqk, degl, ds = vjp(([do_ref[:, sl] for sl in sls], [ds_ref[h] for h in range(HEADS)]))
        for h, sl in enumerate(sls):
            for d_ref, val in zip(d_refs[:4], (du, dw, dqe, dkd)):
                d_ref[:, sl] = val[h]
            d_refs[4][h] = dqk[h]
            d_refs[5][:, sl] = degl[h]
            ds_ref[h] = ds[h]

    rev = lambda n: NCK - 1 - n
    return pl.pallas_call(
        body, name="gdn_scan_bwd", grid=(NCK,),
        in_specs=_chunk_prep_specs(rev=True) + [pl.BlockSpec((HEADS, 1, HD, HD), lambda n: (0, rev(n), 0, 0)),
                                                pl.BlockSpec((CH, HW), lambda n: (rev(n), 0))],
        out_specs=_chunk_prep_specs(rev=True), out_shape=_chunk_prep_shapes((F32,) * 6),
        scratch_shapes=[pltpu.VMEM((HEADS, HD, HD), F32)], compiler_params=_cparams(("arbitrary",)),
    )(*prep, s_hist, do)


def _mix_specs():
    pc = lambda c: pl.BlockSpec((GCH, HW), lambda i: (i, c))
    return [pl.BlockSpec((GCH, HW), lambda i: (i, 0)), pc(3), pc(4), pc(5), _whole((1, HD)), _whole((1, HW)),
            _whole((1, HW)), _whole((HEADS, GCH, GCH)), _whole((GCH, LANE))]


def _mix(o, proj, ong, lng, lnb, ws, bst):
    def body(o_ref, z_ref, u_ref, v_ref, ong_ref, lng_ref, lnb_ref, ws_ref, bs_ref, m_ref):
        m_ref[...] = _mix_fn(o_ref[...], z_ref[...], u_ref[...], v_ref[...], ong_ref[...], lng_ref[...],
                             lnb_ref[...], ws_ref[...], bs_ref[...]).astype(BF16)

    return pl.pallas_call(
        body, name="mix", grid=(T // GCH,), in_specs=_mix_specs(),
        out_specs=pl.BlockSpec((GCH, D), lambda i: (i, 0)), out_shape=jax.ShapeDtypeStruct((T, D), BF16),
        compiler_params=_cparams(("parallel",)),
    )(o, proj, proj, proj, ong, lng, lnb, ws, bst)


def _mix_bwd(o, proj, ong, lng, lnb, ws, bst, dmix):
    def body(o_ref, z_ref, u_ref, v_ref, ong_ref, lng_ref, lnb_ref, ws_ref, bs_ref, dm_ref,
             do_ref, dzuv_ref, dong_ref, dlng_ref, dlnb_ref, dws_ref, dbs_ref):
        _, vjp = jax.vjp(_mix_fn, o_ref[...], z_ref[...], u_ref[...], v_ref[...], ong_ref[...], lng_ref[...],
                         lnb_ref[...], ws_ref[...], bs_ref[...])
        do, dz, du, dv, dong, dlng, dlnb, dws, dbs = vjp(dm_ref[...])
        do_ref[...] = do
        dzuv_ref[:, 0:HW] = dz.astype(BF16)
        dzuv_ref[:, HW:2 * HW] = du.astype(BF16)
        dzuv_ref[:, 2 * HW:3 * HW] = dv.astype(BF16)
        acc = [(dong_ref, dong), (dlng_ref, dlng), (dlnb_ref, dlnb), (dws_ref, dws), (dbs_ref, dbs)]

        @pl.when(pl.program_id(0) == 0)
        def _():
            for r, val in acc:
                r[...] = val

        @pl.when(pl.program_id(0) > 0)
        def _():
            for r, val in acc:
                r[...] += val

    shp = lambda *s: jax.ShapeDtypeStruct(s, F32)
    return pl.pallas_call(
        body, name="mix_bwd", grid=(T // GCH,),
        in_specs=_mix_specs() + [pl.BlockSpec((GCH, D), lambda i: (i, 0))],
        out_specs=[pl.BlockSpec((GCH, HW), lambda i: (i, 0)), pl.BlockSpec((GCH, 3 * HW), lambda i: (i, 1)),
                   _whole((1, HD)), _whole((1, HW)), _whole((1, HW)), _whole((HEADS, GCH, GCH)), _whole((GCH, LANE))],
        out_shape=[shp(T, HW), jax.ShapeDtypeStruct((T, NP), BF16), shp(1, HD), shp(1, HW), shp(1, HW),
                   shp(HEADS, GCH, GCH), shp(GCH, LANE)],
        compiler_params=_cparams(("arbitrary",)),
    )(o, proj, proj, proj, ong, lng, lnb, ws, bst, dmix)


def _swiglu_epilogue(accs, _):
    gate, up = accs
    return [gate, up, _silu(gate) * up]


def _swiglu_bwd_epilogue(accs, extras):
    dact = accs[0]
    gate, up = (e.astype(F32) for e in extras)
    sg = _sigmoid(gate)
    return [dact * up * (sg * (1.0 + gate * (1.0 - sg))), dact * (gate * sg)]


def _layer_fwd(h, p):
    hn = _rmsnorm("rms_mix", h, p["norm_mix"])
    proj = _mm("in_proj", "nn", hn[None], [p["w_in"][None]], tm=1024, tn=640, tk=D)[0][0]
    qkv = _prep(proj, p["conv_w"])
    bg = _gates(proj, p["a_log"], p["dt_bias"])
    prep = _chunk_prep(qkv, bg)
    o, s_hist = _chunk_scan(prep)
    if "late" in p:
        p.update(p.pop("late")(o))
    mix = _mix(o, proj, p["o_norm_g"], p["ln_v_g"], p["ln_v_b"], p["w_s"], p["bst"])
    h1 = _mm("out_proj", "nn", mix[None], [p["w_out"]], tm=1024, tn=512, tk=D // NCHIP, resid=h[None], n_n=D,
             b_spec=((None, D // NCHIP, 512), lambda g, i, j, k: (k, 0, j)))[0][0]
    h2n = _rmsnorm("rms_ffn", h1, p["norm_ffn"])
    gate, up, act = _mm("ffn_in", "nn", h2n[None], [p["w_gate"], p["w_up"]], tm=1024, tn=FF_SH, tk=D,
                        out_dtypes=(BF16, BF16, BF16), epilogue=_swiglu_epilogue)
    h2 = _mm("ffn_out", "nn", act, [p["w_down"]], tm=1024, tn=512, tk=FF_SH, reduce_g=True, resid=h1[None])[0][0]
    saved = dict(h=h, hn=hn, proj=proj, qkv=qkv, bg=bg, prep=prep, o=o, s_hist=s_hist, mix=mix, h1=h1, h2n=h2n,
                 gate=gate, up=up, act=act)
    return h2, saved


def _layer_bwd_ffn(dh2, dh2b, p, s, after=()):
    dh2b = dh2b[None]
    dgate, dup = _mm("ffn_out_bwd", "nt", dh2b, [p["w_down"]], tm=1024, tn=FF_SH, tk=D, out_dtypes=(BF16, BF16),
                     extras=(s["gate"], s["up"]), epilogue=_swiglu_bwd_epilogue, after=after)
    dh2n = _mm("ffn_gate_bwd", "nt", dgate, [p["w_gate"]], tm=1024, tn=512, tk=FF_SH, reduce_g=True)[0]
    dh2n = _mm("ffn_up_bwd", "nt", dup, [p["w_up"]], tm=1024, tn=512, tk=FF_SH, reduce_g=True, resid=dh2n)[0][0]
    dh1, dh1b, d_norm_ffn = _rmsnorm_bwd("rms_ffn_bwd", dh2n, s["h1"], p["norm_ffn"], dh2)
    d_w_down = _mm("ffn_wdown_grad", "tn", s["act"], [dh2b], tm=FF_SH, tn=512, tk=1024)[0]
    d_w_gate = _mm("ffn_wgate_grad", "tn", s["h2n"][None], [dgate], tm=512, tn=FF_SH, tk=1024)[0]
    d_w_up = _mm("ffn_wup_grad", "tn", s["h2n"][None], [dup], tm=512, tn=FF_SH, tk=1024)[0]
    return dh1, dh1b, dict(norm_ffn=d_norm_ffn, w_gate=d_w_gate, w_up=d_w_up, w_down=d_w_down)


def _layer_bwd_mixer(dh1, dh1b, p, s, after=()):
    dh1b = dh1b[None]
    dmix = _mm("out_proj_bwd", "nt", dh1b, [p["w_out"]], tm=1024, tn=D // NCHIP, tk=D, n_n=D, after=after,
               b_spec=((None, D // NCHIP, D), lambda g, i, j, k: (j, 0, k)))[0][0]
    d_w_out = _mm("out_proj_wgrad", "tn", s["mix"][None], [dh1b], tm=512, tn=512, tk=1024)[0][0]
    do, dproj, d_ong, d_lng, d_lnb, d_ws, d_bst = _mix_bwd(
        s["o"], s["proj"], p["o_norm_g"], p["ln_v_g"], p["ln_v_b"], p["w_s"], p["bst"], dmix)
    dqkv, dbg = _chunk_prep_bwd(s["qkv"], s["bg"], _chunk_scan_bwd(s["prep"], s["s_hist"], do))
    dproj, d_conv = _prep_bwd(s["proj"], p["conv_w"], dqkv, dproj)
    dproj, d_a_log, d_dt_bias = _gates_bwd(s["proj"], p["a_log"], p["dt_bias"], dbg, dproj)
    dproj = dproj[None]
    dhn = _mm("in_proj_bwd", "nt", dproj, [p["w_in"][None]], tm=1024, tn=512, tk=640)[0][0]
    dh, dhb, d_norm_mix = _rmsnorm_bwd("rms_mix_bwd", dhn, s["h"], p["norm_mix"], dh1)
    d_w_in = _mm("in_proj_wgrad", "tn", s["hn"][None], [dproj], tm=512, tn=640, tk=1024)[0]
    grads = dict(norm_mix=d_norm_mix, w_in=d_w_in, conv_w=d_conv, a_log=d_a_log, dt_bias=d_dt_bias, o_norm_g=d_ong,
                 ln_v_g=d_lng, ln_v_b=d_lnb, w_s=d_ws, bst=d_bst, w_out=d_w_out)
    return dh, dhb, grads


def _lanes(v, off=0):
    return jnp.zeros((1, LANE), F32).at[0, off:off + v.shape[0]].set(v)


def _w_in_pieces():
    regions = [(0, 2048, 0), (2048, 2056, BA_OFF), (2056, IN_DIM, 2048)]
    sh = IN_DIM // NCHIP
    out = []
    for j in range(NCHIP):
        for lo, hi, at in regions:
            a, b = max(lo, j * sh), min(hi, (j + 1) * sh)
            if a < b:
                out.append((j, a - j * sh, at + a - lo, b - a))
    return out


W_IN_PIECES = _w_in_pieces()
WT = 256


def _assemble_w_in(gathered, own, place):
    def body(place_ref, g_ref, own_ref, o_ref):
        o_ref[:, IN_DIM:] = jnp.zeros((WT, NP - IN_DIM), BF16)
        mine = own_ref[...]
        for j, src, dst, width in W_IN_PIECES:
            val = jnp.where(place_ref[0] == j, mine[:, src:src + width], g_ref[j, :, src:src + width])
            o_ref[:, dst:dst + width] = val

    sh = IN_DIM // NCHIP
    return pl.pallas_call(
        body, name="assemble_w_in",
        grid_spec=pltpu.PrefetchScalarGridSpec(
            num_scalar_prefetch=1, grid=(D // WT,),
            in_specs=[pl.BlockSpec((NCHIP, WT, sh), lambda i, place_ref: (0, i, 0)),
                      pl.BlockSpec((WT, sh), lambda i, place_ref: (i, 0))],
            out_specs=pl.BlockSpec((WT, NP), lambda i, place_ref: (i, 0))),
        out_shape=jax.ShapeDtypeStruct((D, NP), BF16), compiler_params=_cparams(("parallel",)),
    )(place, gathered, own)


def _layer_params(l, big, small):
    return dict(
        {k: v for k, v in big.items() if k != "conv_w"},
        conv_w=jnp.concatenate([big["conv_w"][j, l] for j in range(NCHIP)], axis=1),
        norm_mix=small["norm_mix"][l][None], norm_ffn=small["norm_ffn"][l][None],
        a_log=_lanes(small["a_log"][l], HEADS), dt_bias=_lanes(small["dt_bias"][l], HEADS),
        o_norm_g=small["o_norm_g"][l][None], ln_v_g=small["ln_v_g"][l][None], ln_v_b=small["ln_v_b"][l][None],
        w_s=small["w_s"][l],
        bst=jnp.pad(small["b_s"][l].T, ((0, 0), (0, LANE - HEADS))),
    )


def _reference_layout(g):
    return dict(
        w_in=g["w_in"],
        w_out=g["w_out"].reshape(NCHIP, D // NCHIP, D),
        w_gate=g["w_gate"], w_up=g["w_up"], w_down=g["w_down"],
        conv_w=g["conv_w"], norm_mix=g["norm_mix"][0], norm_ffn=g["norm_ffn"][0],
        a_log=g["a_log"][0, HEADS:2 * HEADS], dt_bias=g["dt_bias"][0, HEADS:2 * HEADS],
        o_norm_g=g["o_norm_g"][0], ln_v_g=g["ln_v_g"][0], ln_v_b=g["ln_v_b"][0], w_s=g["w_s"],
        b_s=g["bst"][:, :HEADS].T,
    )


def _forward(x, tgt, layers, norm_final):
    h = x
    saved, params = [], []
    for p in layers:
        p = p(h) if callable(p) else p
        h, s = _layer_fwd(h, p)
        saved.append(s)
        params.append(p)
    return (saved, params) + tuple(_loss_head(h, norm_final, tgt))


def _local_step(x, tgt, layers, norm_final):
    saved, layers, loss, dh, dhb, d_norm_final = _forward(x, tgt, layers, norm_final)
    grads = [None] * DEPTH
    for l in reversed(range(DEPTH)):
        dh1, dh1b, g_ffn = _layer_bwd_ffn(dh, dhb, layers[l], saved[l])
        dh, dhb, g_mix = _layer_bwd_mixer(dh1, dh1b, layers[l], saved[l])
        grads[l] = {**g_ffn, **g_mix}
    return loss, dh, grads, d_norm_final


def _place():
    x, y, c = lax.axis_index("x"), lax.axis_index("y"), lax.axis_index("c")
    return x, y, c, [(1 - x, y), (x, 1 - y), (1 - x, 1 - y)]


def _remote(src, dst, send_sem, recv_sem, to):
    return pltpu.make_async_remote_copy(src_ref=src, dst_ref=dst, send_sem=send_sem, recv_sem=recv_sem,
                                        device_id=to, device_id_type=MESH)


def _comm_call(name, body, ins, out_shape, n_sems, aliases=None):
    return pl.pallas_call(
        body, name=name, in_specs=[ANY] * len(ins), out_specs=[ANY] * len(out_shape), out_shape=out_shape,
        scratch_shapes=[pltpu.SemaphoreType.DMA((n,)) for n in n_sems], input_output_aliases=aliases or {},
        compiler_params=pltpu.CompilerParams(has_side_effects=True),
    )(*ins)


def _half_rows(ref, of_c, dim):
    hr = ref.shape[dim] // 2
    return pl.ds(pl.multiple_of(of_c * hr, BF16_ROWS), hr)


def _gather_plan(whole):
    def plan(srcs, lands):
        x, y, c, others = _place()
        chip = 2 * x + y
        out = []
        for src, land, all_of_it in zip(srcs, lands, whole):
            for ox, oy in others:
                if all_of_it:
                    out.append((src, land.at[chip], (ox, oy, c)))
                else:
                    out.append((src.at[_half_rows(src, c, 0)], land.at[chip, _half_rows(src, c, 0)], (ox, oy, c)))
        return out
    return plan


def _forward_halves(lands):
    n = len(lands)

    def body(*refs):
        outs = refs[n:2 * n]
        send_s, recv_s = refs[2 * n:]
        x, y, c, others = _place()
        sibling = (x, y, 1 - c)
        copies = []
        for a in range(n):
            for k, (ox, oy) in enumerate(others):
                mine = outs[a].at[2 * ox + oy, _half_rows(outs[a], c, 1)]
                copies.append(_remote(mine, mine, send_s.at[3 * a + k], recv_s.at[3 * a + k], sibling))
        for cp in copies:
            cp.start()
        for a in range(n):
            for k, (ox, oy) in enumerate(others):
                landed = outs[a].at[2 * ox + oy, _half_rows(outs[a], 1 - c, 1)]
                _remote(landed, landed, send_s.at[3 * a + k], recv_s.at[3 * a + k], sibling).wait_recv()
        for cp in copies:
            cp.wait_send()

    out_shape = [jax.ShapeDtypeStruct(g.shape, g.dtype) for g in lands]
    return _comm_call("forward_halves", body, lands, out_shape, [3 * n, 3 * n], aliases={a: a for a in range(n)})


def _exchange_halves(gs):
    n = len(gs)

    def body(*refs):
        ins, outs = refs[:n], refs[n:2 * n]
        send_s, recv_s = refs[2 * n:]
        x, y, c, _ = _place()
        copies = []
        for a in range(n):
            hr = ins[a].shape[1] // 2
            theirs = ins[a].at[:, pl.ds(pl.multiple_of((1 - c) * hr, 8), hr)]
            copies.append(_remote(theirs, outs[a], send_s.at[a], recv_s.at[a], (x, y, 1 - c)))
        for cp in copies:
            cp.start()
        for cp in copies:
            cp.wait()

    out_shape = [jax.ShapeDtypeStruct((g.shape[0], g.shape[1] // 2, g.shape[2]), F32) for g in gs]
    return _comm_call("exchange_halves", body, gs, out_shape, [n, n])


def _scatter_partials(ps):
    n = len(ps)

    def body(*refs):
        ins, outs = refs[:n], refs[n:2 * n]
        send_s, recv_s = refs[2 * n:]
        x, y, c, others = _place()
        copies = []
        for a in range(n):
            for k, (ox, oy) in enumerate(others):
                copies.append(_remote(ins[a].at[2 * ox + oy], outs[a].at[k], send_s.at[3 * a + k],
                                      recv_s.at[3 * a + k], (ox, oy, c)))
        for cp in copies:
            cp.start()
        for cp in copies:
            cp.wait()

    out_shape = [jax.ShapeDtypeStruct((3,) + p.shape[1:], p.dtype) for p in ps]
    return _comm_call("scatter_partials", body, ps, out_shape, [3 * n, 3 * n])


HBM_SPEC = pl.BlockSpec(memory_space=pltpu.HBM)
SEM_SPEC = pl.BlockSpec(memory_space=pltpu.SEMAPHORE)
DATAFLOW = pltpu.SideEffectType.DATAFLOW_SIDE_EFFECTING


def _exchange_plan(srcs, lands):
    x, y, c, _ = _place()
    plan = []
    for src, land in zip(srcs, lands):
        hr = src.shape[1] // 2
        plan.append((src.at[:, pl.ds(pl.multiple_of((1 - c) * hr, 8), hr)], land, (x, y, 1 - c)))
    return plan


def _scatter_plan(srcs, lands):
    x, y, c, others = _place()
    return [(src.at[2 * ox + oy], land.at[k], (ox, oy, c))
            for src, land in zip(srcs, lands) for k, (ox, oy) in enumerate(others)]


def _split_start(name, plan, srcs, land_shapes, n_copies, after=()):
    n = len(srcs)
    lands = [pltpu.with_memory_space_constraint(lax.empty(s.shape, s.dtype), pltpu.HBM) for s in land_shapes]
    srcs = [pltpu.with_memory_space_constraint(s, pltpu.HBM) for s in srcs]

    def body(*refs):
        send_s, recv_s = refs[2 * n + len(after)], refs[2 * n + len(after) + 1]
        for i, (src, dst, to) in enumerate(plan(refs[:n], refs[n:2 * n])):
            _remote(src, dst, send_s.at[i], recv_s.at[i], to).start()
        refs[-1][...] = jnp.zeros_like(refs[-1])

    thru = [pltpu.HBM(s.shape, s.dtype) for s in srcs + lands]
    out = pl.pallas_call(
        body, name=name, in_specs=[HBM_SPEC] * (2 * n) + [ANY] * len(after),
        out_specs=[SEM_SPEC, SEM_SPEC] + [HBM_SPEC] * (2 * n) + [pl.BlockSpec(memory_space=pltpu.VMEM)],
        out_shape=[pltpu.SemaphoreType.DMA((n_copies,)), pltpu.SemaphoreType.DMA((n_copies,))] + thru
        + [jax.ShapeDtypeStruct((F32_ROWS, LANE), F32)],
        input_output_aliases={i: 2 + i for i in range(2 * n)},
        compiler_params=pltpu.CompilerParams(has_side_effects=DATAFLOW),
    )(*srcs, *lands, *after)
    return dict(sems=out[:2], srcs=out[2:2 + n], lands=out[2 + n:2 + 2 * n], token=out[-1])


def _split_wait(name, plan, started, after):
    n = len(started["srcs"])

    def body(*refs):
        send_s, recv_s = refs[2 * n], refs[2 * n + 1]
        for i, (src, dst, to) in enumerate(plan(refs[:n], refs[n:2 * n])):
            cp = _remote(src, dst, send_s.at[i], recv_s.at[i], to)
            cp.wait_send()
            cp.wait_recv()

    arrs = list(started["srcs"]) + list(started["lands"])
    out = pl.pallas_call(
        body, name=name, in_specs=[HBM_SPEC] * (2 * n) + [SEM_SPEC, SEM_SPEC, ANY],
        out_specs=[HBM_SPEC] * (2 * n), out_shape=[pltpu.HBM(s.shape, s.dtype) for s in arrs],
        input_output_aliases={i: i for i in range(2 * n)},
        compiler_params=pltpu.CompilerParams(has_side_effects=DATAFLOW),
    )(*arrs, *started["sems"], after)
    return out[:n], out[n:]


def _join_halves(rs):
    n = len(rs)

    def body(*refs):
        outs = refs[n:2 * n]
        send_s, recv_s = refs[2 * n:]
        x, y, c, _ = _place()
        sibling = (x, y, 1 - c)

        def half(a, of_c):
            hr = outs[a].shape[1] // 2
            return outs[a].at[:, pl.ds(pl.multiple_of(of_c * hr, 8), hr)]

        copies = [_remote(half(a, c), half(a, c), send_s.at[a], recv_s.at[a], sibling) for a in range(n)]
        for cp in copies:
            cp.start()
        for a in range(n):
            landed = half(a, 1 - c)
            _remote(landed, landed, send_s.at[a], recv_s.at[a], sibling).wait_recv()
        for cp in copies:
            cp.wait_send()

    out_shape = [jax.ShapeDtypeStruct(r.shape, r.dtype) for r in rs]
    return _comm_call("join_halves", body, rs, out_shape, [n, n], aliases={a: a for a in range(n)})


NDEV = 8


def _allreduce_small(buf):
    r = buf.shape[0]

    def body(in_ref, out_ref, gath, send_s, recv_s):
        x, y, c, _ = _place()
        me = 4 * x + 2 * y + c
        copies = []
        for rel in range(1, NDEV):
            px = 1 - x if rel & 4 else x
            py = 1 - y if rel & 2 else y
            pc = 1 - c if rel & 1 else c
            copies.append((_remote(in_ref, gath.at[me], send_s.at[rel - 1], recv_s.at[rel - 1], (px, py, pc)),
                           4 * px + 2 * py + pc))
        for cp, _ in copies:
            cp.start()
        gath[me] = in_ref[...]
        for rel, (cp, peer) in enumerate(copies):
            landed = gath.at[peer]
            _remote(landed, landed, send_s.at[rel], recv_s.at[rel], (x, y, c)).wait_recv()
        for cp, _ in copies:
            cp.wait_send()
        total = gath[0]
        for d in range(1, NDEV):
            total = total + gath[d]
        out_ref[...] = total

    vm = pl.BlockSpec(memory_space=pltpu.VMEM)
    return pl.pallas_call(
        body, name="allreduce_small", in_specs=[vm], out_specs=vm, out_shape=jax.ShapeDtypeStruct((r, LANE), F32),
        scratch_shapes=[pltpu.VMEM((NDEV, r, LANE), F32), pltpu.SemaphoreType.DMA((NDEV - 1,)),
                        pltpu.SemaphoreType.DMA((NDEV - 1,))],
        compiler_params=pltpu.CompilerParams(has_side_effects=True, vmem_limit_bytes=VMEM_LIMIT),
    )(buf)


MAX_ROW_TILE = 512
BF16_ROWS = 16


def _row_tile(rows):
    for t in range(min(rows, MAX_ROW_TILE) // BF16_ROWS * BF16_ROWS, 0, -BF16_ROWS):
        if rows % t == 0:
            return t
    raise ValueError(rows)


def _sum_halves(g, theirs, c_arr):
    nch, rows, cols = g.shape
    hr = rows // 2
    tr = _row_tile(hr)

    def body(c_ref, g_ref, t_ref, o_ref, ob_ref):
        s = g_ref[...] + t_ref[...]
        o_ref[...] = s
        ob_ref[...] = s.astype(BF16)

    blk = pl.BlockSpec((None, tr, cols), lambda j, i, c_ref: (j, i, 0))
    return pl.pallas_call(
        body, name="sum_halves",
        grid_spec=pltpu.PrefetchScalarGridSpec(
            num_scalar_prefetch=1, grid=(nch, hr // tr),
            in_specs=[pl.BlockSpec((None, None, tr, cols), lambda j, i, c_ref: (j, c_ref[0], i, 0)), blk],
            out_specs=[blk, blk]),
        out_shape=[jax.ShapeDtypeStruct((nch, hr, cols), F32), jax.ShapeDtypeStruct((nch, hr, cols), BF16)],
        compiler_params=_cparams(("parallel", "parallel")),
    )(c_arr, g.reshape(nch, 2, hr, cols), theirs)


def _sum_halves_w_in(g, theirs, c_arr):
    hr = D // 2
    sh = IN_DIM // NCHIP

    def body(c_ref, g_ref, t_ref, o_ref, ob_ref):
        s = g_ref[...] + t_ref[...]
        for j, dst, src, width in W_IN_PIECES:
            o_ref[j, :, dst:dst + width] = s[:, src:src + width]
            ob_ref[j, :, dst:dst + width] = s[:, src:src + width].astype(BF16)

    out = pl.BlockSpec((NCHIP, WT, sh), lambda i, c_ref: (0, i, 0))
    return pl.pallas_call(
        body, name="sum_halves_w_in",
        grid_spec=pltpu.PrefetchScalarGridSpec(
            num_scalar_prefetch=1, grid=(hr // WT,),
            in_specs=[pl.BlockSpec((None, WT, NP), lambda i, c_ref: (c_ref[0], i, 0)),
                      pl.BlockSpec((None, WT, NP), lambda i, c_ref: (0, i, 0))],
            out_specs=[out, out]),
        out_shape=[jax.ShapeDtypeStruct((NCHIP, hr, sh), F32), jax.ShapeDtypeStruct((NCHIP, hr, sh), BF16)],
        compiler_params=_cparams(("parallel",)),
    )(c_arr, g.reshape(2, hr, NP), theirs)


def _sum_chips(p, q, place, l, into=None):
    _, rows, cols = p.shape
    tr = _row_tile(rows)
    steps = rows // tr

    def body(place_ref, p_ref, q0, q1, q2, *rest):
        rest[-1][...] = ((p_ref[...] + q0[...].astype(F32)) + q1[...].astype(F32)) + q2[...].astype(F32)

    qs = lambda k: pl.BlockSpec((None, tr, cols), lambda i, place_ref: (k, i, 0))
    return pl.pallas_call(
        body, name="sum_chips",
        grid_spec=pltpu.PrefetchScalarGridSpec(
            num_scalar_prefetch=1, grid=(steps,),
            in_specs=[pl.BlockSpec((None, tr, cols), lambda i, place_ref: (place_ref[0], i, 0)), qs(0), qs(1), qs(2)]
            + ([ANY] if into is not None else []),
            out_specs=pl.BlockSpec((None, tr, cols), lambda i, place_ref: (l, place_ref[1] * steps + i, 0))),
        out_shape=jax.ShapeDtypeStruct((DEPTH, 2 * rows, cols), F32),
        input_output_aliases={5: 0} if into is not None else {},
        compiler_params=_cparams(("parallel",)),
    )(place, p, q, q, q, *([into] if into is not None else []))


def _adamw(w, g, m, v):
    layers, rows, cols = w.shape
    tr = _row_tile(rows)

    def body(w_ref, g_ref, m_ref, v_ref, d_ref, nm_ref, nv_ref):
        gv = g_ref[...]
        nm = ADAM_B1 * m_ref[...] + (1.0 - ADAM_B1) * gv
        nv = ADAM_B2 * v_ref[...] + (1.0 - ADAM_B2) * jnp.square(gv)
        m_hat = nm / (1.0 - ADAM_B1 ** ADAM_STEP)
        v_hat = nv / (1.0 - ADAM_B2 ** ADAM_STEP)
        d_ref[...] = -ADAM_LR * (m_hat / (jnp.sqrt(v_hat) + ADAM_EPS) + ADAM_WD * w_ref[...])
        nm_ref[...] = nm
        nv_ref[...] = nv

    blk = pl.BlockSpec((None, tr, cols), lambda l, i: (l, i, 0))
    return pl.pallas_call(
        body, name="adamw", grid=(layers, rows // tr), in_specs=[blk] * 4, out_specs=[blk] * 3,
        out_shape=[jax.ShapeDtypeStruct(w.shape, F32)] * 3, compiler_params=_cparams(("parallel", "parallel")),
    )(w, g, m, v)


BIG = ("w_in", "w_out", "w_gate", "w_up", "w_down")
SMALL = ("norm_mix", "a_log", "dt_bias", "o_norm_g", "ln_v_g", "ln_v_b", "w_s", "b_s", "norm_ffn", "norm_final")
ORDER = ("norm_mix", "w_in", "conv_w", "a_log", "dt_bias", "o_norm_g", "ln_v_g", "ln_v_b", "w_s", "b_s", "w_out",
         "norm_ffn", "w_gate", "w_up", "w_down", "norm_final")


F32_ROWS = 8
PACK_ROWS = 128


def _lane_rows(size):
    return -(-size // (F32_ROWS * LANE)) * F32_ROWS


def _pack(arrs):
    parts = [jnp.pad(a.reshape(-1), (0, _lane_rows(a.size) * LANE - a.size)).reshape(-1, LANE) for a in arrs]
    rows = sum(p.shape[0] for p in parts)
    if rows % PACK_ROWS:
        parts.append(jnp.zeros((-rows % PACK_ROWS, LANE), F32))
    return jnp.concatenate(parts, axis=0)


def _unpack(buf, like):
    out, row = [], 0
    for a in like:
        n = _lane_rows(a.size)
        out.append(buf[row:row + n].reshape(-1)[:a.size].reshape(a.shape))
        row += n
    return out


def kernel(x, norm_mix, w_in, conv_w, a_log, dt_bias, o_norm_g, ln_v_g, ln_v_b, w_s, b_s, w_out, norm_ffn, w_gate, w_up, w_down, norm_final, loss_target, m_norm_mix, m_w_in, m_conv_w, m_a_log, m_dt_bias, m_o_norm_g, m_ln_v_g, m_ln_v_b, m_w_s, m_b_s, m_w_out, m_norm_ffn, m_w_gate, m_w_up, m_w_down, m_norm_final, v_norm_mix, v_w_in, v_conv_w, v_a_log, v_dt_bias, v_o_norm_g, v_ln_v_g, v_ln_v_b, v_w_s, v_b_s, v_w_out, v_norm_ffn, v_w_gate, v_w_up, v_w_down, v_norm_final):
    w = dict(norm_mix=norm_mix, w_in=w_in, conv_w=conv_w, a_log=a_log, dt_bias=dt_bias, o_norm_g=o_norm_g,
             ln_v_g=ln_v_g, ln_v_b=ln_v_b, w_s=w_s, b_s=b_s, w_out=w_out, norm_ffn=norm_ffn, w_gate=w_gate, w_up=w_up,
             w_down=w_down, norm_final=norm_final)
    m = dict(norm_mix=m_norm_mix, w_in=m_w_in, conv_w=m_conv_w, a_log=m_a_log, dt_bias=m_dt_bias, o_norm_g=m_o_norm_g,
             ln_v_g=m_ln_v_g, ln_v_b=m_ln_v_b, w_s=m_w_s, b_s=m_b_s, w_out=m_w_out, norm_ffn=m_norm_ffn,
             w_gate=m_w_gate, w_up=m_w_up, w_down=m_w_down, norm_final=m_norm_final)
    v = dict(norm_mix=v_norm_mix, w_in=v_w_in, conv_w=v_conv_w, a_log=v_a_log, dt_bias=v_dt_bias, o_norm_g=v_o_norm_g,
             ln_v_g=v_ln_v_g, ln_v_b=v_ln_v_b, w_s=v_w_s, b_s=v_b_s, w_out=v_w_out, norm_ffn=v_norm_ffn,
             w_gate=v_w_gate, w_up=v_w_up, w_down=v_w_down, norm_final=v_norm_final)
    chip = 2 * lax.axis_index("x") + lax.axis_index("y")
    place = jnp.stack([chip, lax.axis_index("c")]).astype(jnp.int32)
    c_arr = place[1:]

    own = {n: [w[n][l].astype(BF16) for l in range(DEPTH)] for n in BIG}
    by_chip = lambda a: jax.ShapeDtypeStruct((NCHIP,) + a.shape, a.dtype)

    def start(name, srcs, whole, after=()):
        return _split_start(name, _gather_plan(whole), srcs, [by_chip(a) for a in srcs], 3 * len(srcs), after)

    def finish(name, started, whole, after):
        srcs, lands = _split_wait(name, _gather_plan(whole), started, after)
        passed = iter(_forward_halves([g for g, all_of_it in zip(lands, whole) if not all_of_it]))
        lands = [g if all_of_it else next(passed) for g, all_of_it in zip(lands, whole)]
        return srcs, [lax.dynamic_update_index_in_dim(g, o, chip, 0) for g, o in zip(lands, srcs)]

    ffn = BIG[1:]
    first = start("gather_first_start", [own["w_in"][0], conv_w], [False, True])
    early = start("gather_early_start", [own[n][0] for n in ffn], [False] * len(ffn), [first["token"]])
    later = start("gather_later_start", [own[n][1] for n in BIG], [False] * len(BIG), [early["token"]])
    (own_w_in, _), (w_in_by_chip, conv_by_chip) = finish("gather_first_wait", first, [False, True], later["token"])

    def late(after):
        return dict(zip(ffn, finish("gather_early_wait", early, [False] * len(ffn), after)[1]))

    layer0 = _layer_params(0, dict(w_in=_assemble_w_in(w_in_by_chip, own_w_in, place), conv_w=conv_by_chip, late=late), w)

    def layer1(after):
        srcs, by = finish("gather_later_wait", later, [False] * len(BIG), after)
        big = dict(zip(ffn, by[1:]), w_in=_assemble_w_in(by[0], srcs[0], place), conv_w=conv_by_chip)
        return _layer_params(1, big, w)

    saved, layers, loss_lanes, dh, dhb, d_norm_final = _forward(x[0], loss_target[0], [layer0, layer1],
                                                                 norm_final[None])
    loss = lax.psum(loss_lanes[0, 0], ("x", "y", "c"))

    def sum_halves(mine, theirs):
        return [(_sum_halves_w_in if n == "w_in" else _sum_halves)(g, t, c_arr) for n, g, t in zip(BIG, mine, theirs)]

    def half_shapes(mine):
        return [jax.ShapeDtypeStruct((g.shape[0], g.shape[1] // 2, g.shape[2]), F32) for g in mine]

    gl, sums, arrived = [None] * DEPTH, [None] * DEPTH, [None] * DEPTH
    last = DEPTH - 1
    dh1, dh1b, g_ffn = _layer_bwd_ffn(dh, dhb, layers[last], saved[last])
    dh, dhb, g_mix = _layer_bwd_mixer(dh1, dh1b, layers[last], saved[last])
    gl[last] = _reference_layout({**g_ffn, **g_mix})
    mine = [gl[last][n] for n in BIG]
    exchange = _split_start("exchange_start", _exchange_plan, mine, half_shapes(mine), len(BIG))
    dh1, dh1b, g_ffn = _layer_bwd_ffn(dh, dhb, layers[0], saved[0], after=[exchange["token"]])
    sums[last] = sum_halves(*_split_wait("exchange_wait", _exchange_plan, exchange, dh1))
    partial = [s16 for _, s16 in sums[last]]
    scatter = _split_start("scatter_start", _scatter_plan, partial,
                           [jax.ShapeDtypeStruct((3,) + p.shape[1:], p.dtype) for p in partial], 3 * len(BIG))
    dx, _, g_mix = _layer_bwd_mixer(dh1, dh1b, layers[0], saved[0], after=[scatter["token"]])
    arrived[last] = _split_wait("scatter_wait", _scatter_plan, scatter, dx)[1]
    gl[0] = _reference_layout({**g_ffn, **g_mix})
    mine = [gl[0][n] for n in BIG]
    sums[0] = sum_halves(mine, _exchange_halves(mine))
    arrived[0] = _scatter_partials([s16 for _, s16 in sums[0]])
    reduced = []
    for a in range(len(BIG)):
        buf = None
        for l in range(DEPTH):
            buf = _sum_chips(sums[l][a][0], arrived[l][a], place, l, into=buf)
        reduced.append(buf)
    g_out = dict(zip(BIG, _join_halves(reduced)))

    small_g = [jnp.stack([gl[l][n] for l in range(DEPTH)]) for n in SMALL[:-1]] + [d_norm_final[0]]
    conv_g = jnp.stack([gl[l]["conv_w"] for l in range(DEPTH)])
    total = _allreduce_small(_pack(small_g + [conv_g]))
    *small_r, conv_r = _unpack(total, small_g + [conv_g])
    g_out.update(zip(SMALL, small_r))
    g_out["conv_w"] = lax.dynamic_slice_in_dim(conv_r, chip * conv_w.shape[2], conv_w.shape[2], axis=2)

    delta, new_m, new_v = {}, {}, {}
    for n in BIG:
        delta[n], new_m[n], new_v[n] = _adamw(w[n], g_out[n], m[n], v[n])
    rest = SMALL + ("conv_w",)
    like = [w[n] for n in rest]
    d, nm, nv = _adamw(*[_pack([src[n] for n in rest])[None] for src in (w, g_out, m, v)])
    for dst, buf in ((delta, d), (new_m, nm), (new_v, nv)):
        dst.update(zip(rest, _unpack(buf[0], like)))

    return (loss, dx[None], *[g_out[n] for n in ORDER], *[delta[n] for n in ORDER], *[new_m[n] for n in ORDER],
            *[new_v[n] for n in ORDER])
```

```python
import functools

import jax
import jax.numpy as jnp
from jax import lax
from jax.experimental import pallas as pl
from jax.experimental.pallas import tpu as pltpu

F32 = jnp.float32
BF16 = jnp.bfloat16
MESH = pl.DeviceIdType.MESH
ANY = pl.BlockSpec(memory_space=pl.ANY)
HIGHEST = lax.Precision.HIGHEST

T = 2048
D = 1024
DEPTH = 2
NCHIP = 4
HEADS = 4
HD = 128
HW = HEADS * HD
CH = 64
GCH = 128
IN_DIM = 3080
NP = 3200
BA_OFF = 3072
FF_SH = 704
EPS = 1e-6
LANE = 128
VMEM_LIMIT = 56 * 1024 * 1024

ADAM_LR = 0.001
ADAM_B1 = 0.9
ADAM_B2 = 0.999
ADAM_EPS = 1e-08
ADAM_WD = 0.01
ADAM_STEP = 10


def _cparams(sem=None):
    return pltpu.CompilerParams(dimension_semantics=sem, vmem_limit_bytes=VMEM_LIMIT)


_DIMS = {"nn": (((1,), (0,)), ((), ())), "nt": (((1,), (1,)), ((), ())), "tn": (((0,), (0,)), ((), ()))}


def _mm(name, mode, a, bs, *, tm, tn, tk, out_dtypes=(F32,), reduce_g=False, resid=None, extras=(), epilogue=None,
        b_spec=None, n_n=None, after=()):
    nb = len(bs)
    ga = a.shape[0]
    gbs = [1 if b_spec is not None else b.shape[0] for b in bs]
    g_n = max([ga] + gbs)
    if mode == "tn":
        k_n, m_n = a.shape[1:]
    else:
        m_n, k_n = a.shape[1:]
    if n_n is None:
        n_n = bs[0].shape[1] if mode == "nt" else bs[0].shape[2]
    assert m_n % tm == 0 and n_n % tn == 0 and k_n % tk == 0, (name, m_n, n_n, k_n)
    mi, nj, kk = m_n // tm, n_n // tn, k_n // tk
    if reduce_g:
        grid = (mi, nj, g_n, kk)
        ids = lambda i, j, g, k: (g, i, j, k)
        n_red = g_n * kk
        red_idx = lambda: pl.program_id(2) * kk + pl.program_id(3)
        sem = ("parallel", "parallel", "arbitrary", "arbitrary")
    else:
        grid = (g_n, mi, nj, kk)
        ids = lambda g, i, j, k: (g, i, j, k)
        n_red = kk
        red_idx = lambda: pl.program_id(3)
        sem = ("parallel", "parallel", "parallel", "arbitrary")

    def pick(gsz, g):
        return g if gsz > 1 else 0

    def a_map(*p):
        g, i, j, k = ids(*p)
        return (pick(ga, g), k, i) if mode == "tn" else (pick(ga, g), i, k)

    def b_map(gsz):
        def f(*p):
            g, i, j, k = ids(*p)
            if b_spec is not None:
                return b_spec[1](g, i, j, k)
            return (pick(gsz, g), j, k) if mode == "nt" else (pick(gsz, g), k, j)
        return f

    def o_map(gsz):
        def f(*p):
            g, i, j, k = ids(*p)
            return (0 if reduce_g else pick(gsz, g), i, j)
        return f

    a_spec = pl.BlockSpec((None, tk, tm) if mode == "tn" else (None, tm, tk), a_map)
    b_block = b_spec[0] if b_spec is not None else ((None, tn, tk) if mode == "nt" else (None, tk, tn))
    b_specs = [pl.BlockSpec(b_block, b_map(gs)) for gs in gbs]
    x_specs = [pl.BlockSpec((None, tm, tn), o_map(e.shape[0])) for e in extras]
    r_specs = [pl.BlockSpec((None, tm, tn), o_map(resid.shape[0]))] if resid is not None else []
    g_out = 1 if reduce_g else g_n
    out_shape = [jax.ShapeDtypeStruct((g_out, m_n, n_n), dt) for dt in out_dtypes]
    out_specs = [pl.BlockSpec((None, tm, tn), o_map(g_out)) for _ in out_dtypes]
    nx, nr, no = len(extras), len(r_specs), len(out_dtypes)
    n_in = 1 + nb + nx + nr + len(after)
    dims = _DIMS[mode]

    def body(*refs):
        a_ref = refs[0]
        b_refs = refs[1:1 + nb]
        x_refs = refs[1 + nb:1 + nb + nx]
        r_refs = refs[1 + nb + nx:1 + nb + nx + nr]
        o_refs = refs[n_in:n_in + no]
        acc_refs = refs[n_in + no:]
        r = red_idx()
        av = a_ref[...]
        for b_ref, acc in zip(b_refs, acc_refs):
            p = lax.dot_general(av, b_ref[...], dims, preferred_element_type=F32)

            @pl.when(r == 0)
            def _():
                acc[...] = p

            @pl.when(r > 0)
            def _():
                acc[...] += p

        @pl.when(r == n_red - 1)
        def _():
            accs = [acc[...] for acc in acc_refs]
            if r_refs:
                accs[0] = accs[0] + r_refs[0][...]
            outs = epilogue(accs, [x[...] for x in x_refs]) if epilogue is not None else accs
            for o_ref, o in zip(o_refs, outs):
                o_ref[...] = o.astype(o_ref.dtype)

    return pl.pallas_call(
        body, name=name, grid=grid,
        in_specs=[a_spec] + b_specs + x_specs + r_specs + [ANY] * len(after),
        out_specs=out_specs, out_shape=out_shape,
        scratch_shapes=[pltpu.VMEM((tm, tn), F32) for _ in range(nb)],
        compiler_params=_cparams(sem),
    )(a, *bs, *extras, *([resid] if resid is not None else []), *after)


def _sigmoid(x):
    return 1.0 / (1.0 + jnp.exp(-x))


def _silu(x):
    return x * _sigmoid(x)


def _gelu(x):
    return 0.5 * x * (1.0 + jnp.tanh(0.7978845608028654 * (x + 0.044715 * (x * x * x))))


def _rms_fn(h, gain):
    return h * lax.rsqrt(jnp.mean(h * h, axis=-1, keepdims=True) + EPS) * gain


def _shift_impl(x, s):
    n = x.shape[0]
    rolled = pltpu.roll(x, s % n, 0)
    row = lax.broadcasted_iota(jnp.int32, x.shape, 0)
    return jnp.where((row >= s) & (row < n + s), rolled, 0.0)


@functools.partial(jax.custom_vjp, nondiff_argnums=(1,))
def _shift(x, s):
    return _shift_impl(x, s)


def _shift_fwd(x, s):
    return _shift_impl(x, s), None


def _shift_bwd(s, _, g):
    return (_shift_impl(g, -s),)


_shift.defvjp(_shift_fwd, _shift_bwd)


def _prep_fn(x, w, qk_scale, is_v):
    y = x * w[3:4, :]
    for i in range(3):
        y = y + _shift(x, 3 - i) * w[i:i + 1, :]
    y = _silu(y)
    nrm = lax.rsqrt(jnp.sum(y * y, axis=-1, keepdims=True) + EPS) * qk_scale
    return y * jnp.where(is_v, 1.0, nrm)


def _softplus(x):
    return jnp.maximum(x, 0.0) + jnp.log(1.0 + jnp.exp(-jnp.abs(x)))


def _gates_fn(ba, a_log, dt_bias):
    lane = lax.broadcasted_iota(jnp.int32, ba.shape, 1)
    beta = _sigmoid(ba)
    g = -jnp.exp(a_log) * _softplus(ba + dt_bias)
    return jnp.where(lane < HEADS, beta, g)


def _dot16(a, b, dims=_DIMS["nn"]):
    return lax.dot_general(a.astype(BF16), b.astype(BF16), dims, preferred_element_type=F32)


def _dot32(a, b):
    return jnp.dot(a, b, preferred_element_type=F32, precision=HIGHEST)


def _dot3(a, b, dims=_DIMS["nn"]):
    return lax.dot_general(a, b, dims, preferred_element_type=F32, precision=lax.Precision.HIGH)


def _tri_inverses(mats):
    row = lax.broadcasted_iota(jnp.int32, (CH, CH), 0)
    col = lax.broadcasted_iota(jnp.int32, (CH, CH), 1)
    eye = (row == col).astype(F32)
    ts = [eye - a for a in mats]
    ps = list(mats)
    for _ in range(5):
        ps = [_dot3(p, p) for p in ps]
        ts = [t + _dot3(t, p) for t, p in zip(ts, ps)]
    return ts


@jax.custom_vjp
def _tri_solves(mats, rhs):
    return [_dot3(t, b) for t, b in zip(_tri_inverses(mats), rhs)]


def _tri_solves_fwd(mats, rhs):
    ts = _tri_inverses(mats)
    xs = [_dot3(t, b) for t, b in zip(ts, rhs)]
    return xs, (ts, xs)


def _tri_solves_bwd(res, dxs):
    ts, xs = res
    dbs = [_dot3(t, dx, _DIMS["tn"]) for t, dx in zip(ts, dxs)]
    return [-_dot3(db, x, _DIMS["nt"]) for db, x in zip(dbs, xs)], dbs


_tri_solves.defvjp(_tri_solves_fwd, _tri_solves_bwd)


def _chunk_prep_fn(xs, bgs):
    row = lax.broadcasted_iota(jnp.int32, (CH, CH), 0)
    col = lax.broadcasted_iota(jnp.int32, (CH, CH), 1)
    incl = row >= col
    strict = row > col
    lmat = incl.astype(F32)
    n = len(xs)
    items = [(i, h) for i in range(n) for h in range(HEADS)]
    part = lambda i, h, c: xs[i][:, c * HW + h * HD:c * HW + (h + 1) * HD]
    q = [part(i, h, 0) for i, h in items]
    k = [part(i, h, 1) for i, h in items]
    v = [part(i, h, 2) for i, h in items]
    beta = [bgs[i][:, h:h + 1] for i, h in items]
    gc_all = [_dot32(lmat, bg) for bg in bgs]
    gc = [gc_all[i][:, HEADS + h:HEADS + h + 1] for i, h in items]
    gmat = [jnp.where(strict, jnp.broadcast_to(bgs[i][:, HEADS + h:HEADS + h + 1], (CH, CH)), 0.0) for i, h in items]
    diff = [_dot3(lmat, m) for m in gmat]
    decay = [jnp.where(incl, jnp.exp(jnp.where(incl, d, 0.0)), 0.0) for d in diff]
    k_beta = [kk * b for kk, b in zip(k, beta)]
    kk_t = [_dot16(kb, kk, _DIMS["nt"]) for kb, kk in zip(k_beta, k)]
    qk_t = [_dot16(qq, kk, _DIMS["nt"]) for qq, kk in zip(q, k)]
    a = [jnp.where(strict, m * d, 0.0) for m, d in zip(kk_t, decay)]
    eg = [jnp.exp(g) for g in gc]
    rhs = [jnp.concatenate([vv * b, kb * e], axis=-1) for vv, b, kb, e in zip(v, beta, k_beta, eg)]
    uw = _tri_solves(a, rhs)
    qk = [m * d for m, d in zip(qk_t, decay)]
    g_last = [g[CH - 1:CH, :] for g in gc]
    qe = [qq * e for qq, e in zip(q, eg)]
    kd = [kk * jnp.exp(gl - g) for kk, gl, g in zip(k, g_last, gc)]
    egl = [jnp.broadcast_to(jnp.exp(gl), (1, HD)) for gl in g_last]
    out = []
    for i in range(n):
        mine = slice(i * HEADS, (i + 1) * HEADS)
        cat = lambda vals: jnp.concatenate(vals[mine], axis=-1)
        out.append((cat([x[:, :HD] for x in uw]), cat([x[:, HD:] for x in uw]), cat(qe), cat(kd),
                    jnp.concatenate([m[None] for m in qk[mine]], axis=0), cat(egl)))
    return out


def _chunk_state_fn(u, w, qe, kd, qk, egl, s):
    ws = [_dot16(a, b) for a, b in zip(w, s)]
    qs = [_dot16(a, b) for a, b in zip(qe, s)]
    v_new = [a - b for a, b in zip(u, ws)]
    o = [a + _dot16(b, c) for a, b, c in zip(qs, qk, v_new)]
    s_new = [a * e + _dot16(b, c, _DIMS["tn"]) for a, e, b, c in zip(s, egl, kd, v_new)]
    return o, s_new


def _mix_fn(o, z, ur, vr, ong, lng, lnb, ws, bst):
    row = lax.broadcasted_iota(jnp.int32, (GCH, GCH), 0)
    col = lax.broadcasted_iota(jnp.int32, (GCH, GCH), 1)
    causal = row >= col
    ug = _gelu(ur)
    vg = _gelu(vr)
    outs_dn, outs_gm = [], []
    for h in range(HEADS):
        sl = slice(h * HD, (h + 1) * HD)
        oh = o[:, sl]
        oh = oh * lax.rsqrt(jnp.mean(oh * oh, axis=-1, keepdims=True) + EPS)
        outs_dn.append(oh * ong * _silu(z[:, sl]))
        vh = vg[:, sl]
        mu = jnp.mean(vh, axis=-1, keepdims=True)
        var = jnp.mean(jnp.square(vh - mu), axis=-1, keepdims=True)
        vn = (vh - mu) * lax.rsqrt(var + EPS) * lng[:, sl] + lnb[:, sl]
        sp = _dot16(jnp.where(causal, ws[h], 0.0), vn) + bst[:, h:h + 1]
        outs_gm.append(ug[:, sl] * sp)
    return jnp.concatenate(outs_dn + outs_gm, axis=-1)


def _loss_fn(h, gain, tgt):
    y = _rms_fn(h, gain)
    return 0.5 * jnp.sum(jnp.mean(jnp.square(y - tgt), axis=-1))


RT = 256


def _rows(n=D):
    return pl.BlockSpec((RT, n), lambda i: (i, 0))


def _whole(shape):
    nd = len(shape)
    return pl.BlockSpec(shape, lambda i: (0,) * nd)


def _rmsnorm(name, h, gain):
    def body(h_ref, g_ref, o_ref):
        o_ref[...] = _rms_fn(h_ref[...], g_ref[...]).astype(BF16)

    return pl.pallas_call(
        body, name=name, grid=(T // RT,), in_specs=[_rows(), _whole((1, D))], out_specs=_rows(),
        out_shape=jax.ShapeDtypeStruct((T, D), BF16), compiler_params=_cparams(("parallel",)),
    )(h, gain)


def _rmsnorm_bwd(name, dhn, h, gain, resid):
    def body(dhn_ref, h_ref, g_ref, r_ref, dh_ref, dh16_ref, dg_ref):
        _, vjp = jax.vjp(_rms_fn, h_ref[...], g_ref[...])
        dh, dg = vjp(dhn_ref[...])
        dh = r_ref[...] + dh
        dh_ref[...] = dh
        dh16_ref[...] = dh.astype(BF16)

        @pl.when(pl.program_id(0) == 0)
        def _():
            dg_ref[...] = dg

        @pl.when(pl.program_id(0) > 0)
        def _():
            dg_ref[...] += dg

    return pl.pallas_call(
        body, name=name, grid=(T // RT,), in_specs=[_rows(), _rows(), _whole((1, D)), _rows()],
        out_specs=[_rows(), _rows(), _whole((1, D))],
        out_shape=[jax.ShapeDtypeStruct((T, D), F32), jax.ShapeDtypeStruct((T, D), BF16),
                   jax.ShapeDtypeStruct((1, D), F32)],
        compiler_params=_cparams(("arbitrary",)),
    )(dhn, h, gain, resid)


def _loss_head(h, gain, tgt):
    def body(h_ref, g_ref, t_ref, l_ref, dh_ref, dh16_ref, dg_ref):
        loss, vjp = jax.vjp(lambda hh, gg: _loss_fn(hh, gg, t_ref[...]), h_ref[...], g_ref[...])
        dh, dg = vjp(jnp.ones((), F32))
        dh_ref[...] = dh
        dh16_ref[...] = dh.astype(BF16)
        lv = jnp.full((1, LANE), loss, F32)

        @pl.when(pl.program_id(0) == 0)
        def _():
            dg_ref[...] = dg
            l_ref[...] = lv

        @pl.when(pl.program_id(0) > 0)
        def _():
            dg_ref[...] += dg
            l_ref[...] += lv

    return pl.pallas_call(
        body, name="loss_head", grid=(T // RT,), in_specs=[_rows(), _whole((1, D)), _rows()],
        out_specs=[_whole((1, LANE)), _rows(), _rows(), _whole((1, D))],
        out_shape=[jax.ShapeDtypeStruct((1, LANE), F32), jax.ShapeDtypeStruct((T, D), F32),
                   jax.ShapeDtypeStruct((T, D), BF16), jax.ShapeDtypeStruct((1, D), F32)],
        compiler_params=_cparams(("arbitrary",)),
    )(h, gain, tgt)


def _prep_flags():
    j = pl.program_id(0)
    qk_scale = jnp.where(j < HEADS, HD ** -0.5, 1.0).astype(F32)
    return qk_scale, j >= 2 * HEADS


def _prep(proj, conv_w):
    def body(x_ref, w_ref, o_ref):
        qk_scale, is_v = _prep_flags()
        o_ref[...] = _prep_fn(x_ref[...], w_ref[...], qk_scale, is_v)

    col = lambda j: (0, j)
    return pl.pallas_call(
        body, name="gdn_prep", grid=(3 * HEADS,),
        in_specs=[pl.BlockSpec((T, HD), col), pl.BlockSpec((4, HD), col)], out_specs=pl.BlockSpec((T, HD), col),
        out_shape=jax.ShapeDtypeStruct((T, 3 * HW), F32), compiler_params=_cparams(("parallel",)),
    )(proj, conv_w)


def _prep_bwd(proj, conv_w, dqkv, dproj):
    def body(x_ref, w_ref, d_ref, _, dx_ref, dw_ref):
        qk_scale, is_v = _prep_flags()
        _, vjp = jax.vjp(lambda x, w: _prep_fn(x, w, qk_scale, is_v), x_ref[...], w_ref[...])
        dx, dw = vjp(d_ref[...])
        dx_ref[...] = dx.astype(BF16)
        dw_ref[...] = dw

    col = lambda j: (0, j)
    return pl.pallas_call(
        body, name="gdn_prep_bwd", grid=(3 * HEADS,),
        in_specs=[pl.BlockSpec((T, HD), col), pl.BlockSpec((4, HD), col), pl.BlockSpec((T, HD), col), ANY],
        out_specs=[pl.BlockSpec((T, HD), col), pl.BlockSpec((4, HD), col)],
        out_shape=[jax.ShapeDtypeStruct((T, NP), BF16), jax.ShapeDtypeStruct((4, 3 * HW), F32)],
        input_output_aliases={3: 0}, compiler_params=_cparams(("parallel",)),
    )(proj, conv_w, dqkv, dproj)


BA_BLK = BA_OFF // LANE


def _gates(proj, a_log, dt_bias):
    def body(x_ref, a_ref, d_ref, o_ref):
        o_ref[...] = _gates_fn(x_ref[...], a_ref[...], d_ref[...])

    return pl.pallas_call(
        body, name="gdn_gates", grid=(1,),
        in_specs=[pl.BlockSpec((T, LANE), lambda i: (0, BA_BLK)), _whole((1, LANE)), _whole((1, LANE))],
        out_specs=_whole((T, LANE)),
        out_shape=jax.ShapeDtypeStruct((T, LANE), F32), compiler_params=_cparams(("arbitrary",)),
    )(proj, a_log, dt_bias)


def _gates_bwd(proj, a_log, dt_bias, dbg, dproj):
    def body(x_ref, a_ref, d_ref, dbg_ref, _, dx_ref, da_ref, dd_ref):
        _, vjp = jax.vjp(_gates_fn, x_ref[...], a_ref[...], d_ref[...])
        dx, da_ref[...], dd_ref[...] = vjp(dbg_ref[...])
        dx_ref[...] = dx.astype(BF16)

    ba = pl.BlockSpec((T, LANE), lambda i: (0, BA_BLK))
    return pl.pallas_call(
        body, name="gdn_gates_bwd", grid=(1,),
        in_specs=[ba, _whole((1, LANE)), _whole((1, LANE)), _whole((T, LANE)), ANY],
        out_specs=[ba, _whole((1, LANE)), _whole((1, LANE))],
        out_shape=[jax.ShapeDtypeStruct((T, NP), BF16), jax.ShapeDtypeStruct((1, LANE), F32),
                   jax.ShapeDtypeStruct((1, LANE), F32)],
        input_output_aliases={4: 0}, compiler_params=_cparams(("arbitrary",)),
    )(proj, a_log, dt_bias, dbg, dproj)


NCK = T // CH
CPS = 2


def _chunk_prep_specs(rev=False):
    at = (lambda n: NCK - 1 - n) if rev else (lambda n: n)
    wide = pl.BlockSpec((CH, HW), lambda n: (at(n), 0))
    return [wide, wide, wide, wide, pl.BlockSpec((HEADS, CH, CH), lambda n: (0, at(n), 0)),
            pl.BlockSpec((None, 1, HW), lambda n: (at(n), 0, 0))]


def _chunk_prep_shapes(dtypes):
    shp = [(T, HW), (T, HW), (T, HW), (T, HW), (HEADS, T, CH), (NCK, 1, HW)]
    return [jax.ShapeDtypeStruct(s, dt) for s, dt in zip(shp, dtypes)]


def _chunk_prep(qkv, bg):
    def body(x_ref, bg_ref, *o_refs):
        rows = [slice(ci * CH, (ci + 1) * CH) for ci in range(CPS)]
        res = _chunk_prep_fn([x_ref[r, :] for r in rows], [bg_ref[r, :] for r in rows])
        for ci, (u, w, qe, kd, qk, egl) in enumerate(res):
            for o_ref, val in zip(o_refs[:4], (u, w, qe, kd)):
                o_ref[rows[ci], :] = val.astype(o_ref.dtype)
            o_refs[4][:, rows[ci], :] = qk.astype(BF16)
            o_refs[5][ci] = egl

    wide = pl.BlockSpec((CPS * CH, HW), lambda n: (n, 0))
    return pl.pallas_call(
        body, name="gdn_chunk_prep", grid=(NCK // CPS,),
        in_specs=[pl.BlockSpec((CPS * CH, 3 * HW), lambda n: (n, 0)), pl.BlockSpec((CPS * CH, LANE), lambda n: (n, 0))],
        out_specs=[wide, wide, wide, wide, pl.BlockSpec((HEADS, CPS * CH, CH), lambda n: (0, n, 0)),
                   pl.BlockSpec((CPS, 1, HW), lambda n: (n, 0, 0))],
        out_shape=_chunk_prep_shapes((F32, BF16, BF16, BF16, BF16, F32)),
        compiler_params=_cparams(("parallel",)),
    )(qkv, bg)


def _chunk_prep_bwd(qkv, bg, cots):
    def body(x_ref, bg_ref, du, dw, dqe, dkd, dqk, degl, dx_ref, dbg_ref):
        rows = [slice(ci * CH, (ci + 1) * CH) for ci in range(CPS)]
        _, vjp = jax.vjp(_chunk_prep_fn, [x_ref[r, :] for r in rows], [bg_ref[r, :] for r in rows])
        dxs, dbgs = vjp([(du[r, :], dw[r, :], dqe[r, :], dkd[r, :], dqk[:, r, :], degl[ci])
                         for ci, r in enumerate(rows)])
        for r, dx, dbg in zip(rows, dxs, dbgs):
            dx_ref[r, :] = dx
            dbg_ref[r, :] = dbg

    wide = pl.BlockSpec((CPS * CH, HW), lambda n: (n, 0))
    return pl.pallas_call(
        body, name="gdn_chunk_prep_bwd", grid=(NCK // CPS,),
        in_specs=[pl.BlockSpec((CPS * CH, 3 * HW), lambda n: (n, 0)), pl.BlockSpec((CPS * CH, LANE), lambda n: (n, 0)),
                  wide, wide, wide, wide, pl.BlockSpec((HEADS, CPS * CH, CH), lambda n: (0, n, 0)),
                  pl.BlockSpec((CPS, 1, HW), lambda n: (n, 0, 0))],
        out_specs=[pl.BlockSpec((CPS * CH, 3 * HW), lambda n: (n, 0)), pl.BlockSpec((CPS * CH, LANE), lambda n: (n, 0))],
        out_shape=[jax.ShapeDtypeStruct((T, 3 * HW), F32), jax.ShapeDtypeStruct((T, LANE), F32)],
        compiler_params=_cparams(("parallel",)),
    )(qkv, bg, *cots)


def _head_args(refs):
    u, w, qe, kd, qk, egl = refs
    sls = [slice(h * HD, (h + 1) * HD) for h in range(HEADS)]
    return ([u[:, sl] for sl in sls], [w[:, sl].astype(F32) for sl in sls], [qe[:, sl].astype(F32) for sl in sls],
            [kd[:, sl].astype(F32) for sl in sls], [qk[h].astype(F32) for h in range(HEADS)],
            [egl[:, sl] for sl in sls])


def _chunk_scan(prep):
    def body(*refs):
        o_ref, sh_ref, s_ref = refs[6:]

        @pl.when(pl.program_id(0) == 0)
        def _():
            s_ref[...] = jnp.zeros_like(s_ref)

        s = [s_ref[h] for h in range(HEADS)]
        for h in range(HEADS):
            sh_ref[h, 0] = s[h]
        o, s_new = _chunk_state_fn(*_head_args(refs[:6]), s)
        for h in range(HEADS):
            o_ref[:, h * HD:(h + 1) * HD] = o[h]
            s_ref[h] = s_new[h]

    return pl.pallas_call(
        body, name="gdn_scan", grid=(NCK,), in_specs=_chunk_prep_specs(),
        out_specs=[pl.BlockSpec((CH, HW), lambda n: (n, 0)), pl.BlockSpec((HEADS, 1, HD, HD), lambda n: (0, n, 0, 0))],
        out_shape=[jax.ShapeDtypeStruct((T, HW), F32), jax.ShapeDtypeStruct((HEADS, NCK, HD, HD), F32)],
        scratch_shapes=[pltpu.VMEM((HEADS, HD, HD), F32)], compiler_params=_cparams(("arbitrary",)),
    )(*prep)


def _chunk_scan_bwd(prep, s_hist, do, after=()):
    n_in = 8 + len(after)

    def body(*refs):
        sh_ref, do_ref = refs[6:8]
        d_refs = refs[n_in:n_in + 6]
        ds_ref = refs[n_in + 6]

        @pl.when(pl.program_id(0) == 0)
        def _():
            ds_ref[...] = jnp.zeros_like(ds_ref)

        sls = [slice(h * HD, (h + 1) * HD) for h in range(HEADS)]
        _, vjp = jax.vjp(_chunk_state_fn, *_head_args(refs[:6]), [sh_ref[h, 0] for h in range(HEADS)])
        du, dw, dqe, dkd, dqk, degl, ds = vjp(([do_ref[:, sl] for sl in sls], [ds_ref[h] for h in range(HEADS)]))
        for h, sl in enumerate(sls):
            for d_ref, val in zip(d_refs[:4], (du, dw, dqe, dkd)):
                d_ref[:, sl] = val[h]
            d_refs[4][h] = dqk[h]
            d_refs[5][:, sl] = degl[h]
            ds_ref[h] = ds[h]

    rev = lambda n: NCK - 1 - n
    return pl.pallas_call(
        body, name="gdn_scan_bwd", grid=(NCK,),
        in_specs=_chunk_prep_specs(rev=True) + [pl.BlockSpec((HEADS, 1, HD, HD), lambda n: (0, rev(n), 0, 0)),
                                                pl.BlockSpec((CH, HW), lambda n: (rev(n), 0))] + [ANY] * len(after),
        out_specs=_chunk_prep_specs(rev=True), out_shape=_chunk_prep_shapes((F32,) * 6),
        scratch_shapes=[pltpu.VMEM((HEADS, HD, HD), F32)], compiler_params=_cparams(("arbitrary",)),
    )(*prep, s_hist, do, *after)


def _mix_specs():
    pc = lambda c: pl.BlockSpec((GCH, HW), lambda i: (i, c))
    return [pl.BlockSpec((GCH, HW), lambda i: (i, 0)), pc(3), pc(4), pc(5), _whole((1, HD)), _whole((1, HW)),
            _whole((1, HW)), _whole((HEADS, GCH, GCH)), _whole((GCH, LANE))]


def _mix(o, proj, ong, lng, lnb, ws, bst):
    def body(o_ref, z_ref, u_ref, v_ref, ong_ref, lng_ref, lnb_ref, ws_ref, bs_ref, m_ref):
        m_ref[...] = _mix_fn(o_ref[...], z_ref[...], u_ref[...], v_ref[...], ong_ref[...], lng_ref[...],
                             lnb_ref[...], ws_ref[...], bs_ref[...]).astype(BF16)

    return pl.pallas_call(
        body, name="mix", grid=(T // GCH,), in_specs=_mix_specs(),
        out_specs=pl.BlockSpec((GCH, D), lambda i: (i, 0)), out_shape=jax.ShapeDtypeStruct((T, D), BF16),
        compiler_params=_cparams(("parallel",)),
    )(o, proj, proj, proj, ong, lng, lnb, ws, bst)


def _mix_bwd(o, proj, ong, lng, lnb, ws, bst, dmix):
    def body(o_ref, z_ref, u_ref, v_ref, ong_ref, lng_ref, lnb_ref, ws_ref, bs_ref, dm_ref,
             do_ref, dzuv_ref, dong_ref, dlng_ref, dlnb_ref, dws_ref, dbs_ref):
        _, vjp = jax.vjp(_mix_fn, o_ref[...], z_ref[...], u_ref[...], v_ref[...], ong_ref[...], lng_ref[...],
                         lnb_ref[...], ws_ref[...], bs_ref[...])
        do, dz, du, dv, dong, dlng, dlnb, dws, dbs = vjp(dm_ref[...])
        do_ref[...] = do
        dzuv_ref[:, 0:HW] = dz.astype(BF16)
        dzuv_ref[:, HW:2 * HW] = du.astype(BF16)
        dzuv_ref[:, 2 * HW:3 * HW] = dv.astype(BF16)
        acc = [(dong_ref, dong), (dlng_ref, dlng), (dlnb_ref, dlnb), (dws_ref, dws), (dbs_ref, dbs)]

        @pl.when(pl.program_id(0) == 0)
        def _():
            for r, val in acc:
                r[...] = val

        @pl.when(pl.program_id(0) > 0)
        def _():
            for r, val in acc:
                r[...] += val

    shp = lambda *s: jax.ShapeDtypeStruct(s, F32)
    return pl.pallas_call(
        body, name="mix_bwd", grid=(T // GCH,),
        in_specs=_mix_specs() + [pl.BlockSpec((GCH, D), lambda i: (i, 0))],
        out_specs=[pl.BlockSpec((GCH, HW), lambda i: (i, 0)), pl.BlockSpec((GCH, 3 * HW), lambda i: (i, 1)),
                   _whole((1, HD)), _whole((1, HW)), _whole((1, HW)), _whole((HEADS, GCH, GCH)), _whole((GCH, LANE))],
        out_shape=[shp(T, HW), jax.ShapeDtypeStruct((T, NP), BF16), shp(1, HD), shp(1, HW), shp(1, HW),
                   shp(HEADS, GCH, GCH), shp(GCH, LANE)],
        compiler_params=_cparams(("arbitrary",)),
    )(o, proj, proj, proj, ong, lng, lnb, ws, bst, dmix)


def _swiglu_epilogue(accs, _):
    gate, up = accs
    return [gate, up, _silu(gate) * up]


def _swiglu_bwd_epilogue(accs, extras):
    dact = accs[0]
    gate, up = (e.astype(F32) for e in extras)
    sg = _sigmoid(gate)
    return [dact * up * (sg * (1.0 + gate * (1.0 - sg))), dact * (gate * sg)]


def _layer_fwd(h, p):
    hn = _rmsnorm("rms_mix", h, p["norm_mix"])
    proj = _mm("in_proj", "nn", hn[None], [p["w_in"][None]], tm=1024, tn=640, tk=D)[0][0]
    qkv = _prep(proj, p["conv_w"])
    bg = _gates(proj, p["a_log"], p["dt_bias"])
    prep = _chunk_prep(qkv, bg)
    o, s_hist = _chunk_scan(prep)
    if "late" in p:
        p.update(p.pop("late")(o))
    mix = _mix(o, proj, p["o_norm_g"], p["ln_v_g"], p["ln_v_b"], p["w_s"], p["bst"])
    h1 = _mm("out_proj", "nn", mix[None], [p["w_out"]], tm=1024, tn=512, tk=D // NCHIP, resid=h[None], n_n=D,
             b_spec=((None, D // NCHIP, 512), lambda g, i, j, k: (k, 0, j)))[0][0]
    h2n = _rmsnorm("rms_ffn", h1, p["norm_ffn"])
    gate, up, act = _mm("ffn_in", "nn", h2n[None], [p["w_gate"], p["w_up"]], tm=1024, tn=FF_SH, tk=D,
                        out_dtypes=(BF16, BF16, BF16), epilogue=_swiglu_epilogue)
    h2 = _mm("ffn_out", "nn", act, [p["w_down"]], tm=1024, tn=512, tk=FF_SH, reduce_g=True, resid=h1[None])[0][0]
    saved = dict(h=h, hn=hn, proj=proj, qkv=qkv, bg=bg, prep=prep, o=o, s_hist=s_hist, mix=mix, h1=h1, h2n=h2n,
                 gate=gate, up=up, act=act)
    return h2, saved


def _layer_bwd_ffn(dh2, dh2b, p, s, after=()):
    dh2b = dh2b[None]
    dgate, dup = _mm("ffn_out_bwd", "nt", dh2b, [p["w_down"]], tm=1024, tn=FF_SH, tk=D, out_dtypes=(BF16, BF16),
                     extras=(s["gate"], s["up"]), epilogue=_swiglu_bwd_epilogue, after=after)
    dh2n = _mm("ffn_gate_bwd", "nt", dgate, [p["w_gate"]], tm=1024, tn=512, tk=FF_SH, reduce_g=True)[0]
    dh2n = _mm("ffn_up_bwd", "nt", dup, [p["w_up"]], tm=1024, tn=512, tk=FF_SH, reduce_g=True, resid=dh2n)[0][0]
    dh1, dh1b, d_norm_ffn = _rmsnorm_bwd("rms_ffn_bwd", dh2n, s["h1"], p["norm_ffn"], dh2)
    d_w_down = _mm("ffn_wdown_grad", "tn", s["act"], [dh2b], tm=FF_SH, tn=512, tk=1024)[0]
    d_w_gate = _mm("ffn_wgate_grad", "tn", s["h2n"][None], [dgate], tm=512, tn=FF_SH, tk=1024)[0]
    d_w_up = _mm("ffn_wup_grad", "tn", s["h2n"][None], [dup], tm=512, tn=FF_SH, tk=1024)[0]
    return dh1, dh1b, dict(norm_ffn=d_norm_ffn, w_gate=d_w_gate, w_up=d_w_up, w_down=d_w_down)


def _layer_bwd_mixer(dh1, dh1b, p, s, after=(), midway=None):
    dh1b = dh1b[None]
    dmix = _mm("out_proj_bwd", "nt", dh1b, [p["w_out"]], tm=1024, tn=D // NCHIP, tk=D, n_n=D, after=after,
               b_spec=((None, D // NCHIP, D), lambda g, i, j, k: (j, 0, k)))[0][0]
    d_w_out = _mm("out_proj_wgrad", "tn", s["mix"][None], [dh1b], tm=512, tn=512, tk=1024)[0][0]
    do, dproj, d_ong, d_lng, d_lnb, d_ws, d_bst = _mix_bwd(
        s["o"], s["proj"], p["o_norm_g"], p["ln_v_g"], p["ln_v_b"], p["w_s"], p["bst"], dmix)
    then = midway(do) if midway is not None else ()
    dqkv, dbg = _chunk_prep_bwd(s["qkv"], s["bg"], _chunk_scan_bwd(s["prep"], s["s_hist"], do, then))
    dproj, d_conv = _prep_bwd(s["proj"], p["conv_w"], dqkv, dproj)
    dproj, d_a_log, d_dt_bias = _gates_bwd(s["proj"], p["a_log"], p["dt_bias"], dbg, dproj)
    dproj = dproj[None]
    dhn = _mm("in_proj_bwd", "nt", dproj, [p["w_in"][None]], tm=1024, tn=512, tk=640)[0][0]
    dh, dhb, d_norm_mix = _rmsnorm_bwd("rms_mix_bwd", dhn, s["h"], p["norm_mix"], dh1)
    d_w_in = _mm("in_proj_wgrad", "tn", s["hn"][None], [dproj], tm=512, tn=640, tk=1024)[0]
    grads = dict(norm_mix=d_norm_mix, w_in=d_w_in, conv_w=d_conv, a_log=d_a_log, dt_bias=d_dt_bias, o_norm_g=d_ong,
                 ln_v_g=d_lng, ln_v_b=d_lnb, w_s=d_ws, bst=d_bst, w_out=d_w_out)
    return dh, dhb, grads


def _lanes(v, off=0):
    return jnp.zeros((1, LANE), F32).at[0, off:off + v.shape[0]].set(v)


def _w_in_pieces():
    regions = [(0, 2048, 0), (2048, 2056, BA_OFF), (2056, IN_DIM, 2048)]
    sh = IN_DIM // NCHIP
    out = []
    for j in range(NCHIP):
        for lo, hi, at in regions:
            a, b = max(lo, j * sh), min(hi, (j + 1) * sh)
            if a < b:
                out.append((j, a - j * sh, at + a - lo, b - a))
    return out


W_IN_PIECES = _w_in_pieces()
WT = 256


def _assemble_w_in(gathered, own, place):
    def body(place_ref, g_ref, own_ref, o_ref):
        o_ref[:, IN_DIM:] = jnp.zeros((WT, NP - IN_DIM), BF16)
        mine = own_ref[...]
        for j, src, dst, width in W_IN_PIECES:
            val = jnp.where(place_ref[0] == j, mine[:, src:src + width], g_ref[j, :, src:src + width])
            o_ref[:, dst:dst + width] = val

    sh = IN_DIM // NCHIP
    return pl.pallas_call(
        body, name="assemble_w_in",
        grid_spec=pltpu.PrefetchScalarGridSpec(
            num_scalar_prefetch=1, grid=(D // WT,),
            in_specs=[pl.BlockSpec((NCHIP, WT, sh), lambda i, place_ref: (0, i, 0)),
                      pl.BlockSpec((WT, sh), lambda i, place_ref: (i, 0))],
            out_specs=pl.BlockSpec((WT, NP), lambda i, place_ref: (i, 0))),
        out_shape=jax.ShapeDtypeStruct((D, NP), BF16), compiler_params=_cparams(("parallel",)),
    )(place, gathered, own)


def _layer_params(l, big, small):
    return dict(
        {k: v for k, v in big.items() if k != "conv_w"},
        conv_w=jnp.concatenate([big["conv_w"][j, l] for j in range(NCHIP)], axis=1),
        norm_mix=small["norm_mix"][l][None], norm_ffn=small["norm_ffn"][l][None],
        a_log=_lanes(small["a_log"][l], HEADS), dt_bias=_lanes(small["dt_bias"][l], HEADS),
        o_norm_g=small["o_norm_g"][l][None], ln_v_g=small["ln_v_g"][l][None], ln_v_b=small["ln_v_b"][l][None],
        w_s=small["w_s"][l],
        bst=jnp.pad(small["b_s"][l].T, ((0, 0), (0, LANE - HEADS))),
    )


def _reference_layout(g):
    return dict(
        w_in=g["w_in"],
        w_out=g["w_out"].reshape(NCHIP, D // NCHIP, D),
        w_gate=g["w_gate"], w_up=g["w_up"], w_down=g["w_down"],
        conv_w=g["conv_w"], norm_mix=g["norm_mix"][0], norm_ffn=g["norm_ffn"][0],
        a_log=g["a_log"][0, HEADS:2 * HEADS], dt_bias=g["dt_bias"][0, HEADS:2 * HEADS],
        o_norm_g=g["o_norm_g"][0], ln_v_g=g["ln_v_g"][0], ln_v_b=g["ln_v_b"][0], w_s=g["w_s"],
        b_s=g["bst"][:, :HEADS].T,
    )


def _forward(x, tgt, layers, norm_final):
    h = x
    saved, params = [], []
    for p in layers:
        p = p(h) if callable(p) else p
        h, s = _layer_fwd(h, p)
        saved.append(s)
        params.append(p)
    return (saved, params) + tuple(_loss_head(h, norm_final, tgt))


def _local_step(x, tgt, layers, norm_final):
    saved, layers, loss, dh, dhb, d_norm_final = _forward(x, tgt, layers, norm_final)
    grads = [None] * DEPTH
    for l in reversed(range(DEPTH)):
        dh1, dh1b, g_ffn = _layer_bwd_ffn(dh, dhb, layers[l], saved[l])
        dh, dhb, g_mix = _layer_bwd_mixer(dh1, dh1b, layers[l], saved[l])
        grads[l] = {**g_ffn, **g_mix}
    return loss, dh, grads, d_norm_final


def _place():
    x, y, c = lax.axis_index("x"), lax.axis_index("y"), lax.axis_index("c")
    return x, y, c, [(1 - x, y), (x, 1 - y), (1 - x, 1 - y)]


def _remote(src, dst, send_sem, recv_sem, to):
    return pltpu.make_async_remote_copy(src_ref=src, dst_ref=dst, send_sem=send_sem, recv_sem=recv_sem,
                                        device_id=to, device_id_type=MESH)


def _comm_call(name, body, ins, out_shape, n_sems, aliases=None):
    return pl.pallas_call(
        body, name=name, in_specs=[ANY] * len(ins), out_specs=[ANY] * len(out_shape), out_shape=out_shape,
        scratch_shapes=[pltpu.SemaphoreType.DMA((n,)) for n in n_sems], input_output_aliases=aliases or {},
        compiler_params=pltpu.CompilerParams(has_side_effects=True),
    )(*ins)


def _half_rows(ref, of_c, dim):
    hr = ref.shape[dim] // 2
    return pl.ds(pl.multiple_of(of_c * hr, BF16_ROWS), hr)


def _gather_plan(whole):
    def plan(srcs, lands):
        x, y, c, others = _place()
        chip = 2 * x + y
        out = []
        for src, land, all_of_it in zip(srcs, lands, whole):
            for ox, oy in others:
                if all_of_it:
                    out.append((src, land.at[chip], (ox, oy, c)))
                else:
                    out.append((src.at[_half_rows(src, c, 0)], land.at[chip, _half_rows(src, c, 0)], (ox, oy, c)))
        return out
    return plan


def _forward_halves(lands):
    n = len(lands)

    def body(*refs):
        outs = refs[n:2 * n]
        send_s, recv_s = refs[2 * n:]
        x, y, c, others = _place()
        sibling = (x, y, 1 - c)
        copies = []
        for a in range(n):
            for k, (ox, oy) in enumerate(others):
                mine = outs[a].at[2 * ox + oy, _half_rows(outs[a], c, 1)]
                copies.append(_remote(mine, mine, send_s.at[3 * a + k], recv_s.at[3 * a + k], sibling))
        for cp in copies:
            cp.start()
        for a in range(n):
            for k, (ox, oy) in enumerate(others):
                landed = outs[a].at[2 * ox + oy, _half_rows(outs[a], 1 - c, 1)]
                _remote(landed, landed, send_s.at[3 * a + k], recv_s.at[3 * a + k], sibling).wait_recv()
        for cp in copies:
            cp.wait_send()

    out_shape = [jax.ShapeDtypeStruct(g.shape, g.dtype) for g in lands]
    return _comm_call("forward_halves", body, lands, out_shape, [3 * n, 3 * n], aliases={a: a for a in range(n)})


def _exchange_halves(gs):
    n = len(gs)

    def body(*refs):
        ins, outs = refs[:n], refs[n:2 * n]
        send_s, recv_s = refs[2 * n:]
        x, y, c, _ = _place()
        copies = []
        for a in range(n):
            hr = ins[a].shape[1] // 2
            theirs = ins[a].at[:, pl.ds(pl.multiple_of((1 - c) * hr, 8), hr)]
            copies.append(_remote(theirs, outs[a], send_s.at[a], recv_s.at[a], (x, y, 1 - c)))
        for cp in copies:
            cp.start()
        for cp in copies:
            cp.wait()

    out_shape = [jax.ShapeDtypeStruct((g.shape[0], g.shape[1] // 2, g.shape[2]), F32) for g in gs]
    return _comm_call("exchange_halves", body, gs, out_shape, [n, n])


def _scatter_partials(ps):
    n = len(ps)

    def body(*refs):
        ins, outs = refs[:n], refs[n:2 * n]
        send_s, recv_s = refs[2 * n:]
        x, y, c, others = _place()
        copies = []
        for a in range(n):
            for k, (ox, oy) in enumerate(others):
                copies.append(_remote(ins[a].at[2 * ox + oy], outs[a].at[k], send_s.at[3 * a + k],
                                      recv_s.at[3 * a + k], (ox, oy, c)))
        for cp in copies:
            cp.start()
        for cp in copies:
            cp.wait()

    out_shape = [jax.ShapeDtypeStruct((3,) + p.shape[1:], p.dtype) for p in ps]
    return _comm_call("scatter_partials", body, ps, out_shape, [3 * n, 3 * n])


HBM_SPEC = pl.BlockSpec(memory_space=pltpu.HBM)
SEM_SPEC = pl.BlockSpec(memory_space=pltpu.SEMAPHORE)
DATAFLOW = pltpu.SideEffectType.DATAFLOW_SIDE_EFFECTING


def _exchange_plan(srcs, lands):
    x, y, c, _ = _place()
    plan = []
    for src, land in zip(srcs, lands):
        hr = src.shape[1] // 2
        plan.append((src.at[:, pl.ds(pl.multiple_of((1 - c) * hr, 8), hr)], land, (x, y, 1 - c)))
    return plan


def _scatter_plan(srcs, lands):
    x, y, c, others = _place()
    return [(src.at[2 * ox + oy], land.at[k], (ox, oy, c))
            for src, land in zip(srcs, lands) for k, (ox, oy) in enumerate(others)]


def _split_start(name, plan, srcs, land_shapes, n_copies, after=()):
    n = len(srcs)
    lands = [pltpu.with_memory_space_constraint(lax.empty(s.shape, s.dtype), pltpu.HBM) for s in land_shapes]
    srcs = [pltpu.with_memory_space_constraint(s, pltpu.HBM) for s in srcs]

    def body(*refs):
        send_s, recv_s = refs[2 * n + len(after)], refs[2 * n + len(after) + 1]
        for i, (src, dst, to) in enumerate(plan(refs[:n], refs[n:2 * n])):
            _remote(src, dst, send_s.at[i], recv_s.at[i], to).start()
        refs[-1][...] = jnp.zeros_like(refs[-1])

    thru = [pltpu.HBM(s.shape, s.dtype) for s in srcs + lands]
    out = pl.pallas_call(
        body, name=name, in_specs=[HBM_SPEC] * (2 * n) + [ANY] * len(after),
        out_specs=[SEM_SPEC, SEM_SPEC] + [HBM_SPEC] * (2 * n) + [pl.BlockSpec(memory_space=pltpu.VMEM)],
        out_shape=[pltpu.SemaphoreType.DMA((n_copies,)), pltpu.SemaphoreType.DMA((n_copies,))] + thru
        + [jax.ShapeDtypeStruct((F32_ROWS, LANE), F32)],
        input_output_aliases={i: 2 + i for i in range(2 * n)},
        compiler_params=pltpu.CompilerParams(has_side_effects=DATAFLOW),
    )(*srcs, *lands, *after)
    return dict(sems=out[:2], srcs=out[2:2 + n], lands=out[2 + n:2 + 2 * n], token=out[-1])


def _split_wait(name, plan, started, after):
    n = len(started["srcs"])

    def body(*refs):
        send_s, recv_s = refs[2 * n], refs[2 * n + 1]
        for i, (src, dst, to) in enumerate(plan(refs[:n], refs[n:2 * n])):
            cp = _remote(src, dst, send_s.at[i], recv_s.at[i], to)
            cp.wait_send()
            cp.wait_recv()

    arrs = list(started["srcs"]) + list(started["lands"])
    out = pl.pallas_call(
        body, name=name, in_specs=[HBM_SPEC] * (2 * n) + [SEM_SPEC, SEM_SPEC, ANY],
        out_specs=[HBM_SPEC] * (2 * n), out_shape=[pltpu.HBM(s.shape, s.dtype) for s in arrs],
        input_output_aliases={i: i for i in range(2 * n)},
        compiler_params=pltpu.CompilerParams(has_side_effects=DATAFLOW),
    )(*arrs, *started["sems"], after)
    return out[:n], out[n:]


def _join_halves(rs):
    n = len(rs)

    def body(*refs):
        outs = refs[n:2 * n]
        send_s, recv_s = refs[2 * n:]
        x, y, c, _ = _place()
        sibling = (x, y, 1 - c)

        def half(a, of_c):
            hr = outs[a].shape[1] // 2
            return outs[a].at[:, pl.ds(pl.multiple_of(of_c * hr, 8), hr)]

        copies = [_remote(half(a, c), half(a, c), send_s.at[a], recv_s.at[a], sibling) for a in range(n)]
        for cp in copies:
            cp.start()
        for a in range(n):
            landed = half(a, 1 - c)
            _remote(landed, landed, send_s.at[a], recv_s.at[a], sibling).wait_recv()
        for cp in copies:
            cp.wait_send()

    out_shape = [jax.ShapeDtypeStruct(r.shape, r.dtype) for r in rs]
    return _comm_call("join_halves", body, rs, out_shape, [n, n], aliases={a: a for a in range(n)})


NDEV = 8


def _allreduce_small(buf):
    r = buf.shape[0]

    def body(in_ref, out_ref, gath, send_s, recv_s):
        x, y, c, _ = _place()
        me = 4 * x + 2 * y + c
        copies = []
        for rel in range(1, NDEV):
            px = 1 - x if rel & 4 else x
            py = 1 - y if rel & 2 else y
            pc = 1 - c if rel & 1 else c
            copies.append((_remote(in_ref, gath.at[me], send_s.at[rel - 1], recv_s.at[rel - 1], (px, py, pc)),
                           4 * px + 2 * py + pc))
        for cp, _ in copies:
            cp.start()
        gath[me] = in_ref[...]
        for rel, (cp, peer) in enumerate(copies):
            landed = gath.at[peer]
            _remote(landed, landed, send_s.at[rel], recv_s.at[rel], (x, y, c)).wait_recv()
        for cp, _ in copies:
            cp.wait_send()
        total = gath[0]
        for d in range(1, NDEV):
            total = total + gath[d]
        out_ref[...] = total

    vm = pl.BlockSpec(memory_space=pltpu.VMEM)
    return pl.pallas_call(
        body, name="allreduce_small", in_specs=[vm], out_specs=vm, out_shape=jax.ShapeDtypeStruct((r, LANE), F32),
        scratch_shapes=[pltpu.VMEM((NDEV, r, LANE), F32), pltpu.SemaphoreType.DMA((NDEV - 1,)),
                        pltpu.SemaphoreType.DMA((NDEV - 1,))],
        compiler_params=pltpu.CompilerParams(has_side_effects=True, vmem_limit_bytes=VMEM_LIMIT),
    )(buf)


MAX_ROW_TILE = 512
BF16_ROWS = 16


def _row_tile(rows):
    for t in range(min(rows, MAX_ROW_TILE) // BF16_ROWS * BF16_ROWS, 0, -BF16_ROWS):
        if rows % t == 0:
            return t
    raise ValueError(rows)


def _sum_halves(g, theirs, c_arr):
    nch, rows, cols = g.shape
    hr = rows // 2
    tr = _row_tile(hr)

    def body(c_ref, g_ref, t_ref, o_ref, ob_ref):
        s = g_ref[...] + t_ref[...]
        o_ref[...] = s
        ob_ref[...] = s.astype(BF16)

    blk = pl.BlockSpec((None, tr, cols), lambda j, i, c_ref: (j, i, 0))
    return pl.pallas_call(
        body, name="sum_halves",
        grid_spec=pltpu.PrefetchScalarGridSpec(
            num_scalar_prefetch=1, grid=(nch, hr // tr),
            in_specs=[pl.BlockSpec((None, None, tr, cols), lambda j, i, c_ref: (j, c_ref[0], i, 0)), blk],
            out_specs=[blk, blk]),
        out_shape=[jax.ShapeDtypeStruct((nch, hr, cols), F32), jax.ShapeDtypeStruct((nch, hr, cols), BF16)],
        compiler_params=_cparams(("parallel", "parallel")),
    )(c_arr, g.reshape(nch, 2, hr, cols), theirs)


def _sum_halves_w_in(g, theirs, c_arr):
    hr = D // 2
    sh = IN_DIM // NCHIP

    def body(c_ref, g_ref, t_ref, o_ref, ob_ref):
        s = g_ref[...] + t_ref[...]
        for j, dst, src, width in W_IN_PIECES:
            o_ref[j, :, dst:dst + width] = s[:, src:src + width]
            ob_ref[j, :, dst:dst + width] = s[:, src:src + width].astype(BF16)

    out = pl.BlockSpec((NCHIP, WT, sh), lambda i, c_ref: (0, i, 0))
    return pl.pallas_call(
        body, name="sum_halves_w_in",
        grid_spec=pltpu.PrefetchScalarGridSpec(
            num_scalar_prefetch=1, grid=(hr // WT,),
            in_specs=[pl.BlockSpec((None, WT, NP), lambda i, c_ref: (c_ref[0], i, 0)),
                      pl.BlockSpec((None, WT, NP), lambda i, c_ref: (0, i, 0))],
            out_specs=[out, out]),
        out_shape=[jax.ShapeDtypeStruct((NCHIP, hr, sh), F32), jax.ShapeDtypeStruct((NCHIP, hr, sh), BF16)],
        compiler_params=_cparams(("parallel",)),
    )(c_arr, g.reshape(2, hr, NP), theirs)


def _sum_chips(p, q, place, l, into=None):
    _, rows, cols = p.shape
    tr = _row_tile(rows)
    steps = rows // tr

    def body(place_ref, p_ref, q0, q1, q2, *rest):
        rest[-1][...] = ((p_ref[...] + q0[...].astype(F32)) + q1[...].astype(F32)) + q2[...].astype(F32)

    qs = lambda k: pl.BlockSpec((None, tr, cols), lambda i, place_ref: (k, i, 0))
    return pl.pallas_call(
        body, name="sum_chips",
        grid_spec=pltpu.PrefetchScalarGridSpec(
            num_scalar_prefetch=1, grid=(steps,),
            in_specs=[pl.BlockSpec((None, tr, cols), lambda i, place_ref: (place_ref[0], i, 0)), qs(0), qs(1), qs(2)]
            + ([ANY] if into is not None else []),
            out_specs=pl.BlockSpec((None, tr, cols), lambda i, place_ref: (l, place_ref[1] * steps + i, 0))),
        out_shape=jax.ShapeDtypeStruct((DEPTH, 2 * rows, cols), F32),
        input_output_aliases={5: 0} if into is not None else {},
        compiler_params=_cparams(("parallel",)),
    )(place, p, q, q, q, *([into] if into is not None else []))


def _adamw(w, g, m, v):
    layers, rows, cols = w.shape
    tr = _row_tile(rows)

    def body(w_ref, g_ref, m_ref, v_ref, d_ref, nm_ref, nv_ref):
        gv = g_ref[...]
        nm = ADAM_B1 * m_ref[...] + (1.0 - ADAM_B1) * gv
        nv = ADAM_B2 * v_ref[...] + (1.0 - ADAM_B2) * jnp.square(gv)
        m_hat = nm / (1.0 - ADAM_B1 ** ADAM_STEP)
        v_hat = nv / (1.0 - ADAM_B2 ** ADAM_STEP)
        d_ref[...] = -ADAM_LR * (m_hat / (jnp.sqrt(v_hat) + ADAM_EPS) + ADAM_WD * w_ref[...])
        nm_ref[...] = nm
        nv_ref[...] = nv

    blk = pl.BlockSpec((None, tr, cols), lambda l, i: (l, i, 0))
    return pl.pallas_call(
        body, name="adamw", grid=(layers, rows // tr), in_specs=[blk] * 4, out_specs=[blk] * 3,
        out_shape=[jax.ShapeDtypeStruct(w.shape, F32)] * 3, compiler_params=_cparams(("parallel", "parallel")),
    )(w, g, m, v)


BIG = ("w_in", "w_out", "w_gate", "w_up", "w_down")
SMALL = ("norm_mix", "a_log", "dt_bias", "o_norm_g", "ln_v_g", "ln_v_b", "w_s", "b_s", "norm_ffn", "norm_final")
ORDER = ("norm_mix", "w_in", "conv_w", "a_log", "dt_bias", "o_norm_g", "ln_v_g", "ln_v_b", "w_s", "b_s", "w_out",
         "norm_ffn", "w_gate", "w_up", "w_down", "norm_final")


F32_ROWS = 8
PACK_ROWS = 128


def _lane_rows(size):
    return -(-size // (F32_ROWS * LANE)) * F32_ROWS


def _pack(arrs):
    parts = [jnp.pad(a.reshape(-1), (0, _lane_rows(a.size) * LANE - a.size)).reshape(-1, LANE) for a in arrs]
    rows = sum(p.shape[0] for p in parts)
    if rows % PACK_ROWS:
        parts.append(jnp.zeros((-rows % PACK_ROWS, LANE), F32))
    return jnp.concatenate(parts, axis=0)


def _unpack(buf, like):
    out, row = [], 0
    for a in like:
        n = _lane_rows(a.size)
        out.append(buf[row:row + n].reshape(-1)[:a.size].reshape(a.shape))
        row += n
    return out


def kernel(x, norm_mix, w_in, conv_w, a_log, dt_bias, o_norm_g, ln_v_g, ln_v_b, w_s, b_s, w_out, norm_ffn, w_gate, w_up, w_down, norm_final, loss_target, m_norm_mix, m_w_in, m_conv_w, m_a_log, m_dt_bias, m_o_norm_g, m_ln_v_g, m_ln_v_b, m_w_s, m_b_s, m_w_out, m_norm_ffn, m_w_gate, m_w_up, m_w_down, m_norm_final, v_norm_mix, v_w_in, v_conv_w, v_a_log, v_dt_bias, v_o_norm_g, v_ln_v_g, v_ln_v_b, v_w_s, v_b_s, v_w_out, v_norm_ffn, v_w_gate, v_w_up, v_w_down, v_norm_final):
    w = dict(norm_mix=norm_mix, w_in=w_in, conv_w=conv_w, a_log=a_log, dt_bias=dt_bias, o_norm_g=o_norm_g,
             ln_v_g=ln_v_g, ln_v_b=ln_v_b, w_s=w_s, b_s=b_s, w_out=w_out, norm_ffn=norm_ffn, w_gate=w_gate, w_up=w_up,
             w_down=w_down, norm_final=norm_final)
    m = dict(norm_mix=m_norm_mix, w_in=m_w_in, conv_w=m_conv_w, a_log=m_a_log, dt_bias=m_dt_bias, o_norm_g=m_o_norm_g,
             ln_v_g=m_ln_v_g, ln_v_b=m_ln_v_b, w_s=m_w_s, b_s=m_b_s, w_out=m_w_out, norm_ffn=m_norm_ffn,
             w_gate=m_w_gate, w_up=m_w_up, w_down=m_w_down, norm_final=m_norm_final)
    v = dict(norm_mix=v_norm_mix, w_in=v_w_in, conv_w=v_conv_w, a_log=v_a_log, dt_bias=v_dt_bias, o_norm_g=v_o_norm_g,
             ln_v_g=v_ln_v_g, ln_v_b=v_ln_v_b, w_s=v_w_s, b_s=v_b_s, w_out=v_w_out, norm_ffn=v_norm_ffn,
             w_gate=v_w_gate, w_up=v_w_up, w_down=v_w_down, norm_final=v_norm_final)
    chip = 2 * lax.axis_index("x") + lax.axis_index("y")
    place = jnp.stack([chip, lax.axis_index("c")]).astype(jnp.int32)
    c_arr = place[1:]

    own = {n: [w[n][l].astype(BF16) for l in range(DEPTH)] for n in BIG}
    by_chip = lambda a: jax.ShapeDtypeStruct((NCHIP,) + a.shape, a.dtype)

    def start(name, srcs, whole, after=()):
        return _split_start(name, _gather_plan(whole), srcs, [by_chip(a) for a in srcs], 3 * len(srcs), after)

    def finish(name, started, whole, after):
        srcs, lands = _split_wait(name, _gather_plan(whole), started, after)
        passed = iter(_forward_halves([g for g, all_of_it in zip(lands, whole) if not all_of_it]))
        lands = [g if all_of_it else next(passed) for g, all_of_it in zip(lands, whole)]
        return srcs, [lax.dynamic_update_index_in_dim(g, o, chip, 0) for g, o in zip(lands, srcs)]

    ffn = BIG[1:]
    first = start("gather_first_start", [own["w_in"][0], conv_w], [False, True])
    early = start("gather_early_start", [own[n][0] for n in ffn], [False] * len(ffn), [first["token"]])
    later = start("gather_later_start", [own[n][1] for n in BIG], [False] * len(BIG), [early["token"]])
    (own_w_in, _), (w_in_by_chip, conv_by_chip) = finish("gather_first_wait", first, [False, True], later["token"])

    def late(after):
        return dict(zip(ffn, finish("gather_early_wait", early, [False] * len(ffn), after)[1]))

    layer0 = _layer_params(0, dict(w_in=_assemble_w_in(w_in_by_chip, own_w_in, place), conv_w=conv_by_chip, late=late), w)

    def layer1(after):
        srcs, by = finish("gather_later_wait", later, [False] * len(BIG), after)
        big = dict(zip(ffn, by[1:]), w_in=_assemble_w_in(by[0], srcs[0], place), conv_w=conv_by_chip)
        return _layer_params(1, big, w)

    saved, layers, loss_lanes, dh, dhb, d_norm_final = _forward(x[0], loss_target[0], [layer0, layer1],
                                                                 norm_final[None])
    loss = lax.psum(loss_lanes[0, 0], ("x", "y", "c"))

    sums, arrived = {}, {}

    def exchange_start(tag, l, names, grads, after=()):
        mine = [grads[n] for n in names]
        shapes = [jax.ShapeDtypeStruct((g.shape[0], g.shape[1] // 2, g.shape[2]), F32) for g in mine]
        return tag, l, names, _split_start(f"exchange_{tag}_start", _exchange_plan, mine, shapes, len(mine), after)

    def add_halves(l, names, mine, theirs):
        for n, g, t in zip(names, mine, theirs):
            sums[l, n] = (_sum_halves_w_in if n == "w_in" else _sum_halves)(g, t, c_arr)

    def exchange_wait(handle, after):
        tag, l, names, started = handle
        add_halves(l, names, *_split_wait(f"exchange_{tag}_wait", _exchange_plan, started, after))

    def scatter_start(tag, l, names, after=()):
        partial = [sums[l, n][1] for n in names]
        shapes = [jax.ShapeDtypeStruct((3,) + p.shape[1:], p.dtype) for p in partial]
        return tag, l, names, _split_start(f"scatter_{tag}_start", _scatter_plan, partial, shapes, 3 * len(names), after)

    def scatter_wait(handle, after):
        tag, l, names, started = handle
        for n, q in zip(names, _split_wait(f"scatter_{tag}_wait", _scatter_plan, started, after)[1]):
            arrived[l, n] = q

    last = DEPTH - 1
    swiglu = BIG[2:]
    dh1, dh1b, g_ffn = _layer_bwd_ffn(dh, dhb, layers[last], saved[last])
    dh, dhb, g_mix = _layer_bwd_mixer(dh1, dh1b, layers[last], saved[last])
    gl = [None, _reference_layout({**g_ffn, **g_mix})]
    ex_last = exchange_start("last", last, BIG, gl[last])
    dh1, dh1b, g_ffn = _layer_bwd_ffn(dh, dhb, layers[0], saved[0], after=[ex_last[-1]["token"]])
    exchange_wait(ex_last, dh1)
    sc_last = scatter_start("last", last, BIG)
    ex_ffn = exchange_start("swiglu", 0, swiglu, g_ffn, [sc_last[-1]["token"]])
    sc_ffn = []

    def midway(do):
        exchange_wait(ex_ffn, do)
        sc_ffn.append(scatter_start("swiglu", 0, swiglu))
        return [sc_ffn[0][-1]["token"]]

    dx, _, g_mix = _layer_bwd_mixer(dh1, dh1b, layers[0], saved[0], after=[ex_ffn[-1]["token"]], midway=midway)
    scatter_wait(sc_last, dx)
    scatter_wait(sc_ffn[0], dx)
    gl[0] = _reference_layout({**g_ffn, **g_mix})
    mine = [gl[0][n] for n in BIG[:2]]
    add_halves(0, BIG[:2], mine, _exchange_halves(mine))
    for n, q in zip(BIG[:2], _scatter_partials([sums[0, n][1] for n in BIG[:2]])):
        arrived[0, n] = q
    reduced = []
    for n in BIG:
        buf = None
        for l in range(DEPTH):
            buf = _sum_chips(sums[l, n][0], arrived[l, n], place, l, into=buf)
        reduced.append(buf)
    g_out = dict(zip(BIG, _join_halves(reduced)))

    small_g = [jnp.stack([gl[l][n] for l in range(DEPTH)]) for n in SMALL[:-1]] + [d_norm_final[0]]
    conv_g = jnp.stack([gl[l]["conv_w"] for l in range(DEPTH)])
    total = _allreduce_small(_pack(small_g + [conv_g]))
    *small_r, conv_r = _unpack(total, small_g + [conv_g])
    g_out.update(zip(SMALL, small_r))
    g_out["conv_w"] = lax.dynamic_slice_in_dim(conv_r, chip * conv_w.shape[2], conv_w.shape[2], axis=2)

    delta, new_m, new_v = {}, {}, {}
    for n in BIG:
        delta[n], new_m[n], new_v[n] = _adamw(w[n], g_out[n], m[n], v[n])
    rest = SMALL + ("conv_w",)
    like = [w[n] for n in rest]
    d, nm, nv = _adamw(*[_pack([src[n] for n in rest])[None] for src in (w, g_out, m, v)])
    for dst, buf in ((delta, d), (new_m, nm), (new_v, nv)):
        dst.update(zip(rest, _unpack(buf[0], like)))

    return (loss, dx[None], *[g_out[n] for n in ORDER], *[delta[n] for n in ORDER], *[new_m[n] for n in ORDER],
            *[new_v[n] for n in ORDER])
```

```python
import functools

import jax
import jax.numpy as jnp
from jax import lax
from jax.experimental import pallas as pl
from jax.experimental.pallas import tpu as pltpu

F32 = jnp.float32
BF16 = jnp.bfloat16
MESH = pl.DeviceIdType.MESH
ANY = pl.BlockSpec(memory_space=pl.ANY)
HIGHEST = lax.Precision.HIGHEST

T = 2048
D = 1024
DEPTH = 2
NCHIP = 4
HEADS = 4
HD = 128
HW = HEADS * HD
CH = 64
GCH = 128
IN_DIM = 3080
NP = 3200
BA_OFF = 3072
FF_SH = 704
EPS = 1e-6
LANE = 128
VMEM_LIMIT = 56 * 1024 * 1024

ADAM_LR = 0.001
ADAM_B1 = 0.9
ADAM_B2 = 0.999
ADAM_EPS = 1e-08
ADAM_WD = 0.01
ADAM_STEP = 10


def _cparams(sem=None):
    return pltpu.CompilerParams(dimension_semantics=sem, vmem_limit_bytes=VMEM_LIMIT)


_DIMS = {"nn": (((1,), (0,)), ((), ())), "nt": (((1,), (1,)), ((), ())), "tn": (((0,), (0,)), ((), ()))}


def _mm(name, mode, a, bs, *, tm, tn, tk, out_dtypes=(F32,), reduce_g=False, resid=None, extras=(), epilogue=None,
        b_spec=None, n_n=None, after=()):
    nb = len(bs)
    ga = a.shape[0]
    gbs = [1 if b_spec is not None else b.shape[0] for b in bs]
    g_n = max([ga] + gbs)
    if mode == "tn":
        k_n, m_n = a.shape[1:]
    else:
        m_n, k_n = a.shape[1:]
    if n_n is None:
        n_n = bs[0].shape[1] if mode == "nt" else bs[0].shape[2]
    assert m_n % tm == 0 and n_n % tn == 0 and k_n % tk == 0, (name, m_n, n_n, k_n)
    mi, nj, kk = m_n // tm, n_n // tn, k_n // tk
    if reduce_g:
        grid = (mi, nj, g_n, kk)
        ids = lambda i, j, g, k: (g, i, j, k)
        n_red = g_n * kk
        red_idx = lambda: pl.program_id(2) * kk + pl.program_id(3)
        sem = ("parallel", "parallel", "arbitrary", "arbitrary")
    else:
        grid = (g_n, mi, nj, kk)
        ids = lambda g, i, j, k: (g, i, j, k)
        n_red = kk
        red_idx = lambda: pl.program_id(3)
        sem = ("parallel", "parallel", "parallel", "arbitrary")

    def pick(gsz, g):
        return g if gsz > 1 else 0

    def a_map(*p):
        g, i, j, k = ids(*p)
        return (pick(ga, g), k, i) if mode == "tn" else (pick(ga, g), i, k)

    def b_map(gsz):
        def f(*p):
            g, i, j, k = ids(*p)
            if b_spec is not None:
                return b_spec[1](g, i, j, k)
            return (pick(gsz, g), j, k) if mode == "nt" else (pick(gsz, g), k, j)
        return f

    def o_map(gsz):
        def f(*p):
            g, i, j, k = ids(*p)
            return (0 if reduce_g else pick(gsz, g), i, j)
        return f

    a_spec = pl.BlockSpec((None, tk, tm) if mode == "tn" else (None, tm, tk), a_map)
    b_block = b_spec[0] if b_spec is not None else ((None, tn, tk) if mode == "nt" else (None, tk, tn))
    b_specs = [pl.BlockSpec(b_block, b_map(gs)) for gs in gbs]
    x_specs = [pl.BlockSpec((None, tm, tn), o_map(e.shape[0])) for e in extras]
    r_specs = [pl.BlockSpec((None, tm, tn), o_map(resid.shape[0]))] if resid is not None else []
    g_out = 1 if reduce_g else g_n
    out_shape = [jax.ShapeDtypeStruct((g_out, m_n, n_n), dt) for dt in out_dtypes]
    out_specs = [pl.BlockSpec((None, tm, tn), o_map(g_out)) for _ in out_dtypes]
    nx, nr, no = len(extras), len(r_specs), len(out_dtypes)
    n_in = 1 + nb + nx + nr + len(after)
    dims = _DIMS[mode]

    def body(*refs):
        a_ref = refs[0]
        b_refs = refs[1:1 + nb]
        x_refs = refs[1 + nb:1 + nb + nx]
        r_refs = refs[1 + nb + nx:1 + nb + nx + nr]
        o_refs = refs[n_in:n_in + no]
        acc_refs = refs[n_in + no:]
        r = red_idx()
        av = a_ref[...]
        for b_ref, acc in zip(b_refs, acc_refs):
            p = lax.dot_general(av, b_ref[...], dims, preferred_element_type=F32)

            @pl.when(r == 0)
            def _():
                acc[...] = p

            @pl.when(r > 0)
            def _():
                acc[...] += p

        @pl.when(r == n_red - 1)
        def _():
            accs = [acc[...] for acc in acc_refs]
            if r_refs:
                accs[0] = accs[0] + r_refs[0][...]
            outs = epilogue(accs, [x[...] for x in x_refs]) if epilogue is not None else accs
            for o_ref, o in zip(o_refs, outs):
                o_ref[...] = o.astype(o_ref.dtype)

    return pl.pallas_call(
        body, name=name, grid=grid,
        in_specs=[a_spec] + b_specs + x_specs + r_specs + [ANY] * len(after),
        out_specs=out_specs, out_shape=out_shape,
        scratch_shapes=[pltpu.VMEM((tm, tn), F32) for _ in range(nb)],
        compiler_params=_cparams(sem),
    )(a, *bs, *extras, *([resid] if resid is not None else []), *after)


def _sigmoid(x):
    return 1.0 / (1.0 + jnp.exp(-x))


def _silu(x):
    return x * _sigmoid(x)


def _gelu(x):
    return 0.5 * x * (1.0 + jnp.tanh(0.7978845608028654 * (x + 0.044715 * (x * x * x))))


def _rms_fn(h, gain):
    return h * lax.rsqrt(jnp.mean(h * h, axis=-1, keepdims=True) + EPS) * gain


def _shift_impl(x, s):
    n = x.shape[0]
    rolled = pltpu.roll(x, s % n, 0)
    row = lax.broadcasted_iota(jnp.int32, x.shape, 0)
    return jnp.where((row >= s) & (row < n + s), rolled, 0.0)


@functools.partial(jax.custom_vjp, nondiff_argnums=(1,))
def _shift(x, s):
    return _shift_impl(x, s)


def _shift_fwd(x, s):
    return _shift_impl(x, s), None


def _shift_bwd(s, _, g):
    return (_shift_impl(g, -s),)


_shift.defvjp(_shift_fwd, _shift_bwd)


def _prep_fn(x, w, qk_scale, is_v):
    y = x * w[3:4, :]
    for i in range(3):
        y = y + _shift(x, 3 - i) * w[i:i + 1, :]
    y = _silu(y)
    nrm = lax.rsqrt(jnp.sum(y * y, axis=-1, keepdims=True) + EPS) * qk_scale
    return y * jnp.where(is_v, 1.0, nrm)


def _softplus(x):
    return jnp.maximum(x, 0.0) + jnp.log(1.0 + jnp.exp(-jnp.abs(x)))


def _gates_fn(ba, a_log, dt_bias):
    lane = lax.broadcasted_iota(jnp.int32, ba.shape, 1)
    beta = _sigmoid(ba)
    g = -jnp.exp(a_log) * _softplus(ba + dt_bias)
    return jnp.where(lane < HEADS, beta, g)


def _dot16(a, b, dims=_DIMS["nn"]):
    return lax.dot_general(a.astype(BF16), b.astype(BF16), dims, preferred_element_type=F32)


def _dot32(a, b):
    return jnp.dot(a, b, preferred_element_type=F32, precision=HIGHEST)


def _dot3(a, b, dims=_DIMS["nn"]):
    return lax.dot_general(a, b, dims, preferred_element_type=F32, precision=lax.Precision.HIGH)


def _tri_inverses(mats):
    row = lax.broadcasted_iota(jnp.int32, (CH, CH), 0)
    col = lax.broadcasted_iota(jnp.int32, (CH, CH), 1)
    eye = (row == col).astype(F32)
    ts = [eye - a for a in mats]
    ps = list(mats)
    for _ in range(5):
        ps = [_dot3(p, p) for p in ps]
        ts = [t + _dot3(t, p) for t, p in zip(ts, ps)]
    return ts


@jax.custom_vjp
def _tri_solves(mats, rhs):
    return [_dot3(t, b) for t, b in zip(_tri_inverses(mats), rhs)]


def _tri_solves_fwd(mats, rhs):
    ts = _tri_inverses(mats)
    xs = [_dot3(t, b) for t, b in zip(ts, rhs)]
    return xs, (ts, xs)


def _tri_solves_bwd(res, dxs):
    ts, xs = res
    dbs = [_dot3(t, dx, _DIMS["tn"]) for t, dx in zip(ts, dxs)]
    return [-_dot3(db, x, _DIMS["nt"]) for db, x in zip(dbs, xs)], dbs


_tri_solves.defvjp(_tri_solves_fwd, _tri_solves_bwd)


def _chunk_prep_fn(xs, bgs):
    row = lax.broadcasted_iota(jnp.int32, (CH, CH), 0)
    col = lax.broadcasted_iota(jnp.int32, (CH, CH), 1)
    incl = row >= col
    strict = row > col
    lmat = incl.astype(F32)
    n = len(xs)
    items = [(i, h) for i in range(n) for h in range(HEADS)]
    part = lambda i, h, c: xs[i][:, c * HW + h * HD:c * HW + (h + 1) * HD]
    q = [part(i, h, 0) for i, h in items]
    k = [part(i, h, 1) for i, h in items]
    v = [part(i, h, 2) for i, h in items]
    beta = [bgs[i][:, h:h + 1] for i, h in items]
    gc_all = [_dot32(lmat, bg) for bg in bgs]
    gc = [gc_all[i][:, HEADS + h:HEADS + h + 1] for i, h in items]
    gmat = [jnp.where(strict, jnp.broadcast_to(bgs[i][:, HEADS + h:HEADS + h + 1], (CH, CH)), 0.0) for i, h in items]
    diff = [_dot3(lmat, m) for m in gmat]
    decay = [jnp.where(incl, jnp.exp(jnp.where(incl, d, 0.0)), 0.0) for d in diff]
    k_beta = [kk * b for kk, b in zip(k, beta)]
    kk_t = [_dot16(kb, kk, _DIMS["nt"]) for kb, kk in zip(k_beta, k)]
    qk_t = [_dot16(qq, kk, _DIMS["nt"]) for qq, kk in zip(q, k)]
    a = [jnp.where(strict, m * d, 0.0) for m, d in zip(kk_t, decay)]
    eg = [jnp.exp(g) for g in gc]
    rhs = [jnp.concatenate([vv * b, kb * e], axis=-1) for vv, b, kb, e in zip(v, beta, k_beta, eg)]
    uw = _tri_solves(a, rhs)
    qk = [m * d for m, d in zip(qk_t, decay)]
    g_last = [g[CH - 1:CH, :] for g in gc]
    qe = [qq * e for qq, e in zip(q, eg)]
    kd = [kk * jnp.exp(gl - g) for kk, gl, g in zip(k, g_last, gc)]
    egl = [jnp.broadcast_to(jnp.exp(gl), (1, HD)) for gl in g_last]
    out = []
    for i in range(n):
        mine = slice(i * HEADS, (i + 1) * HEADS)
        cat = lambda vals: jnp.concatenate(vals[mine], axis=-1)
        out.append((cat([x[:, :HD] for x in uw]), cat([x[:, HD:] for x in uw]), cat(qe), cat(kd),
                    jnp.concatenate([m[None] for m in qk[mine]], axis=0), cat(egl)))
    return out


def _chunk_state_fn(u, w, qe, kd, qk, egl, s):
    ws = [_dot16(a, b) for a, b in zip(w, s)]
    qs = [_dot16(a, b) for a, b in zip(qe, s)]
    v_new = [a - b for a, b in zip(u, ws)]
    o = [a + _dot16(b, c) for a, b, c in zip(qs, qk, v_new)]
    s_new = [a * e + _dot16(b, c, _DIMS["tn"]) for a, e, b, c in zip(s, egl, kd, v_new)]
    return o, s_new


def _mix_fn(o, z, ur, vr, ong, lng, lnb, ws, bst):
    row = lax.broadcasted_iota(jnp.int32, (GCH, GCH), 0)
    col = lax.broadcasted_iota(jnp.int32, (GCH, GCH), 1)
    causal = row >= col
    ug = _gelu(ur)
    vg = _gelu(vr)
    outs_dn, outs_gm = [], []
    for h in range(HEADS):
        sl = slice(h * HD, (h + 1) * HD)
        oh = o[:, sl]
        oh = oh * lax.rsqrt(jnp.mean(oh * oh, axis=-1, keepdims=True) + EPS)
        outs_dn.append(oh * ong * _silu(z[:, sl]))
        vh = vg[:, sl]
        mu = jnp.mean(vh, axis=-1, keepdims=True)
        var = jnp.mean(jnp.square(vh - mu), axis=-1, keepdims=True)
        vn = (vh - mu) * lax.rsqrt(var + EPS) * lng[:, sl] + lnb[:, sl]
        sp = _dot16(jnp.where(causal, ws[h], 0.0), vn) + bst[:, h:h + 1]
        outs_gm.append(ug[:, sl] * sp)
    return jnp.concatenate(outs_dn + outs_gm, axis=-1)


def _loss_fn(h, gain, tgt):
    y = _rms_fn(h, gain)
    return 0.5 * jnp.sum(jnp.mean(jnp.square(y - tgt), axis=-1))


RT = 256


def _rows(n=D):
    return pl.BlockSpec((RT, n), lambda i: (i, 0))


def _whole(shape):
    nd = len(shape)
    return pl.BlockSpec(shape, lambda i: (0,) * nd)


def _rmsnorm(name, h, gain):
    def body(h_ref, g_ref, o_ref):
        o_ref[...] = _rms_fn(h_ref[...], g_ref[...]).astype(BF16)

    return pl.pallas_call(
        body, name=name, grid=(T // RT,), in_specs=[_rows(), _whole((1, D))], out_specs=_rows(),
        out_shape=jax.ShapeDtypeStruct((T, D), BF16), compiler_params=_cparams(("parallel",)),
    )(h, gain)


def _rmsnorm_bwd(name, dhn, h, gain, resid):
    def body(dhn_ref, h_ref, g_ref, r_ref, dh_ref, dh16_ref, dg_ref):
        _, vjp = jax.vjp(_rms_fn, h_ref[...], g_ref[...])
        dh, dg = vjp(dhn_ref[...])
        dh = r_ref[...] + dh
        dh_ref[...] = dh
        dh16_ref[...] = dh.astype(BF16)

        @pl.when(pl.program_id(0) == 0)
        def _():
            dg_ref[...] = dg

        @pl.when(pl.program_id(0) > 0)
        def _():
            dg_ref[...] += dg

    return pl.pallas_call(
        body, name=name, grid=(T // RT,), in_specs=[_rows(), _rows(), _whole((1, D)), _rows()],
        out_specs=[_rows(), _rows(), _whole((1, D))],
        out_shape=[jax.ShapeDtypeStruct((T, D), F32), jax.ShapeDtypeStruct((T, D), BF16),
                   jax.ShapeDtypeStruct((1, D), F32)],
        compiler_params=_cparams(("arbitrary",)),
    )(dhn, h, gain, resid)


def _loss_head(h, gain, tgt):
    def body(h_ref, g_ref, t_ref, l_ref, dh_ref, dh16_ref, dg_ref):
        loss, vjp = jax.vjp(lambda hh, gg: _loss_fn(hh, gg, t_ref[...]), h_ref[...], g_ref[...])
        dh, dg = vjp(jnp.ones((), F32))
        dh_ref[...] = dh
        dh16_ref[...] = dh.astype(BF16)
        lv = jnp.full((1, LANE), loss, F32)

        @pl.when(pl.program_id(0) == 0)
        def _():
            dg_ref[...] = dg
            l_ref[...] = lv

        @pl.when(pl.program_id(0) > 0)
        def _():
            dg_ref[...] += dg
            l_ref[...] += lv

    return pl.pallas_call(
        body, name="loss_head", grid=(T // RT,), in_specs=[_rows(), _whole((1, D)), _rows()],
        out_specs=[_whole((1, LANE)), _rows(), _rows(), _whole((1, D))],
        out_shape=[jax.ShapeDtypeStruct((1, LANE), F32), jax.ShapeDtypeStruct((T, D), F32),
                   jax.ShapeDtypeStruct((T, D), BF16), jax.ShapeDtypeStruct((1, D), F32)],
        compiler_params=_cparams(("arbitrary",)),
    )(h, gain, tgt)


def _prep_flags():
    j = pl.program_id(0)
    qk_scale = jnp.where(j < HEADS, HD ** -0.5, 1.0).astype(F32)
    return qk_scale, j >= 2 * HEADS


def _prep(proj, conv_w):
    def body(x_ref, w_ref, o_ref):
        qk_scale, is_v = _prep_flags()
        o_ref[...] = _prep_fn(x_ref[...], w_ref[...], qk_scale, is_v)

    col = lambda j: (0, j)
    return pl.pallas_call(
        body, name="gdn_prep", grid=(3 * HEADS,),
        in_specs=[pl.BlockSpec((T, HD), col), pl.BlockSpec((4, HD), col)], out_specs=pl.BlockSpec((T, HD), col),
        out_shape=jax.ShapeDtypeStruct((T, 3 * HW), F32), compiler_params=_cparams(("parallel",)),
    )(proj, conv_w)


def _prep_bwd(proj, conv_w, dqkv, dproj):
    def body(x_ref, w_ref, d_ref, _, dx_ref, dw_ref):
        qk_scale, is_v = _prep_flags()
        _, vjp = jax.vjp(lambda x, w: _prep_fn(x, w, qk_scale, is_v), x_ref[...], w_ref[...])
        dx, dw = vjp(d_ref[...])
        dx_ref[...] = dx.astype(BF16)
        dw_ref[...] = dw

    col = lambda j: (0, j)
    return pl.pallas_call(
        body, name="gdn_prep_bwd", grid=(3 * HEADS,),
        in_specs=[pl.BlockSpec((T, HD), col), pl.BlockSpec((4, HD), col), pl.BlockSpec((T, HD), col), ANY],
        out_specs=[pl.BlockSpec((T, HD), col), pl.BlockSpec((4, HD), col)],
        out_shape=[jax.ShapeDtypeStruct((T, NP), BF16), jax.ShapeDtypeStruct((4, 3 * HW), F32)],
        input_output_aliases={3: 0}, compiler_params=_cparams(("parallel",)),
    )(proj, conv_w, dqkv, dproj)


BA_BLK = BA_OFF // LANE


def _gates(proj, a_log, dt_bias):
    def body(x_ref, a_ref, d_ref, o_ref):
        o_ref[...] = _gates_fn(x_ref[...], a_ref[...], d_ref[...])

    return pl.pallas_call(
        body, name="gdn_gates", grid=(1,),
        in_specs=[pl.BlockSpec((T, LANE), lambda i: (0, BA_BLK)), _whole((1, LANE)), _whole((1, LANE))],
        out_specs=_whole((T, LANE)),
        out_shape=jax.ShapeDtypeStruct((T, LANE), F32), compiler_params=_cparams(("arbitrary",)),
    )(proj, a_log, dt_bias)


def _gates_bwd(proj, a_log, dt_bias, dbg, dproj):
    def body(x_ref, a_ref, d_ref, dbg_ref, _, dx_ref, da_ref, dd_ref):
        _, vjp = jax.vjp(_gates_fn, x_ref[...], a_ref[...], d_ref[...])
        dx, da_ref[...], dd_ref[...] = vjp(dbg_ref[...])
        dx_ref[...] = dx.astype(BF16)

    ba = pl.BlockSpec((T, LANE), lambda i: (0, BA_BLK))
    return pl.pallas_call(
        body, name="gdn_gates_bwd", grid=(1,),
        in_specs=[ba, _whole((1, LANE)), _whole((1, LANE)), _whole((T, LANE)), ANY],
        out_specs=[ba, _whole((1, LANE)), _whole((1, LANE))],
        out_shape=[jax.ShapeDtypeStruct((T, NP), BF16), jax.ShapeDtypeStruct((1, LANE), F32),
                   jax.ShapeDtypeStruct((1, LANE), F32)],
        input_output_aliases={4: 0}, compiler_params=_cparams(("arbitrary",)),
    )(proj, a_log, dt_bias, dbg, dproj)


NCK = T // CH
CPS = 2


def _chunk_prep_specs(rev=False):
    at = (lambda n: NCK - 1 - n) if rev else (lambda n: n)
    wide = pl.BlockSpec((CH, HW), lambda n: (at(n), 0))
    return [wide, wide, wide, wide, pl.BlockSpec((HEADS, CH, CH), lambda n: (0, at(n), 0)),
            pl.BlockSpec((None, 1, HW), lambda n: (at(n), 0, 0))]


def _chunk_prep_shapes(dtypes):
    shp = [(T, HW), (T, HW), (T, HW), (T, HW), (HEADS, T, CH), (NCK, 1, HW)]
    return [jax.ShapeDtypeStruct(s, dt) for s, dt in zip(shp, dtypes)]


def _chunk_prep(qkv, bg):
    def body(x_ref, bg_ref, *o_refs):
        rows = [slice(ci * CH, (ci + 1) * CH) for ci in range(CPS)]
        res = _chunk_prep_fn([x_ref[r, :] for r in rows], [bg_ref[r, :] for r in rows])
        for ci, (u, w, qe, kd, qk, egl) in enumerate(res):
            for o_ref, val in zip(o_refs[:4], (u, w, qe, kd)):
                o_ref[rows[ci], :] = val.astype(o_ref.dtype)
            o_refs[4][:, rows[ci], :] = qk.astype(BF16)
            o_refs[5][ci] = egl

    wide = pl.BlockSpec((CPS * CH, HW), lambda n: (n, 0))
    return pl.pallas_call(
        body, name="gdn_chunk_prep", grid=(NCK // CPS,),
        in_specs=[pl.BlockSpec((CPS * CH, 3 * HW), lambda n: (n, 0)), pl.BlockSpec((CPS * CH, LANE), lambda n: (n, 0))],
        out_specs=[wide, wide, wide, wide, pl.BlockSpec((HEADS, CPS * CH, CH), lambda n: (0, n, 0)),
                   pl.BlockSpec((CPS, 1, HW), lambda n: (n, 0, 0))],
        out_shape=_chunk_prep_shapes((F32, BF16, BF16, BF16, BF16, F32)),
        compiler_params=_cparams(("parallel",)),
    )(qkv, bg)


def _chunk_prep_bwd(qkv, bg, cots):
    def body(x_ref, bg_ref, du, dw, dqe, dkd, dqk, degl, dx_ref, dbg_ref):
        rows = [slice(ci * CH, (ci + 1) * CH) for ci in range(CPS)]
        _, vjp = jax.vjp(_chunk_prep_fn, [x_ref[r, :] for r in rows], [bg_ref[r, :] for r in rows])
        dxs, dbgs = vjp([(du[r, :], dw[r, :], dqe[r, :], dkd[r, :], dqk[:, r, :], degl[ci])
                         for ci, r in enumerate(rows)])
        for r, dx, dbg in zip(rows, dxs, dbgs):
            dx_ref[r, :] = dx
            dbg_ref[r, :] = dbg

    wide = pl.BlockSpec((CPS * CH, HW), lambda n: (n, 0))
    return pl.pallas_call(
        body, name="gdn_chunk_prep_bwd", grid=(NCK // CPS,),
        in_specs=[pl.BlockSpec((CPS * CH, 3 * HW), lambda n: (n, 0)), pl.BlockSpec((CPS * CH, LANE), lambda n: (n, 0)),
                  wide, wide, wide, wide, pl.BlockSpec((HEADS, CPS * CH, CH), lambda n: (0, n, 0)),
                  pl.BlockSpec((CPS, 1, HW), lambda n: (n, 0, 0))],
        out_specs=[pl.BlockSpec((CPS * CH, 3 * HW), lambda n: (n, 0)), pl.BlockSpec((CPS * CH, LANE), lambda n: (n, 0))],
        out_shape=[jax.ShapeDtypeStruct((T, 3 * HW), F32), jax.ShapeDtypeStruct((T, LANE), F32)],
        compiler_params=_cparams(("parallel",)),
    )(qkv, bg, *cots)


def _head_args(refs):
    u, w, qe, kd, qk, egl = refs
    sls = [slice(h * HD, (h + 1) * HD) for h in range(HEADS)]
    return ([u[:, sl] for sl in sls], [w[:, sl].astype(F32) for sl in sls], [qe[:, sl].astype(F32) for sl in sls],
            [kd[:, sl].astype(F32) for sl in sls], [qk[h].astype(F32) for h in range(HEADS)],
            [egl[:, sl] for sl in sls])


def _chunk_scan(prep):
    def body(*refs):
        o_ref, sh_ref, s_ref = refs[6:]

        @pl.when(pl.program_id(0) == 0)
        def _():
            s_ref[...] = jnp.zeros_like(s_ref)

        s = [s_ref[h] for h in range(HEADS)]
        for h in range(HEADS):
            sh_ref[h, 0] = s[h]
        o, s_new = _chunk_state_fn(*_head_args(refs[:6]), s)
        for h in range(HEADS):
            o_ref[:, h * HD:(h + 1) * HD] = o[h]
            s_ref[h] = s_new[h]

    return pl.pallas_call(
        body, name="gdn_scan", grid=(NCK,), in_specs=_chunk_prep_specs(),
        out_specs=[pl.BlockSpec((CH, HW), lambda n: (n, 0)), pl.BlockSpec((HEADS, 1, HD, HD), lambda n: (0, n, 0, 0))],
        out_shape=[jax.ShapeDtypeStruct((T, HW), F32), jax.ShapeDtypeStruct((HEADS, NCK, HD, HD), F32)],
        scratch_shapes=[pltpu.VMEM((HEADS, HD, HD), F32)], compiler_params=_cparams(("arbitrary",)),
    )(*prep)


def _chunk_scan_bwd(prep, s_hist, do, after=()):
    n_in = 8 + len(after)

    def body(*refs):
        sh_ref, do_ref = refs[6:8]
        d_refs = refs[n_in:n_in + 6]
        ds_ref = refs[n_in + 6]

        @pl.when(pl.program_id(0) == 0)
        def _():
            ds_ref[...] = jnp.zeros_like(ds_ref)

        sls = [slice(h * HD, (h + 1) * HD) for h in range(HEADS)]
        _, vjp = jax.vjp(_chunk_state_fn, *_head_args(refs[:6]), [sh_ref[h, 0] for h in range(HEADS)])
        du, dw, dqe, dkd, dqk, degl, ds = vjp(([do_ref[:, sl] for sl in sls], [ds_ref[h] for h in range(HEADS)]))
        for h, sl in enumerate(sls):
            for d_ref, val in zip(d_refs[:4], (du, dw, dqe, dkd)):
                d_ref[:, sl] = val[h]
            d_refs[4][h] = dqk[h]
            d_refs[5][:, sl] = degl[h]
            ds_ref[h] = ds[h]

    rev = lambda n: NCK - 1 - n
    return pl.pallas_call(
        body, name="gdn_scan_bwd", grid=(NCK,),
        in_specs=_chunk_prep_specs(rev=True) + [pl.BlockSpec((HEADS, 1, HD, HD), lambda n: (0, rev(n), 0, 0)),
                                                pl.BlockSpec((CH, HW), lambda n: (rev(n), 0))] + [ANY] * len(after),
        out_specs=_chunk_prep_specs(rev=True), out_shape=_chunk_prep_shapes((F32,) * 6),
        scratch_shapes=[pltpu.VMEM((HEADS, HD, HD), F32)], compiler_params=_cparams(("arbitrary",)),
    )(*prep, s_hist, do, *after)


def _mix_specs():
    pc = lambda c: pl.BlockSpec((GCH, HW), lambda i: (i, c))
    return [pl.BlockSpec((GCH, HW), lambda i: (i, 0)), pc(3), pc(4), pc(5), _whole((1, HD)), _whole((1, HW)),
            _whole((1, HW)), _whole((HEADS, GCH, GCH)), _whole((GCH, LANE))]


def _mix(o, proj, ong, lng, lnb, ws, bst):
    def body(o_ref, z_ref, u_ref, v_ref, ong_ref, lng_ref, lnb_ref, ws_ref, bs_ref, m_ref):
        m_ref[...] = _mix_fn(o_ref[...], z_ref[...], u_ref[...], v_ref[...], ong_ref[...], lng_ref[...],
                             lnb_ref[...], ws_ref[...], bs_ref[...]).astype(BF16)

    return pl.pallas_call(
        body, name="mix", grid=(T // GCH,), in_specs=_mix_specs(),
        out_specs=pl.BlockSpec((GCH, D), lambda i: (i, 0)), out_shape=jax.ShapeDtypeStruct((T, D), BF16),
        compiler_params=_cparams(("parallel",)),
    )(o, proj, proj, proj, ong, lng, lnb, ws, bst)


def _mix_bwd(o, proj, ong, lng, lnb, ws, bst, dmix):
    def body(o_ref, z_ref, u_ref, v_ref, ong_ref, lng_ref, lnb_ref, ws_ref, bs_ref, dm_ref,
             do_ref, dzuv_ref, dong_ref, dlng_ref, dlnb_ref, dws_ref, dbs_ref):
        _, vjp = jax.vjp(_mix_fn, o_ref[...], z_ref[...], u_ref[...], v_ref[...], ong_ref[...], lng_ref[...],
                         lnb_ref[...], ws_ref[...], bs_ref[...])
        do, dz, du, dv, dong, dlng, dlnb, dws, dbs = vjp(dm_ref[...])
        do_ref[...] = do
        dzuv_ref[:, 0:HW] = dz.astype(BF16)
        dzuv_ref[:, HW:2 * HW] = du.astype(BF16)
        dzuv_ref[:, 2 * HW:3 * HW] = dv.astype(BF16)
        acc = [(dong_ref, dong), (dlng_ref, dlng), (dlnb_ref, dlnb), (dws_ref, dws), (dbs_ref, dbs)]

        @pl.when(pl.program_id(0) == 0)
        def _():
            for r, val in acc:
                r[...] = val

        @pl.when(pl.program_id(0) > 0)
        def _():
            for r, val in acc:
                r[...] += val

    shp = lambda *s: jax.ShapeDtypeStruct(s, F32)
    return pl.pallas_call(
        body, name="mix_bwd", grid=(T // GCH,),
        in_specs=_mix_specs() + [pl.BlockSpec((GCH, D), lambda i: (i, 0))],
        out_specs=[pl.BlockSpec((GCH, HW), lambda i: (i, 0)), pl.BlockSpec((GCH, 3 * HW), lambda i: (i, 1)),
                   _whole((1, HD)), _whole((1, HW)), _whole((1, HW)), _whole((HEADS, GCH, GCH)), _whole((GCH, LANE))],
        out_shape=[shp(T, HW), jax.ShapeDtypeStruct((T, NP), BF16), shp(1, HD), shp(1, HW), shp(1, HW),
                   shp(HEADS, GCH, GCH), shp(GCH, LANE)],
        compiler_params=_cparams(("arbitrary",)),
    )(o, proj, proj, proj, ong, lng, lnb, ws, bst, dmix)


def _swiglu_epilogue(accs, _):
    gate, up = accs
    return [gate, up, _silu(gate) * up]


def _swiglu_bwd_epilogue(accs, extras):
    dact = accs[0]
    gate, up = (e.astype(F32) for e in extras)
    sg = _sigmoid(gate)
    return [dact * up * (sg * (1.0 + gate * (1.0 - sg))), dact * (gate * sg)]


def _layer_fwd(h, p):
    hn = _rmsnorm("rms_mix", h, p["norm_mix"])
    proj = _mm("in_proj", "nn", hn[None], [p["w_in"][None]], tm=1024, tn=640, tk=D)[0][0]
    qkv = _prep(proj, p["conv_w"])
    bg = _gates(proj, p["a_log"], p["dt_bias"])
    prep = _chunk_prep(qkv, bg)
    o, s_hist = _chunk_scan(prep)
    if "late" in p:
        p.update(p.pop("late")(o))
    mix = _mix(o, proj, p["o_norm_g"], p["ln_v_g"], p["ln_v_b"], p["w_s"], p["bst"])
    h1 = _mm("out_proj", "nn", mix[None], [p["w_out"]], tm=1024, tn=512, tk=D // NCHIP, resid=h[None], n_n=D,
             b_spec=((None, D // NCHIP, 512), lambda g, i, j, k: (k, 0, j)))[0][0]
    h2n = _rmsnorm("rms_ffn", h1, p["norm_ffn"])
    gate, up, act = _mm("ffn_in", "nt", h2n[None], [p["w_gate"], p["w_up"]], tm=1024, tn=FF_SH, tk=D,
                        out_dtypes=(BF16, BF16, BF16), epilogue=_swiglu_epilogue)
    h2 = _mm("ffn_out", "nn", act, [p["w_down"]], tm=1024, tn=512, tk=FF_SH, reduce_g=True, resid=h1[None])[0][0]
    saved = dict(h=h, hn=hn, proj=proj, qkv=qkv, bg=bg, prep=prep, o=o, s_hist=s_hist, mix=mix, h1=h1, h2n=h2n,
                 gate=gate, up=up, act=act)
    return h2, saved


def _layer_bwd_ffn(dh2, dh2b, p, s, after=()):
    dh2b = dh2b[None]
    dgate, dup = _mm("ffn_out_bwd", "nt", dh2b, [p["w_down"]], tm=1024, tn=FF_SH, tk=D, out_dtypes=(BF16, BF16),
                     extras=(s["gate"], s["up"]), epilogue=_swiglu_bwd_epilogue, after=after)
    dh2n = _mm("ffn_gate_bwd", "nn", dgate, [p["w_gate"]], tm=1024, tn=512, tk=FF_SH, reduce_g=True)[0]
    dh2n = _mm("ffn_up_bwd", "nn", dup, [p["w_up"]], tm=1024, tn=512, tk=FF_SH, reduce_g=True, resid=dh2n)[0][0]
    dh1, dh1b, d_norm_ffn = _rmsnorm_bwd("rms_ffn_bwd", dh2n, s["h1"], p["norm_ffn"], dh2)
    d_w_down = _mm("ffn_wdown_grad", "tn", s["act"], [dh2b], tm=FF_SH, tn=512, tk=1024)[0]
    d_w_gate = _mm("ffn_wgate_grad", "tn", dgate, [s["h2n"][None]], tm=FF_SH, tn=512, tk=1024)[0]
    d_w_up = _mm("ffn_wup_grad", "tn", dup, [s["h2n"][None]], tm=FF_SH, tn=512, tk=1024)[0]
    return dh1, dh1b, dict(norm_ffn=d_norm_ffn, w_gate=d_w_gate, w_up=d_w_up, w_down=d_w_down)


def _layer_bwd_mixer(dh1, dh1b, p, s, after=(), midway=None):
    dh1b = dh1b[None]
    dmix = _mm("out_proj_bwd", "nt", dh1b, [p["w_out"]], tm=1024, tn=D // NCHIP, tk=D, n_n=D, after=after,
               b_spec=((None, D // NCHIP, D), lambda g, i, j, k: (j, 0, k)))[0][0]
    d_w_out = _mm("out_proj_wgrad", "tn", s["mix"][None], [dh1b], tm=512, tn=512, tk=1024)[0][0]
    do, dproj, d_ong, d_lng, d_lnb, d_ws, d_bst = _mix_bwd(
        s["o"], s["proj"], p["o_norm_g"], p["ln_v_g"], p["ln_v_b"], p["w_s"], p["bst"], dmix)
    then = midway(do) if midway is not None else ()
    dqkv, dbg = _chunk_prep_bwd(s["qkv"], s["bg"], _chunk_scan_bwd(s["prep"], s["s_hist"], do, then))
    dproj, d_conv = _prep_bwd(s["proj"], p["conv_w"], dqkv, dproj)
    dproj, d_a_log, d_dt_bias = _gates_bwd(s["proj"], p["a_log"], p["dt_bias"], dbg, dproj)
    dproj = dproj[None]
    dhn = _mm("in_proj_bwd", "nt", dproj, [p["w_in"][None]], tm=1024, tn=512, tk=640)[0][0]
    dh, dhb, d_norm_mix = _rmsnorm_bwd("rms_mix_bwd", dhn, s["h"], p["norm_mix"], dh1)
    d_w_in = _mm("in_proj_wgrad", "tn", s["hn"][None], [dproj], tm=512, tn=640, tk=1024)[0]
    grads = dict(norm_mix=d_norm_mix, w_in=d_w_in, conv_w=d_conv, a_log=d_a_log, dt_bias=d_dt_bias, o_norm_g=d_ong,
                 ln_v_g=d_lng, ln_v_b=d_lnb, w_s=d_ws, bst=d_bst, w_out=d_w_out)
    return dh, dhb, grads


def _lanes(v, off=0):
    return jnp.zeros((1, LANE), F32).at[0, off:off + v.shape[0]].set(v)


def _w_in_pieces():
    regions = [(0, 2048, 0), (2048, 2056, BA_OFF), (2056, IN_DIM, 2048)]
    sh = IN_DIM // NCHIP
    out = []
    for j in range(NCHIP):
        for lo, hi, at in regions:
            a, b = max(lo, j * sh), min(hi, (j + 1) * sh)
            if a < b:
                out.append((j, a - j * sh, at + a - lo, b - a))
    return out


W_IN_PIECES = _w_in_pieces()
WT = 256


def _assemble_w_in(gathered, own, place):
    def body(place_ref, g_ref, own_ref, o_ref):
        o_ref[:, IN_DIM:] = jnp.zeros((WT, NP - IN_DIM), BF16)
        mine = own_ref[...]
        for j, src, dst, width in W_IN_PIECES:
            val = jnp.where(place_ref[0] == j, mine[:, src:src + width], g_ref[j, :, src:src + width])
            o_ref[:, dst:dst + width] = val

    sh = IN_DIM // NCHIP
    return pl.pallas_call(
        body, name="assemble_w_in",
        grid_spec=pltpu.PrefetchScalarGridSpec(
            num_scalar_prefetch=1, grid=(D // WT,),
            in_specs=[pl.BlockSpec((NCHIP, WT, sh), lambda i, place_ref: (0, i, 0)),
                      pl.BlockSpec((WT, sh), lambda i, place_ref: (i, 0))],
            out_specs=pl.BlockSpec((WT, NP), lambda i, place_ref: (i, 0))),
        out_shape=jax.ShapeDtypeStruct((D, NP), BF16), compiler_params=_cparams(("parallel",)),
    )(place, gathered, own)


def _layer_params(l, big, small):
    return dict(
        {k: v for k, v in big.items() if k != "conv_w"},
        conv_w=jnp.concatenate([big["conv_w"][j, l] for j in range(NCHIP)], axis=1),
        norm_mix=small["norm_mix"][l][None], norm_ffn=small["norm_ffn"][l][None],
        a_log=_lanes(small["a_log"][l], HEADS), dt_bias=_lanes(small["dt_bias"][l], HEADS),
        o_norm_g=small["o_norm_g"][l][None], ln_v_g=small["ln_v_g"][l][None], ln_v_b=small["ln_v_b"][l][None],
        w_s=small["w_s"][l],
        bst=jnp.pad(small["b_s"][l].T, ((0, 0), (0, LANE - HEADS))),
    )


def _reference_layout(g):
    return dict(
        w_in=g["w_in"],
        w_out=g["w_out"].reshape(NCHIP, D // NCHIP, D),
        w_gate=g["w_gate"], w_up=g["w_up"], w_down=g["w_down"],
        conv_w=g["conv_w"], norm_mix=g["norm_mix"][0], norm_ffn=g["norm_ffn"][0],
        a_log=g["a_log"][0, HEADS:2 * HEADS], dt_bias=g["dt_bias"][0, HEADS:2 * HEADS],
        o_norm_g=g["o_norm_g"][0], ln_v_g=g["ln_v_g"][0], ln_v_b=g["ln_v_b"][0], w_s=g["w_s"],
        b_s=g["bst"][:, :HEADS].T,
    )


def _forward(x, tgt, layers, norm_final):
    h = x
    saved, params = [], []
    for p in layers:
        p = p(h) if callable(p) else p
        h, s = _layer_fwd(h, p)
        saved.append(s)
        params.append(p)
    return (saved, params) + tuple(_loss_head(h, norm_final, tgt))


def _local_step(x, tgt, layers, norm_final):
    saved, layers, loss, dh, dhb, d_norm_final = _forward(x, tgt, layers, norm_final)
    grads = [None] * DEPTH
    for l in reversed(range(DEPTH)):
        dh1, dh1b, g_ffn = _layer_bwd_ffn(dh, dhb, layers[l], saved[l])
        dh, dhb, g_mix = _layer_bwd_mixer(dh1, dh1b, layers[l], saved[l])
        grads[l] = {**g_ffn, **g_mix}
    return loss, dh, grads, d_norm_final


def _place():
    x, y, c = lax.axis_index("x"), lax.axis_index("y"), lax.axis_index("c")
    return x, y, c, [(1 - x, y), (x, 1 - y), (1 - x, 1 - y)]


def _remote(src, dst, send_sem, recv_sem, to):
    return pltpu.make_async_remote_copy(src_ref=src, dst_ref=dst, send_sem=send_sem, recv_sem=recv_sem,
                                        device_id=to, device_id_type=MESH)


def _comm_call(name, body, ins, out_shape, n_sems, aliases=None):
    return pl.pallas_call(
        body, name=name, in_specs=[ANY] * len(ins), out_specs=[ANY] * len(out_shape), out_shape=out_shape,
        scratch_shapes=[pltpu.SemaphoreType.DMA((n,)) for n in n_sems], input_output_aliases=aliases or {},
        compiler_params=pltpu.CompilerParams(has_side_effects=True),
    )(*ins)


def _half_rows(ref, of_c, dim):
    hr = ref.shape[dim] // 2
    return pl.ds(pl.multiple_of(of_c * hr, BF16_ROWS), hr)


def _gather_plan(whole):
    def plan(srcs, lands):
        x, y, c, others = _place()
        chip = 2 * x + y
        out = []
        for src, land, all_of_it in zip(srcs, lands, whole):
            for ox, oy in others:
                if all_of_it:
                    out.append((src, land.at[chip], (ox, oy, c)))
                else:
                    out.append((src.at[_half_rows(src, c, 0)], land.at[chip, _half_rows(src, c, 0)], (ox, oy, c)))
        return out
    return plan


def _forward_halves(lands):
    n = len(lands)

    def body(*refs):
        outs = refs[n:2 * n]
        send_s, recv_s = refs[2 * n:]
        x, y, c, others = _place()
        sibling = (x, y, 1 - c)
        copies = []
        for a in range(n):
            for k, (ox, oy) in enumerate(others):
                mine = outs[a].at[2 * ox + oy, _half_rows(outs[a], c, 1)]
                copies.append(_remote(mine, mine, send_s.at[3 * a + k], recv_s.at[3 * a + k], sibling))
        for cp in copies:
            cp.start()
        for a in range(n):
            for k, (ox, oy) in enumerate(others):
                landed = outs[a].at[2 * ox + oy, _half_rows(outs[a], 1 - c, 1)]
                _remote(landed, landed, send_s.at[3 * a + k], recv_s.at[3 * a + k], sibling).wait_recv()
        for cp in copies:
            cp.wait_send()

    out_shape = [jax.ShapeDtypeStruct(g.shape, g.dtype) for g in lands]
    return _comm_call("forward_halves", body, lands, out_shape, [3 * n, 3 * n], aliases={a: a for a in range(n)})


def _exchange_halves(gs):
    n = len(gs)

    def body(*refs):
        ins, outs = refs[:n], refs[n:2 * n]
        send_s, recv_s = refs[2 * n:]
        x, y, c, _ = _place()
        copies = []
        for a in range(n):
            hr = ins[a].shape[1] // 2
            theirs = ins[a].at[:, pl.ds(pl.multiple_of((1 - c) * hr, 8), hr)]
            copies.append(_remote(theirs, outs[a], send_s.at[a], recv_s.at[a], (x, y, 1 - c)))
        for cp in copies:
            cp.start()
        for cp in copies:
            cp.wait()

    out_shape = [jax.ShapeDtypeStruct((g.shape[0], g.shape[1] // 2, g.shape[2]), F32) for g in gs]
    return _comm_call("exchange_halves", body, gs, out_shape, [n, n])


def _scatter_partials(ps):
    n = len(ps)

    def body(*refs):
        ins, outs = refs[:n], refs[n:2 * n]
        send_s, recv_s = refs[2 * n:]
        x, y, c, others = _place()
        copies = []
        for a in range(n):
            for k, (ox, oy) in enumerate(others):
                copies.append(_remote(ins[a].at[2 * ox + oy], outs[a].at[k], send_s.at[3 * a + k],
                                      recv_s.at[3 * a + k], (ox, oy, c)))
        for cp in copies:
            cp.start()
        for cp in copies:
            cp.wait()

    out_shape = [jax.ShapeDtypeStruct((3,) + p.shape[1:], p.dtype) for p in ps]
    return _comm_call("scatter_partials", body, ps, out_shape, [3 * n, 3 * n])


HBM_SPEC = pl.BlockSpec(memory_space=pltpu.HBM)
SEM_SPEC = pl.BlockSpec(memory_space=pltpu.SEMAPHORE)
DATAFLOW = pltpu.SideEffectType.DATAFLOW_SIDE_EFFECTING


def _exchange_plan(srcs, lands):
    x, y, c, _ = _place()
    plan = []
    for src, land in zip(srcs, lands):
        hr = src.shape[1] // 2
        plan.append((src.at[:, pl.ds(pl.multiple_of((1 - c) * hr, 8), hr)], land, (x, y, 1 - c)))
    return plan


def _scatter_plan(srcs, lands):
    x, y, c, others = _place()
    return [(src.at[2 * ox + oy], land.at[k], (ox, oy, c))
            for src, land in zip(srcs, lands) for k, (ox, oy) in enumerate(others)]


def _split_start(name, plan, srcs, land_shapes, n_copies, after=()):
    n = len(srcs)
    lands = [pltpu.with_memory_space_constraint(lax.empty(s.shape, s.dtype), pltpu.HBM) for s in land_shapes]
    srcs = [pltpu.with_memory_space_constraint(s, pltpu.HBM) for s in srcs]

    def body(*refs):
        send_s, recv_s = refs[2 * n + len(after)], refs[2 * n + len(after) + 1]
        for i, (src, dst, to) in enumerate(plan(refs[:n], refs[n:2 * n])):
            _remote(src, dst, send_s.at[i], recv_s.at[i], to).start()
        refs[-1][...] = jnp.zeros_like(refs[-1])

    thru = [pltpu.HBM(s.shape, s.dtype) for s in srcs + lands]
    out = pl.pallas_call(
        body, name=name, in_specs=[HBM_SPEC] * (2 * n) + [ANY] * len(after),
        out_specs=[SEM_SPEC, SEM_SPEC] + [HBM_SPEC] * (2 * n) + [pl.BlockSpec(memory_space=pltpu.VMEM)],
        out_shape=[pltpu.SemaphoreType.DMA((n_copies,)), pltpu.SemaphoreType.DMA((n_copies,))] + thru
        + [jax.ShapeDtypeStruct((F32_ROWS, LANE), F32)],
        input_output_aliases={i: 2 + i for i in range(2 * n)},
        compiler_params=pltpu.CompilerParams(has_side_effects=DATAFLOW),
    )(*srcs, *lands, *after)
    return dict(sems=out[:2], srcs=out[2:2 + n], lands=out[2 + n:2 + 2 * n], token=out[-1])


def _split_wait(name, plan, started, after):
    n = len(started["srcs"])

    def body(*refs):
        send_s, recv_s = refs[2 * n], refs[2 * n + 1]
        for i, (src, dst, to) in enumerate(plan(refs[:n], refs[n:2 * n])):
            cp = _remote(src, dst, send_s.at[i], recv_s.at[i], to)
            cp.wait_send()
            cp.wait_recv()

    arrs = list(started["srcs"]) + list(started["lands"])
    out = pl.pallas_call(
        body, name=name, in_specs=[HBM_SPEC] * (2 * n) + [SEM_SPEC, SEM_SPEC, ANY],
        out_specs=[HBM_SPEC] * (2 * n), out_shape=[pltpu.HBM(s.shape, s.dtype) for s in arrs],
        input_output_aliases={i: i for i in range(2 * n)},
        compiler_params=pltpu.CompilerParams(has_side_effects=DATAFLOW),
    )(*arrs, *started["sems"], after)
    return out[:n], out[n:]


def _join_halves(rs):
    n = len(rs)

    def body(*refs):
        outs = refs[n:2 * n]
        send_s, recv_s = refs[2 * n:]
        x, y, c, _ = _place()
        sibling = (x, y, 1 - c)

        def half(a, of_c):
            hr = outs[a].shape[1] // 2
            return outs[a].at[:, pl.ds(pl.multiple_of(of_c * hr, 8), hr)]

        copies = [_remote(half(a, c), half(a, c), send_s.at[a], recv_s.at[a], sibling) for a in range(n)]
        for cp in copies:
            cp.start()
        for a in range(n):
            landed = half(a, 1 - c)
            _remote(landed, landed, send_s.at[a], recv_s.at[a], sibling).wait_recv()
        for cp in copies:
            cp.wait_send()

    out_shape = [jax.ShapeDtypeStruct(r.shape, r.dtype) for r in rs]
    return _comm_call("join_halves", body, rs, out_shape, [n, n], aliases={a: a for a in range(n)})


NDEV = 8


def _allreduce_small(buf):
    r = buf.shape[0]

    def body(in_ref, out_ref, gath, send_s, recv_s):
        x, y, c, _ = _place()
        me = 4 * x + 2 * y + c
        copies = []
        for rel in range(1, NDEV):
            px = 1 - x if rel & 4 else x
            py = 1 - y if rel & 2 else y
            pc = 1 - c if rel & 1 else c
            copies.append((_remote(in_ref, gath.at[me], send_s.at[rel - 1], recv_s.at[rel - 1], (px, py, pc)),
                           4 * px + 2 * py + pc))
        for cp, _ in copies:
            cp.start()
        gath[me] = in_ref[...]
        for rel, (cp, peer) in enumerate(copies):
            landed = gath.at[peer]
            _remote(landed, landed, send_s.at[rel], recv_s.at[rel], (x, y, c)).wait_recv()
        for cp, _ in copies:
            cp.wait_send()
        total = gath[0]
        for d in range(1, NDEV):
            total = total + gath[d]
        out_ref[...] = total

    vm = pl.BlockSpec(memory_space=pltpu.VMEM)
    return pl.pallas_call(
        body, name="allreduce_small", in_specs=[vm], out_specs=vm, out_shape=jax.ShapeDtypeStruct((r, LANE), F32),
        scratch_shapes=[pltpu.VMEM((NDEV, r, LANE), F32), pltpu.SemaphoreType.DMA((NDEV - 1,)),
                        pltpu.SemaphoreType.DMA((NDEV - 1,))],
        compiler_params=pltpu.CompilerParams(has_side_effects=True, vmem_limit_bytes=VMEM_LIMIT),
    )(buf)


MAX_ROW_TILE = 512
BF16_ROWS = 16


def _row_tile(rows):
    for t in range(min(rows, MAX_ROW_TILE) // BF16_ROWS * BF16_ROWS, 0, -BF16_ROWS):
        if rows % t == 0:
            return t
    raise ValueError(rows)


def _sum_halves(g, theirs, c_arr):
    nch, rows, cols = g.shape
    hr = rows // 2
    tr = _row_tile(hr)

    def body(c_ref, g_ref, t_ref, o_ref, ob_ref):
        s = g_ref[...] + t_ref[...]
        o_ref[...] = s
        ob_ref[...] = s.astype(BF16)

    blk = pl.BlockSpec((None, tr, cols), lambda j, i, c_ref: (j, i, 0))
    return pl.pallas_call(
        body, name="sum_halves",
        grid_spec=pltpu.PrefetchScalarGridSpec(
            num_scalar_prefetch=1, grid=(nch, hr // tr),
            in_specs=[pl.BlockSpec((None, None, tr, cols), lambda j, i, c_ref: (j, c_ref[0], i, 0)), blk],
            out_specs=[blk, blk]),
        out_shape=[jax.ShapeDtypeStruct((nch, hr, cols), F32), jax.ShapeDtypeStruct((nch, hr, cols), BF16)],
        compiler_params=_cparams(("parallel", "parallel")),
    )(c_arr, g.reshape(nch, 2, hr, cols), theirs)


def _sum_halves_w_in(g, theirs, c_arr):
    hr = D // 2
    sh = IN_DIM // NCHIP

    def body(c_ref, g_ref, t_ref, o_ref, ob_ref):
        s = g_ref[...] + t_ref[...]
        for j, dst, src, width in W_IN_PIECES:
            o_ref[j, :, dst:dst + width] = s[:, src:src + width]
            ob_ref[j, :, dst:dst + width] = s[:, src:src + width].astype(BF16)

    out = pl.BlockSpec((NCHIP, WT, sh), lambda i, c_ref: (0, i, 0))
    return pl.pallas_call(
        body, name="sum_halves_w_in",
        grid_spec=pltpu.PrefetchScalarGridSpec(
            num_scalar_prefetch=1, grid=(hr // WT,),
            in_specs=[pl.BlockSpec((None, WT, NP), lambda i, c_ref: (c_ref[0], i, 0)),
                      pl.BlockSpec((None, WT, NP), lambda i, c_ref: (0, i, 0))],
            out_specs=[out, out]),
        out_shape=[jax.ShapeDtypeStruct((NCHIP, hr, sh), F32), jax.ShapeDtypeStruct((NCHIP, hr, sh), BF16)],
        compiler_params=_cparams(("parallel",)),
    )(c_arr, g.reshape(2, hr, NP), theirs)


def _sum_chips(p, q, place, l, into=None):
    _, rows, cols = p.shape
    tr = _row_tile(rows)
    steps = rows // tr

    def body(place_ref, p_ref, q0, q1, q2, *rest):
        rest[-1][...] = ((p_ref[...] + q0[...].astype(F32)) + q1[...].astype(F32)) + q2[...].astype(F32)

    qs = lambda k: pl.BlockSpec((None, tr, cols), lambda i, place_ref: (k, i, 0))
    return pl.pallas_call(
        body, name="sum_chips",
        grid_spec=pltpu.PrefetchScalarGridSpec(
            num_scalar_prefetch=1, grid=(steps,),
            in_specs=[pl.BlockSpec((None, tr, cols), lambda i, place_ref: (place_ref[0], i, 0)), qs(0), qs(1), qs(2)]
            + ([ANY] if into is not None else []),
            out_specs=pl.BlockSpec((None, tr, cols), lambda i, place_ref: (l, place_ref[1] * steps + i, 0))),
        out_shape=jax.ShapeDtypeStruct((DEPTH, 2 * rows, cols), F32),
        input_output_aliases={5: 0} if into is not None else {},
        compiler_params=_cparams(("parallel",)),
    )(place, p, q, q, q, *([into] if into is not None else []))


def _adamw(w, g, m, v):
    layers, rows, cols = w.shape
    tr = _row_tile(rows)

    def body(w_ref, g_ref, m_ref, v_ref, d_ref, nm_ref, nv_ref):
        gv = g_ref[...]
        nm = ADAM_B1 * m_ref[...] + (1.0 - ADAM_B1) * gv
        nv = ADAM_B2 * v_ref[...] + (1.0 - ADAM_B2) * jnp.square(gv)
        m_hat = nm / (1.0 - ADAM_B1 ** ADAM_STEP)
        v_hat = nv / (1.0 - ADAM_B2 ** ADAM_STEP)
        d_ref[...] = -ADAM_LR * (m_hat / (jnp.sqrt(v_hat) + ADAM_EPS) + ADAM_WD * w_ref[...])
        nm_ref[...] = nm
        nv_ref[...] = nv

    blk = pl.BlockSpec((None, tr, cols), lambda l, i: (l, i, 0))
    return pl.pallas_call(
        body, name="adamw", grid=(layers, rows // tr), in_specs=[blk] * 4, out_specs=[blk] * 3,
        out_shape=[jax.ShapeDtypeStruct(w.shape, F32)] * 3, compiler_params=_cparams(("parallel", "parallel")),
    )(w, g, m, v)


BIG = ("w_in", "w_out", "w_gate", "w_up", "w_down")
SMALL = ("norm_mix", "a_log", "dt_bias", "o_norm_g", "ln_v_g", "ln_v_b", "w_s", "b_s", "norm_ffn", "norm_final")
ORDER = ("norm_mix", "w_in", "conv_w", "a_log", "dt_bias", "o_norm_g", "ln_v_g", "ln_v_b", "w_s", "b_s", "w_out",
         "norm_ffn", "w_gate", "w_up", "w_down", "norm_final")


F32_ROWS = 8
PACK_ROWS = 128


def _lane_rows(size):
    return -(-size // (F32_ROWS * LANE)) * F32_ROWS


def _pack(arrs):
    parts = [jnp.pad(a.reshape(-1), (0, _lane_rows(a.size) * LANE - a.size)).reshape(-1, LANE) for a in arrs]
    rows = sum(p.shape[0] for p in parts)
    if rows % PACK_ROWS:
        parts.append(jnp.zeros((-rows % PACK_ROWS, LANE), F32))
    return jnp.concatenate(parts, axis=0)


def _unpack(buf, like):
    out, row = [], 0
    for a in like:
        n = _lane_rows(a.size)
        out.append(buf[row:row + n].reshape(-1)[:a.size].reshape(a.shape))
        row += n
    return out


def kernel(x, norm_mix, w_in, conv_w, a_log, dt_bias, o_norm_g, ln_v_g, ln_v_b, w_s, b_s, w_out, norm_ffn, w_gate, w_up, w_down, norm_final, loss_target, m_norm_mix, m_w_in, m_conv_w, m_a_log, m_dt_bias, m_o_norm_g, m_ln_v_g, m_ln_v_b, m_w_s, m_b_s, m_w_out, m_norm_ffn, m_w_gate, m_w_up, m_w_down, m_norm_final, v_norm_mix, v_w_in, v_conv_w, v_a_log, v_dt_bias, v_o_norm_g, v_ln_v_g, v_ln_v_b, v_w_s, v_b_s, v_w_out, v_norm_ffn, v_w_gate, v_w_up, v_w_down, v_norm_final):
    w = dict(norm_mix=norm_mix, w_in=w_in, conv_w=conv_w, a_log=a_log, dt_bias=dt_bias, o_norm_g=o_norm_g,
             ln_v_g=ln_v_g, ln_v_b=ln_v_b, w_s=w_s, b_s=b_s, w_out=w_out, norm_ffn=norm_ffn, w_gate=w_gate, w_up=w_up,
             w_down=w_down, norm_final=norm_final)
    m = dict(norm_mix=m_norm_mix, w_in=m_w_in, conv_w=m_conv_w, a_log=m_a_log, dt_bias=m_dt_bias, o_norm_g=m_o_norm_g,
             ln_v_g=m_ln_v_g, ln_v_b=m_ln_v_b, w_s=m_w_s, b_s=m_b_s, w_out=m_w_out, norm_ffn=m_norm_ffn,
             w_gate=m_w_gate, w_up=m_w_up, w_down=m_w_down, norm_final=m_norm_final)
    v = dict(norm_mix=v_norm_mix, w_in=v_w_in, conv_w=v_conv_w, a_log=v_a_log, dt_bias=v_dt_bias, o_norm_g=v_o_norm_g,
             ln_v_g=v_ln_v_g, ln_v_b=v_ln_v_b, w_s=v_w_s, b_s=v_b_s, w_out=v_w_out, norm_ffn=v_norm_ffn,
             w_gate=v_w_gate, w_up=v_w_up, w_down=v_w_down, norm_final=v_norm_final)
    chip = 2 * lax.axis_index("x") + lax.axis_index("y")
    place = jnp.stack([chip, lax.axis_index("c")]).astype(jnp.int32)
    c_arr = place[1:]

    def kernel_view(n, a):
        return jnp.swapaxes(a, 1, 2) if n in ("w_gate", "w_up") else a

    own = {n: [kernel_view(n, w[n])[l].astype(BF16) for l in range(DEPTH)] for n in BIG}
    by_chip = lambda a: jax.ShapeDtypeStruct((NCHIP,) + a.shape, a.dtype)

    def start(name, srcs, whole, after=()):
        return _split_start(name, _gather_plan(whole), srcs, [by_chip(a) for a in srcs], 3 * len(srcs), after)

    def finish(name, started, whole, after):
        srcs, lands = _split_wait(name, _gather_plan(whole), started, after)
        passed = iter(_forward_halves([g for g, all_of_it in zip(lands, whole) if not all_of_it]))
        lands = [g if all_of_it else next(passed) for g, all_of_it in zip(lands, whole)]
        return srcs, [lax.dynamic_update_index_in_dim(g, o, chip, 0) for g, o in zip(lands, srcs)]

    ffn = BIG[1:]
    first = start("gather_first_start", [own["w_in"][0], conv_w], [False, True])
    early = start("gather_early_start", [own[n][0] for n in ffn], [False] * len(ffn), [first["token"]])
    later = start("gather_later_start", [own[n][1] for n in BIG], [False] * len(BIG), [early["token"]])
    (own_w_in, _), (w_in_by_chip, conv_by_chip) = finish("gather_first_wait", first, [False, True], later["token"])

    def late(after):
        return dict(zip(ffn, finish("gather_early_wait", early, [False] * len(ffn), after)[1]))

    layer0 = _layer_params(0, dict(w_in=_assemble_w_in(w_in_by_chip, own_w_in, place), conv_w=conv_by_chip, late=late), w)

    def layer1(after):
        srcs, by = finish("gather_later_wait", later, [False] * len(BIG), after)
        big = dict(zip(ffn, by[1:]), w_in=_assemble_w_in(by[0], srcs[0], place), conv_w=conv_by_chip)
        return _layer_params(1, big, w)

    saved, layers, loss_lanes, dh, dhb, d_norm_final = _forward(x[0], loss_target[0], [layer0, layer1],
                                                                 norm_final[None])
    loss = lax.psum(loss_lanes[0, 0], ("x", "y", "c"))

    sums, arrived = {}, {}

    def exchange_start(tag, l, names, grads, after=()):
        mine = [grads[n] for n in names]
        shapes = [jax.ShapeDtypeStruct((g.shape[0], g.shape[1] // 2, g.shape[2]), F32) for g in mine]
        return tag, l, names, _split_start(f"exchange_{tag}_start", _exchange_plan, mine, shapes, len(mine), after)

    def add_halves(l, names, mine, theirs):
        for n, g, t in zip(names, mine, theirs):
            sums[l, n] = (_sum_halves_w_in if n == "w_in" else _sum_halves)(g, t, c_arr)

    def exchange_wait(handle, after):
        tag, l, names, started = handle
        add_halves(l, names, *_split_wait(f"exchange_{tag}_wait", _exchange_plan, started, after))

    def scatter_start(tag, l, names, after=()):
        partial = [sums[l, n][1] for n in names]
        shapes = [jax.ShapeDtypeStruct((3,) + p.shape[1:], p.dtype) for p in partial]
        return tag, l, names, _split_start(f"scatter_{tag}_start", _scatter_plan, partial, shapes, 3 * len(names), after)

    def scatter_wait(handle, after):
        tag, l, names, started = handle
        for n, q in zip(names, _split_wait(f"scatter_{tag}_wait", _scatter_plan, started, after)[1]):
            arrived[l, n] = q

    last = DEPTH - 1
    swiglu = BIG[2:]
    dh1, dh1b, g_ffn = _layer_bwd_ffn(dh, dhb, layers[last], saved[last])
    dh, dhb, g_mix = _layer_bwd_mixer(dh1, dh1b, layers[last], saved[last])
    gl = [None, _reference_layout({**g_ffn, **g_mix})]
    ex_last = exchange_start("last", last, BIG, gl[last])
    dh1, dh1b, g_ffn = _layer_bwd_ffn(dh, dhb, layers[0], saved[0], after=[ex_last[-1]["token"]])
    exchange_wait(ex_last, dh1)
    sc_last = scatter_start("last", last, BIG)
    ex_ffn = exchange_start("swiglu", 0, swiglu, g_ffn, [sc_last[-1]["token"]])
    sc_ffn = []

    def midway(do):
        exchange_wait(ex_ffn, do)
        sc_ffn.append(scatter_start("swiglu", 0, swiglu))
        return [sc_ffn[0][-1]["token"]]

    dx, _, g_mix = _layer_bwd_mixer(dh1, dh1b, layers[0], saved[0], after=[ex_ffn[-1]["token"]], midway=midway)
    scatter_wait(sc_last, dx)
    scatter_wait(sc_ffn[0], dx)
    gl[0] = _reference_layout({**g_ffn, **g_mix})
    mine = [gl[0][n] for n in BIG[:2]]
    add_halves(0, BIG[:2], mine, _exchange_halves(mine))
    for n, q in zip(BIG[:2], _scatter_partials([sums[0, n][1] for n in BIG[:2]])):
        arrived[0, n] = q
    reduced = []
    for n in BIG:
        buf = None
        for l in range(DEPTH):
            buf = _sum_chips(sums[l, n][0], arrived[l, n], place, l, into=buf)
        reduced.append(buf)
    g_out = dict(zip(BIG, _join_halves(reduced)))

    small_g = [jnp.stack([gl[l][n] for l in range(DEPTH)]) for n in SMALL[:-1]] + [d_norm_final[0]]
    conv_g = jnp.stack([gl[l]["conv_w"] for l in range(DEPTH)])
    total = _allreduce_small(_pack(small_g + [conv_g]))
    *small_r, conv_r = _unpack(total, small_g + [conv_g])
    g_out.update(zip(SMALL, small_r))
    g_out["conv_w"] = lax.dynamic_slice_in_dim(conv_r, chip * conv_w.shape[2], conv_w.shape[2], axis=2)

    delta, new_m, new_v = {}, {}, {}
    for n in BIG:
        res = _adamw(kernel_view(n, w[n]), g_out[n], kernel_view(n, m[n]), kernel_view(n, v[n]))
        g_out[n], delta[n], new_m[n], new_v[n] = (kernel_view(n, a) for a in (g_out[n],) + tuple(res))
    rest = SMALL + ("conv_w",)
    like = [w[n] for n in rest]
    d, nm, nv = _adamw(*[_pack([src[n] for n in rest])[None] for src in (w, g_out, m, v)])
    for dst, buf in ((delta, d), (new_m, nm), (new_v, nv)):
        dst.update(zip(rest, _unpack(buf[0], like)))

    return (loss, dx[None], *[g_out[n] for n in ORDER], *[delta[n] for n in ORDER], *[new_m[n] for n in ORDER],
            *[new_v[n] for n in ORDER])
```

```python
import functools

import jax
import jax.numpy as jnp
from jax import lax
from jax.experimental import pallas as pl
from jax.experimental.pallas import tpu as pltpu

F32 = jnp.float32
BF16 = jnp.bfloat16
MESH = pl.DeviceIdType.MESH
ANY = pl.BlockSpec(memory_space=pl.ANY)
HIGHEST = lax.Precision.HIGHEST

T = 2048
D = 1024
DEPTH = 2
NCHIP = 4
HEADS = 4
HD = 128
HW = HEADS * HD
CH = 64
GCH = 128
IN_DIM = 3080
NP = 3200
BA_OFF = 3072
FF_SH = 704
EPS = 1e-6
LANE = 128
VMEM_LIMIT = 56 * 1024 * 1024

ADAM_LR = 0.001
ADAM_B1 = 0.9
ADAM_B2 = 0.999
ADAM_EPS = 1e-08
ADAM_WD = 0.01
ADAM_STEP = 10


def _cparams(sem=None):
    return pltpu.CompilerParams(dimension_semantics=sem, vmem_limit_bytes=VMEM_LIMIT)


_DIMS = {"nn": (((1,), (0,)), ((), ())), "nt": (((1,), (1,)), ((), ())), "tn": (((0,), (0,)), ((), ()))}


def _mm(name, mode, a, bs, *, tm, tn, tk, out_dtypes=(F32,), reduce_g=False, resid=None, extras=(), epilogue=None,
        b_spec=None, n_n=None, after=()):
    nb = len(bs)
    ga = a.shape[0]
    gbs = [1 if b_spec is not None else b.shape[0] for b in bs]
    g_n = max([ga] + gbs)
    if mode == "tn":
        k_n, m_n = a.shape[1:]
    else:
        m_n, k_n = a.shape[1:]
    if n_n is None:
        n_n = bs[0].shape[1] if mode == "nt" else bs[0].shape[2]
    assert m_n % tm == 0 and n_n % tn == 0 and k_n % tk == 0, (name, m_n, n_n, k_n)
    mi, nj, kk = m_n // tm, n_n // tn, k_n // tk
    if reduce_g:
        grid = (mi, nj, g_n, kk)
        ids = lambda i, j, g, k: (g, i, j, k)
        n_red = g_n * kk
        red_idx = lambda: pl.program_id(2) * kk + pl.program_id(3)
        sem = ("parallel", "parallel", "arbitrary", "arbitrary")
    else:
        grid = (g_n, mi, nj, kk)
        ids = lambda g, i, j, k: (g, i, j, k)
        n_red = kk
        red_idx = lambda: pl.program_id(3)
        sem = ("parallel", "parallel", "parallel", "arbitrary")

    def pick(gsz, g):
        return g if gsz > 1 else 0

    def a_map(*p):
        g, i, j, k = ids(*p)
        return (pick(ga, g), k, i) if mode == "tn" else (pick(ga, g), i, k)

    def b_map(gsz):
        def f(*p):
            g, i, j, k = ids(*p)
            if b_spec is not None:
                return b_spec[1](g, i, j, k)
            return (pick(gsz, g), j, k) if mode == "nt" else (pick(gsz, g), k, j)
        return f

    def o_map(gsz):
        def f(*p):
            g, i, j, k = ids(*p)
            return (0 if reduce_g else pick(gsz, g), i, j)
        return f

    a_spec = pl.BlockSpec((None, tk, tm) if mode == "tn" else (None, tm, tk), a_map)
    b_block = b_spec[0] if b_spec is not None else ((None, tn, tk) if mode == "nt" else (None, tk, tn))
    b_specs = [pl.BlockSpec(b_block, b_map(gs)) for gs in gbs]
    x_specs = [pl.BlockSpec((None, tm, tn), o_map(e.shape[0])) for e in extras]
    r_specs = [pl.BlockSpec((None, tm, tn), o_map(resid.shape[0]))] if resid is not None else []
    g_out = 1 if reduce_g else g_n
    out_shape = [jax.ShapeDtypeStruct((g_out, m_n, n_n), dt) for dt in out_dtypes]
    out_specs = [pl.BlockSpec((None, tm, tn), o_map(g_out)) for _ in out_dtypes]
    nx, nr, no = len(extras), len(r_specs), len(out_dtypes)
    n_in = 1 + nb + nx + nr + len(after)
    dims = _DIMS[mode]

    def body(*refs):
        a_ref = refs[0]
        b_refs = refs[1:1 + nb]
        x_refs = refs[1 + nb:1 + nb + nx]
        r_refs = refs[1 + nb + nx:1 + nb + nx + nr]
        o_refs = refs[n_in:n_in + no]
        acc_refs = refs[n_in + no:]
        av = a_ref[...]
        products = [lax.dot_general(av, b_ref[...], dims, preferred_element_type=F32) for b_ref in b_refs]

        def finish(accs):
            if r_refs:
                accs[0] = accs[0] + r_refs[0][...]
            outs = epilogue(accs, [x[...] for x in x_refs]) if epilogue is not None else accs
            for o_ref, o in zip(o_refs, outs):
                o_ref[...] = o.astype(o_ref.dtype)

        if n_red == 1:
            finish(products)
            return
        r = red_idx()
        for p, acc in zip(products, acc_refs):
            @pl.when(r == 0)
            def _():
                acc[...] = p

            @pl.when((r > 0) & (r < n_red - 1))
            def _():
                acc[...] += p

        @pl.when(r == n_red - 1)
        def _():
            finish([acc[...] + p for p, acc in zip(products, acc_refs)])

    return pl.pallas_call(
        body, name=name, grid=grid,
        in_specs=[a_spec] + b_specs + x_specs + r_specs + [ANY] * len(after),
        out_specs=out_specs, out_shape=out_shape,
        scratch_shapes=[pltpu.VMEM((tm, tn), F32) for _ in range(nb if n_red > 1 else 0)],
        compiler_params=_cparams(sem),
    )(a, *bs, *extras, *([resid] if resid is not None else []), *after)


def _sigmoid(x):
    return 1.0 / (1.0 + jnp.exp(-x))


def _silu(x):
    return x * _sigmoid(x)


def _gelu(x):
    return 0.5 * x * (1.0 + jnp.tanh(0.7978845608028654 * (x + 0.044715 * (x * x * x))))


def _rms_fn(h, gain):
    return h * lax.rsqrt(jnp.mean(h * h, axis=-1, keepdims=True) + EPS) * gain


def _shift_impl(x, s):
    n = x.shape[0]
    rolled = pltpu.roll(x, s % n, 0)
    row = lax.broadcasted_iota(jnp.int32, x.shape, 0)
    return jnp.where((row >= s) & (row < n + s), rolled, 0.0)


@functools.partial(jax.custom_vjp, nondiff_argnums=(1,))
def _shift(x, s):
    return _shift_impl(x, s)


def _shift_fwd(x, s):
    return _shift_impl(x, s), None


def _shift_bwd(s, _, g):
    return (_shift_impl(g, -s),)


_shift.defvjp(_shift_fwd, _shift_bwd)


def _prep_fn(x, w, qk_scale, is_v):
    y = x * w[3:4, :]
    for i in range(3):
        y = y + _shift(x, 3 - i) * w[i:i + 1, :]
    y = _silu(y)
    nrm = lax.rsqrt(jnp.sum(y * y, axis=-1, keepdims=True) + EPS) * qk_scale
    return y * jnp.where(is_v, 1.0, nrm)


def _softplus(x):
    return jnp.maximum(x, 0.0) + jnp.log(1.0 + jnp.exp(-jnp.abs(x)))


def _gates_fn(ba, a_log, dt_bias):
    lane = lax.broadcasted_iota(jnp.int32, ba.shape, 1)
    beta = _sigmoid(ba)
    g = -jnp.exp(a_log) * _softplus(ba + dt_bias)
    return jnp.where(lane < HEADS, beta, g)


def _dot16(a, b, dims=_DIMS["nn"]):
    return lax.dot_general(a.astype(BF16), b.astype(BF16), dims, preferred_element_type=F32)


def _dot32(a, b):
    return jnp.dot(a, b, preferred_element_type=F32, precision=HIGHEST)


def _dot3(a, b, dims=_DIMS["nn"]):
    return lax.dot_general(a, b, dims, preferred_element_type=F32, precision=lax.Precision.HIGH)


def _tri_inverses(mats):
    row = lax.broadcasted_iota(jnp.int32, (CH, CH), 0)
    col = lax.broadcasted_iota(jnp.int32, (CH, CH), 1)
    eye = (row == col).astype(F32)
    ts = [eye - a for a in mats]
    ps = list(mats)
    for _ in range(5):
        ps = [_dot3(p, p) for p in ps]
        ts = [t + _dot3(t, p) for t, p in zip(ts, ps)]
    return ts


@jax.custom_vjp
def _tri_solves(mats, rhs):
    return [_dot3(t, b) for t, b in zip(_tri_inverses(mats), rhs)]


def _tri_solves_fwd(mats, rhs):
    ts = _tri_inverses(mats)
    xs = [_dot3(t, b) for t, b in zip(ts, rhs)]
    return xs, (ts, xs)


def _tri_solves_bwd(res, dxs):
    ts, xs = res
    dbs = [_dot3(t, dx, _DIMS["tn"]) for t, dx in zip(ts, dxs)]
    return [-_dot3(db, x, _DIMS["nt"]) for db, x in zip(dbs, xs)], dbs


_tri_solves.defvjp(_tri_solves_fwd, _tri_solves_bwd)


def _chunk_prep_fn(xs, bgs):
    row = lax.broadcasted_iota(jnp.int32, (CH, CH), 0)
    col = lax.broadcasted_iota(jnp.int32, (CH, CH), 1)
    incl = row >= col
    strict = row > col
    lmat = incl.astype(F32)
    n = len(xs)
    items = [(i, h) for i in range(n) for h in range(HEADS)]
    part = lambda i, h, c: xs[i][:, c * HW + h * HD:c * HW + (h + 1) * HD]
    q = [part(i, h, 0) for i, h in items]
    k = [part(i, h, 1) for i, h in items]
    v = [part(i, h, 2) for i, h in items]
    beta = [bgs[i][:, h:h + 1] for i, h in items]
    gc_all = [_dot32(lmat, bg) for bg in bgs]
    gc = [gc_all[i][:, HEADS + h:HEADS + h + 1] for i, h in items]
    gmat = [jnp.where(strict, jnp.broadcast_to(bgs[i][:, HEADS + h:HEADS + h + 1], (CH, CH)), 0.0) for i, h in items]
    diff = [_dot3(lmat, m) for m in gmat]
    decay = [jnp.where(incl, jnp.exp(jnp.where(incl, d, 0.0)), 0.0) for d in diff]
    k_beta = [kk * b for kk, b in zip(k, beta)]
    kk_t = [_dot16(kb, kk, _DIMS["nt"]) for kb, kk in zip(k_beta, k)]
    qk_t = [_dot16(qq, kk, _DIMS["nt"]) for qq, kk in zip(q, k)]
    a = [jnp.where(strict, m * d, 0.0) for m, d in zip(kk_t, decay)]
    eg = [jnp.exp(g) for g in gc]
    rhs = [jnp.concatenate([vv * b, kb * e], axis=-1) for vv, b, kb, e in zip(v, beta, k_beta, eg)]
    uw = _tri_solves(a, rhs)
    qk = [m * d for m, d in zip(qk_t, decay)]
    g_last = [g[CH - 1:CH, :] for g in gc]
    qe = [qq * e for qq, e in zip(q, eg)]
    kd = [kk * jnp.exp(gl - g) for kk, gl, g in zip(k, g_last, gc)]
    egl = [jnp.broadcast_to(jnp.exp(gl), (1, HD)) for gl in g_last]
    out = []
    for i in range(n):
        mine = slice(i * HEADS, (i + 1) * HEADS)
        cat = lambda vals: jnp.concatenate(vals[mine], axis=-1)
        out.append((cat([x[:, :HD] for x in uw]), cat([x[:, HD:] for x in uw]), cat(qe), cat(kd),
                    jnp.concatenate([m[None] for m in qk[mine]], axis=0), cat(egl)))
    return out


def _chunk_state_fn(u, w, qe, kd, qk, egl, s):
    ws = [_dot16(a, b) for a, b in zip(w, s)]
    qs = [_dot16(a, b) for a, b in zip(qe, s)]
    v_new = [a - b for a, b in zip(u, ws)]
    o = [a + _dot16(b, c) for a, b, c in zip(qs, qk, v_new)]
    s_new = [a * e + _dot16(b, c, _DIMS["tn"]) for a, e, b, c in zip(s, egl, kd, v_new)]
    return o, s_new


def _mix_fn(o, z, ur, vr, ong, lng, lnb, ws, bst):
    row = lax.broadcasted_iota(jnp.int32, (GCH, GCH), 0)
    col = lax.broadcasted_iota(jnp.int32, (GCH, GCH), 1)
    causal = row >= col
    ug = _gelu(ur)
    vg = _gelu(vr)
    sls = [slice(h * HD, (h + 1) * HD) for h in range(HEADS)]
    oh = [o[:, sl] for sl in sls]
    oh = [x * lax.rsqrt(jnp.mean(x * x, axis=-1, keepdims=True) + EPS) for x in oh]
    outs_dn = [x * ong * _silu(z[:, sl]) for x, sl in zip(oh, sls)]
    vh = [vg[:, sl] for sl in sls]
    mu = [jnp.mean(x, axis=-1, keepdims=True) for x in vh]
    var = [jnp.mean(jnp.square(x - m), axis=-1, keepdims=True) for x, m in zip(vh, mu)]
    vn = [(x - m) * lax.rsqrt(s + EPS) * lng[:, sl] + lnb[:, sl] for x, m, s, sl in zip(vh, mu, var, sls)]
    mixed = [_dot16(jnp.where(causal, ws[h], 0.0), vn[h]) for h in range(HEADS)]
    outs_gm = [ug[:, sl] * (mixed[h] + bst[:, h:h + 1]) for h, sl in enumerate(sls)]
    return jnp.concatenate(outs_dn + outs_gm, axis=-1)


def _loss_fn(h, gain, tgt):
    y = _rms_fn(h, gain)
    return 0.5 * jnp.sum(jnp.mean(jnp.square(y - tgt), axis=-1))


RT = 256


def _rows(n=D):
    return pl.BlockSpec((RT, n), lambda i: (i, 0))


def _whole(shape):
    nd = len(shape)
    return pl.BlockSpec(shape, lambda i: (0,) * nd)


def _rmsnorm(name, h, gain):
    def body(h_ref, g_ref, o_ref):
        o_ref[...] = _rms_fn(h_ref[...], g_ref[...]).astype(BF16)

    return pl.pallas_call(
        body, name=name, grid=(T // RT,), in_specs=[_rows(), _whole((1, D))], out_specs=_rows(),
        out_shape=jax.ShapeDtypeStruct((T, D), BF16), compiler_params=_cparams(("parallel",)),
    )(h, gain)


def _rmsnorm_bwd(name, dhn, h, gain, resid):
    def body(dhn_ref, h_ref, g_ref, r_ref, dh_ref, dh16_ref, dg_ref):
        _, vjp = jax.vjp(_rms_fn, h_ref[...], g_ref[...])
        dh, dg = vjp(dhn_ref[...])
        dh = r_ref[...] + dh
        dh_ref[...] = dh
        dh16_ref[...] = dh.astype(BF16)

        @pl.when(pl.program_id(0) == 0)
        def _():
            dg_ref[...] = dg

        @pl.when(pl.program_id(0) > 0)
        def _():
            dg_ref[...] += dg

    return pl.pallas_call(
        body, name=name, grid=(T // RT,), in_specs=[_rows(), _rows(), _whole((1, D)), _rows()],
        out_specs=[_rows(), _rows(), _whole((1, D))],
        out_shape=[jax.ShapeDtypeStruct((T, D), F32), jax.ShapeDtypeStruct((T, D), BF16),
                   jax.ShapeDtypeStruct((1, D), F32)],
        compiler_params=_cparams(("arbitrary",)),
    )(dhn, h, gain, resid)


def _loss_head(h, gain, tgt):
    def body(h_ref, g_ref, t_ref, l_ref, dh_ref, dh16_ref, dg_ref):
        loss, vjp = jax.vjp(lambda hh, gg: _loss_fn(hh, gg, t_ref[...]), h_ref[...], g_ref[...])
        dh, dg = vjp(jnp.ones((), F32))
        dh_ref[...] = dh
        dh16_ref[...] = dh.astype(BF16)
        lv = jnp.full((1, LANE), loss, F32)

        @pl.when(pl.program_id(0) == 0)
        def _():
            dg_ref[...] = dg
            l_ref[...] = lv

        @pl.when(pl.program_id(0) > 0)
        def _():
            dg_ref[...] += dg
            l_ref[...] += lv

    return pl.pallas_call(
        body, name="loss_head", grid=(T // RT,), in_specs=[_rows(), _whole((1, D)), _rows()],
        out_specs=[_whole((1, LANE)), _rows(), _rows(), _whole((1, D))],
        out_shape=[jax.ShapeDtypeStruct((1, LANE), F32), jax.ShapeDtypeStruct((T, D), F32),
                   jax.ShapeDtypeStruct((T, D), BF16), jax.ShapeDtypeStruct((1, D), F32)],
        compiler_params=_cparams(("arbitrary",)),
    )(h, gain, tgt)


def _prep_flags():
    j = pl.program_id(0)
    qk_scale = jnp.where(j < HEADS, HD ** -0.5, 1.0).astype(F32)
    return qk_scale, j >= 2 * HEADS


def _prep(proj, conv_w):
    def body(x_ref, w_ref, o_ref):
        qk_scale, is_v = _prep_flags()
        o_ref[...] = _prep_fn(x_ref[...], w_ref[...], qk_scale, is_v)

    col = lambda j: (0, j)
    return pl.pallas_call(
        body, name="gdn_prep", grid=(3 * HEADS,),
        in_specs=[pl.BlockSpec((T, HD), col), pl.BlockSpec((4, HD), col)], out_specs=pl.BlockSpec((T, HD), col),
        out_shape=jax.ShapeDtypeStruct((T, 3 * HW), F32), compiler_params=_cparams(("parallel",)),
    )(proj, conv_w)


def _prep_bwd(proj, conv_w, dqkv, dproj):
    def body(x_ref, w_ref, d_ref, _, dx_ref, dw_ref):
        qk_scale, is_v = _prep_flags()
        _, vjp = jax.vjp(lambda x, w: _prep_fn(x, w, qk_scale, is_v), x_ref[...], w_ref[...])
        dx, dw = vjp(d_ref[...])
        dx_ref[...] = dx.astype(BF16)
        dw_ref[...] = dw

    col = lambda j: (0, j)
    return pl.pallas_call(
        body, name="gdn_prep_bwd", grid=(3 * HEADS,),
        in_specs=[pl.BlockSpec((T, HD), col), pl.BlockSpec((4, HD), col), pl.BlockSpec((T, HD), col), ANY],
        out_specs=[pl.BlockSpec((T, HD), col), pl.BlockSpec((4, HD), col)],
        out_shape=[jax.ShapeDtypeStruct((T, NP), BF16), jax.ShapeDtypeStruct((4, 3 * HW), F32)],
        input_output_aliases={3: 0}, compiler_params=_cparams(("parallel",)),
    )(proj, conv_w, dqkv, dproj)


BA_BLK = BA_OFF // LANE


def _gates(proj, a_log, dt_bias):
    def body(x_ref, a_ref, d_ref, o_ref):
        o_ref[...] = _gates_fn(x_ref[...], a_ref[...], d_ref[...])

    return pl.pallas_call(
        body, name="gdn_gates", grid=(1,),
        in_specs=[pl.BlockSpec((T, LANE), lambda i: (0, BA_BLK)), _whole((1, LANE)), _whole((1, LANE))],
        out_specs=_whole((T, LANE)),
        out_shape=jax.ShapeDtypeStruct((T, LANE), F32), compiler_params=_cparams(("arbitrary",)),
    )(proj, a_log, dt_bias)


def _gates_bwd(proj, a_log, dt_bias, dbg, dproj):
    def body(x_ref, a_ref, d_ref, dbg_ref, _, dx_ref, da_ref, dd_ref):
        _, vjp = jax.vjp(_gates_fn, x_ref[...], a_ref[...], d_ref[...])
        dx, da_ref[...], dd_ref[...] = vjp(dbg_ref[...])
        dx_ref[...] = dx.astype(BF16)

    ba = pl.BlockSpec((T, LANE), lambda i: (0, BA_BLK))
    return pl.pallas_call(
        body, name="gdn_gates_bwd", grid=(1,),
        in_specs=[ba, _whole((1, LANE)), _whole((1, LANE)), _whole((T, LANE)), ANY],
        out_specs=[ba, _whole((1, LANE)), _whole((1, LANE))],
        out_shape=[jax.ShapeDtypeStruct((T, NP), BF16), jax.ShapeDtypeStruct((1, LANE), F32),
                   jax.ShapeDtypeStruct((1, LANE), F32)],
        input_output_aliases={4: 0}, compiler_params=_cparams(("arbitrary",)),
    )(proj, a_log, dt_bias, dbg, dproj)


NCK = T // CH
CPS = 2


def _chunk_prep_specs(rev=False):
    at = (lambda n: NCK - 1 - n) if rev else (lambda n: n)
    wide = pl.BlockSpec((CH, HW), lambda n: (at(n), 0))
    return [wide, wide, wide, wide, pl.BlockSpec((HEADS, CH, CH), lambda n: (0, at(n), 0)),
            pl.BlockSpec((None, 1, HW), lambda n: (at(n), 0, 0))]


def _chunk_prep_shapes(dtypes):
    shp = [(T, HW), (T, HW), (T, HW), (T, HW), (HEADS, T, CH), (NCK, 1, HW)]
    return [jax.ShapeDtypeStruct(s, dt) for s, dt in zip(shp, dtypes)]


def _chunk_prep(qkv, bg):
    def body(x_ref, bg_ref, *o_refs):
        rows = [slice(ci * CH, (ci + 1) * CH) for ci in range(CPS)]
        res = _chunk_prep_fn([x_ref[r, :] for r in rows], [bg_ref[r, :] for r in rows])
        for ci, (u, w, qe, kd, qk, egl) in enumerate(res):
            for o_ref, val in zip(o_refs[:4], (u, w, qe, kd)):
                o_ref[rows[ci], :] = val.astype(o_ref.dtype)
            o_refs[4][:, rows[ci], :] = qk.astype(BF16)
            o_refs[5][ci] = egl

    wide = pl.BlockSpec((CPS * CH, HW), lambda n: (n, 0))
    return pl.pallas_call(
        body, name="gdn_chunk_prep", grid=(NCK // CPS,),
        in_specs=[pl.BlockSpec((CPS * CH, 3 * HW), lambda n: (n, 0)), pl.BlockSpec((CPS * CH, LANE), lambda n: (n, 0))],
        out_specs=[wide, wide, wide, wide, pl.BlockSpec((HEADS, CPS * CH, CH), lambda n: (0, n, 0)),
                   pl.BlockSpec((CPS, 1, HW), lambda n: (n, 0, 0))],
        out_shape=_chunk_prep_shapes((F32, BF16, BF16, BF16, BF16, F32)),
        compiler_params=_cparams(("parallel",)),
    )(qkv, bg)


def _chunk_prep_bwd(qkv, bg, cots):
    def body(x_ref, bg_ref, du, dw, dqe, dkd, dqk, degl, dx_ref, dbg_ref):
        rows = [slice(ci * CH, (ci + 1) * CH) for ci in range(CPS)]
        _, vjp = jax.vjp(_chunk_prep_fn, [x_ref[r, :] for r in rows], [bg_ref[r, :] for r in rows])
        dxs, dbgs = vjp([(du[r, :], dw[r, :], dqe[r, :], dkd[r, :], dqk[:, r, :], degl[ci])
                         for ci, r in enumerate(rows)])
        for r, dx, dbg in zip(rows, dxs, dbgs):
            dx_ref[r, :] = dx
            dbg_ref[r, :] = dbg

    wide = pl.BlockSpec((CPS * CH, HW), lambda n: (n, 0))
    return pl.pallas_call(
        body, name="gdn_chunk_prep_bwd", grid=(NCK // CPS,),
        in_specs=[pl.BlockSpec((CPS * CH, 3 * HW), lambda n: (n, 0)), pl.BlockSpec((CPS * CH, LANE), lambda n: (n, 0)),
                  wide, wide, wide, wide, pl.BlockSpec((HEADS, CPS * CH, CH), lambda n: (0, n, 0)),
                  pl.BlockSpec((CPS, 1, HW), lambda n: (n, 0, 0))],
        out_specs=[pl.BlockSpec((CPS * CH, 3 * HW), lambda n: (n, 0)), pl.BlockSpec((CPS * CH, LANE), lambda n: (n, 0))],
        out_shape=[jax.ShapeDtypeStruct((T, 3 * HW), F32), jax.ShapeDtypeStruct((T, LANE), F32)],
        compiler_params=_cparams(("parallel",)),
    )(qkv, bg, *cots)


def _head_args(refs):
    u, w, qe, kd, qk, egl = refs
    sls = [slice(h * HD, (h + 1) * HD) for h in range(HEADS)]
    return ([u[:, sl] for sl in sls], [w[:, sl].astype(F32) for sl in sls], [qe[:, sl].astype(F32) for sl in sls],
            [kd[:, sl].astype(F32) for sl in sls], [qk[h].astype(F32) for h in range(HEADS)],
            [egl[:, sl] for sl in sls])


def _chunk_scan(prep):
    def body(*refs):
        o_ref, sh_ref, s_ref = refs[6:]

        @pl.when(pl.program_id(0) == 0)
        def _():
            s_ref[...] = jnp.zeros_like(s_ref)

        s = [s_ref[h] for h in range(HEADS)]
        for h in range(HEADS):
            sh_ref[h, 0] = s[h]
        o, s_new = _chunk_state_fn(*_head_args(refs[:6]), s)
        for h in range(HEADS):
            o_ref[:, h * HD:(h + 1) * HD] = o[h]
            s_ref[h] = s_new[h]

    return pl.pallas_call(
        body, name="gdn_scan", grid=(NCK,), in_specs=_chunk_prep_specs(),
        out_specs=[pl.BlockSpec((CH, HW), lambda n: (n, 0)), pl.BlockSpec((HEADS, 1, HD, HD), lambda n: (0, n, 0, 0))],
        out_shape=[jax.ShapeDtypeStruct((T, HW), F32), jax.ShapeDtypeStruct((HEADS, NCK, HD, HD), F32)],
        scratch_shapes=[pltpu.VMEM((HEADS, HD, HD), F32)], compiler_params=_cparams(("arbitrary",)),
    )(*prep)


def _chunk_scan_bwd(prep, s_hist, do, after=()):
    n_in = 8 + len(after)

    def body(*refs):
        sh_ref, do_ref = refs[6:8]
        d_refs = refs[n_in:n_in + 6]
        ds_ref = refs[n_in + 6]

        @pl.when(pl.program_id(0) == 0)
        def _():
            ds_ref[...] = jnp.zeros_like(ds_ref)

        sls = [slice(h * HD, (h + 1) * HD) for h in range(HEADS)]
        _, vjp = jax.vjp(_chunk_state_fn, *_head_args(refs[:6]), [sh_ref[h, 0] for h in range(HEADS)])
        du, dw, dqe, dkd, dqk, degl, ds = vjp(([do_ref[:, sl] for sl in sls], [ds_ref[h] for h in range(HEADS)]))
        for h, sl in enumerate(sls):
            for d_ref, val in zip(d_refs[:4], (du, dw, dqe, dkd)):
                d_ref[:, sl] = val[h]
            d_refs[4][h] = dqk[h]
            d_refs[5][:, sl] = degl[h]
            ds_ref[h] = ds[h]

    rev = lambda n: NCK - 1 - n
    return pl.pallas_call(
        body, name="gdn_scan_bwd", grid=(NCK,),
        in_specs=_chunk_prep_specs(rev=True) + [pl.BlockSpec((HEADS, 1, HD, HD), lambda n: (0, rev(n), 0, 0)),
                                                pl.BlockSpec((CH, HW), lambda n: (rev(n), 0))] + [ANY] * len(after),
        out_specs=_chunk_prep_specs(rev=True), out_shape=_chunk_prep_shapes((F32,) * 6),
        scratch_shapes=[pltpu.VMEM((HEADS, HD, HD), F32)], compiler_params=_cparams(("arbitrary",)),
    )(*prep, s_hist, do, *after)


def _mix_specs():
    pc = lambda c: pl.BlockSpec((GCH, HW), lambda i: (i, c))
    return [pl.BlockSpec((GCH, HW), lambda i: (i, 0)), pc(3), pc(4), pc(5), _whole((1, HD)), _whole((1, HW)),
            _whole((1, HW)), _whole((HEADS, GCH, GCH)), _whole((GCH, LANE))]


def _mix(o, proj, ong, lng, lnb, ws, bst):
    def body(o_ref, z_ref, u_ref, v_ref, ong_ref, lng_ref, lnb_ref, ws_ref, bs_ref, m_ref):
        m_ref[...] = _mix_fn(o_ref[...], z_ref[...], u_ref[...], v_ref[...], ong_ref[...], lng_ref[...],
                             lnb_ref[...], ws_ref[...], bs_ref[...]).astype(BF16)

    return pl.pallas_call(
        body, name="mix", grid=(T // GCH,), in_specs=_mix_specs(),
        out_specs=pl.BlockSpec((GCH, D), lambda i: (i, 0)), out_shape=jax.ShapeDtypeStruct((T, D), BF16),
        compiler_params=_cparams(("parallel",)),
    )(o, proj, proj, proj, ong, lng, lnb, ws, bst)


def _mix_bwd(o, proj, ong, lng, lnb, ws, bst, dmix):
    def body(o_ref, z_ref, u_ref, v_ref, ong_ref, lng_ref, lnb_ref, ws_ref, bs_ref, dm_ref,
             do_ref, dzuv_ref, dong_ref, dlng_ref, dlnb_ref, dws_ref, dbs_ref):
        _, vjp = jax.vjp(_mix_fn, o_ref[...], z_ref[...], u_ref[...], v_ref[...], ong_ref[...], lng_ref[...],
                         lnb_ref[...], ws_ref[...], bs_ref[...])
        do, dz, du, dv, dong, dlng, dlnb, dws, dbs = vjp(dm_ref[...])
        do_ref[...] = do
        dzuv_ref[:, 0:HW] = dz.astype(BF16)
        dzuv_ref[:, HW:2 * HW] = du.astype(BF16)
        dzuv_ref[:, 2 * HW:3 * HW] = dv.astype(BF16)
        acc = [(dong_ref, dong), (dlng_ref, dlng), (dlnb_ref, dlnb), (dws_ref, dws), (dbs_ref, dbs)]

        @pl.when(pl.program_id(0) == 0)
        def _():
            for r, val in acc:
                r[...] = val

        @pl.when(pl.program_id(0) > 0)
        def _():
            for r, val in acc:
                r[...] += val

    shp = lambda *s: jax.ShapeDtypeStruct(s, F32)
    return pl.pallas_call(
        body, name="mix_bwd", grid=(T // GCH,),
        in_specs=_mix_specs() + [pl.BlockSpec((GCH, D), lambda i: (i, 0))],
        out_specs=[pl.BlockSpec((GCH, HW), lambda i: (i, 0)), pl.BlockSpec((GCH, 3 * HW), lambda i: (i, 1)),
                   _whole((1, HD)), _whole((1, HW)), _whole((1, HW)), _whole((HEADS, GCH, GCH)), _whole((GCH, LANE))],
        out_shape=[shp(T, HW), jax.ShapeDtypeStruct((T, NP), BF16), shp(1, HD), shp(1, HW), shp(1, HW),
                   shp(HEADS, GCH, GCH), shp(GCH, LANE)],
        compiler_params=_cparams(("arbitrary",)),
    )(o, proj, proj, proj, ong, lng, lnb, ws, bst, dmix)


def _swiglu_epilogue(accs, _):
    gate, up = accs
    return [gate, up, _silu(gate) * up]


def _swiglu_bwd_epilogue(accs, extras):
    dact = accs[0]
    gate, up = (e.astype(F32) for e in extras)
    sg = _sigmoid(gate)
    return [dact * up * (sg * (1.0 + gate * (1.0 - sg))), dact * (gate * sg)]


def _layer_fwd(h, p):
    hn = _rmsnorm("rms_mix", h, p["norm_mix"])
    proj = _mm("in_proj", "nn", hn[None], [p["w_in"][None]], tm=1024, tn=640, tk=D)[0][0]
    qkv = _prep(proj, p["conv_w"])
    bg = _gates(proj, p["a_log"], p["dt_bias"])
    prep = _chunk_prep(qkv, bg)
    o, s_hist = _chunk_scan(prep)
    if "late" in p:
        p.update(p.pop("late")(o))
    mix = _mix(o, proj, p["o_norm_g"], p["ln_v_g"], p["ln_v_b"], p["w_s"], p["bst"])
    h1 = _mm("out_proj", "nn", mix[None], [p["w_out"]], tm=1024, tn=512, tk=D // NCHIP, resid=h[None], n_n=D,
             b_spec=((None, D // NCHIP, 512), lambda g, i, j, k: (k, 0, j)))[0][0]
    h2n = _rmsnorm("rms_ffn", h1, p["norm_ffn"])
    gate, up, act = _mm("ffn_in", "nt", h2n[None], [p["w_gate"], p["w_up"]], tm=1024, tn=FF_SH, tk=D,
                        out_dtypes=(BF16, BF16, BF16), epilogue=_swiglu_epilogue)
    h2 = _mm("ffn_out", "nn", act, [p["w_down"]], tm=1024, tn=512, tk=FF_SH, reduce_g=True, resid=h1[None])[0][0]
    saved = dict(h=h, hn=hn, proj=proj, qkv=qkv, bg=bg, prep=prep, o=o, s_hist=s_hist, mix=mix, h1=h1, h2n=h2n,
                 gate=gate, up=up, act=act)
    return h2, saved


def _layer_bwd_ffn(dh2, dh2b, p, s, after=()):
    dh2b = dh2b[None]
    dgate, dup = _mm("ffn_out_bwd", "nt", dh2b, [p["w_down"]], tm=1024, tn=FF_SH, tk=D, out_dtypes=(BF16, BF16),
                     extras=(s["gate"], s["up"]), epilogue=_swiglu_bwd_epilogue, after=after)
    dh2n = _mm("ffn_gate_bwd", "nn", dgate, [p["w_gate"]], tm=1024, tn=512, tk=FF_SH, reduce_g=True)[0]
    dh2n = _mm("ffn_up_bwd", "nn", dup, [p["w_up"]], tm=1024, tn=512, tk=FF_SH, reduce_g=True, resid=dh2n)[0][0]
    dh1, dh1b, d_norm_ffn = _rmsnorm_bwd("rms_ffn_bwd", dh2n, s["h1"], p["norm_ffn"], dh2)
    d_w_down = _mm("ffn_wdown_grad", "tn", s["act"], [dh2b], tm=FF_SH, tn=512, tk=T)[0]
    d_w_gate = _mm("ffn_wgate_grad", "tn", dgate, [s["h2n"][None]], tm=FF_SH, tn=512, tk=T)[0]
    d_w_up = _mm("ffn_wup_grad", "tn", dup, [s["h2n"][None]], tm=FF_SH, tn=512, tk=T)[0]
    return dh1, dh1b, dict(norm_ffn=d_norm_ffn, w_gate=d_w_gate, w_up=d_w_up, w_down=d_w_down)


def _layer_bwd_mixer(dh1, dh1b, p, s, after=(), midway=None):
    dh1b = dh1b[None]
    dmix = _mm("out_proj_bwd", "nt", dh1b, [p["w_out"]], tm=1024, tn=D // NCHIP, tk=D, n_n=D, after=after,
               b_spec=((None, D // NCHIP, D), lambda g, i, j, k: (j, 0, k)))[0][0]
    d_w_out = _mm("out_proj_wgrad", "tn", s["mix"][None], [dh1b], tm=512, tn=512, tk=T)[0][0]
    do, dproj, d_ong, d_lng, d_lnb, d_ws, d_bst = _mix_bwd(
        s["o"], s["proj"], p["o_norm_g"], p["ln_v_g"], p["ln_v_b"], p["w_s"], p["bst"], dmix)
    then = midway(do) if midway is not None else ()
    dqkv, dbg = _chunk_prep_bwd(s["qkv"], s["bg"], _chunk_scan_bwd(s["prep"], s["s_hist"], do, then))
    dproj, d_conv = _prep_bwd(s["proj"], p["conv_w"], dqkv, dproj)
    dproj, d_a_log, d_dt_bias = _gates_bwd(s["proj"], p["a_log"], p["dt_bias"], dbg, dproj)
    dproj = dproj[None]
    dhn = _mm("in_proj_bwd", "nt", dproj, [p["w_in"][None]], tm=1024, tn=512, tk=NP)[0][0]
    dh, dhb, d_norm_mix = _rmsnorm_bwd("rms_mix_bwd", dhn, s["h"], p["norm_mix"], dh1)
    d_w_in = _mm("in_proj_wgrad", "tn", s["hn"][None], [dproj], tm=512, tn=640, tk=T)[0]
    grads = dict(norm_mix=d_norm_mix, w_in=d_w_in, conv_w=d_conv, a_log=d_a_log, dt_bias=d_dt_bias, o_norm_g=d_ong,
                 ln_v_g=d_lng, ln_v_b=d_lnb, w_s=d_ws, bst=d_bst, w_out=d_w_out)
    return dh, dhb, grads


def _lanes(v, off=0):
    return jnp.zeros((1, LANE), F32).at[0, off:off + v.shape[0]].set(v)


def _w_in_pieces():
    regions = [(0, 2048, 0), (2048, 2056, BA_OFF), (2056, IN_DIM, 2048)]
    sh = IN_DIM // NCHIP
    out = []
    for j in range(NCHIP):
        for lo, hi, at in regions:
            a, b = max(lo, j * sh), min(hi, (j + 1) * sh)
            if a < b:
                out.append((j, a - j * sh, at + a - lo, b - a))
    return out


W_IN_PIECES = _w_in_pieces()
WT = 256


def _assemble_w_in(gathered, own, place):
    def body(place_ref, g_ref, own_ref, o_ref):
        o_ref[:, IN_DIM:] = jnp.zeros((WT, NP - IN_DIM), BF16)
        mine = own_ref[...]
        for j, src, dst, width in W_IN_PIECES:
            val = jnp.where(place_ref[0] == j, mine[:, src:src + width], g_ref[j, :, src:src + width])
            o_ref[:, dst:dst + width] = val

    sh = IN_DIM // NCHIP
    return pl.pallas_call(
        body, name="assemble_w_in",
        grid_spec=pltpu.PrefetchScalarGridSpec(
            num_scalar_prefetch=1, grid=(D // WT,),
            in_specs=[pl.BlockSpec((NCHIP, WT, sh), lambda i, place_ref: (0, i, 0)),
                      pl.BlockSpec((WT, sh), lambda i, place_ref: (i, 0))],
            out_specs=pl.BlockSpec((WT, NP), lambda i, place_ref: (i, 0))),
        out_shape=jax.ShapeDtypeStruct((D, NP), BF16), compiler_params=_cparams(("parallel",)),
    )(place, gathered, own)


def _layer_params(l, big, small):
    return dict(
        {k: v for k, v in big.items() if k != "conv_w"},
        conv_w=jnp.concatenate([big["conv_w"][j, l] for j in range(NCHIP)], axis=1),
        norm_mix=small["norm_mix"][l][None], norm_ffn=small["norm_ffn"][l][None],
        a_log=_lanes(small["a_log"][l], HEADS), dt_bias=_lanes(small["dt_bias"][l], HEADS),
        o_norm_g=small["o_norm_g"][l][None], ln_v_g=small["ln_v_g"][l][None], ln_v_b=small["ln_v_b"][l][None],
        w_s=small["w_s"][l],
        bst=jnp.pad(small["b_s"][l].T, ((0, 0), (0, LANE - HEADS))),
    )


def _reference_layout(g):
    return dict(
        w_in=g["w_in"],
        w_out=g["w_out"].reshape(NCHIP, D // NCHIP, D),
        w_gate=g["w_gate"], w_up=g["w_up"], w_down=g["w_down"],
        conv_w=g["conv_w"], norm_mix=g["norm_mix"][0], norm_ffn=g["norm_ffn"][0],
        a_log=g["a_log"][0, HEADS:2 * HEADS], dt_bias=g["dt_bias"][0, HEADS:2 * HEADS],
        o_norm_g=g["o_norm_g"][0], ln_v_g=g["ln_v_g"][0], ln_v_b=g["ln_v_b"][0], w_s=g["w_s"],
        b_s=g["bst"][:, :HEADS].T,
    )


def _forward(x, tgt, layers, norm_final):
    h = x
    saved, params = [], []
    for p in layers:
        p = p(h) if callable(p) else p
        h, s = _layer_fwd(h, p)
        saved.append(s)
        params.append(p)
    return (saved, params) + tuple(_loss_head(h, norm_final, tgt))


def _local_step(x, tgt, layers, norm_final):
    saved, layers, loss, dh, dhb, d_norm_final = _forward(x, tgt, layers, norm_final)
    grads = [None] * DEPTH
    for l in reversed(range(DEPTH)):
        dh1, dh1b, g_ffn = _layer_bwd_ffn(dh, dhb, layers[l], saved[l])
        dh, dhb, g_mix = _layer_bwd_mixer(dh1, dh1b, layers[l], saved[l])
        grads[l] = {**g_ffn, **g_mix}
    return loss, dh, grads, d_norm_final


def _place():
    x, y, c = lax.axis_index("x"), lax.axis_index("y"), lax.axis_index("c")
    return x, y, c, [(1 - x, y), (x, 1 - y), (1 - x, 1 - y)]


def _remote(src, dst, send_sem, recv_sem, to):
    return pltpu.make_async_remote_copy(src_ref=src, dst_ref=dst, send_sem=send_sem, recv_sem=recv_sem,
                                        device_id=to, device_id_type=MESH)


def _comm_call(name, body, ins, out_shape, n_sems, aliases=None):
    return pl.pallas_call(
        body, name=name, in_specs=[ANY] * len(ins), out_specs=[ANY] * len(out_shape), out_shape=out_shape,
        scratch_shapes=[pltpu.SemaphoreType.DMA((n,)) for n in n_sems], input_output_aliases=aliases or {},
        compiler_params=pltpu.CompilerParams(has_side_effects=True),
    )(*ins)


def _half_rows(ref, of_c, dim):
    hr = ref.shape[dim] // 2
    return pl.ds(pl.multiple_of(of_c * hr, BF16_ROWS), hr)


def _gather_plan(whole):
    def plan(srcs, lands):
        x, y, c, others = _place()
        chip = 2 * x + y
        out = []
        for src, land, all_of_it in zip(srcs, lands, whole):
            for ox, oy in others:
                if all_of_it:
                    out.append((src, land.at[chip], (ox, oy, c)))
                else:
                    out.append((src.at[_half_rows(src, c, 0)], land.at[chip, _half_rows(src, c, 0)], (ox, oy, c)))
        return out
    return plan


def _forward_halves(lands):
    n = len(lands)

    def body(*refs):
        outs = refs[n:2 * n]
        send_s, recv_s = refs[2 * n:]
        x, y, c, others = _place()
        sibling = (x, y, 1 - c)
        copies = []
        for a in range(n):
            for k, (ox, oy) in enumerate(others):
                mine = outs[a].at[2 * ox + oy, _half_rows(outs[a], c, 1)]
                copies.append(_remote(mine, mine, send_s.at[3 * a + k], recv_s.at[3 * a + k], sibling))
        for cp in copies:
            cp.start()
        for a in range(n):
            for k, (ox, oy) in enumerate(others):
                landed = outs[a].at[2 * ox + oy, _half_rows(outs[a], 1 - c, 1)]
                _remote(landed, landed, send_s.at[3 * a + k], recv_s.at[3 * a + k], sibling).wait_recv()
        for cp in copies:
            cp.wait_send()

    out_shape = [jax.ShapeDtypeStruct(g.shape, g.dtype) for g in lands]
    return _comm_call("forward_halves", body, lands, out_shape, [3 * n, 3 * n], aliases={a: a for a in range(n)})


def _exchange_halves(gs):
    n = len(gs)

    def body(*refs):
        ins, outs = refs[:n], refs[n:2 * n]
        send_s, recv_s = refs[2 * n:]
        x, y, c, _ = _place()
        copies = []
        for a in range(n):
            hr = ins[a].shape[1] // 2
            theirs = ins[a].at[:, pl.ds(pl.multiple_of((1 - c) * hr, 8), hr)]
            copies.append(_remote(theirs, outs[a], send_s.at[a], recv_s.at[a], (x, y, 1 - c)))
        for cp in copies:
            cp.start()
        for cp in copies:
            cp.wait()

    out_shape = [jax.ShapeDtypeStruct((g.shape[0], g.shape[1] // 2, g.shape[2]), F32) for g in gs]
    return _comm_call("exchange_halves", body, gs, out_shape, [n, n])


def _scatter_partials(ps):
    n = len(ps)

    def body(*refs):
        ins, outs = refs[:n], refs[n:2 * n]
        send_s, recv_s = refs[2 * n:]
        x, y, c, others = _place()
        copies = []
        for a in range(n):
            for k, (ox, oy) in enumerate(others):
                copies.append(_remote(ins[a].at[2 * ox + oy], outs[a].at[k], send_s.at[3 * a + k],
                                      recv_s.at[3 * a + k], (ox, oy, c)))
        for cp in copies:
            cp.start()
        for cp in copies:
            cp.wait()

    out_shape = [jax.ShapeDtypeStruct((3,) + p.shape[1:], p.dtype) for p in ps]
    return _comm_call("scatter_partials", body, ps, out_shape, [3 * n, 3 * n])


HBM_SPEC = pl.BlockSpec(memory_space=pltpu.HBM)
SEM_SPEC = pl.BlockSpec(memory_space=pltpu.SEMAPHORE)
DATAFLOW = pltpu.SideEffectType.DATAFLOW_SIDE_EFFECTING


def _exchange_plan(srcs, lands):
    x, y, c, _ = _place()
    plan = []
    for src, land in zip(srcs, lands):
        hr = src.shape[1] // 2
        plan.append((src.at[:, pl.ds(pl.multiple_of((1 - c) * hr, 8), hr)], land, (x, y, 1 - c)))
    return plan


def _scatter_plan(srcs, lands):
    x, y, c, others = _place()
    return [(src.at[2 * ox + oy], land.at[k], (ox, oy, c))
            for src, land in zip(srcs, lands) for k, (ox, oy) in enumerate(others)]


def _split_start(name, plan, srcs, land_shapes, n_copies, after=()):
    n = len(srcs)
    lands = [pltpu.with_memory_space_constraint(lax.empty(s.shape, s.dtype), pltpu.HBM) for s in land_shapes]
    srcs = [pltpu.with_memory_space_constraint(s, pltpu.HBM) for s in srcs]

    def body(*refs):
        send_s, recv_s = refs[2 * n + len(after)], refs[2 * n + len(after) + 1]
        for i, (src, dst, to) in enumerate(plan(refs[:n], refs[n:2 * n])):
            _remote(src, dst, send_s.at[i], recv_s.at[i], to).start()
        refs[-1][...] = jnp.zeros_like(refs[-1])

    thru = [pltpu.HBM(s.shape, s.dtype) for s in srcs + lands]
    out = pl.pallas_call(
        body, name=name, in_specs=[HBM_SPEC] * (2 * n) + [ANY] * len(after),
        out_specs=[SEM_SPEC, SEM_SPEC] + [HBM_SPEC] * (2 * n) + [pl.BlockSpec(memory_space=pltpu.VMEM)],
        out_shape=[pltpu.SemaphoreType.DMA((n_copies,)), pltpu.SemaphoreType.DMA((n_copies,))] + thru
        + [jax.ShapeDtypeStruct((F32_ROWS, LANE), F32)],
        input_output_aliases={i: 2 + i for i in range(2 * n)},
        compiler_params=pltpu.CompilerParams(has_side_effects=DATAFLOW),
    )(*srcs, *lands, *after)
    return dict(sems=out[:2], srcs=out[2:2 + n], lands=out[2 + n:2 + 2 * n], token=out[-1])


def _split_wait(name, plan, started, after):
    n = len(started["srcs"])

    def body(*refs):
        send_s, recv_s = refs[2 * n], refs[2 * n + 1]
        for i, (src, dst, to) in enumerate(plan(refs[:n], refs[n:2 * n])):
            cp = _remote(src, dst, send_s.at[i], recv_s.at[i], to)
            cp.wait_send()
            cp.wait_recv()

    arrs = list(started["srcs"]) + list(started["lands"])
    out = pl.pallas_call(
        body, name=name, in_specs=[HBM_SPEC] * (2 * n) + [SEM_SPEC, SEM_SPEC, ANY],
        out_specs=[HBM_SPEC] * (2 * n), out_shape=[pltpu.HBM(s.shape, s.dtype) for s in arrs],
        input_output_aliases={i: i for i in range(2 * n)},
        compiler_params=pltpu.CompilerParams(has_side_effects=DATAFLOW),
    )(*arrs, *started["sems"], after)
    return out[:n], out[n:]


def _join_halves(rs):
    n = len(rs)

    def body(*refs):
        outs = refs[n:2 * n]
        send_s, recv_s = refs[2 * n:]
        x, y, c, _ = _place()
        sibling = (x, y, 1 - c)

        def half(a, of_c):
            hr = outs[a].shape[1] // 2
            return outs[a].at[:, pl.ds(pl.multiple_of(of_c * hr, 8), hr)]

        copies = [_remote(half(a, c), half(a, c), send_s.at[a], recv_s.at[a], sibling) for a in range(n)]
        for cp in copies:
            cp.start()
        for a in range(n):
            landed = half(a, 1 - c)
            _remote(landed, landed, send_s.at[a], recv_s.at[a], sibling).wait_recv()
        for cp in copies:
            cp.wait_send()

    out_shape = [jax.ShapeDtypeStruct(r.shape, r.dtype) for r in rs]
    return _comm_call("join_halves", body, rs, out_shape, [n, n], aliases={a: a for a in range(n)})


NDEV = 8


def _allreduce_small(buf):
    r = buf.shape[0]

    def body(in_ref, out_ref, gath, send_s, recv_s):
        x, y, c, _ = _place()
        me = 4 * x + 2 * y + c
        copies = []
        for rel in range(1, NDEV):
            px = 1 - x if rel & 4 else x
            py = 1 - y if rel & 2 else y
            pc = 1 - c if rel & 1 else c
            copies.append((_remote(in_ref, gath.at[me], send_s.at[rel - 1], recv_s.at[rel - 1], (px, py, pc)),
                           4 * px + 2 * py + pc))
        for cp, _ in copies:
            cp.start()
        gath[me] = in_ref[...]
        for rel, (cp, peer) in enumerate(copies):
            landed = gath.at[peer]
            _remote(landed, landed, send_s.at[rel], recv_s.at[rel], (x, y, c)).wait_recv()
        for cp, _ in copies:
            cp.wait_send()
        total = gath[0]
        for d in range(1, NDEV):
            total = total + gath[d]
        out_ref[...] = total

    vm = pl.BlockSpec(memory_space=pltpu.VMEM)
    return pl.pallas_call(
        body, name="allreduce_small", in_specs=[vm], out_specs=vm, out_shape=jax.ShapeDtypeStruct((r, LANE), F32),
        scratch_shapes=[pltpu.VMEM((NDEV, r, LANE), F32), pltpu.SemaphoreType.DMA((NDEV - 1,)),
                        pltpu.SemaphoreType.DMA((NDEV - 1,))],
        compiler_params=pltpu.CompilerParams(has_side_effects=True, vmem_limit_bytes=VMEM_LIMIT),
    )(buf)


MAX_ROW_TILE = 512
BF16_ROWS = 16


def _row_tile(rows):
    for t in range(min(rows, MAX_ROW_TILE) // BF16_ROWS * BF16_ROWS, 0, -BF16_ROWS):
        if rows % t == 0:
            return t
    raise ValueError(rows)


def _sum_halves(g, theirs, c_arr):
    nch, rows, cols = g.shape
    hr = rows // 2
    tr = _row_tile(hr)

    def body(c_ref, g_ref, t_ref, o_ref, ob_ref):
        s = g_ref[...] + t_ref[...]
        o_ref[...] = s
        ob_ref[...] = s.astype(BF16)

    blk = pl.BlockSpec((None, tr, cols), lambda j, i, c_ref: (j, i, 0))
    return pl.pallas_call(
        body, name="sum_halves",
        grid_spec=pltpu.PrefetchScalarGridSpec(
            num_scalar_prefetch=1, grid=(nch, hr // tr),
            in_specs=[pl.BlockSpec((None, None, tr, cols), lambda j, i, c_ref: (j, c_ref[0], i, 0)), blk],
            out_specs=[blk, blk]),
        out_shape=[jax.ShapeDtypeStruct((nch, hr, cols), F32), jax.ShapeDtypeStruct((nch, hr, cols), BF16)],
        compiler_params=_cparams(("parallel", "parallel")),
    )(c_arr, g.reshape(nch, 2, hr, cols), theirs)


def _sum_halves_w_in(g, theirs, c_arr):
    hr = D // 2
    sh = IN_DIM // NCHIP

    def body(c_ref, g_ref, t_ref, o_ref, ob_ref):
        s = g_ref[...] + t_ref[...]
        for j, dst, src, width in W_IN_PIECES:
            o_ref[j, :, dst:dst + width] = s[:, src:src + width]
            ob_ref[j, :, dst:dst + width] = s[:, src:src + width].astype(BF16)

    out = pl.BlockSpec((NCHIP, WT, sh), lambda i, c_ref: (0, i, 0))
    return pl.pallas_call(
        body, name="sum_halves_w_in",
        grid_spec=pltpu.PrefetchScalarGridSpec(
            num_scalar_prefetch=1, grid=(hr // WT,),
            in_specs=[pl.BlockSpec((None, WT, NP), lambda i, c_ref: (c_ref[0], i, 0)),
                      pl.BlockSpec((None, WT, NP), lambda i, c_ref: (0, i, 0))],
            out_specs=[out, out]),
        out_shape=[jax.ShapeDtypeStruct((NCHIP, hr, sh), F32), jax.ShapeDtypeStruct((NCHIP, hr, sh), BF16)],
        compiler_params=_cparams(("parallel",)),
    )(c_arr, g.reshape(2, hr, NP), theirs)


def _sum_chips(p, q, place, l, into=None):
    _, rows, cols = p.shape
    tr = _row_tile(rows)
    steps = rows // tr

    def body(place_ref, p_ref, q0, q1, q2, *rest):
        rest[-1][...] = ((p_ref[...] + q0[...].astype(F32)) + q1[...].astype(F32)) + q2[...].astype(F32)

    qs = lambda k: pl.BlockSpec((None, tr, cols), lambda i, place_ref: (k, i, 0))
    return pl.pallas_call(
        body, name="sum_chips",
        grid_spec=pltpu.PrefetchScalarGridSpec(
            num_scalar_prefetch=1, grid=(steps,),
            in_specs=[pl.BlockSpec((None, tr, cols), lambda i, place_ref: (place_ref[0], i, 0)), qs(0), qs(1), qs(2)]
            + ([ANY] if into is not None else []),
            out_specs=pl.BlockSpec((None, tr, cols), lambda i, place_ref: (l, place_ref[1] * steps + i, 0))),
        out_shape=jax.ShapeDtypeStruct((DEPTH, 2 * rows, cols), F32),
        input_output_aliases={5: 0} if into is not None else {},
        compiler_params=_cparams(("parallel",)),
    )(place, p, q, q, q, *([into] if into is not None else []))


def _adamw(w, g, m, v):
    layers, rows, cols = w.shape
    tr = _row_tile(rows)

    def body(w_ref, g_ref, m_ref, v_ref, d_ref, nm_ref, nv_ref):
        gv = g_ref[...]
        nm = ADAM_B1 * m_ref[...] + (1.0 - ADAM_B1) * gv
        nv = ADAM_B2 * v_ref[...] + (1.0 - ADAM_B2) * jnp.square(gv)
        m_hat = nm / (1.0 - ADAM_B1 ** ADAM_STEP)
        v_hat = nv / (1.0 - ADAM_B2 ** ADAM_STEP)
        d_ref[...] = -ADAM_LR * (m_hat / (jnp.sqrt(v_hat) + ADAM_EPS) + ADAM_WD * w_ref[...])
        nm_ref[...] = nm
        nv_ref[...] = nv

    blk = pl.BlockSpec((None, tr, cols), lambda l, i: (l, i, 0))
    return pl.pallas_call(
        body, name="adamw", grid=(layers, rows // tr), in_specs=[blk] * 4, out_specs=[blk] * 3,
        out_shape=[jax.ShapeDtypeStruct(w.shape, F32)] * 3, compiler_params=_cparams(("parallel", "parallel")),
    )(w, g, m, v)


BIG = ("w_in", "w_out", "w_gate", "w_up", "w_down")
SMALL = ("norm_mix", "a_log", "dt_bias", "o_norm_g", "ln_v_g", "ln_v_b", "w_s", "b_s", "norm_ffn", "norm_final")
ORDER = ("norm_mix", "w_in", "conv_w", "a_log", "dt_bias", "o_norm_g", "ln_v_g", "ln_v_b", "w_s", "b_s", "w_out",
         "norm_ffn", "w_gate", "w_up", "w_down", "norm_final")


F32_ROWS = 8
PACK_ROWS = 128


def _lane_rows(size):
    return -(-size // (F32_ROWS * LANE)) * F32_ROWS


def _pack(arrs):
    parts = [jnp.pad(a.reshape(-1), (0, _lane_rows(a.size) * LANE - a.size)).reshape(-1, LANE) for a in arrs]
    rows = sum(p.shape[0] for p in parts)
    if rows % PACK_ROWS:
        parts.append(jnp.zeros((-rows % PACK_ROWS, LANE), F32))
    return jnp.concatenate(parts, axis=0)


def _unpack(buf, like):
    out, row = [], 0
    for a in like:
        n = _lane_rows(a.size)
        out.append(buf[row:row + n].reshape(-1)[:a.size].reshape(a.shape))
        row += n
    return out


def kernel(x, norm_mix, w_in, conv_w, a_log, dt_bias, o_norm_g, ln_v_g, ln_v_b, w_s, b_s, w_out, norm_ffn, w_gate, w_up, w_down, norm_final, loss_target, m_norm_mix, m_w_in, m_conv_w, m_a_log, m_dt_bias, m_o_norm_g, m_ln_v_g, m_ln_v_b, m_w_s, m_b_s, m_w_out, m_norm_ffn, m_w_gate, m_w_up, m_w_down, m_norm_final, v_norm_mix, v_w_in, v_conv_w, v_a_log, v_dt_bias, v_o_norm_g, v_ln_v_g, v_ln_v_b, v_w_s, v_b_s, v_w_out, v_norm_ffn, v_w_gate, v_w_up, v_w_down, v_norm_final):
    w = dict(norm_mix=norm_mix, w_in=w_in, conv_w=conv_w, a_log=a_log, dt_bias=dt_bias, o_norm_g=o_norm_g,
             ln_v_g=ln_v_g, ln_v_b=ln_v_b, w_s=w_s, b_s=b_s, w_out=w_out, norm_ffn=norm_ffn, w_gate=w_gate, w_up=w_up,
             w_down=w_down, norm_final=norm_final)
    m = dict(norm_mix=m_norm_mix, w_in=m_w_in, conv_w=m_conv_w, a_log=m_a_log, dt_bias=m_dt_bias, o_norm_g=m_o_norm_g,
             ln_v_g=m_ln_v_g, ln_v_b=m_ln_v_b, w_s=m_w_s, b_s=m_b_s, w_out=m_w_out, norm_ffn=m_norm_ffn,
             w_gate=m_w_gate, w_up=m_w_up, w_down=m_w_down, norm_final=m_norm_final)
    v = dict(norm_mix=v_norm_mix, w_in=v_w_in, conv_w=v_conv_w, a_log=v_a_log, dt_bias=v_dt_bias, o_norm_g=v_o_norm_g,
             ln_v_g=v_ln_v_g, ln_v_b=v_ln_v_b, w_s=v_w_s, b_s=v_b_s, w_out=v_w_out, norm_ffn=v_norm_ffn,
             w_gate=v_w_gate, w_up=v_w_up, w_down=v_w_down, norm_final=v_norm_final)
    chip = 2 * lax.axis_index("x") + lax.axis_index("y")
    place = jnp.stack([chip, lax.axis_index("c")]).astype(jnp.int32)
    c_arr = place[1:]

    def kernel_view(n, a):
        return jnp.swapaxes(a, 1, 2) if n in ("w_gate", "w_up") else a

    own = {n: [kernel_view(n, w[n])[l].astype(BF16) for l in range(DEPTH)] for n in BIG}
    by_chip = lambda a: jax.ShapeDtypeStruct((NCHIP,) + a.shape, a.dtype)

    def start(name, srcs, whole, after=()):
        return _split_start(name, _gather_plan(whole), srcs, [by_chip(a) for a in srcs], 3 * len(srcs), after)

    def finish(name, started, whole, after):
        srcs, lands = _split_wait(name, _gather_plan(whole), started, after)
        passed = iter(_forward_halves([g for g, all_of_it in zip(lands, whole) if not all_of_it]))
        lands = [g if all_of_it else next(passed) for g, all_of_it in zip(lands, whole)]
        return srcs, [lax.dynamic_update_index_in_dim(g, o, chip, 0) for g, o in zip(lands, srcs)]

    ffn = BIG[1:]
    first = start("gather_first_start", [own["w_in"][0], conv_w], [False, True])
    early = start("gather_early_start", [own[n][0] for n in ffn], [False] * len(ffn), [first["token"]])
    later = start("gather_later_start", [own[n][1] for n in BIG], [False] * len(BIG), [early["token"]])
    (own_w_in, _), (w_in_by_chip, conv_by_chip) = finish("gather_first_wait", first, [False, True], later["token"])

    def late(after):
        return dict(zip(ffn, finish("gather_early_wait", early, [False] * len(ffn), after)[1]))

    layer0 = _layer_params(0, dict(w_in=_assemble_w_in(w_in_by_chip, own_w_in, place), conv_w=conv_by_chip, late=late), w)

    def layer1(after):
        srcs, by = finish("gather_later_wait", later, [False] * len(BIG), after)
        big = dict(zip(ffn, by[1:]), w_in=_assemble_w_in(by[0], srcs[0], place), conv_w=conv_by_chip)
        return _layer_params(1, big, w)

    saved, layers, loss_lanes, dh, dhb, d_norm_final = _forward(x[0], loss_target[0], [layer0, layer1],
                                                                 norm_final[None])
    loss = lax.psum(loss_lanes[0, 0], ("x", "y", "c"))

    sums, arrived = {}, {}

    def exchange_start(tag, l, names, grads, after=()):
        mine = [grads[n] for n in names]
        shapes = [jax.ShapeDtypeStruct((g.shape[0], g.shape[1] // 2, g.shape[2]), F32) for g in mine]
        return tag, l, names, _split_start(f"exchange_{tag}_start", _exchange_plan, mine, shapes, len(mine), after)

    def add_halves(l, names, mine, theirs):
        for n, g, t in zip(names, mine, theirs):
            sums[l, n] = (_sum_halves_w_in if n == "w_in" else _sum_halves)(g, t, c_arr)

    def exchange_wait(handle, after):
        tag, l, names, started = handle
        add_halves(l, names, *_split_wait(f"exchange_{tag}_wait", _exchange_plan, started, after))

    def scatter_start(tag, l, names, after=()):
        partial = [sums[l, n][1] for n in names]
        shapes = [jax.ShapeDtypeStruct((3,) + p.shape[1:], p.dtype) for p in partial]
        return tag, l, names, _split_start(f"scatter_{tag}_start", _scatter_plan, partial, shapes, 3 * len(names), after)

    def scatter_wait(handle, after):
        tag, l, names, started = handle
        for n, q in zip(names, _split_wait(f"scatter_{tag}_wait", _scatter_plan, started, after)[1]):
            arrived[l, n] = q

    last = DEPTH - 1
    swiglu = BIG[2:]
    dh1, dh1b, g_ffn = _layer_bwd_ffn(dh, dhb, layers[last], saved[last])
    dh, dhb, g_mix = _layer_bwd_mixer(dh1, dh1b, layers[last], saved[last])
    gl = [None, _reference_layout({**g_ffn, **g_mix})]
    ex_last = exchange_start("last", last, BIG, gl[last])
    dh1, dh1b, g_ffn = _layer_bwd_ffn(dh, dhb, layers[0], saved[0], after=[ex_last[-1]["token"]])
    exchange_wait(ex_last, dh1)
    sc_last = scatter_start("last", last, BIG)
    ex_ffn = exchange_start("swiglu", 0, swiglu, g_ffn, [sc_last[-1]["token"]])
    sc_ffn = []

    def midway(do):
        exchange_wait(ex_ffn, do)
        sc_ffn.append(scatter_start("swiglu", 0, swiglu))
        return [sc_ffn[0][-1]["token"]]

    dx, _, g_mix = _layer_bwd_mixer(dh1, dh1b, layers[0], saved[0], after=[ex_ffn[-1]["token"]], midway=midway)
    scatter_wait(sc_last, dx)
    scatter_wait(sc_ffn[0], dx)
    gl[0] = _reference_layout({**g_ffn, **g_mix})
    mine = [gl[0][n] for n in BIG[:2]]
    add_halves(0, BIG[:2], mine, _exchange_halves(mine))
    for n, q in zip(BIG[:2], _scatter_partials([sums[0, n][1] for n in BIG[:2]])):
        arrived[0, n] = q
    reduced = []
    for n in BIG:
        buf = None
        for l in range(DEPTH):
            buf = _sum_chips(sums[l, n][0], arrived[l, n], place, l, into=buf)
        reduced.append(buf)
    g_out = dict(zip(BIG, _join_halves(reduced)))

    small_g = [jnp.stack([gl[l][n] for l in range(DEPTH)]) for n in SMALL[:-1]] + [d_norm_final[0]]
    conv_g = jnp.stack([gl[l]["conv_w"] for l in range(DEPTH)])
    total = _allreduce_small(_pack(small_g + [conv_g]))
    *small_r, conv_r = _unpack(total, small_g + [conv_g])
    g_out.update(zip(SMALL, small_r))
    g_out["conv_w"] = lax.dynamic_slice_in_dim(conv_r, chip * conv_w.shape[2], conv_w.shape[2], axis=2)

    delta, new_m, new_v = {}, {}, {}
    for n in BIG:
        res = _adamw(kernel_view(n, w[n]), g_out[n], kernel_view(n, m[n]), kernel_view(n, v[n]))
        g_out[n], delta[n], new_m[n], new_v[n] = (kernel_view(n, a) for a in (g_out[n],) + tuple(res))
    rest = SMALL + ("conv_w",)
    like = [w[n] for n in rest]
    d, nm, nv = _adamw(*[_pack([src[n] for n in rest])[None] for src in (w, g_out, m, v)])
    for dst, buf in ((delta, d), (new_m, nm), (new_v, nv)):
        dst.update(zip(rest, _unpack(buf[0], like)))

    return (loss, dx[None], *[g_out[n] for n in ORDER], *[delta[n] for n in ORDER], *[new_m[n] for n in ORDER],
            *[new_v[n] for n in ORDER])
```

```python
import functools

import jax
import jax.numpy as jnp
from jax import lax
from jax.experimental import pallas as pl
from jax.experimental.pallas import tpu as pltpu

F32 = jnp.float32
BF16 = jnp.bfloat16
MESH = pl.DeviceIdType.MESH
ANY = pl.BlockSpec(memory_space=pl.ANY)
HIGHEST = lax.Precision.HIGHEST

T = 2048
D = 1024
DEPTH = 2
NCHIP = 4
HEADS = 4
HD = 128
HW = HEADS * HD
CH = 64
GCH = 128
IN_DIM = 3080
NP = 3200
BA_OFF = 3072
FF_SH = 704
EPS = 1e-6
LANE = 128
VMEM_LIMIT = 56 * 1024 * 1024

ADAM_LR = 0.001
ADAM_B1 = 0.9
ADAM_B2 = 0.999
ADAM_EPS = 1e-08
ADAM_WD = 0.01
ADAM_STEP = 10


def _cparams(sem=None):
    return pltpu.CompilerParams(dimension_semantics=sem, vmem_limit_bytes=VMEM_LIMIT)


_DIMS = {"nn": (((1,), (0,)), ((), ())), "nt": (((1,), (1,)), ((), ())), "tn": (((0,), (0,)), ((), ()))}


def _mm(name, mode, a, bs, *, tm, tn, tk, out_dtypes=(F32,), reduce_g=False, resid=None, extras=(), epilogue=None,
        b_spec=None, n_n=None, after=()):
    nb = len(bs)
    ga = a.shape[0]
    gbs = [1 if b_spec is not None else b.shape[0] for b in bs]
    g_n = max([ga] + gbs)
    if mode == "tn":
        k_n, m_n = a.shape[1:]
    else:
        m_n, k_n = a.shape[1:]
    if n_n is None:
        n_n = bs[0].shape[1] if mode == "nt" else bs[0].shape[2]
    assert m_n % tm == 0 and n_n % tn == 0 and k_n % tk == 0, (name, m_n, n_n, k_n)
    mi, nj, kk = m_n // tm, n_n // tn, k_n // tk
    if reduce_g:
        grid = (mi, nj, g_n, kk)
        ids = lambda i, j, g, k: (g, i, j, k)
        n_red = g_n * kk
        red_idx = lambda: pl.program_id(2) * kk + pl.program_id(3)
        sem = ("parallel", "parallel", "arbitrary", "arbitrary")
    else:
        grid = (g_n, mi, nj, kk)
        ids = lambda g, i, j, k: (g, i, j, k)
        n_red = kk
        red_idx = lambda: pl.program_id(3)
        sem = ("parallel", "parallel", "parallel", "arbitrary")

    def pick(gsz, g):
        return g if gsz > 1 else 0

    def a_map(*p):
        g, i, j, k = ids(*p)
        return (pick(ga, g), k, i) if mode == "tn" else (pick(ga, g), i, k)

    def b_map(gsz):
        def f(*p):
            g, i, j, k = ids(*p)
            if b_spec is not None:
                return b_spec[1](g, i, j, k)
            return (pick(gsz, g), j, k) if mode == "nt" else (pick(gsz, g), k, j)
        return f

    def o_map(gsz):
        def f(*p):
            g, i, j, k = ids(*p)
            return (0 if reduce_g else pick(gsz, g), i, j)
        return f

    a_spec = pl.BlockSpec((None, tk, tm) if mode == "tn" else (None, tm, tk), a_map)
    b_block = b_spec[0] if b_spec is not None else ((None, tn, tk) if mode == "nt" else (None, tk, tn))
    b_specs = [pl.BlockSpec(b_block, b_map(gs)) for gs in gbs]
    x_specs = [pl.BlockSpec((None, tm, tn), o_map(e.shape[0])) for e in extras]
    r_specs = [pl.BlockSpec((None, tm, tn), o_map(resid.shape[0]))] if resid is not None else []
    g_out = 1 if reduce_g else g_n
    out_shape = [jax.ShapeDtypeStruct((g_out, m_n, n_n), dt) for dt in out_dtypes]
    out_specs = [pl.BlockSpec((None, tm, tn), o_map(g_out)) for _ in out_dtypes]
    nx, nr, no = len(extras), len(r_specs), len(out_dtypes)
    n_in = 1 + nb + nx + nr + len(after)
    dims = _DIMS[mode]

    def body(*refs):
        a_ref = refs[0]
        b_refs = refs[1:1 + nb]
        x_refs = refs[1 + nb:1 + nb + nx]
        r_refs = refs[1 + nb + nx:1 + nb + nx + nr]
        o_refs = refs[n_in:n_in + no]
        acc_refs = refs[n_in + no:]
        av = a_ref[...]
        products = [lax.dot_general(av, b_ref[...], dims, preferred_element_type=F32) for b_ref in b_refs]

        def finish(accs):
            if r_refs:
                accs[0] = accs[0] + r_refs[0][...]
            outs = epilogue(accs, [x[...] for x in x_refs]) if epilogue is not None else accs
            for o_ref, o in zip(o_refs, outs):
                o_ref[...] = o.astype(o_ref.dtype)

        if n_red == 1:
            finish(products)
            return
        r = red_idx()
        for p, acc in zip(products, acc_refs):
            @pl.when(r == 0)
            def _():
                acc[...] = p

            @pl.when((r > 0) & (r < n_red - 1))
            def _():
                acc[...] += p

        @pl.when(r == n_red - 1)
        def _():
            finish([acc[...] + p for p, acc in zip(products, acc_refs)])

    return pl.pallas_call(
        body, name=name, grid=grid,
        in_specs=[a_spec] + b_specs + x_specs + r_specs + [ANY] * len(after),
        out_specs=out_specs, out_shape=out_shape,
        scratch_shapes=[pltpu.VMEM((tm, tn), F32) for _ in range(nb if n_red > 1 else 0)],
        compiler_params=_cparams(sem),
    )(a, *bs, *extras, *([resid] if resid is not None else []), *after)


def _sigmoid(x):
    return 1.0 / (1.0 + jnp.exp(-x))


def _silu(x):
    return x * _sigmoid(x)


def _gelu(x):
    return 0.5 * x * (1.0 + jnp.tanh(0.7978845608028654 * (x + 0.044715 * (x * x * x))))


def _rms_fn(h, gain):
    return h * lax.rsqrt(jnp.mean(h * h, axis=-1, keepdims=True) + EPS) * gain


def _shift_impl(x, s):
    n = x.shape[0]
    rolled = pltpu.roll(x, s % n, 0)
    row = lax.broadcasted_iota(jnp.int32, x.shape, 0)
    return jnp.where((row >= s) & (row < n + s), rolled, 0.0)


@functools.partial(jax.custom_vjp, nondiff_argnums=(1,))
def _shift(x, s):
    return _shift_impl(x, s)


def _shift_fwd(x, s):
    return _shift_impl(x, s), None


def _shift_bwd(s, _, g):
    return (_shift_impl(g, -s),)


_shift.defvjp(_shift_fwd, _shift_bwd)


def _prep_fn(x, w, qk_scale, is_v):
    y = x * w[3:4, :]
    for i in range(3):
        y = y + _shift(x, 3 - i) * w[i:i + 1, :]
    y = _silu(y)
    nrm = lax.rsqrt(jnp.sum(y * y, axis=-1, keepdims=True) + EPS) * qk_scale
    return y * jnp.where(is_v, 1.0, nrm)


def _softplus(x):
    return jnp.maximum(x, 0.0) + jnp.log(1.0 + jnp.exp(-jnp.abs(x)))


def _gates_fn(ba, a_log, dt_bias):
    lane = lax.broadcasted_iota(jnp.int32, ba.shape, 1)
    beta = _sigmoid(ba)
    g = -jnp.exp(a_log) * _softplus(ba + dt_bias)
    return jnp.where(lane < HEADS, beta, g)


def _dot16(a, b, dims=_DIMS["nn"]):
    return lax.dot_general(a.astype(BF16), b.astype(BF16), dims, preferred_element_type=F32)


def _dot32(a, b):
    return jnp.dot(a, b, preferred_element_type=F32, precision=HIGHEST)


def _dot3(a, b, dims=_DIMS["nn"]):
    return lax.dot_general(a, b, dims, preferred_element_type=F32, precision=lax.Precision.HIGH)


def _tri_inverses(mats):
    row = lax.broadcasted_iota(jnp.int32, (CH, CH), 0)
    col = lax.broadcasted_iota(jnp.int32, (CH, CH), 1)
    eye = (row == col).astype(F32)
    ts = [eye - a for a in mats]
    ps = list(mats)
    for _ in range(5):
        ps = [_dot3(p, p) for p in ps]
        ts = [t + _dot3(t, p) for t, p in zip(ts, ps)]
    return ts


@jax.custom_vjp
def _tri_solves(mats, rhs):
    return [_dot3(t, b) for t, b in zip(_tri_inverses(mats), rhs)]


def _tri_solves_fwd(mats, rhs):
    ts = _tri_inverses(mats)
    xs = [_dot3(t, b) for t, b in zip(ts, rhs)]
    return xs, (ts, xs)


def _tri_solves_bwd(res, dxs):
    ts, xs = res
    dbs = [_dot3(t, dx, _DIMS["tn"]) for t, dx in zip(ts, dxs)]
    return [-_dot3(db, x, _DIMS["nt"]) for db, x in zip(dbs, xs)], dbs


_tri_solves.defvjp(_tri_solves_fwd, _tri_solves_bwd)


def _chunk_prep_fn(xs, bgs):
    row = lax.broadcasted_iota(jnp.int32, (CH, CH), 0)
    col = lax.broadcasted_iota(jnp.int32, (CH, CH), 1)
    incl = row >= col
    strict = row > col
    lmat = incl.astype(F32)
    n = len(xs)
    items = [(i, h) for i in range(n) for h in range(HEADS)]
    part = lambda i, h, c: xs[i][:, c * HW + h * HD:c * HW + (h + 1) * HD]
    q = [part(i, h, 0) for i, h in items]
    k = [part(i, h, 1) for i, h in items]
    v = [part(i, h, 2) for i, h in items]
    beta = [bgs[i][:, h:h + 1] for i, h in items]
    gc_all = [_dot32(lmat, bg) for bg in bgs]
    gc = [gc_all[i][:, HEADS + h:HEADS + h + 1] for i, h in items]
    gmat = [jnp.where(strict, jnp.broadcast_to(bgs[i][:, HEADS + h:HEADS + h + 1], (CH, CH)), 0.0) for i, h in items]
    diff = [_dot3(lmat, m) for m in gmat]
    decay = [jnp.where(incl, jnp.exp(jnp.where(incl, d, 0.0)), 0.0) for d in diff]
    k_beta = [kk * b for kk, b in zip(k, beta)]
    kk_t = [_dot16(kb, kk, _DIMS["nt"]) for kb, kk in zip(k_beta, k)]
    qk_t = [_dot16(qq, kk, _DIMS["nt"]) for qq, kk in zip(q, k)]
    a = [jnp.where(strict, m * d, 0.0) for m, d in zip(kk_t, decay)]
    eg = [jnp.exp(g) for g in gc]
    rhs = [jnp.concatenate([vv * b, kb * e], axis=-1) for vv, b, kb, e in zip(v, beta, k_beta, eg)]
    uw = _tri_solves(a, rhs)
    qk = [m * d for m, d in zip(qk_t, decay)]
    g_last = [g[CH - 1:CH, :] for g in gc]
    qe = [qq * e for qq, e in zip(q, eg)]
    kd = [kk * jnp.exp(gl - g) for kk, gl, g in zip(k, g_last, gc)]
    egl = [jnp.broadcast_to(jnp.exp(gl), (1, HD)) for gl in g_last]
    out = []
    for i in range(n):
        mine = slice(i * HEADS, (i + 1) * HEADS)
        cat = lambda vals: jnp.concatenate(vals[mine], axis=-1)
        out.append((cat([x[:, :HD] for x in uw]), cat([x[:, HD:] for x in uw]), cat(qe), cat(kd),
                    jnp.concatenate([m[None] for m in qk[mine]], axis=0), cat(egl)))
    return out


def _chunk_state_fn(u, w, qe, kd, qk, egl, s):
    ws = [_dot16(a, b) for a, b in zip(w, s)]
    qs = [_dot16(a, b) for a, b in zip(qe, s)]
    v_new = [a - b for a, b in zip(u, ws)]
    o = [a + _dot16(b, c) for a, b, c in zip(qs, qk, v_new)]
    s_new = [a * e + _dot16(b, c, _DIMS["tn"]) for a, e, b, c in zip(s, egl, kd, v_new)]
    return o, s_new


def _mix_fn(o, z, ur, vr, ong, lng, lnb, ws, bst):
    row = lax.broadcasted_iota(jnp.int32, (GCH, GCH), 0)
    col = lax.broadcasted_iota(jnp.int32, (GCH, GCH), 1)
    causal = row >= col
    ug = _gelu(ur)
    vg = _gelu(vr)
    sls = [slice(h * HD, (h + 1) * HD) for h in range(HEADS)]
    oh = [o[:, sl] for sl in sls]
    oh = [x * lax.rsqrt(jnp.mean(x * x, axis=-1, keepdims=True) + EPS) for x in oh]
    outs_dn = [x * ong * _silu(z[:, sl]) for x, sl in zip(oh, sls)]
    vh = [vg[:, sl] for sl in sls]
    mu = [jnp.mean(x, axis=-1, keepdims=True) for x in vh]
    var = [jnp.mean(jnp.square(x - m), axis=-1, keepdims=True) for x, m in zip(vh, mu)]
    vn = [(x - m) * lax.rsqrt(s + EPS) * lng[:, sl] + lnb[:, sl] for x, m, s, sl in zip(vh, mu, var, sls)]
    mixed = [_dot16(jnp.where(causal, ws[h], 0.0), vn[h]) for h in range(HEADS)]
    outs_gm = [ug[:, sl] * (mixed[h] + bst[:, h:h + 1]) for h, sl in enumerate(sls)]
    return jnp.concatenate(outs_dn + outs_gm, axis=-1)


def _loss_fn(h, gain, tgt):
    y = _rms_fn(h, gain)
    return 0.5 * jnp.sum(jnp.mean(jnp.square(y - tgt), axis=-1))


RT = 256


def _rows(n=D):
    return pl.BlockSpec((RT, n), lambda i: (i, 0))


def _whole(shape):
    nd = len(shape)
    return pl.BlockSpec(shape, lambda i: (0,) * nd)


def _rmsnorm(name, h, gain):
    def body(h_ref, g_ref, o_ref):
        o_ref[...] = _rms_fn(h_ref[...], g_ref[...]).astype(BF16)

    return pl.pallas_call(
        body, name=name, grid=(T // RT,), in_specs=[_rows(), _whole((1, D))], out_specs=_rows(),
        out_shape=jax.ShapeDtypeStruct((T, D), BF16), compiler_params=_cparams(("parallel",)),
    )(h, gain)


def _rmsnorm_bwd(name, dhn, h, gain, resid):
    def body(dhn_ref, h_ref, g_ref, r_ref, dh_ref, dh16_ref, dg_ref):
        _, vjp = jax.vjp(_rms_fn, h_ref[...], g_ref[...])
        dh, dg = vjp(dhn_ref[...])
        dh = r_ref[...] + dh
        dh_ref[...] = dh
        dh16_ref[...] = dh.astype(BF16)

        @pl.when(pl.program_id(0) == 0)
        def _():
            dg_ref[...] = dg

        @pl.when(pl.program_id(0) > 0)
        def _():
            dg_ref[...] += dg

    return pl.pallas_call(
        body, name=name, grid=(T // RT,), in_specs=[_rows(), _rows(), _whole((1, D)), _rows()],
        out_specs=[_rows(), _rows(), _whole((1, D))],
        out_shape=[jax.ShapeDtypeStruct((T, D), F32), jax.ShapeDtypeStruct((T, D), BF16),
                   jax.ShapeDtypeStruct((1, D), F32)],
        compiler_params=_cparams(("arbitrary",)),
    )(dhn, h, gain, resid)


def _loss_head(h, gain, tgt):
    def body(h_ref, g_ref, t_ref, l_ref, dh_ref, dh16_ref, dg_ref):
        loss, vjp = jax.vjp(lambda hh, gg: _loss_fn(hh, gg, t_ref[...]), h_ref[...], g_ref[...])
        dh, dg = vjp(jnp.ones((), F32))
        dh_ref[...] = dh
        dh16_ref[...] = dh.astype(BF16)
        lv = jnp.full((1, LANE), loss, F32)

        @pl.when(pl.program_id(0) == 0)
        def _():
            dg_ref[...] = dg
            l_ref[...] = lv

        @pl.when(pl.program_id(0) > 0)
        def _():
            dg_ref[...] += dg
            l_ref[...] += lv

    return pl.pallas_call(
        body, name="loss_head", grid=(T // RT,), in_specs=[_rows(), _whole((1, D)), _rows()],
        out_specs=[_whole((1, LANE)), _rows(), _rows(), _whole((1, D))],
        out_shape=[jax.ShapeDtypeStruct((1, LANE), F32), jax.ShapeDtypeStruct((T, D), F32),
                   jax.ShapeDtypeStruct((T, D), BF16), jax.ShapeDtypeStruct((1, D), F32)],
        compiler_params=_cparams(("arbitrary",)),
    )(h, gain, tgt)


def _prep_flags():
    j = pl.program_id(0)
    qk_scale = jnp.where(j < HEADS, HD ** -0.5, 1.0).astype(F32)
    return qk_scale, j >= 2 * HEADS


def _prep(proj, conv_w):
    def body(x_ref, w_ref, o_ref):
        qk_scale, is_v = _prep_flags()
        o_ref[...] = _prep_fn(x_ref[...], w_ref[...], qk_scale, is_v)

    col = lambda j: (0, j)
    return pl.pallas_call(
        body, name="gdn_prep", grid=(3 * HEADS,),
        in_specs=[pl.BlockSpec((T, HD), col), pl.BlockSpec((4, HD), col)], out_specs=pl.BlockSpec((T, HD), col),
        out_shape=jax.ShapeDtypeStruct((T, 3 * HW), F32), compiler_params=_cparams(("parallel",)),
    )(proj, conv_w)


def _prep_bwd(proj, conv_w, dqkv, dproj):
    def body(x_ref, w_ref, d_ref, _, dx_ref, dw_ref):
        qk_scale, is_v = _prep_flags()
        _, vjp = jax.vjp(lambda x, w: _prep_fn(x, w, qk_scale, is_v), x_ref[...], w_ref[...])
        dx, dw = vjp(d_ref[...])
        dx_ref[...] = dx.astype(BF16)
        dw_ref[...] = dw

    col = lambda j: (0, j)
    return pl.pallas_call(
        body, name="gdn_prep_bwd", grid=(3 * HEADS,),
        in_specs=[pl.BlockSpec((T, HD), col), pl.BlockSpec((4, HD), col), pl.BlockSpec((T, HD), col), ANY],
        out_specs=[pl.BlockSpec((T, HD), col), pl.BlockSpec((4, HD), col)],
        out_shape=[jax.ShapeDtypeStruct((T, NP), BF16), jax.ShapeDtypeStruct((4, 3 * HW), F32)],
        input_output_aliases={3: 0}, compiler_params=_cparams(("parallel",)),
    )(proj, conv_w, dqkv, dproj)


BA_BLK = BA_OFF // LANE


def _gates(proj, a_log, dt_bias):
    def body(x_ref, a_ref, d_ref, o_ref):
        o_ref[...] = _gates_fn(x_ref[...], a_ref[...], d_ref[...])

    return pl.pallas_call(
        body, name="gdn_gates", grid=(1,),
        in_specs=[pl.BlockSpec((T, LANE), lambda i: (0, BA_BLK)), _whole((1, LANE)), _whole((1, LANE))],
        out_specs=_whole((T, LANE)),
        out_shape=jax.ShapeDtypeStruct((T, LANE), F32), compiler_params=_cparams(("arbitrary",)),
    )(proj, a_log, dt_bias)


def _gates_bwd(proj, a_log, dt_bias, dbg, dproj):
    def body(x_ref, a_ref, d_ref, dbg_ref, _, dx_ref, da_ref, dd_ref):
        _, vjp = jax.vjp(_gates_fn, x_ref[...], a_ref[...], d_ref[...])
        dx, da_ref[...], dd_ref[...] = vjp(dbg_ref[...])
        dx_ref[...] = dx.astype(BF16)

    ba = pl.BlockSpec((T, LANE), lambda i: (0, BA_BLK))
    return pl.pallas_call(
        body, name="gdn_gates_bwd", grid=(1,),
        in_specs=[ba, _whole((1, LANE)), _whole((1, LANE)), _whole((T, LANE)), ANY],
        out_specs=[ba, _whole((1, LANE)), _whole((1, LANE))],
        out_shape=[jax.ShapeDtypeStruct((T, NP), BF16), jax.ShapeDtypeStruct((1, LANE), F32),
                   jax.ShapeDtypeStruct((1, LANE), F32)],
        input_output_aliases={4: 0}, compiler_params=_cparams(("arbitrary",)),
    )(proj, a_log, dt_bias, dbg, dproj)


NCK = T // CH
CPS = 2


def _chunk_prep_specs(rev=False):
    at = (lambda n: NCK - 1 - n) if rev else (lambda n: n)
    wide = pl.BlockSpec((CH, HW), lambda n: (at(n), 0))
    return [wide, wide, wide, wide, pl.BlockSpec((HEADS, CH, CH), lambda n: (0, at(n), 0)),
            pl.BlockSpec((None, 1, HW), lambda n: (at(n), 0, 0))]


def _chunk_prep_shapes(dtypes):
    shp = [(T, HW), (T, HW), (T, HW), (T, HW), (HEADS, T, CH), (NCK, 1, HW)]
    return [jax.ShapeDtypeStruct(s, dt) for s, dt in zip(shp, dtypes)]


def _chunk_prep(qkv, bg):
    def body(x_ref, bg_ref, *o_refs):
        rows = [slice(ci * CH, (ci + 1) * CH) for ci in range(CPS)]
        res = _chunk_prep_fn([x_ref[r, :] for r in rows], [bg_ref[r, :] for r in rows])
        for ci, (u, w, qe, kd, qk, egl) in enumerate(res):
            for o_ref, val in zip(o_refs[:4], (u, w, qe, kd)):
                o_ref[rows[ci], :] = val.astype(o_ref.dtype)
            o_refs[4][:, rows[ci], :] = qk.astype(BF16)
            o_refs[5][ci] = egl

    wide = pl.BlockSpec((CPS * CH, HW), lambda n: (n, 0))
    return pl.pallas_call(
        body, name="gdn_chunk_prep", grid=(NCK // CPS,),
        in_specs=[pl.BlockSpec((CPS * CH, 3 * HW), lambda n: (n, 0)), pl.BlockSpec((CPS * CH, LANE), lambda n: (n, 0))],
        out_specs=[wide, wide, wide, wide, pl.BlockSpec((HEADS, CPS * CH, CH), lambda n: (0, n, 0)),
                   pl.BlockSpec((CPS, 1, HW), lambda n: (n, 0, 0))],
        out_shape=_chunk_prep_shapes((F32, BF16, BF16, BF16, BF16, F32)),
        compiler_params=_cparams(("parallel",)),
    )(qkv, bg)


def _chunk_prep_bwd(qkv, bg, cots):
    def body(x_ref, bg_ref, du, dw, dqe, dkd, dqk, degl, dx_ref, dbg_ref):
        rows = [slice(ci * CH, (ci + 1) * CH) for ci in range(CPS)]
        _, vjp = jax.vjp(_chunk_prep_fn, [x_ref[r, :] for r in rows], [bg_ref[r, :] for r in rows])
        dxs, dbgs = vjp([(du[r, :], dw[r, :], dqe[r, :], dkd[r, :], dqk[:, r, :], degl[ci])
                         for ci, r in enumerate(rows)])
        for r, dx, dbg in zip(rows, dxs, dbgs):
            dx_ref[r, :] = dx
            dbg_ref[r, :] = dbg

    wide = pl.BlockSpec((CPS * CH, HW), lambda n: (n, 0))
    return pl.pallas_call(
        body, name="gdn_chunk_prep_bwd", grid=(NCK // CPS,),
        in_specs=[pl.BlockSpec((CPS * CH, 3 * HW), lambda n: (n, 0)), pl.BlockSpec((CPS * CH, LANE), lambda n: (n, 0)),
                  wide, wide, wide, wide, pl.BlockSpec((HEADS, CPS * CH, CH), lambda n: (0, n, 0)),
                  pl.BlockSpec((CPS, 1, HW), lambda n: (n, 0, 0))],
        out_specs=[pl.BlockSpec((CPS * CH, 3 * HW), lambda n: (n, 0)), pl.BlockSpec((CPS * CH, LANE), lambda n: (n, 0))],
        out_shape=[jax.ShapeDtypeStruct((T, 3 * HW), F32), jax.ShapeDtypeStruct((T, LANE), F32)],
        compiler_params=_cparams(("parallel",)),
    )(qkv, bg, *cots)


def _head_args(refs):
    u, w, qe, kd, qk, egl = refs
    sls = [slice(h * HD, (h + 1) * HD) for h in range(HEADS)]
    return ([u[:, sl] for sl in sls], [w[:, sl].astype(F32) for sl in sls], [qe[:, sl].astype(F32) for sl in sls],
            [kd[:, sl].astype(F32) for sl in sls], [qk[h].astype(F32) for h in range(HEADS)],
            [egl[:, sl] for sl in sls])


def _chunk_scan(prep):
    def body(*refs):
        o_ref, sh_ref, s_ref = refs[6:]

        @pl.when(pl.program_id(0) == 0)
        def _():
            s_ref[...] = jnp.zeros_like(s_ref)

        s = [s_ref[h] for h in range(HEADS)]
        for h in range(HEADS):
            sh_ref[h, 0] = s[h]
        o, s_new = _chunk_state_fn(*_head_args(refs[:6]), s)
        for h in range(HEADS):
            o_ref[:, h * HD:(h + 1) * HD] = o[h]
            s_ref[h] = s_new[h]

    return pl.pallas_call(
        body, name="gdn_scan", grid=(NCK,), in_specs=_chunk_prep_specs(),
        out_specs=[pl.BlockSpec((CH, HW), lambda n: (n, 0)), pl.BlockSpec((HEADS, 1, HD, HD), lambda n: (0, n, 0, 0))],
        out_shape=[jax.ShapeDtypeStruct((T, HW), F32), jax.ShapeDtypeStruct((HEADS, NCK, HD, HD), F32)],
        scratch_shapes=[pltpu.VMEM((HEADS, HD, HD), F32)], compiler_params=_cparams(("arbitrary",)),
    )(*prep)


def _chunk_scan_bwd(prep, s_hist, do, after=()):
    n_in = 8 + len(after)

    def body(*refs):
        sh_ref, do_ref = refs[6:8]
        d_refs = refs[n_in:n_in + 6]
        ds_ref = refs[n_in + 6]

        @pl.when(pl.program_id(0) == 0)
        def _():
            ds_ref[...] = jnp.zeros_like(ds_ref)

        sls = [slice(h * HD, (h + 1) * HD) for h in range(HEADS)]
        _, vjp = jax.vjp(_chunk_state_fn, *_head_args(refs[:6]), [sh_ref[h, 0] for h in range(HEADS)])
        du, dw, dqe, dkd, dqk, degl, ds = vjp(([do_ref[:, sl] for sl in sls], [ds_ref[h] for h in range(HEADS)]))
        for h, sl in enumerate(sls):
            for d_ref, val in zip(d_refs[:4], (du, dw, dqe, dkd)):
                d_ref[:, sl] = val[h]
            d_refs[4][h] = dqk[h]
            d_refs[5][:, sl] = degl[h]
            ds_ref[h] = ds[h]

    rev = lambda n: NCK - 1 - n
    return pl.pallas_call(
        body, name="gdn_scan_bwd", grid=(NCK,),
        in_specs=_chunk_prep_specs(rev=True) + [pl.BlockSpec((HEADS, 1, HD, HD), lambda n: (0, rev(n), 0, 0)),
                                                pl.BlockSpec((CH, HW), lambda n: (rev(n), 0))] + [ANY] * len(after),
        out_specs=_chunk_prep_specs(rev=True), out_shape=_chunk_prep_shapes((F32,) * 6),
        scratch_shapes=[pltpu.VMEM((HEADS, HD, HD), F32)], compiler_params=_cparams(("arbitrary",)),
    )(*prep, s_hist, do, *after)


def _mix_specs():
    pc = lambda c: pl.BlockSpec((GCH, HW), lambda i: (i, c))
    return [pl.BlockSpec((GCH, HW), lambda i: (i, 0)), pc(3), pc(4), pc(5), _whole((1, HD)), _whole((1, HW)),
            _whole((1, HW)), _whole((HEADS, GCH, GCH)), _whole((GCH, LANE))]


def _mix(o, proj, ong, lng, lnb, ws, bst):
    def body(o_ref, z_ref, u_ref, v_ref, ong_ref, lng_ref, lnb_ref, ws_ref, bs_ref, m_ref):
        m_ref[...] = _mix_fn(o_ref[...], z_ref[...], u_ref[...], v_ref[...], ong_ref[...], lng_ref[...],
                             lnb_ref[...], ws_ref[...], bs_ref[...]).astype(BF16)

    return pl.pallas_call(
        body, name="mix", grid=(T // GCH,), in_specs=_mix_specs(),
        out_specs=pl.BlockSpec((GCH, D), lambda i: (i, 0)), out_shape=jax.ShapeDtypeStruct((T, D), BF16),
        compiler_params=_cparams(("parallel",)),
    )(o, proj, proj, proj, ong, lng, lnb, ws, bst)


def _mix_bwd(o, proj, ong, lng, lnb, ws, bst, dmix):
    def body(o_ref, z_ref, u_ref, v_ref, ong_ref, lng_ref, lnb_ref, ws_ref, bs_ref, dm_ref,
             do_ref, dzuv_ref, dong_ref, dlng_ref, dlnb_ref, dws_ref, dbs_ref):
        _, vjp = jax.vjp(_mix_fn, o_ref[...], z_ref[...], u_ref[...], v_ref[...], ong_ref[...], lng_ref[...],
                         lnb_ref[...], ws_ref[...], bs_ref[...])
        do, dz, du, dv, dong, dlng, dlnb, dws, dbs = vjp(dm_ref[...])
        do_ref[...] = do
        dzuv_ref[:, 0:HW] = dz.astype(BF16)
        dzuv_ref[:, HW:2 * HW] = du.astype(BF16)
        dzuv_ref[:, 2 * HW:3 * HW] = dv.astype(BF16)
        acc = [(dong_ref, dong), (dlng_ref, dlng), (dlnb_ref, dlnb), (dws_ref, dws), (dbs_ref, dbs)]

        @pl.when(pl.program_id(0) == 0)
        def _():
            for r, val in acc:
                r[...] = val

        @pl.when(pl.program_id(0) > 0)
        def _():
            for r, val in acc:
                r[...] += val

    shp = lambda *s: jax.ShapeDtypeStruct(s, F32)
    return pl.pallas_call(
        body, name="mix_bwd", grid=(T // GCH,),
        in_specs=_mix_specs() + [pl.BlockSpec((GCH, D), lambda i: (i, 0))],
        out_specs=[pl.BlockSpec((GCH, HW), lambda i: (i, 0)), pl.BlockSpec((GCH, 3 * HW), lambda i: (i, 1)),
                   _whole((1, HD)), _whole((1, HW)), _whole((1, HW)), _whole((HEADS, GCH, GCH)), _whole((GCH, LANE))],
        out_shape=[shp(T, HW), jax.ShapeDtypeStruct((T, NP), BF16), shp(1, HD), shp(1, HW), shp(1, HW),
                   shp(HEADS, GCH, GCH), shp(GCH, LANE)],
        compiler_params=_cparams(("arbitrary",)),
    )(o, proj, proj, proj, ong, lng, lnb, ws, bst, dmix)


def _swiglu_epilogue(accs, _):
    gate, up = accs
    return [gate, up, _silu(gate) * up]


def _swiglu_bwd_epilogue(accs, extras):
    dact = accs[0]
    gate, up = (e.astype(F32) for e in extras)
    sg = _sigmoid(gate)
    return [dact * up * (sg * (1.0 + gate * (1.0 - sg))), dact * (gate * sg)]


def _layer_fwd(h, p):
    hn = _rmsnorm("rms_mix", h, p["norm_mix"])
    proj = _mm("in_proj", "nn", hn[None], [p["w_in"][None]], tm=1024, tn=640, tk=D)[0][0]
    qkv = _prep(proj, p["conv_w"])
    bg = _gates(proj, p["a_log"], p["dt_bias"])
    prep = _chunk_prep(qkv, bg)
    o, s_hist = _chunk_scan(prep)
    if "late" in p:
        p.update(p.pop("late")(o))
    mix = _mix(o, proj, p["o_norm_g"], p["ln_v_g"], p["ln_v_b"], p["w_s"], p["bst"])
    h1 = _mm("out_proj", "nn", mix[None], [p["w_out"]], tm=1024, tn=512, tk=D // NCHIP, resid=h[None], n_n=D,
             b_spec=((None, D // NCHIP, 512), lambda g, i, j, k: (k, 0, j)))[0][0]
    h2n = _rmsnorm("rms_ffn", h1, p["norm_ffn"])
    gate, up, act = _mm("ffn_in", "nt", h2n[None], [p["w_gate"], p["w_up"]], tm=1024, tn=FF_SH, tk=D,
                        out_dtypes=(BF16, BF16, BF16), epilogue=_swiglu_epilogue)
    h2 = _mm("ffn_out", "nn", act, [p["w_down"]], tm=1024, tn=512, tk=FF_SH, reduce_g=True, resid=h1[None])[0][0]
    saved = dict(h=h, hn=hn, proj=proj, qkv=qkv, bg=bg, prep=prep, o=o, s_hist=s_hist, mix=mix, h1=h1, h2n=h2n,
                 gate=gate, up=up, act=act)
    return h2, saved


def _layer_bwd_ffn(dh2, dh2b, p, s, after=()):
    dh2b = dh2b[None]
    dgate, dup = _mm("ffn_out_bwd", "nt", dh2b, [p["w_down"]], tm=1024, tn=FF_SH, tk=D, out_dtypes=(BF16, BF16),
                     extras=(s["gate"], s["up"]), epilogue=_swiglu_bwd_epilogue, after=after)
    dh2n = _mm("ffn_gate_bwd", "nn", dgate, [p["w_gate"]], tm=1024, tn=512, tk=FF_SH, reduce_g=True)[0]
    dh2n = _mm("ffn_up_bwd", "nn", dup, [p["w_up"]], tm=1024, tn=512, tk=FF_SH, reduce_g=True, resid=dh2n)[0][0]
    dh1, dh1b, d_norm_ffn = _rmsnorm_bwd("rms_ffn_bwd", dh2n, s["h1"], p["norm_ffn"], dh2)
    d_w_down = _mm("ffn_wdown_grad", "tn", s["act"], [dh2b], tm=FF_SH, tn=512, tk=T)[0]
    d_w_gate = _mm("ffn_wgate_grad", "tn", dgate, [s["h2n"][None]], tm=FF_SH, tn=512, tk=T)[0]
    d_w_up = _mm("ffn_wup_grad", "tn", dup, [s["h2n"][None]], tm=FF_SH, tn=512, tk=T)[0]
    return dh1, dh1b, dict(norm_ffn=d_norm_ffn, w_gate=d_w_gate, w_up=d_w_up, w_down=d_w_down)


def _layer_bwd_mixer(dh1, dh1b, p, s, after=(), midway=None, late=None):
    dh1b = dh1b[None]
    dmix = _mm("out_proj_bwd", "nt", dh1b, [p["w_out"]], tm=1024, tn=D // NCHIP, tk=D, n_n=D, after=after,
               b_spec=((None, D // NCHIP, D), lambda g, i, j, k: (j, 0, k)))[0][0]
    d_w_out = _mm("out_proj_wgrad", "tn", s["mix"][None], [dh1b], tm=512, tn=512, tk=T)[0][0]
    do, dproj, d_ong, d_lng, d_lnb, d_ws, d_bst = _mix_bwd(
        s["o"], s["proj"], p["o_norm_g"], p["ln_v_g"], p["ln_v_b"], p["w_s"], p["bst"], dmix)
    then = midway(do) if midway is not None else ()
    dqkv, dbg = _chunk_prep_bwd(s["qkv"], s["bg"], _chunk_scan_bwd(s["prep"], s["s_hist"], do, then))
    dproj, d_conv = _prep_bwd(s["proj"], p["conv_w"], dqkv, dproj)
    dproj, d_a_log, d_dt_bias = _gates_bwd(s["proj"], p["a_log"], p["dt_bias"], dbg, dproj)
    dproj = dproj[None]
    d_w_in = _mm("in_proj_wgrad", "tn", s["hn"][None], [dproj], tm=512, tn=640, tk=T)[0]
    last = late(dict(w_in=d_w_in, w_out=d_w_out)) if late is not None else ()
    dhn = _mm("in_proj_bwd", "nt", dproj, [p["w_in"][None]], tm=1024, tn=512, tk=NP, after=last)[0][0]
    dh, dhb, d_norm_mix = _rmsnorm_bwd("rms_mix_bwd", dhn, s["h"], p["norm_mix"], dh1)
    grads = dict(norm_mix=d_norm_mix, w_in=d_w_in, conv_w=d_conv, a_log=d_a_log, dt_bias=d_dt_bias, o_norm_g=d_ong,
                 ln_v_g=d_lng, ln_v_b=d_lnb, w_s=d_ws, bst=d_bst, w_out=d_w_out)
    return dh, dhb, grads


def _lanes(v, off=0):
    return jnp.zeros((1, LANE), F32).at[0, off:off + v.shape[0]].set(v)


def _w_in_pieces():
    regions = [(0, 2048, 0), (2048, 2056, BA_OFF), (2056, IN_DIM, 2048)]
    sh = IN_DIM // NCHIP
    out = []
    for j in range(NCHIP):
        for lo, hi, at in regions:
            a, b = max(lo, j * sh), min(hi, (j + 1) * sh)
            if a < b:
                out.append((j, a - j * sh, at + a - lo, b - a))
    return out


W_IN_PIECES = _w_in_pieces()
WT = 256


def _assemble_w_in(gathered, own, place):
    def body(place_ref, g_ref, own_ref, o_ref):
        o_ref[:, IN_DIM:] = jnp.zeros((WT, NP - IN_DIM), BF16)
        mine = own_ref[...]
        for j, src, dst, width in W_IN_PIECES:
            val = jnp.where(place_ref[0] == j, mine[:, src:src + width], g_ref[j, :, src:src + width])
            o_ref[:, dst:dst + width] = val

    sh = IN_DIM // NCHIP
    return pl.pallas_call(
        body, name="assemble_w_in",
        grid_spec=pltpu.PrefetchScalarGridSpec(
            num_scalar_prefetch=1, grid=(D // WT,),
            in_specs=[pl.BlockSpec((NCHIP, WT, sh), lambda i, place_ref: (0, i, 0)),
                      pl.BlockSpec((WT, sh), lambda i, place_ref: (i, 0))],
            out_specs=pl.BlockSpec((WT, NP), lambda i, place_ref: (i, 0))),
        out_shape=jax.ShapeDtypeStruct((D, NP), BF16), compiler_params=_cparams(("parallel",)),
    )(place, gathered, own)


def _layer_params(l, big, small):
    return dict(
        {k: v for k, v in big.items() if k != "conv_w"},
        conv_w=jnp.concatenate([big["conv_w"][j, l] for j in range(NCHIP)], axis=1),
        norm_mix=small["norm_mix"][l][None], norm_ffn=small["norm_ffn"][l][None],
        a_log=_lanes(small["a_log"][l], HEADS), dt_bias=_lanes(small["dt_bias"][l], HEADS),
        o_norm_g=small["o_norm_g"][l][None], ln_v_g=small["ln_v_g"][l][None], ln_v_b=small["ln_v_b"][l][None],
        w_s=small["w_s"][l],
        bst=jnp.pad(small["b_s"][l].T, ((0, 0), (0, LANE - HEADS))),
    )


def _reference_layout(g):
    return dict(
        w_in=g["w_in"],
        w_out=g["w_out"].reshape(NCHIP, D // NCHIP, D),
        w_gate=g["w_gate"], w_up=g["w_up"], w_down=g["w_down"],
        conv_w=g["conv_w"], norm_mix=g["norm_mix"][0], norm_ffn=g["norm_ffn"][0],
        a_log=g["a_log"][0, HEADS:2 * HEADS], dt_bias=g["dt_bias"][0, HEADS:2 * HEADS],
        o_norm_g=g["o_norm_g"][0], ln_v_g=g["ln_v_g"][0], ln_v_b=g["ln_v_b"][0], w_s=g["w_s"],
        b_s=g["bst"][:, :HEADS].T,
    )


def _forward(x, tgt, layers, norm_final):
    h = x
    saved, params = [], []
    for p in layers:
        p = p(h) if callable(p) else p
        h, s = _layer_fwd(h, p)
        saved.append(s)
        params.append(p)
    return (saved, params) + tuple(_loss_head(h, norm_final, tgt))


def _local_step(x, tgt, layers, norm_final):
    saved, layers, loss, dh, dhb, d_norm_final = _forward(x, tgt, layers, norm_final)
    grads = [None] * DEPTH
    for l in reversed(range(DEPTH)):
        dh1, dh1b, g_ffn = _layer_bwd_ffn(dh, dhb, layers[l], saved[l])
        dh, dhb, g_mix = _layer_bwd_mixer(dh1, dh1b, layers[l], saved[l])
        grads[l] = {**g_ffn, **g_mix}
    return loss, dh, grads, d_norm_final


def _place():
    x, y, c = lax.axis_index("x"), lax.axis_index("y"), lax.axis_index("c")
    return x, y, c, [(1 - x, y), (x, 1 - y), (1 - x, 1 - y)]


def _remote(src, dst, send_sem, recv_sem, to):
    return pltpu.make_async_remote_copy(src_ref=src, dst_ref=dst, send_sem=send_sem, recv_sem=recv_sem,
                                        device_id=to, device_id_type=MESH)


def _comm_call(name, body, ins, out_shape, n_sems, aliases=None):
    return pl.pallas_call(
        body, name=name, in_specs=[ANY] * len(ins), out_specs=[ANY] * len(out_shape), out_shape=out_shape,
        scratch_shapes=[pltpu.SemaphoreType.DMA((n,)) for n in n_sems], input_output_aliases=aliases or {},
        compiler_params=pltpu.CompilerParams(has_side_effects=True),
    )(*ins)


def _half_rows(ref, of_c, dim):
    hr = ref.shape[dim] // 2
    return pl.ds(pl.multiple_of(of_c * hr, BF16_ROWS), hr)


def _gather_plan(whole):
    def plan(srcs, lands):
        x, y, c, others = _place()
        chip = 2 * x + y
        out = []
        for src, land, all_of_it in zip(srcs, lands, whole):
            for ox, oy in others:
                if all_of_it:
                    out.append((src, land.at[chip], (ox, oy, c)))
                else:
                    out.append((src.at[_half_rows(src, c, 0)], land.at[chip, _half_rows(src, c, 0)], (ox, oy, c)))
        return out
    return plan


def _forward_halves(lands):
    n = len(lands)

    def body(*refs):
        outs = refs[n:2 * n]
        send_s, recv_s = refs[2 * n:]
        x, y, c, others = _place()
        sibling = (x, y, 1 - c)
        copies = []
        for a in range(n):
            for k, (ox, oy) in enumerate(others):
                mine = outs[a].at[2 * ox + oy, _half_rows(outs[a], c, 1)]
                copies.append(_remote(mine, mine, send_s.at[3 * a + k], recv_s.at[3 * a + k], sibling))
        for cp in copies:
            cp.start()
        for a in range(n):
            for k, (ox, oy) in enumerate(others):
                landed = outs[a].at[2 * ox + oy, _half_rows(outs[a], 1 - c, 1)]
                _remote(landed, landed, send_s.at[3 * a + k], recv_s.at[3 * a + k], sibling).wait_recv()
        for cp in copies:
            cp.wait_send()

    out_shape = [jax.ShapeDtypeStruct(g.shape, g.dtype) for g in lands]
    return _comm_call("forward_halves", body, lands, out_shape, [3 * n, 3 * n], aliases={a: a for a in range(n)})


HBM_SPEC = pl.BlockSpec(memory_space=pltpu.HBM)
SEM_SPEC = pl.BlockSpec(memory_space=pltpu.SEMAPHORE)
DATAFLOW = pltpu.SideEffectType.DATAFLOW_SIDE_EFFECTING


def _exchange_plan(srcs, lands):
    x, y, c, _ = _place()
    plan = []
    for src, land in zip(srcs, lands):
        hr = src.shape[1] // 2
        plan.append((src.at[:, pl.ds(pl.multiple_of((1 - c) * hr, 8), hr)], land, (x, y, 1 - c)))
    return plan


def _scatter_plan(srcs, lands):
    x, y, c, others = _place()
    return [(src.at[2 * ox + oy], land.at[k], (ox, oy, c))
            for src, land in zip(srcs, lands) for k, (ox, oy) in enumerate(others)]


def _split_start(name, plan, srcs, land_shapes, n_copies, after=()):
    n = len(srcs)
    lands = [pltpu.with_memory_space_constraint(lax.empty(s.shape, s.dtype), pltpu.HBM) for s in land_shapes]
    srcs = [pltpu.with_memory_space_constraint(s, pltpu.HBM) for s in srcs]

    def body(*refs):
        send_s, recv_s = refs[2 * n + len(after)], refs[2 * n + len(after) + 1]
        for i, (src, dst, to) in enumerate(plan(refs[:n], refs[n:2 * n])):
            _remote(src, dst, send_s.at[i], recv_s.at[i], to).start()
        refs[-1][...] = jnp.zeros_like(refs[-1])

    thru = [pltpu.HBM(s.shape, s.dtype) for s in srcs + lands]
    out = pl.pallas_call(
        body, name=name, in_specs=[HBM_SPEC] * (2 * n) + [ANY] * len(after),
        out_specs=[SEM_SPEC, SEM_SPEC] + [HBM_SPEC] * (2 * n) + [pl.BlockSpec(memory_space=pltpu.VMEM)],
        out_shape=[pltpu.SemaphoreType.DMA((n_copies,)), pltpu.SemaphoreType.DMA((n_copies,))] + thru
        + [jax.ShapeDtypeStruct((F32_ROWS, LANE), F32)],
        input_output_aliases={i: 2 + i for i in range(2 * n)},
        compiler_params=pltpu.CompilerParams(has_side_effects=DATAFLOW),
    )(*srcs, *lands, *after)
    return dict(sems=out[:2], srcs=out[2:2 + n], lands=out[2 + n:2 + 2 * n], token=out[-1])


def _split_wait(name, plan, started, after):
    n = len(started["srcs"])

    def body(*refs):
        send_s, recv_s = refs[2 * n], refs[2 * n + 1]
        for i, (src, dst, to) in enumerate(plan(refs[:n], refs[n:2 * n])):
            cp = _remote(src, dst, send_s.at[i], recv_s.at[i], to)
            cp.wait_send()
            cp.wait_recv()

    arrs = list(started["srcs"]) + list(started["lands"])
    out = pl.pallas_call(
        body, name=name, in_specs=[HBM_SPEC] * (2 * n) + [SEM_SPEC, SEM_SPEC, ANY],
        out_specs=[HBM_SPEC] * (2 * n), out_shape=[pltpu.HBM(s.shape, s.dtype) for s in arrs],
        input_output_aliases={i: i for i in range(2 * n)},
        compiler_params=pltpu.CompilerParams(has_side_effects=DATAFLOW),
    )(*arrs, *started["sems"], after)
    return out[:n], out[n:]


def _join_halves(rs):
    n = len(rs)

    def body(*refs):
        outs = refs[n:2 * n]
        send_s, recv_s = refs[2 * n:]
        x, y, c, _ = _place()
        sibling = (x, y, 1 - c)

        def half(a, of_c):
            hr = outs[a].shape[1] // 2
            return outs[a].at[:, pl.ds(pl.multiple_of(of_c * hr, 8), hr)]

        copies = [_remote(half(a, c), half(a, c), send_s.at[a], recv_s.at[a], sibling) for a in range(n)]
        for cp in copies:
            cp.start()
        for a in range(n):
            landed = half(a, 1 - c)
            _remote(landed, landed, send_s.at[a], recv_s.at[a], sibling).wait_recv()
        for cp in copies:
            cp.wait_send()

    out_shape = [jax.ShapeDtypeStruct(r.shape, r.dtype) for r in rs]
    return _comm_call("join_halves", body, rs, out_shape, [n, n], aliases={a: a for a in range(n)})


def _allreduce_small(buf):
    r = buf.shape[0]
    hr = r // 2

    def body(in_ref, out_ref, theirs, by_chip, send_s, recv_s):
        x, y, c, others = _place()
        chip = 2 * x + y
        sibling = (x, y, 1 - c)
        mine = pl.ds(pl.multiple_of(c * hr, F32_ROWS), hr)
        swap = _remote(in_ref, theirs, send_s.at[0], recv_s.at[0], sibling)
        swap.start()
        swap.wait()
        by_chip[chip] = in_ref[mine, :] + theirs[mine, :]
        sends = [_remote(by_chip.at[chip], by_chip.at[chip], send_s.at[1 + k], recv_s.at[1 + k], (ox, oy, c))
                 for k, (ox, oy) in enumerate(others)]
        for cp in sends:
            cp.start()
        for k, (ox, oy) in enumerate(others):
            landed = by_chip.at[2 * ox + oy]
            _remote(landed, landed, send_s.at[1 + k], recv_s.at[1 + k], (ox, oy, c)).wait_recv()
        for cp in sends:
            cp.wait_send()
        out_ref[mine, :] = (by_chip[0] + by_chip[1]) + (by_chip[2] + by_chip[3])
        back = _remote(out_ref.at[mine], out_ref.at[mine], send_s.at[NCHIP], recv_s.at[NCHIP], sibling)
        back.start()
        other = out_ref.at[pl.ds(pl.multiple_of((1 - c) * hr, F32_ROWS), hr)]
        _remote(other, other, send_s.at[NCHIP], recv_s.at[NCHIP], sibling).wait_recv()
        back.wait_send()

    vm = pl.BlockSpec(memory_space=pltpu.VMEM)
    return pl.pallas_call(
        body, name="allreduce_small", in_specs=[vm], out_specs=vm, out_shape=jax.ShapeDtypeStruct((r, LANE), F32),
        scratch_shapes=[pltpu.VMEM((r, LANE), F32), pltpu.VMEM((NCHIP, hr, LANE), F32),
                        pltpu.SemaphoreType.DMA((NCHIP + 1,)), pltpu.SemaphoreType.DMA((NCHIP + 1,))],
        compiler_params=pltpu.CompilerParams(has_side_effects=True, vmem_limit_bytes=VMEM_LIMIT),
    )(buf)


MAX_ROW_TILE = 512
BF16_ROWS = 16


def _row_tile(rows):
    for t in range(min(rows, MAX_ROW_TILE) // BF16_ROWS * BF16_ROWS, 0, -BF16_ROWS):
        if rows % t == 0:
            return t
    raise ValueError(rows)


def _sum_halves(g, theirs, c_arr):
    nch, rows, cols = g.shape
    hr = rows // 2
    tr = _row_tile(hr)

    def body(c_ref, g_ref, t_ref, o_ref, ob_ref):
        s = g_ref[...] + t_ref[...]
        o_ref[...] = s
        ob_ref[...] = s.astype(BF16)

    blk = pl.BlockSpec((None, tr, cols), lambda j, i, c_ref: (j, i, 0))
    return pl.pallas_call(
        body, name="sum_halves",
        grid_spec=pltpu.PrefetchScalarGridSpec(
            num_scalar_prefetch=1, grid=(nch, hr // tr),
            in_specs=[pl.BlockSpec((None, None, tr, cols), lambda j, i, c_ref: (j, c_ref[0], i, 0)), blk],
            out_specs=[blk, blk]),
        out_shape=[jax.ShapeDtypeStruct((nch, hr, cols), F32), jax.ShapeDtypeStruct((nch, hr, cols), BF16)],
        compiler_params=_cparams(("parallel", "parallel")),
    )(c_arr, g.reshape(nch, 2, hr, cols), theirs)


def _sum_halves_w_in(g, theirs, c_arr):
    hr = D // 2
    sh = IN_DIM // NCHIP

    def body(c_ref, g_ref, t_ref, o_ref, ob_ref):
        s = g_ref[...] + t_ref[...]
        for j, dst, src, width in W_IN_PIECES:
            o_ref[j, :, dst:dst + width] = s[:, src:src + width]
            ob_ref[j, :, dst:dst + width] = s[:, src:src + width].astype(BF16)

    out = pl.BlockSpec((NCHIP, WT, sh), lambda i, c_ref: (0, i, 0))
    return pl.pallas_call(
        body, name="sum_halves_w_in",
        grid_spec=pltpu.PrefetchScalarGridSpec(
            num_scalar_prefetch=1, grid=(hr // WT,),
            in_specs=[pl.BlockSpec((None, WT, NP), lambda i, c_ref: (c_ref[0], i, 0)),
                      pl.BlockSpec((None, WT, NP), lambda i, c_ref: (0, i, 0))],
            out_specs=[out, out]),
        out_shape=[jax.ShapeDtypeStruct((NCHIP, hr, sh), F32), jax.ShapeDtypeStruct((NCHIP, hr, sh), BF16)],
        compiler_params=_cparams(("parallel",)),
    )(c_arr, g.reshape(2, hr, NP), theirs)


def _sum_chips(p, q, place, l, into=None, after=()):
    extra = ([into] if into is not None else []) + list(after)
    _, rows, cols = p.shape
    tr = _row_tile(rows)
    steps = rows // tr

    def body(place_ref, p_ref, q0, q1, q2, *rest):
        rest[-1][...] = ((p_ref[...] + q0[...].astype(F32)) + q1[...].astype(F32)) + q2[...].astype(F32)

    qs = lambda k: pl.BlockSpec((None, tr, cols), lambda i, place_ref: (k, i, 0))
    return pl.pallas_call(
        body, name="sum_chips",
        grid_spec=pltpu.PrefetchScalarGridSpec(
            num_scalar_prefetch=1, grid=(steps,),
            in_specs=[pl.BlockSpec((None, tr, cols), lambda i, place_ref: (place_ref[0], i, 0)), qs(0), qs(1), qs(2)]
            + [ANY] * len(extra),
            out_specs=pl.BlockSpec((None, tr, cols), lambda i, place_ref: (l, place_ref[1] * steps + i, 0))),
        out_shape=jax.ShapeDtypeStruct((DEPTH, 2 * rows, cols), F32),
        input_output_aliases={5: 0} if into is not None else {},
        compiler_params=_cparams(("parallel",)),
    )(place, p, q, q, q, *extra)


def _adamw(w, g, m, v):
    layers, rows, cols = w.shape
    tr = _row_tile(rows)

    def body(w_ref, g_ref, m_ref, v_ref, d_ref, nm_ref, nv_ref):
        gv = g_ref[...]
        nm = ADAM_B1 * m_ref[...] + (1.0 - ADAM_B1) * gv
        nv = ADAM_B2 * v_ref[...] + (1.0 - ADAM_B2) * jnp.square(gv)
        m_hat = nm / (1.0 - ADAM_B1 ** ADAM_STEP)
        v_hat = nv / (1.0 - ADAM_B2 ** ADAM_STEP)
        d_ref[...] = -ADAM_LR * (m_hat / (jnp.sqrt(v_hat) + ADAM_EPS) + ADAM_WD * w_ref[...])
        nm_ref[...] = nm
        nv_ref[...] = nv

    blk = pl.BlockSpec((None, tr, cols), lambda l, i: (l, i, 0))
    return pl.pallas_call(
        body, name="adamw", grid=(layers, rows // tr), in_specs=[blk] * 4, out_specs=[blk] * 3,
        out_shape=[jax.ShapeDtypeStruct(w.shape, F32)] * 3, compiler_params=_cparams(("parallel", "parallel")),
    )(w, g, m, v)


BIG = ("w_in", "w_out", "w_gate", "w_up", "w_down")
SMALL = ("norm_mix", "a_log", "dt_bias", "o_norm_g", "ln_v_g", "ln_v_b", "w_s", "b_s", "norm_ffn", "norm_final")
ORDER = ("norm_mix", "w_in", "conv_w", "a_log", "dt_bias", "o_norm_g", "ln_v_g", "ln_v_b", "w_s", "b_s", "w_out",
         "norm_ffn", "w_gate", "w_up", "w_down", "norm_final")


F32_ROWS = 8
PACK_ROWS = 128


def _lane_rows(size):
    return -(-size // (F32_ROWS * LANE)) * F32_ROWS


def _pack(arrs):
    parts = [jnp.pad(a.reshape(-1), (0, _lane_rows(a.size) * LANE - a.size)).reshape(-1, LANE) for a in arrs]
    rows = sum(p.shape[0] for p in parts)
    if rows % PACK_ROWS:
        parts.append(jnp.zeros((-rows % PACK_ROWS, LANE), F32))
    return jnp.concatenate(parts, axis=0)


def _unpack(buf, like):
    out, row = [], 0
    for a in like:
        n = _lane_rows(a.size)
        out.append(buf[row:row + n].reshape(-1)[:a.size].reshape(a.shape))
        row += n
    return out


def kernel(x, norm_mix, w_in, conv_w, a_log, dt_bias, o_norm_g, ln_v_g, ln_v_b, w_s, b_s, w_out, norm_ffn, w_gate, w_up, w_down, norm_final, loss_target, m_norm_mix, m_w_in, m_conv_w, m_a_log, m_dt_bias, m_o_norm_g, m_ln_v_g, m_ln_v_b, m_w_s, m_b_s, m_w_out, m_norm_ffn, m_w_gate, m_w_up, m_w_down, m_norm_final, v_norm_mix, v_w_in, v_conv_w, v_a_log, v_dt_bias, v_o_norm_g, v_ln_v_g, v_ln_v_b, v_w_s, v_b_s, v_w_out, v_norm_ffn, v_w_gate, v_w_up, v_w_down, v_norm_final):
    w = dict(norm_mix=norm_mix, w_in=w_in, conv_w=conv_w, a_log=a_log, dt_bias=dt_bias, o_norm_g=o_norm_g,
             ln_v_g=ln_v_g, ln_v_b=ln_v_b, w_s=w_s, b_s=b_s, w_out=w_out, norm_ffn=norm_ffn, w_gate=w_gate, w_up=w_up,
             w_down=w_down, norm_final=norm_final)
    m = dict(norm_mix=m_norm_mix, w_in=m_w_in, conv_w=m_conv_w, a_log=m_a_log, dt_bias=m_dt_bias, o_norm_g=m_o_norm_g,
             ln_v_g=m_ln_v_g, ln_v_b=m_ln_v_b, w_s=m_w_s, b_s=m_b_s, w_out=m_w_out, norm_ffn=m_norm_ffn,
             w_gate=m_w_gate, w_up=m_w_up, w_down=m_w_down, norm_final=m_norm_final)
    v = dict(norm_mix=v_norm_mix, w_in=v_w_in, conv_w=v_conv_w, a_log=v_a_log, dt_bias=v_dt_bias, o_norm_g=v_o_norm_g,
             ln_v_g=v_ln_v_g, ln_v_b=v_ln_v_b, w_s=v_w_s, b_s=v_b_s, w_out=v_w_out, norm_ffn=v_norm_ffn,
             w_gate=v_w_gate, w_up=v_w_up, w_down=v_w_down, norm_final=v_norm_final)
    chip = 2 * lax.axis_index("x") + lax.axis_index("y")
    place = jnp.stack([chip, lax.axis_index("c")]).astype(jnp.int32)
    c_arr = place[1:]

    def kernel_view(n, a):
        return jnp.swapaxes(a, 1, 2) if n in ("w_gate", "w_up") else a

    own = {n: [kernel_view(n, w[n])[l].astype(BF16) for l in range(DEPTH)] for n in BIG}
    by_chip = lambda a: jax.ShapeDtypeStruct((NCHIP,) + a.shape, a.dtype)

    def start(name, srcs, whole, after=()):
        return _split_start(name, _gather_plan(whole), srcs, [by_chip(a) for a in srcs], 3 * len(srcs), after)

    def finish(name, started, whole, after):
        srcs, lands = _split_wait(name, _gather_plan(whole), started, after)
        passed = iter(_forward_halves([g for g, all_of_it in zip(lands, whole) if not all_of_it]))
        lands = [g if all_of_it else next(passed) for g, all_of_it in zip(lands, whole)]
        return srcs, [lax.dynamic_update_index_in_dim(g, o, chip, 0) for g, o in zip(lands, srcs)]

    ffn = BIG[1:]
    first = start("gather_first_start", [own["w_in"][0], conv_w], [False, True])
    early = start("gather_early_start", [own[n][0] for n in ffn], [False] * len(ffn), [first["token"]])
    later = start("gather_later_start", [own[n][1] for n in BIG], [False] * len(BIG), [early["token"]])
    (own_w_in, _), (w_in_by_chip, conv_by_chip) = finish("gather_first_wait", first, [False, True], later["token"])

    def late(after):
        return dict(zip(ffn, finish("gather_early_wait", early, [False] * len(ffn), after)[1]))

    layer0 = _layer_params(0, dict(w_in=_assemble_w_in(w_in_by_chip, own_w_in, place), conv_w=conv_by_chip, late=late), w)

    def layer1(after):
        srcs, by = finish("gather_later_wait", later, [False] * len(BIG), after)
        big = dict(zip(ffn, by[1:]), w_in=_assemble_w_in(by[0], srcs[0], place), conv_w=conv_by_chip)
        return _layer_params(1, big, w)

    saved, layers, loss_lanes, dh, dhb, d_norm_final = _forward(x[0], loss_target[0], [layer0, layer1],
                                                                 norm_final[None])
    loss = lax.psum(loss_lanes[0, 0], ("x", "y", "c"))

    sums, arrived = {}, {}

    def exchange_start(tag, l, names, grads, after=()):
        mine = [grads[n] for n in names]
        shapes = [jax.ShapeDtypeStruct((g.shape[0], g.shape[1] // 2, g.shape[2]), F32) for g in mine]
        return tag, l, names, _split_start(f"exchange_{tag}_start", _exchange_plan, mine, shapes, len(mine), after)

    def add_halves(l, names, mine, theirs):
        for n, g, t in zip(names, mine, theirs):
            sums[l, n] = (_sum_halves_w_in if n == "w_in" else _sum_halves)(g, t, c_arr)

    def exchange_wait(handle, after):
        tag, l, names, started = handle
        add_halves(l, names, *_split_wait(f"exchange_{tag}_wait", _exchange_plan, started, after))

    def scatter_start(tag, l, names, after=()):
        partial = [sums[l, n][1] for n in names]
        shapes = [jax.ShapeDtypeStruct((3,) + p.shape[1:], p.dtype) for p in partial]
        return tag, l, names, _split_start(f"scatter_{tag}_start", _scatter_plan, partial, shapes, 3 * len(names), after)

    def scatter_wait(handle, after):
        tag, l, names, started = handle
        for n, q in zip(names, _split_wait(f"scatter_{tag}_wait", _scatter_plan, started, after)[1]):
            arrived[l, n] = q

    last = DEPTH - 1
    swiglu = BIG[2:]
    dh1, dh1b, g_ffn = _layer_bwd_ffn(dh, dhb, layers[last], saved[last])
    dh, dhb, g_mix = _layer_bwd_mixer(dh1, dh1b, layers[last], saved[last])
    gl = [None, _reference_layout({**g_ffn, **g_mix})]
    ex_last = exchange_start("last", last, BIG, gl[last])
    dh1, dh1b, g_ffn = _layer_bwd_ffn(dh, dhb, layers[0], saved[0], after=[ex_last[-1]["token"]])
    exchange_wait(ex_last, dh1)
    sc_last = scatter_start("last", last, BIG)
    ex_ffn = exchange_start("swiglu", 0, swiglu, g_ffn, [sc_last[-1]["token"]])
    sc_ffn = []

    def midway(do):
        exchange_wait(ex_ffn, do)
        sc_ffn.append(scatter_start("swiglu", 0, swiglu))
        return [sc_ffn[0][-1]["token"]]

    ex_rest = []

    def late(grads):
        rest_grads = dict(w_in=grads["w_in"], w_out=grads["w_out"].reshape(NCHIP, D // NCHIP, D))
        ex_rest.append(exchange_start("rest", 0, BIG[:2], rest_grads))
        return [ex_rest[0][-1]["token"]]

    dx, _, g_mix = _layer_bwd_mixer(dh1, dh1b, layers[0], saved[0], after=[ex_ffn[-1]["token"]], midway=midway,
                                    late=late)
    scatter_wait(sc_last, dx)
    scatter_wait(sc_ffn[0], dx)
    exchange_wait(ex_rest[0], dx)
    sc_rest = scatter_start("rest", 0, BIG[:2])
    gl[0] = _reference_layout({**g_ffn, **g_mix})

    reduced = {}
    order = [sc_rest[-1]["token"]]
    for n in BIG:
        for l in (range(DEPTH) if n in swiglu else [last]):
            reduced[n] = _sum_chips(sums[l, n][0], arrived[l, n], place, l, into=reduced.get(n), after=order)
            order = []
    small_g = [jnp.stack([gl[l][n] for l in range(DEPTH)]) for n in SMALL[:-1]] + [d_norm_final[0]]
    conv_g = jnp.stack([gl[l]["conv_w"] for l in range(DEPTH)])
    total = _allreduce_small(_pack(small_g + [conv_g]))
    scatter_wait(sc_rest, total)
    for n in BIG[:2]:
        reduced[n] = _sum_chips(sums[0, n][0], arrived[0, n], place, 0, into=reduced[n])
    g_out = dict(zip(BIG, _join_halves([reduced[n] for n in BIG])))
    *small_r, conv_r = _unpack(total, small_g + [conv_g])
    g_out.update(zip(SMALL, small_r))
    g_out["conv_w"] = lax.dynamic_slice_in_dim(conv_r, chip * conv_w.shape[2], conv_w.shape[2], axis=2)

    delta, new_m, new_v = {}, {}, {}
    for n in BIG:
        res = _adamw(kernel_view(n, w[n]), g_out[n], kernel_view(n, m[n]), kernel_view(n, v[n]))
        g_out[n], delta[n], new_m[n], new_v[n] = (kernel_view(n, a) for a in (g_out[n],) + tuple(res))
    rest = SMALL + ("conv_w",)
    like = [w[n] for n in rest]
    d, nm, nv = _adamw(*[_pack([src[n] for n in rest])[None] for src in (w, g_out, m, v)])
    for dst, buf in ((delta, d), (new_m, nm), (new_v, nv)):
        dst.update(zip(rest, _unpack(buf[0], like)))

    return (loss, dx[None], *[g_out[n] for n in ORDER], *[delta[n] for n in ORDER], *[new_m[n] for n in ORDER],
            *[new_v[n] for n in ORDER])
```

```python
import functools

import jax
import jax.numpy as jnp
from jax import lax
from jax.experimental import pallas as pl
from jax.experimental.pallas import tpu as pltpu

F32 = jnp.float32
BF16 = jnp.bfloat16
MESH = pl.DeviceIdType.MESH
ANY = pl.BlockSpec(memory_space=pl.ANY)
HIGHEST = lax.Precision.HIGHEST

T = 2048
D = 1024
DEPTH = 2
NCHIP = 4
HEADS = 4
HD = 128
HW = HEADS * HD
CH = 64
GCH = 128
IN_DIM = 3080
NP = 3200
BA_OFF = 3072
FF_SH = 704
EPS = 1e-6
LANE = 128
VMEM_LIMIT = 56 * 1024 * 1024

ADAM_LR = 0.001
ADAM_B1 = 0.9
ADAM_B2 = 0.999
ADAM_EPS = 1e-08
ADAM_WD = 0.01
ADAM_STEP = 10


def _cparams(sem=None):
    return pltpu.CompilerParams(dimension_semantics=sem, vmem_limit_bytes=VMEM_LIMIT)


_DIMS = {"nn": (((1,), (0,)), ((), ())), "nt": (((1,), (1,)), ((), ())), "tn": (((0,), (0,)), ((), ()))}


def _mm(name, mode, a, bs, *, tm, tn, tk, out_dtypes=(F32,), reduce_g=False, resid=None, extras=(), epilogue=None,
        b_spec=None, n_n=None, after=()):
    nb = len(bs)
    ga = a.shape[0]
    gbs = [1 if b_spec is not None else b.shape[0] for b in bs]
    g_n = max([ga] + gbs)
    if mode == "tn":
        k_n, m_n = a.shape[1:]
    else:
        m_n, k_n = a.shape[1:]
    if n_n is None:
        n_n = bs[0].shape[1] if mode == "nt" else bs[0].shape[2]
    assert m_n % tm == 0 and n_n % tn == 0 and k_n % tk == 0, (name, m_n, n_n, k_n)
    mi, nj, kk = m_n // tm, n_n // tn, k_n // tk
    if reduce_g:
        grid = (mi, nj, g_n, kk)
        ids = lambda i, j, g, k: (g, i, j, k)
        n_red = g_n * kk
        red_idx = lambda: pl.program_id(2) * kk + pl.program_id(3)
        sem = ("parallel", "parallel", "arbitrary", "arbitrary")
    else:
        grid = (g_n, mi, nj, kk)
        ids = lambda g, i, j, k: (g, i, j, k)
        n_red = kk
        red_idx = lambda: pl.program_id(3)
        sem = ("parallel", "parallel", "parallel", "arbitrary")

    def pick(gsz, g):
        return g if gsz > 1 else 0

    def a_map(*p):
        g, i, j, k = ids(*p)
        return (pick(ga, g), k, i) if mode == "tn" else (pick(ga, g), i, k)

    def b_map(gsz):
        def f(*p):
            g, i, j, k = ids(*p)
            if b_spec is not None:
                return b_spec[1](g, i, j, k)
            return (pick(gsz, g), j, k) if mode == "nt" else (pick(gsz, g), k, j)
        return f

    def o_map(gsz):
        def f(*p):
            g, i, j, k = ids(*p)
            return (0 if reduce_g else pick(gsz, g), i, j)
        return f

    a_spec = pl.BlockSpec((None, tk, tm) if mode == "tn" else (None, tm, tk), a_map)
    b_block = b_spec[0] if b_spec is not None else ((None, tn, tk) if mode == "nt" else (None, tk, tn))
    b_specs = [pl.BlockSpec(b_block, b_map(gs)) for gs in gbs]
    x_specs = [pl.BlockSpec((None, tm, tn), o_map(e.shape[0])) for e in extras]
    r_specs = [pl.BlockSpec((None, tm, tn), o_map(resid.shape[0]))] if resid is not None else []
    g_out = 1 if reduce_g else g_n
    out_shape = [jax.ShapeDtypeStruct((g_out, m_n, n_n), dt) for dt in out_dtypes]
    out_specs = [pl.BlockSpec((None, tm, tn), o_map(g_out)) for _ in out_dtypes]
    nx, nr, no = len(extras), len(r_specs), len(out_dtypes)
    n_in = 1 + nb + nx + nr + len(after)
    dims = _DIMS[mode]

    def body(*refs):
        a_ref = refs[0]
        b_refs = refs[1:1 + nb]
        x_refs = refs[1 + nb:1 + nb + nx]
        r_refs = refs[1 + nb + nx:1 + nb + nx + nr]
        o_refs = refs[n_in:n_in + no]
        acc_refs = refs[n_in + no:]
        av = a_ref[...]
        products = [lax.dot_general(av, b_ref[...], dims, preferred_element_type=F32) for b_ref in b_refs]

        def finish(accs):
            if r_refs:
                accs[0] = accs[0] + r_refs[0][...]
            outs = epilogue(accs, [x[...] for x in x_refs]) if epilogue is not None else accs
            for o_ref, o in zip(o_refs, outs):
                o_ref[...] = o.astype(o_ref.dtype)

        if n_red == 1:
            finish(products)
            return
        r = red_idx()
        for p, acc in zip(products, acc_refs):
            @pl.when(r == 0)
            def _():
                acc[...] = p

            @pl.when((r > 0) & (r < n_red - 1))
            def _():
                acc[...] += p

        @pl.when(r == n_red - 1)
        def _():
            finish([acc[...] + p for p, acc in zip(products, acc_refs)])

    return pl.pallas_call(
        body, name=name, grid=grid,
        in_specs=[a_spec] + b_specs + x_specs + r_specs + [ANY] * len(after),
        out_specs=out_specs, out_shape=out_shape,
        scratch_shapes=[pltpu.VMEM((tm, tn), F32) for _ in range(nb if n_red > 1 else 0)],
        compiler_params=_cparams(sem),
    )(a, *bs, *extras, *([resid] if resid is not None else []), *after)


def _sigmoid(x):
    return 1.0 / (1.0 + jnp.exp(-x))


def _silu(x):
    return x * _sigmoid(x)


def _gelu(x):
    return 0.5 * x * (1.0 + jnp.tanh(0.7978845608028654 * (x + 0.044715 * (x * x * x))))


def _rms_fn(h, gain):
    return h * lax.rsqrt(jnp.mean(h * h, axis=-1, keepdims=True) + EPS) * gain


def _shift_impl(x, s):
    n = x.shape[0]
    rolled = pltpu.roll(x, s % n, 0)
    row = lax.broadcasted_iota(jnp.int32, x.shape, 0)
    return jnp.where((row >= s) & (row < n + s), rolled, 0.0)


@functools.partial(jax.custom_vjp, nondiff_argnums=(1,))
def _shift(x, s):
    return _shift_impl(x, s)


def _shift_fwd(x, s):
    return _shift_impl(x, s), None


def _shift_bwd(s, _, g):
    return (_shift_impl(g, -s),)


_shift.defvjp(_shift_fwd, _shift_bwd)


def _prep_fn(x, w, qk_scale, is_v):
    y = x * w[3:4, :]
    for i in range(3):
        y = y + _shift(x, 3 - i) * w[i:i + 1, :]
    y = _silu(y)
    nrm = lax.rsqrt(jnp.sum(y * y, axis=-1, keepdims=True) + EPS) * qk_scale
    return y * jnp.where(is_v, 1.0, nrm)


def _softplus(x):
    return jnp.maximum(x, 0.0) + jnp.log(1.0 + jnp.exp(-jnp.abs(x)))


def _gates_fn(ba, a_log, dt_bias):
    lane = lax.broadcasted_iota(jnp.int32, ba.shape, 1)
    beta = _sigmoid(ba)
    g = -jnp.exp(a_log) * _softplus(ba + dt_bias)
    return jnp.where(lane < HEADS, beta, g)


def _dot16(a, b, dims=_DIMS["nn"]):
    return lax.dot_general(a.astype(BF16), b.astype(BF16), dims, preferred_element_type=F32)


def _dot32(a, b):
    return jnp.dot(a, b, preferred_element_type=F32, precision=HIGHEST)


def _dot3(a, b, dims=_DIMS["nn"]):
    return lax.dot_general(a, b, dims, preferred_element_type=F32, precision=lax.Precision.HIGH)


def _tri_inverses(mats):
    row = lax.broadcasted_iota(jnp.int32, (CH, CH), 0)
    col = lax.broadcasted_iota(jnp.int32, (CH, CH), 1)
    eye = (row == col).astype(F32)
    ts = [eye - a for a in mats]
    ps = list(mats)
    for _ in range(5):
        ps = [_dot3(p, p) for p in ps]
        ts = [t + _dot3(t, p) for t, p in zip(ts, ps)]
    return ts


@jax.custom_vjp
def _tri_solves(mats, rhs):
    return [_dot3(t, b) for t, b in zip(_tri_inverses(mats), rhs)]


def _tri_solves_fwd(mats, rhs):
    ts = _tri_inverses(mats)
    xs = [_dot3(t, b) for t, b in zip(ts, rhs)]
    return xs, (ts, xs)


def _tri_solves_bwd(res, dxs):
    ts, xs = res
    dbs = [_dot3(t, dx, _DIMS["tn"]) for t, dx in zip(ts, dxs)]
    return [-_dot3(db, x, _DIMS["nt"]) for db, x in zip(dbs, xs)], dbs


_tri_solves.defvjp(_tri_solves_fwd, _tri_solves_bwd)


def _chunk_prep_fn(xs, bgs):
    row = lax.broadcasted_iota(jnp.int32, (CH, CH), 0)
    col = lax.broadcasted_iota(jnp.int32, (CH, CH), 1)
    incl = row >= col
    strict = row > col
    lmat = incl.astype(F32)
    n = len(xs)
    items = [(i, h) for i in range(n) for h in range(HEADS)]
    part = lambda i, h, c: xs[i][:, c * HW + h * HD:c * HW + (h + 1) * HD]
    q = [part(i, h, 0) for i, h in items]
    k = [part(i, h, 1) for i, h in items]
    v = [part(i, h, 2) for i, h in items]
    beta = [bgs[i][:, h:h + 1] for i, h in items]
    gc_all = [_dot32(lmat, bg) for bg in bgs]
    gc = [gc_all[i][:, HEADS + h:HEADS + h + 1] for i, h in items]
    gmat = [jnp.where(strict, jnp.broadcast_to(bgs[i][:, HEADS + h:HEADS + h + 1], (CH, CH)), 0.0) for i, h in items]
    diff = [_dot3(lmat, m) for m in gmat]
    decay = [jnp.where(incl, jnp.exp(jnp.where(incl, d, 0.0)), 0.0) for d in diff]
    k_beta = [kk * b for kk, b in zip(k, beta)]
    kk_t = [_dot16(kb, kk, _DIMS["nt"]) for kb, kk in zip(k_beta, k)]
    qk_t = [_dot16(qq, kk, _DIMS["nt"]) for qq, kk in zip(q, k)]
    a = [jnp.where(strict, m * d, 0.0) for m, d in zip(kk_t, decay)]
    eg = [jnp.exp(g) for g in gc]
    rhs = [jnp.concatenate([vv * b, kb * e], axis=-1) for vv, b, kb, e in zip(v, beta, k_beta, eg)]
    uw = _tri_solves(a, rhs)
    qk = [m * d for m, d in zip(qk_t, decay)]
    g_last = [g[CH - 1:CH, :] for g in gc]
    qe = [qq * e for qq, e in zip(q, eg)]
    kd = [kk * jnp.exp(gl - g) for kk, gl, g in zip(k, g_last, gc)]
    egl = [jnp.broadcast_to(jnp.exp(gl), (1, HD)) for gl in g_last]
    out = []
    for i in range(n):
        mine = slice(i * HEADS, (i + 1) * HEADS)
        cat = lambda vals: jnp.concatenate(vals[mine], axis=-1)
        out.append((cat([x[:, :HD] for x in uw]), cat([x[:, HD:] for x in uw]), cat(qe), cat(kd),
                    jnp.concatenate([m[None] for m in qk[mine]], axis=0), cat(egl)))
    return out


def _chunk_state_fn(u, w, qe, kd, qk, egl, s):
    ws = [_dot16(a, b) for a, b in zip(w, s)]
    qs = [_dot16(a, b) for a, b in zip(qe, s)]
    v_new = [a - b for a, b in zip(u, ws)]
    o = [a + _dot16(b, c) for a, b, c in zip(qs, qk, v_new)]
    s_new = [a * e + _dot16(b, c, _DIMS["tn"]) for a, e, b, c in zip(s, egl, kd, v_new)]
    return o, s_new


def _mix_fn(o, z, ur, vr, ong, lng, lnb, ws, bst):
    row = lax.broadcasted_iota(jnp.int32, (GCH, GCH), 0)
    col = lax.broadcasted_iota(jnp.int32, (GCH, GCH), 1)
    causal = row >= col
    ug = _gelu(ur)
    vg = _gelu(vr)
    sls = [slice(h * HD, (h + 1) * HD) for h in range(HEADS)]
    oh = [o[:, sl] for sl in sls]
    oh = [x * lax.rsqrt(jnp.mean(x * x, axis=-1, keepdims=True) + EPS) for x in oh]
    outs_dn = [x * ong * _silu(z[:, sl]) for x, sl in zip(oh, sls)]
    vh = [vg[:, sl] for sl in sls]
    mu = [jnp.mean(x, axis=-1, keepdims=True) for x in vh]
    var = [jnp.mean(jnp.square(x - m), axis=-1, keepdims=True) for x, m in zip(vh, mu)]
    vn = [(x - m) * lax.rsqrt(s + EPS) * lng[:, sl] + lnb[:, sl] for x, m, s, sl in zip(vh, mu, var, sls)]
    mixed = [_dot16(jnp.where(causal, ws[h], 0.0), vn[h]) for h in range(HEADS)]
    outs_gm = [ug[:, sl] * (mixed[h] + bst[:, h:h + 1]) for h, sl in enumerate(sls)]
    return jnp.concatenate(outs_dn + outs_gm, axis=-1)


def _loss_fn(h, gain, tgt):
    y = _rms_fn(h, gain)
    return 0.5 * jnp.sum(jnp.mean(jnp.square(y - tgt), axis=-1))


RT = 256


def _rows(n=D):
    return pl.BlockSpec((RT, n), lambda i: (i, 0))


def _whole(shape):
    nd = len(shape)
    return pl.BlockSpec(shape, lambda i: (0,) * nd)


def _rmsnorm(name, h, gain):
    def body(h_ref, g_ref, o_ref):
        o_ref[...] = _rms_fn(h_ref[...], g_ref[...]).astype(BF16)

    return pl.pallas_call(
        body, name=name, grid=(T // RT,), in_specs=[_rows(), _whole((1, D))], out_specs=_rows(),
        out_shape=jax.ShapeDtypeStruct((T, D), BF16), compiler_params=_cparams(("parallel",)),
    )(h, gain)


def _rmsnorm_bwd(name, dhn, h, gain, resid):
    def body(dhn_ref, h_ref, g_ref, r_ref, dh_ref, dh16_ref, dg_ref):
        _, vjp = jax.vjp(_rms_fn, h_ref[...], g_ref[...])
        dh, dg = vjp(dhn_ref[...])
        dh = r_ref[...] + dh
        dh_ref[...] = dh
        dh16_ref[...] = dh.astype(BF16)

        @pl.when(pl.program_id(0) == 0)
        def _():
            dg_ref[...] = dg

        @pl.when(pl.program_id(0) > 0)
        def _():
            dg_ref[...] += dg

    return pl.pallas_call(
        body, name=name, grid=(T // RT,), in_specs=[_rows(), _rows(), _whole((1, D)), _rows()],
        out_specs=[_rows(), _rows(), _whole((1, D))],
        out_shape=[jax.ShapeDtypeStruct((T, D), F32), jax.ShapeDtypeStruct((T, D), BF16),
                   jax.ShapeDtypeStruct((1, D), F32)],
        compiler_params=_cparams(("arbitrary",)),
    )(dhn, h, gain, resid)


def _loss_head(h, gain, tgt):
    def body(h_ref, g_ref, t_ref, l_ref, dh_ref, dh16_ref, dg_ref):
        loss, vjp = jax.vjp(lambda hh, gg: _loss_fn(hh, gg, t_ref[...]), h_ref[...], g_ref[...])
        dh, dg = vjp(jnp.ones((), F32))
        dh_ref[...] = dh
        dh16_ref[...] = dh.astype(BF16)
        lv = jnp.full((1, LANE), loss, F32)

        @pl.when(pl.program_id(0) == 0)
        def _():
            dg_ref[...] = dg
            l_ref[...] = lv

        @pl.when(pl.program_id(0) > 0)
        def _():
            dg_ref[...] += dg
            l_ref[...] += lv

    return pl.pallas_call(
        body, name="loss_head", grid=(T // RT,), in_specs=[_rows(), _whole((1, D)), _rows()],
        out_specs=[_whole((1, LANE)), _rows(), _rows(), _whole((1, D))],
        out_shape=[jax.ShapeDtypeStruct((1, LANE), F32), jax.ShapeDtypeStruct((T, D), F32),
                   jax.ShapeDtypeStruct((T, D), BF16), jax.ShapeDtypeStruct((1, D), F32)],
        compiler_params=_cparams(("arbitrary",)),
    )(h, gain, tgt)


def _prep_flags():
    j = pl.program_id(0)
    qk_scale = jnp.where(j < HEADS, HD ** -0.5, 1.0).astype(F32)
    return qk_scale, j >= 2 * HEADS


def _prep(proj, conv_w):
    def body(x_ref, w_ref, o_ref):
        qk_scale, is_v = _prep_flags()
        o_ref[...] = _prep_fn(x_ref[...], w_ref[...], qk_scale, is_v)

    col = lambda j: (0, j)
    return pl.pallas_call(
        body, name="gdn_prep", grid=(3 * HEADS,),
        in_specs=[pl.BlockSpec((T, HD), col), pl.BlockSpec((4, HD), col)], out_specs=pl.BlockSpec((T, HD), col),
        out_shape=jax.ShapeDtypeStruct((T, 3 * HW), F32), compiler_params=_cparams(("parallel",)),
    )(proj, conv_w)


def _prep_bwd(proj, conv_w, dqkv, dproj):
    def body(x_ref, w_ref, d_ref, _, dx_ref, dw_ref):
        qk_scale, is_v = _prep_flags()
        _, vjp = jax.vjp(lambda x, w: _prep_fn(x, w, qk_scale, is_v), x_ref[...], w_ref[...])
        dx, dw = vjp(d_ref[...])
        dx_ref[...] = dx.astype(BF16)
        dw_ref[...] = dw

    col = lambda j: (0, j)
    return pl.pallas_call(
        body, name="gdn_prep_bwd", grid=(3 * HEADS,),
        in_specs=[pl.BlockSpec((T, HD), col), pl.BlockSpec((4, HD), col), pl.BlockSpec((T, HD), col), ANY],
        out_specs=[pl.BlockSpec((T, HD), col), pl.BlockSpec((4, HD), col)],
        out_shape=[jax.ShapeDtypeStruct((T, NP), BF16), jax.ShapeDtypeStruct((4, 3 * HW), F32)],
        input_output_aliases={3: 0}, compiler_params=_cparams(("parallel",)),
    )(proj, conv_w, dqkv, dproj)


BA_BLK = BA_OFF // LANE


def _gates(proj, a_log, dt_bias):
    def body(x_ref, a_ref, d_ref, o_ref):
        o_ref[...] = _gates_fn(x_ref[...], a_ref[...], d_ref[...])

    return pl.pallas_call(
        body, name="gdn_gates", grid=(1,),
        in_specs=[pl.BlockSpec((T, LANE), lambda i: (0, BA_BLK)), _whole((1, LANE)), _whole((1, LANE))],
        out_specs=_whole((T, LANE)),
        out_shape=jax.ShapeDtypeStruct((T, LANE), F32), compiler_params=_cparams(("arbitrary",)),
    )(proj, a_log, dt_bias)


def _gates_bwd(proj, a_log, dt_bias, dbg, dproj):
    def body(x_ref, a_ref, d_ref, dbg_ref, _, dx_ref, da_ref, dd_ref):
        _, vjp = jax.vjp(_gates_fn, x_ref[...], a_ref[...], d_ref[...])
        dx, da_ref[...], dd_ref[...] = vjp(dbg_ref[...])
        dx_ref[...] = dx.astype(BF16)

    ba = pl.BlockSpec((T, LANE), lambda i: (0, BA_BLK))
    return pl.pallas_call(
        body, name="gdn_gates_bwd", grid=(1,),
        in_specs=[ba, _whole((1, LANE)), _whole((1, LANE)), _whole((T, LANE)), ANY],
        out_specs=[ba, _whole((1, LANE)), _whole((1, LANE))],
        out_shape=[jax.ShapeDtypeStruct((T, NP), BF16), jax.ShapeDtypeStruct((1, LANE), F32),
                   jax.ShapeDtypeStruct((1, LANE), F32)],
        input_output_aliases={4: 0}, compiler_params=_cparams(("arbitrary",)),
    )(proj, a_log, dt_bias, dbg, dproj)


NCK = T // CH
CPS = 2


def _chunk_prep_specs(rev=False):
    at = (lambda n: NCK - 1 - n) if rev else (lambda n: n)
    wide = pl.BlockSpec((CH, HW), lambda n: (at(n), 0))
    return [wide, wide, wide, wide, pl.BlockSpec((HEADS, CH, CH), lambda n: (0, at(n), 0)),
            pl.BlockSpec((None, 1, HW), lambda n: (at(n), 0, 0))]


def _chunk_prep_shapes(dtypes):
    shp = [(T, HW), (T, HW), (T, HW), (T, HW), (HEADS, T, CH), (NCK, 1, HW)]
    return [jax.ShapeDtypeStruct(s, dt) for s, dt in zip(shp, dtypes)]


def _chunk_prep(qkv, bg):
    def body(x_ref, bg_ref, *o_refs):
        rows = [slice(ci * CH, (ci + 1) * CH) for ci in range(CPS)]
        res = _chunk_prep_fn([x_ref[r, :] for r in rows], [bg_ref[r, :] for r in rows])
        for ci, (u, w, qe, kd, qk, egl) in enumerate(res):
            for o_ref, val in zip(o_refs[:4], (u, w, qe, kd)):
                o_ref[rows[ci], :] = val.astype(o_ref.dtype)
            o_refs[4][:, rows[ci], :] = qk.astype(BF16)
            o_refs[5][ci] = egl

    wide = pl.BlockSpec((CPS * CH, HW), lambda n: (n, 0))
    return pl.pallas_call(
        body, name="gdn_chunk_prep", grid=(NCK // CPS,),
        in_specs=[pl.BlockSpec((CPS * CH, 3 * HW), lambda n: (n, 0)), pl.BlockSpec((CPS * CH, LANE), lambda n: (n, 0))],
        out_specs=[wide, wide, wide, wide, pl.BlockSpec((HEADS, CPS * CH, CH), lambda n: (0, n, 0)),
                   pl.BlockSpec((CPS, 1, HW), lambda n: (n, 0, 0))],
        out_shape=_chunk_prep_shapes((F32, BF16, BF16, BF16, BF16, F32)),
        compiler_params=_cparams(("parallel",)),
    )(qkv, bg)


def _chunk_prep_bwd(qkv, bg, cots):
    def body(x_ref, bg_ref, du, dw, dqe, dkd, dqk, degl, dx_ref, dbg_ref):
        rows = [slice(ci * CH, (ci + 1) * CH) for ci in range(CPS)]
        _, vjp = jax.vjp(_chunk_prep_fn, [x_ref[r, :] for r in rows], [bg_ref[r, :] for r in rows])
        dxs, dbgs = vjp([(du[r, :], dw[r, :], dqe[r, :], dkd[r, :], dqk[:, r, :], degl[ci])
                         for ci, r in enumerate(rows)])
        for r, dx, dbg in zip(rows, dxs, dbgs):
            dx_ref[r, :] = dx
            dbg_ref[r, :] = dbg

    wide = pl.BlockSpec((CPS * CH, HW), lambda n: (n, 0))
    return pl.pallas_call(
        body, name="gdn_chunk_prep_bwd", grid=(NCK // CPS,),
        in_specs=[pl.BlockSpec((CPS * CH, 3 * HW), lambda n: (n, 0)), pl.BlockSpec((CPS * CH, LANE), lambda n: (n, 0)),
                  wide, wide, wide, wide, pl.BlockSpec((HEADS, CPS * CH, CH), lambda n: (0, n, 0)),
                  pl.BlockSpec((CPS, 1, HW), lambda n: (n, 0, 0))],
        out_specs=[pl.BlockSpec((CPS * CH, 3 * HW), lambda n: (n, 0)), pl.BlockSpec((CPS * CH, LANE), lambda n: (n, 0))],
        out_shape=[jax.ShapeDtypeStruct((T, 3 * HW), F32), jax.ShapeDtypeStruct((T, LANE), F32)],
        compiler_params=_cparams(("parallel",)),
    )(qkv, bg, *cots)


def _head_args(refs):
    u, w, qe, kd, qk, egl = refs
    sls = [slice(h * HD, (h + 1) * HD) for h in range(HEADS)]
    return ([u[:, sl] for sl in sls], [w[:, sl].astype(F32) for sl in sls], [qe[:, sl].astype(F32) for sl in sls],
            [kd[:, sl].astype(F32) for sl in sls], [qk[h].astype(F32) for h in range(HEADS)],
            [egl[:, sl] for sl in sls])


def _chunk_scan(prep):
    def body(*refs):
        o_ref, sh_ref, s_ref = refs[6:]

        @pl.when(pl.program_id(0) == 0)
        def _():
            s_ref[...] = jnp.zeros_like(s_ref)

        s = [s_ref[h] for h in range(HEADS)]
        for h in range(HEADS):
            sh_ref[h, 0] = s[h]
        o, s_new = _chunk_state_fn(*_head_args(refs[:6]), s)
        for h in range(HEADS):
            o_ref[:, h * HD:(h + 1) * HD] = o[h]
            s_ref[h] = s_new[h]

    return pl.pallas_call(
        body, name="gdn_scan", grid=(NCK,), in_specs=_chunk_prep_specs(),
        out_specs=[pl.BlockSpec((CH, HW), lambda n: (n, 0)), pl.BlockSpec((HEADS, 1, HD, HD), lambda n: (0, n, 0, 0))],
        out_shape=[jax.ShapeDtypeStruct((T, HW), F32), jax.ShapeDtypeStruct((HEADS, NCK, HD, HD), F32)],
        scratch_shapes=[pltpu.VMEM((HEADS, HD, HD), F32)], compiler_params=_cparams(("arbitrary",)),
    )(*prep)


def _chunk_scan_bwd(prep, s_hist, do, after=()):
    n_in = 8 + len(after)

    def body(*refs):
        sh_ref, do_ref = refs[6:8]
        d_refs = refs[n_in:n_in + 6]
        ds_ref = refs[n_in + 6]

        @pl.when(pl.program_id(0) == 0)
        def _():
            ds_ref[...] = jnp.zeros_like(ds_ref)

        sls = [slice(h * HD, (h + 1) * HD) for h in range(HEADS)]
        _, vjp = jax.vjp(_chunk_state_fn, *_head_args(refs[:6]), [sh_ref[h, 0] for h in range(HEADS)])
        du, dw, dqe, dkd, dqk, degl, ds = vjp(([do_ref[:, sl] for sl in sls], [ds_ref[h] for h in range(HEADS)]))
        for h, sl in enumerate(sls):
            for d_ref, val in zip(d_refs[:4], (du, dw, dqe, dkd)):
                d_ref[:, sl] = val[h]
            d_refs[4][h] = dqk[h]
            d_refs[5][:, sl] = degl[h]
            ds_ref[h] = ds[h]

    rev = lambda n: NCK - 1 - n
    return pl.pallas_call(
        body, name="gdn_scan_bwd", grid=(NCK,),
        in_specs=_chunk_prep_specs(rev=True) + [pl.BlockSpec((HEADS, 1, HD, HD), lambda n: (0, rev(n), 0, 0)),
                                                pl.BlockSpec((CH, HW), lambda n: (rev(n), 0))] + [ANY] * len(after),
        out_specs=_chunk_prep_specs(rev=True), out_shape=_chunk_prep_shapes((F32,) * 6),
        scratch_shapes=[pltpu.VMEM((HEADS, HD, HD), F32)], compiler_params=_cparams(("arbitrary",)),
    )(*prep, s_hist, do, *after)


def _mix_specs():
    pc = lambda c: pl.BlockSpec((GCH, HW), lambda i: (i, c))
    return [pl.BlockSpec((GCH, HW), lambda i: (i, 0)), pc(3), pc(4), pc(5), _whole((1, HD)), _whole((1, HW)),
            _whole((1, HW)), _whole((HEADS, GCH, GCH)), _whole((GCH, LANE))]


def _mix(o, proj, ong, lng, lnb, ws, bst):
    def body(o_ref, z_ref, u_ref, v_ref, ong_ref, lng_ref, lnb_ref, ws_ref, bs_ref, m_ref):
        m_ref[...] = _mix_fn(o_ref[...], z_ref[...], u_ref[...], v_ref[...], ong_ref[...], lng_ref[...],
                             lnb_ref[...], ws_ref[...], bs_ref[...]).astype(BF16)

    return pl.pallas_call(
        body, name="mix", grid=(T // GCH,), in_specs=_mix_specs(),
        out_specs=pl.BlockSpec((GCH, D), lambda i: (i, 0)), out_shape=jax.ShapeDtypeStruct((T, D), BF16),
        compiler_params=_cparams(("parallel",)),
    )(o, proj, proj, proj, ong, lng, lnb, ws, bst)


def _mix_bwd(o, proj, ong, lng, lnb, ws, bst, dmix):
    def body(o_ref, z_ref, u_ref, v_ref, ong_ref, lng_ref, lnb_ref, ws_ref, bs_ref, dm_ref,
             do_ref, dzuv_ref, dong_ref, dlng_ref, dlnb_ref, dws_ref, dbs_ref):
        _, vjp = jax.vjp(_mix_fn, o_ref[...], z_ref[...], u_ref[...], v_ref[...], ong_ref[...], lng_ref[...],
                         lnb_ref[...], ws_ref[...], bs_ref[...])
        do, dz, du, dv, dong, dlng, dlnb, dws, dbs = vjp(dm_ref[...])
        do_ref[...] = do
        dzuv_ref[:, 0:HW] = dz.astype(BF16)
        dzuv_ref[:, HW:2 * HW] = du.astype(BF16)
        dzuv_ref[:, 2 * HW:3 * HW] = dv.astype(BF16)
        acc = [(dong_ref, dong), (dlng_ref, dlng), (dlnb_ref, dlnb), (dws_ref, dws), (dbs_ref, dbs)]

        @pl.when(pl.program_id(0) == 0)
        def _():
            for r, val in acc:
                r[...] = val

        @pl.when(pl.program_id(0) > 0)
        def _():
            for r, val in acc:
                r[...] += val

    shp = lambda *s: jax.ShapeDtypeStruct(s, F32)
    return pl.pallas_call(
        body, name="mix_bwd", grid=(T // GCH,),
        in_specs=_mix_specs() + [pl.BlockSpec((GCH, D), lambda i: (i, 0))],
        out_specs=[pl.BlockSpec((GCH, HW), lambda i: (i, 0)), pl.BlockSpec((GCH, 3 * HW), lambda i: (i, 1)),
                   _whole((1, HD)), _whole((1, HW)), _whole((1, HW)), _whole((HEADS, GCH, GCH)), _whole((GCH, LANE))],
        out_shape=[shp(T, HW), jax.ShapeDtypeStruct((T, NP), BF16), shp(1, HD), shp(1, HW), shp(1, HW),
                   shp(HEADS, GCH, GCH), shp(GCH, LANE)],
        compiler_params=_cparams(("arbitrary",)),
    )(o, proj, proj, proj, ong, lng, lnb, ws, bst, dmix)


def _swiglu_epilogue(accs, _):
    gate, up = accs
    return [gate, up, _silu(gate) * up]


def _swiglu_bwd_epilogue(accs, extras):
    dact = accs[0]
    gate, up = (e.astype(F32) for e in extras)
    sg = _sigmoid(gate)
    return [dact * up * (sg * (1.0 + gate * (1.0 - sg))), dact * (gate * sg)]


def _layer_fwd(h, p):
    hn = _rmsnorm("rms_mix", h, p["norm_mix"])
    proj = _mm("in_proj", "nn", hn[None], [p["w_in"][None]], tm=1024, tn=640, tk=D)[0][0]
    qkv = _prep(proj, p["conv_w"])
    bg = _gates(proj, p["a_log"], p["dt_bias"])
    prep = _chunk_prep(qkv, bg)
    o, s_hist = _chunk_scan(prep)
    if "late" in p:
        p.update(p.pop("late")(o))
    mix = _mix(o, proj, p["o_norm_g"], p["ln_v_g"], p["ln_v_b"], p["w_s"], p["bst"])
    h1 = _mm("out_proj", "nn", mix[None], [p["w_out"]], tm=1024, tn=512, tk=D // NCHIP, resid=h[None], n_n=D,
             b_spec=((None, D // NCHIP, 512), lambda g, i, j, k: (k, 0, j)))[0][0]
    h2n = _rmsnorm("rms_ffn", h1, p["norm_ffn"])
    gate, up, act = _mm("ffn_in", "nt", h2n[None], [p["w_gate"], p["w_up"]], tm=1024, tn=FF_SH, tk=D,
                        out_dtypes=(BF16, BF16, BF16), epilogue=_swiglu_epilogue)
    h2 = _mm("ffn_out", "nn", act, [p["w_down"]], tm=1024, tn=512, tk=FF_SH, reduce_g=True, resid=h1[None])[0][0]
    saved = dict(h=h, hn=hn, proj=proj, qkv=qkv, bg=bg, prep=prep, o=o, s_hist=s_hist, mix=mix, h1=h1, h2n=h2n,
                 gate=gate, up=up, act=act)
    return h2, saved


def _layer_bwd_ffn(dh2, dh2b, p, s, after=()):
    dh2b = dh2b[None]
    dgate, dup = _mm("ffn_out_bwd", "nt", dh2b, [p["w_down"]], tm=1024, tn=FF_SH, tk=D, out_dtypes=(BF16, BF16),
                     extras=(s["gate"], s["up"]), epilogue=_swiglu_bwd_epilogue, after=after)
    dh2n = _mm("ffn_gate_bwd", "nn", dgate, [p["w_gate"]], tm=1024, tn=512, tk=FF_SH, reduce_g=True)[0]
    dh2n = _mm("ffn_up_bwd", "nn", dup, [p["w_up"]], tm=1024, tn=512, tk=FF_SH, reduce_g=True, resid=dh2n)[0][0]
    dh1, dh1b, d_norm_ffn = _rmsnorm_bwd("rms_ffn_bwd", dh2n, s["h1"], p["norm_ffn"], dh2)
    d_w_down = _mm("ffn_wdown_grad", "tn", s["act"], [dh2b], tm=FF_SH, tn=512, tk=T)[0]
    d_w_gate = _mm("ffn_wgate_grad", "tn", dgate, [s["h2n"][None]], tm=FF_SH, tn=512, tk=T)[0]
    d_w_up = _mm("ffn_wup_grad", "tn", dup, [s["h2n"][None]], tm=FF_SH, tn=512, tk=T)[0]
    return dh1, dh1b, dict(norm_ffn=d_norm_ffn, w_gate=d_w_gate, w_up=d_w_up, w_down=d_w_down)


def _layer_bwd_mixer(dh1, dh1b, p, s, after=(), midway=None, late=None):
    dh1b = dh1b[None]
    dmix = _mm("out_proj_bwd", "nt", dh1b, [p["w_out"]], tm=1024, tn=D // NCHIP, tk=D, n_n=D, after=after,
               b_spec=((None, D // NCHIP, D), lambda g, i, j, k: (j, 0, k)))[0][0]
    d_w_out = _mm("out_proj_wgrad", "tn", s["mix"][None], [dh1b], tm=512, tn=512, tk=T)[0][0]
    do, dproj, d_ong, d_lng, d_lnb, d_ws, d_bst = _mix_bwd(
        s["o"], s["proj"], p["o_norm_g"], p["ln_v_g"], p["ln_v_b"], p["w_s"], p["bst"], dmix)
    then = midway(do) if midway is not None else ()
    dqkv, dbg = _chunk_prep_bwd(s["qkv"], s["bg"], _chunk_scan_bwd(s["prep"], s["s_hist"], do, then))
    dproj, d_conv = _prep_bwd(s["proj"], p["conv_w"], dqkv, dproj)
    dproj, d_a_log, d_dt_bias = _gates_bwd(s["proj"], p["a_log"], p["dt_bias"], dbg, dproj)
    dproj = dproj[None]
    d_w_in = _mm("in_proj_wgrad", "tn", s["hn"][None], [dproj], tm=512, tn=640, tk=T)[0]
    last = late(dict(w_in=d_w_in, w_out=d_w_out)) if late is not None else ()
    dhn = _mm("in_proj_bwd", "nt", dproj, [p["w_in"][None]], tm=1024, tn=512, tk=NP, after=last)[0][0]
    dh, dhb, d_norm_mix = _rmsnorm_bwd("rms_mix_bwd", dhn, s["h"], p["norm_mix"], dh1)
    grads = dict(norm_mix=d_norm_mix, w_in=d_w_in, conv_w=d_conv, a_log=d_a_log, dt_bias=d_dt_bias, o_norm_g=d_ong,
                 ln_v_g=d_lng, ln_v_b=d_lnb, w_s=d_ws, bst=d_bst, w_out=d_w_out)
    return dh, dhb, grads


def _lanes(v, off=0):
    return jnp.zeros((1, LANE), F32).at[0, off:off + v.shape[0]].set(v)


def _w_in_pieces():
    regions = [(0, 2048, 0), (2048, 2056, BA_OFF), (2056, IN_DIM, 2048)]
    sh = IN_DIM // NCHIP
    out = []
    for j in range(NCHIP):
        for lo, hi, at in regions:
            a, b = max(lo, j * sh), min(hi, (j + 1) * sh)
            if a < b:
                out.append((j, a - j * sh, at + a - lo, b - a))
    return out


W_IN_PIECES = _w_in_pieces()
WT = 256


def _assemble_w_in(gathered, own, place):
    def body(place_ref, g_ref, own_ref, o_ref):
        o_ref[:, IN_DIM:] = jnp.zeros((WT, NP - IN_DIM), BF16)
        mine = own_ref[...]
        for j, src, dst, width in W_IN_PIECES:
            val = jnp.where(place_ref[0] == j, mine[:, src:src + width], g_ref[j, :, src:src + width])
            o_ref[:, dst:dst + width] = val

    sh = IN_DIM // NCHIP
    return pl.pallas_call(
        body, name="assemble_w_in",
        grid_spec=pltpu.PrefetchScalarGridSpec(
            num_scalar_prefetch=1, grid=(D // WT,),
            in_specs=[pl.BlockSpec((NCHIP, WT, sh), lambda i, place_ref: (0, i, 0)),
                      pl.BlockSpec((WT, sh), lambda i, place_ref: (i, 0))],
            out_specs=pl.BlockSpec((WT, NP), lambda i, place_ref: (i, 0))),
        out_shape=jax.ShapeDtypeStruct((D, NP), BF16), compiler_params=_cparams(("parallel",)),
    )(place, gathered, own)


def _layer_params(l, big, small):
    return dict(
        {k: v for k, v in big.items() if k != "conv_w"},
        conv_w=jnp.concatenate([big["conv_w"][j, l] for j in range(NCHIP)], axis=1),
        norm_mix=small["norm_mix"][l][None], norm_ffn=small["norm_ffn"][l][None],
        a_log=_lanes(small["a_log"][l], HEADS), dt_bias=_lanes(small["dt_bias"][l], HEADS),
        o_norm_g=small["o_norm_g"][l][None], ln_v_g=small["ln_v_g"][l][None], ln_v_b=small["ln_v_b"][l][None],
        w_s=small["w_s"][l],
        bst=jnp.pad(small["b_s"][l].T, ((0, 0), (0, LANE - HEADS))),
    )


def _reference_layout(g):
    return dict(
        w_in=g["w_in"],
        w_out=g["w_out"].reshape(NCHIP, D // NCHIP, D),
        w_gate=g["w_gate"], w_up=g["w_up"], w_down=g["w_down"],
        conv_w=g["conv_w"], norm_mix=g["norm_mix"][0], norm_ffn=g["norm_ffn"][0],
        a_log=g["a_log"][0, HEADS:2 * HEADS], dt_bias=g["dt_bias"][0, HEADS:2 * HEADS],
        o_norm_g=g["o_norm_g"][0], ln_v_g=g["ln_v_g"][0], ln_v_b=g["ln_v_b"][0], w_s=g["w_s"],
        b_s=g["bst"][:, :HEADS].T,
    )


def _forward(x, tgt, layers, norm_final):
    h = x
    saved, params = [], []
    for p in layers:
        p = p(h) if callable(p) else p
        h, s = _layer_fwd(h, p)
        saved.append(s)
        params.append(p)
    return (saved, params) + tuple(_loss_head(h, norm_final, tgt))


def _local_step(x, tgt, layers, norm_final):
    saved, layers, loss, dh, dhb, d_norm_final = _forward(x, tgt, layers, norm_final)
    grads = [None] * DEPTH
    for l in reversed(range(DEPTH)):
        dh1, dh1b, g_ffn = _layer_bwd_ffn(dh, dhb, layers[l], saved[l])
        dh, dhb, g_mix = _layer_bwd_mixer(dh1, dh1b, layers[l], saved[l])
        grads[l] = {**g_ffn, **g_mix}
    return loss, dh, grads, d_norm_final


def _place():
    x, y, c = lax.axis_index("x"), lax.axis_index("y"), lax.axis_index("c")
    return x, y, c, [(1 - x, y), (x, 1 - y), (1 - x, 1 - y)]


def _remote(src, dst, send_sem, recv_sem, to):
    return pltpu.make_async_remote_copy(src_ref=src, dst_ref=dst, send_sem=send_sem, recv_sem=recv_sem,
                                        device_id=to, device_id_type=MESH)


def _comm_call(name, body, ins, out_shape, n_sems, aliases=None):
    return pl.pallas_call(
        body, name=name, in_specs=[ANY] * len(ins), out_specs=[ANY] * len(out_shape), out_shape=out_shape,
        scratch_shapes=[pltpu.SemaphoreType.DMA((n,)) for n in n_sems], input_output_aliases=aliases or {},
        compiler_params=pltpu.CompilerParams(has_side_effects=True),
    )(*ins)


def _half_rows(ref, of_c, dim):
    hr = ref.shape[dim] // 2
    return pl.ds(pl.multiple_of(of_c * hr, BF16_ROWS), hr)


def _gather_plan(whole):
    def plan(srcs, lands):
        x, y, c, others = _place()
        chip = 2 * x + y
        out = []
        for src, land, all_of_it in zip(srcs, lands, whole):
            for ox, oy in others:
                if all_of_it:
                    out.append((src, land.at[chip], (ox, oy, c)))
                else:
                    out.append((src.at[_half_rows(src, c, 0)], land.at[chip, _half_rows(src, c, 0)], (ox, oy, c)))
        return out
    return plan


def _forward_halves(lands):
    n = len(lands)

    def body(*refs):
        outs = refs[n:2 * n]
        send_s, recv_s = refs[2 * n:]
        x, y, c, others = _place()
        sibling = (x, y, 1 - c)
        copies = []
        for a in range(n):
            for k, (ox, oy) in enumerate(others):
                mine = outs[a].at[2 * ox + oy, _half_rows(outs[a], c, 1)]
                copies.append(_remote(mine, mine, send_s.at[3 * a + k], recv_s.at[3 * a + k], sibling))
        for cp in copies:
            cp.start()
        for a in range(n):
            for k, (ox, oy) in enumerate(others):
                landed = outs[a].at[2 * ox + oy, _half_rows(outs[a], 1 - c, 1)]
                _remote(landed, landed, send_s.at[3 * a + k], recv_s.at[3 * a + k], sibling).wait_recv()
        for cp in copies:
            cp.wait_send()

    out_shape = [jax.ShapeDtypeStruct(g.shape, g.dtype) for g in lands]
    return _comm_call("forward_halves", body, lands, out_shape, [3 * n, 3 * n], aliases={a: a for a in range(n)})


HBM_SPEC = pl.BlockSpec(memory_space=pltpu.HBM)
SEM_SPEC = pl.BlockSpec(memory_space=pltpu.SEMAPHORE)
DATAFLOW = pltpu.SideEffectType.DATAFLOW_SIDE_EFFECTING


def _exchange_plan(srcs, lands):
    x, y, c, _ = _place()
    plan = []
    for src, land in zip(srcs, lands):
        hr = src.shape[1] // 2
        plan.append((src.at[:, pl.ds(pl.multiple_of((1 - c) * hr, 8), hr)], land, (x, y, 1 - c)))
    return plan


def _scatter_plan(srcs, lands):
    x, y, c, others = _place()
    return [(src.at[2 * ox + oy], land.at[k], (ox, oy, c))
            for src, land in zip(srcs, lands) for k, (ox, oy) in enumerate(others)]


def _split_start(name, plan, srcs, land_shapes, n_copies, after=()):
    n = len(srcs)
    lands = [pltpu.with_memory_space_constraint(lax.empty(s.shape, s.dtype), pltpu.HBM) for s in land_shapes]
    srcs = [pltpu.with_memory_space_constraint(s, pltpu.HBM) for s in srcs]

    def body(*refs):
        send_s, recv_s = refs[2 * n + len(after)], refs[2 * n + len(after) + 1]
        for i, (src, dst, to) in enumerate(plan(refs[:n], refs[n:2 * n])):
            _remote(src, dst, send_s.at[i], recv_s.at[i], to).start()
        refs[-1][...] = jnp.zeros_like(refs[-1])

    thru = [pltpu.HBM(s.shape, s.dtype) for s in srcs + lands]
    out = pl.pallas_call(
        body, name=name, in_specs=[HBM_SPEC] * (2 * n) + [ANY] * len(after),
        out_specs=[SEM_SPEC, SEM_SPEC] + [HBM_SPEC] * (2 * n) + [pl.BlockSpec(memory_space=pltpu.VMEM)],
        out_shape=[pltpu.SemaphoreType.DMA((n_copies,)), pltpu.SemaphoreType.DMA((n_copies,))] + thru
        + [jax.ShapeDtypeStruct((F32_ROWS, LANE), F32)],
        input_output_aliases={i: 2 + i for i in range(2 * n)},
        compiler_params=pltpu.CompilerParams(has_side_effects=DATAFLOW),
    )(*srcs, *lands, *after)
    return dict(sems=out[:2], srcs=out[2:2 + n], lands=out[2 + n:2 + 2 * n], token=out[-1])


def _split_wait(name, plan, started, after):
    n = len(started["srcs"])
    after = list(after) if isinstance(after, (list, tuple)) else [after]

    def body(*refs):
        send_s, recv_s = refs[2 * n], refs[2 * n + 1]
        for i, (src, dst, to) in enumerate(plan(refs[:n], refs[n:2 * n])):
            cp = _remote(src, dst, send_s.at[i], recv_s.at[i], to)
            cp.wait_send()
            cp.wait_recv()

    arrs = list(started["srcs"]) + list(started["lands"])
    out = pl.pallas_call(
        body, name=name, in_specs=[HBM_SPEC] * (2 * n) + [SEM_SPEC, SEM_SPEC] + [ANY] * len(after),
        out_specs=[HBM_SPEC] * (2 * n), out_shape=[pltpu.HBM(s.shape, s.dtype) for s in arrs],
        input_output_aliases={i: i for i in range(2 * n)},
        compiler_params=pltpu.CompilerParams(has_side_effects=DATAFLOW),
    )(*arrs, *started["sems"], *after)
    return out[:n], out[n:]


def _join_halves(name, rs):
    n = len(rs)

    def body(*refs):
        outs = refs[n:2 * n]
        send_s, recv_s = refs[2 * n:]
        x, y, c, _ = _place()
        sibling = (x, y, 1 - c)

        def half(a, of_c):
            hr = outs[a].shape[1] // 2
            return outs[a].at[:, pl.ds(pl.multiple_of(of_c * hr, 8), hr)]

        copies = [_remote(half(a, c), half(a, c), send_s.at[a], recv_s.at[a], sibling) for a in range(n)]
        for cp in copies:
            cp.start()
        for a in range(n):
            landed = half(a, 1 - c)
            _remote(landed, landed, send_s.at[a], recv_s.at[a], sibling).wait_recv()
        for cp in copies:
            cp.wait_send()

    out_shape = [jax.ShapeDtypeStruct(r.shape, r.dtype) for r in rs]
    return _comm_call(name, body, rs, out_shape, [n, n], aliases={a: a for a in range(n)})


def _allreduce_small(buf, after=()):
    r = buf.shape[0]
    hr = r // 2

    def body(in_ref, *refs):
        out_ref, theirs, by_chip, send_s, recv_s = refs[len(after):]
        x, y, c, others = _place()
        chip = 2 * x + y
        sibling = (x, y, 1 - c)
        mine = pl.ds(pl.multiple_of(c * hr, F32_ROWS), hr)
        swap = _remote(in_ref, theirs, send_s.at[0], recv_s.at[0], sibling)
        swap.start()
        swap.wait()
        by_chip[chip] = in_ref[mine, :] + theirs[mine, :]
        sends = [_remote(by_chip.at[chip], by_chip.at[chip], send_s.at[1 + k], recv_s.at[1 + k], (ox, oy, c))
                 for k, (ox, oy) in enumerate(others)]
        for cp in sends:
            cp.start()
        for k, (ox, oy) in enumerate(others):
            landed = by_chip.at[2 * ox + oy]
            _remote(landed, landed, send_s.at[1 + k], recv_s.at[1 + k], (ox, oy, c)).wait_recv()
        for cp in sends:
            cp.wait_send()
        out_ref[mine, :] = (by_chip[0] + by_chip[1]) + (by_chip[2] + by_chip[3])
        back = _remote(out_ref.at[mine], out_ref.at[mine], send_s.at[NCHIP], recv_s.at[NCHIP], sibling)
        back.start()
        other = out_ref.at[pl.ds(pl.multiple_of((1 - c) * hr, F32_ROWS), hr)]
        _remote(other, other, send_s.at[NCHIP], recv_s.at[NCHIP], sibling).wait_recv()
        back.wait_send()

    vm = pl.BlockSpec(memory_space=pltpu.VMEM)
    return pl.pallas_call(
        body, name="allreduce_small", in_specs=[vm] + [ANY] * len(after), out_specs=vm,
        out_shape=jax.ShapeDtypeStruct((r, LANE), F32),
        scratch_shapes=[pltpu.VMEM((r, LANE), F32), pltpu.VMEM((NCHIP, hr, LANE), F32),
                        pltpu.SemaphoreType.DMA((NCHIP + 1,)), pltpu.SemaphoreType.DMA((NCHIP + 1,))],
        compiler_params=pltpu.CompilerParams(has_side_effects=True, vmem_limit_bytes=VMEM_LIMIT),
    )(buf, *after)


MAX_ROW_TILE = 512
BF16_ROWS = 16


def _row_tile(rows):
    for t in range(min(rows, MAX_ROW_TILE) // BF16_ROWS * BF16_ROWS, 0, -BF16_ROWS):
        if rows % t == 0:
            return t
    raise ValueError(rows)


def _sum_halves(g, theirs, c_arr):
    nch, rows, cols = g.shape
    hr = rows // 2
    tr = _row_tile(hr)

    def body(c_ref, g_ref, t_ref, o_ref, ob_ref):
        s = g_ref[...] + t_ref[...]
        o_ref[...] = s
        ob_ref[...] = s.astype(BF16)

    blk = pl.BlockSpec((None, tr, cols), lambda j, i, c_ref: (j, i, 0))
    return pl.pallas_call(
        body, name="sum_halves",
        grid_spec=pltpu.PrefetchScalarGridSpec(
            num_scalar_prefetch=1, grid=(nch, hr // tr),
            in_specs=[pl.BlockSpec((None, None, tr, cols), lambda j, i, c_ref: (j, c_ref[0], i, 0)), blk],
            out_specs=[blk, blk]),
        out_shape=[jax.ShapeDtypeStruct((nch, hr, cols), F32), jax.ShapeDtypeStruct((nch, hr, cols), BF16)],
        compiler_params=_cparams(("parallel", "parallel")),
    )(c_arr, g.reshape(nch, 2, hr, cols), theirs)


def _sum_halves_w_in(g, theirs, c_arr):
    hr = D // 2
    sh = IN_DIM // NCHIP

    def body(c_ref, g_ref, t_ref, o_ref, ob_ref):
        s = g_ref[...] + t_ref[...]
        for j, dst, src, width in W_IN_PIECES:
            o_ref[j, :, dst:dst + width] = s[:, src:src + width]
            ob_ref[j, :, dst:dst + width] = s[:, src:src + width].astype(BF16)

    out = pl.BlockSpec((NCHIP, WT, sh), lambda i, c_ref: (0, i, 0))
    return pl.pallas_call(
        body, name="sum_halves_w_in",
        grid_spec=pltpu.PrefetchScalarGridSpec(
            num_scalar_prefetch=1, grid=(hr // WT,),
            in_specs=[pl.BlockSpec((None, WT, NP), lambda i, c_ref: (c_ref[0], i, 0)),
                      pl.BlockSpec((None, WT, NP), lambda i, c_ref: (0, i, 0))],
            out_specs=[out, out]),
        out_shape=[jax.ShapeDtypeStruct((NCHIP, hr, sh), F32), jax.ShapeDtypeStruct((NCHIP, hr, sh), BF16)],
        compiler_params=_cparams(("parallel",)),
    )(c_arr, g.reshape(2, hr, NP), theirs)


def _sum_chips(p, q, place, l, into=None, after=()):
    extra = ([into] if into is not None else []) + list(after)
    _, rows, cols = p.shape
    tr = _row_tile(rows)
    steps = rows // tr

    def body(place_ref, p_ref, q0, q1, q2, *rest):
        rest[-1][...] = ((p_ref[...] + q0[...].astype(F32)) + q1[...].astype(F32)) + q2[...].astype(F32)

    qs = lambda k: pl.BlockSpec((None, tr, cols), lambda i, place_ref: (k, i, 0))
    return pl.pallas_call(
        body, name="sum_chips",
        grid_spec=pltpu.PrefetchScalarGridSpec(
            num_scalar_prefetch=1, grid=(steps,),
            in_specs=[pl.BlockSpec((None, tr, cols), lambda i, place_ref: (place_ref[0], i, 0)), qs(0), qs(1), qs(2)]
            + [ANY] * len(extra),
            out_specs=pl.BlockSpec((None, tr, cols), lambda i, place_ref: (l, place_ref[1] * steps + i, 0))),
        out_shape=jax.ShapeDtypeStruct((DEPTH, 2 * rows, cols), F32),
        input_output_aliases={5: 0} if into is not None else {},
        compiler_params=_cparams(("parallel",)),
    )(place, p, q, q, q, *extra)


def _adamw(w, g, m, v):
    layers, rows, cols = w.shape
    tr = _row_tile(rows)

    def body(w_ref, g_ref, m_ref, v_ref, d_ref, nm_ref, nv_ref):
        gv = g_ref[...]
        nm = ADAM_B1 * m_ref[...] + (1.0 - ADAM_B1) * gv
        nv = ADAM_B2 * v_ref[...] + (1.0 - ADAM_B2) * jnp.square(gv)
        m_hat = nm / (1.0 - ADAM_B1 ** ADAM_STEP)
        v_hat = nv / (1.0 - ADAM_B2 ** ADAM_STEP)
        d_ref[...] = -ADAM_LR * (m_hat / (jnp.sqrt(v_hat) + ADAM_EPS) + ADAM_WD * w_ref[...])
        nm_ref[...] = nm
        nv_ref[...] = nv

    blk = pl.BlockSpec((None, tr, cols), lambda l, i: (l, i, 0))
    return pl.pallas_call(
        body, name="adamw", grid=(layers, rows // tr), in_specs=[blk] * 4, out_specs=[blk] * 3,
        out_shape=[jax.ShapeDtypeStruct(w.shape, F32)] * 3, compiler_params=_cparams(("parallel", "parallel")),
    )(w, g, m, v)


BIG = ("w_in", "w_out", "w_gate", "w_up", "w_down")
SMALL = ("norm_mix", "a_log", "dt_bias", "o_norm_g", "ln_v_g", "ln_v_b", "w_s", "b_s", "norm_ffn", "norm_final")
ORDER = ("norm_mix", "w_in", "conv_w", "a_log", "dt_bias", "o_norm_g", "ln_v_g", "ln_v_b", "w_s", "b_s", "w_out",
         "norm_ffn", "w_gate", "w_up", "w_down", "norm_final")


F32_ROWS = 8
PACK_ROWS = 128


def _lane_rows(size):
    return -(-size // (F32_ROWS * LANE)) * F32_ROWS


def _pack(arrs):
    parts = [jnp.pad(a.reshape(-1), (0, _lane_rows(a.size) * LANE - a.size)).reshape(-1, LANE) for a in arrs]
    rows = sum(p.shape[0] for p in parts)
    if rows % PACK_ROWS:
        parts.append(jnp.zeros((-rows % PACK_ROWS, LANE), F32))
    return jnp.concatenate(parts, axis=0)


def _unpack(buf, like):
    out, row = [], 0
    for a in like:
        n = _lane_rows(a.size)
        out.append(buf[row:row + n].reshape(-1)[:a.size].reshape(a.shape))
        row += n
    return out


def kernel(x, norm_mix, w_in, conv_w, a_log, dt_bias, o_norm_g, ln_v_g, ln_v_b, w_s, b_s, w_out, norm_ffn, w_gate, w_up, w_down, norm_final, loss_target, m_norm_mix, m_w_in, m_conv_w, m_a_log, m_dt_bias, m_o_norm_g, m_ln_v_g, m_ln_v_b, m_w_s, m_b_s, m_w_out, m_norm_ffn, m_w_gate, m_w_up, m_w_down, m_norm_final, v_norm_mix, v_w_in, v_conv_w, v_a_log, v_dt_bias, v_o_norm_g, v_ln_v_g, v_ln_v_b, v_w_s, v_b_s, v_w_out, v_norm_ffn, v_w_gate, v_w_up, v_w_down, v_norm_final):
    w = dict(norm_mix=norm_mix, w_in=w_in, conv_w=conv_w, a_log=a_log, dt_bias=dt_bias, o_norm_g=o_norm_g,
             ln_v_g=ln_v_g, ln_v_b=ln_v_b, w_s=w_s, b_s=b_s, w_out=w_out, norm_ffn=norm_ffn, w_gate=w_gate, w_up=w_up,
             w_down=w_down, norm_final=norm_final)
    m = dict(norm_mix=m_norm_mix, w_in=m_w_in, conv_w=m_conv_w, a_log=m_a_log, dt_bias=m_dt_bias, o_norm_g=m_o_norm_g,
             ln_v_g=m_ln_v_g, ln_v_b=m_ln_v_b, w_s=m_w_s, b_s=m_b_s, w_out=m_w_out, norm_ffn=m_norm_ffn,
             w_gate=m_w_gate, w_up=m_w_up, w_down=m_w_down, norm_final=m_norm_final)
    v = dict(norm_mix=v_norm_mix, w_in=v_w_in, conv_w=v_conv_w, a_log=v_a_log, dt_bias=v_dt_bias, o_norm_g=v_o_norm_g,
             ln_v_g=v_ln_v_g, ln_v_b=v_ln_v_b, w_s=v_w_s, b_s=v_b_s, w_out=v_w_out, norm_ffn=v_norm_ffn,
             w_gate=v_w_gate, w_up=v_w_up, w_down=v_w_down, norm_final=v_norm_final)
    chip = 2 * lax.axis_index("x") + lax.axis_index("y")
    place = jnp.stack([chip, lax.axis_index("c")]).astype(jnp.int32)
    c_arr = place[1:]

    def kernel_view(n, a):
        return jnp.swapaxes(a, 1, 2) if n in ("w_gate", "w_up") else a

    own = {n: [kernel_view(n, w[n])[l].astype(BF16) for l in range(DEPTH)] for n in BIG}
    by_chip = lambda a: jax.ShapeDtypeStruct((NCHIP,) + a.shape, a.dtype)

    def start(name, srcs, whole, after=()):
        return _split_start(name, _gather_plan(whole), srcs, [by_chip(a) for a in srcs], 3 * len(srcs), after)

    def finish(name, started, whole, after):
        srcs, lands = _split_wait(name, _gather_plan(whole), started, after)
        passed = iter(_forward_halves([g for g, all_of_it in zip(lands, whole) if not all_of_it]))
        lands = [g if all_of_it else next(passed) for g, all_of_it in zip(lands, whole)]
        return srcs, [lax.dynamic_update_index_in_dim(g, o, chip, 0) for g, o in zip(lands, srcs)]

    ffn = BIG[1:]
    first = start("gather_first_start", [own["w_in"][0], conv_w], [False, True])
    early = start("gather_early_start", [own[n][0] for n in ffn], [False] * len(ffn), [first["token"]])
    later = start("gather_later_start", [own[n][1] for n in BIG], [False] * len(BIG), [early["token"]])
    (own_w_in, _), (w_in_by_chip, conv_by_chip) = finish("gather_first_wait", first, [False, True], later["token"])

    def late(after):
        return dict(zip(ffn, finish("gather_early_wait", early, [False] * len(ffn), after)[1]))

    layer0 = _layer_params(0, dict(w_in=_assemble_w_in(w_in_by_chip, own_w_in, place), conv_w=conv_by_chip, late=late), w)

    def layer1(after):
        srcs, by = finish("gather_later_wait", later, [False] * len(BIG), after)
        big = dict(zip(ffn, by[1:]), w_in=_assemble_w_in(by[0], srcs[0], place), conv_w=conv_by_chip)
        return _layer_params(1, big, w)

    saved, layers, loss_lanes, dh, dhb, d_norm_final = _forward(x[0], loss_target[0], [layer0, layer1],
                                                                 norm_final[None])
    loss = lax.psum(loss_lanes[0, 0], ("x", "y", "c"))

    sums, arrived = {}, {}

    def exchange_start(tag, l, names, grads, after=()):
        mine = [grads[n] for n in names]
        shapes = [jax.ShapeDtypeStruct((g.shape[0], g.shape[1] // 2, g.shape[2]), F32) for g in mine]
        return tag, l, names, _split_start(f"exchange_{tag}_start", _exchange_plan, mine, shapes, len(mine), after)

    def add_halves(l, names, mine, theirs):
        for n, g, t in zip(names, mine, theirs):
            sums[l, n] = (_sum_halves_w_in if n == "w_in" else _sum_halves)(g, t, c_arr)

    def exchange_wait(handle, after):
        tag, l, names, started = handle
        add_halves(l, names, *_split_wait(f"exchange_{tag}_wait", _exchange_plan, started, after))

    def scatter_start(tag, l, names, after=()):
        partial = [sums[l, n][1] for n in names]
        shapes = [jax.ShapeDtypeStruct((3,) + p.shape[1:], p.dtype) for p in partial]
        return tag, l, names, _split_start(f"scatter_{tag}_start", _scatter_plan, partial, shapes, 3 * len(names), after)

    def scatter_wait(handle, after):
        tag, l, names, started = handle
        for n, q in zip(names, _split_wait(f"scatter_{tag}_wait", _scatter_plan, started, after)[1]):
            arrived[l, n] = q

    last = DEPTH - 1
    swiglu = BIG[2:]
    dh1, dh1b, g_ffn = _layer_bwd_ffn(dh, dhb, layers[last], saved[last])
    dh, dhb, g_mix = _layer_bwd_mixer(dh1, dh1b, layers[last], saved[last])
    gl = [None, _reference_layout({**g_ffn, **g_mix})]
    ex_last = exchange_start("last", last, BIG, gl[last])
    dh1, dh1b, g_ffn = _layer_bwd_ffn(dh, dhb, layers[0], saved[0], after=[ex_last[-1]["token"]])
    exchange_wait(ex_last, dh1)
    sc_last = scatter_start("last", last, BIG)
    ex_ffn = exchange_start("swiglu", 0, swiglu, g_ffn, [sc_last[-1]["token"]])
    sc_ffn = []

    def midway(do):
        exchange_wait(ex_ffn, do)
        sc_ffn.append(scatter_start("swiglu", 0, swiglu))
        return [sc_ffn[0][-1]["token"]]

    ex_rest = []

    def late(grads):
        rest_grads = dict(w_in=grads["w_in"], w_out=grads["w_out"].reshape(NCHIP, D // NCHIP, D))
        ex_rest.append(exchange_start("rest", 0, BIG[:2], rest_grads))
        return [ex_rest[0][-1]["token"]]

    dx, _, g_mix = _layer_bwd_mixer(dh1, dh1b, layers[0], saved[0], after=[ex_ffn[-1]["token"]], midway=midway,
                                    late=late)
    scatter_wait(sc_last, dx)
    scatter_wait(sc_ffn[0], dx)
    exchange_wait(ex_rest[0], dx)
    sc_rest = scatter_start("rest", 0, BIG[:2])
    gl[0] = _reference_layout({**g_ffn, **g_mix})

    travelling = [sc_rest[-1]["token"]]
    reduced, g_out, delta, new_m, new_v = {}, {}, {}, {}, {}

    def adamw_large(names, joined):
        for n, g in zip(names, joined):
            res = _adamw(kernel_view(n, w[n]), g, kernel_view(n, m[n]), kernel_view(n, v[n]))
            g_out[n], delta[n], new_m[n], new_v[n] = (kernel_view(n, a) for a in (g,) + tuple(res))

    for n in BIG:
        for l in (range(DEPTH) if n in swiglu else [last]):
            reduced[n] = _sum_chips(sums[l, n][0], arrived[l, n], place, l, into=reduced.get(n), after=travelling)
    adamw_large(swiglu, _join_halves("join_swiglu", [reduced[n] for n in swiglu]))
    small_g = [jnp.stack([gl[l][n] for l in range(DEPTH)]) for n in SMALL[:-1]] + [d_norm_final[0]]
    conv_g = jnp.stack([gl[l]["conv_w"] for l in range(DEPTH)])
    total = _allreduce_small(_pack(small_g + [conv_g]), travelling)
    scatter_wait(sc_rest, [total, new_v[swiglu[-1]]] + [reduced[n] for n in BIG[:2]])
    for n in BIG[:2]:
        reduced[n] = _sum_chips(sums[0, n][0], arrived[0, n], place, 0, into=reduced[n])
    adamw_large(BIG[:2], _join_halves("join_rest", [reduced[n] for n in BIG[:2]]))
    *small_r, conv_r = _unpack(total, small_g + [conv_g])
    g_out.update(zip(SMALL, small_r))
    g_out["conv_w"] = lax.dynamic_slice_in_dim(conv_r, chip * conv_w.shape[2], conv_w.shape[2], axis=2)

    rest = SMALL + ("conv_w",)
    like = [w[n] for n in rest]
    d, nm, nv = _adamw(*[_pack([src[n] for n in rest])[None] for src in (w, g_out, m, v)])
    for dst, buf in ((delta, d), (new_m, nm), (new_v, nv)):
        dst.update(zip(rest, _unpack(buf[0], like)))

    return (loss, dx[None], *[g_out[n] for n in ORDER], *[delta[n] for n in ORDER], *[new_m[n] for n in ORDER],
            *[new_v[n] for n in ORDER])
```

```python
import functools

import jax
import jax.numpy as jnp
from jax import lax
from jax.experimental import pallas as pl
from jax.experimental.pallas import tpu as pltpu

F32 = jnp.float32
BF16 = jnp.bfloat16
MESH = pl.DeviceIdType.MESH
ANY = pl.BlockSpec(memory_space=pl.ANY)
HIGHEST = lax.Precision.HIGHEST

T = 2048
D = 1024
DEPTH = 2
NCHIP = 4
HEADS = 4
HD = 128
HW = HEADS * HD
CH = 64
GCH = 128
IN_DIM = 3080
NP = 3200
BA_OFF = 3072
FF_SH = 704
EPS = 1e-6
LANE = 128
VMEM_LIMIT = 56 * 1024 * 1024

ADAM_LR = 0.001
ADAM_B1 = 0.9
ADAM_B2 = 0.999
ADAM_EPS = 1e-08
ADAM_WD = 0.01
ADAM_STEP = 10


def _cparams(sem=None):
    return pltpu.CompilerParams(dimension_semantics=sem, vmem_limit_bytes=VMEM_LIMIT)


_DIMS = {"nn": (((1,), (0,)), ((), ())), "nt": (((1,), (1,)), ((), ())), "tn": (((0,), (0,)), ((), ()))}


def _mm(name, mode, a, bs, *, tm, tn, tk, out_dtypes=(F32,), reduce_g=False, resid=None, extras=(), epilogue=None,
        b_spec=None, n_n=None, after=(), fold_g=False):
    nb = len(bs)
    ga = a.shape[0]
    gbs = [1 if b_spec is not None else b.shape[0] for b in bs]
    g_n = max([ga] + gbs)
    if mode == "tn":
        k_n, m_n = a.shape[1:]
    else:
        m_n, k_n = a.shape[1:]
    if n_n is None:
        n_n = bs[0].shape[1] if mode == "nt" else bs[0].shape[2]
    assert m_n % tm == 0 and n_n % tn == 0 and k_n % tk == 0, (name, m_n, n_n, k_n)
    mi, nj, kk = m_n // tm, n_n // tn, k_n // tk
    lead = g_n if fold_g else None
    if reduce_g:
        g_steps = 1 if fold_g else g_n
        grid = (mi, nj, g_steps, kk)
        ids = lambda i, j, g, k: (g, i, j, k)
        n_red = g_steps * kk
        red_idx = lambda: pl.program_id(2) * kk + pl.program_id(3)
        sem = ("parallel", "parallel", "arbitrary", "arbitrary")
    else:
        grid = (g_n, mi, nj, kk)
        ids = lambda g, i, j, k: (g, i, j, k)
        n_red = kk
        red_idx = lambda: pl.program_id(3)
        sem = ("parallel", "parallel", "parallel", "arbitrary")

    def pick(gsz, g):
        return g if gsz > 1 else 0

    def a_map(*p):
        g, i, j, k = ids(*p)
        return (pick(ga, g), k, i) if mode == "tn" else (pick(ga, g), i, k)

    def b_map(gsz):
        def f(*p):
            g, i, j, k = ids(*p)
            if b_spec is not None:
                return b_spec[1](g, i, j, k)
            return (pick(gsz, g), j, k) if mode == "nt" else (pick(gsz, g), k, j)
        return f

    def o_map(gsz):
        def f(*p):
            g, i, j, k = ids(*p)
            return (0 if reduce_g else pick(gsz, g), i, j)
        return f

    a_spec = pl.BlockSpec((lead, tk, tm) if mode == "tn" else (lead, tm, tk), a_map)
    b_block = b_spec[0] if b_spec is not None else ((lead, tn, tk) if mode == "nt" else (lead, tk, tn))
    b_specs = [pl.BlockSpec(b_block, b_map(gs)) for gs in gbs]
    x_specs = [pl.BlockSpec((None, tm, tn), o_map(e.shape[0])) for e in extras]
    r_specs = [pl.BlockSpec((None, tm, tn), o_map(resid.shape[0]))] if resid is not None else []
    g_out = 1 if reduce_g else g_n
    out_shape = [jax.ShapeDtypeStruct((g_out, m_n, n_n), dt) for dt in out_dtypes]
    out_specs = [pl.BlockSpec((None, tm, tn), o_map(g_out)) for _ in out_dtypes]
    nx, nr, no = len(extras), len(r_specs), len(out_dtypes)
    n_in = 1 + nb + nx + nr + len(after)
    dims = _DIMS[mode]

    def body(*refs):
        a_ref = refs[0]
        b_refs = refs[1:1 + nb]
        x_refs = refs[1 + nb:1 + nb + nx]
        r_refs = refs[1 + nb + nx:1 + nb + nx + nr]
        o_refs = refs[n_in:n_in + no]
        acc_refs = refs[n_in + no:]
        if fold_g:
            products = [sum(lax.dot_general(a_ref[g], b_ref[g], dims, preferred_element_type=F32) for g in range(g_n))
                        for b_ref in b_refs]
        else:
            av = a_ref[...]
            products = [lax.dot_general(av, b_ref[...], dims, preferred_element_type=F32) for b_ref in b_refs]

        def finish(accs):
            if r_refs:
                accs[0] = accs[0] + r_refs[0][...]
            outs = epilogue(accs, [x[...] for x in x_refs]) if epilogue is not None else accs
            for o_ref, o in zip(o_refs, outs):
                o_ref[...] = o.astype(o_ref.dtype)

        if n_red == 1:
            finish(products)
            return
        r = red_idx()
        for p, acc in zip(products, acc_refs):
            @pl.when(r == 0)
            def _():
                acc[...] = p

            @pl.when((r > 0) & (r < n_red - 1))
            def _():
                acc[...] += p

        @pl.when(r == n_red - 1)
        def _():
            finish([acc[...] + p for p, acc in zip(products, acc_refs)])

    return pl.pallas_call(
        body, name=name, grid=grid,
        in_specs=[a_spec] + b_specs + x_specs + r_specs + [ANY] * len(after),
        out_specs=out_specs, out_shape=out_shape,
        scratch_shapes=[pltpu.VMEM((tm, tn), F32) for _ in range(nb if n_red > 1 else 0)],
        compiler_params=_cparams(sem),
    )(a, *bs, *extras, *([resid] if resid is not None else []), *after)


def _sigmoid(x):
    return 1.0 / (1.0 + jnp.exp(-x))


def _silu(x):
    return x * _sigmoid(x)


def _gelu(x):
    return 0.5 * x * (1.0 + jnp.tanh(0.7978845608028654 * (x + 0.044715 * (x * x * x))))


def _rms_fn(h, gain):
    return h * lax.rsqrt(jnp.mean(h * h, axis=-1, keepdims=True) + EPS) * gain


def _shift_impl(x, s):
    n = x.shape[0]
    rolled = pltpu.roll(x, s % n, 0)
    row = lax.broadcasted_iota(jnp.int32, x.shape, 0)
    return jnp.where((row >= s) & (row < n + s), rolled, 0.0)


@functools.partial(jax.custom_vjp, nondiff_argnums=(1,))
def _shift(x, s):
    return _shift_impl(x, s)


def _shift_fwd(x, s):
    return _shift_impl(x, s), None


def _shift_bwd(s, _, g):
    return (_shift_impl(g, -s),)


_shift.defvjp(_shift_fwd, _shift_bwd)


def _prep_fn(x, w, qk_scale, is_v):
    y = x * w[3:4, :]
    for i in range(3):
        y = y + _shift(x, 3 - i) * w[i:i + 1, :]
    y = _silu(y)
    nrm = lax.rsqrt(jnp.sum(y * y, axis=-1, keepdims=True) + EPS) * qk_scale
    return y * jnp.where(is_v, 1.0, nrm)


def _softplus(x):
    return jnp.maximum(x, 0.0) + jnp.log(1.0 + jnp.exp(-jnp.abs(x)))


def _gates_fn(ba, a_log, dt_bias):
    lane = lax.broadcasted_iota(jnp.int32, ba.shape, 1)
    beta = _sigmoid(ba)
    g = -jnp.exp(a_log) * _softplus(ba + dt_bias)
    return jnp.where(lane < HEADS, beta, g)


def _dot16(a, b, dims=_DIMS["nn"]):
    return lax.dot_general(a.astype(BF16), b.astype(BF16), dims, preferred_element_type=F32)


def _dot32(a, b):
    return jnp.dot(a, b, preferred_element_type=F32, precision=HIGHEST)


def _dot3(a, b, dims=_DIMS["nn"]):
    return lax.dot_general(a, b, dims, preferred_element_type=F32, precision=lax.Precision.HIGH)


def _tri_inverses(mats):
    row = lax.broadcasted_iota(jnp.int32, (CH, CH), 0)
    col = lax.broadcasted_iota(jnp.int32, (CH, CH), 1)
    eye = (row == col).astype(F32)
    ts = [eye - a for a in mats]
    ps = list(mats)
    for _ in range(5):
        ps = [_dot3(p, p) for p in ps]
        ts = [t + _dot3(t, p) for t, p in zip(ts, ps)]
    return ts


@jax.custom_vjp
def _tri_solves(mats, rhs):
    return [_dot3(t, b) for t, b in zip(_tri_inverses(mats), rhs)]


def _tri_solves_fwd(mats, rhs):
    ts = _tri_inverses(mats)
    xs = [_dot3(t, b) for t, b in zip(ts, rhs)]
    return xs, (ts, xs)


def _tri_solves_bwd(res, dxs):
    ts, xs = res
    dbs = [_dot3(t, dx, _DIMS["tn"]) for t, dx in zip(ts, dxs)]
    return [-_dot3(db, x, _DIMS["nt"]) for db, x in zip(dbs, xs)], dbs


_tri_solves.defvjp(_tri_solves_fwd, _tri_solves_bwd)


def _chunk_prep_fn(xs, bgs):
    row = lax.broadcasted_iota(jnp.int32, (CH, CH), 0)
    col = lax.broadcasted_iota(jnp.int32, (CH, CH), 1)
    incl = row >= col
    strict = row > col
    lmat = incl.astype(F32)
    n = len(xs)
    items = [(i, h) for i in range(n) for h in range(HEADS)]
    part = lambda i, h, c: xs[i][:, c * HW + h * HD:c * HW + (h + 1) * HD]
    q = [part(i, h, 0) for i, h in items]
    k = [part(i, h, 1) for i, h in items]
    v = [part(i, h, 2) for i, h in items]
    beta = [bgs[i][:, h:h + 1] for i, h in items]
    gc_all = [_dot32(lmat, bg) for bg in bgs]
    gc = [gc_all[i][:, HEADS + h:HEADS + h + 1] for i, h in items]
    gmat = [jnp.where(strict, jnp.broadcast_to(bgs[i][:, HEADS + h:HEADS + h + 1], (CH, CH)), 0.0) for i, h in items]
    diff = [_dot3(lmat, m) for m in gmat]
    decay = [jnp.where(incl, jnp.exp(jnp.where(incl, d, 0.0)), 0.0) for d in diff]
    k_beta = [kk * b for kk, b in zip(k, beta)]
    kk_t = [_dot16(kb, kk, _DIMS["nt"]) for kb, kk in zip(k_beta, k)]
    qk_t = [_dot16(qq, kk, _DIMS["nt"]) for qq, kk in zip(q, k)]
    a = [jnp.where(strict, m * d, 0.0) for m, d in zip(kk_t, decay)]
    eg = [jnp.exp(g) for g in gc]
    rhs = [jnp.concatenate([vv * b, kb * e], axis=-1) for vv, b, kb, e in zip(v, beta, k_beta, eg)]
    uw = _tri_solves(a, rhs)
    qk = [m * d for m, d in zip(qk_t, decay)]
    g_last = [g[CH - 1:CH, :] for g in gc]
    qe = [qq * e for qq, e in zip(q, eg)]
    kd = [kk * jnp.exp(gl - g) for kk, gl, g in zip(k, g_last, gc)]
    egl = [jnp.broadcast_to(jnp.exp(gl), (1, HD)) for gl in g_last]
    out = []
    for i in range(n):
        mine = slice(i * HEADS, (i + 1) * HEADS)
        cat = lambda vals: jnp.concatenate(vals[mine], axis=-1)
        out.append((cat([x[:, :HD] for x in uw]), cat([x[:, HD:] for x in uw]), cat(qe), cat(kd),
                    jnp.concatenate([m[None] for m in qk[mine]], axis=0), cat(egl)))
    return out


def _chunk_state_fn(u, w, qe, kd, qk, egl, s):
    ws = [_dot16(a, b) for a, b in zip(w, s)]
    qs = [_dot16(a, b) for a, b in zip(qe, s)]
    v_new = [a - b for a, b in zip(u, ws)]
    o = [a + _dot16(b, c) for a, b, c in zip(qs, qk, v_new)]
    s_new = [a * e + _dot16(b, c, _DIMS["tn"]) for a, e, b, c in zip(s, egl, kd, v_new)]
    return o, s_new


def _mix_fn(o, z, ur, vr, ong, lng, lnb, ws, bst):
    row = lax.broadcasted_iota(jnp.int32, (GCH, GCH), 0)
    col = lax.broadcasted_iota(jnp.int32, (GCH, GCH), 1)
    causal = row >= col
    ug = _gelu(ur)
    vg = _gelu(vr)
    sls = [slice(h * HD, (h + 1) * HD) for h in range(HEADS)]
    oh = [o[:, sl] for sl in sls]
    oh = [x * lax.rsqrt(jnp.mean(x * x, axis=-1, keepdims=True) + EPS) for x in oh]
    outs_dn = [x * ong * _silu(z[:, sl]) for x, sl in zip(oh, sls)]
    vh = [vg[:, sl] for sl in sls]
    mu = [jnp.mean(x, axis=-1, keepdims=True) for x in vh]
    var = [jnp.mean(jnp.square(x - m), axis=-1, keepdims=True) for x, m in zip(vh, mu)]
    vn = [(x - m) * lax.rsqrt(s + EPS) * lng[:, sl] + lnb[:, sl] for x, m, s, sl in zip(vh, mu, var, sls)]
    mixed = [_dot16(jnp.where(causal, ws[h], 0.0), vn[h]) for h in range(HEADS)]
    outs_gm = [ug[:, sl] * (mixed[h] + bst[:, h:h + 1]) for h, sl in enumerate(sls)]
    return jnp.concatenate(outs_dn + outs_gm, axis=-1)


def _loss_fn(h, gain, tgt):
    y = _rms_fn(h, gain)
    return 0.5 * jnp.sum(jnp.mean(jnp.square(y - tgt), axis=-1))


RT = 256


def _rows(n=D):
    return pl.BlockSpec((RT, n), lambda i: (i, 0))


def _whole(shape):
    nd = len(shape)
    return pl.BlockSpec(shape, lambda i: (0,) * nd)


def _rmsnorm(name, h, gain):
    def body(h_ref, g_ref, o_ref):
        o_ref[...] = _rms_fn(h_ref[...], g_ref[...]).astype(BF16)

    return pl.pallas_call(
        body, name=name, grid=(T // RT,), in_specs=[_rows(), _whole((1, D))], out_specs=_rows(),
        out_shape=jax.ShapeDtypeStruct((T, D), BF16), compiler_params=_cparams(("parallel",)),
    )(h, gain)


def _rmsnorm_bwd(name, dhn, h, gain, resid):
    def body(dhn_ref, h_ref, g_ref, r_ref, dh_ref, dh16_ref, dg_ref):
        _, vjp = jax.vjp(_rms_fn, h_ref[...], g_ref[...])
        dh, dg = vjp(dhn_ref[...])
        dh = r_ref[...] + dh
        dh_ref[...] = dh
        dh16_ref[...] = dh.astype(BF16)

        @pl.when(pl.program_id(0) == 0)
        def _():
            dg_ref[...] = dg

        @pl.when(pl.program_id(0) > 0)
        def _():
            dg_ref[...] += dg

    return pl.pallas_call(
        body, name=name, grid=(T // RT,), in_specs=[_rows(), _rows(), _whole((1, D)), _rows()],
        out_specs=[_rows(), _rows(), _whole((1, D))],
        out_shape=[jax.ShapeDtypeStruct((T, D), F32), jax.ShapeDtypeStruct((T, D), BF16),
                   jax.ShapeDtypeStruct((1, D), F32)],
        compiler_params=_cparams(("arbitrary",)),
    )(dhn, h, gain, resid)


def _loss_head(h, gain, tgt):
    def body(h_ref, g_ref, t_ref, l_ref, dh_ref, dh16_ref, dg_ref):
        loss, vjp = jax.vjp(lambda hh, gg: _loss_fn(hh, gg, t_ref[...]), h_ref[...], g_ref[...])
        dh, dg = vjp(jnp.ones((), F32))
        dh_ref[...] = dh
        dh16_ref[...] = dh.astype(BF16)
        lv = jnp.full((1, LANE), loss, F32)

        @pl.when(pl.program_id(0) == 0)
        def _():
            dg_ref[...] = dg
            l_ref[...] = lv

        @pl.when(pl.program_id(0) > 0)
        def _():
            dg_ref[...] += dg
            l_ref[...] += lv

    return pl.pallas_call(
        body, name="loss_head", grid=(T // RT,), in_specs=[_rows(), _whole((1, D)), _rows()],
        out_specs=[_whole((1, LANE)), _rows(), _rows(), _whole((1, D))],
        out_shape=[jax.ShapeDtypeStruct((1, LANE), F32), jax.ShapeDtypeStruct((T, D), F32),
                   jax.ShapeDtypeStruct((T, D), BF16), jax.ShapeDtypeStruct((1, D), F32)],
        compiler_params=_cparams(("arbitrary",)),
    )(h, gain, tgt)


def _prep_flags():
    j = pl.program_id(0)
    qk_scale = jnp.where(j < HEADS, HD ** -0.5, 1.0).astype(F32)
    return qk_scale, j >= 2 * HEADS


def _prep(proj, conv_w):
    def body(x_ref, w_ref, o_ref):
        qk_scale, is_v = _prep_flags()
        o_ref[...] = _prep_fn(x_ref[...], w_ref[...], qk_scale, is_v)

    col = lambda j: (0, j)
    return pl.pallas_call(
        body, name="gdn_prep", grid=(3 * HEADS,),
        in_specs=[pl.BlockSpec((T, HD), col), pl.BlockSpec((4, HD), col)], out_specs=pl.BlockSpec((T, HD), col),
        out_shape=jax.ShapeDtypeStruct((T, 3 * HW), F32), compiler_params=_cparams(("parallel",)),
    )(proj, conv_w)


def _prep_bwd(proj, conv_w, dqkv, dproj):
    def body(x_ref, w_ref, d_ref, _, dx_ref, dw_ref):
        qk_scale, is_v = _prep_flags()
        _, vjp = jax.vjp(lambda x, w: _prep_fn(x, w, qk_scale, is_v), x_ref[...], w_ref[...])
        dx, dw = vjp(d_ref[...])
        dx_ref[...] = dx.astype(BF16)
        dw_ref[...] = dw

    col = lambda j: (0, j)
    return pl.pallas_call(
        body, name="gdn_prep_bwd", grid=(3 * HEADS,),
        in_specs=[pl.BlockSpec((T, HD), col), pl.BlockSpec((4, HD), col), pl.BlockSpec((T, HD), col), ANY],
        out_specs=[pl.BlockSpec((T, HD), col), pl.BlockSpec((4, HD), col)],
        out_shape=[jax.ShapeDtypeStruct((T, NP), BF16), jax.ShapeDtypeStruct((4, 3 * HW), F32)],
        input_output_aliases={3: 0}, compiler_params=_cparams(("parallel",)),
    )(proj, conv_w, dqkv, dproj)


BA_BLK = BA_OFF // LANE


def _gates(proj, a_log, dt_bias):
    def body(x_ref, a_ref, d_ref, o_ref):
        o_ref[...] = _gates_fn(x_ref[...], a_ref[...], d_ref[...])

    return pl.pallas_call(
        body, name="gdn_gates", grid=(1,),
        in_specs=[pl.BlockSpec((T, LANE), lambda i: (0, BA_BLK)), _whole((1, LANE)), _whole((1, LANE))],
        out_specs=_whole((T, LANE)),
        out_shape=jax.ShapeDtypeStruct((T, LANE), F32), compiler_params=_cparams(("arbitrary",)),
    )(proj, a_log, dt_bias)


def _gates_bwd(proj, a_log, dt_bias, dbg, dproj):
    def body(x_ref, a_ref, d_ref, dbg_ref, _, dx_ref, da_ref, dd_ref):
        _, vjp = jax.vjp(_gates_fn, x_ref[...], a_ref[...], d_ref[...])
        dx, da_ref[...], dd_ref[...] = vjp(dbg_ref[...])
        dx_ref[...] = dx.astype(BF16)

    ba = pl.BlockSpec((T, LANE), lambda i: (0, BA_BLK))
    return pl.pallas_call(
        body, name="gdn_gates_bwd", grid=(1,),
        in_specs=[ba, _whole((1, LANE)), _whole((1, LANE)), _whole((T, LANE)), ANY],
        out_specs=[ba, _whole((1, LANE)), _whole((1, LANE))],
        out_shape=[jax.ShapeDtypeStruct((T, NP), BF16), jax.ShapeDtypeStruct((1, LANE), F32),
                   jax.ShapeDtypeStruct((1, LANE), F32)],
        input_output_aliases={4: 0}, compiler_params=_cparams(("arbitrary",)),
    )(proj, a_log, dt_bias, dbg, dproj)


NCK = T // CH
CPS = 2


def _chunk_prep_specs(rev=False):
    at = (lambda n: NCK - 1 - n) if rev else (lambda n: n)
    wide = pl.BlockSpec((CH, HW), lambda n: (at(n), 0))
    return [wide, wide, wide, wide, pl.BlockSpec((HEADS, CH, CH), lambda n: (0, at(n), 0)),
            pl.BlockSpec((None, 1, HW), lambda n: (at(n), 0, 0))]


def _chunk_prep_shapes(dtypes):
    shp = [(T, HW), (T, HW), (T, HW), (T, HW), (HEADS, T, CH), (NCK, 1, HW)]
    return [jax.ShapeDtypeStruct(s, dt) for s, dt in zip(shp, dtypes)]


def _chunk_prep(qkv, bg):
    def body(x_ref, bg_ref, *o_refs):
        rows = [slice(ci * CH, (ci + 1) * CH) for ci in range(CPS)]
        res = _chunk_prep_fn([x_ref[r, :] for r in rows], [bg_ref[r, :] for r in rows])
        for ci, (u, w, qe, kd, qk, egl) in enumerate(res):
            for o_ref, val in zip(o_refs[:4], (u, w, qe, kd)):
                o_ref[rows[ci], :] = val.astype(o_ref.dtype)
            o_refs[4][:, rows[ci], :] = qk.astype(BF16)
            o_refs[5][ci] = egl

    wide = pl.BlockSpec((CPS * CH, HW), lambda n: (n, 0))
    return pl.pallas_call(
        body, name="gdn_chunk_prep", grid=(NCK // CPS,),
        in_specs=[pl.BlockSpec((CPS * CH, 3 * HW), lambda n: (n, 0)), pl.BlockSpec((CPS * CH, LANE), lambda n: (n, 0))],
        out_specs=[wide, wide, wide, wide, pl.BlockSpec((HEADS, CPS * CH, CH), lambda n: (0, n, 0)),
                   pl.BlockSpec((CPS, 1, HW), lambda n: (n, 0, 0))],
        out_shape=_chunk_prep_shapes((F32, BF16, BF16, BF16, BF16, F32)),
        compiler_params=_cparams(("parallel",)),
    )(qkv, bg)


def _chunk_prep_bwd(qkv, bg, cots):
    def body(x_ref, bg_ref, du, dw, dqe, dkd, dqk, degl, dx_ref, dbg_ref):
        rows = [slice(ci * CH, (ci + 1) * CH) for ci in range(CPS)]
        _, vjp = jax.vjp(_chunk_prep_fn, [x_ref[r, :] for r in rows], [bg_ref[r, :] for r in rows])
        dxs, dbgs = vjp([(du[r, :], dw[r, :], dqe[r, :], dkd[r, :], dqk[:, r, :], degl[ci])
                         for ci, r in enumerate(rows)])
        for r, dx, dbg in zip(rows, dxs, dbgs):
            dx_ref[r, :] = dx
            dbg_ref[r, :] = dbg

    wide = pl.BlockSpec((CPS * CH, HW), lambda n: (n, 0))
    return pl.pallas_call(
        body, name="gdn_chunk_prep_bwd", grid=(NCK // CPS,),
        in_specs=[pl.BlockSpec((CPS * CH, 3 * HW), lambda n: (n, 0)), pl.BlockSpec((CPS * CH, LANE), lambda n: (n, 0)),
                  wide, wide, wide, wide, pl.BlockSpec((HEADS, CPS * CH, CH), lambda n: (0, n, 0)),
                  pl.BlockSpec((CPS, 1, HW), lambda n: (n, 0, 0))],
        out_specs=[pl.BlockSpec((CPS * CH, 3 * HW), lambda n: (n, 0)), pl.BlockSpec((CPS * CH, LANE), lambda n: (n, 0))],
        out_shape=[jax.ShapeDtypeStruct((T, 3 * HW), F32), jax.ShapeDtypeStruct((T, LANE), F32)],
        compiler_params=_cparams(("parallel",)),
    )(qkv, bg, *cots)


def _head_args(refs):
    u, w, qe, kd, qk, egl = refs
    sls = [slice(h * HD, (h + 1) * HD) for h in range(HEADS)]
    return ([u[:, sl] for sl in sls], [w[:, sl].astype(F32) for sl in sls], [qe[:, sl].astype(F32) for sl in sls],
            [kd[:, sl].astype(F32) for sl in sls], [qk[h].astype(F32) for h in range(HEADS)],
            [egl[:, sl] for sl in sls])


def _chunk_scan(prep):
    def body(*refs):
        o_ref, sh_ref, s_ref = refs[6:]

        @pl.when(pl.program_id(0) == 0)
        def _():
            s_ref[...] = jnp.zeros_like(s_ref)

        s = [s_ref[h] for h in range(HEADS)]
        for h in range(HEADS):
            sh_ref[h, 0] = s[h]
        o, s_new = _chunk_state_fn(*_head_args(refs[:6]), s)
        for h in range(HEADS):
            o_ref[:, h * HD:(h + 1) * HD] = o[h]
            s_ref[h] = s_new[h]

    return pl.pallas_call(
        body, name="gdn_scan", grid=(NCK,), in_specs=_chunk_prep_specs(),
        out_specs=[pl.BlockSpec((CH, HW), lambda n: (n, 0)), pl.BlockSpec((HEADS, 1, HD, HD), lambda n: (0, n, 0, 0))],
        out_shape=[jax.ShapeDtypeStruct((T, HW), F32), jax.ShapeDtypeStruct((HEADS, NCK, HD, HD), F32)],
        scratch_shapes=[pltpu.VMEM((HEADS, HD, HD), F32)], compiler_params=_cparams(("arbitrary",)),
    )(*prep)


def _chunk_scan_bwd(prep, s_hist, do, after=()):
    n_in = 8 + len(after)

    def body(*refs):
        sh_ref, do_ref = refs[6:8]
        d_refs = refs[n_in:n_in + 6]
        ds_ref = refs[n_in + 6]

        @pl.when(pl.program_id(0) == 0)
        def _():
            ds_ref[...] = jnp.zeros_like(ds_ref)

        sls = [slice(h * HD, (h + 1) * HD) for h in range(HEADS)]
        _, vjp = jax.vjp(_chunk_state_fn, *_head_args(refs[:6]), [sh_ref[h, 0] for h in range(HEADS)])
        du, dw, dqe, dkd, dqk, degl, ds = vjp(([do_ref[:, sl] for sl in sls], [ds_ref[h] for h in range(HEADS)]))
        for h, sl in enumerate(sls):
            for d_ref, val in zip(d_refs[:4], (du, dw, dqe, dkd)):
                d_ref[:, sl] = val[h]
            d_refs[4][h] = dqk[h]
            d_refs[5][:, sl] = degl[h]
            ds_ref[h] = ds[h]

    rev = lambda n: NCK - 1 - n
    return pl.pallas_call(
        body, name="gdn_scan_bwd", grid=(NCK,),
        in_specs=_chunk_prep_specs(rev=True) + [pl.BlockSpec((HEADS, 1, HD, HD), lambda n: (0, rev(n), 0, 0)),
                                                pl.BlockSpec((CH, HW), lambda n: (rev(n), 0))] + [ANY] * len(after),
        out_specs=_chunk_prep_specs(rev=True), out_shape=_chunk_prep_shapes((F32,) * 6),
        scratch_shapes=[pltpu.VMEM((HEADS, HD, HD), F32)], compiler_params=_cparams(("arbitrary",)),
    )(*prep, s_hist, do, *after)


def _mix_specs():
    pc = lambda c: pl.BlockSpec((GCH, HW), lambda i: (i, c))
    return [pl.BlockSpec((GCH, HW), lambda i: (i, 0)), pc(3), pc(4), pc(5), _whole((1, HD)), _whole((1, HW)),
            _whole((1, HW)), _whole((HEADS, GCH, GCH)), _whole((GCH, LANE))]


def _mix(o, proj, ong, lng, lnb, ws, bst):
    def body(o_ref, z_ref, u_ref, v_ref, ong_ref, lng_ref, lnb_ref, ws_ref, bs_ref, m_ref):
        m_ref[...] = _mix_fn(o_ref[...], z_ref[...], u_ref[...], v_ref[...], ong_ref[...], lng_ref[...],
                             lnb_ref[...], ws_ref[...], bs_ref[...]).astype(BF16)

    return pl.pallas_call(
        body, name="mix", grid=(T // GCH,), in_specs=_mix_specs(),
        out_specs=pl.BlockSpec((GCH, D), lambda i: (i, 0)), out_shape=jax.ShapeDtypeStruct((T, D), BF16),
        compiler_params=_cparams(("parallel",)),
    )(o, proj, proj, proj, ong, lng, lnb, ws, bst)


def _mix_bwd(o, proj, ong, lng, lnb, ws, bst, dmix):
    def body(o_ref, z_ref, u_ref, v_ref, ong_ref, lng_ref, lnb_ref, ws_ref, bs_ref, dm_ref,
             do_ref, dzuv_ref, dong_ref, dlng_ref, dlnb_ref, dws_ref, dbs_ref):
        _, vjp = jax.vjp(_mix_fn, o_ref[...], z_ref[...], u_ref[...], v_ref[...], ong_ref[...], lng_ref[...],
                         lnb_ref[...], ws_ref[...], bs_ref[...])
        do, dz, du, dv, dong, dlng, dlnb, dws, dbs = vjp(dm_ref[...])
        do_ref[...] = do
        dzuv_ref[:, 0:HW] = dz.astype(BF16)
        dzuv_ref[:, HW:2 * HW] = du.astype(BF16)
        dzuv_ref[:, 2 * HW:3 * HW] = dv.astype(BF16)
        acc = [(dong_ref, dong), (dlng_ref, dlng), (dlnb_ref, dlnb), (dws_ref, dws), (dbs_ref, dbs)]

        @pl.when(pl.program_id(0) == 0)
        def _():
            for r, val in acc:
                r[...] = val

        @pl.when(pl.program_id(0) > 0)
        def _():
            for r, val in acc:
                r[...] += val

    shp = lambda *s: jax.ShapeDtypeStruct(s, F32)
    return pl.pallas_call(
        body, name="mix_bwd", grid=(T // GCH,),
        in_specs=_mix_specs() + [pl.BlockSpec((GCH, D), lambda i: (i, 0))],
        out_specs=[pl.BlockSpec((GCH, HW), lambda i: (i, 0)), pl.BlockSpec((GCH, 3 * HW), lambda i: (i, 1)),
                   _whole((1, HD)), _whole((1, HW)), _whole((1, HW)), _whole((HEADS, GCH, GCH)), _whole((GCH, LANE))],
        out_shape=[shp(T, HW), jax.ShapeDtypeStruct((T, NP), BF16), shp(1, HD), shp(1, HW), shp(1, HW),
                   shp(HEADS, GCH, GCH), shp(GCH, LANE)],
        compiler_params=_cparams(("arbitrary",)),
    )(o, proj, proj, proj, ong, lng, lnb, ws, bst, dmix)


def _swiglu_epilogue(accs, _):
    gate, up = accs
    return [gate, up, _silu(gate) * up]


def _swiglu_bwd_epilogue(accs, extras):
    dact = accs[0]
    gate, up = (e.astype(F32) for e in extras)
    sg = _sigmoid(gate)
    return [dact * up * (sg * (1.0 + gate * (1.0 - sg))), dact * (gate * sg)]


def _layer_fwd(h, p):
    hn = _rmsnorm("rms_mix", h, p["norm_mix"])
    proj = _mm("in_proj", "nn", hn[None], [p["w_in"][None]], tm=1024, tn=640, tk=D)[0][0]
    qkv = _prep(proj, p["conv_w"])
    bg = _gates(proj, p["a_log"], p["dt_bias"])
    prep = _chunk_prep(qkv, bg)
    o, s_hist = _chunk_scan(prep)
    if "late" in p:
        p.update(p.pop("late")(o))
    mix = _mix(o, proj, p["o_norm_g"], p["ln_v_g"], p["ln_v_b"], p["w_s"], p["bst"])
    h1 = _mm("out_proj", "nn", mix[None], [p["w_out"].reshape(1, D, D)], tm=1024, tn=512, tk=D, resid=h[None])[0][0]
    h2n = _rmsnorm("rms_ffn", h1, p["norm_ffn"])
    gate, up, act = _mm("ffn_in", "nt", h2n[None], [p["w_gate"], p["w_up"]], tm=1024, tn=FF_SH, tk=D,
                        out_dtypes=(BF16, BF16, BF16), epilogue=_swiglu_epilogue)
    h2 = _mm("ffn_out", "nn", act, [p["w_down"]], tm=1024, tn=512, tk=FF_SH, reduce_g=True, fold_g=True,
             resid=h1[None])[0][0]
    saved = dict(h=h, hn=hn, proj=proj, qkv=qkv, bg=bg, prep=prep, o=o, s_hist=s_hist, mix=mix, h1=h1, h2n=h2n,
                 gate=gate, up=up, act=act)
    return h2, saved


def _layer_bwd_ffn(dh2, dh2b, p, s, after=()):
    dh2b = dh2b[None]
    dgate, dup = _mm("ffn_out_bwd", "nt", dh2b, [p["w_down"]], tm=1024, tn=FF_SH, tk=D, out_dtypes=(BF16, BF16),
                     extras=(s["gate"], s["up"]), epilogue=_swiglu_bwd_epilogue, after=after)
    dh2n = _mm("ffn_gate_bwd", "nn", dgate, [p["w_gate"]], tm=1024, tn=512, tk=FF_SH, reduce_g=True, fold_g=True)[0]
    dh2n = _mm("ffn_up_bwd", "nn", dup, [p["w_up"]], tm=1024, tn=512, tk=FF_SH, reduce_g=True, fold_g=True,
               resid=dh2n)[0][0]
    dh1, dh1b, d_norm_ffn = _rmsnorm_bwd("rms_ffn_bwd", dh2n, s["h1"], p["norm_ffn"], dh2)
    d_w_down = _mm("ffn_wdown_grad", "tn", s["act"], [dh2b], tm=FF_SH, tn=512, tk=T)[0]
    d_w_gate = _mm("ffn_wgate_grad", "tn", dgate, [s["h2n"][None]], tm=FF_SH, tn=512, tk=T)[0]
    d_w_up = _mm("ffn_wup_grad", "tn", dup, [s["h2n"][None]], tm=FF_SH, tn=512, tk=T)[0]
    return dh1, dh1b, dict(norm_ffn=d_norm_ffn, w_gate=d_w_gate, w_up=d_w_up, w_down=d_w_down)


def _layer_bwd_mixer(dh1, dh1b, p, s, after=(), midway=None, late=None):
    dh1b = dh1b[None]
    dmix = _mm("out_proj_bwd", "nt", dh1b, [p["w_out"].reshape(1, D, D)], tm=1024, tn=512, tk=D, after=after)[0][0]
    d_w_out = _mm("out_proj_wgrad", "tn", s["mix"][None], [dh1b], tm=512, tn=512, tk=T)[0][0]
    do, dproj, d_ong, d_lng, d_lnb, d_ws, d_bst = _mix_bwd(
        s["o"], s["proj"], p["o_norm_g"], p["ln_v_g"], p["ln_v_b"], p["w_s"], p["bst"], dmix)
    then = midway(do) if midway is not None else ()
    dqkv, dbg = _chunk_prep_bwd(s["qkv"], s["bg"], _chunk_scan_bwd(s["prep"], s["s_hist"], do, then))
    dproj, d_conv = _prep_bwd(s["proj"], p["conv_w"], dqkv, dproj)
    dproj, d_a_log, d_dt_bias = _gates_bwd(s["proj"], p["a_log"], p["dt_bias"], dbg, dproj)
    dproj = dproj[None]
    d_w_in = _mm("in_proj_wgrad", "tn", s["hn"][None], [dproj], tm=512, tn=640, tk=T)[0]
    last = late(dict(w_in=d_w_in, w_out=d_w_out)) if late is not None else ()
    dhn = _mm("in_proj_bwd", "nt", dproj, [p["w_in"][None]], tm=1024, tn=512, tk=NP, after=last)[0][0]
    dh, dhb, d_norm_mix = _rmsnorm_bwd("rms_mix_bwd", dhn, s["h"], p["norm_mix"], dh1)
    grads = dict(norm_mix=d_norm_mix, w_in=d_w_in, conv_w=d_conv, a_log=d_a_log, dt_bias=d_dt_bias, o_norm_g=d_ong,
                 ln_v_g=d_lng, ln_v_b=d_lnb, w_s=d_ws, bst=d_bst, w_out=d_w_out)
    return dh, dhb, grads


def _lanes(v, off=0):
    return jnp.zeros((1, LANE), F32).at[0, off:off + v.shape[0]].set(v)


def _w_in_pieces():
    regions = [(0, 2048, 0), (2048, 2056, BA_OFF), (2056, IN_DIM, 2048)]
    sh = IN_DIM // NCHIP
    out = []
    for j in range(NCHIP):
        for lo, hi, at in regions:
            a, b = max(lo, j * sh), min(hi, (j + 1) * sh)
            if a < b:
                out.append((j, a - j * sh, at + a - lo, b - a))
    return out


W_IN_PIECES = _w_in_pieces()
WT = 256


def _assemble_w_in(gathered, own, place):
    def body(place_ref, g_ref, own_ref, o_ref):
        o_ref[:, IN_DIM:] = jnp.zeros((WT, NP - IN_DIM), BF16)
        mine = own_ref[...]
        for j, src, dst, width in W_IN_PIECES:
            val = jnp.where(place_ref[0] == j, mine[:, src:src + width], g_ref[j, :, src:src + width])
            o_ref[:, dst:dst + width] = val

    sh = IN_DIM // NCHIP
    return pl.pallas_call(
        body, name="assemble_w_in",
        grid_spec=pltpu.PrefetchScalarGridSpec(
            num_scalar_prefetch=1, grid=(D // WT,),
            in_specs=[pl.BlockSpec((NCHIP, WT, sh), lambda i, place_ref: (0, i, 0)),
                      pl.BlockSpec((WT, sh), lambda i, place_ref: (i, 0))],
            out_specs=pl.BlockSpec((WT, NP), lambda i, place_ref: (i, 0))),
        out_shape=jax.ShapeDtypeStruct((D, NP), BF16), compiler_params=_cparams(("parallel",)),
    )(place, gathered, own)


def _layer_params(l, big, small):
    return dict(
        {k: v for k, v in big.items() if k != "conv_w"},
        conv_w=jnp.concatenate([big["conv_w"][j, l] for j in range(NCHIP)], axis=1),
        norm_mix=small["norm_mix"][l][None], norm_ffn=small["norm_ffn"][l][None],
        a_log=_lanes(small["a_log"][l], HEADS), dt_bias=_lanes(small["dt_bias"][l], HEADS),
        o_norm_g=small["o_norm_g"][l][None], ln_v_g=small["ln_v_g"][l][None], ln_v_b=small["ln_v_b"][l][None],
        w_s=small["w_s"][l],
        bst=jnp.pad(small["b_s"][l].T, ((0, 0), (0, LANE - HEADS))),
    )


def _reference_layout(g):
    return dict(
        w_in=g["w_in"],
        w_out=g["w_out"].reshape(NCHIP, D // NCHIP, D),
        w_gate=g["w_gate"], w_up=g["w_up"], w_down=g["w_down"],
        conv_w=g["conv_w"], norm_mix=g["norm_mix"][0], norm_ffn=g["norm_ffn"][0],
        a_log=g["a_log"][0, HEADS:2 * HEADS], dt_bias=g["dt_bias"][0, HEADS:2 * HEADS],
        o_norm_g=g["o_norm_g"][0], ln_v_g=g["ln_v_g"][0], ln_v_b=g["ln_v_b"][0], w_s=g["w_s"],
        b_s=g["bst"][:, :HEADS].T,
    )


def _forward(x, tgt, layers, norm_final):
    h = x
    saved, params = [], []
    for p in layers:
        p = p(h) if callable(p) else p
        h, s = _layer_fwd(h, p)
        saved.append(s)
        params.append(p)
    return (saved, params) + tuple(_loss_head(h, norm_final, tgt))


def _local_step(x, tgt, layers, norm_final):
    saved, layers, loss, dh, dhb, d_norm_final = _forward(x, tgt, layers, norm_final)
    grads = [None] * DEPTH
    for l in reversed(range(DEPTH)):
        dh1, dh1b, g_ffn = _layer_bwd_ffn(dh, dhb, layers[l], saved[l])
        dh, dhb, g_mix = _layer_bwd_mixer(dh1, dh1b, layers[l], saved[l])
        grads[l] = {**g_ffn, **g_mix}
    return loss, dh, grads, d_norm_final


def _place():
    x, y, c = lax.axis_index("x"), lax.axis_index("y"), lax.axis_index("c")
    return x, y, c, [(1 - x, y), (x, 1 - y), (1 - x, 1 - y)]


def _remote(src, dst, send_sem, recv_sem, to):
    return pltpu.make_async_remote_copy(src_ref=src, dst_ref=dst, send_sem=send_sem, recv_sem=recv_sem,
                                        device_id=to, device_id_type=MESH)


def _comm_call(name, body, ins, out_shape, n_sems, aliases=None):
    return pl.pallas_call(
        body, name=name, in_specs=[ANY] * len(ins), out_specs=[ANY] * len(out_shape), out_shape=out_shape,
        scratch_shapes=[pltpu.SemaphoreType.DMA((n,)) for n in n_sems], input_output_aliases=aliases or {},
        compiler_params=pltpu.CompilerParams(has_side_effects=True),
    )(*ins)


def _half_rows(ref, of_c, dim):
    hr = ref.shape[dim] // 2
    return pl.ds(pl.multiple_of(of_c * hr, BF16_ROWS), hr)


def _gather_plan(whole):
    def plan(srcs, lands):
        x, y, c, others = _place()
        chip = 2 * x + y
        out = []
        for src, land, all_of_it in zip(srcs, lands, whole):
            for ox, oy in others:
                if all_of_it:
                    out.append((src, land.at[chip], (ox, oy, c)))
                else:
                    out.append((src.at[_half_rows(src, c, 0)], land.at[chip, _half_rows(src, c, 0)], (ox, oy, c)))
        return out
    return plan


def _forward_halves(lands):
    n = len(lands)

    def body(*refs):
        outs = refs[n:2 * n]
        send_s, recv_s = refs[2 * n:]
        x, y, c, others = _place()
        sibling = (x, y, 1 - c)
        copies = []
        for a in range(n):
            for k, (ox, oy) in enumerate(others):
                mine = outs[a].at[2 * ox + oy, _half_rows(outs[a], c, 1)]
                copies.append(_remote(mine, mine, send_s.at[3 * a + k], recv_s.at[3 * a + k], sibling))
        for cp in copies:
            cp.start()
        for a in range(n):
            for k, (ox, oy) in enumerate(others):
                landed = outs[a].at[2 * ox + oy, _half_rows(outs[a], 1 - c, 1)]
                _remote(landed, landed, send_s.at[3 * a + k], recv_s.at[3 * a + k], sibling).wait_recv()
        for cp in copies:
            cp.wait_send()

    out_shape = [jax.ShapeDtypeStruct(g.shape, g.dtype) for g in lands]
    return _comm_call("forward_halves", body, lands, out_shape, [3 * n, 3 * n], aliases={a: a for a in range(n)})


HBM_SPEC = pl.BlockSpec(memory_space=pltpu.HBM)
SEM_SPEC = pl.BlockSpec(memory_space=pltpu.SEMAPHORE)
DATAFLOW = pltpu.SideEffectType.DATAFLOW_SIDE_EFFECTING


def _exchange_plan(srcs, lands):
    x, y, c, _ = _place()
    plan = []
    for src, land in zip(srcs, lands):
        hr = src.shape[1] // 2
        plan.append((src.at[:, pl.ds(pl.multiple_of((1 - c) * hr, 8), hr)], land, (x, y, 1 - c)))
    return plan


def _scatter_plan(srcs, lands):
    x, y, c, others = _place()
    return [(src.at[2 * ox + oy], land.at[k], (ox, oy, c))
            for src, land in zip(srcs, lands) for k, (ox, oy) in enumerate(others)]


def _split_start(name, plan, srcs, land_shapes, n_copies, after=()):
    n = len(srcs)
    lands = [pltpu.with_memory_space_constraint(lax.empty(s.shape, s.dtype), pltpu.HBM) for s in land_shapes]
    srcs = [pltpu.with_memory_space_constraint(s, pltpu.HBM) for s in srcs]

    def body(*refs):
        send_s, recv_s = refs[2 * n + len(after)], refs[2 * n + len(after) + 1]
        for i, (src, dst, to) in enumerate(plan(refs[:n], refs[n:2 * n])):
            _remote(src, dst, send_s.at[i], recv_s.at[i], to).start()
        refs[-1][...] = jnp.zeros_like(refs[-1])

    thru = [pltpu.HBM(s.shape, s.dtype) for s in srcs + lands]
    out = pl.pallas_call(
        body, name=name, in_specs=[HBM_SPEC] * (2 * n) + [ANY] * len(after),
        out_specs=[SEM_SPEC, SEM_SPEC] + [HBM_SPEC] * (2 * n) + [pl.BlockSpec(memory_space=pltpu.VMEM)],
        out_shape=[pltpu.SemaphoreType.DMA((n_copies,)), pltpu.SemaphoreType.DMA((n_copies,))] + thru
        + [jax.ShapeDtypeStruct((F32_ROWS, LANE), F32)],
        input_output_aliases={i: 2 + i for i in range(2 * n)},
        compiler_params=pltpu.CompilerParams(has_side_effects=DATAFLOW),
    )(*srcs, *lands, *after)
    return dict(sems=out[:2], srcs=out[2:2 + n], lands=out[2 + n:2 + 2 * n], token=out[-1])


def _split_wait(name, plan, started, after):
    n = len(started["srcs"])
    after = list(after) if isinstance(after, (list, tuple)) else [after]

    def body(*refs):
        send_s, recv_s = refs[2 * n], refs[2 * n + 1]
        for i, (src, dst, to) in enumerate(plan(refs[:n], refs[n:2 * n])):
            cp = _remote(src, dst, send_s.at[i], recv_s.at[i], to)
            cp.wait_send()
            cp.wait_recv()

    arrs = list(started["srcs"]) + list(started["lands"])
    out = pl.pallas_call(
        body, name=name, in_specs=[HBM_SPEC] * (2 * n) + [SEM_SPEC, SEM_SPEC] + [ANY] * len(after),
        out_specs=[HBM_SPEC] * (2 * n), out_shape=[pltpu.HBM(s.shape, s.dtype) for s in arrs],
        input_output_aliases={i: i for i in range(2 * n)},
        compiler_params=pltpu.CompilerParams(has_side_effects=DATAFLOW),
    )(*arrs, *started["sems"], *after)
    return out[:n], out[n:]


def _join_halves(name, rs):
    n = len(rs)

    def body(*refs):
        outs = refs[n:2 * n]
        send_s, recv_s = refs[2 * n:]
        x, y, c, _ = _place()
        sibling = (x, y, 1 - c)

        def half(a, of_c):
            hr = outs[a].shape[1] // 2
            return outs[a].at[:, pl.ds(pl.multiple_of(of_c * hr, 8), hr)]

        copies = [_remote(half(a, c), half(a, c), send_s.at[a], recv_s.at[a], sibling) for a in range(n)]
        for cp in copies:
            cp.start()
        for a in range(n):
            landed = half(a, 1 - c)
            _remote(landed, landed, send_s.at[a], recv_s.at[a], sibling).wait_recv()
        for cp in copies:
            cp.wait_send()

    out_shape = [jax.ShapeDtypeStruct(r.shape, r.dtype) for r in rs]
    return _comm_call(name, body, rs, out_shape, [n, n], aliases={a: a for a in range(n)})


def _allreduce_small(buf, after=()):
    r = buf.shape[0]
    hr = r // 2

    def body(in_ref, *refs):
        out_ref, theirs, by_chip, send_s, recv_s = refs[len(after):]
        x, y, c, others = _place()
        chip = 2 * x + y
        sibling = (x, y, 1 - c)
        mine = pl.ds(pl.multiple_of(c * hr, F32_ROWS), hr)
        swap = _remote(in_ref, theirs, send_s.at[0], recv_s.at[0], sibling)
        swap.start()
        swap.wait()
        by_chip[chip] = in_ref[mine, :] + theirs[mine, :]
        sends = [_remote(by_chip.at[chip], by_chip.at[chip], send_s.at[1 + k], recv_s.at[1 + k], (ox, oy, c))
                 for k, (ox, oy) in enumerate(others)]
        for cp in sends:
            cp.start()
        for k, (ox, oy) in enumerate(others):
            landed = by_chip.at[2 * ox + oy]
            _remote(landed, landed, send_s.at[1 + k], recv_s.at[1 + k], (ox, oy, c)).wait_recv()
        for cp in sends:
            cp.wait_send()
        out_ref[mine, :] = (by_chip[0] + by_chip[1]) + (by_chip[2] + by_chip[3])
        back = _remote(out_ref.at[mine], out_ref.at[mine], send_s.at[NCHIP], recv_s.at[NCHIP], sibling)
        back.start()
        other = out_ref.at[pl.ds(pl.multiple_of((1 - c) * hr, F32_ROWS), hr)]
        _remote(other, other, send_s.at[NCHIP], recv_s.at[NCHIP], sibling).wait_recv()
        back.wait_send()

    vm = pl.BlockSpec(memory_space=pltpu.VMEM)
    return pl.pallas_call(
        body, name="allreduce_small", in_specs=[vm] + [ANY] * len(after), out_specs=vm,
        out_shape=jax.ShapeDtypeStruct((r, LANE), F32),
        scratch_shapes=[pltpu.VMEM((r, LANE), F32), pltpu.VMEM((NCHIP, hr, LANE), F32),
                        pltpu.SemaphoreType.DMA((NCHIP + 1,)), pltpu.SemaphoreType.DMA((NCHIP + 1,))],
        compiler_params=pltpu.CompilerParams(has_side_effects=True, vmem_limit_bytes=VMEM_LIMIT),
    )(buf, *after)


MAX_ROW_TILE = 512
BF16_ROWS = 16


def _row_tile(rows):
    for t in range(min(rows, MAX_ROW_TILE) // BF16_ROWS * BF16_ROWS, 0, -BF16_ROWS):
        if rows % t == 0:
            return t
    raise ValueError(rows)


def _sum_halves(g, theirs, c_arr):
    nch, rows, cols = g.shape
    hr = rows // 2
    tr = _row_tile(hr)

    def body(c_ref, g_ref, t_ref, o_ref, ob_ref):
        s = g_ref[...] + t_ref[...]
        o_ref[...] = s
        ob_ref[...] = s.astype(BF16)

    blk = pl.BlockSpec((None, tr, cols), lambda j, i, c_ref: (j, i, 0))
    return pl.pallas_call(
        body, name="sum_halves",
        grid_spec=pltpu.PrefetchScalarGridSpec(
            num_scalar_prefetch=1, grid=(nch, hr // tr),
            in_specs=[pl.BlockSpec((None, None, tr, cols), lambda j, i, c_ref: (j, c_ref[0], i, 0)), blk],
            out_specs=[blk, blk]),
        out_shape=[jax.ShapeDtypeStruct((nch, hr, cols), F32), jax.ShapeDtypeStruct((nch, hr, cols), BF16)],
        compiler_params=_cparams(("parallel", "parallel")),
    )(c_arr, g.reshape(nch, 2, hr, cols), theirs)


def _sum_halves_w_in(g, theirs, c_arr):
    hr = D // 2
    sh = IN_DIM // NCHIP

    def body(c_ref, g_ref, t_ref, o_ref, ob_ref):
        s = g_ref[...] + t_ref[...]
        for j, dst, src, width in W_IN_PIECES:
            o_ref[j, :, dst:dst + width] = s[:, src:src + width]
            ob_ref[j, :, dst:dst + width] = s[:, src:src + width].astype(BF16)

    out = pl.BlockSpec((NCHIP, WT, sh), lambda i, c_ref: (0, i, 0))
    return pl.pallas_call(
        body, name="sum_halves_w_in",
        grid_spec=pltpu.PrefetchScalarGridSpec(
            num_scalar_prefetch=1, grid=(hr // WT,),
            in_specs=[pl.BlockSpec((None, WT, NP), lambda i, c_ref: (c_ref[0], i, 0)),
                      pl.BlockSpec((None, WT, NP), lambda i, c_ref: (0, i, 0))],
            out_specs=[out, out]),
        out_shape=[jax.ShapeDtypeStruct((NCHIP, hr, sh), F32), jax.ShapeDtypeStruct((NCHIP, hr, sh), BF16)],
        compiler_params=_cparams(("parallel",)),
    )(c_arr, g.reshape(2, hr, NP), theirs)


def _sum_chips(p, q, place, l, into=None, after=()):
    extra = ([into] if into is not None else []) + list(after)
    _, rows, cols = p.shape
    tr = _row_tile(rows)
    steps = rows // tr

    def body(place_ref, p_ref, q0, q1, q2, *rest):
        rest[-1][...] = ((p_ref[...] + q0[...].astype(F32)) + q1[...].astype(F32)) + q2[...].astype(F32)

    qs = lambda k: pl.BlockSpec((None, tr, cols), lambda i, place_ref: (k, i, 0))
    return pl.pallas_call(
        body, name="sum_chips",
        grid_spec=pltpu.PrefetchScalarGridSpec(
            num_scalar_prefetch=1, grid=(steps,),
            in_specs=[pl.BlockSpec((None, tr, cols), lambda i, place_ref: (place_ref[0], i, 0)), qs(0), qs(1), qs(2)]
            + [ANY] * len(extra),
            out_specs=pl.BlockSpec((None, tr, cols), lambda i, place_ref: (l, place_ref[1] * steps + i, 0))),
        out_shape=jax.ShapeDtypeStruct((DEPTH, 2 * rows, cols), F32),
        input_output_aliases={5: 0} if into is not None else {},
        compiler_params=_cparams(("parallel",)),
    )(place, p, q, q, q, *extra)


def _adamw(w, g, m, v):
    layers, rows, cols = w.shape
    tr = _row_tile(rows)

    def body(w_ref, g_ref, m_ref, v_ref, d_ref, nm_ref, nv_ref):
        gv = g_ref[...]
        nm = ADAM_B1 * m_ref[...] + (1.0 - ADAM_B1) * gv
        nv = ADAM_B2 * v_ref[...] + (1.0 - ADAM_B2) * jnp.square(gv)
        m_hat = nm / (1.0 - ADAM_B1 ** ADAM_STEP)
        v_hat = nv / (1.0 - ADAM_B2 ** ADAM_STEP)
        d_ref[...] = -ADAM_LR * (m_hat / (jnp.sqrt(v_hat) + ADAM_EPS) + ADAM_WD * w_ref[...])
        nm_ref[...] = nm
        nv_ref[...] = nv

    blk = pl.BlockSpec((None, tr, cols), lambda l, i: (l, i, 0))
    return pl.pallas_call(
        body, name="adamw", grid=(layers, rows // tr), in_specs=[blk] * 4, out_specs=[blk] * 3,
        out_shape=[jax.ShapeDtypeStruct(w.shape, F32)] * 3, compiler_params=_cparams(("parallel", "parallel")),
    )(w, g, m, v)


BIG = ("w_in", "w_out", "w_gate", "w_up", "w_down")
SMALL = ("norm_mix", "a_log", "dt_bias", "o_norm_g", "ln_v_g", "ln_v_b", "w_s", "b_s", "norm_ffn", "norm_final")
ORDER = ("norm_mix", "w_in", "conv_w", "a_log", "dt_bias", "o_norm_g", "ln_v_g", "ln_v_b", "w_s", "b_s", "w_out",
         "norm_ffn", "w_gate", "w_up", "w_down", "norm_final")


F32_ROWS = 8
PACK_ROWS = 128


def _lane_rows(size):
    return -(-size // (F32_ROWS * LANE)) * F32_ROWS


def _pack(arrs):
    parts = [jnp.pad(a.reshape(-1), (0, _lane_rows(a.size) * LANE - a.size)).reshape(-1, LANE) for a in arrs]
    rows = sum(p.shape[0] for p in parts)
    if rows % PACK_ROWS:
        parts.append(jnp.zeros((-rows % PACK_ROWS, LANE), F32))
    return jnp.concatenate(parts, axis=0)


def _unpack(buf, like):
    out, row = [], 0
    for a in like:
        n = _lane_rows(a.size)
        out.append(buf[row:row + n].reshape(-1)[:a.size].reshape(a.shape))
        row += n
    return out


def kernel(x, norm_mix, w_in, conv_w, a_log, dt_bias, o_norm_g, ln_v_g, ln_v_b, w_s, b_s, w_out, norm_ffn, w_gate, w_up, w_down, norm_final, loss_target, m_norm_mix, m_w_in, m_conv_w, m_a_log, m_dt_bias, m_o_norm_g, m_ln_v_g, m_ln_v_b, m_w_s, m_b_s, m_w_out, m_norm_ffn, m_w_gate, m_w_up, m_w_down, m_norm_final, v_norm_mix, v_w_in, v_conv_w, v_a_log, v_dt_bias, v_o_norm_g, v_ln_v_g, v_ln_v_b, v_w_s, v_b_s, v_w_out, v_norm_ffn, v_w_gate, v_w_up, v_w_down, v_norm_final):
    w = dict(norm_mix=norm_mix, w_in=w_in, conv_w=conv_w, a_log=a_log, dt_bias=dt_bias, o_norm_g=o_norm_g,
             ln_v_g=ln_v_g, ln_v_b=ln_v_b, w_s=w_s, b_s=b_s, w_out=w_out, norm_ffn=norm_ffn, w_gate=w_gate, w_up=w_up,
             w_down=w_down, norm_final=norm_final)
    m = dict(norm_mix=m_norm_mix, w_in=m_w_in, conv_w=m_conv_w, a_log=m_a_log, dt_bias=m_dt_bias, o_norm_g=m_o_norm_g,
             ln_v_g=m_ln_v_g, ln_v_b=m_ln_v_b, w_s=m_w_s, b_s=m_b_s, w_out=m_w_out, norm_ffn=m_norm_ffn,
             w_gate=m_w_gate, w_up=m_w_up, w_down=m_w_down, norm_final=m_norm_final)
    v = dict(norm_mix=v_norm_mix, w_in=v_w_in, conv_w=v_conv_w, a_log=v_a_log, dt_bias=v_dt_bias, o_norm_g=v_o_norm_g,
             ln_v_g=v_ln_v_g, ln_v_b=v_ln_v_b, w_s=v_w_s, b_s=v_b_s, w_out=v_w_out, norm_ffn=v_norm_ffn,
             w_gate=v_w_gate, w_up=v_w_up, w_down=v_w_down, norm_final=v_norm_final)
    chip = 2 * lax.axis_index("x") + lax.axis_index("y")
    place = jnp.stack([chip, lax.axis_index("c")]).astype(jnp.int32)
    c_arr = place[1:]

    def kernel_view(n, a):
        return jnp.swapaxes(a, 1, 2) if n in ("w_gate", "w_up") else a

    own = {n: [kernel_view(n, w[n])[l].astype(BF16) for l in range(DEPTH)] for n in BIG}
    by_chip = lambda a: jax.ShapeDtypeStruct((NCHIP,) + a.shape, a.dtype)

    def start(name, srcs, whole, after=()):
        return _split_start(name, _gather_plan(whole), srcs, [by_chip(a) for a in srcs], 3 * len(srcs), after)

    def finish(name, started, whole, after):
        srcs, lands = _split_wait(name, _gather_plan(whole), started, after)
        passed = iter(_forward_halves([g for g, all_of_it in zip(lands, whole) if not all_of_it]))
        lands = [g if all_of_it else next(passed) for g, all_of_it in zip(lands, whole)]
        return srcs, [lax.dynamic_update_index_in_dim(g, o, chip, 0) for g, o in zip(lands, srcs)]

    ffn = BIG[1:]
    first = start("gather_first_start", [own["w_in"][0], conv_w], [False, True])
    early = start("gather_early_start", [own[n][0] for n in ffn], [False] * len(ffn), [first["token"]])
    later = start("gather_later_start", [own[n][1] for n in BIG], [False] * len(BIG), [early["token"]])
    (own_w_in, _), (w_in_by_chip, conv_by_chip) = finish("gather_first_wait", first, [False, True], later["token"])

    def late(after):
        return dict(zip(ffn, finish("gather_early_wait", early, [False] * len(ffn), after)[1]))

    layer0 = _layer_params(0, dict(w_in=_assemble_w_in(w_in_by_chip, own_w_in, place), conv_w=conv_by_chip, late=late), w)

    def layer1(after):
        srcs, by = finish("gather_later_wait", later, [False] * len(BIG), after)
        big = dict(zip(ffn, by[1:]), w_in=_assemble_w_in(by[0], srcs[0], place), conv_w=conv_by_chip)
        return _layer_params(1, big, w)

    saved, layers, loss_lanes, dh, dhb, d_norm_final = _forward(x[0], loss_target[0], [layer0, layer1],
                                                                 norm_final[None])
    loss = lax.psum(loss_lanes[0, 0], ("x", "y", "c"))

    sums, arrived = {}, {}

    def exchange_start(tag, l, names, grads, after=()):
        mine = [grads[n] for n in names]
        shapes = [jax.ShapeDtypeStruct((g.shape[0], g.shape[1] // 2, g.shape[2]), F32) for g in mine]
        return tag, l, names, _split_start(f"exchange_{tag}_start", _exchange_plan, mine, shapes, len(mine), after)

    def add_halves(l, names, mine, theirs):
        for n, g, t in zip(names, mine, theirs):
            sums[l, n] = (_sum_halves_w_in if n == "w_in" else _sum_halves)(g, t, c_arr)

    def exchange_wait(handle, after):
        tag, l, names, started = handle
        add_halves(l, names, *_split_wait(f"exchange_{tag}_wait", _exchange_plan, started, after))

    def scatter_start(tag, l, names, after=()):
        partial = [sums[l, n][1] for n in names]
        shapes = [jax.ShapeDtypeStruct((3,) + p.shape[1:], p.dtype) for p in partial]
        return tag, l, names, _split_start(f"scatter_{tag}_start", _scatter_plan, partial, shapes, 3 * len(names), after)

    def scatter_wait(handle, after):
        tag, l, names, started = handle
        for n, q in zip(names, _split_wait(f"scatter_{tag}_wait", _scatter_plan, started, after)[1]):
            arrived[l, n] = q

    last = DEPTH - 1
    swiglu = BIG[2:]
    dh1, dh1b, g_ffn = _layer_bwd_ffn(dh, dhb, layers[last], saved[last])
    dh, dhb, g_mix = _layer_bwd_mixer(dh1, dh1b, layers[last], saved[last])
    gl = [None, _reference_layout({**g_ffn, **g_mix})]
    ex_last = exchange_start("last", last, BIG, gl[last])
    dh1, dh1b, g_ffn = _layer_bwd_ffn(dh, dhb, layers[0], saved[0], after=[ex_last[-1]["token"]])
    exchange_wait(ex_last, dh1)
    sc_last = scatter_start("last", last, BIG)
    ex_ffn = exchange_start("swiglu", 0, swiglu, g_ffn, [sc_last[-1]["token"]])
    sc_ffn = []

    def midway(do):
        exchange_wait(ex_ffn, do)
        sc_ffn.append(scatter_start("swiglu", 0, swiglu))
        return [sc_ffn[0][-1]["token"]]

    ex_rest = []

    def late(grads):
        rest_grads = dict(w_in=grads["w_in"], w_out=grads["w_out"].reshape(NCHIP, D // NCHIP, D))
        ex_rest.append(exchange_start("rest", 0, BIG[:2], rest_grads))
        return [ex_rest[0][-1]["token"]]

    dx, _, g_mix = _layer_bwd_mixer(dh1, dh1b, layers[0], saved[0], after=[ex_ffn[-1]["token"]], midway=midway,
                                    late=late)
    scatter_wait(sc_last, dx)
    scatter_wait(sc_ffn[0], dx)
    gl[0] = _reference_layout({**g_ffn, **g_mix})

    small_g = [jnp.stack([gl[l][n] for l in range(DEPTH)]) for n in SMALL[:-1]] + [d_norm_final[0]]
    conv_g = jnp.stack([gl[l]["conv_w"] for l in range(DEPTH)])
    total = _allreduce_small(_pack(small_g + [conv_g]))
    exchange_wait(ex_rest[0], total)
    sc_rest = scatter_start("rest", 0, BIG[:2])

    travelling = [sc_rest[-1]["token"]]
    reduced, g_out, delta, new_m, new_v = {}, {}, {}, {}, {}

    def adamw_large(names, joined):
        for n, g in zip(names, joined):
            res = _adamw(kernel_view(n, w[n]), g, kernel_view(n, m[n]), kernel_view(n, v[n]))
            g_out[n], delta[n], new_m[n], new_v[n] = (kernel_view(n, a) for a in (g,) + tuple(res))

    for n in BIG:
        for l in (range(DEPTH) if n in swiglu else [last]):
            reduced[n] = _sum_chips(sums[l, n][0], arrived[l, n], place, l, into=reduced.get(n), after=travelling)
    adamw_large(swiglu, _join_halves("join_swiglu", [reduced[n] for n in swiglu]))
    scatter_wait(sc_rest, [new_v[n] for n in swiglu] + [reduced[n] for n in BIG[:2]])
    for n in BIG[:2]:
        reduced[n] = _sum_chips(sums[0, n][0], arrived[0, n], place, 0, into=reduced[n])
    adamw_large(BIG[:2], _join_halves("join_rest", [reduced[n] for n in BIG[:2]]))
    *small_r, conv_r = _unpack(total, small_g + [conv_g])
    g_out.update(zip(SMALL, small_r))
    g_out["conv_w"] = lax.dynamic_slice_in_dim(conv_r, chip * conv_w.shape[2], conv_w.shape[2], axis=2)

    rest = SMALL + ("conv_w",)
    like = [w[n] for n in rest]
    d, nm, nv = _adamw(*[_pack([src[n] for n in rest])[None] for src in (w, g_out, m, v)])
    for dst, buf in ((delta, d), (new_m, nm), (new_v, nv)):
        dst.update(zip(rest, _unpack(buf[0], like)))

    return (loss, dx[None], *[g_out[n] for n in ORDER], *[delta[n] for n in ORDER], *[new_m[n] for n in ORDER],
            *[new_v[n] for n in ORDER])
```

```python
import functools

import jax
import jax.numpy as jnp
from jax import lax
from jax.experimental import pallas as pl
from jax.experimental.pallas import tpu as pltpu

F32 = jnp.float32
BF16 = jnp.bfloat16
MESH = pl.DeviceIdType.MESH
ANY = pl.BlockSpec(memory_space=pl.ANY)
HIGHEST = lax.Precision.HIGHEST

T = 2048
D = 1024
DEPTH = 2
NCHIP = 4
HEADS = 4
HD = 128
HW = HEADS * HD
CH = 64
GCH = 128
IN_DIM = 3080
NP = 3200
BA_OFF = 3072
FF_SH = 704
EPS = 1e-6
LANE = 128
VMEM_LIMIT = 56 * 1024 * 1024

ADAM_LR = 0.001
ADAM_B1 = 0.9
ADAM_B2 = 0.999
ADAM_EPS = 1e-08
ADAM_WD = 0.01
ADAM_STEP = 10


def _cparams(sem=None):
    return pltpu.CompilerParams(dimension_semantics=sem, vmem_limit_bytes=VMEM_LIMIT)


_DIMS = {"nn": (((1,), (0,)), ((), ())), "nt": (((1,), (1,)), ((), ())), "tn": (((0,), (0,)), ((), ()))}


def _mm(name, mode, a, bs, *, tm, tn, tk, out_dtypes=(F32,), reduce_g=False, resid=None, extras=(), epilogue=None,
        b_spec=None, n_n=None, after=(), fold_g=False, sub_m=1):
    assert sub_m == 1 or (mode != "tn" and tm % (8 * sub_m) == 0), (name, sub_m)
    nb = len(bs)
    ga = a.shape[0]
    gbs = [1 if b_spec is not None else b.shape[0] for b in bs]
    g_n = max([ga] + gbs)
    if mode == "tn":
        k_n, m_n = a.shape[1:]
    else:
        m_n, k_n = a.shape[1:]
    if n_n is None:
        n_n = bs[0].shape[1] if mode == "nt" else bs[0].shape[2]
    assert m_n % tm == 0 and n_n % tn == 0 and k_n % tk == 0, (name, m_n, n_n, k_n)
    mi, nj, kk = m_n // tm, n_n // tn, k_n // tk
    lead = g_n if fold_g else None
    if reduce_g:
        g_steps = 1 if fold_g else g_n
        grid = (mi, nj, g_steps, kk)
        ids = lambda i, j, g, k: (g, i, j, k)
        n_red = g_steps * kk
        red_idx = lambda: pl.program_id(2) * kk + pl.program_id(3)
        sem = ("parallel", "parallel", "arbitrary", "arbitrary")
    else:
        grid = (g_n, mi, nj, kk)
        ids = lambda g, i, j, k: (g, i, j, k)
        n_red = kk
        red_idx = lambda: pl.program_id(3)
        sem = ("parallel", "parallel", "parallel", "arbitrary")

    def pick(gsz, g):
        return g if gsz > 1 else 0

    def a_map(*p):
        g, i, j, k = ids(*p)
        return (pick(ga, g), k, i) if mode == "tn" else (pick(ga, g), i, k)

    def b_map(gsz):
        def f(*p):
            g, i, j, k = ids(*p)
            if b_spec is not None:
                return b_spec[1](g, i, j, k)
            return (pick(gsz, g), j, k) if mode == "nt" else (pick(gsz, g), k, j)
        return f

    def o_map(gsz):
        def f(*p):
            g, i, j, k = ids(*p)
            return (0 if reduce_g else pick(gsz, g), i, j)
        return f

    a_spec = pl.BlockSpec((lead, tk, tm) if mode == "tn" else (lead, tm, tk), a_map)
    b_block = b_spec[0] if b_spec is not None else ((lead, tn, tk) if mode == "nt" else (lead, tk, tn))
    b_specs = [pl.BlockSpec(b_block, b_map(gs)) for gs in gbs]
    x_specs = [pl.BlockSpec((None, tm, tn), o_map(e.shape[0])) for e in extras]
    r_specs = [pl.BlockSpec((None, tm, tn), o_map(resid.shape[0]))] if resid is not None else []
    g_out = 1 if reduce_g else g_n
    out_shape = [jax.ShapeDtypeStruct((g_out, m_n, n_n), dt) for dt in out_dtypes]
    out_specs = [pl.BlockSpec((None, tm, tn), o_map(g_out)) for _ in out_dtypes]
    nx, nr, no = len(extras), len(r_specs), len(out_dtypes)
    n_in = 1 + nb + nx + nr + len(after)
    dims = _DIMS[mode]

    def body(*refs):
        a_ref = refs[0]
        b_refs = refs[1:1 + nb]
        x_refs = refs[1 + nb:1 + nb + nx]
        r_refs = refs[1 + nb + nx:1 + nb + nx + nr]
        o_refs = refs[n_in:n_in + no]
        acc_refs = refs[n_in + no:]
        def dots(rows):
            if fold_g:
                return [sum(lax.dot_general(a_ref[g, rows, :], b_ref[g], dims, preferred_element_type=F32)
                            for g in range(g_n)) for b_ref in b_refs]
            av = a_ref[...] if mode == "tn" else a_ref[rows, :]
            return [lax.dot_general(av, b_ref[...], dims, preferred_element_type=F32) for b_ref in b_refs]

        def finish(accs, rows=slice(None)):
            if r_refs:
                accs[0] = accs[0] + r_refs[0][rows, :]
            outs = epilogue(accs, [x[rows, :] for x in x_refs]) if epilogue is not None else accs
            for o_ref, o in zip(o_refs, outs):
                o_ref[rows, :] = o.astype(o_ref.dtype)

        if n_red == 1:
            slabs = [slice(s * (tm // sub_m), (s + 1) * (tm // sub_m)) for s in range(sub_m)]
            ahead = dots(slabs[0])
            for s, rows in enumerate(slabs):
                now, ahead = ahead, (dots(slabs[s + 1]) if s + 1 < sub_m else None)
                finish(now, rows)
            return
        products = dots(slice(None))
        r = red_idx()
        for p, acc in zip(products, acc_refs):
            @pl.when(r == 0)
            def _():
                acc[...] = p

            @pl.when((r > 0) & (r < n_red - 1))
            def _():
                acc[...] += p

        @pl.when(r == n_red - 1)
        def _():
            finish([acc[...] + p for p, acc in zip(products, acc_refs)])

    return pl.pallas_call(
        body, name=name, grid=grid,
        in_specs=[a_spec] + b_specs + x_specs + r_specs + [ANY] * len(after),
        out_specs=out_specs, out_shape=out_shape,
        scratch_shapes=[pltpu.VMEM((tm, tn), F32) for _ in range(nb if n_red > 1 else 0)],
        compiler_params=_cparams(sem),
    )(a, *bs, *extras, *([resid] if resid is not None else []), *after)


def _sigmoid(x):
    return 1.0 / (1.0 + jnp.exp(-x))


def _silu(x):
    return x * _sigmoid(x)


def _gelu(x):
    return 0.5 * x * (1.0 + jnp.tanh(0.7978845608028654 * (x + 0.044715 * (x * x * x))))


def _rms_fn(h, gain):
    return h * lax.rsqrt(jnp.mean(h * h, axis=-1, keepdims=True) + EPS) * gain


def _shift_impl(x, s):
    n = x.shape[0]
    rolled = pltpu.roll(x, s % n, 0)
    row = lax.broadcasted_iota(jnp.int32, x.shape, 0)
    return jnp.where((row >= s) & (row < n + s), rolled, 0.0)


@functools.partial(jax.custom_vjp, nondiff_argnums=(1,))
def _shift(x, s):
    return _shift_impl(x, s)


def _shift_fwd(x, s):
    return _shift_impl(x, s), None


def _shift_bwd(s, _, g):
    return (_shift_impl(g, -s),)


_shift.defvjp(_shift_fwd, _shift_bwd)


def _prep_fn(x, w, qk_scale, is_v):
    y = x * w[3:4, :]
    for i in range(3):
        y = y + _shift(x, 3 - i) * w[i:i + 1, :]
    y = _silu(y)
    nrm = lax.rsqrt(jnp.sum(y * y, axis=-1, keepdims=True) + EPS) * qk_scale
    return y * jnp.where(is_v, 1.0, nrm)


def _softplus(x):
    return jnp.maximum(x, 0.0) + jnp.log(1.0 + jnp.exp(-jnp.abs(x)))


def _gates_fn(ba, a_log, dt_bias):
    lane = lax.broadcasted_iota(jnp.int32, ba.shape, 1)
    beta = _sigmoid(ba)
    g = -jnp.exp(a_log) * _softplus(ba + dt_bias)
    return jnp.where(lane < HEADS, beta, g)


def _dot16(a, b, dims=_DIMS["nn"]):
    return lax.dot_general(a.astype(BF16), b.astype(BF16), dims, preferred_element_type=F32)


def _dot32(a, b):
    return jnp.dot(a, b, preferred_element_type=F32, precision=HIGHEST)


def _dot3(a, b, dims=_DIMS["nn"]):
    return lax.dot_general(a, b, dims, preferred_element_type=F32, precision=lax.Precision.HIGH)


def _tri_inverses(mats):
    row = lax.broadcasted_iota(jnp.int32, (CH, CH), 0)
    col = lax.broadcasted_iota(jnp.int32, (CH, CH), 1)
    eye = (row == col).astype(F32)
    ts = [eye - a for a in mats]
    ps = list(mats)
    for _ in range(5):
        ps = [_dot3(p, p) for p in ps]
        ts = [t + _dot3(t, p) for t, p in zip(ts, ps)]
    return ts


@jax.custom_vjp
def _tri_solves(mats, rhs):
    return [_dot3(t, b) for t, b in zip(_tri_inverses(mats), rhs)]


def _tri_solves_fwd(mats, rhs):
    ts = _tri_inverses(mats)
    xs = [_dot3(t, b) for t, b in zip(ts, rhs)]
    return xs, (ts, xs)


def _tri_solves_bwd(res, dxs):
    ts, xs = res
    dbs = [_dot3(t, dx, _DIMS["tn"]) for t, dx in zip(ts, dxs)]
    return [-_dot3(db, x, _DIMS["nt"]) for db, x in zip(dbs, xs)], dbs


_tri_solves.defvjp(_tri_solves_fwd, _tri_solves_bwd)


def _chunk_prep_fn(xs, bgs):
    row = lax.broadcasted_iota(jnp.int32, (CH, CH), 0)
    col = lax.broadcasted_iota(jnp.int32, (CH, CH), 1)
    incl = row >= col
    strict = row > col
    lmat = incl.astype(F32)
    n = len(xs)
    items = [(i, h) for i in range(n) for h in range(HEADS)]
    part = lambda i, h, c: xs[i][:, c * HW + h * HD:c * HW + (h + 1) * HD]
    q = [part(i, h, 0) for i, h in items]
    k = [part(i, h, 1) for i, h in items]
    v = [part(i, h, 2) for i, h in items]
    beta = [bgs[i][:, h:h + 1] for i, h in items]
    gc_all = [_dot32(lmat, bg) for bg in bgs]
    gc = [gc_all[i][:, HEADS + h:HEADS + h + 1] for i, h in items]
    gmat = [jnp.where(strict, jnp.broadcast_to(bgs[i][:, HEADS + h:HEADS + h + 1], (CH, CH)), 0.0) for i, h in items]
    diff = [_dot3(lmat, m) for m in gmat]
    decay = [jnp.where(incl, jnp.exp(jnp.where(incl, d, 0.0)), 0.0) for d in diff]
    k_beta = [kk * b for kk, b in zip(k, beta)]
    kk_t = [_dot16(kb, kk, _DIMS["nt"]) for kb, kk in zip(k_beta, k)]
    qk_t = [_dot16(qq, kk, _DIMS["nt"]) for qq, kk in zip(q, k)]
    a = [jnp.where(strict, m * d, 0.0) for m, d in zip(kk_t, decay)]
    eg = [jnp.exp(g) for g in gc]
    rhs = [jnp.concatenate([vv * b, kb * e], axis=-1) for vv, b, kb, e in zip(v, beta, k_beta, eg)]
    uw = _tri_solves(a, rhs)
    qk = [m * d for m, d in zip(qk_t, decay)]
    g_last = [g[CH - 1:CH, :] for g in gc]
    qe = [qq * e for qq, e in zip(q, eg)]
    kd = [kk * jnp.exp(gl - g) for kk, gl, g in zip(k, g_last, gc)]
    egl = [jnp.broadcast_to(jnp.exp(gl), (1, HD)) for gl in g_last]
    out = []
    for i in range(n):
        mine = slice(i * HEADS, (i + 1) * HEADS)
        cat = lambda vals: jnp.concatenate(vals[mine], axis=-1)
        out.append((cat([x[:, :HD] for x in uw]), cat([x[:, HD:] for x in uw]), cat(qe), cat(kd),
                    jnp.concatenate([m[None] for m in qk[mine]], axis=0), cat(egl)))
    return out


def _chunk_state_fn(u, w, qe, kd, qk, egl, s):
    ws = [_dot16(a, b) for a, b in zip(w, s)]
    qs = [_dot16(a, b) for a, b in zip(qe, s)]
    v_new = [a - b for a, b in zip(u, ws)]
    o = [a + _dot16(b, c) for a, b, c in zip(qs, qk, v_new)]
    s_new = [a * e + _dot16(b, c, _DIMS["tn"]) for a, e, b, c in zip(s, egl, kd, v_new)]
    return o, s_new


def _mix_fn(o, z, ur, vr, ong, lng, lnb, ws, bst):
    row = lax.broadcasted_iota(jnp.int32, (GCH, GCH), 0)
    col = lax.broadcasted_iota(jnp.int32, (GCH, GCH), 1)
    causal = row >= col
    ug = _gelu(ur)
    vg = _gelu(vr)
    sls = [slice(h * HD, (h + 1) * HD) for h in range(HEADS)]
    oh = [o[:, sl] for sl in sls]
    oh = [x * lax.rsqrt(jnp.mean(x * x, axis=-1, keepdims=True) + EPS) for x in oh]
    outs_dn = [x * ong * _silu(z[:, sl]) for x, sl in zip(oh, sls)]
    vh = [vg[:, sl] for sl in sls]
    mu = [jnp.mean(x, axis=-1, keepdims=True) for x in vh]
    var = [jnp.mean(jnp.square(x - m), axis=-1, keepdims=True) for x, m in zip(vh, mu)]
    vn = [(x - m) * lax.rsqrt(s + EPS) * lng[:, sl] + lnb[:, sl] for x, m, s, sl in zip(vh, mu, var, sls)]
    mixed = [_dot16(jnp.where(causal, ws[h], 0.0), vn[h]) for h in range(HEADS)]
    outs_gm = [ug[:, sl] * (mixed[h] + bst[:, h:h + 1]) for h, sl in enumerate(sls)]
    return jnp.concatenate(outs_dn + outs_gm, axis=-1)


def _loss_fn(h, gain, tgt):
    y = _rms_fn(h, gain)
    return 0.5 * jnp.sum(jnp.mean(jnp.square(y - tgt), axis=-1))


RT = 256


def _rows(n=D):
    return pl.BlockSpec((RT, n), lambda i: (i, 0))


def _whole(shape):
    nd = len(shape)
    return pl.BlockSpec(shape, lambda i: (0,) * nd)


def _rmsnorm(name, h, gain):
    def body(h_ref, g_ref, o_ref):
        o_ref[...] = _rms_fn(h_ref[...], g_ref[...]).astype(BF16)

    return pl.pallas_call(
        body, name=name, grid=(T // RT,), in_specs=[_rows(), _whole((1, D))], out_specs=_rows(),
        out_shape=jax.ShapeDtypeStruct((T, D), BF16), compiler_params=_cparams(("parallel",)),
    )(h, gain)


def _rmsnorm_bwd(name, dhn, h, gain, resid):
    def body(dhn_ref, h_ref, g_ref, r_ref, dh_ref, dh16_ref, dg_ref):
        _, vjp = jax.vjp(_rms_fn, h_ref[...], g_ref[...])
        dh, dg = vjp(dhn_ref[...])
        dh = r_ref[...] + dh
        dh_ref[...] = dh
        dh16_ref[...] = dh.astype(BF16)

        @pl.when(pl.program_id(0) == 0)
        def _():
            dg_ref[...] = dg

        @pl.when(pl.program_id(0) > 0)
        def _():
            dg_ref[...] += dg

    return pl.pallas_call(
        body, name=name, grid=(T // RT,), in_specs=[_rows(), _rows(), _whole((1, D)), _rows()],
        out_specs=[_rows(), _rows(), _whole((1, D))],
        out_shape=[jax.ShapeDtypeStruct((T, D), F32), jax.ShapeDtypeStruct((T, D), BF16),
                   jax.ShapeDtypeStruct((1, D), F32)],
        compiler_params=_cparams(("arbitrary",)),
    )(dhn, h, gain, resid)


def _loss_head(h, gain, tgt):
    def body(h_ref, g_ref, t_ref, l_ref, dh_ref, dh16_ref, dg_ref):
        loss, vjp = jax.vjp(lambda hh, gg: _loss_fn(hh, gg, t_ref[...]), h_ref[...], g_ref[...])
        dh, dg = vjp(jnp.ones((), F32))
        dh_ref[...] = dh
        dh16_ref[...] = dh.astype(BF16)
        lv = jnp.full((1, LANE), loss, F32)

        @pl.when(pl.program_id(0) == 0)
        def _():
            dg_ref[...] = dg
            l_ref[...] = lv

        @pl.when(pl.program_id(0) > 0)
        def _():
            dg_ref[...] += dg
            l_ref[...] += lv

    return pl.pallas_call(
        body, name="loss_head", grid=(T // RT,), in_specs=[_rows(), _whole((1, D)), _rows()],
        out_specs=[_whole((1, LANE)), _rows(), _rows(), _whole((1, D))],
        out_shape=[jax.ShapeDtypeStruct((1, LANE), F32), jax.ShapeDtypeStruct((T, D), F32),
                   jax.ShapeDtypeStruct((T, D), BF16), jax.ShapeDtypeStruct((1, D), F32)],
        compiler_params=_cparams(("arbitrary",)),
    )(h, gain, tgt)


def _prep_flags():
    j = pl.program_id(0)
    qk_scale = jnp.where(j < HEADS, HD ** -0.5, 1.0).astype(F32)
    return qk_scale, j >= 2 * HEADS


def _prep(proj, conv_w):
    def body(x_ref, w_ref, o_ref):
        qk_scale, is_v = _prep_flags()
        o_ref[...] = _prep_fn(x_ref[...], w_ref[...], qk_scale, is_v)

    col = lambda j: (0, j)
    return pl.pallas_call(
        body, name="gdn_prep", grid=(3 * HEADS,),
        in_specs=[pl.BlockSpec((T, HD), col), pl.BlockSpec((4, HD), col)], out_specs=pl.BlockSpec((T, HD), col),
        out_shape=jax.ShapeDtypeStruct((T, 3 * HW), F32), compiler_params=_cparams(("parallel",)),
    )(proj, conv_w)


def _prep_bwd(proj, conv_w, dqkv, dproj):
    def body(x_ref, w_ref, d_ref, _, dx_ref, dw_ref):
        qk_scale, is_v = _prep_flags()
        _, vjp = jax.vjp(lambda x, w: _prep_fn(x, w, qk_scale, is_v), x_ref[...], w_ref[...])
        dx, dw = vjp(d_ref[...])
        dx_ref[...] = dx.astype(BF16)
        dw_ref[...] = dw

    col = lambda j: (0, j)
    return pl.pallas_call(
        body, name="gdn_prep_bwd", grid=(3 * HEADS,),
        in_specs=[pl.BlockSpec((T, HD), col), pl.BlockSpec((4, HD), col), pl.BlockSpec((T, HD), col), ANY],
        out_specs=[pl.BlockSpec((T, HD), col), pl.BlockSpec((4, HD), col)],
        out_shape=[jax.ShapeDtypeStruct((T, NP), BF16), jax.ShapeDtypeStruct((4, 3 * HW), F32)],
        input_output_aliases={3: 0}, compiler_params=_cparams(("parallel",)),
    )(proj, conv_w, dqkv, dproj)


BA_BLK = BA_OFF // LANE


def _gates(proj, a_log, dt_bias):
    def body(x_ref, a_ref, d_ref, o_ref):
        o_ref[...] = _gates_fn(x_ref[...], a_ref[...], d_ref[...])

    return pl.pallas_call(
        body, name="gdn_gates", grid=(1,),
        in_specs=[pl.BlockSpec((T, LANE), lambda i: (0, BA_BLK)), _whole((1, LANE)), _whole((1, LANE))],
        out_specs=_whole((T, LANE)),
        out_shape=jax.ShapeDtypeStruct((T, LANE), F32), compiler_params=_cparams(("arbitrary",)),
    )(proj, a_log, dt_bias)


def _gates_bwd(proj, a_log, dt_bias, dbg, dproj):
    def body(x_ref, a_ref, d_ref, dbg_ref, _, dx_ref, da_ref, dd_ref):
        _, vjp = jax.vjp(_gates_fn, x_ref[...], a_ref[...], d_ref[...])
        dx, da_ref[...], dd_ref[...] = vjp(dbg_ref[...])
        dx_ref[...] = dx.astype(BF16)

    ba = pl.BlockSpec((T, LANE), lambda i: (0, BA_BLK))
    return pl.pallas_call(
        body, name="gdn_gates_bwd", grid=(1,),
        in_specs=[ba, _whole((1, LANE)), _whole((1, LANE)), _whole((T, LANE)), ANY],
        out_specs=[ba, _whole((1, LANE)), _whole((1, LANE))],
        out_shape=[jax.ShapeDtypeStruct((T, NP), BF16), jax.ShapeDtypeStruct((1, LANE), F32),
                   jax.ShapeDtypeStruct((1, LANE), F32)],
        input_output_aliases={4: 0}, compiler_params=_cparams(("arbitrary",)),
    )(proj, a_log, dt_bias, dbg, dproj)


NCK = T // CH
CPS = 2


def _chunk_prep_specs(rev=False):
    at = (lambda n: NCK - 1 - n) if rev else (lambda n: n)
    wide = pl.BlockSpec((CH, HW), lambda n: (at(n), 0))
    return [wide, wide, wide, wide, pl.BlockSpec((HEADS, CH, CH), lambda n: (0, at(n), 0)),
            pl.BlockSpec((None, 1, HW), lambda n: (at(n), 0, 0))]


def _chunk_prep_shapes(dtypes):
    shp = [(T, HW), (T, HW), (T, HW), (T, HW), (HEADS, T, CH), (NCK, 1, HW)]
    return [jax.ShapeDtypeStruct(s, dt) for s, dt in zip(shp, dtypes)]


def _chunk_prep(qkv, bg):
    def body(x_ref, bg_ref, *o_refs):
        rows = [slice(ci * CH, (ci + 1) * CH) for ci in range(CPS)]
        res = _chunk_prep_fn([x_ref[r, :] for r in rows], [bg_ref[r, :] for r in rows])
        for ci, (u, w, qe, kd, qk, egl) in enumerate(res):
            for o_ref, val in zip(o_refs[:4], (u, w, qe, kd)):
                o_ref[rows[ci], :] = val.astype(o_ref.dtype)
            o_refs[4][:, rows[ci], :] = qk.astype(BF16)
            o_refs[5][ci] = egl

    wide = pl.BlockSpec((CPS * CH, HW), lambda n: (n, 0))
    return pl.pallas_call(
        body, name="gdn_chunk_prep", grid=(NCK // CPS,),
        in_specs=[pl.BlockSpec((CPS * CH, 3 * HW), lambda n: (n, 0)), pl.BlockSpec((CPS * CH, LANE), lambda n: (n, 0))],
        out_specs=[wide, wide, wide, wide, pl.BlockSpec((HEADS, CPS * CH, CH), lambda n: (0, n, 0)),
                   pl.BlockSpec((CPS, 1, HW), lambda n: (n, 0, 0))],
        out_shape=_chunk_prep_shapes((F32, BF16, BF16, BF16, BF16, F32)),
        compiler_params=_cparams(("parallel",)),
    )(qkv, bg)


def _chunk_prep_bwd(qkv, bg, cots):
    def body(x_ref, bg_ref, du, dw, dqe, dkd, dqk, degl, dx_ref, dbg_ref):
        rows = [slice(ci * CH, (ci + 1) * CH) for ci in range(CPS)]
        _, vjp = jax.vjp(_chunk_prep_fn, [x_ref[r, :] for r in rows], [bg_ref[r, :] for r in rows])
        dxs, dbgs = vjp([(du[r, :], dw[r, :], dqe[r, :], dkd[r, :], dqk[:, r, :], degl[ci])
                         for ci, r in enumerate(rows)])
        for r, dx, dbg in zip(rows, dxs, dbgs):
            dx_ref[r, :] = dx
            dbg_ref[r, :] = dbg

    wide = pl.BlockSpec((CPS * CH, HW), lambda n: (n, 0))
    return pl.pallas_call(
        body, name="gdn_chunk_prep_bwd", grid=(NCK // CPS,),
        in_specs=[pl.BlockSpec((CPS * CH, 3 * HW), lambda n: (n, 0)), pl.BlockSpec((CPS * CH, LANE), lambda n: (n, 0)),
                  wide, wide, wide, wide, pl.BlockSpec((HEADS, CPS * CH, CH), lambda n: (0, n, 0)),
                  pl.BlockSpec((CPS, 1, HW), lambda n: (n, 0, 0))],
        out_specs=[pl.BlockSpec((CPS * CH, 3 * HW), lambda n: (n, 0)), pl.BlockSpec((CPS * CH, LANE), lambda n: (n, 0))],
        out_shape=[jax.ShapeDtypeStruct((T, 3 * HW), F32), jax.ShapeDtypeStruct((T, LANE), F32)],
        compiler_params=_cparams(("parallel",)),
    )(qkv, bg, *cots)


def _head_args(refs):
    u, w, qe, kd, qk, egl = refs
    sls = [slice(h * HD, (h + 1) * HD) for h in range(HEADS)]
    return ([u[:, sl] for sl in sls], [w[:, sl].astype(F32) for sl in sls], [qe[:, sl].astype(F32) for sl in sls],
            [kd[:, sl].astype(F32) for sl in sls], [qk[h].astype(F32) for h in range(HEADS)],
            [egl[:, sl] for sl in sls])


def _chunk_scan(prep):
    def body(*refs):
        o_ref, sh_ref, s_ref = refs[6:]

        @pl.when(pl.program_id(0) == 0)
        def _():
            s_ref[...] = jnp.zeros_like(s_ref)

        s = [s_ref[h] for h in range(HEADS)]
        for h in range(HEADS):
            sh_ref[h, 0] = s[h]
        o, s_new = _chunk_state_fn(*_head_args(refs[:6]), s)
        for h in range(HEADS):
            o_ref[:, h * HD:(h + 1) * HD] = o[h]
            s_ref[h] = s_new[h]

    return pl.pallas_call(
        body, name="gdn_scan", grid=(NCK,), in_specs=_chunk_prep_specs(),
        out_specs=[pl.BlockSpec((CH, HW), lambda n: (n, 0)), pl.BlockSpec((HEADS, 1, HD, HD), lambda n: (0, n, 0, 0))],
        out_shape=[jax.ShapeDtypeStruct((T, HW), F32), jax.ShapeDtypeStruct((HEADS, NCK, HD, HD), F32)],
        scratch_shapes=[pltpu.VMEM((HEADS, HD, HD), F32)], compiler_params=_cparams(("arbitrary",)),
    )(*prep)


def _chunk_scan_bwd(prep, s_hist, do, after=()):
    n_in = 8 + len(after)

    def body(*refs):
        sh_ref, do_ref = refs[6:8]
        d_refs = refs[n_in:n_in + 6]
        ds_ref = refs[n_in + 6]

        @pl.when(pl.program_id(0) == 0)
        def _():
            ds_ref[...] = jnp.zeros_like(ds_ref)

        sls = [slice(h * HD, (h + 1) * HD) for h in range(HEADS)]
        _, vjp = jax.vjp(_chunk_state_fn, *_head_args(refs[:6]), [sh_ref[h, 0] for h in range(HEADS)])
        du, dw, dqe, dkd, dqk, degl, ds = vjp(([do_ref[:, sl] for sl in sls], [ds_ref[h] for h in range(HEADS)]))
        for h, sl in enumerate(sls):
            for d_ref, val in zip(d_refs[:4], (du, dw, dqe, dkd)):
                d_ref[:, sl] = val[h]
            d_refs[4][h] = dqk[h]
            d_refs[5][:, sl] = degl[h]
            ds_ref[h] = ds[h]

    rev = lambda n: NCK - 1 - n
    return pl.pallas_call(
        body, name="gdn_scan_bwd", grid=(NCK,),
        in_specs=_chunk_prep_specs(rev=True) + [pl.BlockSpec((HEADS, 1, HD, HD), lambda n: (0, rev(n), 0, 0)),
                                                pl.BlockSpec((CH, HW), lambda n: (rev(n), 0))] + [ANY] * len(after),
        out_specs=_chunk_prep_specs(rev=True), out_shape=_chunk_prep_shapes((F32,) * 6),
        scratch_shapes=[pltpu.VMEM((HEADS, HD, HD), F32)], compiler_params=_cparams(("arbitrary",)),
    )(*prep, s_hist, do, *after)


def _mix_specs():
    pc = lambda c: pl.BlockSpec((GCH, HW), lambda i: (i, c))
    return [pl.BlockSpec((GCH, HW), lambda i: (i, 0)), pc(3), pc(4), pc(5), _whole((1, HD)), _whole((1, HW)),
            _whole((1, HW)), _whole((HEADS, GCH, GCH)), _whole((GCH, LANE))]


def _mix(o, proj, ong, lng, lnb, ws, bst):
    def body(o_ref, z_ref, u_ref, v_ref, ong_ref, lng_ref, lnb_ref, ws_ref, bs_ref, m_ref):
        m_ref[...] = _mix_fn(o_ref[...], z_ref[...], u_ref[...], v_ref[...], ong_ref[...], lng_ref[...],
                             lnb_ref[...], ws_ref[...], bs_ref[...]).astype(BF16)

    return pl.pallas_call(
        body, name="mix", grid=(T // GCH,), in_specs=_mix_specs(),
        out_specs=pl.BlockSpec((GCH, D), lambda i: (i, 0)), out_shape=jax.ShapeDtypeStruct((T, D), BF16),
        compiler_params=_cparams(("parallel",)),
    )(o, proj, proj, proj, ong, lng, lnb, ws, bst)


def _mix_bwd(o, proj, ong, lng, lnb, ws, bst, dmix):
    def body(o_ref, z_ref, u_ref, v_ref, ong_ref, lng_ref, lnb_ref, ws_ref, bs_ref, dm_ref,
             do_ref, dzuv_ref, dong_ref, dlng_ref, dlnb_ref, dws_ref, dbs_ref):
        _, vjp = jax.vjp(_mix_fn, o_ref[...], z_ref[...], u_ref[...], v_ref[...], ong_ref[...], lng_ref[...],
                         lnb_ref[...], ws_ref[...], bs_ref[...])
        do, dz, du, dv, dong, dlng, dlnb, dws, dbs = vjp(dm_ref[...])
        do_ref[...] = do
        dzuv_ref[:, 0:HW] = dz.astype(BF16)
        dzuv_ref[:, HW:2 * HW] = du.astype(BF16)
        dzuv_ref[:, 2 * HW:3 * HW] = dv.astype(BF16)
        acc = [(dong_ref, dong), (dlng_ref, dlng), (dlnb_ref, dlnb), (dws_ref, dws), (dbs_ref, dbs)]

        @pl.when(pl.program_id(0) == 0)
        def _():
            for r, val in acc:
                r[...] = val

        @pl.when(pl.program_id(0) > 0)
        def _():
            for r, val in acc:
                r[...] += val

    shp = lambda *s: jax.ShapeDtypeStruct(s, F32)
    return pl.pallas_call(
        body, name="mix_bwd", grid=(T // GCH,),
        in_specs=_mix_specs() + [pl.BlockSpec((GCH, D), lambda i: (i, 0))],
        out_specs=[pl.BlockSpec((GCH, HW), lambda i: (i, 0)), pl.BlockSpec((GCH, 3 * HW), lambda i: (i, 1)),
                   _whole((1, HD)), _whole((1, HW)), _whole((1, HW)), _whole((HEADS, GCH, GCH)), _whole((GCH, LANE))],
        out_shape=[shp(T, HW), jax.ShapeDtypeStruct((T, NP), BF16), shp(1, HD), shp(1, HW), shp(1, HW),
                   shp(HEADS, GCH, GCH), shp(GCH, LANE)],
        compiler_params=_cparams(("arbitrary",)),
    )(o, proj, proj, proj, ong, lng, lnb, ws, bst, dmix)


def _swiglu_epilogue(accs, _):
    gate, up = accs
    return [gate, up, _silu(gate) * up]


def _swiglu_bwd_epilogue(accs, extras):
    dact = accs[0]
    gate, up = (e.astype(F32) for e in extras)
    sg = _sigmoid(gate)
    return [dact * up * (sg * (1.0 + gate * (1.0 - sg))), dact * (gate * sg)]


def _layer_fwd(h, p):
    hn = _rmsnorm("rms_mix", h, p["norm_mix"])
    proj = _mm("in_proj", "nn", hn[None], [p["w_in"][None]], tm=1024, tn=640, tk=D, sub_m=2)[0][0]
    qkv = _prep(proj, p["conv_w"])
    bg = _gates(proj, p["a_log"], p["dt_bias"])
    prep = _chunk_prep(qkv, bg)
    o, s_hist = _chunk_scan(prep)
    if "late" in p:
        p.update(p.pop("late")(o))
    mix = _mix(o, proj, p["o_norm_g"], p["ln_v_g"], p["ln_v_b"], p["w_s"], p["bst"])
    h1 = _mm("out_proj", "nn", mix[None], [p["w_out"].reshape(1, D, D)], tm=1024, tn=512, tk=D, resid=h[None],
             sub_m=2)[0][0]
    h2n = _rmsnorm("rms_ffn", h1, p["norm_ffn"])
    gate, up, act = _mm("ffn_in", "nt", h2n[None], [p["w_gate"], p["w_up"]], tm=1024, tn=FF_SH, tk=D,
                        out_dtypes=(BF16, BF16, BF16), epilogue=_swiglu_epilogue, sub_m=4)
    h2 = _mm("ffn_out", "nn", act, [p["w_down"]], tm=1024, tn=512, tk=FF_SH, reduce_g=True, fold_g=True,
             resid=h1[None], sub_m=2)[0][0]
    saved = dict(h=h, hn=hn, proj=proj, qkv=qkv, bg=bg, prep=prep, o=o, s_hist=s_hist, mix=mix, h1=h1, h2n=h2n,
                 gate=gate, up=up, act=act)
    return h2, saved


def _layer_bwd_ffn(dh2, dh2b, p, s, after=()):
    dh2b = dh2b[None]
    dgate, dup = _mm("ffn_out_bwd", "nt", dh2b, [p["w_down"]], tm=1024, tn=FF_SH, tk=D, out_dtypes=(BF16, BF16),
                     extras=(s["gate"], s["up"]), epilogue=_swiglu_bwd_epilogue, after=after, sub_m=4)
    dh2n = _mm("ffn_gate_bwd", "nn", dgate, [p["w_gate"]], tm=1024, tn=512, tk=FF_SH, reduce_g=True, fold_g=True,
               sub_m=2)[0]
    dh2n = _mm("ffn_up_bwd", "nn", dup, [p["w_up"]], tm=1024, tn=512, tk=FF_SH, reduce_g=True, fold_g=True,
               resid=dh2n, sub_m=2)[0][0]
    dh1, dh1b, d_norm_ffn = _rmsnorm_bwd("rms_ffn_bwd", dh2n, s["h1"], p["norm_ffn"], dh2)
    d_w_down = _mm("ffn_wdown_grad", "tn", s["act"], [dh2b], tm=FF_SH, tn=512, tk=T)[0]
    d_w_gate = _mm("ffn_wgate_grad", "tn", dgate, [s["h2n"][None]], tm=FF_SH, tn=512, tk=T)[0]
    d_w_up = _mm("ffn_wup_grad", "tn", dup, [s["h2n"][None]], tm=FF_SH, tn=512, tk=T)[0]
    return dh1, dh1b, dict(norm_ffn=d_norm_ffn, w_gate=d_w_gate, w_up=d_w_up, w_down=d_w_down)


def _layer_bwd_mixer(dh1, dh1b, p, s, after=(), midway=None, late=None):
    dh1b = dh1b[None]
    dmix = _mm("out_proj_bwd", "nt", dh1b, [p["w_out"].reshape(1, D, D)], tm=1024, tn=512, tk=D, after=after,
               sub_m=2)[0][0]
    d_w_out = _mm("out_proj_wgrad", "tn", s["mix"][None], [dh1b], tm=512, tn=512, tk=T)[0][0]
    do, dproj, d_ong, d_lng, d_lnb, d_ws, d_bst = _mix_bwd(
        s["o"], s["proj"], p["o_norm_g"], p["ln_v_g"], p["ln_v_b"], p["w_s"], p["bst"], dmix)
    then = midway(do) if midway is not None else ()
    dqkv, dbg = _chunk_prep_bwd(s["qkv"], s["bg"], _chunk_scan_bwd(s["prep"], s["s_hist"], do, then))
    dproj, d_conv = _prep_bwd(s["proj"], p["conv_w"], dqkv, dproj)
    dproj, d_a_log, d_dt_bias = _gates_bwd(s["proj"], p["a_log"], p["dt_bias"], dbg, dproj)
    dproj = dproj[None]
    d_w_in = _mm("in_proj_wgrad", "tn", s["hn"][None], [dproj], tm=512, tn=640, tk=T)[0]
    last = late(dict(w_in=d_w_in, w_out=d_w_out)) if late is not None else ()
    dhn = _mm("in_proj_bwd", "nt", dproj, [p["w_in"][None]], tm=1024, tn=512, tk=NP, after=last,
              sub_m=2)[0][0]
    dh, dhb, d_norm_mix = _rmsnorm_bwd("rms_mix_bwd", dhn, s["h"], p["norm_mix"], dh1)
    grads = dict(norm_mix=d_norm_mix, w_in=d_w_in, conv_w=d_conv, a_log=d_a_log, dt_bias=d_dt_bias, o_norm_g=d_ong,
                 ln_v_g=d_lng, ln_v_b=d_lnb, w_s=d_ws, bst=d_bst, w_out=d_w_out)
    return dh, dhb, grads


def _lanes(v, off=0):
    return jnp.zeros((1, LANE), F32).at[0, off:off + v.shape[0]].set(v)


def _w_in_pieces():
    regions = [(0, 2048, 0), (2048, 2056, BA_OFF), (2056, IN_DIM, 2048)]
    sh = IN_DIM // NCHIP
    out = []
    for j in range(NCHIP):
        for lo, hi, at in regions:
            a, b = max(lo, j * sh), min(hi, (j + 1) * sh)
            if a < b:
                out.append((j, a - j * sh, at + a - lo, b - a))
    return out


W_IN_PIECES = _w_in_pieces()
WT = 256


def _assemble_w_in(gathered, own, place):
    def body(place_ref, g_ref, own_ref, o_ref):
        o_ref[:, IN_DIM:] = jnp.zeros((WT, NP - IN_DIM), BF16)
        mine = own_ref[...]
        for j, src, dst, width in W_IN_PIECES:
            val = jnp.where(place_ref[0] == j, mine[:, src:src + width], g_ref[j, :, src:src + width])
            o_ref[:, dst:dst + width] = val

    sh = IN_DIM // NCHIP
    return pl.pallas_call(
        body, name="assemble_w_in",
        grid_spec=pltpu.PrefetchScalarGridSpec(
            num_scalar_prefetch=1, grid=(D // WT,),
            in_specs=[pl.BlockSpec((NCHIP, WT, sh), lambda i, place_ref: (0, i, 0)),
                      pl.BlockSpec((WT, sh), lambda i, place_ref: (i, 0))],
            out_specs=pl.BlockSpec((WT, NP), lambda i, place_ref: (i, 0))),
        out_shape=jax.ShapeDtypeStruct((D, NP), BF16), compiler_params=_cparams(("parallel",)),
    )(place, gathered, own)


def _layer_params(l, big, small):
    return dict(
        {k: v for k, v in big.items() if k != "conv_w"},
        conv_w=jnp.concatenate([big["conv_w"][j, l] for j in range(NCHIP)], axis=1),
        norm_mix=small["norm_mix"][l][None], norm_ffn=small["norm_ffn"][l][None],
        a_log=_lanes(small["a_log"][l], HEADS), dt_bias=_lanes(small["dt_bias"][l], HEADS),
        o_norm_g=small["o_norm_g"][l][None], ln_v_g=small["ln_v_g"][l][None], ln_v_b=small["ln_v_b"][l][None],
        w_s=small["w_s"][l],
        bst=jnp.pad(small["b_s"][l].T, ((0, 0), (0, LANE - HEADS))),
    )


def _reference_layout(g):
    return dict(
        w_in=g["w_in"],
        w_out=g["w_out"].reshape(NCHIP, D // NCHIP, D),
        w_gate=g["w_gate"], w_up=g["w_up"], w_down=g["w_down"],
        conv_w=g["conv_w"], norm_mix=g["norm_mix"][0], norm_ffn=g["norm_ffn"][0],
        a_log=g["a_log"][0, HEADS:2 * HEADS], dt_bias=g["dt_bias"][0, HEADS:2 * HEADS],
        o_norm_g=g["o_norm_g"][0], ln_v_g=g["ln_v_g"][0], ln_v_b=g["ln_v_b"][0], w_s=g["w_s"],
        b_s=g["bst"][:, :HEADS].T,
    )


def _forward(x, tgt, layers, norm_final):
    h = x
    saved, params = [], []
    for p in layers:
        p = p(h) if callable(p) else p
        h, s = _layer_fwd(h, p)
        saved.append(s)
        params.append(p)
    return (saved, params) + tuple(_loss_head(h, norm_final, tgt))


def _local_step(x, tgt, layers, norm_final):
    saved, layers, loss, dh, dhb, d_norm_final = _forward(x, tgt, layers, norm_final)
    grads = [None] * DEPTH
    for l in reversed(range(DEPTH)):
        dh1, dh1b, g_ffn = _layer_bwd_ffn(dh, dhb, layers[l], saved[l])
        dh, dhb, g_mix = _layer_bwd_mixer(dh1, dh1b, layers[l], saved[l])
        grads[l] = {**g_ffn, **g_mix}
    return loss, dh, grads, d_norm_final


def _place():
    x, y, c = lax.axis_index("x"), lax.axis_index("y"), lax.axis_index("c")
    return x, y, c, [(1 - x, y), (x, 1 - y), (1 - x, 1 - y)]


def _remote(src, dst, send_sem, recv_sem, to):
    return pltpu.make_async_remote_copy(src_ref=src, dst_ref=dst, send_sem=send_sem, recv_sem=recv_sem,
                                        device_id=to, device_id_type=MESH)


def _comm_call(name, body, ins, out_shape, n_sems, aliases=None):
    return pl.pallas_call(
        body, name=name, in_specs=[ANY] * len(ins), out_specs=[ANY] * len(out_shape), out_shape=out_shape,
        scratch_shapes=[pltpu.SemaphoreType.DMA((n,)) for n in n_sems], input_output_aliases=aliases or {},
        compiler_params=pltpu.CompilerParams(has_side_effects=True),
    )(*ins)


def _half_rows(ref, of_c, dim):
    hr = ref.shape[dim] // 2
    return pl.ds(pl.multiple_of(of_c * hr, BF16_ROWS), hr)


def _gather_plan(whole):
    def plan(srcs, lands):
        x, y, c, others = _place()
        chip = 2 * x + y
        out = []
        for src, land, all_of_it in zip(srcs, lands, whole):
            for ox, oy in others:
                if all_of_it:
                    out.append((src, land.at[chip], (ox, oy, c)))
                else:
                    out.append((src.at[_half_rows(src, c, 0)], land.at[chip, _half_rows(src, c, 0)], (ox, oy, c)))
        return out
    return plan


def _forward_halves(lands):
    n = len(lands)

    def body(*refs):
        outs = refs[n:2 * n]
        send_s, recv_s = refs[2 * n:]
        x, y, c, others = _place()
        sibling = (x, y, 1 - c)
        copies = []
        for a in range(n):
            for k, (ox, oy) in enumerate(others):
                mine = outs[a].at[2 * ox + oy, _half_rows(outs[a], c, 1)]
                copies.append(_remote(mine, mine, send_s.at[3 * a + k], recv_s.at[3 * a + k], sibling))
        for cp in copies:
            cp.start()
        for a in range(n):
            for k, (ox, oy) in enumerate(others):
                landed = outs[a].at[2 * ox + oy, _half_rows(outs[a], 1 - c, 1)]
                _remote(landed, landed, send_s.at[3 * a + k], recv_s.at[3 * a + k], sibling).wait_recv()
        for cp in copies:
            cp.wait_send()

    out_shape = [jax.ShapeDtypeStruct(g.shape, g.dtype) for g in lands]
    return _comm_call("forward_halves", body, lands, out_shape, [3 * n, 3 * n], aliases={a: a for a in range(n)})


HBM_SPEC = pl.BlockSpec(memory_space=pltpu.HBM)
SEM_SPEC = pl.BlockSpec(memory_space=pltpu.SEMAPHORE)
DATAFLOW = pltpu.SideEffectType.DATAFLOW_SIDE_EFFECTING


def _exchange_plan(srcs, lands):
    x, y, c, _ = _place()
    plan = []
    for src, land in zip(srcs, lands):
        hr = src.shape[1] // 2
        plan.append((src.at[:, pl.ds(pl.multiple_of((1 - c) * hr, 8), hr)], land, (x, y, 1 - c)))
    return plan


def _scatter_plan(srcs, lands):
    x, y, c, others = _place()
    return [(src.at[2 * ox + oy], land.at[k], (ox, oy, c))
            for src, land in zip(srcs, lands) for k, (ox, oy) in enumerate(others)]


def _split_start(name, plan, srcs, land_shapes, n_copies, after=()):
    n = len(srcs)
    lands = [pltpu.with_memory_space_constraint(lax.empty(s.shape, s.dtype), pltpu.HBM) for s in land_shapes]
    srcs = [pltpu.with_memory_space_constraint(s, pltpu.HBM) for s in srcs]

    def body(*refs):
        send_s, recv_s = refs[2 * n + len(after)], refs[2 * n + len(after) + 1]
        for i, (src, dst, to) in enumerate(plan(refs[:n], refs[n:2 * n])):
            _remote(src, dst, send_s.at[i], recv_s.at[i], to).start()
        refs[-1][...] = jnp.zeros_like(refs[-1])

    thru = [pltpu.HBM(s.shape, s.dtype) for s in srcs + lands]
    out = pl.pallas_call(
        body, name=name, in_specs=[HBM_SPEC] * (2 * n) + [ANY] * len(after),
        out_specs=[SEM_SPEC, SEM_SPEC] + [HBM_SPEC] * (2 * n) + [pl.BlockSpec(memory_space=pltpu.VMEM)],
        out_shape=[pltpu.SemaphoreType.DMA((n_copies,)), pltpu.SemaphoreType.DMA((n_copies,))] + thru
        + [jax.ShapeDtypeStruct((F32_ROWS, LANE), F32)],
        input_output_aliases={i: 2 + i for i in range(2 * n)},
        compiler_params=pltpu.CompilerParams(has_side_effects=DATAFLOW),
    )(*srcs, *lands, *after)
    return dict(sems=out[:2], srcs=out[2:2 + n], lands=out[2 + n:2 + 2 * n], token=out[-1])


def _split_wait(name, plan, started, after):
    n = len(started["srcs"])
    after = list(after) if isinstance(after, (list, tuple)) else [after]

    def body(*refs):
        send_s, recv_s = refs[2 * n], refs[2 * n + 1]
        for i, (src, dst, to) in enumerate(plan(refs[:n], refs[n:2 * n])):
            cp = _remote(src, dst, send_s.at[i], recv_s.at[i], to)
            cp.wait_send()
            cp.wait_recv()

    arrs = list(started["srcs"]) + list(started["lands"])
    out = pl.pallas_call(
        body, name=name, in_specs=[HBM_SPEC] * (2 * n) + [SEM_SPEC, SEM_SPEC] + [ANY] * len(after),
        out_specs=[HBM_SPEC] * (2 * n), out_shape=[pltpu.HBM(s.shape, s.dtype) for s in arrs],
        input_output_aliases={i: i for i in range(2 * n)},
        compiler_params=pltpu.CompilerParams(has_side_effects=DATAFLOW),
    )(*arrs, *started["sems"], *after)
    return out[:n], out[n:]


def _join_halves(name, rs):
    n = len(rs)

    def body(*refs):
        outs = refs[n:2 * n]
        send_s, recv_s = refs[2 * n:]
        x, y, c, _ = _place()
        sibling = (x, y, 1 - c)

        def half(a, of_c):
            hr = outs[a].shape[1] // 2
            return outs[a].at[:, pl.ds(pl.multiple_of(of_c * hr, 8), hr)]

        copies = [_remote(half(a, c), half(a, c), send_s.at[a], recv_s.at[a], sibling) for a in range(n)]
        for cp in copies:
            cp.start()
        for a in range(n):
            landed = half(a, 1 - c)
            _remote(landed, landed, send_s.at[a], recv_s.at[a], sibling).wait_recv()
        for cp in copies:
            cp.wait_send()

    out_shape = [jax.ShapeDtypeStruct(r.shape, r.dtype) for r in rs]
    return _comm_call(name, body, rs, out_shape, [n, n], aliases={a: a for a in range(n)})


def _allreduce_small(buf, after=()):
    r = buf.shape[0]
    hr = r // 2

    def body(in_ref, *refs):
        out_ref, theirs, by_chip, send_s, recv_s = refs[len(after):]
        x, y, c, others = _place()
        chip = 2 * x + y
        sibling = (x, y, 1 - c)
        mine = pl.ds(pl.multiple_of(c * hr, F32_ROWS), hr)
        swap = _remote(in_ref, theirs, send_s.at[0], recv_s.at[0], sibling)
        swap.start()
        swap.wait()
        by_chip[chip] = in_ref[mine, :] + theirs[mine, :]
        sends = [_remote(by_chip.at[chip], by_chip.at[chip], send_s.at[1 + k], recv_s.at[1 + k], (ox, oy, c))
                 for k, (ox, oy) in enumerate(others)]
        for cp in sends:
            cp.start()
        for k, (ox, oy) in enumerate(others):
            landed = by_chip.at[2 * ox + oy]
            _remote(landed, landed, send_s.at[1 + k], recv_s.at[1 + k], (ox, oy, c)).wait_recv()
        for cp in sends:
            cp.wait_send()
        out_ref[mine, :] = (by_chip[0] + by_chip[1]) + (by_chip[2] + by_chip[3])
        back = _remote(out_ref.at[mine], out_ref.at[mine], send_s.at[NCHIP], recv_s.at[NCHIP], sibling)
        back.start()
        other = out_ref.at[pl.ds(pl.multiple_of((1 - c) * hr, F32_ROWS), hr)]
        _remote(other, other, send_s.at[NCHIP], recv_s.at[NCHIP], sibling).wait_recv()
        back.wait_send()

    vm = pl.BlockSpec(memory_space=pltpu.VMEM)
    return pl.pallas_call(
        body, name="allreduce_small", in_specs=[vm] + [ANY] * len(after), out_specs=vm,
        out_shape=jax.ShapeDtypeStruct((r, LANE), F32),
        scratch_shapes=[pltpu.VMEM((r, LANE), F32), pltpu.VMEM((NCHIP, hr, LANE), F32),
                        pltpu.SemaphoreType.DMA((NCHIP + 1,)), pltpu.SemaphoreType.DMA((NCHIP + 1,))],
        compiler_params=pltpu.CompilerParams(has_side_effects=True, vmem_limit_bytes=VMEM_LIMIT),
    )(buf, *after)


MAX_ROW_TILE = 512
BF16_ROWS = 16


def _row_tile(rows):
    for t in range(min(rows, MAX_ROW_TILE) // BF16_ROWS * BF16_ROWS, 0, -BF16_ROWS):
        if rows % t == 0:
            return t
    raise ValueError(rows)


def _sum_halves(g, theirs, c_arr):
    nch, rows, cols = g.shape
    hr = rows // 2
    tr = _row_tile(hr)

    def body(c_ref, g_ref, t_ref, o_ref, ob_ref):
        s = g_ref[...] + t_ref[...]
        o_ref[...] = s
        ob_ref[...] = s.astype(BF16)

    blk = pl.BlockSpec((None, tr, cols), lambda j, i, c_ref: (j, i, 0))
    return pl.pallas_call(
        body, name="sum_halves",
        grid_spec=pltpu.PrefetchScalarGridSpec(
            num_scalar_prefetch=1, grid=(nch, hr // tr),
            in_specs=[pl.BlockSpec((None, None, tr, cols), lambda j, i, c_ref: (j, c_ref[0], i, 0)), blk],
            out_specs=[blk, blk]),
        out_shape=[jax.ShapeDtypeStruct((nch, hr, cols), F32), jax.ShapeDtypeStruct((nch, hr, cols), BF16)],
        compiler_params=_cparams(("parallel", "parallel")),
    )(c_arr, g.reshape(nch, 2, hr, cols), theirs)


def _sum_halves_w_in(g, theirs, c_arr):
    hr = D // 2
    sh = IN_DIM // NCHIP

    def body(c_ref, g_ref, t_ref, o_ref, ob_ref):
        s = g_ref[...] + t_ref[...]
        for j, dst, src, width in W_IN_PIECES:
            o_ref[j, :, dst:dst + width] = s[:, src:src + width]
            ob_ref[j, :, dst:dst + width] = s[:, src:src + width].astype(BF16)

    out = pl.BlockSpec((NCHIP, WT, sh), lambda i, c_ref: (0, i, 0))
    return pl.pallas_call(
        body, name="sum_halves_w_in",
        grid_spec=pltpu.PrefetchScalarGridSpec(
            num_scalar_prefetch=1, grid=(hr // WT,),
            in_specs=[pl.BlockSpec((None, WT, NP), lambda i, c_ref: (c_ref[0], i, 0)),
                      pl.BlockSpec((None, WT, NP), lambda i, c_ref: (0, i, 0))],
            out_specs=[out, out]),
        out_shape=[jax.ShapeDtypeStruct((NCHIP, hr, sh), F32), jax.ShapeDtypeStruct((NCHIP, hr, sh), BF16)],
        compiler_params=_cparams(("parallel",)),
    )(c_arr, g.reshape(2, hr, NP), theirs)


def _sum_chips(p, q, place, l, into=None, after=()):
    extra = ([into] if into is not None else []) + list(after)
    _, rows, cols = p.shape
    tr = _row_tile(rows)
    steps = rows // tr

    def body(place_ref, p_ref, q0, q1, q2, *rest):
        rest[-1][...] = ((p_ref[...] + q0[...].astype(F32)) + q1[...].astype(F32)) + q2[...].astype(F32)

    qs = lambda k: pl.BlockSpec((None, tr, cols), lambda i, place_ref: (k, i, 0))
    return pl.pallas_call(
        body, name="sum_chips",
        grid_spec=pltpu.PrefetchScalarGridSpec(
            num_scalar_prefetch=1, grid=(steps,),
            in_specs=[pl.BlockSpec((None, tr, cols), lambda i, place_ref: (place_ref[0], i, 0)), qs(0), qs(1), qs(2)]
            + [ANY] * len(extra),
            out_specs=pl.BlockSpec((None, tr, cols), lambda i, place_ref: (l, place_ref[1] * steps + i, 0))),
        out_shape=jax.ShapeDtypeStruct((DEPTH, 2 * rows, cols), F32),
        input_output_aliases={5: 0} if into is not None else {},
        compiler_params=_cparams(("parallel",)),
    )(place, p, q, q, q, *extra)


def _adamw(w, g, m, v):
    layers, rows, cols = w.shape
    tr = _row_tile(rows)

    def body(w_ref, g_ref, m_ref, v_ref, d_ref, nm_ref, nv_ref):
        gv = g_ref[...]
        nm = ADAM_B1 * m_ref[...] + (1.0 - ADAM_B1) * gv
        nv = ADAM_B2 * v_ref[...] + (1.0 - ADAM_B2) * jnp.square(gv)
        m_hat = nm / (1.0 - ADAM_B1 ** ADAM_STEP)
        v_hat = nv / (1.0 - ADAM_B2 ** ADAM_STEP)
        d_ref[...] = -ADAM_LR * (m_hat / (jnp.sqrt(v_hat) + ADAM_EPS) + ADAM_WD * w_ref[...])
        nm_ref[...] = nm
        nv_ref[...] = nv

    blk = pl.BlockSpec((None, tr, cols), lambda l, i: (l, i, 0))
    return pl.pallas_call(
        body, name="adamw", grid=(layers, rows // tr), in_specs=[blk] * 4, out_specs=[blk] * 3,
        out_shape=[jax.ShapeDtypeStruct(w.shape, F32)] * 3, compiler_params=_cparams(("parallel", "parallel")),
    )(w, g, m, v)


BIG = ("w_in", "w_out", "w_gate", "w_up", "w_down")
SMALL = ("norm_mix", "a_log", "dt_bias", "o_norm_g", "ln_v_g", "ln_v_b", "w_s", "b_s", "norm_ffn", "norm_final")
ORDER = ("norm_mix", "w_in", "conv_w", "a_log", "dt_bias", "o_norm_g", "ln_v_g", "ln_v_b", "w_s", "b_s", "w_out",
         "norm_ffn", "w_gate", "w_up", "w_down", "norm_final")


F32_ROWS = 8
PACK_ROWS = 128


def _lane_rows(size):
    return -(-size // (F32_ROWS * LANE)) * F32_ROWS


def _pack(arrs):
    parts = [jnp.pad(a.reshape(-1), (0, _lane_rows(a.size) * LANE - a.size)).reshape(-1, LANE) for a in arrs]
    rows = sum(p.shape[0] for p in parts)
    if rows % PACK_ROWS:
        parts.append(jnp.zeros((-rows % PACK_ROWS, LANE), F32))
    return jnp.concatenate(parts, axis=0)


def _unpack(buf, like):
    out, row = [], 0
    for a in like:
        n = _lane_rows(a.size)
        out.append(buf[row:row + n].reshape(-1)[:a.size].reshape(a.shape))
        row += n
    return out


def kernel(x, norm_mix, w_in, conv_w, a_log, dt_bias, o_norm_g, ln_v_g, ln_v_b, w_s, b_s, w_out, norm_ffn, w_gate, w_up, w_down, norm_final, loss_target, m_norm_mix, m_w_in, m_conv_w, m_a_log, m_dt_bias, m_o_norm_g, m_ln_v_g, m_ln_v_b, m_w_s, m_b_s, m_w_out, m_norm_ffn, m_w_gate, m_w_up, m_w_down, m_norm_final, v_norm_mix, v_w_in, v_conv_w, v_a_log, v_dt_bias, v_o_norm_g, v_ln_v_g, v_ln_v_b, v_w_s, v_b_s, v_w_out, v_norm_ffn, v_w_gate, v_w_up, v_w_down, v_norm_final):
    w = dict(norm_mix=norm_mix, w_in=w_in, conv_w=conv_w, a_log=a_log, dt_bias=dt_bias, o_norm_g=o_norm_g,
             ln_v_g=ln_v_g, ln_v_b=ln_v_b, w_s=w_s, b_s=b_s, w_out=w_out, norm_ffn=norm_ffn, w_gate=w_gate, w_up=w_up,
             w_down=w_down, norm_final=norm_final)
    m = dict(norm_mix=m_norm_mix, w_in=m_w_in, conv_w=m_conv_w, a_log=m_a_log, dt_bias=m_dt_bias, o_norm_g=m_o_norm_g,
             ln_v_g=m_ln_v_g, ln_v_b=m_ln_v_b, w_s=m_w_s, b_s=m_b_s, w_out=m_w_out, norm_ffn=m_norm_ffn,
             w_gate=m_w_gate, w_up=m_w_up, w_down=m_w_down, norm_final=m_norm_final)
    v = dict(norm_mix=v_norm_mix, w_in=v_w_in, conv_w=v_conv_w, a_log=v_a_log, dt_bias=v_dt_bias, o_norm_g=v_o_norm_g,
             ln_v_g=v_ln_v_g, ln_v_b=v_ln_v_b, w_s=v_w_s, b_s=v_b_s, w_out=v_w_out, norm_ffn=v_norm_ffn,
             w_gate=v_w_gate, w_up=v_w_up, w_down=v_w_down, norm_final=v_norm_final)
    chip = 2 * lax.axis_index("x") + lax.axis_index("y")
    place = jnp.stack([chip, lax.axis_index("c")]).astype(jnp.int32)
    c_arr = place[1:]

    def kernel_view(n, a):
        return jnp.swapaxes(a, 1, 2) if n in ("w_gate", "w_up") else a

    own = {n: [kernel_view(n, w[n])[l].astype(BF16) for l in range(DEPTH)] for n in BIG}
    by_chip = lambda a: jax.ShapeDtypeStruct((NCHIP,) + a.shape, a.dtype)

    def start(name, srcs, whole, after=()):
        return _split_start(name, _gather_plan(whole), srcs, [by_chip(a) for a in srcs], 3 * len(srcs), after)

    def finish(name, started, whole, after):
        srcs, lands = _split_wait(name, _gather_plan(whole), started, after)
        passed = iter(_forward_halves([g for g, all_of_it in zip(lands, whole) if not all_of_it]))
        lands = [g if all_of_it else next(passed) for g, all_of_it in zip(lands, whole)]
        return srcs, [lax.dynamic_update_index_in_dim(g, o, chip, 0) for g, o in zip(lands, srcs)]

    ffn = BIG[1:]
    first = start("gather_first_start", [own["w_in"][0], conv_w], [False, True])
    early = start("gather_early_start", [own[n][0] for n in ffn], [False] * len(ffn), [first["token"]])
    later = start("gather_later_start", [own[n][1] for n in BIG], [False] * len(BIG), [early["token"]])
    (own_w_in, _), (w_in_by_chip, conv_by_chip) = finish("gather_first_wait", first, [False, True], later["token"])

    def late(after):
        return dict(zip(ffn, finish("gather_early_wait", early, [False] * len(ffn), after)[1]))

    layer0 = _layer_params(0, dict(w_in=_assemble_w_in(w_in_by_chip, own_w_in, place), conv_w=conv_by_chip, late=late), w)

    def layer1(after):
        srcs, by = finish("gather_later_wait", later, [False] * len(BIG), after)
        big = dict(zip(ffn, by[1:]), w_in=_assemble_w_in(by[0], srcs[0], place), conv_w=conv_by_chip)
        return _layer_params(1, big, w)

    saved, layers, loss_lanes, dh, dhb, d_norm_final = _forward(x[0], loss_target[0], [layer0, layer1],
                                                                 norm_final[None])

    sums, arrived = {}, {}

    def exchange_start(tag, l, names, grads, after=()):
        mine = [grads[n] for n in names]
        shapes = [jax.ShapeDtypeStruct((g.shape[0], g.shape[1] // 2, g.shape[2]), F32) for g in mine]
        return tag, l, names, _split_start(f"exchange_{tag}_start", _exchange_plan, mine, shapes, len(mine), after)

    def add_halves(l, names, mine, theirs):
        for n, g, t in zip(names, mine, theirs):
            sums[l, n] = (_sum_halves_w_in if n == "w_in" else _sum_halves)(g, t, c_arr)

    def exchange_wait(handle, after):
        tag, l, names, started = handle
        add_halves(l, names, *_split_wait(f"exchange_{tag}_wait", _exchange_plan, started, after))

    def scatter_start(tag, l, names, after=()):
        partial = [sums[l, n][1] for n in names]
        shapes = [jax.ShapeDtypeStruct((3,) + p.shape[1:], p.dtype) for p in partial]
        return tag, l, names, _split_start(f"scatter_{tag}_start", _scatter_plan, partial, shapes, 3 * len(names), after)

    def scatter_wait(handle, after):
        tag, l, names, started = handle
        for n, q in zip(names, _split_wait(f"scatter_{tag}_wait", _scatter_plan, started, after)[1]):
            arrived[l, n] = q

    last = DEPTH - 1
    swiglu = BIG[2:]
    dh1, dh1b, g_ffn = _layer_bwd_ffn(dh, dhb, layers[last], saved[last])
    dh, dhb, g_mix = _layer_bwd_mixer(dh1, dh1b, layers[last], saved[last])
    gl = [None, _reference_layout({**g_ffn, **g_mix})]
    ex_last = exchange_start("last", last, BIG, gl[last])
    dh1, dh1b, g_ffn = _layer_bwd_ffn(dh, dhb, layers[0], saved[0], after=[ex_last[-1]["token"]])
    exchange_wait(ex_last, dh1)
    sc_last = scatter_start("last", last, BIG)
    ex_ffn = exchange_start("swiglu", 0, swiglu, g_ffn, [sc_last[-1]["token"]])
    sc_ffn = []

    def midway(do):
        exchange_wait(ex_ffn, do)
        sc_ffn.append(scatter_start("swiglu", 0, swiglu))
        return [sc_ffn[0][-1]["token"]]

    ex_rest = []

    def late(grads):
        rest_grads = dict(w_in=grads["w_in"], w_out=grads["w_out"].reshape(NCHIP, D // NCHIP, D))
        ex_rest.append(exchange_start("rest", 0, BIG[:2], rest_grads))
        return [ex_rest[0][-1]["token"]]

    dx, _, g_mix = _layer_bwd_mixer(dh1, dh1b, layers[0], saved[0], after=[ex_ffn[-1]["token"]], midway=midway,
                                    late=late)
    scatter_wait(sc_last, dx)
    scatter_wait(sc_ffn[0], dx)
    gl[0] = _reference_layout({**g_ffn, **g_mix})

    small_g = [jnp.stack([gl[l][n] for l in range(DEPTH)]) for n in SMALL[:-1]] + [d_norm_final[0]]
    conv_g = jnp.stack([gl[l]["conv_w"] for l in range(DEPTH)])
    summed = small_g + [conv_g, loss_lanes[0, :1]]
    total = _allreduce_small(_pack(summed))
    exchange_wait(ex_rest[0], total)
    sc_rest = scatter_start("rest", 0, BIG[:2])

    travelling = [sc_rest[-1]["token"]]
    reduced, g_out, delta, new_m, new_v = {}, {}, {}, {}, {}

    def adamw_large(names, joined):
        for n, g in zip(names, joined):
            res = _adamw(kernel_view(n, w[n]), g, kernel_view(n, m[n]), kernel_view(n, v[n]))
            g_out[n], delta[n], new_m[n], new_v[n] = (kernel_view(n, a) for a in (g,) + tuple(res))

    for n in BIG:
        for l in (range(DEPTH) if n in swiglu else [last]):
            reduced[n] = _sum_chips(sums[l, n][0], arrived[l, n], place, l, into=reduced.get(n), after=travelling)
    adamw_large(swiglu, _join_halves("join_swiglu", [reduced[n] for n in swiglu]))
    scatter_wait(sc_rest, [new_v[n] for n in swiglu] + [reduced[n] for n in BIG[:2]])
    for n in BIG[:2]:
        reduced[n] = _sum_chips(sums[0, n][0], arrived[0, n], place, 0, into=reduced[n])
    adamw_large(BIG[:2], _join_halves("join_rest", [reduced[n] for n in BIG[:2]]))
    *small_r, conv_r, loss = _unpack(total, summed)
    g_out.update(zip(SMALL, small_r))
    g_out["conv_w"] = lax.dynamic_slice_in_dim(conv_r, chip * conv_w.shape[2], conv_w.shape[2], axis=2)

    rest = SMALL + ("conv_w",)
    like = [w[n] for n in rest]
    d, nm, nv = _adamw(*[_pack([src[n] for n in rest])[None] for src in (w, g_out, m, v)])
    for dst, buf in ((delta, d), (new_m, nm), (new_v, nv)):
        dst.update(zip(rest, _unpack(buf[0], like)))

    return (loss[0], dx[None], *[g_out[n] for n in ORDER], *[delta[n] for n in ORDER], *[new_m[n] for n in ORDER],
            *[new_v[n] for n in ORDER])
```

```python
import functools

import jax
import jax.numpy as jnp
from jax import lax
from jax.experimental import pallas as pl
from jax.experimental.pallas import tpu as pltpu

F32 = jnp.float32
BF16 = jnp.bfloat16
MESH = pl.DeviceIdType.MESH
ANY = pl.BlockSpec(memory_space=pl.ANY)
HIGHEST = lax.Precision.HIGHEST

T = 2048
D = 1024
DEPTH = 2
NCHIP = 4
HEADS = 4
HD = 128
HW = HEADS * HD
CH = 64
GCH = 128
IN_DIM = 3080
NP = 3200
BA_OFF = 3072
FF_SH = 704
EPS = 1e-6
LANE = 128
VMEM_LIMIT = 56 * 1024 * 1024

ADAM_LR = 0.001
ADAM_B1 = 0.9
ADAM_B2 = 0.999
ADAM_EPS = 1e-08
ADAM_WD = 0.01
ADAM_STEP = 10


def _cparams(sem=None):
    return pltpu.CompilerParams(dimension_semantics=sem, vmem_limit_bytes=VMEM_LIMIT)


_DIMS = {"nn": (((1,), (0,)), ((), ())), "nt": (((1,), (1,)), ((), ())), "tn": (((0,), (0,)), ((), ()))}


def _mm(name, mode, a, bs, *, tm, tn, tk, out_dtypes=(F32,), reduce_g=False, resid=None, extras=(), epilogue=None,
        b_spec=None, n_n=None, after=(), fold_g=False, sub_m=1):
    assert sub_m == 1 or (mode != "tn" and tm % (8 * sub_m) == 0), (name, sub_m)
    nb = len(bs)
    ga = a.shape[0]
    gbs = [1 if b_spec is not None else b.shape[0] for b in bs]
    g_n = max([ga] + gbs)
    if mode == "tn":
        k_n, m_n = a.shape[1:]
    else:
        m_n, k_n = a.shape[1:]
    if n_n is None:
        n_n = bs[0].shape[1] if mode == "nt" else bs[0].shape[2]
    assert m_n % tm == 0 and n_n % tn == 0 and k_n % tk == 0, (name, m_n, n_n, k_n)
    mi, nj, kk = m_n // tm, n_n // tn, k_n // tk
    lead = g_n if fold_g else None
    if reduce_g:
        g_steps = 1 if fold_g else g_n
        grid = (mi, nj, g_steps, kk)
        ids = lambda i, j, g, k: (g, i, j, k)
        n_red = g_steps * kk
        red_idx = lambda: pl.program_id(2) * kk + pl.program_id(3)
        sem = ("parallel", "parallel", "arbitrary", "arbitrary")
    else:
        grid = (g_n, mi, nj, kk)
        ids = lambda g, i, j, k: (g, i, j, k)
        n_red = kk
        red_idx = lambda: pl.program_id(3)
        sem = ("parallel", "parallel", "parallel", "arbitrary")

    def pick(gsz, g):
        return g if gsz > 1 else 0

    def a_map(*p):
        g, i, j, k = ids(*p)
        return (pick(ga, g), k, i) if mode == "tn" else (pick(ga, g), i, k)

    def b_map(gsz):
        def f(*p):
            g, i, j, k = ids(*p)
            if b_spec is not None:
                return b_spec[1](g, i, j, k)
            return (pick(gsz, g), j, k) if mode == "nt" else (pick(gsz, g), k, j)
        return f

    def o_map(gsz):
        def f(*p):
            g, i, j, k = ids(*p)
            return (0 if reduce_g else pick(gsz, g), i, j)
        return f

    a_spec = pl.BlockSpec((lead, tk, tm) if mode == "tn" else (lead, tm, tk), a_map)
    b_block = b_spec[0] if b_spec is not None else ((lead, tn, tk) if mode == "nt" else (lead, tk, tn))
    b_specs = [pl.BlockSpec(b_block, b_map(gs)) for gs in gbs]
    x_specs = [pl.BlockSpec((None, tm, tn), o_map(e.shape[0])) for e in extras]
    r_specs = [pl.BlockSpec((None, tm, tn), o_map(resid.shape[0]))] if resid is not None else []
    g_out = 1 if reduce_g else g_n
    out_shape = [jax.ShapeDtypeStruct((g_out, m_n, n_n), dt) for dt in out_dtypes]
    out_specs = [pl.BlockSpec((None, tm, tn), o_map(g_out)) for _ in out_dtypes]
    nx, nr, no = len(extras), len(r_specs), len(out_dtypes)
    n_in = 1 + nb + nx + nr + len(after)
    dims = _DIMS[mode]

    def body(*refs):
        a_ref = refs[0]
        b_refs = refs[1:1 + nb]
        x_refs = refs[1 + nb:1 + nb + nx]
        r_refs = refs[1 + nb + nx:1 + nb + nx + nr]
        o_refs = refs[n_in:n_in + no]
        acc_refs = refs[n_in + no:]
        def dots(rows):
            if fold_g:
                return [sum(lax.dot_general(a_ref[g, rows, :], b_ref[g], dims, preferred_element_type=F32)
                            for g in range(g_n)) for b_ref in b_refs]
            av = a_ref[...] if mode == "tn" else a_ref[rows, :]
            return [lax.dot_general(av, b_ref[...], dims, preferred_element_type=F32) for b_ref in b_refs]

        def finish(accs, rows=slice(None)):
            if r_refs:
                accs[0] = accs[0] + r_refs[0][rows, :]
            outs = epilogue(accs, [x[rows, :] for x in x_refs]) if epilogue is not None else accs
            for o_ref, o in zip(o_refs, outs):
                o_ref[rows, :] = o.astype(o_ref.dtype)

        if n_red == 1:
            slabs = [slice(s * (tm // sub_m), (s + 1) * (tm // sub_m)) for s in range(sub_m)]
            ahead = dots(slabs[0])
            for s, rows in enumerate(slabs):
                now, ahead = ahead, (dots(slabs[s + 1]) if s + 1 < sub_m else None)
                finish(now, rows)
            return
        products = dots(slice(None))
        r = red_idx()
        for p, acc in zip(products, acc_refs):
            @pl.when(r == 0)
            def _():
                acc[...] = p

            @pl.when((r > 0) & (r < n_red - 1))
            def _():
                acc[...] += p

        @pl.when(r == n_red - 1)
        def _():
            finish([acc[...] + p for p, acc in zip(products, acc_refs)])

    return pl.pallas_call(
        body, name=name, grid=grid,
        in_specs=[a_spec] + b_specs + x_specs + r_specs + [ANY] * len(after),
        out_specs=out_specs, out_shape=out_shape,
        scratch_shapes=[pltpu.VMEM((tm, tn), F32) for _ in range(nb if n_red > 1 else 0)],
        compiler_params=_cparams(sem),
    )(a, *bs, *extras, *([resid] if resid is not None else []), *after)


def _sigmoid(x):
    return 1.0 / (1.0 + jnp.exp(-x))


def _silu(x):
    return x * _sigmoid(x)


def _gelu(x):
    return 0.5 * x * (1.0 + jnp.tanh(0.7978845608028654 * (x + 0.044715 * (x * x * x))))


def _rms_fn(h, gain):
    return h * lax.rsqrt(jnp.mean(h * h, axis=-1, keepdims=True) + EPS) * gain


def _shift_impl(x, s):
    n = x.shape[0]
    rolled = pltpu.roll(x, s % n, 0)
    row = lax.broadcasted_iota(jnp.int32, x.shape, 0)
    return jnp.where((row >= s) & (row < n + s), rolled, 0.0)


@functools.partial(jax.custom_vjp, nondiff_argnums=(1,))
def _shift(x, s):
    return _shift_impl(x, s)


def _shift_fwd(x, s):
    return _shift_impl(x, s), None


def _shift_bwd(s, _, g):
    return (_shift_impl(g, -s),)


_shift.defvjp(_shift_fwd, _shift_bwd)


def _prep_fn(x, w, qk_scale, is_v):
    y = x * w[3:4, :]
    for i in range(3):
        y = y + _shift(x, 3 - i) * w[i:i + 1, :]
    y = _silu(y)
    nrm = lax.rsqrt(jnp.sum(y * y, axis=-1, keepdims=True) + EPS) * qk_scale
    return y * jnp.where(is_v, 1.0, nrm)


def _softplus(x):
    return jnp.maximum(x, 0.0) + jnp.log(1.0 + jnp.exp(-jnp.abs(x)))


def _gates_fn(ba, a_log, dt_bias):
    lane = lax.broadcasted_iota(jnp.int32, ba.shape, 1)
    beta = _sigmoid(ba)
    g = -jnp.exp(a_log) * _softplus(ba + dt_bias)
    return jnp.where(lane < HEADS, beta, g)


def _dot16(a, b, dims=_DIMS["nn"]):
    return lax.dot_general(a.astype(BF16), b.astype(BF16), dims, preferred_element_type=F32)


def _dot32(a, b):
    return jnp.dot(a, b, preferred_element_type=F32, precision=HIGHEST)


def _dot3(a, b, dims=_DIMS["nn"]):
    return lax.dot_general(a, b, dims, preferred_element_type=F32, precision=lax.Precision.HIGH)


def _tri_inverses(mats):
    row = lax.broadcasted_iota(jnp.int32, (CH, CH), 0)
    col = lax.broadcasted_iota(jnp.int32, (CH, CH), 1)
    eye = (row == col).astype(F32)
    ts = [eye - a for a in mats]
    ps = list(mats)
    for _ in range(5):
        ps = [_dot3(p, p) for p in ps]
        ts = [t + _dot3(t, p) for t, p in zip(ts, ps)]
    return ts


@jax.custom_vjp
def _tri_solves(mats, rhs):
    return [_dot3(t, b) for t, b in zip(_tri_inverses(mats), rhs)]


def _tri_solves_fwd(mats, rhs):
    ts = _tri_inverses(mats)
    xs = [_dot3(t, b) for t, b in zip(ts, rhs)]
    return xs, (ts, xs)


def _tri_solves_bwd(res, dxs):
    ts, xs = res
    dbs = [_dot3(t, dx, _DIMS["tn"]) for t, dx in zip(ts, dxs)]
    return [-_dot3(db, x, _DIMS["nt"]) for db, x in zip(dbs, xs)], dbs


_tri_solves.defvjp(_tri_solves_fwd, _tri_solves_bwd)


def _chunk_prep_fn(xs, bgs):
    row = lax.broadcasted_iota(jnp.int32, (CH, CH), 0)
    col = lax.broadcasted_iota(jnp.int32, (CH, CH), 1)
    incl = row >= col
    strict = row > col
    lmat = incl.astype(F32)
    n = len(xs)
    items = [(i, h) for i in range(n) for h in range(HEADS)]
    part = lambda i, h, c: xs[i][:, c * HW + h * HD:c * HW + (h + 1) * HD]
    q = [part(i, h, 0) for i, h in items]
    k = [part(i, h, 1) for i, h in items]
    v = [part(i, h, 2) for i, h in items]
    beta = [bgs[i][:, h:h + 1] for i, h in items]
    gc_all = [_dot32(lmat, bg) for bg in bgs]
    gc = [gc_all[i][:, HEADS + h:HEADS + h + 1] for i, h in items]
    gmat = [jnp.where(strict, jnp.broadcast_to(bgs[i][:, HEADS + h:HEADS + h + 1], (CH, CH)), 0.0) for i, h in items]
    diff = [_dot3(lmat, m) for m in gmat]
    decay = [jnp.where(incl, jnp.exp(jnp.where(incl, d, 0.0)), 0.0) for d in diff]
    k_beta = [kk * b for kk, b in zip(k, beta)]
    kk_t = [_dot16(kb, kk, _DIMS["nt"]) for kb, kk in zip(k_beta, k)]
    qk_t = [_dot16(qq, kk, _DIMS["nt"]) for qq, kk in zip(q, k)]
    a = [jnp.where(strict, m * d, 0.0) for m, d in zip(kk_t, decay)]
    eg = [jnp.exp(g) for g in gc]
    rhs = [jnp.concatenate([vv * b, kb * e], axis=-1) for vv, b, kb, e in zip(v, beta, k_beta, eg)]
    uw = _tri_solves(a, rhs)
    qk = [m * d for m, d in zip(qk_t, decay)]
    g_last = [g[CH - 1:CH, :] for g in gc]
    qe = [qq * e for qq, e in zip(q, eg)]
    kd = [kk * jnp.exp(gl - g) for kk, gl, g in zip(k, g_last, gc)]
    egl = [jnp.broadcast_to(jnp.exp(gl), (1, HD)) for gl in g_last]
    out = []
    for i in range(n):
        mine = slice(i * HEADS, (i + 1) * HEADS)
        cat = lambda vals: jnp.concatenate(vals[mine], axis=-1)
        out.append((cat([x[:, :HD] for x in uw]), cat([x[:, HD:] for x in uw]), cat(qe), cat(kd),
                    jnp.concatenate([m[None] for m in qk[mine]], axis=0), cat(egl)))
    return out


def _chunk_state_fn(u, w, qe, kd, qk, egl, s):
    ws = [_dot16(a, b) for a, b in zip(w, s)]
    qs = [_dot16(a, b) for a, b in zip(qe, s)]
    v_new = [a - b for a, b in zip(u, ws)]
    o = [a + _dot16(b, c) for a, b, c in zip(qs, qk, v_new)]
    s_new = [a * e + _dot16(b, c, _DIMS["tn"]) for a, e, b, c in zip(s, egl, kd, v_new)]
    return o, s_new


def _mix_fn(o, z, ur, vr, ong, lng, lnb, ws, bst):
    row = lax.broadcasted_iota(jnp.int32, (GCH, GCH), 0)
    col = lax.broadcasted_iota(jnp.int32, (GCH, GCH), 1)
    causal = row >= col
    ug = _gelu(ur)
    vg = _gelu(vr)
    sls = [slice(h * HD, (h + 1) * HD) for h in range(HEADS)]
    oh = [o[:, sl] for sl in sls]
    oh = [x * lax.rsqrt(jnp.mean(x * x, axis=-1, keepdims=True) + EPS) for x in oh]
    outs_dn = [x * ong * _silu(z[:, sl]) for x, sl in zip(oh, sls)]
    vh = [vg[:, sl] for sl in sls]
    mu = [jnp.mean(x, axis=-1, keepdims=True) for x in vh]
    var = [jnp.mean(jnp.square(x - m), axis=-1, keepdims=True) for x, m in zip(vh, mu)]
    vn = [(x - m) * lax.rsqrt(s + EPS) * lng[:, sl] + lnb[:, sl] for x, m, s, sl in zip(vh, mu, var, sls)]
    mixed = [_dot16(jnp.where(causal, ws[h], 0.0), vn[h]) for h in range(HEADS)]
    outs_gm = [ug[:, sl] * (mixed[h] + bst[:, h:h + 1]) for h, sl in enumerate(sls)]
    return jnp.concatenate(outs_dn + outs_gm, axis=-1)


def _loss_fn(h, gain, tgt):
    y = _rms_fn(h, gain)
    return 0.5 * jnp.sum(jnp.mean(jnp.square(y - tgt), axis=-1))


RT = 256


def _rows(n=D):
    return pl.BlockSpec((RT, n), lambda i: (i, 0))


def _whole(shape):
    nd = len(shape)
    return pl.BlockSpec(shape, lambda i: (0,) * nd)


def _rmsnorm(name, h, gain):
    def body(h_ref, g_ref, o_ref):
        o_ref[...] = _rms_fn(h_ref[...], g_ref[...]).astype(BF16)

    return pl.pallas_call(
        body, name=name, grid=(T // RT,), in_specs=[_rows(), _whole((1, D))], out_specs=_rows(),
        out_shape=jax.ShapeDtypeStruct((T, D), BF16), compiler_params=_cparams(("parallel",)),
    )(h, gain)


def _rmsnorm_bwd(name, dhn, h, gain, resid):
    def body(dhn_ref, h_ref, g_ref, r_ref, dh_ref, dh16_ref, dg_ref):
        _, vjp = jax.vjp(_rms_fn, h_ref[...], g_ref[...])
        dh, dg = vjp(dhn_ref[...])
        dh = r_ref[...] + dh
        dh_ref[...] = dh
        dh16_ref[...] = dh.astype(BF16)

        @pl.when(pl.program_id(0) == 0)
        def _():
            dg_ref[...] = dg

        @pl.when(pl.program_id(0) > 0)
        def _():
            dg_ref[...] += dg

    return pl.pallas_call(
        body, name=name, grid=(T // RT,), in_specs=[_rows(), _rows(), _whole((1, D)), _rows()],
        out_specs=[_rows(), _rows(), _whole((1, D))],
        out_shape=[jax.ShapeDtypeStruct((T, D), F32), jax.ShapeDtypeStruct((T, D), BF16),
                   jax.ShapeDtypeStruct((1, D), F32)],
        compiler_params=_cparams(("arbitrary",)),
    )(dhn, h, gain, resid)


def _loss_head(h, gain, tgt):
    def body(h_ref, g_ref, t_ref, l_ref, dh_ref, dh16_ref, dg_ref):
        loss, vjp = jax.vjp(lambda hh, gg: _loss_fn(hh, gg, t_ref[...]), h_ref[...], g_ref[...])
        dh, dg = vjp(jnp.ones((), F32))
        dh_ref[...] = dh
        dh16_ref[...] = dh.astype(BF16)
        lv = jnp.full((1, LANE), loss, F32)

        @pl.when(pl.program_id(0) == 0)
        def _():
            dg_ref[...] = dg
            l_ref[...] = lv

        @pl.when(pl.program_id(0) > 0)
        def _():
            dg_ref[...] += dg
            l_ref[...] += lv

    return pl.pallas_call(
        body, name="loss_head", grid=(T // RT,), in_specs=[_rows(), _whole((1, D)), _rows()],
        out_specs=[_whole((1, LANE)), _rows(), _rows(), _whole((1, D))],
        out_shape=[jax.ShapeDtypeStruct((1, LANE), F32), jax.ShapeDtypeStruct((T, D), F32),
                   jax.ShapeDtypeStruct((T, D), BF16), jax.ShapeDtypeStruct((1, D), F32)],
        compiler_params=_cparams(("arbitrary",)),
    )(h, gain, tgt)


def _prep_flags():
    j = pl.program_id(0)
    qk_scale = jnp.where(j < HEADS, HD ** -0.5, 1.0).astype(F32)
    return qk_scale, j >= 2 * HEADS


def _prep(proj, conv_w):
    def body(x_ref, w_ref, o_ref):
        qk_scale, is_v = _prep_flags()
        o_ref[...] = _prep_fn(x_ref[...], w_ref[...], qk_scale, is_v)

    col = lambda j: (0, j)
    return pl.pallas_call(
        body, name="gdn_prep", grid=(3 * HEADS,),
        in_specs=[pl.BlockSpec((T, HD), col), pl.BlockSpec((4, HD), col)], out_specs=pl.BlockSpec((T, HD), col),
        out_shape=jax.ShapeDtypeStruct((T, 3 * HW), F32), compiler_params=_cparams(("parallel",)),
    )(proj, conv_w)


def _prep_bwd(proj, conv_w, dqkv, dproj):
    def body(x_ref, w_ref, d_ref, _, dx_ref, dw_ref):
        qk_scale, is_v = _prep_flags()
        _, vjp = jax.vjp(lambda x, w: _prep_fn(x, w, qk_scale, is_v), x_ref[...], w_ref[...])
        dx, dw = vjp(d_ref[...])
        dx_ref[...] = dx.astype(BF16)
        dw_ref[...] = dw

    col = lambda j: (0, j)
    return pl.pallas_call(
        body, name="gdn_prep_bwd", grid=(3 * HEADS,),
        in_specs=[pl.BlockSpec((T, HD), col), pl.BlockSpec((4, HD), col), pl.BlockSpec((T, HD), col), ANY],
        out_specs=[pl.BlockSpec((T, HD), col), pl.BlockSpec((4, HD), col)],
        out_shape=[jax.ShapeDtypeStruct((T, NP), BF16), jax.ShapeDtypeStruct((4, 3 * HW), F32)],
        input_output_aliases={3: 0}, compiler_params=_cparams(("parallel",)),
    )(proj, conv_w, dqkv, dproj)


BA_BLK = BA_OFF // LANE


def _gates(proj, a_log, dt_bias):
    def body(x_ref, a_ref, d_ref, o_ref):
        o_ref[...] = _gates_fn(x_ref[...], a_ref[...], d_ref[...])

    return pl.pallas_call(
        body, name="gdn_gates", grid=(1,),
        in_specs=[pl.BlockSpec((T, LANE), lambda i: (0, BA_BLK)), _whole((1, LANE)), _whole((1, LANE))],
        out_specs=_whole((T, LANE)),
        out_shape=jax.ShapeDtypeStruct((T, LANE), F32), compiler_params=_cparams(("arbitrary",)),
    )(proj, a_log, dt_bias)


def _gates_bwd(proj, a_log, dt_bias, dbg, dproj):
    def body(x_ref, a_ref, d_ref, dbg_ref, _, dx_ref, da_ref, dd_ref):
        _, vjp = jax.vjp(_gates_fn, x_ref[...], a_ref[...], d_ref[...])
        dx, da_ref[...], dd_ref[...] = vjp(dbg_ref[...])
        dx_ref[...] = dx.astype(BF16)

    ba = pl.BlockSpec((T, LANE), lambda i: (0, BA_BLK))
    return pl.pallas_call(
        body, name="gdn_gates_bwd", grid=(1,),
        in_specs=[ba, _whole((1, LANE)), _whole((1, LANE)), _whole((T, LANE)), ANY],
        out_specs=[ba, _whole((1, LANE)), _whole((1, LANE))],
        out_shape=[jax.ShapeDtypeStruct((T, NP), BF16), jax.ShapeDtypeStruct((1, LANE), F32),
                   jax.ShapeDtypeStruct((1, LANE), F32)],
        input_output_aliases={4: 0}, compiler_params=_cparams(("arbitrary",)),
    )(proj, a_log, dt_bias, dbg, dproj)


NCK = T // CH
CPS = 2


def _chunk_prep_specs(rev=False):
    at = (lambda n: NCK - 1 - n) if rev else (lambda n: n)
    wide = pl.BlockSpec((CH, HW), lambda n: (at(n), 0))
    return [wide, wide, wide, wide, pl.BlockSpec((HEADS, CH, CH), lambda n: (0, at(n), 0)),
            pl.BlockSpec((None, 1, HW), lambda n: (at(n), 0, 0))]


def _chunk_prep_shapes(dtypes):
    shp = [(T, HW), (T, HW), (T, HW), (T, HW), (HEADS, T, CH), (NCK, 1, HW)]
    return [jax.ShapeDtypeStruct(s, dt) for s, dt in zip(shp, dtypes)]


def _chunk_prep(qkv, bg):
    def body(x_ref, bg_ref, *o_refs):
        rows = [slice(ci * CH, (ci + 1) * CH) for ci in range(CPS)]
        res = _chunk_prep_fn([x_ref[r, :] for r in rows], [bg_ref[r, :] for r in rows])
        for ci, (u, w, qe, kd, qk, egl) in enumerate(res):
            for o_ref, val in zip(o_refs[:4], (u, w, qe, kd)):
                o_ref[rows[ci], :] = val.astype(o_ref.dtype)
            o_refs[4][:, rows[ci], :] = qk.astype(BF16)
            o_refs[5][ci] = egl

    wide = pl.BlockSpec((CPS * CH, HW), lambda n: (n, 0))
    return pl.pallas_call(
        body, name="gdn_chunk_prep", grid=(NCK // CPS,),
        in_specs=[pl.BlockSpec((CPS * CH, 3 * HW), lambda n: (n, 0)), pl.BlockSpec((CPS * CH, LANE), lambda n: (n, 0))],
        out_specs=[wide, wide, wide, wide, pl.BlockSpec((HEADS, CPS * CH, CH), lambda n: (0, n, 0)),
                   pl.BlockSpec((CPS, 1, HW), lambda n: (n, 0, 0))],
        out_shape=_chunk_prep_shapes((F32, BF16, BF16, BF16, BF16, F32)),
        compiler_params=_cparams(("parallel",)),
    )(qkv, bg)


def _chunk_prep_bwd(qkv, bg, cots):
    def body(x_ref, bg_ref, du, dw, dqe, dkd, dqk, degl, dx_ref, dbg_ref):
        rows = [slice(ci * CH, (ci + 1) * CH) for ci in range(CPS)]
        _, vjp = jax.vjp(_chunk_prep_fn, [x_ref[r, :] for r in rows], [bg_ref[r, :] for r in rows])
        dxs, dbgs = vjp([(du[r, :], dw[r, :], dqe[r, :], dkd[r, :], dqk[:, r, :], degl[ci])
                         for ci, r in enumerate(rows)])
        for r, dx, dbg in zip(rows, dxs, dbgs):
            dx_ref[r, :] = dx
            dbg_ref[r, :] = dbg

    wide = pl.BlockSpec((CPS * CH, HW), lambda n: (n, 0))
    return pl.pallas_call(
        body, name="gdn_chunk_prep_bwd", grid=(NCK // CPS,),
        in_specs=[pl.BlockSpec((CPS * CH, 3 * HW), lambda n: (n, 0)), pl.BlockSpec((CPS * CH, LANE), lambda n: (n, 0)),
                  wide, wide, wide, wide, pl.BlockSpec((HEADS, CPS * CH, CH), lambda n: (0, n, 0)),
                  pl.BlockSpec((CPS, 1, HW), lambda n: (n, 0, 0))],
        out_specs=[pl.BlockSpec((CPS * CH, 3 * HW), lambda n: (n, 0)), pl.BlockSpec((CPS * CH, LANE), lambda n: (n, 0))],
        out_shape=[jax.ShapeDtypeStruct((T, 3 * HW), F32), jax.ShapeDtypeStruct((T, LANE), F32)],
        compiler_params=_cparams(("parallel",)),
    )(qkv, bg, *cots)


def _head_args(refs):
    u, w, qe, kd, qk, egl = refs
    sls = [slice(h * HD, (h + 1) * HD) for h in range(HEADS)]
    return ([u[:, sl] for sl in sls], [w[:, sl].astype(F32) for sl in sls], [qe[:, sl].astype(F32) for sl in sls],
            [kd[:, sl].astype(F32) for sl in sls], [qk[h].astype(F32) for h in range(HEADS)],
            [egl[:, sl] for sl in sls])


def _chunk_scan(prep, after=()):
    def body(*refs):
        o_ref, sh_ref, s_ref = refs[6 + len(after):]

        @pl.when(pl.program_id(0) == 0)
        def _():
            s_ref[...] = jnp.zeros_like(s_ref)

        s = [s_ref[h] for h in range(HEADS)]
        for h in range(HEADS):
            sh_ref[h, 0] = s[h]
        o, s_new = _chunk_state_fn(*_head_args(refs[:6]), s)
        for h in range(HEADS):
            o_ref[:, h * HD:(h + 1) * HD] = o[h]
            s_ref[h] = s_new[h]

    return pl.pallas_call(
        body, name="gdn_scan", grid=(NCK,), in_specs=_chunk_prep_specs() + [ANY] * len(after),
        out_specs=[pl.BlockSpec((CH, HW), lambda n: (n, 0)), pl.BlockSpec((HEADS, 1, HD, HD), lambda n: (0, n, 0, 0))],
        out_shape=[jax.ShapeDtypeStruct((T, HW), F32), jax.ShapeDtypeStruct((HEADS, NCK, HD, HD), F32)],
        scratch_shapes=[pltpu.VMEM((HEADS, HD, HD), F32)], compiler_params=_cparams(("arbitrary",)),
    )(*prep, *after)


def _chunk_scan_bwd(prep, s_hist, do, after=()):
    n_in = 8 + len(after)

    def body(*refs):
        sh_ref, do_ref = refs[6:8]
        d_refs = refs[n_in:n_in + 6]
        ds_ref = refs[n_in + 6]

        @pl.when(pl.program_id(0) == 0)
        def _():
            ds_ref[...] = jnp.zeros_like(ds_ref)

        sls = [slice(h * HD, (h + 1) * HD) for h in range(HEADS)]
        _, vjp = jax.vjp(_chunk_state_fn, *_head_args(refs[:6]), [sh_ref[h, 0] for h in range(HEADS)])
        du, dw, dqe, dkd, dqk, degl, ds = vjp(([do_ref[:, sl] for sl in sls], [ds_ref[h] for h in range(HEADS)]))
        for h, sl in enumerate(sls):
            for d_ref, val in zip(d_refs[:4], (du, dw, dqe, dkd)):
                d_ref[:, sl] = val[h]
            d_refs[4][h] = dqk[h]
            d_refs[5][:, sl] = degl[h]
            ds_ref[h] = ds[h]

    rev = lambda n: NCK - 1 - n
    return pl.pallas_call(
        body, name="gdn_scan_bwd", grid=(NCK,),
        in_specs=_chunk_prep_specs(rev=True) + [pl.BlockSpec((HEADS, 1, HD, HD), lambda n: (0, rev(n), 0, 0)),
                                                pl.BlockSpec((CH, HW), lambda n: (rev(n), 0))] + [ANY] * len(after),
        out_specs=_chunk_prep_specs(rev=True), out_shape=_chunk_prep_shapes((F32,) * 6),
        scratch_shapes=[pltpu.VMEM((HEADS, HD, HD), F32)], compiler_params=_cparams(("arbitrary",)),
    )(*prep, s_hist, do, *after)


def _mix_specs():
    pc = lambda c: pl.BlockSpec((GCH, HW), lambda i: (i, c))
    return [pl.BlockSpec((GCH, HW), lambda i: (i, 0)), pc(3), pc(4), pc(5), _whole((1, HD)), _whole((1, HW)),
            _whole((1, HW)), _whole((HEADS, GCH, GCH)), _whole((GCH, LANE))]


def _mix(o, proj, ong, lng, lnb, ws, bst):
    def body(o_ref, z_ref, u_ref, v_ref, ong_ref, lng_ref, lnb_ref, ws_ref, bs_ref, m_ref):
        m_ref[...] = _mix_fn(o_ref[...], z_ref[...], u_ref[...], v_ref[...], ong_ref[...], lng_ref[...],
                             lnb_ref[...], ws_ref[...], bs_ref[...]).astype(BF16)

    return pl.pallas_call(
        body, name="mix", grid=(T // GCH,), in_specs=_mix_specs(),
        out_specs=pl.BlockSpec((GCH, D), lambda i: (i, 0)), out_shape=jax.ShapeDtypeStruct((T, D), BF16),
        compiler_params=_cparams(("parallel",)),
    )(o, proj, proj, proj, ong, lng, lnb, ws, bst)


def _mix_bwd(o, proj, ong, lng, lnb, ws, bst, dmix):
    def body(o_ref, z_ref, u_ref, v_ref, ong_ref, lng_ref, lnb_ref, ws_ref, bs_ref, dm_ref,
             do_ref, dzuv_ref, dong_ref, dlng_ref, dlnb_ref, dws_ref, dbs_ref):
        _, vjp = jax.vjp(_mix_fn, o_ref[...], z_ref[...], u_ref[...], v_ref[...], ong_ref[...], lng_ref[...],
                         lnb_ref[...], ws_ref[...], bs_ref[...])
        do, dz, du, dv, dong, dlng, dlnb, dws, dbs = vjp(dm_ref[...])
        do_ref[...] = do
        dzuv_ref[:, 0:HW] = dz.astype(BF16)
        dzuv_ref[:, HW:2 * HW] = du.astype(BF16)
        dzuv_ref[:, 2 * HW:3 * HW] = dv.astype(BF16)
        acc = [(dong_ref, dong), (dlng_ref, dlng), (dlnb_ref, dlnb), (dws_ref, dws), (dbs_ref, dbs)]

        @pl.when(pl.program_id(0) == 0)
        def _():
            for r, val in acc:
                r[...] = val

        @pl.when(pl.program_id(0) > 0)
        def _():
            for r, val in acc:
                r[...] += val

    shp = lambda *s: jax.ShapeDtypeStruct(s, F32)
    return pl.pallas_call(
        body, name="mix_bwd", grid=(T // GCH,),
        in_specs=_mix_specs() + [pl.BlockSpec((GCH, D), lambda i: (i, 0))],
        out_specs=[pl.BlockSpec((GCH, HW), lambda i: (i, 0)), pl.BlockSpec((GCH, 3 * HW), lambda i: (i, 1)),
                   _whole((1, HD)), _whole((1, HW)), _whole((1, HW)), _whole((HEADS, GCH, GCH)), _whole((GCH, LANE))],
        out_shape=[shp(T, HW), jax.ShapeDtypeStruct((T, NP), BF16), shp(1, HD), shp(1, HW), shp(1, HW),
                   shp(HEADS, GCH, GCH), shp(GCH, LANE)],
        compiler_params=_cparams(("arbitrary",)),
    )(o, proj, proj, proj, ong, lng, lnb, ws, bst, dmix)


def _swiglu_epilogue(accs, _):
    gate, up = accs
    return [gate, up, _silu(gate) * up]


def _swiglu_bwd_epilogue(accs, extras):
    dact = accs[0]
    gate, up = (e.astype(F32) for e in extras)
    sg = _sigmoid(gate)
    return [dact * up * (sg * (1.0 + gate * (1.0 - sg))), dact * (gate * sg)]


def _layer_fwd(h, p):
    hn = p.pop("hn") if "hn" in p else _rmsnorm("rms_mix", h, p["norm_mix"])
    proj = _mm("in_proj", "nn", hn[None], [p["w_in"][None]], tm=1024, tn=640, tk=D, sub_m=2)[0][0]
    qkv = _prep(proj, p["conv_w"])
    bg = _gates(proj, p["a_log"], p["dt_bias"])
    prep = _chunk_prep(qkv, bg)
    o, s_hist = _chunk_scan(prep, p.pop("before_scan")(prep[0]) if "before_scan" in p else ())
    if "late" in p:
        p.update(p.pop("late")(o))
    mix = _mix(o, proj, p["o_norm_g"], p["ln_v_g"], p["ln_v_b"], p["w_s"], p["bst"])
    h1 = _mm("out_proj", "nn", mix[None], [p["w_out"].reshape(1, D, D)], tm=1024, tn=512, tk=D, resid=h[None],
             sub_m=2)[0][0]
    h2n = _rmsnorm("rms_ffn", h1, p["norm_ffn"])
    gate, up, act = _mm("ffn_in", "nt", h2n[None], [p["w_gate"], p["w_up"]], tm=1024, tn=FF_SH, tk=D,
                        out_dtypes=(BF16, BF16, BF16), epilogue=_swiglu_epilogue, sub_m=4)
    then = p.pop("before_ffn_out")(act) if "before_ffn_out" in p else ()
    h2 = _mm("ffn_out", "nn", act, [p["w_down"]], tm=1024, tn=512, tk=FF_SH, reduce_g=True, fold_g=True,
             resid=h1[None], sub_m=2, after=then)[0][0]
    saved = dict(h=h, hn=hn, proj=proj, qkv=qkv, bg=bg, prep=prep, o=o, s_hist=s_hist, mix=mix, h1=h1, h2n=h2n,
                 gate=gate, up=up, act=act)
    return h2, saved


def _layer_bwd_ffn(dh2, dh2b, p, s, after=()):
    dh2b = dh2b[None]
    dgate, dup = _mm("ffn_out_bwd", "nt", dh2b, [p["w_down"]], tm=1024, tn=FF_SH, tk=D, out_dtypes=(BF16, BF16),
                     extras=(s["gate"], s["up"]), epilogue=_swiglu_bwd_epilogue, after=after, sub_m=4)
    dh2n = _mm("ffn_gate_bwd", "nn", dgate, [p["w_gate"]], tm=1024, tn=512, tk=FF_SH, reduce_g=True, fold_g=True,
               sub_m=2)[0]
    dh2n = _mm("ffn_up_bwd", "nn", dup, [p["w_up"]], tm=1024, tn=512, tk=FF_SH, reduce_g=True, fold_g=True,
               resid=dh2n, sub_m=2)[0][0]
    dh1, dh1b, d_norm_ffn = _rmsnorm_bwd("rms_ffn_bwd", dh2n, s["h1"], p["norm_ffn"], dh2)
    d_w_down = _mm("ffn_wdown_grad", "tn", s["act"], [dh2b], tm=FF_SH, tn=512, tk=T)[0]
    d_w_gate = _mm("ffn_wgate_grad", "tn", dgate, [s["h2n"][None]], tm=FF_SH, tn=512, tk=T)[0]
    d_w_up = _mm("ffn_wup_grad", "tn", dup, [s["h2n"][None]], tm=FF_SH, tn=512, tk=T)[0]
    return dh1, dh1b, dict(norm_ffn=d_norm_ffn, w_gate=d_w_gate, w_up=d_w_up, w_down=d_w_down)


def _layer_bwd_mixer(dh1, dh1b, p, s, after=(), midway=None, late=None):
    dh1b = dh1b[None]
    dmix = _mm("out_proj_bwd", "nt", dh1b, [p["w_out"].reshape(1, D, D)], tm=1024, tn=512, tk=D, after=after,
               sub_m=2)[0][0]
    d_w_out = _mm("out_proj_wgrad", "tn", s["mix"][None], [dh1b], tm=512, tn=512, tk=T)[0][0]
    do, dproj, d_ong, d_lng, d_lnb, d_ws, d_bst = _mix_bwd(
        s["o"], s["proj"], p["o_norm_g"], p["ln_v_g"], p["ln_v_b"], p["w_s"], p["bst"], dmix)
    then = midway(do) if midway is not None else ()
    dqkv, dbg = _chunk_prep_bwd(s["qkv"], s["bg"], _chunk_scan_bwd(s["prep"], s["s_hist"], do, then))
    dproj, d_conv = _prep_bwd(s["proj"], p["conv_w"], dqkv, dproj)
    dproj, d_a_log, d_dt_bias = _gates_bwd(s["proj"], p["a_log"], p["dt_bias"], dbg, dproj)
    dproj = dproj[None]
    d_w_in = _mm("in_proj_wgrad", "tn", s["hn"][None], [dproj], tm=512, tn=640, tk=T)[0]
    last = late(dict(w_in=d_w_in, w_out=d_w_out)) if late is not None else ()
    dhn = _mm("in_proj_bwd", "nt", dproj, [p["w_in"][None]], tm=1024, tn=512, tk=NP, after=last,
              sub_m=2)[0][0]
    dh, dhb, d_norm_mix = _rmsnorm_bwd("rms_mix_bwd", dhn, s["h"], p["norm_mix"], dh1)
    grads = dict(norm_mix=d_norm_mix, w_in=d_w_in, conv_w=d_conv, a_log=d_a_log, dt_bias=d_dt_bias, o_norm_g=d_ong,
                 ln_v_g=d_lng, ln_v_b=d_lnb, w_s=d_ws, bst=d_bst, w_out=d_w_out)
    return dh, dhb, grads


def _lanes(v, off=0):
    return jnp.zeros((1, LANE), F32).at[0, off:off + v.shape[0]].set(v)


def _w_in_pieces():
    regions = [(0, 2048, 0), (2048, 2056, BA_OFF), (2056, IN_DIM, 2048)]
    sh = IN_DIM // NCHIP
    out = []
    for j in range(NCHIP):
        for lo, hi, at in regions:
            a, b = max(lo, j * sh), min(hi, (j + 1) * sh)
            if a < b:
                out.append((j, a - j * sh, at + a - lo, b - a))
    return out


W_IN_PIECES = _w_in_pieces()
WT = 256


def _assemble_w_in(gathered, own, place):
    def body(place_ref, g_ref, own_ref, o_ref):
        o_ref[:, IN_DIM:] = jnp.zeros((WT, NP - IN_DIM), BF16)
        mine = own_ref[...]
        for j, src, dst, width in W_IN_PIECES:
            val = jnp.where(place_ref[0] == j, mine[:, src:src + width], g_ref[j, :, src:src + width])
            o_ref[:, dst:dst + width] = val

    sh = IN_DIM // NCHIP
    return pl.pallas_call(
        body, name="assemble_w_in",
        grid_spec=pltpu.PrefetchScalarGridSpec(
            num_scalar_prefetch=1, grid=(D // WT,),
            in_specs=[pl.BlockSpec((NCHIP, WT, sh), lambda i, place_ref: (0, i, 0)),
                      pl.BlockSpec((WT, sh), lambda i, place_ref: (i, 0))],
            out_specs=pl.BlockSpec((WT, NP), lambda i, place_ref: (i, 0))),
        out_shape=jax.ShapeDtypeStruct((D, NP), BF16), compiler_params=_cparams(("parallel",)),
    )(place, gathered, own)


def _layer_params(l, big, small):
    return dict(
        {k: v for k, v in big.items() if k != "conv_w"},
        conv_w=jnp.concatenate([big["conv_w"][j, l] for j in range(NCHIP)], axis=1),
        norm_mix=small["norm_mix"][l][None], norm_ffn=small["norm_ffn"][l][None],
        a_log=_lanes(small["a_log"][l], HEADS), dt_bias=_lanes(small["dt_bias"][l], HEADS),
        o_norm_g=small["o_norm_g"][l][None], ln_v_g=small["ln_v_g"][l][None], ln_v_b=small["ln_v_b"][l][None],
        w_s=small["w_s"][l],
        bst=jnp.pad(small["b_s"][l].T, ((0, 0), (0, LANE - HEADS))),
    )


def _reference_layout(g):
    return dict(
        w_in=g["w_in"],
        w_out=g["w_out"].reshape(NCHIP, D // NCHIP, D),
        w_gate=g["w_gate"], w_up=g["w_up"], w_down=g["w_down"],
        conv_w=g["conv_w"], norm_mix=g["norm_mix"][0], norm_ffn=g["norm_ffn"][0],
        a_log=g["a_log"][0, HEADS:2 * HEADS], dt_bias=g["dt_bias"][0, HEADS:2 * HEADS],
        o_norm_g=g["o_norm_g"][0], ln_v_g=g["ln_v_g"][0], ln_v_b=g["ln_v_b"][0], w_s=g["w_s"],
        b_s=g["bst"][:, :HEADS].T,
    )


def _forward(x, tgt, layers, norm_final):
    h = x
    saved, params = [], []
    for p in layers:
        p = p(h) if callable(p) else p
        h, s = _layer_fwd(h, p)
        saved.append(s)
        params.append(p)
    return (saved, params) + tuple(_loss_head(h, norm_final, tgt))


def _local_step(x, tgt, layers, norm_final):
    saved, layers, loss, dh, dhb, d_norm_final = _forward(x, tgt, layers, norm_final)
    grads = [None] * DEPTH
    for l in reversed(range(DEPTH)):
        dh1, dh1b, g_ffn = _layer_bwd_ffn(dh, dhb, layers[l], saved[l])
        dh, dhb, g_mix = _layer_bwd_mixer(dh1, dh1b, layers[l], saved[l])
        grads[l] = {**g_ffn, **g_mix}
    return loss, dh, grads, d_norm_final


def _place():
    x, y, c = lax.axis_index("x"), lax.axis_index("y"), lax.axis_index("c")
    return x, y, c, [(1 - x, y), (x, 1 - y), (1 - x, 1 - y)]


def _remote(src, dst, send_sem, recv_sem, to):
    return pltpu.make_async_remote_copy(src_ref=src, dst_ref=dst, send_sem=send_sem, recv_sem=recv_sem,
                                        device_id=to, device_id_type=MESH)


def _comm_call(name, body, ins, out_shape, n_sems, aliases=None):
    return pl.pallas_call(
        body, name=name, in_specs=[ANY] * len(ins), out_specs=[ANY] * len(out_shape), out_shape=out_shape,
        scratch_shapes=[pltpu.SemaphoreType.DMA((n,)) for n in n_sems], input_output_aliases=aliases or {},
        compiler_params=pltpu.CompilerParams(has_side_effects=True),
    )(*ins)


def _half_rows(ref, of_c, dim):
    hr = ref.shape[dim] // 2
    return pl.ds(pl.multiple_of(of_c * hr, BF16_ROWS), hr)


def _gather_plan(whole):
    def plan(srcs, lands):
        x, y, c, others = _place()
        chip = 2 * x + y
        out = []
        for src, land, all_of_it in zip(srcs, lands, whole):
            for ox, oy in others:
                if all_of_it:
                    out.append((src, land.at[chip], (ox, oy, c)))
                else:
                    out.append((src.at[_half_rows(src, c, 0)], land.at[chip, _half_rows(src, c, 0)], (ox, oy, c)))
        return out
    return plan


def _forward_halves(lands):
    n = len(lands)

    def body(*refs):
        outs = refs[n:2 * n]
        send_s, recv_s = refs[2 * n:]
        x, y, c, others = _place()
        sibling = (x, y, 1 - c)
        copies = []
        for a in range(n):
            for k, (ox, oy) in enumerate(others):
                mine = outs[a].at[2 * ox + oy, _half_rows(outs[a], c, 1)]
                copies.append(_remote(mine, mine, send_s.at[3 * a + k], recv_s.at[3 * a + k], sibling))
        for cp in copies:
            cp.start()
        for a in range(n):
            for k, (ox, oy) in enumerate(others):
                landed = outs[a].at[2 * ox + oy, _half_rows(outs[a], 1 - c, 1)]
                _remote(landed, landed, send_s.at[3 * a + k], recv_s.at[3 * a + k], sibling).wait_recv()
        for cp in copies:
            cp.wait_send()

    out_shape = [jax.ShapeDtypeStruct(g.shape, g.dtype) for g in lands]
    return _comm_call("forward_halves", body, lands, out_shape, [3 * n, 3 * n], aliases={a: a for a in range(n)})


def _forward_refs(bufs, incoming):
    x, y, c, others = _place()
    return (x, y, 1 - c), [b.at[2 * ox + oy, _half_rows(b, 1 - c if incoming else c, 1)]
                           for b in bufs for ox, oy in others]


def _forward_start(name, bufs, after):
    n = len(bufs)
    bufs = [pltpu.with_memory_space_constraint(b, pltpu.HBM) for b in bufs]

    def body(*refs):
        send_s, recv_s = refs[n + len(after)], refs[n + len(after) + 1]
        sibling, mine = _forward_refs(refs[:n], incoming=False)
        for i, ref in enumerate(mine):
            _remote(ref, ref, send_s.at[i], recv_s.at[i], sibling).start()
        refs[-1][...] = jnp.zeros_like(refs[-1])

    out = pl.pallas_call(
        body, name=name, in_specs=[HBM_SPEC] * n + [ANY] * len(after),
        out_specs=[SEM_SPEC, SEM_SPEC] + [HBM_SPEC] * n + [pl.BlockSpec(memory_space=pltpu.VMEM)],
        out_shape=[pltpu.SemaphoreType.DMA((3 * n,)), pltpu.SemaphoreType.DMA((3 * n,))]
        + [pltpu.HBM(b.shape, b.dtype) for b in bufs] + [jax.ShapeDtypeStruct((F32_ROWS, LANE), F32)],
        input_output_aliases={i: 2 + i for i in range(n)},
        compiler_params=pltpu.CompilerParams(has_side_effects=DATAFLOW),
    )(*bufs, *after)
    return dict(sems=out[:2], bufs=out[2:2 + n], token=out[-1])


def _forward_wait(name, started, after):
    n = len(started["bufs"])

    def body(*refs):
        send_s, recv_s = refs[n], refs[n + 1]
        sibling, mine = _forward_refs(refs[:n], incoming=False)
        _, theirs = _forward_refs(refs[:n], incoming=True)
        for i, (sent, landed) in enumerate(zip(mine, theirs)):
            _remote(sent, sent, send_s.at[i], recv_s.at[i], sibling).wait_send()
            _remote(landed, landed, send_s.at[i], recv_s.at[i], sibling).wait_recv()

    return pl.pallas_call(
        body, name=name, in_specs=[HBM_SPEC] * n + [SEM_SPEC, SEM_SPEC] + [ANY] * len(after),
        out_specs=[HBM_SPEC] * n, out_shape=[pltpu.HBM(b.shape, b.dtype) for b in started["bufs"]],
        input_output_aliases={i: i for i in range(n)},
        compiler_params=pltpu.CompilerParams(has_side_effects=DATAFLOW),
    )(*started["bufs"], *started["sems"], *after)


HBM_SPEC = pl.BlockSpec(memory_space=pltpu.HBM)
SEM_SPEC = pl.BlockSpec(memory_space=pltpu.SEMAPHORE)
DATAFLOW = pltpu.SideEffectType.DATAFLOW_SIDE_EFFECTING


def _exchange_plan(srcs, lands):
    x, y, c, _ = _place()
    plan = []
    for src, land in zip(srcs, lands):
        hr = src.shape[1] // 2
        plan.append((src.at[:, pl.ds(pl.multiple_of((1 - c) * hr, 8), hr)], land, (x, y, 1 - c)))
    return plan


def _scatter_plan(srcs, lands):
    x, y, c, others = _place()
    return [(src.at[2 * ox + oy], land.at[k], (ox, oy, c))
            for src, land in zip(srcs, lands) for k, (ox, oy) in enumerate(others)]


def _split_start(name, plan, srcs, land_shapes, n_copies, after=()):
    n = len(srcs)
    lands = [pltpu.with_memory_space_constraint(lax.empty(s.shape, s.dtype), pltpu.HBM) for s in land_shapes]
    srcs = [pltpu.with_memory_space_constraint(s, pltpu.HBM) for s in srcs]

    def body(*refs):
        send_s, recv_s = refs[2 * n + len(after)], refs[2 * n + len(after) + 1]
        for i, (src, dst, to) in enumerate(plan(refs[:n], refs[n:2 * n])):
            _remote(src, dst, send_s.at[i], recv_s.at[i], to).start()
        refs[-1][...] = jnp.zeros_like(refs[-1])

    thru = [pltpu.HBM(s.shape, s.dtype) for s in srcs + lands]
    out = pl.pallas_call(
        body, name=name, in_specs=[HBM_SPEC] * (2 * n) + [ANY] * len(after),
        out_specs=[SEM_SPEC, SEM_SPEC] + [HBM_SPEC] * (2 * n) + [pl.BlockSpec(memory_space=pltpu.VMEM)],
        out_shape=[pltpu.SemaphoreType.DMA((n_copies,)), pltpu.SemaphoreType.DMA((n_copies,))] + thru
        + [jax.ShapeDtypeStruct((F32_ROWS, LANE), F32)],
        input_output_aliases={i: 2 + i for i in range(2 * n)},
        compiler_params=pltpu.CompilerParams(has_side_effects=DATAFLOW),
    )(*srcs, *lands, *after)
    return dict(sems=out[:2], srcs=out[2:2 + n], lands=out[2 + n:2 + 2 * n], token=out[-1])


def _split_wait(name, plan, started, after):
    n = len(started["srcs"])
    after = list(after) if isinstance(after, (list, tuple)) else [after]

    def body(*refs):
        send_s, recv_s = refs[2 * n], refs[2 * n + 1]
        for i, (src, dst, to) in enumerate(plan(refs[:n], refs[n:2 * n])):
            cp = _remote(src, dst, send_s.at[i], recv_s.at[i], to)
            cp.wait_send()
            cp.wait_recv()

    arrs = list(started["srcs"]) + list(started["lands"])
    out = pl.pallas_call(
        body, name=name, in_specs=[HBM_SPEC] * (2 * n) + [SEM_SPEC, SEM_SPEC] + [ANY] * len(after),
        out_specs=[HBM_SPEC] * (2 * n), out_shape=[pltpu.HBM(s.shape, s.dtype) for s in arrs],
        input_output_aliases={i: i for i in range(2 * n)},
        compiler_params=pltpu.CompilerParams(has_side_effects=DATAFLOW),
    )(*arrs, *started["sems"], *after)
    return out[:n], out[n:]


def _join_halves(name, rs):
    n = len(rs)

    def body(*refs):
        outs = refs[n:2 * n]
        send_s, recv_s = refs[2 * n:]
        x, y, c, _ = _place()
        sibling = (x, y, 1 - c)

        def half(a, of_c):
            hr = outs[a].shape[1] // 2
            return outs[a].at[:, pl.ds(pl.multiple_of(of_c * hr, 8), hr)]

        copies = [_remote(half(a, c), half(a, c), send_s.at[a], recv_s.at[a], sibling) for a in range(n)]
        for cp in copies:
            cp.start()
        for a in range(n):
            landed = half(a, 1 - c)
            _remote(landed, landed, send_s.at[a], recv_s.at[a], sibling).wait_recv()
        for cp in copies:
            cp.wait_send()

    out_shape = [jax.ShapeDtypeStruct(r.shape, r.dtype) for r in rs]
    return _comm_call(name, body, rs, out_shape, [n, n], aliases={a: a for a in range(n)})


def _allreduce_small(buf, after=()):
    r = buf.shape[0]
    hr = r // 2

    def body(in_ref, *refs):
        out_ref, theirs, by_chip, send_s, recv_s = refs[len(after):]
        x, y, c, others = _place()
        chip = 2 * x + y
        sibling = (x, y, 1 - c)
        mine = pl.ds(pl.multiple_of(c * hr, F32_ROWS), hr)
        swap = _remote(in_ref, theirs, send_s.at[0], recv_s.at[0], sibling)
        swap.start()
        swap.wait()
        by_chip[chip] = in_ref[mine, :] + theirs[mine, :]
        sends = [_remote(by_chip.at[chip], by_chip.at[chip], send_s.at[1 + k], recv_s.at[1 + k], (ox, oy, c))
                 for k, (ox, oy) in enumerate(others)]
        for cp in sends:
            cp.start()
        for k, (ox, oy) in enumerate(others):
            landed = by_chip.at[2 * ox + oy]
            _remote(landed, landed, send_s.at[1 + k], recv_s.at[1 + k], (ox, oy, c)).wait_recv()
        for cp in sends:
            cp.wait_send()
        out_ref[mine, :] = (by_chip[0] + by_chip[1]) + (by_chip[2] + by_chip[3])
        back = _remote(out_ref.at[mine], out_ref.at[mine], send_s.at[NCHIP], recv_s.at[NCHIP], sibling)
        back.start()
        other = out_ref.at[pl.ds(pl.multiple_of((1 - c) * hr, F32_ROWS), hr)]
        _remote(other, other, send_s.at[NCHIP], recv_s.at[NCHIP], sibling).wait_recv()
        back.wait_send()

    vm = pl.BlockSpec(memory_space=pltpu.VMEM)
    return pl.pallas_call(
        body, name="allreduce_small", in_specs=[vm] + [ANY] * len(after), out_specs=vm,
        out_shape=jax.ShapeDtypeStruct((r, LANE), F32),
        scratch_shapes=[pltpu.VMEM((r, LANE), F32), pltpu.VMEM((NCHIP, hr, LANE), F32),
                        pltpu.SemaphoreType.DMA((NCHIP + 1,)), pltpu.SemaphoreType.DMA((NCHIP + 1,))],
        compiler_params=pltpu.CompilerParams(has_side_effects=True, vmem_limit_bytes=VMEM_LIMIT),
    )(buf, *after)


MAX_ROW_TILE = 512
BF16_ROWS = 16


def _row_tile(rows):
    for t in range(min(rows, MAX_ROW_TILE) // BF16_ROWS * BF16_ROWS, 0, -BF16_ROWS):
        if rows % t == 0:
            return t
    raise ValueError(rows)


def _sum_halves(g, theirs, c_arr):
    nch, rows, cols = g.shape
    hr = rows // 2
    tr = _row_tile(hr)

    def body(c_ref, g_ref, t_ref, o_ref, ob_ref):
        s = g_ref[...] + t_ref[...]
        o_ref[...] = s
        ob_ref[...] = s.astype(BF16)

    blk = pl.BlockSpec((None, tr, cols), lambda j, i, c_ref: (j, i, 0))
    return pl.pallas_call(
        body, name="sum_halves",
        grid_spec=pltpu.PrefetchScalarGridSpec(
            num_scalar_prefetch=1, grid=(nch, hr // tr),
            in_specs=[pl.BlockSpec((None, None, tr, cols), lambda j, i, c_ref: (j, c_ref[0], i, 0)), blk],
            out_specs=[blk, blk]),
        out_shape=[jax.ShapeDtypeStruct((nch, hr, cols), F32), jax.ShapeDtypeStruct((nch, hr, cols), BF16)],
        compiler_params=_cparams(("parallel", "parallel")),
    )(c_arr, g.reshape(nch, 2, hr, cols), theirs)


def _sum_halves_w_in(g, theirs, c_arr):
    hr = D // 2
    sh = IN_DIM // NCHIP

    def body(c_ref, g_ref, t_ref, o_ref, ob_ref):
        s = g_ref[...] + t_ref[...]
        for j, dst, src, width in W_IN_PIECES:
            o_ref[j, :, dst:dst + width] = s[:, src:src + width]
            ob_ref[j, :, dst:dst + width] = s[:, src:src + width].astype(BF16)

    out = pl.BlockSpec((NCHIP, WT, sh), lambda i, c_ref: (0, i, 0))
    return pl.pallas_call(
        body, name="sum_halves_w_in",
        grid_spec=pltpu.PrefetchScalarGridSpec(
            num_scalar_prefetch=1, grid=(hr // WT,),
            in_specs=[pl.BlockSpec((None, WT, NP), lambda i, c_ref: (c_ref[0], i, 0)),
                      pl.BlockSpec((None, WT, NP), lambda i, c_ref: (0, i, 0))],
            out_specs=[out, out]),
        out_shape=[jax.ShapeDtypeStruct((NCHIP, hr, sh), F32), jax.ShapeDtypeStruct((NCHIP, hr, sh), BF16)],
        compiler_params=_cparams(("parallel",)),
    )(c_arr, g.reshape(2, hr, NP), theirs)


def _sum_chips(p, q, place, l, into=None, after=()):
    extra = ([into] if into is not None else []) + list(after)
    _, rows, cols = p.shape
    tr = _row_tile(rows)
    steps = rows // tr

    def body(place_ref, p_ref, q0, q1, q2, *rest):
        rest[-1][...] = ((p_ref[...] + q0[...].astype(F32)) + q1[...].astype(F32)) + q2[...].astype(F32)

    qs = lambda k: pl.BlockSpec((None, tr, cols), lambda i, place_ref: (k, i, 0))
    return pl.pallas_call(
        body, name="sum_chips",
        grid_spec=pltpu.PrefetchScalarGridSpec(
            num_scalar_prefetch=1, grid=(steps,),
            in_specs=[pl.BlockSpec((None, tr, cols), lambda i, place_ref: (place_ref[0], i, 0)), qs(0), qs(1), qs(2)]
            + [ANY] * len(extra),
            out_specs=pl.BlockSpec((None, tr, cols), lambda i, place_ref: (l, place_ref[1] * steps + i, 0))),
        out_shape=jax.ShapeDtypeStruct((DEPTH, 2 * rows, cols), F32),
        input_output_aliases={5: 0} if into is not None else {},
        compiler_params=_cparams(("parallel",)),
    )(place, p, q, q, q, *extra)


def _adamw(w, g, m, v):
    layers, rows, cols = w.shape
    tr = _row_tile(rows)

    def body(w_ref, g_ref, m_ref, v_ref, d_ref, nm_ref, nv_ref):
        gv = g_ref[...]
        nm = ADAM_B1 * m_ref[...] + (1.0 - ADAM_B1) * gv
        nv = ADAM_B2 * v_ref[...] + (1.0 - ADAM_B2) * jnp.square(gv)
        m_hat = nm / (1.0 - ADAM_B1 ** ADAM_STEP)
        v_hat = nv / (1.0 - ADAM_B2 ** ADAM_STEP)
        d_ref[...] = -ADAM_LR * (m_hat / (jnp.sqrt(v_hat) + ADAM_EPS) + ADAM_WD * w_ref[...])
        nm_ref[...] = nm
        nv_ref[...] = nv

    blk = pl.BlockSpec((None, tr, cols), lambda l, i: (l, i, 0))
    return pl.pallas_call(
        body, name="adamw", grid=(layers, rows // tr), in_specs=[blk] * 4, out_specs=[blk] * 3,
        out_shape=[jax.ShapeDtypeStruct(w.shape, F32)] * 3, compiler_params=_cparams(("parallel", "parallel")),
    )(w, g, m, v)


BIG = ("w_in", "w_out", "w_gate", "w_up", "w_down")
SMALL = ("norm_mix", "a_log", "dt_bias", "o_norm_g", "ln_v_g", "ln_v_b", "w_s", "b_s", "norm_ffn", "norm_final")
ORDER = ("norm_mix", "w_in", "conv_w", "a_log", "dt_bias", "o_norm_g", "ln_v_g", "ln_v_b", "w_s", "b_s", "w_out",
         "norm_ffn", "w_gate", "w_up", "w_down", "norm_final")


F32_ROWS = 8
PACK_ROWS = 128


def _lane_rows(size):
    return -(-size // (F32_ROWS * LANE)) * F32_ROWS


def _pack(arrs):
    parts = [jnp.pad(a.reshape(-1), (0, _lane_rows(a.size) * LANE - a.size)).reshape(-1, LANE) for a in arrs]
    rows = sum(p.shape[0] for p in parts)
    if rows % PACK_ROWS:
        parts.append(jnp.zeros((-rows % PACK_ROWS, LANE), F32))
    return jnp.concatenate(parts, axis=0)


def _unpack(buf, like):
    out, row = [], 0
    for a in like:
        n = _lane_rows(a.size)
        out.append(buf[row:row + n].reshape(-1)[:a.size].reshape(a.shape))
        row += n
    return out


def kernel(x, norm_mix, w_in, conv_w, a_log, dt_bias, o_norm_g, ln_v_g, ln_v_b, w_s, b_s, w_out, norm_ffn, w_gate, w_up, w_down, norm_final, loss_target, m_norm_mix, m_w_in, m_conv_w, m_a_log, m_dt_bias, m_o_norm_g, m_ln_v_g, m_ln_v_b, m_w_s, m_b_s, m_w_out, m_norm_ffn, m_w_gate, m_w_up, m_w_down, m_norm_final, v_norm_mix, v_w_in, v_conv_w, v_a_log, v_dt_bias, v_o_norm_g, v_ln_v_g, v_ln_v_b, v_w_s, v_b_s, v_w_out, v_norm_ffn, v_w_gate, v_w_up, v_w_down, v_norm_final):
    w = dict(norm_mix=norm_mix, w_in=w_in, conv_w=conv_w, a_log=a_log, dt_bias=dt_bias, o_norm_g=o_norm_g,
             ln_v_g=ln_v_g, ln_v_b=ln_v_b, w_s=w_s, b_s=b_s, w_out=w_out, norm_ffn=norm_ffn, w_gate=w_gate, w_up=w_up,
             w_down=w_down, norm_final=norm_final)
    m = dict(norm_mix=m_norm_mix, w_in=m_w_in, conv_w=m_conv_w, a_log=m_a_log, dt_bias=m_dt_bias, o_norm_g=m_o_norm_g,
             ln_v_g=m_ln_v_g, ln_v_b=m_ln_v_b, w_s=m_w_s, b_s=m_b_s, w_out=m_w_out, norm_ffn=m_norm_ffn,
             w_gate=m_w_gate, w_up=m_w_up, w_down=m_w_down, norm_final=m_norm_final)
    v = dict(norm_mix=v_norm_mix, w_in=v_w_in, conv_w=v_conv_w, a_log=v_a_log, dt_bias=v_dt_bias, o_norm_g=v_o_norm_g,
             ln_v_g=v_ln_v_g, ln_v_b=v_ln_v_b, w_s=v_w_s, b_s=v_b_s, w_out=v_w_out, norm_ffn=v_norm_ffn,
             w_gate=v_w_gate, w_up=v_w_up, w_down=v_w_down, norm_final=v_norm_final)
    chip = 2 * lax.axis_index("x") + lax.axis_index("y")
    place = jnp.stack([chip, lax.axis_index("c")]).astype(jnp.int32)
    c_arr = place[1:]

    def kernel_view(n, a):
        return jnp.swapaxes(a, 1, 2) if n in ("w_gate", "w_up") else a

    own = {n: [kernel_view(n, w[n])[l].astype(BF16) for l in range(DEPTH)] for n in BIG}
    by_chip = lambda a: jax.ShapeDtypeStruct((NCHIP,) + a.shape, a.dtype)

    def start(name, srcs, whole, after=()):
        return _split_start(name, _gather_plan(whole), srcs, [by_chip(a) for a in srcs], 3 * len(srcs), after)

    def finish(name, started, whole, after):
        srcs, lands = _split_wait(name, _gather_plan(whole), started, after)
        passed = iter(_forward_halves([g for g, all_of_it in zip(lands, whole) if not all_of_it]))
        lands = [g if all_of_it else next(passed) for g, all_of_it in zip(lands, whole)]
        return srcs, [lax.dynamic_update_index_in_dim(g, o, chip, 0) for g, o in zip(lands, srcs)]

    ffn = BIG[1:]
    first = start("gather_first_start", [own["w_in"][0], conv_w], [False, True])
    early = start("gather_early_start", [own[n][0] for n in ffn], [False] * len(ffn), [first["token"]])
    later = start("gather_later_start", [own[n][1] for n in BIG], [False] * len(BIG), [early["token"]])
    hn = _rmsnorm("rms_mix", x[0], norm_mix[0][None])
    (own_w_in, _), (w_in_by_chip, conv_by_chip) = finish("gather_first_wait", first, [False, True], [later["token"], hn])

    passing = {}

    def pass_on(tag, started, n):
        def at(after):
            srcs, lands = _split_wait(f"gather_{tag}_wait", _gather_plan([False] * n), started, after)
            passing[tag] = srcs, _forward_start(f"forward_{tag}_start", lands, ())
            return [passing[tag][1]["token"]]
        return at

    def passed_on(tag, after):
        srcs, fwd = passing[tag]
        lands = _forward_wait(f"forward_{tag}_wait", fwd, [after])
        return srcs, [lax.dynamic_update_index_in_dim(g, o, chip, 0) for g, o in zip(lands, srcs)]

    def late(after):
        return dict(zip(ffn, passed_on("early", after)[1]))

    layer0 = _layer_params(0, dict(
        hn=hn, w_in=_assemble_w_in(w_in_by_chip, own_w_in, place), conv_w=conv_by_chip, late=late,
        before_scan=pass_on("early", early, len(ffn)), before_ffn_out=pass_on("later", later, len(BIG))), w)

    def layer1(after):
        srcs, by = passed_on("later", after)
        big = dict(zip(ffn, by[1:]), w_in=_assemble_w_in(by[0], srcs[0], place), conv_w=conv_by_chip)
        return _layer_params(1, big, w)

    saved, layers, loss_lanes, dh, dhb, d_norm_final = _forward(x[0], loss_target[0], [layer0, layer1],
                                                                 norm_final[None])

    sums, arrived = {}, {}

    def exchange_start(tag, l, names, grads, after=()):
        mine = [grads[n] for n in names]
        shapes = [jax.ShapeDtypeStruct((g.shape[0], g.shape[1] // 2, g.shape[2]), F32) for g in mine]
        return tag, l, names, _split_start(f"exchange_{tag}_start", _exchange_plan, mine, shapes, len(mine), after)

    def add_halves(l, names, mine, theirs):
        for n, g, t in zip(names, mine, theirs):
            sums[l, n] = (_sum_halves_w_in if n == "w_in" else _sum_halves)(g, t, c_arr)

    def exchange_wait(handle, after):
        tag, l, names, started = handle
        add_halves(l, names, *_split_wait(f"exchange_{tag}_wait", _exchange_plan, started, after))

    def scatter_start(tag, l, names, after=()):
        partial = [sums[l, n][1] for n in names]
        shapes = [jax.ShapeDtypeStruct((3,) + p.shape[1:], p.dtype) for p in partial]
        return tag, l, names, _split_start(f"scatter_{tag}_start", _scatter_plan, partial, shapes, 3 * len(names), after)

    def scatter_wait(handle, after):
        tag, l, names, started = handle
        for n, q in zip(names, _split_wait(f"scatter_{tag}_wait", _scatter_plan, started, after)[1]):
            arrived[l, n] = q

    last = DEPTH - 1
    swiglu = BIG[2:]
    dh1, dh1b, g_ffn = _layer_bwd_ffn(dh, dhb, layers[last], saved[last])
    dh, dhb, g_mix = _layer_bwd_mixer(dh1, dh1b, layers[last], saved[last])
    gl = [None, _reference_layout({**g_ffn, **g_mix})]
    ex_last = exchange_start("last", last, BIG, gl[last])
    dh1, dh1b, g_ffn = _layer_bwd_ffn(dh, dhb, layers[0], saved[0], after=[ex_last[-1]["token"]])
    exchange_wait(ex_last, dh1)
    sc_last = scatter_start("last", last, BIG)
    ex_ffn = exchange_start("swiglu", 0, swiglu, g_ffn, [sc_last[-1]["token"]])
    sc_ffn = []

    def midway(do):
        exchange_wait(ex_ffn, do)
        sc_ffn.append(scatter_start("swiglu", 0, swiglu))
        return [sc_ffn[0][-1]["token"]]

    ex_rest = []

    def late(grads):
        rest_grads = dict(w_in=grads["w_in"], w_out=grads["w_out"].reshape(NCHIP, D // NCHIP, D))
        ex_rest.append(exchange_start("rest", 0, BIG[:2], rest_grads))
        return [ex_rest[0][-1]["token"]]

    dx, _, g_mix = _layer_bwd_mixer(dh1, dh1b, layers[0], saved[0], after=[ex_ffn[-1]["token"]], midway=midway,
                                    late=late)
    scatter_wait(sc_last, dx)
    scatter_wait(sc_ffn[0], dx)
    gl[0] = _reference_layout({**g_ffn, **g_mix})

    small_g = [jnp.stack([gl[l][n] for l in range(DEPTH)]) for n in SMALL[:-1]] + [d_norm_final[0]]
    conv_g = jnp.stack([gl[l]["conv_w"] for l in range(DEPTH)])
    summed = small_g + [conv_g, loss_lanes[0, :1]]
    total = _allreduce_small(_pack(summed))
    exchange_wait(ex_rest[0], total)
    sc_rest = scatter_start("rest", 0, BIG[:2])

    travelling = [sc_rest[-1]["token"]]
    reduced, g_out, delta, new_m, new_v = {}, {}, {}, {}, {}

    def adamw_large(names, joined):
        for n, g in zip(names, joined):
            res = _adamw(kernel_view(n, w[n]), g, kernel_view(n, m[n]), kernel_view(n, v[n]))
            g_out[n], delta[n], new_m[n], new_v[n] = (kernel_view(n, a) for a in (g,) + tuple(res))

    for n in BIG:
        for l in (range(DEPTH) if n in swiglu else [last]):
            reduced[n] = _sum_chips(sums[l, n][0], arrived[l, n], place, l, into=reduced.get(n), after=travelling)
    adamw_large(swiglu, _join_halves("join_swiglu", [reduced[n] for n in swiglu]))
    scatter_wait(sc_rest, [new_v[n] for n in swiglu] + [reduced[n] for n in BIG[:2]])
    for n in BIG[:2]:
        reduced[n] = _sum_chips(sums[0, n][0], arrived[0, n], place, 0, into=reduced[n])
    adamw_large(BIG[:2], _join_halves("join_rest", [reduced[n] for n in BIG[:2]]))
    *small_r, conv_r, loss = _unpack(total, summed)
    g_out.update(zip(SMALL, small_r))
    g_out["conv_w"] = lax.dynamic_slice_in_dim(conv_r, chip * conv_w.shape[2], conv_w.shape[2], axis=2)

    rest = SMALL + ("conv_w",)
    like = [w[n] for n in rest]
    d, nm, nv = _adamw(*[_pack([src[n] for n in rest])[None] for src in (w, g_out, m, v)])
    for dst, buf in ((delta, d), (new_m, nm), (new_v, nv)):
        dst.update(zip(rest, _unpack(buf[0], like)))

    return (loss[0], dx[None], *[g_out[n] for n in ORDER], *[delta[n] for n in ORDER], *[new_m[n] for n in ORDER],
            *[new_v[n] for n in ORDER])
```

```python
import functools

import jax
import jax.numpy as jnp
from jax import lax
from jax.experimental import pallas as pl
from jax.experimental.pallas import tpu as pltpu

F32 = jnp.float32
BF16 = jnp.bfloat16
MESH = pl.DeviceIdType.MESH
ANY = pl.BlockSpec(memory_space=pl.ANY)
HIGHEST = lax.Precision.HIGHEST

T = 2048
D = 1024
DEPTH = 2
NCHIP = 4
HEADS = 4
HD = 128
HW = HEADS * HD
CH = 64
GCH = 128
IN_DIM = 3080
NP = 3200
BA_OFF = 3072
FF_SH = 704
EPS = 1e-6
LANE = 128
VMEM_LIMIT = 56 * 1024 * 1024

ADAM_LR = 0.001
ADAM_B1 = 0.9
ADAM_B2 = 0.999
ADAM_EPS = 1e-08
ADAM_WD = 0.01
ADAM_STEP = 10


def _cparams(sem=None):
    return pltpu.CompilerParams(dimension_semantics=sem, vmem_limit_bytes=VMEM_LIMIT)


_DIMS = {"nn": (((1,), (0,)), ((), ())), "nt": (((1,), (1,)), ((), ())), "tn": (((0,), (0,)), ((), ()))}


def _mm(name, mode, a, bs, *, tm, tn, tk, out_dtypes=(F32,), reduce_g=False, resid=None, extras=(), epilogue=None,
        b_spec=None, n_n=None, after=(), fold_g=False, sub_m=1):
    assert sub_m == 1 or (mode != "tn" and tm % (8 * sub_m) == 0), (name, sub_m)
    nb = len(bs)
    ga = a.shape[0]
    gbs = [1 if b_spec is not None else b.shape[0] for b in bs]
    g_n = max([ga] + gbs)
    if mode == "tn":
        k_n, m_n = a.shape[1:]
    else:
        m_n, k_n = a.shape[1:]
    if n_n is None:
        n_n = bs[0].shape[1] if mode == "nt" else bs[0].shape[2]
    assert m_n % tm == 0 and n_n % tn == 0 and k_n % tk == 0, (name, m_n, n_n, k_n)
    mi, nj, kk = m_n // tm, n_n // tn, k_n // tk
    lead = g_n if fold_g else None
    if reduce_g:
        g_steps = 1 if fold_g else g_n
        grid = (mi, nj, g_steps, kk)
        ids = lambda i, j, g, k: (g, i, j, k)
        n_red = g_steps * kk
        red_idx = lambda: pl.program_id(2) * kk + pl.program_id(3)
        sem = ("parallel", "parallel", "arbitrary", "arbitrary")
    else:
        grid = (g_n, mi, nj, kk)
        ids = lambda g, i, j, k: (g, i, j, k)
        n_red = kk
        red_idx = lambda: pl.program_id(3)
        sem = ("parallel", "parallel", "parallel", "arbitrary")

    def pick(gsz, g):
        return g if gsz > 1 else 0

    def a_map(*p):
        g, i, j, k = ids(*p)
        return (pick(ga, g), k, i) if mode == "tn" else (pick(ga, g), i, k)

    def b_map(gsz):
        def f(*p):
            g, i, j, k = ids(*p)
            if b_spec is not None:
                return b_spec[1](g, i, j, k)
            return (pick(gsz, g), j, k) if mode == "nt" else (pick(gsz, g), k, j)
        return f

    def o_map(gsz):
        def f(*p):
            g, i, j, k = ids(*p)
            return (0 if reduce_g else pick(gsz, g), i, j)
        return f

    a_spec = pl.BlockSpec((lead, tk, tm) if mode == "tn" else (lead, tm, tk), a_map)
    b_block = b_spec[0] if b_spec is not None else ((lead, tn, tk) if mode == "nt" else (lead, tk, tn))
    b_specs = [pl.BlockSpec(b_block, b_map(gs)) for gs in gbs]
    x_specs = [pl.BlockSpec((None, tm, tn), o_map(e.shape[0])) for e in extras]
    r_specs = [pl.BlockSpec((None, tm, tn), o_map(resid.shape[0]))] if resid is not None else []
    g_out = 1 if reduce_g else g_n
    out_shape = [jax.ShapeDtypeStruct((g_out, m_n, n_n), dt) for dt in out_dtypes]
    out_specs = [pl.BlockSpec((None, tm, tn), o_map(g_out)) for _ in out_dtypes]
    nx, nr, no = len(extras), len(r_specs), len(out_dtypes)
    n_in = 1 + nb + nx + nr + len(after)
    dims = _DIMS[mode]

    def body(*refs):
        a_ref = refs[0]
        b_refs = refs[1:1 + nb]
        x_refs = refs[1 + nb:1 + nb + nx]
        r_refs = refs[1 + nb + nx:1 + nb + nx + nr]
        o_refs = refs[n_in:n_in + no]
        acc_refs = refs[n_in + no:]
        def dots(rows):
            if fold_g:
                return [sum(lax.dot_general(a_ref[g, rows, :], b_ref[g], dims, preferred_element_type=F32)
                            for g in range(g_n)) for b_ref in b_refs]
            av = a_ref[...] if mode == "tn" else a_ref[rows, :]
            return [lax.dot_general(av, b_ref[...], dims, preferred_element_type=F32) for b_ref in b_refs]

        def finish(accs, rows=slice(None)):
            if r_refs:
                accs[0] = accs[0] + r_refs[0][rows, :]
            outs = epilogue(accs, [x[rows, :] for x in x_refs]) if epilogue is not None else accs
            for o_ref, o in zip(o_refs, outs):
                o_ref[rows, :] = o.astype(o_ref.dtype)

        if n_red == 1:
            slabs = [slice(s * (tm // sub_m), (s + 1) * (tm // sub_m)) for s in range(sub_m)]
            ahead = dots(slabs[0])
            for s, rows in enumerate(slabs):
                now, ahead = ahead, (dots(slabs[s + 1]) if s + 1 < sub_m else None)
                finish(now, rows)
            return
        products = dots(slice(None))
        r = red_idx()
        for p, acc in zip(products, acc_refs):
            @pl.when(r == 0)
            def _():
                acc[...] = p

            @pl.when((r > 0) & (r < n_red - 1))
            def _():
                acc[...] += p

        @pl.when(r == n_red - 1)
        def _():
            finish([acc[...] + p for p, acc in zip(products, acc_refs)])

    return pl.pallas_call(
        body, name=name, grid=grid,
        in_specs=[a_spec] + b_specs + x_specs + r_specs + [ANY] * len(after),
        out_specs=out_specs, out_shape=out_shape,
        scratch_shapes=[pltpu.VMEM((tm, tn), F32) for _ in range(nb if n_red > 1 else 0)],
        compiler_params=_cparams(sem),
    )(a, *bs, *extras, *([resid] if resid is not None else []), *after)


def _sigmoid(x):
    return 1.0 / (1.0 + jnp.exp(-x))


def _silu(x):
    return x * _sigmoid(x)


def _gelu(x):
    return 0.5 * x * (1.0 + jnp.tanh(0.7978845608028654 * (x + 0.044715 * (x * x * x))))


def _rms_fn(h, gain):
    return h * lax.rsqrt(jnp.mean(h * h, axis=-1, keepdims=True) + EPS) * gain


def _shift_impl(x, s):
    n = x.shape[0]
    rolled = pltpu.roll(x, s % n, 0)
    row = lax.broadcasted_iota(jnp.int32, x.shape, 0)
    return jnp.where((row >= s) & (row < n + s), rolled, 0.0)


@functools.partial(jax.custom_vjp, nondiff_argnums=(1,))
def _shift(x, s):
    return _shift_impl(x, s)


def _shift_fwd(x, s):
    return _shift_impl(x, s), None


def _shift_bwd(s, _, g):
    return (_shift_impl(g, -s),)


_shift.defvjp(_shift_fwd, _shift_bwd)


def _prep_fn(x, w, qk_scale, is_v):
    y = x * w[3:4, :]
    for i in range(3):
        y = y + _shift(x, 3 - i) * w[i:i + 1, :]
    y = _silu(y)
    nrm = lax.rsqrt(jnp.sum(y * y, axis=-1, keepdims=True) + EPS) * qk_scale
    return y * jnp.where(is_v, 1.0, nrm)


def _softplus(x):
    return jnp.maximum(x, 0.0) + jnp.log(1.0 + jnp.exp(-jnp.abs(x)))


def _gates_fn(ba, a_log, dt_bias):
    lane = lax.broadcasted_iota(jnp.int32, ba.shape, 1)
    beta = _sigmoid(ba)
    g = -jnp.exp(a_log) * _softplus(ba + dt_bias)
    return jnp.where(lane < HEADS, beta, g)


def _dot16(a, b, dims=_DIMS["nn"]):
    return lax.dot_general(a.astype(BF16), b.astype(BF16), dims, preferred_element_type=F32)


def _dot32(a, b):
    return jnp.dot(a, b, preferred_element_type=F32, precision=HIGHEST)


def _dot3(a, b, dims=_DIMS["nn"]):
    return lax.dot_general(a, b, dims, preferred_element_type=F32, precision=lax.Precision.HIGH)


def _tri_inverses(mats):
    row = lax.broadcasted_iota(jnp.int32, (CH, CH), 0)
    col = lax.broadcasted_iota(jnp.int32, (CH, CH), 1)
    eye = (row == col).astype(F32)
    ts = [eye - a for a in mats]
    ps = list(mats)
    for _ in range(5):
        ps = [_dot3(p, p) for p in ps]
        ts = [t + _dot3(t, p) for t, p in zip(ts, ps)]
    return ts


@jax.custom_vjp
def _tri_solves(mats, rhs):
    return [_dot3(t, b) for t, b in zip(_tri_inverses(mats), rhs)]


def _tri_solves_fwd(mats, rhs):
    ts = _tri_inverses(mats)
    xs = [_dot3(t, b) for t, b in zip(ts, rhs)]
    return xs, (ts, xs)


def _tri_solves_bwd(res, dxs):
    ts, xs = res
    dbs = [_dot3(t, dx, _DIMS["tn"]) for t, dx in zip(ts, dxs)]
    return [-_dot3(db, x, _DIMS["nt"]) for db, x in zip(dbs, xs)], dbs


_tri_solves.defvjp(_tri_solves_fwd, _tri_solves_bwd)


def _chunk_prep_fn(xs, bgs):
    row = lax.broadcasted_iota(jnp.int32, (CH, CH), 0)
    col = lax.broadcasted_iota(jnp.int32, (CH, CH), 1)
    incl = row >= col
    strict = row > col
    lmat = incl.astype(F32)
    n = len(xs)
    items = [(i, h) for i in range(n) for h in range(HEADS)]
    part = lambda i, h, c: xs[i][:, c * HW + h * HD:c * HW + (h + 1) * HD]
    q = [part(i, h, 0) for i, h in items]
    k = [part(i, h, 1) for i, h in items]
    v = [part(i, h, 2) for i, h in items]
    beta = [bgs[i][:, h:h + 1] for i, h in items]
    gc_all = [_dot32(lmat, bg) for bg in bgs]
    gc = [gc_all[i][:, HEADS + h:HEADS + h + 1] for i, h in items]
    gmat = [jnp.where(strict, jnp.broadcast_to(bgs[i][:, HEADS + h:HEADS + h + 1], (CH, CH)), 0.0) for i, h in items]
    diff = [_dot3(lmat, m) for m in gmat]
    decay = [jnp.where(incl, jnp.exp(jnp.where(incl, d, 0.0)), 0.0) for d in diff]
    k_beta = [kk * b for kk, b in zip(k, beta)]
    kk_t = [_dot16(kb, kk, _DIMS["nt"]) for kb, kk in zip(k_beta, k)]
    qk_t = [_dot16(qq, kk, _DIMS["nt"]) for qq, kk in zip(q, k)]
    a = [jnp.where(strict, m * d, 0.0) for m, d in zip(kk_t, decay)]
    eg = [jnp.exp(g) for g in gc]
    rhs = [jnp.concatenate([vv * b, kb * e], axis=-1) for vv, b, kb, e in zip(v, beta, k_beta, eg)]
    uw = _tri_solves(a, rhs)
    qk = [m * d for m, d in zip(qk_t, decay)]
    g_last = [g[CH - 1:CH, :] for g in gc]
    qe = [qq * e for qq, e in zip(q, eg)]
    kd = [kk * jnp.exp(gl - g) for kk, gl, g in zip(k, g_last, gc)]
    egl = [jnp.broadcast_to(jnp.exp(gl), (1, HD)) for gl in g_last]
    out = []
    for i in range(n):
        mine = slice(i * HEADS, (i + 1) * HEADS)
        cat = lambda vals: jnp.concatenate(vals[mine], axis=-1)
        out.append((cat([x[:, :HD] for x in uw]), cat([x[:, HD:] for x in uw]), cat(qe), cat(kd),
                    jnp.concatenate([m[None] for m in qk[mine]], axis=0), cat(egl)))
    return out


def _chunk_state_fn(u, w, qe, kd, qk, egl, s):
    ws = [_dot16(a, b) for a, b in zip(w, s)]
    qs = [_dot16(a, b) for a, b in zip(qe, s)]
    v_new = [a - b for a, b in zip(u, ws)]
    o = [a + _dot16(b, c) for a, b, c in zip(qs, qk, v_new)]
    s_new = [a * e + _dot16(b, c, _DIMS["tn"]) for a, e, b, c in zip(s, egl, kd, v_new)]
    return o, s_new


def _mix_fn(o, z, ur, vr, ong, lng, lnb, ws, bst):
    row = lax.broadcasted_iota(jnp.int32, (GCH, GCH), 0)
    col = lax.broadcasted_iota(jnp.int32, (GCH, GCH), 1)
    causal = row >= col
    ug = _gelu(ur)
    vg = _gelu(vr)
    sls = [slice(h * HD, (h + 1) * HD) for h in range(HEADS)]
    oh = [o[:, sl] for sl in sls]
    oh = [x * lax.rsqrt(jnp.mean(x * x, axis=-1, keepdims=True) + EPS) for x in oh]
    outs_dn = [x * ong * _silu(z[:, sl]) for x, sl in zip(oh, sls)]
    vh = [vg[:, sl] for sl in sls]
    mu = [jnp.mean(x, axis=-1, keepdims=True) for x in vh]
    var = [jnp.mean(jnp.square(x - m), axis=-1, keepdims=True) for x, m in zip(vh, mu)]
    vn = [(x - m) * lax.rsqrt(s + EPS) * lng[:, sl] + lnb[:, sl] for x, m, s, sl in zip(vh, mu, var, sls)]
    mixed = [_dot16(jnp.where(causal, ws[h], 0.0), vn[h]) for h in range(HEADS)]
    outs_gm = [ug[:, sl] * (mixed[h] + bst[:, h:h + 1]) for h, sl in enumerate(sls)]
    return jnp.concatenate(outs_dn + outs_gm, axis=-1)


def _loss_fn(h, gain, tgt):
    y = _rms_fn(h, gain)
    return 0.5 * jnp.sum(jnp.mean(jnp.square(y - tgt), axis=-1))


RT = 256


def _rows(n=D):
    return pl.BlockSpec((RT, n), lambda i: (i, 0))


def _whole(shape):
    nd = len(shape)
    return pl.BlockSpec(shape, lambda i: (0,) * nd)


def _rmsnorm(name, h, gain):
    def body(h_ref, g_ref, o_ref):
        o_ref[...] = _rms_fn(h_ref[...], g_ref[...]).astype(BF16)

    return pl.pallas_call(
        body, name=name, grid=(T // RT,), in_specs=[_rows(), _whole((1, D))], out_specs=_rows(),
        out_shape=jax.ShapeDtypeStruct((T, D), BF16), compiler_params=_cparams(("parallel",)),
    )(h, gain)


def _rmsnorm_bwd(name, dhn, h, gain, resid):
    def body(dhn_ref, h_ref, g_ref, r_ref, dh_ref, dh16_ref, dg_ref):
        _, vjp = jax.vjp(_rms_fn, h_ref[...], g_ref[...])
        dh, dg = vjp(dhn_ref[...])
        dh = r_ref[...] + dh
        dh_ref[...] = dh
        dh16_ref[...] = dh.astype(BF16)

        @pl.when(pl.program_id(0) == 0)
        def _():
            dg_ref[...] = dg

        @pl.when(pl.program_id(0) > 0)
        def _():
            dg_ref[...] += dg

    return pl.pallas_call(
        body, name=name, grid=(T // RT,), in_specs=[_rows(), _rows(), _whole((1, D)), _rows()],
        out_specs=[_rows(), _rows(), _whole((1, D))],
        out_shape=[jax.ShapeDtypeStruct((T, D), F32), jax.ShapeDtypeStruct((T, D), BF16),
                   jax.ShapeDtypeStruct((1, D), F32)],
        compiler_params=_cparams(("arbitrary",)),
    )(dhn, h, gain, resid)


def _loss_head(h, gain, tgt):
    def body(h_ref, g_ref, t_ref, l_ref, dh_ref, dh16_ref, dg_ref):
        loss, vjp = jax.vjp(lambda hh, gg: _loss_fn(hh, gg, t_ref[...]), h_ref[...], g_ref[...])
        dh, dg = vjp(jnp.ones((), F32))
        dh_ref[...] = dh
        dh16_ref[...] = dh.astype(BF16)
        lv = jnp.full((1, LANE), loss, F32)

        @pl.when(pl.program_id(0) == 0)
        def _():
            dg_ref[...] = dg
            l_ref[...] = lv

        @pl.when(pl.program_id(0) > 0)
        def _():
            dg_ref[...] += dg
            l_ref[...] += lv

    return pl.pallas_call(
        body, name="loss_head", grid=(T // RT,), in_specs=[_rows(), _whole((1, D)), _rows()],
        out_specs=[_whole((1, LANE)), _rows(), _rows(), _whole((1, D))],
        out_shape=[jax.ShapeDtypeStruct((1, LANE), F32), jax.ShapeDtypeStruct((T, D), F32),
                   jax.ShapeDtypeStruct((T, D), BF16), jax.ShapeDtypeStruct((1, D), F32)],
        compiler_params=_cparams(("arbitrary",)),
    )(h, gain, tgt)


def _prep_flags():
    j = pl.program_id(0)
    qk_scale = jnp.where(j < HEADS, HD ** -0.5, 1.0).astype(F32)
    return qk_scale, j >= 2 * HEADS


def _prep(proj, conv_w):
    def body(x_ref, w_ref, o_ref):
        qk_scale, is_v = _prep_flags()
        o_ref[...] = _prep_fn(x_ref[...], w_ref[...], qk_scale, is_v)

    col = lambda j: (0, j)
    return pl.pallas_call(
        body, name="gdn_prep", grid=(3 * HEADS,),
        in_specs=[pl.BlockSpec((T, HD), col), pl.BlockSpec((4, HD), col)], out_specs=pl.BlockSpec((T, HD), col),
        out_shape=jax.ShapeDtypeStruct((T, 3 * HW), F32), compiler_params=_cparams(("parallel",)),
    )(proj, conv_w)


def _prep_bwd(proj, conv_w, dqkv, dproj):
    def body(x_ref, w_ref, d_ref, _, dx_ref, dw_ref):
        qk_scale, is_v = _prep_flags()
        _, vjp = jax.vjp(lambda x, w: _prep_fn(x, w, qk_scale, is_v), x_ref[...], w_ref[...])
        dx, dw = vjp(d_ref[...])
        dx_ref[...] = dx.astype(BF16)
        dw_ref[...] = dw

    col = lambda j: (0, j)
    return pl.pallas_call(
        body, name="gdn_prep_bwd", grid=(3 * HEADS,),
        in_specs=[pl.BlockSpec((T, HD), col), pl.BlockSpec((4, HD), col), pl.BlockSpec((T, HD), col), ANY],
        out_specs=[pl.BlockSpec((T, HD), col), pl.BlockSpec((4, HD), col)],
        out_shape=[jax.ShapeDtypeStruct((T, NP), BF16), jax.ShapeDtypeStruct((4, 3 * HW), F32)],
        input_output_aliases={3: 0}, compiler_params=_cparams(("parallel",)),
    )(proj, conv_w, dqkv, dproj)


BA_BLK = BA_OFF // LANE


def _gates(proj, a_log, dt_bias):
    def body(x_ref, a_ref, d_ref, o_ref):
        o_ref[...] = _gates_fn(x_ref[...], a_ref[...], d_ref[...])

    return pl.pallas_call(
        body, name="gdn_gates", grid=(1,),
        in_specs=[pl.BlockSpec((T, LANE), lambda i: (0, BA_BLK)), _whole((1, LANE)), _whole((1, LANE))],
        out_specs=_whole((T, LANE)),
        out_shape=jax.ShapeDtypeStruct((T, LANE), F32), compiler_params=_cparams(("arbitrary",)),
    )(proj, a_log, dt_bias)


def _gates_bwd(proj, a_log, dt_bias, dbg, dproj):
    def body(x_ref, a_ref, d_ref, dbg_ref, _, dx_ref, da_ref, dd_ref):
        _, vjp = jax.vjp(_gates_fn, x_ref[...], a_ref[...], d_ref[...])
        dx, da_ref[...], dd_ref[...] = vjp(dbg_ref[...])
        dx_ref[...] = dx.astype(BF16)

    ba = pl.BlockSpec((T, LANE), lambda i: (0, BA_BLK))
    return pl.pallas_call(
        body, name="gdn_gates_bwd", grid=(1,),
        in_specs=[ba, _whole((1, LANE)), _whole((1, LANE)), _whole((T, LANE)), ANY],
        out_specs=[ba, _whole((1, LANE)), _whole((1, LANE))],
        out_shape=[jax.ShapeDtypeStruct((T, NP), BF16), jax.ShapeDtypeStruct((1, LANE), F32),
                   jax.ShapeDtypeStruct((1, LANE), F32)],
        input_output_aliases={4: 0}, compiler_params=_cparams(("arbitrary",)),
    )(proj, a_log, dt_bias, dbg, dproj)


NCK = T // CH
CPS = 2


def _chunk_prep_specs(rev=False):
    at = (lambda n: NCK - 1 - n) if rev else (lambda n: n)
    wide = pl.BlockSpec((CH, HW), lambda n: (at(n), 0))
    return [wide, wide, wide, wide, pl.BlockSpec((HEADS, CH, CH), lambda n: (0, at(n), 0)),
            pl.BlockSpec((None, 1, HW), lambda n: (at(n), 0, 0))]


def _chunk_prep_shapes(dtypes):
    shp = [(T, HW), (T, HW), (T, HW), (T, HW), (HEADS, T, CH), (NCK, 1, HW)]
    return [jax.ShapeDtypeStruct(s, dt) for s, dt in zip(shp, dtypes)]


def _chunk_prep(qkv, bg):
    def body(x_ref, bg_ref, *o_refs):
        rows = [slice(ci * CH, (ci + 1) * CH) for ci in range(CPS)]
        res = _chunk_prep_fn([x_ref[r, :] for r in rows], [bg_ref[r, :] for r in rows])
        for ci, (u, w, qe, kd, qk, egl) in enumerate(res):
            for o_ref, val in zip(o_refs[:4], (u, w, qe, kd)):
                o_ref[rows[ci], :] = val.astype(o_ref.dtype)
            o_refs[4][:, rows[ci], :] = qk.astype(BF16)
            o_refs[5][ci] = egl

    wide = pl.BlockSpec((CPS * CH, HW), lambda n: (n, 0))
    return pl.pallas_call(
        body, name="gdn_chunk_prep", grid=(NCK // CPS,),
        in_specs=[pl.BlockSpec((CPS * CH, 3 * HW), lambda n: (n, 0)), pl.BlockSpec((CPS * CH, LANE), lambda n: (n, 0))],
        out_specs=[wide, wide, wide, wide, pl.BlockSpec((HEADS, CPS * CH, CH), lambda n: (0, n, 0)),
                   pl.BlockSpec((CPS, 1, HW), lambda n: (n, 0, 0))],
        out_shape=_chunk_prep_shapes((F32, BF16, BF16, BF16, BF16, F32)),
        compiler_params=_cparams(("parallel",)),
    )(qkv, bg)


def _chunk_prep_bwd(qkv, bg, cots):
    def body(x_ref, bg_ref, du, dw, dqe, dkd, dqk, degl, dx_ref, dbg_ref):
        rows = [slice(ci * CH, (ci + 1) * CH) for ci in range(CPS)]
        _, vjp = jax.vjp(_chunk_prep_fn, [x_ref[r, :] for r in rows], [bg_ref[r, :] for r in rows])
        dxs, dbgs = vjp([(du[r, :], dw[r, :], dqe[r, :], dkd[r, :], dqk[:, r, :], degl[ci])
                         for ci, r in enumerate(rows)])
        for r, dx, dbg in zip(rows, dxs, dbgs):
            dx_ref[r, :] = dx
            dbg_ref[r, :] = dbg

    wide = pl.BlockSpec((CPS * CH, HW), lambda n: (n, 0))
    return pl.pallas_call(
        body, name="gdn_chunk_prep_bwd", grid=(NCK // CPS,),
        in_specs=[pl.BlockSpec((CPS * CH, 3 * HW), lambda n: (n, 0)), pl.BlockSpec((CPS * CH, LANE), lambda n: (n, 0)),
                  wide, wide, wide, wide, pl.BlockSpec((HEADS, CPS * CH, CH), lambda n: (0, n, 0)),
                  pl.BlockSpec((CPS, 1, HW), lambda n: (n, 0, 0))],
        out_specs=[pl.BlockSpec((CPS * CH, 3 * HW), lambda n: (n, 0)), pl.BlockSpec((CPS * CH, LANE), lambda n: (n, 0))],
        out_shape=[jax.ShapeDtypeStruct((T, 3 * HW), F32), jax.ShapeDtypeStruct((T, LANE), F32)],
        compiler_params=_cparams(("parallel",)),
    )(qkv, bg, *cots)


def _head_args(refs):
    u, w, qe, kd, qk, egl = refs
    sls = [slice(h * HD, (h + 1) * HD) for h in range(HEADS)]
    return ([u[:, sl] for sl in sls], [w[:, sl].astype(F32) for sl in sls], [qe[:, sl].astype(F32) for sl in sls],
            [kd[:, sl].astype(F32) for sl in sls], [qk[h].astype(F32) for h in range(HEADS)],
            [egl[:, sl] for sl in sls])


def _chunk_scan(prep, after=()):
    def body(*refs):
        o_ref, sh_ref, s_ref = refs[6 + len(after):]

        @pl.when(pl.program_id(0) == 0)
        def _():
            s_ref[...] = jnp.zeros_like(s_ref)

        s = [s_ref[h] for h in range(HEADS)]
        for h in range(HEADS):
            sh_ref[h, 0] = s[h]
        o, s_new = _chunk_state_fn(*_head_args(refs[:6]), s)
        for h in range(HEADS):
            o_ref[:, h * HD:(h + 1) * HD] = o[h]
            s_ref[h] = s_new[h]

    return pl.pallas_call(
        body, name="gdn_scan", grid=(NCK,), in_specs=_chunk_prep_specs() + [ANY] * len(after),
        out_specs=[pl.BlockSpec((CH, HW), lambda n: (n, 0)), pl.BlockSpec((HEADS, 1, HD, HD), lambda n: (0, n, 0, 0))],
        out_shape=[jax.ShapeDtypeStruct((T, HW), F32), jax.ShapeDtypeStruct((HEADS, NCK, HD, HD), F32)],
        scratch_shapes=[pltpu.VMEM((HEADS, HD, HD), F32)], compiler_params=_cparams(("arbitrary",)),
    )(*prep, *after)


def _chunk_scan_bwd(prep, s_hist, do, after=()):
    n_in = 8 + len(after)

    def body(*refs):
        sh_ref, do_ref = refs[6:8]
        d_refs = refs[n_in:n_in + 6]
        ds_ref = refs[n_in + 6]

        @pl.when(pl.program_id(0) == 0)
        def _():
            ds_ref[...] = jnp.zeros_like(ds_ref)

        sls = [slice(h * HD, (h + 1) * HD) for h in range(HEADS)]
        _, vjp = jax.vjp(_chunk_state_fn, *_head_args(refs[:6]), [sh_ref[h, 0] for h in range(HEADS)])
        du, dw, dqe, dkd, dqk, degl, ds = vjp(([do_ref[:, sl] for sl in sls], [ds_ref[h] for h in range(HEADS)]))
        for h, sl in enumerate(sls):
            for d_ref, val in zip(d_refs[:4], (du, dw, dqe, dkd)):
                d_ref[:, sl] = val[h]
            d_refs[4][h] = dqk[h]
            d_refs[5][:, sl] = degl[h]
            ds_ref[h] = ds[h]

    rev = lambda n: NCK - 1 - n
    return pl.pallas_call(
        body, name="gdn_scan_bwd", grid=(NCK,),
        in_specs=_chunk_prep_specs(rev=True) + [pl.BlockSpec((HEADS, 1, HD, HD), lambda n: (0, rev(n), 0, 0)),
                                                pl.BlockSpec((CH, HW), lambda n: (rev(n), 0))] + [ANY] * len(after),
        out_specs=_chunk_prep_specs(rev=True), out_shape=_chunk_prep_shapes((F32,) * 6),
        scratch_shapes=[pltpu.VMEM((HEADS, HD, HD), F32)], compiler_params=_cparams(("arbitrary",)),
    )(*prep, s_hist, do, *after)


def _mix_specs():
    pc = lambda c: pl.BlockSpec((GCH, HW), lambda i: (i, c))
    return [pl.BlockSpec((GCH, HW), lambda i: (i, 0)), pc(3), pc(4), pc(5), _whole((1, HD)), _whole((1, HW)),
            _whole((1, HW)), _whole((HEADS, GCH, GCH)), _whole((GCH, LANE))]


def _mix(o, proj, ong, lng, lnb, ws, bst):
    def body(o_ref, z_ref, u_ref, v_ref, ong_ref, lng_ref, lnb_ref, ws_ref, bs_ref, m_ref):
        m_ref[...] = _mix_fn(o_ref[...], z_ref[...], u_ref[...], v_ref[...], ong_ref[...], lng_ref[...],
                             lnb_ref[...], ws_ref[...], bs_ref[...]).astype(BF16)

    return pl.pallas_call(
        body, name="mix", grid=(T // GCH,), in_specs=_mix_specs(),
        out_specs=pl.BlockSpec((GCH, D), lambda i: (i, 0)), out_shape=jax.ShapeDtypeStruct((T, D), BF16),
        compiler_params=_cparams(("parallel",)),
    )(o, proj, proj, proj, ong, lng, lnb, ws, bst)


def _mix_bwd(o, proj, ong, lng, lnb, ws, bst, dmix):
    def body(o_ref, z_ref, u_ref, v_ref, ong_ref, lng_ref, lnb_ref, ws_ref, bs_ref, dm_ref,
             do_ref, dzuv_ref, dong_ref, dlng_ref, dlnb_ref, dws_ref, dbs_ref):
        _, vjp = jax.vjp(_mix_fn, o_ref[...], z_ref[...], u_ref[...], v_ref[...], ong_ref[...], lng_ref[...],
                         lnb_ref[...], ws_ref[...], bs_ref[...])
        do, dz, du, dv, dong, dlng, dlnb, dws, dbs = vjp(dm_ref[...])
        do_ref[...] = do
        dzuv_ref[:, 0:HW] = dz.astype(BF16)
        dzuv_ref[:, HW:2 * HW] = du.astype(BF16)
        dzuv_ref[:, 2 * HW:3 * HW] = dv.astype(BF16)
        acc = [(dong_ref, dong), (dlng_ref, dlng), (dlnb_ref, dlnb), (dws_ref, dws), (dbs_ref, dbs)]

        @pl.when(pl.program_id(0) == 0)
        def _():
            for r, val in acc:
                r[...] = val

        @pl.when(pl.program_id(0) > 0)
        def _():
            for r, val in acc:
                r[...] += val

    shp = lambda *s: jax.ShapeDtypeStruct(s, F32)
    return pl.pallas_call(
        body, name="mix_bwd", grid=(T // GCH,),
        in_specs=_mix_specs() + [pl.BlockSpec((GCH, D), lambda i: (i, 0))],
        out_specs=[pl.BlockSpec((GCH, HW), lambda i: (i, 0)), pl.BlockSpec((GCH, 3 * HW), lambda i: (i, 1)),
                   _whole((1, HD)), _whole((1, HW)), _whole((1, HW)), _whole((HEADS, GCH, GCH)), _whole((GCH, LANE))],
        out_shape=[shp(T, HW), jax.ShapeDtypeStruct((T, NP), BF16), shp(1, HD), shp(1, HW), shp(1, HW),
                   shp(HEADS, GCH, GCH), shp(GCH, LANE)],
        compiler_params=_cparams(("arbitrary",)),
    )(o, proj, proj, proj, ong, lng, lnb, ws, bst, dmix)


def _swiglu_epilogue(accs, _):
    gate, up = accs
    return [gate, up, _silu(gate) * up]


def _swiglu_bwd_epilogue(accs, extras):
    dact = accs[0]
    gate, up = (e.astype(F32) for e in extras)
    sg = _sigmoid(gate)
    return [dact * up * (sg * (1.0 + gate * (1.0 - sg))), dact * (gate * sg)]


def _layer_fwd(h, p):
    hn = p.pop("hn") if "hn" in p else _rmsnorm("rms_mix", h, p["norm_mix"])
    proj = _mm("in_proj", "nn", hn[None], [p["w_in"][None]], tm=1024, tn=640, tk=D, sub_m=2)[0][0]
    qkv = _prep(proj, p["conv_w"])
    bg = _gates(proj, p["a_log"], p["dt_bias"])
    prep = _chunk_prep(qkv, bg)
    o, s_hist = _chunk_scan(prep, p.pop("before_scan")(prep[0]) if "before_scan" in p else ())
    if "late" in p:
        p.update(p.pop("late")(o))
    mix = _mix(o, proj, p["o_norm_g"], p["ln_v_g"], p["ln_v_b"], p["w_s"], p["bst"])
    h1 = _mm("out_proj", "nn", mix[None], [p["w_out"].reshape(1, D, D)], tm=1024, tn=512, tk=D, resid=h[None],
             sub_m=2)[0][0]
    h2n = _rmsnorm("rms_ffn", h1, p["norm_ffn"])
    gate, up, act = _mm("ffn_in", "nt", h2n[None], [p["w_gate"], p["w_up"]], tm=1024, tn=FF_SH, tk=D,
                        out_dtypes=(BF16, BF16, BF16), epilogue=_swiglu_epilogue, sub_m=4)
    then = p.pop("before_ffn_out")(act) if "before_ffn_out" in p else ()
    h2 = _mm("ffn_out", "nn", act, [p["w_down"]], tm=1024, tn=512, tk=FF_SH, reduce_g=True, fold_g=True,
             resid=h1[None], sub_m=2, after=then)[0][0]
    saved = dict(h=h, hn=hn, proj=proj, qkv=qkv, bg=bg, prep=prep, o=o, s_hist=s_hist, mix=mix, h1=h1, h2n=h2n,
                 gate=gate, up=up, act=act)
    return h2, saved


def _layer_bwd_ffn(dh2, dh2b, p, s, after=()):
    dh2b = dh2b[None]
    dgate, dup = _mm("ffn_out_bwd", "nt", dh2b, [p["w_down"]], tm=1024, tn=FF_SH, tk=D, out_dtypes=(BF16, BF16),
                     extras=(s["gate"], s["up"]), epilogue=_swiglu_bwd_epilogue, after=after, sub_m=4)
    dh2n = _mm("ffn_gate_bwd", "nn", dgate, [p["w_gate"]], tm=1024, tn=512, tk=FF_SH, reduce_g=True, fold_g=True,
               sub_m=2)[0]
    dh2n = _mm("ffn_up_bwd", "nn", dup, [p["w_up"]], tm=1024, tn=512, tk=FF_SH, reduce_g=True, fold_g=True,
               resid=dh2n, sub_m=2)[0][0]
    dh1, dh1b, d_norm_ffn = _rmsnorm_bwd("rms_ffn_bwd", dh2n, s["h1"], p["norm_ffn"], dh2)
    d_w_down = _mm("ffn_wdown_grad", "tn", s["act"], [dh2b], tm=FF_SH, tn=512, tk=T)[0]
    d_w_gate = _mm("ffn_wgate_grad", "tn", dgate, [s["h2n"][None]], tm=FF_SH, tn=512, tk=T)[0]
    d_w_up = _mm("ffn_wup_grad", "tn", dup, [s["h2n"][None]], tm=FF_SH, tn=512, tk=T)[0]
    return dh1, dh1b, dict(norm_ffn=d_norm_ffn, w_gate=d_w_gate, w_up=d_w_up, w_down=d_w_down)


def _layer_bwd_mixer(dh1, dh1b, p, s, after=(), midway=None, late=None):
    dh1b = dh1b[None]
    dmix = _mm("out_proj_bwd", "nt", dh1b, [p["w_out"].reshape(1, D, D)], tm=1024, tn=512, tk=D, after=after,
               sub_m=2)[0][0]
    d_w_out = _mm("out_proj_wgrad", "tn", s["mix"][None], [dh1b], tm=512, tn=512, tk=T)[0][0]
    do, dproj, d_ong, d_lng, d_lnb, d_ws, d_bst = _mix_bwd(
        s["o"], s["proj"], p["o_norm_g"], p["ln_v_g"], p["ln_v_b"], p["w_s"], p["bst"], dmix)
    then = midway(do) if midway is not None else ()
    dqkv, dbg = _chunk_prep_bwd(s["qkv"], s["bg"], _chunk_scan_bwd(s["prep"], s["s_hist"], do, then))
    dproj, d_conv = _prep_bwd(s["proj"], p["conv_w"], dqkv, dproj)
    dproj, d_a_log, d_dt_bias = _gates_bwd(s["proj"], p["a_log"], p["dt_bias"], dbg, dproj)
    dproj = dproj[None]
    d_w_in = _mm("in_proj_wgrad", "tn", s["hn"][None], [dproj], tm=512, tn=640, tk=T)[0]
    last = late(dict(w_in=d_w_in, w_out=d_w_out)) if late is not None else ()
    dhn = _mm("in_proj_bwd", "nt", dproj, [p["w_in"][None]], tm=1024, tn=512, tk=NP, after=last,
              sub_m=2)[0][0]
    dh, dhb, d_norm_mix = _rmsnorm_bwd("rms_mix_bwd", dhn, s["h"], p["norm_mix"], dh1)
    grads = dict(norm_mix=d_norm_mix, w_in=d_w_in, conv_w=d_conv, a_log=d_a_log, dt_bias=d_dt_bias, o_norm_g=d_ong,
                 ln_v_g=d_lng, ln_v_b=d_lnb, w_s=d_ws, bst=d_bst, w_out=d_w_out)
    return dh, dhb, grads


def _lanes(v, off=0):
    return jnp.zeros((1, LANE), F32).at[0, off:off + v.shape[0]].set(v)


def _w_in_pieces():
    regions = [(0, 2048, 0), (2048, 2056, BA_OFF), (2056, IN_DIM, 2048)]
    sh = IN_DIM // NCHIP
    out = []
    for j in range(NCHIP):
        for lo, hi, at in regions:
            a, b = max(lo, j * sh), min(hi, (j + 1) * sh)
            if a < b:
                out.append((j, a - j * sh, at + a - lo, b - a))
    return out


W_IN_PIECES = _w_in_pieces()
WT = 256


def _assemble_w_in(gathered, own, place):
    def body(place_ref, g_ref, own_ref, o_ref):
        o_ref[:, IN_DIM:] = jnp.zeros((WT, NP - IN_DIM), BF16)
        mine = own_ref[...]
        for j, src, dst, width in W_IN_PIECES:
            val = jnp.where(place_ref[0] == j, mine[:, src:src + width], g_ref[j, :, src:src + width])
            o_ref[:, dst:dst + width] = val

    sh = IN_DIM // NCHIP
    return pl.pallas_call(
        body, name="assemble_w_in",
        grid_spec=pltpu.PrefetchScalarGridSpec(
            num_scalar_prefetch=1, grid=(D // WT,),
            in_specs=[pl.BlockSpec((NCHIP, WT, sh), lambda i, place_ref: (0, i, 0)),
                      pl.BlockSpec((WT, sh), lambda i, place_ref: (i, 0))],
            out_specs=pl.BlockSpec((WT, NP), lambda i, place_ref: (i, 0))),
        out_shape=jax.ShapeDtypeStruct((D, NP), BF16), compiler_params=_cparams(("parallel",)),
    )(place, gathered, own)


def _layer_params(l, big, small):
    return dict(
        {k: v for k, v in big.items() if k != "conv_w"},
        conv_w=jnp.concatenate([big["conv_w"][j, l] for j in range(NCHIP)], axis=1),
        norm_mix=small["norm_mix"][l][None], norm_ffn=small["norm_ffn"][l][None],
        a_log=_lanes(small["a_log"][l], HEADS), dt_bias=_lanes(small["dt_bias"][l], HEADS),
        o_norm_g=small["o_norm_g"][l][None], ln_v_g=small["ln_v_g"][l][None], ln_v_b=small["ln_v_b"][l][None],
        w_s=small["w_s"][l],
        bst=jnp.pad(small["b_s"][l].T, ((0, 0), (0, LANE - HEADS))),
    )


def _reference_layout(g):
    return dict(
        w_in=g["w_in"],
        w_out=g["w_out"].reshape(NCHIP, D // NCHIP, D),
        w_gate=g["w_gate"], w_up=g["w_up"], w_down=g["w_down"],
        conv_w=g["conv_w"], norm_mix=g["norm_mix"][0], norm_ffn=g["norm_ffn"][0],
        a_log=g["a_log"][0, HEADS:2 * HEADS], dt_bias=g["dt_bias"][0, HEADS:2 * HEADS],
        o_norm_g=g["o_norm_g"][0], ln_v_g=g["ln_v_g"][0], ln_v_b=g["ln_v_b"][0], w_s=g["w_s"],
        b_s=g["bst"][:, :HEADS].T,
    )


def _forward(x, tgt, layers, norm_final):
    h = x
    saved, params = [], []
    for p in layers:
        p = p(h) if callable(p) else p
        h, s = _layer_fwd(h, p)
        saved.append(s)
        params.append(p)
    return (saved, params) + tuple(_loss_head(h, norm_final, tgt))


def _local_step(x, tgt, layers, norm_final):
    saved, layers, loss, dh, dhb, d_norm_final = _forward(x, tgt, layers, norm_final)
    grads = [None] * DEPTH
    for l in reversed(range(DEPTH)):
        dh1, dh1b, g_ffn = _layer_bwd_ffn(dh, dhb, layers[l], saved[l])
        dh, dhb, g_mix = _layer_bwd_mixer(dh1, dh1b, layers[l], saved[l])
        grads[l] = {**g_ffn, **g_mix}
    return loss, dh, grads, d_norm_final


def _place():
    x, y, c = lax.axis_index("x"), lax.axis_index("y"), lax.axis_index("c")
    return x, y, c, [(1 - x, y), (x, 1 - y), (1 - x, 1 - y)]


def _remote(src, dst, send_sem, recv_sem, to):
    return pltpu.make_async_remote_copy(src_ref=src, dst_ref=dst, send_sem=send_sem, recv_sem=recv_sem,
                                        device_id=to, device_id_type=MESH)


def _comm_call(name, body, ins, out_shape, n_sems, aliases=None):
    return pl.pallas_call(
        body, name=name, in_specs=[ANY] * len(ins), out_specs=[ANY] * len(out_shape), out_shape=out_shape,
        scratch_shapes=[pltpu.SemaphoreType.DMA((n,)) for n in n_sems], input_output_aliases=aliases or {},
        compiler_params=pltpu.CompilerParams(has_side_effects=True),
    )(*ins)


def _half_rows(ref, of_c, dim):
    hr = ref.shape[dim] // 2
    return pl.ds(pl.multiple_of(of_c * hr, BF16_ROWS), hr)


def _gather_plan(whole):
    def plan(srcs, lands):
        x, y, c, others = _place()
        chip = 2 * x + y
        out = []
        for src, land, all_of_it in zip(srcs, lands, whole):
            for ox, oy in others:
                if all_of_it:
                    out.append((src, land.at[chip], (ox, oy, c)))
                else:
                    out.append((src.at[_half_rows(src, c, 0)], land.at[chip, _half_rows(src, c, 0)], (ox, oy, c)))
        return out
    return plan


def _forward_halves(lands):
    n = len(lands)

    def body(*refs):
        outs = refs[n:2 * n]
        send_s, recv_s = refs[2 * n:]
        x, y, c, others = _place()
        sibling = (x, y, 1 - c)
        copies = []
        for a in range(n):
            for k, (ox, oy) in enumerate(others):
                mine = outs[a].at[2 * ox + oy, _half_rows(outs[a], c, 1)]
                copies.append(_remote(mine, mine, send_s.at[3 * a + k], recv_s.at[3 * a + k], sibling))
        for cp in copies:
            cp.start()
        for a in range(n):
            for k, (ox, oy) in enumerate(others):
                landed = outs[a].at[2 * ox + oy, _half_rows(outs[a], 1 - c, 1)]
                _remote(landed, landed, send_s.at[3 * a + k], recv_s.at[3 * a + k], sibling).wait_recv()
        for cp in copies:
            cp.wait_send()

    out_shape = [jax.ShapeDtypeStruct(g.shape, g.dtype) for g in lands]
    return _comm_call("forward_halves", body, lands, out_shape, [3 * n, 3 * n], aliases={a: a for a in range(n)})


def _forward_refs(bufs, incoming):
    x, y, c, others = _place()
    return (x, y, 1 - c), [b.at[2 * ox + oy, _half_rows(b, 1 - c if incoming else c, 1)]
                           for b in bufs for ox, oy in others]


def _forward_start(name, bufs, after):
    n = len(bufs)
    bufs = [pltpu.with_memory_space_constraint(b, pltpu.HBM) for b in bufs]

    def body(*refs):
        send_s, recv_s = refs[n + len(after)], refs[n + len(after) + 1]
        sibling, mine = _forward_refs(refs[:n], incoming=False)
        for i, ref in enumerate(mine):
            _remote(ref, ref, send_s.at[i], recv_s.at[i], sibling).start()
        refs[-1][...] = jnp.zeros_like(refs[-1])

    out = pl.pallas_call(
        body, name=name, in_specs=[HBM_SPEC] * n + [ANY] * len(after),
        out_specs=[SEM_SPEC, SEM_SPEC] + [HBM_SPEC] * n + [pl.BlockSpec(memory_space=pltpu.VMEM)],
        out_shape=[pltpu.SemaphoreType.DMA((3 * n,)), pltpu.SemaphoreType.DMA((3 * n,))]
        + [pltpu.HBM(b.shape, b.dtype) for b in bufs] + [jax.ShapeDtypeStruct((F32_ROWS, LANE), F32)],
        input_output_aliases={i: 2 + i for i in range(n)},
        compiler_params=pltpu.CompilerParams(has_side_effects=DATAFLOW),
    )(*bufs, *after)
    return dict(sems=out[:2], bufs=out[2:2 + n], token=out[-1])


def _forward_wait(name, started, after):
    n = len(started["bufs"])

    def body(*refs):
        send_s, recv_s = refs[n], refs[n + 1]
        sibling, mine = _forward_refs(refs[:n], incoming=False)
        _, theirs = _forward_refs(refs[:n], incoming=True)
        for i, (sent, landed) in enumerate(zip(mine, theirs)):
            _remote(sent, sent, send_s.at[i], recv_s.at[i], sibling).wait_send()
            _remote(landed, landed, send_s.at[i], recv_s.at[i], sibling).wait_recv()

    return pl.pallas_call(
        body, name=name, in_specs=[HBM_SPEC] * n + [SEM_SPEC, SEM_SPEC] + [ANY] * len(after),
        out_specs=[HBM_SPEC] * n, out_shape=[pltpu.HBM(b.shape, b.dtype) for b in started["bufs"]],
        input_output_aliases={i: i for i in range(n)},
        compiler_params=pltpu.CompilerParams(has_side_effects=DATAFLOW),
    )(*started["bufs"], *started["sems"], *after)


HBM_SPEC = pl.BlockSpec(memory_space=pltpu.HBM)
SEM_SPEC = pl.BlockSpec(memory_space=pltpu.SEMAPHORE)
DATAFLOW = pltpu.SideEffectType.DATAFLOW_SIDE_EFFECTING


def _exchange_plan(srcs, lands):
    x, y, c, _ = _place()
    plan = []
    for src, land in zip(srcs, lands):
        hr = src.shape[1] // 2
        plan.append((src.at[:, pl.ds(pl.multiple_of((1 - c) * hr, 8), hr)], land, (x, y, 1 - c)))
    return plan


def _scatter_plan(srcs, lands):
    x, y, c, others = _place()
    return [(src.at[2 * ox + oy], land.at[k], (ox, oy, c))
            for src, land in zip(srcs, lands) for k, (ox, oy) in enumerate(others)]


def _split_start(name, plan, srcs, land_shapes, n_copies, after=()):
    n = len(srcs)
    lands = [pltpu.with_memory_space_constraint(lax.empty(s.shape, s.dtype), pltpu.HBM) for s in land_shapes]
    srcs = [pltpu.with_memory_space_constraint(s, pltpu.HBM) for s in srcs]

    def body(*refs):
        send_s, recv_s = refs[2 * n + len(after)], refs[2 * n + len(after) + 1]
        for i, (src, dst, to) in enumerate(plan(refs[:n], refs[n:2 * n])):
            _remote(src, dst, send_s.at[i], recv_s.at[i], to).start()
        refs[-1][...] = jnp.zeros_like(refs[-1])

    thru = [pltpu.HBM(s.shape, s.dtype) for s in srcs + lands]
    out = pl.pallas_call(
        body, name=name, in_specs=[HBM_SPEC] * (2 * n) + [ANY] * len(after),
        out_specs=[SEM_SPEC, SEM_SPEC] + [HBM_SPEC] * (2 * n) + [pl.BlockSpec(memory_space=pltpu.VMEM)],
        out_shape=[pltpu.SemaphoreType.DMA((n_copies,)), pltpu.SemaphoreType.DMA((n_copies,))] + thru
        + [jax.ShapeDtypeStruct((F32_ROWS, LANE), F32)],
        input_output_aliases={i: 2 + i for i in range(2 * n)},
        compiler_params=pltpu.CompilerParams(has_side_effects=DATAFLOW),
    )(*srcs, *lands, *after)
    return dict(sems=out[:2], srcs=out[2:2 + n], lands=out[2 + n:2 + 2 * n], token=out[-1])


def _split_wait(name, plan, started, after):
    n = len(started["srcs"])
    after = list(after) if isinstance(after, (list, tuple)) else [after]

    def body(*refs):
        send_s, recv_s = refs[2 * n], refs[2 * n + 1]
        for i, (src, dst, to) in enumerate(plan(refs[:n], refs[n:2 * n])):
            cp = _remote(src, dst, send_s.at[i], recv_s.at[i], to)
            cp.wait_send()
            cp.wait_recv()

    arrs = list(started["srcs"]) + list(started["lands"])
    out = pl.pallas_call(
        body, name=name, in_specs=[HBM_SPEC] * (2 * n) + [SEM_SPEC, SEM_SPEC] + [ANY] * len(after),
        out_specs=[HBM_SPEC] * (2 * n), out_shape=[pltpu.HBM(s.shape, s.dtype) for s in arrs],
        input_output_aliases={i: i for i in range(2 * n)},
        compiler_params=pltpu.CompilerParams(has_side_effects=DATAFLOW),
    )(*arrs, *started["sems"], *after)
    return out[:n], out[n:]


def _join_halves(name, rs):
    n = len(rs)

    def body(*refs):
        outs = refs[n:2 * n]
        send_s, recv_s = refs[2 * n:]
        x, y, c, _ = _place()
        sibling = (x, y, 1 - c)

        def half(a, of_c):
            hr = outs[a].shape[1] // 2
            return outs[a].at[:, pl.ds(pl.multiple_of(of_c * hr, 8), hr)]

        copies = [_remote(half(a, c), half(a, c), send_s.at[a], recv_s.at[a], sibling) for a in range(n)]
        for cp in copies:
            cp.start()
        for a in range(n):
            landed = half(a, 1 - c)
            _remote(landed, landed, send_s.at[a], recv_s.at[a], sibling).wait_recv()
        for cp in copies:
            cp.wait_send()

    out_shape = [jax.ShapeDtypeStruct(r.shape, r.dtype) for r in rs]
    return _comm_call(name, body, rs, out_shape, [n, n], aliases={a: a for a in range(n)})


def _allreduce_small(buf, after=()):
    r = buf.shape[0]
    hr = r // 2

    def body(in_ref, *refs):
        out_ref, theirs, by_chip, send_s, recv_s = refs[len(after):]
        x, y, c, others = _place()
        chip = 2 * x + y
        sibling = (x, y, 1 - c)
        mine = pl.ds(pl.multiple_of(c * hr, F32_ROWS), hr)
        swap = _remote(in_ref, theirs, send_s.at[0], recv_s.at[0], sibling)
        swap.start()
        swap.wait()
        by_chip[chip] = in_ref[mine, :] + theirs[mine, :]
        sends = [_remote(by_chip.at[chip], by_chip.at[chip], send_s.at[1 + k], recv_s.at[1 + k], (ox, oy, c))
                 for k, (ox, oy) in enumerate(others)]
        for cp in sends:
            cp.start()
        for k, (ox, oy) in enumerate(others):
            landed = by_chip.at[2 * ox + oy]
            _remote(landed, landed, send_s.at[1 + k], recv_s.at[1 + k], (ox, oy, c)).wait_recv()
        for cp in sends:
            cp.wait_send()
        out_ref[mine, :] = (by_chip[0] + by_chip[1]) + (by_chip[2] + by_chip[3])
        back = _remote(out_ref.at[mine], out_ref.at[mine], send_s.at[NCHIP], recv_s.at[NCHIP], sibling)
        back.start()
        other = out_ref.at[pl.ds(pl.multiple_of((1 - c) * hr, F32_ROWS), hr)]
        _remote(other, other, send_s.at[NCHIP], recv_s.at[NCHIP], sibling).wait_recv()
        back.wait_send()

    vm = pl.BlockSpec(memory_space=pltpu.VMEM)
    return pl.pallas_call(
        body, name="allreduce_small", in_specs=[vm] + [ANY] * len(after), out_specs=vm,
        out_shape=jax.ShapeDtypeStruct((r, LANE), F32),
        scratch_shapes=[pltpu.VMEM((r, LANE), F32), pltpu.VMEM((NCHIP, hr, LANE), F32),
                        pltpu.SemaphoreType.DMA((NCHIP + 1,)), pltpu.SemaphoreType.DMA((NCHIP + 1,))],
        compiler_params=pltpu.CompilerParams(has_side_effects=True, vmem_limit_bytes=VMEM_LIMIT),
    )(buf, *after)


MAX_ROW_TILE = 512
BF16_ROWS = 16


def _row_tile(rows):
    for t in range(min(rows, MAX_ROW_TILE) // BF16_ROWS * BF16_ROWS, 0, -BF16_ROWS):
        if rows % t == 0:
            return t
    raise ValueError(rows)


def _sum_halves(g, theirs, c_arr):
    nch, rows, cols = g.shape
    hr = rows // 2
    tr = _row_tile(hr)

    def body(c_ref, g_ref, t_ref, o_ref, ob_ref):
        s = g_ref[...] + t_ref[...]
        o_ref[...] = s
        ob_ref[...] = s.astype(BF16)

    blk = pl.BlockSpec((None, tr, cols), lambda j, i, c_ref: (j, i, 0))
    return pl.pallas_call(
        body, name="sum_halves",
        grid_spec=pltpu.PrefetchScalarGridSpec(
            num_scalar_prefetch=1, grid=(nch, hr // tr),
            in_specs=[pl.BlockSpec((None, None, tr, cols), lambda j, i, c_ref: (j, c_ref[0], i, 0)), blk],
            out_specs=[blk, blk]),
        out_shape=[jax.ShapeDtypeStruct((nch, hr, cols), F32), jax.ShapeDtypeStruct((nch, hr, cols), BF16)],
        compiler_params=_cparams(("parallel", "parallel")),
    )(c_arr, g.reshape(nch, 2, hr, cols), theirs)


def _sum_halves_w_in(g, theirs, c_arr):
    hr = D // 2
    sh = IN_DIM // NCHIP

    def body(c_ref, g_ref, t_ref, o_ref, ob_ref):
        s = g_ref[...] + t_ref[...]
        for j, dst, src, width in W_IN_PIECES:
            o_ref[j, :, dst:dst + width] = s[:, src:src + width]
            ob_ref[j, :, dst:dst + width] = s[:, src:src + width].astype(BF16)

    out = pl.BlockSpec((NCHIP, WT, sh), lambda i, c_ref: (0, i, 0))
    return pl.pallas_call(
        body, name="sum_halves_w_in",
        grid_spec=pltpu.PrefetchScalarGridSpec(
            num_scalar_prefetch=1, grid=(hr // WT,),
            in_specs=[pl.BlockSpec((None, WT, NP), lambda i, c_ref: (c_ref[0], i, 0)),
                      pl.BlockSpec((None, WT, NP), lambda i, c_ref: (0, i, 0))],
            out_specs=[out, out]),
        out_shape=[jax.ShapeDtypeStruct((NCHIP, hr, sh), F32), jax.ShapeDtypeStruct((NCHIP, hr, sh), BF16)],
        compiler_params=_cparams(("parallel",)),
    )(c_arr, g.reshape(2, hr, NP), theirs)


def _sum_chips(p, q, place, l, into=None, after=()):
    extra = ([into] if into is not None else []) + list(after)
    _, rows, cols = p.shape
    tr = _row_tile(rows)
    steps = rows // tr

    def body(place_ref, p_ref, q0, q1, q2, *rest):
        rest[-1][...] = ((p_ref[...] + q0[...].astype(F32)) + q1[...].astype(F32)) + q2[...].astype(F32)

    qs = lambda k: pl.BlockSpec((None, tr, cols), lambda i, place_ref: (k, i, 0))
    return pl.pallas_call(
        body, name="sum_chips",
        grid_spec=pltpu.PrefetchScalarGridSpec(
            num_scalar_prefetch=1, grid=(steps,),
            in_specs=[pl.BlockSpec((None, tr, cols), lambda i, place_ref: (place_ref[0], i, 0)), qs(0), qs(1), qs(2)]
            + [ANY] * len(extra),
            out_specs=pl.BlockSpec((None, tr, cols), lambda i, place_ref: (l, place_ref[1] * steps + i, 0))),
        out_shape=jax.ShapeDtypeStruct((DEPTH, 2 * rows, cols), F32),
        input_output_aliases={5: 0} if into is not None else {},
        compiler_params=_cparams(("parallel",)),
    )(place, p, q, q, q, *extra)


def _adamw(w, g, m, v):
    layers, rows, cols = w.shape
    tr = _row_tile(rows)

    def body(w_ref, g_ref, m_ref, v_ref, d_ref, nm_ref, nv_ref):
        gv = g_ref[...]
        nm = ADAM_B1 * m_ref[...] + (1.0 - ADAM_B1) * gv
        nv = ADAM_B2 * v_ref[...] + (1.0 - ADAM_B2) * jnp.square(gv)
        m_hat = nm / (1.0 - ADAM_B1 ** ADAM_STEP)
        v_hat = nv / (1.0 - ADAM_B2 ** ADAM_STEP)
        d_ref[...] = -ADAM_LR * (m_hat / (jnp.sqrt(v_hat) + ADAM_EPS) + ADAM_WD * w_ref[...])
        nm_ref[...] = nm
        nv_ref[...] = nv

    blk = pl.BlockSpec((None, tr, cols), lambda l, i: (l, i, 0))
    return pl.pallas_call(
        body, name="adamw", grid=(layers, rows // tr), in_specs=[blk] * 4, out_specs=[blk] * 3,
        out_shape=[jax.ShapeDtypeStruct(w.shape, F32)] * 3, compiler_params=_cparams(("parallel", "parallel")),
    )(w, g, m, v)


BIG = ("w_in", "w_out", "w_gate", "w_up", "w_down")
SMALL = ("norm_mix", "a_log", "dt_bias", "o_norm_g", "ln_v_g", "ln_v_b", "w_s", "b_s", "norm_ffn", "norm_final")
ORDER = ("norm_mix", "w_in", "conv_w", "a_log", "dt_bias", "o_norm_g", "ln_v_g", "ln_v_b", "w_s", "b_s", "w_out",
         "norm_ffn", "w_gate", "w_up", "w_down", "norm_final")


F32_ROWS = 8
PACK_ROWS = 128


def _lane_rows(size):
    return -(-size // (F32_ROWS * LANE)) * F32_ROWS


def _pack(arrs):
    parts = [jnp.pad(a.reshape(-1), (0, _lane_rows(a.size) * LANE - a.size)).reshape(-1, LANE) for a in arrs]
    rows = sum(p.shape[0] for p in parts)
    if rows % PACK_ROWS:
        parts.append(jnp.zeros((-rows % PACK_ROWS, LANE), F32))
    return jnp.concatenate(parts, axis=0)


def _unpack(buf, like):
    out, row = [], 0
    for a in like:
        n = _lane_rows(a.size)
        out.append(buf[row:row + n].reshape(-1)[:a.size].reshape(a.shape))
        row += n
    return out


def kernel(x, norm_mix, w_in, conv_w, a_log, dt_bias, o_norm_g, ln_v_g, ln_v_b, w_s, b_s, w_out, norm_ffn, w_gate, w_up, w_down, norm_final, loss_target, m_norm_mix, m_w_in, m_conv_w, m_a_log, m_dt_bias, m_o_norm_g, m_ln_v_g, m_ln_v_b, m_w_s, m_b_s, m_w_out, m_norm_ffn, m_w_gate, m_w_up, m_w_down, m_norm_final, v_norm_mix, v_w_in, v_conv_w, v_a_log, v_dt_bias, v_o_norm_g, v_ln_v_g, v_ln_v_b, v_w_s, v_b_s, v_w_out, v_norm_ffn, v_w_gate, v_w_up, v_w_down, v_norm_final):
    w = dict(norm_mix=norm_mix, w_in=w_in, conv_w=conv_w, a_log=a_log, dt_bias=dt_bias, o_norm_g=o_norm_g,
             ln_v_g=ln_v_g, ln_v_b=ln_v_b, w_s=w_s, b_s=b_s, w_out=w_out, norm_ffn=norm_ffn, w_gate=w_gate, w_up=w_up,
             w_down=w_down, norm_final=norm_final)
    m = dict(norm_mix=m_norm_mix, w_in=m_w_in, conv_w=m_conv_w, a_log=m_a_log, dt_bias=m_dt_bias, o_norm_g=m_o_norm_g,
             ln_v_g=m_ln_v_g, ln_v_b=m_ln_v_b, w_s=m_w_s, b_s=m_b_s, w_out=m_w_out, norm_ffn=m_norm_ffn,
             w_gate=m_w_gate, w_up=m_w_up, w_down=m_w_down, norm_final=m_norm_final)
    v = dict(norm_mix=v_norm_mix, w_in=v_w_in, conv_w=v_conv_w, a_log=v_a_log, dt_bias=v_dt_bias, o_norm_g=v_o_norm_g,
             ln_v_g=v_ln_v_g, ln_v_b=v_ln_v_b, w_s=v_w_s, b_s=v_b_s, w_out=v_w_out, norm_ffn=v_norm_ffn,
             w_gate=v_w_gate, w_up=v_w_up, w_down=v_w_down, norm_final=v_norm_final)
    chip = 2 * lax.axis_index("x") + lax.axis_index("y")
    place = jnp.stack([chip, lax.axis_index("c")]).astype(jnp.int32)
    c_arr = place[1:]

    def kernel_view(n, a):
        return jnp.swapaxes(a, 1, 2) if n in ("w_gate", "w_up") else a

    own = {n: [kernel_view(n, w[n])[l].astype(BF16) for l in range(DEPTH)] for n in BIG}
    by_chip = lambda a: jax.ShapeDtypeStruct((NCHIP,) + a.shape, a.dtype)

    def start(name, srcs, whole, after=()):
        return _split_start(name, _gather_plan(whole), srcs, [by_chip(a) for a in srcs], 3 * len(srcs), after)

    def finish(name, started, whole, after):
        srcs, lands = _split_wait(name, _gather_plan(whole), started, after)
        passed = iter(_forward_halves([g for g, all_of_it in zip(lands, whole) if not all_of_it]))
        lands = [g if all_of_it else next(passed) for g, all_of_it in zip(lands, whole)]
        return srcs, [lax.dynamic_update_index_in_dim(g, o, chip, 0) for g, o in zip(lands, srcs)]

    ffn = BIG[1:]
    first = start("gather_first_start", [own["w_in"][0], conv_w], [False, True])
    early = start("gather_early_start", [own[n][0] for n in ffn], [False] * len(ffn), [first["token"]])
    mid = start("gather_mid_start", [own["w_in"][1]], [False], [early["token"]])
    later = start("gather_later_start", [own[n][1] for n in ffn], [False] * len(ffn), [mid["token"]])
    hn = _rmsnorm("rms_mix", x[0], norm_mix[0][None])
    (own_w_in, _), (w_in_by_chip, conv_by_chip) = finish("gather_first_wait", first, [False, True], [later["token"], hn])

    passing = {}

    def pass_on(tag, started, n):
        def at(after):
            srcs, lands = _split_wait(f"gather_{tag}_wait", _gather_plan([False] * n), started, after)
            passing[tag] = srcs, _forward_start(f"forward_{tag}_start", lands, ())
            return [passing[tag][1]["token"]]
        return at

    def passed_on(tag, after):
        srcs, fwd = passing[tag]
        lands = _forward_wait(f"forward_{tag}_wait", fwd, [after])
        return srcs, [lax.dynamic_update_index_in_dim(g, o, chip, 0) for g, o in zip(lands, srcs)]

    def late(tag):
        return lambda after: dict(zip(ffn, passed_on(tag, after)[1]))

    layer0 = _layer_params(0, dict(
        hn=hn, w_in=_assemble_w_in(w_in_by_chip, own_w_in, place), conv_w=conv_by_chip, late=late("early"),
        before_scan=pass_on("early", early, len(ffn)), before_ffn_out=pass_on("mid", mid, 1)), w)

    def layer1(after):
        (own_w_in1,), (w_in1_by_chip,) = passed_on("mid", after)
        return _layer_params(1, dict(w_in=_assemble_w_in(w_in1_by_chip, own_w_in1, place), conv_w=conv_by_chip,
                                     late=late("later"), before_scan=pass_on("later", later, len(ffn))), w)

    saved, layers, loss_lanes, dh, dhb, d_norm_final = _forward(x[0], loss_target[0], [layer0, layer1],
                                                                 norm_final[None])

    sums, arrived = {}, {}

    def exchange_start(tag, l, names, grads, after=()):
        mine = [grads[n] for n in names]
        shapes = [jax.ShapeDtypeStruct((g.shape[0], g.shape[1] // 2, g.shape[2]), F32) for g in mine]
        return tag, l, names, _split_start(f"exchange_{tag}_start", _exchange_plan, mine, shapes, len(mine), after)

    def add_halves(l, names, mine, theirs):
        for n, g, t in zip(names, mine, theirs):
            sums[l, n] = (_sum_halves_w_in if n == "w_in" else _sum_halves)(g, t, c_arr)

    def exchange_wait(handle, after):
        tag, l, names, started = handle
        add_halves(l, names, *_split_wait(f"exchange_{tag}_wait", _exchange_plan, started, after))

    def scatter_start(tag, l, names, after=()):
        partial = [sums[l, n][1] for n in names]
        shapes = [jax.ShapeDtypeStruct((3,) + p.shape[1:], p.dtype) for p in partial]
        return tag, l, names, _split_start(f"scatter_{tag}_start", _scatter_plan, partial, shapes, 3 * len(names), after)

    def scatter_wait(handle, after):
        tag, l, names, started = handle
        for n, q in zip(names, _split_wait(f"scatter_{tag}_wait", _scatter_plan, started, after)[1]):
            arrived[l, n] = q

    last = DEPTH - 1
    swiglu = BIG[2:]
    dh1, dh1b, g_ffn = _layer_bwd_ffn(dh, dhb, layers[last], saved[last])
    dh, dhb, g_mix = _layer_bwd_mixer(dh1, dh1b, layers[last], saved[last])
    gl = [None, _reference_layout({**g_ffn, **g_mix})]
    ex_last = exchange_start("last", last, BIG, gl[last])
    dh1, dh1b, g_ffn = _layer_bwd_ffn(dh, dhb, layers[0], saved[0], after=[ex_last[-1]["token"]])
    exchange_wait(ex_last, dh1)
    sc_last = scatter_start("last", last, BIG)
    ex_ffn = exchange_start("swiglu", 0, swiglu, g_ffn, [sc_last[-1]["token"]])
    sc_ffn = []

    def midway(do):
        exchange_wait(ex_ffn, do)
        sc_ffn.append(scatter_start("swiglu", 0, swiglu))
        return [sc_ffn[0][-1]["token"]]

    ex_rest = []

    def late(grads):
        rest_grads = dict(w_in=grads["w_in"], w_out=grads["w_out"].reshape(NCHIP, D // NCHIP, D))
        ex_rest.append(exchange_start("rest", 0, BIG[:2], rest_grads))
        return [ex_rest[0][-1]["token"]]

    dx, _, g_mix = _layer_bwd_mixer(dh1, dh1b, layers[0], saved[0], after=[ex_ffn[-1]["token"]], midway=midway,
                                    late=late)
    scatter_wait(sc_last, dx)
    scatter_wait(sc_ffn[0], dx)
    gl[0] = _reference_layout({**g_ffn, **g_mix})

    small_g = [jnp.stack([gl[l][n] for l in range(DEPTH)]) for n in SMALL[:-1]] + [d_norm_final[0]]
    conv_g = jnp.stack([gl[l]["conv_w"] for l in range(DEPTH)])
    summed = small_g + [conv_g, loss_lanes[0, :1]]
    total = _allreduce_small(_pack(summed))
    exchange_wait(ex_rest[0], total)
    sc_rest = scatter_start("rest", 0, BIG[:2])

    travelling = [sc_rest[-1]["token"]]
    reduced, g_out, delta, new_m, new_v = {}, {}, {}, {}, {}

    def adamw_large(names, joined):
        for n, g in zip(names, joined):
            res = _adamw(kernel_view(n, w[n]), g, kernel_view(n, m[n]), kernel_view(n, v[n]))
            g_out[n], delta[n], new_m[n], new_v[n] = (kernel_view(n, a) for a in (g,) + tuple(res))

    for n in BIG:
        for l in (range(DEPTH) if n in swiglu else [last]):
            reduced[n] = _sum_chips(sums[l, n][0], arrived[l, n], place, l, into=reduced.get(n), after=travelling)
    adamw_large(swiglu, _join_halves("join_swiglu", [reduced[n] for n in swiglu]))
    scatter_wait(sc_rest, [new_v[n] for n in swiglu] + [reduced[n] for n in BIG[:2]])
    for n in BIG[:2]:
        reduced[n] = _sum_chips(sums[0, n][0], arrived[0, n], place, 0, into=reduced[n])
    adamw_large(BIG[:2], _join_halves("join_rest", [reduced[n] for n in BIG[:2]]))
    *small_r, conv_r, loss = _unpack(total, summed)
    g_out.update(zip(SMALL, small_r))
    g_out["conv_w"] = lax.dynamic_slice_in_dim(conv_r, chip * conv_w.shape[2], conv_w.shape[2], axis=2)

    rest = SMALL + ("conv_w",)
    like = [w[n] for n in rest]
    d, nm, nv = _adamw(*[_pack([src[n] for n in rest])[None] for src in (w, g_out, m, v)])
    for dst, buf in ((delta, d), (new_m, nm), (new_v, nv)):
        dst.update(zip(rest, _unpack(buf[0], like)))

    return (loss[0], dx[None], *[g_out[n] for n in ORDER], *[delta[n] for n in ORDER], *[new_m[n] for n in ORDER],
            *[new_v[n] for n in ORDER])
```

```python
import functools

import jax
import jax.numpy as jnp
from jax import lax
from jax.experimental import pallas as pl
from jax.experimental.pallas import tpu as pltpu

F32 = jnp.float32
BF16 = jnp.bfloat16
MESH = pl.DeviceIdType.MESH
ANY = pl.BlockSpec(memory_space=pl.ANY)
HIGHEST = lax.Precision.HIGHEST

T = 2048
D = 1024
DEPTH = 2
NCHIP = 4
HEADS = 4
HD = 128
HW = HEADS * HD
CH = 64
GCH = 128
IN_DIM = 3080
NP = 3200
BA_OFF = 3072
FF_SH = 704
EPS = 1e-6
LANE = 128
VMEM_LIMIT = 56 * 1024 * 1024

ADAM_LR = 0.001
ADAM_B1 = 0.9
ADAM_B2 = 0.999
ADAM_EPS = 1e-08
ADAM_WD = 0.01
ADAM_STEP = 10


def _cparams(sem=None):
    return pltpu.CompilerParams(dimension_semantics=sem, vmem_limit_bytes=VMEM_LIMIT)


_DIMS = {"nn": (((1,), (0,)), ((), ())), "nt": (((1,), (1,)), ((), ())), "tn": (((0,), (0,)), ((), ()))}


def _mm(name, mode, a, bs, *, tm, tn, tk, out_dtypes=(F32,), reduce_g=False, resid=None, extras=(), epilogue=None,
        b_spec=None, n_n=None, after=(), fold_g=False, sub_m=1):
    assert sub_m == 1 or (mode != "tn" and tm % (8 * sub_m) == 0), (name, sub_m)
    nb = len(bs)
    ga = a.shape[0]
    gbs = [1 if b_spec is not None else b.shape[0] for b in bs]
    g_n = max([ga] + gbs)
    if mode == "tn":
        k_n, m_n = a.shape[1:]
    else:
        m_n, k_n = a.shape[1:]
    if n_n is None:
        n_n = bs[0].shape[1] if mode == "nt" else bs[0].shape[2]
    assert m_n % tm == 0 and n_n % tn == 0 and k_n % tk == 0, (name, m_n, n_n, k_n)
    mi, nj, kk = m_n // tm, n_n // tn, k_n // tk
    lead = g_n if fold_g else None
    if reduce_g:
        g_steps = 1 if fold_g else g_n
        grid = (mi, nj, g_steps, kk)
        ids = lambda i, j, g, k: (g, i, j, k)
        n_red = g_steps * kk
        red_idx = lambda: pl.program_id(2) * kk + pl.program_id(3)
        sem = ("parallel", "parallel", "arbitrary", "arbitrary")
    else:
        grid = (g_n, mi, nj, kk)
        ids = lambda g, i, j, k: (g, i, j, k)
        n_red = kk
        red_idx = lambda: pl.program_id(3)
        sem = ("parallel", "parallel", "parallel", "arbitrary")

    def pick(gsz, g):
        return g if gsz > 1 else 0

    def a_map(*p):
        g, i, j, k = ids(*p)
        return (pick(ga, g), k, i) if mode == "tn" else (pick(ga, g), i, k)

    def b_map(gsz):
        def f(*p):
            g, i, j, k = ids(*p)
            if b_spec is not None:
                return b_spec[1](g, i, j, k)
            return (pick(gsz, g), j, k) if mode == "nt" else (pick(gsz, g), k, j)
        return f

    def o_map(gsz):
        def f(*p):
            g, i, j, k = ids(*p)
            return (0 if reduce_g else pick(gsz, g), i, j)
        return f

    a_spec = pl.BlockSpec((lead, tk, tm) if mode == "tn" else (lead, tm, tk), a_map)
    b_block = b_spec[0] if b_spec is not None else ((lead, tn, tk) if mode == "nt" else (lead, tk, tn))
    b_specs = [pl.BlockSpec(b_block, b_map(gs)) for gs in gbs]
    x_specs = [pl.BlockSpec((None, tm, tn), o_map(e.shape[0])) for e in extras]
    r_specs = [pl.BlockSpec((None, tm, tn), o_map(resid.shape[0]))] if resid is not None else []
    g_out = 1 if reduce_g else g_n
    out_shape = [jax.ShapeDtypeStruct((g_out, m_n, n_n), dt) for dt in out_dtypes]
    out_specs = [pl.BlockSpec((None, tm, tn), o_map(g_out)) for _ in out_dtypes]
    nx, nr, no = len(extras), len(r_specs), len(out_dtypes)
    n_in = 1 + nb + nx + nr + len(after)
    dims = _DIMS[mode]

    def body(*refs):
        a_ref = refs[0]
        b_refs = refs[1:1 + nb]
        x_refs = refs[1 + nb:1 + nb + nx]
        r_refs = refs[1 + nb + nx:1 + nb + nx + nr]
        o_refs = refs[n_in:n_in + no]
        acc_refs = refs[n_in + no:]
        def dots(rows):
            if fold_g:
                return [sum(lax.dot_general(a_ref[g, rows, :], b_ref[g], dims, preferred_element_type=F32)
                            for g in range(g_n)) for b_ref in b_refs]
            av = a_ref[...] if mode == "tn" else a_ref[rows, :]
            return [lax.dot_general(av, b_ref[...], dims, preferred_element_type=F32) for b_ref in b_refs]

        def finish(accs, rows=slice(None)):
            if r_refs:
                accs[0] = accs[0] + r_refs[0][rows, :]
            outs = epilogue(accs, [x[rows, :] for x in x_refs]) if epilogue is not None else accs
            for o_ref, o in zip(o_refs, outs):
                o_ref[rows, :] = o.astype(o_ref.dtype)

        if n_red == 1:
            slabs = [slice(s * (tm // sub_m), (s + 1) * (tm // sub_m)) for s in range(sub_m)]
            ahead = dots(slabs[0])
            for s, rows in enumerate(slabs):
                now, ahead = ahead, (dots(slabs[s + 1]) if s + 1 < sub_m else None)
                finish(now, rows)
            return
        products = dots(slice(None))
        r = red_idx()
        for p, acc in zip(products, acc_refs):
            @pl.when(r == 0)
            def _():
                acc[...] = p

            @pl.when((r > 0) & (r < n_red - 1))
            def _():
                acc[...] += p

        @pl.when(r == n_red - 1)
        def _():
            finish([acc[...] + p for p, acc in zip(products, acc_refs)])

    return pl.pallas_call(
        body, name=name, grid=grid,
        in_specs=[a_spec] + b_specs + x_specs + r_specs + [ANY] * len(after),
        out_specs=out_specs, out_shape=out_shape,
        scratch_shapes=[pltpu.VMEM((tm, tn), F32) for _ in range(nb if n_red > 1 else 0)],
        compiler_params=_cparams(sem),
    )(a, *bs, *extras, *([resid] if resid is not None else []), *after)


def _sigmoid(x):
    return 1.0 / (1.0 + jnp.exp(-x))


def _silu(x):
    return x * _sigmoid(x)


def _gelu(x):
    return 0.5 * x * (1.0 + jnp.tanh(0.7978845608028654 * (x + 0.044715 * (x * x * x))))


def _rms_fn(h, gain):
    return h * lax.rsqrt(jnp.mean(h * h, axis=-1, keepdims=True) + EPS) * gain


def _shift_impl(x, s):
    n = x.shape[0]
    rolled = pltpu.roll(x, s % n, 0)
    row = lax.broadcasted_iota(jnp.int32, x.shape, 0)
    return jnp.where((row >= s) & (row < n + s), rolled, 0.0)


@functools.partial(jax.custom_vjp, nondiff_argnums=(1,))
def _shift(x, s):
    return _shift_impl(x, s)


def _shift_fwd(x, s):
    return _shift_impl(x, s), None


def _shift_bwd(s, _, g):
    return (_shift_impl(g, -s),)


_shift.defvjp(_shift_fwd, _shift_bwd)


def _prep_fn(x, w, qk_scale, is_v):
    y = x * w[3:4, :]
    for i in range(3):
        y = y + _shift(x, 3 - i) * w[i:i + 1, :]
    y = _silu(y)
    nrm = lax.rsqrt(jnp.sum(y * y, axis=-1, keepdims=True) + EPS) * qk_scale
    return y * jnp.where(is_v, 1.0, nrm)


def _softplus(x):
    return jnp.maximum(x, 0.0) + jnp.log(1.0 + jnp.exp(-jnp.abs(x)))


def _gates_fn(ba, a_log, dt_bias):
    lane = lax.broadcasted_iota(jnp.int32, ba.shape, 1)
    beta = _sigmoid(ba)
    g = -jnp.exp(a_log) * _softplus(ba + dt_bias)
    return jnp.where(lane < HEADS, beta, g)


def _dot16(a, b, dims=_DIMS["nn"]):
    return lax.dot_general(a.astype(BF16), b.astype(BF16), dims, preferred_element_type=F32)


def _dot32(a, b):
    return jnp.dot(a, b, preferred_element_type=F32, precision=HIGHEST)


def _dot3(a, b, dims=_DIMS["nn"]):
    return lax.dot_general(a, b, dims, preferred_element_type=F32, precision=lax.Precision.HIGH)


def _tri_inverses(mats):
    row = lax.broadcasted_iota(jnp.int32, (CH, CH), 0)
    col = lax.broadcasted_iota(jnp.int32, (CH, CH), 1)
    eye = (row == col).astype(F32)
    ts = [eye - a for a in mats]
    ps = list(mats)
    for _ in range(5):
        ps = [_dot3(p, p) for p in ps]
        ts = [t + _dot3(t, p) for t, p in zip(ts, ps)]
    return ts


@jax.custom_vjp
def _tri_solves(mats, rhs):
    return [_dot3(t, b) for t, b in zip(_tri_inverses(mats), rhs)]


def _tri_solves_fwd(mats, rhs):
    ts = _tri_inverses(mats)
    xs = [_dot3(t, b) for t, b in zip(ts, rhs)]
    return xs, (ts, xs)


def _tri_solves_bwd(res, dxs):
    ts, xs = res
    dbs = [_dot3(t, dx, _DIMS["tn"]) for t, dx in zip(ts, dxs)]
    return [-_dot3(db, x, _DIMS["nt"]) for db, x in zip(dbs, xs)], dbs


_tri_solves.defvjp(_tri_solves_fwd, _tri_solves_bwd)


def _chunk_prep_fn(xs, bgs):
    row = lax.broadcasted_iota(jnp.int32, (CH, CH), 0)
    col = lax.broadcasted_iota(jnp.int32, (CH, CH), 1)
    incl = row >= col
    strict = row > col
    lmat = incl.astype(F32)
    n = len(xs)
    items = [(i, h) for i in range(n) for h in range(HEADS)]
    part = lambda i, h, c: xs[i][:, c * HW + h * HD:c * HW + (h + 1) * HD]
    q = [part(i, h, 0) for i, h in items]
    k = [part(i, h, 1) for i, h in items]
    v = [part(i, h, 2) for i, h in items]
    beta = [bgs[i][:, h:h + 1] for i, h in items]
    gc_all = [_dot32(lmat, bg) for bg in bgs]
    gc = [gc_all[i][:, HEADS + h:HEADS + h + 1] for i, h in items]
    gmat = [jnp.where(strict, jnp.broadcast_to(bgs[i][:, HEADS + h:HEADS + h + 1], (CH, CH)), 0.0) for i, h in items]
    diff = [_dot3(lmat, m) for m in gmat]
    decay = [jnp.where(incl, jnp.exp(jnp.where(incl, d, 0.0)), 0.0) for d in diff]
    k_beta = [kk * b for kk, b in zip(k, beta)]
    kk_t = [_dot16(kb, kk, _DIMS["nt"]) for kb, kk in zip(k_beta, k)]
    qk_t = [_dot16(qq, kk, _DIMS["nt"]) for qq, kk in zip(q, k)]
    a = [jnp.where(strict, m * d, 0.0) for m, d in zip(kk_t, decay)]
    eg = [jnp.exp(g) for g in gc]
    rhs = [jnp.concatenate([vv * b, kb * e], axis=-1) for vv, b, kb, e in zip(v, beta, k_beta, eg)]
    uw = _tri_solves(a, rhs)
    qk = [m * d for m, d in zip(qk_t, decay)]
    g_last = [g[CH - 1:CH, :] for g in gc]
    qe = [qq * e for qq, e in zip(q, eg)]
    kd = [kk * jnp.exp(gl - g) for kk, gl, g in zip(k, g_last, gc)]
    egl = [jnp.broadcast_to(jnp.exp(gl), (1, HD)) for gl in g_last]
    out = []
    for i in range(n):
        mine = slice(i * HEADS, (i + 1) * HEADS)
        cat = lambda vals: jnp.concatenate(vals[mine], axis=-1)
        out.append((cat([x[:, :HD] for x in uw]), cat([x[:, HD:] for x in uw]), cat(qe), cat(kd),
                    jnp.concatenate([m[None] for m in qk[mine]], axis=0), cat(egl)))
    return out


def _chunk_state_fn(u, w, qe, kd, qk, egl, s):
    ws = [_dot16(a, b) for a, b in zip(w, s)]
    qs = [_dot16(a, b) for a, b in zip(qe, s)]
    v_new = [a - b for a, b in zip(u, ws)]
    o = [a + _dot16(b, c) for a, b, c in zip(qs, qk, v_new)]
    s_new = [a * e + _dot16(b, c, _DIMS["tn"]) for a, e, b, c in zip(s, egl, kd, v_new)]
    return o, s_new


def _mix_fn(o, z, ur, vr, ong, lng, lnb, ws, bst):
    row = lax.broadcasted_iota(jnp.int32, (GCH, GCH), 0)
    col = lax.broadcasted_iota(jnp.int32, (GCH, GCH), 1)
    causal = row >= col
    ug = _gelu(ur)
    vg = _gelu(vr)
    sls = [slice(h * HD, (h + 1) * HD) for h in range(HEADS)]
    oh = [o[:, sl] for sl in sls]
    oh = [x * lax.rsqrt(jnp.mean(x * x, axis=-1, keepdims=True) + EPS) for x in oh]
    outs_dn = [x * ong * _silu(z[:, sl]) for x, sl in zip(oh, sls)]
    vh = [vg[:, sl] for sl in sls]
    mu = [jnp.mean(x, axis=-1, keepdims=True) for x in vh]
    var = [jnp.mean(jnp.square(x - m), axis=-1, keepdims=True) for x, m in zip(vh, mu)]
    vn = [(x - m) * lax.rsqrt(s + EPS) * lng[:, sl] + lnb[:, sl] for x, m, s, sl in zip(vh, mu, var, sls)]
    mixed = [_dot16(jnp.where(causal, ws[h], 0.0), vn[h]) for h in range(HEADS)]
    outs_gm = [ug[:, sl] * (mixed[h] + bst[:, h:h + 1]) for h, sl in enumerate(sls)]
    return jnp.concatenate(outs_dn + outs_gm, axis=-1)


def _loss_fn(h, gain, tgt):
    y = _rms_fn(h, gain)
    return 0.5 * jnp.sum(jnp.mean(jnp.square(y - tgt), axis=-1))


RT = 512


def _rows(n=D):
    return pl.BlockSpec((RT, n), lambda i: (i, 0))


def _whole(shape):
    nd = len(shape)
    return pl.BlockSpec(shape, lambda i: (0,) * nd)


def _rmsnorm(name, h, gain):
    def body(h_ref, g_ref, o_ref):
        o_ref[...] = _rms_fn(h_ref[...], g_ref[...]).astype(BF16)

    return pl.pallas_call(
        body, name=name, grid=(T // RT,), in_specs=[_rows(), _whole((1, D))], out_specs=_rows(),
        out_shape=jax.ShapeDtypeStruct((T, D), BF16), compiler_params=_cparams(("parallel",)),
    )(h, gain)


def _rmsnorm_bwd(name, dhn, h, gain, resid):
    def body(dhn_ref, h_ref, g_ref, r_ref, dh_ref, dh16_ref, dg_ref):
        _, vjp = jax.vjp(_rms_fn, h_ref[...], g_ref[...])
        dh, dg = vjp(dhn_ref[...])
        dh = r_ref[...] + dh
        dh_ref[...] = dh
        dh16_ref[...] = dh.astype(BF16)

        @pl.when(pl.program_id(0) == 0)
        def _():
            dg_ref[...] = dg

        @pl.when(pl.program_id(0) > 0)
        def _():
            dg_ref[...] += dg

    return pl.pallas_call(
        body, name=name, grid=(T // RT,), in_specs=[_rows(), _rows(), _whole((1, D)), _rows()],
        out_specs=[_rows(), _rows(), _whole((1, D))],
        out_shape=[jax.ShapeDtypeStruct((T, D), F32), jax.ShapeDtypeStruct((T, D), BF16),
                   jax.ShapeDtypeStruct((1, D), F32)],
        compiler_params=_cparams(("arbitrary",)),
    )(dhn, h, gain, resid)


def _loss_head(h, gain, tgt):
    def body(h_ref, g_ref, t_ref, l_ref, dh_ref, dh16_ref, dg_ref):
        loss, vjp = jax.vjp(lambda hh, gg: _loss_fn(hh, gg, t_ref[...]), h_ref[...], g_ref[...])
        dh, dg = vjp(jnp.ones((), F32))
        dh_ref[...] = dh
        dh16_ref[...] = dh.astype(BF16)
        lv = jnp.full((1, LANE), loss, F32)

        @pl.when(pl.program_id(0) == 0)
        def _():
            dg_ref[...] = dg
            l_ref[...] = lv

        @pl.when(pl.program_id(0) > 0)
        def _():
            dg_ref[...] += dg
            l_ref[...] += lv

    return pl.pallas_call(
        body, name="loss_head", grid=(T // RT,), in_specs=[_rows(), _whole((1, D)), _rows()],
        out_specs=[_whole((1, LANE)), _rows(), _rows(), _whole((1, D))],
        out_shape=[jax.ShapeDtypeStruct((1, LANE), F32), jax.ShapeDtypeStruct((T, D), F32),
                   jax.ShapeDtypeStruct((T, D), BF16), jax.ShapeDtypeStruct((1, D), F32)],
        compiler_params=_cparams(("arbitrary",)),
    )(h, gain, tgt)


def _prep_flags():
    j = pl.program_id(0)
    qk_scale = jnp.where(j < HEADS, HD ** -0.5, 1.0).astype(F32)
    return qk_scale, j >= 2 * HEADS


def _prep(proj, conv_w):
    def body(x_ref, w_ref, o_ref):
        qk_scale, is_v = _prep_flags()
        o_ref[...] = _prep_fn(x_ref[...], w_ref[...], qk_scale, is_v)

    col = lambda j: (0, j)
    return pl.pallas_call(
        body, name="gdn_prep", grid=(3 * HEADS,),
        in_specs=[pl.BlockSpec((T, HD), col), pl.BlockSpec((4, HD), col)], out_specs=pl.BlockSpec((T, HD), col),
        out_shape=jax.ShapeDtypeStruct((T, 3 * HW), F32), compiler_params=_cparams(("parallel",)),
    )(proj, conv_w)


def _prep_bwd(proj, conv_w, dqkv, dproj):
    def body(x_ref, w_ref, d_ref, _, dx_ref, dw_ref):
        qk_scale, is_v = _prep_flags()
        _, vjp = jax.vjp(lambda x, w: _prep_fn(x, w, qk_scale, is_v), x_ref[...], w_ref[...])
        dx, dw = vjp(d_ref[...])
        dx_ref[...] = dx.astype(BF16)
        dw_ref[...] = dw

    col = lambda j: (0, j)
    return pl.pallas_call(
        body, name="gdn_prep_bwd", grid=(3 * HEADS,),
        in_specs=[pl.BlockSpec((T, HD), col), pl.BlockSpec((4, HD), col), pl.BlockSpec((T, HD), col), ANY],
        out_specs=[pl.BlockSpec((T, HD), col), pl.BlockSpec((4, HD), col)],
        out_shape=[jax.ShapeDtypeStruct((T, NP), BF16), jax.ShapeDtypeStruct((4, 3 * HW), F32)],
        input_output_aliases={3: 0}, compiler_params=_cparams(("parallel",)),
    )(proj, conv_w, dqkv, dproj)


BA_BLK = BA_OFF // LANE


def _gates(proj, a_log, dt_bias):
    def body(x_ref, a_ref, d_ref, o_ref):
        o_ref[...] = _gates_fn(x_ref[...], a_ref[...], d_ref[...])

    return pl.pallas_call(
        body, name="gdn_gates", grid=(1,),
        in_specs=[pl.BlockSpec((T, LANE), lambda i: (0, BA_BLK)), _whole((1, LANE)), _whole((1, LANE))],
        out_specs=_whole((T, LANE)),
        out_shape=jax.ShapeDtypeStruct((T, LANE), F32), compiler_params=_cparams(("arbitrary",)),
    )(proj, a_log, dt_bias)


def _gates_bwd(proj, a_log, dt_bias, dbg, dproj):
    def body(x_ref, a_ref, d_ref, dbg_ref, _, dx_ref, da_ref, dd_ref):
        _, vjp = jax.vjp(_gates_fn, x_ref[...], a_ref[...], d_ref[...])
        dx, da_ref[...], dd_ref[...] = vjp(dbg_ref[...])
        dx_ref[...] = dx.astype(BF16)

    ba = pl.BlockSpec((T, LANE), lambda i: (0, BA_BLK))
    return pl.pallas_call(
        body, name="gdn_gates_bwd", grid=(1,),
        in_specs=[ba, _whole((1, LANE)), _whole((1, LANE)), _whole((T, LANE)), ANY],
        out_specs=[ba, _whole((1, LANE)), _whole((1, LANE))],
        out_shape=[jax.ShapeDtypeStruct((T, NP), BF16), jax.ShapeDtypeStruct((1, LANE), F32),
                   jax.ShapeDtypeStruct((1, LANE), F32)],
        input_output_aliases={4: 0}, compiler_params=_cparams(("arbitrary",)),
    )(proj, a_log, dt_bias, dbg, dproj)


NCK = T // CH
CPS = 4


def _chunk_prep_specs(rev=False):
    at = (lambda n: NCK - 1 - n) if rev else (lambda n: n)
    wide = pl.BlockSpec((CH, HW), lambda n: (at(n), 0))
    return [wide, wide, wide, wide, pl.BlockSpec((HEADS, CH, CH), lambda n: (0, at(n), 0)),
            pl.BlockSpec((None, 1, HW), lambda n: (at(n), 0, 0))]


def _chunk_prep_shapes(dtypes):
    shp = [(T, HW), (T, HW), (T, HW), (T, HW), (HEADS, T, CH), (NCK, 1, HW)]
    return [jax.ShapeDtypeStruct(s, dt) for s, dt in zip(shp, dtypes)]


def _chunk_prep(qkv, bg):
    def body(x_ref, bg_ref, *o_refs):
        rows = [slice(ci * CH, (ci + 1) * CH) for ci in range(CPS)]
        res = _chunk_prep_fn([x_ref[r, :] for r in rows], [bg_ref[r, :] for r in rows])
        for ci, (u, w, qe, kd, qk, egl) in enumerate(res):
            for o_ref, val in zip(o_refs[:4], (u, w, qe, kd)):
                o_ref[rows[ci], :] = val.astype(o_ref.dtype)
            o_refs[4][:, rows[ci], :] = qk.astype(BF16)
            o_refs[5][ci] = egl

    wide = pl.BlockSpec((CPS * CH, HW), lambda n: (n, 0))
    return pl.pallas_call(
        body, name="gdn_chunk_prep", grid=(NCK // CPS,),
        in_specs=[pl.BlockSpec((CPS * CH, 3 * HW), lambda n: (n, 0)), pl.BlockSpec((CPS * CH, LANE), lambda n: (n, 0))],
        out_specs=[wide, wide, wide, wide, pl.BlockSpec((HEADS, CPS * CH, CH), lambda n: (0, n, 0)),
                   pl.BlockSpec((CPS, 1, HW), lambda n: (n, 0, 0))],
        out_shape=_chunk_prep_shapes((F32, BF16, BF16, BF16, BF16, F32)),
        compiler_params=_cparams(("parallel",)),
    )(qkv, bg)


def _chunk_prep_bwd(qkv, bg, cots):
    def body(x_ref, bg_ref, du, dw, dqe, dkd, dqk, degl, dx_ref, dbg_ref):
        rows = [slice(ci * CH, (ci + 1) * CH) for ci in range(CPS)]
        _, vjp = jax.vjp(_chunk_prep_fn, [x_ref[r, :] for r in rows], [bg_ref[r, :] for r in rows])
        dxs, dbgs = vjp([(du[r, :], dw[r, :], dqe[r, :], dkd[r, :], dqk[:, r, :], degl[ci])
                         for ci, r in enumerate(rows)])
        for r, dx, dbg in zip(rows, dxs, dbgs):
            dx_ref[r, :] = dx
            dbg_ref[r, :] = dbg

    wide = pl.BlockSpec((CPS * CH, HW), lambda n: (n, 0))
    return pl.pallas_call(
        body, name="gdn_chunk_prep_bwd", grid=(NCK // CPS,),
        in_specs=[pl.BlockSpec((CPS * CH, 3 * HW), lambda n: (n, 0)), pl.BlockSpec((CPS * CH, LANE), lambda n: (n, 0)),
                  wide, wide, wide, wide, pl.BlockSpec((HEADS, CPS * CH, CH), lambda n: (0, n, 0)),
                  pl.BlockSpec((CPS, 1, HW), lambda n: (n, 0, 0))],
        out_specs=[pl.BlockSpec((CPS * CH, 3 * HW), lambda n: (n, 0)), pl.BlockSpec((CPS * CH, LANE), lambda n: (n, 0))],
        out_shape=[jax.ShapeDtypeStruct((T, 3 * HW), F32), jax.ShapeDtypeStruct((T, LANE), F32)],
        compiler_params=_cparams(("parallel",)),
    )(qkv, bg, *cots)


def _head_args(refs):
    u, w, qe, kd, qk, egl = refs
    sls = [slice(h * HD, (h + 1) * HD) for h in range(HEADS)]
    return ([u[:, sl] for sl in sls], [w[:, sl].astype(F32) for sl in sls], [qe[:, sl].astype(F32) for sl in sls],
            [kd[:, sl].astype(F32) for sl in sls], [qk[h].astype(F32) for h in range(HEADS)],
            [egl[:, sl] for sl in sls])


def _chunk_scan(prep, after=()):
    def body(*refs):
        o_ref, sh_ref, s_ref = refs[6 + len(after):]

        @pl.when(pl.program_id(0) == 0)
        def _():
            s_ref[...] = jnp.zeros_like(s_ref)

        s = [s_ref[h] for h in range(HEADS)]
        for h in range(HEADS):
            sh_ref[h, 0] = s[h]
        o, s_new = _chunk_state_fn(*_head_args(refs[:6]), s)
        for h in range(HEADS):
            o_ref[:, h * HD:(h + 1) * HD] = o[h]
            s_ref[h] = s_new[h]

    return pl.pallas_call(
        body, name="gdn_scan", grid=(NCK,), in_specs=_chunk_prep_specs() + [ANY] * len(after),
        out_specs=[pl.BlockSpec((CH, HW), lambda n: (n, 0)), pl.BlockSpec((HEADS, 1, HD, HD), lambda n: (0, n, 0, 0))],
        out_shape=[jax.ShapeDtypeStruct((T, HW), F32), jax.ShapeDtypeStruct((HEADS, NCK, HD, HD), F32)],
        scratch_shapes=[pltpu.VMEM((HEADS, HD, HD), F32)], compiler_params=_cparams(("arbitrary",)),
    )(*prep, *after)


def _chunk_scan_bwd(prep, s_hist, do, after=()):
    n_in = 8 + len(after)

    def body(*refs):
        sh_ref, do_ref = refs[6:8]
        d_refs = refs[n_in:n_in + 6]
        ds_ref = refs[n_in + 6]

        @pl.when(pl.program_id(0) == 0)
        def _():
            ds_ref[...] = jnp.zeros_like(ds_ref)

        sls = [slice(h * HD, (h + 1) * HD) for h in range(HEADS)]
        _, vjp = jax.vjp(_chunk_state_fn, *_head_args(refs[:6]), [sh_ref[h, 0] for h in range(HEADS)])
        du, dw, dqe, dkd, dqk, degl, ds = vjp(([do_ref[:, sl] for sl in sls], [ds_ref[h] for h in range(HEADS)]))
        for h, sl in enumerate(sls):
            for d_ref, val in zip(d_refs[:4], (du, dw, dqe, dkd)):
                d_ref[:, sl] = val[h]
            d_refs[4][h] = dqk[h]
            d_refs[5][:, sl] = degl[h]
            ds_ref[h] = ds[h]

    rev = lambda n: NCK - 1 - n
    return pl.pallas_call(
        body, name="gdn_scan_bwd", grid=(NCK,),
        in_specs=_chunk_prep_specs(rev=True) + [pl.BlockSpec((HEADS, 1, HD, HD), lambda n: (0, rev(n), 0, 0)),
                                                pl.BlockSpec((CH, HW), lambda n: (rev(n), 0))] + [ANY] * len(after),
        out_specs=_chunk_prep_specs(rev=True), out_shape=_chunk_prep_shapes((F32,) * 6),
        scratch_shapes=[pltpu.VMEM((HEADS, HD, HD), F32)], compiler_params=_cparams(("arbitrary",)),
    )(*prep, s_hist, do, *after)


def _mix_specs():
    pc = lambda c: pl.BlockSpec((GCH, HW), lambda i: (i, c))
    return [pl.BlockSpec((GCH, HW), lambda i: (i, 0)), pc(3), pc(4), pc(5), _whole((1, HD)), _whole((1, HW)),
            _whole((1, HW)), _whole((HEADS, GCH, GCH)), _whole((GCH, LANE))]


def _mix(o, proj, ong, lng, lnb, ws, bst):
    def body(o_ref, z_ref, u_ref, v_ref, ong_ref, lng_ref, lnb_ref, ws_ref, bs_ref, m_ref):
        m_ref[...] = _mix_fn(o_ref[...], z_ref[...], u_ref[...], v_ref[...], ong_ref[...], lng_ref[...],
                             lnb_ref[...], ws_ref[...], bs_ref[...]).astype(BF16)

    return pl.pallas_call(
        body, name="mix", grid=(T // GCH,), in_specs=_mix_specs(),
        out_specs=pl.BlockSpec((GCH, D), lambda i: (i, 0)), out_shape=jax.ShapeDtypeStruct((T, D), BF16),
        compiler_params=_cparams(("parallel",)),
    )(o, proj, proj, proj, ong, lng, lnb, ws, bst)


def _mix_bwd(o, proj, ong, lng, lnb, ws, bst, dmix):
    def body(o_ref, z_ref, u_ref, v_ref, ong_ref, lng_ref, lnb_ref, ws_ref, bs_ref, dm_ref,
             do_ref, dzuv_ref, dong_ref, dlng_ref, dlnb_ref, dws_ref, dbs_ref):
        _, vjp = jax.vjp(_mix_fn, o_ref[...], z_ref[...], u_ref[...], v_ref[...], ong_ref[...], lng_ref[...],
                         lnb_ref[...], ws_ref[...], bs_ref[...])
        do, dz, du, dv, dong, dlng, dlnb, dws, dbs = vjp(dm_ref[...])
        do_ref[...] = do
        dzuv_ref[:, 0:HW] = dz.astype(BF16)
        dzuv_ref[:, HW:2 * HW] = du.astype(BF16)
        dzuv_ref[:, 2 * HW:3 * HW] = dv.astype(BF16)
        acc = [(dong_ref, dong), (dlng_ref, dlng), (dlnb_ref, dlnb), (dws_ref, dws), (dbs_ref, dbs)]

        @pl.when(pl.program_id(0) == 0)
        def _():
            for r, val in acc:
                r[...] = val

        @pl.when(pl.program_id(0) > 0)
        def _():
            for r, val in acc:
                r[...] += val

    shp = lambda *s: jax.ShapeDtypeStruct(s, F32)
    return pl.pallas_call(
        body, name="mix_bwd", grid=(T // GCH,),
        in_specs=_mix_specs() + [pl.BlockSpec((GCH, D), lambda i: (i, 0))],
        out_specs=[pl.BlockSpec((GCH, HW), lambda i: (i, 0)), pl.BlockSpec((GCH, 3 * HW), lambda i: (i, 1)),
                   _whole((1, HD)), _whole((1, HW)), _whole((1, HW)), _whole((HEADS, GCH, GCH)), _whole((GCH, LANE))],
        out_shape=[shp(T, HW), jax.ShapeDtypeStruct((T, NP), BF16), shp(1, HD), shp(1, HW), shp(1, HW),
                   shp(HEADS, GCH, GCH), shp(GCH, LANE)],
        compiler_params=_cparams(("arbitrary",)),
    )(o, proj, proj, proj, ong, lng, lnb, ws, bst, dmix)


def _swiglu_epilogue(accs, _):
    gate, up = accs
    return [gate, up, _silu(gate) * up]


def _swiglu_bwd_epilogue(accs, extras):
    dact = accs[0]
    gate, up = (e.astype(F32) for e in extras)
    sg = _sigmoid(gate)
    return [dact * up * (sg * (1.0 + gate * (1.0 - sg))), dact * (gate * sg)]


def _layer_fwd(h, p):
    hn = p.pop("hn") if "hn" in p else _rmsnorm("rms_mix", h, p["norm_mix"])
    proj = _mm("in_proj", "nn", hn[None], [p["w_in"][None]], tm=1024, tn=640, tk=D, sub_m=2)[0][0]
    qkv = _prep(proj, p["conv_w"])
    bg = _gates(proj, p["a_log"], p["dt_bias"])
    prep = _chunk_prep(qkv, bg)
    o, s_hist = _chunk_scan(prep, p.pop("before_scan")(prep[0]) if "before_scan" in p else ())
    if "late" in p:
        p.update(p.pop("late")(o))
    mix = _mix(o, proj, p["o_norm_g"], p["ln_v_g"], p["ln_v_b"], p["w_s"], p["bst"])
    h1 = _mm("out_proj", "nn", mix[None], [p["w_out"].reshape(1, D, D)], tm=1024, tn=512, tk=D, resid=h[None],
             sub_m=2)[0][0]
    h2n = _rmsnorm("rms_ffn", h1, p["norm_ffn"])
    gate, up, act = _mm("ffn_in", "nt", h2n[None], [p["w_gate"], p["w_up"]], tm=1024, tn=FF_SH, tk=D,
                        out_dtypes=(BF16, BF16, BF16), epilogue=_swiglu_epilogue, sub_m=4)
    then = p.pop("before_ffn_out")(act) if "before_ffn_out" in p else ()
    h2 = _mm("ffn_out", "nn", act, [p["w_down"]], tm=1024, tn=512, tk=FF_SH, reduce_g=True, fold_g=True,
             resid=h1[None], sub_m=2, after=then)[0][0]
    saved = dict(h=h, hn=hn, proj=proj, qkv=qkv, bg=bg, prep=prep, o=o, s_hist=s_hist, mix=mix, h1=h1, h2n=h2n,
                 gate=gate, up=up, act=act)
    return h2, saved


def _layer_bwd_ffn(dh2, dh2b, p, s, after=()):
    dh2b = dh2b[None]
    dgate, dup = _mm("ffn_out_bwd", "nt", dh2b, [p["w_down"]], tm=1024, tn=FF_SH, tk=D, out_dtypes=(BF16, BF16),
                     extras=(s["gate"], s["up"]), epilogue=_swiglu_bwd_epilogue, after=after, sub_m=4)
    dh2n = _mm("ffn_gate_bwd", "nn", dgate, [p["w_gate"]], tm=1024, tn=512, tk=FF_SH, reduce_g=True, fold_g=True,
               sub_m=2)[0]
    dh2n = _mm("ffn_up_bwd", "nn", dup, [p["w_up"]], tm=1024, tn=512, tk=FF_SH, reduce_g=True, fold_g=True,
               resid=dh2n, sub_m=2)[0][0]
    dh1, dh1b, d_norm_ffn = _rmsnorm_bwd("rms_ffn_bwd", dh2n, s["h1"], p["norm_ffn"], dh2)
    d_w_down = _mm("ffn_wdown_grad", "tn", s["act"], [dh2b], tm=FF_SH, tn=512, tk=T)[0]
    d_w_gate = _mm("ffn_wgate_grad", "tn", dgate, [s["h2n"][None]], tm=FF_SH, tn=512, tk=T)[0]
    d_w_up = _mm("ffn_wup_grad", "tn", dup, [s["h2n"][None]], tm=FF_SH, tn=512, tk=T)[0]
    return dh1, dh1b, dict(norm_ffn=d_norm_ffn, w_gate=d_w_gate, w_up=d_w_up, w_down=d_w_down)


def _layer_bwd_mixer(dh1, dh1b, p, s, after=(), midway=None, late=None):
    dh1b = dh1b[None]
    dmix = _mm("out_proj_bwd", "nt", dh1b, [p["w_out"].reshape(1, D, D)], tm=1024, tn=512, tk=D, after=after,
               sub_m=2)[0][0]
    d_w_out = _mm("out_proj_wgrad", "tn", s["mix"][None], [dh1b], tm=512, tn=512, tk=T)[0][0]
    do, dproj, d_ong, d_lng, d_lnb, d_ws, d_bst = _mix_bwd(
        s["o"], s["proj"], p["o_norm_g"], p["ln_v_g"], p["ln_v_b"], p["w_s"], p["bst"], dmix)
    then = midway(do) if midway is not None else ()
    dqkv, dbg = _chunk_prep_bwd(s["qkv"], s["bg"], _chunk_scan_bwd(s["prep"], s["s_hist"], do, then))
    dproj, d_conv = _prep_bwd(s["proj"], p["conv_w"], dqkv, dproj)
    dproj, d_a_log, d_dt_bias = _gates_bwd(s["proj"], p["a_log"], p["dt_bias"], dbg, dproj)
    dproj = dproj[None]
    d_w_in = _mm("in_proj_wgrad", "tn", s["hn"][None], [dproj], tm=512, tn=640, tk=T)[0]
    last = late(dict(w_in=d_w_in, w_out=d_w_out)) if late is not None else ()
    dhn = _mm("in_proj_bwd", "nt", dproj, [p["w_in"][None]], tm=1024, tn=512, tk=NP, after=last,
              sub_m=2)[0][0]
    dh, dhb, d_norm_mix = _rmsnorm_bwd("rms_mix_bwd", dhn, s["h"], p["norm_mix"], dh1)
    grads = dict(norm_mix=d_norm_mix, w_in=d_w_in, conv_w=d_conv, a_log=d_a_log, dt_bias=d_dt_bias, o_norm_g=d_ong,
                 ln_v_g=d_lng, ln_v_b=d_lnb, w_s=d_ws, bst=d_bst, w_out=d_w_out)
    return dh, dhb, grads


def _lanes(v, off=0):
    return jnp.zeros((1, LANE), F32).at[0, off:off + v.shape[0]].set(v)


def _w_in_pieces():
    regions = [(0, 2048, 0), (2048, 2056, BA_OFF), (2056, IN_DIM, 2048)]
    sh = IN_DIM // NCHIP
    out = []
    for j in range(NCHIP):
        for lo, hi, at in regions:
            a, b = max(lo, j * sh), min(hi, (j + 1) * sh)
            if a < b:
                out.append((j, a - j * sh, at + a - lo, b - a))
    return out


W_IN_PIECES = _w_in_pieces()
WT = 256


def _assemble_w_in(gathered, own, place):
    def body(place_ref, g_ref, own_ref, o_ref):
        o_ref[:, IN_DIM:] = jnp.zeros((WT, NP - IN_DIM), BF16)
        mine = own_ref[...]
        for j, src, dst, width in W_IN_PIECES:
            val = jnp.where(place_ref[0] == j, mine[:, src:src + width], g_ref[j, :, src:src + width])
            o_ref[:, dst:dst + width] = val

    sh = IN_DIM // NCHIP
    return pl.pallas_call(
        body, name="assemble_w_in",
        grid_spec=pltpu.PrefetchScalarGridSpec(
            num_scalar_prefetch=1, grid=(D // WT,),
            in_specs=[pl.BlockSpec((NCHIP, WT, sh), lambda i, place_ref: (0, i, 0)),
                      pl.BlockSpec((WT, sh), lambda i, place_ref: (i, 0))],
            out_specs=pl.BlockSpec((WT, NP), lambda i, place_ref: (i, 0))),
        out_shape=jax.ShapeDtypeStruct((D, NP), BF16), compiler_params=_cparams(("parallel",)),
    )(place, gathered, own)


def _layer_params(l, big, small):
    return dict(
        {k: v for k, v in big.items() if k != "conv_w"},
        conv_w=jnp.concatenate([big["conv_w"][j, l] for j in range(NCHIP)], axis=1),
        norm_mix=small["norm_mix"][l][None], norm_ffn=small["norm_ffn"][l][None],
        a_log=_lanes(small["a_log"][l], HEADS), dt_bias=_lanes(small["dt_bias"][l], HEADS),
        o_norm_g=small["o_norm_g"][l][None], ln_v_g=small["ln_v_g"][l][None], ln_v_b=small["ln_v_b"][l][None],
        w_s=small["w_s"][l],
        bst=jnp.pad(small["b_s"][l].T, ((0, 0), (0, LANE - HEADS))),
    )


def _reference_layout(g):
    return dict(
        w_in=g["w_in"],
        w_out=g["w_out"].reshape(NCHIP, D // NCHIP, D),
        w_gate=g["w_gate"], w_up=g["w_up"], w_down=g["w_down"],
        conv_w=g["conv_w"], norm_mix=g["norm_mix"][0], norm_ffn=g["norm_ffn"][0],
        a_log=g["a_log"][0, HEADS:2 * HEADS], dt_bias=g["dt_bias"][0, HEADS:2 * HEADS],
        o_norm_g=g["o_norm_g"][0], ln_v_g=g["ln_v_g"][0], ln_v_b=g["ln_v_b"][0], w_s=g["w_s"],
        b_s=g["bst"][:, :HEADS].T,
    )


def _forward(x, tgt, layers, norm_final):
    h = x
    saved, params = [], []
    for p in layers:
        p = p(h) if callable(p) else p
        h, s = _layer_fwd(h, p)
        saved.append(s)
        params.append(p)
    return (saved, params) + tuple(_loss_head(h, norm_final, tgt))


def _local_step(x, tgt, layers, norm_final):
    saved, layers, loss, dh, dhb, d_norm_final = _forward(x, tgt, layers, norm_final)
    grads = [None] * DEPTH
    for l in reversed(range(DEPTH)):
        dh1, dh1b, g_ffn = _layer_bwd_ffn(dh, dhb, layers[l], saved[l])
        dh, dhb, g_mix = _layer_bwd_mixer(dh1, dh1b, layers[l], saved[l])
        grads[l] = {**g_ffn, **g_mix}
    return loss, dh, grads, d_norm_final


def _place():
    x, y, c = lax.axis_index("x"), lax.axis_index("y"), lax.axis_index("c")
    return x, y, c, [(1 - x, y), (x, 1 - y), (1 - x, 1 - y)]


def _remote(src, dst, send_sem, recv_sem, to):
    return pltpu.make_async_remote_copy(src_ref=src, dst_ref=dst, send_sem=send_sem, recv_sem=recv_sem,
                                        device_id=to, device_id_type=MESH)


def _comm_call(name, body, ins, out_shape, n_sems, aliases=None):
    return pl.pallas_call(
        body, name=name, in_specs=[ANY] * len(ins), out_specs=[ANY] * len(out_shape), out_shape=out_shape,
        scratch_shapes=[pltpu.SemaphoreType.DMA((n,)) for n in n_sems], input_output_aliases=aliases or {},
        compiler_params=pltpu.CompilerParams(has_side_effects=True),
    )(*ins)


def _half_rows(ref, of_c, dim):
    hr = ref.shape[dim] // 2
    return pl.ds(pl.multiple_of(of_c * hr, BF16_ROWS), hr)


def _gather_plan(whole):
    def plan(srcs, lands):
        x, y, c, others = _place()
        chip = 2 * x + y
        out = []
        for src, land, all_of_it in zip(srcs, lands, whole):
            for ox, oy in others:
                if all_of_it:
                    out.append((src, land.at[chip], (ox, oy, c)))
                else:
                    out.append((src.at[_half_rows(src, c, 0)], land.at[chip, _half_rows(src, c, 0)], (ox, oy, c)))
        return out
    return plan


def _forward_halves(lands):
    n = len(lands)

    def body(*refs):
        outs = refs[n:2 * n]
        send_s, recv_s = refs[2 * n:]
        x, y, c, others = _place()
        sibling = (x, y, 1 - c)
        copies = []
        for a in range(n):
            for k, (ox, oy) in enumerate(others):
                mine = outs[a].at[2 * ox + oy, _half_rows(outs[a], c, 1)]
                copies.append(_remote(mine, mine, send_s.at[3 * a + k], recv_s.at[3 * a + k], sibling))
        for cp in copies:
            cp.start()
        for a in range(n):
            for k, (ox, oy) in enumerate(others):
                landed = outs[a].at[2 * ox + oy, _half_rows(outs[a], 1 - c, 1)]
                _remote(landed, landed, send_s.at[3 * a + k], recv_s.at[3 * a + k], sibling).wait_recv()
        for cp in copies:
            cp.wait_send()

    out_shape = [jax.ShapeDtypeStruct(g.shape, g.dtype) for g in lands]
    return _comm_call("forward_halves", body, lands, out_shape, [3 * n, 3 * n], aliases={a: a for a in range(n)})


def _forward_refs(bufs, incoming):
    x, y, c, others = _place()
    return (x, y, 1 - c), [b.at[2 * ox + oy, _half_rows(b, 1 - c if incoming else c, 1)]
                           for b in bufs for ox, oy in others]


def _forward_start(name, bufs, after):
    n = len(bufs)
    bufs = [pltpu.with_memory_space_constraint(b, pltpu.HBM) for b in bufs]

    def body(*refs):
        send_s, recv_s = refs[n + len(after)], refs[n + len(after) + 1]
        sibling, mine = _forward_refs(refs[:n], incoming=False)
        for i, ref in enumerate(mine):
            _remote(ref, ref, send_s.at[i], recv_s.at[i], sibling).start()
        refs[-1][...] = jnp.zeros_like(refs[-1])

    out = pl.pallas_call(
        body, name=name, in_specs=[HBM_SPEC] * n + [ANY] * len(after),
        out_specs=[SEM_SPEC, SEM_SPEC] + [HBM_SPEC] * n + [pl.BlockSpec(memory_space=pltpu.VMEM)],
        out_shape=[pltpu.SemaphoreType.DMA((3 * n,)), pltpu.SemaphoreType.DMA((3 * n,))]
        + [pltpu.HBM(b.shape, b.dtype) for b in bufs] + [jax.ShapeDtypeStruct((F32_ROWS, LANE), F32)],
        input_output_aliases={i: 2 + i for i in range(n)},
        compiler_params=pltpu.CompilerParams(has_side_effects=DATAFLOW),
    )(*bufs, *after)
    return dict(sems=out[:2], bufs=out[2:2 + n], token=out[-1])


def _forward_wait(name, started, after):
    n = len(started["bufs"])

    def body(*refs):
        send_s, recv_s = refs[n], refs[n + 1]
        sibling, mine = _forward_refs(refs[:n], incoming=False)
        _, theirs = _forward_refs(refs[:n], incoming=True)
        for i, (sent, landed) in enumerate(zip(mine, theirs)):
            _remote(sent, sent, send_s.at[i], recv_s.at[i], sibling).wait_send()
            _remote(landed, landed, send_s.at[i], recv_s.at[i], sibling).wait_recv()

    return pl.pallas_call(
        body, name=name, in_specs=[HBM_SPEC] * n + [SEM_SPEC, SEM_SPEC] + [ANY] * len(after),
        out_specs=[HBM_SPEC] * n, out_shape=[pltpu.HBM(b.shape, b.dtype) for b in started["bufs"]],
        input_output_aliases={i: i for i in range(n)},
        compiler_params=pltpu.CompilerParams(has_side_effects=DATAFLOW),
    )(*started["bufs"], *started["sems"], *after)


HBM_SPEC = pl.BlockSpec(memory_space=pltpu.HBM)
SEM_SPEC = pl.BlockSpec(memory_space=pltpu.SEMAPHORE)
DATAFLOW = pltpu.SideEffectType.DATAFLOW_SIDE_EFFECTING


def _exchange_plan(srcs, lands):
    x, y, c, _ = _place()
    plan = []
    for src, land in zip(srcs, lands):
        hr = src.shape[1] // 2
        plan.append((src.at[:, pl.ds(pl.multiple_of((1 - c) * hr, 8), hr)], land, (x, y, 1 - c)))
    return plan


def _scatter_plan(srcs, lands):
    x, y, c, others = _place()
    return [(src.at[2 * ox + oy], land.at[k], (ox, oy, c))
            for src, land in zip(srcs, lands) for k, (ox, oy) in enumerate(others)]


def _split_start(name, plan, srcs, land_shapes, n_copies, after=()):
    n = len(srcs)
    lands = [pltpu.with_memory_space_constraint(lax.empty(s.shape, s.dtype), pltpu.HBM) for s in land_shapes]
    srcs = [pltpu.with_memory_space_constraint(s, pltpu.HBM) for s in srcs]

    def body(*refs):
        send_s, recv_s = refs[2 * n + len(after)], refs[2 * n + len(after) + 1]
        for i, (src, dst, to) in enumerate(plan(refs[:n], refs[n:2 * n])):
            _remote(src, dst, send_s.at[i], recv_s.at[i], to).start()
        refs[-1][...] = jnp.zeros_like(refs[-1])

    thru = [pltpu.HBM(s.shape, s.dtype) for s in srcs + lands]
    out = pl.pallas_call(
        body, name=name, in_specs=[HBM_SPEC] * (2 * n) + [ANY] * len(after),
        out_specs=[SEM_SPEC, SEM_SPEC] + [HBM_SPEC] * (2 * n) + [pl.BlockSpec(memory_space=pltpu.VMEM)],
        out_shape=[pltpu.SemaphoreType.DMA((n_copies,)), pltpu.SemaphoreType.DMA((n_copies,))] + thru
        + [jax.ShapeDtypeStruct((F32_ROWS, LANE), F32)],
        input_output_aliases={i: 2 + i for i in range(2 * n)},
        compiler_params=pltpu.CompilerParams(has_side_effects=DATAFLOW),
    )(*srcs, *lands, *after)
    return dict(sems=out[:2], srcs=out[2:2 + n], lands=out[2 + n:2 + 2 * n], token=out[-1])


def _split_wait(name, plan, started, after):
    n = len(started["srcs"])
    after = list(after) if isinstance(after, (list, tuple)) else [after]

    def body(*refs):
        send_s, recv_s = refs[2 * n], refs[2 * n + 1]
        for i, (src, dst, to) in enumerate(plan(refs[:n], refs[n:2 * n])):
            cp = _remote(src, dst, send_s.at[i], recv_s.at[i], to)
            cp.wait_send()
            cp.wait_recv()

    arrs = list(started["srcs"]) + list(started["lands"])
    out = pl.pallas_call(
        body, name=name, in_specs=[HBM_SPEC] * (2 * n) + [SEM_SPEC, SEM_SPEC] + [ANY] * len(after),
        out_specs=[HBM_SPEC] * (2 * n), out_shape=[pltpu.HBM(s.shape, s.dtype) for s in arrs],
        input_output_aliases={i: i for i in range(2 * n)},
        compiler_params=pltpu.CompilerParams(has_side_effects=DATAFLOW),
    )(*arrs, *started["sems"], *after)
    return out[:n], out[n:]


def _join_halves(name, rs):
    n = len(rs)

    def body(*refs):
        outs = refs[n:2 * n]
        send_s, recv_s = refs[2 * n:]
        x, y, c, _ = _place()
        sibling = (x, y, 1 - c)

        def half(a, of_c):
            hr = outs[a].shape[1] // 2
            return outs[a].at[:, pl.ds(pl.multiple_of(of_c * hr, 8), hr)]

        copies = [_remote(half(a, c), half(a, c), send_s.at[a], recv_s.at[a], sibling) for a in range(n)]
        for cp in copies:
            cp.start()
        for a in range(n):
            landed = half(a, 1 - c)
            _remote(landed, landed, send_s.at[a], recv_s.at[a], sibling).wait_recv()
        for cp in copies:
            cp.wait_send()

    out_shape = [jax.ShapeDtypeStruct(r.shape, r.dtype) for r in rs]
    return _comm_call(name, body, rs, out_shape, [n, n], aliases={a: a for a in range(n)})


def _allreduce_small(buf, after=()):
    r = buf.shape[0]
    hr = r // 2

    def body(in_ref, *refs):
        out_ref, theirs, by_chip, send_s, recv_s = refs[len(after):]
        x, y, c, others = _place()
        chip = 2 * x + y
        sibling = (x, y, 1 - c)
        mine = pl.ds(pl.multiple_of(c * hr, F32_ROWS), hr)
        swap = _remote(in_ref, theirs, send_s.at[0], recv_s.at[0], sibling)
        swap.start()
        swap.wait()
        by_chip[chip] = in_ref[mine, :] + theirs[mine, :]
        sends = [_remote(by_chip.at[chip], by_chip.at[chip], send_s.at[1 + k], recv_s.at[1 + k], (ox, oy, c))
                 for k, (ox, oy) in enumerate(others)]
        for cp in sends:
            cp.start()
        for k, (ox, oy) in enumerate(others):
            landed = by_chip.at[2 * ox + oy]
            _remote(landed, landed, send_s.at[1 + k], recv_s.at[1 + k], (ox, oy, c)).wait_recv()
        for cp in sends:
            cp.wait_send()
        out_ref[mine, :] = (by_chip[0] + by_chip[1]) + (by_chip[2] + by_chip[3])
        back = _remote(out_ref.at[mine], out_ref.at[mine], send_s.at[NCHIP], recv_s.at[NCHIP], sibling)
        back.start()
        other = out_ref.at[pl.ds(pl.multiple_of((1 - c) * hr, F32_ROWS), hr)]
        _remote(other, other, send_s.at[NCHIP], recv_s.at[NCHIP], sibling).wait_recv()
        back.wait_send()

    vm = pl.BlockSpec(memory_space=pltpu.VMEM)
    return pl.pallas_call(
        body, name="allreduce_small", in_specs=[vm] + [ANY] * len(after), out_specs=vm,
        out_shape=jax.ShapeDtypeStruct((r, LANE), F32),
        scratch_shapes=[pltpu.VMEM((r, LANE), F32), pltpu.VMEM((NCHIP, hr, LANE), F32),
                        pltpu.SemaphoreType.DMA((NCHIP + 1,)), pltpu.SemaphoreType.DMA((NCHIP + 1,))],
        compiler_params=pltpu.CompilerParams(has_side_effects=True, vmem_limit_bytes=VMEM_LIMIT),
    )(buf, *after)


MAX_ROW_TILE = 512
BF16_ROWS = 16


def _row_tile(rows):
    for t in range(min(rows, MAX_ROW_TILE) // BF16_ROWS * BF16_ROWS, 0, -BF16_ROWS):
        if rows % t == 0:
            return t
    raise ValueError(rows)


def _sum_halves(g, theirs, c_arr):
    nch, rows, cols = g.shape
    hr = rows // 2
    tr = _row_tile(hr)

    def body(c_ref, g_ref, t_ref, o_ref, ob_ref):
        s = g_ref[...] + t_ref[...]
        o_ref[...] = s
        ob_ref[...] = s.astype(BF16)

    blk = pl.BlockSpec((None, tr, cols), lambda j, i, c_ref: (j, i, 0))
    return pl.pallas_call(
        body, name="sum_halves",
        grid_spec=pltpu.PrefetchScalarGridSpec(
            num_scalar_prefetch=1, grid=(nch, hr // tr),
            in_specs=[pl.BlockSpec((None, None, tr, cols), lambda j, i, c_ref: (j, c_ref[0], i, 0)), blk],
            out_specs=[blk, blk]),
        out_shape=[jax.ShapeDtypeStruct((nch, hr, cols), F32), jax.ShapeDtypeStruct((nch, hr, cols), BF16)],
        compiler_params=_cparams(("parallel", "parallel")),
    )(c_arr, g.reshape(nch, 2, hr, cols), theirs)


def _sum_halves_w_in(g, theirs, c_arr):
    hr = D // 2
    sh = IN_DIM // NCHIP

    def body(c_ref, g_ref, t_ref, o_ref, ob_ref):
        s = g_ref[...] + t_ref[...]
        for j, dst, src, width in W_IN_PIECES:
            o_ref[j, :, dst:dst + width] = s[:, src:src + width]
            ob_ref[j, :, dst:dst + width] = s[:, src:src + width].astype(BF16)

    out = pl.BlockSpec((NCHIP, WT, sh), lambda i, c_ref: (0, i, 0))
    return pl.pallas_call(
        body, name="sum_halves_w_in",
        grid_spec=pltpu.PrefetchScalarGridSpec(
            num_scalar_prefetch=1, grid=(hr // WT,),
            in_specs=[pl.BlockSpec((None, WT, NP), lambda i, c_ref: (c_ref[0], i, 0)),
                      pl.BlockSpec((None, WT, NP), lambda i, c_ref: (0, i, 0))],
            out_specs=[out, out]),
        out_shape=[jax.ShapeDtypeStruct((NCHIP, hr, sh), F32), jax.ShapeDtypeStruct((NCHIP, hr, sh), BF16)],
        compiler_params=_cparams(("parallel",)),
    )(c_arr, g.reshape(2, hr, NP), theirs)


def _sum_chips(p, q, place, l, into=None, after=()):
    extra = ([into] if into is not None else []) + list(after)
    _, rows, cols = p.shape
    tr = _row_tile(rows)
    steps = rows // tr

    def body(place_ref, p_ref, q0, q1, q2, *rest):
        rest[-1][...] = ((p_ref[...] + q0[...].astype(F32)) + q1[...].astype(F32)) + q2[...].astype(F32)

    qs = lambda k: pl.BlockSpec((None, tr, cols), lambda i, place_ref: (k, i, 0))
    return pl.pallas_call(
        body, name="sum_chips",
        grid_spec=pltpu.PrefetchScalarGridSpec(
            num_scalar_prefetch=1, grid=(steps,),
            in_specs=[pl.BlockSpec((None, tr, cols), lambda i, place_ref: (place_ref[0], i, 0)), qs(0), qs(1), qs(2)]
            + [ANY] * len(extra),
            out_specs=pl.BlockSpec((None, tr, cols), lambda i, place_ref: (l, place_ref[1] * steps + i, 0))),
        out_shape=jax.ShapeDtypeStruct((DEPTH, 2 * rows, cols), F32),
        input_output_aliases={5: 0} if into is not None else {},
        compiler_params=_cparams(("parallel",)),
    )(place, p, q, q, q, *extra)


def _adamw(w, g, m, v):
    layers, rows, cols = w.shape
    tr = _row_tile(rows)

    def body(w_ref, g_ref, m_ref, v_ref, d_ref, nm_ref, nv_ref):
        gv = g_ref[...]
        nm = ADAM_B1 * m_ref[...] + (1.0 - ADAM_B1) * gv
        nv = ADAM_B2 * v_ref[...] + (1.0 - ADAM_B2) * jnp.square(gv)
        m_hat = nm / (1.0 - ADAM_B1 ** ADAM_STEP)
        v_hat = nv / (1.0 - ADAM_B2 ** ADAM_STEP)
        d_ref[...] = -ADAM_LR * (m_hat / (jnp.sqrt(v_hat) + ADAM_EPS) + ADAM_WD * w_ref[...])
        nm_ref[...] = nm
        nv_ref[...] = nv

    blk = pl.BlockSpec((None, tr, cols), lambda l, i: (l, i, 0))
    return pl.pallas_call(
        body, name="adamw", grid=(layers, rows // tr), in_specs=[blk] * 4, out_specs=[blk] * 3,
        out_shape=[jax.ShapeDtypeStruct(w.shape, F32)] * 3, compiler_params=_cparams(("parallel", "parallel")),
    )(w, g, m, v)


BIG = ("w_in", "w_out", "w_gate", "w_up", "w_down")
SMALL = ("norm_mix", "a_log", "dt_bias", "o_norm_g", "ln_v_g", "ln_v_b", "w_s", "b_s", "norm_ffn", "norm_final")
ORDER = ("norm_mix", "w_in", "conv_w", "a_log", "dt_bias", "o_norm_g", "ln_v_g", "ln_v_b", "w_s", "b_s", "w_out",
         "norm_ffn", "w_gate", "w_up", "w_down", "norm_final")


F32_ROWS = 8
PACK_ROWS = 128


def _lane_rows(size):
    return -(-size // (F32_ROWS * LANE)) * F32_ROWS


def _pack(arrs):
    parts = [jnp.pad(a.reshape(-1), (0, _lane_rows(a.size) * LANE - a.size)).reshape(-1, LANE) for a in arrs]
    rows = sum(p.shape[0] for p in parts)
    if rows % PACK_ROWS:
        parts.append(jnp.zeros((-rows % PACK_ROWS, LANE), F32))
    return jnp.concatenate(parts, axis=0)


def _unpack(buf, like):
    out, row = [], 0
    for a in like:
        n = _lane_rows(a.size)
        out.append(buf[row:row + n].reshape(-1)[:a.size].reshape(a.shape))
        row += n
    return out


def kernel(x, norm_mix, w_in, conv_w, a_log, dt_bias, o_norm_g, ln_v_g, ln_v_b, w_s, b_s, w_out, norm_ffn, w_gate, w_up, w_down, norm_final, loss_target, m_norm_mix, m_w_in, m_conv_w, m_a_log, m_dt_bias, m_o_norm_g, m_ln_v_g, m_ln_v_b, m_w_s, m_b_s, m_w_out, m_norm_ffn, m_w_gate, m_w_up, m_w_down, m_norm_final, v_norm_mix, v_w_in, v_conv_w, v_a_log, v_dt_bias, v_o_norm_g, v_ln_v_g, v_ln_v_b, v_w_s, v_b_s, v_w_out, v_norm_ffn, v_w_gate, v_w_up, v_w_down, v_norm_final):
    w = dict(norm_mix=norm_mix, w_in=w_in, conv_w=conv_w, a_log=a_log, dt_bias=dt_bias, o_norm_g=o_norm_g,
             ln_v_g=ln_v_g, ln_v_b=ln_v_b, w_s=w_s, b_s=b_s, w_out=w_out, norm_ffn=norm_ffn, w_gate=w_gate, w_up=w_up,
             w_down=w_down, norm_final=norm_final)
    m = dict(norm_mix=m_norm_mix, w_in=m_w_in, conv_w=m_conv_w, a_log=m_a_log, dt_bias=m_dt_bias, o_norm_g=m_o_norm_g,
             ln_v_g=m_ln_v_g, ln_v_b=m_ln_v_b, w_s=m_w_s, b_s=m_b_s, w_out=m_w_out, norm_ffn=m_norm_ffn,
             w_gate=m_w_gate, w_up=m_w_up, w_down=m_w_down, norm_final=m_norm_final)
    v = dict(norm_mix=v_norm_mix, w_in=v_w_in, conv_w=v_conv_w, a_log=v_a_log, dt_bias=v_dt_bias, o_norm_g=v_o_norm_g,
             ln_v_g=v_ln_v_g, ln_v_b=v_ln_v_b, w_s=v_w_s, b_s=v_b_s, w_out=v_w_out, norm_ffn=v_norm_ffn,
             w_gate=v_w_gate, w_up=v_w_up, w_down=v_w_down, norm_final=v_norm_final)
    chip = 2 * lax.axis_index("x") + lax.axis_index("y")
    place = jnp.stack([chip, lax.axis_index("c")]).astype(jnp.int32)
    c_arr = place[1:]

    def kernel_view(n, a):
        return jnp.swapaxes(a, 1, 2) if n in ("w_gate", "w_up") else a

    own = {n: [kernel_view(n, w[n])[l].astype(BF16) for l in range(DEPTH)] for n in BIG}
    by_chip = lambda a: jax.ShapeDtypeStruct((NCHIP,) + a.shape, a.dtype)

    def start(name, srcs, whole, after=()):
        return _split_start(name, _gather_plan(whole), srcs, [by_chip(a) for a in srcs], 3 * len(srcs), after)

    def finish(name, started, whole, after):
        srcs, lands = _split_wait(name, _gather_plan(whole), started, after)
        passed = iter(_forward_halves([g for g, all_of_it in zip(lands, whole) if not all_of_it]))
        lands = [g if all_of_it else next(passed) for g, all_of_it in zip(lands, whole)]
        return srcs, [lax.dynamic_update_index_in_dim(g, o, chip, 0) for g, o in zip(lands, srcs)]

    ffn = BIG[1:]
    first = start("gather_first_start", [own["w_in"][0], conv_w], [False, True])
    early = start("gather_early_start", [own[n][0] for n in ffn], [False] * len(ffn), [first["token"]])
    mid = start("gather_mid_start", [own["w_in"][1]], [False], [early["token"]])
    later = start("gather_later_start", [own[n][1] for n in ffn], [False] * len(ffn), [mid["token"]])
    hn = _rmsnorm("rms_mix", x[0], norm_mix[0][None])
    (own_w_in, _), (w_in_by_chip, conv_by_chip) = finish("gather_first_wait", first, [False, True], [later["token"], hn])

    passing = {}

    def pass_on(tag, started, n):
        def at(after):
            srcs, lands = _split_wait(f"gather_{tag}_wait", _gather_plan([False] * n), started, after)
            passing[tag] = srcs, _forward_start(f"forward_{tag}_start", lands, ())
            return [passing[tag][1]["token"]]
        return at

    def passed_on(tag, after):
        srcs, fwd = passing[tag]
        lands = _forward_wait(f"forward_{tag}_wait", fwd, [after])
        return srcs, [lax.dynamic_update_index_in_dim(g, o, chip, 0) for g, o in zip(lands, srcs)]

    def late(tag):
        return lambda after: dict(zip(ffn, passed_on(tag, after)[1]))

    layer0 = _layer_params(0, dict(
        hn=hn, w_in=_assemble_w_in(w_in_by_chip, own_w_in, place), conv_w=conv_by_chip, late=late("early"),
        before_scan=pass_on("early", early, len(ffn)), before_ffn_out=pass_on("mid", mid, 1)), w)

    def layer1(after):
        (own_w_in1,), (w_in1_by_chip,) = passed_on("mid", after)
        return _layer_params(1, dict(w_in=_assemble_w_in(w_in1_by_chip, own_w_in1, place), conv_w=conv_by_chip,
                                     late=late("later"), before_scan=pass_on("later", later, len(ffn))), w)

    saved, layers, loss_lanes, dh, dhb, d_norm_final = _forward(x[0], loss_target[0], [layer0, layer1],
                                                                 norm_final[None])

    sums, arrived = {}, {}

    def exchange_start(tag, l, names, grads, after=()):
        mine = [grads[n] for n in names]
        shapes = [jax.ShapeDtypeStruct((g.shape[0], g.shape[1] // 2, g.shape[2]), F32) for g in mine]
        return tag, l, names, _split_start(f"exchange_{tag}_start", _exchange_plan, mine, shapes, len(mine), after)

    def add_halves(l, names, mine, theirs):
        for n, g, t in zip(names, mine, theirs):
            sums[l, n] = (_sum_halves_w_in if n == "w_in" else _sum_halves)(g, t, c_arr)

    def exchange_wait(handle, after):
        tag, l, names, started = handle
        add_halves(l, names, *_split_wait(f"exchange_{tag}_wait", _exchange_plan, started, after))

    def scatter_start(tag, l, names, after=()):
        partial = [sums[l, n][1] for n in names]
        shapes = [jax.ShapeDtypeStruct((3,) + p.shape[1:], p.dtype) for p in partial]
        return tag, l, names, _split_start(f"scatter_{tag}_start", _scatter_plan, partial, shapes, 3 * len(names), after)

    def scatter_wait(handle, after):
        tag, l, names, started = handle
        for n, q in zip(names, _split_wait(f"scatter_{tag}_wait", _scatter_plan, started, after)[1]):
            arrived[l, n] = q

    last = DEPTH - 1
    swiglu = BIG[2:]
    dh1, dh1b, g_ffn = _layer_bwd_ffn(dh, dhb, layers[last], saved[last])
    dh, dhb, g_mix = _layer_bwd_mixer(dh1, dh1b, layers[last], saved[last])
    gl = [None, _reference_layout({**g_ffn, **g_mix})]
    ex_last = exchange_start("last", last, BIG, gl[last])
    dh1, dh1b, g_ffn = _layer_bwd_ffn(dh, dhb, layers[0], saved[0], after=[ex_last[-1]["token"]])
    exchange_wait(ex_last, dh1)
    sc_last = scatter_start("last", last, BIG)
    ex_ffn = exchange_start("swiglu", 0, swiglu, g_ffn, [sc_last[-1]["token"]])
    sc_ffn = []

    def midway(do):
        exchange_wait(ex_ffn, do)
        sc_ffn.append(scatter_start("swiglu", 0, swiglu))
        return [sc_ffn[0][-1]["token"]]

    ex_rest = []

    def late(grads):
        rest_grads = dict(w_in=grads["w_in"], w_out=grads["w_out"].reshape(NCHIP, D // NCHIP, D))
        ex_rest.append(exchange_start("rest", 0, BIG[:2], rest_grads))
        return [ex_rest[0][-1]["token"]]

    dx, _, g_mix = _layer_bwd_mixer(dh1, dh1b, layers[0], saved[0], after=[ex_ffn[-1]["token"]], midway=midway,
                                    late=late)
    scatter_wait(sc_last, dx)
    scatter_wait(sc_ffn[0], dx)
    gl[0] = _reference_layout({**g_ffn, **g_mix})

    small_g = [jnp.stack([gl[l][n] for l in range(DEPTH)]) for n in SMALL[:-1]] + [d_norm_final[0]]
    conv_g = jnp.stack([gl[l]["conv_w"] for l in range(DEPTH)])
    summed = small_g + [conv_g, loss_lanes[0, :1]]
    total = _allreduce_small(_pack(summed))
    exchange_wait(ex_rest[0], total)
    sc_rest = scatter_start("rest", 0, BIG[:2])

    travelling = [sc_rest[-1]["token"]]
    reduced, g_out, delta, new_m, new_v = {}, {}, {}, {}, {}

    def adamw_large(names, joined):
        for n, g in zip(names, joined):
            res = _adamw(kernel_view(n, w[n]), g, kernel_view(n, m[n]), kernel_view(n, v[n]))
            g_out[n], delta[n], new_m[n], new_v[n] = (kernel_view(n, a) for a in (g,) + tuple(res))

    for n in BIG:
        for l in (range(DEPTH) if n in swiglu else [last]):
            reduced[n] = _sum_chips(sums[l, n][0], arrived[l, n], place, l, into=reduced.get(n), after=travelling)
    adamw_large(swiglu, _join_halves("join_swiglu", [reduced[n] for n in swiglu]))
    scatter_wait(sc_rest, [new_v[n] for n in swiglu] + [reduced[n] for n in BIG[:2]])
    for n in BIG[:2]:
        reduced[n] = _sum_chips(sums[0, n][0], arrived[0, n], place, 0, into=reduced[n])
    adamw_large(BIG[:2], _join_halves("join_rest", [reduced[n] for n in BIG[:2]]))
    *small_r, conv_r, loss = _unpack(total, summed)
    g_out.update(zip(SMALL, small_r))
    g_out["conv_w"] = lax.dynamic_slice_in_dim(conv_r, chip * conv_w.shape[2], conv_w.shape[2], axis=2)

    rest = SMALL + ("conv_w",)
    like = [w[n] for n in rest]
    d, nm, nv = _adamw(*[_pack([src[n] for n in rest])[None] for src in (w, g_out, m, v)])
    for dst, buf in ((delta, d), (new_m, nm), (new_v, nv)):
        dst.update(zip(rest, _unpack(buf[0], like)))

    return (loss[0], dx[None], *[g_out[n] for n in ORDER], *[delta[n] for n in ORDER], *[new_m[n] for n in ORDER],
            *[new_v[n] for n in ORDER])
```

```python
import functools

import jax
import jax.numpy as jnp
from jax import lax
from jax.experimental import pallas as pl
from jax.experimental.pallas import tpu as pltpu

F32 = jnp.float32
BF16 = jnp.bfloat16
MESH = pl.DeviceIdType.MESH
ANY = pl.BlockSpec(memory_space=pl.ANY)
HIGHEST = lax.Precision.HIGHEST

T = 2048
D = 1024
DEPTH = 2
NCHIP = 4
HEADS = 4
HD = 128
HW = HEADS * HD
CH = 64
GCH = 128
IN_DIM = 3080
NP = 3200
BA_OFF = 3072
FF_SH = 704
EPS = 1e-6
LANE = 128
VMEM_LIMIT = 56 * 1024 * 1024

ADAM_LR = 0.001
ADAM_B1 = 0.9
ADAM_B2 = 0.999
ADAM_EPS = 1e-08
ADAM_WD = 0.01
ADAM_STEP = 10


def _cparams(sem=None):
    return pltpu.CompilerParams(dimension_semantics=sem, vmem_limit_bytes=VMEM_LIMIT)


_DIMS = {"nn": (((1,), (0,)), ((), ())), "nt": (((1,), (1,)), ((), ())), "tn": (((0,), (0,)), ((), ()))}


def _mm(name, mode, a, bs, *, tm, tn, tk, out_dtypes=(F32,), reduce_g=False, resid=None, extras=(), epilogue=None,
        b_spec=None, n_n=None, after=(), fold_g=False, sub_m=1):
    assert sub_m == 1 or (mode != "tn" and tm % (8 * sub_m) == 0), (name, sub_m)
    nb = len(bs)
    ga = a.shape[0]
    gbs = [1 if b_spec is not None else b.shape[0] for b in bs]
    g_n = max([ga] + gbs)
    if mode == "tn":
        k_n, m_n = a.shape[1:]
    else:
        m_n, k_n = a.shape[1:]
    if n_n is None:
        n_n = bs[0].shape[1] if mode == "nt" else bs[0].shape[2]
    assert m_n % tm == 0 and n_n % tn == 0 and k_n % tk == 0, (name, m_n, n_n, k_n)
    mi, nj, kk = m_n // tm, n_n // tn, k_n // tk
    lead = g_n if fold_g else None
    if reduce_g:
        g_steps = 1 if fold_g else g_n
        grid = (mi, nj, g_steps, kk)
        ids = lambda i, j, g, k: (g, i, j, k)
        n_red = g_steps * kk
        red_idx = lambda: pl.program_id(2) * kk + pl.program_id(3)
        sem = ("parallel", "parallel", "arbitrary", "arbitrary")
    else:
        grid = (g_n, mi, nj, kk)
        ids = lambda g, i, j, k: (g, i, j, k)
        n_red = kk
        red_idx = lambda: pl.program_id(3)
        sem = ("parallel", "parallel", "parallel", "arbitrary")

    def pick(gsz, g):
        return g if gsz > 1 else 0

    def a_map(*p):
        g, i, j, k = ids(*p)
        return (pick(ga, g), k, i) if mode == "tn" else (pick(ga, g), i, k)

    def b_map(gsz):
        def f(*p):
            g, i, j, k = ids(*p)
            if b_spec is not None:
                return b_spec[1](g, i, j, k)
            return (pick(gsz, g), j, k) if mode == "nt" else (pick(gsz, g), k, j)
        return f

    def o_map(gsz):
        def f(*p):
            g, i, j, k = ids(*p)
            return (0 if reduce_g else pick(gsz, g), i, j)
        return f

    a_spec = pl.BlockSpec((lead, tk, tm) if mode == "tn" else (lead, tm, tk), a_map)
    b_block = b_spec[0] if b_spec is not None else ((lead, tn, tk) if mode == "nt" else (lead, tk, tn))
    b_specs = [pl.BlockSpec(b_block, b_map(gs)) for gs in gbs]
    x_specs = [pl.BlockSpec((None, tm, tn), o_map(e.shape[0])) for e in extras]
    r_specs = [pl.BlockSpec((None, tm, tn), o_map(resid.shape[0]))] if resid is not None else []
    g_out = 1 if reduce_g else g_n
    out_shape = [jax.ShapeDtypeStruct((g_out, m_n, n_n), dt) for dt in out_dtypes]
    out_specs = [pl.BlockSpec((None, tm, tn), o_map(g_out)) for _ in out_dtypes]
    nx, nr, no = len(extras), len(r_specs), len(out_dtypes)
    n_in = 1 + nb + nx + nr + len(after)
    dims = _DIMS[mode]

    def body(*refs):
        a_ref = refs[0]
        b_refs = refs[1:1 + nb]
        x_refs = refs[1 + nb:1 + nb + nx]
        r_refs = refs[1 + nb + nx:1 + nb + nx + nr]
        o_refs = refs[n_in:n_in + no]
        acc_refs = refs[n_in + no:]
        def dots(rows):
            if fold_g:
                return [sum(lax.dot_general(a_ref[g, rows, :], b_ref[g], dims, preferred_element_type=F32)
                            for g in range(g_n)) for b_ref in b_refs]
            av = a_ref[...] if mode == "tn" else a_ref[rows, :]
            return [lax.dot_general(av, b_ref[...], dims, preferred_element_type=F32) for b_ref in b_refs]

        def finish(accs, rows=slice(None)):
            if r_refs:
                accs[0] = accs[0] + r_refs[0][rows, :]
            outs = epilogue(accs, [x[rows, :] for x in x_refs]) if epilogue is not None else accs
            for o_ref, o in zip(o_refs, outs):
                o_ref[rows, :] = o.astype(o_ref.dtype)

        if n_red == 1:
            slabs = [slice(s * (tm // sub_m), (s + 1) * (tm // sub_m)) for s in range(sub_m)]
            ahead = dots(slabs[0])
            for s, rows in enumerate(slabs):
                now, ahead = ahead, (dots(slabs[s + 1]) if s + 1 < sub_m else None)
                finish(now, rows)
            return
        products = dots(slice(None))
        r = red_idx()
        for p, acc in zip(products, acc_refs):
            @pl.when(r == 0)
            def _():
                acc[...] = p

            @pl.when((r > 0) & (r < n_red - 1))
            def _():
                acc[...] += p

        @pl.when(r == n_red - 1)
        def _():
            finish([acc[...] + p for p, acc in zip(products, acc_refs)])

    return pl.pallas_call(
        body, name=name, grid=grid,
        in_specs=[a_spec] + b_specs + x_specs + r_specs + [ANY] * len(after),
        out_specs=out_specs, out_shape=out_shape,
        scratch_shapes=[pltpu.VMEM((tm, tn), F32) for _ in range(nb if n_red > 1 else 0)],
        compiler_params=_cparams(sem),
    )(a, *bs, *extras, *([resid] if resid is not None else []), *after)


def _sigmoid(x):
    return 1.0 / (1.0 + jnp.exp(-x))


def _silu(x):
    return x * _sigmoid(x)


def _gelu(x):
    return 0.5 * x * (1.0 + jnp.tanh(0.7978845608028654 * (x + 0.044715 * (x * x * x))))


def _rms_fn(h, gain):
    return h * lax.rsqrt(jnp.mean(h * h, axis=-1, keepdims=True) + EPS) * gain


def _shift_impl(x, s):
    n = x.shape[0]
    rolled = pltpu.roll(x, s % n, 0)
    row = lax.broadcasted_iota(jnp.int32, x.shape, 0)
    return jnp.where((row >= s) & (row < n + s), rolled, 0.0)


@functools.partial(jax.custom_vjp, nondiff_argnums=(1,))
def _shift(x, s):
    return _shift_impl(x, s)


def _shift_fwd(x, s):
    return _shift_impl(x, s), None


def _shift_bwd(s, _, g):
    return (_shift_impl(g, -s),)


_shift.defvjp(_shift_fwd, _shift_bwd)


def _prep_fn(x, w, qk_scale, is_v):
    y = x * w[3:4, :]
    for i in range(3):
        y = y + _shift(x, 3 - i) * w[i:i + 1, :]
    y = _silu(y)
    nrm = lax.rsqrt(jnp.sum(y * y, axis=-1, keepdims=True) + EPS) * qk_scale
    return y * jnp.where(is_v, 1.0, nrm)


def _softplus(x):
    return jnp.maximum(x, 0.0) + jnp.log(1.0 + jnp.exp(-jnp.abs(x)))


def _gates_fn(ba, a_log, dt_bias):
    lane = lax.broadcasted_iota(jnp.int32, ba.shape, 1)
    beta = _sigmoid(ba)
    g = -jnp.exp(a_log) * _softplus(ba + dt_bias)
    return jnp.where(lane < HEADS, beta, g)


def _dot16(a, b, dims=_DIMS["nn"]):
    return lax.dot_general(a.astype(BF16), b.astype(BF16), dims, preferred_element_type=F32)


def _dot32(a, b):
    return jnp.dot(a, b, preferred_element_type=F32, precision=HIGHEST)


def _dot3(a, b, dims=_DIMS["nn"]):
    return lax.dot_general(a, b, dims, preferred_element_type=F32, precision=lax.Precision.HIGH)


def _tri_inverses(mats):
    row = lax.broadcasted_iota(jnp.int32, (CH, CH), 0)
    col = lax.broadcasted_iota(jnp.int32, (CH, CH), 1)
    eye = (row == col).astype(F32)
    ts = [eye - a for a in mats]
    ps = list(mats)
    for _ in range(5):
        ps = [_dot3(p, p) for p in ps]
        ts = [t + _dot3(t, p) for t, p in zip(ts, ps)]
    return ts


@jax.custom_vjp
def _tri_solves(mats, rhs):
    return [_dot3(t, b) for t, b in zip(_tri_inverses(mats), rhs)]


def _tri_solves_fwd(mats, rhs):
    ts = _tri_inverses(mats)
    xs = [_dot3(t, b) for t, b in zip(ts, rhs)]
    return xs, (ts, xs)


def _tri_solves_bwd(res, dxs):
    ts, xs = res
    dbs = [_dot3(t, dx, _DIMS["tn"]) for t, dx in zip(ts, dxs)]
    return [-_dot3(db, x, _DIMS["nt"]) for db, x in zip(dbs, xs)], dbs


_tri_solves.defvjp(_tri_solves_fwd, _tri_solves_bwd)


def _chunk_prep_fn(xs, bgs):
    row = lax.broadcasted_iota(jnp.int32, (CH, CH), 0)
    col = lax.broadcasted_iota(jnp.int32, (CH, CH), 1)
    incl = row >= col
    strict = row > col
    lmat = incl.astype(F32)
    n = len(xs)
    items = [(i, h) for i in range(n) for h in range(HEADS)]
    part = lambda i, h, c: xs[i][:, c * HW + h * HD:c * HW + (h + 1) * HD]
    q = [part(i, h, 0) for i, h in items]
    k = [part(i, h, 1) for i, h in items]
    v = [part(i, h, 2) for i, h in items]
    beta = [bgs[i][:, h:h + 1] for i, h in items]
    gc_all = [_dot32(lmat, bg) for bg in bgs]
    gc = [gc_all[i][:, HEADS + h:HEADS + h + 1] for i, h in items]
    gmat = [jnp.where(strict, jnp.broadcast_to(bgs[i][:, HEADS + h:HEADS + h + 1], (CH, CH)), 0.0) for i, h in items]
    diff = [_dot3(lmat, m) for m in gmat]
    decay = [jnp.where(incl, jnp.exp(jnp.where(incl, d, 0.0)), 0.0) for d in diff]
    k_beta = [kk * b for kk, b in zip(k, beta)]
    kk_t = [_dot16(kb, kk, _DIMS["nt"]) for kb, kk in zip(k_beta, k)]
    qk_t = [_dot16(qq, kk, _DIMS["nt"]) for qq, kk in zip(q, k)]
    a = [jnp.where(strict, m * d, 0.0) for m, d in zip(kk_t, decay)]
    eg = [jnp.exp(g) for g in gc]
    rhs = [jnp.concatenate([vv * b, kb * e], axis=-1) for vv, b, kb, e in zip(v, beta, k_beta, eg)]
    uw = _tri_solves(a, rhs)
    qk = [m * d for m, d in zip(qk_t, decay)]
    g_last = [g[CH - 1:CH, :] for g in gc]
    qe = [qq * e for qq, e in zip(q, eg)]
    kd = [kk * jnp.exp(gl - g) for kk, gl, g in zip(k, g_last, gc)]
    egl = [jnp.broadcast_to(jnp.exp(gl), (1, HD)) for gl in g_last]
    out = []
    for i in range(n):
        mine = slice(i * HEADS, (i + 1) * HEADS)
        cat = lambda vals: jnp.concatenate(vals[mine], axis=-1)
        out.append((cat([x[:, :HD] for x in uw]), cat([x[:, HD:] for x in uw]), cat(qe), cat(kd),
                    jnp.concatenate([m[None] for m in qk[mine]], axis=0), cat(egl)))
    return out


def _chunk_state_fn(u, w, qe, kd, qk, egl, s):
    ws = [_dot16(a, b) for a, b in zip(w, s)]
    qs = [_dot16(a, b) for a, b in zip(qe, s)]
    v_new = [a - b for a, b in zip(u, ws)]
    o = [a + _dot16(b, c) for a, b, c in zip(qs, qk, v_new)]
    s_new = [a * e + _dot16(b, c, _DIMS["tn"]) for a, e, b, c in zip(s, egl, kd, v_new)]
    return o, s_new


def _mix_fn(o, z, ur, vr, ong, lng, lnb, ws, bst):
    row = lax.broadcasted_iota(jnp.int32, (GCH, GCH), 0)
    col = lax.broadcasted_iota(jnp.int32, (GCH, GCH), 1)
    causal = row >= col
    ug = _gelu(ur)
    vg = _gelu(vr)
    sls = [slice(h * HD, (h + 1) * HD) for h in range(HEADS)]
    oh = [o[:, sl] for sl in sls]
    oh = [x * lax.rsqrt(jnp.mean(x * x, axis=-1, keepdims=True) + EPS) for x in oh]
    outs_dn = [x * ong * _silu(z[:, sl]) for x, sl in zip(oh, sls)]
    vh = [vg[:, sl] for sl in sls]
    mu = [jnp.mean(x, axis=-1, keepdims=True) for x in vh]
    var = [jnp.mean(jnp.square(x - m), axis=-1, keepdims=True) for x, m in zip(vh, mu)]
    vn = [(x - m) * lax.rsqrt(s + EPS) * lng[:, sl] + lnb[:, sl] for x, m, s, sl in zip(vh, mu, var, sls)]
    mixed = [_dot16(jnp.where(causal, ws[h], 0.0), vn[h]) for h in range(HEADS)]
    outs_gm = [ug[:, sl] * (mixed[h] + bst[:, h:h + 1]) for h, sl in enumerate(sls)]
    return jnp.concatenate(outs_dn + outs_gm, axis=-1)


def _loss_fn(h, gain, tgt):
    y = _rms_fn(h, gain)
    return 0.5 * jnp.sum(jnp.mean(jnp.square(y - tgt), axis=-1))


RT = 512


def _rows(n=D):
    return pl.BlockSpec((RT, n), lambda i: (i, 0))


def _whole(shape):
    nd = len(shape)
    return pl.BlockSpec(shape, lambda i: (0,) * nd)


def _rmsnorm(name, h, gain):
    def body(h_ref, g_ref, o_ref):
        o_ref[...] = _rms_fn(h_ref[...], g_ref[...]).astype(BF16)

    return pl.pallas_call(
        body, name=name, grid=(T // RT,), in_specs=[_rows(), _whole((1, D))], out_specs=_rows(),
        out_shape=jax.ShapeDtypeStruct((T, D), BF16), compiler_params=_cparams(("parallel",)),
    )(h, gain)


def _rmsnorm_bwd(name, dhn, h, gain, resid):
    def body(dhn_ref, h_ref, g_ref, r_ref, dh_ref, dh16_ref, dg_ref):
        _, vjp = jax.vjp(_rms_fn, h_ref[...], g_ref[...])
        dh, dg = vjp(dhn_ref[...])
        dh = r_ref[...] + dh
        dh_ref[...] = dh
        dh16_ref[...] = dh.astype(BF16)

        @pl.when(pl.program_id(0) == 0)
        def _():
            dg_ref[...] = dg

        @pl.when(pl.program_id(0) > 0)
        def _():
            dg_ref[...] += dg

    return pl.pallas_call(
        body, name=name, grid=(T // RT,), in_specs=[_rows(), _rows(), _whole((1, D)), _rows()],
        out_specs=[_rows(), _rows(), _whole((1, D))],
        out_shape=[jax.ShapeDtypeStruct((T, D), F32), jax.ShapeDtypeStruct((T, D), BF16),
                   jax.ShapeDtypeStruct((1, D), F32)],
        compiler_params=_cparams(("arbitrary",)),
    )(dhn, h, gain, resid)


def _loss_head(h, gain, tgt):
    def body(h_ref, g_ref, t_ref, l_ref, dh_ref, dh16_ref, dg_ref):
        loss, vjp = jax.vjp(lambda hh, gg: _loss_fn(hh, gg, t_ref[...]), h_ref[...], g_ref[...])
        dh, dg = vjp(jnp.ones((), F32))
        dh_ref[...] = dh
        dh16_ref[...] = dh.astype(BF16)
        lv = jnp.full((1, LANE), loss, F32)

        @pl.when(pl.program_id(0) == 0)
        def _():
            dg_ref[...] = dg
            l_ref[...] = lv

        @pl.when(pl.program_id(0) > 0)
        def _():
            dg_ref[...] += dg
            l_ref[...] += lv

    return pl.pallas_call(
        body, name="loss_head", grid=(T // RT,), in_specs=[_rows(), _whole((1, D)), _rows()],
        out_specs=[_whole((1, LANE)), _rows(), _rows(), _whole((1, D))],
        out_shape=[jax.ShapeDtypeStruct((1, LANE), F32), jax.ShapeDtypeStruct((T, D), F32),
                   jax.ShapeDtypeStruct((T, D), BF16), jax.ShapeDtypeStruct((1, D), F32)],
        compiler_params=_cparams(("arbitrary",)),
    )(h, gain, tgt)


def _prep_flags():
    j = pl.program_id(0)
    qk_scale = jnp.where(j < HEADS, HD ** -0.5, 1.0).astype(F32)
    return qk_scale, j >= 2 * HEADS


def _prep(proj, conv_w):
    def body(x_ref, w_ref, o_ref):
        qk_scale, is_v = _prep_flags()
        o_ref[...] = _prep_fn(x_ref[...], w_ref[...], qk_scale, is_v)

    col = lambda j: (0, j)
    return pl.pallas_call(
        body, name="gdn_prep", grid=(3 * HEADS,),
        in_specs=[pl.BlockSpec((T, HD), col), pl.BlockSpec((4, HD), col)], out_specs=pl.BlockSpec((T, HD), col),
        out_shape=jax.ShapeDtypeStruct((T, 3 * HW), F32), compiler_params=_cparams(("parallel",)),
    )(proj, conv_w)


def _prep_bwd(proj, conv_w, dqkv, dproj):
    def body(x_ref, w_ref, d_ref, _, dx_ref, dw_ref):
        qk_scale, is_v = _prep_flags()
        _, vjp = jax.vjp(lambda x, w: _prep_fn(x, w, qk_scale, is_v), x_ref[...], w_ref[...])
        dx, dw = vjp(d_ref[...])
        dx_ref[...] = dx.astype(BF16)
        dw_ref[...] = dw

    col = lambda j: (0, j)
    return pl.pallas_call(
        body, name="gdn_prep_bwd", grid=(3 * HEADS,),
        in_specs=[pl.BlockSpec((T, HD), col), pl.BlockSpec((4, HD), col), pl.BlockSpec((T, HD), col), ANY],
        out_specs=[pl.BlockSpec((T, HD), col), pl.BlockSpec((4, HD), col)],
        out_shape=[jax.ShapeDtypeStruct((T, NP), BF16), jax.ShapeDtypeStruct((4, 3 * HW), F32)],
        input_output_aliases={3: 0}, compiler_params=_cparams(("parallel",)),
    )(proj, conv_w, dqkv, dproj)


BA_BLK = BA_OFF // LANE


def _gates(proj, a_log, dt_bias):
    def body(x_ref, a_ref, d_ref, o_ref):
        o_ref[...] = _gates_fn(x_ref[...], a_ref[...], d_ref[...])

    return pl.pallas_call(
        body, name="gdn_gates", grid=(1,),
        in_specs=[pl.BlockSpec((T, LANE), lambda i: (0, BA_BLK)), _whole((1, LANE)), _whole((1, LANE))],
        out_specs=_whole((T, LANE)),
        out_shape=jax.ShapeDtypeStruct((T, LANE), F32), compiler_params=_cparams(("arbitrary",)),
    )(proj, a_log, dt_bias)


def _gates_bwd(proj, a_log, dt_bias, dbg, dproj):
    def body(x_ref, a_ref, d_ref, dbg_ref, _, dx_ref, da_ref, dd_ref):
        _, vjp = jax.vjp(_gates_fn, x_ref[...], a_ref[...], d_ref[...])
        dx, da_ref[...], dd_ref[...] = vjp(dbg_ref[...])
        dx_ref[...] = dx.astype(BF16)

    ba = pl.BlockSpec((T, LANE), lambda i: (0, BA_BLK))
    return pl.pallas_call(
        body, name="gdn_gates_bwd", grid=(1,),
        in_specs=[ba, _whole((1, LANE)), _whole((1, LANE)), _whole((T, LANE)), ANY],
        out_specs=[ba, _whole((1, LANE)), _whole((1, LANE))],
        out_shape=[jax.ShapeDtypeStruct((T, NP), BF16), jax.ShapeDtypeStruct((1, LANE), F32),
                   jax.ShapeDtypeStruct((1, LANE), F32)],
        input_output_aliases={4: 0}, compiler_params=_cparams(("arbitrary",)),
    )(proj, a_log, dt_bias, dbg, dproj)


NCK = T // CH
CPS = 4


def _chunk_prep_specs(rev=False):
    at = (lambda n: NCK - 1 - n) if rev else (lambda n: n)
    wide = pl.BlockSpec((CH, HW), lambda n: (at(n), 0))
    return [wide, wide, wide, wide, pl.BlockSpec((HEADS, CH, CH), lambda n: (0, at(n), 0)),
            pl.BlockSpec((None, 1, HW), lambda n: (at(n), 0, 0))]


def _chunk_prep_shapes(dtypes):
    shp = [(T, HW), (T, HW), (T, HW), (T, HW), (HEADS, T, CH), (NCK, 1, HW)]
    return [jax.ShapeDtypeStruct(s, dt) for s, dt in zip(shp, dtypes)]


def _chunk_prep(qkv, bg):
    def body(x_ref, bg_ref, *o_refs):
        rows = [slice(ci * CH, (ci + 1) * CH) for ci in range(CPS)]
        res = _chunk_prep_fn([x_ref[r, :] for r in rows], [bg_ref[r, :] for r in rows])
        for ci, (u, w, qe, kd, qk, egl) in enumerate(res):
            for o_ref, val in zip(o_refs[:4], (u, w, qe, kd)):
                o_ref[rows[ci], :] = val.astype(o_ref.dtype)
            o_refs[4][:, rows[ci], :] = qk.astype(BF16)
            o_refs[5][ci] = egl

    wide = pl.BlockSpec((CPS * CH, HW), lambda n: (n, 0))
    return pl.pallas_call(
        body, name="gdn_chunk_prep", grid=(NCK // CPS,),
        in_specs=[pl.BlockSpec((CPS * CH, 3 * HW), lambda n: (n, 0)), pl.BlockSpec((CPS * CH, LANE), lambda n: (n, 0))],
        out_specs=[wide, wide, wide, wide, pl.BlockSpec((HEADS, CPS * CH, CH), lambda n: (0, n, 0)),
                   pl.BlockSpec((CPS, 1, HW), lambda n: (n, 0, 0))],
        out_shape=_chunk_prep_shapes((F32, BF16, BF16, BF16, BF16, F32)),
        compiler_params=_cparams(("parallel",)),
    )(qkv, bg)


def _chunk_prep_bwd(qkv, bg, cots):
    def body(x_ref, bg_ref, du, dw, dqe, dkd, dqk, degl, dx_ref, dbg_ref):
        rows = [slice(ci * CH, (ci + 1) * CH) for ci in range(CPS)]
        _, vjp = jax.vjp(_chunk_prep_fn, [x_ref[r, :] for r in rows], [bg_ref[r, :] for r in rows])
        dxs, dbgs = vjp([(du[r, :], dw[r, :], dqe[r, :], dkd[r, :], dqk[:, r, :], degl[ci])
                         for ci, r in enumerate(rows)])
        for r, dx, dbg in zip(rows, dxs, dbgs):
            dx_ref[r, :] = dx
            dbg_ref[r, :] = dbg

    wide = pl.BlockSpec((CPS * CH, HW), lambda n: (n, 0))
    return pl.pallas_call(
        body, name="gdn_chunk_prep_bwd", grid=(NCK // CPS,),
        in_specs=[pl.BlockSpec((CPS * CH, 3 * HW), lambda n: (n, 0)), pl.BlockSpec((CPS * CH, LANE), lambda n: (n, 0)),
                  wide, wide, wide, wide, pl.BlockSpec((HEADS, CPS * CH, CH), lambda n: (0, n, 0)),
                  pl.BlockSpec((CPS, 1, HW), lambda n: (n, 0, 0))],
        out_specs=[pl.BlockSpec((CPS * CH, 3 * HW), lambda n: (n, 0)), pl.BlockSpec((CPS * CH, LANE), lambda n: (n, 0))],
        out_shape=[jax.ShapeDtypeStruct((T, 3 * HW), F32), jax.ShapeDtypeStruct((T, LANE), F32)],
        compiler_params=_cparams(("parallel",)),
    )(qkv, bg, *cots)


def _head_args(refs):
    u, w, qe, kd, qk, egl = refs
    sls = [slice(h * HD, (h + 1) * HD) for h in range(HEADS)]
    return ([u[:, sl] for sl in sls], [w[:, sl].astype(F32) for sl in sls], [qe[:, sl].astype(F32) for sl in sls],
            [kd[:, sl].astype(F32) for sl in sls], [qk[h].astype(F32) for h in range(HEADS)],
            [egl[:, sl] for sl in sls])


def _chunk_scan(prep, after=()):
    def body(*refs):
        o_ref, sh_ref, s_ref = refs[6 + len(after):]

        @pl.when(pl.program_id(0) == 0)
        def _():
            s_ref[...] = jnp.zeros_like(s_ref)

        s = [s_ref[h] for h in range(HEADS)]
        for h in range(HEADS):
            sh_ref[h, 0] = s[h]
        o, s_new = _chunk_state_fn(*_head_args(refs[:6]), s)
        for h in range(HEADS):
            o_ref[:, h * HD:(h + 1) * HD] = o[h]
            s_ref[h] = s_new[h]

    return pl.pallas_call(
        body, name="gdn_scan", grid=(NCK,), in_specs=_chunk_prep_specs() + [ANY] * len(after),
        out_specs=[pl.BlockSpec((CH, HW), lambda n: (n, 0)), pl.BlockSpec((HEADS, 1, HD, HD), lambda n: (0, n, 0, 0))],
        out_shape=[jax.ShapeDtypeStruct((T, HW), F32), jax.ShapeDtypeStruct((HEADS, NCK, HD, HD), F32)],
        scratch_shapes=[pltpu.VMEM((HEADS, HD, HD), F32)], compiler_params=_cparams(("arbitrary",)),
    )(*prep, *after)


def _chunk_scan_bwd(prep, s_hist, do, after=()):
    n_in = 8 + len(after)

    def body(*refs):
        sh_ref, do_ref = refs[6:8]
        d_refs = refs[n_in:n_in + 6]
        ds_ref = refs[n_in + 6]

        @pl.when(pl.program_id(0) == 0)
        def _():
            ds_ref[...] = jnp.zeros_like(ds_ref)

        sls = [slice(h * HD, (h + 1) * HD) for h in range(HEADS)]
        _, vjp = jax.vjp(_chunk_state_fn, *_head_args(refs[:6]), [sh_ref[h, 0] for h in range(HEADS)])
        du, dw, dqe, dkd, dqk, degl, ds = vjp(([do_ref[:, sl] for sl in sls], [ds_ref[h] for h in range(HEADS)]))
        for h, sl in enumerate(sls):
            for d_ref, val in zip(d_refs[:4], (du, dw, dqe, dkd)):
                d_ref[:, sl] = val[h]
            d_refs[4][h] = dqk[h]
            d_refs[5][:, sl] = degl[h]
            ds_ref[h] = ds[h]

    rev = lambda n: NCK - 1 - n
    return pl.pallas_call(
        body, name="gdn_scan_bwd", grid=(NCK,),
        in_specs=_chunk_prep_specs(rev=True) + [pl.BlockSpec((HEADS, 1, HD, HD), lambda n: (0, rev(n), 0, 0)),
                                                pl.BlockSpec((CH, HW), lambda n: (rev(n), 0))] + [ANY] * len(after),
        out_specs=_chunk_prep_specs(rev=True), out_shape=_chunk_prep_shapes((F32,) * 6),
        scratch_shapes=[pltpu.VMEM((HEADS, HD, HD), F32)], compiler_params=_cparams(("arbitrary",)),
    )(*prep, s_hist, do, *after)


def _mix_specs():
    pc = lambda c: pl.BlockSpec((GCH, HW), lambda i: (i, c))
    return [pl.BlockSpec((GCH, HW), lambda i: (i, 0)), pc(3), pc(4), pc(5), _whole((1, HD)), _whole((1, HW)),
            _whole((1, HW)), _whole((HEADS, GCH, GCH)), _whole((GCH, LANE))]


def _mix(o, proj, ong, lng, lnb, ws, bst):
    def body(o_ref, z_ref, u_ref, v_ref, ong_ref, lng_ref, lnb_ref, ws_ref, bs_ref, m_ref):
        m_ref[...] = _mix_fn(o_ref[...], z_ref[...], u_ref[...], v_ref[...], ong_ref[...], lng_ref[...],
                             lnb_ref[...], ws_ref[...], bs_ref[...]).astype(BF16)

    return pl.pallas_call(
        body, name="mix", grid=(T // GCH,), in_specs=_mix_specs(),
        out_specs=pl.BlockSpec((GCH, D), lambda i: (i, 0)), out_shape=jax.ShapeDtypeStruct((T, D), BF16),
        compiler_params=_cparams(("parallel",)),
    )(o, proj, proj, proj, ong, lng, lnb, ws, bst)


def _mix_bwd(o, proj, ong, lng, lnb, ws, bst, dmix):
    def body(o_ref, z_ref, u_ref, v_ref, ong_ref, lng_ref, lnb_ref, ws_ref, bs_ref, dm_ref,
             do_ref, dzuv_ref, dong_ref, dlng_ref, dlnb_ref, dws_ref, dbs_ref):
        _, vjp = jax.vjp(_mix_fn, o_ref[...], z_ref[...], u_ref[...], v_ref[...], ong_ref[...], lng_ref[...],
                         lnb_ref[...], ws_ref[...], bs_ref[...])
        do, dz, du, dv, dong, dlng, dlnb, dws, dbs = vjp(dm_ref[...])
        do_ref[...] = do
        dzuv_ref[:, 0:HW] = dz.astype(BF16)
        dzuv_ref[:, HW:2 * HW] = du.astype(BF16)
        dzuv_ref[:, 2 * HW:3 * HW] = dv.astype(BF16)
        acc = [(dong_ref, dong), (dlng_ref, dlng), (dlnb_ref, dlnb), (dws_ref, dws), (dbs_ref, dbs)]

        @pl.when(pl.program_id(0) == 0)
        def _():
            for r, val in acc:
                r[...] = val

        @pl.when(pl.program_id(0) > 0)
        def _():
            for r, val in acc:
                r[...] += val

    shp = lambda *s: jax.ShapeDtypeStruct(s, F32)
    return pl.pallas_call(
        body, name="mix_bwd", grid=(T // GCH,),
        in_specs=_mix_specs() + [pl.BlockSpec((GCH, D), lambda i: (i, 0))],
        out_specs=[pl.BlockSpec((GCH, HW), lambda i: (i, 0)), pl.BlockSpec((GCH, 3 * HW), lambda i: (i, 1)),
                   _whole((1, HD)), _whole((1, HW)), _whole((1, HW)), _whole((HEADS, GCH, GCH)), _whole((GCH, LANE))],
        out_shape=[shp(T, HW), jax.ShapeDtypeStruct((T, NP), BF16), shp(1, HD), shp(1, HW), shp(1, HW),
                   shp(HEADS, GCH, GCH), shp(GCH, LANE)],
        compiler_params=_cparams(("arbitrary",)),
    )(o, proj, proj, proj, ong, lng, lnb, ws, bst, dmix)


def _swiglu_epilogue(accs, _):
    gate, up = accs
    return [gate, up, _silu(gate) * up]


def _swiglu_bwd_epilogue(accs, extras):
    dact = accs[0]
    gate, up = (e.astype(F32) for e in extras)
    sg = _sigmoid(gate)
    return [dact * up * (sg * (1.0 + gate * (1.0 - sg))), dact * (gate * sg)]


def _layer_fwd(h, p):
    hn = p.pop("hn") if "hn" in p else _rmsnorm("rms_mix", h, p["norm_mix"])
    proj = _mm("in_proj", "nn", hn[None], [p["w_in"][None]], tm=1024, tn=640, tk=D, sub_m=2)[0][0]
    qkv = _prep(proj, p["conv_w"])
    bg = _gates(proj, p["a_log"], p["dt_bias"])
    prep = _chunk_prep(qkv, bg)
    o, s_hist = _chunk_scan(prep, p.pop("before_scan")(prep[0]) if "before_scan" in p else ())
    if "late" in p:
        p.update(p.pop("late")(o))
    mix = _mix(o, proj, p["o_norm_g"], p["ln_v_g"], p["ln_v_b"], p["w_s"], p["bst"])
    h1 = _mm("out_proj", "nn", mix[None], [p["w_out"].reshape(1, D, D)], tm=1024, tn=512, tk=D, resid=h[None],
             sub_m=2)[0][0]
    h2n = _rmsnorm("rms_ffn", h1, p["norm_ffn"])
    gate, up, act = _mm("ffn_in", "nt", h2n[None], [p["w_gate"], p["w_up"]], tm=1024, tn=FF_SH, tk=D,
                        out_dtypes=(BF16, BF16, BF16), epilogue=_swiglu_epilogue, sub_m=4)
    then = p.pop("before_ffn_out")(act) if "before_ffn_out" in p else ()
    h2 = _mm("ffn_out", "nn", act, [p["w_down"]], tm=1024, tn=512, tk=FF_SH, reduce_g=True, fold_g=True,
             resid=h1[None], sub_m=2, after=then)[0][0]
    saved = dict(h=h, hn=hn, proj=proj, qkv=qkv, bg=bg, prep=prep, o=o, s_hist=s_hist, mix=mix, h1=h1, h2n=h2n,
                 gate=gate, up=up, act=act)
    return h2, saved


def _layer_bwd_ffn(dh2, dh2b, p, s, after=()):
    dh2b = dh2b[None]
    dgate, dup = _mm("ffn_out_bwd", "nt", dh2b, [p["w_down"]], tm=1024, tn=FF_SH, tk=D, out_dtypes=(BF16, BF16),
                     extras=(s["gate"], s["up"]), epilogue=_swiglu_bwd_epilogue, after=after, sub_m=4)
    dh2n = _mm("ffn_gate_bwd", "nn", dgate, [p["w_gate"]], tm=1024, tn=512, tk=FF_SH, reduce_g=True, fold_g=True,
               sub_m=2)[0]
    dh2n = _mm("ffn_up_bwd", "nn", dup, [p["w_up"]], tm=1024, tn=512, tk=FF_SH, reduce_g=True, fold_g=True,
               resid=dh2n, sub_m=2)[0][0]
    dh1, dh1b, d_norm_ffn = _rmsnorm_bwd("rms_ffn_bwd", dh2n, s["h1"], p["norm_ffn"], dh2)
    d_w_down = _mm("ffn_wdown_grad", "tn", s["act"], [dh2b], tm=FF_SH, tn=512, tk=T)[0]
    d_w_gate = _mm("ffn_wgate_grad", "tn", dgate, [s["h2n"][None]], tm=FF_SH, tn=512, tk=T)[0]
    d_w_up = _mm("ffn_wup_grad", "tn", dup, [s["h2n"][None]], tm=FF_SH, tn=512, tk=T)[0]
    return dh1, dh1b, dict(norm_ffn=d_norm_ffn, w_gate=d_w_gate, w_up=d_w_up, w_down=d_w_down)


def _layer_bwd_mixer(dh1, dh1b, p, s, after=(), midway=None, late=None):
    dh1b = dh1b[None]
    dmix = _mm("out_proj_bwd", "nt", dh1b, [p["w_out"].reshape(1, D, D)], tm=1024, tn=512, tk=D, after=after,
               sub_m=2)[0][0]
    d_w_out = _mm("out_proj_wgrad", "tn", s["mix"][None], [dh1b], tm=512, tn=512, tk=T)[0][0]
    do, dproj, d_ong, d_lng, d_lnb, d_ws, d_bst = _mix_bwd(
        s["o"], s["proj"], p["o_norm_g"], p["ln_v_g"], p["ln_v_b"], p["w_s"], p["bst"], dmix)
    then = midway(do) if midway is not None else ()
    dqkv, dbg = _chunk_prep_bwd(s["qkv"], s["bg"], _chunk_scan_bwd(s["prep"], s["s_hist"], do, then))
    dproj, d_conv = _prep_bwd(s["proj"], p["conv_w"], dqkv, dproj)
    dproj, d_a_log, d_dt_bias = _gates_bwd(s["proj"], p["a_log"], p["dt_bias"], dbg, dproj)
    dproj = dproj[None]
    d_w_in = _mm("in_proj_wgrad", "tn", s["hn"][None], [dproj], tm=512, tn=640, tk=T)[0]
    last = late(dict(w_in=d_w_in, w_out=d_w_out)) if late is not None else ()
    dhn = _mm("in_proj_bwd", "nt", dproj, [p["w_in"][None]], tm=1024, tn=512, tk=NP, after=last,
              sub_m=2)[0][0]
    dh, dhb, d_norm_mix = _rmsnorm_bwd("rms_mix_bwd", dhn, s["h"], p["norm_mix"], dh1)
    grads = dict(norm_mix=d_norm_mix, w_in=d_w_in, conv_w=d_conv, a_log=d_a_log, dt_bias=d_dt_bias, o_norm_g=d_ong,
                 ln_v_g=d_lng, ln_v_b=d_lnb, w_s=d_ws, bst=d_bst, w_out=d_w_out)
    return dh, dhb, grads


def _lanes(v, off=0):
    return jnp.zeros((1, LANE), F32).at[0, off:off + v.shape[0]].set(v)


def _w_in_pieces():
    regions = [(0, 2048, 0), (2048, 2056, BA_OFF), (2056, IN_DIM, 2048)]
    sh = IN_DIM // NCHIP
    out = []
    for j in range(NCHIP):
        for lo, hi, at in regions:
            a, b = max(lo, j * sh), min(hi, (j + 1) * sh)
            if a < b:
                out.append((j, a - j * sh, at + a - lo, b - a))
    return out


W_IN_PIECES = _w_in_pieces()
WT = 256


def _assemble_w_in(gathered, own, place):
    def body(place_ref, g_ref, own_ref, o_ref):
        o_ref[:, IN_DIM:] = jnp.zeros((WT, NP - IN_DIM), BF16)
        mine = own_ref[...]
        for j, src, dst, width in W_IN_PIECES:
            val = jnp.where(place_ref[0] == j, mine[:, src:src + width], g_ref[j, :, src:src + width])
            o_ref[:, dst:dst + width] = val

    sh = IN_DIM // NCHIP
    return pl.pallas_call(
        body, name="assemble_w_in",
        grid_spec=pltpu.PrefetchScalarGridSpec(
            num_scalar_prefetch=1, grid=(D // WT,),
            in_specs=[pl.BlockSpec((NCHIP, WT, sh), lambda i, place_ref: (0, i, 0)),
                      pl.BlockSpec((WT, sh), lambda i, place_ref: (i, 0))],
            out_specs=pl.BlockSpec((WT, NP), lambda i, place_ref: (i, 0))),
        out_shape=jax.ShapeDtypeStruct((D, NP), BF16), compiler_params=_cparams(("parallel",)),
    )(place, gathered, own)


def _layer_params(l, big, small):
    return dict(
        {k: v for k, v in big.items() if k != "conv_w"},
        conv_w=jnp.concatenate([big["conv_w"][j, l] for j in range(NCHIP)], axis=1),
        norm_mix=small["norm_mix"][l][None], norm_ffn=small["norm_ffn"][l][None],
        a_log=_lanes(small["a_log"][l], HEADS), dt_bias=_lanes(small["dt_bias"][l], HEADS),
        o_norm_g=small["o_norm_g"][l][None], ln_v_g=small["ln_v_g"][l][None], ln_v_b=small["ln_v_b"][l][None],
        w_s=small["w_s"][l],
        bst=jnp.pad(small["b_s"][l].T, ((0, 0), (0, LANE - HEADS))),
    )


def _reference_layout(g):
    return dict(
        w_in=g["w_in"],
        w_out=g["w_out"].reshape(NCHIP, D // NCHIP, D),
        w_gate=g["w_gate"], w_up=g["w_up"], w_down=g["w_down"],
        conv_w=g["conv_w"], norm_mix=g["norm_mix"][0], norm_ffn=g["norm_ffn"][0],
        a_log=g["a_log"][0, HEADS:2 * HEADS], dt_bias=g["dt_bias"][0, HEADS:2 * HEADS],
        o_norm_g=g["o_norm_g"][0], ln_v_g=g["ln_v_g"][0], ln_v_b=g["ln_v_b"][0], w_s=g["w_s"],
        b_s=g["bst"][:, :HEADS].T,
    )


def _forward(x, tgt, layers, norm_final):
    h = x
    saved, params = [], []
    for p in layers:
        p = p(h) if callable(p) else p
        h, s = _layer_fwd(h, p)
        saved.append(s)
        params.append(p)
    return (saved, params) + tuple(_loss_head(h, norm_final, tgt))


def _local_step(x, tgt, layers, norm_final):
    saved, layers, loss, dh, dhb, d_norm_final = _forward(x, tgt, layers, norm_final)
    grads = [None] * DEPTH
    for l in reversed(range(DEPTH)):
        dh1, dh1b, g_ffn = _layer_bwd_ffn(dh, dhb, layers[l], saved[l])
        dh, dhb, g_mix = _layer_bwd_mixer(dh1, dh1b, layers[l], saved[l])
        grads[l] = {**g_ffn, **g_mix}
    return loss, dh, grads, d_norm_final


def _place():
    x, y, c = lax.axis_index("x"), lax.axis_index("y"), lax.axis_index("c")
    return x, y, c, [(1 - x, y), (x, 1 - y), (1 - x, 1 - y)]


def _remote(src, dst, send_sem, recv_sem, to):
    return pltpu.make_async_remote_copy(src_ref=src, dst_ref=dst, send_sem=send_sem, recv_sem=recv_sem,
                                        device_id=to, device_id_type=MESH)


def _comm_call(name, body, ins, out_shape, n_sems, aliases=None):
    return pl.pallas_call(
        body, name=name, in_specs=[ANY] * len(ins), out_specs=[ANY] * len(out_shape), out_shape=out_shape,
        scratch_shapes=[pltpu.SemaphoreType.DMA((n,)) for n in n_sems], input_output_aliases=aliases or {},
        compiler_params=pltpu.CompilerParams(has_side_effects=True),
    )(*ins)


def _half_rows(ref, of_c, dim):
    hr = ref.shape[dim] // 2
    return pl.ds(pl.multiple_of(of_c * hr, BF16_ROWS), hr)


def _gather_plan(whole):
    def plan(srcs, lands):
        x, y, c, others = _place()
        chip = 2 * x + y
        out = []
        for src, land, all_of_it in zip(srcs, lands, whole):
            for ox, oy in others:
                if all_of_it:
                    out.append((src, land.at[chip], (ox, oy, c)))
                else:
                    out.append((src.at[_half_rows(src, c, 0)], land.at[chip, _half_rows(src, c, 0)], (ox, oy, c)))
        return out
    return plan


def _forward_halves(lands):
    n = len(lands)

    def body(*refs):
        outs = refs[n:2 * n]
        send_s, recv_s = refs[2 * n:]
        x, y, c, others = _place()
        sibling = (x, y, 1 - c)
        copies = []
        for a in range(n):
            for k, (ox, oy) in enumerate(others):
                mine = outs[a].at[2 * ox + oy, _half_rows(outs[a], c, 1)]
                copies.append(_remote(mine, mine, send_s.at[3 * a + k], recv_s.at[3 * a + k], sibling))
        for cp in copies:
            cp.start()
        for a in range(n):
            for k, (ox, oy) in enumerate(others):
                landed = outs[a].at[2 * ox + oy, _half_rows(outs[a], 1 - c, 1)]
                _remote(landed, landed, send_s.at[3 * a + k], recv_s.at[3 * a + k], sibling).wait_recv()
        for cp in copies:
            cp.wait_send()

    out_shape = [jax.ShapeDtypeStruct(g.shape, g.dtype) for g in lands]
    return _comm_call("forward_halves", body, lands, out_shape, [3 * n, 3 * n], aliases={a: a for a in range(n)})


def _forward_refs(bufs, incoming):
    x, y, c, others = _place()
    return (x, y, 1 - c), [b.at[2 * ox + oy, _half_rows(b, 1 - c if incoming else c, 1)]
                           for b in bufs for ox, oy in others]


def _forward_start(name, bufs, after):
    n = len(bufs)
    bufs = [pltpu.with_memory_space_constraint(b, pltpu.HBM) for b in bufs]

    def body(*refs):
        send_s, recv_s = refs[n + len(after)], refs[n + len(after) + 1]
        sibling, mine = _forward_refs(refs[:n], incoming=False)
        for i, ref in enumerate(mine):
            _remote(ref, ref, send_s.at[i], recv_s.at[i], sibling).start()
        refs[-1][...] = jnp.zeros_like(refs[-1])

    out = pl.pallas_call(
        body, name=name, in_specs=[HBM_SPEC] * n + [ANY] * len(after),
        out_specs=[SEM_SPEC, SEM_SPEC] + [HBM_SPEC] * n + [pl.BlockSpec(memory_space=pltpu.VMEM)],
        out_shape=[pltpu.SemaphoreType.DMA((3 * n,)), pltpu.SemaphoreType.DMA((3 * n,))]
        + [pltpu.HBM(b.shape, b.dtype) for b in bufs] + [jax.ShapeDtypeStruct((F32_ROWS, LANE), F32)],
        input_output_aliases={i: 2 + i for i in range(n)},
        compiler_params=pltpu.CompilerParams(has_side_effects=DATAFLOW),
    )(*bufs, *after)
    return dict(sems=out[:2], bufs=out[2:2 + n], token=out[-1])


def _forward_wait(name, started, after):
    n = len(started["bufs"])

    def body(*refs):
        send_s, recv_s = refs[n], refs[n + 1]
        sibling, mine = _forward_refs(refs[:n], incoming=False)
        _, theirs = _forward_refs(refs[:n], incoming=True)
        for i, (sent, landed) in enumerate(zip(mine, theirs)):
            _remote(sent, sent, send_s.at[i], recv_s.at[i], sibling).wait_send()
            _remote(landed, landed, send_s.at[i], recv_s.at[i], sibling).wait_recv()

    return pl.pallas_call(
        body, name=name, in_specs=[HBM_SPEC] * n + [SEM_SPEC, SEM_SPEC] + [ANY] * len(after),
        out_specs=[HBM_SPEC] * n, out_shape=[pltpu.HBM(b.shape, b.dtype) for b in started["bufs"]],
        input_output_aliases={i: i for i in range(n)},
        compiler_params=pltpu.CompilerParams(has_side_effects=DATAFLOW),
    )(*started["bufs"], *started["sems"], *after)


HBM_SPEC = pl.BlockSpec(memory_space=pltpu.HBM)
SEM_SPEC = pl.BlockSpec(memory_space=pltpu.SEMAPHORE)
DATAFLOW = pltpu.SideEffectType.DATAFLOW_SIDE_EFFECTING


def _exchange_plan(srcs, lands):
    x, y, c, _ = _place()
    plan = []
    for src, land in zip(srcs, lands):
        hr = src.shape[1] // 2
        plan.append((src.at[:, pl.ds(pl.multiple_of((1 - c) * hr, 8), hr)], land, (x, y, 1 - c)))
    return plan


def _scatter_plan(srcs, lands):
    x, y, c, others = _place()
    return [(src.at[2 * ox + oy], land.at[k], (ox, oy, c))
            for src, land in zip(srcs, lands) for k, (ox, oy) in enumerate(others)]


def _split_start(name, plan, srcs, land_shapes, n_copies, after=()):
    n = len(srcs)
    lands = [pltpu.with_memory_space_constraint(lax.empty(s.shape, s.dtype), pltpu.HBM) for s in land_shapes]
    srcs = [pltpu.with_memory_space_constraint(s, pltpu.HBM) for s in srcs]

    def body(*refs):
        send_s, recv_s = refs[2 * n + len(after)], refs[2 * n + len(after) + 1]
        for i, (src, dst, to) in enumerate(plan(refs[:n], refs[n:2 * n])):
            _remote(src, dst, send_s.at[i], recv_s.at[i], to).start()
        refs[-1][...] = jnp.zeros_like(refs[-1])

    thru = [pltpu.HBM(s.shape, s.dtype) for s in srcs + lands]
    out = pl.pallas_call(
        body, name=name, in_specs=[HBM_SPEC] * (2 * n) + [ANY] * len(after),
        out_specs=[SEM_SPEC, SEM_SPEC] + [HBM_SPEC] * (2 * n) + [pl.BlockSpec(memory_space=pltpu.VMEM)],
        out_shape=[pltpu.SemaphoreType.DMA((n_copies,)), pltpu.SemaphoreType.DMA((n_copies,))] + thru
        + [jax.ShapeDtypeStruct((F32_ROWS, LANE), F32)],
        input_output_aliases={i: 2 + i for i in range(2 * n)},
        compiler_params=pltpu.CompilerParams(has_side_effects=DATAFLOW),
    )(*srcs, *lands, *after)
    return dict(sems=out[:2], srcs=out[2:2 + n], lands=out[2 + n:2 + 2 * n], token=out[-1])


def _split_wait(name, plan, started, after):
    n = len(started["srcs"])
    after = list(after) if isinstance(after, (list, tuple)) else [after]

    def body(*refs):
        send_s, recv_s = refs[2 * n], refs[2 * n + 1]
        for i, (src, dst, to) in enumerate(plan(refs[:n], refs[n:2 * n])):
            cp = _remote(src, dst, send_s.at[i], recv_s.at[i], to)
            cp.wait_send()
            cp.wait_recv()

    arrs = list(started["srcs"]) + list(started["lands"])
    out = pl.pallas_call(
        body, name=name, in_specs=[HBM_SPEC] * (2 * n) + [SEM_SPEC, SEM_SPEC] + [ANY] * len(after),
        out_specs=[HBM_SPEC] * (2 * n), out_shape=[pltpu.HBM(s.shape, s.dtype) for s in arrs],
        input_output_aliases={i: i for i in range(2 * n)},
        compiler_params=pltpu.CompilerParams(has_side_effects=DATAFLOW),
    )(*arrs, *started["sems"], *after)
    return out[:n], out[n:]


def _join_halves(name, rs):
    n = len(rs)

    def body(*refs):
        outs = refs[n:2 * n]
        send_s, recv_s = refs[2 * n:]
        x, y, c, _ = _place()
        sibling = (x, y, 1 - c)

        def half(a, of_c):
            hr = outs[a].shape[1] // 2
            return outs[a].at[:, pl.ds(pl.multiple_of(of_c * hr, 8), hr)]

        copies = [_remote(half(a, c), half(a, c), send_s.at[a], recv_s.at[a], sibling) for a in range(n)]
        for cp in copies:
            cp.start()
        for a in range(n):
            landed = half(a, 1 - c)
            _remote(landed, landed, send_s.at[a], recv_s.at[a], sibling).wait_recv()
        for cp in copies:
            cp.wait_send()

    out_shape = [jax.ShapeDtypeStruct(r.shape, r.dtype) for r in rs]
    return _comm_call(name, body, rs, out_shape, [n, n], aliases={a: a for a in range(n)})


def _allreduce_small(buf, after=()):
    r = buf.shape[0]
    hr = r // 2

    def body(in_ref, *refs):
        out_ref, theirs, by_chip, send_s, recv_s = refs[len(after):]
        x, y, c, others = _place()
        chip = 2 * x + y
        sibling = (x, y, 1 - c)
        mine = pl.ds(pl.multiple_of(c * hr, F32_ROWS), hr)
        swap = _remote(in_ref, theirs, send_s.at[0], recv_s.at[0], sibling)
        swap.start()
        swap.wait()
        by_chip[chip] = in_ref[mine, :] + theirs[mine, :]
        sends = [_remote(by_chip.at[chip], by_chip.at[chip], send_s.at[1 + k], recv_s.at[1 + k], (ox, oy, c))
                 for k, (ox, oy) in enumerate(others)]
        for cp in sends:
            cp.start()
        for k, (ox, oy) in enumerate(others):
            landed = by_chip.at[2 * ox + oy]
            _remote(landed, landed, send_s.at[1 + k], recv_s.at[1 + k], (ox, oy, c)).wait_recv()
        for cp in sends:
            cp.wait_send()
        out_ref[mine, :] = (by_chip[0] + by_chip[1]) + (by_chip[2] + by_chip[3])
        back = _remote(out_ref.at[mine], out_ref.at[mine], send_s.at[NCHIP], recv_s.at[NCHIP], sibling)
        back.start()
        other = out_ref.at[pl.ds(pl.multiple_of((1 - c) * hr, F32_ROWS), hr)]
        _remote(other, other, send_s.at[NCHIP], recv_s.at[NCHIP], sibling).wait_recv()
        back.wait_send()

    vm = pl.BlockSpec(memory_space=pltpu.VMEM)
    return pl.pallas_call(
        body, name="allreduce_small", in_specs=[vm] + [ANY] * len(after), out_specs=vm,
        out_shape=jax.ShapeDtypeStruct((r, LANE), F32),
        scratch_shapes=[pltpu.VMEM((r, LANE), F32), pltpu.VMEM((NCHIP, hr, LANE), F32),
                        pltpu.SemaphoreType.DMA((NCHIP + 1,)), pltpu.SemaphoreType.DMA((NCHIP + 1,))],
        compiler_params=pltpu.CompilerParams(has_side_effects=True, vmem_limit_bytes=VMEM_LIMIT),
    )(buf, *after)


MAX_ROW_TILE = 512
BF16_ROWS = 16


def _row_tile(rows):
    for t in range(min(rows, MAX_ROW_TILE) // BF16_ROWS * BF16_ROWS, 0, -BF16_ROWS):
        if rows % t == 0:
            return t
    raise ValueError(rows)


def _sum_halves(g, theirs, c_arr):
    nch, rows, cols = g.shape
    hr = rows // 2
    tr = _row_tile(hr)

    def body(c_ref, g_ref, t_ref, o_ref, ob_ref):
        s = g_ref[...] + t_ref[...]
        o_ref[...] = s
        ob_ref[...] = s.astype(BF16)

    blk = pl.BlockSpec((None, tr, cols), lambda j, i, c_ref: (j, i, 0))
    return pl.pallas_call(
        body, name="sum_halves",
        grid_spec=pltpu.PrefetchScalarGridSpec(
            num_scalar_prefetch=1, grid=(nch, hr // tr),
            in_specs=[pl.BlockSpec((None, None, tr, cols), lambda j, i, c_ref: (j, c_ref[0], i, 0)), blk],
            out_specs=[blk, blk]),
        out_shape=[jax.ShapeDtypeStruct((nch, hr, cols), F32), jax.ShapeDtypeStruct((nch, hr, cols), BF16)],
        compiler_params=_cparams(("parallel", "parallel")),
    )(c_arr, g.reshape(nch, 2, hr, cols), theirs)


def _sum_halves_w_in(g, theirs, c_arr):
    hr = D // 2
    sh = IN_DIM // NCHIP

    def body(c_ref, g_ref, t_ref, o_ref, ob_ref):
        s = g_ref[...] + t_ref[...]
        for j, dst, src, width in W_IN_PIECES:
            o_ref[j, :, dst:dst + width] = s[:, src:src + width]
            ob_ref[j, :, dst:dst + width] = s[:, src:src + width].astype(BF16)

    out = pl.BlockSpec((NCHIP, WT, sh), lambda i, c_ref: (0, i, 0))
    return pl.pallas_call(
        body, name="sum_halves_w_in",
        grid_spec=pltpu.PrefetchScalarGridSpec(
            num_scalar_prefetch=1, grid=(hr // WT,),
            in_specs=[pl.BlockSpec((None, WT, NP), lambda i, c_ref: (c_ref[0], i, 0)),
                      pl.BlockSpec((None, WT, NP), lambda i, c_ref: (0, i, 0))],
            out_specs=[out, out]),
        out_shape=[jax.ShapeDtypeStruct((NCHIP, hr, sh), F32), jax.ShapeDtypeStruct((NCHIP, hr, sh), BF16)],
        compiler_params=_cparams(("parallel",)),
    )(c_arr, g.reshape(2, hr, NP), theirs)


def _sum_chips(p, q, place, l, into=None, after=()):
    extra = ([into] if into is not None else []) + list(after)
    _, rows, cols = p.shape
    tr = _row_tile(rows)
    steps = rows // tr

    def body(place_ref, p_ref, q0, q1, q2, *rest):
        rest[-1][...] = ((p_ref[...] + q0[...].astype(F32)) + q1[...].astype(F32)) + q2[...].astype(F32)

    qs = lambda k: pl.BlockSpec((None, tr, cols), lambda i, place_ref: (k, i, 0))
    return pl.pallas_call(
        body, name="sum_chips",
        grid_spec=pltpu.PrefetchScalarGridSpec(
            num_scalar_prefetch=1, grid=(steps,),
            in_specs=[pl.BlockSpec((None, tr, cols), lambda i, place_ref: (place_ref[0], i, 0)), qs(0), qs(1), qs(2)]
            + [ANY] * len(extra),
            out_specs=pl.BlockSpec((None, tr, cols), lambda i, place_ref: (l, place_ref[1] * steps + i, 0))),
        out_shape=jax.ShapeDtypeStruct((DEPTH, 2 * rows, cols), F32),
        input_output_aliases={5: 0} if into is not None else {},
        compiler_params=_cparams(("parallel",)),
    )(place, p, q, q, q, *extra)


def _adamw_fn(w, g, m, v):
    nm = ADAM_B1 * m + (1.0 - ADAM_B1) * g
    nv = ADAM_B2 * v + (1.0 - ADAM_B2) * jnp.square(g)
    m_hat = nm / (1.0 - ADAM_B1 ** ADAM_STEP)
    v_hat = nv / (1.0 - ADAM_B2 ** ADAM_STEP)
    return -ADAM_LR * (m_hat / (jnp.sqrt(v_hat) + ADAM_EPS) + ADAM_WD * w), nm, nv


def _adamw(w, g, m, v):
    layers, rows, cols = w.shape
    tr = _row_tile(rows)

    def body(w_ref, g_ref, m_ref, v_ref, d_ref, nm_ref, nv_ref):
        d_ref[...], nm_ref[...], nv_ref[...] = _adamw_fn(w_ref[...], g_ref[...], m_ref[...], v_ref[...])

    blk = pl.BlockSpec((None, tr, cols), lambda l, i: (l, i, 0))
    return pl.pallas_call(
        body, name="adamw", grid=(layers, rows // tr), in_specs=[blk] * 4, out_specs=[blk] * 3,
        out_shape=[jax.ShapeDtypeStruct(w.shape, F32)] * 3, compiler_params=_cparams(("parallel", "parallel")),
    )(w, g, m, v)


def _adamw_small(ws, gs, ms, vs):
    n = len(ws)

    def body(*refs):
        for i in range(n):
            w_ref, g_ref, m_ref, v_ref, d_ref, nm_ref, nv_ref = (refs[k * n + i] for k in range(7))
            d_ref[...], nm_ref[...], nv_ref[...] = _adamw_fn(w_ref[...], g_ref[...], m_ref[...], v_ref[...])

    vm = pl.BlockSpec(memory_space=pltpu.VMEM)
    out = pl.pallas_call(
        body, name="adamw_small", in_specs=[vm] * (4 * n), out_specs=[vm] * (3 * n),
        out_shape=[jax.ShapeDtypeStruct(a.shape, F32) for a in list(ws) * 3],
        compiler_params=pltpu.CompilerParams(vmem_limit_bytes=VMEM_LIMIT),
    )(*ws, *gs, *ms, *vs)
    return out[:n], out[n:2 * n], out[2 * n:]


BIG = ("w_in", "w_out", "w_gate", "w_up", "w_down")
SMALL = ("norm_mix", "a_log", "dt_bias", "o_norm_g", "ln_v_g", "ln_v_b", "w_s", "b_s", "norm_ffn", "norm_final")
ORDER = ("norm_mix", "w_in", "conv_w", "a_log", "dt_bias", "o_norm_g", "ln_v_g", "ln_v_b", "w_s", "b_s", "w_out",
         "norm_ffn", "w_gate", "w_up", "w_down", "norm_final")


F32_ROWS = 8
PACK_ROWS = 128


def _lane_rows(size):
    return -(-size // (F32_ROWS * LANE)) * F32_ROWS


def _pack(arrs):
    parts = [jnp.pad(a.reshape(-1), (0, _lane_rows(a.size) * LANE - a.size)).reshape(-1, LANE) for a in arrs]
    rows = sum(p.shape[0] for p in parts)
    if rows % PACK_ROWS:
        parts.append(jnp.zeros((-rows % PACK_ROWS, LANE), F32))
    return jnp.concatenate(parts, axis=0)


def _unpack(buf, like):
    out, row = [], 0
    for a in like:
        n = _lane_rows(a.size)
        out.append(buf[row:row + n].reshape(-1)[:a.size].reshape(a.shape))
        row += n
    return out


def kernel(x, norm_mix, w_in, conv_w, a_log, dt_bias, o_norm_g, ln_v_g, ln_v_b, w_s, b_s, w_out, norm_ffn, w_gate, w_up, w_down, norm_final, loss_target, m_norm_mix, m_w_in, m_conv_w, m_a_log, m_dt_bias, m_o_norm_g, m_ln_v_g, m_ln_v_b, m_w_s, m_b_s, m_w_out, m_norm_ffn, m_w_gate, m_w_up, m_w_down, m_norm_final, v_norm_mix, v_w_in, v_conv_w, v_a_log, v_dt_bias, v_o_norm_g, v_ln_v_g, v_ln_v_b, v_w_s, v_b_s, v_w_out, v_norm_ffn, v_w_gate, v_w_up, v_w_down, v_norm_final):
    w = dict(norm_mix=norm_mix, w_in=w_in, conv_w=conv_w, a_log=a_log, dt_bias=dt_bias, o_norm_g=o_norm_g,
             ln_v_g=ln_v_g, ln_v_b=ln_v_b, w_s=w_s, b_s=b_s, w_out=w_out, norm_ffn=norm_ffn, w_gate=w_gate, w_up=w_up,
             w_down=w_down, norm_final=norm_final)
    m = dict(norm_mix=m_norm_mix, w_in=m_w_in, conv_w=m_conv_w, a_log=m_a_log, dt_bias=m_dt_bias, o_norm_g=m_o_norm_g,
             ln_v_g=m_ln_v_g, ln_v_b=m_ln_v_b, w_s=m_w_s, b_s=m_b_s, w_out=m_w_out, norm_ffn=m_norm_ffn,
             w_gate=m_w_gate, w_up=m_w_up, w_down=m_w_down, norm_final=m_norm_final)
    v = dict(norm_mix=v_norm_mix, w_in=v_w_in, conv_w=v_conv_w, a_log=v_a_log, dt_bias=v_dt_bias, o_norm_g=v_o_norm_g,
             ln_v_g=v_ln_v_g, ln_v_b=v_ln_v_b, w_s=v_w_s, b_s=v_b_s, w_out=v_w_out, norm_ffn=v_norm_ffn,
             w_gate=v_w_gate, w_up=v_w_up, w_down=v_w_down, norm_final=v_norm_final)
    chip = 2 * lax.axis_index("x") + lax.axis_index("y")
    place = jnp.stack([chip, lax.axis_index("c")]).astype(jnp.int32)
    c_arr = place[1:]

    def kernel_view(n, a):
        return jnp.swapaxes(a, 1, 2) if n in ("w_gate", "w_up") else a

    own = {n: [kernel_view(n, w[n])[l].astype(BF16) for l in range(DEPTH)] for n in BIG}
    by_chip = lambda a: jax.ShapeDtypeStruct((NCHIP,) + a.shape, a.dtype)

    def start(name, srcs, whole, after=()):
        return _split_start(name, _gather_plan(whole), srcs, [by_chip(a) for a in srcs], 3 * len(srcs), after)

    def finish(name, started, whole, after):
        srcs, lands = _split_wait(name, _gather_plan(whole), started, after)
        passed = iter(_forward_halves([g for g, all_of_it in zip(lands, whole) if not all_of_it]))
        lands = [g if all_of_it else next(passed) for g, all_of_it in zip(lands, whole)]
        return srcs, [lax.dynamic_update_index_in_dim(g, o, chip, 0) for g, o in zip(lands, srcs)]

    ffn = BIG[1:]
    first = start("gather_first_start", [own["w_in"][0], conv_w], [False, True])
    early = start("gather_early_start", [own[n][0] for n in ffn], [False] * len(ffn), [first["token"]])
    mid = start("gather_mid_start", [own["w_in"][1]], [False], [early["token"]])
    later = start("gather_later_start", [own[n][1] for n in ffn], [False] * len(ffn), [mid["token"]])
    hn = _rmsnorm("rms_mix", x[0], norm_mix[0][None])
    (own_w_in, _), (w_in_by_chip, conv_by_chip) = finish("gather_first_wait", first, [False, True], [later["token"], hn])

    passing = {}

    def pass_on(tag, started, n):
        def at(after):
            srcs, lands = _split_wait(f"gather_{tag}_wait", _gather_plan([False] * n), started, after)
            passing[tag] = srcs, _forward_start(f"forward_{tag}_start", lands, ())
            return [passing[tag][1]["token"]]
        return at

    def passed_on(tag, after):
        srcs, fwd = passing[tag]
        lands = _forward_wait(f"forward_{tag}_wait", fwd, [after])
        return srcs, [lax.dynamic_update_index_in_dim(g, o, chip, 0) for g, o in zip(lands, srcs)]

    def late(tag):
        return lambda after: dict(zip(ffn, passed_on(tag, after)[1]))

    layer0 = _layer_params(0, dict(
        hn=hn, w_in=_assemble_w_in(w_in_by_chip, own_w_in, place), conv_w=conv_by_chip, late=late("early"),
        before_scan=pass_on("early", early, len(ffn)), before_ffn_out=pass_on("mid", mid, 1)), w)

    def layer1(after):
        (own_w_in1,), (w_in1_by_chip,) = passed_on("mid", after)
        return _layer_params(1, dict(w_in=_assemble_w_in(w_in1_by_chip, own_w_in1, place), conv_w=conv_by_chip,
                                     late=late("later"), before_scan=pass_on("later", later, len(ffn))), w)

    saved, layers, loss_lanes, dh, dhb, d_norm_final = _forward(x[0], loss_target[0], [layer0, layer1],
                                                                 norm_final[None])

    sums, arrived = {}, {}

    def exchange_start(tag, l, names, grads, after=()):
        mine = [grads[n] for n in names]
        shapes = [jax.ShapeDtypeStruct((g.shape[0], g.shape[1] // 2, g.shape[2]), F32) for g in mine]
        return tag, l, names, _split_start(f"exchange_{tag}_start", _exchange_plan, mine, shapes, len(mine), after)

    def add_halves(l, names, mine, theirs):
        for n, g, t in zip(names, mine, theirs):
            sums[l, n] = (_sum_halves_w_in if n == "w_in" else _sum_halves)(g, t, c_arr)

    def exchange_wait(handle, after):
        tag, l, names, started = handle
        add_halves(l, names, *_split_wait(f"exchange_{tag}_wait", _exchange_plan, started, after))

    def scatter_start(tag, l, names, after=()):
        partial = [sums[l, n][1] for n in names]
        shapes = [jax.ShapeDtypeStruct((3,) + p.shape[1:], p.dtype) for p in partial]
        return tag, l, names, _split_start(f"scatter_{tag}_start", _scatter_plan, partial, shapes, 3 * len(names), after)

    def scatter_wait(handle, after):
        tag, l, names, started = handle
        for n, q in zip(names, _split_wait(f"scatter_{tag}_wait", _scatter_plan, started, after)[1]):
            arrived[l, n] = q

    last = DEPTH - 1
    swiglu = BIG[2:]
    dh1, dh1b, g_ffn = _layer_bwd_ffn(dh, dhb, layers[last], saved[last])
    dh, dhb, g_mix = _layer_bwd_mixer(dh1, dh1b, layers[last], saved[last])
    gl = [None, _reference_layout({**g_ffn, **g_mix})]
    ex_last = exchange_start("last", last, BIG, gl[last])
    dh1, dh1b, g_ffn = _layer_bwd_ffn(dh, dhb, layers[0], saved[0], after=[ex_last[-1]["token"]])
    exchange_wait(ex_last, dh1)
    sc_last = scatter_start("last", last, BIG)
    ex_ffn = exchange_start("swiglu", 0, swiglu, g_ffn, [sc_last[-1]["token"]])
    sc_ffn = []

    def midway(do):
        exchange_wait(ex_ffn, do)
        sc_ffn.append(scatter_start("swiglu", 0, swiglu))
        return [sc_ffn[0][-1]["token"]]

    ex_rest = []

    def late(grads):
        rest_grads = dict(w_in=grads["w_in"], w_out=grads["w_out"].reshape(NCHIP, D // NCHIP, D))
        ex_rest.append(exchange_start("rest", 0, BIG[:2], rest_grads))
        return [ex_rest[0][-1]["token"]]

    dx, _, g_mix = _layer_bwd_mixer(dh1, dh1b, layers[0], saved[0], after=[ex_ffn[-1]["token"]], midway=midway,
                                    late=late)
    scatter_wait(sc_last, dx)
    scatter_wait(sc_ffn[0], dx)
    gl[0] = _reference_layout({**g_ffn, **g_mix})

    small_g = [jnp.stack([gl[l][n] for l in range(DEPTH)]) for n in SMALL[:-1]] + [d_norm_final[0]]
    conv_g = jnp.stack([gl[l]["conv_w"] for l in range(DEPTH)])
    summed = small_g + [conv_g, loss_lanes[0, :1]]
    total = _allreduce_small(_pack(summed))
    exchange_wait(ex_rest[0], total)
    sc_rest = scatter_start("rest", 0, BIG[:2])

    travelling = [sc_rest[-1]["token"]]
    reduced, g_out, delta, new_m, new_v = {}, {}, {}, {}, {}

    def adamw_large(names, joined):
        for n, g in zip(names, joined):
            res = _adamw(kernel_view(n, w[n]), g, kernel_view(n, m[n]), kernel_view(n, v[n]))
            g_out[n], delta[n], new_m[n], new_v[n] = (kernel_view(n, a) for a in (g,) + tuple(res))

    for n in BIG:
        for l in (range(DEPTH) if n in swiglu else [last]):
            reduced[n] = _sum_chips(sums[l, n][0], arrived[l, n], place, l, into=reduced.get(n), after=travelling)
    adamw_large(swiglu, _join_halves("join_swiglu", [reduced[n] for n in swiglu]))
    scatter_wait(sc_rest, [new_v[n] for n in swiglu] + [reduced[n] for n in BIG[:2]])
    for n in BIG[:2]:
        reduced[n] = _sum_chips(sums[0, n][0], arrived[0, n], place, 0, into=reduced[n])
    adamw_large(BIG[:2], _join_halves("join_rest", [reduced[n] for n in BIG[:2]]))
    *small_r, conv_r, loss = _unpack(total, summed)
    g_out.update(zip(SMALL, small_r))
    g_out["conv_w"] = lax.dynamic_slice_in_dim(conv_r, chip * conv_w.shape[2], conv_w.shape[2], axis=2)

    rest = SMALL + ("conv_w",)
    rows_of = lambda a: a.reshape(1, -1) if a.ndim == 1 else a
    results = _adamw_small(*[[rows_of(src[n]) for n in rest] for src in (w, g_out, m, v)])
    for dst, arrs in zip((delta, new_m, new_v), results):
        dst.update({n: a.reshape(w[n].shape) for n, a in zip(rest, arrs)})

    return (loss[0], dx[None], *[g_out[n] for n in ORDER], *[delta[n] for n in ORDER], *[new_m[n] for n in ORDER],
            *[new_v[n] for n in ORDER])
```

```python
import functools

import jax
import jax.numpy as jnp
from jax import lax
from jax.experimental import pallas as pl
from jax.experimental.pallas import tpu as pltpu

F32 = jnp.float32
BF16 = jnp.bfloat16
MESH = pl.DeviceIdType.MESH
ANY = pl.BlockSpec(memory_space=pl.ANY)
HIGHEST = lax.Precision.HIGHEST

T = 2048
D = 1024
DEPTH = 2
NCHIP = 4
HEADS = 4
HD = 128
HW = HEADS * HD
CH = 64
GCH = 128
IN_DIM = 3080
NP = 3200
BA_OFF = 3072
FF_SH = 704
EPS = 1e-6
LANE = 128
VMEM_LIMIT = 56 * 1024 * 1024

ADAM_LR = 0.001
ADAM_B1 = 0.9
ADAM_B2 = 0.999
ADAM_EPS = 1e-08
ADAM_WD = 0.01
ADAM_STEP = 10


def _cparams(sem=None):
    return pltpu.CompilerParams(dimension_semantics=sem, vmem_limit_bytes=VMEM_LIMIT)


_DIMS = {"nn": (((1,), (0,)), ((), ())), "nt": (((1,), (1,)), ((), ())), "tn": (((0,), (0,)), ((), ()))}


def _mm(name, mode, a, bs, *, tm, tn, tk, out_dtypes=(F32,), reduce_g=False, resid=None, extras=(), epilogue=None,
        b_spec=None, n_n=None, after=(), fold_g=False, sub_m=1):
    assert sub_m == 1 or (mode != "tn" and tm % (8 * sub_m) == 0), (name, sub_m)
    nb = len(bs)
    ga = a.shape[0]
    gbs = [1 if b_spec is not None else b.shape[0] for b in bs]
    g_n = max([ga] + gbs)
    if mode == "tn":
        k_n, m_n = a.shape[1:]
    else:
        m_n, k_n = a.shape[1:]
    if n_n is None:
        n_n = bs[0].shape[1] if mode == "nt" else bs[0].shape[2]
    assert m_n % tm == 0 and n_n % tn == 0 and k_n % tk == 0, (name, m_n, n_n, k_n)
    mi, nj, kk = m_n // tm, n_n // tn, k_n // tk
    lead = g_n if fold_g else None
    if reduce_g:
        g_steps = 1 if fold_g else g_n
        grid = (mi, nj, g_steps, kk)
        ids = lambda i, j, g, k: (g, i, j, k)
        n_red = g_steps * kk
        red_idx = lambda: pl.program_id(2) * kk + pl.program_id(3)
        sem = ("parallel", "parallel", "arbitrary", "arbitrary")
    else:
        grid = (g_n, mi, nj, kk)
        ids = lambda g, i, j, k: (g, i, j, k)
        n_red = kk
        red_idx = lambda: pl.program_id(3)
        sem = ("parallel", "parallel", "parallel", "arbitrary")

    def pick(gsz, g):
        return g if gsz > 1 else 0

    def a_map(*p):
        g, i, j, k = ids(*p)
        return (pick(ga, g), k, i) if mode == "tn" else (pick(ga, g), i, k)

    def b_map(gsz):
        def f(*p):
            g, i, j, k = ids(*p)
            if b_spec is not None:
                return b_spec[1](g, i, j, k)
            return (pick(gsz, g), j, k) if mode == "nt" else (pick(gsz, g), k, j)
        return f

    def o_map(gsz):
        def f(*p):
            g, i, j, k = ids(*p)
            return (0 if reduce_g else pick(gsz, g), i, j)
        return f

    a_spec = pl.BlockSpec((lead, tk, tm) if mode == "tn" else (lead, tm, tk), a_map)
    b_block = b_spec[0] if b_spec is not None else ((lead, tn, tk) if mode == "nt" else (lead, tk, tn))
    b_specs = [pl.BlockSpec(b_block, b_map(gs)) for gs in gbs]
    x_specs = [pl.BlockSpec((None, tm, tn), o_map(e.shape[0])) for e in extras]
    r_specs = [pl.BlockSpec((None, tm, tn), o_map(resid.shape[0]))] if resid is not None else []
    g_out = 1 if reduce_g else g_n
    out_shape = [jax.ShapeDtypeStruct((g_out, m_n, n_n), dt) for dt in out_dtypes]
    out_specs = [pl.BlockSpec((None, tm, tn), o_map(g_out)) for _ in out_dtypes]
    nx, nr, no = len(extras), len(r_specs), len(out_dtypes)
    n_in = 1 + nb + nx + nr + len(after)
    dims = _DIMS[mode]

    def body(*refs):
        a_ref = refs[0]
        b_refs = refs[1:1 + nb]
        x_refs = refs[1 + nb:1 + nb + nx]
        r_refs = refs[1 + nb + nx:1 + nb + nx + nr]
        o_refs = refs[n_in:n_in + no]
        acc_refs = refs[n_in + no:]
        def dots(rows):
            if fold_g:
                return [sum(lax.dot_general(a_ref[g, rows, :], b_ref[g], dims, preferred_element_type=F32)
                            for g in range(g_n)) for b_ref in b_refs]
            av = a_ref[...] if mode == "tn" else a_ref[rows, :]
            return [lax.dot_general(av, b_ref[...], dims, preferred_element_type=F32) for b_ref in b_refs]

        def finish(accs, rows=slice(None)):
            if r_refs:
                accs[0] = accs[0] + r_refs[0][rows, :]
            outs = epilogue(accs, [x[rows, :] for x in x_refs]) if epilogue is not None else accs
            for o_ref, o in zip(o_refs, outs):
                o_ref[rows, :] = o.astype(o_ref.dtype)

        if n_red == 1:
            slabs = [slice(s * (tm // sub_m), (s + 1) * (tm // sub_m)) for s in range(sub_m)]
            ahead = dots(slabs[0])
            for s, rows in enumerate(slabs):
                now, ahead = ahead, (dots(slabs[s + 1]) if s + 1 < sub_m else None)
                finish(now, rows)
            return
        products = dots(slice(None))
        r = red_idx()
        for p, acc in zip(products, acc_refs):
            @pl.when(r == 0)
            def _():
                acc[...] = p

            @pl.when((r > 0) & (r < n_red - 1))
            def _():
                acc[...] += p

        @pl.when(r == n_red - 1)
        def _():
            finish([acc[...] + p for p, acc in zip(products, acc_refs)])

    return pl.pallas_call(
        body, name=name, grid=grid,
        in_specs=[a_spec] + b_specs + x_specs + r_specs + [ANY] * len(after),
        out_specs=out_specs, out_shape=out_shape,
        scratch_shapes=[pltpu.VMEM((tm, tn), F32) for _ in range(nb if n_red > 1 else 0)],
        compiler_params=_cparams(sem),
    )(a, *bs, *extras, *([resid] if resid is not None else []), *after)


def _sigmoid(x):
    return 1.0 / (1.0 + jnp.exp(-x))


def _silu(x):
    return x * _sigmoid(x)


def _gelu(x):
    return 0.5 * x * (1.0 + jnp.tanh(0.7978845608028654 * (x + 0.044715 * (x * x * x))))


def _rms_fn(h, gain):
    return h * lax.rsqrt(jnp.mean(h * h, axis=-1, keepdims=True) + EPS) * gain


def _shift_impl(x, s):
    n = x.shape[0]
    rolled = pltpu.roll(x, s % n, 0)
    row = lax.broadcasted_iota(jnp.int32, x.shape, 0)
    return jnp.where((row >= s) & (row < n + s), rolled, 0.0)


@functools.partial(jax.custom_vjp, nondiff_argnums=(1,))
def _shift(x, s):
    return _shift_impl(x, s)


def _shift_fwd(x, s):
    return _shift_impl(x, s), None


def _shift_bwd(s, _, g):
    return (_shift_impl(g, -s),)


_shift.defvjp(_shift_fwd, _shift_bwd)


def _prep_fn(x, w, qk_scale, is_v):
    y = x * w[3:4, :]
    for i in range(3):
        y = y + _shift(x, 3 - i) * w[i:i + 1, :]
    y = _silu(y)
    nrm = lax.rsqrt(jnp.sum(y * y, axis=-1, keepdims=True) + EPS) * qk_scale
    return y * jnp.where(is_v, 1.0, nrm)


def _softplus(x):
    return jnp.maximum(x, 0.0) + jnp.log(1.0 + jnp.exp(-jnp.abs(x)))


def _gates_fn(ba, a_log, dt_bias):
    lane = lax.broadcasted_iota(jnp.int32, ba.shape, 1)
    beta = _sigmoid(ba)
    g = -jnp.exp(a_log) * _softplus(ba + dt_bias)
    return jnp.where(lane < HEADS, beta, g)


def _dot16(a, b, dims=_DIMS["nn"]):
    return lax.dot_general(a.astype(BF16), b.astype(BF16), dims, preferred_element_type=F32)


def _dot32(a, b):
    return jnp.dot(a, b, preferred_element_type=F32, precision=HIGHEST)


def _dot3(a, b, dims=_DIMS["nn"]):
    return lax.dot_general(a, b, dims, preferred_element_type=F32, precision=lax.Precision.HIGH)


def _tri_inverses(mats, tick=lambda: None):
    row = lax.broadcasted_iota(jnp.int32, (CH, CH), 0)
    col = lax.broadcasted_iota(jnp.int32, (CH, CH), 1)
    eye = (row == col).astype(F32)
    ts = [eye - a for a in mats]
    ps = list(mats)
    for _ in range(5):
        ps = [_dot3(p, p) for p in ps]
        tick()
        ts = [t + _dot3(t, p) for t, p in zip(ts, ps)]
        tick()
    return ts


@jax.custom_vjp
def _tri_solves(mats, rhs):
    return [_dot3(t, b) for t, b in zip(_tri_inverses(mats), rhs)]


def _tri_solves_fwd(mats, rhs):
    ts = _tri_inverses(mats)
    xs = [_dot3(t, b) for t, b in zip(ts, rhs)]
    return xs, (ts, xs)


def _tri_solves_bwd(res, dxs):
    ts, xs = res
    dbs = [_dot3(t, dx, _DIMS["tn"]) for t, dx in zip(ts, dxs)]
    return [-_dot3(db, x, _DIMS["nt"]) for db, x in zip(dbs, xs)], dbs


_tri_solves.defvjp(_tri_solves_fwd, _tri_solves_bwd)


def _chunk_prep_fn(xs, bgs, tick=None):
    step = tick or (lambda: None)
    row = lax.broadcasted_iota(jnp.int32, (CH, CH), 0)
    col = lax.broadcasted_iota(jnp.int32, (CH, CH), 1)
    incl = row >= col
    strict = row > col
    lmat = incl.astype(F32)
    n = len(xs)
    items = [(i, h) for i in range(n) for h in range(HEADS)]
    part = lambda i, h, c: xs[i][:, c * HW + h * HD:c * HW + (h + 1) * HD]
    q = [part(i, h, 0) for i, h in items]
    k = [part(i, h, 1) for i, h in items]
    v = [part(i, h, 2) for i, h in items]
    beta = [bgs[i][:, h:h + 1] for i, h in items]
    gc_all = [_dot32(lmat, bg) for bg in bgs]
    step()
    gc = [gc_all[i][:, HEADS + h:HEADS + h + 1] for i, h in items]
    gmat = [jnp.where(strict, jnp.broadcast_to(bgs[i][:, HEADS + h:HEADS + h + 1], (CH, CH)), 0.0) for i, h in items]
    diff = [_dot3(lmat, m) for m in gmat]
    step()
    decay = [jnp.where(incl, jnp.exp(jnp.where(incl, d, 0.0)), 0.0) for d in diff]
    k_beta = [kk * b for kk, b in zip(k, beta)]
    kk_t = [_dot16(kb, kk, _DIMS["nt"]) for kb, kk in zip(k_beta, k)]
    step()
    qk_t = [_dot16(qq, kk, _DIMS["nt"]) for qq, kk in zip(q, k)]
    step()
    a = [jnp.where(strict, m * d, 0.0) for m, d in zip(kk_t, decay)]
    eg = [jnp.exp(g) for g in gc]
    rhs = [jnp.concatenate([vv * b, kb * e], axis=-1) for vv, b, kb, e in zip(v, beta, k_beta, eg)]
    if tick is None:
        uw = _tri_solves(a, rhs)
    else:
        uw = [_dot3(t, b) for t, b in zip(_tri_inverses(a, tick), rhs)]
    qk = [m * d for m, d in zip(qk_t, decay)]
    g_last = [g[CH - 1:CH, :] for g in gc]
    qe = [qq * e for qq, e in zip(q, eg)]
    kd = [kk * jnp.exp(gl - g) for kk, gl, g in zip(k, g_last, gc)]
    egl = [jnp.broadcast_to(jnp.exp(gl), (1, HD)) for gl in g_last]
    out = []
    for i in range(n):
        mine = slice(i * HEADS, (i + 1) * HEADS)
        cat = lambda vals: jnp.concatenate(vals[mine], axis=-1)
        out.append((cat([x[:, :HD] for x in uw]), cat([x[:, HD:] for x in uw]), cat(qe), cat(kd),
                    jnp.concatenate([m[None] for m in qk[mine]], axis=0), cat(egl)))
    return out


def _state_levels(chunks, s, outs, befores, final):
    for u, w, qe, kd, qk, egl in chunks:
        befores.append(s)
        ws = [_dot16(a, b) for a, b in zip(w, s)]
        qs = [_dot16(a, b) for a, b in zip(qe, s)]
        yield
        v_new = [a - b for a, b in zip(u, ws)]
        outs.append([a + _dot16(b, c) for a, b, c in zip(qs, qk, v_new)])
        s = [a * e + _dot16(b, c, _DIMS["tn"]) for a, e, b, c in zip(s, egl, kd, v_new)]
        yield
    final.append(s)


def _chunk_state_fn(u, w, qe, kd, qk, egl, s):
    ws = [_dot16(a, b) for a, b in zip(w, s)]
    qs = [_dot16(a, b) for a, b in zip(qe, s)]
    v_new = [a - b for a, b in zip(u, ws)]
    o = [a + _dot16(b, c) for a, b, c in zip(qs, qk, v_new)]
    s_new = [a * e + _dot16(b, c, _DIMS["tn"]) for a, e, b, c in zip(s, egl, kd, v_new)]
    return o, s_new


def _mix_fn(o, z, ur, vr, ong, lng, lnb, ws, bst):
    row = lax.broadcasted_iota(jnp.int32, (GCH, GCH), 0)
    col = lax.broadcasted_iota(jnp.int32, (GCH, GCH), 1)
    causal = row >= col
    ug = _gelu(ur)
    vg = _gelu(vr)
    sls = [slice(h * HD, (h + 1) * HD) for h in range(HEADS)]
    oh = [o[:, sl] for sl in sls]
    oh = [x * lax.rsqrt(jnp.mean(x * x, axis=-1, keepdims=True) + EPS) for x in oh]
    outs_dn = [x * ong * _silu(z[:, sl]) for x, sl in zip(oh, sls)]
    vh = [vg[:, sl] for sl in sls]
    mu = [jnp.mean(x, axis=-1, keepdims=True) for x in vh]
    var = [jnp.mean(jnp.square(x - m), axis=-1, keepdims=True) for x, m in zip(vh, mu)]
    vn = [(x - m) * lax.rsqrt(s + EPS) * lng[:, sl] + lnb[:, sl] for x, m, s, sl in zip(vh, mu, var, sls)]
    mixed = [_dot16(jnp.where(causal, ws[h], 0.0), vn[h]) for h in range(HEADS)]
    outs_gm = [ug[:, sl] * (mixed[h] + bst[:, h:h + 1]) for h, sl in enumerate(sls)]
    return jnp.concatenate(outs_dn + outs_gm, axis=-1)


def _loss_fn(h, gain, tgt):
    y = _rms_fn(h, gain)
    return 0.5 * jnp.sum(jnp.mean(jnp.square(y - tgt), axis=-1))


RT = 512


def _rows(n=D):
    return pl.BlockSpec((RT, n), lambda i: (i, 0))


def _whole(shape):
    nd = len(shape)
    return pl.BlockSpec(shape, lambda i: (0,) * nd)


def _rmsnorm(name, h, gain):
    def body(h_ref, g_ref, o_ref):
        o_ref[...] = _rms_fn(h_ref[...], g_ref[...]).astype(BF16)

    return pl.pallas_call(
        body, name=name, grid=(T // RT,), in_specs=[_rows(), _whole((1, D))], out_specs=_rows(),
        out_shape=jax.ShapeDtypeStruct((T, D), BF16), compiler_params=_cparams(("parallel",)),
    )(h, gain)


def _rmsnorm_bwd(name, dhn, h, gain, resid):
    def body(dhn_ref, h_ref, g_ref, r_ref, dh_ref, dh16_ref, dg_ref):
        _, vjp = jax.vjp(_rms_fn, h_ref[...], g_ref[...])
        dh, dg = vjp(dhn_ref[...])
        dh = r_ref[...] + dh
        dh_ref[...] = dh
        dh16_ref[...] = dh.astype(BF16)

        @pl.when(pl.program_id(0) == 0)
        def _():
            dg_ref[...] = dg

        @pl.when(pl.program_id(0) > 0)
        def _():
            dg_ref[...] += dg

    return pl.pallas_call(
        body, name=name, grid=(T // RT,), in_specs=[_rows(), _rows(), _whole((1, D)), _rows()],
        out_specs=[_rows(), _rows(), _whole((1, D))],
        out_shape=[jax.ShapeDtypeStruct((T, D), F32), jax.ShapeDtypeStruct((T, D), BF16),
                   jax.ShapeDtypeStruct((1, D), F32)],
        compiler_params=_cparams(("arbitrary",)),
    )(dhn, h, gain, resid)


def _loss_head(h, gain, tgt):
    def body(h_ref, g_ref, t_ref, l_ref, dh_ref, dh16_ref, dg_ref):
        loss, vjp = jax.vjp(lambda hh, gg: _loss_fn(hh, gg, t_ref[...]), h_ref[...], g_ref[...])
        dh, dg = vjp(jnp.ones((), F32))
        dh_ref[...] = dh
        dh16_ref[...] = dh.astype(BF16)
        lv = jnp.full((1, LANE), loss, F32)

        @pl.when(pl.program_id(0) == 0)
        def _():
            dg_ref[...] = dg
            l_ref[...] = lv

        @pl.when(pl.program_id(0) > 0)
        def _():
            dg_ref[...] += dg
            l_ref[...] += lv

    return pl.pallas_call(
        body, name="loss_head", grid=(T // RT,), in_specs=[_rows(), _whole((1, D)), _rows()],
        out_specs=[_whole((1, LANE)), _rows(), _rows(), _whole((1, D))],
        out_shape=[jax.ShapeDtypeStruct((1, LANE), F32), jax.ShapeDtypeStruct((T, D), F32),
                   jax.ShapeDtypeStruct((T, D), BF16), jax.ShapeDtypeStruct((1, D), F32)],
        compiler_params=_cparams(("arbitrary",)),
    )(h, gain, tgt)


def _prep_flags():
    j = pl.program_id(0)
    qk_scale = jnp.where(j < HEADS, HD ** -0.5, 1.0).astype(F32)
    return qk_scale, j >= 2 * HEADS


def _prep(proj, conv_w):
    def body(x_ref, w_ref, o_ref):
        qk_scale, is_v = _prep_flags()
        o_ref[...] = _prep_fn(x_ref[...], w_ref[...], qk_scale, is_v)

    col = lambda j: (0, j)
    return pl.pallas_call(
        body, name="gdn_prep", grid=(3 * HEADS,),
        in_specs=[pl.BlockSpec((T, HD), col), pl.BlockSpec((4, HD), col)], out_specs=pl.BlockSpec((T, HD), col),
        out_shape=jax.ShapeDtypeStruct((T, 3 * HW), F32), compiler_params=_cparams(("parallel",)),
    )(proj, conv_w)


def _prep_bwd(proj, conv_w, dqkv, dproj):
    def body(x_ref, w_ref, d_ref, _, dx_ref, dw_ref):
        qk_scale, is_v = _prep_flags()
        _, vjp = jax.vjp(lambda x, w: _prep_fn(x, w, qk_scale, is_v), x_ref[...], w_ref[...])
        dx, dw = vjp(d_ref[...])
        dx_ref[...] = dx.astype(BF16)
        dw_ref[...] = dw

    col = lambda j: (0, j)
    return pl.pallas_call(
        body, name="gdn_prep_bwd", grid=(3 * HEADS,),
        in_specs=[pl.BlockSpec((T, HD), col), pl.BlockSpec((4, HD), col), pl.BlockSpec((T, HD), col), ANY],
        out_specs=[pl.BlockSpec((T, HD), col), pl.BlockSpec((4, HD), col)],
        out_shape=[jax.ShapeDtypeStruct((T, NP), BF16), jax.ShapeDtypeStruct((4, 3 * HW), F32)],
        input_output_aliases={3: 0}, compiler_params=_cparams(("parallel",)),
    )(proj, conv_w, dqkv, dproj)


BA_BLK = BA_OFF // LANE


def _gates(proj, a_log, dt_bias):
    def body(x_ref, a_ref, d_ref, o_ref):
        o_ref[...] = _gates_fn(x_ref[...], a_ref[...], d_ref[...])

    return pl.pallas_call(
        body, name="gdn_gates", grid=(1,),
        in_specs=[pl.BlockSpec((T, LANE), lambda i: (0, BA_BLK)), _whole((1, LANE)), _whole((1, LANE))],
        out_specs=_whole((T, LANE)),
        out_shape=jax.ShapeDtypeStruct((T, LANE), F32), compiler_params=_cparams(("arbitrary",)),
    )(proj, a_log, dt_bias)


def _gates_bwd(proj, a_log, dt_bias, dbg, dproj):
    def body(x_ref, a_ref, d_ref, dbg_ref, _, dx_ref, da_ref, dd_ref):
        _, vjp = jax.vjp(_gates_fn, x_ref[...], a_ref[...], d_ref[...])
        dx, da_ref[...], dd_ref[...] = vjp(dbg_ref[...])
        dx_ref[...] = dx.astype(BF16)

    ba = pl.BlockSpec((T, LANE), lambda i: (0, BA_BLK))
    return pl.pallas_call(
        body, name="gdn_gates_bwd", grid=(1,),
        in_specs=[ba, _whole((1, LANE)), _whole((1, LANE)), _whole((T, LANE)), ANY],
        out_specs=[ba, _whole((1, LANE)), _whole((1, LANE))],
        out_shape=[jax.ShapeDtypeStruct((T, NP), BF16), jax.ShapeDtypeStruct((1, LANE), F32),
                   jax.ShapeDtypeStruct((1, LANE), F32)],
        input_output_aliases={4: 0}, compiler_params=_cparams(("arbitrary",)),
    )(proj, a_log, dt_bias, dbg, dproj)


NCK = T // CH
CPS = 4


def _chunk_prep_specs(rev=False):
    at = (lambda n: NCK - 1 - n) if rev else (lambda n: n)
    wide = pl.BlockSpec((CH, HW), lambda n: (at(n), 0))
    return [wide, wide, wide, wide, pl.BlockSpec((HEADS, CH, CH), lambda n: (0, at(n), 0)),
            pl.BlockSpec((None, 1, HW), lambda n: (at(n), 0, 0))]


def _chunk_prep_shapes(dtypes):
    shp = [(T, HW), (T, HW), (T, HW), (T, HW), (HEADS, T, CH), (NCK, 1, HW)]
    return [jax.ShapeDtypeStruct(s, dt) for s, dt in zip(shp, dtypes)]


NGROUP = NCK // CPS
PREP_DTYPES = (F32, BF16, BF16, BF16, BF16, F32)


def _delta_rule(qkv, bg):
    def body(x_ref, bg_ref, *refs):
        prep_out, (o_ref, sh_ref), held, s_ref = refs[:6], refs[6:8], refs[8:14], refs[14]
        i = pl.program_id(0)

        @pl.when(i == 0)
        def _():
            for r in held + (s_ref,):
                r[...] = jnp.zeros_like(r)

        rows = [slice(ci * CH, (ci + 1) * CH) for ci in range(CPS)]
        u_h, w_h, qe_h, kd_h, qk_h, egl_h = held
        chunks = [_head_args((u_h.at[r, :], w_h.at[r, :], qe_h.at[r, :], kd_h.at[r, :], qk_h.at[:, r, :], egl_h.at[ci]))
                  for ci, r in enumerate(rows)]
        start = [jnp.where(i <= 1, 0.0, s_ref[h]) for h in range(HEADS)]
        outs, befores, final = [], [], []
        levels = _state_levels(chunks, start, outs, befores, final)
        res = _chunk_prep_fn([x_ref[r, :] for r in rows], [bg_ref[r, :] for r in rows], tick=lambda: next(levels, None))
        for _ in levels:
            pass
        for ci, (u, w, qe, kd, qk, egl) in enumerate(res):
            for refs_pair, val in zip(zip(prep_out[:4], held[:4]), (u, w, qe, kd)):
                for ref in refs_pair:
                    ref[rows[ci], :] = val.astype(ref.dtype)
            for ref in (prep_out[4], qk_h):
                ref[:, rows[ci], :] = qk.astype(ref.dtype)
            for ref in (prep_out[5], egl_h):
                ref[ci] = egl
        for ci, r in enumerate(rows):
            for h in range(HEADS):
                o_ref[r, h * HD:(h + 1) * HD] = outs[ci][h]
                sh_ref[h, ci] = befores[ci][h]
        for h in range(HEADS):
            s_ref[h] = final[0][h]

    now = lambda n: jnp.minimum(n, NGROUP - 1)
    was = lambda n: jnp.maximum(n - 1, 0)
    wide = lambda at: pl.BlockSpec((CPS * CH, HW), lambda n: (at(n), 0))
    held = [pltpu.VMEM(s, dt) for s, dt in zip(
        [(CPS * CH, HW)] * 4 + [(HEADS, CPS * CH, CH), (CPS, 1, HW)], PREP_DTYPES)]
    out = pl.pallas_call(
        body, name="gdn_delta_rule", grid=(NGROUP + 1,),
        in_specs=[pl.BlockSpec((CPS * CH, 3 * HW), lambda n: (now(n), 0)),
                  pl.BlockSpec((CPS * CH, LANE), lambda n: (now(n), 0))],
        out_specs=[wide(now)] * 4 + [pl.BlockSpec((HEADS, CPS * CH, CH), lambda n: (0, now(n), 0)),
                                     pl.BlockSpec((CPS, 1, HW), lambda n: (now(n), 0, 0)), wide(was),
                                     pl.BlockSpec((HEADS, CPS, HD, HD), lambda n: (0, was(n), 0, 0))],
        out_shape=_chunk_prep_shapes(PREP_DTYPES) + [jax.ShapeDtypeStruct((T, HW), F32),
                                                     jax.ShapeDtypeStruct((HEADS, NCK, HD, HD), F32)],
        scratch_shapes=held + [pltpu.VMEM((HEADS, HD, HD), F32)], compiler_params=_cparams(("arbitrary",)),
    )(qkv, bg)
    return out[:6], out[6], out[7]


def _chunk_prep_bwd(qkv, bg, cots):
    def body(x_ref, bg_ref, du, dw, dqe, dkd, dqk, degl, dx_ref, dbg_ref):
        rows = [slice(ci * CH, (ci + 1) * CH) for ci in range(CPS)]
        _, vjp = jax.vjp(_chunk_prep_fn, [x_ref[r, :] for r in rows], [bg_ref[r, :] for r in rows])
        dxs, dbgs = vjp([(du[r, :], dw[r, :], dqe[r, :], dkd[r, :], dqk[:, r, :], degl[ci])
                         for ci, r in enumerate(rows)])
        for r, dx, dbg in zip(rows, dxs, dbgs):
            dx_ref[r, :] = dx
            dbg_ref[r, :] = dbg

    wide = pl.BlockSpec((CPS * CH, HW), lambda n: (n, 0))
    return pl.pallas_call(
        body, name="gdn_chunk_prep_bwd", grid=(NCK // CPS,),
        in_specs=[pl.BlockSpec((CPS * CH, 3 * HW), lambda n: (n, 0)), pl.BlockSpec((CPS * CH, LANE), lambda n: (n, 0)),
                  wide, wide, wide, wide, pl.BlockSpec((HEADS, CPS * CH, CH), lambda n: (0, n, 0)),
                  pl.BlockSpec((CPS, 1, HW), lambda n: (n, 0, 0))],
        out_specs=[pl.BlockSpec((CPS * CH, 3 * HW), lambda n: (n, 0)), pl.BlockSpec((CPS * CH, LANE), lambda n: (n, 0))],
        out_shape=[jax.ShapeDtypeStruct((T, 3 * HW), F32), jax.ShapeDtypeStruct((T, LANE), F32)],
        compiler_params=_cparams(("parallel",)),
    )(qkv, bg, *cots)


def _head_args(refs):
    u, w, qe, kd, qk, egl = refs
    sls = [slice(h * HD, (h + 1) * HD) for h in range(HEADS)]
    return ([u[:, sl] for sl in sls], [w[:, sl].astype(F32) for sl in sls], [qe[:, sl].astype(F32) for sl in sls],
            [kd[:, sl].astype(F32) for sl in sls], [qk[h].astype(F32) for h in range(HEADS)],
            [egl[:, sl] for sl in sls])


def _chunk_scan_bwd(prep, s_hist, do, after=()):
    n_in = 8 + len(after)

    def body(*refs):
        sh_ref, do_ref = refs[6:8]
        d_refs = refs[n_in:n_in + 6]
        ds_ref = refs[n_in + 6]

        @pl.when(pl.program_id(0) == 0)
        def _():
            ds_ref[...] = jnp.zeros_like(ds_ref)

        sls = [slice(h * HD, (h + 1) * HD) for h in range(HEADS)]
        _, vjp = jax.vjp(_chunk_state_fn, *_head_args(refs[:6]), [sh_ref[h, 0] for h in range(HEADS)])
        du, dw, dqe, dkd, dqk, degl, ds = vjp(([do_ref[:, sl] for sl in sls], [ds_ref[h] for h in range(HEADS)]))
        for h, sl in enumerate(sls):
            for d_ref, val in zip(d_refs[:4], (du, dw, dqe, dkd)):
                d_ref[:, sl] = val[h]
            d_refs[4][h] = dqk[h]
            d_refs[5][:, sl] = degl[h]
            ds_ref[h] = ds[h]

    rev = lambda n: NCK - 1 - n
    return pl.pallas_call(
        body, name="gdn_scan_bwd", grid=(NCK,),
        in_specs=_chunk_prep_specs(rev=True) + [pl.BlockSpec((HEADS, 1, HD, HD), lambda n: (0, rev(n), 0, 0)),
                                                pl.BlockSpec((CH, HW), lambda n: (rev(n), 0))] + [ANY] * len(after),
        out_specs=_chunk_prep_specs(rev=True), out_shape=_chunk_prep_shapes((F32,) * 6),
        scratch_shapes=[pltpu.VMEM((HEADS, HD, HD), F32)], compiler_params=_cparams(("arbitrary",)),
    )(*prep, s_hist, do, *after)


def _mix_specs():
    pc = lambda c: pl.BlockSpec((GCH, HW), lambda i: (i, c))
    return [pl.BlockSpec((GCH, HW), lambda i: (i, 0)), pc(3), pc(4), pc(5), _whole((1, HD)), _whole((1, HW)),
            _whole((1, HW)), _whole((HEADS, GCH, GCH)), _whole((GCH, LANE))]


def _mix(o, proj, ong, lng, lnb, ws, bst, after=()):
    def body(o_ref, z_ref, u_ref, v_ref, ong_ref, lng_ref, lnb_ref, ws_ref, bs_ref, *rest):
        rest[-1][...] = _mix_fn(o_ref[...], z_ref[...], u_ref[...], v_ref[...], ong_ref[...], lng_ref[...],
                                lnb_ref[...], ws_ref[...], bs_ref[...]).astype(BF16)

    return pl.pallas_call(
        body, name="mix", grid=(T // GCH,), in_specs=_mix_specs() + [ANY] * len(after),
        out_specs=pl.BlockSpec((GCH, D), lambda i: (i, 0)), out_shape=jax.ShapeDtypeStruct((T, D), BF16),
        compiler_params=_cparams(("parallel",)),
    )(o, proj, proj, proj, ong, lng, lnb, ws, bst, *after)


def _mix_bwd(o, proj, ong, lng, lnb, ws, bst, dmix):
    def body(o_ref, z_ref, u_ref, v_ref, ong_ref, lng_ref, lnb_ref, ws_ref, bs_ref, dm_ref,
             do_ref, dzuv_ref, dong_ref, dlng_ref, dlnb_ref, dws_ref, dbs_ref):
        _, vjp = jax.vjp(_mix_fn, o_ref[...], z_ref[...], u_ref[...], v_ref[...], ong_ref[...], lng_ref[...],
                         lnb_ref[...], ws_ref[...], bs_ref[...])
        do, dz, du, dv, dong, dlng, dlnb, dws, dbs = vjp(dm_ref[...])
        do_ref[...] = do
        dzuv_ref[:, 0:HW] = dz.astype(BF16)
        dzuv_ref[:, HW:2 * HW] = du.astype(BF16)
        dzuv_ref[:, 2 * HW:3 * HW] = dv.astype(BF16)
        acc = [(dong_ref, dong), (dlng_ref, dlng), (dlnb_ref, dlnb), (dws_ref, dws), (dbs_ref, dbs)]

        @pl.when(pl.program_id(0) == 0)
        def _():
            for r, val in acc:
                r[...] = val

        @pl.when(pl.program_id(0) > 0)
        def _():
            for r, val in acc:
                r[...] += val

    shp = lambda *s: jax.ShapeDtypeStruct(s, F32)
    return pl.pallas_call(
        body, name="mix_bwd", grid=(T // GCH,),
        in_specs=_mix_specs() + [pl.BlockSpec((GCH, D), lambda i: (i, 0))],
        out_specs=[pl.BlockSpec((GCH, HW), lambda i: (i, 0)), pl.BlockSpec((GCH, 3 * HW), lambda i: (i, 1)),
                   _whole((1, HD)), _whole((1, HW)), _whole((1, HW)), _whole((HEADS, GCH, GCH)), _whole((GCH, LANE))],
        out_shape=[shp(T, HW), jax.ShapeDtypeStruct((T, NP), BF16), shp(1, HD), shp(1, HW), shp(1, HW),
                   shp(HEADS, GCH, GCH), shp(GCH, LANE)],
        compiler_params=_cparams(("arbitrary",)),
    )(o, proj, proj, proj, ong, lng, lnb, ws, bst, dmix)


def _swiglu_epilogue(accs, _):
    gate, up = accs
    return [gate, up, _silu(gate) * up]


def _swiglu_bwd_epilogue(accs, extras):
    dact = accs[0]
    gate, up = (e.astype(F32) for e in extras)
    sg = _sigmoid(gate)
    return [dact * up * (sg * (1.0 + gate * (1.0 - sg))), dact * (gate * sg)]


def _layer_fwd(h, p):
    hn = p.pop("hn") if "hn" in p else _rmsnorm("rms_mix", h, p["norm_mix"])
    proj = _mm("in_proj", "nn", hn[None], [p["w_in"][None]], tm=1024, tn=640, tk=D, sub_m=2)[0][0]
    qkv = _prep(proj, p["conv_w"])
    bg = _gates(proj, p["a_log"], p["dt_bias"])
    prep, o, s_hist = _delta_rule(qkv, bg)
    mix = _mix(o, proj, p["o_norm_g"], p["ln_v_g"], p["ln_v_b"], p["w_s"], p["bst"],
               p.pop("before_mix")(o) if "before_mix" in p else ())
    if "late" in p:
        p.update(p.pop("late")(mix))
    h1 = _mm("out_proj", "nn", mix[None], [p["w_out"].reshape(1, D, D)], tm=1024, tn=512, tk=D, resid=h[None],
             sub_m=2)[0][0]
    h2n = _rmsnorm("rms_ffn", h1, p["norm_ffn"])
    gate, up, act = _mm("ffn_in", "nt", h2n[None], [p["w_gate"], p["w_up"]], tm=1024, tn=FF_SH, tk=D,
                        out_dtypes=(BF16, BF16, BF16), epilogue=_swiglu_epilogue, sub_m=4)
    then = p.pop("before_ffn_out")(act) if "before_ffn_out" in p else ()
    h2 = _mm("ffn_out", "nn", act, [p["w_down"]], tm=1024, tn=512, tk=FF_SH, reduce_g=True, fold_g=True,
             resid=h1[None], sub_m=2, after=then)[0][0]
    saved = dict(h=h, hn=hn, proj=proj, qkv=qkv, bg=bg, prep=prep, o=o, s_hist=s_hist, mix=mix, h1=h1, h2n=h2n,
                 gate=gate, up=up, act=act)
    return h2, saved


def _layer_bwd_ffn(dh2, dh2b, p, s, after=()):
    dh2b = dh2b[None]
    dgate, dup = _mm("ffn_out_bwd", "nt", dh2b, [p["w_down"]], tm=1024, tn=FF_SH, tk=D, out_dtypes=(BF16, BF16),
                     extras=(s["gate"], s["up"]), epilogue=_swiglu_bwd_epilogue, after=after, sub_m=4)
    dh2n = _mm("ffn_gate_bwd", "nn", dgate, [p["w_gate"]], tm=1024, tn=512, tk=FF_SH, reduce_g=True, fold_g=True,
               sub_m=2)[0]
    dh2n = _mm("ffn_up_bwd", "nn", dup, [p["w_up"]], tm=1024, tn=512, tk=FF_SH, reduce_g=True, fold_g=True,
               resid=dh2n, sub_m=2)[0][0]
    dh1, dh1b, d_norm_ffn = _rmsnorm_bwd("rms_ffn_bwd", dh2n, s["h1"], p["norm_ffn"], dh2)
    d_w_down = _mm("ffn_wdown_grad", "tn", s["act"], [dh2b], tm=FF_SH, tn=512, tk=T)[0]
    d_w_gate = _mm("ffn_wgate_grad", "tn", dgate, [s["h2n"][None]], tm=FF_SH, tn=512, tk=T)[0]
    d_w_up = _mm("ffn_wup_grad", "tn", dup, [s["h2n"][None]], tm=FF_SH, tn=512, tk=T)[0]
    return dh1, dh1b, dict(norm_ffn=d_norm_ffn, w_gate=d_w_gate, w_up=d_w_up, w_down=d_w_down)


def _layer_bwd_mixer(dh1, dh1b, p, s, after=(), midway=None, late=None):
    dh1b = dh1b[None]
    dmix = _mm("out_proj_bwd", "nt", dh1b, [p["w_out"].reshape(1, D, D)], tm=1024, tn=512, tk=D, after=after,
               sub_m=2)[0][0]
    d_w_out = _mm("out_proj_wgrad", "tn", s["mix"][None], [dh1b], tm=512, tn=512, tk=T)[0][0]
    do, dproj, d_ong, d_lng, d_lnb, d_ws, d_bst = _mix_bwd(
        s["o"], s["proj"], p["o_norm_g"], p["ln_v_g"], p["ln_v_b"], p["w_s"], p["bst"], dmix)
    then = midway(do) if midway is not None else ()
    dqkv, dbg = _chunk_prep_bwd(s["qkv"], s["bg"], _chunk_scan_bwd(s["prep"], s["s_hist"], do, then))
    dproj, d_conv = _prep_bwd(s["proj"], p["conv_w"], dqkv, dproj)
    dproj, d_a_log, d_dt_bias = _gates_bwd(s["proj"], p["a_log"], p["dt_bias"], dbg, dproj)
    dproj = dproj[None]
    d_w_in = _mm("in_proj_wgrad", "tn", s["hn"][None], [dproj], tm=512, tn=640, tk=T)[0]
    last = late(dict(w_in=d_w_in, w_out=d_w_out)) if late is not None else ()
    dhn = _mm("in_proj_bwd", "nt", dproj, [p["w_in"][None]], tm=1024, tn=512, tk=NP, after=last,
              sub_m=2)[0][0]
    dh, dhb, d_norm_mix = _rmsnorm_bwd("rms_mix_bwd", dhn, s["h"], p["norm_mix"], dh1)
    grads = dict(norm_mix=d_norm_mix, w_in=d_w_in, conv_w=d_conv, a_log=d_a_log, dt_bias=d_dt_bias, o_norm_g=d_ong,
                 ln_v_g=d_lng, ln_v_b=d_lnb, w_s=d_ws, bst=d_bst, w_out=d_w_out)
    return dh, dhb, grads


def _lanes(v, off=0):
    return jnp.zeros((1, LANE), F32).at[0, off:off + v.shape[0]].set(v)


def _w_in_pieces():
    regions = [(0, 2048, 0), (2048, 2056, BA_OFF), (2056, IN_DIM, 2048)]
    sh = IN_DIM // NCHIP
    out = []
    for j in range(NCHIP):
        for lo, hi, at in regions:
            a, b = max(lo, j * sh), min(hi, (j + 1) * sh)
            if a < b:
                out.append((j, a - j * sh, at + a - lo, b - a))
    return out


W_IN_PIECES = _w_in_pieces()
WT = 256


def _assemble_w_in(gathered, own, place):
    def body(place_ref, g_ref, own_ref, o_ref):
        o_ref[:, IN_DIM:] = jnp.zeros((WT, NP - IN_DIM), BF16)
        mine = own_ref[...]
        for j, src, dst, width in W_IN_PIECES:
            val = jnp.where(place_ref[0] == j, mine[:, src:src + width], g_ref[j, :, src:src + width])
            o_ref[:, dst:dst + width] = val

    sh = IN_DIM // NCHIP
    return pl.pallas_call(
        body, name="assemble_w_in",
        grid_spec=pltpu.PrefetchScalarGridSpec(
            num_scalar_prefetch=1, grid=(D // WT,),
            in_specs=[pl.BlockSpec((NCHIP, WT, sh), lambda i, place_ref: (0, i, 0)),
                      pl.BlockSpec((WT, sh), lambda i, place_ref: (i, 0))],
            out_specs=pl.BlockSpec((WT, NP), lambda i, place_ref: (i, 0))),
        out_shape=jax.ShapeDtypeStruct((D, NP), BF16), compiler_params=_cparams(("parallel",)),
    )(place, gathered, own)


def _layer_params(l, big, small):
    return dict(
        {k: v for k, v in big.items() if k != "conv_w"},
        conv_w=jnp.concatenate([big["conv_w"][j, l] for j in range(NCHIP)], axis=1),
        norm_mix=small["norm_mix"][l][None], norm_ffn=small["norm_ffn"][l][None],
        a_log=_lanes(small["a_log"][l], HEADS), dt_bias=_lanes(small["dt_bias"][l], HEADS),
        o_norm_g=small["o_norm_g"][l][None], ln_v_g=small["ln_v_g"][l][None], ln_v_b=small["ln_v_b"][l][None],
        w_s=small["w_s"][l],
        bst=jnp.pad(small["b_s"][l].T, ((0, 0), (0, LANE - HEADS))),
    )


def _reference_layout(g):
    return dict(
        w_in=g["w_in"],
        w_out=g["w_out"].reshape(NCHIP, D // NCHIP, D),
        w_gate=g["w_gate"], w_up=g["w_up"], w_down=g["w_down"],
        conv_w=g["conv_w"], norm_mix=g["norm_mix"][0], norm_ffn=g["norm_ffn"][0],
        a_log=g["a_log"][0, HEADS:2 * HEADS], dt_bias=g["dt_bias"][0, HEADS:2 * HEADS],
        o_norm_g=g["o_norm_g"][0], ln_v_g=g["ln_v_g"][0], ln_v_b=g["ln_v_b"][0], w_s=g["w_s"],
        b_s=g["bst"][:, :HEADS].T,
    )


def _forward(x, tgt, layers, norm_final):
    h = x
    saved, params = [], []
    for p in layers:
        p = p(h) if callable(p) else p
        h, s = _layer_fwd(h, p)
        saved.append(s)
        params.append(p)
    return (saved, params) + tuple(_loss_head(h, norm_final, tgt))


def _local_step(x, tgt, layers, norm_final):
    saved, layers, loss, dh, dhb, d_norm_final = _forward(x, tgt, layers, norm_final)
    grads = [None] * DEPTH
    for l in reversed(range(DEPTH)):
        dh1, dh1b, g_ffn = _layer_bwd_ffn(dh, dhb, layers[l], saved[l])
        dh, dhb, g_mix = _layer_bwd_mixer(dh1, dh1b, layers[l], saved[l])
        grads[l] = {**g_ffn, **g_mix}
    return loss, dh, grads, d_norm_final


def _place():
    x, y, c = lax.axis_index("x"), lax.axis_index("y"), lax.axis_index("c")
    return x, y, c, [(1 - x, y), (x, 1 - y), (1 - x, 1 - y)]


def _remote(src, dst, send_sem, recv_sem, to):
    return pltpu.make_async_remote_copy(src_ref=src, dst_ref=dst, send_sem=send_sem, recv_sem=recv_sem,
                                        device_id=to, device_id_type=MESH)


def _comm_call(name, body, ins, out_shape, n_sems, aliases=None):
    return pl.pallas_call(
        body, name=name, in_specs=[ANY] * len(ins), out_specs=[ANY] * len(out_shape), out_shape=out_shape,
        scratch_shapes=[pltpu.SemaphoreType.DMA((n,)) for n in n_sems], input_output_aliases=aliases or {},
        compiler_params=pltpu.CompilerParams(has_side_effects=True),
    )(*ins)


def _half_rows(ref, of_c, dim):
    hr = ref.shape[dim] // 2
    return pl.ds(pl.multiple_of(of_c * hr, BF16_ROWS), hr)


def _gather_plan(whole):
    def plan(srcs, lands):
        x, y, c, others = _place()
        chip = 2 * x + y
        out = []
        for src, land, all_of_it in zip(srcs, lands, whole):
            for ox, oy in others:
                if all_of_it:
                    out.append((src, land.at[chip], (ox, oy, c)))
                else:
                    out.append((src.at[_half_rows(src, c, 0)], land.at[chip, _half_rows(src, c, 0)], (ox, oy, c)))
        return out
    return plan


def _forward_halves(lands):
    n = len(lands)

    def body(*refs):
        outs = refs[n:2 * n]
        send_s, recv_s = refs[2 * n:]
        x, y, c, others = _place()
        sibling = (x, y, 1 - c)
        copies = []
        for a in range(n):
            for k, (ox, oy) in enumerate(others):
                mine = outs[a].at[2 * ox + oy, _half_rows(outs[a], c, 1)]
                copies.append(_remote(mine, mine, send_s.at[3 * a + k], recv_s.at[3 * a + k], sibling))
        for cp in copies:
            cp.start()
        for a in range(n):
            for k, (ox, oy) in enumerate(others):
                landed = outs[a].at[2 * ox + oy, _half_rows(outs[a], 1 - c, 1)]
                _remote(landed, landed, send_s.at[3 * a + k], recv_s.at[3 * a + k], sibling).wait_recv()
        for cp in copies:
            cp.wait_send()

    out_shape = [jax.ShapeDtypeStruct(g.shape, g.dtype) for g in lands]
    return _comm_call("forward_halves", body, lands, out_shape, [3 * n, 3 * n], aliases={a: a for a in range(n)})


def _forward_refs(bufs, incoming):
    x, y, c, others = _place()
    return (x, y, 1 - c), [b.at[2 * ox + oy, _half_rows(b, 1 - c if incoming else c, 1)]
                           for b in bufs for ox, oy in others]


def _forward_start(name, bufs, after):
    n = len(bufs)
    bufs = [pltpu.with_memory_space_constraint(b, pltpu.HBM) for b in bufs]

    def body(*refs):
        send_s, recv_s = refs[n + len(after)], refs[n + len(after) + 1]
        sibling, mine = _forward_refs(refs[:n], incoming=False)
        for i, ref in enumerate(mine):
            _remote(ref, ref, send_s.at[i], recv_s.at[i], sibling).start()
        refs[-1][...] = jnp.zeros_like(refs[-1])

    out = pl.pallas_call(
        body, name=name, in_specs=[HBM_SPEC] * n + [ANY] * len(after),
        out_specs=[SEM_SPEC, SEM_SPEC] + [HBM_SPEC] * n + [pl.BlockSpec(memory_space=pltpu.VMEM)],
        out_shape=[pltpu.SemaphoreType.DMA((3 * n,)), pltpu.SemaphoreType.DMA((3 * n,))]
        + [pltpu.HBM(b.shape, b.dtype) for b in bufs] + [jax.ShapeDtypeStruct((F32_ROWS, LANE), F32)],
        input_output_aliases={i: 2 + i for i in range(n)},
        compiler_params=pltpu.CompilerParams(has_side_effects=DATAFLOW),
    )(*bufs, *after)
    return dict(sems=out[:2], bufs=out[2:2 + n], token=out[-1])


def _forward_wait(name, started, after):
    n = len(started["bufs"])

    def body(*refs):
        send_s, recv_s = refs[n], refs[n + 1]
        sibling, mine = _forward_refs(refs[:n], incoming=False)
        _, theirs = _forward_refs(refs[:n], incoming=True)
        for i, (sent, landed) in enumerate(zip(mine, theirs)):
            _remote(sent, sent, send_s.at[i], recv_s.at[i], sibling).wait_send()
            _remote(landed, landed, send_s.at[i], recv_s.at[i], sibling).wait_recv()

    return pl.pallas_call(
        body, name=name, in_specs=[HBM_SPEC] * n + [SEM_SPEC, SEM_SPEC] + [ANY] * len(after),
        out_specs=[HBM_SPEC] * n, out_shape=[pltpu.HBM(b.shape, b.dtype) for b in started["bufs"]],
        input_output_aliases={i: i for i in range(n)},
        compiler_params=pltpu.CompilerParams(has_side_effects=DATAFLOW),
    )(*started["bufs"], *started["sems"], *after)


HBM_SPEC = pl.BlockSpec(memory_space=pltpu.HBM)
SEM_SPEC = pl.BlockSpec(memory_space=pltpu.SEMAPHORE)
DATAFLOW = pltpu.SideEffectType.DATAFLOW_SIDE_EFFECTING


def _exchange_plan(srcs, lands):
    x, y, c, _ = _place()
    plan = []
    for src, land in zip(srcs, lands):
        hr = src.shape[1] // 2
        plan.append((src.at[:, pl.ds(pl.multiple_of((1 - c) * hr, 8), hr)], land, (x, y, 1 - c)))
    return plan


def _scatter_plan(srcs, lands):
    x, y, c, others = _place()
    return [(src.at[2 * ox + oy], land.at[k], (ox, oy, c))
            for src, land in zip(srcs, lands) for k, (ox, oy) in enumerate(others)]


def _split_start(name, plan, srcs, land_shapes, n_copies, after=()):
    n = len(srcs)
    lands = [pltpu.with_memory_space_constraint(lax.empty(s.shape, s.dtype), pltpu.HBM) for s in land_shapes]
    srcs = [pltpu.with_memory_space_constraint(s, pltpu.HBM) for s in srcs]

    def body(*refs):
        send_s, recv_s = refs[2 * n + len(after)], refs[2 * n + len(after) + 1]
        for i, (src, dst, to) in enumerate(plan(refs[:n], refs[n:2 * n])):
            _remote(src, dst, send_s.at[i], recv_s.at[i], to).start()
        refs[-1][...] = jnp.zeros_like(refs[-1])

    thru = [pltpu.HBM(s.shape, s.dtype) for s in srcs + lands]
    out = pl.pallas_call(
        body, name=name, in_specs=[HBM_SPEC] * (2 * n) + [ANY] * len(after),
        out_specs=[SEM_SPEC, SEM_SPEC] + [HBM_SPEC] * (2 * n) + [pl.BlockSpec(memory_space=pltpu.VMEM)],
        out_shape=[pltpu.SemaphoreType.DMA((n_copies,)), pltpu.SemaphoreType.DMA((n_copies,))] + thru
        + [jax.ShapeDtypeStruct((F32_ROWS, LANE), F32)],
        input_output_aliases={i: 2 + i for i in range(2 * n)},
        compiler_params=pltpu.CompilerParams(has_side_effects=DATAFLOW),
    )(*srcs, *lands, *after)
    return dict(sems=out[:2], srcs=out[2:2 + n], lands=out[2 + n:2 + 2 * n], token=out[-1])


def _split_wait(name, plan, started, after):
    n = len(started["srcs"])
    after = list(after) if isinstance(after, (list, tuple)) else [after]

    def body(*refs):
        send_s, recv_s = refs[2 * n], refs[2 * n + 1]
        for i, (src, dst, to) in enumerate(plan(refs[:n], refs[n:2 * n])):
            cp = _remote(src, dst, send_s.at[i], recv_s.at[i], to)
            cp.wait_send()
            cp.wait_recv()

    arrs = list(started["srcs"]) + list(started["lands"])
    out = pl.pallas_call(
        body, name=name, in_specs=[HBM_SPEC] * (2 * n) + [SEM_SPEC, SEM_SPEC] + [ANY] * len(after),
        out_specs=[HBM_SPEC] * (2 * n), out_shape=[pltpu.HBM(s.shape, s.dtype) for s in arrs],
        input_output_aliases={i: i for i in range(2 * n)},
        compiler_params=pltpu.CompilerParams(has_side_effects=DATAFLOW),
    )(*arrs, *started["sems"], *after)
    return out[:n], out[n:]


def _join_halves(name, rs):
    n = len(rs)

    def body(*refs):
        outs = refs[n:2 * n]
        send_s, recv_s = refs[2 * n:]
        x, y, c, _ = _place()
        sibling = (x, y, 1 - c)

        def half(a, of_c):
            hr = outs[a].shape[1] // 2
            return outs[a].at[:, pl.ds(pl.multiple_of(of_c * hr, 8), hr)]

        copies = [_remote(half(a, c), half(a, c), send_s.at[a], recv_s.at[a], sibling) for a in range(n)]
        for cp in copies:
            cp.start()
        for a in range(n):
            landed = half(a, 1 - c)
            _remote(landed, landed, send_s.at[a], recv_s.at[a], sibling).wait_recv()
        for cp in copies:
            cp.wait_send()

    out_shape = [jax.ShapeDtypeStruct(r.shape, r.dtype) for r in rs]
    return _comm_call(name, body, rs, out_shape, [n, n], aliases={a: a for a in range(n)})


def _allreduce_small(buf, after=()):
    r = buf.shape[0]
    hr = r // 2

    def body(in_ref, *refs):
        out_ref, theirs, by_chip, send_s, recv_s = refs[len(after):]
        x, y, c, others = _place()
        chip = 2 * x + y
        sibling = (x, y, 1 - c)
        mine = pl.ds(pl.multiple_of(c * hr, F32_ROWS), hr)
        swap = _remote(in_ref, theirs, send_s.at[0], recv_s.at[0], sibling)
        swap.start()
        swap.wait()
        by_chip[chip] = in_ref[mine, :] + theirs[mine, :]
        sends = [_remote(by_chip.at[chip], by_chip.at[chip], send_s.at[1 + k], recv_s.at[1 + k], (ox, oy, c))
                 for k, (ox, oy) in enumerate(others)]
        for cp in sends:
            cp.start()
        for k, (ox, oy) in enumerate(others):
            landed = by_chip.at[2 * ox + oy]
            _remote(landed, landed, send_s.at[1 + k], recv_s.at[1 + k], (ox, oy, c)).wait_recv()
        for cp in sends:
            cp.wait_send()
        out_ref[mine, :] = (by_chip[0] + by_chip[1]) + (by_chip[2] + by_chip[3])
        back = _remote(out_ref.at[mine], out_ref.at[mine], send_s.at[NCHIP], recv_s.at[NCHIP], sibling)
        back.start()
        other = out_ref.at[pl.ds(pl.multiple_of((1 - c) * hr, F32_ROWS), hr)]
        _remote(other, other, send_s.at[NCHIP], recv_s.at[NCHIP], sibling).wait_recv()
        back.wait_send()

    vm = pl.BlockSpec(memory_space=pltpu.VMEM)
    return pl.pallas_call(
        body, name="allreduce_small", in_specs=[vm] + [ANY] * len(after), out_specs=vm,
        out_shape=jax.ShapeDtypeStruct((r, LANE), F32),
        scratch_shapes=[pltpu.VMEM((r, LANE), F32), pltpu.VMEM((NCHIP, hr, LANE), F32),
                        pltpu.SemaphoreType.DMA((NCHIP + 1,)), pltpu.SemaphoreType.DMA((NCHIP + 1,))],
        compiler_params=pltpu.CompilerParams(has_side_effects=True, vmem_limit_bytes=VMEM_LIMIT),
    )(buf, *after)


MAX_ROW_TILE = 512
BF16_ROWS = 16


def _row_tile(rows):
    for t in range(min(rows, MAX_ROW_TILE) // BF16_ROWS * BF16_ROWS, 0, -BF16_ROWS):
        if rows % t == 0:
            return t
    raise ValueError(rows)


def _sum_halves(g, theirs, c_arr):
    nch, rows, cols = g.shape
    hr = rows // 2
    tr = _row_tile(hr)

    def body(c_ref, g_ref, t_ref, o_ref, ob_ref):
        s = g_ref[...] + t_ref[...]
        o_ref[...] = s
        ob_ref[...] = s.astype(BF16)

    blk = pl.BlockSpec((None, tr, cols), lambda j, i, c_ref: (j, i, 0))
    return pl.pallas_call(
        body, name="sum_halves",
        grid_spec=pltpu.PrefetchScalarGridSpec(
            num_scalar_prefetch=1, grid=(nch, hr // tr),
            in_specs=[pl.BlockSpec((None, None, tr, cols), lambda j, i, c_ref: (j, c_ref[0], i, 0)), blk],
            out_specs=[blk, blk]),
        out_shape=[jax.ShapeDtypeStruct((nch, hr, cols), F32), jax.ShapeDtypeStruct((nch, hr, cols), BF16)],
        compiler_params=_cparams(("parallel", "parallel")),
    )(c_arr, g.reshape(nch, 2, hr, cols), theirs)


def _sum_halves_w_in(g, theirs, c_arr):
    hr = D // 2
    sh = IN_DIM // NCHIP

    def body(c_ref, g_ref, t_ref, o_ref, ob_ref):
        s = g_ref[...] + t_ref[...]
        for j, dst, src, width in W_IN_PIECES:
            o_ref[j, :, dst:dst + width] = s[:, src:src + width]
            ob_ref[j, :, dst:dst + width] = s[:, src:src + width].astype(BF16)

    out = pl.BlockSpec((NCHIP, WT, sh), lambda i, c_ref: (0, i, 0))
    return pl.pallas_call(
        body, name="sum_halves_w_in",
        grid_spec=pltpu.PrefetchScalarGridSpec(
            num_scalar_prefetch=1, grid=(hr // WT,),
            in_specs=[pl.BlockSpec((None, WT, NP), lambda i, c_ref: (c_ref[0], i, 0)),
                      pl.BlockSpec((None, WT, NP), lambda i, c_ref: (0, i, 0))],
            out_specs=[out, out]),
        out_shape=[jax.ShapeDtypeStruct((NCHIP, hr, sh), F32), jax.ShapeDtypeStruct((NCHIP, hr, sh), BF16)],
        compiler_params=_cparams(("parallel",)),
    )(c_arr, g.reshape(2, hr, NP), theirs)


def _sum_chips(p, q, place, l, into=None, after=()):
    extra = ([into] if into is not None else []) + list(after)
    _, rows, cols = p.shape
    tr = _row_tile(rows)
    steps = rows // tr

    def body(place_ref, p_ref, q0, q1, q2, *rest):
        rest[-1][...] = ((p_ref[...] + q0[...].astype(F32)) + q1[...].astype(F32)) + q2[...].astype(F32)

    qs = lambda k: pl.BlockSpec((None, tr, cols), lambda i, place_ref: (k, i, 0))
    return pl.pallas_call(
        body, name="sum_chips",
        grid_spec=pltpu.PrefetchScalarGridSpec(
            num_scalar_prefetch=1, grid=(steps,),
            in_specs=[pl.BlockSpec((None, tr, cols), lambda i, place_ref: (place_ref[0], i, 0)), qs(0), qs(1), qs(2)]
            + [ANY] * len(extra),
            out_specs=pl.BlockSpec((None, tr, cols), lambda i, place_ref: (l, place_ref[1] * steps + i, 0))),
        out_shape=jax.ShapeDtypeStruct((DEPTH, 2 * rows, cols), F32),
        input_output_aliases={5: 0} if into is not None else {},
        compiler_params=_cparams(("parallel",)),
    )(place, p, q, q, q, *extra)


def _adamw_fn(w, g, m, v):
    nm = ADAM_B1 * m + (1.0 - ADAM_B1) * g
    nv = ADAM_B2 * v + (1.0 - ADAM_B2) * jnp.square(g)
    m_hat = nm / (1.0 - ADAM_B1 ** ADAM_STEP)
    v_hat = nv / (1.0 - ADAM_B2 ** ADAM_STEP)
    return -ADAM_LR * (m_hat / (jnp.sqrt(v_hat) + ADAM_EPS) + ADAM_WD * w), nm, nv


def _adamw(w, g, m, v):
    layers, rows, cols = w.shape
    tr = _row_tile(rows)

    def body(w_ref, g_ref, m_ref, v_ref, d_ref, nm_ref, nv_ref):
        d_ref[...], nm_ref[...], nv_ref[...] = _adamw_fn(w_ref[...], g_ref[...], m_ref[...], v_ref[...])

    blk = pl.BlockSpec((None, tr, cols), lambda l, i: (l, i, 0))
    return pl.pallas_call(
        body, name="adamw", grid=(layers, rows // tr), in_specs=[blk] * 4, out_specs=[blk] * 3,
        out_shape=[jax.ShapeDtypeStruct(w.shape, F32)] * 3, compiler_params=_cparams(("parallel", "parallel")),
    )(w, g, m, v)


def _adamw_small(ws, gs, ms, vs):
    n = len(ws)

    def body(*refs):
        for i in range(n):
            w_ref, g_ref, m_ref, v_ref, d_ref, nm_ref, nv_ref = (refs[k * n + i] for k in range(7))
            d_ref[...], nm_ref[...], nv_ref[...] = _adamw_fn(w_ref[...], g_ref[...], m_ref[...], v_ref[...])

    vm = pl.BlockSpec(memory_space=pltpu.VMEM)
    out = pl.pallas_call(
        body, name="adamw_small", in_specs=[vm] * (4 * n), out_specs=[vm] * (3 * n),
        out_shape=[jax.ShapeDtypeStruct(a.shape, F32) for a in list(ws) * 3],
        compiler_params=pltpu.CompilerParams(vmem_limit_bytes=VMEM_LIMIT),
    )(*ws, *gs, *ms, *vs)
    return out[:n], out[n:2 * n], out[2 * n:]


BIG = ("w_in", "w_out", "w_gate", "w_up", "w_down")
SMALL = ("norm_mix", "a_log", "dt_bias", "o_norm_g", "ln_v_g", "ln_v_b", "w_s", "b_s", "norm_ffn", "norm_final")
ORDER = ("norm_mix", "w_in", "conv_w", "a_log", "dt_bias", "o_norm_g", "ln_v_g", "ln_v_b", "w_s", "b_s", "w_out",
         "norm_ffn", "w_gate", "w_up", "w_down", "norm_final")


F32_ROWS = 8
PACK_ROWS = 128


def _lane_rows(size):
    return -(-size // (F32_ROWS * LANE)) * F32_ROWS


def _pack(arrs):
    parts = [jnp.pad(a.reshape(-1), (0, _lane_rows(a.size) * LANE - a.size)).reshape(-1, LANE) for a in arrs]
    rows = sum(p.shape[0] for p in parts)
    if rows % PACK_ROWS:
        parts.append(jnp.zeros((-rows % PACK_ROWS, LANE), F32))
    return jnp.concatenate(parts, axis=0)


def _unpack(buf, like):
    out, row = [], 0
    for a in like:
        n = _lane_rows(a.size)
        out.append(buf[row:row + n].reshape(-1)[:a.size].reshape(a.shape))
        row += n
    return out


def kernel(x, norm_mix, w_in, conv_w, a_log, dt_bias, o_norm_g, ln_v_g, ln_v_b, w_s, b_s, w_out, norm_ffn, w_gate, w_up, w_down, norm_final, loss_target, m_norm_mix, m_w_in, m_conv_w, m_a_log, m_dt_bias, m_o_norm_g, m_ln_v_g, m_ln_v_b, m_w_s, m_b_s, m_w_out, m_norm_ffn, m_w_gate, m_w_up, m_w_down, m_norm_final, v_norm_mix, v_w_in, v_conv_w, v_a_log, v_dt_bias, v_o_norm_g, v_ln_v_g, v_ln_v_b, v_w_s, v_b_s, v_w_out, v_norm_ffn, v_w_gate, v_w_up, v_w_down, v_norm_final):
    w = dict(norm_mix=norm_mix, w_in=w_in, conv_w=conv_w, a_log=a_log, dt_bias=dt_bias, o_norm_g=o_norm_g,
             ln_v_g=ln_v_g, ln_v_b=ln_v_b, w_s=w_s, b_s=b_s, w_out=w_out, norm_ffn=norm_ffn, w_gate=w_gate, w_up=w_up,
             w_down=w_down, norm_final=norm_final)
    m = dict(norm_mix=m_norm_mix, w_in=m_w_in, conv_w=m_conv_w, a_log=m_a_log, dt_bias=m_dt_bias, o_norm_g=m_o_norm_g,
             ln_v_g=m_ln_v_g, ln_v_b=m_ln_v_b, w_s=m_w_s, b_s=m_b_s, w_out=m_w_out, norm_ffn=m_norm_ffn,
             w_gate=m_w_gate, w_up=m_w_up, w_down=m_w_down, norm_final=m_norm_final)
    v = dict(norm_mix=v_norm_mix, w_in=v_w_in, conv_w=v_conv_w, a_log=v_a_log, dt_bias=v_dt_bias, o_norm_g=v_o_norm_g,
             ln_v_g=v_ln_v_g, ln_v_b=v_ln_v_b, w_s=v_w_s, b_s=v_b_s, w_out=v_w_out, norm_ffn=v_norm_ffn,
             w_gate=v_w_gate, w_up=v_w_up, w_down=v_w_down, norm_final=v_norm_final)
    chip = 2 * lax.axis_index("x") + lax.axis_index("y")
    place = jnp.stack([chip, lax.axis_index("c")]).astype(jnp.int32)
    c_arr = place[1:]

    def kernel_view(n, a):
        return jnp.swapaxes(a, 1, 2) if n in ("w_gate", "w_up") else a

    own = {n: [kernel_view(n, w[n])[l].astype(BF16) for l in range(DEPTH)] for n in BIG}
    by_chip = lambda a: jax.ShapeDtypeStruct((NCHIP,) + a.shape, a.dtype)

    def start(name, srcs, whole, after=()):
        return _split_start(name, _gather_plan(whole), srcs, [by_chip(a) for a in srcs], 3 * len(srcs), after)

    def finish(name, started, whole, after):
        srcs, lands = _split_wait(name, _gather_plan(whole), started, after)
        passed = iter(_forward_halves([g for g, all_of_it in zip(lands, whole) if not all_of_it]))
        lands = [g if all_of_it else next(passed) for g, all_of_it in zip(lands, whole)]
        return srcs, [lax.dynamic_update_index_in_dim(g, o, chip, 0) for g, o in zip(lands, srcs)]

    ffn = BIG[1:]
    first = start("gather_first_start", [own["w_in"][0], conv_w], [False, True])
    early = start("gather_early_start", [own[n][0] for n in ffn], [False] * len(ffn), [first["token"]])
    mid = start("gather_mid_start", [own["w_in"][1]], [False], [early["token"]])
    later = start("gather_later_start", [own[n][1] for n in ffn], [False] * len(ffn), [mid["token"]])
    hn = _rmsnorm("rms_mix", x[0], norm_mix[0][None])
    (own_w_in, _), (w_in_by_chip, conv_by_chip) = finish("gather_first_wait", first, [False, True], [later["token"], hn])

    passing = {}

    def pass_on(tag, started, n):
        def at(after):
            srcs, lands = _split_wait(f"gather_{tag}_wait", _gather_plan([False] * n), started, after)
            passing[tag] = srcs, _forward_start(f"forward_{tag}_start", lands, ())
            return [passing[tag][1]["token"]]
        return at

    def passed_on(tag, after):
        srcs, fwd = passing[tag]
        lands = _forward_wait(f"forward_{tag}_wait", fwd, [after])
        return srcs, [lax.dynamic_update_index_in_dim(g, o, chip, 0) for g, o in zip(lands, srcs)]

    def late(tag):
        return lambda after: dict(zip(ffn, passed_on(tag, after)[1]))

    layer0 = _layer_params(0, dict(
        hn=hn, w_in=_assemble_w_in(w_in_by_chip, own_w_in, place), conv_w=conv_by_chip, late=late("early"),
        before_mix=pass_on("early", early, len(ffn)), before_ffn_out=pass_on("mid", mid, 1)), w)

    def layer1(after):
        (own_w_in1,), (w_in1_by_chip,) = passed_on("mid", after)
        return _layer_params(1, dict(w_in=_assemble_w_in(w_in1_by_chip, own_w_in1, place), conv_w=conv_by_chip,
                                     late=late("later"), before_mix=pass_on("later", later, len(ffn))), w)

    saved, layers, loss_lanes, dh, dhb, d_norm_final = _forward(x[0], loss_target[0], [layer0, layer1],
                                                                 norm_final[None])

    sums, arrived = {}, {}

    def exchange_start(tag, l, names, grads, after=()):
        mine = [grads[n] for n in names]
        shapes = [jax.ShapeDtypeStruct((g.shape[0], g.shape[1] // 2, g.shape[2]), F32) for g in mine]
        return tag, l, names, _split_start(f"exchange_{tag}_start", _exchange_plan, mine, shapes, len(mine), after)

    def add_halves(l, names, mine, theirs):
        for n, g, t in zip(names, mine, theirs):
            sums[l, n] = (_sum_halves_w_in if n == "w_in" else _sum_halves)(g, t, c_arr)

    def exchange_wait(handle, after):
        tag, l, names, started = handle
        add_halves(l, names, *_split_wait(f"exchange_{tag}_wait", _exchange_plan, started, after))

    def scatter_start(tag, l, names, after=()):
        partial = [sums[l, n][1] for n in names]
        shapes = [jax.ShapeDtypeStruct((3,) + p.shape[1:], p.dtype) for p in partial]
        return tag, l, names, _split_start(f"scatter_{tag}_start", _scatter_plan, partial, shapes, 3 * len(names), after)

    def scatter_wait(handle, after):
        tag, l, names, started = handle
        for n, q in zip(names, _split_wait(f"scatter_{tag}_wait", _scatter_plan, started, after)[1]):
            arrived[l, n] = q

    last = DEPTH - 1
    swiglu = BIG[2:]
    dh1, dh1b, g_ffn = _layer_bwd_ffn(dh, dhb, layers[last], saved[last])
    dh, dhb, g_mix = _layer_bwd_mixer(dh1, dh1b, layers[last], saved[last])
    gl = [None, _reference_layout({**g_ffn, **g_mix})]
    ex_last = exchange_start("last", last, BIG, gl[last])
    dh1, dh1b, g_ffn = _layer_bwd_ffn(dh, dhb, layers[0], saved[0], after=[ex_last[-1]["token"]])
    exchange_wait(ex_last, dh1)
    sc_last = scatter_start("last", last, BIG)
    ex_ffn = exchange_start("swiglu", 0, swiglu, g_ffn, [sc_last[-1]["token"]])
    sc_ffn = []

    def midway(do):
        exchange_wait(ex_ffn, do)
        sc_ffn.append(scatter_start("swiglu", 0, swiglu))
        return [sc_ffn[0][-1]["token"]]

    ex_rest = []

    def late(grads):
        rest_grads = dict(w_in=grads["w_in"], w_out=grads["w_out"].reshape(NCHIP, D // NCHIP, D))
        ex_rest.append(exchange_start("rest", 0, BIG[:2], rest_grads))
        return [ex_rest[0][-1]["token"]]

    dx, _, g_mix = _layer_bwd_mixer(dh1, dh1b, layers[0], saved[0], after=[ex_ffn[-1]["token"]], midway=midway,
                                    late=late)
    scatter_wait(sc_last, dx)
    scatter_wait(sc_ffn[0], dx)
    gl[0] = _reference_layout({**g_ffn, **g_mix})

    small_g = [jnp.stack([gl[l][n] for l in range(DEPTH)]) for n in SMALL[:-1]] + [d_norm_final[0]]
    conv_g = jnp.stack([gl[l]["conv_w"] for l in range(DEPTH)])
    summed = small_g + [conv_g, loss_lanes[0, :1]]
    total = _allreduce_small(_pack(summed))
    exchange_wait(ex_rest[0], total)
    sc_rest = scatter_start("rest", 0, BIG[:2])

    travelling = [sc_rest[-1]["token"]]
    reduced, g_out, delta, new_m, new_v = {}, {}, {}, {}, {}

    def adamw_large(names, joined):
        for n, g in zip(names, joined):
            res = _adamw(kernel_view(n, w[n]), g, kernel_view(n, m[n]), kernel_view(n, v[n]))
            g_out[n], delta[n], new_m[n], new_v[n] = (kernel_view(n, a) for a in (g,) + tuple(res))

    for n in BIG:
        for l in (range(DEPTH) if n in swiglu else [last]):
            reduced[n] = _sum_chips(sums[l, n][0], arrived[l, n], place, l, into=reduced.get(n), after=travelling)
    adamw_large(swiglu, _join_halves("join_swiglu", [reduced[n] for n in swiglu]))
    scatter_wait(sc_rest, [new_v[n] for n in swiglu] + [reduced[n] for n in BIG[:2]])
    for n in BIG[:2]:
        reduced[n] = _sum_chips(sums[0, n][0], arrived[0, n], place, 0, into=reduced[n])
    adamw_large(BIG[:2], _join_halves("join_rest", [reduced[n] for n in BIG[:2]]))
    *small_r, conv_r, loss = _unpack(total, summed)
    g_out.update(zip(SMALL, small_r))
    g_out["conv_w"] = lax.dynamic_slice_in_dim(conv_r, chip * conv_w.shape[2], conv_w.shape[2], axis=2)

    rest = SMALL + ("conv_w",)
    rows_of = lambda a: a.reshape(1, -1) if a.ndim == 1 else a
    results = _adamw_small(*[[rows_of(src[n]) for n in rest] for src in (w, g_out, m, v)])
    for dst, arrs in zip((delta, new_m, new_v), results):
        dst.update({n: a.reshape(w[n].shape) for n, a in zip(rest, arrs)})

    return (loss[0], dx[None], *[g_out[n] for n in ORDER], *[delta[n] for n in ORDER], *[new_m[n] for n in ORDER],
            *[new_v[n] for n in ORDER])
```

```python
import functools

import jax
import jax.numpy as jnp
from jax import lax
from jax.experimental import pallas as pl
from jax.experimental.pallas import tpu as pltpu

F32 = jnp.float32
BF16 = jnp.bfloat16
MESH = pl.DeviceIdType.MESH
ANY = pl.BlockSpec(memory_space=pl.ANY)
HIGHEST = lax.Precision.HIGHEST

T = 2048
D = 1024
DEPTH = 2
NCHIP = 4
HEADS = 4
HD = 128
HW = HEADS * HD
CH = 64
GCH = 128
IN_DIM = 3080
NP = 3200
BA_OFF = 3072
FF_SH = 704
EPS = 1e-6
LANE = 128
VMEM_LIMIT = 56 * 1024 * 1024

ADAM_LR = 0.001
ADAM_B1 = 0.9
ADAM_B2 = 0.999
ADAM_EPS = 1e-08
ADAM_WD = 0.01
ADAM_STEP = 10


def _cparams(sem=None):
    return pltpu.CompilerParams(dimension_semantics=sem, vmem_limit_bytes=VMEM_LIMIT)


_DIMS = {"nn": (((1,), (0,)), ((), ())), "nt": (((1,), (1,)), ((), ())), "tn": (((0,), (0,)), ((), ()))}


def _mm(name, mode, a, bs, *, tm, tn, tk, out_dtypes=(F32,), reduce_g=False, resid=None, extras=(), epilogue=None,
        b_spec=None, n_n=None, after=(), fold_g=False, sub_m=1):
    assert sub_m == 1 or (mode != "tn" and tm % (8 * sub_m) == 0), (name, sub_m)
    nb = len(bs)
    ga = a.shape[0]
    gbs = [1 if b_spec is not None else b.shape[0] for b in bs]
    g_n = max([ga] + gbs)
    if mode == "tn":
        k_n, m_n = a.shape[1:]
    else:
        m_n, k_n = a.shape[1:]
    if n_n is None:
        n_n = bs[0].shape[1] if mode == "nt" else bs[0].shape[2]
    assert m_n % tm == 0 and n_n % tn == 0 and k_n % tk == 0, (name, m_n, n_n, k_n)
    mi, nj, kk = m_n // tm, n_n // tn, k_n // tk
    lead = g_n if fold_g else None
    if reduce_g:
        g_steps = 1 if fold_g else g_n
        grid = (mi, nj, g_steps, kk)
        ids = lambda i, j, g, k: (g, i, j, k)
        n_red = g_steps * kk
        red_idx = lambda: pl.program_id(2) * kk + pl.program_id(3)
        sem = ("parallel", "parallel", "arbitrary", "arbitrary")
    else:
        grid = (g_n, mi, nj, kk)
        ids = lambda g, i, j, k: (g, i, j, k)
        n_red = kk
        red_idx = lambda: pl.program_id(3)
        sem = ("parallel", "parallel", "parallel", "arbitrary")

    def pick(gsz, g):
        return g if gsz > 1 else 0

    def a_map(*p):
        g, i, j, k = ids(*p)
        return (pick(ga, g), k, i) if mode == "tn" else (pick(ga, g), i, k)

    def b_map(gsz):
        def f(*p):
            g, i, j, k = ids(*p)
            if b_spec is not None:
                return b_spec[1](g, i, j, k)
            return (pick(gsz, g), j, k) if mode == "nt" else (pick(gsz, g), k, j)
        return f

    def o_map(gsz):
        def f(*p):
            g, i, j, k = ids(*p)
            return (0 if reduce_g else pick(gsz, g), i, j)
        return f

    a_spec = pl.BlockSpec((lead, tk, tm) if mode == "tn" else (lead, tm, tk), a_map)
    b_block = b_spec[0] if b_spec is not None else ((lead, tn, tk) if mode == "nt" else (lead, tk, tn))
    b_specs = [pl.BlockSpec(b_block, b_map(gs)) for gs in gbs]
    x_specs = [pl.BlockSpec((None, tm, tn), o_map(e.shape[0])) for e in extras]
    r_specs = [pl.BlockSpec((None, tm, tn), o_map(resid.shape[0]))] if resid is not None else []
    g_out = 1 if reduce_g else g_n
    out_shape = [jax.ShapeDtypeStruct((g_out, m_n, n_n), dt) for dt in out_dtypes]
    out_specs = [pl.BlockSpec((None, tm, tn), o_map(g_out)) for _ in out_dtypes]
    nx, nr, no = len(extras), len(r_specs), len(out_dtypes)
    n_in = 1 + nb + nx + nr + len(after)
    dims = _DIMS[mode]

    def body(*refs):
        a_ref = refs[0]
        b_refs = refs[1:1 + nb]
        x_refs = refs[1 + nb:1 + nb + nx]
        r_refs = refs[1 + nb + nx:1 + nb + nx + nr]
        o_refs = refs[n_in:n_in + no]
        acc_refs = refs[n_in + no:]
        def dots(rows):
            if fold_g:
                return [sum(lax.dot_general(a_ref[g, rows, :], b_ref[g], dims, preferred_element_type=F32)
                            for g in range(g_n)) for b_ref in b_refs]
            av = a_ref[...] if mode == "tn" else a_ref[rows, :]
            return [lax.dot_general(av, b_ref[...], dims, preferred_element_type=F32) for b_ref in b_refs]

        def finish(accs, rows=slice(None)):
            if r_refs:
                accs[0] = accs[0] + r_refs[0][rows, :]
            outs = epilogue(accs, [x[rows, :] for x in x_refs]) if epilogue is not None else accs
            for o_ref, o in zip(o_refs, outs):
                o_ref[rows, :] = o.astype(o_ref.dtype)

        if n_red == 1:
            slabs = [slice(s * (tm // sub_m), (s + 1) * (tm // sub_m)) for s in range(sub_m)]
            ahead = dots(slabs[0])
            for s, rows in enumerate(slabs):
                now, ahead = ahead, (dots(slabs[s + 1]) if s + 1 < sub_m else None)
                finish(now, rows)
            return
        products = dots(slice(None))
        r = red_idx()
        for p, acc in zip(products, acc_refs):
            @pl.when(r == 0)
            def _():
                acc[...] = p

            @pl.when((r > 0) & (r < n_red - 1))
            def _():
                acc[...] += p

        @pl.when(r == n_red - 1)
        def _():
            finish([acc[...] + p for p, acc in zip(products, acc_refs)])

    return pl.pallas_call(
        body, name=name, grid=grid,
        in_specs=[a_spec] + b_specs + x_specs + r_specs + [ANY] * len(after),
        out_specs=out_specs, out_shape=out_shape,
        scratch_shapes=[pltpu.VMEM((tm, tn), F32) for _ in range(nb if n_red > 1 else 0)],
        compiler_params=_cparams(sem),
    )(a, *bs, *extras, *([resid] if resid is not None else []), *after)


def _sigmoid(x):
    return 1.0 / (1.0 + jnp.exp(-x))


def _silu(x):
    return x * _sigmoid(x)


def _gelu(x):
    return 0.5 * x * (1.0 + jnp.tanh(0.7978845608028654 * (x + 0.044715 * (x * x * x))))


def _rms_fn(h, gain):
    return h * lax.rsqrt(jnp.mean(h * h, axis=-1, keepdims=True) + EPS) * gain


def _shift_impl(x, s):
    n = x.shape[0]
    rolled = pltpu.roll(x, s % n, 0)
    row = lax.broadcasted_iota(jnp.int32, x.shape, 0)
    return jnp.where((row >= s) & (row < n + s), rolled, 0.0)


@functools.partial(jax.custom_vjp, nondiff_argnums=(1,))
def _shift(x, s):
    return _shift_impl(x, s)


def _shift_fwd(x, s):
    return _shift_impl(x, s), None


def _shift_bwd(s, _, g):
    return (_shift_impl(g, -s),)


_shift.defvjp(_shift_fwd, _shift_bwd)


def _prep_fn(x, w, qk_scale, is_v):
    y = x * w[3:4, :]
    for i in range(3):
        y = y + _shift(x, 3 - i) * w[i:i + 1, :]
    y = _silu(y)
    nrm = lax.rsqrt(jnp.sum(y * y, axis=-1, keepdims=True) + EPS) * qk_scale
    return y * jnp.where(is_v, 1.0, nrm)


def _softplus(x):
    return jnp.maximum(x, 0.0) + jnp.log(1.0 + jnp.exp(-jnp.abs(x)))


def _gates_fn(ba, a_log, dt_bias):
    lane = lax.broadcasted_iota(jnp.int32, ba.shape, 1)
    beta = _sigmoid(ba)
    g = -jnp.exp(a_log) * _softplus(ba + dt_bias)
    return jnp.where(lane < HEADS, beta, g)


def _dot16(a, b, dims=_DIMS["nn"]):
    return lax.dot_general(a.astype(BF16), b.astype(BF16), dims, preferred_element_type=F32)


def _dot32(a, b):
    return jnp.dot(a, b, preferred_element_type=F32, precision=HIGHEST)


def _dot3(a, b, dims=_DIMS["nn"]):
    return lax.dot_general(a, b, dims, preferred_element_type=F32, precision=lax.Precision.HIGH)


def _tri_inverses(mats, tick=lambda: None):
    row = lax.broadcasted_iota(jnp.int32, (CH, CH), 0)
    col = lax.broadcasted_iota(jnp.int32, (CH, CH), 1)
    eye = (row == col).astype(F32)
    ts = [eye - a for a in mats]
    ps = list(mats)
    for _ in range(5):
        ps = [_dot3(p, p) for p in ps]
        tick()
        ts = [t + _dot3(t, p) for t, p in zip(ts, ps)]
        tick()
    return ts


@jax.custom_vjp
def _tri_solves(mats, rhs):
    return [_dot3(t, b) for t, b in zip(_tri_inverses(mats), rhs)]


def _tri_solves_fwd(mats, rhs):
    ts = _tri_inverses(mats)
    xs = [_dot3(t, b) for t, b in zip(ts, rhs)]
    return xs, (ts, xs)


def _tri_solves_bwd(res, dxs):
    ts, xs = res
    dbs = [_dot3(t, dx, _DIMS["tn"]) for t, dx in zip(ts, dxs)]
    return [-_dot3(db, x, _DIMS["nt"]) for db, x in zip(dbs, xs)], dbs


_tri_solves.defvjp(_tri_solves_fwd, _tri_solves_bwd)


def _chunk_prep_fn(xs, bgs, tick=None):
    step = tick or (lambda: None)
    row = lax.broadcasted_iota(jnp.int32, (CH, CH), 0)
    col = lax.broadcasted_iota(jnp.int32, (CH, CH), 1)
    incl = row >= col
    strict = row > col
    lmat = incl.astype(F32)
    n = len(xs)
    items = [(i, h) for i in range(n) for h in range(HEADS)]
    part = lambda i, h, c: xs[i][:, c * HW + h * HD:c * HW + (h + 1) * HD]
    q = [part(i, h, 0) for i, h in items]
    k = [part(i, h, 1) for i, h in items]
    v = [part(i, h, 2) for i, h in items]
    beta = [bgs[i][:, h:h + 1] for i, h in items]
    gc_all = [_dot32(lmat, bg) for bg in bgs]
    step()
    gc = [gc_all[i][:, HEADS + h:HEADS + h + 1] for i, h in items]
    gmat = [jnp.where(strict, jnp.broadcast_to(bgs[i][:, HEADS + h:HEADS + h + 1], (CH, CH)), 0.0) for i, h in items]
    diff = [_dot3(lmat, m) for m in gmat]
    step()
    decay = [jnp.where(incl, jnp.exp(jnp.where(incl, d, 0.0)), 0.0) for d in diff]
    k_beta = [kk * b for kk, b in zip(k, beta)]
    kk_t = [_dot16(kb, kk, _DIMS["nt"]) for kb, kk in zip(k_beta, k)]
    step()
    qk_t = [_dot16(qq, kk, _DIMS["nt"]) for qq, kk in zip(q, k)]
    step()
    a = [jnp.where(strict, m * d, 0.0) for m, d in zip(kk_t, decay)]
    eg = [jnp.exp(g) for g in gc]
    rhs = [jnp.concatenate([vv * b, kb * e], axis=-1) for vv, b, kb, e in zip(v, beta, k_beta, eg)]
    if tick is None:
        uw = _tri_solves(a, rhs)
    else:
        uw = [_dot3(t, b) for t, b in zip(_tri_inverses(a, tick), rhs)]
    qk = [m * d for m, d in zip(qk_t, decay)]
    g_last = [g[CH - 1:CH, :] for g in gc]
    qe = [qq * e for qq, e in zip(q, eg)]
    kd = [kk * jnp.exp(gl - g) for kk, gl, g in zip(k, g_last, gc)]
    egl = [jnp.broadcast_to(jnp.exp(gl), (1, HD)) for gl in g_last]
    out = []
    for i in range(n):
        mine = slice(i * HEADS, (i + 1) * HEADS)
        cat = lambda vals: jnp.concatenate(vals[mine], axis=-1)
        out.append((cat([x[:, :HD] for x in uw]), cat([x[:, HD:] for x in uw]), cat(qe), cat(kd),
                    jnp.concatenate([m[None] for m in qk[mine]], axis=0), cat(egl)))
    return out


def _state_levels(chunks, s, outs, befores, final):
    for u, w, qe, kd, qk, egl in chunks:
        befores.append(s)
        ws = [_dot16(a, b) for a, b in zip(w, s)]
        qs = [_dot16(a, b) for a, b in zip(qe, s)]
        yield
        v_new = [a - b for a, b in zip(u, ws)]
        outs.append([a + _dot16(b, c) for a, b, c in zip(qs, qk, v_new)])
        s = [a * e + _dot16(b, c, _DIMS["tn"]) for a, e, b, c in zip(s, egl, kd, v_new)]
        yield
    final.append(s)


def _chunk_state_fn(u, w, qe, kd, qk, egl, s):
    ws = [_dot16(a, b) for a, b in zip(w, s)]
    qs = [_dot16(a, b) for a, b in zip(qe, s)]
    v_new = [a - b for a, b in zip(u, ws)]
    o = [a + _dot16(b, c) for a, b, c in zip(qs, qk, v_new)]
    s_new = [a * e + _dot16(b, c, _DIMS["tn"]) for a, e, b, c in zip(s, egl, kd, v_new)]
    return o, s_new


def _mix_fn(o, z, ur, vr, ong, lng, lnb, ws, bst):
    row = lax.broadcasted_iota(jnp.int32, (GCH, GCH), 0)
    col = lax.broadcasted_iota(jnp.int32, (GCH, GCH), 1)
    causal = row >= col
    ug = _gelu(ur)
    vg = _gelu(vr)
    sls = [slice(h * HD, (h + 1) * HD) for h in range(HEADS)]
    oh = [o[:, sl] for sl in sls]
    oh = [x * lax.rsqrt(jnp.mean(x * x, axis=-1, keepdims=True) + EPS) for x in oh]
    outs_dn = [x * ong * _silu(z[:, sl]) for x, sl in zip(oh, sls)]
    vh = [vg[:, sl] for sl in sls]
    mu = [jnp.mean(x, axis=-1, keepdims=True) for x in vh]
    var = [jnp.mean(jnp.square(x - m), axis=-1, keepdims=True) for x, m in zip(vh, mu)]
    vn = [(x - m) * lax.rsqrt(s + EPS) * lng[:, sl] + lnb[:, sl] for x, m, s, sl in zip(vh, mu, var, sls)]
    mixed = [_dot16(jnp.where(causal, ws[h], 0.0), vn[h]) for h in range(HEADS)]
    outs_gm = [ug[:, sl] * (mixed[h] + bst[:, h:h + 1]) for h, sl in enumerate(sls)]
    return jnp.concatenate(outs_dn + outs_gm, axis=-1)


def _loss_fn(h, gain, tgt):
    y = _rms_fn(h, gain)
    return 0.5 * jnp.sum(jnp.mean(jnp.square(y - tgt), axis=-1))


RT = 512


def _rows(n=D):
    return pl.BlockSpec((RT, n), lambda i: (i, 0))


def _whole(shape):
    nd = len(shape)
    return pl.BlockSpec(shape, lambda i: (0,) * nd)


def _rmsnorm(name, h, gain):
    def body(h_ref, g_ref, o_ref):
        o_ref[...] = _rms_fn(h_ref[...], g_ref[...]).astype(BF16)

    return pl.pallas_call(
        body, name=name, grid=(T // RT,), in_specs=[_rows(), _whole((1, D))], out_specs=_rows(),
        out_shape=jax.ShapeDtypeStruct((T, D), BF16), compiler_params=_cparams(("parallel",)),
    )(h, gain)


def _rmsnorm_bwd(name, dhn, h, gain, resid):
    def body(dhn_ref, h_ref, g_ref, r_ref, dh_ref, dh16_ref, dg_ref):
        _, vjp = jax.vjp(_rms_fn, h_ref[...], g_ref[...])
        dh, dg = vjp(dhn_ref[...])
        dh = r_ref[...] + dh
        dh_ref[...] = dh
        dh16_ref[...] = dh.astype(BF16)

        @pl.when(pl.program_id(0) == 0)
        def _():
            dg_ref[...] = dg

        @pl.when(pl.program_id(0) > 0)
        def _():
            dg_ref[...] += dg

    return pl.pallas_call(
        body, name=name, grid=(T // RT,), in_specs=[_rows(), _rows(), _whole((1, D)), _rows()],
        out_specs=[_rows(), _rows(), _whole((1, D))],
        out_shape=[jax.ShapeDtypeStruct((T, D), F32), jax.ShapeDtypeStruct((T, D), BF16),
                   jax.ShapeDtypeStruct((1, D), F32)],
        compiler_params=_cparams(("arbitrary",)),
    )(dhn, h, gain, resid)


def _loss_head(h, gain, tgt):
    def body(h_ref, g_ref, t_ref, l_ref, dh_ref, dh16_ref, dg_ref):
        loss, vjp = jax.vjp(lambda hh, gg: _loss_fn(hh, gg, t_ref[...]), h_ref[...], g_ref[...])
        dh, dg = vjp(jnp.ones((), F32))
        dh_ref[...] = dh
        dh16_ref[...] = dh.astype(BF16)
        lv = jnp.full((1, LANE), loss, F32)

        @pl.when(pl.program_id(0) == 0)
        def _():
            dg_ref[...] = dg
            l_ref[...] = lv

        @pl.when(pl.program_id(0) > 0)
        def _():
            dg_ref[...] += dg
            l_ref[...] += lv

    return pl.pallas_call(
        body, name="loss_head", grid=(T // RT,), in_specs=[_rows(), _whole((1, D)), _rows()],
        out_specs=[_whole((1, LANE)), _rows(), _rows(), _whole((1, D))],
        out_shape=[jax.ShapeDtypeStruct((1, LANE), F32), jax.ShapeDtypeStruct((T, D), F32),
                   jax.ShapeDtypeStruct((T, D), BF16), jax.ShapeDtypeStruct((1, D), F32)],
        compiler_params=_cparams(("arbitrary",)),
    )(h, gain, tgt)


def _prep_flags():
    j = pl.program_id(0)
    qk_scale = jnp.where(j < HEADS, HD ** -0.5, 1.0).astype(F32)
    return qk_scale, j >= 2 * HEADS


def _prep(proj, conv_w):
    def body(x_ref, w_ref, o_ref):
        qk_scale, is_v = _prep_flags()
        o_ref[...] = _prep_fn(x_ref[...], w_ref[...], qk_scale, is_v)

    col = lambda j: (0, j)
    return pl.pallas_call(
        body, name="gdn_prep", grid=(3 * HEADS,),
        in_specs=[pl.BlockSpec((T, HD), col), pl.BlockSpec((4, HD), col)], out_specs=pl.BlockSpec((T, HD), col),
        out_shape=jax.ShapeDtypeStruct((T, 3 * HW), F32), compiler_params=_cparams(("parallel",)),
    )(proj, conv_w)


def _prep_bwd(proj, conv_w, dqkv, dproj):
    def body(x_ref, w_ref, d_ref, _, dx_ref, dw_ref):
        qk_scale, is_v = _prep_flags()
        _, vjp = jax.vjp(lambda x, w: _prep_fn(x, w, qk_scale, is_v), x_ref[...], w_ref[...])
        dx, dw = vjp(d_ref[...])
        dx_ref[...] = dx.astype(BF16)
        dw_ref[...] = dw

    col = lambda j: (0, j)
    return pl.pallas_call(
        body, name="gdn_prep_bwd", grid=(3 * HEADS,),
        in_specs=[pl.BlockSpec((T, HD), col), pl.BlockSpec((4, HD), col), pl.BlockSpec((T, HD), col), ANY],
        out_specs=[pl.BlockSpec((T, HD), col), pl.BlockSpec((4, HD), col)],
        out_shape=[jax.ShapeDtypeStruct((T, NP), BF16), jax.ShapeDtypeStruct((4, 3 * HW), F32)],
        input_output_aliases={3: 0}, compiler_params=_cparams(("parallel",)),
    )(proj, conv_w, dqkv, dproj)


BA_BLK = BA_OFF // LANE


def _gates(proj, a_log, dt_bias):
    def body(x_ref, a_ref, d_ref, o_ref):
        o_ref[...] = _gates_fn(x_ref[...], a_ref[...], d_ref[...])

    return pl.pallas_call(
        body, name="gdn_gates", grid=(1,),
        in_specs=[pl.BlockSpec((T, LANE), lambda i: (0, BA_BLK)), _whole((1, LANE)), _whole((1, LANE))],
        out_specs=_whole((T, LANE)),
        out_shape=jax.ShapeDtypeStruct((T, LANE), F32), compiler_params=_cparams(("arbitrary",)),
    )(proj, a_log, dt_bias)


def _gates_bwd(proj, a_log, dt_bias, dbg, dproj):
    def body(x_ref, a_ref, d_ref, dbg_ref, _, dx_ref, da_ref, dd_ref):
        _, vjp = jax.vjp(_gates_fn, x_ref[...], a_ref[...], d_ref[...])
        dx, da_ref[...], dd_ref[...] = vjp(dbg_ref[...])
        dx_ref[...] = dx.astype(BF16)

    ba = pl.BlockSpec((T, LANE), lambda i: (0, BA_BLK))
    return pl.pallas_call(
        body, name="gdn_gates_bwd", grid=(1,),
        in_specs=[ba, _whole((1, LANE)), _whole((1, LANE)), _whole((T, LANE)), ANY],
        out_specs=[ba, _whole((1, LANE)), _whole((1, LANE))],
        out_shape=[jax.ShapeDtypeStruct((T, NP), BF16), jax.ShapeDtypeStruct((1, LANE), F32),
                   jax.ShapeDtypeStruct((1, LANE), F32)],
        input_output_aliases={4: 0}, compiler_params=_cparams(("arbitrary",)),
    )(proj, a_log, dt_bias, dbg, dproj)


NCK = T // CH
CPS = 4


def _chunk_group_specs(at):
    wide = pl.BlockSpec((CPS * CH, HW), lambda n: (at(n), 0))
    return [wide, wide, wide, wide, pl.BlockSpec((HEADS, CPS * CH, CH), lambda n: (0, at(n), 0)),
            pl.BlockSpec((CPS, 1, HW), lambda n: (at(n), 0, 0))]


def _chunk_prep_shapes(dtypes):
    shp = [(T, HW), (T, HW), (T, HW), (T, HW), (HEADS, T, CH), (NCK, 1, HW)]
    return [jax.ShapeDtypeStruct(s, dt) for s, dt in zip(shp, dtypes)]


NGROUP = NCK // CPS
PREP_DTYPES = (F32, BF16, BF16, BF16, BF16, F32)


def _delta_rule(qkv, bg):
    def body(x_ref, bg_ref, *refs):
        prep_out, (o_ref, sh_ref), held, s_ref = refs[:6], refs[6:8], refs[8:14], refs[14]
        i = pl.program_id(0)

        @pl.when(i == 0)
        def _():
            for r in held + (s_ref,):
                r[...] = jnp.zeros_like(r)

        rows = [slice(ci * CH, (ci + 1) * CH) for ci in range(CPS)]
        u_h, w_h, qe_h, kd_h, qk_h, egl_h = held
        chunks = [_head_args((u_h.at[r, :], w_h.at[r, :], qe_h.at[r, :], kd_h.at[r, :], qk_h.at[:, r, :], egl_h.at[ci]))
                  for ci, r in enumerate(rows)]
        start = [jnp.where(i <= 1, 0.0, s_ref[h]) for h in range(HEADS)]
        outs, befores, final = [], [], []
        levels = _state_levels(chunks, start, outs, befores, final)
        res = _chunk_prep_fn([x_ref[r, :] for r in rows], [bg_ref[r, :] for r in rows], tick=lambda: next(levels, None))
        for _ in levels:
            pass
        for ci, (u, w, qe, kd, qk, egl) in enumerate(res):
            for refs_pair, val in zip(zip(prep_out[:4], held[:4]), (u, w, qe, kd)):
                for ref in refs_pair:
                    ref[rows[ci], :] = val.astype(ref.dtype)
            for ref in (prep_out[4], qk_h):
                ref[:, rows[ci], :] = qk.astype(ref.dtype)
            for ref in (prep_out[5], egl_h):
                ref[ci] = egl
        for ci, r in enumerate(rows):
            for h in range(HEADS):
                o_ref[r, h * HD:(h + 1) * HD] = outs[ci][h]
                sh_ref[h, ci] = befores[ci][h]
        for h in range(HEADS):
            s_ref[h] = final[0][h]

    now = lambda n: jnp.minimum(n, NGROUP - 1)
    was = lambda n: jnp.maximum(n - 1, 0)
    wide = lambda at: pl.BlockSpec((CPS * CH, HW), lambda n: (at(n), 0))
    held = [pltpu.VMEM(s, dt) for s, dt in zip(
        [(CPS * CH, HW)] * 4 + [(HEADS, CPS * CH, CH), (CPS, 1, HW)], PREP_DTYPES)]
    out = pl.pallas_call(
        body, name="gdn_delta_rule", grid=(NGROUP + 1,),
        in_specs=[pl.BlockSpec((CPS * CH, 3 * HW), lambda n: (now(n), 0)),
                  pl.BlockSpec((CPS * CH, LANE), lambda n: (now(n), 0))],
        out_specs=[wide(now)] * 4 + [pl.BlockSpec((HEADS, CPS * CH, CH), lambda n: (0, now(n), 0)),
                                     pl.BlockSpec((CPS, 1, HW), lambda n: (now(n), 0, 0)), wide(was),
                                     pl.BlockSpec((HEADS, CPS, HD, HD), lambda n: (0, was(n), 0, 0))],
        out_shape=_chunk_prep_shapes(PREP_DTYPES) + [jax.ShapeDtypeStruct((T, HW), F32),
                                                     jax.ShapeDtypeStruct((HEADS, NCK, HD, HD), F32)],
        scratch_shapes=held + [pltpu.VMEM((HEADS, HD, HD), F32)], compiler_params=_cparams(("arbitrary",)),
    )(qkv, bg)
    return out[:6], out[6], out[7]


def _chunk_prep_bwd(qkv, bg, cots):
    def body(x_ref, bg_ref, du, dw, dqe, dkd, dqk, degl, dx_ref, dbg_ref):
        rows = [slice(ci * CH, (ci + 1) * CH) for ci in range(CPS)]
        _, vjp = jax.vjp(_chunk_prep_fn, [x_ref[r, :] for r in rows], [bg_ref[r, :] for r in rows])
        dxs, dbgs = vjp([(du[r, :], dw[r, :], dqe[r, :], dkd[r, :], dqk[:, r, :], degl[ci])
                         for ci, r in enumerate(rows)])
        for r, dx, dbg in zip(rows, dxs, dbgs):
            dx_ref[r, :] = dx
            dbg_ref[r, :] = dbg

    wide = pl.BlockSpec((CPS * CH, HW), lambda n: (n, 0))
    return pl.pallas_call(
        body, name="gdn_chunk_prep_bwd", grid=(NCK // CPS,),
        in_specs=[pl.BlockSpec((CPS * CH, 3 * HW), lambda n: (n, 0)), pl.BlockSpec((CPS * CH, LANE), lambda n: (n, 0)),
                  wide, wide, wide, wide, pl.BlockSpec((HEADS, CPS * CH, CH), lambda n: (0, n, 0)),
                  pl.BlockSpec((CPS, 1, HW), lambda n: (n, 0, 0))],
        out_specs=[pl.BlockSpec((CPS * CH, 3 * HW), lambda n: (n, 0)), pl.BlockSpec((CPS * CH, LANE), lambda n: (n, 0))],
        out_shape=[jax.ShapeDtypeStruct((T, 3 * HW), F32), jax.ShapeDtypeStruct((T, LANE), F32)],
        compiler_params=_cparams(("parallel",)),
    )(qkv, bg, *cots)


def _head_args(refs):
    u, w, qe, kd, qk, egl = refs
    sls = [slice(h * HD, (h + 1) * HD) for h in range(HEADS)]
    return ([u[:, sl] for sl in sls], [w[:, sl].astype(F32) for sl in sls], [qe[:, sl].astype(F32) for sl in sls],
            [kd[:, sl].astype(F32) for sl in sls], [qk[h].astype(F32) for h in range(HEADS)],
            [egl[:, sl] for sl in sls])


def _chunk_scan_bwd(prep, s_hist, do, after=()):
    n_in = 8 + len(after)

    def body(*refs):
        u_r, w_r, qe_r, kd_r, qk_r, egl_r, sh_ref, do_ref = refs[:8]
        d_refs = refs[n_in:n_in + 6]
        ds_ref = refs[n_in + 6]

        @pl.when(pl.program_id(0) == 0)
        def _():
            ds_ref[...] = jnp.zeros_like(ds_ref)

        sls = [slice(h * HD, (h + 1) * HD) for h in range(HEADS)]
        ds = [ds_ref[h] for h in range(HEADS)]
        for ci in reversed(range(CPS)):
            r = slice(ci * CH, (ci + 1) * CH)
            args = _head_args((u_r.at[r, :], w_r.at[r, :], qe_r.at[r, :], kd_r.at[r, :], qk_r.at[:, r, :], egl_r.at[ci]))
            _, vjp = jax.vjp(_chunk_state_fn, *args, [sh_ref[h, ci] for h in range(HEADS)])
            du, dw, dqe, dkd, dqk, degl, ds = vjp(([do_ref[r, sl] for sl in sls], ds))
            for h, sl in enumerate(sls):
                for d_ref, val in zip(d_refs[:4], (du, dw, dqe, dkd)):
                    d_ref[r, sl] = val[h]
                d_refs[4][h, r, :] = dqk[h]
                d_refs[5][ci, :, sl] = degl[h]
        for h in range(HEADS):
            ds_ref[h] = ds[h]

    rev = lambda n: NGROUP - 1 - n
    return pl.pallas_call(
        body, name="gdn_scan_bwd", grid=(NGROUP,),
        in_specs=_chunk_group_specs(rev) + [pl.BlockSpec((HEADS, CPS, HD, HD), lambda n: (0, rev(n), 0, 0)),
                                            pl.BlockSpec((CPS * CH, HW), lambda n: (rev(n), 0))] + [ANY] * len(after),
        out_specs=_chunk_group_specs(rev), out_shape=_chunk_prep_shapes((F32,) * 6),
        scratch_shapes=[pltpu.VMEM((HEADS, HD, HD), F32)], compiler_params=_cparams(("arbitrary",)),
    )(*prep, s_hist, do, *after)


def _mix_specs():
    pc = lambda c: pl.BlockSpec((GCH, HW), lambda i: (i, c))
    return [pl.BlockSpec((GCH, HW), lambda i: (i, 0)), pc(3), pc(4), pc(5), _whole((1, HD)), _whole((1, HW)),
            _whole((1, HW)), _whole((HEADS, GCH, GCH)), _whole((GCH, LANE))]


def _mix(o, proj, ong, lng, lnb, ws, bst, after=()):
    def body(o_ref, z_ref, u_ref, v_ref, ong_ref, lng_ref, lnb_ref, ws_ref, bs_ref, *rest):
        rest[-1][...] = _mix_fn(o_ref[...], z_ref[...], u_ref[...], v_ref[...], ong_ref[...], lng_ref[...],
                                lnb_ref[...], ws_ref[...], bs_ref[...]).astype(BF16)

    return pl.pallas_call(
        body, name="mix", grid=(T // GCH,), in_specs=_mix_specs() + [ANY] * len(after),
        out_specs=pl.BlockSpec((GCH, D), lambda i: (i, 0)), out_shape=jax.ShapeDtypeStruct((T, D), BF16),
        compiler_params=_cparams(("parallel",)),
    )(o, proj, proj, proj, ong, lng, lnb, ws, bst, *after)


def _mix_bwd(o, proj, ong, lng, lnb, ws, bst, dmix):
    def body(o_ref, z_ref, u_ref, v_ref, ong_ref, lng_ref, lnb_ref, ws_ref, bs_ref, dm_ref,
             do_ref, dzuv_ref, dong_ref, dlng_ref, dlnb_ref, dws_ref, dbs_ref):
        _, vjp = jax.vjp(_mix_fn, o_ref[...], z_ref[...], u_ref[...], v_ref[...], ong_ref[...], lng_ref[...],
                         lnb_ref[...], ws_ref[...], bs_ref[...])
        do, dz, du, dv, dong, dlng, dlnb, dws, dbs = vjp(dm_ref[...])
        do_ref[...] = do
        dzuv_ref[:, 0:HW] = dz.astype(BF16)
        dzuv_ref[:, HW:2 * HW] = du.astype(BF16)
        dzuv_ref[:, 2 * HW:3 * HW] = dv.astype(BF16)
        acc = [(dong_ref, dong), (dlng_ref, dlng), (dlnb_ref, dlnb), (dws_ref, dws), (dbs_ref, dbs)]

        @pl.when(pl.program_id(0) == 0)
        def _():
            for r, val in acc:
                r[...] = val

        @pl.when(pl.program_id(0) > 0)
        def _():
            for r, val in acc:
                r[...] += val

    shp = lambda *s: jax.ShapeDtypeStruct(s, F32)
    return pl.pallas_call(
        body, name="mix_bwd", grid=(T // GCH,),
        in_specs=_mix_specs() + [pl.BlockSpec((GCH, D), lambda i: (i, 0))],
        out_specs=[pl.BlockSpec((GCH, HW), lambda i: (i, 0)), pl.BlockSpec((GCH, 3 * HW), lambda i: (i, 1)),
                   _whole((1, HD)), _whole((1, HW)), _whole((1, HW)), _whole((HEADS, GCH, GCH)), _whole((GCH, LANE))],
        out_shape=[shp(T, HW), jax.ShapeDtypeStruct((T, NP), BF16), shp(1, HD), shp(1, HW), shp(1, HW),
                   shp(HEADS, GCH, GCH), shp(GCH, LANE)],
        compiler_params=_cparams(("arbitrary",)),
    )(o, proj, proj, proj, ong, lng, lnb, ws, bst, dmix)


def _swiglu_epilogue(accs, _):
    gate, up = accs
    return [gate, up, _silu(gate) * up]


def _swiglu_bwd_epilogue(accs, extras):
    dact = accs[0]
    gate, up = (e.astype(F32) for e in extras)
    sg = _sigmoid(gate)
    return [dact * up * (sg * (1.0 + gate * (1.0 - sg))), dact * (gate * sg)]


def _layer_fwd(h, p):
    hn = p.pop("hn") if "hn" in p else _rmsnorm("rms_mix", h, p["norm_mix"])
    proj = _mm("in_proj", "nn", hn[None], [p["w_in"][None]], tm=1024, tn=640, tk=D, sub_m=2)[0][0]
    qkv = _prep(proj, p["conv_w"])
    bg = _gates(proj, p["a_log"], p["dt_bias"])
    prep, o, s_hist = _delta_rule(qkv, bg)
    mix = _mix(o, proj, p["o_norm_g"], p["ln_v_g"], p["ln_v_b"], p["w_s"], p["bst"],
               p.pop("before_mix")(o) if "before_mix" in p else ())
    if "late" in p:
        p.update(p.pop("late")(mix))
    h1 = _mm("out_proj", "nn", mix[None], [p["w_out"].reshape(1, D, D)], tm=1024, tn=512, tk=D, resid=h[None],
             sub_m=2)[0][0]
    h2n = _rmsnorm("rms_ffn", h1, p["norm_ffn"])
    gate, up, act = _mm("ffn_in", "nt", h2n[None], [p["w_gate"], p["w_up"]], tm=1024, tn=FF_SH, tk=D,
                        out_dtypes=(BF16, BF16, BF16), epilogue=_swiglu_epilogue, sub_m=4)
    then = p.pop("before_ffn_out")(act) if "before_ffn_out" in p else ()
    h2 = _mm("ffn_out", "nn", act, [p["w_down"]], tm=1024, tn=512, tk=FF_SH, reduce_g=True, fold_g=True,
             resid=h1[None], sub_m=2, after=then)[0][0]
    saved = dict(h=h, hn=hn, proj=proj, qkv=qkv, bg=bg, prep=prep, o=o, s_hist=s_hist, mix=mix, h1=h1, h2n=h2n,
                 gate=gate, up=up, act=act)
    return h2, saved


def _layer_bwd_ffn(dh2, dh2b, p, s, after=()):
    dh2b = dh2b[None]
    dgate, dup = _mm("ffn_out_bwd", "nt", dh2b, [p["w_down"]], tm=1024, tn=FF_SH, tk=D, out_dtypes=(BF16, BF16),
                     extras=(s["gate"], s["up"]), epilogue=_swiglu_bwd_epilogue, after=after, sub_m=4)
    dh2n = _mm("ffn_gate_bwd", "nn", dgate, [p["w_gate"]], tm=1024, tn=512, tk=FF_SH, reduce_g=True, fold_g=True,
               sub_m=2)[0]
    dh2n = _mm("ffn_up_bwd", "nn", dup, [p["w_up"]], tm=1024, tn=512, tk=FF_SH, reduce_g=True, fold_g=True,
               resid=dh2n, sub_m=2)[0][0]
    dh1, dh1b, d_norm_ffn = _rmsnorm_bwd("rms_ffn_bwd", dh2n, s["h1"], p["norm_ffn"], dh2)
    d_w_down = _mm("ffn_wdown_grad", "tn", s["act"], [dh2b], tm=FF_SH, tn=512, tk=T)[0]
    d_w_gate = _mm("ffn_wgate_grad", "tn", dgate, [s["h2n"][None]], tm=FF_SH, tn=512, tk=T)[0]
    d_w_up = _mm("ffn_wup_grad", "tn", dup, [s["h2n"][None]], tm=FF_SH, tn=512, tk=T)[0]
    return dh1, dh1b, dict(norm_ffn=d_norm_ffn, w_gate=d_w_gate, w_up=d_w_up, w_down=d_w_down)


def _layer_bwd_mixer(dh1, dh1b, p, s, after=(), midway=None, late=None):
    dh1b = dh1b[None]
    dmix = _mm("out_proj_bwd", "nt", dh1b, [p["w_out"].reshape(1, D, D)], tm=1024, tn=512, tk=D, after=after,
               sub_m=2)[0][0]
    d_w_out = _mm("out_proj_wgrad", "tn", s["mix"][None], [dh1b], tm=512, tn=512, tk=T)[0][0]
    do, dproj, d_ong, d_lng, d_lnb, d_ws, d_bst = _mix_bwd(
        s["o"], s["proj"], p["o_norm_g"], p["ln_v_g"], p["ln_v_b"], p["w_s"], p["bst"], dmix)
    then = midway(do) if midway is not None else ()
    dqkv, dbg = _chunk_prep_bwd(s["qkv"], s["bg"], _chunk_scan_bwd(s["prep"], s["s_hist"], do, then))
    dproj, d_conv = _prep_bwd(s["proj"], p["conv_w"], dqkv, dproj)
    dproj, d_a_log, d_dt_bias = _gates_bwd(s["proj"], p["a_log"], p["dt_bias"], dbg, dproj)
    dproj = dproj[None]
    d_w_in = _mm("in_proj_wgrad", "tn", s["hn"][None], [dproj], tm=512, tn=640, tk=T)[0]
    last = late(dict(w_in=d_w_in, w_out=d_w_out)) if late is not None else ()
    dhn = _mm("in_proj_bwd", "nt", dproj, [p["w_in"][None]], tm=1024, tn=512, tk=NP, after=last,
              sub_m=2)[0][0]
    dh, dhb, d_norm_mix = _rmsnorm_bwd("rms_mix_bwd", dhn, s["h"], p["norm_mix"], dh1)
    grads = dict(norm_mix=d_norm_mix, w_in=d_w_in, conv_w=d_conv, a_log=d_a_log, dt_bias=d_dt_bias, o_norm_g=d_ong,
                 ln_v_g=d_lng, ln_v_b=d_lnb, w_s=d_ws, bst=d_bst, w_out=d_w_out)
    return dh, dhb, grads


def _lanes(v, off=0):
    return jnp.zeros((1, LANE), F32).at[0, off:off + v.shape[0]].set(v)


def _w_in_pieces():
    regions = [(0, 2048, 0), (2048, 2056, BA_OFF), (2056, IN_DIM, 2048)]
    sh = IN_DIM // NCHIP
    out = []
    for j in range(NCHIP):
        for lo, hi, at in regions:
            a, b = max(lo, j * sh), min(hi, (j + 1) * sh)
            if a < b:
                out.append((j, a - j * sh, at + a - lo, b - a))
    return out


W_IN_PIECES = _w_in_pieces()
WT = 256


def _assemble_w_in(gathered, own, place):
    def body(place_ref, g_ref, own_ref, o_ref):
        o_ref[:, IN_DIM:] = jnp.zeros((WT, NP - IN_DIM), BF16)
        mine = own_ref[...]
        for j, src, dst, width in W_IN_PIECES:
            val = jnp.where(place_ref[0] == j, mine[:, src:src + width], g_ref[j, :, src:src + width])
            o_ref[:, dst:dst + width] = val

    sh = IN_DIM // NCHIP
    return pl.pallas_call(
        body, name="assemble_w_in",
        grid_spec=pltpu.PrefetchScalarGridSpec(
            num_scalar_prefetch=1, grid=(D // WT,),
            in_specs=[pl.BlockSpec((NCHIP, WT, sh), lambda i, place_ref: (0, i, 0)),
                      pl.BlockSpec((WT, sh), lambda i, place_ref: (i, 0))],
            out_specs=pl.BlockSpec((WT, NP), lambda i, place_ref: (i, 0))),
        out_shape=jax.ShapeDtypeStruct((D, NP), BF16), compiler_params=_cparams(("parallel",)),
    )(place, gathered, own)


def _layer_params(l, big, small):
    return dict(
        {k: v for k, v in big.items() if k != "conv_w"},
        conv_w=jnp.concatenate([big["conv_w"][j, l] for j in range(NCHIP)], axis=1),
        norm_mix=small["norm_mix"][l][None], norm_ffn=small["norm_ffn"][l][None],
        a_log=_lanes(small["a_log"][l], HEADS), dt_bias=_lanes(small["dt_bias"][l], HEADS),
        o_norm_g=small["o_norm_g"][l][None], ln_v_g=small["ln_v_g"][l][None], ln_v_b=small["ln_v_b"][l][None],
        w_s=small["w_s"][l],
        bst=jnp.pad(small["b_s"][l].T, ((0, 0), (0, LANE - HEADS))),
    )


def _reference_layout(g):
    return dict(
        w_in=g["w_in"],
        w_out=g["w_out"].reshape(NCHIP, D // NCHIP, D),
        w_gate=g["w_gate"], w_up=g["w_up"], w_down=g["w_down"],
        conv_w=g["conv_w"], norm_mix=g["norm_mix"][0], norm_ffn=g["norm_ffn"][0],
        a_log=g["a_log"][0, HEADS:2 * HEADS], dt_bias=g["dt_bias"][0, HEADS:2 * HEADS],
        o_norm_g=g["o_norm_g"][0], ln_v_g=g["ln_v_g"][0], ln_v_b=g["ln_v_b"][0], w_s=g["w_s"],
        b_s=g["bst"][:, :HEADS].T,
    )


def _forward(x, tgt, layers, norm_final):
    h = x
    saved, params = [], []
    for p in layers:
        p = p(h) if callable(p) else p
        h, s = _layer_fwd(h, p)
        saved.append(s)
        params.append(p)
    return (saved, params) + tuple(_loss_head(h, norm_final, tgt))


def _local_step(x, tgt, layers, norm_final):
    saved, layers, loss, dh, dhb, d_norm_final = _forward(x, tgt, layers, norm_final)
    grads = [None] * DEPTH
    for l in reversed(range(DEPTH)):
        dh1, dh1b, g_ffn = _layer_bwd_ffn(dh, dhb, layers[l], saved[l])
        dh, dhb, g_mix = _layer_bwd_mixer(dh1, dh1b, layers[l], saved[l])
        grads[l] = {**g_ffn, **g_mix}
    return loss, dh, grads, d_norm_final


def _place():
    x, y, c = lax.axis_index("x"), lax.axis_index("y"), lax.axis_index("c")
    return x, y, c, [(1 - x, y), (x, 1 - y), (1 - x, 1 - y)]


def _remote(src, dst, send_sem, recv_sem, to):
    return pltpu.make_async_remote_copy(src_ref=src, dst_ref=dst, send_sem=send_sem, recv_sem=recv_sem,
                                        device_id=to, device_id_type=MESH)


def _comm_call(name, body, ins, out_shape, n_sems, aliases=None):
    return pl.pallas_call(
        body, name=name, in_specs=[ANY] * len(ins), out_specs=[ANY] * len(out_shape), out_shape=out_shape,
        scratch_shapes=[pltpu.SemaphoreType.DMA((n,)) for n in n_sems], input_output_aliases=aliases or {},
        compiler_params=pltpu.CompilerParams(has_side_effects=True),
    )(*ins)


def _half_rows(ref, of_c, dim):
    hr = ref.shape[dim] // 2
    return pl.ds(pl.multiple_of(of_c * hr, BF16_ROWS), hr)


def _gather_plan(whole):
    def plan(srcs, lands):
        x, y, c, others = _place()
        chip = 2 * x + y
        out = []
        for src, land, all_of_it in zip(srcs, lands, whole):
            for ox, oy in others:
                if all_of_it:
                    out.append((src, land.at[chip], (ox, oy, c)))
                else:
                    out.append((src.at[_half_rows(src, c, 0)], land.at[chip, _half_rows(src, c, 0)], (ox, oy, c)))
        return out
    return plan


def _forward_halves(lands):
    n = len(lands)

    def body(*refs):
        outs = refs[n:2 * n]
        send_s, recv_s = refs[2 * n:]
        x, y, c, others = _place()
        sibling = (x, y, 1 - c)
        copies = []
        for a in range(n):
            for k, (ox, oy) in enumerate(others):
                mine = outs[a].at[2 * ox + oy, _half_rows(outs[a], c, 1)]
                copies.append(_remote(mine, mine, send_s.at[3 * a + k], recv_s.at[3 * a + k], sibling))
        for cp in copies:
            cp.start()
        for a in range(n):
            for k, (ox, oy) in enumerate(others):
                landed = outs[a].at[2 * ox + oy, _half_rows(outs[a], 1 - c, 1)]
                _remote(landed, landed, send_s.at[3 * a + k], recv_s.at[3 * a + k], sibling).wait_recv()
        for cp in copies:
            cp.wait_send()

    out_shape = [jax.ShapeDtypeStruct(g.shape, g.dtype) for g in lands]
    return _comm_call("forward_halves", body, lands, out_shape, [3 * n, 3 * n], aliases={a: a for a in range(n)})


def _forward_refs(bufs, incoming):
    x, y, c, others = _place()
    return (x, y, 1 - c), [b.at[2 * ox + oy, _half_rows(b, 1 - c if incoming else c, 1)]
                           for b in bufs for ox, oy in others]


def _forward_start(name, bufs, after):
    n = len(bufs)
    bufs = [pltpu.with_memory_space_constraint(b, pltpu.HBM) for b in bufs]

    def body(*refs):
        send_s, recv_s = refs[n + len(after)], refs[n + len(after) + 1]
        sibling, mine = _forward_refs(refs[:n], incoming=False)
        for i, ref in enumerate(mine):
            _remote(ref, ref, send_s.at[i], recv_s.at[i], sibling).start()
        refs[-1][...] = jnp.zeros_like(refs[-1])

    out = pl.pallas_call(
        body, name=name, in_specs=[HBM_SPEC] * n + [ANY] * len(after),
        out_specs=[SEM_SPEC, SEM_SPEC] + [HBM_SPEC] * n + [pl.BlockSpec(memory_space=pltpu.VMEM)],
        out_shape=[pltpu.SemaphoreType.DMA((3 * n,)), pltpu.SemaphoreType.DMA((3 * n,))]
        + [pltpu.HBM(b.shape, b.dtype) for b in bufs] + [jax.ShapeDtypeStruct((F32_ROWS, LANE), F32)],
        input_output_aliases={i: 2 + i for i in range(n)},
        compiler_params=pltpu.CompilerParams(has_side_effects=DATAFLOW),
    )(*bufs, *after)
    return dict(sems=out[:2], bufs=out[2:2 + n], token=out[-1])


def _forward_wait(name, started, after):
    n = len(started["bufs"])

    def body(*refs):
        send_s, recv_s = refs[n], refs[n + 1]
        sibling, mine = _forward_refs(refs[:n], incoming=False)
        _, theirs = _forward_refs(refs[:n], incoming=True)
        for i, (sent, landed) in enumerate(zip(mine, theirs)):
            _remote(sent, sent, send_s.at[i], recv_s.at[i], sibling).wait_send()
            _remote(landed, landed, send_s.at[i], recv_s.at[i], sibling).wait_recv()

    return pl.pallas_call(
        body, name=name, in_specs=[HBM_SPEC] * n + [SEM_SPEC, SEM_SPEC] + [ANY] * len(after),
        out_specs=[HBM_SPEC] * n, out_shape=[pltpu.HBM(b.shape, b.dtype) for b in started["bufs"]],
        input_output_aliases={i: i for i in range(n)},
        compiler_params=pltpu.CompilerParams(has_side_effects=DATAFLOW),
    )(*started["bufs"], *started["sems"], *after)


HBM_SPEC = pl.BlockSpec(memory_space=pltpu.HBM)
SEM_SPEC = pl.BlockSpec(memory_space=pltpu.SEMAPHORE)
DATAFLOW = pltpu.SideEffectType.DATAFLOW_SIDE_EFFECTING


def _exchange_plan(srcs, lands):
    x, y, c, _ = _place()
    plan = []
    for src, land in zip(srcs, lands):
        hr = src.shape[1] // 2
        plan.append((src.at[:, pl.ds(pl.multiple_of((1 - c) * hr, 8), hr)], land, (x, y, 1 - c)))
    return plan


def _scatter_plan(srcs, lands):
    x, y, c, others = _place()
    return [(src.at[2 * ox + oy], land.at[k], (ox, oy, c))
            for src, land in zip(srcs, lands) for k, (ox, oy) in enumerate(others)]


def _split_start(name, plan, srcs, land_shapes, n_copies, after=()):
    n = len(srcs)
    lands = [pltpu.with_memory_space_constraint(lax.empty(s.shape, s.dtype), pltpu.HBM) for s in land_shapes]
    srcs = [pltpu.with_memory_space_constraint(s, pltpu.HBM) for s in srcs]

    def body(*refs):
        send_s, recv_s = refs[2 * n + len(after)], refs[2 * n + len(after) + 1]
        for i, (src, dst, to) in enumerate(plan(refs[:n], refs[n:2 * n])):
            _remote(src, dst, send_s.at[i], recv_s.at[i], to).start()
        refs[-1][...] = jnp.zeros_like(refs[-1])

    thru = [pltpu.HBM(s.shape, s.dtype) for s in srcs + lands]
    out = pl.pallas_call(
        body, name=name, in_specs=[HBM_SPEC] * (2 * n) + [ANY] * len(after),
        out_specs=[SEM_SPEC, SEM_SPEC] + [HBM_SPEC] * (2 * n) + [pl.BlockSpec(memory_space=pltpu.VMEM)],
        out_shape=[pltpu.SemaphoreType.DMA((n_copies,)), pltpu.SemaphoreType.DMA((n_copies,))] + thru
        + [jax.ShapeDtypeStruct((F32_ROWS, LANE), F32)],
        input_output_aliases={i: 2 + i for i in range(2 * n)},
        compiler_params=pltpu.CompilerParams(has_side_effects=DATAFLOW),
    )(*srcs, *lands, *after)
    return dict(sems=out[:2], srcs=out[2:2 + n], lands=out[2 + n:2 + 2 * n], token=out[-1])


def _split_wait(name, plan, started, after):
    n = len(started["srcs"])
    after = list(after) if isinstance(after, (list, tuple)) else [after]

    def body(*refs):
        send_s, recv_s = refs[2 * n], refs[2 * n + 1]
        for i, (src, dst, to) in enumerate(plan(refs[:n], refs[n:2 * n])):
            cp = _remote(src, dst, send_s.at[i], recv_s.at[i], to)
            cp.wait_send()
            cp.wait_recv()

    arrs = list(started["srcs"]) + list(started["lands"])
    out = pl.pallas_call(
        body, name=name, in_specs=[HBM_SPEC] * (2 * n) + [SEM_SPEC, SEM_SPEC] + [ANY] * len(after),
        out_specs=[HBM_SPEC] * (2 * n), out_shape=[pltpu.HBM(s.shape, s.dtype) for s in arrs],
        input_output_aliases={i: i for i in range(2 * n)},
        compiler_params=pltpu.CompilerParams(has_side_effects=DATAFLOW),
    )(*arrs, *started["sems"], *after)
    return out[:n], out[n:]


def _join_halves(name, rs):
    n = len(rs)

    def body(*refs):
        outs = refs[n:2 * n]
        send_s, recv_s = refs[2 * n:]
        x, y, c, _ = _place()
        sibling = (x, y, 1 - c)

        def half(a, of_c):
            hr = outs[a].shape[1] // 2
            return outs[a].at[:, pl.ds(pl.multiple_of(of_c * hr, 8), hr)]

        copies = [_remote(half(a, c), half(a, c), send_s.at[a], recv_s.at[a], sibling) for a in range(n)]
        for cp in copies:
            cp.start()
        for a in range(n):
            landed = half(a, 1 - c)
            _remote(landed, landed, send_s.at[a], recv_s.at[a], sibling).wait_recv()
        for cp in copies:
            cp.wait_send()

    out_shape = [jax.ShapeDtypeStruct(r.shape, r.dtype) for r in rs]
    return _comm_call(name, body, rs, out_shape, [n, n], aliases={a: a for a in range(n)})


def _allreduce_small(buf, after=()):
    r = buf.shape[0]
    hr = r // 2

    def body(in_ref, *refs):
        out_ref, theirs, by_chip, send_s, recv_s = refs[len(after):]
        x, y, c, others = _place()
        chip = 2 * x + y
        sibling = (x, y, 1 - c)
        mine = pl.ds(pl.multiple_of(c * hr, F32_ROWS), hr)
        swap = _remote(in_ref, theirs, send_s.at[0], recv_s.at[0], sibling)
        swap.start()
        swap.wait()
        by_chip[chip] = in_ref[mine, :] + theirs[mine, :]
        sends = [_remote(by_chip.at[chip], by_chip.at[chip], send_s.at[1 + k], recv_s.at[1 + k], (ox, oy, c))
                 for k, (ox, oy) in enumerate(others)]
        for cp in sends:
            cp.start()
        for k, (ox, oy) in enumerate(others):
            landed = by_chip.at[2 * ox + oy]
            _remote(landed, landed, send_s.at[1 + k], recv_s.at[1 + k], (ox, oy, c)).wait_recv()
        for cp in sends:
            cp.wait_send()
        out_ref[mine, :] = (by_chip[0] + by_chip[1]) + (by_chip[2] + by_chip[3])
        back = _remote(out_ref.at[mine], out_ref.at[mine], send_s.at[NCHIP], recv_s.at[NCHIP], sibling)
        back.start()
        other = out_ref.at[pl.ds(pl.multiple_of((1 - c) * hr, F32_ROWS), hr)]
        _remote(other, other, send_s.at[NCHIP], recv_s.at[NCHIP], sibling).wait_recv()
        back.wait_send()

    vm = pl.BlockSpec(memory_space=pltpu.VMEM)
    return pl.pallas_call(
        body, name="allreduce_small", in_specs=[vm] + [ANY] * len(after), out_specs=vm,
        out_shape=jax.ShapeDtypeStruct((r, LANE), F32),
        scratch_shapes=[pltpu.VMEM((r, LANE), F32), pltpu.VMEM((NCHIP, hr, LANE), F32),
                        pltpu.SemaphoreType.DMA((NCHIP + 1,)), pltpu.SemaphoreType.DMA((NCHIP + 1,))],
        compiler_params=pltpu.CompilerParams(has_side_effects=True, vmem_limit_bytes=VMEM_LIMIT),
    )(buf, *after)


MAX_ROW_TILE = 512
BF16_ROWS = 16


def _row_tile(rows):
    for t in range(min(rows, MAX_ROW_TILE) // BF16_ROWS * BF16_ROWS, 0, -BF16_ROWS):
        if rows % t == 0:
            return t
    raise ValueError(rows)


def _sum_halves(g, theirs, c_arr):
    nch, rows, cols = g.shape
    hr = rows // 2
    tr = _row_tile(hr)

    def body(c_ref, g_ref, t_ref, o_ref, ob_ref):
        s = g_ref[...] + t_ref[...]
        o_ref[...] = s
        ob_ref[...] = s.astype(BF16)

    blk = pl.BlockSpec((None, tr, cols), lambda j, i, c_ref: (j, i, 0))
    return pl.pallas_call(
        body, name="sum_halves",
        grid_spec=pltpu.PrefetchScalarGridSpec(
            num_scalar_prefetch=1, grid=(nch, hr // tr),
            in_specs=[pl.BlockSpec((None, None, tr, cols), lambda j, i, c_ref: (j, c_ref[0], i, 0)), blk],
            out_specs=[blk, blk]),
        out_shape=[jax.ShapeDtypeStruct((nch, hr, cols), F32), jax.ShapeDtypeStruct((nch, hr, cols), BF16)],
        compiler_params=_cparams(("parallel", "parallel")),
    )(c_arr, g.reshape(nch, 2, hr, cols), theirs)


def _sum_halves_w_in(g, theirs, c_arr):
    hr = D // 2
    sh = IN_DIM // NCHIP

    def body(c_ref, g_ref, t_ref, o_ref, ob_ref):
        s = g_ref[...] + t_ref[...]
        for j, dst, src, width in W_IN_PIECES:
            o_ref[j, :, dst:dst + width] = s[:, src:src + width]
            ob_ref[j, :, dst:dst + width] = s[:, src:src + width].astype(BF16)

    out = pl.BlockSpec((NCHIP, WT, sh), lambda i, c_ref: (0, i, 0))
    return pl.pallas_call(
        body, name="sum_halves_w_in",
        grid_spec=pltpu.PrefetchScalarGridSpec(
            num_scalar_prefetch=1, grid=(hr // WT,),
            in_specs=[pl.BlockSpec((None, WT, NP), lambda i, c_ref: (c_ref[0], i, 0)),
                      pl.BlockSpec((None, WT, NP), lambda i, c_ref: (0, i, 0))],
            out_specs=[out, out]),
        out_shape=[jax.ShapeDtypeStruct((NCHIP, hr, sh), F32), jax.ShapeDtypeStruct((NCHIP, hr, sh), BF16)],
        compiler_params=_cparams(("parallel",)),
    )(c_arr, g.reshape(2, hr, NP), theirs)


def _sum_chips(p, q, place, l, into=None, after=()):
    extra = ([into] if into is not None else []) + list(after)
    _, rows, cols = p.shape
    tr = _row_tile(rows)
    steps = rows // tr

    def body(place_ref, p_ref, q0, q1, q2, *rest):
        rest[-1][...] = ((p_ref[...] + q0[...].astype(F32)) + q1[...].astype(F32)) + q2[...].astype(F32)

    qs = lambda k: pl.BlockSpec((None, tr, cols), lambda i, place_ref: (k, i, 0))
    return pl.pallas_call(
        body, name="sum_chips",
        grid_spec=pltpu.PrefetchScalarGridSpec(
            num_scalar_prefetch=1, grid=(steps,),
            in_specs=[pl.BlockSpec((None, tr, cols), lambda i, place_ref: (place_ref[0], i, 0)), qs(0), qs(1), qs(2)]
            + [ANY] * len(extra),
            out_specs=pl.BlockSpec((None, tr, cols), lambda i, place_ref: (l, place_ref[1] * steps + i, 0))),
        out_shape=jax.ShapeDtypeStruct((DEPTH, 2 * rows, cols), F32),
        input_output_aliases={5: 0} if into is not None else {},
        compiler_params=_cparams(("parallel",)),
    )(place, p, q, q, q, *extra)


def _adamw_fn(w, g, m, v):
    nm = ADAM_B1 * m + (1.0 - ADAM_B1) * g
    nv = ADAM_B2 * v + (1.0 - ADAM_B2) * jnp.square(g)
    m_hat = nm / (1.0 - ADAM_B1 ** ADAM_STEP)
    v_hat = nv / (1.0 - ADAM_B2 ** ADAM_STEP)
    return -ADAM_LR * (m_hat / (jnp.sqrt(v_hat) + ADAM_EPS) + ADAM_WD * w), nm, nv


def _adamw(w, g, m, v):
    layers, rows, cols = w.shape
    tr = _row_tile(rows)

    def body(w_ref, g_ref, m_ref, v_ref, d_ref, nm_ref, nv_ref):
        d_ref[...], nm_ref[...], nv_ref[...] = _adamw_fn(w_ref[...], g_ref[...], m_ref[...], v_ref[...])

    blk = pl.BlockSpec((None, tr, cols), lambda l, i: (l, i, 0))
    return pl.pallas_call(
        body, name="adamw", grid=(layers, rows // tr), in_specs=[blk] * 4, out_specs=[blk] * 3,
        out_shape=[jax.ShapeDtypeStruct(w.shape, F32)] * 3, compiler_params=_cparams(("parallel", "parallel")),
    )(w, g, m, v)


def _adamw_small(ws, gs, ms, vs):
    n = len(ws)

    def body(*refs):
        for i in range(n):
            w_ref, g_ref, m_ref, v_ref, d_ref, nm_ref, nv_ref = (refs[k * n + i] for k in range(7))
            d_ref[...], nm_ref[...], nv_ref[...] = _adamw_fn(w_ref[...], g_ref[...], m_ref[...], v_ref[...])

    vm = pl.BlockSpec(memory_space=pltpu.VMEM)
    out = pl.pallas_call(
        body, name="adamw_small", in_specs=[vm] * (4 * n), out_specs=[vm] * (3 * n),
        out_shape=[jax.ShapeDtypeStruct(a.shape, F32) for a in list(ws) * 3],
        compiler_params=pltpu.CompilerParams(vmem_limit_bytes=VMEM_LIMIT),
    )(*ws, *gs, *ms, *vs)
    return out[:n], out[n:2 * n], out[2 * n:]


BIG = ("w_in", "w_out", "w_gate", "w_up", "w_down")
SMALL = ("norm_mix", "a_log", "dt_bias", "o_norm_g", "ln_v_g", "ln_v_b", "w_s", "b_s", "norm_ffn", "norm_final")
ORDER = ("norm_mix", "w_in", "conv_w", "a_log", "dt_bias", "o_norm_g", "ln_v_g", "ln_v_b", "w_s", "b_s", "w_out",
         "norm_ffn", "w_gate", "w_up", "w_down", "norm_final")


F32_ROWS = 8
PACK_ROWS = 128


def _lane_rows(size):
    return -(-size // (F32_ROWS * LANE)) * F32_ROWS


def _pack(arrs):
    parts = [jnp.pad(a.reshape(-1), (0, _lane_rows(a.size) * LANE - a.size)).reshape(-1, LANE) for a in arrs]
    rows = sum(p.shape[0] for p in parts)
    if rows % PACK_ROWS:
        parts.append(jnp.zeros((-rows % PACK_ROWS, LANE), F32))
    return jnp.concatenate(parts, axis=0)


def _unpack(buf, like):
    out, row = [], 0
    for a in like:
        n = _lane_rows(a.size)
        out.append(buf[row:row + n].reshape(-1)[:a.size].reshape(a.shape))
        row += n
    return out


def kernel(x, norm_mix, w_in, conv_w, a_log, dt_bias, o_norm_g, ln_v_g, ln_v_b, w_s, b_s, w_out, norm_ffn, w_gate, w_up, w_down, norm_final, loss_target, m_norm_mix, m_w_in, m_conv_w, m_a_log, m_dt_bias, m_o_norm_g, m_ln_v_g, m_ln_v_b, m_w_s, m_b_s, m_w_out, m_norm_ffn, m_w_gate, m_w_up, m_w_down, m_norm_final, v_norm_mix, v_w_in, v_conv_w, v_a_log, v_dt_bias, v_o_norm_g, v_ln_v_g, v_ln_v_b, v_w_s, v_b_s, v_w_out, v_norm_ffn, v_w_gate, v_w_up, v_w_down, v_norm_final):
    w = dict(norm_mix=norm_mix, w_in=w_in, conv_w=conv_w, a_log=a_log, dt_bias=dt_bias, o_norm_g=o_norm_g,
             ln_v_g=ln_v_g, ln_v_b=ln_v_b, w_s=w_s, b_s=b_s, w_out=w_out, norm_ffn=norm_ffn, w_gate=w_gate, w_up=w_up,
             w_down=w_down, norm_final=norm_final)
    m = dict(norm_mix=m_norm_mix, w_in=m_w_in, conv_w=m_conv_w, a_log=m_a_log, dt_bias=m_dt_bias, o_norm_g=m_o_norm_g,
             ln_v_g=m_ln_v_g, ln_v_b=m_ln_v_b, w_s=m_w_s, b_s=m_b_s, w_out=m_w_out, norm_ffn=m_norm_ffn,
             w_gate=m_w_gate, w_up=m_w_up, w_down=m_w_down, norm_final=m_norm_final)
    v = dict(norm_mix=v_norm_mix, w_in=v_w_in, conv_w=v_conv_w, a_log=v_a_log, dt_bias=v_dt_bias, o_norm_g=v_o_norm_g,
             ln_v_g=v_ln_v_g, ln_v_b=v_ln_v_b, w_s=v_w_s, b_s=v_b_s, w_out=v_w_out, norm_ffn=v_norm_ffn,
             w_gate=v_w_gate, w_up=v_w_up, w_down=v_w_down, norm_final=v_norm_final)
    chip = 2 * lax.axis_index("x") + lax.axis_index("y")
    place = jnp.stack([chip, lax.axis_index("c")]).astype(jnp.int32)
    c_arr = place[1:]

    def kernel_view(n, a):
        return jnp.swapaxes(a, 1, 2) if n in ("w_gate", "w_up") else a

    own = {n: [kernel_view(n, w[n])[l].astype(BF16) for l in range(DEPTH)] for n in BIG}
    by_chip = lambda a: jax.ShapeDtypeStruct((NCHIP,) + a.shape, a.dtype)

    def start(name, srcs, whole, after=()):
        return _split_start(name, _gather_plan(whole), srcs, [by_chip(a) for a in srcs], 3 * len(srcs), after)

    def finish(name, started, whole, after):
        srcs, lands = _split_wait(name, _gather_plan(whole), started, after)
        passed = iter(_forward_halves([g for g, all_of_it in zip(lands, whole) if not all_of_it]))
        lands = [g if all_of_it else next(passed) for g, all_of_it in zip(lands, whole)]
        return srcs, [lax.dynamic_update_index_in_dim(g, o, chip, 0) for g, o in zip(lands, srcs)]

    ffn = BIG[1:]
    first = start("gather_first_start", [own["w_in"][0], conv_w], [False, True])
    early = start("gather_early_start", [own[n][0] for n in ffn], [False] * len(ffn), [first["token"]])
    mid = start("gather_mid_start", [own["w_in"][1]], [False], [early["token"]])
    later = start("gather_later_start", [own[n][1] for n in ffn], [False] * len(ffn), [mid["token"]])
    hn = _rmsnorm("rms_mix", x[0], norm_mix[0][None])
    (own_w_in, _), (w_in_by_chip, conv_by_chip) = finish("gather_first_wait", first, [False, True], [later["token"], hn])

    passing = {}

    def pass_on(tag, started, n):
        def at(after):
            srcs, lands = _split_wait(f"gather_{tag}_wait", _gather_plan([False] * n), started, after)
            passing[tag] = srcs, _forward_start(f"forward_{tag}_start", lands, ())
            return [passing[tag][1]["token"]]
        return at

    def passed_on(tag, after):
        srcs, fwd = passing[tag]
        lands = _forward_wait(f"forward_{tag}_wait", fwd, [after])
        return srcs, [lax.dynamic_update_index_in_dim(g, o, chip, 0) for g, o in zip(lands, srcs)]

    def late(tag):
        return lambda after: dict(zip(ffn, passed_on(tag, after)[1]))

    layer0 = _layer_params(0, dict(
        hn=hn, w_in=_assemble_w_in(w_in_by_chip, own_w_in, place), conv_w=conv_by_chip, late=late("early"),
        before_mix=pass_on("early", early, len(ffn)), before_ffn_out=pass_on("mid", mid, 1)), w)

    def layer1(after):
        (own_w_in1,), (w_in1_by_chip,) = passed_on("mid", after)
        return _layer_params(1, dict(w_in=_assemble_w_in(w_in1_by_chip, own_w_in1, place), conv_w=conv_by_chip,
                                     late=late("later"), before_mix=pass_on("later", later, len(ffn))), w)

    saved, layers, loss_lanes, dh, dhb, d_norm_final = _forward(x[0], loss_target[0], [layer0, layer1],
                                                                 norm_final[None])

    sums, arrived = {}, {}

    def exchange_start(tag, l, names, grads, after=()):
        mine = [grads[n] for n in names]
        shapes = [jax.ShapeDtypeStruct((g.shape[0], g.shape[1] // 2, g.shape[2]), F32) for g in mine]
        return tag, l, names, _split_start(f"exchange_{tag}_start", _exchange_plan, mine, shapes, len(mine), after)

    def add_halves(l, names, mine, theirs):
        for n, g, t in zip(names, mine, theirs):
            sums[l, n] = (_sum_halves_w_in if n == "w_in" else _sum_halves)(g, t, c_arr)

    def exchange_wait(handle, after):
        tag, l, names, started = handle
        add_halves(l, names, *_split_wait(f"exchange_{tag}_wait", _exchange_plan, started, after))

    def scatter_start(tag, l, names, after=()):
        partial = [sums[l, n][1] for n in names]
        shapes = [jax.ShapeDtypeStruct((3,) + p.shape[1:], p.dtype) for p in partial]
        return tag, l, names, _split_start(f"scatter_{tag}_start", _scatter_plan, partial, shapes, 3 * len(names), after)

    def scatter_wait(handle, after):
        tag, l, names, started = handle
        for n, q in zip(names, _split_wait(f"scatter_{tag}_wait", _scatter_plan, started, after)[1]):
            arrived[l, n] = q

    last = DEPTH - 1
    swiglu = BIG[2:]
    dh1, dh1b, g_ffn = _layer_bwd_ffn(dh, dhb, layers[last], saved[last])
    dh, dhb, g_mix = _layer_bwd_mixer(dh1, dh1b, layers[last], saved[last])
    gl = [None, _reference_layout({**g_ffn, **g_mix})]
    ex_last = exchange_start("last", last, BIG, gl[last])
    dh1, dh1b, g_ffn = _layer_bwd_ffn(dh, dhb, layers[0], saved[0], after=[ex_last[-1]["token"]])
    exchange_wait(ex_last, dh1)
    sc_last = scatter_start("last", last, BIG)
    ex_ffn = exchange_start("swiglu", 0, swiglu, g_ffn, [sc_last[-1]["token"]])
    sc_ffn = []

    def midway(do):
        exchange_wait(ex_ffn, do)
        sc_ffn.append(scatter_start("swiglu", 0, swiglu))
        return [sc_ffn[0][-1]["token"]]

    ex_rest = []

    def late(grads):
        rest_grads = dict(w_in=grads["w_in"], w_out=grads["w_out"].reshape(NCHIP, D // NCHIP, D))
        ex_rest.append(exchange_start("rest", 0, BIG[:2], rest_grads))
        return [ex_rest[0][-1]["token"]]

    dx, _, g_mix = _layer_bwd_mixer(dh1, dh1b, layers[0], saved[0], after=[ex_ffn[-1]["token"]], midway=midway,
                                    late=late)
    scatter_wait(sc_last, dx)
    scatter_wait(sc_ffn[0], dx)
    gl[0] = _reference_layout({**g_ffn, **g_mix})

    small_g = [jnp.stack([gl[l][n] for l in range(DEPTH)]) for n in SMALL[:-1]] + [d_norm_final[0]]
    conv_g = jnp.stack([gl[l]["conv_w"] for l in range(DEPTH)])
    summed = small_g + [conv_g, loss_lanes[0, :1]]
    total = _allreduce_small(_pack(summed))
    exchange_wait(ex_rest[0], total)
    sc_rest = scatter_start("rest", 0, BIG[:2])

    travelling = [sc_rest[-1]["token"]]
    reduced, g_out, delta, new_m, new_v = {}, {}, {}, {}, {}

    done = []

    def adamw_large(names, joined):
        for n, g in zip(names, joined):
            res = _adamw(kernel_view(n, w[n]), g, kernel_view(n, m[n]), kernel_view(n, v[n]))
            done.append(res[2])
            g_out[n], delta[n], new_m[n], new_v[n] = (kernel_view(n, a) for a in (g,) + tuple(res))

    for n in BIG:
        for l in (range(DEPTH) if n in swiglu else [last]):
            reduced[n] = _sum_chips(sums[l, n][0], arrived[l, n], place, l, into=reduced.get(n), after=travelling)
    adamw_large(swiglu, _join_halves("join_swiglu", [reduced[n] for n in swiglu]))
    scatter_wait(sc_rest, done + [reduced[n] for n in BIG[:2]])
    for n in BIG[:2]:
        reduced[n] = _sum_chips(sums[0, n][0], arrived[0, n], place, 0, into=reduced[n])
    adamw_large(BIG[:2], _join_halves("join_rest", [reduced[n] for n in BIG[:2]]))
    *small_r, conv_r, loss = _unpack(total, summed)
    g_out.update(zip(SMALL, small_r))
    g_out["conv_w"] = lax.dynamic_slice_in_dim(conv_r, chip * conv_w.shape[2], conv_w.shape[2], axis=2)

    rest = SMALL + ("conv_w",)
    rows_of = lambda a: a.reshape(1, -1) if a.ndim == 1 else a
    results = _adamw_small(*[[rows_of(src[n]) for n in rest] for src in (w, g_out, m, v)])
    for dst, arrs in zip((delta, new_m, new_v), results):
        dst.update({n: a.reshape(w[n].shape) for n, a in zip(rest, arrs)})

    return (loss[0], dx[None], *[g_out[n] for n in ORDER], *[delta[n] for n in ORDER], *[new_m[n] for n in ORDER],
            *[new_v[n] for n in ORDER])
```

```python
import functools

import jax
import jax.numpy as jnp
from jax import lax
from jax.experimental import pallas as pl
from jax.experimental.pallas import tpu as pltpu

F32 = jnp.float32
BF16 = jnp.bfloat16
MESH = pl.DeviceIdType.MESH
ANY = pl.BlockSpec(memory_space=pl.ANY)
HIGHEST = lax.Precision.HIGHEST

T = 2048
D = 1024
DEPTH = 2
NCHIP = 4
HEADS = 4
HD = 128
HW = HEADS * HD
CH = 64
GCH = 128
IN_DIM = 3080
NP = 3200
BA_OFF = 3072
FF_SH = 704
EPS = 1e-6
LANE = 128
VMEM_LIMIT = 56 * 1024 * 1024

ADAM_LR = 0.001
ADAM_B1 = 0.9
ADAM_B2 = 0.999
ADAM_EPS = 1e-08
ADAM_WD = 0.01
ADAM_STEP = 10


def _cparams(sem=None):
    return pltpu.CompilerParams(dimension_semantics=sem, vmem_limit_bytes=VMEM_LIMIT)


_DIMS = {"nn": (((1,), (0,)), ((), ())), "nt": (((1,), (1,)), ((), ())), "tn": (((0,), (0,)), ((), ()))}


def _mm(name, mode, a, bs, *, tm, tn, tk, out_dtypes=(F32,), reduce_g=False, resid=None, extras=(), epilogue=None,
        after=(), fold_g=False, sub_m=1):
    assert sub_m == 1 or (mode != "tn" and tm % (8 * sub_m) == 0), (name, sub_m)
    nb = len(bs)
    ga = a.shape[0]
    gbs = [b.shape[0] for b in bs]
    g_n = max([ga] + gbs)
    if mode == "tn":
        k_n, m_n = a.shape[1:]
    else:
        m_n, k_n = a.shape[1:]
    n_n = bs[0].shape[1] if mode == "nt" else bs[0].shape[2]
    assert m_n % tm == 0 and n_n % tn == 0 and k_n % tk == 0, (name, m_n, n_n, k_n)
    mi, nj, kk = m_n // tm, n_n // tn, k_n // tk
    lead = g_n if fold_g else None
    g_steps = 1 if fold_g else g_n
    grid = (mi, nj, g_steps, kk)
    ids = lambda i, j, g, k: (g, i, j, k)
    n_red = (g_steps if reduce_g else 1) * kk
    red_idx = lambda: (pl.program_id(2) * kk if reduce_g else 0) + pl.program_id(3)
    sem = ("parallel", "parallel", "arbitrary" if reduce_g else "parallel", "arbitrary")

    def pick(gsz, g):
        return g if gsz > 1 else 0

    def a_map(*p):
        g, i, j, k = ids(*p)
        return (pick(ga, g), k, i) if mode == "tn" else (pick(ga, g), i, k)

    def b_map(gsz):
        def f(*p):
            g, i, j, k = ids(*p)
            return (pick(gsz, g), j, k) if mode == "nt" else (pick(gsz, g), k, j)
        return f

    def o_map(gsz):
        def f(*p):
            g, i, j, k = ids(*p)
            return (0 if reduce_g else pick(gsz, g), i, j)
        return f

    a_spec = pl.BlockSpec((lead, tk, tm) if mode == "tn" else (lead, tm, tk), a_map)
    b_specs = [pl.BlockSpec((lead, tn, tk) if mode == "nt" else (lead, tk, tn), b_map(gs)) for gs in gbs]
    x_specs = [pl.BlockSpec((None, tm, tn), o_map(e.shape[0])) for e in extras]
    r_specs = [pl.BlockSpec((None, tm, tn), o_map(resid.shape[0]))] if resid is not None else []
    g_out = 1 if reduce_g else g_n
    out_shape = [jax.ShapeDtypeStruct((g_out, m_n, n_n), dt) for dt in out_dtypes]
    out_specs = [pl.BlockSpec((None, tm, tn), o_map(g_out)) for _ in out_dtypes]
    nx, nr, no = len(extras), len(r_specs), len(out_dtypes)
    n_in = 1 + nb + nx + nr + len(after)
    dims = _DIMS[mode]

    def body(*refs):
        a_ref = refs[0]
        b_refs = refs[1:1 + nb]
        x_refs = refs[1 + nb:1 + nb + nx]
        r_refs = refs[1 + nb + nx:1 + nb + nx + nr]
        o_refs = refs[n_in:n_in + no]
        acc_refs = refs[n_in + no:]
        def dots(rows):
            if fold_g:
                return [sum(lax.dot_general(a_ref[g, rows, :], b_ref[g], dims, preferred_element_type=F32)
                            for g in range(g_n)) for b_ref in b_refs]
            av = a_ref[...] if mode == "tn" else a_ref[rows, :]
            return [lax.dot_general(av, b_ref[...], dims, preferred_element_type=F32) for b_ref in b_refs]

        def finish(accs, rows=slice(None)):
            if r_refs:
                accs[0] = accs[0] + r_refs[0][rows, :]
            outs = epilogue(accs, [x[rows, :] for x in x_refs]) if epilogue is not None else accs
            for o_ref, o in zip(o_refs, outs):
                o_ref[rows, :] = o.astype(o_ref.dtype)

        if n_red == 1:
            slabs = [slice(s * (tm // sub_m), (s + 1) * (tm // sub_m)) for s in range(sub_m)]
            ahead = dots(slabs[0])
            for s, rows in enumerate(slabs):
                now, ahead = ahead, (dots(slabs[s + 1]) if s + 1 < sub_m else None)
                finish(now, rows)
            return
        products = dots(slice(None))
        r = red_idx()
        for p, acc in zip(products, acc_refs):
            @pl.when(r == 0)
            def _():
                acc[...] = p

            @pl.when((r > 0) & (r < n_red - 1))
            def _():
                acc[...] += p

        @pl.when(r == n_red - 1)
        def _():
            finish([acc[...] + p for p, acc in zip(products, acc_refs)])

    return pl.pallas_call(
        body, name=name, grid=grid,
        in_specs=[a_spec] + b_specs + x_specs + r_specs + [ANY] * len(after),
        out_specs=out_specs, out_shape=out_shape,
        scratch_shapes=[pltpu.VMEM((tm, tn), F32) for _ in range(nb if n_red > 1 else 0)],
        compiler_params=_cparams(sem),
    )(a, *bs, *extras, *([resid] if resid is not None else []), *after)


def _sigmoid(x):
    return 1.0 / (1.0 + jnp.exp(-x))


def _silu(x):
    return x * _sigmoid(x)


def _gelu(x):
    return 0.5 * x * (1.0 + jnp.tanh(0.7978845608028654 * (x + 0.044715 * (x * x * x))))


def _rms_fn(h, gain):
    return h * lax.rsqrt(jnp.mean(h * h, axis=-1, keepdims=True) + EPS) * gain


def _shift_impl(x, s):
    n = x.shape[0]
    rolled = pltpu.roll(x, s % n, 0)
    row = lax.broadcasted_iota(jnp.int32, x.shape, 0)
    return jnp.where((row >= s) & (row < n + s), rolled, 0.0)


@functools.partial(jax.custom_vjp, nondiff_argnums=(1,))
def _shift(x, s):
    return _shift_impl(x, s)


def _shift_fwd(x, s):
    return _shift_impl(x, s), None


def _shift_bwd(s, _, g):
    return (_shift_impl(g, -s),)


_shift.defvjp(_shift_fwd, _shift_bwd)


def _prep_fn(x, w, qk_scale, is_v):
    y = x * w[3:4, :]
    for i in range(3):
        y = y + _shift(x, 3 - i) * w[i:i + 1, :]
    y = _silu(y)
    nrm = lax.rsqrt(jnp.sum(y * y, axis=-1, keepdims=True) + EPS) * qk_scale
    return y * jnp.where(is_v, 1.0, nrm)


def _softplus(x):
    return jnp.maximum(x, 0.0) + jnp.log(1.0 + jnp.exp(-jnp.abs(x)))


def _gates_fn(ba, a_log, dt_bias):
    lane = lax.broadcasted_iota(jnp.int32, ba.shape, 1)
    beta = _sigmoid(ba)
    g = -jnp.exp(a_log) * _softplus(ba + dt_bias)
    return jnp.where(lane < HEADS, beta, g)


def _dot16(a, b, dims=_DIMS["nn"]):
    return lax.dot_general(a.astype(BF16), b.astype(BF16), dims, preferred_element_type=F32)


def _dot32(a, b):
    return jnp.dot(a, b, preferred_element_type=F32, precision=HIGHEST)


def _dot3(a, b, dims=_DIMS["nn"]):
    return lax.dot_general(a, b, dims, preferred_element_type=F32, precision=lax.Precision.HIGH)


def _tri_inverses(mats, tick=lambda: None):
    row = lax.broadcasted_iota(jnp.int32, (CH, CH), 0)
    col = lax.broadcasted_iota(jnp.int32, (CH, CH), 1)
    eye = (row == col).astype(F32)
    ts = [eye - a for a in mats]
    ps = list(mats)
    for _ in range(5):
        ps = [_dot3(p, p) for p in ps]
        tick()
        ts = [t + _dot3(t, p) for t, p in zip(ts, ps)]
        tick()
    return ts


@jax.custom_vjp
def _tri_solves(mats, rhs):
    return [_dot3(t, b) for t, b in zip(_tri_inverses(mats), rhs)]


def _tri_solves_fwd(mats, rhs):
    ts = _tri_inverses(mats)
    xs = [_dot3(t, b) for t, b in zip(ts, rhs)]
    return xs, (ts, xs)


def _tri_solves_bwd(res, dxs):
    ts, xs = res
    dbs = [_dot3(t, dx, _DIMS["tn"]) for t, dx in zip(ts, dxs)]
    return [-_dot3(db, x, _DIMS["nt"]) for db, x in zip(dbs, xs)], dbs


_tri_solves.defvjp(_tri_solves_fwd, _tri_solves_bwd)


def _chunk_prep_fn(xs, bgs, tick=None):
    step = tick or (lambda: None)
    row = lax.broadcasted_iota(jnp.int32, (CH, CH), 0)
    col = lax.broadcasted_iota(jnp.int32, (CH, CH), 1)
    incl = row >= col
    strict = row > col
    lmat = incl.astype(F32)
    n = len(xs)
    items = [(i, h) for i in range(n) for h in range(HEADS)]
    part = lambda i, h, c: xs[i][:, c * HW + h * HD:c * HW + (h + 1) * HD]
    q = [part(i, h, 0) for i, h in items]
    k = [part(i, h, 1) for i, h in items]
    v = [part(i, h, 2) for i, h in items]
    beta = [bgs[i][:, h:h + 1] for i, h in items]
    gc_all = [_dot32(lmat, bg) for bg in bgs]
    step()
    gc = [gc_all[i][:, HEADS + h:HEADS + h + 1] for i, h in items]
    gmat = [jnp.where(strict, jnp.broadcast_to(bgs[i][:, HEADS + h:HEADS + h + 1], (CH, CH)), 0.0) for i, h in items]
    diff = [_dot3(lmat, m) for m in gmat]
    step()
    decay = [jnp.where(incl, jnp.exp(jnp.where(incl, d, 0.0)), 0.0) for d in diff]
    k_beta = [kk * b for kk, b in zip(k, beta)]
    kk_t = [_dot16(kb, kk, _DIMS["nt"]) for kb, kk in zip(k_beta, k)]
    step()
    qk_t = [_dot16(qq, kk, _DIMS["nt"]) for qq, kk in zip(q, k)]
    step()
    a = [jnp.where(strict, m * d, 0.0) for m, d in zip(kk_t, decay)]
    eg = [jnp.exp(g) for g in gc]
    rhs = [jnp.concatenate([vv * b, kb * e], axis=-1) for vv, b, kb, e in zip(v, beta, k_beta, eg)]
    if tick is None:
        uw = _tri_solves(a, rhs)
    else:
        uw = [_dot3(t, b) for t, b in zip(_tri_inverses(a, tick), rhs)]
    qk = [m * d for m, d in zip(qk_t, decay)]
    g_last = [g[CH - 1:CH, :] for g in gc]
    qe = [qq * e for qq, e in zip(q, eg)]
    kd = [kk * jnp.exp(gl - g) for kk, gl, g in zip(k, g_last, gc)]
    egl = [jnp.broadcast_to(jnp.exp(gl), (1, HD)) for gl in g_last]
    out = []
    for i in range(n):
        mine = slice(i * HEADS, (i + 1) * HEADS)
        cat = lambda vals: jnp.concatenate(vals[mine], axis=-1)
        out.append((cat([x[:, :HD] for x in uw]), cat([x[:, HD:] for x in uw]), cat(qe), cat(kd),
                    jnp.concatenate([m[None] for m in qk[mine]], axis=0), cat(egl)))
    return out


def _state_levels(chunks, s, outs, befores, final):
    for u, w, qe, kd, qk, egl in chunks:
        befores.append(s)
        ws = [_dot16(a, b) for a, b in zip(w, s)]
        qs = [_dot16(a, b) for a, b in zip(qe, s)]
        yield
        v_new = [a - b for a, b in zip(u, ws)]
        outs.append([a + _dot16(b, c) for a, b, c in zip(qs, qk, v_new)])
        s = [a * e + _dot16(b, c, _DIMS["tn"]) for a, e, b, c in zip(s, egl, kd, v_new)]
        yield
    final.append(s)


def _chunk_state_fn(u, w, qe, kd, qk, egl, s):
    ws = [_dot16(a, b) for a, b in zip(w, s)]
    qs = [_dot16(a, b) for a, b in zip(qe, s)]
    v_new = [a - b for a, b in zip(u, ws)]
    o = [a + _dot16(b, c) for a, b, c in zip(qs, qk, v_new)]
    s_new = [a * e + _dot16(b, c, _DIMS["tn"]) for a, e, b, c in zip(s, egl, kd, v_new)]
    return o, s_new


def _mix_fn(o, z, ur, vr, ong, lng, lnb, ws, bst):
    row = lax.broadcasted_iota(jnp.int32, (GCH, GCH), 0)
    col = lax.broadcasted_iota(jnp.int32, (GCH, GCH), 1)
    causal = row >= col
    ug = _gelu(ur)
    vg = _gelu(vr)
    sls = [slice(h * HD, (h + 1) * HD) for h in range(HEADS)]
    oh = [o[:, sl] for sl in sls]
    oh = [x * lax.rsqrt(jnp.mean(x * x, axis=-1, keepdims=True) + EPS) for x in oh]
    outs_dn = [x * ong * _silu(z[:, sl]) for x, sl in zip(oh, sls)]
    vh = [vg[:, sl] for sl in sls]
    mu = [jnp.mean(x, axis=-1, keepdims=True) for x in vh]
    var = [jnp.mean(jnp.square(x - m), axis=-1, keepdims=True) for x, m in zip(vh, mu)]
    vn = [(x - m) * lax.rsqrt(s + EPS) * lng[:, sl] + lnb[:, sl] for x, m, s, sl in zip(vh, mu, var, sls)]
    mixed = [_dot16(jnp.where(causal, ws[h], 0.0), vn[h]) for h in range(HEADS)]
    outs_gm = [ug[:, sl] * (mixed[h] + bst[:, h:h + 1]) for h, sl in enumerate(sls)]
    return jnp.concatenate(outs_dn + outs_gm, axis=-1)


def _loss_fn(h, gain, tgt):
    y = _rms_fn(h, gain)
    return 0.5 * jnp.sum(jnp.mean(jnp.square(y - tgt), axis=-1))


RT = 512


def _rows(n=D):
    return pl.BlockSpec((RT, n), lambda i: (i, 0))


def _whole(shape):
    nd = len(shape)
    return pl.BlockSpec(shape, lambda i: (0,) * nd)


def _rmsnorm(name, h, gain):
    def body(h_ref, g_ref, o_ref):
        o_ref[...] = _rms_fn(h_ref[...], g_ref[...]).astype(BF16)

    return pl.pallas_call(
        body, name=name, grid=(T // RT,), in_specs=[_rows(), _whole((1, D))], out_specs=_rows(),
        out_shape=jax.ShapeDtypeStruct((T, D), BF16), compiler_params=_cparams(("parallel",)),
    )(h, gain)


def _rmsnorm_bwd(name, dhn, h, gain, resid):
    def body(dhn_ref, h_ref, g_ref, r_ref, dh_ref, dh16_ref, dg_ref):
        _, vjp = jax.vjp(_rms_fn, h_ref[...], g_ref[...])
        dh, dg = vjp(dhn_ref[...])
        dh = r_ref[...] + dh
        dh_ref[...] = dh
        dh16_ref[...] = dh.astype(BF16)

        @pl.when(pl.program_id(0) == 0)
        def _():
            dg_ref[...] = dg

        @pl.when(pl.program_id(0) > 0)
        def _():
            dg_ref[...] += dg

    return pl.pallas_call(
        body, name=name, grid=(T // RT,), in_specs=[_rows(), _rows(), _whole((1, D)), _rows()],
        out_specs=[_rows(), _rows(), _whole((1, D))],
        out_shape=[jax.ShapeDtypeStruct((T, D), F32), jax.ShapeDtypeStruct((T, D), BF16),
                   jax.ShapeDtypeStruct((1, D), F32)],
        compiler_params=_cparams(("arbitrary",)),
    )(dhn, h, gain, resid)


def _loss_head(h, gain, tgt):
    def body(h_ref, g_ref, t_ref, l_ref, dh_ref, dh16_ref, dg_ref):
        loss, vjp = jax.vjp(lambda hh, gg: _loss_fn(hh, gg, t_ref[...]), h_ref[...], g_ref[...])
        dh, dg = vjp(jnp.ones((), F32))
        dh_ref[...] = dh
        dh16_ref[...] = dh.astype(BF16)
        lv = jnp.full((1, LANE), loss, F32)

        @pl.when(pl.program_id(0) == 0)
        def _():
            dg_ref[...] = dg
            l_ref[...] = lv

        @pl.when(pl.program_id(0) > 0)
        def _():
            dg_ref[...] += dg
            l_ref[...] += lv

    return pl.pallas_call(
        body, name="loss_head", grid=(T // RT,), in_specs=[_rows(), _whole((1, D)), _rows()],
        out_specs=[_whole((1, LANE)), _rows(), _rows(), _whole((1, D))],
        out_shape=[jax.ShapeDtypeStruct((1, LANE), F32), jax.ShapeDtypeStruct((T, D), F32),
                   jax.ShapeDtypeStruct((T, D), BF16), jax.ShapeDtypeStruct((1, D), F32)],
        compiler_params=_cparams(("arbitrary",)),
    )(h, gain, tgt)


def _prep_flags():
    j = pl.program_id(0)
    qk_scale = jnp.where(j < HEADS, HD ** -0.5, 1.0).astype(F32)
    return qk_scale, j >= 2 * HEADS


def _prep(proj, conv_w):
    def body(x_ref, w_ref, o_ref):
        qk_scale, is_v = _prep_flags()
        o_ref[...] = _prep_fn(x_ref[...], w_ref[...], qk_scale, is_v)

    col = lambda j: (0, j)
    return pl.pallas_call(
        body, name="gdn_prep", grid=(3 * HEADS,),
        in_specs=[pl.BlockSpec((T, HD), col), pl.BlockSpec((4, HD), col)], out_specs=pl.BlockSpec((T, HD), col),
        out_shape=jax.ShapeDtypeStruct((T, 3 * HW), F32), compiler_params=_cparams(("parallel",)),
    )(proj, conv_w)


def _prep_bwd(proj, conv_w, dqkv, dproj):
    def body(x_ref, w_ref, d_ref, _, dx_ref, dw_ref):
        qk_scale, is_v = _prep_flags()
        _, vjp = jax.vjp(lambda x, w: _prep_fn(x, w, qk_scale, is_v), x_ref[...], w_ref[...])
        dx, dw = vjp(d_ref[...])
        dx_ref[...] = dx.astype(BF16)
        dw_ref[...] = dw

    col = lambda j: (0, j)
    return pl.pallas_call(
        body, name="gdn_prep_bwd", grid=(3 * HEADS,),
        in_specs=[pl.BlockSpec((T, HD), col), pl.BlockSpec((4, HD), col), pl.BlockSpec((T, HD), col), ANY],
        out_specs=[pl.BlockSpec((T, HD), col), pl.BlockSpec((4, HD), col)],
        out_shape=[jax.ShapeDtypeStruct((T, NP), BF16), jax.ShapeDtypeStruct((4, 3 * HW), F32)],
        input_output_aliases={3: 0}, compiler_params=_cparams(("parallel",)),
    )(proj, conv_w, dqkv, dproj)


BA_BLK = BA_OFF // LANE


def _gates(proj, a_log, dt_bias):
    def body(x_ref, a_ref, d_ref, o_ref):
        o_ref[...] = _gates_fn(x_ref[...], a_ref[...], d_ref[...])

    return pl.pallas_call(
        body, name="gdn_gates", grid=(1,),
        in_specs=[pl.BlockSpec((T, LANE), lambda i: (0, BA_BLK)), _whole((1, LANE)), _whole((1, LANE))],
        out_specs=_whole((T, LANE)),
        out_shape=jax.ShapeDtypeStruct((T, LANE), F32), compiler_params=_cparams(("arbitrary",)),
    )(proj, a_log, dt_bias)


def _gates_bwd(proj, a_log, dt_bias, dbg, dproj):
    def body(x_ref, a_ref, d_ref, dbg_ref, _, dx_ref, da_ref, dd_ref):
        _, vjp = jax.vjp(_gates_fn, x_ref[...], a_ref[...], d_ref[...])
        dx, da_ref[...], dd_ref[...] = vjp(dbg_ref[...])
        dx_ref[...] = dx.astype(BF16)

    ba = pl.BlockSpec((T, LANE), lambda i: (0, BA_BLK))
    return pl.pallas_call(
        body, name="gdn_gates_bwd", grid=(1,),
        in_specs=[ba, _whole((1, LANE)), _whole((1, LANE)), _whole((T, LANE)), ANY],
        out_specs=[ba, _whole((1, LANE)), _whole((1, LANE))],
        out_shape=[jax.ShapeDtypeStruct((T, NP), BF16), jax.ShapeDtypeStruct((1, LANE), F32),
                   jax.ShapeDtypeStruct((1, LANE), F32)],
        input_output_aliases={4: 0}, compiler_params=_cparams(("arbitrary",)),
    )(proj, a_log, dt_bias, dbg, dproj)


NCK = T // CH
CPS = 4


def _chunk_group_specs(at):
    wide = pl.BlockSpec((CPS * CH, HW), lambda n: (at(n), 0))
    return [wide, wide, wide, wide, pl.BlockSpec((HEADS, CPS * CH, CH), lambda n: (0, at(n), 0)),
            pl.BlockSpec((CPS, 1, HW), lambda n: (at(n), 0, 0))]


def _chunk_prep_shapes(dtypes):
    shp = [(T, HW), (T, HW), (T, HW), (T, HW), (HEADS, T, CH), (NCK, 1, HW)]
    return [jax.ShapeDtypeStruct(s, dt) for s, dt in zip(shp, dtypes)]


NGROUP = NCK // CPS
PREP_DTYPES = (F32, BF16, BF16, BF16, BF16, F32)


def _delta_rule(qkv, bg):
    def body(x_ref, bg_ref, *refs):
        prep_out, (o_ref, sh_ref), held, s_ref = refs[:6], refs[6:8], refs[8:14], refs[14]
        i = pl.program_id(0)

        @pl.when(i == 0)
        def _():
            for r in held + (s_ref,):
                r[...] = jnp.zeros_like(r)

        rows = [slice(ci * CH, (ci + 1) * CH) for ci in range(CPS)]
        u_h, w_h, qe_h, kd_h, qk_h, egl_h = held
        chunks = [_head_args((u_h.at[r, :], w_h.at[r, :], qe_h.at[r, :], kd_h.at[r, :], qk_h.at[:, r, :], egl_h.at[ci]))
                  for ci, r in enumerate(rows)]
        start = [jnp.where(i <= 1, 0.0, s_ref[h]) for h in range(HEADS)]
        outs, befores, final = [], [], []
        levels = _state_levels(chunks, start, outs, befores, final)
        res = _chunk_prep_fn([x_ref[r, :] for r in rows], [bg_ref[r, :] for r in rows], tick=lambda: next(levels, None))
        for _ in levels:
            pass
        for ci, (u, w, qe, kd, qk, egl) in enumerate(res):
            for refs_pair, val in zip(zip(prep_out[:4], held[:4]), (u, w, qe, kd)):
                for ref in refs_pair:
                    ref[rows[ci], :] = val.astype(ref.dtype)
            for ref in (prep_out[4], qk_h):
                ref[:, rows[ci], :] = qk.astype(ref.dtype)
            for ref in (prep_out[5], egl_h):
                ref[ci] = egl
        for ci, r in enumerate(rows):
            for h in range(HEADS):
                o_ref[r, h * HD:(h + 1) * HD] = outs[ci][h]
                sh_ref[h, ci] = befores[ci][h]
        for h in range(HEADS):
            s_ref[h] = final[0][h]

    now = lambda n: jnp.minimum(n, NGROUP - 1)
    was = lambda n: jnp.maximum(n - 1, 0)
    wide = lambda at: pl.BlockSpec((CPS * CH, HW), lambda n: (at(n), 0))
    held = [pltpu.VMEM(s, dt) for s, dt in zip(
        [(CPS * CH, HW)] * 4 + [(HEADS, CPS * CH, CH), (CPS, 1, HW)], PREP_DTYPES)]
    out = pl.pallas_call(
        body, name="gdn_delta_rule", grid=(NGROUP + 1,),
        in_specs=[pl.BlockSpec((CPS * CH, 3 * HW), lambda n: (now(n), 0)),
                  pl.BlockSpec((CPS * CH, LANE), lambda n: (now(n), 0))],
        out_specs=[wide(now)] * 4 + [pl.BlockSpec((HEADS, CPS * CH, CH), lambda n: (0, now(n), 0)),
                                     pl.BlockSpec((CPS, 1, HW), lambda n: (now(n), 0, 0)), wide(was),
                                     pl.BlockSpec((HEADS, CPS, HD, HD), lambda n: (0, was(n), 0, 0))],
        out_shape=_chunk_prep_shapes(PREP_DTYPES) + [jax.ShapeDtypeStruct((T, HW), F32),
                                                     jax.ShapeDtypeStruct((HEADS, NCK, HD, HD), F32)],
        scratch_shapes=held + [pltpu.VMEM((HEADS, HD, HD), F32)], compiler_params=_cparams(("arbitrary",)),
    )(qkv, bg)
    return out[:6], out[6], out[7]


def _chunk_prep_bwd(qkv, bg, cots):
    def body(x_ref, bg_ref, du, dw, dqe, dkd, dqk, degl, dx_ref, dbg_ref):
        rows = [slice(ci * CH, (ci + 1) * CH) for ci in range(CPS)]
        _, vjp = jax.vjp(_chunk_prep_fn, [x_ref[r, :] for r in rows], [bg_ref[r, :] for r in rows])
        dxs, dbgs = vjp([(du[r, :], dw[r, :], dqe[r, :], dkd[r, :], dqk[:, r, :], degl[ci])
                         for ci, r in enumerate(rows)])
        for r, dx, dbg in zip(rows, dxs, dbgs):
            dx_ref[r, :] = dx
            dbg_ref[r, :] = dbg

    wide = pl.BlockSpec((CPS * CH, HW), lambda n: (n, 0))
    return pl.pallas_call(
        body, name="gdn_chunk_prep_bwd", grid=(NCK // CPS,),
        in_specs=[pl.BlockSpec((CPS * CH, 3 * HW), lambda n: (n, 0)), pl.BlockSpec((CPS * CH, LANE), lambda n: (n, 0)),
                  wide, wide, wide, wide, pl.BlockSpec((HEADS, CPS * CH, CH), lambda n: (0, n, 0)),
                  pl.BlockSpec((CPS, 1, HW), lambda n: (n, 0, 0))],
        out_specs=[pl.BlockSpec((CPS * CH, 3 * HW), lambda n: (n, 0)), pl.BlockSpec((CPS * CH, LANE), lambda n: (n, 0))],
        out_shape=[jax.ShapeDtypeStruct((T, 3 * HW), F32), jax.ShapeDtypeStruct((T, LANE), F32)],
        compiler_params=_cparams(("parallel",)),
    )(qkv, bg, *cots)


def _head_args(refs):
    u, w, qe, kd, qk, egl = refs
    sls = [slice(h * HD, (h + 1) * HD) for h in range(HEADS)]
    return ([u[:, sl] for sl in sls], [w[:, sl].astype(F32) for sl in sls], [qe[:, sl].astype(F32) for sl in sls],
            [kd[:, sl].astype(F32) for sl in sls], [qk[h].astype(F32) for h in range(HEADS)],
            [egl[:, sl] for sl in sls])


def _chunk_scan_bwd(prep, s_hist, do, after=()):
    n_in = 8 + len(after)

    def body(*refs):
        u_r, w_r, qe_r, kd_r, qk_r, egl_r, sh_ref, do_ref = refs[:8]
        d_refs = refs[n_in:n_in + 6]
        ds_ref = refs[n_in + 6]

        @pl.when(pl.program_id(0) == 0)
        def _():
            ds_ref[...] = jnp.zeros_like(ds_ref)

        sls = [slice(h * HD, (h + 1) * HD) for h in range(HEADS)]
        ds = [ds_ref[h] for h in range(HEADS)]
        for ci in reversed(range(CPS)):
            r = slice(ci * CH, (ci + 1) * CH)
            args = _head_args((u_r.at[r, :], w_r.at[r, :], qe_r.at[r, :], kd_r.at[r, :], qk_r.at[:, r, :], egl_r.at[ci]))
            _, vjp = jax.vjp(_chunk_state_fn, *args, [sh_ref[h, ci] for h in range(HEADS)])
            du, dw, dqe, dkd, dqk, degl, ds = vjp(([do_ref[r, sl] for sl in sls], ds))
            for h, sl in enumerate(sls):
                for d_ref, val in zip(d_refs[:4], (du, dw, dqe, dkd)):
                    d_ref[r, sl] = val[h]
                d_refs[4][h, r, :] = dqk[h]
                d_refs[5][ci, :, sl] = degl[h]
        for h in range(HEADS):
            ds_ref[h] = ds[h]

    rev = lambda n: NGROUP - 1 - n
    return pl.pallas_call(
        body, name="gdn_scan_bwd", grid=(NGROUP,),
        in_specs=_chunk_group_specs(rev) + [pl.BlockSpec((HEADS, CPS, HD, HD), lambda n: (0, rev(n), 0, 0)),
                                            pl.BlockSpec((CPS * CH, HW), lambda n: (rev(n), 0))] + [ANY] * len(after),
        out_specs=_chunk_group_specs(rev), out_shape=_chunk_prep_shapes((F32,) * 6),
        scratch_shapes=[pltpu.VMEM((HEADS, HD, HD), F32)], compiler_params=_cparams(("arbitrary",)),
    )(*prep, s_hist, do, *after)


def _mix_specs():
    pc = lambda c: pl.BlockSpec((GCH, HW), lambda i: (i, c))
    return [pl.BlockSpec((GCH, HW), lambda i: (i, 0)), pc(3), pc(4), pc(5), _whole((1, HD)), _whole((1, HW)),
            _whole((1, HW)), _whole((HEADS, GCH, GCH)), _whole((GCH, LANE))]


def _mix(o, proj, ong, lng, lnb, ws, bst, after=()):
    def body(o_ref, z_ref, u_ref, v_ref, ong_ref, lng_ref, lnb_ref, ws_ref, bs_ref, *rest):
        rest[-1][...] = _mix_fn(o_ref[...], z_ref[...], u_ref[...], v_ref[...], ong_ref[...], lng_ref[...],
                                lnb_ref[...], ws_ref[...], bs_ref[...]).astype(BF16)

    return pl.pallas_call(
        body, name="mix", grid=(T // GCH,), in_specs=_mix_specs() + [ANY] * len(after),
        out_specs=pl.BlockSpec((GCH, D), lambda i: (i, 0)), out_shape=jax.ShapeDtypeStruct((T, D), BF16),
        compiler_params=_cparams(("parallel",)),
    )(o, proj, proj, proj, ong, lng, lnb, ws, bst, *after)


def _mix_bwd(o, proj, ong, lng, lnb, ws, bst, dmix):
    def body(o_ref, z_ref, u_ref, v_ref, ong_ref, lng_ref, lnb_ref, ws_ref, bs_ref, dm_ref,
             do_ref, dzuv_ref, dong_ref, dlng_ref, dlnb_ref, dws_ref, dbs_ref):
        _, vjp = jax.vjp(_mix_fn, o_ref[...], z_ref[...], u_ref[...], v_ref[...], ong_ref[...], lng_ref[...],
                         lnb_ref[...], ws_ref[...], bs_ref[...])
        do, dz, du, dv, dong, dlng, dlnb, dws, dbs = vjp(dm_ref[...])
        do_ref[...] = do
        dzuv_ref[:, 0:HW] = dz.astype(BF16)
        dzuv_ref[:, HW:2 * HW] = du.astype(BF16)
        dzuv_ref[:, 2 * HW:3 * HW] = dv.astype(BF16)
        acc = [(dong_ref, dong), (dlng_ref, dlng), (dlnb_ref, dlnb), (dws_ref, dws), (dbs_ref, dbs)]

        @pl.when(pl.program_id(0) == 0)
        def _():
            for r, val in acc:
                r[...] = val

        @pl.when(pl.program_id(0) > 0)
        def _():
            for r, val in acc:
                r[...] += val

    shp = lambda *s: jax.ShapeDtypeStruct(s, F32)
    return pl.pallas_call(
        body, name="mix_bwd", grid=(T // GCH,),
        in_specs=_mix_specs() + [pl.BlockSpec((GCH, D), lambda i: (i, 0))],
        out_specs=[pl.BlockSpec((GCH, HW), lambda i: (i, 0)), pl.BlockSpec((GCH, 3 * HW), lambda i: (i, 1)),
                   _whole((1, HD)), _whole((1, HW)), _whole((1, HW)), _whole((HEADS, GCH, GCH)), _whole((GCH, LANE))],
        out_shape=[shp(T, HW), jax.ShapeDtypeStruct((T, NP), BF16), shp(1, HD), shp(1, HW), shp(1, HW),
                   shp(HEADS, GCH, GCH), shp(GCH, LANE)],
        compiler_params=_cparams(("arbitrary",)),
    )(o, proj, proj, proj, ong, lng, lnb, ws, bst, dmix)


def _swiglu_epilogue(accs, _):
    gate, up = accs
    return [gate, up, _silu(gate) * up]


def _swiglu_bwd_epilogue(accs, extras):
    dact = accs[0]
    gate, up = (e.astype(F32) for e in extras)
    sg = _sigmoid(gate)
    return [dact * up * (sg * (1.0 + gate * (1.0 - sg))), dact * (gate * sg)]


def _layer_fwd(h, p):
    hn = p.pop("hn") if "hn" in p else _rmsnorm("rms_mix", h, p["norm_mix"])
    proj = _mm("in_proj", "nn", hn[None], [p["w_in"][None]], tm=1024, tn=640, tk=D, sub_m=2)[0][0]
    qkv = _prep(proj, p["conv_w"])
    bg = _gates(proj, p["a_log"], p["dt_bias"])
    prep, o, s_hist = _delta_rule(qkv, bg)
    mix = _mix(o, proj, p["o_norm_g"], p["ln_v_g"], p["ln_v_b"], p["w_s"], p["bst"],
               p.pop("before_mix")(o) if "before_mix" in p else ())
    if "late" in p:
        p.update(p.pop("late")(mix))
    h1 = _mm("out_proj", "nn", mix[None], [p["w_out"].reshape(1, D, D)], tm=T, tn=512, tk=D, resid=h[None],
             sub_m=4)[0][0]
    h2n = _rmsnorm("rms_ffn", h1, p["norm_ffn"])
    gate, up, act = _mm("ffn_in", "nt", h2n[None], [p["w_gate"], p["w_up"]], tm=1024, tn=FF_SH, tk=D,
                        out_dtypes=(BF16, BF16, BF16), epilogue=_swiglu_epilogue, sub_m=4)
    then = p.pop("before_ffn_out")(act) if "before_ffn_out" in p else ()
    h2 = _mm("ffn_out", "nn", act, [p["w_down"]], tm=1024, tn=512, tk=FF_SH, reduce_g=True, fold_g=True,
             resid=h1[None], sub_m=2, after=then)[0][0]
    saved = dict(h=h, hn=hn, proj=proj, qkv=qkv, bg=bg, prep=prep, o=o, s_hist=s_hist, mix=mix, h1=h1, h2n=h2n,
                 gate=gate, up=up, act=act)
    return h2, saved


def _layer_bwd_ffn(dh2, dh2b, p, s, after=()):
    dh2b = dh2b[None]
    dgate, dup = _mm("ffn_out_bwd", "nt", dh2b, [p["w_down"]], tm=1024, tn=FF_SH, tk=D, out_dtypes=(BF16, BF16),
                     extras=(s["gate"], s["up"]), epilogue=_swiglu_bwd_epilogue, after=after, sub_m=4)
    dh2n = _mm("ffn_gate_bwd", "nn", dgate, [p["w_gate"]], tm=1024, tn=512, tk=FF_SH, reduce_g=True, fold_g=True,
               sub_m=2)[0]
    dh2n = _mm("ffn_up_bwd", "nn", dup, [p["w_up"]], tm=1024, tn=512, tk=FF_SH, reduce_g=True, fold_g=True,
               resid=dh2n, sub_m=2)[0][0]
    dh1, dh1b, d_norm_ffn = _rmsnorm_bwd("rms_ffn_bwd", dh2n, s["h1"], p["norm_ffn"], dh2)
    d_w_down = _mm("ffn_wdown_grad", "tn", s["act"], [dh2b], tm=FF_SH, tn=512, tk=T)[0]
    d_w_gate = _mm("ffn_wgate_grad", "tn", dgate, [s["h2n"][None]], tm=FF_SH, tn=512, tk=T)[0]
    d_w_up = _mm("ffn_wup_grad", "tn", dup, [s["h2n"][None]], tm=FF_SH, tn=512, tk=T)[0]
    return dh1, dh1b, dict(norm_ffn=d_norm_ffn, w_gate=d_w_gate, w_up=d_w_up, w_down=d_w_down)


def _layer_bwd_mixer(dh1, dh1b, p, s, after=(), midway=None, late=None):
    dh1b = dh1b[None]
    dmix = _mm("out_proj_bwd", "nt", dh1b, [p["w_out"].reshape(1, D, D)], tm=T, tn=512, tk=D, after=after,
               sub_m=4)[0][0]
    d_w_out = _mm("out_proj_wgrad", "tn", s["mix"][None], [dh1b], tm=1024, tn=512, tk=T)[0][0]
    do, dproj, d_ong, d_lng, d_lnb, d_ws, d_bst = _mix_bwd(
        s["o"], s["proj"], p["o_norm_g"], p["ln_v_g"], p["ln_v_b"], p["w_s"], p["bst"], dmix)
    then = midway(do) if midway is not None else ()
    dqkv, dbg = _chunk_prep_bwd(s["qkv"], s["bg"], _chunk_scan_bwd(s["prep"], s["s_hist"], do, then))
    dproj, d_conv = _prep_bwd(s["proj"], p["conv_w"], dqkv, dproj)
    dproj, d_a_log, d_dt_bias = _gates_bwd(s["proj"], p["a_log"], p["dt_bias"], dbg, dproj)
    dproj = dproj[None]
    d_w_in = _mm("in_proj_wgrad", "tn", s["hn"][None], [dproj], tm=512, tn=640, tk=T)[0]
    last = late(dict(w_in=d_w_in, w_out=d_w_out)) if late is not None else ()
    dhn = _mm("in_proj_bwd", "nt", dproj, [p["w_in"][None]], tm=1024, tn=512, tk=NP, after=last,
              sub_m=2)[0][0]
    dh, dhb, d_norm_mix = _rmsnorm_bwd("rms_mix_bwd", dhn, s["h"], p["norm_mix"], dh1)
    grads = dict(norm_mix=d_norm_mix, w_in=d_w_in, conv_w=d_conv, a_log=d_a_log, dt_bias=d_dt_bias, o_norm_g=d_ong,
                 ln_v_g=d_lng, ln_v_b=d_lnb, w_s=d_ws, bst=d_bst, w_out=d_w_out)
    return dh, dhb, grads


def _lanes(v, off=0):
    return jnp.zeros((1, LANE), F32).at[0, off:off + v.shape[0]].set(v)


def _w_in_pieces():
    regions = [(0, 2048, 0), (2048, 2056, BA_OFF), (2056, IN_DIM, 2048)]
    sh = IN_DIM // NCHIP
    out = []
    for j in range(NCHIP):
        for lo, hi, at in regions:
            a, b = max(lo, j * sh), min(hi, (j + 1) * sh)
            if a < b:
                out.append((j, a - j * sh, at + a - lo, b - a))
    return out


W_IN_PIECES = _w_in_pieces()
WT = 256


def _assemble_w_in(gathered, own, place):
    def body(place_ref, g_ref, own_ref, o_ref):
        o_ref[:, IN_DIM:] = jnp.zeros((WT, NP - IN_DIM), BF16)
        mine = own_ref[...]
        for j, src, dst, width in W_IN_PIECES:
            val = jnp.where(place_ref[0] == j, mine[:, src:src + width], g_ref[j, :, src:src + width])
            o_ref[:, dst:dst + width] = val

    sh = IN_DIM // NCHIP
    return pl.pallas_call(
        body, name="assemble_w_in",
        grid_spec=pltpu.PrefetchScalarGridSpec(
            num_scalar_prefetch=1, grid=(D // WT,),
            in_specs=[pl.BlockSpec((NCHIP, WT, sh), lambda i, place_ref: (0, i, 0)),
                      pl.BlockSpec((WT, sh), lambda i, place_ref: (i, 0))],
            out_specs=pl.BlockSpec((WT, NP), lambda i, place_ref: (i, 0))),
        out_shape=jax.ShapeDtypeStruct((D, NP), BF16), compiler_params=_cparams(("parallel",)),
    )(place, gathered, own)


def _layer_params(l, big, small):
    return dict(
        {k: v for k, v in big.items() if k != "conv_w"},
        conv_w=jnp.concatenate([big["conv_w"][j, l] for j in range(NCHIP)], axis=1),
        norm_mix=small["norm_mix"][l][None], norm_ffn=small["norm_ffn"][l][None],
        a_log=_lanes(small["a_log"][l], HEADS), dt_bias=_lanes(small["dt_bias"][l], HEADS),
        o_norm_g=small["o_norm_g"][l][None], ln_v_g=small["ln_v_g"][l][None], ln_v_b=small["ln_v_b"][l][None],
        w_s=small["w_s"][l],
        bst=jnp.pad(small["b_s"][l].T, ((0, 0), (0, LANE - HEADS))),
    )


def _reference_layout(g):
    return dict(
        w_in=g["w_in"],
        w_out=g["w_out"].reshape(NCHIP, D // NCHIP, D),
        w_gate=g["w_gate"], w_up=g["w_up"], w_down=g["w_down"],
        conv_w=g["conv_w"], norm_mix=g["norm_mix"][0], norm_ffn=g["norm_ffn"][0],
        a_log=g["a_log"][0, HEADS:2 * HEADS], dt_bias=g["dt_bias"][0, HEADS:2 * HEADS],
        o_norm_g=g["o_norm_g"][0], ln_v_g=g["ln_v_g"][0], ln_v_b=g["ln_v_b"][0], w_s=g["w_s"],
        b_s=g["bst"][:, :HEADS].T,
    )


def _forward(x, tgt, layers, norm_final):
    h = x
    saved, params = [], []
    for p in layers:
        p = p(h) if callable(p) else p
        h, s = _layer_fwd(h, p)
        saved.append(s)
        params.append(p)
    return (saved, params) + tuple(_loss_head(h, norm_final, tgt))


def _local_step(x, tgt, layers, norm_final):
    saved, layers, loss, dh, dhb, d_norm_final = _forward(x, tgt, layers, norm_final)
    grads = [None] * DEPTH
    for l in reversed(range(DEPTH)):
        dh1, dh1b, g_ffn = _layer_bwd_ffn(dh, dhb, layers[l], saved[l])
        dh, dhb, g_mix = _layer_bwd_mixer(dh1, dh1b, layers[l], saved[l])
        grads[l] = {**g_ffn, **g_mix}
    return loss, dh, grads, d_norm_final


def _place():
    x, y, c = lax.axis_index("x"), lax.axis_index("y"), lax.axis_index("c")
    return x, y, c, [(1 - x, y), (x, 1 - y), (1 - x, 1 - y)]


def _remote(src, dst, send_sem, recv_sem, to):
    return pltpu.make_async_remote_copy(src_ref=src, dst_ref=dst, send_sem=send_sem, recv_sem=recv_sem,
                                        device_id=to, device_id_type=MESH)


def _comm_call(name, body, ins, out_shape, n_sems, aliases=None):
    return pl.pallas_call(
        body, name=name, in_specs=[ANY] * len(ins), out_specs=[ANY] * len(out_shape), out_shape=out_shape,
        scratch_shapes=[pltpu.SemaphoreType.DMA((n,)) for n in n_sems], input_output_aliases=aliases or {},
        compiler_params=pltpu.CompilerParams(has_side_effects=True),
    )(*ins)


def _half_rows(ref, of_c, dim):
    hr = ref.shape[dim] // 2
    return pl.ds(pl.multiple_of(of_c * hr, BF16_ROWS), hr)


def _gather_plan(whole):
    def plan(srcs, lands):
        x, y, c, others = _place()
        chip = 2 * x + y
        out = []
        for src, land, all_of_it in zip(srcs, lands, whole):
            for ox, oy in others:
                if all_of_it:
                    out.append((src, land.at[chip], (ox, oy, c)))
                else:
                    out.append((src.at[_half_rows(src, c, 0)], land.at[chip, _half_rows(src, c, 0)], (ox, oy, c)))
        return out
    return plan


def _forward_halves(lands):
    n = len(lands)

    def body(*refs):
        outs = refs[n:2 * n]
        send_s, recv_s = refs[2 * n:]
        x, y, c, others = _place()
        sibling = (x, y, 1 - c)
        copies = []
        for a in range(n):
            for k, (ox, oy) in enumerate(others):
                mine = outs[a].at[2 * ox + oy, _half_rows(outs[a], c, 1)]
                copies.append(_remote(mine, mine, send_s.at[3 * a + k], recv_s.at[3 * a + k], sibling))
        for cp in copies:
            cp.start()
        for a in range(n):
            for k, (ox, oy) in enumerate(others):
                landed = outs[a].at[2 * ox + oy, _half_rows(outs[a], 1 - c, 1)]
                _remote(landed, landed, send_s.at[3 * a + k], recv_s.at[3 * a + k], sibling).wait_recv()
        for cp in copies:
            cp.wait_send()

    out_shape = [jax.ShapeDtypeStruct(g.shape, g.dtype) for g in lands]
    return _comm_call("forward_halves", body, lands, out_shape, [3 * n, 3 * n], aliases={a: a for a in range(n)})


def _forward_refs(bufs, incoming):
    x, y, c, others = _place()
    return (x, y, 1 - c), [b.at[2 * ox + oy, _half_rows(b, 1 - c if incoming else c, 1)]
                           for b in bufs for ox, oy in others]


def _forward_start(name, bufs, after):
    n = len(bufs)
    bufs = [pltpu.with_memory_space_constraint(b, pltpu.HBM) for b in bufs]

    def body(*refs):
        send_s, recv_s = refs[n + len(after)], refs[n + len(after) + 1]
        sibling, mine = _forward_refs(refs[:n], incoming=False)
        for i, ref in enumerate(mine):
            _remote(ref, ref, send_s.at[i], recv_s.at[i], sibling).start()
        refs[-1][...] = jnp.zeros_like(refs[-1])

    out = pl.pallas_call(
        body, name=name, in_specs=[HBM_SPEC] * n + [ANY] * len(after),
        out_specs=[SEM_SPEC, SEM_SPEC] + [HBM_SPEC] * n + [pl.BlockSpec(memory_space=pltpu.VMEM)],
        out_shape=[pltpu.SemaphoreType.DMA((3 * n,)), pltpu.SemaphoreType.DMA((3 * n,))]
        + [pltpu.HBM(b.shape, b.dtype) for b in bufs] + [jax.ShapeDtypeStruct((F32_ROWS, LANE), F32)],
        input_output_aliases={i: 2 + i for i in range(n)},
        compiler_params=pltpu.CompilerParams(has_side_effects=DATAFLOW),
    )(*bufs, *after)
    return dict(sems=out[:2], bufs=out[2:2 + n], token=out[-1])


def _forward_wait(name, started, after):
    n = len(started["bufs"])

    def body(*refs):
        send_s, recv_s = refs[n], refs[n + 1]
        sibling, mine = _forward_refs(refs[:n], incoming=False)
        _, theirs = _forward_refs(refs[:n], incoming=True)
        for i, (sent, landed) in enumerate(zip(mine, theirs)):
            _remote(sent, sent, send_s.at[i], recv_s.at[i], sibling).wait_send()
            _remote(landed, landed, send_s.at[i], recv_s.at[i], sibling).wait_recv()

    return pl.pallas_call(
        body, name=name, in_specs=[HBM_SPEC] * n + [SEM_SPEC, SEM_SPEC] + [ANY] * len(after),
        out_specs=[HBM_SPEC] * n, out_shape=[pltpu.HBM(b.shape, b.dtype) for b in started["bufs"]],
        input_output_aliases={i: i for i in range(n)},
        compiler_params=pltpu.CompilerParams(has_side_effects=DATAFLOW),
    )(*started["bufs"], *started["sems"], *after)


HBM_SPEC = pl.BlockSpec(memory_space=pltpu.HBM)
SEM_SPEC = pl.BlockSpec(memory_space=pltpu.SEMAPHORE)
DATAFLOW = pltpu.SideEffectType.DATAFLOW_SIDE_EFFECTING


def _exchange_plan(srcs, lands):
    x, y, c, _ = _place()
    plan = []
    for src, land in zip(srcs, lands):
        hr = src.shape[1] // 2
        plan.append((src.at[:, pl.ds(pl.multiple_of((1 - c) * hr, 8), hr)], land, (x, y, 1 - c)))
    return plan


def _scatter_plan(srcs, lands):
    x, y, c, others = _place()
    return [(src.at[2 * ox + oy], land.at[k], (ox, oy, c))
            for src, land in zip(srcs, lands) for k, (ox, oy) in enumerate(others)]


def _split_start(name, plan, srcs, land_shapes, n_copies, after=()):
    n = len(srcs)
    lands = [pltpu.with_memory_space_constraint(lax.empty(s.shape, s.dtype), pltpu.HBM) for s in land_shapes]
    srcs = [pltpu.with_memory_space_constraint(s, pltpu.HBM) for s in srcs]

    def body(*refs):
        send_s, recv_s = refs[2 * n + len(after)], refs[2 * n + len(after) + 1]
        for i, (src, dst, to) in enumerate(plan(refs[:n], refs[n:2 * n])):
            _remote(src, dst, send_s.at[i], recv_s.at[i], to).start()
        refs[-1][...] = jnp.zeros_like(refs[-1])

    thru = [pltpu.HBM(s.shape, s.dtype) for s in srcs + lands]
    out = pl.pallas_call(
        body, name=name, in_specs=[HBM_SPEC] * (2 * n) + [ANY] * len(after),
        out_specs=[SEM_SPEC, SEM_SPEC] + [HBM_SPEC] * (2 * n) + [pl.BlockSpec(memory_space=pltpu.VMEM)],
        out_shape=[pltpu.SemaphoreType.DMA((n_copies,)), pltpu.SemaphoreType.DMA((n_copies,))] + thru
        + [jax.ShapeDtypeStruct((F32_ROWS, LANE), F32)],
        input_output_aliases={i: 2 + i for i in range(2 * n)},
        compiler_params=pltpu.CompilerParams(has_side_effects=DATAFLOW),
    )(*srcs, *lands, *after)
    return dict(sems=out[:2], srcs=out[2:2 + n], lands=out[2 + n:2 + 2 * n], token=out[-1])


def _split_wait(name, plan, started, after):
    n = len(started["srcs"])
    after = list(after) if isinstance(after, (list, tuple)) else [after]

    def body(*refs):
        send_s, recv_s = refs[2 * n], refs[2 * n + 1]
        for i, (src, dst, to) in enumerate(plan(refs[:n], refs[n:2 * n])):
            cp = _remote(src, dst, send_s.at[i], recv_s.at[i], to)
            cp.wait_send()
            cp.wait_recv()

    arrs = list(started["srcs"]) + list(started["lands"])
    out = pl.pallas_call(
        body, name=name, in_specs=[HBM_SPEC] * (2 * n) + [SEM_SPEC, SEM_SPEC] + [ANY] * len(after),
        out_specs=[HBM_SPEC] * (2 * n), out_shape=[pltpu.HBM(s.shape, s.dtype) for s in arrs],
        input_output_aliases={i: i for i in range(2 * n)},
        compiler_params=pltpu.CompilerParams(has_side_effects=DATAFLOW),
    )(*arrs, *started["sems"], *after)
    return out[:n], out[n:]


def _join_halves(name, rs):
    n = len(rs)

    def body(*refs):
        outs = refs[n:2 * n]
        send_s, recv_s = refs[2 * n:]
        x, y, c, _ = _place()
        sibling = (x, y, 1 - c)

        def half(a, of_c):
            hr = outs[a].shape[1] // 2
            return outs[a].at[:, pl.ds(pl.multiple_of(of_c * hr, 8), hr)]

        copies = [_remote(half(a, c), half(a, c), send_s.at[a], recv_s.at[a], sibling) for a in range(n)]
        for cp in copies:
            cp.start()
        for a in range(n):
            landed = half(a, 1 - c)
            _remote(landed, landed, send_s.at[a], recv_s.at[a], sibling).wait_recv()
        for cp in copies:
            cp.wait_send()

    out_shape = [jax.ShapeDtypeStruct(r.shape, r.dtype) for r in rs]
    return _comm_call(name, body, rs, out_shape, [n, n], aliases={a: a for a in range(n)})


def _allreduce_small(buf, after=()):
    r = buf.shape[0]
    hr = r // 2

    def body(in_ref, *refs):
        out_ref, theirs, by_chip, send_s, recv_s = refs[len(after):]
        x, y, c, others = _place()
        chip = 2 * x + y
        sibling = (x, y, 1 - c)
        mine = pl.ds(pl.multiple_of(c * hr, F32_ROWS), hr)
        swap = _remote(in_ref, theirs, send_s.at[0], recv_s.at[0], sibling)
        swap.start()
        swap.wait()
        by_chip[chip] = in_ref[mine, :] + theirs[mine, :]
        sends = [_remote(by_chip.at[chip], by_chip.at[chip], send_s.at[1 + k], recv_s.at[1 + k], (ox, oy, c))
                 for k, (ox, oy) in enumerate(others)]
        for cp in sends:
            cp.start()
        for k, (ox, oy) in enumerate(others):
            landed = by_chip.at[2 * ox + oy]
            _remote(landed, landed, send_s.at[1 + k], recv_s.at[1 + k], (ox, oy, c)).wait_recv()
        for cp in sends:
            cp.wait_send()
        out_ref[mine, :] = (by_chip[0] + by_chip[1]) + (by_chip[2] + by_chip[3])
        back = _remote(out_ref.at[mine], out_ref.at[mine], send_s.at[NCHIP], recv_s.at[NCHIP], sibling)
        back.start()
        other = out_ref.at[pl.ds(pl.multiple_of((1 - c) * hr, F32_ROWS), hr)]
        _remote(other, other, send_s.at[NCHIP], recv_s.at[NCHIP], sibling).wait_recv()
        back.wait_send()

    vm = pl.BlockSpec(memory_space=pltpu.VMEM)
    return pl.pallas_call(
        body, name="allreduce_small", in_specs=[vm] + [ANY] * len(after), out_specs=vm,
        out_shape=jax.ShapeDtypeStruct((r, LANE), F32),
        scratch_shapes=[pltpu.VMEM((r, LANE), F32), pltpu.VMEM((NCHIP, hr, LANE), F32),
                        pltpu.SemaphoreType.DMA((NCHIP + 1,)), pltpu.SemaphoreType.DMA((NCHIP + 1,))],
        compiler_params=pltpu.CompilerParams(has_side_effects=True, vmem_limit_bytes=VMEM_LIMIT),
    )(buf, *after)


MAX_ROW_TILE = 512
BF16_ROWS = 16


def _row_tile(rows):
    for t in range(min(rows, MAX_ROW_TILE) // BF16_ROWS * BF16_ROWS, 0, -BF16_ROWS):
        if rows % t == 0:
            return t
    raise ValueError(rows)


def _sum_halves(g, theirs, c_arr):
    nch, rows, cols = g.shape
    hr = rows // 2
    tr = _row_tile(hr)

    def body(c_ref, g_ref, t_ref, o_ref, ob_ref):
        s = g_ref[...] + t_ref[...]
        o_ref[...] = s
        ob_ref[...] = s.astype(BF16)

    blk = pl.BlockSpec((None, tr, cols), lambda j, i, c_ref: (j, i, 0))
    return pl.pallas_call(
        body, name="sum_halves",
        grid_spec=pltpu.PrefetchScalarGridSpec(
            num_scalar_prefetch=1, grid=(nch, hr // tr),
            in_specs=[pl.BlockSpec((None, None, tr, cols), lambda j, i, c_ref: (j, c_ref[0], i, 0)), blk],
            out_specs=[blk, blk]),
        out_shape=[jax.ShapeDtypeStruct((nch, hr, cols), F32), jax.ShapeDtypeStruct((nch, hr, cols), BF16)],
        compiler_params=_cparams(("parallel", "parallel")),
    )(c_arr, g.reshape(nch, 2, hr, cols), theirs)


def _sum_halves_w_in(g, theirs, c_arr):
    hr = D // 2
    sh = IN_DIM // NCHIP

    def body(c_ref, g_ref, t_ref, o_ref, ob_ref):
        s = g_ref[...] + t_ref[...]
        for j, dst, src, width in W_IN_PIECES:
            o_ref[j, :, dst:dst + width] = s[:, src:src + width]
            ob_ref[j, :, dst:dst + width] = s[:, src:src + width].astype(BF16)

    out = pl.BlockSpec((NCHIP, WT, sh), lambda i, c_ref: (0, i, 0))
    return pl.pallas_call(
        body, name="sum_halves_w_in",
        grid_spec=pltpu.PrefetchScalarGridSpec(
            num_scalar_prefetch=1, grid=(hr // WT,),
            in_specs=[pl.BlockSpec((None, WT, NP), lambda i, c_ref: (c_ref[0], i, 0)),
                      pl.BlockSpec((None, WT, NP), lambda i, c_ref: (0, i, 0))],
            out_specs=[out, out]),
        out_shape=[jax.ShapeDtypeStruct((NCHIP, hr, sh), F32), jax.ShapeDtypeStruct((NCHIP, hr, sh), BF16)],
        compiler_params=_cparams(("parallel",)),
    )(c_arr, g.reshape(2, hr, NP), theirs)


def _sum_chips(p, q, place, l, into=None, after=()):
    extra = ([into] if into is not None else []) + list(after)
    _, rows, cols = p.shape
    tr = _row_tile(rows)
    steps = rows // tr

    def body(place_ref, p_ref, q0, q1, q2, *rest):
        rest[-1][...] = ((p_ref[...] + q0[...].astype(F32)) + q1[...].astype(F32)) + q2[...].astype(F32)

    qs = lambda k: pl.BlockSpec((None, tr, cols), lambda i, place_ref: (k, i, 0))
    return pl.pallas_call(
        body, name="sum_chips",
        grid_spec=pltpu.PrefetchScalarGridSpec(
            num_scalar_prefetch=1, grid=(steps,),
            in_specs=[pl.BlockSpec((None, tr, cols), lambda i, place_ref: (place_ref[0], i, 0)), qs(0), qs(1), qs(2)]
            + [ANY] * len(extra),
            out_specs=pl.BlockSpec((None, tr, cols), lambda i, place_ref: (l, place_ref[1] * steps + i, 0))),
        out_shape=jax.ShapeDtypeStruct((DEPTH, 2 * rows, cols), F32),
        input_output_aliases={5: 0} if into is not None else {},
        compiler_params=_cparams(("parallel",)),
    )(place, p, q, q, q, *extra)


def _adamw_fn(w, g, m, v):
    nm = ADAM_B1 * m + (1.0 - ADAM_B1) * g
    nv = ADAM_B2 * v + (1.0 - ADAM_B2) * jnp.square(g)
    m_hat = nm / (1.0 - ADAM_B1 ** ADAM_STEP)
    v_hat = nv / (1.0 - ADAM_B2 ** ADAM_STEP)
    return -ADAM_LR * (m_hat / (jnp.sqrt(v_hat) + ADAM_EPS) + ADAM_WD * w), nm, nv


def _adamw(w, g, m, v):
    layers, rows, cols = w.shape
    tr = _row_tile(rows)

    def body(w_ref, g_ref, m_ref, v_ref, d_ref, nm_ref, nv_ref):
        d_ref[...], nm_ref[...], nv_ref[...] = _adamw_fn(w_ref[...], g_ref[...], m_ref[...], v_ref[...])

    blk = pl.BlockSpec((None, tr, cols), lambda l, i: (l, i, 0))
    return pl.pallas_call(
        body, name="adamw", grid=(layers, rows // tr), in_specs=[blk] * 4, out_specs=[blk] * 3,
        out_shape=[jax.ShapeDtypeStruct(w.shape, F32)] * 3, compiler_params=_cparams(("parallel", "parallel")),
    )(w, g, m, v)


def _adamw_small(ws, gs, ms, vs):
    n = len(ws)

    def body(*refs):
        for i in range(n):
            w_ref, g_ref, m_ref, v_ref, d_ref, nm_ref, nv_ref = (refs[k * n + i] for k in range(7))
            d_ref[...], nm_ref[...], nv_ref[...] = _adamw_fn(w_ref[...], g_ref[...], m_ref[...], v_ref[...])

    vm = pl.BlockSpec(memory_space=pltpu.VMEM)
    out = pl.pallas_call(
        body, name="adamw_small", in_specs=[vm] * (4 * n), out_specs=[vm] * (3 * n),
        out_shape=[jax.ShapeDtypeStruct(a.shape, F32) for a in list(ws) * 3],
        compiler_params=pltpu.CompilerParams(vmem_limit_bytes=VMEM_LIMIT),
    )(*ws, *gs, *ms, *vs)
    return out[:n], out[n:2 * n], out[2 * n:]


BIG = ("w_in", "w_out", "w_gate", "w_up", "w_down")
SMALL = ("norm_mix", "a_log", "dt_bias", "o_norm_g", "ln_v_g", "ln_v_b", "w_s", "b_s", "norm_ffn", "norm_final")
ORDER = ("norm_mix", "w_in", "conv_w", "a_log", "dt_bias", "o_norm_g", "ln_v_g", "ln_v_b", "w_s", "b_s", "w_out",
         "norm_ffn", "w_gate", "w_up", "w_down", "norm_final")


F32_ROWS = 8
PACK_ROWS = 128


def _lane_rows(size):
    return -(-size // (F32_ROWS * LANE)) * F32_ROWS


def _pack(arrs):
    parts = [jnp.pad(a.reshape(-1), (0, _lane_rows(a.size) * LANE - a.size)).reshape(-1, LANE) for a in arrs]
    rows = sum(p.shape[0] for p in parts)
    if rows % PACK_ROWS:
        parts.append(jnp.zeros((-rows % PACK_ROWS, LANE), F32))
    return jnp.concatenate(parts, axis=0)


def _unpack(buf, like):
    out, row = [], 0
    for a in like:
        n = _lane_rows(a.size)
        out.append(buf[row:row + n].reshape(-1)[:a.size].reshape(a.shape))
        row += n
    return out


def kernel(x, norm_mix, w_in, conv_w, a_log, dt_bias, o_norm_g, ln_v_g, ln_v_b, w_s, b_s, w_out, norm_ffn, w_gate, w_up, w_down, norm_final, loss_target, m_norm_mix, m_w_in, m_conv_w, m_a_log, m_dt_bias, m_o_norm_g, m_ln_v_g, m_ln_v_b, m_w_s, m_b_s, m_w_out, m_norm_ffn, m_w_gate, m_w_up, m_w_down, m_norm_final, v_norm_mix, v_w_in, v_conv_w, v_a_log, v_dt_bias, v_o_norm_g, v_ln_v_g, v_ln_v_b, v_w_s, v_b_s, v_w_out, v_norm_ffn, v_w_gate, v_w_up, v_w_down, v_norm_final):
    w = dict(norm_mix=norm_mix, w_in=w_in, conv_w=conv_w, a_log=a_log, dt_bias=dt_bias, o_norm_g=o_norm_g,
             ln_v_g=ln_v_g, ln_v_b=ln_v_b, w_s=w_s, b_s=b_s, w_out=w_out, norm_ffn=norm_ffn, w_gate=w_gate, w_up=w_up,
             w_down=w_down, norm_final=norm_final)
    m = dict(norm_mix=m_norm_mix, w_in=m_w_in, conv_w=m_conv_w, a_log=m_a_log, dt_bias=m_dt_bias, o_norm_g=m_o_norm_g,
             ln_v_g=m_ln_v_g, ln_v_b=m_ln_v_b, w_s=m_w_s, b_s=m_b_s, w_out=m_w_out, norm_ffn=m_norm_ffn,
             w_gate=m_w_gate, w_up=m_w_up, w_down=m_w_down, norm_final=m_norm_final)
    v = dict(norm_mix=v_norm_mix, w_in=v_w_in, conv_w=v_conv_w, a_log=v_a_log, dt_bias=v_dt_bias, o_norm_g=v_o_norm_g,
             ln_v_g=v_ln_v_g, ln_v_b=v_ln_v_b, w_s=v_w_s, b_s=v_b_s, w_out=v_w_out, norm_ffn=v_norm_ffn,
             w_gate=v_w_gate, w_up=v_w_up, w_down=v_w_down, norm_final=v_norm_final)
    chip = 2 * lax.axis_index("x") + lax.axis_index("y")
    place = jnp.stack([chip, lax.axis_index("c")]).astype(jnp.int32)
    c_arr = place[1:]

    def kernel_view(n, a):
        return jnp.swapaxes(a, 1, 2) if n in ("w_gate", "w_up") else a

    own = {n: [kernel_view(n, w[n])[l].astype(BF16) for l in range(DEPTH)] for n in BIG}
    by_chip = lambda a: jax.ShapeDtypeStruct((NCHIP,) + a.shape, a.dtype)

    def start(name, srcs, whole, after=()):
        return _split_start(name, _gather_plan(whole), srcs, [by_chip(a) for a in srcs], 3 * len(srcs), after)

    def finish(name, started, whole, after):
        srcs, lands = _split_wait(name, _gather_plan(whole), started, after)
        passed = iter(_forward_halves([g for g, all_of_it in zip(lands, whole) if not all_of_it]))
        lands = [g if all_of_it else next(passed) for g, all_of_it in zip(lands, whole)]
        return srcs, [lax.dynamic_update_index_in_dim(g, o, chip, 0) for g, o in zip(lands, srcs)]

    ffn = BIG[1:]
    first = start("gather_first_start", [own["w_in"][0], conv_w], [False, True])
    early = start("gather_early_start", [own[n][0] for n in ffn], [False] * len(ffn), [first["token"]])
    mid = start("gather_mid_start", [own["w_in"][1]], [False], [early["token"]])
    later = start("gather_later_start", [own[n][1] for n in ffn], [False] * len(ffn), [mid["token"]])
    hn = _rmsnorm("rms_mix", x[0], norm_mix[0][None])
    (own_w_in, _), (w_in_by_chip, conv_by_chip) = finish("gather_first_wait", first, [False, True], [later["token"], hn])

    passing = {}

    def pass_on(tag, started, n):
        def at(after):
            srcs, lands = _split_wait(f"gather_{tag}_wait", _gather_plan([False] * n), started, after)
            passing[tag] = srcs, _forward_start(f"forward_{tag}_start", lands, ())
            return [passing[tag][1]["token"]]
        return at

    def passed_on(tag, after):
        srcs, fwd = passing[tag]
        lands = _forward_wait(f"forward_{tag}_wait", fwd, [after])
        return srcs, [lax.dynamic_update_index_in_dim(g, o, chip, 0) for g, o in zip(lands, srcs)]

    def late(tag):
        return lambda after: dict(zip(ffn, passed_on(tag, after)[1]))

    layer0 = _layer_params(0, dict(
        hn=hn, w_in=_assemble_w_in(w_in_by_chip, own_w_in, place), conv_w=conv_by_chip, late=late("early"),
        before_mix=pass_on("early", early, len(ffn)), before_ffn_out=pass_on("mid", mid, 1)), w)

    def layer1(after):
        (own_w_in1,), (w_in1_by_chip,) = passed_on("mid", after)
        return _layer_params(1, dict(w_in=_assemble_w_in(w_in1_by_chip, own_w_in1, place), conv_w=conv_by_chip,
                                     late=late("later"), before_mix=pass_on("later", later, len(ffn))), w)

    saved, layers, loss_lanes, dh, dhb, d_norm_final = _forward(x[0], loss_target[0], [layer0, layer1],
                                                                 norm_final[None])

    sums, arrived = {}, {}

    def exchange_start(tag, l, names, grads, after=()):
        mine = [grads[n] for n in names]
        shapes = [jax.ShapeDtypeStruct((g.shape[0], g.shape[1] // 2, g.shape[2]), F32) for g in mine]
        return tag, l, names, _split_start(f"exchange_{tag}_start", _exchange_plan, mine, shapes, len(mine), after)

    def add_halves(l, names, mine, theirs):
        for n, g, t in zip(names, mine, theirs):
            sums[l, n] = (_sum_halves_w_in if n == "w_in" else _sum_halves)(g, t, c_arr)

    def exchange_wait(handle, after):
        tag, l, names, started = handle
        add_halves(l, names, *_split_wait(f"exchange_{tag}_wait", _exchange_plan, started, after))

    def scatter_start(tag, l, names, after=()):
        partial = [sums[l, n][1] for n in names]
        shapes = [jax.ShapeDtypeStruct((3,) + p.shape[1:], p.dtype) for p in partial]
        return tag, l, names, _split_start(f"scatter_{tag}_start", _scatter_plan, partial, shapes, 3 * len(names), after)

    def scatter_wait(handle, after):
        tag, l, names, started = handle
        for n, q in zip(names, _split_wait(f"scatter_{tag}_wait", _scatter_plan, started, after)[1]):
            arrived[l, n] = q

    last = DEPTH - 1
    swiglu = BIG[2:]
    dh1, dh1b, g_ffn = _layer_bwd_ffn(dh, dhb, layers[last], saved[last])
    dh, dhb, g_mix = _layer_bwd_mixer(dh1, dh1b, layers[last], saved[last])
    gl = [None, _reference_layout({**g_ffn, **g_mix})]
    ex_last = exchange_start("last", last, BIG, gl[last])
    dh1, dh1b, g_ffn = _layer_bwd_ffn(dh, dhb, layers[0], saved[0], after=[ex_last[-1]["token"]])
    exchange_wait(ex_last, dh1)
    sc_last = scatter_start("last", last, BIG)
    ex_ffn = exchange_start("swiglu", 0, swiglu, g_ffn, [sc_last[-1]["token"]])
    sc_ffn = []

    def midway(do):
        exchange_wait(ex_ffn, do)
        sc_ffn.append(scatter_start("swiglu", 0, swiglu))
        return [sc_ffn[0][-1]["token"]]

    ex_rest = []

    def late(grads):
        rest_grads = dict(w_in=grads["w_in"], w_out=grads["w_out"].reshape(NCHIP, D // NCHIP, D))
        ex_rest.append(exchange_start("rest", 0, BIG[:2], rest_grads))
        return [ex_rest[0][-1]["token"]]

    dx, _, g_mix = _layer_bwd_mixer(dh1, dh1b, layers[0], saved[0], after=[ex_ffn[-1]["token"]], midway=midway,
                                    late=late)
    scatter_wait(sc_last, dx)
    scatter_wait(sc_ffn[0], dx)
    gl[0] = _reference_layout({**g_ffn, **g_mix})

    small_g = [jnp.stack([gl[l][n] for l in range(DEPTH)]) for n in SMALL[:-1]] + [d_norm_final[0]]
    conv_g = jnp.stack([gl[l]["conv_w"] for l in range(DEPTH)])
    summed = small_g + [conv_g, loss_lanes[0, :1]]
    total = _allreduce_small(_pack(summed))
    exchange_wait(ex_rest[0], total)
    sc_rest = scatter_start("rest", 0, BIG[:2])

    travelling = [sc_rest[-1]["token"]]
    reduced, g_out, delta, new_m, new_v = {}, {}, {}, {}, {}

    done = []

    def adamw_large(names, joined):
        for n, g in zip(names, joined):
            res = _adamw(kernel_view(n, w[n]), g, kernel_view(n, m[n]), kernel_view(n, v[n]))
            done.append(res[2])
            g_out[n], delta[n], new_m[n], new_v[n] = (kernel_view(n, a) for a in (g,) + tuple(res))

    for n in BIG:
        for l in (range(DEPTH) if n in swiglu else [last]):
            reduced[n] = _sum_chips(sums[l, n][0], arrived[l, n], place, l, into=reduced.get(n), after=travelling)
    adamw_large(swiglu, _join_halves("join_swiglu", [reduced[n] for n in swiglu]))
    scatter_wait(sc_rest, done + [reduced[n] for n in BIG[:2]])
    for n in BIG[:2]:
        reduced[n] = _sum_chips(sums[0, n][0], arrived[0, n], place, 0, into=reduced[n])
    adamw_large(BIG[:2], _join_halves("join_rest", [reduced[n] for n in BIG[:2]]))
    *small_r, conv_r, loss = _unpack(total, summed)
    g_out.update(zip(SMALL, small_r))
    g_out["conv_w"] = lax.dynamic_slice_in_dim(conv_r, chip * conv_w.shape[2], conv_w.shape[2], axis=2)

    rest = SMALL + ("conv_w",)
    rows_of = lambda a: a.reshape(1, -1) if a.ndim == 1 else a
    results = _adamw_small(*[[rows_of(src[n]) for n in rest] for src in (w, g_out, m, v)])
    for dst, arrs in zip((delta, new_m, new_v), results):
        dst.update({n: a.reshape(w[n].shape) for n, a in zip(rest, arrs)})

    return (loss[0], dx[None], *[g_out[n] for n in ORDER], *[delta[n] for n in ORDER], *[new_m[n] for n in ORDER],
            *[new_v[n] for n in ORDER])
```

```python
import functools

import jax
import jax.numpy as jnp
from jax import lax
from jax.experimental import pallas as pl
from jax.experimental.pallas import tpu as pltpu

F32 = jnp.float32
BF16 = jnp.bfloat16
MESH = pl.DeviceIdType.MESH
ANY = pl.BlockSpec(memory_space=pl.ANY)
HIGHEST = lax.Precision.HIGHEST

T = 2048
D = 1024
DEPTH = 2
NCHIP = 4
HEADS = 4
HD = 128
HW = HEADS * HD
CH = 64
GCH = 128
IN_DIM = 3080
NP = 3200
BA_OFF = 3072
FF_SH = 704
EPS = 1e-6
LANE = 128
VMEM_LIMIT = 56 * 1024 * 1024

ADAM_LR = 0.001
ADAM_B1 = 0.9
ADAM_B2 = 0.999
ADAM_EPS = 1e-08
ADAM_WD = 0.01
ADAM_STEP = 10


def _cparams(sem=None):
    return pltpu.CompilerParams(dimension_semantics=sem, vmem_limit_bytes=VMEM_LIMIT)


_DIMS = {"nn": (((1,), (0,)), ((), ())), "nt": (((1,), (1,)), ((), ())), "tn": (((0,), (0,)), ((), ()))}


def _mm(name, mode, a, bs, *, tm, tn, tk, out_dtypes=(F32,), reduce_g=False, resid=None, extras=(), epilogue=None,
        after=(), fold_g=False, sub_m=1):
    assert sub_m == 1 or (mode != "tn" and tm % (8 * sub_m) == 0), (name, sub_m)
    nb = len(bs)
    ga = a.shape[0]
    gbs = [b.shape[0] for b in bs]
    g_n = max([ga] + gbs)
    if mode == "tn":
        k_n, m_n = a.shape[1:]
    else:
        m_n, k_n = a.shape[1:]
    n_n = bs[0].shape[1] if mode == "nt" else bs[0].shape[2]
    assert m_n % tm == 0 and n_n % tn == 0 and k_n % tk == 0, (name, m_n, n_n, k_n)
    mi, nj, kk = m_n // tm, n_n // tn, k_n // tk
    lead = g_n if fold_g else None
    g_steps = 1 if fold_g else g_n
    grid = (mi, nj, g_steps, kk)
    ids = lambda i, j, g, k: (g, i, j, k)
    n_red = (g_steps if reduce_g else 1) * kk
    red_idx = lambda: (pl.program_id(2) * kk if reduce_g else 0) + pl.program_id(3)
    sem = ("parallel", "parallel", "arbitrary" if reduce_g else "parallel", "arbitrary")

    def pick(gsz, g):
        return g if gsz > 1 else 0

    def a_map(*p):
        g, i, j, k = ids(*p)
        return (pick(ga, g), k, i) if mode == "tn" else (pick(ga, g), i, k)

    def b_map(gsz):
        def f(*p):
            g, i, j, k = ids(*p)
            return (pick(gsz, g), j, k) if mode == "nt" else (pick(gsz, g), k, j)
        return f

    def o_map(gsz):
        def f(*p):
            g, i, j, k = ids(*p)
            return (0 if reduce_g else pick(gsz, g), i, j)
        return f

    a_spec = pl.BlockSpec((lead, tk, tm) if mode == "tn" else (lead, tm, tk), a_map)
    b_specs = [pl.BlockSpec((lead, tn, tk) if mode == "nt" else (lead, tk, tn), b_map(gs)) for gs in gbs]
    x_specs = [pl.BlockSpec((None, tm, tn), o_map(e.shape[0])) for e in extras]
    r_specs = [pl.BlockSpec((None, tm, tn), o_map(resid.shape[0]))] if resid is not None else []
    g_out = 1 if reduce_g else g_n
    out_shape = [jax.ShapeDtypeStruct((g_out, m_n, n_n), dt) for dt in out_dtypes]
    out_specs = [pl.BlockSpec((None, tm, tn), o_map(g_out)) for _ in out_dtypes]
    nx, nr, no = len(extras), len(r_specs), len(out_dtypes)
    n_in = 1 + nb + nx + nr + len(after)
    dims = _DIMS[mode]

    def body(*refs):
        a_ref = refs[0]
        b_refs = refs[1:1 + nb]
        x_refs = refs[1 + nb:1 + nb + nx]
        r_refs = refs[1 + nb + nx:1 + nb + nx + nr]
        o_refs = refs[n_in:n_in + no]
        acc_refs = refs[n_in + no:]
        def dots(rows):
            if fold_g:
                return [sum(lax.dot_general(a_ref[g, rows, :], b_ref[g], dims, preferred_element_type=F32)
                            for g in range(g_n)) for b_ref in b_refs]
            av = a_ref[...] if mode == "tn" else a_ref[rows, :]
            return [lax.dot_general(av, b_ref[...], dims, preferred_element_type=F32) for b_ref in b_refs]

        def finish(accs, rows=slice(None)):
            if r_refs:
                accs[0] = accs[0] + r_refs[0][rows, :]
            outs = epilogue(accs, [x[rows, :] for x in x_refs]) if epilogue is not None else accs
            for o_ref, o in zip(o_refs, outs):
                o_ref[rows, :] = o.astype(o_ref.dtype)

        if n_red == 1:
            slabs = [slice(s * (tm // sub_m), (s + 1) * (tm // sub_m)) for s in range(sub_m)]
            ahead = dots(slabs[0])
            for s, rows in enumerate(slabs):
                now, ahead = ahead, (dots(slabs[s + 1]) if s + 1 < sub_m else None)
                finish(now, rows)
            return
        products = dots(slice(None))
        r = red_idx()
        for p, acc in zip(products, acc_refs):
            @pl.when(r == 0)
            def _():
                acc[...] = p

            @pl.when((r > 0) & (r < n_red - 1))
            def _():
                acc[...] += p

        @pl.when(r == n_red - 1)
        def _():
            finish([acc[...] + p for p, acc in zip(products, acc_refs)])

    return pl.pallas_call(
        body, name=name, grid=grid,
        in_specs=[a_spec] + b_specs + x_specs + r_specs + [ANY] * len(after),
        out_specs=out_specs, out_shape=out_shape,
        scratch_shapes=[pltpu.VMEM((tm, tn), F32) for _ in range(nb if n_red > 1 else 0)],
        compiler_params=_cparams(sem),
    )(a, *bs, *extras, *([resid] if resid is not None else []), *after)


def _sigmoid(x):
    return 0.5 * jnp.tanh(0.5 * x) + 0.5


def _silu(x):
    return x * _sigmoid(x)


def _gelu(x):
    return 0.5 * x * (1.0 + jnp.tanh(0.7978845608028654 * (x + 0.044715 * (x * x * x))))


def _rms_fn(h, gain):
    return h * lax.rsqrt(jnp.mean(h * h, axis=-1, keepdims=True) + EPS) * gain


def _shift_impl(x, s):
    n = x.shape[0]
    rolled = pltpu.roll(x, s % n, 0)
    row = lax.broadcasted_iota(jnp.int32, x.shape, 0)
    return jnp.where((row >= s) & (row < n + s), rolled, 0.0)


@functools.partial(jax.custom_vjp, nondiff_argnums=(1,))
def _shift(x, s):
    return _shift_impl(x, s)


def _shift_fwd(x, s):
    return _shift_impl(x, s), None


def _shift_bwd(s, _, g):
    return (_shift_impl(g, -s),)


_shift.defvjp(_shift_fwd, _shift_bwd)


def _prep_fn(x, w, qk_scale, is_v):
    y = x * w[3:4, :]
    for i in range(3):
        y = y + _shift(x, 3 - i) * w[i:i + 1, :]
    y = _silu(y)
    nrm = lax.rsqrt(jnp.sum(y * y, axis=-1, keepdims=True) + EPS) * qk_scale
    return y * jnp.where(is_v, 1.0, nrm)


def _softplus(x):
    return jnp.maximum(x, 0.0) + jnp.log(1.0 + jnp.exp(-jnp.abs(x)))


def _gates_fn(ba, a_log, dt_bias):
    lane = lax.broadcasted_iota(jnp.int32, ba.shape, 1)
    beta = _sigmoid(ba)
    g = -jnp.exp(a_log) * _softplus(ba + dt_bias)
    return jnp.where(lane < HEADS, beta, g)


def _dot16(a, b, dims=_DIMS["nn"]):
    return lax.dot_general(a.astype(BF16), b.astype(BF16), dims, preferred_element_type=F32)


def _dot32(a, b):
    return jnp.dot(a, b, preferred_element_type=F32, precision=HIGHEST)


def _dot3(a, b, dims=_DIMS["nn"]):
    return lax.dot_general(a, b, dims, preferred_element_type=F32, precision=lax.Precision.HIGH)


def _tri_inverses(mats, tick=lambda: None):
    row = lax.broadcasted_iota(jnp.int32, (CH, CH), 0)
    col = lax.broadcasted_iota(jnp.int32, (CH, CH), 1)
    eye = (row == col).astype(F32)
    ts = [eye - a for a in mats]
    ps = list(mats)
    for _ in range(5):
        ps = [_dot3(p, p) for p in ps]
        tick()
        ts = [t + _dot3(t, p) for t, p in zip(ts, ps)]
        tick()
    return ts


@jax.custom_vjp
def _tri_solves(mats, rhs):
    return [_dot3(t, b) for t, b in zip(_tri_inverses(mats), rhs)]


def _tri_solves_fwd(mats, rhs):
    ts = _tri_inverses(mats)
    xs = [_dot3(t, b) for t, b in zip(ts, rhs)]
    return xs, (ts, xs)


def _tri_solves_bwd(res, dxs):
    ts, xs = res
    dbs = [_dot3(t, dx, _DIMS["tn"]) for t, dx in zip(ts, dxs)]
    return [-_dot3(db, x, _DIMS["nt"]) for db, x in zip(dbs, xs)], dbs


_tri_solves.defvjp(_tri_solves_fwd, _tri_solves_bwd)


def _chunk_prep_fn(xs, bgs, tick=None):
    step = tick or (lambda: None)
    row = lax.broadcasted_iota(jnp.int32, (CH, CH), 0)
    col = lax.broadcasted_iota(jnp.int32, (CH, CH), 1)
    incl = row >= col
    strict = row > col
    lmat = incl.astype(F32)
    n = len(xs)
    items = [(i, h) for i in range(n) for h in range(HEADS)]
    part = lambda i, h, c: xs[i][:, c * HW + h * HD:c * HW + (h + 1) * HD]
    q = [part(i, h, 0) for i, h in items]
    k = [part(i, h, 1) for i, h in items]
    v = [part(i, h, 2) for i, h in items]
    beta = [bgs[i][:, h:h + 1] for i, h in items]
    gc_all = [_dot32(lmat, bg) for bg in bgs]
    step()
    gc = [gc_all[i][:, HEADS + h:HEADS + h + 1] for i, h in items]
    gmat = [jnp.where(strict, jnp.broadcast_to(bgs[i][:, HEADS + h:HEADS + h + 1], (CH, CH)), 0.0) for i, h in items]
    diff = [_dot3(lmat, m) for m in gmat]
    step()
    decay = [jnp.where(incl, jnp.exp(jnp.where(incl, d, 0.0)), 0.0) for d in diff]
    k_beta = [kk * b for kk, b in zip(k, beta)]
    kk_t = [_dot16(kb, kk, _DIMS["nt"]) for kb, kk in zip(k_beta, k)]
    step()
    qk_t = [_dot16(qq, kk, _DIMS["nt"]) for qq, kk in zip(q, k)]
    step()
    a = [jnp.where(strict, m * d, 0.0) for m, d in zip(kk_t, decay)]
    eg = [jnp.exp(g) for g in gc]
    rhs = [jnp.concatenate([vv * b, kb * e], axis=-1) for vv, b, kb, e in zip(v, beta, k_beta, eg)]
    if tick is None:
        uw = _tri_solves(a, rhs)
    else:
        uw = [_dot3(t, b) for t, b in zip(_tri_inverses(a, tick), rhs)]
    qk = [m * d for m, d in zip(qk_t, decay)]
    g_last = [g[CH - 1:CH, :] for g in gc]
    qe = [qq * e for qq, e in zip(q, eg)]
    kd = [kk * jnp.exp(gl - g) for kk, gl, g in zip(k, g_last, gc)]
    egl = [jnp.broadcast_to(jnp.exp(gl), (1, HD)) for gl in g_last]
    out = []
    for i in range(n):
        mine = slice(i * HEADS, (i + 1) * HEADS)
        cat = lambda vals: jnp.concatenate(vals[mine], axis=-1)
        out.append((cat([x[:, :HD] for x in uw]), cat([x[:, HD:] for x in uw]), cat(qe), cat(kd),
                    jnp.concatenate([m[None] for m in qk[mine]], axis=0), cat(egl)))
    return out


def _state_levels(chunks, s, outs, befores, final):
    for u, w, qe, kd, qk, egl in chunks:
        befores.append(s)
        ws = [_dot16(a, b) for a, b in zip(w, s)]
        qs = [_dot16(a, b) for a, b in zip(qe, s)]
        yield
        v_new = [a - b for a, b in zip(u, ws)]
        outs.append([a + _dot16(b, c) for a, b, c in zip(qs, qk, v_new)])
        s = [a * e + _dot16(b, c, _DIMS["tn"]) for a, e, b, c in zip(s, egl, kd, v_new)]
        yield
    final.append(s)


def _chunk_state_fn(u, w, qe, kd, qk, egl, s):
    ws = [_dot16(a, b) for a, b in zip(w, s)]
    qs = [_dot16(a, b) for a, b in zip(qe, s)]
    v_new = [a - b for a, b in zip(u, ws)]
    o = [a + _dot16(b, c) for a, b, c in zip(qs, qk, v_new)]
    s_new = [a * e + _dot16(b, c, _DIMS["tn"]) for a, e, b, c in zip(s, egl, kd, v_new)]
    return o, s_new


def _mix_fn(o, z, ur, vr, ong, lng, lnb, ws, bst):
    row = lax.broadcasted_iota(jnp.int32, (GCH, GCH), 0)
    col = lax.broadcasted_iota(jnp.int32, (GCH, GCH), 1)
    causal = row >= col
    ug = _gelu(ur)
    vg = _gelu(vr)
    sls = [slice(h * HD, (h + 1) * HD) for h in range(HEADS)]
    oh = [o[:, sl] for sl in sls]
    oh = [x * lax.rsqrt(jnp.mean(x * x, axis=-1, keepdims=True) + EPS) for x in oh]
    outs_dn = [x * ong * _silu(z[:, sl]) for x, sl in zip(oh, sls)]
    vh = [vg[:, sl] for sl in sls]
    mu = [jnp.mean(x, axis=-1, keepdims=True) for x in vh]
    var = [jnp.mean(jnp.square(x - m), axis=-1, keepdims=True) for x, m in zip(vh, mu)]
    vn = [(x - m) * lax.rsqrt(s + EPS) * lng[:, sl] + lnb[:, sl] for x, m, s, sl in zip(vh, mu, var, sls)]
    mixed = [_dot16(jnp.where(causal, ws[h], 0.0), vn[h]) for h in range(HEADS)]
    outs_gm = [ug[:, sl] * (mixed[h] + bst[:, h:h + 1]) for h, sl in enumerate(sls)]
    return jnp.concatenate(outs_dn + outs_gm, axis=-1)


def _loss_fn(h, gain, tgt):
    y = _rms_fn(h, gain)
    return 0.5 * jnp.sum(jnp.mean(jnp.square(y - tgt), axis=-1))


RT = 512


def _rows(n=D):
    return pl.BlockSpec((RT, n), lambda i: (i, 0))


def _whole(shape):
    nd = len(shape)
    return pl.BlockSpec(shape, lambda i: (0,) * nd)


def _rmsnorm(name, h, gain):
    def body(h_ref, g_ref, o_ref):
        o_ref[...] = _rms_fn(h_ref[...], g_ref[...]).astype(BF16)

    return pl.pallas_call(
        body, name=name, grid=(T // RT,), in_specs=[_rows(), _whole((1, D))], out_specs=_rows(),
        out_shape=jax.ShapeDtypeStruct((T, D), BF16), compiler_params=_cparams(("parallel",)),
    )(h, gain)


def _rmsnorm_bwd(name, dhn, h, gain, resid):
    def body(dhn_ref, h_ref, g_ref, r_ref, dh_ref, dh16_ref, dg_ref):
        _, vjp = jax.vjp(_rms_fn, h_ref[...], g_ref[...])
        dh, dg = vjp(dhn_ref[...])
        dh = r_ref[...] + dh
        dh_ref[...] = dh
        dh16_ref[...] = dh.astype(BF16)

        @pl.when(pl.program_id(0) == 0)
        def _():
            dg_ref[...] = dg

        @pl.when(pl.program_id(0) > 0)
        def _():
            dg_ref[...] += dg

    return pl.pallas_call(
        body, name=name, grid=(T // RT,), in_specs=[_rows(), _rows(), _whole((1, D)), _rows()],
        out_specs=[_rows(), _rows(), _whole((1, D))],
        out_shape=[jax.ShapeDtypeStruct((T, D), F32), jax.ShapeDtypeStruct((T, D), BF16),
                   jax.ShapeDtypeStruct((1, D), F32)],
        compiler_params=_cparams(("arbitrary",)),
    )(dhn, h, gain, resid)


def _loss_head(h, gain, tgt):
    def body(h_ref, g_ref, t_ref, l_ref, dh_ref, dh16_ref, dg_ref):
        loss, vjp = jax.vjp(lambda hh, gg: _loss_fn(hh, gg, t_ref[...]), h_ref[...], g_ref[...])
        dh, dg = vjp(jnp.ones((), F32))
        dh_ref[...] = dh
        dh16_ref[...] = dh.astype(BF16)
        lv = jnp.full((1, LANE), loss, F32)

        @pl.when(pl.program_id(0) == 0)
        def _():
            dg_ref[...] = dg
            l_ref[...] = lv

        @pl.when(pl.program_id(0) > 0)
        def _():
            dg_ref[...] += dg
            l_ref[...] += lv

    return pl.pallas_call(
        body, name="loss_head", grid=(T // RT,), in_specs=[_rows(), _whole((1, D)), _rows()],
        out_specs=[_whole((1, LANE)), _rows(), _rows(), _whole((1, D))],
        out_shape=[jax.ShapeDtypeStruct((1, LANE), F32), jax.ShapeDtypeStruct((T, D), F32),
                   jax.ShapeDtypeStruct((T, D), BF16), jax.ShapeDtypeStruct((1, D), F32)],
        compiler_params=_cparams(("arbitrary",)),
    )(h, gain, tgt)


def _prep_flags():
    j = pl.program_id(0)
    qk_scale = jnp.where(j < HEADS, HD ** -0.5, 1.0).astype(F32)
    return qk_scale, j >= 2 * HEADS


def _prep(proj, conv_w):
    def body(x_ref, w_ref, o_ref):
        qk_scale, is_v = _prep_flags()
        o_ref[...] = _prep_fn(x_ref[...], w_ref[...], qk_scale, is_v)

    col = lambda j: (0, j)
    return pl.pallas_call(
        body, name="gdn_prep", grid=(3 * HEADS,),
        in_specs=[pl.BlockSpec((T, HD), col), pl.BlockSpec((4, HD), col)], out_specs=pl.BlockSpec((T, HD), col),
        out_shape=jax.ShapeDtypeStruct((T, 3 * HW), F32), compiler_params=_cparams(("parallel",)),
    )(proj, conv_w)


def _prep_bwd(proj, conv_w, dqkv, dproj):
    def body(x_ref, w_ref, d_ref, _, dx_ref, dw_ref):
        qk_scale, is_v = _prep_flags()
        _, vjp = jax.vjp(lambda x, w: _prep_fn(x, w, qk_scale, is_v), x_ref[...], w_ref[...])
        dx, dw = vjp(d_ref[...])
        dx_ref[...] = dx.astype(BF16)
        dw_ref[...] = dw

    col = lambda j: (0, j)
    return pl.pallas_call(
        body, name="gdn_prep_bwd", grid=(3 * HEADS,),
        in_specs=[pl.BlockSpec((T, HD), col), pl.BlockSpec((4, HD), col), pl.BlockSpec((T, HD), col), ANY],
        out_specs=[pl.BlockSpec((T, HD), col), pl.BlockSpec((4, HD), col)],
        out_shape=[jax.ShapeDtypeStruct((T, NP), BF16), jax.ShapeDtypeStruct((4, 3 * HW), F32)],
        input_output_aliases={3: 0}, compiler_params=_cparams(("parallel",)),
    )(proj, conv_w, dqkv, dproj)


BA_BLK = BA_OFF // LANE


def _gates(proj, a_log, dt_bias):
    def body(x_ref, a_ref, d_ref, o_ref):
        o_ref[...] = _gates_fn(x_ref[...], a_ref[...], d_ref[...])

    return pl.pallas_call(
        body, name="gdn_gates", grid=(1,),
        in_specs=[pl.BlockSpec((T, LANE), lambda i: (0, BA_BLK)), _whole((1, LANE)), _whole((1, LANE))],
        out_specs=_whole((T, LANE)),
        out_shape=jax.ShapeDtypeStruct((T, LANE), F32), compiler_params=_cparams(("arbitrary",)),
    )(proj, a_log, dt_bias)


def _gates_bwd(proj, a_log, dt_bias, dbg, dproj):
    def body(x_ref, a_ref, d_ref, dbg_ref, _, dx_ref, da_ref, dd_ref):
        _, vjp = jax.vjp(_gates_fn, x_ref[...], a_ref[...], d_ref[...])
        dx, da_ref[...], dd_ref[...] = vjp(dbg_ref[...])
        dx_ref[...] = dx.astype(BF16)

    ba = pl.BlockSpec((T, LANE), lambda i: (0, BA_BLK))
    return pl.pallas_call(
        body, name="gdn_gates_bwd", grid=(1,),
        in_specs=[ba, _whole((1, LANE)), _whole((1, LANE)), _whole((T, LANE)), ANY],
        out_specs=[ba, _whole((1, LANE)), _whole((1, LANE))],
        out_shape=[jax.ShapeDtypeStruct((T, NP), BF16), jax.ShapeDtypeStruct((1, LANE), F32),
                   jax.ShapeDtypeStruct((1, LANE), F32)],
        input_output_aliases={4: 0}, compiler_params=_cparams(("arbitrary",)),
    )(proj, a_log, dt_bias, dbg, dproj)


NCK = T // CH
CPS = 4


def _chunk_group_specs(at):
    wide = pl.BlockSpec((CPS * CH, HW), lambda n: (at(n), 0))
    return [wide, wide, wide, wide, pl.BlockSpec((HEADS, CPS * CH, CH), lambda n: (0, at(n), 0)),
            pl.BlockSpec((CPS, 1, HW), lambda n: (at(n), 0, 0))]


def _chunk_prep_shapes(dtypes):
    shp = [(T, HW), (T, HW), (T, HW), (T, HW), (HEADS, T, CH), (NCK, 1, HW)]
    return [jax.ShapeDtypeStruct(s, dt) for s, dt in zip(shp, dtypes)]


NGROUP = NCK // CPS
PREP_DTYPES = (F32, BF16, BF16, BF16, BF16, F32)


def _delta_rule(qkv, bg):
    def body(x_ref, bg_ref, *refs):
        prep_out, (o_ref, sh_ref), held, s_ref = refs[:6], refs[6:8], refs[8:14], refs[14]
        i = pl.program_id(0)

        @pl.when(i == 0)
        def _():
            for r in held + (s_ref,):
                r[...] = jnp.zeros_like(r)

        rows = [slice(ci * CH, (ci + 1) * CH) for ci in range(CPS)]
        u_h, w_h, qe_h, kd_h, qk_h, egl_h = held
        chunks = [_head_args((u_h.at[r, :], w_h.at[r, :], qe_h.at[r, :], kd_h.at[r, :], qk_h.at[:, r, :], egl_h.at[ci]))
                  for ci, r in enumerate(rows)]
        start = [jnp.where(i <= 1, 0.0, s_ref[h]) for h in range(HEADS)]
        outs, befores, final = [], [], []
        levels = _state_levels(chunks, start, outs, befores, final)
        res = _chunk_prep_fn([x_ref[r, :] for r in rows], [bg_ref[r, :] for r in rows], tick=lambda: next(levels, None))
        for _ in levels:
            pass
        for ci, (u, w, qe, kd, qk, egl) in enumerate(res):
            for refs_pair, val in zip(zip(prep_out[:4], held[:4]), (u, w, qe, kd)):
                for ref in refs_pair:
                    ref[rows[ci], :] = val.astype(ref.dtype)
            for ref in (prep_out[4], qk_h):
                ref[:, rows[ci], :] = qk.astype(ref.dtype)
            for ref in (prep_out[5], egl_h):
                ref[ci] = egl
        for ci, r in enumerate(rows):
            for h in range(HEADS):
                o_ref[r, h * HD:(h + 1) * HD] = outs[ci][h]
                sh_ref[h, ci] = befores[ci][h]
        for h in range(HEADS):
            s_ref[h] = final[0][h]

    now = lambda n: jnp.minimum(n, NGROUP - 1)
    was = lambda n: jnp.maximum(n - 1, 0)
    wide = lambda at: pl.BlockSpec((CPS * CH, HW), lambda n: (at(n), 0))
    held = [pltpu.VMEM(s, dt) for s, dt in zip(
        [(CPS * CH, HW)] * 4 + [(HEADS, CPS * CH, CH), (CPS, 1, HW)], PREP_DTYPES)]
    out = pl.pallas_call(
        body, name="gdn_delta_rule", grid=(NGROUP + 1,),
        in_specs=[pl.BlockSpec((CPS * CH, 3 * HW), lambda n: (now(n), 0)),
                  pl.BlockSpec((CPS * CH, LANE), lambda n: (now(n), 0))],
        out_specs=[wide(now)] * 4 + [pl.BlockSpec((HEADS, CPS * CH, CH), lambda n: (0, now(n), 0)),
                                     pl.BlockSpec((CPS, 1, HW), lambda n: (now(n), 0, 0)), wide(was),
                                     pl.BlockSpec((HEADS, CPS, HD, HD), lambda n: (0, was(n), 0, 0))],
        out_shape=_chunk_prep_shapes(PREP_DTYPES) + [jax.ShapeDtypeStruct((T, HW), F32),
                                                     jax.ShapeDtypeStruct((HEADS, NCK, HD, HD), F32)],
        scratch_shapes=held + [pltpu.VMEM((HEADS, HD, HD), F32)], compiler_params=_cparams(("arbitrary",)),
    )(qkv, bg)
    return out[:6], out[6], out[7]


def _chunk_prep_bwd(qkv, bg, cots):
    def body(x_ref, bg_ref, du, dw, dqe, dkd, dqk, degl, dx_ref, dbg_ref):
        rows = [slice(ci * CH, (ci + 1) * CH) for ci in range(CPS)]
        _, vjp = jax.vjp(_chunk_prep_fn, [x_ref[r, :] for r in rows], [bg_ref[r, :] for r in rows])
        dxs, dbgs = vjp([(du[r, :], dw[r, :], dqe[r, :], dkd[r, :], dqk[:, r, :], degl[ci])
                         for ci, r in enumerate(rows)])
        for r, dx, dbg in zip(rows, dxs, dbgs):
            dx_ref[r, :] = dx
            dbg_ref[r, :] = dbg

    wide = pl.BlockSpec((CPS * CH, HW), lambda n: (n, 0))
    return pl.pallas_call(
        body, name="gdn_chunk_prep_bwd", grid=(NCK // CPS,),
        in_specs=[pl.BlockSpec((CPS * CH, 3 * HW), lambda n: (n, 0)), pl.BlockSpec((CPS * CH, LANE), lambda n: (n, 0)),
                  wide, wide, wide, wide, pl.BlockSpec((HEADS, CPS * CH, CH), lambda n: (0, n, 0)),
                  pl.BlockSpec((CPS, 1, HW), lambda n: (n, 0, 0))],
        out_specs=[pl.BlockSpec((CPS * CH, 3 * HW), lambda n: (n, 0)), pl.BlockSpec((CPS * CH, LANE), lambda n: (n, 0))],
        out_shape=[jax.ShapeDtypeStruct((T, 3 * HW), F32), jax.ShapeDtypeStruct((T, LANE), F32)],
        compiler_params=_cparams(("parallel",)),
    )(qkv, bg, *cots)


def _head_args(refs):
    u, w, qe, kd, qk, egl = refs
    sls = [slice(h * HD, (h + 1) * HD) for h in range(HEADS)]
    return ([u[:, sl] for sl in sls], [w[:, sl].astype(F32) for sl in sls], [qe[:, sl].astype(F32) for sl in sls],
            [kd[:, sl].astype(F32) for sl in sls], [qk[h].astype(F32) for h in range(HEADS)],
            [egl[:, sl] for sl in sls])


def _chunk_scan_bwd(prep, s_hist, do, after=()):
    n_in = 8 + len(after)

    def body(*refs):
        u_r, w_r, qe_r, kd_r, qk_r, egl_r, sh_ref, do_ref = refs[:8]
        d_refs = refs[n_in:n_in + 6]
        ds_ref = refs[n_in + 6]

        @pl.when(pl.program_id(0) == 0)
        def _():
            ds_ref[...] = jnp.zeros_like(ds_ref)

        sls = [slice(h * HD, (h + 1) * HD) for h in range(HEADS)]
        ds = [ds_ref[h] for h in range(HEADS)]
        for ci in reversed(range(CPS)):
            r = slice(ci * CH, (ci + 1) * CH)
            args = _head_args((u_r.at[r, :], w_r.at[r, :], qe_r.at[r, :], kd_r.at[r, :], qk_r.at[:, r, :], egl_r.at[ci]))
            _, vjp = jax.vjp(_chunk_state_fn, *args, [sh_ref[h, ci] for h in range(HEADS)])
            du, dw, dqe, dkd, dqk, degl, ds = vjp(([do_ref[r, sl] for sl in sls], ds))
            for h, sl in enumerate(sls):
                for d_ref, val in zip(d_refs[:4], (du, dw, dqe, dkd)):
                    d_ref[r, sl] = val[h]
                d_refs[4][h, r, :] = dqk[h]
                d_refs[5][ci, :, sl] = degl[h]
        for h in range(HEADS):
            ds_ref[h] = ds[h]

    rev = lambda n: NGROUP - 1 - n
    return pl.pallas_call(
        body, name="gdn_scan_bwd", grid=(NGROUP,),
        in_specs=_chunk_group_specs(rev) + [pl.BlockSpec((HEADS, CPS, HD, HD), lambda n: (0, rev(n), 0, 0)),
                                            pl.BlockSpec((CPS * CH, HW), lambda n: (rev(n), 0))] + [ANY] * len(after),
        out_specs=_chunk_group_specs(rev), out_shape=_chunk_prep_shapes((F32,) * 6),
        scratch_shapes=[pltpu.VMEM((HEADS, HD, HD), F32)], compiler_params=_cparams(("arbitrary",)),
    )(*prep, s_hist, do, *after)


def _mix_specs():
    pc = lambda c: pl.BlockSpec((GCH, HW), lambda i: (i, c))
    return [pl.BlockSpec((GCH, HW), lambda i: (i, 0)), pc(3), pc(4), pc(5), _whole((1, HD)), _whole((1, HW)),
            _whole((1, HW)), _whole((HEADS, GCH, GCH)), _whole((GCH, LANE))]


def _mix(o, proj, ong, lng, lnb, ws, bst, after=()):
    def body(o_ref, z_ref, u_ref, v_ref, ong_ref, lng_ref, lnb_ref, ws_ref, bs_ref, *rest):
        rest[-1][...] = _mix_fn(o_ref[...], z_ref[...], u_ref[...], v_ref[...], ong_ref[...], lng_ref[...],
                                lnb_ref[...], ws_ref[...], bs_ref[...]).astype(BF16)

    return pl.pallas_call(
        body, name="mix", grid=(T // GCH,), in_specs=_mix_specs() + [ANY] * len(after),
        out_specs=pl.BlockSpec((GCH, D), lambda i: (i, 0)), out_shape=jax.ShapeDtypeStruct((T, D), BF16),
        compiler_params=_cparams(("parallel",)),
    )(o, proj, proj, proj, ong, lng, lnb, ws, bst, *after)


def _mix_bwd(o, proj, ong, lng, lnb, ws, bst, dmix):
    def body(o_ref, z_ref, u_ref, v_ref, ong_ref, lng_ref, lnb_ref, ws_ref, bs_ref, dm_ref,
             do_ref, dzuv_ref, dong_ref, dlng_ref, dlnb_ref, dws_ref, dbs_ref):
        _, vjp = jax.vjp(_mix_fn, o_ref[...], z_ref[...], u_ref[...], v_ref[...], ong_ref[...], lng_ref[...],
                         lnb_ref[...], ws_ref[...], bs_ref[...])
        do, dz, du, dv, dong, dlng, dlnb, dws, dbs = vjp(dm_ref[...])
        do_ref[...] = do
        dzuv_ref[:, 0:HW] = dz.astype(BF16)
        dzuv_ref[:, HW:2 * HW] = du.astype(BF16)
        dzuv_ref[:, 2 * HW:3 * HW] = dv.astype(BF16)
        acc = [(dong_ref, dong), (dlng_ref, dlng), (dlnb_ref, dlnb), (dws_ref, dws), (dbs_ref, dbs)]

        @pl.when(pl.program_id(0) == 0)
        def _():
            for r, val in acc:
                r[...] = val

        @pl.when(pl.program_id(0) > 0)
        def _():
            for r, val in acc:
                r[...] += val

    shp = lambda *s: jax.ShapeDtypeStruct(s, F32)
    return pl.pallas_call(
        body, name="mix_bwd", grid=(T // GCH,),
        in_specs=_mix_specs() + [pl.BlockSpec((GCH, D), lambda i: (i, 0))],
        out_specs=[pl.BlockSpec((GCH, HW), lambda i: (i, 0)), pl.BlockSpec((GCH, 3 * HW), lambda i: (i, 1)),
                   _whole((1, HD)), _whole((1, HW)), _whole((1, HW)), _whole((HEADS, GCH, GCH)), _whole((GCH, LANE))],
        out_shape=[shp(T, HW), jax.ShapeDtypeStruct((T, NP), BF16), shp(1, HD), shp(1, HW), shp(1, HW),
                   shp(HEADS, GCH, GCH), shp(GCH, LANE)],
        compiler_params=_cparams(("arbitrary",)),
    )(o, proj, proj, proj, ong, lng, lnb, ws, bst, dmix)


def _swiglu_epilogue(accs, _):
    gate, up = accs
    return [gate, up, _silu(gate) * up]


def _swiglu_bwd_epilogue(accs, extras):
    dact = accs[0]
    gate, up = (e.astype(F32) for e in extras)
    sg = _sigmoid(gate)
    return [dact * up * (sg * (1.0 + gate * (1.0 - sg))), dact * (gate * sg)]


def _layer_fwd(h, p):
    hn = p.pop("hn") if "hn" in p else _rmsnorm("rms_mix", h, p["norm_mix"])
    proj = _mm("in_proj", "nn", hn[None], [p["w_in"][None]], tm=1024, tn=640, tk=D, sub_m=2)[0][0]
    qkv = _prep(proj, p["conv_w"])
    bg = _gates(proj, p["a_log"], p["dt_bias"])
    prep, o, s_hist = _delta_rule(qkv, bg)
    mix = _mix(o, proj, p["o_norm_g"], p["ln_v_g"], p["ln_v_b"], p["w_s"], p["bst"],
               p.pop("before_mix")(o) if "before_mix" in p else ())
    if "late" in p:
        p.update(p.pop("late")(mix))
    h1 = _mm("out_proj", "nn", mix[None], [p["w_out"].reshape(1, D, D)], tm=T, tn=512, tk=D, resid=h[None],
             sub_m=4)[0][0]
    h2n = _rmsnorm("rms_ffn", h1, p["norm_ffn"])
    gate, up, act = _mm("ffn_in", "nt", h2n[None], [p["w_gate"], p["w_up"]], tm=1024, tn=FF_SH, tk=D,
                        out_dtypes=(BF16, BF16, BF16), epilogue=_swiglu_epilogue, sub_m=4)
    then = p.pop("before_ffn_out")(act) if "before_ffn_out" in p else ()
    h2 = _mm("ffn_out", "nn", act, [p["w_down"]], tm=1024, tn=512, tk=FF_SH, reduce_g=True, fold_g=True,
             resid=h1[None], sub_m=2, after=then)[0][0]
    saved = dict(h=h, hn=hn, proj=proj, qkv=qkv, bg=bg, prep=prep, o=o, s_hist=s_hist, mix=mix, h1=h1, h2n=h2n,
                 gate=gate, up=up, act=act)
    return h2, saved


def _layer_bwd_ffn(dh2, dh2b, p, s, after=()):
    dh2b = dh2b[None]
    dgate, dup = _mm("ffn_out_bwd", "nt", dh2b, [p["w_down"]], tm=1024, tn=FF_SH, tk=D, out_dtypes=(BF16, BF16),
                     extras=(s["gate"], s["up"]), epilogue=_swiglu_bwd_epilogue, after=after, sub_m=4)
    dh2n = _mm("ffn_gate_bwd", "nn", dgate, [p["w_gate"]], tm=1024, tn=512, tk=FF_SH, reduce_g=True, fold_g=True,
               sub_m=2)[0]
    dh2n = _mm("ffn_up_bwd", "nn", dup, [p["w_up"]], tm=1024, tn=512, tk=FF_SH, reduce_g=True, fold_g=True,
               resid=dh2n, sub_m=2)[0][0]
    dh1, dh1b, d_norm_ffn = _rmsnorm_bwd("rms_ffn_bwd", dh2n, s["h1"], p["norm_ffn"], dh2)
    d_w_down = _mm("ffn_wdown_grad", "tn", s["act"], [dh2b], tm=FF_SH, tn=512, tk=T)[0]
    d_w_gate = _mm("ffn_wgate_grad", "tn", dgate, [s["h2n"][None]], tm=FF_SH, tn=512, tk=T)[0]
    d_w_up = _mm("ffn_wup_grad", "tn", dup, [s["h2n"][None]], tm=FF_SH, tn=512, tk=T)[0]
    return dh1, dh1b, dict(norm_ffn=d_norm_ffn, w_gate=d_w_gate, w_up=d_w_up, w_down=d_w_down)


def _layer_bwd_mixer(dh1, dh1b, p, s, after=(), midway=None, late=None):
    dh1b = dh1b[None]
    dmix = _mm("out_proj_bwd", "nt", dh1b, [p["w_out"].reshape(1, D, D)], tm=T, tn=512, tk=D, after=after,
               sub_m=4)[0][0]
    d_w_out = _mm("out_proj_wgrad", "tn", s["mix"][None], [dh1b], tm=1024, tn=512, tk=T)[0][0]
    do, dproj, d_ong, d_lng, d_lnb, d_ws, d_bst = _mix_bwd(
        s["o"], s["proj"], p["o_norm_g"], p["ln_v_g"], p["ln_v_b"], p["w_s"], p["bst"], dmix)
    then = midway(do) if midway is not None else ()
    dqkv, dbg = _chunk_prep_bwd(s["qkv"], s["bg"], _chunk_scan_bwd(s["prep"], s["s_hist"], do, then))
    dproj, d_conv = _prep_bwd(s["proj"], p["conv_w"], dqkv, dproj)
    dproj, d_a_log, d_dt_bias = _gates_bwd(s["proj"], p["a_log"], p["dt_bias"], dbg, dproj)
    dproj = dproj[None]
    d_w_in = _mm("in_proj_wgrad", "tn", s["hn"][None], [dproj], tm=512, tn=640, tk=T)[0]
    last = late(dict(w_in=d_w_in, w_out=d_w_out)) if late is not None else ()
    dhn = _mm("in_proj_bwd", "nt", dproj, [p["w_in"][None]], tm=1024, tn=512, tk=NP, after=last,
              sub_m=2)[0][0]
    dh, dhb, d_norm_mix = _rmsnorm_bwd("rms_mix_bwd", dhn, s["h"], p["norm_mix"], dh1)
    grads = dict(norm_mix=d_norm_mix, w_in=d_w_in, conv_w=d_conv, a_log=d_a_log, dt_bias=d_dt_bias, o_norm_g=d_ong,
                 ln_v_g=d_lng, ln_v_b=d_lnb, w_s=d_ws, bst=d_bst, w_out=d_w_out)
    return dh, dhb, grads


def _lanes(v, off=0):
    return jnp.zeros((1, LANE), F32).at[0, off:off + v.shape[0]].set(v)


def _w_in_pieces():
    regions = [(0, 2048, 0), (2048, 2056, BA_OFF), (2056, IN_DIM, 2048)]
    sh = IN_DIM // NCHIP
    out = []
    for j in range(NCHIP):
        for lo, hi, at in regions:
            a, b = max(lo, j * sh), min(hi, (j + 1) * sh)
            if a < b:
                out.append((j, a - j * sh, at + a - lo, b - a))
    return out


W_IN_PIECES = _w_in_pieces()
WT = 256


def _assemble_w_in(gathered, own, place):
    def body(place_ref, g_ref, own_ref, o_ref):
        o_ref[:, IN_DIM:] = jnp.zeros((WT, NP - IN_DIM), BF16)
        mine = own_ref[...]
        for j, src, dst, width in W_IN_PIECES:
            val = jnp.where(place_ref[0] == j, mine[:, src:src + width], g_ref[j, :, src:src + width])
            o_ref[:, dst:dst + width] = val

    sh = IN_DIM // NCHIP
    return pl.pallas_call(
        body, name="assemble_w_in",
        grid_spec=pltpu.PrefetchScalarGridSpec(
            num_scalar_prefetch=1, grid=(D // WT,),
            in_specs=[pl.BlockSpec((NCHIP, WT, sh), lambda i, place_ref: (0, i, 0)),
                      pl.BlockSpec((WT, sh), lambda i, place_ref: (i, 0))],
            out_specs=pl.BlockSpec((WT, NP), lambda i, place_ref: (i, 0))),
        out_shape=jax.ShapeDtypeStruct((D, NP), BF16), compiler_params=_cparams(("parallel",)),
    )(place, gathered, own)


def _layer_params(l, big, small):
    return dict(
        {k: v for k, v in big.items() if k != "conv_w"},
        conv_w=jnp.concatenate([big["conv_w"][j, l] for j in range(NCHIP)], axis=1),
        norm_mix=small["norm_mix"][l][None], norm_ffn=small["norm_ffn"][l][None],
        a_log=_lanes(small["a_log"][l], HEADS), dt_bias=_lanes(small["dt_bias"][l], HEADS),
        o_norm_g=small["o_norm_g"][l][None], ln_v_g=small["ln_v_g"][l][None], ln_v_b=small["ln_v_b"][l][None],
        w_s=small["w_s"][l],
        bst=jnp.pad(small["b_s"][l].T, ((0, 0), (0, LANE - HEADS))),
    )


def _reference_layout(g):
    return dict(
        w_in=g["w_in"],
        w_out=g["w_out"].reshape(NCHIP, D // NCHIP, D),
        w_gate=g["w_gate"], w_up=g["w_up"], w_down=g["w_down"],
        conv_w=g["conv_w"], norm_mix=g["norm_mix"][0], norm_ffn=g["norm_ffn"][0],
        a_log=g["a_log"][0, HEADS:2 * HEADS], dt_bias=g["dt_bias"][0, HEADS:2 * HEADS],
        o_norm_g=g["o_norm_g"][0], ln_v_g=g["ln_v_g"][0], ln_v_b=g["ln_v_b"][0], w_s=g["w_s"],
        b_s=g["bst"][:, :HEADS].T,
    )


def _forward(x, tgt, layers, norm_final):
    h = x
    saved, params = [], []
    for p in layers:
        p = p(h) if callable(p) else p
        h, s = _layer_fwd(h, p)
        saved.append(s)
        params.append(p)
    return (saved, params) + tuple(_loss_head(h, norm_final, tgt))


def _local_step(x, tgt, layers, norm_final):
    saved, layers, loss, dh, dhb, d_norm_final = _forward(x, tgt, layers, norm_final)
    grads = [None] * DEPTH
    for l in reversed(range(DEPTH)):
        dh1, dh1b, g_ffn = _layer_bwd_ffn(dh, dhb, layers[l], saved[l])
        dh, dhb, g_mix = _layer_bwd_mixer(dh1, dh1b, layers[l], saved[l])
        grads[l] = {**g_ffn, **g_mix}
    return loss, dh, grads, d_norm_final


def _place():
    x, y, c = lax.axis_index("x"), lax.axis_index("y"), lax.axis_index("c")
    return x, y, c, [(1 - x, y), (x, 1 - y), (1 - x, 1 - y)]


def _remote(src, dst, send_sem, recv_sem, to):
    return pltpu.make_async_remote_copy(src_ref=src, dst_ref=dst, send_sem=send_sem, recv_sem=recv_sem,
                                        device_id=to, device_id_type=MESH)


def _comm_call(name, body, ins, out_shape, n_sems, aliases=None):
    return pl.pallas_call(
        body, name=name, in_specs=[ANY] * len(ins), out_specs=[ANY] * len(out_shape), out_shape=out_shape,
        scratch_shapes=[pltpu.SemaphoreType.DMA((n,)) for n in n_sems], input_output_aliases=aliases or {},
        compiler_params=pltpu.CompilerParams(has_side_effects=True),
    )(*ins)


def _half_rows(ref, of_c, dim):
    hr = ref.shape[dim] // 2
    return pl.ds(pl.multiple_of(of_c * hr, BF16_ROWS), hr)


def _gather_plan(whole):
    def plan(srcs, lands):
        x, y, c, others = _place()
        chip = 2 * x + y
        out = []
        for src, land, all_of_it in zip(srcs, lands, whole):
            for ox, oy in others:
                if all_of_it:
                    out.append((src, land.at[chip], (ox, oy, c)))
                else:
                    out.append((src.at[_half_rows(src, c, 0)], land.at[chip, _half_rows(src, c, 0)], (ox, oy, c)))
        return out
    return plan


def _forward_halves(lands):
    n = len(lands)

    def body(*refs):
        outs = refs[n:2 * n]
        send_s, recv_s = refs[2 * n:]
        x, y, c, others = _place()
        sibling = (x, y, 1 - c)
        copies = []
        for a in range(n):
            for k, (ox, oy) in enumerate(others):
                mine = outs[a].at[2 * ox + oy, _half_rows(outs[a], c, 1)]
                copies.append(_remote(mine, mine, send_s.at[3 * a + k], recv_s.at[3 * a + k], sibling))
        for cp in copies:
            cp.start()
        for a in range(n):
            for k, (ox, oy) in enumerate(others):
                landed = outs[a].at[2 * ox + oy, _half_rows(outs[a], 1 - c, 1)]
                _remote(landed, landed, send_s.at[3 * a + k], recv_s.at[3 * a + k], sibling).wait_recv()
        for cp in copies:
            cp.wait_send()

    out_shape = [jax.ShapeDtypeStruct(g.shape, g.dtype) for g in lands]
    return _comm_call("forward_halves", body, lands, out_shape, [3 * n, 3 * n], aliases={a: a for a in range(n)})


def _forward_refs(bufs, incoming):
    x, y, c, others = _place()
    return (x, y, 1 - c), [b.at[2 * ox + oy, _half_rows(b, 1 - c if incoming else c, 1)]
                           for b in bufs for ox, oy in others]


def _forward_start(name, bufs, after):
    n = len(bufs)
    bufs = [pltpu.with_memory_space_constraint(b, pltpu.HBM) for b in bufs]

    def body(*refs):
        send_s, recv_s = refs[n + len(after)], refs[n + len(after) + 1]
        sibling, mine = _forward_refs(refs[:n], incoming=False)
        for i, ref in enumerate(mine):
            _remote(ref, ref, send_s.at[i], recv_s.at[i], sibling).start()
        refs[-1][...] = jnp.zeros_like(refs[-1])

    out = pl.pallas_call(
        body, name=name, in_specs=[HBM_SPEC] * n + [ANY] * len(after),
        out_specs=[SEM_SPEC, SEM_SPEC] + [HBM_SPEC] * n + [pl.BlockSpec(memory_space=pltpu.VMEM)],
        out_shape=[pltpu.SemaphoreType.DMA((3 * n,)), pltpu.SemaphoreType.DMA((3 * n,))]
        + [pltpu.HBM(b.shape, b.dtype) for b in bufs] + [jax.ShapeDtypeStruct((F32_ROWS, LANE), F32)],
        input_output_aliases={i: 2 + i for i in range(n)},
        compiler_params=pltpu.CompilerParams(has_side_effects=DATAFLOW),
    )(*bufs, *after)
    return dict(sems=out[:2], bufs=out[2:2 + n], token=out[-1])


def _forward_wait(name, started, after):
    n = len(started["bufs"])

    def body(*refs):
        send_s, recv_s = refs[n], refs[n + 1]
        sibling, mine = _forward_refs(refs[:n], incoming=False)
        _, theirs = _forward_refs(refs[:n], incoming=True)
        for i, (sent, landed) in enumerate(zip(mine, theirs)):
            _remote(sent, sent, send_s.at[i], recv_s.at[i], sibling).wait_send()
            _remote(landed, landed, send_s.at[i], recv_s.at[i], sibling).wait_recv()

    return pl.pallas_call(
        body, name=name, in_specs=[HBM_SPEC] * n + [SEM_SPEC, SEM_SPEC] + [ANY] * len(after),
        out_specs=[HBM_SPEC] * n, out_shape=[pltpu.HBM(b.shape, b.dtype) for b in started["bufs"]],
        input_output_aliases={i: i for i in range(n)},
        compiler_params=pltpu.CompilerParams(has_side_effects=DATAFLOW),
    )(*started["bufs"], *started["sems"], *after)


HBM_SPEC = pl.BlockSpec(memory_space=pltpu.HBM)
SEM_SPEC = pl.BlockSpec(memory_space=pltpu.SEMAPHORE)
DATAFLOW = pltpu.SideEffectType.DATAFLOW_SIDE_EFFECTING


def _exchange_plan(srcs, lands):
    x, y, c, _ = _place()
    plan = []
    for src, land in zip(srcs, lands):
        hr = src.shape[1] // 2
        plan.append((src.at[:, pl.ds(pl.multiple_of((1 - c) * hr, 8), hr)], land, (x, y, 1 - c)))
    return plan


def _scatter_plan(srcs, lands):
    x, y, c, others = _place()
    return [(src.at[2 * ox + oy], land.at[k], (ox, oy, c))
            for src, land in zip(srcs, lands) for k, (ox, oy) in enumerate(others)]


def _split_start(name, plan, srcs, land_shapes, n_copies, after=()):
    n = len(srcs)
    lands = [pltpu.with_memory_space_constraint(lax.empty(s.shape, s.dtype), pltpu.HBM) for s in land_shapes]
    srcs = [pltpu.with_memory_space_constraint(s, pltpu.HBM) for s in srcs]

    def body(*refs):
        send_s, recv_s = refs[2 * n + len(after)], refs[2 * n + len(after) + 1]
        for i, (src, dst, to) in enumerate(plan(refs[:n], refs[n:2 * n])):
            _remote(src, dst, send_s.at[i], recv_s.at[i], to).start()
        refs[-1][...] = jnp.zeros_like(refs[-1])

    thru = [pltpu.HBM(s.shape, s.dtype) for s in srcs + lands]
    out = pl.pallas_call(
        body, name=name, in_specs=[HBM_SPEC] * (2 * n) + [ANY] * len(after),
        out_specs=[SEM_SPEC, SEM_SPEC] + [HBM_SPEC] * (2 * n) + [pl.BlockSpec(memory_space=pltpu.VMEM)],
        out_shape=[pltpu.SemaphoreType.DMA((n_copies,)), pltpu.SemaphoreType.DMA((n_copies,))] + thru
        + [jax.ShapeDtypeStruct((F32_ROWS, LANE), F32)],
        input_output_aliases={i: 2 + i for i in range(2 * n)},
        compiler_params=pltpu.CompilerParams(has_side_effects=DATAFLOW),
    )(*srcs, *lands, *after)
    return dict(sems=out[:2], srcs=out[2:2 + n], lands=out[2 + n:2 + 2 * n], token=out[-1])


def _split_wait(name, plan, started, after):
    n = len(started["srcs"])
    after = list(after) if isinstance(after, (list, tuple)) else [after]

    def body(*refs):
        send_s, recv_s = refs[2 * n], refs[2 * n + 1]
        for i, (src, dst, to) in enumerate(plan(refs[:n], refs[n:2 * n])):
            cp = _remote(src, dst, send_s.at[i], recv_s.at[i], to)
            cp.wait_send()
            cp.wait_recv()

    arrs = list(started["srcs"]) + list(started["lands"])
    out = pl.pallas_call(
        body, name=name, in_specs=[HBM_SPEC] * (2 * n) + [SEM_SPEC, SEM_SPEC] + [ANY] * len(after),
        out_specs=[HBM_SPEC] * (2 * n), out_shape=[pltpu.HBM(s.shape, s.dtype) for s in arrs],
        input_output_aliases={i: i for i in range(2 * n)},
        compiler_params=pltpu.CompilerParams(has_side_effects=DATAFLOW),
    )(*arrs, *started["sems"], *after)
    return out[:n], out[n:]


def _join_halves(name, rs):
    n = len(rs)

    def body(*refs):
        outs = refs[n:2 * n]
        send_s, recv_s = refs[2 * n:]
        x, y, c, _ = _place()
        sibling = (x, y, 1 - c)

        def half(a, of_c):
            hr = outs[a].shape[1] // 2
            return outs[a].at[:, pl.ds(pl.multiple_of(of_c * hr, 8), hr)]

        copies = [_remote(half(a, c), half(a, c), send_s.at[a], recv_s.at[a], sibling) for a in range(n)]
        for cp in copies:
            cp.start()
        for a in range(n):
            landed = half(a, 1 - c)
            _remote(landed, landed, send_s.at[a], recv_s.at[a], sibling).wait_recv()
        for cp in copies:
            cp.wait_send()

    out_shape = [jax.ShapeDtypeStruct(r.shape, r.dtype) for r in rs]
    return _comm_call(name, body, rs, out_shape, [n, n], aliases={a: a for a in range(n)})


def _allreduce_small(buf, after=()):
    r = buf.shape[0]
    hr = r // 2

    def body(in_ref, *refs):
        out_ref, theirs, by_chip, send_s, recv_s = refs[len(after):]
        x, y, c, others = _place()
        chip = 2 * x + y
        sibling = (x, y, 1 - c)
        mine = pl.ds(pl.multiple_of(c * hr, F32_ROWS), hr)
        swap = _remote(in_ref, theirs, send_s.at[0], recv_s.at[0], sibling)
        swap.start()
        swap.wait()
        by_chip[chip] = in_ref[mine, :] + theirs[mine, :]
        sends = [_remote(by_chip.at[chip], by_chip.at[chip], send_s.at[1 + k], recv_s.at[1 + k], (ox, oy, c))
                 for k, (ox, oy) in enumerate(others)]
        for cp in sends:
            cp.start()
        for k, (ox, oy) in enumerate(others):
            landed = by_chip.at[2 * ox + oy]
            _remote(landed, landed, send_s.at[1 + k], recv_s.at[1 + k], (ox, oy, c)).wait_recv()
        for cp in sends:
            cp.wait_send()
        out_ref[mine, :] = (by_chip[0] + by_chip[1]) + (by_chip[2] + by_chip[3])
        back = _remote(out_ref.at[mine], out_ref.at[mine], send_s.at[NCHIP], recv_s.at[NCHIP], sibling)
        back.start()
        other = out_ref.at[pl.ds(pl.multiple_of((1 - c) * hr, F32_ROWS), hr)]
        _remote(other, other, send_s.at[NCHIP], recv_s.at[NCHIP], sibling).wait_recv()
        back.wait_send()

    vm = pl.BlockSpec(memory_space=pltpu.VMEM)
    return pl.pallas_call(
        body, name="allreduce_small", in_specs=[vm] + [ANY] * len(after), out_specs=vm,
        out_shape=jax.ShapeDtypeStruct((r, LANE), F32),
        scratch_shapes=[pltpu.VMEM((r, LANE), F32), pltpu.VMEM((NCHIP, hr, LANE), F32),
                        pltpu.SemaphoreType.DMA((NCHIP + 1,)), pltpu.SemaphoreType.DMA((NCHIP + 1,))],
        compiler_params=pltpu.CompilerParams(has_side_effects=True, vmem_limit_bytes=VMEM_LIMIT),
    )(buf, *after)


MAX_ROW_TILE = 512
BF16_ROWS = 16


def _row_tile(rows):
    for t in range(min(rows, MAX_ROW_TILE) // BF16_ROWS * BF16_ROWS, 0, -BF16_ROWS):
        if rows % t == 0:
            return t
    raise ValueError(rows)


def _sum_halves(g, theirs, c_arr):
    nch, rows, cols = g.shape
    hr = rows // 2
    tr = _row_tile(hr)

    def body(c_ref, g_ref, t_ref, o_ref, ob_ref):
        s = g_ref[...] + t_ref[...]
        o_ref[...] = s
        ob_ref[...] = s.astype(BF16)

    blk = pl.BlockSpec((None, tr, cols), lambda j, i, c_ref: (j, i, 0))
    return pl.pallas_call(
        body, name="sum_halves",
        grid_spec=pltpu.PrefetchScalarGridSpec(
            num_scalar_prefetch=1, grid=(nch, hr // tr),
            in_specs=[pl.BlockSpec((None, None, tr, cols), lambda j, i, c_ref: (j, c_ref[0], i, 0)), blk],
            out_specs=[blk, blk]),
        out_shape=[jax.ShapeDtypeStruct((nch, hr, cols), F32), jax.ShapeDtypeStruct((nch, hr, cols), BF16)],
        compiler_params=_cparams(("parallel", "parallel")),
    )(c_arr, g.reshape(nch, 2, hr, cols), theirs)


def _sum_halves_w_in(g, theirs, c_arr):
    hr = D // 2
    sh = IN_DIM // NCHIP

    def body(c_ref, g_ref, t_ref, o_ref, ob_ref):
        s = g_ref[...] + t_ref[...]
        for j, dst, src, width in W_IN_PIECES:
            o_ref[j, :, dst:dst + width] = s[:, src:src + width]
            ob_ref[j, :, dst:dst + width] = s[:, src:src + width].astype(BF16)

    out = pl.BlockSpec((NCHIP, WT, sh), lambda i, c_ref: (0, i, 0))
    return pl.pallas_call(
        body, name="sum_halves_w_in",
        grid_spec=pltpu.PrefetchScalarGridSpec(
            num_scalar_prefetch=1, grid=(hr // WT,),
            in_specs=[pl.BlockSpec((None, WT, NP), lambda i, c_ref: (c_ref[0], i, 0)),
                      pl.BlockSpec((None, WT, NP), lambda i, c_ref: (0, i, 0))],
            out_specs=[out, out]),
        out_shape=[jax.ShapeDtypeStruct((NCHIP, hr, sh), F32), jax.ShapeDtypeStruct((NCHIP, hr, sh), BF16)],
        compiler_params=_cparams(("parallel",)),
    )(c_arr, g.reshape(2, hr, NP), theirs)


def _sum_chips(p, q, place, l, into=None, after=()):
    extra = ([into] if into is not None else []) + list(after)
    _, rows, cols = p.shape
    tr = _row_tile(rows)
    steps = rows // tr

    def body(place_ref, p_ref, q0, q1, q2, *rest):
        rest[-1][...] = ((p_ref[...] + q0[...].astype(F32)) + q1[...].astype(F32)) + q2[...].astype(F32)

    qs = lambda k: pl.BlockSpec((None, tr, cols), lambda i, place_ref: (k, i, 0))
    return pl.pallas_call(
        body, name="sum_chips",
        grid_spec=pltpu.PrefetchScalarGridSpec(
            num_scalar_prefetch=1, grid=(steps,),
            in_specs=[pl.BlockSpec((None, tr, cols), lambda i, place_ref: (place_ref[0], i, 0)), qs(0), qs(1), qs(2)]
            + [ANY] * len(extra),
            out_specs=pl.BlockSpec((None, tr, cols), lambda i, place_ref: (l, place_ref[1] * steps + i, 0))),
        out_shape=jax.ShapeDtypeStruct((DEPTH, 2 * rows, cols), F32),
        input_output_aliases={5: 0} if into is not None else {},
        compiler_params=_cparams(("parallel",)),
    )(place, p, q, q, q, *extra)


def _adamw_fn(w, g, m, v):
    nm = ADAM_B1 * m + (1.0 - ADAM_B1) * g
    nv = ADAM_B2 * v + (1.0 - ADAM_B2) * jnp.square(g)
    m_hat = nm / (1.0 - ADAM_B1 ** ADAM_STEP)
    v_hat = nv / (1.0 - ADAM_B2 ** ADAM_STEP)
    return -ADAM_LR * (m_hat / (jnp.sqrt(v_hat) + ADAM_EPS) + ADAM_WD * w), nm, nv


def _adamw(w, g, m, v):
    layers, rows, cols = w.shape
    tr = _row_tile(rows)

    def body(w_ref, g_ref, m_ref, v_ref, d_ref, nm_ref, nv_ref):
        d_ref[...], nm_ref[...], nv_ref[...] = _adamw_fn(w_ref[...], g_ref[...], m_ref[...], v_ref[...])

    blk = pl.BlockSpec((None, tr, cols), lambda l, i: (l, i, 0))
    return pl.pallas_call(
        body, name="adamw", grid=(layers, rows // tr), in_specs=[blk] * 4, out_specs=[blk] * 3,
        out_shape=[jax.ShapeDtypeStruct(w.shape, F32)] * 3, compiler_params=_cparams(("parallel", "parallel")),
    )(w, g, m, v)


def _adamw_small(ws, gs, ms, vs):
    n = len(ws)

    def body(*refs):
        for i in range(n):
            w_ref, g_ref, m_ref, v_ref, d_ref, nm_ref, nv_ref = (refs[k * n + i] for k in range(7))
            d_ref[...], nm_ref[...], nv_ref[...] = _adamw_fn(w_ref[...], g_ref[...], m_ref[...], v_ref[...])

    vm = pl.BlockSpec(memory_space=pltpu.VMEM)
    out = pl.pallas_call(
        body, name="adamw_small", in_specs=[vm] * (4 * n), out_specs=[vm] * (3 * n),
        out_shape=[jax.ShapeDtypeStruct(a.shape, F32) for a in list(ws) * 3],
        compiler_params=pltpu.CompilerParams(vmem_limit_bytes=VMEM_LIMIT),
    )(*ws, *gs, *ms, *vs)
    return out[:n], out[n:2 * n], out[2 * n:]


BIG = ("w_in", "w_out", "w_gate", "w_up", "w_down")
SMALL = ("norm_mix", "a_log", "dt_bias", "o_norm_g", "ln_v_g", "ln_v_b", "w_s", "b_s", "norm_ffn", "norm_final")
ORDER = ("norm_mix", "w_in", "conv_w", "a_log", "dt_bias", "o_norm_g", "ln_v_g", "ln_v_b", "w_s", "b_s", "w_out",
         "norm_ffn", "w_gate", "w_up", "w_down", "norm_final")


F32_ROWS = 8
PACK_ROWS = 128


def _lane_rows(size):
    return -(-size // (F32_ROWS * LANE)) * F32_ROWS


def _pack(arrs):
    parts = [jnp.pad(a.reshape(-1), (0, _lane_rows(a.size) * LANE - a.size)).reshape(-1, LANE) for a in arrs]
    rows = sum(p.shape[0] for p in parts)
    if rows % PACK_ROWS:
        parts.append(jnp.zeros((-rows % PACK_ROWS, LANE), F32))
    return jnp.concatenate(parts, axis=0)


def _unpack(buf, like):
    out, row = [], 0
    for a in like:
        n = _lane_rows(a.size)
        out.append(buf[row:row + n].reshape(-1)[:a.size].reshape(a.shape))
        row += n
    return out


def kernel(x, norm_mix, w_in, conv_w, a_log, dt_bias, o_norm_g, ln_v_g, ln_v_b, w_s, b_s, w_out, norm_ffn, w_gate, w_up, w_down, norm_final, loss_target, m_norm_mix, m_w_in, m_conv_w, m_a_log, m_dt_bias, m_o_norm_g, m_ln_v_g, m_ln_v_b, m_w_s, m_b_s, m_w_out, m_norm_ffn, m_w_gate, m_w_up, m_w_down, m_norm_final, v_norm_mix, v_w_in, v_conv_w, v_a_log, v_dt_bias, v_o_norm_g, v_ln_v_g, v_ln_v_b, v_w_s, v_b_s, v_w_out, v_norm_ffn, v_w_gate, v_w_up, v_w_down, v_norm_final):
    w = dict(norm_mix=norm_mix, w_in=w_in, conv_w=conv_w, a_log=a_log, dt_bias=dt_bias, o_norm_g=o_norm_g,
             ln_v_g=ln_v_g, ln_v_b=ln_v_b, w_s=w_s, b_s=b_s, w_out=w_out, norm_ffn=norm_ffn, w_gate=w_gate, w_up=w_up,
             w_down=w_down, norm_final=norm_final)
    m = dict(norm_mix=m_norm_mix, w_in=m_w_in, conv_w=m_conv_w, a_log=m_a_log, dt_bias=m_dt_bias, o_norm_g=m_o_norm_g,
             ln_v_g=m_ln_v_g, ln_v_b=m_ln_v_b, w_s=m_w_s, b_s=m_b_s, w_out=m_w_out, norm_ffn=m_norm_ffn,
             w_gate=m_w_gate, w_up=m_w_up, w_down=m_w_down, norm_final=m_norm_final)
    v = dict(norm_mix=v_norm_mix, w_in=v_w_in, conv_w=v_conv_w, a_log=v_a_log, dt_bias=v_dt_bias, o_norm_g=v_o_norm_g,
             ln_v_g=v_ln_v_g, ln_v_b=v_ln_v_b, w_s=v_w_s, b_s=v_b_s, w_out=v_w_out, norm_ffn=v_norm_ffn,
             w_gate=v_w_gate, w_up=v_w_up, w_down=v_w_down, norm_final=v_norm_final)
    chip = 2 * lax.axis_index("x") + lax.axis_index("y")
    place = jnp.stack([chip, lax.axis_index("c")]).astype(jnp.int32)
    c_arr = place[1:]

    def kernel_view(n, a):
        return jnp.swapaxes(a, 1, 2) if n in ("w_gate", "w_up") else a

    own = {n: [kernel_view(n, w[n])[l].astype(BF16) for l in range(DEPTH)] for n in BIG}
    by_chip = lambda a: jax.ShapeDtypeStruct((NCHIP,) + a.shape, a.dtype)

    def start(name, srcs, whole, after=()):
        return _split_start(name, _gather_plan(whole), srcs, [by_chip(a) for a in srcs], 3 * len(srcs), after)

    def finish(name, started, whole, after):
        srcs, lands = _split_wait(name, _gather_plan(whole), started, after)
        passed = iter(_forward_halves([g for g, all_of_it in zip(lands, whole) if not all_of_it]))
        lands = [g if all_of_it else next(passed) for g, all_of_it in zip(lands, whole)]
        return srcs, [lax.dynamic_update_index_in_dim(g, o, chip, 0) for g, o in zip(lands, srcs)]

    ffn = BIG[1:]
    first = start("gather_first_start", [own["w_in"][0], conv_w], [False, True])
    early = start("gather_early_start", [own[n][0] for n in ffn], [False] * len(ffn), [first["token"]])
    mid = start("gather_mid_start", [own["w_in"][1]], [False], [early["token"]])
    later = start("gather_later_start", [own[n][1] for n in ffn], [False] * len(ffn), [mid["token"]])
    hn = _rmsnorm("rms_mix", x[0], norm_mix[0][None])
    (own_w_in, _), (w_in_by_chip, conv_by_chip) = finish("gather_first_wait", first, [False, True], [later["token"], hn])

    passing = {}

    def pass_on(tag, started, n):
        def at(after):
            srcs, lands = _split_wait(f"gather_{tag}_wait", _gather_plan([False] * n), started, after)
            passing[tag] = srcs, _forward_start(f"forward_{tag}_start", lands, ())
            return [passing[tag][1]["token"]]
        return at

    def passed_on(tag, after):
        srcs, fwd = passing[tag]
        lands = _forward_wait(f"forward_{tag}_wait", fwd, [after])
        return srcs, [lax.dynamic_update_index_in_dim(g, o, chip, 0) for g, o in zip(lands, srcs)]

    def late(tag):
        return lambda after: dict(zip(ffn, passed_on(tag, after)[1]))

    layer0 = _layer_params(0, dict(
        hn=hn, w_in=_assemble_w_in(w_in_by_chip, own_w_in, place), conv_w=conv_by_chip, late=late("early"),
        before_mix=pass_on("early", early, len(ffn)), before_ffn_out=pass_on("mid", mid, 1)), w)

    def layer1(after):
        (own_w_in1,), (w_in1_by_chip,) = passed_on("mid", after)
        return _layer_params(1, dict(w_in=_assemble_w_in(w_in1_by_chip, own_w_in1, place), conv_w=conv_by_chip,
                                     late=late("later"), before_mix=pass_on("later", later, len(ffn))), w)

    saved, layers, loss_lanes, dh, dhb, d_norm_final = _forward(x[0], loss_target[0], [layer0, layer1],
                                                                 norm_final[None])

    sums, arrived = {}, {}

    def exchange_start(tag, l, names, grads, after=()):
        mine = [grads[n] for n in names]
        shapes = [jax.ShapeDtypeStruct((g.shape[0], g.shape[1] // 2, g.shape[2]), F32) for g in mine]
        return tag, l, names, _split_start(f"exchange_{tag}_start", _exchange_plan, mine, shapes, len(mine), after)

    def add_halves(l, names, mine, theirs):
        for n, g, t in zip(names, mine, theirs):
            sums[l, n] = (_sum_halves_w_in if n == "w_in" else _sum_halves)(g, t, c_arr)

    def exchange_wait(handle, after):
        tag, l, names, started = handle
        add_halves(l, names, *_split_wait(f"exchange_{tag}_wait", _exchange_plan, started, after))

    def scatter_start(tag, l, names, after=()):
        partial = [sums[l, n][1] for n in names]
        shapes = [jax.ShapeDtypeStruct((3,) + p.shape[1:], p.dtype) for p in partial]
        return tag, l, names, _split_start(f"scatter_{tag}_start", _scatter_plan, partial, shapes, 3 * len(names), after)

    def scatter_wait(handle, after):
        tag, l, names, started = handle
        for n, q in zip(names, _split_wait(f"scatter_{tag}_wait", _scatter_plan, started, after)[1]):
            arrived[l, n] = q

    last = DEPTH - 1
    swiglu = BIG[2:]
    dh1, dh1b, g_ffn = _layer_bwd_ffn(dh, dhb, layers[last], saved[last])
    dh, dhb, g_mix = _layer_bwd_mixer(dh1, dh1b, layers[last], saved[last])
    gl = [None, _reference_layout({**g_ffn, **g_mix})]
    ex_last = exchange_start("last", last, BIG, gl[last])
    dh1, dh1b, g_ffn = _layer_bwd_ffn(dh, dhb, layers[0], saved[0], after=[ex_last[-1]["token"]])
    exchange_wait(ex_last, dh1)
    sc_last = scatter_start("last", last, BIG)
    ex_ffn = exchange_start("swiglu", 0, swiglu, g_ffn, [sc_last[-1]["token"]])
    sc_ffn = []

    def midway(do):
        exchange_wait(ex_ffn, do)
        sc_ffn.append(scatter_start("swiglu", 0, swiglu))
        return [sc_ffn[0][-1]["token"]]

    ex_rest = []

    def late(grads):
        rest_grads = dict(w_in=grads["w_in"], w_out=grads["w_out"].reshape(NCHIP, D // NCHIP, D))
        ex_rest.append(exchange_start("rest", 0, BIG[:2], rest_grads))
        return [ex_rest[0][-1]["token"]]

    dx, _, g_mix = _layer_bwd_mixer(dh1, dh1b, layers[0], saved[0], after=[ex_ffn[-1]["token"]], midway=midway,
                                    late=late)
    scatter_wait(sc_last, dx)
    scatter_wait(sc_ffn[0], dx)
    gl[0] = _reference_layout({**g_ffn, **g_mix})

    small_g = [jnp.stack([gl[l][n] for l in range(DEPTH)]) for n in SMALL[:-1]] + [d_norm_final[0]]
    conv_g = jnp.stack([gl[l]["conv_w"] for l in range(DEPTH)])
    summed = small_g + [conv_g, loss_lanes[0, :1]]
    total = _allreduce_small(_pack(summed))
    exchange_wait(ex_rest[0], total)
    sc_rest = scatter_start("rest", 0, BIG[:2])

    travelling = [sc_rest[-1]["token"]]
    reduced, g_out, delta, new_m, new_v = {}, {}, {}, {}, {}

    done = []

    def adamw_large(names, joined):
        for n, g in zip(names, joined):
            res = _adamw(kernel_view(n, w[n]), g, kernel_view(n, m[n]), kernel_view(n, v[n]))
            done.append(res[2])
            g_out[n], delta[n], new_m[n], new_v[n] = (kernel_view(n, a) for a in (g,) + tuple(res))

    for n in BIG:
        for l in (range(DEPTH) if n in swiglu else [last]):
            reduced[n] = _sum_chips(sums[l, n][0], arrived[l, n], place, l, into=reduced.get(n), after=travelling)
    adamw_large(swiglu, _join_halves("join_swiglu", [reduced[n] for n in swiglu]))
    scatter_wait(sc_rest, done + [reduced[n] for n in BIG[:2]])
    for n in BIG[:2]:
        reduced[n] = _sum_chips(sums[0, n][0], arrived[0, n], place, 0, into=reduced[n])
    adamw_large(BIG[:2], _join_halves("join_rest", [reduced[n] for n in BIG[:2]]))
    *small_r, conv_r, loss = _unpack(total, summed)
    g_out.update(zip(SMALL, small_r))
    g_out["conv_w"] = lax.dynamic_slice_in_dim(conv_r, chip * conv_w.shape[2], conv_w.shape[2], axis=2)

    rest = SMALL + ("conv_w",)
    rows_of = lambda a: a.reshape(1, -1) if a.ndim == 1 else a
    results = _adamw_small(*[[rows_of(src[n]) for n in rest] for src in (w, g_out, m, v)])
    for dst, arrs in zip((delta, new_m, new_v), results):
        dst.update({n: a.reshape(w[n].shape) for n, a in zip(rest, arrs)})

    return (loss[0], dx[None], *[g_out[n] for n in ORDER], *[delta[n] for n in ORDER], *[new_m[n] for n in ORDER],
            *[new_v[n] for n in ORDER])
```

```python
import functools

import jax
import jax.numpy as jnp
from jax import lax
from jax.experimental import pallas as pl
from jax.experimental.pallas import tpu as pltpu

F32 = jnp.float32
BF16 = jnp.bfloat16
MESH = pl.DeviceIdType.MESH
ANY = pl.BlockSpec(memory_space=pl.ANY)
HIGHEST = lax.Precision.HIGHEST

T = 2048
D = 1024
DEPTH = 2
NCHIP = 4
HEADS = 4
HD = 128
HW = HEADS * HD
CH = 64
GCH = 128
IN_DIM = 3080
NP = 3200
BA_OFF = 3072
FF_SH = 704
EPS = 1e-6
LANE = 128
VMEM_LIMIT = 56 * 1024 * 1024

ADAM_LR = 0.001
ADAM_B1 = 0.9
ADAM_B2 = 0.999
ADAM_EPS = 1e-08
ADAM_WD = 0.01
ADAM_STEP = 10


def _cparams(sem=None):
    return pltpu.CompilerParams(dimension_semantics=sem, vmem_limit_bytes=VMEM_LIMIT)


_DIMS = {"nn": (((1,), (0,)), ((), ())), "nt": (((1,), (1,)), ((), ())), "tn": (((0,), (0,)), ((), ()))}


def _mm(name, mode, a, bs, *, tm, tn, tk, out_dtypes=(F32,), reduce_g=False, resid=None, extras=(), epilogue=None,
        after=(), fold_g=False, sub_m=1):
    assert sub_m == 1 or (mode != "tn" and tm % (8 * sub_m) == 0), (name, sub_m)
    nb = len(bs)
    ga = a.shape[0]
    gbs = [b.shape[0] for b in bs]
    g_n = max([ga] + gbs)
    if mode == "tn":
        k_n, m_n = a.shape[1:]
    else:
        m_n, k_n = a.shape[1:]
    n_n = bs[0].shape[1] if mode == "nt" else bs[0].shape[2]
    assert m_n % tm == 0 and n_n % tn == 0 and k_n % tk == 0, (name, m_n, n_n, k_n)
    mi, nj, kk = m_n // tm, n_n // tn, k_n // tk
    lead = g_n if fold_g else None
    g_steps = 1 if fold_g else g_n
    grid = (mi, nj, g_steps, kk)
    ids = lambda i, j, g, k: (g, i, j, k)
    n_red = (g_steps if reduce_g else 1) * kk
    red_idx = lambda: (pl.program_id(2) * kk if reduce_g else 0) + pl.program_id(3)
    sem = ("parallel", "parallel", "arbitrary" if reduce_g else "parallel", "arbitrary")

    def pick(gsz, g):
        return g if gsz > 1 else 0

    def a_map(*p):
        g, i, j, k = ids(*p)
        return (pick(ga, g), k, i) if mode == "tn" else (pick(ga, g), i, k)

    def b_map(gsz):
        def f(*p):
            g, i, j, k = ids(*p)
            return (pick(gsz, g), j, k) if mode == "nt" else (pick(gsz, g), k, j)
        return f

    def o_map(gsz):
        def f(*p):
            g, i, j, k = ids(*p)
            return (0 if reduce_g else pick(gsz, g), i, j)
        return f

    a_spec = pl.BlockSpec((lead, tk, tm) if mode == "tn" else (lead, tm, tk), a_map)
    b_specs = [pl.BlockSpec((lead, tn, tk) if mode == "nt" else (lead, tk, tn), b_map(gs)) for gs in gbs]
    x_specs = [pl.BlockSpec((None, tm, tn), o_map(e.shape[0])) for e in extras]
    r_specs = [pl.BlockSpec((None, tm, tn), o_map(resid.shape[0]))] if resid is not None else []
    g_out = 1 if reduce_g else g_n
    out_shape = [jax.ShapeDtypeStruct((g_out, m_n, n_n), dt) for dt in out_dtypes]
    out_specs = [pl.BlockSpec((None, tm, tn), o_map(g_out)) for _ in out_dtypes]
    nx, nr, no = len(extras), len(r_specs), len(out_dtypes)
    n_in = 1 + nb + nx + nr + len(after)
    dims = _DIMS[mode]

    def body(*refs):
        a_ref = refs[0]
        b_refs = refs[1:1 + nb]
        x_refs = refs[1 + nb:1 + nb + nx]
        r_refs = refs[1 + nb + nx:1 + nb + nx + nr]
        o_refs = refs[n_in:n_in + no]
        acc_refs = refs[n_in + no:]
        def dots(rows):
            if fold_g:
                return [sum(lax.dot_general(a_ref[g, rows, :], b_ref[g], dims, preferred_element_type=F32)
                            for g in range(g_n)) for b_ref in b_refs]
            av = a_ref[...] if mode == "tn" else a_ref[rows, :]
            return [lax.dot_general(av, b_ref[...], dims, preferred_element_type=F32) for b_ref in b_refs]

        def finish(accs, rows=slice(None)):
            if r_refs:
                accs[0] = accs[0] + r_refs[0][rows, :]
            outs = epilogue(accs, [x[rows, :] for x in x_refs]) if epilogue is not None else accs
            for o_ref, o in zip(o_refs, outs):
                o_ref[rows, :] = o.astype(o_ref.dtype)

        if n_red == 1:
            slabs = [slice(s * (tm // sub_m), (s + 1) * (tm // sub_m)) for s in range(sub_m)]
            ahead = dots(slabs[0])
            for s, rows in enumerate(slabs):
                now, ahead = ahead, (dots(slabs[s + 1]) if s + 1 < sub_m else None)
                finish(now, rows)
            return
        products = dots(slice(None))
        r = red_idx()
        for p, acc in zip(products, acc_refs):
            @pl.when(r == 0)
            def _():
                acc[...] = p

            @pl.when((r > 0) & (r < n_red - 1))
            def _():
                acc[...] += p

        @pl.when(r == n_red - 1)
        def _():
            finish([acc[...] + p for p, acc in zip(products, acc_refs)])

    return pl.pallas_call(
        body, name=name, grid=grid,
        in_specs=[a_spec] + b_specs + x_specs + r_specs + [ANY] * len(after),
        out_specs=out_specs, out_shape=out_shape,
        scratch_shapes=[pltpu.VMEM((tm, tn), F32) for _ in range(nb if n_red > 1 else 0)],
        compiler_params=_cparams(sem),
    )(a, *bs, *extras, *([resid] if resid is not None else []), *after)


def _sigmoid(x):
    return 1.0 / (1.0 + jnp.exp(-x))


def _silu(x):
    return x * _sigmoid(x)


def _gelu(x):
    return 0.5 * x * (1.0 + jnp.tanh(0.7978845608028654 * (x + 0.044715 * (x * x * x))))


def _rms_fn(h, gain):
    return h * lax.rsqrt(jnp.mean(h * h, axis=-1, keepdims=True) + EPS) * gain


def _shift_impl(x, s):
    n = x.shape[0]
    rolled = pltpu.roll(x, s % n, 0)
    row = lax.broadcasted_iota(jnp.int32, x.shape, 0)
    return jnp.where((row >= s) & (row < n + s), rolled, 0.0)


@functools.partial(jax.custom_vjp, nondiff_argnums=(1,))
def _shift(x, s):
    return _shift_impl(x, s)


def _shift_fwd(x, s):
    return _shift_impl(x, s), None


def _shift_bwd(s, _, g):
    return (_shift_impl(g, -s),)


_shift.defvjp(_shift_fwd, _shift_bwd)


def _prep_fn(x, w, qk_scale, is_v):
    y = x * w[3:4, :]
    for i in range(3):
        y = y + _shift(x, 3 - i) * w[i:i + 1, :]
    y = _silu(y)
    nrm = lax.rsqrt(jnp.sum(y * y, axis=-1, keepdims=True) + EPS) * qk_scale
    return y * jnp.where(is_v, 1.0, nrm)


def _softplus(x):
    return jnp.maximum(x, 0.0) + jnp.log(1.0 + jnp.exp(-jnp.abs(x)))


def _gates_fn(ba, a_log, dt_bias):
    lane = lax.broadcasted_iota(jnp.int32, ba.shape, 1)
    beta = _sigmoid(ba)
    g = -jnp.exp(a_log) * _softplus(ba + dt_bias)
    return jnp.where(lane < HEADS, beta, g)


def _dot16(a, b, dims=_DIMS["nn"]):
    return lax.dot_general(a.astype(BF16), b.astype(BF16), dims, preferred_element_type=F32)


def _dot32(a, b):
    return jnp.dot(a, b, preferred_element_type=F32, precision=HIGHEST)


def _dot3(a, b, dims=_DIMS["nn"]):
    return lax.dot_general(a, b, dims, preferred_element_type=F32, precision=lax.Precision.HIGH)


def _tri_inverses(mats, tick=lambda: None):
    row = lax.broadcasted_iota(jnp.int32, (CH, CH), 0)
    col = lax.broadcasted_iota(jnp.int32, (CH, CH), 1)
    eye = (row == col).astype(F32)
    ts = [eye - a for a in mats]
    ps = list(mats)
    for _ in range(5):
        ps = [_dot3(p, p) for p in ps]
        tick()
        ts = [t + _dot3(t, p) for t, p in zip(ts, ps)]
        tick()
    return ts


@jax.custom_vjp
def _tri_solves(mats, rhs):
    return [_dot3(t, b) for t, b in zip(_tri_inverses(mats), rhs)]


def _tri_solves_fwd(mats, rhs):
    ts = _tri_inverses(mats)
    xs = [_dot3(t, b) for t, b in zip(ts, rhs)]
    return xs, (ts, xs)


def _tri_solves_bwd(res, dxs):
    ts, xs = res
    dbs = [_dot3(t, dx, _DIMS["tn"]) for t, dx in zip(ts, dxs)]
    return [-_dot3(db, x, _DIMS["nt"]) for db, x in zip(dbs, xs)], dbs


_tri_solves.defvjp(_tri_solves_fwd, _tri_solves_bwd)


def _chunk_prep_fn(xs, bgs, tick=None):
    step = tick or (lambda: None)
    row = lax.broadcasted_iota(jnp.int32, (CH, CH), 0)
    col = lax.broadcasted_iota(jnp.int32, (CH, CH), 1)
    incl = row >= col
    strict = row > col
    lmat = incl.astype(F32)
    n = len(xs)
    items = [(i, h) for i in range(n) for h in range(HEADS)]
    part = lambda i, h, c: xs[i][:, c * HW + h * HD:c * HW + (h + 1) * HD]
    q = [part(i, h, 0) for i, h in items]
    k = [part(i, h, 1) for i, h in items]
    v = [part(i, h, 2) for i, h in items]
    beta = [bgs[i][:, h:h + 1] for i, h in items]
    gc_all = [_dot32(lmat, bg) for bg in bgs]
    step()
    gc = [gc_all[i][:, HEADS + h:HEADS + h + 1] for i, h in items]
    gmat = [jnp.where(strict, jnp.broadcast_to(bgs[i][:, HEADS + h:HEADS + h + 1], (CH, CH)), 0.0) for i, h in items]
    diff = [_dot3(lmat, m) for m in gmat]
    step()
    decay = [jnp.where(incl, jnp.exp(jnp.where(incl, d, 0.0)), 0.0) for d in diff]
    k_beta = [kk * b for kk, b in zip(k, beta)]
    kk_t = [_dot16(kb, kk, _DIMS["nt"]) for kb, kk in zip(k_beta, k)]
    step()
    qk_t = [_dot16(qq, kk, _DIMS["nt"]) for qq, kk in zip(q, k)]
    step()
    a = [jnp.where(strict, m * d, 0.0) for m, d in zip(kk_t, decay)]
    eg = [jnp.exp(g) for g in gc]
    rhs = [jnp.concatenate([vv * b, kb * e], axis=-1) for vv, b, kb, e in zip(v, beta, k_beta, eg)]
    if tick is None:
        uw = _tri_solves(a, rhs)
    else:
        uw = [_dot3(t, b) for t, b in zip(_tri_inverses(a, tick), rhs)]
    qk = [m * d for m, d in zip(qk_t, decay)]
    g_last = [g[CH - 1:CH, :] for g in gc]
    qe = [qq * e for qq, e in zip(q, eg)]
    kd = [kk * jnp.exp(gl - g) for kk, gl, g in zip(k, g_last, gc)]
    egl = [jnp.broadcast_to(jnp.exp(gl), (1, HD)) for gl in g_last]
    out = []
    for i in range(n):
        mine = slice(i * HEADS, (i + 1) * HEADS)
        cat = lambda vals: jnp.concatenate(vals[mine], axis=-1)
        out.append((cat([x[:, :HD] for x in uw]), cat([x[:, HD:] for x in uw]), cat(qe), cat(kd),
                    jnp.concatenate([m[None] for m in qk[mine]], axis=0), cat(egl)))
    return out


def _state_levels(chunks, s, outs, befores, final):
    for u, w, qe, kd, qk, egl in chunks:
        befores.append(s)
        ws = [_dot16(a, b) for a, b in zip(w, s)]
        qs = [_dot16(a, b) for a, b in zip(qe, s)]
        yield
        v_new = [a - b for a, b in zip(u, ws)]
        outs.append([a + _dot16(b, c) for a, b, c in zip(qs, qk, v_new)])
        s = [a * e + _dot16(b, c, _DIMS["tn"]) for a, e, b, c in zip(s, egl, kd, v_new)]
        yield
    final.append(s)


def _chunk_state_fn(u, w, qe, kd, qk, egl, s):
    ws = [_dot16(a, b) for a, b in zip(w, s)]
    qs = [_dot16(a, b) for a, b in zip(qe, s)]
    v_new = [a - b for a, b in zip(u, ws)]
    o = [a + _dot16(b, c) for a, b, c in zip(qs, qk, v_new)]
    s_new = [a * e + _dot16(b, c, _DIMS["tn"]) for a, e, b, c in zip(s, egl, kd, v_new)]
    return o, s_new


def _mix_fn(o, z, ur, vr, ong, lng, lnb, ws, bst):
    row = lax.broadcasted_iota(jnp.int32, (GCH, GCH), 0)
    col = lax.broadcasted_iota(jnp.int32, (GCH, GCH), 1)
    causal = row >= col
    ug = _gelu(ur)
    vg = _gelu(vr)
    sls = [slice(h * HD, (h + 1) * HD) for h in range(HEADS)]
    oh = [o[:, sl] for sl in sls]
    oh = [x * lax.rsqrt(jnp.mean(x * x, axis=-1, keepdims=True) + EPS) for x in oh]
    outs_dn = [x * ong * _silu(z[:, sl]) for x, sl in zip(oh, sls)]
    vh = [vg[:, sl] for sl in sls]
    mu = [jnp.mean(x, axis=-1, keepdims=True) for x in vh]
    var = [jnp.mean(jnp.square(x - m), axis=-1, keepdims=True) for x, m in zip(vh, mu)]
    vn = [(x - m) * lax.rsqrt(s + EPS) * lng[:, sl] + lnb[:, sl] for x, m, s, sl in zip(vh, mu, var, sls)]
    mixed = [_dot16(jnp.where(causal, ws[h], 0.0), vn[h]) for h in range(HEADS)]
    outs_gm = [ug[:, sl] * (mixed[h] + bst[:, h:h + 1]) for h, sl in enumerate(sls)]
    return jnp.concatenate(outs_dn + outs_gm, axis=-1)


def _loss_fn(h, gain, tgt):
    y = _rms_fn(h, gain)
    return 0.5 * jnp.sum(jnp.mean(jnp.square(y - tgt), axis=-1))


RT = 512


def _rows(n=D):
    return pl.BlockSpec((RT, n), lambda i: (i, 0))


def _whole(shape):
    nd = len(shape)
    return pl.BlockSpec(shape, lambda i: (0,) * nd)


def _rmsnorm(name, h, gain):
    def body(h_ref, g_ref, o_ref):
        o_ref[...] = _rms_fn(h_ref[...], g_ref[...]).astype(BF16)

    return pl.pallas_call(
        body, name=name, grid=(T // RT,), in_specs=[_rows(), _whole((1, D))], out_specs=_rows(),
        out_shape=jax.ShapeDtypeStruct((T, D), BF16), compiler_params=_cparams(("parallel",)),
    )(h, gain)


def _rmsnorm_bwd(name, dhn, h, gain, resid):
    def body(dhn_ref, h_ref, g_ref, r_ref, dh_ref, dh16_ref, dg_ref):
        _, vjp = jax.vjp(_rms_fn, h_ref[...], g_ref[...])
        dh, dg = vjp(dhn_ref[...])
        dh = r_ref[...] + dh
        dh_ref[...] = dh
        dh16_ref[...] = dh.astype(BF16)

        @pl.when(pl.program_id(0) == 0)
        def _():
            dg_ref[...] = dg

        @pl.when(pl.program_id(0) > 0)
        def _():
            dg_ref[...] += dg

    return pl.pallas_call(
        body, name=name, grid=(T // RT,), in_specs=[_rows(), _rows(), _whole((1, D)), _rows()],
        out_specs=[_rows(), _rows(), _whole((1, D))],
        out_shape=[jax.ShapeDtypeStruct((T, D), F32), jax.ShapeDtypeStruct((T, D), BF16),
                   jax.ShapeDtypeStruct((1, D), F32)],
        compiler_params=_cparams(("arbitrary",)),
    )(dhn, h, gain, resid)


def _loss_head(h, gain, tgt):
    def body(h_ref, g_ref, t_ref, l_ref, dh_ref, dh16_ref, dg_ref):
        loss, vjp = jax.vjp(lambda hh, gg: _loss_fn(hh, gg, t_ref[...]), h_ref[...], g_ref[...])
        dh, dg = vjp(jnp.ones((), F32))
        dh_ref[...] = dh
        dh16_ref[...] = dh.astype(BF16)
        lv = jnp.full((1, LANE), loss, F32)

        @pl.when(pl.program_id(0) == 0)
        def _():
            dg_ref[...] = dg
            l_ref[...] = lv

        @pl.when(pl.program_id(0) > 0)
        def _():
            dg_ref[...] += dg
            l_ref[...] += lv

    return pl.pallas_call(
        body, name="loss_head", grid=(T // RT,), in_specs=[_rows(), _whole((1, D)), _rows()],
        out_specs=[_whole((1, LANE)), _rows(), _rows(), _whole((1, D))],
        out_shape=[jax.ShapeDtypeStruct((1, LANE), F32), jax.ShapeDtypeStruct((T, D), F32),
                   jax.ShapeDtypeStruct((T, D), BF16), jax.ShapeDtypeStruct((1, D), F32)],
        compiler_params=_cparams(("arbitrary",)),
    )(h, gain, tgt)


def _prep_flags():
    j = pl.program_id(0)
    qk_scale = jnp.where(j < HEADS, HD ** -0.5, 1.0).astype(F32)
    return qk_scale, j >= 2 * HEADS


def _prep(proj, conv_w):
    def body(x_ref, w_ref, o_ref):
        qk_scale, is_v = _prep_flags()
        o_ref[...] = _prep_fn(x_ref[...], w_ref[...], qk_scale, is_v)

    col = lambda j: (0, j)
    return pl.pallas_call(
        body, name="gdn_prep", grid=(3 * HEADS,),
        in_specs=[pl.BlockSpec((T, HD), col), pl.BlockSpec((4, HD), col)], out_specs=pl.BlockSpec((T, HD), col),
        out_shape=jax.ShapeDtypeStruct((T, 3 * HW), F32), compiler_params=_cparams(("parallel",)),
    )(proj, conv_w)


def _prep_bwd(proj, conv_w, dqkv, dproj):
    def body(x_ref, w_ref, d_ref, _, dx_ref, dw_ref):
        qk_scale, is_v = _prep_flags()
        _, vjp = jax.vjp(lambda x, w: _prep_fn(x, w, qk_scale, is_v), x_ref[...], w_ref[...])
        dx, dw = vjp(d_ref[...])
        dx_ref[...] = dx.astype(BF16)
        dw_ref[...] = dw

    col = lambda j: (0, j)
    return pl.pallas_call(
        body, name="gdn_prep_bwd", grid=(3 * HEADS,),
        in_specs=[pl.BlockSpec((T, HD), col), pl.BlockSpec((4, HD), col), pl.BlockSpec((T, HD), col), ANY],
        out_specs=[pl.BlockSpec((T, HD), col), pl.BlockSpec((4, HD), col)],
        out_shape=[jax.ShapeDtypeStruct((T, NP), BF16), jax.ShapeDtypeStruct((4, 3 * HW), F32)],
        input_output_aliases={3: 0}, compiler_params=_cparams(("parallel",)),
    )(proj, conv_w, dqkv, dproj)


BA_BLK = BA_OFF // LANE


def _gates(proj, a_log, dt_bias):
    def body(x_ref, a_ref, d_ref, o_ref):
        o_ref[...] = _gates_fn(x_ref[...], a_ref[...], d_ref[...])

    return pl.pallas_call(
        body, name="gdn_gates", grid=(1,),
        in_specs=[pl.BlockSpec((T, LANE), lambda i: (0, BA_BLK)), _whole((1, LANE)), _whole((1, LANE))],
        out_specs=_whole((T, LANE)),
        out_shape=jax.ShapeDtypeStruct((T, LANE), F32), compiler_params=_cparams(("arbitrary",)),
    )(proj, a_log, dt_bias)


def _gates_bwd(proj, a_log, dt_bias, dbg, dproj):
    def body(x_ref, a_ref, d_ref, dbg_ref, _, dx_ref, da_ref, dd_ref):
        _, vjp = jax.vjp(_gates_fn, x_ref[...], a_ref[...], d_ref[...])
        dx, da_ref[...], dd_ref[...] = vjp(dbg_ref[...])
        dx_ref[...] = dx.astype(BF16)

    ba = pl.BlockSpec((T, LANE), lambda i: (0, BA_BLK))
    return pl.pallas_call(
        body, name="gdn_gates_bwd", grid=(1,),
        in_specs=[ba, _whole((1, LANE)), _whole((1, LANE)), _whole((T, LANE)), ANY],
        out_specs=[ba, _whole((1, LANE)), _whole((1, LANE))],
        out_shape=[jax.ShapeDtypeStruct((T, NP), BF16), jax.ShapeDtypeStruct((1, LANE), F32),
                   jax.ShapeDtypeStruct((1, LANE), F32)],
        input_output_aliases={4: 0}, compiler_params=_cparams(("arbitrary",)),
    )(proj, a_log, dt_bias, dbg, dproj)


NCK = T // CH
CPS = 4


def _chunk_group_specs(at):
    wide = pl.BlockSpec((CPS * CH, HW), lambda n: (at(n), 0))
    return [wide, wide, wide, wide, pl.BlockSpec((HEADS, CPS * CH, CH), lambda n: (0, at(n), 0)),
            pl.BlockSpec((CPS, 1, HW), lambda n: (at(n), 0, 0))]


def _chunk_prep_shapes(dtypes):
    shp = [(T, HW), (T, HW), (T, HW), (T, HW), (HEADS, T, CH), (NCK, 1, HW)]
    return [jax.ShapeDtypeStruct(s, dt) for s, dt in zip(shp, dtypes)]


NGROUP = NCK // CPS
PREP_DTYPES = (F32, BF16, BF16, BF16, BF16, F32)


def _delta_rule(qkv, bg):
    def body(x_ref, bg_ref, *refs):
        prep_out, (o_ref, sh_ref), held, s_ref = refs[:6], refs[6:8], refs[8:14], refs[14]
        i = pl.program_id(0)

        @pl.when(i == 0)
        def _():
            for r in held + (s_ref,):
                r[...] = jnp.zeros_like(r)

        rows = [slice(ci * CH, (ci + 1) * CH) for ci in range(CPS)]
        u_h, w_h, qe_h, kd_h, qk_h, egl_h = held
        chunks = [_head_args((u_h.at[r, :], w_h.at[r, :], qe_h.at[r, :], kd_h.at[r, :], qk_h.at[:, r, :], egl_h.at[ci]))
                  for ci, r in enumerate(rows)]
        start = [jnp.where(i <= 1, 0.0, s_ref[h]) for h in range(HEADS)]
        outs, befores, final = [], [], []
        levels = _state_levels(chunks, start, outs, befores, final)
        res = _chunk_prep_fn([x_ref[r, :] for r in rows], [bg_ref[r, :] for r in rows], tick=lambda: next(levels, None))
        for _ in levels:
            pass
        for ci, (u, w, qe, kd, qk, egl) in enumerate(res):
            for refs_pair, val in zip(zip(prep_out[:4], held[:4]), (u, w, qe, kd)):
                for ref in refs_pair:
                    ref[rows[ci], :] = val.astype(ref.dtype)
            for ref in (prep_out[4], qk_h):
                ref[:, rows[ci], :] = qk.astype(ref.dtype)
            for ref in (prep_out[5], egl_h):
                ref[ci] = egl
        for ci, r in enumerate(rows):
            for h in range(HEADS):
                o_ref[r, h * HD:(h + 1) * HD] = outs[ci][h]
                sh_ref[h, ci] = befores[ci][h]
        for h in range(HEADS):
            s_ref[h] = final[0][h]

    now = lambda n: jnp.minimum(n, NGROUP - 1)
    was = lambda n: jnp.maximum(n - 1, 0)
    wide = lambda at: pl.BlockSpec((CPS * CH, HW), lambda n: (at(n), 0))
    held = [pltpu.VMEM(s, dt) for s, dt in zip(
        [(CPS * CH, HW)] * 4 + [(HEADS, CPS * CH, CH), (CPS, 1, HW)], PREP_DTYPES)]
    out = pl.pallas_call(
        body, name="gdn_delta_rule", grid=(NGROUP + 1,),
        in_specs=[pl.BlockSpec((CPS * CH, 3 * HW), lambda n: (now(n), 0)),
                  pl.BlockSpec((CPS * CH, LANE), lambda n: (now(n), 0))],
        out_specs=[wide(now)] * 4 + [pl.BlockSpec((HEADS, CPS * CH, CH), lambda n: (0, now(n), 0)),
                                     pl.BlockSpec((CPS, 1, HW), lambda n: (now(n), 0, 0)), wide(was),
                                     pl.BlockSpec((HEADS, CPS, HD, HD), lambda n: (0, was(n), 0, 0))],
        out_shape=_chunk_prep_shapes(PREP_DTYPES) + [jax.ShapeDtypeStruct((T, HW), F32),
                                                     jax.ShapeDtypeStruct((HEADS, NCK, HD, HD), F32)],
        scratch_shapes=held + [pltpu.VMEM((HEADS, HD, HD), F32)], compiler_params=_cparams(("arbitrary",)),
    )(qkv, bg)
    return out[:6], out[6], out[7]


def _chunk_prep_bwd(qkv, bg, cots):
    def body(x_ref, bg_ref, du, dw, dqe, dkd, dqk, degl, dx_ref, dbg_ref):
        rows = [slice(ci * CH, (ci + 1) * CH) for ci in range(CPS)]
        _, vjp = jax.vjp(_chunk_prep_fn, [x_ref[r, :] for r in rows], [bg_ref[r, :] for r in rows])
        dxs, dbgs = vjp([(du[r, :], dw[r, :], dqe[r, :], dkd[r, :], dqk[:, r, :], degl[ci])
                         for ci, r in enumerate(rows)])
        for r, dx, dbg in zip(rows, dxs, dbgs):
            dx_ref[r, :] = dx
            dbg_ref[r, :] = dbg

    wide = pl.BlockSpec((CPS * CH, HW), lambda n: (n, 0))
    return pl.pallas_call(
        body, name="gdn_chunk_prep_bwd", grid=(NCK // CPS,),
        in_specs=[pl.BlockSpec((CPS * CH, 3 * HW), lambda n: (n, 0)), pl.BlockSpec((CPS * CH, LANE), lambda n: (n, 0)),
                  wide, wide, wide, wide, pl.BlockSpec((HEADS, CPS * CH, CH), lambda n: (0, n, 0)),
                  pl.BlockSpec((CPS, 1, HW), lambda n: (n, 0, 0))],
        out_specs=[pl.BlockSpec((CPS * CH, 3 * HW), lambda n: (n, 0)), pl.BlockSpec((CPS * CH, LANE), lambda n: (n, 0))],
        out_shape=[jax.ShapeDtypeStruct((T, 3 * HW), F32), jax.ShapeDtypeStruct((T, LANE), F32)],
        compiler_params=_cparams(("parallel",)),
    )(qkv, bg, *cots)


def _head_args(refs):
    u, w, qe, kd, qk, egl = refs
    sls = [slice(h * HD, (h + 1) * HD) for h in range(HEADS)]
    return ([u[:, sl] for sl in sls], [w[:, sl].astype(F32) for sl in sls], [qe[:, sl].astype(F32) for sl in sls],
            [kd[:, sl].astype(F32) for sl in sls], [qk[h].astype(F32) for h in range(HEADS)],
            [egl[:, sl] for sl in sls])


def _chunk_scan_bwd(prep, s_hist, do, after=()):
    n_in = 8 + len(after)

    def body(*refs):
        u_r, w_r, qe_r, kd_r, qk_r, egl_r, sh_ref, do_ref = refs[:8]
        d_refs = refs[n_in:n_in + 6]
        ds_ref = refs[n_in + 6]

        @pl.when(pl.program_id(0) == 0)
        def _():
            ds_ref[...] = jnp.zeros_like(ds_ref)

        sls = [slice(h * HD, (h + 1) * HD) for h in range(HEADS)]
        ds = [ds_ref[h] for h in range(HEADS)]
        for ci in reversed(range(CPS)):
            r = slice(ci * CH, (ci + 1) * CH)
            args = _head_args((u_r.at[r, :], w_r.at[r, :], qe_r.at[r, :], kd_r.at[r, :], qk_r.at[:, r, :], egl_r.at[ci]))
            _, vjp = jax.vjp(_chunk_state_fn, *args, [sh_ref[h, ci] for h in range(HEADS)])
            du, dw, dqe, dkd, dqk, degl, ds = vjp(([do_ref[r, sl] for sl in sls], ds))
            for h, sl in enumerate(sls):
                for d_ref, val in zip(d_refs[:4], (du, dw, dqe, dkd)):
                    d_ref[r, sl] = val[h]
                d_refs[4][h, r, :] = dqk[h]
                d_refs[5][ci, :, sl] = degl[h]
        for h in range(HEADS):
            ds_ref[h] = ds[h]

    rev = lambda n: NGROUP - 1 - n
    return pl.pallas_call(
        body, name="gdn_scan_bwd", grid=(NGROUP,),
        in_specs=_chunk_group_specs(rev) + [pl.BlockSpec((HEADS, CPS, HD, HD), lambda n: (0, rev(n), 0, 0)),
                                            pl.BlockSpec((CPS * CH, HW), lambda n: (rev(n), 0))] + [ANY] * len(after),
        out_specs=_chunk_group_specs(rev), out_shape=_chunk_prep_shapes((F32,) * 6),
        scratch_shapes=[pltpu.VMEM((HEADS, HD, HD), F32)], compiler_params=_cparams(("arbitrary",)),
    )(*prep, s_hist, do, *after)


MIX_ROWS = 2 * GCH


def _mix_rows(o, z, ur, vr, *params):
    chunks = [slice(c * GCH, (c + 1) * GCH) for c in range(MIX_ROWS // GCH)]
    return jnp.concatenate([_mix_fn(o[r], z[r], ur[r], vr[r], *params) for r in chunks], axis=0)


def _mix_specs():
    pc = lambda c: pl.BlockSpec((MIX_ROWS, HW), lambda i: (i, c))
    return [pl.BlockSpec((MIX_ROWS, HW), lambda i: (i, 0)), pc(3), pc(4), pc(5), _whole((1, HD)), _whole((1, HW)),
            _whole((1, HW)), _whole((HEADS, GCH, GCH)), _whole((GCH, LANE))]


def _mix(o, proj, ong, lng, lnb, ws, bst, after=()):
    def body(o_ref, z_ref, u_ref, v_ref, ong_ref, lng_ref, lnb_ref, ws_ref, bs_ref, *rest):
        rest[-1][...] = _mix_rows(o_ref[...], z_ref[...], u_ref[...], v_ref[...], ong_ref[...], lng_ref[...],
                                  lnb_ref[...], ws_ref[...], bs_ref[...]).astype(BF16)

    return pl.pallas_call(
        body, name="mix", grid=(T // MIX_ROWS,), in_specs=_mix_specs() + [ANY] * len(after),
        out_specs=pl.BlockSpec((MIX_ROWS, D), lambda i: (i, 0)), out_shape=jax.ShapeDtypeStruct((T, D), BF16),
        compiler_params=_cparams(("parallel",)),
    )(o, proj, proj, proj, ong, lng, lnb, ws, bst, *after)


def _mix_bwd(o, proj, ong, lng, lnb, ws, bst, dmix):
    def body(o_ref, z_ref, u_ref, v_ref, ong_ref, lng_ref, lnb_ref, ws_ref, bs_ref, dm_ref,
             do_ref, dzuv_ref, dong_ref, dlng_ref, dlnb_ref, dws_ref, dbs_ref):
        _, vjp = jax.vjp(_mix_rows, o_ref[...], z_ref[...], u_ref[...], v_ref[...], ong_ref[...], lng_ref[...],
                         lnb_ref[...], ws_ref[...], bs_ref[...])
        do, dz, du, dv, dong, dlng, dlnb, dws, dbs = vjp(dm_ref[...])
        do_ref[...] = do
        dzuv_ref[:, 0:HW] = dz.astype(BF16)
        dzuv_ref[:, HW:2 * HW] = du.astype(BF16)
        dzuv_ref[:, 2 * HW:3 * HW] = dv.astype(BF16)
        acc = [(dong_ref, dong), (dlng_ref, dlng), (dlnb_ref, dlnb), (dws_ref, dws), (dbs_ref, dbs)]

        @pl.when(pl.program_id(0) == 0)
        def _():
            for r, val in acc:
                r[...] = val

        @pl.when(pl.program_id(0) > 0)
        def _():
            for r, val in acc:
                r[...] += val

    shp = lambda *s: jax.ShapeDtypeStruct(s, F32)
    return pl.pallas_call(
        body, name="mix_bwd", grid=(T // MIX_ROWS,),
        in_specs=_mix_specs() + [pl.BlockSpec((MIX_ROWS, D), lambda i: (i, 0))],
        out_specs=[pl.BlockSpec((MIX_ROWS, HW), lambda i: (i, 0)), pl.BlockSpec((MIX_ROWS, 3 * HW), lambda i: (i, 1)),
                   _whole((1, HD)), _whole((1, HW)), _whole((1, HW)), _whole((HEADS, GCH, GCH)), _whole((GCH, LANE))],
        out_shape=[shp(T, HW), jax.ShapeDtypeStruct((T, NP), BF16), shp(1, HD), shp(1, HW), shp(1, HW),
                   shp(HEADS, GCH, GCH), shp(GCH, LANE)],
        compiler_params=_cparams(("arbitrary",)),
    )(o, proj, proj, proj, ong, lng, lnb, ws, bst, dmix)


def _swiglu_epilogue(accs, _):
    gate, up = accs
    return [gate, up, _silu(gate) * up]


def _swiglu_bwd_epilogue(accs, extras):
    dact = accs[0]
    gate, up = (e.astype(F32) for e in extras)
    sg = _sigmoid(gate)
    return [dact * up * (sg * (1.0 + gate * (1.0 - sg))), dact * (gate * sg)]


def _layer_fwd(h, p):
    hn = p.pop("hn") if "hn" in p else _rmsnorm("rms_mix", h, p["norm_mix"])
    proj = _mm("in_proj", "nn", hn[None], [p["w_in"][None]], tm=1024, tn=640, tk=D, sub_m=2)[0][0]
    qkv = _prep(proj, p["conv_w"])
    bg = _gates(proj, p["a_log"], p["dt_bias"])
    prep, o, s_hist = _delta_rule(qkv, bg)
    mix = _mix(o, proj, p["o_norm_g"], p["ln_v_g"], p["ln_v_b"], p["w_s"], p["bst"],
               p.pop("before_mix")(o) if "before_mix" in p else ())
    if "late" in p:
        p.update(p.pop("late")(mix))
    h1 = _mm("out_proj", "nn", mix[None], [p["w_out"].reshape(1, D, D)], tm=T, tn=512, tk=D, resid=h[None],
             sub_m=4)[0][0]
    h2n = _rmsnorm("rms_ffn", h1, p["norm_ffn"])
    gate, up, act = _mm("ffn_in", "nt", h2n[None], [p["w_gate"], p["w_up"]], tm=1024, tn=FF_SH, tk=D,
                        out_dtypes=(BF16, BF16, BF16), epilogue=_swiglu_epilogue, sub_m=4)
    then = p.pop("before_ffn_out")(act) if "before_ffn_out" in p else ()
    h2 = _mm("ffn_out", "nn", act, [p["w_down"]], tm=1024, tn=512, tk=FF_SH, reduce_g=True, fold_g=True,
             resid=h1[None], sub_m=2, after=then)[0][0]
    saved = dict(h=h, hn=hn, proj=proj, qkv=qkv, bg=bg, prep=prep, o=o, s_hist=s_hist, mix=mix, h1=h1, h2n=h2n,
                 gate=gate, up=up, act=act)
    return h2, saved


def _layer_bwd_ffn(dh2, dh2b, p, s, after=()):
    dh2b = dh2b[None]
    dgate, dup = _mm("ffn_out_bwd", "nt", dh2b, [p["w_down"]], tm=1024, tn=FF_SH, tk=D, out_dtypes=(BF16, BF16),
                     extras=(s["gate"], s["up"]), epilogue=_swiglu_bwd_epilogue, after=after, sub_m=4)
    dh2n = _mm("ffn_gate_bwd", "nn", dgate, [p["w_gate"]], tm=1024, tn=512, tk=FF_SH, reduce_g=True, fold_g=True,
               sub_m=2)[0]
    dh2n = _mm("ffn_up_bwd", "nn", dup, [p["w_up"]], tm=1024, tn=512, tk=FF_SH, reduce_g=True, fold_g=True,
               resid=dh2n, sub_m=2)[0][0]
    dh1, dh1b, d_norm_ffn = _rmsnorm_bwd("rms_ffn_bwd", dh2n, s["h1"], p["norm_ffn"], dh2)
    d_w_down = _mm("ffn_wdown_grad", "tn", s["act"], [dh2b], tm=FF_SH, tn=512, tk=T)[0]
    d_w_gate = _mm("ffn_wgate_grad", "tn", dgate, [s["h2n"][None]], tm=FF_SH, tn=512, tk=T)[0]
    d_w_up = _mm("ffn_wup_grad", "tn", dup, [s["h2n"][None]], tm=FF_SH, tn=512, tk=T)[0]
    return dh1, dh1b, dict(norm_ffn=d_norm_ffn, w_gate=d_w_gate, w_up=d_w_up, w_down=d_w_down)


def _layer_bwd_mixer(dh1, dh1b, p, s, after=(), midway=None, late=None):
    dh1b = dh1b[None]
    dmix = _mm("out_proj_bwd", "nt", dh1b, [p["w_out"].reshape(1, D, D)], tm=T, tn=512, tk=D, after=after,
               sub_m=4)[0][0]
    d_w_out = _mm("out_proj_wgrad", "tn", s["mix"][None], [dh1b], tm=1024, tn=512, tk=T)[0][0]
    do, dproj, d_ong, d_lng, d_lnb, d_ws, d_bst = _mix_bwd(
        s["o"], s["proj"], p["o_norm_g"], p["ln_v_g"], p["ln_v_b"], p["w_s"], p["bst"], dmix)
    then = midway(do) if midway is not None else ()
    dqkv, dbg = _chunk_prep_bwd(s["qkv"], s["bg"], _chunk_scan_bwd(s["prep"], s["s_hist"], do, then))
    dproj, d_conv = _prep_bwd(s["proj"], p["conv_w"], dqkv, dproj)
    dproj, d_a_log, d_dt_bias = _gates_bwd(s["proj"], p["a_log"], p["dt_bias"], dbg, dproj)
    dproj = dproj[None]
    d_w_in = _mm("in_proj_wgrad", "tn", s["hn"][None], [dproj], tm=512, tn=640, tk=T)[0]
    last = late(dict(w_in=d_w_in, w_out=d_w_out)) if late is not None else ()
    dhn = _mm("in_proj_bwd", "nt", dproj, [p["w_in"][None]], tm=1024, tn=512, tk=NP, after=last,
              sub_m=2)[0][0]
    dh, dhb, d_norm_mix = _rmsnorm_bwd("rms_mix_bwd", dhn, s["h"], p["norm_mix"], dh1)
    grads = dict(norm_mix=d_norm_mix, w_in=d_w_in, conv_w=d_conv, a_log=d_a_log, dt_bias=d_dt_bias, o_norm_g=d_ong,
                 ln_v_g=d_lng, ln_v_b=d_lnb, w_s=d_ws, bst=d_bst, w_out=d_w_out)
    return dh, dhb, grads


def _lanes(v, off=0):
    return jnp.zeros((1, LANE), F32).at[0, off:off + v.shape[0]].set(v)


def _w_in_pieces():
    regions = [(0, 2048, 0), (2048, 2056, BA_OFF), (2056, IN_DIM, 2048)]
    sh = IN_DIM // NCHIP
    out = []
    for j in range(NCHIP):
        for lo, hi, at in regions:
            a, b = max(lo, j * sh), min(hi, (j + 1) * sh)
            if a < b:
                out.append((j, a - j * sh, at + a - lo, b - a))
    return out


W_IN_PIECES = _w_in_pieces()
WT = 256


def _assemble_w_in(gathered, own, place):
    def body(place_ref, g_ref, own_ref, o_ref):
        o_ref[:, IN_DIM:] = jnp.zeros((WT, NP - IN_DIM), BF16)
        mine = own_ref[...]
        for j, src, dst, width in W_IN_PIECES:
            val = jnp.where(place_ref[0] == j, mine[:, src:src + width], g_ref[j, :, src:src + width])
            o_ref[:, dst:dst + width] = val

    sh = IN_DIM // NCHIP
    return pl.pallas_call(
        body, name="assemble_w_in",
        grid_spec=pltpu.PrefetchScalarGridSpec(
            num_scalar_prefetch=1, grid=(D // WT,),
            in_specs=[pl.BlockSpec((NCHIP, WT, sh), lambda i, place_ref: (0, i, 0)),
                      pl.BlockSpec((WT, sh), lambda i, place_ref: (i, 0))],
            out_specs=pl.BlockSpec((WT, NP), lambda i, place_ref: (i, 0))),
        out_shape=jax.ShapeDtypeStruct((D, NP), BF16), compiler_params=_cparams(("parallel",)),
    )(place, gathered, own)


def _layer_params(l, big, small):
    return dict(
        {k: v for k, v in big.items() if k != "conv_w"},
        conv_w=jnp.concatenate([big["conv_w"][j, l] for j in range(NCHIP)], axis=1),
        norm_mix=small["norm_mix"][l][None], norm_ffn=small["norm_ffn"][l][None],
        a_log=_lanes(small["a_log"][l], HEADS), dt_bias=_lanes(small["dt_bias"][l], HEADS),
        o_norm_g=small["o_norm_g"][l][None], ln_v_g=small["ln_v_g"][l][None], ln_v_b=small["ln_v_b"][l][None],
        w_s=small["w_s"][l],
        bst=jnp.pad(small["b_s"][l].T, ((0, 0), (0, LANE - HEADS))),
    )


def _reference_layout(g):
    return dict(
        w_in=g["w_in"],
        w_out=g["w_out"].reshape(NCHIP, D // NCHIP, D),
        w_gate=g["w_gate"], w_up=g["w_up"], w_down=g["w_down"],
        conv_w=g["conv_w"], norm_mix=g["norm_mix"][0], norm_ffn=g["norm_ffn"][0],
        a_log=g["a_log"][0, HEADS:2 * HEADS], dt_bias=g["dt_bias"][0, HEADS:2 * HEADS],
        o_norm_g=g["o_norm_g"][0], ln_v_g=g["ln_v_g"][0], ln_v_b=g["ln_v_b"][0], w_s=g["w_s"],
        b_s=g["bst"][:, :HEADS].T,
    )


def _forward(x, tgt, layers, norm_final):
    h = x
    saved, params = [], []
    for p in layers:
        p = p(h) if callable(p) else p
        h, s = _layer_fwd(h, p)
        saved.append(s)
        params.append(p)
    return (saved, params) + tuple(_loss_head(h, norm_final, tgt))


def _local_step(x, tgt, layers, norm_final):
    saved, layers, loss, dh, dhb, d_norm_final = _forward(x, tgt, layers, norm_final)
    grads = [None] * DEPTH
    for l in reversed(range(DEPTH)):
        dh1, dh1b, g_ffn = _layer_bwd_ffn(dh, dhb, layers[l], saved[l])
        dh, dhb, g_mix = _layer_bwd_mixer(dh1, dh1b, layers[l], saved[l])
        grads[l] = {**g_ffn, **g_mix}
    return loss, dh, grads, d_norm_final


def _place():
    x, y, c = lax.axis_index("x"), lax.axis_index("y"), lax.axis_index("c")
    return x, y, c, [(1 - x, y), (x, 1 - y), (1 - x, 1 - y)]


def _remote(src, dst, send_sem, recv_sem, to):
    return pltpu.make_async_remote_copy(src_ref=src, dst_ref=dst, send_sem=send_sem, recv_sem=recv_sem,
                                        device_id=to, device_id_type=MESH)


def _comm_call(name, body, ins, out_shape, n_sems, aliases=None):
    return pl.pallas_call(
        body, name=name, in_specs=[ANY] * len(ins), out_specs=[ANY] * len(out_shape), out_shape=out_shape,
        scratch_shapes=[pltpu.SemaphoreType.DMA((n,)) for n in n_sems], input_output_aliases=aliases or {},
        compiler_params=pltpu.CompilerParams(has_side_effects=True),
    )(*ins)


def _half_rows(ref, of_c, dim):
    hr = ref.shape[dim] // 2
    return pl.ds(pl.multiple_of(of_c * hr, BF16_ROWS), hr)


def _gather_plan(whole):
    def plan(srcs, lands):
        x, y, c, others = _place()
        chip = 2 * x + y
        out = []
        for src, land, all_of_it in zip(srcs, lands, whole):
            for ox, oy in others:
                if all_of_it:
                    out.append((src, land.at[chip], (ox, oy, c)))
                else:
                    out.append((src.at[_half_rows(src, c, 0)], land.at[chip, _half_rows(src, c, 0)], (ox, oy, c)))
        return out
    return plan


def _forward_halves(lands):
    n = len(lands)

    def body(*refs):
        outs = refs[n:2 * n]
        send_s, recv_s = refs[2 * n:]
        x, y, c, others = _place()
        sibling = (x, y, 1 - c)
        copies = []
        for a in range(n):
            for k, (ox, oy) in enumerate(others):
                mine = outs[a].at[2 * ox + oy, _half_rows(outs[a], c, 1)]
                copies.append(_remote(mine, mine, send_s.at[3 * a + k], recv_s.at[3 * a + k], sibling))
        for cp in copies:
            cp.start()
        for a in range(n):
            for k, (ox, oy) in enumerate(others):
                landed = outs[a].at[2 * ox + oy, _half_rows(outs[a], 1 - c, 1)]
                _remote(landed, landed, send_s.at[3 * a + k], recv_s.at[3 * a + k], sibling).wait_recv()
        for cp in copies:
            cp.wait_send()

    out_shape = [jax.ShapeDtypeStruct(g.shape, g.dtype) for g in lands]
    return _comm_call("forward_halves", body, lands, out_shape, [3 * n, 3 * n], aliases={a: a for a in range(n)})


def _forward_refs(bufs, incoming):
    x, y, c, others = _place()
    return (x, y, 1 - c), [b.at[2 * ox + oy, _half_rows(b, 1 - c if incoming else c, 1)]
                           for b in bufs for ox, oy in others]


def _forward_start(name, bufs, after):
    n = len(bufs)
    bufs = [pltpu.with_memory_space_constraint(b, pltpu.HBM) for b in bufs]

    def body(*refs):
        send_s, recv_s = refs[n + len(after)], refs[n + len(after) + 1]
        sibling, mine = _forward_refs(refs[:n], incoming=False)
        for i, ref in enumerate(mine):
            _remote(ref, ref, send_s.at[i], recv_s.at[i], sibling).start()
        refs[-1][...] = jnp.zeros_like(refs[-1])

    out = pl.pallas_call(
        body, name=name, in_specs=[HBM_SPEC] * n + [ANY] * len(after),
        out_specs=[SEM_SPEC, SEM_SPEC] + [HBM_SPEC] * n + [pl.BlockSpec(memory_space=pltpu.VMEM)],
        out_shape=[pltpu.SemaphoreType.DMA((3 * n,)), pltpu.SemaphoreType.DMA((3 * n,))]
        + [pltpu.HBM(b.shape, b.dtype) for b in bufs] + [jax.ShapeDtypeStruct((F32_ROWS, LANE), F32)],
        input_output_aliases={i: 2 + i for i in range(n)},
        compiler_params=pltpu.CompilerParams(has_side_effects=DATAFLOW),
    )(*bufs, *after)
    return dict(sems=out[:2], bufs=out[2:2 + n], token=out[-1])


def _forward_wait(name, started, after):
    n = len(started["bufs"])

    def body(*refs):
        send_s, recv_s = refs[n], refs[n + 1]
        sibling, mine = _forward_refs(refs[:n], incoming=False)
        _, theirs = _forward_refs(refs[:n], incoming=True)
        for i, (sent, landed) in enumerate(zip(mine, theirs)):
            _remote(sent, sent, send_s.at[i], recv_s.at[i], sibling).wait_send()
            _remote(landed, landed, send_s.at[i], recv_s.at[i], sibling).wait_recv()

    return pl.pallas_call(
        body, name=name, in_specs=[HBM_SPEC] * n + [SEM_SPEC, SEM_SPEC] + [ANY] * len(after),
        out_specs=[HBM_SPEC] * n, out_shape=[pltpu.HBM(b.shape, b.dtype) for b in started["bufs"]],
        input_output_aliases={i: i for i in range(n)},
        compiler_params=pltpu.CompilerParams(has_side_effects=DATAFLOW),
    )(*started["bufs"], *started["sems"], *after)


HBM_SPEC = pl.BlockSpec(memory_space=pltpu.HBM)
SEM_SPEC = pl.BlockSpec(memory_space=pltpu.SEMAPHORE)
DATAFLOW = pltpu.SideEffectType.DATAFLOW_SIDE_EFFECTING


def _exchange_plan(srcs, lands):
    x, y, c, _ = _place()
    plan = []
    for src, land in zip(srcs, lands):
        hr = src.shape[1] // 2
        plan.append((src.at[:, pl.ds(pl.multiple_of((1 - c) * hr, 8), hr)], land, (x, y, 1 - c)))
    return plan


def _scatter_plan(srcs, lands):
    x, y, c, others = _place()
    return [(src.at[2 * ox + oy], land.at[k], (ox, oy, c))
            for src, land in zip(srcs, lands) for k, (ox, oy) in enumerate(others)]


def _split_start(name, plan, srcs, land_shapes, n_copies, after=()):
    n = len(srcs)
    lands = [pltpu.with_memory_space_constraint(lax.empty(s.shape, s.dtype), pltpu.HBM) for s in land_shapes]
    srcs = [pltpu.with_memory_space_constraint(s, pltpu.HBM) for s in srcs]

    def body(*refs):
        send_s, recv_s = refs[2 * n + len(after)], refs[2 * n + len(after) + 1]
        for i, (src, dst, to) in enumerate(plan(refs[:n], refs[n:2 * n])):
            _remote(src, dst, send_s.at[i], recv_s.at[i], to).start()
        refs[-1][...] = jnp.zeros_like(refs[-1])

    thru = [pltpu.HBM(s.shape, s.dtype) for s in srcs + lands]
    out = pl.pallas_call(
        body, name=name, in_specs=[HBM_SPEC] * (2 * n) + [ANY] * len(after),
        out_specs=[SEM_SPEC, SEM_SPEC] + [HBM_SPEC] * (2 * n) + [pl.BlockSpec(memory_space=pltpu.VMEM)],
        out_shape=[pltpu.SemaphoreType.DMA((n_copies,)), pltpu.SemaphoreType.DMA((n_copies,))] + thru
        + [jax.ShapeDtypeStruct((F32_ROWS, LANE), F32)],
        input_output_aliases={i: 2 + i for i in range(2 * n)},
        compiler_params=pltpu.CompilerParams(has_side_effects=DATAFLOW),
    )(*srcs, *lands, *after)
    return dict(sems=out[:2], srcs=out[2:2 + n], lands=out[2 + n:2 + 2 * n], token=out[-1])


def _split_wait(name, plan, started, after):
    n = len(started["srcs"])
    after = list(after) if isinstance(after, (list, tuple)) else [after]

    def body(*refs):
        send_s, recv_s = refs[2 * n], refs[2 * n + 1]
        for i, (src, dst, to) in enumerate(plan(refs[:n], refs[n:2 * n])):
            cp = _remote(src, dst, send_s.at[i], recv_s.at[i], to)
            cp.wait_send()
            cp.wait_recv()

    arrs = list(started["srcs"]) + list(started["lands"])
    out = pl.pallas_call(
        body, name=name, in_specs=[HBM_SPEC] * (2 * n) + [SEM_SPEC, SEM_SPEC] + [ANY] * len(after),
        out_specs=[HBM_SPEC] * (2 * n), out_shape=[pltpu.HBM(s.shape, s.dtype) for s in arrs],
        input_output_aliases={i: i for i in range(2 * n)},
        compiler_params=pltpu.CompilerParams(has_side_effects=DATAFLOW),
    )(*arrs, *started["sems"], *after)
    return out[:n], out[n:]


def _join_halves(name, rs):
    n = len(rs)

    def body(*refs):
        outs = refs[n:2 * n]
        send_s, recv_s = refs[2 * n:]
        x, y, c, _ = _place()
        sibling = (x, y, 1 - c)

        def half(a, of_c):
            hr = outs[a].shape[1] // 2
            return outs[a].at[:, pl.ds(pl.multiple_of(of_c * hr, 8), hr)]

        copies = [_remote(half(a, c), half(a, c), send_s.at[a], recv_s.at[a], sibling) for a in range(n)]
        for cp in copies:
            cp.start()
        for a in range(n):
            landed = half(a, 1 - c)
            _remote(landed, landed, send_s.at[a], recv_s.at[a], sibling).wait_recv()
        for cp in copies:
            cp.wait_send()

    out_shape = [jax.ShapeDtypeStruct(r.shape, r.dtype) for r in rs]
    return _comm_call(name, body, rs, out_shape, [n, n], aliases={a: a for a in range(n)})


def _allreduce_small(buf, after=()):
    r = buf.shape[0]
    hr = r // 2

    def body(in_ref, *refs):
        out_ref, theirs, by_chip, send_s, recv_s = refs[len(after):]
        x, y, c, others = _place()
        chip = 2 * x + y
        sibling = (x, y, 1 - c)
        mine = pl.ds(pl.multiple_of(c * hr, F32_ROWS), hr)
        swap = _remote(in_ref, theirs, send_s.at[0], recv_s.at[0], sibling)
        swap.start()
        swap.wait()
        by_chip[chip] = in_ref[mine, :] + theirs[mine, :]
        sends = [_remote(by_chip.at[chip], by_chip.at[chip], send_s.at[1 + k], recv_s.at[1 + k], (ox, oy, c))
                 for k, (ox, oy) in enumerate(others)]
        for cp in sends:
            cp.start()
        for k, (ox, oy) in enumerate(others):
            landed = by_chip.at[2 * ox + oy]
            _remote(landed, landed, send_s.at[1 + k], recv_s.at[1 + k], (ox, oy, c)).wait_recv()
        for cp in sends:
            cp.wait_send()
        out_ref[mine, :] = (by_chip[0] + by_chip[1]) + (by_chip[2] + by_chip[3])
        back = _remote(out_ref.at[mine], out_ref.at[mine], send_s.at[NCHIP], recv_s.at[NCHIP], sibling)
        back.start()
        other = out_ref.at[pl.ds(pl.multiple_of((1 - c) * hr, F32_ROWS), hr)]
        _remote(other, other, send_s.at[NCHIP], recv_s.at[NCHIP], sibling).wait_recv()
        back.wait_send()

    vm = pl.BlockSpec(memory_space=pltpu.VMEM)
    return pl.pallas_call(
        body, name="allreduce_small", in_specs=[vm] + [ANY] * len(after), out_specs=vm,
        out_shape=jax.ShapeDtypeStruct((r, LANE), F32),
        scratch_shapes=[pltpu.VMEM((r, LANE), F32), pltpu.VMEM((NCHIP, hr, LANE), F32),
                        pltpu.SemaphoreType.DMA((NCHIP + 1,)), pltpu.SemaphoreType.DMA((NCHIP + 1,))],
        compiler_params=pltpu.CompilerParams(has_side_effects=True, vmem_limit_bytes=VMEM_LIMIT),
    )(buf, *after)


MAX_ROW_TILE = 512
BF16_ROWS = 16


def _row_tile(rows):
    for t in range(min(rows, MAX_ROW_TILE) // BF16_ROWS * BF16_ROWS, 0, -BF16_ROWS):
        if rows % t == 0:
            return t
    raise ValueError(rows)


def _sum_halves(g, theirs, c_arr):
    nch, rows, cols = g.shape
    hr = rows // 2
    tr = _row_tile(hr)

    def body(c_ref, g_ref, t_ref, o_ref, ob_ref):
        s = g_ref[...] + t_ref[...]
        o_ref[...] = s
        ob_ref[...] = s.astype(BF16)

    blk = pl.BlockSpec((None, tr, cols), lambda j, i, c_ref: (j, i, 0))
    return pl.pallas_call(
        body, name="sum_halves",
        grid_spec=pltpu.PrefetchScalarGridSpec(
            num_scalar_prefetch=1, grid=(nch, hr // tr),
            in_specs=[pl.BlockSpec((None, None, tr, cols), lambda j, i, c_ref: (j, c_ref[0], i, 0)), blk],
            out_specs=[blk, blk]),
        out_shape=[jax.ShapeDtypeStruct((nch, hr, cols), F32), jax.ShapeDtypeStruct((nch, hr, cols), BF16)],
        compiler_params=_cparams(("parallel", "parallel")),
    )(c_arr, g.reshape(nch, 2, hr, cols), theirs)


def _sum_halves_w_in(g, theirs, c_arr):
    hr = D // 2
    sh = IN_DIM // NCHIP

    def body(c_ref, g_ref, t_ref, o_ref, ob_ref):
        s = g_ref[...] + t_ref[...]
        for j, dst, src, width in W_IN_PIECES:
            o_ref[j, :, dst:dst + width] = s[:, src:src + width]
            ob_ref[j, :, dst:dst + width] = s[:, src:src + width].astype(BF16)

    out = pl.BlockSpec((NCHIP, WT, sh), lambda i, c_ref: (0, i, 0))
    return pl.pallas_call(
        body, name="sum_halves_w_in",
        grid_spec=pltpu.PrefetchScalarGridSpec(
            num_scalar_prefetch=1, grid=(hr // WT,),
            in_specs=[pl.BlockSpec((None, WT, NP), lambda i, c_ref: (c_ref[0], i, 0)),
                      pl.BlockSpec((None, WT, NP), lambda i, c_ref: (0, i, 0))],
            out_specs=[out, out]),
        out_shape=[jax.ShapeDtypeStruct((NCHIP, hr, sh), F32), jax.ShapeDtypeStruct((NCHIP, hr, sh), BF16)],
        compiler_params=_cparams(("parallel",)),
    )(c_arr, g.reshape(2, hr, NP), theirs)


def _sum_chips(p, q, place, l, into=None, after=()):
    extra = ([into] if into is not None else []) + list(after)
    _, rows, cols = p.shape
    tr = _row_tile(rows)
    steps = rows // tr

    def body(place_ref, p_ref, q0, q1, q2, *rest):
        rest[-1][...] = ((p_ref[...] + q0[...].astype(F32)) + q1[...].astype(F32)) + q2[...].astype(F32)

    qs = lambda k: pl.BlockSpec((None, tr, cols), lambda i, place_ref: (k, i, 0))
    return pl.pallas_call(
        body, name="sum_chips",
        grid_spec=pltpu.PrefetchScalarGridSpec(
            num_scalar_prefetch=1, grid=(steps,),
            in_specs=[pl.BlockSpec((None, tr, cols), lambda i, place_ref: (place_ref[0], i, 0)), qs(0), qs(1), qs(2)]
            + [ANY] * len(extra),
            out_specs=pl.BlockSpec((None, tr, cols), lambda i, place_ref: (l, place_ref[1] * steps + i, 0))),
        out_shape=jax.ShapeDtypeStruct((DEPTH, 2 * rows, cols), F32),
        input_output_aliases={5: 0} if into is not None else {},
        compiler_params=_cparams(("parallel",)),
    )(place, p, q, q, q, *extra)


def _adamw_fn(w, g, m, v):
    nm = ADAM_B1 * m + (1.0 - ADAM_B1) * g
    nv = ADAM_B2 * v + (1.0 - ADAM_B2) * jnp.square(g)
    m_hat = nm / (1.0 - ADAM_B1 ** ADAM_STEP)
    v_hat = nv / (1.0 - ADAM_B2 ** ADAM_STEP)
    return -ADAM_LR * (m_hat / (jnp.sqrt(v_hat) + ADAM_EPS) + ADAM_WD * w), nm, nv


def _adamw(w, g, m, v):
    layers, rows, cols = w.shape
    tr = _row_tile(rows)

    def body(w_ref, g_ref, m_ref, v_ref, d_ref, nm_ref, nv_ref):
        d_ref[...], nm_ref[...], nv_ref[...] = _adamw_fn(w_ref[...], g_ref[...], m_ref[...], v_ref[...])

    blk = pl.BlockSpec((None, tr, cols), lambda l, i: (l, i, 0))
    return pl.pallas_call(
        body, name="adamw", grid=(layers, rows // tr), in_specs=[blk] * 4, out_specs=[blk] * 3,
        out_shape=[jax.ShapeDtypeStruct(w.shape, F32)] * 3, compiler_params=_cparams(("parallel", "parallel")),
    )(w, g, m, v)


def _adamw_small(ws, gs, ms, vs):
    n = len(ws)

    def body(*refs):
        for i in range(n):
            w_ref, g_ref, m_ref, v_ref, d_ref, nm_ref, nv_ref = (refs[k * n + i] for k in range(7))
            d_ref[...], nm_ref[...], nv_ref[...] = _adamw_fn(w_ref[...], g_ref[...], m_ref[...], v_ref[...])

    vm = pl.BlockSpec(memory_space=pltpu.VMEM)
    out = pl.pallas_call(
        body, name="adamw_small", in_specs=[vm] * (4 * n), out_specs=[vm] * (3 * n),
        out_shape=[jax.ShapeDtypeStruct(a.shape, F32) for a in list(ws) * 3],
        compiler_params=pltpu.CompilerParams(vmem_limit_bytes=VMEM_LIMIT),
    )(*ws, *gs, *ms, *vs)
    return out[:n], out[n:2 * n], out[2 * n:]


BIG = ("w_in", "w_out", "w_gate", "w_up", "w_down")
SMALL = ("norm_mix", "a_log", "dt_bias", "o_norm_g", "ln_v_g", "ln_v_b", "w_s", "b_s", "norm_ffn", "norm_final")
ORDER = ("norm_mix", "w_in", "conv_w", "a_log", "dt_bias", "o_norm_g", "ln_v_g", "ln_v_b", "w_s", "b_s", "w_out",
         "norm_ffn", "w_gate", "w_up", "w_down", "norm_final")


F32_ROWS = 8
PACK_ROWS = 128


def _lane_rows(size):
    return -(-size // (F32_ROWS * LANE)) * F32_ROWS


def _pack(arrs):
    parts = [jnp.pad(a.reshape(-1), (0, _lane_rows(a.size) * LANE - a.size)).reshape(-1, LANE) for a in arrs]
    rows = sum(p.shape[0] for p in parts)
    if rows % PACK_ROWS:
        parts.append(jnp.zeros((-rows % PACK_ROWS, LANE), F32))
    return jnp.concatenate(parts, axis=0)


def _unpack(buf, like):
    out, row = [], 0
    for a in like:
        n = _lane_rows(a.size)
        out.append(buf[row:row + n].reshape(-1)[:a.size].reshape(a.shape))
        row += n
    return out


def kernel(x, norm_mix, w_in, conv_w, a_log, dt_bias, o_norm_g, ln_v_g, ln_v_b, w_s, b_s, w_out, norm_ffn, w_gate, w_up, w_down, norm_final, loss_target, m_norm_mix, m_w_in, m_conv_w, m_a_log, m_dt_bias, m_o_norm_g, m_ln_v_g, m_ln_v_b, m_w_s, m_b_s, m_w_out, m_norm_ffn, m_w_gate, m_w_up, m_w_down, m_norm_final, v_norm_mix, v_w_in, v_conv_w, v_a_log, v_dt_bias, v_o_norm_g, v_ln_v_g, v_ln_v_b, v_w_s, v_b_s, v_w_out, v_norm_ffn, v_w_gate, v_w_up, v_w_down, v_norm_final):
    w = dict(norm_mix=norm_mix, w_in=w_in, conv_w=conv_w, a_log=a_log, dt_bias=dt_bias, o_norm_g=o_norm_g,
             ln_v_g=ln_v_g, ln_v_b=ln_v_b, w_s=w_s, b_s=b_s, w_out=w_out, norm_ffn=norm_ffn, w_gate=w_gate, w_up=w_up,
             w_down=w_down, norm_final=norm_final)
    m = dict(norm_mix=m_norm_mix, w_in=m_w_in, conv_w=m_conv_w, a_log=m_a_log, dt_bias=m_dt_bias, o_norm_g=m_o_norm_g,
             ln_v_g=m_ln_v_g, ln_v_b=m_ln_v_b, w_s=m_w_s, b_s=m_b_s, w_out=m_w_out, norm_ffn=m_norm_ffn,
             w_gate=m_w_gate, w_up=m_w_up, w_down=m_w_down, norm_final=m_norm_final)
    v = dict(norm_mix=v_norm_mix, w_in=v_w_in, conv_w=v_conv_w, a_log=v_a_log, dt_bias=v_dt_bias, o_norm_g=v_o_norm_g,
             ln_v_g=v_ln_v_g, ln_v_b=v_ln_v_b, w_s=v_w_s, b_s=v_b_s, w_out=v_w_out, norm_ffn=v_norm_ffn,
             w_gate=v_w_gate, w_up=v_w_up, w_down=v_w_down, norm_final=v_norm_final)
    chip = 2 * lax.axis_index("x") + lax.axis_index("y")
    place = jnp.stack([chip, lax.axis_index("c")]).astype(jnp.int32)
    c_arr = place[1:]

    def kernel_view(n, a):
        return jnp.swapaxes(a, 1, 2) if n in ("w_gate", "w_up") else a

    own = {n: [kernel_view(n, w[n])[l].astype(BF16) for l in range(DEPTH)] for n in BIG}
    by_chip = lambda a: jax.ShapeDtypeStruct((NCHIP,) + a.shape, a.dtype)

    def start(name, srcs, whole, after=()):
        return _split_start(name, _gather_plan(whole), srcs, [by_chip(a) for a in srcs], 3 * len(srcs), after)

    def finish(name, started, whole, after):
        srcs, lands = _split_wait(name, _gather_plan(whole), started, after)
        passed = iter(_forward_halves([g for g, all_of_it in zip(lands, whole) if not all_of_it]))
        lands = [g if all_of_it else next(passed) for g, all_of_it in zip(lands, whole)]
        return srcs, [lax.dynamic_update_index_in_dim(g, o, chip, 0) for g, o in zip(lands, srcs)]

    ffn = BIG[1:]
    first = start("gather_first_start", [own["w_in"][0], conv_w], [False, True])
    early = start("gather_early_start", [own[n][0] for n in ffn], [False] * len(ffn), [first["token"]])
    mid = start("gather_mid_start", [own["w_in"][1]], [False], [early["token"]])
    later = start("gather_later_start", [own[n][1] for n in ffn], [False] * len(ffn), [mid["token"]])
    hn = _rmsnorm("rms_mix", x[0], norm_mix[0][None])
    (own_w_in, _), (w_in_by_chip, conv_by_chip) = finish("gather_first_wait", first, [False, True], [later["token"], hn])

    passing = {}

    def pass_on(tag, started, n):
        def at(after):
            srcs, lands = _split_wait(f"gather_{tag}_wait", _gather_plan([False] * n), started, after)
            passing[tag] = srcs, _forward_start(f"forward_{tag}_start", lands, ())
            return [passing[tag][1]["token"]]
        return at

    def passed_on(tag, after):
        srcs, fwd = passing[tag]
        lands = _forward_wait(f"forward_{tag}_wait", fwd, [after])
        return srcs, [lax.dynamic_update_index_in_dim(g, o, chip, 0) for g, o in zip(lands, srcs)]

    def late(tag):
        return lambda after: dict(zip(ffn, passed_on(tag, after)[1]))

    layer0 = _layer_params(0, dict(
        hn=hn, w_in=_assemble_w_in(w_in_by_chip, own_w_in, place), conv_w=conv_by_chip, late=late("early"),
        before_mix=pass_on("early", early, len(ffn)), before_ffn_out=pass_on("mid", mid, 1)), w)

    def layer1(after):
        (own_w_in1,), (w_in1_by_chip,) = passed_on("mid", after)
        return _layer_params(1, dict(w_in=_assemble_w_in(w_in1_by_chip, own_w_in1, place), conv_w=conv_by_chip,
                                     late=late("later"), before_mix=pass_on("later", later, len(ffn))), w)

    saved, layers, loss_lanes, dh, dhb, d_norm_final = _forward(x[0], loss_target[0], [layer0, layer1],
                                                                 norm_final[None])

    sums, arrived = {}, {}

    def exchange_start(tag, l, names, grads, after=()):
        mine = [grads[n] for n in names]
        shapes = [jax.ShapeDtypeStruct((g.shape[0], g.shape[1] // 2, g.shape[2]), F32) for g in mine]
        return tag, l, names, _split_start(f"exchange_{tag}_start", _exchange_plan, mine, shapes, len(mine), after)

    def add_halves(l, names, mine, theirs):
        for n, g, t in zip(names, mine, theirs):
            sums[l, n] = (_sum_halves_w_in if n == "w_in" else _sum_halves)(g, t, c_arr)

    def exchange_wait(handle, after):
        tag, l, names, started = handle
        add_halves(l, names, *_split_wait(f"exchange_{tag}_wait", _exchange_plan, started, after))

    def scatter_start(tag, l, names, after=()):
        partial = [sums[l, n][1] for n in names]
        shapes = [jax.ShapeDtypeStruct((3,) + p.shape[1:], p.dtype) for p in partial]
        return tag, l, names, _split_start(f"scatter_{tag}_start", _scatter_plan, partial, shapes, 3 * len(names), after)

    def scatter_wait(handle, after):
        tag, l, names, started = handle
        for n, q in zip(names, _split_wait(f"scatter_{tag}_wait", _scatter_plan, started, after)[1]):
            arrived[l, n] = q

    last = DEPTH - 1
    swiglu = BIG[2:]
    dh1, dh1b, g_ffn = _layer_bwd_ffn(dh, dhb, layers[last], saved[last])
    dh, dhb, g_mix = _layer_bwd_mixer(dh1, dh1b, layers[last], saved[last])
    gl = [None, _reference_layout({**g_ffn, **g_mix})]
    ex_last = exchange_start("last", last, BIG, gl[last])
    dh1, dh1b, g_ffn = _layer_bwd_ffn(dh, dhb, layers[0], saved[0], after=[ex_last[-1]["token"]])
    exchange_wait(ex_last, dh1)
    sc_last = scatter_start("last", last, BIG)
    ex_ffn = exchange_start("swiglu", 0, swiglu, g_ffn, [sc_last[-1]["token"]])
    sc_ffn = []

    def midway(do):
        exchange_wait(ex_ffn, do)
        sc_ffn.append(scatter_start("swiglu", 0, swiglu))
        return [sc_ffn[0][-1]["token"]]

    ex_rest = []

    def late(grads):
        rest_grads = dict(w_in=grads["w_in"], w_out=grads["w_out"].reshape(NCHIP, D // NCHIP, D))
        ex_rest.append(exchange_start("rest", 0, BIG[:2], rest_grads))
        return [ex_rest[0][-1]["token"]]

    dx, _, g_mix = _layer_bwd_mixer(dh1, dh1b, layers[0], saved[0], after=[ex_ffn[-1]["token"]], midway=midway,
                                    late=late)
    scatter_wait(sc_last, dx)
    scatter_wait(sc_ffn[0], dx)
    gl[0] = _reference_layout({**g_ffn, **g_mix})

    small_g = [jnp.stack([gl[l][n] for l in range(DEPTH)]) for n in SMALL[:-1]] + [d_norm_final[0]]
    conv_g = jnp.stack([gl[l]["conv_w"] for l in range(DEPTH)])
    summed = small_g + [conv_g, loss_lanes[0, :1]]
    total = _allreduce_small(_pack(summed))
    exchange_wait(ex_rest[0], total)
    sc_rest = scatter_start("rest", 0, BIG[:2])

    travelling = [sc_rest[-1]["token"]]
    reduced, g_out, delta, new_m, new_v = {}, {}, {}, {}, {}

    done = []

    def adamw_large(names, joined):
        for n, g in zip(names, joined):
            res = _adamw(kernel_view(n, w[n]), g, kernel_view(n, m[n]), kernel_view(n, v[n]))
            done.append(res[2])
            g_out[n], delta[n], new_m[n], new_v[n] = (kernel_view(n, a) for a in (g,) + tuple(res))

    for n in BIG:
        for l in (range(DEPTH) if n in swiglu else [last]):
            reduced[n] = _sum_chips(sums[l, n][0], arrived[l, n], place, l, into=reduced.get(n), after=travelling)
    adamw_large(swiglu, _join_halves("join_swiglu", [reduced[n] for n in swiglu]))
    scatter_wait(sc_rest, done + [reduced[n] for n in BIG[:2]])
    for n in BIG[:2]:
        reduced[n] = _sum_chips(sums[0, n][0], arrived[0, n], place, 0, into=reduced[n])
    adamw_large(BIG[:2], _join_halves("join_rest", [reduced[n] for n in BIG[:2]]))
    *small_r, conv_r, loss = _unpack(total, summed)
    g_out.update(zip(SMALL, small_r))
    g_out["conv_w"] = lax.dynamic_slice_in_dim(conv_r, chip * conv_w.shape[2], conv_w.shape[2], axis=2)

    rest = SMALL + ("conv_w",)
    rows_of = lambda a: a.reshape(1, -1) if a.ndim == 1 else a
    results = _adamw_small(*[[rows_of(src[n]) for n in rest] for src in (w, g_out, m, v)])
    for dst, arrs in zip((delta, new_m, new_v), results):
        dst.update({n: a.reshape(w[n].shape) for n, a in zip(rest, arrs)})

    return (loss[0], dx[None], *[g_out[n] for n in ORDER], *[delta[n] for n in ORDER], *[new_m[n] for n in ORDER],
            *[new_v[n] for n in ORDER])
```

```python
import functools

import jax
import jax.numpy as jnp
from jax import lax
from jax.experimental import pallas as pl
from jax.experimental.pallas import tpu as pltpu

F32 = jnp.float32
BF16 = jnp.bfloat16
MESH = pl.DeviceIdType.MESH
ANY = pl.BlockSpec(memory_space=pl.ANY)
HIGHEST = lax.Precision.HIGHEST

T = 2048
D = 1024
DEPTH = 2
NCHIP = 4
HEADS = 4
HD = 128
HW = HEADS * HD
CH = 64
GCH = 128
IN_DIM = 3080
NP = 3200
BA_OFF = 3072
FF_SH = 704
EPS = 1e-6
LANE = 128
VMEM_LIMIT = 56 * 1024 * 1024

ADAM_LR = 0.001
ADAM_B1 = 0.9
ADAM_B2 = 0.999
ADAM_EPS = 1e-08
ADAM_WD = 0.01
ADAM_STEP = 10


def _cparams(sem=None):
    return pltpu.CompilerParams(dimension_semantics=sem, vmem_limit_bytes=VMEM_LIMIT)


_DIMS = {"nn": (((1,), (0,)), ((), ())), "nt": (((1,), (1,)), ((), ())), "tn": (((0,), (0,)), ((), ()))}


def _mm(name, mode, a, bs, *, tm, tn, tk, out_dtypes=(F32,), reduce_g=False, resid=None, extras=(), epilogue=None,
        after=(), fold_g=False, sub_m=1):
    assert sub_m == 1 or (mode != "tn" and tm % (8 * sub_m) == 0), (name, sub_m)
    nb = len(bs)
    ga = a.shape[0]
    gbs = [b.shape[0] for b in bs]
    g_n = max([ga] + gbs)
    if mode == "tn":
        k_n, m_n = a.shape[1:]
    else:
        m_n, k_n = a.shape[1:]
    n_n = bs[0].shape[1] if mode == "nt" else bs[0].shape[2]
    assert m_n % tm == 0 and n_n % tn == 0 and k_n % tk == 0, (name, m_n, n_n, k_n)
    mi, nj, kk = m_n // tm, n_n // tn, k_n // tk
    lead = g_n if fold_g else None
    g_steps = 1 if fold_g else g_n
    grid = (mi, nj, g_steps, kk)
    ids = lambda i, j, g, k: (g, i, j, k)
    n_red = (g_steps if reduce_g else 1) * kk
    red_idx = lambda: (pl.program_id(2) * kk if reduce_g else 0) + pl.program_id(3)
    sem = ("parallel", "parallel", "arbitrary" if reduce_g else "parallel", "arbitrary")

    def pick(gsz, g):
        return g if gsz > 1 else 0

    def a_map(*p):
        g, i, j, k = ids(*p)
        return (pick(ga, g), k, i) if mode == "tn" else (pick(ga, g), i, k)

    def b_map(gsz):
        def f(*p):
            g, i, j, k = ids(*p)
            return (pick(gsz, g), j, k) if mode == "nt" else (pick(gsz, g), k, j)
        return f

    def o_map(gsz):
        def f(*p):
            g, i, j, k = ids(*p)
            return (0 if reduce_g else pick(gsz, g), i, j)
        return f

    a_spec = pl.BlockSpec((lead, tk, tm) if mode == "tn" else (lead, tm, tk), a_map)
    b_specs = [pl.BlockSpec((lead, tn, tk) if mode == "nt" else (lead, tk, tn), b_map(gs)) for gs in gbs]
    x_specs = [pl.BlockSpec((None, tm, tn), o_map(e.shape[0])) for e in extras]
    r_specs = [pl.BlockSpec((None, tm, tn), o_map(resid.shape[0]))] if resid is not None else []
    g_out = 1 if reduce_g else g_n
    out_shape = [jax.ShapeDtypeStruct((g_out, m_n, n_n), dt) for dt in out_dtypes]
    out_specs = [pl.BlockSpec((None, tm, tn), o_map(g_out)) for _ in out_dtypes]
    nx, nr, no = len(extras), len(r_specs), len(out_dtypes)
    n_in = 1 + nb + nx + nr + len(after)
    dims = _DIMS[mode]

    def body(*refs):
        a_ref = refs[0]
        b_refs = refs[1:1 + nb]
        x_refs = refs[1 + nb:1 + nb + nx]
        r_refs = refs[1 + nb + nx:1 + nb + nx + nr]
        o_refs = refs[n_in:n_in + no]
        acc_refs = refs[n_in + no:]
        def dots(rows):
            if fold_g:
                return [sum(lax.dot_general(a_ref[g, rows, :], b_ref[g], dims, preferred_element_type=F32)
                            for g in range(g_n)) for b_ref in b_refs]
            av = a_ref[...] if mode == "tn" else a_ref[rows, :]
            return [lax.dot_general(av, b_ref[...], dims, preferred_element_type=F32) for b_ref in b_refs]

        def finish(accs, rows=slice(None)):
            if r_refs:
                accs[0] = accs[0] + r_refs[0][rows, :]
            outs = epilogue(accs, [x[rows, :] for x in x_refs]) if epilogue is not None else accs
            for o_ref, o in zip(o_refs, outs):
                o_ref[rows, :] = o.astype(o_ref.dtype)

        if n_red == 1:
            slabs = [slice(s * (tm // sub_m), (s + 1) * (tm // sub_m)) for s in range(sub_m)]
            ahead = dots(slabs[0])
            for s, rows in enumerate(slabs):
                now, ahead = ahead, (dots(slabs[s + 1]) if s + 1 < sub_m else None)
                finish(now, rows)
            return
        products = dots(slice(None))
        r = red_idx()
        for p, acc in zip(products, acc_refs):
            @pl.when(r == 0)
            def _():
                acc[...] = p

            @pl.when((r > 0) & (r < n_red - 1))
            def _():
                acc[...] += p

        @pl.when(r == n_red - 1)
        def _():
            finish([acc[...] + p for p, acc in zip(products, acc_refs)])

    return pl.pallas_call(
        body, name=name, grid=grid,
        in_specs=[a_spec] + b_specs + x_specs + r_specs + [ANY] * len(after),
        out_specs=out_specs, out_shape=out_shape,
        scratch_shapes=[pltpu.VMEM((tm, tn), F32) for _ in range(nb if n_red > 1 else 0)],
        compiler_params=_cparams(sem),
    )(a, *bs, *extras, *([resid] if resid is not None else []), *after)


def _sigmoid(x):
    return 1.0 / (1.0 + jnp.exp(-x))


def _silu(x):
    return x * _sigmoid(x)


def _gelu(x):
    return 0.5 * x * (1.0 + jnp.tanh(0.7978845608028654 * (x + 0.044715 * (x * x * x))))


def _rms_fn(h, gain):
    return h * lax.rsqrt(jnp.mean(h * h, axis=-1, keepdims=True) + EPS) * gain


def _shift_impl(x, s):
    n = x.shape[0]
    rolled = pltpu.roll(x, s % n, 0)
    row = lax.broadcasted_iota(jnp.int32, x.shape, 0)
    return jnp.where((row >= s) & (row < n + s), rolled, 0.0)


@functools.partial(jax.custom_vjp, nondiff_argnums=(1,))
def _shift(x, s):
    return _shift_impl(x, s)


def _shift_fwd(x, s):
    return _shift_impl(x, s), None


def _shift_bwd(s, _, g):
    return (_shift_impl(g, -s),)


_shift.defvjp(_shift_fwd, _shift_bwd)


def _prep_fn(x, w, qk_scale, is_v):
    y = x * w[3:4, :]
    for i in range(3):
        y = y + _shift(x, 3 - i) * w[i:i + 1, :]
    y = _silu(y)
    nrm = lax.rsqrt(jnp.sum(y * y, axis=-1, keepdims=True) + EPS) * qk_scale
    return y * jnp.where(is_v, 1.0, nrm)


def _softplus(x):
    return jnp.maximum(x, 0.0) + jnp.log(1.0 + jnp.exp(-jnp.abs(x)))


def _gates_fn(ba, a_log, dt_bias):
    lane = lax.broadcasted_iota(jnp.int32, ba.shape, 1)
    beta = _sigmoid(ba)
    g = -jnp.exp(a_log) * _softplus(ba + dt_bias)
    return jnp.where(lane < HEADS, beta, g)


def _dot16(a, b, dims=_DIMS["nn"]):
    return lax.dot_general(a.astype(BF16), b.astype(BF16), dims, preferred_element_type=F32)


def _dot32(a, b):
    return jnp.dot(a, b, preferred_element_type=F32, precision=HIGHEST)


def _dot3(a, b, dims=_DIMS["nn"]):
    return lax.dot_general(a, b, dims, preferred_element_type=F32, precision=lax.Precision.HIGH)


def _tri_inverses(mats, tick=lambda: None):
    row = lax.broadcasted_iota(jnp.int32, (CH, CH), 0)
    col = lax.broadcasted_iota(jnp.int32, (CH, CH), 1)
    eye = (row == col).astype(F32)
    ts = [eye - a for a in mats]
    ps = list(mats)
    for _ in range(5):
        ps = [_dot3(p, p) for p in ps]
        tick()
        ts = [t + _dot3(t, p) for t, p in zip(ts, ps)]
        tick()
    return ts


@jax.custom_vjp
def _tri_solves(mats, rhs):
    return [_dot3(t, b) for t, b in zip(_tri_inverses(mats), rhs)]


def _tri_solves_fwd(mats, rhs):
    ts = _tri_inverses(mats)
    xs = [_dot3(t, b) for t, b in zip(ts, rhs)]
    return xs, (ts, xs)


def _tri_solves_bwd(res, dxs):
    ts, xs = res
    dbs = [_dot3(t, dx, _DIMS["tn"]) for t, dx in zip(ts, dxs)]
    return [-_dot3(db, x, _DIMS["nt"]) for db, x in zip(dbs, xs)], dbs


_tri_solves.defvjp(_tri_solves_fwd, _tri_solves_bwd)


def _chunk_prep_fn(xs, bgs, tick=None):
    step = tick or (lambda: None)
    row = lax.broadcasted_iota(jnp.int32, (CH, CH), 0)
    col = lax.broadcasted_iota(jnp.int32, (CH, CH), 1)
    incl = row >= col
    strict = row > col
    lmat = incl.astype(F32)
    n = len(xs)
    items = [(i, h) for i in range(n) for h in range(HEADS)]
    part = lambda i, h, c: xs[i][:, c * HW + h * HD:c * HW + (h + 1) * HD]
    q = [part(i, h, 0) for i, h in items]
    k = [part(i, h, 1) for i, h in items]
    v = [part(i, h, 2) for i, h in items]
    beta = [bgs[i][:, h:h + 1] for i, h in items]
    gc_all = [_dot32(lmat, bg) for bg in bgs]
    step()
    gc = [gc_all[i][:, HEADS + h:HEADS + h + 1] for i, h in items]
    gmat = [jnp.where(strict, jnp.broadcast_to(bgs[i][:, HEADS + h:HEADS + h + 1], (CH, CH)), 0.0) for i, h in items]
    diff = [_dot3(lmat, m) for m in gmat]
    step()
    decay = [jnp.where(incl, jnp.exp(jnp.where(incl, d, 0.0)), 0.0) for d in diff]
    k_beta = [kk * b for kk, b in zip(k, beta)]
    kk_t = [_dot16(kb, kk, _DIMS["nt"]) for kb, kk in zip(k_beta, k)]
    step()
    qk_t = [_dot16(qq, kk, _DIMS["nt"]) for qq, kk in zip(q, k)]
    step()
    a = [jnp.where(strict, m * d, 0.0) for m, d in zip(kk_t, decay)]
    eg = [jnp.exp(g) for g in gc]
    rhs = [jnp.concatenate([vv * b, kb * e], axis=-1) for vv, b, kb, e in zip(v, beta, k_beta, eg)]
    if tick is None:
        uw = _tri_solves(a, rhs)
    else:
        uw = [_dot3(t, b) for t, b in zip(_tri_inverses(a, tick), rhs)]
    qk = [m * d for m, d in zip(qk_t, decay)]
    g_last = [g[CH - 1:CH, :] for g in gc]
    qe = [qq * e for qq, e in zip(q, eg)]
    kd = [kk * jnp.exp(gl - g) for kk, gl, g in zip(k, g_last, gc)]
    egl = [jnp.broadcast_to(jnp.exp(gl), (1, HD)) for gl in g_last]
    out = []
    for i in range(n):
        mine = slice(i * HEADS, (i + 1) * HEADS)
        cat = lambda vals: jnp.concatenate(vals[mine], axis=-1)
        out.append((cat([x[:, :HD] for x in uw]), cat([x[:, HD:] for x in uw]), cat(qe), cat(kd),
                    jnp.concatenate([m[None] for m in qk[mine]], axis=0), cat(egl)))
    return out


def _state_levels(chunks, s, outs, befores, final):
    for u, w, qe, kd, qk, egl in chunks:
        befores.append(s)
        ws = [_dot16(a, b) for a, b in zip(w, s)]
        qs = [_dot16(a, b) for a, b in zip(qe, s)]
        yield
        v_new = [a - b for a, b in zip(u, ws)]
        outs.append([a + _dot16(b, c) for a, b, c in zip(qs, qk, v_new)])
        s = [a * e + _dot16(b, c, _DIMS["tn"]) for a, e, b, c in zip(s, egl, kd, v_new)]
        yield
    final.append(s)


def _chunk_state_fn(u, w, qe, kd, qk, egl, s):
    ws = [_dot16(a, b) for a, b in zip(w, s)]
    qs = [_dot16(a, b) for a, b in zip(qe, s)]
    v_new = [a - b for a, b in zip(u, ws)]
    o = [a + _dot16(b, c) for a, b, c in zip(qs, qk, v_new)]
    s_new = [a * e + _dot16(b, c, _DIMS["tn"]) for a, e, b, c in zip(s, egl, kd, v_new)]
    return o, s_new


def _mix_fn(o, z, ur, vr, ong, lng, lnb, ws, bst):
    row = lax.broadcasted_iota(jnp.int32, (GCH, GCH), 0)
    col = lax.broadcasted_iota(jnp.int32, (GCH, GCH), 1)
    causal = row >= col
    ug = _gelu(ur)
    vg = _gelu(vr)
    sls = [slice(h * HD, (h + 1) * HD) for h in range(HEADS)]
    oh = [o[:, sl] for sl in sls]
    oh = [x * lax.rsqrt(jnp.mean(x * x, axis=-1, keepdims=True) + EPS) for x in oh]
    outs_dn = [x * ong * _silu(z[:, sl]) for x, sl in zip(oh, sls)]
    vh = [vg[:, sl] for sl in sls]
    mu = [jnp.mean(x, axis=-1, keepdims=True) for x in vh]
    var = [jnp.mean(jnp.square(x - m), axis=-1, keepdims=True) for x, m in zip(vh, mu)]
    vn = [(x - m) * lax.rsqrt(s + EPS) * lng[:, sl] + lnb[:, sl] for x, m, s, sl in zip(vh, mu, var, sls)]
    mixed = [_dot16(jnp.where(causal, ws[h], 0.0), vn[h]) for h in range(HEADS)]
    outs_gm = [ug[:, sl] * (mixed[h] + bst[:, h:h + 1]) for h, sl in enumerate(sls)]
    return jnp.concatenate(outs_dn + outs_gm, axis=-1)


def _loss_fn(h, gain, tgt):
    y = _rms_fn(h, gain)
    return 0.5 * jnp.sum(jnp.mean(jnp.square(y - tgt), axis=-1))


RT = 512


def _rows(n=D):
    return pl.BlockSpec((RT, n), lambda i: (i, 0))


def _whole(shape):
    nd = len(shape)
    return pl.BlockSpec(shape, lambda i: (0,) * nd)


def _rmsnorm(name, h, gain):
    def body(h_ref, g_ref, o_ref):
        o_ref[...] = _rms_fn(h_ref[...], g_ref[...]).astype(BF16)

    return pl.pallas_call(
        body, name=name, grid=(T // RT,), in_specs=[_rows(), _whole((1, D))], out_specs=_rows(),
        out_shape=jax.ShapeDtypeStruct((T, D), BF16), compiler_params=_cparams(("parallel",)),
    )(h, gain)


def _rmsnorm_bwd(name, dhn, h, gain, resid):
    def body(dhn_ref, h_ref, g_ref, r_ref, dh_ref, dh16_ref, dg_ref):
        _, vjp = jax.vjp(_rms_fn, h_ref[...], g_ref[...])
        dh, dg = vjp(dhn_ref[...])
        dh = r_ref[...] + dh
        dh_ref[...] = dh
        dh16_ref[...] = dh.astype(BF16)

        @pl.when(pl.program_id(0) == 0)
        def _():
            dg_ref[...] = dg

        @pl.when(pl.program_id(0) > 0)
        def _():
            dg_ref[...] += dg

    return pl.pallas_call(
        body, name=name, grid=(T // RT,), in_specs=[_rows(), _rows(), _whole((1, D)), _rows()],
        out_specs=[_rows(), _rows(), _whole((1, D))],
        out_shape=[jax.ShapeDtypeStruct((T, D), F32), jax.ShapeDtypeStruct((T, D), BF16),
                   jax.ShapeDtypeStruct((1, D), F32)],
        compiler_params=_cparams(("arbitrary",)),
    )(dhn, h, gain, resid)


def _loss_head(h, gain, tgt):
    def body(h_ref, g_ref, t_ref, l_ref, dh_ref, dh16_ref, dg_ref):
        loss, vjp = jax.vjp(lambda hh, gg: _loss_fn(hh, gg, t_ref[...]), h_ref[...], g_ref[...])
        dh, dg = vjp(jnp.ones((), F32))
        dh_ref[...] = dh
        dh16_ref[...] = dh.astype(BF16)
        lv = jnp.full((1, LANE), loss, F32)

        @pl.when(pl.program_id(0) == 0)
        def _():
            dg_ref[...] = dg
            l_ref[...] = lv

        @pl.when(pl.program_id(0) > 0)
        def _():
            dg_ref[...] += dg
            l_ref[...] += lv

    return pl.pallas_call(
        body, name="loss_head", grid=(T // RT,), in_specs=[_rows(), _whole((1, D)), _rows()],
        out_specs=[_whole((1, LANE)), _rows(), _rows(), _whole((1, D))],
        out_shape=[jax.ShapeDtypeStruct((1, LANE), F32), jax.ShapeDtypeStruct((T, D), F32),
                   jax.ShapeDtypeStruct((T, D), BF16), jax.ShapeDtypeStruct((1, D), F32)],
        compiler_params=_cparams(("arbitrary",)),
    )(h, gain, tgt)


def _prep_flags():
    j = pl.program_id(0)
    qk_scale = jnp.where(j < HEADS, HD ** -0.5, 1.0).astype(F32)
    return qk_scale, j >= 2 * HEADS


def _prep(proj, conv_w):
    def body(x_ref, w_ref, o_ref):
        qk_scale, is_v = _prep_flags()
        o_ref[...] = _prep_fn(x_ref[...], w_ref[...], qk_scale, is_v)

    col = lambda j: (0, j)
    return pl.pallas_call(
        body, name="gdn_prep", grid=(3 * HEADS,),
        in_specs=[pl.BlockSpec((T, HD), col), pl.BlockSpec((4, HD), col)], out_specs=pl.BlockSpec((T, HD), col),
        out_shape=jax.ShapeDtypeStruct((T, 3 * HW), F32), compiler_params=_cparams(("parallel",)),
    )(proj, conv_w)


def _prep_bwd(proj, conv_w, dqkv, dproj):
    def body(x_ref, w_ref, d_ref, _, dx_ref, dw_ref):
        qk_scale, is_v = _prep_flags()
        _, vjp = jax.vjp(lambda x, w: _prep_fn(x, w, qk_scale, is_v), x_ref[...], w_ref[...])
        dx, dw = vjp(d_ref[...])
        dx_ref[...] = dx.astype(BF16)
        dw_ref[...] = dw

    col = lambda j: (0, j)
    return pl.pallas_call(
        body, name="gdn_prep_bwd", grid=(3 * HEADS,),
        in_specs=[pl.BlockSpec((T, HD), col), pl.BlockSpec((4, HD), col), pl.BlockSpec((T, HD), col), ANY],
        out_specs=[pl.BlockSpec((T, HD), col), pl.BlockSpec((4, HD), col)],
        out_shape=[jax.ShapeDtypeStruct((T, NP), BF16), jax.ShapeDtypeStruct((4, 3 * HW), F32)],
        input_output_aliases={3: 0}, compiler_params=_cparams(("parallel",)),
    )(proj, conv_w, dqkv, dproj)


BA_BLK = BA_OFF // LANE


def _gates(proj, a_log, dt_bias):
    def body(x_ref, a_ref, d_ref, o_ref):
        o_ref[...] = _gates_fn(x_ref[...], a_ref[...], d_ref[...])

    return pl.pallas_call(
        body, name="gdn_gates", grid=(1,),
        in_specs=[pl.BlockSpec((T, LANE), lambda i: (0, BA_BLK)), _whole((1, LANE)), _whole((1, LANE))],
        out_specs=_whole((T, LANE)),
        out_shape=jax.ShapeDtypeStruct((T, LANE), F32), compiler_params=_cparams(("arbitrary",)),
    )(proj, a_log, dt_bias)


def _gates_bwd(proj, a_log, dt_bias, dbg, dproj):
    def body(x_ref, a_ref, d_ref, dbg_ref, _, dx_ref, da_ref, dd_ref):
        _, vjp = jax.vjp(_gates_fn, x_ref[...], a_ref[...], d_ref[...])
        dx, da_ref[...], dd_ref[...] = vjp(dbg_ref[...])
        dx_ref[...] = dx.astype(BF16)

    ba = pl.BlockSpec((T, LANE), lambda i: (0, BA_BLK))
    return pl.pallas_call(
        body, name="gdn_gates_bwd", grid=(1,),
        in_specs=[ba, _whole((1, LANE)), _whole((1, LANE)), _whole((T, LANE)), ANY],
        out_specs=[ba, _whole((1, LANE)), _whole((1, LANE))],
        out_shape=[jax.ShapeDtypeStruct((T, NP), BF16), jax.ShapeDtypeStruct((1, LANE), F32),
                   jax.ShapeDtypeStruct((1, LANE), F32)],
        input_output_aliases={4: 0}, compiler_params=_cparams(("arbitrary",)),
    )(proj, a_log, dt_bias, dbg, dproj)


NCK = T // CH
CPS = 4


def _chunk_group_specs(at):
    wide = pl.BlockSpec((CPS * CH, HW), lambda n: (at(n), 0))
    return [wide, wide, wide, wide, pl.BlockSpec((HEADS, CPS * CH, CH), lambda n: (0, at(n), 0)),
            pl.BlockSpec((CPS, 1, HW), lambda n: (at(n), 0, 0))]


def _chunk_prep_shapes(dtypes):
    shp = [(T, HW), (T, HW), (T, HW), (T, HW), (HEADS, T, CH), (NCK, 1, HW)]
    return [jax.ShapeDtypeStruct(s, dt) for s, dt in zip(shp, dtypes)]


NGROUP = NCK // CPS
PREP_DTYPES = (F32, BF16, BF16, BF16, BF16, F32)


def _delta_rule(qkv, bg):
    def body(x_ref, bg_ref, *refs):
        prep_out, (o_ref, sh_ref), held, s_ref = refs[:6], refs[6:8], refs[8:14], refs[14]
        i = pl.program_id(0)

        @pl.when(i == 0)
        def _():
            for r in held + (s_ref,):
                r[...] = jnp.zeros_like(r)

        rows = [slice(ci * CH, (ci + 1) * CH) for ci in range(CPS)]
        u_h, w_h, qe_h, kd_h, qk_h, egl_h = held
        chunks = [_head_args((u_h.at[r, :], w_h.at[r, :], qe_h.at[r, :], kd_h.at[r, :], qk_h.at[:, r, :], egl_h.at[ci]))
                  for ci, r in enumerate(rows)]
        start = [jnp.where(i <= 1, 0.0, s_ref[h]) for h in range(HEADS)]
        outs, befores, final = [], [], []
        levels = _state_levels(chunks, start, outs, befores, final)
        res = _chunk_prep_fn([x_ref[r, :] for r in rows], [bg_ref[r, :] for r in rows], tick=lambda: next(levels, None))
        for _ in levels:
            pass
        for ci, (u, w, qe, kd, qk, egl) in enumerate(res):
            for refs_pair, val in zip(zip(prep_out[:4], held[:4]), (u, w, qe, kd)):
                for ref in refs_pair:
                    ref[rows[ci], :] = val.astype(ref.dtype)
            for ref in (prep_out[4], qk_h):
                ref[:, rows[ci], :] = qk.astype(ref.dtype)
            for ref in (prep_out[5], egl_h):
                ref[ci] = egl
        for ci, r in enumerate(rows):
            for h in range(HEADS):
                o_ref[r, h * HD:(h + 1) * HD] = outs[ci][h]
                sh_ref[h, ci] = befores[ci][h]
        for h in range(HEADS):
            s_ref[h] = final[0][h]

    now = lambda n: jnp.minimum(n, NGROUP - 1)
    was = lambda n: jnp.maximum(n - 1, 0)
    wide = lambda at: pl.BlockSpec((CPS * CH, HW), lambda n: (at(n), 0))
    held = [pltpu.VMEM(s, dt) for s, dt in zip(
        [(CPS * CH, HW)] * 4 + [(HEADS, CPS * CH, CH), (CPS, 1, HW)], PREP_DTYPES)]
    out = pl.pallas_call(
        body, name="gdn_delta_rule", grid=(NGROUP + 1,),
        in_specs=[pl.BlockSpec((CPS * CH, 3 * HW), lambda n: (now(n), 0)),
                  pl.BlockSpec((CPS * CH, LANE), lambda n: (now(n), 0))],
        out_specs=[wide(now)] * 4 + [pl.BlockSpec((HEADS, CPS * CH, CH), lambda n: (0, now(n), 0)),
                                     pl.BlockSpec((CPS, 1, HW), lambda n: (now(n), 0, 0)), wide(was),
                                     pl.BlockSpec((HEADS, CPS, HD, HD), lambda n: (0, was(n), 0, 0))],
        out_shape=_chunk_prep_shapes(PREP_DTYPES) + [jax.ShapeDtypeStruct((T, HW), F32),
                                                     jax.ShapeDtypeStruct((HEADS, NCK, HD, HD), F32)],
        scratch_shapes=held + [pltpu.VMEM((HEADS, HD, HD), F32)], compiler_params=_cparams(("arbitrary",)),
    )(qkv, bg)
    return out[:6], out[6], out[7]


def _chunk_prep_bwd(qkv, bg, cots):
    def body(x_ref, bg_ref, du, dw, dqe, dkd, dqk, degl, dx_ref, dbg_ref):
        rows = [slice(ci * CH, (ci + 1) * CH) for ci in range(CPS)]
        _, vjp = jax.vjp(_chunk_prep_fn, [x_ref[r, :] for r in rows], [bg_ref[r, :] for r in rows])
        dxs, dbgs = vjp([(du[r, :], dw[r, :], dqe[r, :], dkd[r, :], dqk[:, r, :], degl[ci])
                         for ci, r in enumerate(rows)])
        for r, dx, dbg in zip(rows, dxs, dbgs):
            dx_ref[r, :] = dx
            dbg_ref[r, :] = dbg

    wide = pl.BlockSpec((CPS * CH, HW), lambda n: (n, 0))
    return pl.pallas_call(
        body, name="gdn_chunk_prep_bwd", grid=(NCK // CPS,),
        in_specs=[pl.BlockSpec((CPS * CH, 3 * HW), lambda n: (n, 0)), pl.BlockSpec((CPS * CH, LANE), lambda n: (n, 0)),
                  wide, wide, wide, wide, pl.BlockSpec((HEADS, CPS * CH, CH), lambda n: (0, n, 0)),
                  pl.BlockSpec((CPS, 1, HW), lambda n: (n, 0, 0))],
        out_specs=[pl.BlockSpec((CPS * CH, 3 * HW), lambda n: (n, 0)), pl.BlockSpec((CPS * CH, LANE), lambda n: (n, 0))],
        out_shape=[jax.ShapeDtypeStruct((T, 3 * HW), F32), jax.ShapeDtypeStruct((T, LANE), F32)],
        compiler_params=_cparams(("parallel",)),
    )(qkv, bg, *cots)


def _head_args(refs):
    u, w, qe, kd, qk, egl = refs
    sls = [slice(h * HD, (h + 1) * HD) for h in range(HEADS)]
    return ([u[:, sl] for sl in sls], [w[:, sl].astype(F32) for sl in sls], [qe[:, sl].astype(F32) for sl in sls],
            [kd[:, sl].astype(F32) for sl in sls], [qk[h].astype(F32) for h in range(HEADS)],
            [egl[:, sl] for sl in sls])


def _chunk_scan_bwd(prep, s_hist, do, after=()):
    n_in = 8 + len(after)

    def body(*refs):
        u_r, w_r, qe_r, kd_r, qk_r, egl_r, sh_ref, do_ref = refs[:8]
        d_refs = refs[n_in:n_in + 6]
        ds_ref = refs[n_in + 6]

        @pl.when(pl.program_id(0) == 0)
        def _():
            ds_ref[...] = jnp.zeros_like(ds_ref)

        sls = [slice(h * HD, (h + 1) * HD) for h in range(HEADS)]
        ds = [ds_ref[h] for h in range(HEADS)]
        for ci in reversed(range(CPS)):
            r = slice(ci * CH, (ci + 1) * CH)
            args = _head_args((u_r.at[r, :], w_r.at[r, :], qe_r.at[r, :], kd_r.at[r, :], qk_r.at[:, r, :], egl_r.at[ci]))
            _, vjp = jax.vjp(_chunk_state_fn, *args, [sh_ref[h, ci] for h in range(HEADS)])
            du, dw, dqe, dkd, dqk, degl, ds = vjp(([do_ref[r, sl] for sl in sls], ds))
            for h, sl in enumerate(sls):
                for d_ref, val in zip(d_refs[:4], (du, dw, dqe, dkd)):
                    d_ref[r, sl] = val[h]
                d_refs[4][h, r, :] = dqk[h]
                d_refs[5][ci, :, sl] = degl[h]
        for h in range(HEADS):
            ds_ref[h] = ds[h]

    rev = lambda n: NGROUP - 1 - n
    return pl.pallas_call(
        body, name="gdn_scan_bwd", grid=(NGROUP,),
        in_specs=_chunk_group_specs(rev) + [pl.BlockSpec((HEADS, CPS, HD, HD), lambda n: (0, rev(n), 0, 0)),
                                            pl.BlockSpec((CPS * CH, HW), lambda n: (rev(n), 0))] + [ANY] * len(after),
        out_specs=_chunk_group_specs(rev), out_shape=_chunk_prep_shapes((F32,) * 6),
        scratch_shapes=[pltpu.VMEM((HEADS, HD, HD), F32)], compiler_params=_cparams(("arbitrary",)),
    )(*prep, s_hist, do, *after)


MIX_ROWS = 4 * GCH


def _mix_rows(o, z, ur, vr, *params):
    chunks = [slice(c * GCH, (c + 1) * GCH) for c in range(MIX_ROWS // GCH)]
    return jnp.concatenate([_mix_fn(o[r], z[r], ur[r], vr[r], *params) for r in chunks], axis=0)


def _mix_specs():
    pc = lambda c: pl.BlockSpec((MIX_ROWS, HW), lambda i: (i, c))
    return [pl.BlockSpec((MIX_ROWS, HW), lambda i: (i, 0)), pc(3), pc(4), pc(5), _whole((1, HD)), _whole((1, HW)),
            _whole((1, HW)), _whole((HEADS, GCH, GCH)), _whole((GCH, LANE))]


def _mix(o, proj, ong, lng, lnb, ws, bst, after=()):
    def body(o_ref, z_ref, u_ref, v_ref, ong_ref, lng_ref, lnb_ref, ws_ref, bs_ref, *rest):
        rest[-1][...] = _mix_rows(o_ref[...], z_ref[...], u_ref[...], v_ref[...], ong_ref[...], lng_ref[...],
                                  lnb_ref[...], ws_ref[...], bs_ref[...]).astype(BF16)

    return pl.pallas_call(
        body, name="mix", grid=(T // MIX_ROWS,), in_specs=_mix_specs() + [ANY] * len(after),
        out_specs=pl.BlockSpec((MIX_ROWS, D), lambda i: (i, 0)), out_shape=jax.ShapeDtypeStruct((T, D), BF16),
        compiler_params=_cparams(("parallel",)),
    )(o, proj, proj, proj, ong, lng, lnb, ws, bst, *after)


def _mix_bwd(o, proj, ong, lng, lnb, ws, bst, dmix):
    def body(o_ref, z_ref, u_ref, v_ref, ong_ref, lng_ref, lnb_ref, ws_ref, bs_ref, dm_ref,
             do_ref, dzuv_ref, dong_ref, dlng_ref, dlnb_ref, dws_ref, dbs_ref):
        _, vjp = jax.vjp(_mix_rows, o_ref[...], z_ref[...], u_ref[...], v_ref[...], ong_ref[...], lng_ref[...],
                         lnb_ref[...], ws_ref[...], bs_ref[...])
        do, dz, du, dv, dong, dlng, dlnb, dws, dbs = vjp(dm_ref[...])
        do_ref[...] = do
        dzuv_ref[:, 0:HW] = dz.astype(BF16)
        dzuv_ref[:, HW:2 * HW] = du.astype(BF16)
        dzuv_ref[:, 2 * HW:3 * HW] = dv.astype(BF16)
        acc = [(dong_ref, dong), (dlng_ref, dlng), (dlnb_ref, dlnb), (dws_ref, dws), (dbs_ref, dbs)]

        @pl.when(pl.program_id(0) == 0)
        def _():
            for r, val in acc:
                r[...] = val

        @pl.when(pl.program_id(0) > 0)
        def _():
            for r, val in acc:
                r[...] += val

    shp = lambda *s: jax.ShapeDtypeStruct(s, F32)
    return pl.pallas_call(
        body, name="mix_bwd", grid=(T // MIX_ROWS,),
        in_specs=_mix_specs() + [pl.BlockSpec((MIX_ROWS, D), lambda i: (i, 0))],
        out_specs=[pl.BlockSpec((MIX_ROWS, HW), lambda i: (i, 0)), pl.BlockSpec((MIX_ROWS, 3 * HW), lambda i: (i, 1)),
                   _whole((1, HD)), _whole((1, HW)), _whole((1, HW)), _whole((HEADS, GCH, GCH)), _whole((GCH, LANE))],
        out_shape=[shp(T, HW), jax.ShapeDtypeStruct((T, NP), BF16), shp(1, HD), shp(1, HW), shp(1, HW),
                   shp(HEADS, GCH, GCH), shp(GCH, LANE)],
        compiler_params=_cparams(("arbitrary",)),
    )(o, proj, proj, proj, ong, lng, lnb, ws, bst, dmix)


def _swiglu_epilogue(accs, _):
    gate, up = accs
    return [gate, up, _silu(gate) * up]


def _swiglu_bwd_epilogue(accs, extras):
    dact = accs[0]
    gate, up = (e.astype(F32) for e in extras)
    sg = _sigmoid(gate)
    return [dact * up * (sg * (1.0 + gate * (1.0 - sg))), dact * (gate * sg)]


def _layer_fwd(h, p):
    hn = p.pop("hn") if "hn" in p else _rmsnorm("rms_mix", h, p["norm_mix"])
    proj = _mm("in_proj", "nn", hn[None], [p["w_in"][None]], tm=1024, tn=640, tk=D, sub_m=2)[0][0]
    qkv = _prep(proj, p["conv_w"])
    bg = _gates(proj, p["a_log"], p["dt_bias"])
    prep, o, s_hist = _delta_rule(qkv, bg)
    mix = _mix(o, proj, p["o_norm_g"], p["ln_v_g"], p["ln_v_b"], p["w_s"], p["bst"],
               p.pop("before_mix")(o) if "before_mix" in p else ())
    if "late" in p:
        p.update(p.pop("late")(mix))
    h1 = _mm("out_proj", "nn", mix[None], [p["w_out"].reshape(1, D, D)], tm=T, tn=512, tk=D, resid=h[None],
             sub_m=4)[0][0]
    h2n = _rmsnorm("rms_ffn", h1, p["norm_ffn"])
    gate, up, act = _mm("ffn_in", "nt", h2n[None], [p["w_gate"], p["w_up"]], tm=1024, tn=FF_SH, tk=D,
                        out_dtypes=(BF16, BF16, BF16), epilogue=_swiglu_epilogue, sub_m=4)
    then = p.pop("before_ffn_out")(act) if "before_ffn_out" in p else ()
    h2 = _mm("ffn_out", "nn", act, [p["w_down"]], tm=1024, tn=512, tk=FF_SH, reduce_g=True, fold_g=True,
             resid=h1[None], sub_m=2, after=then)[0][0]
    saved = dict(h=h, hn=hn, proj=proj, qkv=qkv, bg=bg, prep=prep, o=o, s_hist=s_hist, mix=mix, h1=h1, h2n=h2n,
                 gate=gate, up=up, act=act)
    return h2, saved


def _layer_bwd_ffn(dh2, dh2b, p, s, after=()):
    dh2b = dh2b[None]
    dgate, dup = _mm("ffn_out_bwd", "nt", dh2b, [p["w_down"]], tm=1024, tn=FF_SH, tk=D, out_dtypes=(BF16, BF16),
                     extras=(s["gate"], s["up"]), epilogue=_swiglu_bwd_epilogue, after=after, sub_m=4)
    dh2n = _mm("ffn_gate_bwd", "nn", dgate, [p["w_gate"]], tm=1024, tn=512, tk=FF_SH, reduce_g=True, fold_g=True,
               sub_m=2)[0]
    dh2n = _mm("ffn_up_bwd", "nn", dup, [p["w_up"]], tm=1024, tn=512, tk=FF_SH, reduce_g=True, fold_g=True,
               resid=dh2n, sub_m=2)[0][0]
    dh1, dh1b, d_norm_ffn = _rmsnorm_bwd("rms_ffn_bwd", dh2n, s["h1"], p["norm_ffn"], dh2)
    d_w_down = _mm("ffn_wdown_grad", "tn", s["act"], [dh2b], tm=FF_SH, tn=512, tk=T)[0]
    d_w_gate = _mm("ffn_wgate_grad", "tn", dgate, [s["h2n"][None]], tm=FF_SH, tn=512, tk=T)[0]
    d_w_up = _mm("ffn_wup_grad", "tn", dup, [s["h2n"][None]], tm=FF_SH, tn=512, tk=T)[0]
    return dh1, dh1b, dict(norm_ffn=d_norm_ffn, w_gate=d_w_gate, w_up=d_w_up, w_down=d_w_down)


def _layer_bwd_mixer(dh1, dh1b, p, s, after=(), midway=None, late=None):
    dh1b = dh1b[None]
    dmix = _mm("out_proj_bwd", "nt", dh1b, [p["w_out"].reshape(1, D, D)], tm=T, tn=512, tk=D, after=after,
               sub_m=4)[0][0]
    d_w_out = _mm("out_proj_wgrad", "tn", s["mix"][None], [dh1b], tm=1024, tn=512, tk=T)[0][0]
    do, dproj, d_ong, d_lng, d_lnb, d_ws, d_bst = _mix_bwd(
        s["o"], s["proj"], p["o_norm_g"], p["ln_v_g"], p["ln_v_b"], p["w_s"], p["bst"], dmix)
    then = midway(do) if midway is not None else ()
    dqkv, dbg = _chunk_prep_bwd(s["qkv"], s["bg"], _chunk_scan_bwd(s["prep"], s["s_hist"], do, then))
    dproj, d_conv = _prep_bwd(s["proj"], p["conv_w"], dqkv, dproj)
    dproj, d_a_log, d_dt_bias = _gates_bwd(s["proj"], p["a_log"], p["dt_bias"], dbg, dproj)
    dproj = dproj[None]
    d_w_in = _mm("in_proj_wgrad", "tn", s["hn"][None], [dproj], tm=512, tn=640, tk=T)[0]
    last = late(dict(w_in=d_w_in, w_out=d_w_out)) if late is not None else ()
    dhn = _mm("in_proj_bwd", "nt", dproj, [p["w_in"][None]], tm=1024, tn=512, tk=NP, after=last,
              sub_m=2)[0][0]
    dh, dhb, d_norm_mix = _rmsnorm_bwd("rms_mix_bwd", dhn, s["h"], p["norm_mix"], dh1)
    grads = dict(norm_mix=d_norm_mix, w_in=d_w_in, conv_w=d_conv, a_log=d_a_log, dt_bias=d_dt_bias, o_norm_g=d_ong,
                 ln_v_g=d_lng, ln_v_b=d_lnb, w_s=d_ws, bst=d_bst, w_out=d_w_out)
    return dh, dhb, grads


def _lanes(v, off=0):
    return jnp.zeros((1, LANE), F32).at[0, off:off + v.shape[0]].set(v)


def _w_in_pieces():
    regions = [(0, 2048, 0), (2048, 2056, BA_OFF), (2056, IN_DIM, 2048)]
    sh = IN_DIM // NCHIP
    out = []
    for j in range(NCHIP):
        for lo, hi, at in regions:
            a, b = max(lo, j * sh), min(hi, (j + 1) * sh)
            if a < b:
                out.append((j, a - j * sh, at + a - lo, b - a))
    return out


W_IN_PIECES = _w_in_pieces()
WT = 256


def _assemble_w_in(gathered, own, place):
    def body(place_ref, g_ref, own_ref, o_ref):
        o_ref[:, IN_DIM:] = jnp.zeros((WT, NP - IN_DIM), BF16)
        mine = own_ref[...]
        for j, src, dst, width in W_IN_PIECES:
            val = jnp.where(place_ref[0] == j, mine[:, src:src + width], g_ref[j, :, src:src + width])
            o_ref[:, dst:dst + width] = val

    sh = IN_DIM // NCHIP
    return pl.pallas_call(
        body, name="assemble_w_in",
        grid_spec=pltpu.PrefetchScalarGridSpec(
            num_scalar_prefetch=1, grid=(D // WT,),
            in_specs=[pl.BlockSpec((NCHIP, WT, sh), lambda i, place_ref: (0, i, 0)),
                      pl.BlockSpec((WT, sh), lambda i, place_ref: (i, 0))],
            out_specs=pl.BlockSpec((WT, NP), lambda i, place_ref: (i, 0))),
        out_shape=jax.ShapeDtypeStruct((D, NP), BF16), compiler_params=_cparams(("parallel",)),
    )(place, gathered, own)


def _layer_params(l, big, small):
    return dict(
        {k: v for k, v in big.items() if k != "conv_w"},
        conv_w=jnp.concatenate([big["conv_w"][j, l] for j in range(NCHIP)], axis=1),
        norm_mix=small["norm_mix"][l][None], norm_ffn=small["norm_ffn"][l][None],
        a_log=_lanes(small["a_log"][l], HEADS), dt_bias=_lanes(small["dt_bias"][l], HEADS),
        o_norm_g=small["o_norm_g"][l][None], ln_v_g=small["ln_v_g"][l][None], ln_v_b=small["ln_v_b"][l][None],
        w_s=small["w_s"][l],
        bst=jnp.pad(small["b_s"][l].T, ((0, 0), (0, LANE - HEADS))),
    )


def _reference_layout(g):
    return dict(
        w_in=g["w_in"],
        w_out=g["w_out"].reshape(NCHIP, D // NCHIP, D),
        w_gate=g["w_gate"], w_up=g["w_up"], w_down=g["w_down"],
        conv_w=g["conv_w"], norm_mix=g["norm_mix"][0], norm_ffn=g["norm_ffn"][0],
        a_log=g["a_log"][0, HEADS:2 * HEADS], dt_bias=g["dt_bias"][0, HEADS:2 * HEADS],
        o_norm_g=g["o_norm_g"][0], ln_v_g=g["ln_v_g"][0], ln_v_b=g["ln_v_b"][0], w_s=g["w_s"],
        b_s=g["bst"][:, :HEADS].T,
    )


def _forward(x, tgt, layers, norm_final):
    h = x
    saved, params = [], []
    for p in layers:
        p = p(h) if callable(p) else p
        h, s = _layer_fwd(h, p)
        saved.append(s)
        params.append(p)
    return (saved, params) + tuple(_loss_head(h, norm_final, tgt))


def _local_step(x, tgt, layers, norm_final):
    saved, layers, loss, dh, dhb, d_norm_final = _forward(x, tgt, layers, norm_final)
    grads = [None] * DEPTH
    for l in reversed(range(DEPTH)):
        dh1, dh1b, g_ffn = _layer_bwd_ffn(dh, dhb, layers[l], saved[l])
        dh, dhb, g_mix = _layer_bwd_mixer(dh1, dh1b, layers[l], saved[l])
        grads[l] = {**g_ffn, **g_mix}
    return loss, dh, grads, d_norm_final


def _place():
    x, y, c = lax.axis_index("x"), lax.axis_index("y"), lax.axis_index("c")
    return x, y, c, [(1 - x, y), (x, 1 - y), (1 - x, 1 - y)]


def _remote(src, dst, send_sem, recv_sem, to):
    return pltpu.make_async_remote_copy(src_ref=src, dst_ref=dst, send_sem=send_sem, recv_sem=recv_sem,
                                        device_id=to, device_id_type=MESH)


def _comm_call(name, body, ins, out_shape, n_sems, aliases=None):
    return pl.pallas_call(
        body, name=name, in_specs=[ANY] * len(ins), out_specs=[ANY] * len(out_shape), out_shape=out_shape,
        scratch_shapes=[pltpu.SemaphoreType.DMA((n,)) for n in n_sems], input_output_aliases=aliases or {},
        compiler_params=pltpu.CompilerParams(has_side_effects=True),
    )(*ins)


def _half_rows(ref, of_c, dim):
    hr = ref.shape[dim] // 2
    return pl.ds(pl.multiple_of(of_c * hr, BF16_ROWS), hr)


def _gather_plan(whole):
    def plan(srcs, lands):
        x, y, c, others = _place()
        chip = 2 * x + y
        out = []
        for src, land, all_of_it in zip(srcs, lands, whole):
            for ox, oy in others:
                if all_of_it:
                    out.append((src, land.at[chip], (ox, oy, c)))
                else:
                    out.append((src.at[_half_rows(src, c, 0)], land.at[chip, _half_rows(src, c, 0)], (ox, oy, c)))
        return out
    return plan


def _forward_halves(lands):
    n = len(lands)

    def body(*refs):
        outs = refs[n:2 * n]
        send_s, recv_s = refs[2 * n:]
        x, y, c, others = _place()
        sibling = (x, y, 1 - c)
        copies = []
        for a in range(n):
            for k, (ox, oy) in enumerate(others):
                mine = outs[a].at[2 * ox + oy, _half_rows(outs[a], c, 1)]
                copies.append(_remote(mine, mine, send_s.at[3 * a + k], recv_s.at[3 * a + k], sibling))
        for cp in copies:
            cp.start()
        for a in range(n):
            for k, (ox, oy) in enumerate(others):
                landed = outs[a].at[2 * ox + oy, _half_rows(outs[a], 1 - c, 1)]
                _remote(landed, landed, send_s.at[3 * a + k], recv_s.at[3 * a + k], sibling).wait_recv()
        for cp in copies:
            cp.wait_send()

    out_shape = [jax.ShapeDtypeStruct(g.shape, g.dtype) for g in lands]
    return _comm_call("forward_halves", body, lands, out_shape, [3 * n, 3 * n], aliases={a: a for a in range(n)})


def _forward_refs(bufs, incoming):
    x, y, c, others = _place()
    return (x, y, 1 - c), [b.at[2 * ox + oy, _half_rows(b, 1 - c if incoming else c, 1)]
                           for b in bufs for ox, oy in others]


def _forward_start(name, bufs, after):
    n = len(bufs)
    bufs = [pltpu.with_memory_space_constraint(b, pltpu.HBM) for b in bufs]

    def body(*refs):
        send_s, recv_s = refs[n + len(after)], refs[n + len(after) + 1]
        sibling, mine = _forward_refs(refs[:n], incoming=False)
        for i, ref in enumerate(mine):
            _remote(ref, ref, send_s.at[i], recv_s.at[i], sibling).start()
        refs[-1][...] = jnp.zeros_like(refs[-1])

    out = pl.pallas_call(
        body, name=name, in_specs=[HBM_SPEC] * n + [ANY] * len(after),
        out_specs=[SEM_SPEC, SEM_SPEC] + [HBM_SPEC] * n + [pl.BlockSpec(memory_space=pltpu.VMEM)],
        out_shape=[pltpu.SemaphoreType.DMA((3 * n,)), pltpu.SemaphoreType.DMA((3 * n,))]
        + [pltpu.HBM(b.shape, b.dtype) for b in bufs] + [jax.ShapeDtypeStruct((F32_ROWS, LANE), F32)],
        input_output_aliases={i: 2 + i for i in range(n)},
        compiler_params=pltpu.CompilerParams(has_side_effects=DATAFLOW),
    )(*bufs, *after)
    return dict(sems=out[:2], bufs=out[2:2 + n], token=out[-1])


def _forward_wait(name, started, after):
    n = len(started["bufs"])

    def body(*refs):
        send_s, recv_s = refs[n], refs[n + 1]
        sibling, mine = _forward_refs(refs[:n], incoming=False)
        _, theirs = _forward_refs(refs[:n], incoming=True)
        for i, (sent, landed) in enumerate(zip(mine, theirs)):
            _remote(sent, sent, send_s.at[i], recv_s.at[i], sibling).wait_send()
            _remote(landed, landed, send_s.at[i], recv_s.at[i], sibling).wait_recv()

    return pl.pallas_call(
        body, name=name, in_specs=[HBM_SPEC] * n + [SEM_SPEC, SEM_SPEC] + [ANY] * len(after),
        out_specs=[HBM_SPEC] * n, out_shape=[pltpu.HBM(b.shape, b.dtype) for b in started["bufs"]],
        input_output_aliases={i: i for i in range(n)},
        compiler_params=pltpu.CompilerParams(has_side_effects=DATAFLOW),
    )(*started["bufs"], *started["sems"], *after)


HBM_SPEC = pl.BlockSpec(memory_space=pltpu.HBM)
SEM_SPEC = pl.BlockSpec(memory_space=pltpu.SEMAPHORE)
DATAFLOW = pltpu.SideEffectType.DATAFLOW_SIDE_EFFECTING


def _exchange_plan(srcs, lands):
    x, y, c, _ = _place()
    plan = []
    for src, land in zip(srcs, lands):
        hr = src.shape[1] // 2
        plan.append((src.at[:, pl.ds(pl.multiple_of((1 - c) * hr, 8), hr)], land, (x, y, 1 - c)))
    return plan


def _scatter_plan(srcs, lands):
    x, y, c, others = _place()
    return [(src.at[2 * ox + oy], land.at[k], (ox, oy, c))
            for src, land in zip(srcs, lands) for k, (ox, oy) in enumerate(others)]


def _split_start(name, plan, srcs, land_shapes, n_copies, after=()):
    n = len(srcs)
    lands = [pltpu.with_memory_space_constraint(lax.empty(s.shape, s.dtype), pltpu.HBM) for s in land_shapes]
    srcs = [pltpu.with_memory_space_constraint(s, pltpu.HBM) for s in srcs]

    def body(*refs):
        send_s, recv_s = refs[2 * n + len(after)], refs[2 * n + len(after) + 1]
        for i, (src, dst, to) in enumerate(plan(refs[:n], refs[n:2 * n])):
            _remote(src, dst, send_s.at[i], recv_s.at[i], to).start()
        refs[-1][...] = jnp.zeros_like(refs[-1])

    thru = [pltpu.HBM(s.shape, s.dtype) for s in srcs + lands]
    out = pl.pallas_call(
        body, name=name, in_specs=[HBM_SPEC] * (2 * n) + [ANY] * len(after),
        out_specs=[SEM_SPEC, SEM_SPEC] + [HBM_SPEC] * (2 * n) + [pl.BlockSpec(memory_space=pltpu.VMEM)],
        out_shape=[pltpu.SemaphoreType.DMA((n_copies,)), pltpu.SemaphoreType.DMA((n_copies,))] + thru
        + [jax.ShapeDtypeStruct((F32_ROWS, LANE), F32)],
        input_output_aliases={i: 2 + i for i in range(2 * n)},
        compiler_params=pltpu.CompilerParams(has_side_effects=DATAFLOW),
    )(*srcs, *lands, *after)
    return dict(sems=out[:2], srcs=out[2:2 + n], lands=out[2 + n:2 + 2 * n], token=out[-1])


def _split_wait(name, plan, started, after):
    n = len(started["srcs"])
    after = list(after) if isinstance(after, (list, tuple)) else [after]

    def body(*refs):
        send_s, recv_s = refs[2 * n], refs[2 * n + 1]
        for i, (src, dst, to) in enumerate(plan(refs[:n], refs[n:2 * n])):
            cp = _remote(src, dst, send_s.at[i], recv_s.at[i], to)
            cp.wait_send()
            cp.wait_recv()

    arrs = list(started["srcs"]) + list(started["lands"])
    out = pl.pallas_call(
        body, name=name, in_specs=[HBM_SPEC] * (2 * n) + [SEM_SPEC, SEM_SPEC] + [ANY] * len(after),
        out_specs=[HBM_SPEC] * (2 * n), out_shape=[pltpu.HBM(s.shape, s.dtype) for s in arrs],
        input_output_aliases={i: i for i in range(2 * n)},
        compiler_params=pltpu.CompilerParams(has_side_effects=DATAFLOW),
    )(*arrs, *started["sems"], *after)
    return out[:n], out[n:]


def _join_halves(name, rs):
    n = len(rs)

    def body(*refs):
        outs = refs[n:2 * n]
        send_s, recv_s = refs[2 * n:]
        x, y, c, _ = _place()
        sibling = (x, y, 1 - c)

        def half(a, of_c):
            hr = outs[a].shape[1] // 2
            return outs[a].at[:, pl.ds(pl.multiple_of(of_c * hr, 8), hr)]

        copies = [_remote(half(a, c), half(a, c), send_s.at[a], recv_s.at[a], sibling) for a in range(n)]
        for cp in copies:
            cp.start()
        for a in range(n):
            landed = half(a, 1 - c)
            _remote(landed, landed, send_s.at[a], recv_s.at[a], sibling).wait_recv()
        for cp in copies:
            cp.wait_send()

    out_shape = [jax.ShapeDtypeStruct(r.shape, r.dtype) for r in rs]
    return _comm_call(name, body, rs, out_shape, [n, n], aliases={a: a for a in range(n)})


def _allreduce_small(buf, after=()):
    r = buf.shape[0]
    hr = r // 2

    def body(in_ref, *refs):
        out_ref, theirs, by_chip, send_s, recv_s = refs[len(after):]
        x, y, c, others = _place()
        chip = 2 * x + y
        sibling = (x, y, 1 - c)
        mine = pl.ds(pl.multiple_of(c * hr, F32_ROWS), hr)
        swap = _remote(in_ref, theirs, send_s.at[0], recv_s.at[0], sibling)
        swap.start()
        swap.wait()
        by_chip[chip] = in_ref[mine, :] + theirs[mine, :]
        sends = [_remote(by_chip.at[chip], by_chip.at[chip], send_s.at[1 + k], recv_s.at[1 + k], (ox, oy, c))
                 for k, (ox, oy) in enumerate(others)]
        for cp in sends:
            cp.start()
        for k, (ox, oy) in enumerate(others):
            landed = by_chip.at[2 * ox + oy]
            _remote(landed, landed, send_s.at[1 + k], recv_s.at[1 + k], (ox, oy, c)).wait_recv()
        for cp in sends:
            cp.wait_send()
        out_ref[mine, :] = (by_chip[0] + by_chip[1]) + (by_chip[2] + by_chip[3])
        back = _remote(out_ref.at[mine], out_ref.at[mine], send_s.at[NCHIP], recv_s.at[NCHIP], sibling)
        back.start()
        other = out_ref.at[pl.ds(pl.multiple_of((1 - c) * hr, F32_ROWS), hr)]
        _remote(other, other, send_s.at[NCHIP], recv_s.at[NCHIP], sibling).wait_recv()
        back.wait_send()

    vm = pl.BlockSpec(memory_space=pltpu.VMEM)
    return pl.pallas_call(
        body, name="allreduce_small", in_specs=[vm] + [ANY] * len(after), out_specs=vm,
        out_shape=jax.ShapeDtypeStruct((r, LANE), F32),
        scratch_shapes=[pltpu.VMEM((r, LANE), F32), pltpu.VMEM((NCHIP, hr, LANE), F32),
                        pltpu.SemaphoreType.DMA((NCHIP + 1,)), pltpu.SemaphoreType.DMA((NCHIP + 1,))],
        compiler_params=pltpu.CompilerParams(has_side_effects=True, vmem_limit_bytes=VMEM_LIMIT),
    )(buf, *after)


MAX_ROW_TILE = 512
BF16_ROWS = 16


def _row_tile(rows):
    for t in range(min(rows, MAX_ROW_TILE) // BF16_ROWS * BF16_ROWS, 0, -BF16_ROWS):
        if rows % t == 0:
            return t
    raise ValueError(rows)


def _sum_halves(g, theirs, c_arr):
    nch, rows, cols = g.shape
    hr = rows // 2
    tr = _row_tile(hr)

    def body(c_ref, g_ref, t_ref, o_ref, ob_ref):
        s = g_ref[...] + t_ref[...]
        o_ref[...] = s
        ob_ref[...] = s.astype(BF16)

    blk = pl.BlockSpec((None, tr, cols), lambda j, i, c_ref: (j, i, 0))
    return pl.pallas_call(
        body, name="sum_halves",
        grid_spec=pltpu.PrefetchScalarGridSpec(
            num_scalar_prefetch=1, grid=(nch, hr // tr),
            in_specs=[pl.BlockSpec((None, None, tr, cols), lambda j, i, c_ref: (j, c_ref[0], i, 0)), blk],
            out_specs=[blk, blk]),
        out_shape=[jax.ShapeDtypeStruct((nch, hr, cols), F32), jax.ShapeDtypeStruct((nch, hr, cols), BF16)],
        compiler_params=_cparams(("parallel", "parallel")),
    )(c_arr, g.reshape(nch, 2, hr, cols), theirs)


def _sum_halves_w_in(g, theirs, c_arr):
    hr = D // 2
    sh = IN_DIM // NCHIP

    def body(c_ref, g_ref, t_ref, o_ref, ob_ref):
        s = g_ref[...] + t_ref[...]
        for j, dst, src, width in W_IN_PIECES:
            o_ref[j, :, dst:dst + width] = s[:, src:src + width]
            ob_ref[j, :, dst:dst + width] = s[:, src:src + width].astype(BF16)

    out = pl.BlockSpec((NCHIP, WT, sh), lambda i, c_ref: (0, i, 0))
    return pl.pallas_call(
        body, name="sum_halves_w_in",
        grid_spec=pltpu.PrefetchScalarGridSpec(
            num_scalar_prefetch=1, grid=(hr // WT,),
            in_specs=[pl.BlockSpec((None, WT, NP), lambda i, c_ref: (c_ref[0], i, 0)),
                      pl.BlockSpec((None, WT, NP), lambda i, c_ref: (0, i, 0))],
            out_specs=[out, out]),
        out_shape=[jax.ShapeDtypeStruct((NCHIP, hr, sh), F32), jax.ShapeDtypeStruct((NCHIP, hr, sh), BF16)],
        compiler_params=_cparams(("parallel",)),
    )(c_arr, g.reshape(2, hr, NP), theirs)


def _sum_chips(p, q, place, l, into=None, after=()):
    extra = ([into] if into is not None else []) + list(after)
    _, rows, cols = p.shape
    tr = _row_tile(rows)
    steps = rows // tr

    def body(place_ref, p_ref, q0, q1, q2, *rest):
        rest[-1][...] = ((p_ref[...] + q0[...].astype(F32)) + q1[...].astype(F32)) + q2[...].astype(F32)

    qs = lambda k: pl.BlockSpec((None, tr, cols), lambda i, place_ref: (k, i, 0))
    return pl.pallas_call(
        body, name="sum_chips",
        grid_spec=pltpu.PrefetchScalarGridSpec(
            num_scalar_prefetch=1, grid=(steps,),
            in_specs=[pl.BlockSpec((None, tr, cols), lambda i, place_ref: (place_ref[0], i, 0)), qs(0), qs(1), qs(2)]
            + [ANY] * len(extra),
            out_specs=pl.BlockSpec((None, tr, cols), lambda i, place_ref: (l, place_ref[1] * steps + i, 0))),
        out_shape=jax.ShapeDtypeStruct((DEPTH, 2 * rows, cols), F32),
        input_output_aliases={5: 0} if into is not None else {},
        compiler_params=_cparams(("parallel",)),
    )(place, p, q, q, q, *extra)


def _adamw_fn(w, g, m, v):
    nm = ADAM_B1 * m + (1.0 - ADAM_B1) * g
    nv = ADAM_B2 * v + (1.0 - ADAM_B2) * jnp.square(g)
    m_hat = nm / (1.0 - ADAM_B1 ** ADAM_STEP)
    v_hat = nv / (1.0 - ADAM_B2 ** ADAM_STEP)
    return -ADAM_LR * (m_hat / (jnp.sqrt(v_hat) + ADAM_EPS) + ADAM_WD * w), nm, nv


def _adamw(w, g, m, v):
    layers, rows, cols = w.shape
    tr = _row_tile(rows)

    def body(w_ref, g_ref, m_ref, v_ref, d_ref, nm_ref, nv_ref):
        d_ref[...], nm_ref[...], nv_ref[...] = _adamw_fn(w_ref[...], g_ref[...], m_ref[...], v_ref[...])

    blk = pl.BlockSpec((None, tr, cols), lambda l, i: (l, i, 0))
    return pl.pallas_call(
        body, name="adamw", grid=(layers, rows // tr), in_specs=[blk] * 4, out_specs=[blk] * 3,
        out_shape=[jax.ShapeDtypeStruct(w.shape, F32)] * 3, compiler_params=_cparams(("parallel", "parallel")),
    )(w, g, m, v)


def _adamw_small(ws, gs, ms, vs):
    n = len(ws)

    def body(*refs):
        for i in range(n):
            w_ref, g_ref, m_ref, v_ref, d_ref, nm_ref, nv_ref = (refs[k * n + i] for k in range(7))
            d_ref[...], nm_ref[...], nv_ref[...] = _adamw_fn(w_ref[...], g_ref[...], m_ref[...], v_ref[...])

    vm = pl.BlockSpec(memory_space=pltpu.VMEM)
    out = pl.pallas_call(
        body, name="adamw_small", in_specs=[vm] * (4 * n), out_specs=[vm] * (3 * n),
        out_shape=[jax.ShapeDtypeStruct(a.shape, F32) for a in list(ws) * 3],
        compiler_params=pltpu.CompilerParams(vmem_limit_bytes=VMEM_LIMIT),
    )(*ws, *gs, *ms, *vs)
    return out[:n], out[n:2 * n], out[2 * n:]


BIG = ("w_in", "w_out", "w_gate", "w_up", "w_down")
SMALL = ("norm_mix", "a_log", "dt_bias", "o_norm_g", "ln_v_g", "ln_v_b", "w_s", "b_s", "norm_ffn", "norm_final")
ORDER = ("norm_mix", "w_in", "conv_w", "a_log", "dt_bias", "o_norm_g", "ln_v_g", "ln_v_b", "w_s", "b_s", "w_out",
         "norm_ffn", "w_gate", "w_up", "w_down", "norm_final")


F32_ROWS = 8
PACK_ROWS = 128


def _lane_rows(size):
    return -(-size // (F32_ROWS * LANE)) * F32_ROWS


def _pack(arrs):
    parts = [jnp.pad(a.reshape(-1), (0, _lane_rows(a.size) * LANE - a.size)).reshape(-1, LANE) for a in arrs]
    rows = sum(p.shape[0] for p in parts)
    if rows % PACK_ROWS:
        parts.append(jnp.zeros((-rows % PACK_ROWS, LANE), F32))
    return jnp.concatenate(parts, axis=0)


def _unpack(buf, like):
    out, row = [], 0
    for a in like:
        n = _lane_rows(a.size)
        out.append(buf[row:row + n].reshape(-1)[:a.size].reshape(a.shape))
        row += n
    return out


def kernel(x, norm_mix, w_in, conv_w, a_log, dt_bias, o_norm_g, ln_v_g, ln_v_b, w_s, b_s, w_out, norm_ffn, w_gate, w_up, w_down, norm_final, loss_target, m_norm_mix, m_w_in, m_conv_w, m_a_log, m_dt_bias, m_o_norm_g, m_ln_v_g, m_ln_v_b, m_w_s, m_b_s, m_w_out, m_norm_ffn, m_w_gate, m_w_up, m_w_down, m_norm_final, v_norm_mix, v_w_in, v_conv_w, v_a_log, v_dt_bias, v_o_norm_g, v_ln_v_g, v_ln_v_b, v_w_s, v_b_s, v_w_out, v_norm_ffn, v_w_gate, v_w_up, v_w_down, v_norm_final):
    w = dict(norm_mix=norm_mix, w_in=w_in, conv_w=conv_w, a_log=a_log, dt_bias=dt_bias, o_norm_g=o_norm_g,
             ln_v_g=ln_v_g, ln_v_b=ln_v_b, w_s=w_s, b_s=b_s, w_out=w_out, norm_ffn=norm_ffn, w_gate=w_gate, w_up=w_up,
             w_down=w_down, norm_final=norm_final)
    m = dict(norm_mix=m_norm_mix, w_in=m_w_in, conv_w=m_conv_w, a_log=m_a_log, dt_bias=m_dt_bias, o_norm_g=m_o_norm_g,
             ln_v_g=m_ln_v_g, ln_v_b=m_ln_v_b, w_s=m_w_s, b_s=m_b_s, w_out=m_w_out, norm_ffn=m_norm_ffn,
             w_gate=m_w_gate, w_up=m_w_up, w_down=m_w_down, norm_final=m_norm_final)
    v = dict(norm_mix=v_norm_mix, w_in=v_w_in, conv_w=v_conv_w, a_log=v_a_log, dt_bias=v_dt_bias, o_norm_g=v_o_norm_g,
             ln_v_g=v_ln_v_g, ln_v_b=v_ln_v_b, w_s=v_w_s, b_s=v_b_s, w_out=v_w_out, norm_ffn=v_norm_ffn,
             w_gate=v_w_gate, w_up=v_w_up, w_down=v_w_down, norm_final=v_norm_final)
    chip = 2 * lax.axis_index("x") + lax.axis_index("y")
    place = jnp.stack([chip, lax.axis_index("c")]).astype(jnp.int32)
    c_arr = place[1:]

    def kernel_view(n, a):
        return jnp.swapaxes(a, 1, 2) if n in ("w_gate", "w_up") else a

    own = {n: [kernel_view(n, w[n])[l].astype(BF16) for l in range(DEPTH)] for n in BIG}
    by_chip = lambda a: jax.ShapeDtypeStruct((NCHIP,) + a.shape, a.dtype)

    def start(name, srcs, whole, after=()):
        return _split_start(name, _gather_plan(whole), srcs, [by_chip(a) for a in srcs], 3 * len(srcs), after)

    def finish(name, started, whole, after):
        srcs, lands = _split_wait(name, _gather_plan(whole), started, after)
        passed = iter(_forward_halves([g for g, all_of_it in zip(lands, whole) if not all_of_it]))
        lands = [g if all_of_it else next(passed) for g, all_of_it in zip(lands, whole)]
        return srcs, [lax.dynamic_update_index_in_dim(g, o, chip, 0) for g, o in zip(lands, srcs)]

    ffn = BIG[1:]
    first = start("gather_first_start", [own["w_in"][0], conv_w], [False, True])
    early = start("gather_early_start", [own[n][0] for n in ffn], [False] * len(ffn), [first["token"]])
    mid = start("gather_mid_start", [own["w_in"][1]], [False], [early["token"]])
    later = start("gather_later_start", [own[n][1] for n in ffn], [False] * len(ffn), [mid["token"]])
    hn = _rmsnorm("rms_mix", x[0], norm_mix[0][None])
    (own_w_in, _), (w_in_by_chip, conv_by_chip) = finish("gather_first_wait", first, [False, True], [later["token"], hn])

    passing = {}

    def pass_on(tag, started, n):
        def at(after):
            srcs, lands = _split_wait(f"gather_{tag}_wait", _gather_plan([False] * n), started, after)
            passing[tag] = srcs, _forward_start(f"forward_{tag}_start", lands, ())
            return [passing[tag][1]["token"]]
        return at

    def passed_on(tag, after):
        srcs, fwd = passing[tag]
        lands = _forward_wait(f"forward_{tag}_wait", fwd, [after])
        return srcs, [lax.dynamic_update_index_in_dim(g, o, chip, 0) for g, o in zip(lands, srcs)]

    def late(tag):
        return lambda after: dict(zip(ffn, passed_on(tag, after)[1]))

    layer0 = _layer_params(0, dict(
        hn=hn, w_in=_assemble_w_in(w_in_by_chip, own_w_in, place), conv_w=conv_by_chip, late=late("early"),
        before_mix=pass_on("early", early, len(ffn)), before_ffn_out=pass_on("mid", mid, 1)), w)

    def layer1(after):
        (own_w_in1,), (w_in1_by_chip,) = passed_on("mid", after)
        return _layer_params(1, dict(w_in=_assemble_w_in(w_in1_by_chip, own_w_in1, place), conv_w=conv_by_chip,
                                     late=late("later"), before_mix=pass_on("later", later, len(ffn))), w)

    saved, layers, loss_lanes, dh, dhb, d_norm_final = _forward(x[0], loss_target[0], [layer0, layer1],
                                                                 norm_final[None])

    sums, arrived = {}, {}

    def exchange_start(tag, l, names, grads, after=()):
        mine = [grads[n] for n in names]
        shapes = [jax.ShapeDtypeStruct((g.shape[0], g.shape[1] // 2, g.shape[2]), F32) for g in mine]
        return tag, l, names, _split_start(f"exchange_{tag}_start", _exchange_plan, mine, shapes, len(mine), after)

    def add_halves(l, names, mine, theirs):
        for n, g, t in zip(names, mine, theirs):
            sums[l, n] = (_sum_halves_w_in if n == "w_in" else _sum_halves)(g, t, c_arr)

    def exchange_wait(handle, after):
        tag, l, names, started = handle
        add_halves(l, names, *_split_wait(f"exchange_{tag}_wait", _exchange_plan, started, after))

    def scatter_start(tag, l, names, after=()):
        partial = [sums[l, n][1] for n in names]
        shapes = [jax.ShapeDtypeStruct((3,) + p.shape[1:], p.dtype) for p in partial]
        return tag, l, names, _split_start(f"scatter_{tag}_start", _scatter_plan, partial, shapes, 3 * len(names), after)

    def scatter_wait(handle, after):
        tag, l, names, started = handle
        for n, q in zip(names, _split_wait(f"scatter_{tag}_wait", _scatter_plan, started, after)[1]):
            arrived[l, n] = q

    last = DEPTH - 1
    swiglu = BIG[2:]
    dh1, dh1b, g_ffn = _layer_bwd_ffn(dh, dhb, layers[last], saved[last])
    dh, dhb, g_mix = _layer_bwd_mixer(dh1, dh1b, layers[last], saved[last])
    gl = [None, _reference_layout({**g_ffn, **g_mix})]
    ex_last = exchange_start("last", last, BIG, gl[last])
    dh1, dh1b, g_ffn = _layer_bwd_ffn(dh, dhb, layers[0], saved[0], after=[ex_last[-1]["token"]])
    exchange_wait(ex_last, dh1)
    sc_last = scatter_start("last", last, BIG)
    ex_ffn = exchange_start("swiglu", 0, swiglu, g_ffn, [sc_last[-1]["token"]])
    sc_ffn = []

    def midway(do):
        exchange_wait(ex_ffn, do)
        sc_ffn.append(scatter_start("swiglu", 0, swiglu))
        return [sc_ffn[0][-1]["token"]]

    ex_rest = []

    def late(grads):
        rest_grads = dict(w_in=grads["w_in"], w_out=grads["w_out"].reshape(NCHIP, D // NCHIP, D))
        ex_rest.append(exchange_start("rest", 0, BIG[:2], rest_grads))
        return [ex_rest[0][-1]["token"]]

    dx, _, g_mix = _layer_bwd_mixer(dh1, dh1b, layers[0], saved[0], after=[ex_ffn[-1]["token"]], midway=midway,
                                    late=late)
    scatter_wait(sc_last, dx)
    scatter_wait(sc_ffn[0], dx)
    gl[0] = _reference_layout({**g_ffn, **g_mix})

    small_g = [jnp.stack([gl[l][n] for l in range(DEPTH)]) for n in SMALL[:-1]] + [d_norm_final[0]]
    conv_g = jnp.stack([gl[l]["conv_w"] for l in range(DEPTH)])
    summed = small_g + [conv_g, loss_lanes[0, :1]]
    total = _allreduce_small(_pack(summed))
    exchange_wait(ex_rest[0], total)
    sc_rest = scatter_start("rest", 0, BIG[:2])

    travelling = [sc_rest[-1]["token"]]
    reduced, g_out, delta, new_m, new_v = {}, {}, {}, {}, {}

    done = []

    def adamw_large(names, joined):
        for n, g in zip(names, joined):
            res = _adamw(kernel_view(n, w[n]), g, kernel_view(n, m[n]), kernel_view(n, v[n]))
            done.append(res[2])
            g_out[n], delta[n], new_m[n], new_v[n] = (kernel_view(n, a) for a in (g,) + tuple(res))

    for n in BIG:
        for l in (range(DEPTH) if n in swiglu else [last]):
            reduced[n] = _sum_chips(sums[l, n][0], arrived[l, n], place, l, into=reduced.get(n), after=travelling)
    adamw_large(swiglu, _join_halves("join_swiglu", [reduced[n] for n in swiglu]))
    scatter_wait(sc_rest, done + [reduced[n] for n in BIG[:2]])
    for n in BIG[:2]:
        reduced[n] = _sum_chips(sums[0, n][0], arrived[0, n], place, 0, into=reduced[n])
    adamw_large(BIG[:2], _join_halves("join_rest", [reduced[n] for n in BIG[:2]]))
    *small_r, conv_r, loss = _unpack(total, summed)
    g_out.update(zip(SMALL, small_r))
    g_out["conv_w"] = lax.dynamic_slice_in_dim(conv_r, chip * conv_w.shape[2], conv_w.shape[2], axis=2)

    rest = SMALL + ("conv_w",)
    rows_of = lambda a: a.reshape(1, -1) if a.ndim == 1 else a
    results = _adamw_small(*[[rows_of(src[n]) for n in rest] for src in (w, g_out, m, v)])
    for dst, arrs in zip((delta, new_m, new_v), results):
        dst.update({n: a.reshape(w[n].shape) for n, a in zip(rest, arrs)})

    return (loss[0], dx[None], *[g_out[n] for n in ORDER], *[delta[n] for n in ORDER], *[new_m[n] for n in ORDER],
            *[new_v[n] for n in ORDER])
```

```python
import functools

import jax
import jax.numpy as jnp
from jax import lax
from jax.experimental import pallas as pl
from jax.experimental.pallas import tpu as pltpu

F32 = jnp.float32
BF16 = jnp.bfloat16
MESH = pl.DeviceIdType.MESH
ANY = pl.BlockSpec(memory_space=pl.ANY)
HIGHEST = lax.Precision.HIGHEST

T = 2048
D = 1024
DEPTH = 2
NCHIP = 4
HEADS = 4
HD = 128
HW = HEADS * HD
CH = 64
GCH = 128
IN_DIM = 3080
NP = 3200
BA_OFF = 3072
FF_SH = 704
EPS = 1e-6
LANE = 128
VMEM_LIMIT = 56 * 1024 * 1024

ADAM_LR = 0.001
ADAM_B1 = 0.9
ADAM_B2 = 0.999
ADAM_EPS = 1e-08
ADAM_WD = 0.01
ADAM_STEP = 10


def _cparams(sem=None):
    return pltpu.CompilerParams(dimension_semantics=sem, vmem_limit_bytes=VMEM_LIMIT)


_DIMS = {"nn": (((1,), (0,)), ((), ())), "nt": (((1,), (1,)), ((), ())), "tn": (((0,), (0,)), ((), ()))}


def _mm(name, mode, a, bs, *, tm, tn, tk, out_dtypes=(F32,), reduce_g=False, resid=None, extras=(), epilogue=None,
        after=(), fold_g=False, sub_m=1):
    assert sub_m == 1 or (mode != "tn" and tm % (8 * sub_m) == 0), (name, sub_m)
    nb = len(bs)
    ga = a.shape[0]
    gbs = [b.shape[0] for b in bs]
    g_n = max([ga] + gbs)
    if mode == "tn":
        k_n, m_n = a.shape[1:]
    else:
        m_n, k_n = a.shape[1:]
    n_n = bs[0].shape[1] if mode == "nt" else bs[0].shape[2]
    assert m_n % tm == 0 and n_n % tn == 0 and k_n % tk == 0, (name, m_n, n_n, k_n)
    mi, nj, kk = m_n // tm, n_n // tn, k_n // tk
    lead = g_n if fold_g else None
    g_steps = 1 if fold_g else g_n
    grid = (mi, nj, g_steps, kk)
    ids = lambda i, j, g, k: (g, i, j, k)
    n_red = (g_steps if reduce_g else 1) * kk
    red_idx = lambda: (pl.program_id(2) * kk if reduce_g else 0) + pl.program_id(3)
    sem = ("parallel", "parallel", "arbitrary" if reduce_g else "parallel", "arbitrary")

    def pick(gsz, g):
        return g if gsz > 1 else 0

    def a_map(*p):
        g, i, j, k = ids(*p)
        return (pick(ga, g), k, i) if mode == "tn" else (pick(ga, g), i, k)

    def b_map(gsz):
        def f(*p):
            g, i, j, k = ids(*p)
            return (pick(gsz, g), j, k) if mode == "nt" else (pick(gsz, g), k, j)
        return f

    def o_map(gsz):
        def f(*p):
            g, i, j, k = ids(*p)
            return (0 if reduce_g else pick(gsz, g), i, j)
        return f

    a_spec = pl.BlockSpec((lead, tk, tm) if mode == "tn" else (lead, tm, tk), a_map)
    b_specs = [pl.BlockSpec((lead, tn, tk) if mode == "nt" else (lead, tk, tn), b_map(gs)) for gs in gbs]
    x_specs = [pl.BlockSpec((None, tm, tn), o_map(e.shape[0])) for e in extras]
    r_specs = [pl.BlockSpec((None, tm, tn), o_map(resid.shape[0]))] if resid is not None else []
    g_out = 1 if reduce_g else g_n
    out_shape = [jax.ShapeDtypeStruct((g_out, m_n, n_n), dt) for dt in out_dtypes]
    out_specs = [pl.BlockSpec((None, tm, tn), o_map(g_out)) for _ in out_dtypes]
    nx, nr, no = len(extras), len(r_specs), len(out_dtypes)
    n_in = 1 + nb + nx + nr + len(after)
    dims = _DIMS[mode]

    def body(*refs):
        a_ref = refs[0]
        b_refs = refs[1:1 + nb]
        x_refs = refs[1 + nb:1 + nb + nx]
        r_refs = refs[1 + nb + nx:1 + nb + nx + nr]
        o_refs = refs[n_in:n_in + no]
        acc_refs = refs[n_in + no:]
        def dots(rows):
            if fold_g:
                return [sum(lax.dot_general(a_ref[g, rows, :], b_ref[g], dims, preferred_element_type=F32)
                            for g in range(g_n)) for b_ref in b_refs]
            av = a_ref[...] if mode == "tn" else a_ref[rows, :]
            return [lax.dot_general(av, b_ref[...], dims, preferred_element_type=F32) for b_ref in b_refs]

        def finish(accs, rows=slice(None)):
            if r_refs:
                accs[0] = accs[0] + r_refs[0][rows, :]
            outs = epilogue(accs, [x[rows, :] for x in x_refs]) if epilogue is not None else accs
            for o_ref, o in zip(o_refs, outs):
                o_ref[rows, :] = o.astype(o_ref.dtype)

        if n_red == 1:
            slabs = [slice(s * (tm // sub_m), (s + 1) * (tm // sub_m)) for s in range(sub_m)]
            ahead = dots(slabs[0])
            for s, rows in enumerate(slabs):
                now, ahead = ahead, (dots(slabs[s + 1]) if s + 1 < sub_m else None)
                finish(now, rows)
            return
        products = dots(slice(None))
        r = red_idx()
        for p, acc in zip(products, acc_refs):
            @pl.when(r == 0)
            def _():
                acc[...] = p

            @pl.when((r > 0) & (r < n_red - 1))
            def _():
                acc[...] += p

        @pl.when(r == n_red - 1)
        def _():
            finish([acc[...] + p for p, acc in zip(products, acc_refs)])

    return pl.pallas_call(
        body, name=name, grid=grid,
        in_specs=[a_spec] + b_specs + x_specs + r_specs + [ANY] * len(after),
        out_specs=out_specs, out_shape=out_shape,
        scratch_shapes=[pltpu.VMEM((tm, tn), F32) for _ in range(nb if n_red > 1 else 0)],
        compiler_params=_cparams(sem),
    )(a, *bs, *extras, *([resid] if resid is not None else []), *after)


def _sigmoid(x):
    return 1.0 / (1.0 + jnp.exp(-x))


def _silu(x):
    return x * _sigmoid(x)


def _gelu(x):
    return 0.5 * x * (1.0 + jnp.tanh(0.7978845608028654 * (x + 0.044715 * (x * x * x))))


def _rms_fn(h, gain):
    return h * lax.rsqrt(jnp.mean(h * h, axis=-1, keepdims=True) + EPS) * gain


def _shift_impl(x, s):
    n = x.shape[0]
    rolled = pltpu.roll(x, s % n, 0)
    row = lax.broadcasted_iota(jnp.int32, x.shape, 0)
    return jnp.where((row >= s) & (row < n + s), rolled, 0.0)


@functools.partial(jax.custom_vjp, nondiff_argnums=(1,))
def _shift(x, s):
    return _shift_impl(x, s)


def _shift_fwd(x, s):
    return _shift_impl(x, s), None


def _shift_bwd(s, _, g):
    return (_shift_impl(g, -s),)


_shift.defvjp(_shift_fwd, _shift_bwd)


def _prep_fn(x, w, qk_scale, is_v):
    y = x * w[3:4, :]
    for i in range(3):
        y = y + _shift(x, 3 - i) * w[i:i + 1, :]
    y = _silu(y)
    nrm = lax.rsqrt(jnp.sum(y * y, axis=-1, keepdims=True) + EPS) * qk_scale
    return y * jnp.where(is_v, 1.0, nrm)


def _softplus(x):
    return jnp.maximum(x, 0.0) + jnp.log(1.0 + jnp.exp(-jnp.abs(x)))


def _gates_fn(ba, a_log, dt_bias):
    lane = lax.broadcasted_iota(jnp.int32, ba.shape, 1)
    beta = _sigmoid(ba)
    g = -jnp.exp(a_log) * _softplus(ba + dt_bias)
    return jnp.where(lane < HEADS, beta, g)


def _dot16(a, b, dims=_DIMS["nn"]):
    return lax.dot_general(a.astype(BF16), b.astype(BF16), dims, preferred_element_type=F32)


def _dot32(a, b):
    return jnp.dot(a, b, preferred_element_type=F32, precision=HIGHEST)


def _dot3(a, b, dims=_DIMS["nn"]):
    return lax.dot_general(a, b, dims, preferred_element_type=F32, precision=lax.Precision.HIGH)


def _tri_inverses(mats, tick=lambda: None):
    row = lax.broadcasted_iota(jnp.int32, (CH, CH), 0)
    col = lax.broadcasted_iota(jnp.int32, (CH, CH), 1)
    eye = (row == col).astype(F32)
    ts = [eye - a for a in mats]
    ps = list(mats)
    for _ in range(5):
        ps = [_dot3(p, p) for p in ps]
        tick()
        ts = [t + _dot3(t, p) for t, p in zip(ts, ps)]
        tick()
    return ts


@jax.custom_vjp
def _tri_solves(mats, rhs):
    return [_dot3(t, b) for t, b in zip(_tri_inverses(mats), rhs)]


def _tri_solves_fwd(mats, rhs):
    ts = _tri_inverses(mats)
    xs = [_dot3(t, b) for t, b in zip(ts, rhs)]
    return xs, (ts, xs)


def _tri_solves_bwd(res, dxs):
    ts, xs = res
    dbs = [_dot3(t, dx, _DIMS["tn"]) for t, dx in zip(ts, dxs)]
    return [-_dot3(db, x, _DIMS["nt"]) for db, x in zip(dbs, xs)], dbs


_tri_solves.defvjp(_tri_solves_fwd, _tri_solves_bwd)


def _chunk_prep_fn(xs, bgs, tick=None):
    step = tick or (lambda: None)
    row = lax.broadcasted_iota(jnp.int32, (CH, CH), 0)
    col = lax.broadcasted_iota(jnp.int32, (CH, CH), 1)
    incl = row >= col
    strict = row > col
    lmat = incl.astype(F32)
    n = len(xs)
    items = [(i, h) for i in range(n) for h in range(HEADS)]
    part = lambda i, h, c: xs[i][:, c * HW + h * HD:c * HW + (h + 1) * HD]
    q = [part(i, h, 0) for i, h in items]
    k = [part(i, h, 1) for i, h in items]
    v = [part(i, h, 2) for i, h in items]
    beta = [bgs[i][:, h:h + 1] for i, h in items]
    gc_all = [_dot32(lmat, bg) for bg in bgs]
    step()
    gc = [gc_all[i][:, HEADS + h:HEADS + h + 1] for i, h in items]
    gmat = [jnp.where(strict, jnp.broadcast_to(bgs[i][:, HEADS + h:HEADS + h + 1], (CH, CH)), 0.0) for i, h in items]
    diff = [_dot3(lmat, m) for m in gmat]
    step()
    decay = [jnp.where(incl, jnp.exp(jnp.where(incl, d, 0.0)), 0.0) for d in diff]
    k_beta = [kk * b for kk, b in zip(k, beta)]
    kk_t = [_dot16(kb, kk, _DIMS["nt"]) for kb, kk in zip(k_beta, k)]
    step()
    qk_t = [_dot16(qq, kk, _DIMS["nt"]) for qq, kk in zip(q, k)]
    step()
    a = [jnp.where(strict, m * d, 0.0) for m, d in zip(kk_t, decay)]
    eg = [jnp.exp(g) for g in gc]
    rhs = [jnp.concatenate([vv * b, kb * e], axis=-1) for vv, b, kb, e in zip(v, beta, k_beta, eg)]
    if tick is None:
        uw = _tri_solves(a, rhs)
    else:
        uw = [_dot3(t, b) for t, b in zip(_tri_inverses(a, tick), rhs)]
    qk = [m * d for m, d in zip(qk_t, decay)]
    g_last = [g[CH - 1:CH, :] for g in gc]
    qe = [qq * e for qq, e in zip(q, eg)]
    kd = [kk * jnp.exp(gl - g) for kk, gl, g in zip(k, g_last, gc)]
    egl = [jnp.broadcast_to(jnp.exp(gl), (1, HD)) for gl in g_last]
    out = []
    for i in range(n):
        mine = slice(i * HEADS, (i + 1) * HEADS)
        cat = lambda vals: jnp.concatenate(vals[mine], axis=-1)
        out.append((cat([x[:, :HD] for x in uw]), cat([x[:, HD:] for x in uw]), cat(qe), cat(kd),
                    jnp.concatenate([m[None] for m in qk[mine]], axis=0), cat(egl)))
    return out


def _state_levels(chunks, s, outs, befores, final):
    for u, w, qe, kd, qk, egl in chunks:
        befores.append(s)
        ws = [_dot16(a, b) for a, b in zip(w, s)]
        qs = [_dot16(a, b) for a, b in zip(qe, s)]
        yield
        v_new = [a - b for a, b in zip(u, ws)]
        outs.append([a + _dot16(b, c) for a, b, c in zip(qs, qk, v_new)])
        s = [a * e + _dot16(b, c, _DIMS["tn"]) for a, e, b, c in zip(s, egl, kd, v_new)]
        yield
    final.append(s)


def _chunk_state_fn(u, w, qe, kd, qk, egl, s):
    ws = [_dot16(a, b) for a, b in zip(w, s)]
    qs = [_dot16(a, b) for a, b in zip(qe, s)]
    v_new = [a - b for a, b in zip(u, ws)]
    o = [a + _dot16(b, c) for a, b, c in zip(qs, qk, v_new)]
    s_new = [a * e + _dot16(b, c, _DIMS["tn"]) for a, e, b, c in zip(s, egl, kd, v_new)]
    return o, s_new


def _mix_fn(o, z, ur, vr, ong, lng, lnb, ws, bst):
    row = lax.broadcasted_iota(jnp.int32, (GCH, GCH), 0)
    col = lax.broadcasted_iota(jnp.int32, (GCH, GCH), 1)
    causal = row >= col
    ug = _gelu(ur)
    vg = _gelu(vr)
    sls = [slice(h * HD, (h + 1) * HD) for h in range(HEADS)]
    oh = [o[:, sl] for sl in sls]
    oh = [x * lax.rsqrt(jnp.mean(x * x, axis=-1, keepdims=True) + EPS) for x in oh]
    outs_dn = [x * ong * _silu(z[:, sl]) for x, sl in zip(oh, sls)]
    vh = [vg[:, sl] for sl in sls]
    mu = [jnp.mean(x, axis=-1, keepdims=True) for x in vh]
    var = [jnp.mean(jnp.square(x - m), axis=-1, keepdims=True) for x, m in zip(vh, mu)]
    vn = [(x - m) * lax.rsqrt(s + EPS) * lng[:, sl] + lnb[:, sl] for x, m, s, sl in zip(vh, mu, var, sls)]
    mixed = [_dot16(jnp.where(causal, ws[h], 0.0), vn[h]) for h in range(HEADS)]
    outs_gm = [ug[:, sl] * (mixed[h] + bst[:, h:h + 1]) for h, sl in enumerate(sls)]
    return jnp.concatenate(outs_dn + outs_gm, axis=-1)


def _loss_fn(h, gain, tgt):
    y = _rms_fn(h, gain)
    return 0.5 * jnp.sum(jnp.mean(jnp.square(y - tgt), axis=-1))


RT = 512


def _rows(n=D):
    return pl.BlockSpec((RT, n), lambda i: (i, 0))


def _whole(shape):
    nd = len(shape)
    return pl.BlockSpec(shape, lambda i: (0,) * nd)


def _rmsnorm(name, h, gain):
    def body(h_ref, g_ref, o_ref):
        o_ref[...] = _rms_fn(h_ref[...], g_ref[...]).astype(BF16)

    return pl.pallas_call(
        body, name=name, grid=(T // RT,), in_specs=[_rows(), _whole((1, D))], out_specs=_rows(),
        out_shape=jax.ShapeDtypeStruct((T, D), BF16), compiler_params=_cparams(("parallel",)),
    )(h, gain)


def _rmsnorm_bwd(name, dhn, h, gain, resid):
    def body(dhn_ref, h_ref, g_ref, r_ref, dh_ref, dh16_ref, dg_ref):
        _, vjp = jax.vjp(_rms_fn, h_ref[...], g_ref[...])
        dh, dg = vjp(dhn_ref[...])
        dh = r_ref[...] + dh
        dh_ref[...] = dh
        dh16_ref[...] = dh.astype(BF16)

        @pl.when(pl.program_id(0) == 0)
        def _():
            dg_ref[...] = dg

        @pl.when(pl.program_id(0) > 0)
        def _():
            dg_ref[...] += dg

    return pl.pallas_call(
        body, name=name, grid=(T // RT,), in_specs=[_rows(), _rows(), _whole((1, D)), _rows()],
        out_specs=[_rows(), _rows(), _whole((1, D))],
        out_shape=[jax.ShapeDtypeStruct((T, D), F32), jax.ShapeDtypeStruct((T, D), BF16),
                   jax.ShapeDtypeStruct((1, D), F32)],
        compiler_params=_cparams(("arbitrary",)),
    )(dhn, h, gain, resid)


def _loss_head(h, gain, tgt):
    def body(h_ref, g_ref, t_ref, l_ref, dh_ref, dh16_ref, dg_ref):
        loss, vjp = jax.vjp(lambda hh, gg: _loss_fn(hh, gg, t_ref[...]), h_ref[...], g_ref[...])
        dh, dg = vjp(jnp.ones((), F32))
        dh_ref[...] = dh
        dh16_ref[...] = dh.astype(BF16)
        lv = jnp.full((1, LANE), loss, F32)

        @pl.when(pl.program_id(0) == 0)
        def _():
            dg_ref[...] = dg
            l_ref[...] = lv

        @pl.when(pl.program_id(0) > 0)
        def _():
            dg_ref[...] += dg
            l_ref[...] += lv

    return pl.pallas_call(
        body, name="loss_head", grid=(T // RT,), in_specs=[_rows(), _whole((1, D)), _rows()],
        out_specs=[_whole((1, LANE)), _rows(), _rows(), _whole((1, D))],
        out_shape=[jax.ShapeDtypeStruct((1, LANE), F32), jax.ShapeDtypeStruct((T, D), F32),
                   jax.ShapeDtypeStruct((T, D), BF16), jax.ShapeDtypeStruct((1, D), F32)],
        compiler_params=_cparams(("arbitrary",)),
    )(h, gain, tgt)


def _prep_flags():
    j = pl.program_id(0)
    qk_scale = jnp.where(j < HEADS, HD ** -0.5, 1.0).astype(F32)
    return qk_scale, j >= 2 * HEADS


def _prep(proj, conv_w):
    def body(x_ref, w_ref, o_ref):
        qk_scale, is_v = _prep_flags()
        o_ref[...] = _prep_fn(x_ref[...], w_ref[...], qk_scale, is_v)

    col = lambda j: (0, j)
    return pl.pallas_call(
        body, name="gdn_prep", grid=(3 * HEADS,),
        in_specs=[pl.BlockSpec((T, HD), col), pl.BlockSpec((4, HD), col)], out_specs=pl.BlockSpec((T, HD), col),
        out_shape=jax.ShapeDtypeStruct((T, 3 * HW), F32), compiler_params=_cparams(("parallel",)),
    )(proj, conv_w)


def _prep_bwd(proj, conv_w, dqkv, dproj):
    def body(x_ref, w_ref, d_ref, _, dx_ref, dw_ref):
        qk_scale, is_v = _prep_flags()
        _, vjp = jax.vjp(lambda x, w: _prep_fn(x, w, qk_scale, is_v), x_ref[...], w_ref[...])
        dx, dw = vjp(d_ref[...])
        dx_ref[...] = dx.astype(BF16)
        dw_ref[...] = dw

    col = lambda j: (0, j)
    return pl.pallas_call(
        body, name="gdn_prep_bwd", grid=(3 * HEADS,),
        in_specs=[pl.BlockSpec((T, HD), col), pl.BlockSpec((4, HD), col), pl.BlockSpec((T, HD), col), ANY],
        out_specs=[pl.BlockSpec((T, HD), col), pl.BlockSpec((4, HD), col)],
        out_shape=[jax.ShapeDtypeStruct((T, NP), BF16), jax.ShapeDtypeStruct((4, 3 * HW), F32)],
        input_output_aliases={3: 0}, compiler_params=_cparams(("parallel",)),
    )(proj, conv_w, dqkv, dproj)


BA_BLK = BA_OFF // LANE


def _gates(proj, a_log, dt_bias):
    def body(x_ref, a_ref, d_ref, o_ref):
        o_ref[...] = _gates_fn(x_ref[...], a_ref[...], d_ref[...])

    return pl.pallas_call(
        body, name="gdn_gates", grid=(1,),
        in_specs=[pl.BlockSpec((T, LANE), lambda i: (0, BA_BLK)), _whole((1, LANE)), _whole((1, LANE))],
        out_specs=_whole((T, LANE)),
        out_shape=jax.ShapeDtypeStruct((T, LANE), F32), compiler_params=_cparams(("arbitrary",)),
    )(proj, a_log, dt_bias)


def _gates_bwd(proj, a_log, dt_bias, dbg, dproj):
    def body(x_ref, a_ref, d_ref, dbg_ref, _, dx_ref, da_ref, dd_ref):
        _, vjp = jax.vjp(_gates_fn, x_ref[...], a_ref[...], d_ref[...])
        dx, da_ref[...], dd_ref[...] = vjp(dbg_ref[...])
        dx_ref[...] = dx.astype(BF16)

    ba = pl.BlockSpec((T, LANE), lambda i: (0, BA_BLK))
    return pl.pallas_call(
        body, name="gdn_gates_bwd", grid=(1,),
        in_specs=[ba, _whole((1, LANE)), _whole((1, LANE)), _whole((T, LANE)), ANY],
        out_specs=[ba, _whole((1, LANE)), _whole((1, LANE))],
        out_shape=[jax.ShapeDtypeStruct((T, NP), BF16), jax.ShapeDtypeStruct((1, LANE), F32),
                   jax.ShapeDtypeStruct((1, LANE), F32)],
        input_output_aliases={4: 0}, compiler_params=_cparams(("arbitrary",)),
    )(proj, a_log, dt_bias, dbg, dproj)


NCK = T // CH
CPS = 4


def _chunk_group_specs(at):
    wide = pl.BlockSpec((CPS * CH, HW), lambda n: (at(n), 0))
    return [wide, wide, wide, wide, pl.BlockSpec((HEADS, CPS * CH, CH), lambda n: (0, at(n), 0)),
            pl.BlockSpec((CPS, 1, HW), lambda n: (at(n), 0, 0))]


def _chunk_prep_shapes(dtypes):
    shp = [(T, HW), (T, HW), (T, HW), (T, HW), (HEADS, T, CH), (NCK, 1, HW)]
    return [jax.ShapeDtypeStruct(s, dt) for s, dt in zip(shp, dtypes)]


NGROUP = NCK // CPS
PREP_DTYPES = (F32, BF16, BF16, BF16, BF16, F32)


def _delta_rule(qkv, bg):
    def body(x_ref, bg_ref, *refs):
        prep_out, (o_ref, sh_ref), held, s_ref = refs[:6], refs[6:8], refs[8:14], refs[14]
        i = pl.program_id(0)

        @pl.when(i == 0)
        def _():
            for r in held + (s_ref,):
                r[...] = jnp.zeros_like(r)

        rows = [slice(ci * CH, (ci + 1) * CH) for ci in range(CPS)]
        u_h, w_h, qe_h, kd_h, qk_h, egl_h = held
        chunks = [_head_args((u_h.at[r, :], w_h.at[r, :], qe_h.at[r, :], kd_h.at[r, :], qk_h.at[:, r, :], egl_h.at[ci]))
                  for ci, r in enumerate(rows)]
        start = [jnp.where(i <= 1, 0.0, s_ref[h]) for h in range(HEADS)]
        outs, befores, final = [], [], []
        levels = _state_levels(chunks, start, outs, befores, final)
        res = _chunk_prep_fn([x_ref[r, :] for r in rows], [bg_ref[r, :] for r in rows], tick=lambda: next(levels, None))
        for _ in levels:
            pass
        for ci, (u, w, qe, kd, qk, egl) in enumerate(res):
            for refs_pair, val in zip(zip(prep_out[:4], held[:4]), (u, w, qe, kd)):
                for ref in refs_pair:
                    ref[rows[ci], :] = val.astype(ref.dtype)
            for ref in (prep_out[4], qk_h):
                ref[:, rows[ci], :] = qk.astype(ref.dtype)
            for ref in (prep_out[5], egl_h):
                ref[ci] = egl
        for ci, r in enumerate(rows):
            for h in range(HEADS):
                o_ref[r, h * HD:(h + 1) * HD] = outs[ci][h]
                sh_ref[h, ci] = befores[ci][h]
        for h in range(HEADS):
            s_ref[h] = final[0][h]

    now = lambda n: jnp.minimum(n, NGROUP - 1)
    was = lambda n: jnp.maximum(n - 1, 0)
    wide = lambda at: pl.BlockSpec((CPS * CH, HW), lambda n: (at(n), 0))
    held = [pltpu.VMEM(s, dt) for s, dt in zip(
        [(CPS * CH, HW)] * 4 + [(HEADS, CPS * CH, CH), (CPS, 1, HW)], PREP_DTYPES)]
    out = pl.pallas_call(
        body, name="gdn_delta_rule", grid=(NGROUP + 1,),
        in_specs=[pl.BlockSpec((CPS * CH, 3 * HW), lambda n: (now(n), 0)),
                  pl.BlockSpec((CPS * CH, LANE), lambda n: (now(n), 0))],
        out_specs=[wide(now)] * 4 + [pl.BlockSpec((HEADS, CPS * CH, CH), lambda n: (0, now(n), 0)),
                                     pl.BlockSpec((CPS, 1, HW), lambda n: (now(n), 0, 0)), wide(was),
                                     pl.BlockSpec((HEADS, CPS, HD, HD), lambda n: (0, was(n), 0, 0))],
        out_shape=_chunk_prep_shapes(PREP_DTYPES) + [jax.ShapeDtypeStruct((T, HW), F32),
                                                     jax.ShapeDtypeStruct((HEADS, NCK, HD, HD), F32)],
        scratch_shapes=held + [pltpu.VMEM((HEADS, HD, HD), F32)], compiler_params=_cparams(("arbitrary",)),
    )(qkv, bg)
    return out[:6], out[6], out[7]


def _chunk_prep_bwd(qkv, bg, cots):
    def body(x_ref, bg_ref, du, dw, dqe, dkd, dqk, degl, dx_ref, dbg_ref):
        rows = [slice(ci * CH, (ci + 1) * CH) for ci in range(CPS)]
        _, vjp = jax.vjp(_chunk_prep_fn, [x_ref[r, :] for r in rows], [bg_ref[r, :] for r in rows])
        dxs, dbgs = vjp([(du[r, :], dw[r, :], dqe[r, :], dkd[r, :], dqk[:, r, :], degl[ci])
                         for ci, r in enumerate(rows)])
        for r, dx, dbg in zip(rows, dxs, dbgs):
            dx_ref[r, :] = dx
            dbg_ref[r, :] = dbg

    wide = pl.BlockSpec((CPS * CH, HW), lambda n: (n, 0))
    return pl.pallas_call(
        body, name="gdn_chunk_prep_bwd", grid=(NCK // CPS,),
        in_specs=[pl.BlockSpec((CPS * CH, 3 * HW), lambda n: (n, 0)), pl.BlockSpec((CPS * CH, LANE), lambda n: (n, 0)),
                  wide, wide, wide, wide, pl.BlockSpec((HEADS, CPS * CH, CH), lambda n: (0, n, 0)),
                  pl.BlockSpec((CPS, 1, HW), lambda n: (n, 0, 0))],
        out_specs=[pl.BlockSpec((CPS * CH, 3 * HW), lambda n: (n, 0)), pl.BlockSpec((CPS * CH, LANE), lambda n: (n, 0))],
        out_shape=[jax.ShapeDtypeStruct((T, 3 * HW), F32), jax.ShapeDtypeStruct((T, LANE), F32)],
        compiler_params=_cparams(("parallel",)),
    )(qkv, bg, *cots)


def _head_args(refs):
    u, w, qe, kd, qk, egl = refs
    sls = [slice(h * HD, (h + 1) * HD) for h in range(HEADS)]
    return ([u[:, sl] for sl in sls], [w[:, sl].astype(F32) for sl in sls], [qe[:, sl].astype(F32) for sl in sls],
            [kd[:, sl].astype(F32) for sl in sls], [qk[h].astype(F32) for h in range(HEADS)],
            [egl[:, sl] for sl in sls])


def _chunk_scan_bwd(prep, s_hist, do, after=()):
    n_in = 8 + len(after)

    def body(*refs):
        u_r, w_r, qe_r, kd_r, qk_r, egl_r, sh_ref, do_ref = refs[:8]
        d_refs = refs[n_in:n_in + 6]
        ds_ref = refs[n_in + 6]

        @pl.when(pl.program_id(0) == 0)
        def _():
            ds_ref[...] = jnp.zeros_like(ds_ref)

        sls = [slice(h * HD, (h + 1) * HD) for h in range(HEADS)]
        ds = [ds_ref[h] for h in range(HEADS)]
        for ci in reversed(range(CPS)):
            r = slice(ci * CH, (ci + 1) * CH)
            args = _head_args((u_r.at[r, :], w_r.at[r, :], qe_r.at[r, :], kd_r.at[r, :], qk_r.at[:, r, :], egl_r.at[ci]))
            _, vjp = jax.vjp(_chunk_state_fn, *args, [sh_ref[h, ci] for h in range(HEADS)])
            du, dw, dqe, dkd, dqk, degl, ds = vjp(([do_ref[r, sl] for sl in sls], ds))
            for h, sl in enumerate(sls):
                for d_ref, val in zip(d_refs[:4], (du, dw, dqe, dkd)):
                    d_ref[r, sl] = val[h]
                d_refs[4][h, r, :] = dqk[h]
                d_refs[5][ci, :, sl] = degl[h]
        for h in range(HEADS):
            ds_ref[h] = ds[h]

    rev = lambda n: NGROUP - 1 - n
    return pl.pallas_call(
        body, name="gdn_scan_bwd", grid=(NGROUP,),
        in_specs=_chunk_group_specs(rev) + [pl.BlockSpec((HEADS, CPS, HD, HD), lambda n: (0, rev(n), 0, 0)),
                                            pl.BlockSpec((CPS * CH, HW), lambda n: (rev(n), 0))] + [ANY] * len(after),
        out_specs=_chunk_group_specs(rev), out_shape=_chunk_prep_shapes((F32,) * 6),
        scratch_shapes=[pltpu.VMEM((HEADS, HD, HD), F32)], compiler_params=_cparams(("arbitrary",)),
    )(*prep, s_hist, do, *after)


MIX_ROWS = 4 * GCH


def _mix_rows(o, z, ur, vr, *params):
    chunks = [slice(c * GCH, (c + 1) * GCH) for c in range(MIX_ROWS // GCH)]
    return jnp.concatenate([_mix_fn(o[r], z[r], ur[r], vr[r], *params) for r in chunks], axis=0)


def _mix_specs():
    pc = lambda c: pl.BlockSpec((MIX_ROWS, HW), lambda i: (i, c))
    return [pl.BlockSpec((MIX_ROWS, HW), lambda i: (i, 0)), pc(3), pc(4), pc(5), _whole((1, HD)), _whole((1, HW)),
            _whole((1, HW)), _whole((HEADS, GCH, GCH)), _whole((GCH, LANE))]


def _mix(o, proj, ong, lng, lnb, ws, bst, after=()):
    def body(o_ref, z_ref, u_ref, v_ref, ong_ref, lng_ref, lnb_ref, ws_ref, bs_ref, *rest):
        rest[-1][...] = _mix_rows(o_ref[...], z_ref[...], u_ref[...], v_ref[...], ong_ref[...], lng_ref[...],
                                  lnb_ref[...], ws_ref[...], bs_ref[...]).astype(BF16)

    return pl.pallas_call(
        body, name="mix", grid=(T // MIX_ROWS,), in_specs=_mix_specs() + [ANY] * len(after),
        out_specs=pl.BlockSpec((MIX_ROWS, D), lambda i: (i, 0)), out_shape=jax.ShapeDtypeStruct((T, D), BF16),
        compiler_params=_cparams(("parallel",)),
    )(o, proj, proj, proj, ong, lng, lnb, ws, bst, *after)


def _mix_bwd(o, proj, ong, lng, lnb, ws, bst, dmix):
    def body(o_ref, z_ref, u_ref, v_ref, ong_ref, lng_ref, lnb_ref, ws_ref, bs_ref, dm_ref,
             do_ref, dzuv_ref, dong_ref, dlng_ref, dlnb_ref, dws_ref, dbs_ref):
        _, vjp = jax.vjp(_mix_rows, o_ref[...], z_ref[...], u_ref[...], v_ref[...], ong_ref[...], lng_ref[...],
                         lnb_ref[...], ws_ref[...], bs_ref[...])
        do, dz, du, dv, dong, dlng, dlnb, dws, dbs = vjp(dm_ref[...])
        do_ref[...] = do
        dzuv_ref[:, 0:HW] = dz.astype(BF16)
        dzuv_ref[:, HW:2 * HW] = du.astype(BF16)
        dzuv_ref[:, 2 * HW:3 * HW] = dv.astype(BF16)
        acc = [(dong_ref, dong), (dlng_ref, dlng), (dlnb_ref, dlnb), (dws_ref, dws), (dbs_ref, dbs)]

        @pl.when(pl.program_id(0) == 0)
        def _():
            for r, val in acc:
                r[...] = val

        @pl.when(pl.program_id(0) > 0)
        def _():
            for r, val in acc:
                r[...] += val

    shp = lambda *s: jax.ShapeDtypeStruct(s, F32)
    return pl.pallas_call(
        body, name="mix_bwd", grid=(T // MIX_ROWS,),
        in_specs=_mix_specs() + [pl.BlockSpec((MIX_ROWS, D), lambda i: (i, 0))],
        out_specs=[pl.BlockSpec((MIX_ROWS, HW), lambda i: (i, 0)), pl.BlockSpec((MIX_ROWS, 3 * HW), lambda i: (i, 1)),
                   _whole((1, HD)), _whole((1, HW)), _whole((1, HW)), _whole((HEADS, GCH, GCH)), _whole((GCH, LANE))],
        out_shape=[shp(T, HW), jax.ShapeDtypeStruct((T, NP), BF16), shp(1, HD), shp(1, HW), shp(1, HW),
                   shp(HEADS, GCH, GCH), shp(GCH, LANE)],
        compiler_params=_cparams(("arbitrary",)),
    )(o, proj, proj, proj, ong, lng, lnb, ws, bst, dmix)


def _swiglu_epilogue(accs, _):
    gate, up = accs
    return [gate, up, _silu(gate) * up]


def _swiglu_bwd_epilogue(accs, extras):
    dact = accs[0]
    gate, up = (e.astype(F32) for e in extras)
    sg = _sigmoid(gate)
    return [dact * up * (sg * (1.0 + gate * (1.0 - sg))), dact * (gate * sg)]


def _layer_fwd(h, p):
    hn = p.pop("hn") if "hn" in p else _rmsnorm("rms_mix", h, p["norm_mix"])
    proj = _mm("in_proj", "nn", hn[None], [p["w_in"][None]], tm=512, tn=NP, tk=D, sub_m=2)[0][0]
    qkv = _prep(proj, p["conv_w"])
    bg = _gates(proj, p["a_log"], p["dt_bias"])
    prep, o, s_hist = _delta_rule(qkv, bg)
    mix = _mix(o, proj, p["o_norm_g"], p["ln_v_g"], p["ln_v_b"], p["w_s"], p["bst"],
               p.pop("before_mix")(o) if "before_mix" in p else ())
    if "late" in p:
        p.update(p.pop("late")(mix))
    h1 = _mm("out_proj", "nn", mix[None], [p["w_out"].reshape(1, D, D)], tm=T, tn=512, tk=D, resid=h[None],
             sub_m=4)[0][0]
    h2n = _rmsnorm("rms_ffn", h1, p["norm_ffn"])
    gate, up, act = _mm("ffn_in", "nt", h2n[None], [p["w_gate"], p["w_up"]], tm=1024, tn=FF_SH, tk=D,
                        out_dtypes=(BF16, BF16, BF16), epilogue=_swiglu_epilogue, sub_m=4)
    then = p.pop("before_ffn_out")(act) if "before_ffn_out" in p else ()
    h2 = _mm("ffn_out", "nn", act, [p["w_down"]], tm=1024, tn=512, tk=FF_SH, reduce_g=True, fold_g=True,
             resid=h1[None], sub_m=2, after=then)[0][0]
    saved = dict(h=h, hn=hn, proj=proj, qkv=qkv, bg=bg, prep=prep, o=o, s_hist=s_hist, mix=mix, h1=h1, h2n=h2n,
                 gate=gate, up=up, act=act)
    return h2, saved


def _layer_bwd_ffn(dh2, dh2b, p, s, after=()):
    dh2b = dh2b[None]
    dgate, dup = _mm("ffn_out_bwd", "nt", dh2b, [p["w_down"]], tm=1024, tn=FF_SH, tk=D, out_dtypes=(BF16, BF16),
                     extras=(s["gate"], s["up"]), epilogue=_swiglu_bwd_epilogue, after=after, sub_m=4)
    dh2n = _mm("ffn_gate_bwd", "nn", dgate, [p["w_gate"]], tm=1024, tn=512, tk=FF_SH, reduce_g=True, fold_g=True,
               sub_m=2)[0]
    dh2n = _mm("ffn_up_bwd", "nn", dup, [p["w_up"]], tm=1024, tn=512, tk=FF_SH, reduce_g=True, fold_g=True,
               resid=dh2n, sub_m=2)[0][0]
    dh1, dh1b, d_norm_ffn = _rmsnorm_bwd("rms_ffn_bwd", dh2n, s["h1"], p["norm_ffn"], dh2)
    d_w_down = _mm("ffn_wdown_grad", "tn", s["act"], [dh2b], tm=FF_SH, tn=512, tk=T)[0]
    d_w_gate = _mm("ffn_wgate_grad", "tn", dgate, [s["h2n"][None]], tm=FF_SH, tn=512, tk=T)[0]
    d_w_up = _mm("ffn_wup_grad", "tn", dup, [s["h2n"][None]], tm=FF_SH, tn=512, tk=T)[0]
    return dh1, dh1b, dict(norm_ffn=d_norm_ffn, w_gate=d_w_gate, w_up=d_w_up, w_down=d_w_down)


def _layer_bwd_mixer(dh1, dh1b, p, s, after=(), midway=None, late=None):
    dh1b = dh1b[None]
    dmix = _mm("out_proj_bwd", "nt", dh1b, [p["w_out"].reshape(1, D, D)], tm=T, tn=512, tk=D, after=after,
               sub_m=4)[0][0]
    d_w_out = _mm("out_proj_wgrad", "tn", s["mix"][None], [dh1b], tm=1024, tn=512, tk=T)[0][0]
    do, dproj, d_ong, d_lng, d_lnb, d_ws, d_bst = _mix_bwd(
        s["o"], s["proj"], p["o_norm_g"], p["ln_v_g"], p["ln_v_b"], p["w_s"], p["bst"], dmix)
    then = midway(do) if midway is not None else ()
    dqkv, dbg = _chunk_prep_bwd(s["qkv"], s["bg"], _chunk_scan_bwd(s["prep"], s["s_hist"], do, then))
    dproj, d_conv = _prep_bwd(s["proj"], p["conv_w"], dqkv, dproj)
    dproj, d_a_log, d_dt_bias = _gates_bwd(s["proj"], p["a_log"], p["dt_bias"], dbg, dproj)
    dproj = dproj[None]
    d_w_in = _mm("in_proj_wgrad", "tn", s["hn"][None], [dproj], tm=512, tn=640, tk=T)[0]
    last = late(dict(w_in=d_w_in, w_out=d_w_out)) if late is not None else ()
    dhn = _mm("in_proj_bwd", "nt", dproj, [p["w_in"][None]], tm=1024, tn=512, tk=NP, after=last,
              sub_m=2)[0][0]
    dh, dhb, d_norm_mix = _rmsnorm_bwd("rms_mix_bwd", dhn, s["h"], p["norm_mix"], dh1)
    grads = dict(norm_mix=d_norm_mix, w_in=d_w_in, conv_w=d_conv, a_log=d_a_log, dt_bias=d_dt_bias, o_norm_g=d_ong,
                 ln_v_g=d_lng, ln_v_b=d_lnb, w_s=d_ws, bst=d_bst, w_out=d_w_out)
    return dh, dhb, grads


def _lanes(v, off=0):
    return jnp.zeros((1, LANE), F32).at[0, off:off + v.shape[0]].set(v)


def _w_in_pieces():
    regions = [(0, 2048, 0), (2048, 2056, BA_OFF), (2056, IN_DIM, 2048)]
    sh = IN_DIM // NCHIP
    out = []
    for j in range(NCHIP):
        for lo, hi, at in regions:
            a, b = max(lo, j * sh), min(hi, (j + 1) * sh)
            if a < b:
                out.append((j, a - j * sh, at + a - lo, b - a))
    return out


W_IN_PIECES = _w_in_pieces()
WT = 256


def _assemble_w_in(gathered, own, place):
    def body(place_ref, g_ref, own_ref, o_ref):
        o_ref[:, IN_DIM:] = jnp.zeros((WT, NP - IN_DIM), BF16)
        mine = own_ref[...]
        for j, src, dst, width in W_IN_PIECES:
            val = jnp.where(place_ref[0] == j, mine[:, src:src + width], g_ref[j, :, src:src + width])
            o_ref[:, dst:dst + width] = val

    sh = IN_DIM // NCHIP
    return pl.pallas_call(
        body, name="assemble_w_in",
        grid_spec=pltpu.PrefetchScalarGridSpec(
            num_scalar_prefetch=1, grid=(D // WT,),
            in_specs=[pl.BlockSpec((NCHIP, WT, sh), lambda i, place_ref: (0, i, 0)),
                      pl.BlockSpec((WT, sh), lambda i, place_ref: (i, 0))],
            out_specs=pl.BlockSpec((WT, NP), lambda i, place_ref: (i, 0))),
        out_shape=jax.ShapeDtypeStruct((D, NP), BF16), compiler_params=_cparams(("parallel",)),
    )(place, gathered, own)


def _layer_params(l, big, small):
    return dict(
        {k: v for k, v in big.items() if k != "conv_w"},
        conv_w=jnp.concatenate([big["conv_w"][j, l] for j in range(NCHIP)], axis=1),
        norm_mix=small["norm_mix"][l][None], norm_ffn=small["norm_ffn"][l][None],
        a_log=_lanes(small["a_log"][l], HEADS), dt_bias=_lanes(small["dt_bias"][l], HEADS),
        o_norm_g=small["o_norm_g"][l][None], ln_v_g=small["ln_v_g"][l][None], ln_v_b=small["ln_v_b"][l][None],
        w_s=small["w_s"][l],
        bst=jnp.pad(small["b_s"][l].T, ((0, 0), (0, LANE - HEADS))),
    )


def _reference_layout(g):
    return dict(
        w_in=g["w_in"],
        w_out=g["w_out"].reshape(NCHIP, D // NCHIP, D),
        w_gate=g["w_gate"], w_up=g["w_up"], w_down=g["w_down"],
        conv_w=g["conv_w"], norm_mix=g["norm_mix"][0], norm_ffn=g["norm_ffn"][0],
        a_log=g["a_log"][0, HEADS:2 * HEADS], dt_bias=g["dt_bias"][0, HEADS:2 * HEADS],
        o_norm_g=g["o_norm_g"][0], ln_v_g=g["ln_v_g"][0], ln_v_b=g["ln_v_b"][0], w_s=g["w_s"],
        b_s=g["bst"][:, :HEADS].T,
    )


def _forward(x, tgt, layers, norm_final):
    h = x
    saved, params = [], []
    for p in layers:
        p = p(h) if callable(p) else p
        h, s = _layer_fwd(h, p)
        saved.append(s)
        params.append(p)
    return (saved, params) + tuple(_loss_head(h, norm_final, tgt))


def _local_step(x, tgt, layers, norm_final):
    saved, layers, loss, dh, dhb, d_norm_final = _forward(x, tgt, layers, norm_final)
    grads = [None] * DEPTH
    for l in reversed(range(DEPTH)):
        dh1, dh1b, g_ffn = _layer_bwd_ffn(dh, dhb, layers[l], saved[l])
        dh, dhb, g_mix = _layer_bwd_mixer(dh1, dh1b, layers[l], saved[l])
        grads[l] = {**g_ffn, **g_mix}
    return loss, dh, grads, d_norm_final


def _place():
    x, y, c = lax.axis_index("x"), lax.axis_index("y"), lax.axis_index("c")
    return x, y, c, [(1 - x, y), (x, 1 - y), (1 - x, 1 - y)]


def _remote(src, dst, send_sem, recv_sem, to):
    return pltpu.make_async_remote_copy(src_ref=src, dst_ref=dst, send_sem=send_sem, recv_sem=recv_sem,
                                        device_id=to, device_id_type=MESH)


def _comm_call(name, body, ins, out_shape, n_sems, aliases=None):
    return pl.pallas_call(
        body, name=name, in_specs=[ANY] * len(ins), out_specs=[ANY] * len(out_shape), out_shape=out_shape,
        scratch_shapes=[pltpu.SemaphoreType.DMA((n,)) for n in n_sems], input_output_aliases=aliases or {},
        compiler_params=pltpu.CompilerParams(has_side_effects=True),
    )(*ins)


def _half_rows(ref, of_c, dim):
    hr = ref.shape[dim] // 2
    return pl.ds(pl.multiple_of(of_c * hr, BF16_ROWS), hr)


def _gather_plan(whole):
    def plan(srcs, lands):
        x, y, c, others = _place()
        chip = 2 * x + y
        out = []
        for src, land, all_of_it in zip(srcs, lands, whole):
            for ox, oy in others:
                if all_of_it:
                    out.append((src, land.at[chip], (ox, oy, c)))
                else:
                    out.append((src.at[_half_rows(src, c, 0)], land.at[chip, _half_rows(src, c, 0)], (ox, oy, c)))
        return out
    return plan


def _forward_halves(lands):
    n = len(lands)

    def body(*refs):
        outs = refs[n:2 * n]
        send_s, recv_s = refs[2 * n:]
        x, y, c, others = _place()
        sibling = (x, y, 1 - c)
        copies = []
        for a in range(n):
            for k, (ox, oy) in enumerate(others):
                mine = outs[a].at[2 * ox + oy, _half_rows(outs[a], c, 1)]
                copies.append(_remote(mine, mine, send_s.at[3 * a + k], recv_s.at[3 * a + k], sibling))
        for cp in copies:
            cp.start()
        for a in range(n):
            for k, (ox, oy) in enumerate(others):
                landed = outs[a].at[2 * ox + oy, _half_rows(outs[a], 1 - c, 1)]
                _remote(landed, landed, send_s.at[3 * a + k], recv_s.at[3 * a + k], sibling).wait_recv()
        for cp in copies:
            cp.wait_send()

    out_shape = [jax.ShapeDtypeStruct(g.shape, g.dtype) for g in lands]
    return _comm_call("forward_halves", body, lands, out_shape, [3 * n, 3 * n], aliases={a: a for a in range(n)})


def _forward_refs(bufs, incoming):
    x, y, c, others = _place()
    return (x, y, 1 - c), [b.at[2 * ox + oy, _half_rows(b, 1 - c if incoming else c, 1)]
                           for b in bufs for ox, oy in others]


def _forward_start(name, bufs, after):
    n = len(bufs)
    bufs = [pltpu.with_memory_space_constraint(b, pltpu.HBM) for b in bufs]

    def body(*refs):
        send_s, recv_s = refs[n + len(after)], refs[n + len(after) + 1]
        sibling, mine = _forward_refs(refs[:n], incoming=False)
        for i, ref in enumerate(mine):
            _remote(ref, ref, send_s.at[i], recv_s.at[i], sibling).start()
        refs[-1][...] = jnp.zeros_like(refs[-1])

    out = pl.pallas_call(
        body, name=name, in_specs=[HBM_SPEC] * n + [ANY] * len(after),
        out_specs=[SEM_SPEC, SEM_SPEC] + [HBM_SPEC] * n + [pl.BlockSpec(memory_space=pltpu.VMEM)],
        out_shape=[pltpu.SemaphoreType.DMA((3 * n,)), pltpu.SemaphoreType.DMA((3 * n,))]
        + [pltpu.HBM(b.shape, b.dtype) for b in bufs] + [jax.ShapeDtypeStruct((F32_ROWS, LANE), F32)],
        input_output_aliases={i: 2 + i for i in range(n)},
        compiler_params=pltpu.CompilerParams(has_side_effects=DATAFLOW),
    )(*bufs, *after)
    return dict(sems=out[:2], bufs=out[2:2 + n], token=out[-1])


def _forward_wait(name, started, after):
    n = len(started["bufs"])

    def body(*refs):
        send_s, recv_s = refs[n], refs[n + 1]
        sibling, mine = _forward_refs(refs[:n], incoming=False)
        _, theirs = _forward_refs(refs[:n], incoming=True)
        for i, (sent, landed) in enumerate(zip(mine, theirs)):
            _remote(sent, sent, send_s.at[i], recv_s.at[i], sibling).wait_send()
            _remote(landed, landed, send_s.at[i], recv_s.at[i], sibling).wait_recv()

    return pl.pallas_call(
        body, name=name, in_specs=[HBM_SPEC] * n + [SEM_SPEC, SEM_SPEC] + [ANY] * len(after),
        out_specs=[HBM_SPEC] * n, out_shape=[pltpu.HBM(b.shape, b.dtype) for b in started["bufs"]],
        input_output_aliases={i: i for i in range(n)},
        compiler_params=pltpu.CompilerParams(has_side_effects=DATAFLOW),
    )(*started["bufs"], *started["sems"], *after)


HBM_SPEC = pl.BlockSpec(memory_space=pltpu.HBM)
SEM_SPEC = pl.BlockSpec(memory_space=pltpu.SEMAPHORE)
DATAFLOW = pltpu.SideEffectType.DATAFLOW_SIDE_EFFECTING


def _exchange_plan(srcs, lands):
    x, y, c, _ = _place()
    plan = []
    for src, land in zip(srcs, lands):
        hr = src.shape[1] // 2
        plan.append((src.at[:, pl.ds(pl.multiple_of((1 - c) * hr, 8), hr)], land, (x, y, 1 - c)))
    return plan


def _scatter_plan(srcs, lands):
    x, y, c, others = _place()
    return [(src.at[2 * ox + oy], land.at[k], (ox, oy, c))
            for src, land in zip(srcs, lands) for k, (ox, oy) in enumerate(others)]


def _split_start(name, plan, srcs, land_shapes, n_copies, after=()):
    n = len(srcs)
    lands = [pltpu.with_memory_space_constraint(lax.empty(s.shape, s.dtype), pltpu.HBM) for s in land_shapes]
    srcs = [pltpu.with_memory_space_constraint(s, pltpu.HBM) for s in srcs]

    def body(*refs):
        send_s, recv_s = refs[2 * n + len(after)], refs[2 * n + len(after) + 1]
        for i, (src, dst, to) in enumerate(plan(refs[:n], refs[n:2 * n])):
            _remote(src, dst, send_s.at[i], recv_s.at[i], to).start()
        refs[-1][...] = jnp.zeros_like(refs[-1])

    thru = [pltpu.HBM(s.shape, s.dtype) for s in srcs + lands]
    out = pl.pallas_call(
        body, name=name, in_specs=[HBM_SPEC] * (2 * n) + [ANY] * len(after),
        out_specs=[SEM_SPEC, SEM_SPEC] + [HBM_SPEC] * (2 * n) + [pl.BlockSpec(memory_space=pltpu.VMEM)],
        out_shape=[pltpu.SemaphoreType.DMA((n_copies,)), pltpu.SemaphoreType.DMA((n_copies,))] + thru
        + [jax.ShapeDtypeStruct((F32_ROWS, LANE), F32)],
        input_output_aliases={i: 2 + i for i in range(2 * n)},
        compiler_params=pltpu.CompilerParams(has_side_effects=DATAFLOW),
    )(*srcs, *lands, *after)
    return dict(sems=out[:2], srcs=out[2:2 + n], lands=out[2 + n:2 + 2 * n], token=out[-1])


def _split_wait(name, plan, started, after):
    n = len(started["srcs"])
    after = list(after) if isinstance(after, (list, tuple)) else [after]

    def body(*refs):
        send_s, recv_s = refs[2 * n], refs[2 * n + 1]
        for i, (src, dst, to) in enumerate(plan(refs[:n], refs[n:2 * n])):
            cp = _remote(src, dst, send_s.at[i], recv_s.at[i], to)
            cp.wait_send()
            cp.wait_recv()

    arrs = list(started["srcs"]) + list(started["lands"])
    out = pl.pallas_call(
        body, name=name, in_specs=[HBM_SPEC] * (2 * n) + [SEM_SPEC, SEM_SPEC] + [ANY] * len(after),
        out_specs=[HBM_SPEC] * (2 * n), out_shape=[pltpu.HBM(s.shape, s.dtype) for s in arrs],
        input_output_aliases={i: i for i in range(2 * n)},
        compiler_params=pltpu.CompilerParams(has_side_effects=DATAFLOW),
    )(*arrs, *started["sems"], *after)
    return out[:n], out[n:]


def _join_halves(name, rs):
    n = len(rs)

    def body(*refs):
        outs = refs[n:2 * n]
        send_s, recv_s = refs[2 * n:]
        x, y, c, _ = _place()
        sibling = (x, y, 1 - c)

        def half(a, of_c):
            hr = outs[a].shape[1] // 2
            return outs[a].at[:, pl.ds(pl.multiple_of(of_c * hr, 8), hr)]

        copies = [_remote(half(a, c), half(a, c), send_s.at[a], recv_s.at[a], sibling) for a in range(n)]
        for cp in copies:
            cp.start()
        for a in range(n):
            landed = half(a, 1 - c)
            _remote(landed, landed, send_s.at[a], recv_s.at[a], sibling).wait_recv()
        for cp in copies:
            cp.wait_send()

    out_shape = [jax.ShapeDtypeStruct(r.shape, r.dtype) for r in rs]
    return _comm_call(name, body, rs, out_shape, [n, n], aliases={a: a for a in range(n)})


def _allreduce_small(buf, after=()):
    r = buf.shape[0]
    hr = r // 2

    def body(in_ref, *refs):
        out_ref, theirs, by_chip, send_s, recv_s = refs[len(after):]
        x, y, c, others = _place()
        chip = 2 * x + y
        sibling = (x, y, 1 - c)
        mine = pl.ds(pl.multiple_of(c * hr, F32_ROWS), hr)
        swap = _remote(in_ref, theirs, send_s.at[0], recv_s.at[0], sibling)
        swap.start()
        swap.wait()
        by_chip[chip] = in_ref[mine, :] + theirs[mine, :]
        sends = [_remote(by_chip.at[chip], by_chip.at[chip], send_s.at[1 + k], recv_s.at[1 + k], (ox, oy, c))
                 for k, (ox, oy) in enumerate(others)]
        for cp in sends:
            cp.start()
        for k, (ox, oy) in enumerate(others):
            landed = by_chip.at[2 * ox + oy]
            _remote(landed, landed, send_s.at[1 + k], recv_s.at[1 + k], (ox, oy, c)).wait_recv()
        for cp in sends:
            cp.wait_send()
        out_ref[mine, :] = (by_chip[0] + by_chip[1]) + (by_chip[2] + by_chip[3])
        back = _remote(out_ref.at[mine], out_ref.at[mine], send_s.at[NCHIP], recv_s.at[NCHIP], sibling)
        back.start()
        other = out_ref.at[pl.ds(pl.multiple_of((1 - c) * hr, F32_ROWS), hr)]
        _remote(other, other, send_s.at[NCHIP], recv_s.at[NCHIP], sibling).wait_recv()
        back.wait_send()

    vm = pl.BlockSpec(memory_space=pltpu.VMEM)
    return pl.pallas_call(
        body, name="allreduce_small", in_specs=[vm] + [ANY] * len(after), out_specs=vm,
        out_shape=jax.ShapeDtypeStruct((r, LANE), F32),
        scratch_shapes=[pltpu.VMEM((r, LANE), F32), pltpu.VMEM((NCHIP, hr, LANE), F32),
                        pltpu.SemaphoreType.DMA((NCHIP + 1,)), pltpu.SemaphoreType.DMA((NCHIP + 1,))],
        compiler_params=pltpu.CompilerParams(has_side_effects=True, vmem_limit_bytes=VMEM_LIMIT),
    )(buf, *after)


MAX_ROW_TILE = 512
BF16_ROWS = 16


def _row_tile(rows):
    for t in range(min(rows, MAX_ROW_TILE) // BF16_ROWS * BF16_ROWS, 0, -BF16_ROWS):
        if rows % t == 0:
            return t
    raise ValueError(rows)


def _sum_halves(g, theirs, c_arr):
    nch, rows, cols = g.shape
    hr = rows // 2
    tr = _row_tile(hr)

    def body(c_ref, g_ref, t_ref, o_ref, ob_ref):
        s = g_ref[...] + t_ref[...]
        o_ref[...] = s
        ob_ref[...] = s.astype(BF16)

    blk = pl.BlockSpec((None, tr, cols), lambda j, i, c_ref: (j, i, 0))
    return pl.pallas_call(
        body, name="sum_halves",
        grid_spec=pltpu.PrefetchScalarGridSpec(
            num_scalar_prefetch=1, grid=(nch, hr // tr),
            in_specs=[pl.BlockSpec((None, None, tr, cols), lambda j, i, c_ref: (j, c_ref[0], i, 0)), blk],
            out_specs=[blk, blk]),
        out_shape=[jax.ShapeDtypeStruct((nch, hr, cols), F32), jax.ShapeDtypeStruct((nch, hr, cols), BF16)],
        compiler_params=_cparams(("parallel", "parallel")),
    )(c_arr, g.reshape(nch, 2, hr, cols), theirs)


def _sum_halves_w_in(g, theirs, c_arr):
    hr = D // 2
    sh = IN_DIM // NCHIP

    def body(c_ref, g_ref, t_ref, o_ref, ob_ref):
        s = g_ref[...] + t_ref[...]
        for j, dst, src, width in W_IN_PIECES:
            o_ref[j, :, dst:dst + width] = s[:, src:src + width]
            ob_ref[j, :, dst:dst + width] = s[:, src:src + width].astype(BF16)

    out = pl.BlockSpec((NCHIP, WT, sh), lambda i, c_ref: (0, i, 0))
    return pl.pallas_call(
        body, name="sum_halves_w_in",
        grid_spec=pltpu.PrefetchScalarGridSpec(
            num_scalar_prefetch=1, grid=(hr // WT,),
            in_specs=[pl.BlockSpec((None, WT, NP), lambda i, c_ref: (c_ref[0], i, 0)),
                      pl.BlockSpec((None, WT, NP), lambda i, c_ref: (0, i, 0))],
            out_specs=[out, out]),
        out_shape=[jax.ShapeDtypeStruct((NCHIP, hr, sh), F32), jax.ShapeDtypeStruct((NCHIP, hr, sh), BF16)],
        compiler_params=_cparams(("parallel",)),
    )(c_arr, g.reshape(2, hr, NP), theirs)


def _sum_chips(p, q, place, l, into=None, after=()):
    extra = ([into] if into is not None else []) + list(after)
    _, rows, cols = p.shape
    tr = _row_tile(rows)
    steps = rows // tr

    def body(place_ref, p_ref, q0, q1, q2, *rest):
        rest[-1][...] = ((p_ref[...] + q0[...].astype(F32)) + q1[...].astype(F32)) + q2[...].astype(F32)

    qs = lambda k: pl.BlockSpec((None, tr, cols), lambda i, place_ref: (k, i, 0))
    return pl.pallas_call(
        body, name="sum_chips",
        grid_spec=pltpu.PrefetchScalarGridSpec(
            num_scalar_prefetch=1, grid=(steps,),
            in_specs=[pl.BlockSpec((None, tr, cols), lambda i, place_ref: (place_ref[0], i, 0)), qs(0), qs(1), qs(2)]
            + [ANY] * len(extra),
            out_specs=pl.BlockSpec((None, tr, cols), lambda i, place_ref: (l, place_ref[1] * steps + i, 0))),
        out_shape=jax.ShapeDtypeStruct((DEPTH, 2 * rows, cols), F32),
        input_output_aliases={5: 0} if into is not None else {},
        compiler_params=_cparams(("parallel",)),
    )(place, p, q, q, q, *extra)


def _adamw_fn(w, g, m, v):
    nm = ADAM_B1 * m + (1.0 - ADAM_B1) * g
    nv = ADAM_B2 * v + (1.0 - ADAM_B2) * jnp.square(g)
    m_hat = nm / (1.0 - ADAM_B1 ** ADAM_STEP)
    v_hat = nv / (1.0 - ADAM_B2 ** ADAM_STEP)
    return -ADAM_LR * (m_hat / (jnp.sqrt(v_hat) + ADAM_EPS) + ADAM_WD * w), nm, nv


def _adamw(w, g, m, v):
    layers, rows, cols = w.shape
    tr = _row_tile(rows)

    def body(w_ref, g_ref, m_ref, v_ref, d_ref, nm_ref, nv_ref):
        d_ref[...], nm_ref[...], nv_ref[...] = _adamw_fn(w_ref[...], g_ref[...], m_ref[...], v_ref[...])

    blk = pl.BlockSpec((None, tr, cols), lambda l, i: (l, i, 0))
    return pl.pallas_call(
        body, name="adamw", grid=(layers, rows // tr), in_specs=[blk] * 4, out_specs=[blk] * 3,
        out_shape=[jax.ShapeDtypeStruct(w.shape, F32)] * 3, compiler_params=_cparams(("parallel", "parallel")),
    )(w, g, m, v)


def _adamw_small(ws, gs, ms, vs):
    n = len(ws)

    def body(*refs):
        for i in range(n):
            w_ref, g_ref, m_ref, v_ref, d_ref, nm_ref, nv_ref = (refs[k * n + i] for k in range(7))
            d_ref[...], nm_ref[...], nv_ref[...] = _adamw_fn(w_ref[...], g_ref[...], m_ref[...], v_ref[...])

    vm = pl.BlockSpec(memory_space=pltpu.VMEM)
    out = pl.pallas_call(
        body, name="adamw_small", in_specs=[vm] * (4 * n), out_specs=[vm] * (3 * n),
        out_shape=[jax.ShapeDtypeStruct(a.shape, F32) for a in list(ws) * 3],
        compiler_params=pltpu.CompilerParams(vmem_limit_bytes=VMEM_LIMIT),
    )(*ws, *gs, *ms, *vs)
    return out[:n], out[n:2 * n], out[2 * n:]


BIG = ("w_in", "w_out", "w_gate", "w_up", "w_down")
SMALL = ("norm_mix", "a_log", "dt_bias", "o_norm_g", "ln_v_g", "ln_v_b", "w_s", "b_s", "norm_ffn", "norm_final")
ORDER = ("norm_mix", "w_in", "conv_w", "a_log", "dt_bias", "o_norm_g", "ln_v_g", "ln_v_b", "w_s", "b_s", "w_out",
         "norm_ffn", "w_gate", "w_up", "w_down", "norm_final")


F32_ROWS = 8
PACK_ROWS = 128


def _lane_rows(size):
    return -(-size // (F32_ROWS * LANE)) * F32_ROWS


def _pack(arrs):
    parts = [jnp.pad(a.reshape(-1), (0, _lane_rows(a.size) * LANE - a.size)).reshape(-1, LANE) for a in arrs]
    rows = sum(p.shape[0] for p in parts)
    if rows % PACK_ROWS:
        parts.append(jnp.zeros((-rows % PACK_ROWS, LANE), F32))
    return jnp.concatenate(parts, axis=0)


def _unpack(buf, like):
    out, row = [], 0
    for a in like:
        n = _lane_rows(a.size)
        out.append(buf[row:row + n].reshape(-1)[:a.size].reshape(a.shape))
        row += n
    return out


def kernel(x, norm_mix, w_in, conv_w, a_log, dt_bias, o_norm_g, ln_v_g, ln_v_b, w_s, b_s, w_out, norm_ffn, w_gate, w_up, w_down, norm_final, loss_target, m_norm_mix, m_w_in, m_conv_w, m_a_log, m_dt_bias, m_o_norm_g, m_ln_v_g, m_ln_v_b, m_w_s, m_b_s, m_w_out, m_norm_ffn, m_w_gate, m_w_up, m_w_down, m_norm_final, v_norm_mix, v_w_in, v_conv_w, v_a_log, v_dt_bias, v_o_norm_g, v_ln_v_g, v_ln_v_b, v_w_s, v_b_s, v_w_out, v_norm_ffn, v_w_gate, v_w_up, v_w_down, v_norm_final):
    w = dict(norm_mix=norm_mix, w_in=w_in, conv_w=conv_w, a_log=a_log, dt_bias=dt_bias, o_norm_g=o_norm_g,
             ln_v_g=ln_v_g, ln_v_b=ln_v_b, w_s=w_s, b_s=b_s, w_out=w_out, norm_ffn=norm_ffn, w_gate=w_gate, w_up=w_up,
             w_down=w_down, norm_final=norm_final)
    m = dict(norm_mix=m_norm_mix, w_in=m_w_in, conv_w=m_conv_w, a_log=m_a_log, dt_bias=m_dt_bias, o_norm_g=m_o_norm_g,
             ln_v_g=m_ln_v_g, ln_v_b=m_ln_v_b, w_s=m_w_s, b_s=m_b_s, w_out=m_w_out, norm_ffn=m_norm_ffn,
             w_gate=m_w_gate, w_up=m_w_up, w_down=m_w_down, norm_final=m_norm_final)
    v = dict(norm_mix=v_norm_mix, w_in=v_w_in, conv_w=v_conv_w, a_log=v_a_log, dt_bias=v_dt_bias, o_norm_g=v_o_norm_g,
             ln_v_g=v_ln_v_g, ln_v_b=v_ln_v_b, w_s=v_w_s, b_s=v_b_s, w_out=v_w_out, norm_ffn=v_norm_ffn,
             w_gate=v_w_gate, w_up=v_w_up, w_down=v_w_down, norm_final=v_norm_final)
    chip = 2 * lax.axis_index("x") + lax.axis_index("y")
    place = jnp.stack([chip, lax.axis_index("c")]).astype(jnp.int32)
    c_arr = place[1:]

    def kernel_view(n, a):
        return jnp.swapaxes(a, 1, 2) if n in ("w_gate", "w_up") else a

    own = {n: [kernel_view(n, w[n])[l].astype(BF16) for l in range(DEPTH)] for n in BIG}
    by_chip = lambda a: jax.ShapeDtypeStruct((NCHIP,) + a.shape, a.dtype)

    def start(name, srcs, whole, after=()):
        return _split_start(name, _gather_plan(whole), srcs, [by_chip(a) for a in srcs], 3 * len(srcs), after)

    def finish(name, started, whole, after):
        srcs, lands = _split_wait(name, _gather_plan(whole), started, after)
        passed = iter(_forward_halves([g for g, all_of_it in zip(lands, whole) if not all_of_it]))
        lands = [g if all_of_it else next(passed) for g, all_of_it in zip(lands, whole)]
        return srcs, [lax.dynamic_update_index_in_dim(g, o, chip, 0) for g, o in zip(lands, srcs)]

    ffn = BIG[1:]
    first = start("gather_first_start", [own["w_in"][0], conv_w], [False, True])
    early = start("gather_early_start", [own[n][0] for n in ffn], [False] * len(ffn), [first["token"]])
    mid = start("gather_mid_start", [own["w_in"][1]], [False], [early["token"]])
    later = start("gather_later_start", [own[n][1] for n in ffn], [False] * len(ffn), [mid["token"]])
    hn = _rmsnorm("rms_mix", x[0], norm_mix[0][None])
    (own_w_in, _), (w_in_by_chip, conv_by_chip) = finish("gather_first_wait", first, [False, True], [later["token"], hn])

    passing = {}

    def pass_on(tag, started, n):
        def at(after):
            srcs, lands = _split_wait(f"gather_{tag}_wait", _gather_plan([False] * n), started, after)
            passing[tag] = srcs, _forward_start(f"forward_{tag}_start", lands, ())
            return [passing[tag][1]["token"]]
        return at

    def passed_on(tag, after):
        srcs, fwd = passing[tag]
        lands = _forward_wait(f"forward_{tag}_wait", fwd, [after])
        return srcs, [lax.dynamic_update_index_in_dim(g, o, chip, 0) for g, o in zip(lands, srcs)]

    def late(tag):
        return lambda after: dict(zip(ffn, passed_on(tag, after)[1]))

    layer0 = _layer_params(0, dict(
        hn=hn, w_in=_assemble_w_in(w_in_by_chip, own_w_in, place), conv_w=conv_by_chip, late=late("early"),
        before_mix=pass_on("early", early, len(ffn)), before_ffn_out=pass_on("mid", mid, 1)), w)

    def layer1(after):
        (own_w_in1,), (w_in1_by_chip,) = passed_on("mid", after)
        return _layer_params(1, dict(w_in=_assemble_w_in(w_in1_by_chip, own_w_in1, place), conv_w=conv_by_chip,
                                     late=late("later"), before_mix=pass_on("later", later, len(ffn))), w)

    saved, layers, loss_lanes, dh, dhb, d_norm_final = _forward(x[0], loss_target[0], [layer0, layer1],
                                                                 norm_final[None])

    sums, arrived = {}, {}

    def exchange_start(tag, l, names, grads, after=()):
        mine = [grads[n] for n in names]
        shapes = [jax.ShapeDtypeStruct((g.shape[0], g.shape[1] // 2, g.shape[2]), F32) for g in mine]
        return tag, l, names, _split_start(f"exchange_{tag}_start", _exchange_plan, mine, shapes, len(mine), after)

    def add_halves(l, names, mine, theirs):
        for n, g, t in zip(names, mine, theirs):
            sums[l, n] = (_sum_halves_w_in if n == "w_in" else _sum_halves)(g, t, c_arr)

    def exchange_wait(handle, after):
        tag, l, names, started = handle
        add_halves(l, names, *_split_wait(f"exchange_{tag}_wait", _exchange_plan, started, after))

    def scatter_start(tag, l, names, after=()):
        partial = [sums[l, n][1] for n in names]
        shapes = [jax.ShapeDtypeStruct((3,) + p.shape[1:], p.dtype) for p in partial]
        return tag, l, names, _split_start(f"scatter_{tag}_start", _scatter_plan, partial, shapes, 3 * len(names), after)

    def scatter_wait(handle, after):
        tag, l, names, started = handle
        for n, q in zip(names, _split_wait(f"scatter_{tag}_wait", _scatter_plan, started, after)[1]):
            arrived[l, n] = q

    last = DEPTH - 1
    swiglu = BIG[2:]
    dh1, dh1b, g_ffn = _layer_bwd_ffn(dh, dhb, layers[last], saved[last])
    dh, dhb, g_mix = _layer_bwd_mixer(dh1, dh1b, layers[last], saved[last])
    gl = [None, _reference_layout({**g_ffn, **g_mix})]
    ex_last = exchange_start("last", last, BIG, gl[last])
    dh1, dh1b, g_ffn = _layer_bwd_ffn(dh, dhb, layers[0], saved[0], after=[ex_last[-1]["token"]])
    exchange_wait(ex_last, dh1)
    sc_last = scatter_start("last", last, BIG)
    ex_ffn = exchange_start("swiglu", 0, swiglu, g_ffn, [sc_last[-1]["token"]])
    sc_ffn = []

    def midway(do):
        exchange_wait(ex_ffn, do)
        sc_ffn.append(scatter_start("swiglu", 0, swiglu))
        return [sc_ffn[0][-1]["token"]]

    ex_rest = []

    def late(grads):
        rest_grads = dict(w_in=grads["w_in"], w_out=grads["w_out"].reshape(NCHIP, D // NCHIP, D))
        ex_rest.append(exchange_start("rest", 0, BIG[:2], rest_grads))
        return [ex_rest[0][-1]["token"]]

    dx, _, g_mix = _layer_bwd_mixer(dh1, dh1b, layers[0], saved[0], after=[ex_ffn[-1]["token"]], midway=midway,
                                    late=late)
    scatter_wait(sc_last, dx)
    scatter_wait(sc_ffn[0], dx)
    gl[0] = _reference_layout({**g_ffn, **g_mix})

    small_g = [jnp.stack([gl[l][n] for l in range(DEPTH)]) for n in SMALL[:-1]] + [d_norm_final[0]]
    conv_g = jnp.stack([gl[l]["conv_w"] for l in range(DEPTH)])
    summed = small_g + [conv_g, loss_lanes[0, :1]]
    total = _allreduce_small(_pack(summed))
    exchange_wait(ex_rest[0], total)
    sc_rest = scatter_start("rest", 0, BIG[:2])

    travelling = [sc_rest[-1]["token"]]
    reduced, g_out, delta, new_m, new_v = {}, {}, {}, {}, {}

    done = []

    def adamw_large(names, joined):
        for n, g in zip(names, joined):
            res = _adamw(kernel_view(n, w[n]), g, kernel_view(n, m[n]), kernel_view(n, v[n]))
            done.append(res[2])
            g_out[n], delta[n], new_m[n], new_v[n] = (kernel_view(n, a) for a in (g,) + tuple(res))

    for n in BIG:
        for l in (range(DEPTH) if n in swiglu else [last]):
            reduced[n] = _sum_chips(sums[l, n][0], arrived[l, n], place, l, into=reduced.get(n), after=travelling)
    adamw_large(swiglu, _join_halves("join_swiglu", [reduced[n] for n in swiglu]))
    scatter_wait(sc_rest, done + [reduced[n] for n in BIG[:2]])
    for n in BIG[:2]:
        reduced[n] = _sum_chips(sums[0, n][0], arrived[0, n], place, 0, into=reduced[n])
    adamw_large(BIG[:2], _join_halves("join_rest", [reduced[n] for n in BIG[:2]]))
    *small_r, conv_r, loss = _unpack(total, summed)
    g_out.update(zip(SMALL, small_r))
    g_out["conv_w"] = lax.dynamic_slice_in_dim(conv_r, chip * conv_w.shape[2], conv_w.shape[2], axis=2)

    rest = SMALL + ("conv_w",)
    rows_of = lambda a: a.reshape(1, -1) if a.ndim == 1 else a
    results = _adamw_small(*[[rows_of(src[n]) for n in rest] for src in (w, g_out, m, v)])
    for dst, arrs in zip((delta, new_m, new_v), results):
        dst.update({n: a.reshape(w[n].shape) for n, a in zip(rest, arrs)})

    return (loss[0], dx[None], *[g_out[n] for n in ORDER], *[delta[n] for n in ORDER], *[new_m[n] for n in ORDER],
            *[new_v[n] for n in ORDER])
```

```python
import functools

import jax
import jax.numpy as jnp
from jax import lax
from jax.experimental import pallas as pl
from jax.experimental.pallas import tpu as pltpu

F32 = jnp.float32
BF16 = jnp.bfloat16
MESH = pl.DeviceIdType.MESH
ANY = pl.BlockSpec(memory_space=pl.ANY)
HIGHEST = lax.Precision.HIGHEST

T = 2048
D = 1024
DEPTH = 2
NCHIP = 4
HEADS = 4
HD = 128
HW = HEADS * HD
CH = 64
GCH = 128
IN_DIM = 3080
NP = 3200
BA_OFF = 3072
FF_SH = 704
EPS = 1e-6
LANE = 128
VMEM_LIMIT = 56 * 1024 * 1024

ADAM_LR = 0.001
ADAM_B1 = 0.9
ADAM_B2 = 0.999
ADAM_EPS = 1e-08
ADAM_WD = 0.01
ADAM_STEP = 10


def _cparams(sem=None):
    return pltpu.CompilerParams(dimension_semantics=sem, vmem_limit_bytes=VMEM_LIMIT)


_DIMS = {"nn": (((1,), (0,)), ((), ())), "nt": (((1,), (1,)), ((), ())), "tn": (((0,), (0,)), ((), ()))}


def _mm(name, mode, a, bs, *, tm, tn, tk, out_dtypes=(F32,), reduce_g=False, resid=None, extras=(), epilogue=None,
        after=(), fold_g=False, sub_m=1):
    assert sub_m == 1 or (mode != "tn" and tm % (8 * sub_m) == 0), (name, sub_m)
    nb = len(bs)
    ga = a.shape[0]
    gbs = [b.shape[0] for b in bs]
    g_n = max([ga] + gbs)
    if mode == "tn":
        k_n, m_n = a.shape[1:]
    else:
        m_n, k_n = a.shape[1:]
    n_n = bs[0].shape[1] if mode == "nt" else bs[0].shape[2]
    assert m_n % tm == 0 and n_n % tn == 0 and k_n % tk == 0, (name, m_n, n_n, k_n)
    mi, nj, kk = m_n // tm, n_n // tn, k_n // tk
    lead = g_n if fold_g else None
    g_steps = 1 if fold_g else g_n
    grid = (mi, nj, g_steps, kk)
    ids = lambda i, j, g, k: (g, i, j, k)
    n_red = (g_steps if reduce_g else 1) * kk
    red_idx = lambda: (pl.program_id(2) * kk if reduce_g else 0) + pl.program_id(3)
    sem = ("parallel", "parallel", "arbitrary" if reduce_g else "parallel", "arbitrary")

    def pick(gsz, g):
        return g if gsz > 1 else 0

    def a_map(*p):
        g, i, j, k = ids(*p)
        return (pick(ga, g), k, i) if mode == "tn" else (pick(ga, g), i, k)

    def b_map(gsz):
        def f(*p):
            g, i, j, k = ids(*p)
            return (pick(gsz, g), j, k) if mode == "nt" else (pick(gsz, g), k, j)
        return f

    def o_map(gsz):
        def f(*p):
            g, i, j, k = ids(*p)
            return (0 if reduce_g else pick(gsz, g), i, j)
        return f

    a_spec = pl.BlockSpec((lead, tk, tm) if mode == "tn" else (lead, tm, tk), a_map)
    b_specs = [pl.BlockSpec((lead, tn, tk) if mode == "nt" else (lead, tk, tn), b_map(gs)) for gs in gbs]
    x_specs = [pl.BlockSpec((None, tm, tn), o_map(e.shape[0])) for e in extras]
    r_specs = [pl.BlockSpec((None, tm, tn), o_map(resid.shape[0]))] if resid is not None else []
    g_out = 1 if reduce_g else g_n
    out_shape = [jax.ShapeDtypeStruct((g_out, m_n, n_n), dt) for dt in out_dtypes]
    out_specs = [pl.BlockSpec((None, tm, tn), o_map(g_out)) for _ in out_dtypes]
    nx, nr, no = len(extras), len(r_specs), len(out_dtypes)
    n_in = 1 + nb + nx + nr + len(after)
    dims = _DIMS[mode]

    def body(*refs):
        a_ref = refs[0]
        b_refs = refs[1:1 + nb]
        x_refs = refs[1 + nb:1 + nb + nx]
        r_refs = refs[1 + nb + nx:1 + nb + nx + nr]
        o_refs = refs[n_in:n_in + no]
        acc_refs = refs[n_in + no:]
        def dots(rows):
            if fold_g:
                return [sum(lax.dot_general(a_ref[g, rows, :], b_ref[g], dims, preferred_element_type=F32)
                            for g in range(g_n)) for b_ref in b_refs]
            av = a_ref[...] if mode == "tn" else a_ref[rows, :]
            return [lax.dot_general(av, b_ref[...], dims, preferred_element_type=F32) for b_ref in b_refs]

        def finish(accs, rows=slice(None)):
            if r_refs:
                accs[0] = accs[0] + r_refs[0][rows, :]
            outs = epilogue(accs, [x[rows, :] for x in x_refs]) if epilogue is not None else accs
            for o_ref, o in zip(o_refs, outs):
                o_ref[rows, :] = o.astype(o_ref.dtype)

        if n_red == 1:
            slabs = [slice(s * (tm // sub_m), (s + 1) * (tm // sub_m)) for s in range(sub_m)]
            ahead = dots(slabs[0])
            for s, rows in enumerate(slabs):
                now, ahead = ahead, (dots(slabs[s + 1]) if s + 1 < sub_m else None)
                finish(now, rows)
            return
        products = dots(slice(None))
        r = red_idx()
        for p, acc in zip(products, acc_refs):
            @pl.when(r == 0)
            def _():
                acc[...] = p

            @pl.when((r > 0) & (r < n_red - 1))
            def _():
                acc[...] += p

        @pl.when(r == n_red - 1)
        def _():
            finish([acc[...] + p for p, acc in zip(products, acc_refs)])

    return pl.pallas_call(
        body, name=name, grid=grid,
        in_specs=[a_spec] + b_specs + x_specs + r_specs + [ANY] * len(after),
        out_specs=out_specs, out_shape=out_shape,
        scratch_shapes=[pltpu.VMEM((tm, tn), F32) for _ in range(nb if n_red > 1 else 0)],
        compiler_params=_cparams(sem),
    )(a, *bs, *extras, *([resid] if resid is not None else []), *after)


def _sigmoid(x):
    return 1.0 / (1.0 + jnp.exp(-x))


def _silu(x):
    return x * _sigmoid(x)


def _gelu(x):
    return 0.5 * x * (1.0 + jnp.tanh(0.7978845608028654 * (x + 0.044715 * (x * x * x))))


def _rms_fn(h, gain):
    return h * lax.rsqrt(jnp.mean(h * h, axis=-1, keepdims=True) + EPS) * gain


def _shift_impl(x, s):
    n = x.shape[0]
    rolled = pltpu.roll(x, s % n, 0)
    row = lax.broadcasted_iota(jnp.int32, x.shape, 0)
    return jnp.where((row >= s) & (row < n + s), rolled, 0.0)


@functools.partial(jax.custom_vjp, nondiff_argnums=(1,))
def _shift(x, s):
    return _shift_impl(x, s)


def _shift_fwd(x, s):
    return _shift_impl(x, s), None


def _shift_bwd(s, _, g):
    return (_shift_impl(g, -s),)


_shift.defvjp(_shift_fwd, _shift_bwd)


def _prep_fn(x, w, qk_scale, is_v):
    y = x * w[3:4, :]
    for i in range(3):
        y = y + _shift(x, 3 - i) * w[i:i + 1, :]
    y = _silu(y)
    nrm = lax.rsqrt(jnp.sum(y * y, axis=-1, keepdims=True) + EPS) * qk_scale
    return y * jnp.where(is_v, 1.0, nrm)


def _softplus(x):
    return jnp.maximum(x, 0.0) + jnp.log(1.0 + jnp.exp(-jnp.abs(x)))


def _gates_fn(ba, a_log, dt_bias):
    lane = lax.broadcasted_iota(jnp.int32, ba.shape, 1)
    beta = _sigmoid(ba)
    g = -jnp.exp(a_log) * _softplus(ba + dt_bias)
    return jnp.where(lane < HEADS, beta, g)


def _dot16(a, b, dims=_DIMS["nn"]):
    return lax.dot_general(a.astype(BF16), b.astype(BF16), dims, preferred_element_type=F32)


def _dot32(a, b):
    return jnp.dot(a, b, preferred_element_type=F32, precision=HIGHEST)


def _dot3(a, b, dims=_DIMS["nn"]):
    return lax.dot_general(a, b, dims, preferred_element_type=F32, precision=lax.Precision.HIGH)


def _tri_inverses(mats, tick=lambda: None):
    row = lax.broadcasted_iota(jnp.int32, (CH, CH), 0)
    col = lax.broadcasted_iota(jnp.int32, (CH, CH), 1)
    eye = (row == col).astype(F32)
    ts = [eye - a for a in mats]
    ps = list(mats)
    for _ in range(5):
        ps = [_dot3(p, p) for p in ps]
        tick()
        ts = [t + _dot3(t, p) for t, p in zip(ts, ps)]
        tick()
    return ts


@jax.custom_vjp
def _tri_solves(mats, rhs):
    return [_dot3(t, b) for t, b in zip(_tri_inverses(mats), rhs)]


def _tri_solves_fwd(mats, rhs):
    ts = _tri_inverses(mats)
    xs = [_dot3(t, b) for t, b in zip(ts, rhs)]
    return xs, (ts, xs)


def _tri_solves_bwd(res, dxs):
    ts, xs = res
    dbs = [_dot3(t, dx, _DIMS["tn"]) for t, dx in zip(ts, dxs)]
    return [-_dot3(db, x, _DIMS["nt"]) for db, x in zip(dbs, xs)], dbs


_tri_solves.defvjp(_tri_solves_fwd, _tri_solves_bwd)


def _chunk_prep_fn(xs, bgs, tick=None):
    step = tick or (lambda: None)
    row = lax.broadcasted_iota(jnp.int32, (CH, CH), 0)
    col = lax.broadcasted_iota(jnp.int32, (CH, CH), 1)
    incl = row >= col
    strict = row > col
    lmat = incl.astype(F32)
    n = len(xs)
    items = [(i, h) for i in range(n) for h in range(HEADS)]
    part = lambda i, h, c: xs[i][:, c * HW + h * HD:c * HW + (h + 1) * HD]
    q = [part(i, h, 0) for i, h in items]
    k = [part(i, h, 1) for i, h in items]
    v = [part(i, h, 2) for i, h in items]
    beta = [bgs[i][:, h:h + 1] for i, h in items]
    gc_all = [_dot32(lmat, bg) for bg in bgs]
    step()
    gc = [gc_all[i][:, HEADS + h:HEADS + h + 1] for i, h in items]
    gmat = [jnp.where(strict, jnp.broadcast_to(bgs[i][:, HEADS + h:HEADS + h + 1], (CH, CH)), 0.0) for i, h in items]
    diff = [_dot3(lmat, m) for m in gmat]
    step()
    decay = [jnp.where(incl, jnp.exp(jnp.where(incl, d, 0.0)), 0.0) for d in diff]
    k_beta = [kk * b for kk, b in zip(k, beta)]
    kk_t = [_dot16(kb, kk, _DIMS["nt"]) for kb, kk in zip(k_beta, k)]
    step()
    qk_t = [_dot16(qq, kk, _DIMS["nt"]) for qq, kk in zip(q, k)]
    step()
    a = [jnp.where(strict, m * d, 0.0) for m, d in zip(kk_t, decay)]
    eg = [jnp.exp(g) for g in gc]
    rhs = [jnp.concatenate([vv * b, kb * e], axis=-1) for vv, b, kb, e in zip(v, beta, k_beta, eg)]
    if tick is None:
        uw = _tri_solves(a, rhs)
    else:
        uw = [_dot3(t, b) for t, b in zip(_tri_inverses(a, tick), rhs)]
    qk = [m * d for m, d in zip(qk_t, decay)]
    g_last = [g[CH - 1:CH, :] for g in gc]
    qe = [qq * e for qq, e in zip(q, eg)]
    kd = [kk * jnp.exp(gl - g) for kk, gl, g in zip(k, g_last, gc)]
    egl = [jnp.broadcast_to(jnp.exp(gl), (1, HD)) for gl in g_last]
    out = []
    for i in range(n):
        mine = slice(i * HEADS, (i + 1) * HEADS)
        cat = lambda vals: jnp.concatenate(vals[mine], axis=-1)
        out.append((cat([x[:, :HD] for x in uw]), cat([x[:, HD:] for x in uw]), cat(qe), cat(kd),
                    jnp.concatenate([m[None] for m in qk[mine]], axis=0), cat(egl)))
    return out


def _state_levels(chunks, s, outs, befores, final):
    for u, w, qe, kd, qk, egl in chunks:
        befores.append(s)
        ws = [_dot16(a, b) for a, b in zip(w, s)]
        qs = [_dot16(a, b) for a, b in zip(qe, s)]
        yield
        v_new = [a - b for a, b in zip(u, ws)]
        outs.append([a + _dot16(b, c) for a, b, c in zip(qs, qk, v_new)])
        s = [a * e + _dot16(b, c, _DIMS["tn"]) for a, e, b, c in zip(s, egl, kd, v_new)]
        yield
    final.append(s)


def _chunk_state_fn(u, w, qe, kd, qk, egl, s):
    ws = [_dot16(a, b) for a, b in zip(w, s)]
    qs = [_dot16(a, b) for a, b in zip(qe, s)]
    v_new = [a - b for a, b in zip(u, ws)]
    o = [a + _dot16(b, c) for a, b, c in zip(qs, qk, v_new)]
    s_new = [a * e + _dot16(b, c, _DIMS["tn"]) for a, e, b, c in zip(s, egl, kd, v_new)]
    return o, s_new


def _mix_fn(o, z, ur, vr, ong, lng, lnb, ws, bst):
    row = lax.broadcasted_iota(jnp.int32, (GCH, GCH), 0)
    col = lax.broadcasted_iota(jnp.int32, (GCH, GCH), 1)
    causal = row >= col
    ug = _gelu(ur)
    vg = _gelu(vr)
    sls = [slice(h * HD, (h + 1) * HD) for h in range(HEADS)]
    oh = [o[:, sl] for sl in sls]
    oh = [x * lax.rsqrt(jnp.mean(x * x, axis=-1, keepdims=True) + EPS) for x in oh]
    outs_dn = [x * ong * _silu(z[:, sl]) for x, sl in zip(oh, sls)]
    vh = [vg[:, sl] for sl in sls]
    mu = [jnp.mean(x, axis=-1, keepdims=True) for x in vh]
    var = [jnp.mean(jnp.square(x - m), axis=-1, keepdims=True) for x, m in zip(vh, mu)]
    vn = [(x - m) * lax.rsqrt(s + EPS) * lng[:, sl] + lnb[:, sl] for x, m, s, sl in zip(vh, mu, var, sls)]
    mixed = [_dot16(jnp.where(causal, ws[h], 0.0), vn[h]) for h in range(HEADS)]
    outs_gm = [ug[:, sl] * (mixed[h] + bst[:, h:h + 1]) for h, sl in enumerate(sls)]
    return jnp.concatenate(outs_dn + outs_gm, axis=-1)


def _loss_fn(h, gain, tgt):
    y = _rms_fn(h, gain)
    return 0.5 * jnp.sum(jnp.mean(jnp.square(y - tgt), axis=-1))


RT = 512


def _rows(n=D):
    return pl.BlockSpec((RT, n), lambda i: (i, 0))


def _whole(shape):
    nd = len(shape)
    return pl.BlockSpec(shape, lambda i: (0,) * nd)


def _rmsnorm(name, h, gain):
    def body(h_ref, g_ref, o_ref):
        o_ref[...] = _rms_fn(h_ref[...], g_ref[...]).astype(BF16)

    return pl.pallas_call(
        body, name=name, grid=(T // RT,), in_specs=[_rows(), _whole((1, D))], out_specs=_rows(),
        out_shape=jax.ShapeDtypeStruct((T, D), BF16), compiler_params=_cparams(("parallel",)),
    )(h, gain)


def _rmsnorm_bwd(name, dhn, h, gain, resid):
    def body(dhn_ref, h_ref, g_ref, r_ref, dh_ref, dh16_ref, dg_ref):
        _, vjp = jax.vjp(_rms_fn, h_ref[...], g_ref[...])
        dh, dg = vjp(dhn_ref[...])
        dh = r_ref[...] + dh
        dh_ref[...] = dh
        dh16_ref[...] = dh.astype(BF16)

        @pl.when(pl.program_id(0) == 0)
        def _():
            dg_ref[...] = dg

        @pl.when(pl.program_id(0) > 0)
        def _():
            dg_ref[...] += dg

    return pl.pallas_call(
        body, name=name, grid=(T // RT,), in_specs=[_rows(), _rows(), _whole((1, D)), _rows()],
        out_specs=[_rows(), _rows(), _whole((1, D))],
        out_shape=[jax.ShapeDtypeStruct((T, D), F32), jax.ShapeDtypeStruct((T, D), BF16),
                   jax.ShapeDtypeStruct((1, D), F32)],
        compiler_params=_cparams(("arbitrary",)),
    )(dhn, h, gain, resid)


def _loss_head(h, gain, tgt):
    def body(h_ref, g_ref, t_ref, l_ref, dh_ref, dh16_ref, dg_ref):
        loss, vjp = jax.vjp(lambda hh, gg: _loss_fn(hh, gg, t_ref[...]), h_ref[...], g_ref[...])
        dh, dg = vjp(jnp.ones((), F32))
        dh_ref[...] = dh
        dh16_ref[...] = dh.astype(BF16)
        lv = jnp.full((1, LANE), loss, F32)

        @pl.when(pl.program_id(0) == 0)
        def _():
            dg_ref[...] = dg
            l_ref[...] = lv

        @pl.when(pl.program_id(0) > 0)
        def _():
            dg_ref[...] += dg
            l_ref[...] += lv

    return pl.pallas_call(
        body, name="loss_head", grid=(T // RT,), in_specs=[_rows(), _whole((1, D)), _rows()],
        out_specs=[_whole((1, LANE)), _rows(), _rows(), _whole((1, D))],
        out_shape=[jax.ShapeDtypeStruct((1, LANE), F32), jax.ShapeDtypeStruct((T, D), F32),
                   jax.ShapeDtypeStruct((T, D), BF16), jax.ShapeDtypeStruct((1, D), F32)],
        compiler_params=_cparams(("arbitrary",)),
    )(h, gain, tgt)


def _prep_flags():
    j = pl.program_id(0)
    qk_scale = jnp.where(j < HEADS, HD ** -0.5, 1.0).astype(F32)
    return qk_scale, j >= 2 * HEADS


def _prep(proj, conv_w):
    def body(x_ref, w_ref, o_ref):
        qk_scale, is_v = _prep_flags()
        o_ref[...] = _prep_fn(x_ref[...], w_ref[...], qk_scale, is_v)

    col = lambda j: (0, j)
    return pl.pallas_call(
        body, name="gdn_prep", grid=(3 * HEADS,),
        in_specs=[pl.BlockSpec((T, HD), col), pl.BlockSpec((4, HD), col)], out_specs=pl.BlockSpec((T, HD), col),
        out_shape=jax.ShapeDtypeStruct((T, 3 * HW), F32), compiler_params=_cparams(("parallel",)),
    )(proj, conv_w)


def _prep_bwd(proj, conv_w, dqkv, dproj):
    def body(x_ref, w_ref, d_ref, _, dx_ref, dw_ref):
        qk_scale, is_v = _prep_flags()
        _, vjp = jax.vjp(lambda x, w: _prep_fn(x, w, qk_scale, is_v), x_ref[...], w_ref[...])
        dx, dw = vjp(d_ref[...])
        dx_ref[...] = dx.astype(BF16)
        dw_ref[...] = dw

    col = lambda j: (0, j)
    return pl.pallas_call(
        body, name="gdn_prep_bwd", grid=(3 * HEADS,),
        in_specs=[pl.BlockSpec((T, HD), col), pl.BlockSpec((4, HD), col), pl.BlockSpec((T, HD), col), ANY],
        out_specs=[pl.BlockSpec((T, HD), col), pl.BlockSpec((4, HD), col)],
        out_shape=[jax.ShapeDtypeStruct((T, NP), BF16), jax.ShapeDtypeStruct((4, 3 * HW), F32)],
        input_output_aliases={3: 0}, compiler_params=_cparams(("parallel",)),
    )(proj, conv_w, dqkv, dproj)


BA_BLK = BA_OFF // LANE


def _gates(proj, a_log, dt_bias):
    def body(x_ref, a_ref, d_ref, o_ref):
        o_ref[...] = _gates_fn(x_ref[...], a_ref[...], d_ref[...])

    return pl.pallas_call(
        body, name="gdn_gates", grid=(1,),
        in_specs=[pl.BlockSpec((T, LANE), lambda i: (0, BA_BLK)), _whole((1, LANE)), _whole((1, LANE))],
        out_specs=_whole((T, LANE)),
        out_shape=jax.ShapeDtypeStruct((T, LANE), F32), compiler_params=_cparams(("arbitrary",)),
    )(proj, a_log, dt_bias)


def _gates_bwd(proj, a_log, dt_bias, dbg, dproj):
    def body(x_ref, a_ref, d_ref, dbg_ref, _, dx_ref, da_ref, dd_ref):
        _, vjp = jax.vjp(_gates_fn, x_ref[...], a_ref[...], d_ref[...])
        dx, da_ref[...], dd_ref[...] = vjp(dbg_ref[...])
        dx_ref[...] = dx.astype(BF16)

    ba = pl.BlockSpec((T, LANE), lambda i: (0, BA_BLK))
    return pl.pallas_call(
        body, name="gdn_gates_bwd", grid=(1,),
        in_specs=[ba, _whole((1, LANE)), _whole((1, LANE)), _whole((T, LANE)), ANY],
        out_specs=[ba, _whole((1, LANE)), _whole((1, LANE))],
        out_shape=[jax.ShapeDtypeStruct((T, NP), BF16), jax.ShapeDtypeStruct((1, LANE), F32),
                   jax.ShapeDtypeStruct((1, LANE), F32)],
        input_output_aliases={4: 0}, compiler_params=_cparams(("arbitrary",)),
    )(proj, a_log, dt_bias, dbg, dproj)


NCK = T // CH
CPS = 4


def _chunk_group_specs(at):
    wide = pl.BlockSpec((CPS * CH, HW), lambda n: (at(n), 0))
    return [wide, wide, wide, wide, pl.BlockSpec((HEADS, CPS * CH, CH), lambda n: (0, at(n), 0)),
            pl.BlockSpec((CPS, 1, HW), lambda n: (at(n), 0, 0))]


def _chunk_prep_shapes(dtypes):
    shp = [(T, HW), (T, HW), (T, HW), (T, HW), (HEADS, T, CH), (NCK, 1, HW)]
    return [jax.ShapeDtypeStruct(s, dt) for s, dt in zip(shp, dtypes)]


NGROUP = NCK // CPS
PREP_DTYPES = (F32, BF16, BF16, BF16, BF16, F32)


def _delta_rule(qkv, bg):
    def body(x_ref, bg_ref, *refs):
        prep_out, (o_ref, sh_ref), held, s_ref = refs[:6], refs[6:8], refs[8:14], refs[14]
        i = pl.program_id(0)

        @pl.when(i == 0)
        def _():
            for r in held + (s_ref,):
                r[...] = jnp.zeros_like(r)

        rows = [slice(ci * CH, (ci + 1) * CH) for ci in range(CPS)]
        u_h, w_h, qe_h, kd_h, qk_h, egl_h = held
        chunks = [_head_args((u_h.at[r, :], w_h.at[r, :], qe_h.at[r, :], kd_h.at[r, :], qk_h.at[:, r, :], egl_h.at[ci]))
                  for ci, r in enumerate(rows)]
        start = [jnp.where(i <= 1, 0.0, s_ref[h]) for h in range(HEADS)]
        outs, befores, final = [], [], []
        levels = _state_levels(chunks, start, outs, befores, final)
        res = _chunk_prep_fn([x_ref[r, :] for r in rows], [bg_ref[r, :] for r in rows], tick=lambda: next(levels, None))
        for _ in levels:
            pass
        for ci, (u, w, qe, kd, qk, egl) in enumerate(res):
            for refs_pair, val in zip(zip(prep_out[:4], held[:4]), (u, w, qe, kd)):
                for ref in refs_pair:
                    ref[rows[ci], :] = val.astype(ref.dtype)
            for ref in (prep_out[4], qk_h):
                ref[:, rows[ci], :] = qk.astype(ref.dtype)
            for ref in (prep_out[5], egl_h):
                ref[ci] = egl
        for ci, r in enumerate(rows):
            for h in range(HEADS):
                o_ref[r, h * HD:(h + 1) * HD] = outs[ci][h]
                sh_ref[h, ci] = befores[ci][h]
        for h in range(HEADS):
            s_ref[h] = final[0][h]

    now = lambda n: jnp.minimum(n, NGROUP - 1)
    was = lambda n: jnp.maximum(n - 1, 0)
    wide = lambda at: pl.BlockSpec((CPS * CH, HW), lambda n: (at(n), 0))
    held = [pltpu.VMEM(s, dt) for s, dt in zip(
        [(CPS * CH, HW)] * 4 + [(HEADS, CPS * CH, CH), (CPS, 1, HW)], PREP_DTYPES)]
    out = pl.pallas_call(
        body, name="gdn_delta_rule", grid=(NGROUP + 1,),
        in_specs=[pl.BlockSpec((CPS * CH, 3 * HW), lambda n: (now(n), 0)),
                  pl.BlockSpec((CPS * CH, LANE), lambda n: (now(n), 0))],
        out_specs=[wide(now)] * 4 + [pl.BlockSpec((HEADS, CPS * CH, CH), lambda n: (0, now(n), 0)),
                                     pl.BlockSpec((CPS, 1, HW), lambda n: (now(n), 0, 0)), wide(was),
                                     pl.BlockSpec((HEADS, CPS, HD, HD), lambda n: (0, was(n), 0, 0))],
        out_shape=_chunk_prep_shapes(PREP_DTYPES) + [jax.ShapeDtypeStruct((T, HW), F32),
                                                     jax.ShapeDtypeStruct((HEADS, NCK, HD, HD), F32)],
        scratch_shapes=held + [pltpu.VMEM((HEADS, HD, HD), F32)], compiler_params=_cparams(("arbitrary",)),
    )(qkv, bg)
    return out[:6], out[6], out[7]


def _chunk_prep_bwd(qkv, bg, cots):
    def body(x_ref, bg_ref, du, dw, dqe, dkd, dqk, degl, dx_ref, dbg_ref):
        rows = [slice(ci * CH, (ci + 1) * CH) for ci in range(CPS)]
        _, vjp = jax.vjp(_chunk_prep_fn, [x_ref[r, :] for r in rows], [bg_ref[r, :] for r in rows])
        dxs, dbgs = vjp([(du[r, :], dw[r, :], dqe[r, :], dkd[r, :], dqk[:, r, :], degl[ci])
                         for ci, r in enumerate(rows)])
        for r, dx, dbg in zip(rows, dxs, dbgs):
            dx_ref[r, :] = dx
            dbg_ref[r, :] = dbg

    wide = pl.BlockSpec((CPS * CH, HW), lambda n: (n, 0))
    return pl.pallas_call(
        body, name="gdn_chunk_prep_bwd", grid=(NCK // CPS,),
        in_specs=[pl.BlockSpec((CPS * CH, 3 * HW), lambda n: (n, 0)), pl.BlockSpec((CPS * CH, LANE), lambda n: (n, 0)),
                  wide, wide, wide, wide, pl.BlockSpec((HEADS, CPS * CH, CH), lambda n: (0, n, 0)),
                  pl.BlockSpec((CPS, 1, HW), lambda n: (n, 0, 0))],
        out_specs=[pl.BlockSpec((CPS * CH, 3 * HW), lambda n: (n, 0)), pl.BlockSpec((CPS * CH, LANE), lambda n: (n, 0))],
        out_shape=[jax.ShapeDtypeStruct((T, 3 * HW), F32), jax.ShapeDtypeStruct((T, LANE), F32)],
        compiler_params=_cparams(("parallel",)),
    )(qkv, bg, *cots)


def _head_args(refs):
    u, w, qe, kd, qk, egl = refs
    sls = [slice(h * HD, (h + 1) * HD) for h in range(HEADS)]
    return ([u[:, sl] for sl in sls], [w[:, sl].astype(F32) for sl in sls], [qe[:, sl].astype(F32) for sl in sls],
            [kd[:, sl].astype(F32) for sl in sls], [qk[h].astype(F32) for h in range(HEADS)],
            [egl[:, sl] for sl in sls])


def _chunk_scan_bwd(prep, s_hist, do, after=()):
    n_in = 8 + len(after)

    def body(*refs):
        u_r, w_r, qe_r, kd_r, qk_r, egl_r, sh_ref, do_ref = refs[:8]
        d_refs = refs[n_in:n_in + 6]
        ds_ref = refs[n_in + 6]

        @pl.when(pl.program_id(0) == 0)
        def _():
            ds_ref[...] = jnp.zeros_like(ds_ref)

        sls = [slice(h * HD, (h + 1) * HD) for h in range(HEADS)]
        ds = [ds_ref[h] for h in range(HEADS)]
        for ci in reversed(range(CPS)):
            r = slice(ci * CH, (ci + 1) * CH)
            args = _head_args((u_r.at[r, :], w_r.at[r, :], qe_r.at[r, :], kd_r.at[r, :], qk_r.at[:, r, :], egl_r.at[ci]))
            _, vjp = jax.vjp(_chunk_state_fn, *args, [sh_ref[h, ci] for h in range(HEADS)])
            du, dw, dqe, dkd, dqk, degl, ds = vjp(([do_ref[r, sl] for sl in sls], ds))
            for h, sl in enumerate(sls):
                for d_ref, val in zip(d_refs[:4], (du, dw, dqe, dkd)):
                    d_ref[r, sl] = val[h]
                d_refs[4][h, r, :] = dqk[h]
                d_refs[5][ci, :, sl] = degl[h]
        for h in range(HEADS):
            ds_ref[h] = ds[h]

    rev = lambda n: NGROUP - 1 - n
    return pl.pallas_call(
        body, name="gdn_scan_bwd", grid=(NGROUP,),
        in_specs=_chunk_group_specs(rev) + [pl.BlockSpec((HEADS, CPS, HD, HD), lambda n: (0, rev(n), 0, 0)),
                                            pl.BlockSpec((CPS * CH, HW), lambda n: (rev(n), 0))] + [ANY] * len(after),
        out_specs=_chunk_group_specs(rev), out_shape=_chunk_prep_shapes((F32,) * 6),
        scratch_shapes=[pltpu.VMEM((HEADS, HD, HD), F32)], compiler_params=_cparams(("arbitrary",)),
    )(*prep, s_hist, do, *after)


MIX_ROWS = 4 * GCH


def _mix_rows(o, z, ur, vr, *params):
    chunks = [slice(c * GCH, (c + 1) * GCH) for c in range(MIX_ROWS // GCH)]
    return jnp.concatenate([_mix_fn(o[r], z[r], ur[r], vr[r], *params) for r in chunks], axis=0)


def _mix_specs():
    pc = lambda c: pl.BlockSpec((MIX_ROWS, HW), lambda i: (i, c))
    return [pl.BlockSpec((MIX_ROWS, HW), lambda i: (i, 0)), pc(3), pc(4), pc(5), _whole((1, HD)), _whole((1, HW)),
            _whole((1, HW)), _whole((HEADS, GCH, GCH)), _whole((GCH, LANE))]


def _mix(o, proj, ong, lng, lnb, ws, bst, after=()):
    def body(o_ref, z_ref, u_ref, v_ref, ong_ref, lng_ref, lnb_ref, ws_ref, bs_ref, *rest):
        rest[-1][...] = _mix_rows(o_ref[...], z_ref[...], u_ref[...], v_ref[...], ong_ref[...], lng_ref[...],
                                  lnb_ref[...], ws_ref[...], bs_ref[...]).astype(BF16)

    return pl.pallas_call(
        body, name="mix", grid=(T // MIX_ROWS,), in_specs=_mix_specs() + [ANY] * len(after),
        out_specs=pl.BlockSpec((MIX_ROWS, D), lambda i: (i, 0)), out_shape=jax.ShapeDtypeStruct((T, D), BF16),
        compiler_params=_cparams(("parallel",)),
    )(o, proj, proj, proj, ong, lng, lnb, ws, bst, *after)


def _mix_bwd(o, proj, ong, lng, lnb, ws, bst, dmix):
    def body(o_ref, z_ref, u_ref, v_ref, ong_ref, lng_ref, lnb_ref, ws_ref, bs_ref, dm_ref,
             do_ref, dzuv_ref, dong_ref, dlng_ref, dlnb_ref, dws_ref, dbs_ref):
        _, vjp = jax.vjp(_mix_rows, o_ref[...], z_ref[...], u_ref[...], v_ref[...], ong_ref[...], lng_ref[...],
                         lnb_ref[...], ws_ref[...], bs_ref[...])
        do, dz, du, dv, dong, dlng, dlnb, dws, dbs = vjp(dm_ref[...])
        do_ref[...] = do
        dzuv_ref[:, 0:HW] = dz.astype(BF16)
        dzuv_ref[:, HW:2 * HW] = du.astype(BF16)
        dzuv_ref[:, 2 * HW:3 * HW] = dv.astype(BF16)
        acc = [(dong_ref, dong), (dlng_ref, dlng), (dlnb_ref, dlnb), (dws_ref, dws), (dbs_ref, dbs)]

        @pl.when(pl.program_id(0) == 0)
        def _():
            for r, val in acc:
                r[...] = val

        @pl.when(pl.program_id(0) > 0)
        def _():
            for r, val in acc:
                r[...] += val

    shp = lambda *s: jax.ShapeDtypeStruct(s, F32)
    return pl.pallas_call(
        body, name="mix_bwd", grid=(T // MIX_ROWS,),
        in_specs=_mix_specs() + [pl.BlockSpec((MIX_ROWS, D), lambda i: (i, 0))],
        out_specs=[pl.BlockSpec((MIX_ROWS, HW), lambda i: (i, 0)), pl.BlockSpec((MIX_ROWS, 3 * HW), lambda i: (i, 1)),
                   _whole((1, HD)), _whole((1, HW)), _whole((1, HW)), _whole((HEADS, GCH, GCH)), _whole((GCH, LANE))],
        out_shape=[shp(T, HW), jax.ShapeDtypeStruct((T, NP), BF16), shp(1, HD), shp(1, HW), shp(1, HW),
                   shp(HEADS, GCH, GCH), shp(GCH, LANE)],
        compiler_params=_cparams(("arbitrary",)),
    )(o, proj, proj, proj, ong, lng, lnb, ws, bst, dmix)


def _swiglu_epilogue(accs, _):
    gate, up = accs
    return [gate, up, _silu(gate) * up]


def _swiglu_bwd_epilogue(accs, extras):
    dact = accs[0]
    gate, up = (e.astype(F32) for e in extras)
    sg = _sigmoid(gate)
    return [dact * up * (sg * (1.0 + gate * (1.0 - sg))), dact * (gate * sg)]


def _layer_fwd(h, p):
    hn = p.pop("hn") if "hn" in p else _rmsnorm("rms_mix", h, p["norm_mix"])
    proj = _mm("in_proj", "nn", hn[None], [p["w_in"][None]], tm=512, tn=NP, tk=D, sub_m=2)[0][0]
    qkv = _prep(proj, p["conv_w"])
    bg = _gates(proj, p["a_log"], p["dt_bias"])
    prep, o, s_hist = _delta_rule(qkv, bg)
    mix = _mix(o, proj, p["o_norm_g"], p["ln_v_g"], p["ln_v_b"], p["w_s"], p["bst"],
               p.pop("before_mix")(o) if "before_mix" in p else ())
    if "late" in p:
        p.update(p.pop("late")(mix))
    h1 = _mm("out_proj", "nn", mix[None], [p["w_out"].reshape(1, D, D)], tm=T, tn=512, tk=D, resid=h[None],
             sub_m=4)[0][0]
    h2n = _rmsnorm("rms_ffn", h1, p["norm_ffn"])
    gate, up, act = _mm("ffn_in", "nt", h2n[None], [p["w_gate"], p["w_up"]], tm=1024, tn=FF_SH, tk=D,
                        out_dtypes=(BF16, BF16, BF16), epilogue=_swiglu_epilogue, sub_m=4)
    then = p.pop("before_ffn_out")(act) if "before_ffn_out" in p else ()
    h2 = _mm("ffn_out", "nn", act, [p["w_down"]], tm=1024, tn=512, tk=FF_SH, reduce_g=True, fold_g=True,
             resid=h1[None], sub_m=2, after=then)[0][0]
    saved = dict(h=h, hn=hn, proj=proj, qkv=qkv, bg=bg, prep=prep, o=o, s_hist=s_hist, mix=mix, h1=h1, h2n=h2n,
                 gate=gate, up=up, act=act)
    return h2, saved


def _layer_bwd_ffn(dh2, dh2b, p, s, after=()):
    dh2b = dh2b[None]
    dgate, dup = _mm("ffn_out_bwd", "nt", dh2b, [p["w_down"]], tm=1024, tn=FF_SH, tk=D, out_dtypes=(BF16, BF16),
                     extras=(s["gate"], s["up"]), epilogue=_swiglu_bwd_epilogue, after=after, sub_m=4)
    dh2n = _mm("ffn_gate_bwd", "nn", dgate, [p["w_gate"]], tm=1024, tn=512, tk=FF_SH, reduce_g=True, fold_g=True,
               sub_m=2)[0]
    dh2n = _mm("ffn_up_bwd", "nn", dup, [p["w_up"]], tm=1024, tn=512, tk=FF_SH, reduce_g=True, fold_g=True,
               resid=dh2n, sub_m=2)[0][0]
    dh1, dh1b, d_norm_ffn = _rmsnorm_bwd("rms_ffn_bwd", dh2n, s["h1"], p["norm_ffn"], dh2)
    d_w_down = _mm("ffn_wdown_grad", "tn", s["act"], [dh2b], tm=FF_SH, tn=512, tk=T)[0]
    d_w_gate = _mm("ffn_wgate_grad", "tn", dgate, [s["h2n"][None]], tm=FF_SH, tn=512, tk=T)[0]
    d_w_up = _mm("ffn_wup_grad", "tn", dup, [s["h2n"][None]], tm=FF_SH, tn=512, tk=T)[0]
    return dh1, dh1b, dict(norm_ffn=d_norm_ffn, w_gate=d_w_gate, w_up=d_w_up, w_down=d_w_down)


def _layer_bwd_mixer(dh1, dh1b, p, s, after=(), midway=None, late=None):
    dh1b = dh1b[None]
    dmix = _mm("out_proj_bwd", "nt", dh1b, [p["w_out"].reshape(1, D, D)], tm=T, tn=512, tk=D, after=after,
               sub_m=4)[0][0]
    d_w_out = _mm("out_proj_wgrad", "tn", s["mix"][None], [dh1b], tm=1024, tn=512, tk=T)[0][0]
    do, dproj, d_ong, d_lng, d_lnb, d_ws, d_bst = _mix_bwd(
        s["o"], s["proj"], p["o_norm_g"], p["ln_v_g"], p["ln_v_b"], p["w_s"], p["bst"], dmix)
    then = midway(do) if midway is not None else ()
    dqkv, dbg = _chunk_prep_bwd(s["qkv"], s["bg"], _chunk_scan_bwd(s["prep"], s["s_hist"], do, then))
    dproj, d_conv = _prep_bwd(s["proj"], p["conv_w"], dqkv, dproj)
    dproj, d_a_log, d_dt_bias = _gates_bwd(s["proj"], p["a_log"], p["dt_bias"], dbg, dproj)
    dproj = dproj[None]
    d_w_in = _mm("in_proj_wgrad", "tn", s["hn"][None], [dproj], tm=256, tn=NP, tk=T)[0]
    last = late(dict(w_in=d_w_in, w_out=d_w_out)) if late is not None else ()
    dhn = _mm("in_proj_bwd", "nt", dproj, [p["w_in"][None]], tm=1024, tn=512, tk=NP, after=last,
              sub_m=2)[0][0]
    dh, dhb, d_norm_mix = _rmsnorm_bwd("rms_mix_bwd", dhn, s["h"], p["norm_mix"], dh1)
    grads = dict(norm_mix=d_norm_mix, w_in=d_w_in, conv_w=d_conv, a_log=d_a_log, dt_bias=d_dt_bias, o_norm_g=d_ong,
                 ln_v_g=d_lng, ln_v_b=d_lnb, w_s=d_ws, bst=d_bst, w_out=d_w_out)
    return dh, dhb, grads


def _lanes(v, off=0):
    return jnp.zeros((1, LANE), F32).at[0, off:off + v.shape[0]].set(v)


def _w_in_pieces():
    regions = [(0, 2048, 0), (2048, 2056, BA_OFF), (2056, IN_DIM, 2048)]
    sh = IN_DIM // NCHIP
    out = []
    for j in range(NCHIP):
        for lo, hi, at in regions:
            a, b = max(lo, j * sh), min(hi, (j + 1) * sh)
            if a < b:
                out.append((j, a - j * sh, at + a - lo, b - a))
    return out


W_IN_PIECES = _w_in_pieces()
WT = 256


def _assemble_w_in(gathered, own, place):
    def body(place_ref, g_ref, own_ref, o_ref):
        o_ref[:, IN_DIM:] = jnp.zeros((WT, NP - IN_DIM), BF16)
        mine = own_ref[...]
        for j, src, dst, width in W_IN_PIECES:
            val = jnp.where(place_ref[0] == j, mine[:, src:src + width], g_ref[j, :, src:src + width])
            o_ref[:, dst:dst + width] = val

    sh = IN_DIM // NCHIP
    return pl.pallas_call(
        body, name="assemble_w_in",
        grid_spec=pltpu.PrefetchScalarGridSpec(
            num_scalar_prefetch=1, grid=(D // WT,),
            in_specs=[pl.BlockSpec((NCHIP, WT, sh), lambda i, place_ref: (0, i, 0)),
                      pl.BlockSpec((WT, sh), lambda i, place_ref: (i, 0))],
            out_specs=pl.BlockSpec((WT, NP), lambda i, place_ref: (i, 0))),
        out_shape=jax.ShapeDtypeStruct((D, NP), BF16), compiler_params=_cparams(("parallel",)),
    )(place, gathered, own)


def _layer_params(l, big, small):
    return dict(
        {k: v for k, v in big.items() if k != "conv_w"},
        conv_w=jnp.concatenate([big["conv_w"][j, l] for j in range(NCHIP)], axis=1),
        norm_mix=small["norm_mix"][l][None], norm_ffn=small["norm_ffn"][l][None],
        a_log=_lanes(small["a_log"][l], HEADS), dt_bias=_lanes(small["dt_bias"][l], HEADS),
        o_norm_g=small["o_norm_g"][l][None], ln_v_g=small["ln_v_g"][l][None], ln_v_b=small["ln_v_b"][l][None],
        w_s=small["w_s"][l],
        bst=jnp.pad(small["b_s"][l].T, ((0, 0), (0, LANE - HEADS))),
    )


def _reference_layout(g):
    return dict(
        w_in=g["w_in"],
        w_out=g["w_out"].reshape(NCHIP, D // NCHIP, D),
        w_gate=g["w_gate"], w_up=g["w_up"], w_down=g["w_down"],
        conv_w=g["conv_w"], norm_mix=g["norm_mix"][0], norm_ffn=g["norm_ffn"][0],
        a_log=g["a_log"][0, HEADS:2 * HEADS], dt_bias=g["dt_bias"][0, HEADS:2 * HEADS],
        o_norm_g=g["o_norm_g"][0], ln_v_g=g["ln_v_g"][0], ln_v_b=g["ln_v_b"][0], w_s=g["w_s"],
        b_s=g["bst"][:, :HEADS].T,
    )


def _forward(x, tgt, layers, norm_final):
    h = x
    saved, params = [], []
    for p in layers:
        p = p(h) if callable(p) else p
        h, s = _layer_fwd(h, p)
        saved.append(s)
        params.append(p)
    return (saved, params) + tuple(_loss_head(h, norm_final, tgt))


def _local_step(x, tgt, layers, norm_final):
    saved, layers, loss, dh, dhb, d_norm_final = _forward(x, tgt, layers, norm_final)
    grads = [None] * DEPTH
    for l in reversed(range(DEPTH)):
        dh1, dh1b, g_ffn = _layer_bwd_ffn(dh, dhb, layers[l], saved[l])
        dh, dhb, g_mix = _layer_bwd_mixer(dh1, dh1b, layers[l], saved[l])
        grads[l] = {**g_ffn, **g_mix}
    return loss, dh, grads, d_norm_final


def _place():
    x, y, c = lax.axis_index("x"), lax.axis_index("y"), lax.axis_index("c")
    return x, y, c, [(1 - x, y), (x, 1 - y), (1 - x, 1 - y)]


def _remote(src, dst, send_sem, recv_sem, to):
    return pltpu.make_async_remote_copy(src_ref=src, dst_ref=dst, send_sem=send_sem, recv_sem=recv_sem,
                                        device_id=to, device_id_type=MESH)


def _comm_call(name, body, ins, out_shape, n_sems, aliases=None):
    return pl.pallas_call(
        body, name=name, in_specs=[ANY] * len(ins), out_specs=[ANY] * len(out_shape), out_shape=out_shape,
        scratch_shapes=[pltpu.SemaphoreType.DMA((n,)) for n in n_sems], input_output_aliases=aliases or {},
        compiler_params=pltpu.CompilerParams(has_side_effects=True),
    )(*ins)


def _half_rows(ref, of_c, dim):
    hr = ref.shape[dim] // 2
    return pl.ds(pl.multiple_of(of_c * hr, BF16_ROWS), hr)


def _gather_plan(whole):
    def plan(srcs, lands):
        x, y, c, others = _place()
        chip = 2 * x + y
        out = []
        for src, land, all_of_it in zip(srcs, lands, whole):
            for ox, oy in others:
                if all_of_it:
                    out.append((src, land.at[chip], (ox, oy, c)))
                else:
                    out.append((src.at[_half_rows(src, c, 0)], land.at[chip, _half_rows(src, c, 0)], (ox, oy, c)))
        return out
    return plan


def _forward_halves(lands):
    n = len(lands)

    def body(*refs):
        outs = refs[n:2 * n]
        send_s, recv_s = refs[2 * n:]
        x, y, c, others = _place()
        sibling = (x, y, 1 - c)
        copies = []
        for a in range(n):
            for k, (ox, oy) in enumerate(others):
                mine = outs[a].at[2 * ox + oy, _half_rows(outs[a], c, 1)]
                copies.append(_remote(mine, mine, send_s.at[3 * a + k], recv_s.at[3 * a + k], sibling))
        for cp in copies:
            cp.start()
        for a in range(n):
            for k, (ox, oy) in enumerate(others):
                landed = outs[a].at[2 * ox + oy, _half_rows(outs[a], 1 - c, 1)]
                _remote(landed, landed, send_s.at[3 * a + k], recv_s.at[3 * a + k], sibling).wait_recv()
        for cp in copies:
            cp.wait_send()

    out_shape = [jax.ShapeDtypeStruct(g.shape, g.dtype) for g in lands]
    return _comm_call("forward_halves", body, lands, out_shape, [3 * n, 3 * n], aliases={a: a for a in range(n)})


def _forward_refs(bufs, incoming):
    x, y, c, others = _place()
    return (x, y, 1 - c), [b.at[2 * ox + oy, _half_rows(b, 1 - c if incoming else c, 1)]
                           for b in bufs for ox, oy in others]


def _forward_start(name, bufs, after):
    n = len(bufs)
    bufs = [pltpu.with_memory_space_constraint(b, pltpu.HBM) for b in bufs]

    def body(*refs):
        send_s, recv_s = refs[n + len(after)], refs[n + len(after) + 1]
        sibling, mine = _forward_refs(refs[:n], incoming=False)
        for i, ref in enumerate(mine):
            _remote(ref, ref, send_s.at[i], recv_s.at[i], sibling).start()
        refs[-1][...] = jnp.zeros_like(refs[-1])

    out = pl.pallas_call(
        body, name=name, in_specs=[HBM_SPEC] * n + [ANY] * len(after),
        out_specs=[SEM_SPEC, SEM_SPEC] + [HBM_SPEC] * n + [pl.BlockSpec(memory_space=pltpu.VMEM)],
        out_shape=[pltpu.SemaphoreType.DMA((3 * n,)), pltpu.SemaphoreType.DMA((3 * n,))]
        + [pltpu.HBM(b.shape, b.dtype) for b in bufs] + [jax.ShapeDtypeStruct((F32_ROWS, LANE), F32)],
        input_output_aliases={i: 2 + i for i in range(n)},
        compiler_params=pltpu.CompilerParams(has_side_effects=DATAFLOW),
    )(*bufs, *after)
    return dict(sems=out[:2], bufs=out[2:2 + n], token=out[-1])


def _forward_wait(name, started, after):
    n = len(started["bufs"])

    def body(*refs):
        send_s, recv_s = refs[n], refs[n + 1]
        sibling, mine = _forward_refs(refs[:n], incoming=False)
        _, theirs = _forward_refs(refs[:n], incoming=True)
        for i, (sent, landed) in enumerate(zip(mine, theirs)):
            _remote(sent, sent, send_s.at[i], recv_s.at[i], sibling).wait_send()
            _remote(landed, landed, send_s.at[i], recv_s.at[i], sibling).wait_recv()

    return pl.pallas_call(
        body, name=name, in_specs=[HBM_SPEC] * n + [SEM_SPEC, SEM_SPEC] + [ANY] * len(after),
        out_specs=[HBM_SPEC] * n, out_shape=[pltpu.HBM(b.shape, b.dtype) for b in started["bufs"]],
        input_output_aliases={i: i for i in range(n)},
        compiler_params=pltpu.CompilerParams(has_side_effects=DATAFLOW),
    )(*started["bufs"], *started["sems"], *after)


HBM_SPEC = pl.BlockSpec(memory_space=pltpu.HBM)
SEM_SPEC = pl.BlockSpec(memory_space=pltpu.SEMAPHORE)
DATAFLOW = pltpu.SideEffectType.DATAFLOW_SIDE_EFFECTING


def _exchange_plan(srcs, lands):
    x, y, c, _ = _place()
    plan = []
    for src, land in zip(srcs, lands):
        hr = src.shape[1] // 2
        plan.append((src.at[:, pl.ds(pl.multiple_of((1 - c) * hr, 8), hr)], land, (x, y, 1 - c)))
    return plan


def _scatter_plan(srcs, lands):
    x, y, c, others = _place()
    return [(src.at[2 * ox + oy], land.at[k], (ox, oy, c))
            for src, land in zip(srcs, lands) for k, (ox, oy) in enumerate(others)]


def _split_start(name, plan, srcs, land_shapes, n_copies, after=()):
    n = len(srcs)
    lands = [pltpu.with_memory_space_constraint(lax.empty(s.shape, s.dtype), pltpu.HBM) for s in land_shapes]
    srcs = [pltpu.with_memory_space_constraint(s, pltpu.HBM) for s in srcs]

    def body(*refs):
        send_s, recv_s = refs[2 * n + len(after)], refs[2 * n + len(after) + 1]
        for i, (src, dst, to) in enumerate(plan(refs[:n], refs[n:2 * n])):
            _remote(src, dst, send_s.at[i], recv_s.at[i], to).start()
        refs[-1][...] = jnp.zeros_like(refs[-1])

    thru = [pltpu.HBM(s.shape, s.dtype) for s in srcs + lands]
    out = pl.pallas_call(
        body, name=name, in_specs=[HBM_SPEC] * (2 * n) + [ANY] * len(after),
        out_specs=[SEM_SPEC, SEM_SPEC] + [HBM_SPEC] * (2 * n) + [pl.BlockSpec(memory_space=pltpu.VMEM)],
        out_shape=[pltpu.SemaphoreType.DMA((n_copies,)), pltpu.SemaphoreType.DMA((n_copies,))] + thru
        + [jax.ShapeDtypeStruct((F32_ROWS, LANE), F32)],
        input_output_aliases={i: 2 + i for i in range(2 * n)},
        compiler_params=pltpu.CompilerParams(has_side_effects=DATAFLOW),
    )(*srcs, *lands, *after)
    return dict(sems=out[:2], srcs=out[2:2 + n], lands=out[2 + n:2 + 2 * n], token=out[-1])


def _split_wait(name, plan, started, after):
    n = len(started["srcs"])
    after = list(after) if isinstance(after, (list, tuple)) else [after]

    def body(*refs):
        send_s, recv_s = refs[2 * n], refs[2 * n + 1]
        for i, (src, dst, to) in enumerate(plan(refs[:n], refs[n:2 * n])):
            cp = _remote(src, dst, send_s.at[i], recv_s.at[i], to)
            cp.wait_send()
            cp.wait_recv()

    arrs = list(started["srcs"]) + list(started["lands"])
    out = pl.pallas_call(
        body, name=name, in_specs=[HBM_SPEC] * (2 * n) + [SEM_SPEC, SEM_SPEC] + [ANY] * len(after),
        out_specs=[HBM_SPEC] * (2 * n), out_shape=[pltpu.HBM(s.shape, s.dtype) for s in arrs],
        input_output_aliases={i: i for i in range(2 * n)},
        compiler_params=pltpu.CompilerParams(has_side_effects=DATAFLOW),
    )(*arrs, *started["sems"], *after)
    return out[:n], out[n:]


def _join_halves(name, rs):
    n = len(rs)

    def body(*refs):
        outs = refs[n:2 * n]
        send_s, recv_s = refs[2 * n:]
        x, y, c, _ = _place()
        sibling = (x, y, 1 - c)

        def half(a, of_c):
            hr = outs[a].shape[1] // 2
            return outs[a].at[:, pl.ds(pl.multiple_of(of_c * hr, 8), hr)]

        copies = [_remote(half(a, c), half(a, c), send_s.at[a], recv_s.at[a], sibling) for a in range(n)]
        for cp in copies:
            cp.start()
        for a in range(n):
            landed = half(a, 1 - c)
            _remote(landed, landed, send_s.at[a], recv_s.at[a], sibling).wait_recv()
        for cp in copies:
            cp.wait_send()

    out_shape = [jax.ShapeDtypeStruct(r.shape, r.dtype) for r in rs]
    return _comm_call(name, body, rs, out_shape, [n, n], aliases={a: a for a in range(n)})


def _allreduce_small(buf, after=()):
    r = buf.shape[0]
    hr = r // 2

    def body(in_ref, *refs):
        out_ref, theirs, by_chip, send_s, recv_s = refs[len(after):]
        x, y, c, others = _place()
        chip = 2 * x + y
        sibling = (x, y, 1 - c)
        mine = pl.ds(pl.multiple_of(c * hr, F32_ROWS), hr)
        swap = _remote(in_ref, theirs, send_s.at[0], recv_s.at[0], sibling)
        swap.start()
        swap.wait()
        by_chip[chip] = in_ref[mine, :] + theirs[mine, :]
        sends = [_remote(by_chip.at[chip], by_chip.at[chip], send_s.at[1 + k], recv_s.at[1 + k], (ox, oy, c))
                 for k, (ox, oy) in enumerate(others)]
        for cp in sends:
            cp.start()
        for k, (ox, oy) in enumerate(others):
            landed = by_chip.at[2 * ox + oy]
            _remote(landed, landed, send_s.at[1 + k], recv_s.at[1 + k], (ox, oy, c)).wait_recv()
        for cp in sends:
            cp.wait_send()
        out_ref[mine, :] = (by_chip[0] + by_chip[1]) + (by_chip[2] + by_chip[3])
        back = _remote(out_ref.at[mine], out_ref.at[mine], send_s.at[NCHIP], recv_s.at[NCHIP], sibling)
        back.start()
        other = out_ref.at[pl.ds(pl.multiple_of((1 - c) * hr, F32_ROWS), hr)]
        _remote(other, other, send_s.at[NCHIP], recv_s.at[NCHIP], sibling).wait_recv()
        back.wait_send()

    vm = pl.BlockSpec(memory_space=pltpu.VMEM)
    return pl.pallas_call(
        body, name="allreduce_small", in_specs=[vm] + [ANY] * len(after), out_specs=vm,
        out_shape=jax.ShapeDtypeStruct((r, LANE), F32),
        scratch_shapes=[pltpu.VMEM((r, LANE), F32), pltpu.VMEM((NCHIP, hr, LANE), F32),
                        pltpu.SemaphoreType.DMA((NCHIP + 1,)), pltpu.SemaphoreType.DMA((NCHIP + 1,))],
        compiler_params=pltpu.CompilerParams(has_side_effects=True, vmem_limit_bytes=VMEM_LIMIT),
    )(buf, *after)


MAX_ROW_TILE = 512
BF16_ROWS = 16


def _row_tile(rows):
    for t in range(min(rows, MAX_ROW_TILE) // BF16_ROWS * BF16_ROWS, 0, -BF16_ROWS):
        if rows % t == 0:
            return t
    raise ValueError(rows)


def _sum_halves(g, theirs, c_arr):
    nch, rows, cols = g.shape
    hr = rows // 2
    tr = _row_tile(hr)

    def body(c_ref, g_ref, t_ref, o_ref, ob_ref):
        s = g_ref[...] + t_ref[...]
        o_ref[...] = s
        ob_ref[...] = s.astype(BF16)

    blk = pl.BlockSpec((None, tr, cols), lambda j, i, c_ref: (j, i, 0))
    return pl.pallas_call(
        body, name="sum_halves",
        grid_spec=pltpu.PrefetchScalarGridSpec(
            num_scalar_prefetch=1, grid=(nch, hr // tr),
            in_specs=[pl.BlockSpec((None, None, tr, cols), lambda j, i, c_ref: (j, c_ref[0], i, 0)), blk],
            out_specs=[blk, blk]),
        out_shape=[jax.ShapeDtypeStruct((nch, hr, cols), F32), jax.ShapeDtypeStruct((nch, hr, cols), BF16)],
        compiler_params=_cparams(("parallel", "parallel")),
    )(c_arr, g.reshape(nch, 2, hr, cols), theirs)


def _sum_halves_w_in(g, theirs, c_arr):
    hr = D // 2
    sh = IN_DIM // NCHIP

    def body(c_ref, g_ref, t_ref, o_ref, ob_ref):
        s = g_ref[...] + t_ref[...]
        for j, dst, src, width in W_IN_PIECES:
            o_ref[j, :, dst:dst + width] = s[:, src:src + width]
            ob_ref[j, :, dst:dst + width] = s[:, src:src + width].astype(BF16)

    out = pl.BlockSpec((NCHIP, WT, sh), lambda i, c_ref: (0, i, 0))
    return pl.pallas_call(
        body, name="sum_halves_w_in",
        grid_spec=pltpu.PrefetchScalarGridSpec(
            num_scalar_prefetch=1, grid=(hr // WT,),
            in_specs=[pl.BlockSpec((None, WT, NP), lambda i, c_ref: (c_ref[0], i, 0)),
                      pl.BlockSpec((None, WT, NP), lambda i, c_ref: (0, i, 0))],
            out_specs=[out, out]),
        out_shape=[jax.ShapeDtypeStruct((NCHIP, hr, sh), F32), jax.ShapeDtypeStruct((NCHIP, hr, sh), BF16)],
        compiler_params=_cparams(("parallel",)),
    )(c_arr, g.reshape(2, hr, NP), theirs)


def _sum_chips(p, q, place, l, into=None, after=()):
    extra = ([into] if into is not None else []) + list(after)
    _, rows, cols = p.shape
    tr = _row_tile(rows)
    steps = rows // tr

    def body(place_ref, p_ref, q0, q1, q2, *rest):
        rest[-1][...] = ((p_ref[...] + q0[...].astype(F32)) + q1[...].astype(F32)) + q2[...].astype(F32)

    qs = lambda k: pl.BlockSpec((None, tr, cols), lambda i, place_ref: (k, i, 0))
    return pl.pallas_call(
        body, name="sum_chips",
        grid_spec=pltpu.PrefetchScalarGridSpec(
            num_scalar_prefetch=1, grid=(steps,),
            in_specs=[pl.BlockSpec((None, tr, cols), lambda i, place_ref: (place_ref[0], i, 0)), qs(0), qs(1), qs(2)]
            + [ANY] * len(extra),
            out_specs=pl.BlockSpec((None, tr, cols), lambda i, place_ref: (l, place_ref[1] * steps + i, 0))),
        out_shape=jax.ShapeDtypeStruct((DEPTH, 2 * rows, cols), F32),
        input_output_aliases={5: 0} if into is not None else {},
        compiler_params=_cparams(("parallel",)),
    )(place, p, q, q, q, *extra)


def _adamw_fn(w, g, m, v):
    nm = ADAM_B1 * m + (1.0 - ADAM_B1) * g
    nv = ADAM_B2 * v + (1.0 - ADAM_B2) * jnp.square(g)
    m_hat = nm / (1.0 - ADAM_B1 ** ADAM_STEP)
    v_hat = nv / (1.0 - ADAM_B2 ** ADAM_STEP)
    return -ADAM_LR * (m_hat / (jnp.sqrt(v_hat) + ADAM_EPS) + ADAM_WD * w), nm, nv


def _adamw(w, g, m, v):
    layers, rows, cols = w.shape
    tr = _row_tile(rows)

    def body(w_ref, g_ref, m_ref, v_ref, d_ref, nm_ref, nv_ref):
        d_ref[...], nm_ref[...], nv_ref[...] = _adamw_fn(w_ref[...], g_ref[...], m_ref[...], v_ref[...])

    blk = pl.BlockSpec((None, tr, cols), lambda l, i: (l, i, 0))
    return pl.pallas_call(
        body, name="adamw", grid=(layers, rows // tr), in_specs=[blk] * 4, out_specs=[blk] * 3,
        out_shape=[jax.ShapeDtypeStruct(w.shape, F32)] * 3, compiler_params=_cparams(("parallel", "parallel")),
    )(w, g, m, v)


def _adamw_small(ws, gs, ms, vs):
    n = len(ws)

    def body(*refs):
        for i in range(n):
            w_ref, g_ref, m_ref, v_ref, d_ref, nm_ref, nv_ref = (refs[k * n + i] for k in range(7))
            d_ref[...], nm_ref[...], nv_ref[...] = _adamw_fn(w_ref[...], g_ref[...], m_ref[...], v_ref[...])

    vm = pl.BlockSpec(memory_space=pltpu.VMEM)
    out = pl.pallas_call(
        body, name="adamw_small", in_specs=[vm] * (4 * n), out_specs=[vm] * (3 * n),
        out_shape=[jax.ShapeDtypeStruct(a.shape, F32) for a in list(ws) * 3],
        compiler_params=pltpu.CompilerParams(vmem_limit_bytes=VMEM_LIMIT),
    )(*ws, *gs, *ms, *vs)
    return out[:n], out[n:2 * n], out[2 * n:]


BIG = ("w_in", "w_out", "w_gate", "w_up", "w_down")
SMALL = ("norm_mix", "a_log", "dt_bias", "o_norm_g", "ln_v_g", "ln_v_b", "w_s", "b_s", "norm_ffn", "norm_final")
ORDER = ("norm_mix", "w_in", "conv_w", "a_log", "dt_bias", "o_norm_g", "ln_v_g", "ln_v_b", "w_s", "b_s", "w_out",
         "norm_ffn", "w_gate", "w_up", "w_down", "norm_final")


F32_ROWS = 8
PACK_ROWS = 128


def _lane_rows(size):
    return -(-size // (F32_ROWS * LANE)) * F32_ROWS


def _pack(arrs):
    parts = [jnp.pad(a.reshape(-1), (0, _lane_rows(a.size) * LANE - a.size)).reshape(-1, LANE) for a in arrs]
    rows = sum(p.shape[0] for p in parts)
    if rows % PACK_ROWS:
        parts.append(jnp.zeros((-rows % PACK_ROWS, LANE), F32))
    return jnp.concatenate(parts, axis=0)


def _unpack(buf, like):
    out, row = [], 0
    for a in like:
        n = _lane_rows(a.size)
        out.append(buf[row:row + n].reshape(-1)[:a.size].reshape(a.shape))
        row += n
    return out


def kernel(x, norm_mix, w_in, conv_w, a_log, dt_bias, o_norm_g, ln_v_g, ln_v_b, w_s, b_s, w_out, norm_ffn, w_gate, w_up, w_down, norm_final, loss_target, m_norm_mix, m_w_in, m_conv_w, m_a_log, m_dt_bias, m_o_norm_g, m_ln_v_g, m_ln_v_b, m_w_s, m_b_s, m_w_out, m_norm_ffn, m_w_gate, m_w_up, m_w_down, m_norm_final, v_norm_mix, v_w_in, v_conv_w, v_a_log, v_dt_bias, v_o_norm_g, v_ln_v_g, v_ln_v_b, v_w_s, v_b_s, v_w_out, v_norm_ffn, v_w_gate, v_w_up, v_w_down, v_norm_final):
    w = dict(norm_mix=norm_mix, w_in=w_in, conv_w=conv_w, a_log=a_log, dt_bias=dt_bias, o_norm_g=o_norm_g,
             ln_v_g=ln_v_g, ln_v_b=ln_v_b, w_s=w_s, b_s=b_s, w_out=w_out, norm_ffn=norm_ffn, w_gate=w_gate, w_up=w_up,
             w_down=w_down, norm_final=norm_final)
    m = dict(norm_mix=m_norm_mix, w_in=m_w_in, conv_w=m_conv_w, a_log=m_a_log, dt_bias=m_dt_bias, o_norm_g=m_o_norm_g,
             ln_v_g=m_ln_v_g, ln_v_b=m_ln_v_b, w_s=m_w_s, b_s=m_b_s, w_out=m_w_out, norm_ffn=m_norm_ffn,
             w_gate=m_w_gate, w_up=m_w_up, w_down=m_w_down, norm_final=m_norm_final)
    v = dict(norm_mix=v_norm_mix, w_in=v_w_in, conv_w=v_conv_w, a_log=v_a_log, dt_bias=v_dt_bias, o_norm_g=v_o_norm_g,
             ln_v_g=v_ln_v_g, ln_v_b=v_ln_v_b, w_s=v_w_s, b_s=v_b_s, w_out=v_w_out, norm_ffn=v_norm_ffn,
             w_gate=v_w_gate, w_up=v_w_up, w_down=v_w_down, norm_final=v_norm_final)
    chip = 2 * lax.axis_index("x") + lax.axis_index("y")
    place = jnp.stack([chip, lax.axis_index("c")]).astype(jnp.int32)
    c_arr = place[1:]

    def kernel_view(n, a):
        return jnp.swapaxes(a, 1, 2) if n in ("w_gate", "w_up") else a

    own = {n: [kernel_view(n, w[n])[l].astype(BF16) for l in range(DEPTH)] for n in BIG}
    by_chip = lambda a: jax.ShapeDtypeStruct((NCHIP,) + a.shape, a.dtype)

    def start(name, srcs, whole, after=()):
        return _split_start(name, _gather_plan(whole), srcs, [by_chip(a) for a in srcs], 3 * len(srcs), after)

    def finish(name, started, whole, after):
        srcs, lands = _split_wait(name, _gather_plan(whole), started, after)
        passed = iter(_forward_halves([g for g, all_of_it in zip(lands, whole) if not all_of_it]))
        lands = [g if all_of_it else next(passed) for g, all_of_it in zip(lands, whole)]
        return srcs, [lax.dynamic_update_index_in_dim(g, o, chip, 0) for g, o in zip(lands, srcs)]

    ffn = BIG[1:]
    first = start("gather_first_start", [own["w_in"][0], conv_w], [False, True])
    early = start("gather_early_start", [own[n][0] for n in ffn], [False] * len(ffn), [first["token"]])
    mid = start("gather_mid_start", [own["w_in"][1]], [False], [early["token"]])
    later = start("gather_later_start", [own[n][1] for n in ffn], [False] * len(ffn), [mid["token"]])
    hn = _rmsnorm("rms_mix", x[0], norm_mix[0][None])
    (own_w_in, _), (w_in_by_chip, conv_by_chip) = finish("gather_first_wait", first, [False, True], [later["token"], hn])

    passing = {}

    def pass_on(tag, started, n):
        def at(after):
            srcs, lands = _split_wait(f"gather_{tag}_wait", _gather_plan([False] * n), started, after)
            passing[tag] = srcs, _forward_start(f"forward_{tag}_start", lands, ())
            return [passing[tag][1]["token"]]
        return at

    def passed_on(tag, after):
        srcs, fwd = passing[tag]
        lands = _forward_wait(f"forward_{tag}_wait", fwd, [after])
        return srcs, [lax.dynamic_update_index_in_dim(g, o, chip, 0) for g, o in zip(lands, srcs)]

    def late(tag):
        return lambda after: dict(zip(ffn, passed_on(tag, after)[1]))

    layer0 = _layer_params(0, dict(
        hn=hn, w_in=_assemble_w_in(w_in_by_chip, own_w_in, place), conv_w=conv_by_chip, late=late("early"),
        before_mix=pass_on("early", early, len(ffn)), before_ffn_out=pass_on("mid", mid, 1)), w)

    def layer1(after):
        (own_w_in1,), (w_in1_by_chip,) = passed_on("mid", after)
        return _layer_params(1, dict(w_in=_assemble_w_in(w_in1_by_chip, own_w_in1, place), conv_w=conv_by_chip,
                                     late=late("later"), before_mix=pass_on("later", later, len(ffn))), w)

    saved, layers, loss_lanes, dh, dhb, d_norm_final = _forward(x[0], loss_target[0], [layer0, layer1],
                                                                 norm_final[None])

    sums, arrived = {}, {}

    def exchange_start(tag, l, names, grads, after=()):
        mine = [grads[n] for n in names]
        shapes = [jax.ShapeDtypeStruct((g.shape[0], g.shape[1] // 2, g.shape[2]), F32) for g in mine]
        return tag, l, names, _split_start(f"exchange_{tag}_start", _exchange_plan, mine, shapes, len(mine), after)

    def add_halves(l, names, mine, theirs):
        for n, g, t in zip(names, mine, theirs):
            sums[l, n] = (_sum_halves_w_in if n == "w_in" else _sum_halves)(g, t, c_arr)

    def exchange_wait(handle, after):
        tag, l, names, started = handle
        add_halves(l, names, *_split_wait(f"exchange_{tag}_wait", _exchange_plan, started, after))

    def scatter_start(tag, l, names, after=()):
        partial = [sums[l, n][1] for n in names]
        shapes = [jax.ShapeDtypeStruct((3,) + p.shape[1:], p.dtype) for p in partial]
        return tag, l, names, _split_start(f"scatter_{tag}_start", _scatter_plan, partial, shapes, 3 * len(names), after)

    def scatter_wait(handle, after):
        tag, l, names, started = handle
        for n, q in zip(names, _split_wait(f"scatter_{tag}_wait", _scatter_plan, started, after)[1]):
            arrived[l, n] = q

    last = DEPTH - 1
    swiglu = BIG[2:]
    dh1, dh1b, g_ffn = _layer_bwd_ffn(dh, dhb, layers[last], saved[last])
    dh, dhb, g_mix = _layer_bwd_mixer(dh1, dh1b, layers[last], saved[last])
    gl = [None, _reference_layout({**g_ffn, **g_mix})]
    ex_last = exchange_start("last", last, BIG, gl[last])
    dh1, dh1b, g_ffn = _layer_bwd_ffn(dh, dhb, layers[0], saved[0], after=[ex_last[-1]["token"]])
    exchange_wait(ex_last, dh1)
    sc_last = scatter_start("last", last, BIG)
    ex_ffn = exchange_start("swiglu", 0, swiglu, g_ffn, [sc_last[-1]["token"]])
    sc_ffn = []

    def midway(do):
        exchange_wait(ex_ffn, do)
        sc_ffn.append(scatter_start("swiglu", 0, swiglu))
        return [sc_ffn[0][-1]["token"]]

    ex_rest = []

    def late(grads):
        rest_grads = dict(w_in=grads["w_in"], w_out=grads["w_out"].reshape(NCHIP, D // NCHIP, D))
        ex_rest.append(exchange_start("rest", 0, BIG[:2], rest_grads))
        return [ex_rest[0][-1]["token"]]

    dx, _, g_mix = _layer_bwd_mixer(dh1, dh1b, layers[0], saved[0], after=[ex_ffn[-1]["token"]], midway=midway,
                                    late=late)
    scatter_wait(sc_last, dx)
    scatter_wait(sc_ffn[0], dx)
    gl[0] = _reference_layout({**g_ffn, **g_mix})

    small_g = [jnp.stack([gl[l][n] for l in range(DEPTH)]) for n in SMALL[:-1]] + [d_norm_final[0]]
    conv_g = jnp.stack([gl[l]["conv_w"] for l in range(DEPTH)])
    summed = small_g + [conv_g, loss_lanes[0, :1]]
    total = _allreduce_small(_pack(summed))
    exchange_wait(ex_rest[0], total)
    sc_rest = scatter_start("rest", 0, BIG[:2])

    travelling = [sc_rest[-1]["token"]]
    reduced, g_out, delta, new_m, new_v = {}, {}, {}, {}, {}

    done = []

    def adamw_large(names, joined):
        for n, g in zip(names, joined):
            res = _adamw(kernel_view(n, w[n]), g, kernel_view(n, m[n]), kernel_view(n, v[n]))
            done.append(res[2])
            g_out[n], delta[n], new_m[n], new_v[n] = (kernel_view(n, a) for a in (g,) + tuple(res))

    for n in BIG:
        for l in (range(DEPTH) if n in swiglu else [last]):
            reduced[n] = _sum_chips(sums[l, n][0], arrived[l, n], place, l, into=reduced.get(n), after=travelling)
    adamw_large(swiglu, _join_halves("join_swiglu", [reduced[n] for n in swiglu]))
    scatter_wait(sc_rest, done + [reduced[n] for n in BIG[:2]])
    for n in BIG[:2]:
        reduced[n] = _sum_chips(sums[0, n][0], arrived[0, n], place, 0, into=reduced[n])
    adamw_large(BIG[:2], _join_halves("join_rest", [reduced[n] for n in BIG[:2]]))
    *small_r, conv_r, loss = _unpack(total, summed)
    g_out.update(zip(SMALL, small_r))
    g_out["conv_w"] = lax.dynamic_slice_in_dim(conv_r, chip * conv_w.shape[2], conv_w.shape[2], axis=2)

    rest = SMALL + ("conv_w",)
    rows_of = lambda a: a.reshape(1, -1) if a.ndim == 1 else a
    results = _adamw_small(*[[rows_of(src[n]) for n in rest] for src in (w, g_out, m, v)])
    for dst, arrs in zip((delta, new_m, new_v), results):
        dst.update({n: a.reshape(w[n].shape) for n, a in zip(rest, arrs)})

    return (loss[0], dx[None], *[g_out[n] for n in ORDER], *[delta[n] for n in ORDER], *[new_m[n] for n in ORDER],
            *[new_v[n] for n in ORDER])
```

```python
import functools

import jax
import jax.numpy as jnp
from jax import lax
from jax.experimental import pallas as pl
from jax.experimental.pallas import tpu as pltpu

F32 = jnp.float32
BF16 = jnp.bfloat16
MESH = pl.DeviceIdType.MESH
ANY = pl.BlockSpec(memory_space=pl.ANY)
HIGHEST = lax.Precision.HIGHEST

T = 2048
D = 1024
DEPTH = 2
NCHIP = 4
HEADS = 4
HD = 128
HW = HEADS * HD
CH = 64
GCH = 128
IN_DIM = 3080
NP = 3200
BA_OFF = 3072
FF_SH = 704
EPS = 1e-6
LANE = 128
VMEM_LIMIT = 56 * 1024 * 1024

ADAM_LR = 0.001
ADAM_B1 = 0.9
ADAM_B2 = 0.999
ADAM_EPS = 1e-08
ADAM_WD = 0.01
ADAM_STEP = 10


def _cparams(sem=None):
    return pltpu.CompilerParams(dimension_semantics=sem, vmem_limit_bytes=VMEM_LIMIT)


_DIMS = {"nn": (((1,), (0,)), ((), ())), "nt": (((1,), (1,)), ((), ())), "tn": (((0,), (0,)), ((), ()))}


def _mm(name, mode, a, bs, *, tm, tn, tk, out_dtypes=(F32,), reduce_g=False, resid=None, extras=(), epilogue=None,
        after=(), fold_g=False, sub_m=1):
    assert sub_m == 1 or (mode != "tn" and tm % (8 * sub_m) == 0), (name, sub_m)
    nb = len(bs)
    ga = a.shape[0]
    gbs = [b.shape[0] for b in bs]
    g_n = max([ga] + gbs)
    if mode == "tn":
        k_n, m_n = a.shape[1:]
    else:
        m_n, k_n = a.shape[1:]
    n_n = bs[0].shape[1] if mode == "nt" else bs[0].shape[2]
    assert m_n % tm == 0 and n_n % tn == 0 and k_n % tk == 0, (name, m_n, n_n, k_n)
    mi, nj, kk = m_n // tm, n_n // tn, k_n // tk
    lead = g_n if fold_g else None
    g_steps = 1 if fold_g else g_n
    grid = (mi, nj, g_steps, kk)
    ids = lambda i, j, g, k: (g, i, j, k)
    n_red = (g_steps if reduce_g else 1) * kk
    red_idx = lambda: (pl.program_id(2) * kk if reduce_g else 0) + pl.program_id(3)
    sem = ("parallel", "parallel", "arbitrary" if reduce_g else "parallel", "arbitrary")

    def pick(gsz, g):
        return g if gsz > 1 else 0

    def a_map(*p):
        g, i, j, k = ids(*p)
        return (pick(ga, g), k, i) if mode == "tn" else (pick(ga, g), i, k)

    def b_map(gsz):
        def f(*p):
            g, i, j, k = ids(*p)
            return (pick(gsz, g), j, k) if mode == "nt" else (pick(gsz, g), k, j)
        return f

    def o_map(gsz):
        def f(*p):
            g, i, j, k = ids(*p)
            return (0 if reduce_g else pick(gsz, g), i, j)
        return f

    a_spec = pl.BlockSpec((lead, tk, tm) if mode == "tn" else (lead, tm, tk), a_map)
    b_specs = [pl.BlockSpec((lead, tn, tk) if mode == "nt" else (lead, tk, tn), b_map(gs)) for gs in gbs]
    x_specs = [pl.BlockSpec((None, tm, tn), o_map(e.shape[0])) for e in extras]
    r_specs = [pl.BlockSpec((None, tm, tn), o_map(resid.shape[0]))] if resid is not None else []
    g_out = 1 if reduce_g else g_n
    out_shape = [jax.ShapeDtypeStruct((g_out, m_n, n_n), dt) for dt in out_dtypes]
    out_specs = [pl.BlockSpec((None, tm, tn), o_map(g_out)) for _ in out_dtypes]
    nx, nr, no = len(extras), len(r_specs), len(out_dtypes)
    n_in = 1 + nb + nx + nr + len(after)
    dims = _DIMS[mode]

    def body(*refs):
        a_ref = refs[0]
        b_refs = refs[1:1 + nb]
        x_refs = refs[1 + nb:1 + nb + nx]
        r_refs = refs[1 + nb + nx:1 + nb + nx + nr]
        o_refs = refs[n_in:n_in + no]
        acc_refs = refs[n_in + no:]
        def dots(rows):
            if fold_g:
                return [sum(lax.dot_general(a_ref[g, rows, :], b_ref[g], dims, preferred_element_type=F32)
                            for g in range(g_n)) for b_ref in b_refs]
            av = a_ref[...] if mode == "tn" else a_ref[rows, :]
            return [lax.dot_general(av, b_ref[...], dims, preferred_element_type=F32) for b_ref in b_refs]

        def finish(accs, rows=slice(None)):
            if r_refs:
                accs[0] = accs[0] + r_refs[0][rows, :]
            outs = epilogue(accs, [x[rows, :] for x in x_refs]) if epilogue is not None else accs
            for o_ref, o in zip(o_refs, outs):
                o_ref[rows, :] = o.astype(o_ref.dtype)

        if n_red == 1:
            slabs = [slice(s * (tm // sub_m), (s + 1) * (tm // sub_m)) for s in range(sub_m)]
            ahead = dots(slabs[0])
            for s, rows in enumerate(slabs):
                now, ahead = ahead, (dots(slabs[s + 1]) if s + 1 < sub_m else None)
                finish(now, rows)
            return
        products = dots(slice(None))
        r = red_idx()
        for p, acc in zip(products, acc_refs):
            @pl.when(r == 0)
            def _():
                acc[...] = p

            @pl.when((r > 0) & (r < n_red - 1))
            def _():
                acc[...] += p

        @pl.when(r == n_red - 1)
        def _():
            finish([acc[...] + p for p, acc in zip(products, acc_refs)])

    return pl.pallas_call(
        body, name=name, grid=grid,
        in_specs=[a_spec] + b_specs + x_specs + r_specs + [ANY] * len(after),
        out_specs=out_specs, out_shape=out_shape,
        scratch_shapes=[pltpu.VMEM((tm, tn), F32) for _ in range(nb if n_red > 1 else 0)],
        compiler_params=_cparams(sem),
    )(a, *bs, *extras, *([resid] if resid is not None else []), *after)


def _sigmoid(x):
    return 1.0 / (1.0 + jnp.exp(-x))


def _silu(x):
    return x * _sigmoid(x)


def _gelu(x):
    return 0.5 * x * (1.0 + jnp.tanh(0.7978845608028654 * (x + 0.044715 * (x * x * x))))


def _rms_fn(h, gain):
    return h * lax.rsqrt(jnp.mean(h * h, axis=-1, keepdims=True) + EPS) * gain


def _shift_impl(x, s):
    n = x.shape[0]
    rolled = pltpu.roll(x, s % n, 0)
    row = lax.broadcasted_iota(jnp.int32, x.shape, 0)
    return jnp.where((row >= s) & (row < n + s), rolled, 0.0)


@functools.partial(jax.custom_vjp, nondiff_argnums=(1,))
def _shift(x, s):
    return _shift_impl(x, s)


def _shift_fwd(x, s):
    return _shift_impl(x, s), None


def _shift_bwd(s, _, g):
    return (_shift_impl(g, -s),)


_shift.defvjp(_shift_fwd, _shift_bwd)


def _prep_fn(x, w, qk_scale, is_v):
    y = x * w[3:4, :]
    for i in range(3):
        y = y + _shift(x, 3 - i) * w[i:i + 1, :]
    y = _silu(y)
    nrm = lax.rsqrt(jnp.sum(y * y, axis=-1, keepdims=True) + EPS) * qk_scale
    return y * jnp.where(is_v, 1.0, nrm)


def _softplus(x):
    return jnp.maximum(x, 0.0) + jnp.log(1.0 + jnp.exp(-jnp.abs(x)))


def _gates_fn(ba, a_log, dt_bias):
    lane = lax.broadcasted_iota(jnp.int32, ba.shape, 1)
    beta = _sigmoid(ba)
    g = -jnp.exp(a_log) * _softplus(ba + dt_bias)
    return jnp.where(lane < HEADS, beta, g)


def _dot16(a, b, dims=_DIMS["nn"]):
    return lax.dot_general(a.astype(BF16), b.astype(BF16), dims, preferred_element_type=F32)


def _dot32(a, b):
    return jnp.dot(a, b, preferred_element_type=F32, precision=HIGHEST)


def _dot3(a, b, dims=_DIMS["nn"]):
    return lax.dot_general(a, b, dims, preferred_element_type=F32, precision=lax.Precision.HIGH)


def _tri_inverses(mats, tick=lambda: None):
    row = lax.broadcasted_iota(jnp.int32, (CH, CH), 0)
    col = lax.broadcasted_iota(jnp.int32, (CH, CH), 1)
    eye = (row == col).astype(F32)
    ts = [eye - a for a in mats]
    ps = list(mats)
    for _ in range(5):
        ps = [_dot3(p, p) for p in ps]
        tick()
        ts = [t + _dot3(t, p) for t, p in zip(ts, ps)]
        tick()
    return ts


@jax.custom_vjp
def _tri_solves(mats, rhs):
    return [_dot3(t, b) for t, b in zip(_tri_inverses(mats), rhs)]


def _tri_solves_fwd(mats, rhs):
    ts = _tri_inverses(mats)
    xs = [_dot3(t, b) for t, b in zip(ts, rhs)]
    return xs, (ts, xs)


def _tri_solves_bwd(res, dxs):
    ts, xs = res
    dbs = [_dot3(t, dx, _DIMS["tn"]) for t, dx in zip(ts, dxs)]
    return [-_dot3(db, x, _DIMS["nt"]) for db, x in zip(dbs, xs)], dbs


_tri_solves.defvjp(_tri_solves_fwd, _tri_solves_bwd)


def _chunk_prep_fn(xs, bgs, tick=None):
    step = tick or (lambda: None)
    row = lax.broadcasted_iota(jnp.int32, (CH, CH), 0)
    col = lax.broadcasted_iota(jnp.int32, (CH, CH), 1)
    incl = row >= col
    strict = row > col
    lmat = incl.astype(F32)
    n = len(xs)
    items = [(i, h) for i in range(n) for h in range(HEADS)]
    part = lambda i, h, c: xs[i][:, c * HW + h * HD:c * HW + (h + 1) * HD]
    q = [part(i, h, 0) for i, h in items]
    k = [part(i, h, 1) for i, h in items]
    v = [part(i, h, 2) for i, h in items]
    beta = [bgs[i][:, h:h + 1] for i, h in items]
    gc_all = [_dot32(lmat, bg) for bg in bgs]
    step()
    gc = [gc_all[i][:, HEADS + h:HEADS + h + 1] for i, h in items]
    gmat = [jnp.where(strict, jnp.broadcast_to(bgs[i][:, HEADS + h:HEADS + h + 1], (CH, CH)), 0.0) for i, h in items]
    diff = [_dot3(lmat, m) for m in gmat]
    step()
    decay = [jnp.where(incl, jnp.exp(jnp.where(incl, d, 0.0)), 0.0) for d in diff]
    k_beta = [kk * b for kk, b in zip(k, beta)]
    kk_t = [_dot16(kb, kk, _DIMS["nt"]) for kb, kk in zip(k_beta, k)]
    step()
    qk_t = [_dot16(qq, kk, _DIMS["nt"]) for qq, kk in zip(q, k)]
    step()
    a = [jnp.where(strict, m * d, 0.0) for m, d in zip(kk_t, decay)]
    eg = [jnp.exp(g) for g in gc]
    rhs = [jnp.concatenate([vv * b, kb * e], axis=-1) for vv, b, kb, e in zip(v, beta, k_beta, eg)]
    if tick is None:
        uw = _tri_solves(a, rhs)
    else:
        uw = [_dot3(t, b) for t, b in zip(_tri_inverses(a, tick), rhs)]
    qk = [m * d for m, d in zip(qk_t, decay)]
    g_last = [g[CH - 1:CH, :] for g in gc]
    qe = [qq * e for qq, e in zip(q, eg)]
    kd = [kk * jnp.exp(gl - g) for kk, gl, g in zip(k, g_last, gc)]
    egl = [jnp.broadcast_to(jnp.exp(gl), (1, HD)) for gl in g_last]
    out = []
    for i in range(n):
        mine = slice(i * HEADS, (i + 1) * HEADS)
        cat = lambda vals: jnp.concatenate(vals[mine], axis=-1)
        out.append((cat([x[:, :HD] for x in uw]), cat([x[:, HD:] for x in uw]), cat(qe), cat(kd),
                    jnp.concatenate([m[None] for m in qk[mine]], axis=0), cat(egl)))
    return out


def _state_levels(chunks, s, outs, befores, final):
    for u, w, qe, kd, qk, egl in chunks:
        befores.append(s)
        ws = [_dot16(a, b) for a, b in zip(w, s)]
        qs = [_dot16(a, b) for a, b in zip(qe, s)]
        yield
        v_new = [a - b for a, b in zip(u, ws)]
        outs.append([a + _dot16(b, c) for a, b, c in zip(qs, qk, v_new)])
        s = [a * e + _dot16(b, c, _DIMS["tn"]) for a, e, b, c in zip(s, egl, kd, v_new)]
        yield
    final.append(s)


def _chunk_state_fn(u, w, qe, kd, qk, egl, s):
    ws = [_dot16(a, b) for a, b in zip(w, s)]
    qs = [_dot16(a, b) for a, b in zip(qe, s)]
    v_new = [a - b for a, b in zip(u, ws)]
    o = [a + _dot16(b, c) for a, b, c in zip(qs, qk, v_new)]
    s_new = [a * e + _dot16(b, c, _DIMS["tn"]) for a, e, b, c in zip(s, egl, kd, v_new)]
    return o, s_new


def _mix_fn(o, z, ur, vr, ong, lng, lnb, ws, bst):
    row = lax.broadcasted_iota(jnp.int32, (GCH, GCH), 0)
    col = lax.broadcasted_iota(jnp.int32, (GCH, GCH), 1)
    causal = row >= col
    ug = _gelu(ur)
    vg = _gelu(vr)
    sls = [slice(h * HD, (h + 1) * HD) for h in range(HEADS)]
    oh = [o[:, sl] for sl in sls]
    oh = [x * lax.rsqrt(jnp.mean(x * x, axis=-1, keepdims=True) + EPS) for x in oh]
    outs_dn = [x * ong * _silu(z[:, sl]) for x, sl in zip(oh, sls)]
    vh = [vg[:, sl] for sl in sls]
    mu = [jnp.mean(x, axis=-1, keepdims=True) for x in vh]
    var = [jnp.mean(jnp.square(x - m), axis=-1, keepdims=True) for x, m in zip(vh, mu)]
    vn = [(x - m) * lax.rsqrt(s + EPS) * lng[:, sl] + lnb[:, sl] for x, m, s, sl in zip(vh, mu, var, sls)]
    mixed = [_dot16(jnp.where(causal, ws[h], 0.0), vn[h]) for h in range(HEADS)]
    outs_gm = [ug[:, sl] * (mixed[h] + bst[:, h:h + 1]) for h, sl in enumerate(sls)]
    return jnp.concatenate(outs_dn + outs_gm, axis=-1)


def _loss_fn(h, gain, tgt):
    y = _rms_fn(h, gain)
    return 0.5 * jnp.sum(jnp.mean(jnp.square(y - tgt), axis=-1))


RT = 512


def _rows(n=D):
    return pl.BlockSpec((RT, n), lambda i: (i, 0))


def _whole(shape):
    nd = len(shape)
    return pl.BlockSpec(shape, lambda i: (0,) * nd)


def _rmsnorm(name, h, gain):
    def body(h_ref, g_ref, o_ref):
        o_ref[...] = _rms_fn(h_ref[...], g_ref[...]).astype(BF16)

    return pl.pallas_call(
        body, name=name, grid=(T // RT,), in_specs=[_rows(), _whole((1, D))], out_specs=_rows(),
        out_shape=jax.ShapeDtypeStruct((T, D), BF16), compiler_params=_cparams(("parallel",)),
    )(h, gain)


def _rmsnorm_bwd(name, dhn, h, gain, resid):
    def body(dhn_ref, h_ref, g_ref, r_ref, dh_ref, dh16_ref, dg_ref):
        _, vjp = jax.vjp(_rms_fn, h_ref[...], g_ref[...])
        dh, dg = vjp(dhn_ref[...])
        dh = r_ref[...] + dh
        dh_ref[...] = dh
        dh16_ref[...] = dh.astype(BF16)

        @pl.when(pl.program_id(0) == 0)
        def _():
            dg_ref[...] = dg

        @pl.when(pl.program_id(0) > 0)
        def _():
            dg_ref[...] += dg

    return pl.pallas_call(
        body, name=name, grid=(T // RT,), in_specs=[_rows(), _rows(), _whole((1, D)), _rows()],
        out_specs=[_rows(), _rows(), _whole((1, D))],
        out_shape=[jax.ShapeDtypeStruct((T, D), F32), jax.ShapeDtypeStruct((T, D), BF16),
                   jax.ShapeDtypeStruct((1, D), F32)],
        compiler_params=_cparams(("arbitrary",)),
    )(dhn, h, gain, resid)


def _loss_head(h, gain, tgt):
    def body(h_ref, g_ref, t_ref, l_ref, dh_ref, dh16_ref, dg_ref):
        loss, vjp = jax.vjp(lambda hh, gg: _loss_fn(hh, gg, t_ref[...]), h_ref[...], g_ref[...])
        dh, dg = vjp(jnp.ones((), F32))
        dh_ref[...] = dh
        dh16_ref[...] = dh.astype(BF16)
        lv = jnp.full((1, LANE), loss, F32)

        @pl.when(pl.program_id(0) == 0)
        def _():
            dg_ref[...] = dg
            l_ref[...] = lv

        @pl.when(pl.program_id(0) > 0)
        def _():
            dg_ref[...] += dg
            l_ref[...] += lv

    return pl.pallas_call(
        body, name="loss_head", grid=(T // RT,), in_specs=[_rows(), _whole((1, D)), _rows()],
        out_specs=[_whole((1, LANE)), _rows(), _rows(), _whole((1, D))],
        out_shape=[jax.ShapeDtypeStruct((1, LANE), F32), jax.ShapeDtypeStruct((T, D), F32),
                   jax.ShapeDtypeStruct((T, D), BF16), jax.ShapeDtypeStruct((1, D), F32)],
        compiler_params=_cparams(("arbitrary",)),
    )(h, gain, tgt)


def _prep_flags():
    j = pl.program_id(0)
    qk_scale = jnp.where(j < HEADS, HD ** -0.5, 1.0).astype(F32)
    return qk_scale, j >= 2 * HEADS


def _prep(proj, conv_w):
    def body(x_ref, w_ref, o_ref):
        qk_scale, is_v = _prep_flags()
        o_ref[...] = _prep_fn(x_ref[...], w_ref[...], qk_scale, is_v)

    col = lambda j: (0, j)
    return pl.pallas_call(
        body, name="gdn_prep", grid=(3 * HEADS,),
        in_specs=[pl.BlockSpec((T, HD), col), pl.BlockSpec((4, HD), col)], out_specs=pl.BlockSpec((T, HD), col),
        out_shape=jax.ShapeDtypeStruct((T, 3 * HW), F32), compiler_params=_cparams(("parallel",)),
    )(proj, conv_w)


def _prep_bwd(proj, conv_w, dqkv, dproj):
    def body(x_ref, w_ref, d_ref, _, dx_ref, dw_ref):
        qk_scale, is_v = _prep_flags()
        _, vjp = jax.vjp(lambda x, w: _prep_fn(x, w, qk_scale, is_v), x_ref[...], w_ref[...])
        dx, dw = vjp(d_ref[...])
        dx_ref[...] = dx.astype(BF16)
        dw_ref[...] = dw

    col = lambda j: (0, j)
    return pl.pallas_call(
        body, name="gdn_prep_bwd", grid=(3 * HEADS,),
        in_specs=[pl.BlockSpec((T, HD), col), pl.BlockSpec((4, HD), col), pl.BlockSpec((T, HD), col), ANY],
        out_specs=[pl.BlockSpec((T, HD), col), pl.BlockSpec((4, HD), col)],
        out_shape=[jax.ShapeDtypeStruct((T, NP), BF16), jax.ShapeDtypeStruct((4, 3 * HW), F32)],
        input_output_aliases={3: 0}, compiler_params=_cparams(("parallel",)),
    )(proj, conv_w, dqkv, dproj)


BA_BLK = BA_OFF // LANE


def _gates(proj, a_log, dt_bias):
    def body(x_ref, a_ref, d_ref, o_ref):
        o_ref[...] = _gates_fn(x_ref[...], a_ref[...], d_ref[...])

    return pl.pallas_call(
        body, name="gdn_gates", grid=(1,),
        in_specs=[pl.BlockSpec((T, LANE), lambda i: (0, BA_BLK)), _whole((1, LANE)), _whole((1, LANE))],
        out_specs=_whole((T, LANE)),
        out_shape=jax.ShapeDtypeStruct((T, LANE), F32), compiler_params=_cparams(("arbitrary",)),
    )(proj, a_log, dt_bias)


def _gates_bwd(proj, a_log, dt_bias, dbg, dproj):
    def body(x_ref, a_ref, d_ref, dbg_ref, _, dx_ref, da_ref, dd_ref):
        _, vjp = jax.vjp(_gates_fn, x_ref[...], a_ref[...], d_ref[...])
        dx, da_ref[...], dd_ref[...] = vjp(dbg_ref[...])
        dx_ref[...] = dx.astype(BF16)

    ba = pl.BlockSpec((T, LANE), lambda i: (0, BA_BLK))
    return pl.pallas_call(
        body, name="gdn_gates_bwd", grid=(1,),
        in_specs=[ba, _whole((1, LANE)), _whole((1, LANE)), _whole((T, LANE)), ANY],
        out_specs=[ba, _whole((1, LANE)), _whole((1, LANE))],
        out_shape=[jax.ShapeDtypeStruct((T, NP), BF16), jax.ShapeDtypeStruct((1, LANE), F32),
                   jax.ShapeDtypeStruct((1, LANE), F32)],
        input_output_aliases={4: 0}, compiler_params=_cparams(("arbitrary",)),
    )(proj, a_log, dt_bias, dbg, dproj)


NCK = T // CH
CPS = 4


def _chunk_group_specs(at):
    wide = pl.BlockSpec((CPS * CH, HW), lambda n: (at(n), 0))
    return [wide, wide, wide, wide, pl.BlockSpec((HEADS, CPS * CH, CH), lambda n: (0, at(n), 0)),
            pl.BlockSpec((CPS, 1, HW), lambda n: (at(n), 0, 0))]


def _chunk_prep_shapes(dtypes):
    shp = [(T, HW), (T, HW), (T, HW), (T, HW), (HEADS, T, CH), (NCK, 1, HW)]
    return [jax.ShapeDtypeStruct(s, dt) for s, dt in zip(shp, dtypes)]


NGROUP = NCK // CPS
PREP_DTYPES = (F32, BF16, BF16, BF16, BF16, F32)


def _delta_rule(qkv, bg):
    def body(x_ref, bg_ref, *refs):
        prep_out, (o_ref, sh_ref), held, s_ref = refs[:6], refs[6:8], refs[8:14], refs[14]
        i = pl.program_id(0)

        @pl.when(i == 0)
        def _():
            for r in held + (s_ref,):
                r[...] = jnp.zeros_like(r)

        rows = [slice(ci * CH, (ci + 1) * CH) for ci in range(CPS)]
        u_h, w_h, qe_h, kd_h, qk_h, egl_h = held
        chunks = [_head_args((u_h.at[r, :], w_h.at[r, :], qe_h.at[r, :], kd_h.at[r, :], qk_h.at[:, r, :], egl_h.at[ci]))
                  for ci, r in enumerate(rows)]
        start = [jnp.where(i <= 1, 0.0, s_ref[h]) for h in range(HEADS)]
        outs, befores, final = [], [], []
        levels = _state_levels(chunks, start, outs, befores, final)
        res = _chunk_prep_fn([x_ref[r, :] for r in rows], [bg_ref[r, :] for r in rows], tick=lambda: next(levels, None))
        for _ in levels:
            pass
        for ci, (u, w, qe, kd, qk, egl) in enumerate(res):
            for refs_pair, val in zip(zip(prep_out[:4], held[:4]), (u, w, qe, kd)):
                for ref in refs_pair:
                    ref[rows[ci], :] = val.astype(ref.dtype)
            for ref in (prep_out[4], qk_h):
                ref[:, rows[ci], :] = qk.astype(ref.dtype)
            for ref in (prep_out[5], egl_h):
                ref[ci] = egl
        for ci, r in enumerate(rows):
            for h in range(HEADS):
                o_ref[r, h * HD:(h + 1) * HD] = outs[ci][h]
                sh_ref[h, ci] = befores[ci][h]
        for h in range(HEADS):
            s_ref[h] = final[0][h]

    now = lambda n: jnp.minimum(n, NGROUP - 1)
    was = lambda n: jnp.maximum(n - 1, 0)
    wide = lambda at: pl.BlockSpec((CPS * CH, HW), lambda n: (at(n), 0))
    held = [pltpu.VMEM(s, dt) for s, dt in zip(
        [(CPS * CH, HW)] * 4 + [(HEADS, CPS * CH, CH), (CPS, 1, HW)], PREP_DTYPES)]
    out = pl.pallas_call(
        body, name="gdn_delta_rule", grid=(NGROUP + 1,),
        in_specs=[pl.BlockSpec((CPS * CH, 3 * HW), lambda n: (now(n), 0)),
                  pl.BlockSpec((CPS * CH, LANE), lambda n: (now(n), 0))],
        out_specs=[wide(now)] * 4 + [pl.BlockSpec((HEADS, CPS * CH, CH), lambda n: (0, now(n), 0)),
                                     pl.BlockSpec((CPS, 1, HW), lambda n: (now(n), 0, 0)), wide(was),
                                     pl.BlockSpec((HEADS, CPS, HD, HD), lambda n: (0, was(n), 0, 0))],
        out_shape=_chunk_prep_shapes(PREP_DTYPES) + [jax.ShapeDtypeStruct((T, HW), F32),
                                                     jax.ShapeDtypeStruct((HEADS, NCK, HD, HD), F32)],
        scratch_shapes=held + [pltpu.VMEM((HEADS, HD, HD), F32)], compiler_params=_cparams(("arbitrary",)),
    )(qkv, bg)
    return out[:6], out[6], out[7]


def _chunk_prep_bwd(qkv, bg, cots):
    def body(x_ref, bg_ref, du, dw, dqe, dkd, dqk, degl, dx_ref, dbg_ref):
        rows = [slice(ci * CH, (ci + 1) * CH) for ci in range(CPS)]
        _, vjp = jax.vjp(_chunk_prep_fn, [x_ref[r, :] for r in rows], [bg_ref[r, :] for r in rows])
        dxs, dbgs = vjp([(du[r, :], dw[r, :], dqe[r, :], dkd[r, :], dqk[:, r, :], degl[ci])
                         for ci, r in enumerate(rows)])
        for r, dx, dbg in zip(rows, dxs, dbgs):
            dx_ref[r, :] = dx
            dbg_ref[r, :] = dbg

    wide = pl.BlockSpec((CPS * CH, HW), lambda n: (n, 0))
    return pl.pallas_call(
        body, name="gdn_chunk_prep_bwd", grid=(NCK // CPS,),
        in_specs=[pl.BlockSpec((CPS * CH, 3 * HW), lambda n: (n, 0)), pl.BlockSpec((CPS * CH, LANE), lambda n: (n, 0)),
                  wide, wide, wide, wide, pl.BlockSpec((HEADS, CPS * CH, CH), lambda n: (0, n, 0)),
                  pl.BlockSpec((CPS, 1, HW), lambda n: (n, 0, 0))],
        out_specs=[pl.BlockSpec((CPS * CH, 3 * HW), lambda n: (n, 0)), pl.BlockSpec((CPS * CH, LANE), lambda n: (n, 0))],
        out_shape=[jax.ShapeDtypeStruct((T, 3 * HW), F32), jax.ShapeDtypeStruct((T, LANE), F32)],
        compiler_params=_cparams(("parallel",)),
    )(qkv, bg, *cots)


def _head_args(refs):
    u, w, qe, kd, qk, egl = refs
    sls = [slice(h * HD, (h + 1) * HD) for h in range(HEADS)]
    return ([u[:, sl] for sl in sls], [w[:, sl].astype(F32) for sl in sls], [qe[:, sl].astype(F32) for sl in sls],
            [kd[:, sl].astype(F32) for sl in sls], [qk[h].astype(F32) for h in range(HEADS)],
            [egl[:, sl] for sl in sls])


def _chunk_scan_bwd(prep, s_hist, do, after=()):
    n_in = 8 + len(after)

    def body(*refs):
        u_r, w_r, qe_r, kd_r, qk_r, egl_r, sh_ref, do_ref = refs[:8]
        d_refs = refs[n_in:n_in + 6]
        ds_ref = refs[n_in + 6]

        @pl.when(pl.program_id(0) == 0)
        def _():
            ds_ref[...] = jnp.zeros_like(ds_ref)

        sls = [slice(h * HD, (h + 1) * HD) for h in range(HEADS)]
        ds = [ds_ref[h] for h in range(HEADS)]
        for ci in reversed(range(CPS)):
            r = slice(ci * CH, (ci + 1) * CH)
            args = _head_args((u_r.at[r, :], w_r.at[r, :], qe_r.at[r, :], kd_r.at[r, :], qk_r.at[:, r, :], egl_r.at[ci]))
            _, vjp = jax.vjp(_chunk_state_fn, *args, [sh_ref[h, ci] for h in range(HEADS)])
            du, dw, dqe, dkd, dqk, degl, ds = vjp(([do_ref[r, sl] for sl in sls], ds))
            for h, sl in enumerate(sls):
                for d_ref, val in zip(d_refs[:4], (du, dw, dqe, dkd)):
                    d_ref[r, sl] = val[h]
                d_refs[4][h, r, :] = dqk[h]
                d_refs[5][ci, :, sl] = degl[h]
        for h in range(HEADS):
            ds_ref[h] = ds[h]

    rev = lambda n: NGROUP - 1 - n
    return pl.pallas_call(
        body, name="gdn_scan_bwd", grid=(NGROUP,),
        in_specs=_chunk_group_specs(rev) + [pl.BlockSpec((HEADS, CPS, HD, HD), lambda n: (0, rev(n), 0, 0)),
                                            pl.BlockSpec((CPS * CH, HW), lambda n: (rev(n), 0))] + [ANY] * len(after),
        out_specs=_chunk_group_specs(rev), out_shape=_chunk_prep_shapes((F32,) * 6),
        scratch_shapes=[pltpu.VMEM((HEADS, HD, HD), F32)], compiler_params=_cparams(("arbitrary",)),
    )(*prep, s_hist, do, *after)


MIX_ROWS = 4 * GCH


def _mix_rows(o, z, ur, vr, *params):
    chunks = [slice(c * GCH, (c + 1) * GCH) for c in range(MIX_ROWS // GCH)]
    return jnp.concatenate([_mix_fn(o[r], z[r], ur[r], vr[r], *params) for r in chunks], axis=0)


def _mix_specs():
    pc = lambda c: pl.BlockSpec((MIX_ROWS, HW), lambda i: (i, c))
    return [pl.BlockSpec((MIX_ROWS, HW), lambda i: (i, 0)), pc(3), pc(4), pc(5), _whole((1, HD)), _whole((1, HW)),
            _whole((1, HW)), _whole((HEADS, GCH, GCH)), _whole((GCH, LANE))]


def _mix(o, proj, ong, lng, lnb, ws, bst, after=()):
    def body(o_ref, z_ref, u_ref, v_ref, ong_ref, lng_ref, lnb_ref, ws_ref, bs_ref, *rest):
        rest[-1][...] = _mix_rows(o_ref[...], z_ref[...], u_ref[...], v_ref[...], ong_ref[...], lng_ref[...],
                                  lnb_ref[...], ws_ref[...], bs_ref[...]).astype(BF16)

    return pl.pallas_call(
        body, name="mix", grid=(T // MIX_ROWS,), in_specs=_mix_specs() + [ANY] * len(after),
        out_specs=pl.BlockSpec((MIX_ROWS, D), lambda i: (i, 0)), out_shape=jax.ShapeDtypeStruct((T, D), BF16),
        compiler_params=_cparams(("parallel",)),
    )(o, proj, proj, proj, ong, lng, lnb, ws, bst, *after)


def _mix_bwd(o, proj, ong, lng, lnb, ws, bst, dmix):
    def body(o_ref, z_ref, u_ref, v_ref, ong_ref, lng_ref, lnb_ref, ws_ref, bs_ref, dm_ref,
             do_ref, dzuv_ref, dong_ref, dlng_ref, dlnb_ref, dws_ref, dbs_ref):
        _, vjp = jax.vjp(_mix_rows, o_ref[...], z_ref[...], u_ref[...], v_ref[...], ong_ref[...], lng_ref[...],
                         lnb_ref[...], ws_ref[...], bs_ref[...])
        do, dz, du, dv, dong, dlng, dlnb, dws, dbs = vjp(dm_ref[...])
        do_ref[...] = do
        dzuv_ref[:, 0:HW] = dz.astype(BF16)
        dzuv_ref[:, HW:2 * HW] = du.astype(BF16)
        dzuv_ref[:, 2 * HW:3 * HW] = dv.astype(BF16)
        acc = [(dong_ref, dong), (dlng_ref, dlng), (dlnb_ref, dlnb), (dws_ref, dws), (dbs_ref, dbs)]

        @pl.when(pl.program_id(0) == 0)
        def _():
            for r, val in acc:
                r[...] = val

        @pl.when(pl.program_id(0) > 0)
        def _():
            for r, val in acc:
                r[...] += val

    shp = lambda *s: jax.ShapeDtypeStruct(s, F32)
    return pl.pallas_call(
        body, name="mix_bwd", grid=(T // MIX_ROWS,),
        in_specs=_mix_specs() + [pl.BlockSpec((MIX_ROWS, D), lambda i: (i, 0))],
        out_specs=[pl.BlockSpec((MIX_ROWS, HW), lambda i: (i, 0)), pl.BlockSpec((MIX_ROWS, 3 * HW), lambda i: (i, 1)),
                   _whole((1, HD)), _whole((1, HW)), _whole((1, HW)), _whole((HEADS, GCH, GCH)), _whole((GCH, LANE))],
        out_shape=[shp(T, HW), jax.ShapeDtypeStruct((T, NP), BF16), shp(1, HD), shp(1, HW), shp(1, HW),
                   shp(HEADS, GCH, GCH), shp(GCH, LANE)],
        compiler_params=_cparams(("arbitrary",)),
    )(o, proj, proj, proj, ong, lng, lnb, ws, bst, dmix)


def _swiglu_epilogue(accs, _):
    gate, up = accs
    return [gate, up, _silu(gate) * up]


def _swiglu_bwd_epilogue(accs, extras):
    dact = accs[0]
    gate, up = (e.astype(F32) for e in extras)
    sg = _sigmoid(gate)
    return [dact * up * (sg * (1.0 + gate * (1.0 - sg))), dact * (gate * sg)]


def _layer_fwd(h, p):
    hn = p.pop("hn") if "hn" in p else _rmsnorm("rms_mix", h, p["norm_mix"])
    proj = _mm("in_proj", "nn", hn[None], [p["w_in"][None]], tm=512, tn=NP, tk=D, sub_m=2)[0][0]
    qkv = _prep(proj, p["conv_w"])
    bg = _gates(proj, p["a_log"], p["dt_bias"])
    prep, o, s_hist = _delta_rule(qkv, bg)
    mix = _mix(o, proj, p["o_norm_g"], p["ln_v_g"], p["ln_v_b"], p["w_s"], p["bst"],
               p.pop("before_mix")(o) if "before_mix" in p else ())
    if "late" in p:
        p.update(p.pop("late")(mix))
    h1 = _mm("out_proj", "nn", mix[None], [p["w_out"].reshape(1, D, D)], tm=T, tn=512, tk=D, resid=h[None],
             sub_m=4)[0][0]
    h2n = _rmsnorm("rms_ffn", h1, p["norm_ffn"])
    gate, up, act = _mm("ffn_in", "nt", h2n[None], [p["w_gate"], p["w_up"]], tm=T, tn=FF_SH, tk=D,
                        out_dtypes=(BF16, BF16, BF16), epilogue=_swiglu_epilogue, sub_m=4)
    then = p.pop("before_ffn_out")(act) if "before_ffn_out" in p else ()
    h2 = _mm("ffn_out", "nn", act, [p["w_down"]], tm=1024, tn=512, tk=FF_SH, reduce_g=True, fold_g=True,
             resid=h1[None], sub_m=2, after=then)[0][0]
    saved = dict(h=h, hn=hn, proj=proj, qkv=qkv, bg=bg, prep=prep, o=o, s_hist=s_hist, mix=mix, h1=h1, h2n=h2n,
                 gate=gate, up=up, act=act)
    return h2, saved


def _layer_bwd_ffn(dh2, dh2b, p, s, after=()):
    dh2b = dh2b[None]
    dgate, dup = _mm("ffn_out_bwd", "nt", dh2b, [p["w_down"]], tm=T, tn=FF_SH, tk=D, out_dtypes=(BF16, BF16),
                     extras=(s["gate"], s["up"]), epilogue=_swiglu_bwd_epilogue, after=after, sub_m=4)
    dh2n = _mm("ffn_gate_bwd", "nn", dgate, [p["w_gate"]], tm=1024, tn=512, tk=FF_SH, reduce_g=True, fold_g=True,
               sub_m=2)[0]
    dh2n = _mm("ffn_up_bwd", "nn", dup, [p["w_up"]], tm=1024, tn=512, tk=FF_SH, reduce_g=True, fold_g=True,
               resid=dh2n, sub_m=2)[0][0]
    dh1, dh1b, d_norm_ffn = _rmsnorm_bwd("rms_ffn_bwd", dh2n, s["h1"], p["norm_ffn"], dh2)
    d_w_down = _mm("ffn_wdown_grad", "tn", s["act"], [dh2b], tm=FF_SH, tn=512, tk=T)[0]
    d_w_gate = _mm("ffn_wgate_grad", "tn", dgate, [s["h2n"][None]], tm=FF_SH, tn=512, tk=T)[0]
    d_w_up = _mm("ffn_wup_grad", "tn", dup, [s["h2n"][None]], tm=FF_SH, tn=512, tk=T)[0]
    return dh1, dh1b, dict(norm_ffn=d_norm_ffn, w_gate=d_w_gate, w_up=d_w_up, w_down=d_w_down)


def _layer_bwd_mixer(dh1, dh1b, p, s, after=(), midway=None, late=None):
    dh1b = dh1b[None]
    dmix = _mm("out_proj_bwd", "nt", dh1b, [p["w_out"].reshape(1, D, D)], tm=T, tn=512, tk=D, after=after,
               sub_m=4)[0][0]
    d_w_out = _mm("out_proj_wgrad", "tn", s["mix"][None], [dh1b], tm=1024, tn=512, tk=T)[0][0]
    do, dproj, d_ong, d_lng, d_lnb, d_ws, d_bst = _mix_bwd(
        s["o"], s["proj"], p["o_norm_g"], p["ln_v_g"], p["ln_v_b"], p["w_s"], p["bst"], dmix)
    then = midway(do) if midway is not None else ()
    dqkv, dbg = _chunk_prep_bwd(s["qkv"], s["bg"], _chunk_scan_bwd(s["prep"], s["s_hist"], do, then))
    dproj, d_conv = _prep_bwd(s["proj"], p["conv_w"], dqkv, dproj)
    dproj, d_a_log, d_dt_bias = _gates_bwd(s["proj"], p["a_log"], p["dt_bias"], dbg, dproj)
    dproj = dproj[None]
    d_w_in = _mm("in_proj_wgrad", "tn", s["hn"][None], [dproj], tm=512, tn=640, tk=T)[0]
    last = late(dict(w_in=d_w_in, w_out=d_w_out)) if late is not None else ()
    dhn = _mm("in_proj_bwd", "nt", dproj, [p["w_in"][None]], tm=1024, tn=512, tk=NP, after=last,
              sub_m=2)[0][0]
    dh, dhb, d_norm_mix = _rmsnorm_bwd("rms_mix_bwd", dhn, s["h"], p["norm_mix"], dh1)
    grads = dict(norm_mix=d_norm_mix, w_in=d_w_in, conv_w=d_conv, a_log=d_a_log, dt_bias=d_dt_bias, o_norm_g=d_ong,
                 ln_v_g=d_lng, ln_v_b=d_lnb, w_s=d_ws, bst=d_bst, w_out=d_w_out)
    return dh, dhb, grads


def _lanes(v, off=0):
    return jnp.zeros((1, LANE), F32).at[0, off:off + v.shape[0]].set(v)


def _w_in_pieces():
    regions = [(0, 2048, 0), (2048, 2056, BA_OFF), (2056, IN_DIM, 2048)]
    sh = IN_DIM // NCHIP
    out = []
    for j in range(NCHIP):
        for lo, hi, at in regions:
            a, b = max(lo, j * sh), min(hi, (j + 1) * sh)
            if a < b:
                out.append((j, a - j * sh, at + a - lo, b - a))
    return out


W_IN_PIECES = _w_in_pieces()
WT = 256


def _assemble_w_in(gathered, own, place):
    def body(place_ref, g_ref, own_ref, o_ref):
        o_ref[:, IN_DIM:] = jnp.zeros((WT, NP - IN_DIM), BF16)
        mine = own_ref[...]
        for j, src, dst, width in W_IN_PIECES:
            val = jnp.where(place_ref[0] == j, mine[:, src:src + width], g_ref[j, :, src:src + width])
            o_ref[:, dst:dst + width] = val

    sh = IN_DIM // NCHIP
    return pl.pallas_call(
        body, name="assemble_w_in",
        grid_spec=pltpu.PrefetchScalarGridSpec(
            num_scalar_prefetch=1, grid=(D // WT,),
            in_specs=[pl.BlockSpec((NCHIP, WT, sh), lambda i, place_ref: (0, i, 0)),
                      pl.BlockSpec((WT, sh), lambda i, place_ref: (i, 0))],
            out_specs=pl.BlockSpec((WT, NP), lambda i, place_ref: (i, 0))),
        out_shape=jax.ShapeDtypeStruct((D, NP), BF16), compiler_params=_cparams(("parallel",)),
    )(place, gathered, own)


def _layer_params(l, big, small):
    return dict(
        {k: v for k, v in big.items() if k != "conv_w"},
        conv_w=jnp.concatenate([big["conv_w"][j, l] for j in range(NCHIP)], axis=1),
        norm_mix=small["norm_mix"][l][None], norm_ffn=small["norm_ffn"][l][None],
        a_log=_lanes(small["a_log"][l], HEADS), dt_bias=_lanes(small["dt_bias"][l], HEADS),
        o_norm_g=small["o_norm_g"][l][None], ln_v_g=small["ln_v_g"][l][None], ln_v_b=small["ln_v_b"][l][None],
        w_s=small["w_s"][l],
        bst=jnp.pad(small["b_s"][l].T, ((0, 0), (0, LANE - HEADS))),
    )


def _reference_layout(g):
    return dict(
        w_in=g["w_in"],
        w_out=g["w_out"].reshape(NCHIP, D // NCHIP, D),
        w_gate=g["w_gate"], w_up=g["w_up"], w_down=g["w_down"],
        conv_w=g["conv_w"], norm_mix=g["norm_mix"][0], norm_ffn=g["norm_ffn"][0],
        a_log=g["a_log"][0, HEADS:2 * HEADS], dt_bias=g["dt_bias"][0, HEADS:2 * HEADS],
        o_norm_g=g["o_norm_g"][0], ln_v_g=g["ln_v_g"][0], ln_v_b=g["ln_v_b"][0], w_s=g["w_s"],
        b_s=g["bst"][:, :HEADS].T,
    )


def _forward(x, tgt, layers, norm_final):
    h = x
    saved, params = [], []
    for p in layers:
        p = p(h) if callable(p) else p
        h, s = _layer_fwd(h, p)
        saved.append(s)
        params.append(p)
    return (saved, params) + tuple(_loss_head(h, norm_final, tgt))


def _local_step(x, tgt, layers, norm_final):
    saved, layers, loss, dh, dhb, d_norm_final = _forward(x, tgt, layers, norm_final)
    grads = [None] * DEPTH
    for l in reversed(range(DEPTH)):
        dh1, dh1b, g_ffn = _layer_bwd_ffn(dh, dhb, layers[l], saved[l])
        dh, dhb, g_mix = _layer_bwd_mixer(dh1, dh1b, layers[l], saved[l])
        grads[l] = {**g_ffn, **g_mix}
    return loss, dh, grads, d_norm_final


def _place():
    x, y, c = lax.axis_index("x"), lax.axis_index("y"), lax.axis_index("c")
    return x, y, c, [(1 - x, y), (x, 1 - y), (1 - x, 1 - y)]


def _remote(src, dst, send_sem, recv_sem, to):
    return pltpu.make_async_remote_copy(src_ref=src, dst_ref=dst, send_sem=send_sem, recv_sem=recv_sem,
                                        device_id=to, device_id_type=MESH)


def _comm_call(name, body, ins, out_shape, n_sems, aliases=None):
    return pl.pallas_call(
        body, name=name, in_specs=[ANY] * len(ins), out_specs=[ANY] * len(out_shape), out_shape=out_shape,
        scratch_shapes=[pltpu.SemaphoreType.DMA((n,)) for n in n_sems], input_output_aliases=aliases or {},
        compiler_params=pltpu.CompilerParams(has_side_effects=True),
    )(*ins)


def _half_rows(ref, of_c, dim):
    hr = ref.shape[dim] // 2
    return pl.ds(pl.multiple_of(of_c * hr, BF16_ROWS), hr)


def _gather_plan(whole):
    def plan(srcs, lands):
        x, y, c, others = _place()
        chip = 2 * x + y
        out = []
        for src, land, all_of_it in zip(srcs, lands, whole):
            for ox, oy in others:
                if all_of_it:
                    out.append((src, land.at[chip], (ox, oy, c)))
                else:
                    out.append((src.at[_half_rows(src, c, 0)], land.at[chip, _half_rows(src, c, 0)], (ox, oy, c)))
        return out
    return plan


def _forward_halves(lands):
    n = len(lands)

    def body(*refs):
        outs = refs[n:2 * n]
        send_s, recv_s = refs[2 * n:]
        x, y, c, others = _place()
        sibling = (x, y, 1 - c)
        copies = []
        for a in range(n):
            for k, (ox, oy) in enumerate(others):
                mine = outs[a].at[2 * ox + oy, _half_rows(outs[a], c, 1)]
                copies.append(_remote(mine, mine, send_s.at[3 * a + k], recv_s.at[3 * a + k], sibling))
        for cp in copies:
            cp.start()
        for a in range(n):
            for k, (ox, oy) in enumerate(others):
                landed = outs[a].at[2 * ox + oy, _half_rows(outs[a], 1 - c, 1)]
                _remote(landed, landed, send_s.at[3 * a + k], recv_s.at[3 * a + k], sibling).wait_recv()
        for cp in copies:
            cp.wait_send()

    out_shape = [jax.ShapeDtypeStruct(g.shape, g.dtype) for g in lands]
    return _comm_call("forward_halves", body, lands, out_shape, [3 * n, 3 * n], aliases={a: a for a in range(n)})


def _forward_refs(bufs, incoming):
    x, y, c, others = _place()
    return (x, y, 1 - c), [b.at[2 * ox + oy, _half_rows(b, 1 - c if incoming else c, 1)]
                           for b in bufs for ox, oy in others]


def _forward_start(name, bufs, after):
    n = len(bufs)
    bufs = [pltpu.with_memory_space_constraint(b, pltpu.HBM) for b in bufs]

    def body(*refs):
        send_s, recv_s = refs[n + len(after)], refs[n + len(after) + 1]
        sibling, mine = _forward_refs(refs[:n], incoming=False)
        for i, ref in enumerate(mine):
            _remote(ref, ref, send_s.at[i], recv_s.at[i], sibling).start()
        refs[-1][...] = jnp.zeros_like(refs[-1])

    out = pl.pallas_call(
        body, name=name, in_specs=[HBM_SPEC] * n + [ANY] * len(after),
        out_specs=[SEM_SPEC, SEM_SPEC] + [HBM_SPEC] * n + [pl.BlockSpec(memory_space=pltpu.VMEM)],
        out_shape=[pltpu.SemaphoreType.DMA((3 * n,)), pltpu.SemaphoreType.DMA((3 * n,))]
        + [pltpu.HBM(b.shape, b.dtype) for b in bufs] + [jax.ShapeDtypeStruct((F32_ROWS, LANE), F32)],
        input_output_aliases={i: 2 + i for i in range(n)},
        compiler_params=pltpu.CompilerParams(has_side_effects=DATAFLOW),
    )(*bufs, *after)
    return dict(sems=out[:2], bufs=out[2:2 + n], token=out[-1])


def _forward_wait(name, started, after):
    n = len(started["bufs"])

    def body(*refs):
        send_s, recv_s = refs[n], refs[n + 1]
        sibling, mine = _forward_refs(refs[:n], incoming=False)
        _, theirs = _forward_refs(refs[:n], incoming=True)
        for i, (sent, landed) in enumerate(zip(mine, theirs)):
            _remote(sent, sent, send_s.at[i], recv_s.at[i], sibling).wait_send()
            _remote(landed, landed, send_s.at[i], recv_s.at[i], sibling).wait_recv()

    return pl.pallas_call(
        body, name=name, in_specs=[HBM_SPEC] * n + [SEM_SPEC, SEM_SPEC] + [ANY] * len(after),
        out_specs=[HBM_SPEC] * n, out_shape=[pltpu.HBM(b.shape, b.dtype) for b in started["bufs"]],
        input_output_aliases={i: i for i in range(n)},
        compiler_params=pltpu.CompilerParams(has_side_effects=DATAFLOW),
    )(*started["bufs"], *started["sems"], *after)


HBM_SPEC = pl.BlockSpec(memory_space=pltpu.HBM)
SEM_SPEC = pl.BlockSpec(memory_space=pltpu.SEMAPHORE)
DATAFLOW = pltpu.SideEffectType.DATAFLOW_SIDE_EFFECTING


def _exchange_plan(srcs, lands):
    x, y, c, _ = _place()
    plan = []
    for src, land in zip(srcs, lands):
        hr = src.shape[1] // 2
        plan.append((src.at[:, pl.ds(pl.multiple_of((1 - c) * hr, 8), hr)], land, (x, y, 1 - c)))
    return plan


def _scatter_plan(srcs, lands):
    x, y, c, others = _place()
    return [(src.at[2 * ox + oy], land.at[k], (ox, oy, c))
            for src, land in zip(srcs, lands) for k, (ox, oy) in enumerate(others)]


def _split_start(name, plan, srcs, land_shapes, n_copies, after=()):
    n = len(srcs)
    lands = [pltpu.with_memory_space_constraint(lax.empty(s.shape, s.dtype), pltpu.HBM) for s in land_shapes]
    srcs = [pltpu.with_memory_space_constraint(s, pltpu.HBM) for s in srcs]

    def body(*refs):
        send_s, recv_s = refs[2 * n + len(after)], refs[2 * n + len(after) + 1]
        for i, (src, dst, to) in enumerate(plan(refs[:n], refs[n:2 * n])):
            _remote(src, dst, send_s.at[i], recv_s.at[i], to).start()
        refs[-1][...] = jnp.zeros_like(refs[-1])

    thru = [pltpu.HBM(s.shape, s.dtype) for s in srcs + lands]
    out = pl.pallas_call(
        body, name=name, in_specs=[HBM_SPEC] * (2 * n) + [ANY] * len(after),
        out_specs=[SEM_SPEC, SEM_SPEC] + [HBM_SPEC] * (2 * n) + [pl.BlockSpec(memory_space=pltpu.VMEM)],
        out_shape=[pltpu.SemaphoreType.DMA((n_copies,)), pltpu.SemaphoreType.DMA((n_copies,))] + thru
        + [jax.ShapeDtypeStruct((F32_ROWS, LANE), F32)],
        input_output_aliases={i: 2 + i for i in range(2 * n)},
        compiler_params=pltpu.CompilerParams(has_side_effects=DATAFLOW),
    )(*srcs, *lands, *after)
    return dict(sems=out[:2], srcs=out[2:2 + n], lands=out[2 + n:2 + 2 * n], token=out[-1])


def _split_wait(name, plan, started, after):
    n = len(started["srcs"])
    after = list(after) if isinstance(after, (list, tuple)) else [after]

    def body(*refs):
        send_s, recv_s = refs[2 * n], refs[2 * n + 1]
        for i, (src, dst, to) in enumerate(plan(refs[:n], refs[n:2 * n])):
            cp = _remote(src, dst, send_s.at[i], recv_s.at[i], to)
            cp.wait_send()
            cp.wait_recv()

    arrs = list(started["srcs"]) + list(started["lands"])
    out = pl.pallas_call(
        body, name=name, in_specs=[HBM_SPEC] * (2 * n) + [SEM_SPEC, SEM_SPEC] + [ANY] * len(after),
        out_specs=[HBM_SPEC] * (2 * n), out_shape=[pltpu.HBM(s.shape, s.dtype) for s in arrs],
        input_output_aliases={i: i for i in range(2 * n)},
        compiler_params=pltpu.CompilerParams(has_side_effects=DATAFLOW),
    )(*arrs, *started["sems"], *after)
    return out[:n], out[n:]


def _join_halves(name, rs):
    n = len(rs)

    def body(*refs):
        outs = refs[n:2 * n]
        send_s, recv_s = refs[2 * n:]
        x, y, c, _ = _place()
        sibling = (x, y, 1 - c)

        def half(a, of_c):
            hr = outs[a].shape[1] // 2
            return outs[a].at[:, pl.ds(pl.multiple_of(of_c * hr, 8), hr)]

        copies = [_remote(half(a, c), half(a, c), send_s.at[a], recv_s.at[a], sibling) for a in range(n)]
        for cp in copies:
            cp.start()
        for a in range(n):
            landed = half(a, 1 - c)
            _remote(landed, landed, send_s.at[a], recv_s.at[a], sibling).wait_recv()
        for cp in copies:
            cp.wait_send()

    out_shape = [jax.ShapeDtypeStruct(r.shape, r.dtype) for r in rs]
    return _comm_call(name, body, rs, out_shape, [n, n], aliases={a: a for a in range(n)})


def _allreduce_small(buf, after=()):
    r = buf.shape[0]
    hr = r // 2

    def body(in_ref, *refs):
        out_ref, theirs, by_chip, send_s, recv_s = refs[len(after):]
        x, y, c, others = _place()
        chip = 2 * x + y
        sibling = (x, y, 1 - c)
        mine = pl.ds(pl.multiple_of(c * hr, F32_ROWS), hr)
        swap = _remote(in_ref, theirs, send_s.at[0], recv_s.at[0], sibling)
        swap.start()
        swap.wait()
        by_chip[chip] = in_ref[mine, :] + theirs[mine, :]
        sends = [_remote(by_chip.at[chip], by_chip.at[chip], send_s.at[1 + k], recv_s.at[1 + k], (ox, oy, c))
                 for k, (ox, oy) in enumerate(others)]
        for cp in sends:
            cp.start()
        for k, (ox, oy) in enumerate(others):
            landed = by_chip.at[2 * ox + oy]
            _remote(landed, landed, send_s.at[1 + k], recv_s.at[1 + k], (ox, oy, c)).wait_recv()
        for cp in sends:
            cp.wait_send()
        out_ref[mine, :] = (by_chip[0] + by_chip[1]) + (by_chip[2] + by_chip[3])
        back = _remote(out_ref.at[mine], out_ref.at[mine], send_s.at[NCHIP], recv_s.at[NCHIP], sibling)
        back.start()
        other = out_ref.at[pl.ds(pl.multiple_of((1 - c) * hr, F32_ROWS), hr)]
        _remote(other, other, send_s.at[NCHIP], recv_s.at[NCHIP], sibling).wait_recv()
        back.wait_send()

    vm = pl.BlockSpec(memory_space=pltpu.VMEM)
    return pl.pallas_call(
        body, name="allreduce_small", in_specs=[vm] + [ANY] * len(after), out_specs=vm,
        out_shape=jax.ShapeDtypeStruct((r, LANE), F32),
        scratch_shapes=[pltpu.VMEM((r, LANE), F32), pltpu.VMEM((NCHIP, hr, LANE), F32),
                        pltpu.SemaphoreType.DMA((NCHIP + 1,)), pltpu.SemaphoreType.DMA((NCHIP + 1,))],
        compiler_params=pltpu.CompilerParams(has_side_effects=True, vmem_limit_bytes=VMEM_LIMIT),
    )(buf, *after)


MAX_ROW_TILE = 512
BF16_ROWS = 16


def _row_tile(rows):
    for t in range(min(rows, MAX_ROW_TILE) // BF16_ROWS * BF16_ROWS, 0, -BF16_ROWS):
        if rows % t == 0:
            return t
    raise ValueError(rows)


def _sum_halves(g, theirs, c_arr):
    nch, rows, cols = g.shape
    hr = rows // 2
    tr = _row_tile(hr)

    def body(c_ref, g_ref, t_ref, o_ref, ob_ref):
        s = g_ref[...] + t_ref[...]
        o_ref[...] = s
        ob_ref[...] = s.astype(BF16)

    blk = pl.BlockSpec((None, tr, cols), lambda j, i, c_ref: (j, i, 0))
    return pl.pallas_call(
        body, name="sum_halves",
        grid_spec=pltpu.PrefetchScalarGridSpec(
            num_scalar_prefetch=1, grid=(nch, hr // tr),
            in_specs=[pl.BlockSpec((None, None, tr, cols), lambda j, i, c_ref: (j, c_ref[0], i, 0)), blk],
            out_specs=[blk, blk]),
        out_shape=[jax.ShapeDtypeStruct((nch, hr, cols), F32), jax.ShapeDtypeStruct((nch, hr, cols), BF16)],
        compiler_params=_cparams(("parallel", "parallel")),
    )(c_arr, g.reshape(nch, 2, hr, cols), theirs)


def _sum_halves_w_in(g, theirs, c_arr):
    hr = D // 2
    sh = IN_DIM // NCHIP

    def body(c_ref, g_ref, t_ref, o_ref, ob_ref):
        s = g_ref[...] + t_ref[...]
        for j, dst, src, width in W_IN_PIECES:
            o_ref[j, :, dst:dst + width] = s[:, src:src + width]
            ob_ref[j, :, dst:dst + width] = s[:, src:src + width].astype(BF16)

    out = pl.BlockSpec((NCHIP, WT, sh), lambda i, c_ref: (0, i, 0))
    return pl.pallas_call(
        body, name="sum_halves_w_in",
        grid_spec=pltpu.PrefetchScalarGridSpec(
            num_scalar_prefetch=1, grid=(hr // WT,),
            in_specs=[pl.BlockSpec((None, WT, NP), lambda i, c_ref: (c_ref[0], i, 0)),
                      pl.BlockSpec((None, WT, NP), lambda i, c_ref: (0, i, 0))],
            out_specs=[out, out]),
        out_shape=[jax.ShapeDtypeStruct((NCHIP, hr, sh), F32), jax.ShapeDtypeStruct((NCHIP, hr, sh), BF16)],
        compiler_params=_cparams(("parallel",)),
    )(c_arr, g.reshape(2, hr, NP), theirs)


def _sum_chips(p, q, place, l, into=None, after=()):
    extra = ([into] if into is not None else []) + list(after)
    _, rows, cols = p.shape
    tr = _row_tile(rows)
    steps = rows // tr

    def body(place_ref, p_ref, q0, q1, q2, *rest):
        rest[-1][...] = ((p_ref[...] + q0[...].astype(F32)) + q1[...].astype(F32)) + q2[...].astype(F32)

    qs = lambda k: pl.BlockSpec((None, tr, cols), lambda i, place_ref: (k, i, 0))
    return pl.pallas_call(
        body, name="sum_chips",
        grid_spec=pltpu.PrefetchScalarGridSpec(
            num_scalar_prefetch=1, grid=(steps,),
            in_specs=[pl.BlockSpec((None, tr, cols), lambda i, place_ref: (place_ref[0], i, 0)), qs(0), qs(1), qs(2)]
            + [ANY] * len(extra),
            out_specs=pl.BlockSpec((None, tr, cols), lambda i, place_ref: (l, place_ref[1] * steps + i, 0))),
        out_shape=jax.ShapeDtypeStruct((DEPTH, 2 * rows, cols), F32),
        input_output_aliases={5: 0} if into is not None else {},
        compiler_params=_cparams(("parallel",)),
    )(place, p, q, q, q, *extra)


def _adamw_fn(w, g, m, v):
    nm = ADAM_B1 * m + (1.0 - ADAM_B1) * g
    nv = ADAM_B2 * v + (1.0 - ADAM_B2) * jnp.square(g)
    m_hat = nm / (1.0 - ADAM_B1 ** ADAM_STEP)
    v_hat = nv / (1.0 - ADAM_B2 ** ADAM_STEP)
    return -ADAM_LR * (m_hat / (jnp.sqrt(v_hat) + ADAM_EPS) + ADAM_WD * w), nm, nv


def _adamw(w, g, m, v):
    layers, rows, cols = w.shape
    tr = _row_tile(rows)

    def body(w_ref, g_ref, m_ref, v_ref, d_ref, nm_ref, nv_ref):
        d_ref[...], nm_ref[...], nv_ref[...] = _adamw_fn(w_ref[...], g_ref[...], m_ref[...], v_ref[...])

    blk = pl.BlockSpec((None, tr, cols), lambda l, i: (l, i, 0))
    return pl.pallas_call(
        body, name="adamw", grid=(layers, rows // tr), in_specs=[blk] * 4, out_specs=[blk] * 3,
        out_shape=[jax.ShapeDtypeStruct(w.shape, F32)] * 3, compiler_params=_cparams(("parallel", "parallel")),
    )(w, g, m, v)


def _adamw_small(ws, gs, ms, vs):
    n = len(ws)

    def body(*refs):
        for i in range(n):
            w_ref, g_ref, m_ref, v_ref, d_ref, nm_ref, nv_ref = (refs[k * n + i] for k in range(7))
            d_ref[...], nm_ref[...], nv_ref[...] = _adamw_fn(w_ref[...], g_ref[...], m_ref[...], v_ref[...])

    vm = pl.BlockSpec(memory_space=pltpu.VMEM)
    out = pl.pallas_call(
        body, name="adamw_small", in_specs=[vm] * (4 * n), out_specs=[vm] * (3 * n),
        out_shape=[jax.ShapeDtypeStruct(a.shape, F32) for a in list(ws) * 3],
        compiler_params=pltpu.CompilerParams(vmem_limit_bytes=VMEM_LIMIT),
    )(*ws, *gs, *ms, *vs)
    return out[:n], out[n:2 * n], out[2 * n:]


BIG = ("w_in", "w_out", "w_gate", "w_up", "w_down")
SMALL = ("norm_mix", "a_log", "dt_bias", "o_norm_g", "ln_v_g", "ln_v_b", "w_s", "b_s", "norm_ffn", "norm_final")
ORDER = ("norm_mix", "w_in", "conv_w", "a_log", "dt_bias", "o_norm_g", "ln_v_g", "ln_v_b", "w_s", "b_s", "w_out",
         "norm_ffn", "w_gate", "w_up", "w_down", "norm_final")


F32_ROWS = 8
PACK_ROWS = 128


def _lane_rows(size):
    return -(-size // (F32_ROWS * LANE)) * F32_ROWS


def _pack(arrs):
    parts = [jnp.pad(a.reshape(-1), (0, _lane_rows(a.size) * LANE - a.size)).reshape(-1, LANE) for a in arrs]
    rows = sum(p.shape[0] for p in parts)
    if rows % PACK_ROWS:
        parts.append(jnp.zeros((-rows % PACK_ROWS, LANE), F32))
    return jnp.concatenate(parts, axis=0)


def _unpack(buf, like):
    out, row = [], 0
    for a in like:
        n = _lane_rows(a.size)
        out.append(buf[row:row + n].reshape(-1)[:a.size].reshape(a.shape))
        row += n
    return out


def kernel(x, norm_mix, w_in, conv_w, a_log, dt_bias, o_norm_g, ln_v_g, ln_v_b, w_s, b_s, w_out, norm_ffn, w_gate, w_up, w_down, norm_final, loss_target, m_norm_mix, m_w_in, m_conv_w, m_a_log, m_dt_bias, m_o_norm_g, m_ln_v_g, m_ln_v_b, m_w_s, m_b_s, m_w_out, m_norm_ffn, m_w_gate, m_w_up, m_w_down, m_norm_final, v_norm_mix, v_w_in, v_conv_w, v_a_log, v_dt_bias, v_o_norm_g, v_ln_v_g, v_ln_v_b, v_w_s, v_b_s, v_w_out, v_norm_ffn, v_w_gate, v_w_up, v_w_down, v_norm_final):
    w = dict(norm_mix=norm_mix, w_in=w_in, conv_w=conv_w, a_log=a_log, dt_bias=dt_bias, o_norm_g=o_norm_g,
             ln_v_g=ln_v_g, ln_v_b=ln_v_b, w_s=w_s, b_s=b_s, w_out=w_out, norm_ffn=norm_ffn, w_gate=w_gate, w_up=w_up,
             w_down=w_down, norm_final=norm_final)
    m = dict(norm_mix=m_norm_mix, w_in=m_w_in, conv_w=m_conv_w, a_log=m_a_log, dt_bias=m_dt_bias, o_norm_g=m_o_norm_g,
             ln_v_g=m_ln_v_g, ln_v_b=m_ln_v_b, w_s=m_w_s, b_s=m_b_s, w_out=m_w_out, norm_ffn=m_norm_ffn,
             w_gate=m_w_gate, w_up=m_w_up, w_down=m_w_down, norm_final=m_norm_final)
    v = dict(norm_mix=v_norm_mix, w_in=v_w_in, conv_w=v_conv_w, a_log=v_a_log, dt_bias=v_dt_bias, o_norm_g=v_o_norm_g,
             ln_v_g=v_ln_v_g, ln_v_b=v_ln_v_b, w_s=v_w_s, b_s=v_b_s, w_out=v_w_out, norm_ffn=v_norm_ffn,
             w_gate=v_w_gate, w_up=v_w_up, w_down=v_w_down, norm_final=v_norm_final)
    chip = 2 * lax.axis_index("x") + lax.axis_index("y")
    place = jnp.stack([chip, lax.axis_index("c")]).astype(jnp.int32)
    c_arr = place[1:]

    def kernel_view(n, a):
        return jnp.swapaxes(a, 1, 2) if n in ("w_gate", "w_up") else a

    own = {n: [kernel_view(n, w[n])[l].astype(BF16) for l in range(DEPTH)] for n in BIG}
    by_chip = lambda a: jax.ShapeDtypeStruct((NCHIP,) + a.shape, a.dtype)

    def start(name, srcs, whole, after=()):
        return _split_start(name, _gather_plan(whole), srcs, [by_chip(a) for a in srcs], 3 * len(srcs), after)

    def finish(name, started, whole, after):
        srcs, lands = _split_wait(name, _gather_plan(whole), started, after)
        passed = iter(_forward_halves([g for g, all_of_it in zip(lands, whole) if not all_of_it]))
        lands = [g if all_of_it else next(passed) for g, all_of_it in zip(lands, whole)]
        return srcs, [lax.dynamic_update_index_in_dim(g, o, chip, 0) for g, o in zip(lands, srcs)]

    ffn = BIG[1:]
    first = start("gather_first_start", [own["w_in"][0], conv_w], [False, True])
    early = start("gather_early_start", [own[n][0] for n in ffn], [False] * len(ffn), [first["token"]])
    mid = start("gather_mid_start", [own["w_in"][1]], [False], [early["token"]])
    later = start("gather_later_start", [own[n][1] for n in ffn], [False] * len(ffn), [mid["token"]])
    hn = _rmsnorm("rms_mix", x[0], norm_mix[0][None])
    (own_w_in, _), (w_in_by_chip, conv_by_chip) = finish("gather_first_wait", first, [False, True], [later["token"], hn])

    passing = {}

    def pass_on(tag, started, n):
        def at(after):
            srcs, lands = _split_wait(f"gather_{tag}_wait", _gather_plan([False] * n), started, after)
            passing[tag] = srcs, _forward_start(f"forward_{tag}_start", lands, ())
            return [passing[tag][1]["token"]]
        return at

    def passed_on(tag, after):
        srcs, fwd = passing[tag]
        lands = _forward_wait(f"forward_{tag}_wait", fwd, [after])
        return srcs, [lax.dynamic_update_index_in_dim(g, o, chip, 0) for g, o in zip(lands, srcs)]

    def late(tag):
        return lambda after: dict(zip(ffn, passed_on(tag, after)[1]))

    layer0 = _layer_params(0, dict(
        hn=hn, w_in=_assemble_w_in(w_in_by_chip, own_w_in, place), conv_w=conv_by_chip, late=late("early"),
        before_mix=pass_on("early", early, len(ffn)), before_ffn_out=pass_on("mid", mid, 1)), w)

    def layer1(after):
        (own_w_in1,), (w_in1_by_chip,) = passed_on("mid", after)
        return _layer_params(1, dict(w_in=_assemble_w_in(w_in1_by_chip, own_w_in1, place), conv_w=conv_by_chip,
                                     late=late("later"), before_mix=pass_on("later", later, len(ffn))), w)

    saved, layers, loss_lanes, dh, dhb, d_norm_final = _forward(x[0], loss_target[0], [layer0, layer1],
                                                                 norm_final[None])

    sums, arrived = {}, {}

    def exchange_start(tag, l, names, grads, after=()):
        mine = [grads[n] for n in names]
        shapes = [jax.ShapeDtypeStruct((g.shape[0], g.shape[1] // 2, g.shape[2]), F32) for g in mine]
        return tag, l, names, _split_start(f"exchange_{tag}_start", _exchange_plan, mine, shapes, len(mine), after)

    def add_halves(l, names, mine, theirs):
        for n, g, t in zip(names, mine, theirs):
            sums[l, n] = (_sum_halves_w_in if n == "w_in" else _sum_halves)(g, t, c_arr)

    def exchange_wait(handle, after):
        tag, l, names, started = handle
        add_halves(l, names, *_split_wait(f"exchange_{tag}_wait", _exchange_plan, started, after))

    def scatter_start(tag, l, names, after=()):
        partial = [sums[l, n][1] for n in names]
        shapes = [jax.ShapeDtypeStruct((3,) + p.shape[1:], p.dtype) for p in partial]
        return tag, l, names, _split_start(f"scatter_{tag}_start", _scatter_plan, partial, shapes, 3 * len(names), after)

    def scatter_wait(handle, after):
        tag, l, names, started = handle
        for n, q in zip(names, _split_wait(f"scatter_{tag}_wait", _scatter_plan, started, after)[1]):
            arrived[l, n] = q

    last = DEPTH - 1
    swiglu = BIG[2:]
    dh1, dh1b, g_ffn = _layer_bwd_ffn(dh, dhb, layers[last], saved[last])
    dh, dhb, g_mix = _layer_bwd_mixer(dh1, dh1b, layers[last], saved[last])
    gl = [None, _reference_layout({**g_ffn, **g_mix})]
    ex_last = exchange_start("last", last, BIG, gl[last])
    dh1, dh1b, g_ffn = _layer_bwd_ffn(dh, dhb, layers[0], saved[0], after=[ex_last[-1]["token"]])
    exchange_wait(ex_last, dh1)
    sc_last = scatter_start("last", last, BIG)
    ex_ffn = exchange_start("swiglu", 0, swiglu, g_ffn, [sc_last[-1]["token"]])
    sc_ffn = []

    def midway(do):
        exchange_wait(ex_ffn, do)
        sc_ffn.append(scatter_start("swiglu", 0, swiglu))
        return [sc_ffn[0][-1]["token"]]

    ex_rest = []

    def late(grads):
        rest_grads = dict(w_in=grads["w_in"], w_out=grads["w_out"].reshape(NCHIP, D // NCHIP, D))
        ex_rest.append(exchange_start("rest", 0, BIG[:2], rest_grads))
        return [ex_rest[0][-1]["token"]]

    dx, _, g_mix = _layer_bwd_mixer(dh1, dh1b, layers[0], saved[0], after=[ex_ffn[-1]["token"]], midway=midway,
                                    late=late)
    scatter_wait(sc_last, dx)
    scatter_wait(sc_ffn[0], dx)
    gl[0] = _reference_layout({**g_ffn, **g_mix})

    small_g = [jnp.stack([gl[l][n] for l in range(DEPTH)]) for n in SMALL[:-1]] + [d_norm_final[0]]
    conv_g = jnp.stack([gl[l]["conv_w"] for l in range(DEPTH)])
    summed = small_g + [conv_g, loss_lanes[0, :1]]
    total = _allreduce_small(_pack(summed))
    exchange_wait(ex_rest[0], total)
    sc_rest = scatter_start("rest", 0, BIG[:2])

    travelling = [sc_rest[-1]["token"]]
    reduced, g_out, delta, new_m, new_v = {}, {}, {}, {}, {}

    done = []

    def adamw_large(names, joined):
        for n, g in zip(names, joined):
            res = _adamw(kernel_view(n, w[n]), g, kernel_view(n, m[n]), kernel_view(n, v[n]))
            done.append(res[2])
            g_out[n], delta[n], new_m[n], new_v[n] = (kernel_view(n, a) for a in (g,) + tuple(res))

    for n in BIG:
        for l in (range(DEPTH) if n in swiglu else [last]):
            reduced[n] = _sum_chips(sums[l, n][0], arrived[l, n], place, l, into=reduced.get(n), after=travelling)
    adamw_large(swiglu, _join_halves("join_swiglu", [reduced[n] for n in swiglu]))
    scatter_wait(sc_rest, done + [reduced[n] for n in BIG[:2]])
    for n in BIG[:2]:
        reduced[n] = _sum_chips(sums[0, n][0], arrived[0, n], place, 0, into=reduced[n])
    adamw_large(BIG[:2], _join_halves("join_rest", [reduced[n] for n in BIG[:2]]))
    *small_r, conv_r, loss = _unpack(total, summed)
    g_out.update(zip(SMALL, small_r))
    g_out["conv_w"] = lax.dynamic_slice_in_dim(conv_r, chip * conv_w.shape[2], conv_w.shape[2], axis=2)

    rest = SMALL + ("conv_w",)
    rows_of = lambda a: a.reshape(1, -1) if a.ndim == 1 else a
    results = _adamw_small(*[[rows_of(src[n]) for n in rest] for src in (w, g_out, m, v)])
    for dst, arrs in zip((delta, new_m, new_v), results):
        dst.update({n: a.reshape(w[n].shape) for n, a in zip(rest, arrs)})

    return (loss[0], dx[None], *[g_out[n] for n in ORDER], *[delta[n] for n in ORDER], *[new_m[n] for n in ORDER],
            *[new_v[n] for n in ORDER])
```

```python
import functools

import jax
import jax.numpy as jnp
from jax import lax
from jax.experimental import pallas as pl
from jax.experimental.pallas import tpu as pltpu

F32 = jnp.float32
BF16 = jnp.bfloat16
MESH = pl.DeviceIdType.MESH
ANY = pl.BlockSpec(memory_space=pl.ANY)
HIGHEST = lax.Precision.HIGHEST

T = 2048
D = 1024
DEPTH = 2
NCHIP = 4
HEADS = 4
HD = 128
HW = HEADS * HD
CH = 64
GCH = 128
IN_DIM = 3080
NP = 3200
BA_OFF = 3072
FF_SH = 704
EPS = 1e-6
LANE = 128
VMEM_LIMIT = 56 * 1024 * 1024

ADAM_LR = 0.001
ADAM_B1 = 0.9
ADAM_B2 = 0.999
ADAM_EPS = 1e-08
ADAM_WD = 0.01
ADAM_STEP = 10


def _cparams(sem=None):
    return pltpu.CompilerParams(dimension_semantics=sem, vmem_limit_bytes=VMEM_LIMIT)


_DIMS = {"nn": (((1,), (0,)), ((), ())), "nt": (((1,), (1,)), ((), ())), "tn": (((0,), (0,)), ((), ()))}


def _mm(name, mode, a, bs, *, tm, tn, tk, out_dtypes=(F32,), reduce_g=False, resid=None, extras=(), epilogue=None,
        after=(), fold_g=False, sub_m=1):
    assert sub_m == 1 or (mode != "tn" and tm % (8 * sub_m) == 0), (name, sub_m)
    nb = len(bs)
    ga = a.shape[0]
    gbs = [b.shape[0] for b in bs]
    g_n = max([ga] + gbs)
    if mode == "tn":
        k_n, m_n = a.shape[1:]
    else:
        m_n, k_n = a.shape[1:]
    n_n = bs[0].shape[1] if mode == "nt" else bs[0].shape[2]
    assert m_n % tm == 0 and n_n % tn == 0 and k_n % tk == 0, (name, m_n, n_n, k_n)
    mi, nj, kk = m_n // tm, n_n // tn, k_n // tk
    lead = g_n if fold_g else None
    g_steps = 1 if fold_g else g_n
    grid = (mi, nj, g_steps, kk)
    ids = lambda i, j, g, k: (g, i, j, k)
    n_red = (g_steps if reduce_g else 1) * kk
    red_idx = lambda: (pl.program_id(2) * kk if reduce_g else 0) + pl.program_id(3)
    sem = ("parallel", "parallel", "arbitrary" if reduce_g else "parallel", "arbitrary")

    def pick(gsz, g):
        return g if gsz > 1 else 0

    def a_map(*p):
        g, i, j, k = ids(*p)
        return (pick(ga, g), k, i) if mode == "tn" else (pick(ga, g), i, k)

    def b_map(gsz):
        def f(*p):
            g, i, j, k = ids(*p)
            return (pick(gsz, g), j, k) if mode == "nt" else (pick(gsz, g), k, j)
        return f

    def o_map(gsz):
        def f(*p):
            g, i, j, k = ids(*p)
            return (0 if reduce_g else pick(gsz, g), i, j)
        return f

    a_spec = pl.BlockSpec((lead, tk, tm) if mode == "tn" else (lead, tm, tk), a_map)
    b_specs = [pl.BlockSpec((lead, tn, tk) if mode == "nt" else (lead, tk, tn), b_map(gs)) for gs in gbs]
    x_specs = [pl.BlockSpec((None, tm, tn), o_map(e.shape[0])) for e in extras]
    r_specs = [pl.BlockSpec((None, tm, tn), o_map(resid.shape[0]))] if resid is not None else []
    g_out = 1 if reduce_g else g_n
    out_shape = [jax.ShapeDtypeStruct((g_out, m_n, n_n), dt) for dt in out_dtypes]
    out_specs = [pl.BlockSpec((None, tm, tn), o_map(g_out)) for _ in out_dtypes]
    nx, nr, no = len(extras), len(r_specs), len(out_dtypes)
    n_in = 1 + nb + nx + nr + len(after)
    dims = _DIMS[mode]

    def body(*refs):
        a_ref = refs[0]
        b_refs = refs[1:1 + nb]
        x_refs = refs[1 + nb:1 + nb + nx]
        r_refs = refs[1 + nb + nx:1 + nb + nx + nr]
        o_refs = refs[n_in:n_in + no]
        acc_refs = refs[n_in + no:]
        def dots(rows):
            if fold_g:
                return [sum(lax.dot_general(a_ref[g, rows, :], b_ref[g], dims, preferred_element_type=F32)
                            for g in range(g_n)) for b_ref in b_refs]
            av = a_ref[...] if mode == "tn" else a_ref[rows, :]
            return [lax.dot_general(av, b_ref[...], dims, preferred_element_type=F32) for b_ref in b_refs]

        def finish(accs, rows=slice(None)):
            if r_refs:
                accs[0] = accs[0] + r_refs[0][rows, :]
            outs = epilogue(accs, [x[rows, :] for x in x_refs]) if epilogue is not None else accs
            for o_ref, o in zip(o_refs, outs):
                o_ref[rows, :] = o.astype(o_ref.dtype)

        if n_red == 1:
            slabs = [slice(s * (tm // sub_m), (s + 1) * (tm // sub_m)) for s in range(sub_m)]
            ahead = dots(slabs[0])
            for s, rows in enumerate(slabs):
                now, ahead = ahead, (dots(slabs[s + 1]) if s + 1 < sub_m else None)
                finish(now, rows)
            return
        products = dots(slice(None))
        r = red_idx()
        for p, acc in zip(products, acc_refs):
            @pl.when(r == 0)
            def _():
                acc[...] = p

            @pl.when((r > 0) & (r < n_red - 1))
            def _():
                acc[...] += p

        @pl.when(r == n_red - 1)
        def _():
            finish([acc[...] + p for p, acc in zip(products, acc_refs)])

    return pl.pallas_call(
        body, name=name, grid=grid,
        in_specs=[a_spec] + b_specs + x_specs + r_specs + [ANY] * len(after),
        out_specs=out_specs, out_shape=out_shape,
        scratch_shapes=[pltpu.VMEM((tm, tn), F32) for _ in range(nb if n_red > 1 else 0)],
        compiler_params=_cparams(sem),
    )(a, *bs, *extras, *([resid] if resid is not None else []), *after)


def _sigmoid(x):
    return 1.0 / (1.0 + jnp.exp(-x))


def _silu(x):
    return x * _sigmoid(x)


def _gelu(x):
    return 0.5 * x * (1.0 + jnp.tanh(0.7978845608028654 * (x + 0.044715 * (x * x * x))))


def _rms_fn(h, gain):
    return h * lax.rsqrt(jnp.mean(h * h, axis=-1, keepdims=True) + EPS) * gain


def _shift_impl(x, s):
    n = x.shape[0]
    rolled = pltpu.roll(x, s % n, 0)
    row = lax.broadcasted_iota(jnp.int32, x.shape, 0)
    return jnp.where((row >= s) & (row < n + s), rolled, 0.0)


@functools.partial(jax.custom_vjp, nondiff_argnums=(1,))
def _shift(x, s):
    return _shift_impl(x, s)


def _shift_fwd(x, s):
    return _shift_impl(x, s), None


def _shift_bwd(s, _, g):
    return (_shift_impl(g, -s),)


_shift.defvjp(_shift_fwd, _shift_bwd)


def _prep_fn(x, w, qk_scale, is_v):
    y = x * w[3:4, :]
    for i in range(3):
        y = y + _shift(x, 3 - i) * w[i:i + 1, :]
    y = _silu(y)
    nrm = lax.rsqrt(jnp.sum(y * y, axis=-1, keepdims=True) + EPS) * qk_scale
    return y * jnp.where(is_v, 1.0, nrm)


def _softplus(x):
    return jnp.maximum(x, 0.0) + jnp.log(1.0 + jnp.exp(-jnp.abs(x)))


def _gates_fn(ba, a_log, dt_bias):
    lane = lax.broadcasted_iota(jnp.int32, ba.shape, 1)
    beta = _sigmoid(ba)
    g = -jnp.exp(a_log) * _softplus(ba + dt_bias)
    return jnp.where(lane < HEADS, beta, g)


def _dot16(a, b, dims=_DIMS["nn"]):
    return lax.dot_general(a.astype(BF16), b.astype(BF16), dims, preferred_element_type=F32)


def _dot32(a, b):
    return jnp.dot(a, b, preferred_element_type=F32, precision=HIGHEST)


def _dot3(a, b, dims=_DIMS["nn"]):
    return lax.dot_general(a, b, dims, preferred_element_type=F32, precision=lax.Precision.HIGH)


def _tri_inverses(mats, tick=lambda: None):
    row = lax.broadcasted_iota(jnp.int32, (CH, CH), 0)
    col = lax.broadcasted_iota(jnp.int32, (CH, CH), 1)
    eye = (row == col).astype(F32)
    ts = [eye - a for a in mats]
    ps = list(mats)
    for _ in range(5):
        ps = [_dot3(p, p) for p in ps]
        tick()
        ts = [t + _dot3(t, p) for t, p in zip(ts, ps)]
        tick()
    return ts


@jax.custom_vjp
def _tri_solves(mats, rhs):
    return [_dot3(t, b) for t, b in zip(_tri_inverses(mats), rhs)]


def _tri_solves_fwd(mats, rhs):
    ts = _tri_inverses(mats)
    xs = [_dot3(t, b) for t, b in zip(ts, rhs)]
    return xs, (ts, xs)


def _tri_solves_bwd(res, dxs):
    ts, xs = res
    dbs = [_dot3(t, dx, _DIMS["tn"]) for t, dx in zip(ts, dxs)]
    return [-_dot3(db, x, _DIMS["nt"]) for db, x in zip(dbs, xs)], dbs


_tri_solves.defvjp(_tri_solves_fwd, _tri_solves_bwd)


def _chunk_prep_fn(xs, bgs, tick=None):
    step = tick or (lambda: None)
    row = lax.broadcasted_iota(jnp.int32, (CH, CH), 0)
    col = lax.broadcasted_iota(jnp.int32, (CH, CH), 1)
    incl = row >= col
    strict = row > col
    lmat = incl.astype(F32)
    n = len(xs)
    items = [(i, h) for i in range(n) for h in range(HEADS)]
    part = lambda i, h, c: xs[i][:, c * HW + h * HD:c * HW + (h + 1) * HD]
    q = [part(i, h, 0) for i, h in items]
    k = [part(i, h, 1) for i, h in items]
    v = [part(i, h, 2) for i, h in items]
    beta = [bgs[i][:, h:h + 1] for i, h in items]
    gc_all = [_dot32(lmat, bg) for bg in bgs]
    step()
    gc = [gc_all[i][:, HEADS + h:HEADS + h + 1] for i, h in items]
    gmat = [jnp.where(strict, jnp.broadcast_to(bgs[i][:, HEADS + h:HEADS + h + 1], (CH, CH)), 0.0) for i, h in items]
    diff = [_dot3(lmat, m) for m in gmat]
    step()
    decay = [jnp.where(incl, jnp.exp(jnp.where(incl, d, 0.0)), 0.0) for d in diff]
    k_beta = [kk * b for kk, b in zip(k, beta)]
    kk_t = [_dot16(kb, kk, _DIMS["nt"]) for kb, kk in zip(k_beta, k)]
    step()
    qk_t = [_dot16(qq, kk, _DIMS["nt"]) for qq, kk in zip(q, k)]
    step()
    a = [jnp.where(strict, m * d, 0.0) for m, d in zip(kk_t, decay)]
    eg = [jnp.exp(g) for g in gc]
    rhs = [jnp.concatenate([vv * b, kb * e], axis=-1) for vv, b, kb, e in zip(v, beta, k_beta, eg)]
    if tick is None:
        uw = _tri_solves(a, rhs)
    else:
        uw = [_dot3(t, b) for t, b in zip(_tri_inverses(a, tick), rhs)]
    qk = [m * d for m, d in zip(qk_t, decay)]
    g_last = [g[CH - 1:CH, :] for g in gc]
    qe = [qq * e for qq, e in zip(q, eg)]
    kd = [kk * jnp.exp(gl - g) for kk, gl, g in zip(k, g_last, gc)]
    egl = [jnp.broadcast_to(jnp.exp(gl), (1, HD)) for gl in g_last]
    out = []
    for i in range(n):
        mine = slice(i * HEADS, (i + 1) * HEADS)
        cat = lambda vals: jnp.concatenate(vals[mine], axis=-1)
        out.append((cat([x[:, :HD] for x in uw]), cat([x[:, HD:] for x in uw]), cat(qe), cat(kd),
                    jnp.concatenate([m[None] for m in qk[mine]], axis=0), cat(egl)))
    return out


def _state_levels(chunks, s, outs, befores, final):
    for u, w, qe, kd, qk, egl in chunks:
        befores.append(s)
        ws = [_dot16(a, b) for a, b in zip(w, s)]
        qs = [_dot16(a, b) for a, b in zip(qe, s)]
        yield
        v_new = [a - b for a, b in zip(u, ws)]
        outs.append([a + _dot16(b, c) for a, b, c in zip(qs, qk, v_new)])
        s = [a * e + _dot16(b, c, _DIMS["tn"]) for a, e, b, c in zip(s, egl, kd, v_new)]
        yield
    final.append(s)


def _chunk_state_fn(u, w, qe, kd, qk, egl, s):
    ws = [_dot16(a, b) for a, b in zip(w, s)]
    qs = [_dot16(a, b) for a, b in zip(qe, s)]
    v_new = [a - b for a, b in zip(u, ws)]
    o = [a + _dot16(b, c) for a, b, c in zip(qs, qk, v_new)]
    s_new = [a * e + _dot16(b, c, _DIMS["tn"]) for a, e, b, c in zip(s, egl, kd, v_new)]
    return o, s_new


def _mix_fn(o, z, ur, vr, ong, lng, lnb, ws, bst):
    row = lax.broadcasted_iota(jnp.int32, (GCH, GCH), 0)
    col = lax.broadcasted_iota(jnp.int32, (GCH, GCH), 1)
    causal = row >= col
    ug = _gelu(ur)
    vg = _gelu(vr)
    sls = [slice(h * HD, (h + 1) * HD) for h in range(HEADS)]
    oh = [o[:, sl] for sl in sls]
    oh = [x * lax.rsqrt(jnp.mean(x * x, axis=-1, keepdims=True) + EPS) for x in oh]
    outs_dn = [x * ong * _silu(z[:, sl]) for x, sl in zip(oh, sls)]
    vh = [vg[:, sl] for sl in sls]
    mu = [jnp.mean(x, axis=-1, keepdims=True) for x in vh]
    var = [jnp.mean(jnp.square(x - m), axis=-1, keepdims=True) for x, m in zip(vh, mu)]
    vn = [(x - m) * lax.rsqrt(s + EPS) * lng[:, sl] + lnb[:, sl] for x, m, s, sl in zip(vh, mu, var, sls)]
    mixed = [_dot16(jnp.where(causal, ws[h], 0.0), vn[h]) for h in range(HEADS)]
    outs_gm = [ug[:, sl] * (mixed[h] + bst[:, h:h + 1]) for h, sl in enumerate(sls)]
    return jnp.concatenate(outs_dn + outs_gm, axis=-1)


def _loss_fn(h, gain, tgt):
    y = _rms_fn(h, gain)
    return 0.5 * jnp.sum(jnp.mean(jnp.square(y - tgt), axis=-1))


RT = 512


def _rows(n=D):
    return pl.BlockSpec((RT, n), lambda i: (i, 0))


def _whole(shape):
    nd = len(shape)
    return pl.BlockSpec(shape, lambda i: (0,) * nd)


def _rmsnorm(name, h, gain):
    def body(h_ref, g_ref, o_ref):
        o_ref[...] = _rms_fn(h_ref[...], g_ref[...]).astype(BF16)

    return pl.pallas_call(
        body, name=name, grid=(T // RT,), in_specs=[_rows(), _whole((1, D))], out_specs=_rows(),
        out_shape=jax.ShapeDtypeStruct((T, D), BF16), compiler_params=_cparams(("parallel",)),
    )(h, gain)


def _rmsnorm_bwd(name, dhn, h, gain, resid):
    def body(dhn_ref, h_ref, g_ref, r_ref, dh_ref, dh16_ref, dg_ref):
        _, vjp = jax.vjp(_rms_fn, h_ref[...], g_ref[...])
        dh, dg = vjp(dhn_ref[...])
        dh = r_ref[...] + dh
        dh_ref[...] = dh
        dh16_ref[...] = dh.astype(BF16)

        @pl.when(pl.program_id(0) == 0)
        def _():
            dg_ref[...] = dg

        @pl.when(pl.program_id(0) > 0)
        def _():
            dg_ref[...] += dg

    return pl.pallas_call(
        body, name=name, grid=(T // RT,), in_specs=[_rows(), _rows(), _whole((1, D)), _rows()],
        out_specs=[_rows(), _rows(), _whole((1, D))],
        out_shape=[jax.ShapeDtypeStruct((T, D), F32), jax.ShapeDtypeStruct((T, D), BF16),
                   jax.ShapeDtypeStruct((1, D), F32)],
        compiler_params=_cparams(("arbitrary",)),
    )(dhn, h, gain, resid)


def _loss_head(h, gain, tgt):
    def body(h_ref, g_ref, t_ref, l_ref, dh_ref, dh16_ref, dg_ref):
        loss, vjp = jax.vjp(lambda hh, gg: _loss_fn(hh, gg, t_ref[...]), h_ref[...], g_ref[...])
        dh, dg = vjp(jnp.ones((), F32))
        dh_ref[...] = dh
        dh16_ref[...] = dh.astype(BF16)
        lv = jnp.full((1, LANE), loss, F32)

        @pl.when(pl.program_id(0) == 0)
        def _():
            dg_ref[...] = dg
            l_ref[...] = lv

        @pl.when(pl.program_id(0) > 0)
        def _():
            dg_ref[...] += dg
            l_ref[...] += lv

    return pl.pallas_call(
        body, name="loss_head", grid=(T // RT,), in_specs=[_rows(), _whole((1, D)), _rows()],
        out_specs=[_whole((1, LANE)), _rows(), _rows(), _whole((1, D))],
        out_shape=[jax.ShapeDtypeStruct((1, LANE), F32), jax.ShapeDtypeStruct((T, D), F32),
                   jax.ShapeDtypeStruct((T, D), BF16), jax.ShapeDtypeStruct((1, D), F32)],
        compiler_params=_cparams(("arbitrary",)),
    )(h, gain, tgt)


def _prep_flags():
    j = pl.program_id(0)
    qk_scale = jnp.where(j < HEADS, HD ** -0.5, 1.0).astype(F32)
    return qk_scale, j >= 2 * HEADS


def _prep(proj, conv_w):
    def body(x_ref, w_ref, o_ref):
        qk_scale, is_v = _prep_flags()
        o_ref[...] = _prep_fn(x_ref[...], w_ref[...], qk_scale, is_v)

    col = lambda j: (0, j)
    return pl.pallas_call(
        body, name="gdn_prep", grid=(3 * HEADS,),
        in_specs=[pl.BlockSpec((T, HD), col), pl.BlockSpec((4, HD), col)], out_specs=pl.BlockSpec((T, HD), col),
        out_shape=jax.ShapeDtypeStruct((T, 3 * HW), F32), compiler_params=_cparams(("parallel",)),
    )(proj, conv_w)


def _prep_bwd(proj, conv_w, dqkv, dproj):
    def body(x_ref, w_ref, d_ref, _, dx_ref, dw_ref):
        qk_scale, is_v = _prep_flags()
        _, vjp = jax.vjp(lambda x, w: _prep_fn(x, w, qk_scale, is_v), x_ref[...], w_ref[...])
        dx, dw = vjp(d_ref[...])
        dx_ref[...] = dx.astype(BF16)
        dw_ref[...] = dw

    col = lambda j: (0, j)
    return pl.pallas_call(
        body, name="gdn_prep_bwd", grid=(3 * HEADS,),
        in_specs=[pl.BlockSpec((T, HD), col), pl.BlockSpec((4, HD), col), pl.BlockSpec((T, HD), col), ANY],
        out_specs=[pl.BlockSpec((T, HD), col), pl.BlockSpec((4, HD), col)],
        out_shape=[jax.ShapeDtypeStruct((T, NP), BF16), jax.ShapeDtypeStruct((4, 3 * HW), F32)],
        input_output_aliases={3: 0}, compiler_params=_cparams(("parallel",)),
    )(proj, conv_w, dqkv, dproj)


BA_BLK = BA_OFF // LANE


def _gates(proj, a_log, dt_bias):
    def body(x_ref, a_ref, d_ref, o_ref):
        o_ref[...] = _gates_fn(x_ref[...], a_ref[...], d_ref[...])

    return pl.pallas_call(
        body, name="gdn_gates", grid=(1,),
        in_specs=[pl.BlockSpec((T, LANE), lambda i: (0, BA_BLK)), _whole((1, LANE)), _whole((1, LANE))],
        out_specs=_whole((T, LANE)),
        out_shape=jax.ShapeDtypeStruct((T, LANE), F32), compiler_params=_cparams(("arbitrary",)),
    )(proj, a_log, dt_bias)


def _gates_bwd(proj, a_log, dt_bias, dbg, dproj):
    def body(x_ref, a_ref, d_ref, dbg_ref, _, dx_ref, da_ref, dd_ref):
        _, vjp = jax.vjp(_gates_fn, x_ref[...], a_ref[...], d_ref[...])
        dx, da_ref[...], dd_ref[...] = vjp(dbg_ref[...])
        dx_ref[...] = dx.astype(BF16)

    ba = pl.BlockSpec((T, LANE), lambda i: (0, BA_BLK))
    return pl.pallas_call(
        body, name="gdn_gates_bwd", grid=(1,),
        in_specs=[ba, _whole((1, LANE)), _whole((1, LANE)), _whole((T, LANE)), ANY],
        out_specs=[ba, _whole((1, LANE)), _whole((1, LANE))],
        out_shape=[jax.ShapeDtypeStruct((T, NP), BF16), jax.ShapeDtypeStruct((1, LANE), F32),
                   jax.ShapeDtypeStruct((1, LANE), F32)],
        input_output_aliases={4: 0}, compiler_params=_cparams(("arbitrary",)),
    )(proj, a_log, dt_bias, dbg, dproj)


NCK = T // CH
CPS = 4


def _chunk_group_specs(at):
    wide = pl.BlockSpec((CPS * CH, HW), lambda n: (at(n), 0))
    return [wide, wide, wide, wide, pl.BlockSpec((HEADS, CPS * CH, CH), lambda n: (0, at(n), 0)),
            pl.BlockSpec((CPS, 1, HW), lambda n: (at(n), 0, 0))]


def _chunk_prep_shapes(dtypes):
    shp = [(T, HW), (T, HW), (T, HW), (T, HW), (HEADS, T, CH), (NCK, 1, HW)]
    return [jax.ShapeDtypeStruct(s, dt) for s, dt in zip(shp, dtypes)]


NGROUP = NCK // CPS
PREP_DTYPES = (F32, BF16, BF16, BF16, BF16, F32)


def _delta_rule(qkv, bg):
    def body(x_ref, bg_ref, *refs):
        prep_out, (o_ref, sh_ref), held, s_ref = refs[:6], refs[6:8], refs[8:14], refs[14]
        i = pl.program_id(0)

        @pl.when(i == 0)
        def _():
            for r in held + (s_ref,):
                r[...] = jnp.zeros_like(r)

        rows = [slice(ci * CH, (ci + 1) * CH) for ci in range(CPS)]
        u_h, w_h, qe_h, kd_h, qk_h, egl_h = held
        chunks = [_head_args((u_h.at[r, :], w_h.at[r, :], qe_h.at[r, :], kd_h.at[r, :], qk_h.at[:, r, :], egl_h.at[ci]))
                  for ci, r in enumerate(rows)]
        start = [jnp.where(i <= 1, 0.0, s_ref[h]) for h in range(HEADS)]
        outs, befores, final = [], [], []
        levels = _state_levels(chunks, start, outs, befores, final)
        res = _chunk_prep_fn([x_ref[r, :] for r in rows], [bg_ref[r, :] for r in rows], tick=lambda: next(levels, None))
        for _ in levels:
            pass
        for ci, (u, w, qe, kd, qk, egl) in enumerate(res):
            for refs_pair, val in zip(zip(prep_out[:4], held[:4]), (u, w, qe, kd)):
                for ref in refs_pair:
                    ref[rows[ci], :] = val.astype(ref.dtype)
            for ref in (prep_out[4], qk_h):
                ref[:, rows[ci], :] = qk.astype(ref.dtype)
            for ref in (prep_out[5], egl_h):
                ref[ci] = egl
        for ci, r in enumerate(rows):
            for h in range(HEADS):
                o_ref[r, h * HD:(h + 1) * HD] = outs[ci][h]
                sh_ref[h, ci] = befores[ci][h]
        for h in range(HEADS):
            s_ref[h] = final[0][h]

    now = lambda n: jnp.minimum(n, NGROUP - 1)
    was = lambda n: jnp.maximum(n - 1, 0)
    wide = lambda at: pl.BlockSpec((CPS * CH, HW), lambda n: (at(n), 0))
    held = [pltpu.VMEM(s, dt) for s, dt in zip(
        [(CPS * CH, HW)] * 4 + [(HEADS, CPS * CH, CH), (CPS, 1, HW)], PREP_DTYPES)]
    out = pl.pallas_call(
        body, name="gdn_delta_rule", grid=(NGROUP + 1,),
        in_specs=[pl.BlockSpec((CPS * CH, 3 * HW), lambda n: (now(n), 0)),
                  pl.BlockSpec((CPS * CH, LANE), lambda n: (now(n), 0))],
        out_specs=[wide(now)] * 4 + [pl.BlockSpec((HEADS, CPS * CH, CH), lambda n: (0, now(n), 0)),
                                     pl.BlockSpec((CPS, 1, HW), lambda n: (now(n), 0, 0)), wide(was),
                                     pl.BlockSpec((HEADS, CPS, HD, HD), lambda n: (0, was(n), 0, 0))],
        out_shape=_chunk_prep_shapes(PREP_DTYPES) + [jax.ShapeDtypeStruct((T, HW), F32),
                                                     jax.ShapeDtypeStruct((HEADS, NCK, HD, HD), F32)],
        scratch_shapes=held + [pltpu.VMEM((HEADS, HD, HD), F32)], compiler_params=_cparams(("arbitrary",)),
    )(qkv, bg)
    return out[:6], out[6], out[7]


def _chunk_prep_bwd(qkv, bg, cots):
    def body(x_ref, bg_ref, du, dw, dqe, dkd, dqk, degl, dx_ref, dbg_ref):
        rows = [slice(ci * CH, (ci + 1) * CH) for ci in range(CPS)]
        _, vjp = jax.vjp(_chunk_prep_fn, [x_ref[r, :] for r in rows], [bg_ref[r, :] for r in rows])
        dxs, dbgs = vjp([(du[r, :], dw[r, :], dqe[r, :], dkd[r, :], dqk[:, r, :], degl[ci])
                         for ci, r in enumerate(rows)])
        for r, dx, dbg in zip(rows, dxs, dbgs):
            dx_ref[r, :] = dx
            dbg_ref[r, :] = dbg

    wide = pl.BlockSpec((CPS * CH, HW), lambda n: (n, 0))
    return pl.pallas_call(
        body, name="gdn_chunk_prep_bwd", grid=(NCK // CPS,),
        in_specs=[pl.BlockSpec((CPS * CH, 3 * HW), lambda n: (n, 0)), pl.BlockSpec((CPS * CH, LANE), lambda n: (n, 0)),
                  wide, wide, wide, wide, pl.BlockSpec((HEADS, CPS * CH, CH), lambda n: (0, n, 0)),
                  pl.BlockSpec((CPS, 1, HW), lambda n: (n, 0, 0))],
        out_specs=[pl.BlockSpec((CPS * CH, 3 * HW), lambda n: (n, 0)), pl.BlockSpec((CPS * CH, LANE), lambda n: (n, 0))],
        out_shape=[jax.ShapeDtypeStruct((T, 3 * HW), F32), jax.ShapeDtypeStruct((T, LANE), F32)],
        compiler_params=_cparams(("parallel",)),
    )(qkv, bg, *cots)


def _head_args(refs):
    u, w, qe, kd, qk, egl = refs
    sls = [slice(h * HD, (h + 1) * HD) for h in range(HEADS)]
    return ([u[:, sl] for sl in sls], [w[:, sl].astype(F32) for sl in sls], [qe[:, sl].astype(F32) for sl in sls],
            [kd[:, sl].astype(F32) for sl in sls], [qk[h].astype(F32) for h in range(HEADS)],
            [egl[:, sl] for sl in sls])


def _chunk_scan_bwd(prep, s_hist, do, after=()):
    n_in = 8 + len(after)

    def body(*refs):
        u_r, w_r, qe_r, kd_r, qk_r, egl_r, sh_ref, do_ref = refs[:8]
        d_refs = refs[n_in:n_in + 6]
        ds_ref = refs[n_in + 6]

        @pl.when(pl.program_id(0) == 0)
        def _():
            ds_ref[...] = jnp.zeros_like(ds_ref)

        sls = [slice(h * HD, (h + 1) * HD) for h in range(HEADS)]
        ds = [ds_ref[h] for h in range(HEADS)]
        for ci in reversed(range(CPS)):
            r = slice(ci * CH, (ci + 1) * CH)
            args = _head_args((u_r.at[r, :], w_r.at[r, :], qe_r.at[r, :], kd_r.at[r, :], qk_r.at[:, r, :], egl_r.at[ci]))
            _, vjp = jax.vjp(_chunk_state_fn, *args, [sh_ref[h, ci] for h in range(HEADS)])
            du, dw, dqe, dkd, dqk, degl, ds = vjp(([do_ref[r, sl] for sl in sls], ds))
            for h, sl in enumerate(sls):
                for d_ref, val in zip(d_refs[:4], (du, dw, dqe, dkd)):
                    d_ref[r, sl] = val[h]
                d_refs[4][h, r, :] = dqk[h]
                d_refs[5][ci, :, sl] = degl[h]
        for h in range(HEADS):
            ds_ref[h] = ds[h]

    rev = lambda n: NGROUP - 1 - n
    return pl.pallas_call(
        body, name="gdn_scan_bwd", grid=(NGROUP,),
        in_specs=_chunk_group_specs(rev) + [pl.BlockSpec((HEADS, CPS, HD, HD), lambda n: (0, rev(n), 0, 0)),
                                            pl.BlockSpec((CPS * CH, HW), lambda n: (rev(n), 0))] + [ANY] * len(after),
        out_specs=_chunk_group_specs(rev), out_shape=_chunk_prep_shapes((F32,) * 6),
        scratch_shapes=[pltpu.VMEM((HEADS, HD, HD), F32)], compiler_params=_cparams(("arbitrary",)),
    )(*prep, s_hist, do, *after)


MIX_ROWS = 4 * GCH


def _mix_rows(o, z, ur, vr, *params):
    chunks = [slice(c * GCH, (c + 1) * GCH) for c in range(MIX_ROWS // GCH)]
    return jnp.concatenate([_mix_fn(o[r], z[r], ur[r], vr[r], *params) for r in chunks], axis=0)


def _mix_specs():
    pc = lambda c: pl.BlockSpec((MIX_ROWS, HW), lambda i: (i, c))
    return [pl.BlockSpec((MIX_ROWS, HW), lambda i: (i, 0)), pc(3), pc(4), pc(5), _whole((1, HD)), _whole((1, HW)),
            _whole((1, HW)), _whole((HEADS, GCH, GCH)), _whole((GCH, LANE))]


def _mix(o, proj, ong, lng, lnb, ws, bst, after=()):
    def body(o_ref, z_ref, u_ref, v_ref, ong_ref, lng_ref, lnb_ref, ws_ref, bs_ref, *rest):
        rest[-1][...] = _mix_rows(o_ref[...], z_ref[...], u_ref[...], v_ref[...], ong_ref[...], lng_ref[...],
                                  lnb_ref[...], ws_ref[...], bs_ref[...]).astype(BF16)

    return pl.pallas_call(
        body, name="mix", grid=(T // MIX_ROWS,), in_specs=_mix_specs() + [ANY] * len(after),
        out_specs=pl.BlockSpec((MIX_ROWS, D), lambda i: (i, 0)), out_shape=jax.ShapeDtypeStruct((T, D), BF16),
        compiler_params=_cparams(("parallel",)),
    )(o, proj, proj, proj, ong, lng, lnb, ws, bst, *after)


def _mix_bwd(o, proj, ong, lng, lnb, ws, bst, dmix):
    def body(o_ref, z_ref, u_ref, v_ref, ong_ref, lng_ref, lnb_ref, ws_ref, bs_ref, dm_ref,
             do_ref, dzuv_ref, dong_ref, dlng_ref, dlnb_ref, dws_ref, dbs_ref):
        _, vjp = jax.vjp(_mix_rows, o_ref[...], z_ref[...], u_ref[...], v_ref[...], ong_ref[...], lng_ref[...],
                         lnb_ref[...], ws_ref[...], bs_ref[...])
        do, dz, du, dv, dong, dlng, dlnb, dws, dbs = vjp(dm_ref[...])
        do_ref[...] = do
        dzuv_ref[:, 0:HW] = dz.astype(BF16)
        dzuv_ref[:, HW:2 * HW] = du.astype(BF16)
        dzuv_ref[:, 2 * HW:3 * HW] = dv.astype(BF16)
        acc = [(dong_ref, dong), (dlng_ref, dlng), (dlnb_ref, dlnb), (dws_ref, dws), (dbs_ref, dbs)]

        @pl.when(pl.program_id(0) == 0)
        def _():
            for r, val in acc:
                r[...] = val

        @pl.when(pl.program_id(0) > 0)
        def _():
            for r, val in acc:
                r[...] += val

    shp = lambda *s: jax.ShapeDtypeStruct(s, F32)
    return pl.pallas_call(
        body, name="mix_bwd", grid=(T // MIX_ROWS,),
        in_specs=_mix_specs() + [pl.BlockSpec((MIX_ROWS, D), lambda i: (i, 0))],
        out_specs=[pl.BlockSpec((MIX_ROWS, HW), lambda i: (i, 0)), pl.BlockSpec((MIX_ROWS, 3 * HW), lambda i: (i, 1)),
                   _whole((1, HD)), _whole((1, HW)), _whole((1, HW)), _whole((HEADS, GCH, GCH)), _whole((GCH, LANE))],
        out_shape=[shp(T, HW), jax.ShapeDtypeStruct((T, NP), BF16), shp(1, HD), shp(1, HW), shp(1, HW),
                   shp(HEADS, GCH, GCH), shp(GCH, LANE)],
        compiler_params=_cparams(("arbitrary",)),
    )(o, proj, proj, proj, ong, lng, lnb, ws, bst, dmix)


def _swiglu_epilogue(accs, _):
    gate, up = accs
    return [gate, up, _silu(gate) * up]


def _swiglu_bwd_epilogue(accs, extras):
    dact = accs[0]
    gate, up = (e.astype(F32) for e in extras)
    sg = _sigmoid(gate)
    return [dact * up * (sg * (1.0 + gate * (1.0 - sg))), dact * (gate * sg)]


def _layer_fwd(h, p):
    hn = p.pop("hn") if "hn" in p else _rmsnorm("rms_mix", h, p["norm_mix"])
    proj = _mm("in_proj", "nn", hn[None], [p["w_in"][None]], tm=512, tn=NP, tk=D, sub_m=2)[0][0]
    qkv = _prep(proj, p["conv_w"])
    bg = _gates(proj, p["a_log"], p["dt_bias"])
    prep, o, s_hist = _delta_rule(qkv, bg)
    mix = _mix(o, proj, p["o_norm_g"], p["ln_v_g"], p["ln_v_b"], p["w_s"], p["bst"],
               p.pop("before_mix")(o) if "before_mix" in p else ())
    if "late" in p:
        p.update(p.pop("late")(mix))
    h1 = _mm("out_proj", "nn", mix[None], [p["w_out"].reshape(1, D, D)], tm=T, tn=512, tk=D, resid=h[None],
             sub_m=4)[0][0]
    h2n = _rmsnorm("rms_ffn", h1, p["norm_ffn"])
    gate, up, act = _mm("ffn_in", "nt", h2n[None], [p["w_gate"], p["w_up"]], tm=1024, tn=FF_SH, tk=D,
                        out_dtypes=(BF16, BF16, BF16), epilogue=_swiglu_epilogue, sub_m=4)
    then = p.pop("before_ffn_out")(act) if "before_ffn_out" in p else ()
    h2 = _mm("ffn_out", "nn", act, [p["w_down"]], tm=1024, tn=512, tk=FF_SH, reduce_g=True, fold_g=True,
             resid=h1[None], sub_m=2, after=then)[0][0]
    saved = dict(h=h, hn=hn, proj=proj, qkv=qkv, bg=bg, prep=prep, o=o, s_hist=s_hist, mix=mix, h1=h1, h2n=h2n,
                 gate=gate, up=up, act=act)
    return h2, saved


def _layer_bwd_ffn(dh2, dh2b, p, s, after=()):
    dh2b = dh2b[None]
    dgate, dup = _mm("ffn_out_bwd", "nt", dh2b, [p["w_down"]], tm=1024, tn=FF_SH, tk=D, out_dtypes=(BF16, BF16),
                     extras=(s["gate"], s["up"]), epilogue=_swiglu_bwd_epilogue, after=after, sub_m=4)
    dh2n = _mm("ffn_gate_bwd", "nn", dgate, [p["w_gate"]], tm=1024, tn=512, tk=FF_SH, reduce_g=True, fold_g=True,
               sub_m=2)[0]
    dh2n = _mm("ffn_up_bwd", "nn", dup, [p["w_up"]], tm=1024, tn=512, tk=FF_SH, reduce_g=True, fold_g=True,
               resid=dh2n, sub_m=2)[0][0]
    dh1, dh1b, d_norm_ffn = _rmsnorm_bwd("rms_ffn_bwd", dh2n, s["h1"], p["norm_ffn"], dh2)
    d_w_down = _mm("ffn_wdown_grad", "tn", s["act"], [dh2b], tm=FF_SH, tn=512, tk=T)[0]
    d_w_gate = _mm("ffn_wgate_grad", "tn", dgate, [s["h2n"][None]], tm=FF_SH, tn=512, tk=T)[0]
    d_w_up = _mm("ffn_wup_grad", "tn", dup, [s["h2n"][None]], tm=FF_SH, tn=512, tk=T)[0]
    return dh1, dh1b, dict(norm_ffn=d_norm_ffn, w_gate=d_w_gate, w_up=d_w_up, w_down=d_w_down)


def _layer_bwd_mixer(dh1, dh1b, p, s, after=(), midway=None, late=None):
    dh1b = dh1b[None]
    dmix = _mm("out_proj_bwd", "nt", dh1b, [p["w_out"].reshape(1, D, D)], tm=T, tn=512, tk=D, after=after,
               sub_m=4)[0][0]
    d_w_out = _mm("out_proj_wgrad", "tn", s["mix"][None], [dh1b], tm=1024, tn=512, tk=T)[0][0]
    do, dproj, d_ong, d_lng, d_lnb, d_ws, d_bst = _mix_bwd(
        s["o"], s["proj"], p["o_norm_g"], p["ln_v_g"], p["ln_v_b"], p["w_s"], p["bst"], dmix)
    then = midway(do) if midway is not None else ()
    dqkv, dbg = _chunk_prep_bwd(s["qkv"], s["bg"], _chunk_scan_bwd(s["prep"], s["s_hist"], do, then))
    dproj, d_conv = _prep_bwd(s["proj"], p["conv_w"], dqkv, dproj)
    dproj, d_a_log, d_dt_bias = _gates_bwd(s["proj"], p["a_log"], p["dt_bias"], dbg, dproj)
    dproj = dproj[None]
    d_w_in = _mm("in_proj_wgrad", "tn", s["hn"][None], [dproj], tm=512, tn=640, tk=T)[0]
    last = late(dict(w_in=d_w_in, w_out=d_w_out)) if late is not None else ()
    dhn = _mm("in_proj_bwd", "nt", dproj, [p["w_in"][None]], tm=1024, tn=512, tk=NP, after=last,
              sub_m=2)[0][0]
    dh, dhb, d_norm_mix = _rmsnorm_bwd("rms_mix_bwd", dhn, s["h"], p["norm_mix"], dh1)
    grads = dict(norm_mix=d_norm_mix, w_in=d_w_in, conv_w=d_conv, a_log=d_a_log, dt_bias=d_dt_bias, o_norm_g=d_ong,
                 ln_v_g=d_lng, ln_v_b=d_lnb, w_s=d_ws, bst=d_bst, w_out=d_w_out)
    return dh, dhb, grads


def _lanes(v, off=0):
    return jnp.zeros((1, LANE), F32).at[0, off:off + v.shape[0]].set(v)


def _w_in_pieces():
    regions = [(0, 2048, 0), (2048, 2056, BA_OFF), (2056, IN_DIM, 2048)]
    sh = IN_DIM // NCHIP
    out = []
    for j in range(NCHIP):
        for lo, hi, at in regions:
            a, b = max(lo, j * sh), min(hi, (j + 1) * sh)
            if a < b:
                out.append((j, a - j * sh, at + a - lo, b - a))
    return out


W_IN_PIECES = _w_in_pieces()
WT = 256


def _assemble_w_in(gathered, own, place):
    def body(place_ref, g_ref, own_ref, o_ref):
        o_ref[:, IN_DIM:] = jnp.zeros((WT, NP - IN_DIM), BF16)
        mine = own_ref[...]
        for j, src, dst, width in W_IN_PIECES:
            val = jnp.where(place_ref[0] == j, mine[:, src:src + width], g_ref[j, :, src:src + width])
            o_ref[:, dst:dst + width] = val

    sh = IN_DIM // NCHIP
    return pl.pallas_call(
        body, name="assemble_w_in",
        grid_spec=pltpu.PrefetchScalarGridSpec(
            num_scalar_prefetch=1, grid=(D // WT,),
            in_specs=[pl.BlockSpec((NCHIP, WT, sh), lambda i, place_ref: (0, i, 0)),
                      pl.BlockSpec((WT, sh), lambda i, place_ref: (i, 0))],
            out_specs=pl.BlockSpec((WT, NP), lambda i, place_ref: (i, 0))),
        out_shape=jax.ShapeDtypeStruct((D, NP), BF16), compiler_params=_cparams(("parallel",)),
    )(place, gathered, own)


def _layer_params(l, big, small):
    return dict(
        {k: v for k, v in big.items() if k != "conv_w"},
        conv_w=jnp.concatenate([big["conv_w"][j, l] for j in range(NCHIP)], axis=1),
        norm_mix=small["norm_mix"][l][None], norm_ffn=small["norm_ffn"][l][None],
        a_log=_lanes(small["a_log"][l], HEADS), dt_bias=_lanes(small["dt_bias"][l], HEADS),
        o_norm_g=small["o_norm_g"][l][None], ln_v_g=small["ln_v_g"][l][None], ln_v_b=small["ln_v_b"][l][None],
        w_s=small["w_s"][l],
        bst=jnp.pad(small["b_s"][l].T, ((0, 0), (0, LANE - HEADS))),
    )


def _reference_layout(g):
    return dict(
        w_in=g["w_in"],
        w_out=g["w_out"].reshape(NCHIP, D // NCHIP, D),
        w_gate=g["w_gate"], w_up=g["w_up"], w_down=g["w_down"],
        conv_w=g["conv_w"], norm_mix=g["norm_mix"][0], norm_ffn=g["norm_ffn"][0],
        a_log=g["a_log"][0, HEADS:2 * HEADS], dt_bias=g["dt_bias"][0, HEADS:2 * HEADS],
        o_norm_g=g["o_norm_g"][0], ln_v_g=g["ln_v_g"][0], ln_v_b=g["ln_v_b"][0], w_s=g["w_s"],
        b_s=g["bst"][:, :HEADS].T,
    )


def _forward(x, tgt, layers, norm_final):
    h = x
    saved, params = [], []
    for p in layers:
        p = p(h) if callable(p) else p
        h, s = _layer_fwd(h, p)
        saved.append(s)
        params.append(p)
    return (saved, params) + tuple(_loss_head(h, norm_final, tgt))


def _local_step(x, tgt, layers, norm_final):
    saved, layers, loss, dh, dhb, d_norm_final = _forward(x, tgt, layers, norm_final)
    grads = [None] * DEPTH
    for l in reversed(range(DEPTH)):
        dh1, dh1b, g_ffn = _layer_bwd_ffn(dh, dhb, layers[l], saved[l])
        dh, dhb, g_mix = _layer_bwd_mixer(dh1, dh1b, layers[l], saved[l])
        grads[l] = {**g_ffn, **g_mix}
    return loss, dh, grads, d_norm_final


def _place():
    x, y, c = lax.axis_index("x"), lax.axis_index("y"), lax.axis_index("c")
    return x, y, c, [(1 - x, y), (x, 1 - y), (1 - x, 1 - y)]


def _remote(src, dst, send_sem, recv_sem, to):
    return pltpu.make_async_remote_copy(src_ref=src, dst_ref=dst, send_sem=send_sem, recv_sem=recv_sem,
                                        device_id=to, device_id_type=MESH)


def _comm_call(name, body, ins, out_shape, n_sems, aliases=None):
    return pl.pallas_call(
        body, name=name, in_specs=[ANY] * len(ins), out_specs=[ANY] * len(out_shape), out_shape=out_shape,
        scratch_shapes=[pltpu.SemaphoreType.DMA((n,)) for n in n_sems], input_output_aliases=aliases or {},
        compiler_params=pltpu.CompilerParams(has_side_effects=True),
    )(*ins)


def _half_rows(ref, of_c, dim):
    hr = ref.shape[dim] // 2
    return pl.ds(pl.multiple_of(of_c * hr, BF16_ROWS), hr)


def _gather_plan(whole):
    def plan(srcs, lands):
        x, y, c, others = _place()
        chip = 2 * x + y
        out = []
        for src, land, all_of_it in zip(srcs, lands, whole):
            for ox, oy in others:
                if all_of_it:
                    out.append((src, land.at[chip], (ox, oy, c)))
                else:
                    out.append((src.at[_half_rows(src, c, 0)], land.at[chip, _half_rows(src, c, 0)], (ox, oy, c)))
        return out
    return plan


def _forward_halves(lands):
    n = len(lands)

    def body(*refs):
        outs = refs[n:2 * n]
        send_s, recv_s = refs[2 * n:]
        x, y, c, others = _place()
        sibling = (x, y, 1 - c)
        copies = []
        for a in range(n):
            for k, (ox, oy) in enumerate(others):
                mine = outs[a].at[2 * ox + oy, _half_rows(outs[a], c, 1)]
                copies.append(_remote(mine, mine, send_s.at[3 * a + k], recv_s.at[3 * a + k], sibling))
        for cp in copies:
            cp.start()
        for a in range(n):
            for k, (ox, oy) in enumerate(others):
                landed = outs[a].at[2 * ox + oy, _half_rows(outs[a], 1 - c, 1)]
                _remote(landed, landed, send_s.at[3 * a + k], recv_s.at[3 * a + k], sibling).wait_recv()
        for cp in copies:
            cp.wait_send()

    out_shape = [jax.ShapeDtypeStruct(g.shape, g.dtype) for g in lands]
    return _comm_call("forward_halves", body, lands, out_shape, [3 * n, 3 * n], aliases={a: a for a in range(n)})


def _forward_refs(bufs, incoming):
    x, y, c, others = _place()
    return (x, y, 1 - c), [b.at[2 * ox + oy, _half_rows(b, 1 - c if incoming else c, 1)]
                           for b in bufs for ox, oy in others]


def _forward_start(name, bufs, after):
    n = len(bufs)
    bufs = [pltpu.with_memory_space_constraint(b, pltpu.HBM) for b in bufs]

    def body(*refs):
        send_s, recv_s = refs[n + len(after)], refs[n + len(after) + 1]
        sibling, mine = _forward_refs(refs[:n], incoming=False)
        for i, ref in enumerate(mine):
            _remote(ref, ref, send_s.at[i], recv_s.at[i], sibling).start()
        refs[-1][...] = jnp.zeros_like(refs[-1])

    out = pl.pallas_call(
        body, name=name, in_specs=[HBM_SPEC] * n + [ANY] * len(after),
        out_specs=[SEM_SPEC, SEM_SPEC] + [HBM_SPEC] * n + [pl.BlockSpec(memory_space=pltpu.VMEM)],
        out_shape=[pltpu.SemaphoreType.DMA((3 * n,)), pltpu.SemaphoreType.DMA((3 * n,))]
        + [pltpu.HBM(b.shape, b.dtype) for b in bufs] + [jax.ShapeDtypeStruct((F32_ROWS, LANE), F32)],
        input_output_aliases={i: 2 + i for i in range(n)},
        compiler_params=pltpu.CompilerParams(has_side_effects=DATAFLOW),
    )(*bufs, *after)
    return dict(sems=out[:2], bufs=out[2:2 + n], token=out[-1])


def _forward_wait(name, started, after):
    n = len(started["bufs"])

    def body(*refs):
        send_s, recv_s = refs[n], refs[n + 1]
        sibling, mine = _forward_refs(refs[:n], incoming=False)
        _, theirs = _forward_refs(refs[:n], incoming=True)
        for i, (sent, landed) in enumerate(zip(mine, theirs)):
            _remote(sent, sent, send_s.at[i], recv_s.at[i], sibling).wait_send()
            _remote(landed, landed, send_s.at[i], recv_s.at[i], sibling).wait_recv()

    return pl.pallas_call(
        body, name=name, in_specs=[HBM_SPEC] * n + [SEM_SPEC, SEM_SPEC] + [ANY] * len(after),
        out_specs=[HBM_SPEC] * n, out_shape=[pltpu.HBM(b.shape, b.dtype) for b in started["bufs"]],
        input_output_aliases={i: i for i in range(n)},
        compiler_params=pltpu.CompilerParams(has_side_effects=DATAFLOW),
    )(*started["bufs"], *started["sems"], *after)


HBM_SPEC = pl.BlockSpec(memory_space=pltpu.HBM)
SEM_SPEC = pl.BlockSpec(memory_space=pltpu.SEMAPHORE)
DATAFLOW = pltpu.SideEffectType.DATAFLOW_SIDE_EFFECTING


def _exchange_plan(srcs, lands):
    x, y, c, _ = _place()
    plan = []
    for src, land in zip(srcs, lands):
        hr = src.shape[1] // 2
        plan.append((src.at[:, pl.ds(pl.multiple_of((1 - c) * hr, 8), hr)], land, (x, y, 1 - c)))
    return plan


def _scatter_plan(srcs, lands):
    x, y, c, others = _place()
    return [(src.at[2 * ox + oy], land.at[k], (ox, oy, c))
            for src, land in zip(srcs, lands) for k, (ox, oy) in enumerate(others)]


def _split_start(name, plan, srcs, land_shapes, n_copies, after=()):
    n = len(srcs)
    lands = [pltpu.with_memory_space_constraint(lax.empty(s.shape, s.dtype), pltpu.HBM) for s in land_shapes]
    srcs = [pltpu.with_memory_space_constraint(s, pltpu.HBM) for s in srcs]

    def body(*refs):
        send_s, recv_s = refs[2 * n + len(after)], refs[2 * n + len(after) + 1]
        for i, (src, dst, to) in enumerate(plan(refs[:n], refs[n:2 * n])):
            _remote(src, dst, send_s.at[i], recv_s.at[i], to).start()
        refs[-1][...] = jnp.zeros_like(refs[-1])

    thru = [pltpu.HBM(s.shape, s.dtype) for s in srcs + lands]
    out = pl.pallas_call(
        body, name=name, in_specs=[HBM_SPEC] * (2 * n) + [ANY] * len(after),
        out_specs=[SEM_SPEC, SEM_SPEC] + [HBM_SPEC] * (2 * n) + [pl.BlockSpec(memory_space=pltpu.VMEM)],
        out_shape=[pltpu.SemaphoreType.DMA((n_copies,)), pltpu.SemaphoreType.DMA((n_copies,))] + thru
        + [jax.ShapeDtypeStruct((F32_ROWS, LANE), F32)],
        input_output_aliases={i: 2 + i for i in range(2 * n)},
        compiler_params=pltpu.CompilerParams(has_side_effects=DATAFLOW),
    )(*srcs, *lands, *after)
    return dict(sems=out[:2], srcs=out[2:2 + n], lands=out[2 + n:2 + 2 * n], token=out[-1])


def _split_wait(name, plan, started, after):
    n = len(started["srcs"])
    after = list(after) if isinstance(after, (list, tuple)) else [after]

    def body(*refs):
        send_s, recv_s = refs[2 * n], refs[2 * n + 1]
        for i, (src, dst, to) in enumerate(plan(refs[:n], refs[n:2 * n])):
            cp = _remote(src, dst, send_s.at[i], recv_s.at[i], to)
            cp.wait_send()
            cp.wait_recv()

    arrs = list(started["srcs"]) + list(started["lands"])
    out = pl.pallas_call(
        body, name=name, in_specs=[HBM_SPEC] * (2 * n) + [SEM_SPEC, SEM_SPEC] + [ANY] * len(after),
        out_specs=[HBM_SPEC] * (2 * n), out_shape=[pltpu.HBM(s.shape, s.dtype) for s in arrs],
        input_output_aliases={i: i for i in range(2 * n)},
        compiler_params=pltpu.CompilerParams(has_side_effects=DATAFLOW),
    )(*arrs, *started["sems"], *after)
    return out[:n], out[n:]


def _join_halves(name, rs):
    n = len(rs)

    def body(*refs):
        outs = refs[n:2 * n]
        send_s, recv_s = refs[2 * n:]
        x, y, c, _ = _place()
        sibling = (x, y, 1 - c)

        def half(a, of_c):
            hr = outs[a].shape[1] // 2
            return outs[a].at[:, pl.ds(pl.multiple_of(of_c * hr, 8), hr)]

        copies = [_remote(half(a, c), half(a, c), send_s.at[a], recv_s.at[a], sibling) for a in range(n)]
        for cp in copies:
            cp.start()
        for a in range(n):
            landed = half(a, 1 - c)
            _remote(landed, landed, send_s.at[a], recv_s.at[a], sibling).wait_recv()
        for cp in copies:
            cp.wait_send()

    out_shape = [jax.ShapeDtypeStruct(r.shape, r.dtype) for r in rs]
    return _comm_call(name, body, rs, out_shape, [n, n], aliases={a: a for a in range(n)})


def _allreduce_small(buf, after=()):
    r = buf.shape[0]
    hr = r // 2

    def body(in_ref, *refs):
        out_ref, theirs, by_chip, send_s, recv_s = refs[len(after):]
        x, y, c, others = _place()
        chip = 2 * x + y
        sibling = (x, y, 1 - c)
        mine = pl.ds(pl.multiple_of(c * hr, F32_ROWS), hr)
        swap = _remote(in_ref, theirs, send_s.at[0], recv_s.at[0], sibling)
        swap.start()
        swap.wait()
        by_chip[chip] = in_ref[mine, :] + theirs[mine, :]
        sends = [_remote(by_chip.at[chip], by_chip.at[chip], send_s.at[1 + k], recv_s.at[1 + k], (ox, oy, c))
                 for k, (ox, oy) in enumerate(others)]
        for cp in sends:
            cp.start()
        for k, (ox, oy) in enumerate(others):
            landed = by_chip.at[2 * ox + oy]
            _remote(landed, landed, send_s.at[1 + k], recv_s.at[1 + k], (ox, oy, c)).wait_recv()
        for cp in sends:
            cp.wait_send()
        out_ref[mine, :] = (by_chip[0] + by_chip[1]) + (by_chip[2] + by_chip[3])
        back = _remote(out_ref.at[mine], out_ref.at[mine], send_s.at[NCHIP], recv_s.at[NCHIP], sibling)
        back.start()
        other = out_ref.at[pl.ds(pl.multiple_of((1 - c) * hr, F32_ROWS), hr)]
        _remote(other, other, send_s.at[NCHIP], recv_s.at[NCHIP], sibling).wait_recv()
        back.wait_send()

    vm = pl.BlockSpec(memory_space=pltpu.VMEM)
    return pl.pallas_call(
        body, name="allreduce_small", in_specs=[vm] + [ANY] * len(after), out_specs=vm,
        out_shape=jax.ShapeDtypeStruct((r, LANE), F32),
        scratch_shapes=[pltpu.VMEM((r, LANE), F32), pltpu.VMEM((NCHIP, hr, LANE), F32),
                        pltpu.SemaphoreType.DMA((NCHIP + 1,)), pltpu.SemaphoreType.DMA((NCHIP + 1,))],
        compiler_params=pltpu.CompilerParams(has_side_effects=True, vmem_limit_bytes=VMEM_LIMIT),
    )(buf, *after)


MAX_ROW_TILE = 512
BF16_ROWS = 16


def _row_tile(rows):
    for t in range(min(rows, MAX_ROW_TILE) // BF16_ROWS * BF16_ROWS, 0, -BF16_ROWS):
        if rows % t == 0:
            return t
    raise ValueError(rows)


def _sum_halves(g, theirs, place):
    nch, rows, cols = g.shape
    hr = rows // 2
    tr = _row_tile(hr)

    def body(place_ref, g_ref, t_ref, o_ref, ob_ref):
        s = g_ref[...] + t_ref[...]
        ob_ref[...] = s.astype(BF16)

        @pl.when(pl.program_id(1) == place_ref[0])
        def _():
            o_ref[...] = s

    blk = pl.BlockSpec((None, tr, cols), lambda i, j, place_ref: (j, i, 0))
    return pl.pallas_call(
        body, name="sum_halves",
        grid_spec=pltpu.PrefetchScalarGridSpec(
            num_scalar_prefetch=1, grid=(hr // tr, nch),
            in_specs=[pl.BlockSpec((None, None, tr, cols), lambda i, j, place_ref: (j, place_ref[1], i, 0)), blk],
            out_specs=[pl.BlockSpec((tr, cols), lambda i, j, place_ref: (i, 0)), blk]),
        out_shape=[jax.ShapeDtypeStruct((hr, cols), F32), jax.ShapeDtypeStruct((nch, hr, cols), BF16)],
        compiler_params=_cparams(("parallel", "arbitrary")),
    )(place, g.reshape(nch, 2, hr, cols), theirs)


def _sum_halves_w_in(g, theirs, place):
    hr = D // 2
    sh = IN_DIM // NCHIP

    def body(place_ref, g_ref, t_ref, o_ref, ob_ref):
        s = g_ref[...] + t_ref[...]
        for j, dst, src, width in W_IN_PIECES:
            ob_ref[j, :, dst:dst + width] = s[:, src:src + width].astype(BF16)

            @pl.when(place_ref[0] == j)
            def _():
                o_ref[:, dst:dst + width] = s[:, src:src + width]

    return pl.pallas_call(
        body, name="sum_halves_w_in",
        grid_spec=pltpu.PrefetchScalarGridSpec(
            num_scalar_prefetch=1, grid=(hr // WT,),
            in_specs=[pl.BlockSpec((None, WT, NP), lambda i, place_ref: (place_ref[1], i, 0)),
                      pl.BlockSpec((None, WT, NP), lambda i, place_ref: (0, i, 0))],
            out_specs=[pl.BlockSpec((WT, sh), lambda i, place_ref: (i, 0)),
                       pl.BlockSpec((NCHIP, WT, sh), lambda i, place_ref: (0, i, 0))]),
        out_shape=[jax.ShapeDtypeStruct((hr, sh), F32), jax.ShapeDtypeStruct((NCHIP, hr, sh), BF16)],
        compiler_params=_cparams(("parallel",)),
    )(place, g.reshape(2, hr, NP), theirs)


def _sum_chips(p, q, place, l, into=None, after=()):
    extra = ([into] if into is not None else []) + list(after)
    rows, cols = p.shape
    tr = _row_tile(rows)
    steps = rows // tr

    def body(place_ref, p_ref, q0, q1, q2, *rest):
        rest[-1][...] = ((p_ref[...] + q0[...].astype(F32)) + q1[...].astype(F32)) + q2[...].astype(F32)

    qs = lambda k: pl.BlockSpec((None, tr, cols), lambda i, place_ref: (k, i, 0))
    return pl.pallas_call(
        body, name="sum_chips",
        grid_spec=pltpu.PrefetchScalarGridSpec(
            num_scalar_prefetch=1, grid=(steps,),
            in_specs=[pl.BlockSpec((tr, cols), lambda i, place_ref: (i, 0)), qs(0), qs(1), qs(2)]
            + [ANY] * len(extra),
            out_specs=pl.BlockSpec((None, tr, cols), lambda i, place_ref: (l, place_ref[1] * steps + i, 0))),
        out_shape=jax.ShapeDtypeStruct((DEPTH, 2 * rows, cols), F32),
        input_output_aliases={5: 0} if into is not None else {},
        compiler_params=_cparams(("parallel",)),
    )(place, p, q, q, q, *extra)


def _adamw_fn(w, g, m, v):
    nm = ADAM_B1 * m + (1.0 - ADAM_B1) * g
    nv = ADAM_B2 * v + (1.0 - ADAM_B2) * jnp.square(g)
    m_hat = nm / (1.0 - ADAM_B1 ** ADAM_STEP)
    v_hat = nv / (1.0 - ADAM_B2 ** ADAM_STEP)
    return -ADAM_LR * (m_hat / (jnp.sqrt(v_hat) + ADAM_EPS) + ADAM_WD * w), nm, nv


def _adamw(w, g, m, v):
    layers, rows, cols = w.shape
    tr = _row_tile(rows)

    def body(w_ref, g_ref, m_ref, v_ref, d_ref, nm_ref, nv_ref):
        d_ref[...], nm_ref[...], nv_ref[...] = _adamw_fn(w_ref[...], g_ref[...], m_ref[...], v_ref[...])

    blk = pl.BlockSpec((None, tr, cols), lambda l, i: (l, i, 0))
    return pl.pallas_call(
        body, name="adamw", grid=(layers, rows // tr), in_specs=[blk] * 4, out_specs=[blk] * 3,
        out_shape=[jax.ShapeDtypeStruct(w.shape, F32)] * 3, compiler_params=_cparams(("parallel", "parallel")),
    )(w, g, m, v)


def _adamw_small(ws, gs, ms, vs):
    n = len(ws)

    def body(*refs):
        for i in range(n):
            w_ref, g_ref, m_ref, v_ref, d_ref, nm_ref, nv_ref = (refs[k * n + i] for k in range(7))
            d_ref[...], nm_ref[...], nv_ref[...] = _adamw_fn(w_ref[...], g_ref[...], m_ref[...], v_ref[...])

    vm = pl.BlockSpec(memory_space=pltpu.VMEM)
    out = pl.pallas_call(
        body, name="adamw_small", in_specs=[vm] * (4 * n), out_specs=[vm] * (3 * n),
        out_shape=[jax.ShapeDtypeStruct(a.shape, F32) for a in list(ws) * 3],
        compiler_params=pltpu.CompilerParams(vmem_limit_bytes=VMEM_LIMIT),
    )(*ws, *gs, *ms, *vs)
    return out[:n], out[n:2 * n], out[2 * n:]


BIG = ("w_in", "w_out", "w_gate", "w_up", "w_down")
SMALL = ("norm_mix", "a_log", "dt_bias", "o_norm_g", "ln_v_g", "ln_v_b", "w_s", "b_s", "norm_ffn", "norm_final")
ORDER = ("norm_mix", "w_in", "conv_w", "a_log", "dt_bias", "o_norm_g", "ln_v_g", "ln_v_b", "w_s", "b_s", "w_out",
         "norm_ffn", "w_gate", "w_up", "w_down", "norm_final")


F32_ROWS = 8
PACK_ROWS = 128


def _lane_rows(size):
    return -(-size // (F32_ROWS * LANE)) * F32_ROWS


def _pack(arrs):
    parts = [jnp.pad(a.reshape(-1), (0, _lane_rows(a.size) * LANE - a.size)).reshape(-1, LANE) for a in arrs]
    rows = sum(p.shape[0] for p in parts)
    if rows % PACK_ROWS:
        parts.append(jnp.zeros((-rows % PACK_ROWS, LANE), F32))
    return jnp.concatenate(parts, axis=0)


def _unpack(buf, like):
    out, row = [], 0
    for a in like:
        n = _lane_rows(a.size)
        out.append(buf[row:row + n].reshape(-1)[:a.size].reshape(a.shape))
        row += n
    return out


def kernel(x, norm_mix, w_in, conv_w, a_log, dt_bias, o_norm_g, ln_v_g, ln_v_b, w_s, b_s, w_out, norm_ffn, w_gate, w_up, w_down, norm_final, loss_target, m_norm_mix, m_w_in, m_conv_w, m_a_log, m_dt_bias, m_o_norm_g, m_ln_v_g, m_ln_v_b, m_w_s, m_b_s, m_w_out, m_norm_ffn, m_w_gate, m_w_up, m_w_down, m_norm_final, v_norm_mix, v_w_in, v_conv_w, v_a_log, v_dt_bias, v_o_norm_g, v_ln_v_g, v_ln_v_b, v_w_s, v_b_s, v_w_out, v_norm_ffn, v_w_gate, v_w_up, v_w_down, v_norm_final):
    w = dict(norm_mix=norm_mix, w_in=w_in, conv_w=conv_w, a_log=a_log, dt_bias=dt_bias, o_norm_g=o_norm_g,
             ln_v_g=ln_v_g, ln_v_b=ln_v_b, w_s=w_s, b_s=b_s, w_out=w_out, norm_ffn=norm_ffn, w_gate=w_gate, w_up=w_up,
             w_down=w_down, norm_final=norm_final)
    m = dict(norm_mix=m_norm_mix, w_in=m_w_in, conv_w=m_conv_w, a_log=m_a_log, dt_bias=m_dt_bias, o_norm_g=m_o_norm_g,
             ln_v_g=m_ln_v_g, ln_v_b=m_ln_v_b, w_s=m_w_s, b_s=m_b_s, w_out=m_w_out, norm_ffn=m_norm_ffn,
             w_gate=m_w_gate, w_up=m_w_up, w_down=m_w_down, norm_final=m_norm_final)
    v = dict(norm_mix=v_norm_mix, w_in=v_w_in, conv_w=v_conv_w, a_log=v_a_log, dt_bias=v_dt_bias, o_norm_g=v_o_norm_g,
             ln_v_g=v_ln_v_g, ln_v_b=v_ln_v_b, w_s=v_w_s, b_s=v_b_s, w_out=v_w_out, norm_ffn=v_norm_ffn,
             w_gate=v_w_gate, w_up=v_w_up, w_down=v_w_down, norm_final=v_norm_final)
    chip = 2 * lax.axis_index("x") + lax.axis_index("y")
    place = jnp.stack([chip, lax.axis_index("c")]).astype(jnp.int32)

    def kernel_view(n, a):
        return jnp.swapaxes(a, 1, 2) if n in ("w_gate", "w_up") else a

    own = {n: [kernel_view(n, w[n])[l].astype(BF16) for l in range(DEPTH)] for n in BIG}
    by_chip = lambda a: jax.ShapeDtypeStruct((NCHIP,) + a.shape, a.dtype)

    def start(name, srcs, whole, after=()):
        return _split_start(name, _gather_plan(whole), srcs, [by_chip(a) for a in srcs], 3 * len(srcs), after)

    def finish(name, started, whole, after):
        srcs, lands = _split_wait(name, _gather_plan(whole), started, after)
        passed = iter(_forward_halves([g for g, all_of_it in zip(lands, whole) if not all_of_it]))
        lands = [g if all_of_it else next(passed) for g, all_of_it in zip(lands, whole)]
        return srcs, [lax.dynamic_update_index_in_dim(g, o, chip, 0) for g, o in zip(lands, srcs)]

    ffn = BIG[1:]
    first = start("gather_first_start", [own["w_in"][0], conv_w], [False, True])
    early = start("gather_early_start", [own[n][0] for n in ffn], [False] * len(ffn), [first["token"]])
    mid = start("gather_mid_start", [own["w_in"][1]], [False], [early["token"]])
    later = start("gather_later_start", [own[n][1] for n in ffn], [False] * len(ffn), [mid["token"]])
    hn = _rmsnorm("rms_mix", x[0], norm_mix[0][None])
    (own_w_in, _), (w_in_by_chip, conv_by_chip) = finish("gather_first_wait", first, [False, True], [later["token"], hn])

    passing = {}

    def pass_on(tag, started, n):
        def at(after):
            srcs, lands = _split_wait(f"gather_{tag}_wait", _gather_plan([False] * n), started, after)
            passing[tag] = srcs, _forward_start(f"forward_{tag}_start", lands, ())
            return [passing[tag][1]["token"]]
        return at

    def passed_on(tag, after):
        srcs, fwd = passing[tag]
        lands = _forward_wait(f"forward_{tag}_wait", fwd, [after])
        return srcs, [lax.dynamic_update_index_in_dim(g, o, chip, 0) for g, o in zip(lands, srcs)]

    def late(tag):
        return lambda after: dict(zip(ffn, passed_on(tag, after)[1]))

    layer0 = _layer_params(0, dict(
        hn=hn, w_in=_assemble_w_in(w_in_by_chip, own_w_in, place), conv_w=conv_by_chip, late=late("early"),
        before_mix=pass_on("early", early, len(ffn)), before_ffn_out=pass_on("mid", mid, 1)), w)

    def layer1(after):
        (own_w_in1,), (w_in1_by_chip,) = passed_on("mid", after)
        return _layer_params(1, dict(w_in=_assemble_w_in(w_in1_by_chip, own_w_in1, place), conv_w=conv_by_chip,
                                     late=late("later"), before_mix=pass_on("later", later, len(ffn))), w)

    saved, layers, loss_lanes, dh, dhb, d_norm_final = _forward(x[0], loss_target[0], [layer0, layer1],
                                                                 norm_final[None])

    sums, arrived = {}, {}

    def exchange_start(tag, l, names, grads, after=()):
        mine = [grads[n] for n in names]
        shapes = [jax.ShapeDtypeStruct((g.shape[0], g.shape[1] // 2, g.shape[2]), F32) for g in mine]
        return tag, l, names, _split_start(f"exchange_{tag}_start", _exchange_plan, mine, shapes, len(mine), after)

    def add_halves(l, names, mine, theirs):
        for n, g, t in zip(names, mine, theirs):
            sums[l, n] = (_sum_halves_w_in if n == "w_in" else _sum_halves)(g, t, place)

    def exchange_wait(handle, after):
        tag, l, names, started = handle
        add_halves(l, names, *_split_wait(f"exchange_{tag}_wait", _exchange_plan, started, after))

    def scatter_start(tag, l, names, after=()):
        partial = [sums[l, n][1] for n in names]
        shapes = [jax.ShapeDtypeStruct((3,) + p.shape[1:], p.dtype) for p in partial]
        return tag, l, names, _split_start(f"scatter_{tag}_start", _scatter_plan, partial, shapes, 3 * len(names), after)

    def scatter_wait(handle, after):
        tag, l, names, started = handle
        for n, q in zip(names, _split_wait(f"scatter_{tag}_wait", _scatter_plan, started, after)[1]):
            arrived[l, n] = q

    last = DEPTH - 1
    swiglu = BIG[2:]
    dh1, dh1b, g_ffn = _layer_bwd_ffn(dh, dhb, layers[last], saved[last])
    dh, dhb, g_mix = _layer_bwd_mixer(dh1, dh1b, layers[last], saved[last])
    gl = [None, _reference_layout({**g_ffn, **g_mix})]
    ex_last = exchange_start("last", last, BIG, gl[last])
    dh1, dh1b, g_ffn = _layer_bwd_ffn(dh, dhb, layers[0], saved[0], after=[ex_last[-1]["token"]])
    exchange_wait(ex_last, dh1)
    sc_last = scatter_start("last", last, BIG)
    ex_ffn = exchange_start("swiglu", 0, swiglu, g_ffn, [sc_last[-1]["token"]])
    sc_ffn = []

    def midway(do):
        exchange_wait(ex_ffn, do)
        sc_ffn.append(scatter_start("swiglu", 0, swiglu))
        return [sc_ffn[0][-1]["token"]]

    ex_rest = []

    def late(grads):
        rest_grads = dict(w_in=grads["w_in"], w_out=grads["w_out"].reshape(NCHIP, D // NCHIP, D))
        ex_rest.append(exchange_start("rest", 0, BIG[:2], rest_grads))
        return [ex_rest[0][-1]["token"]]

    dx, _, g_mix = _layer_bwd_mixer(dh1, dh1b, layers[0], saved[0], after=[ex_ffn[-1]["token"]], midway=midway,
                                    late=late)
    scatter_wait(sc_last, dx)
    scatter_wait(sc_ffn[0], dx)
    gl[0] = _reference_layout({**g_ffn, **g_mix})

    small_g = [jnp.stack([gl[l][n] for l in range(DEPTH)]) for n in SMALL[:-1]] + [d_norm_final[0]]
    conv_g = jnp.stack([gl[l]["conv_w"] for l in range(DEPTH)])
    summed = small_g + [conv_g, loss_lanes[0, :1]]
    total = _allreduce_small(_pack(summed))
    exchange_wait(ex_rest[0], total)
    sc_rest = scatter_start("rest", 0, BIG[:2])

    travelling = [sc_rest[-1]["token"]]
    reduced, g_out, delta, new_m, new_v = {}, {}, {}, {}, {}

    done = []

    def adamw_large(names, joined):
        for n, g in zip(names, joined):
            res = _adamw(kernel_view(n, w[n]), g, kernel_view(n, m[n]), kernel_view(n, v[n]))
            done.append(res[2])
            g_out[n], delta[n], new_m[n], new_v[n] = (kernel_view(n, a) for a in (g,) + tuple(res))

    for n in BIG:
        for l in (range(DEPTH) if n in swiglu else [last]):
            reduced[n] = _sum_chips(sums[l, n][0], arrived[l, n], place, l, into=reduced.get(n), after=travelling)
    adamw_large(swiglu, _join_halves("join_swiglu", [reduced[n] for n in swiglu]))
    scatter_wait(sc_rest, done + [reduced[n] for n in BIG[:2]])
    for n in BIG[:2]:
        reduced[n] = _sum_chips(sums[0, n][0], arrived[0, n], place, 0, into=reduced[n])
    adamw_large(BIG[:2], _join_halves("join_rest", [reduced[n] for n in BIG[:2]]))
    *small_r, conv_r, loss = _unpack(total, summed)
    g_out.update(zip(SMALL, small_r))
    g_out["conv_w"] = lax.dynamic_slice_in_dim(conv_r, chip * conv_w.shape[2], conv_w.shape[2], axis=2)

    rest = SMALL + ("conv_w",)
    rows_of = lambda a: a.reshape(1, -1) if a.ndim == 1 else a
    results = _adamw_small(*[[rows_of(src[n]) for n in rest] for src in (w, g_out, m, v)])
    for dst, arrs in zip((delta, new_m, new_v), results):
        dst.update({n: a.reshape(w[n].shape) for n, a in zip(rest, arrs)})

    return (loss[0], dx[None], *[g_out[n] for n in ORDER], *[delta[n] for n in ORDER], *[new_m[n] for n in ORDER],
            *[new_v[n] for n in ORDER])
```
